```python
import jax, jax.numpy as jnp
from jax import lax
import numpy as np

D_MODEL = 1024
BATCH = 16
SEQ = 2048
DEPTH = 1

POOL_WINDOWS = (2, 4, 8, 16)
POOL_GROUPS = len(POOL_WINDOWS)
POOL_WIDTH = D_MODEL // 2
POOL_GROUP_DIM = POOL_WIDTH // POOL_GROUPS
HEAD_DIM = 64
ATTN_WIDTH = D_MODEL // 2
N_HEADS = ATTN_WIDTH // HEAD_DIM
Q_BLOCK = 128
N_BRANCHES = 2
D_FF = ((8 * D_MODEL + 3 * 256 - 1) // (3 * 256)) * 256
RMS_EPS = 1e-6
IN_SPLITS = (POOL_WIDTH, ATTN_WIDTH, ATTN_WIDTH, ATTN_WIDTH, N_HEADS, D_MODEL, D_MODEL)
IN_WIDTH = sum(IN_SPLITS)

kernel_name = "hybrid_pool_fox_gated_block"


def rmsnorm(x, g):
    xf = x.astype(jnp.float32)
    r = lax.rsqrt(jnp.mean(xf * xf, axis=-1, keepdims=True) + RMS_EPS)
    return (xf * r).astype(x.dtype) * g


def causal_multiscale_pool(u):
    B, S, _ = u.shape
    ug = u.reshape(B, S, POOL_GROUPS, POOL_GROUP_DIM)
    c = jnp.cumsum(ug.astype(jnp.float32), axis=1)
    t = jnp.arange(S)
    outs = []
    for g, w in enumerate(POOL_WINDOWS):
        cg = c[:, :, g]
        c_prev = jnp.pad(cg, ((0, 0), (w, 0), (0, 0)))[:, :S]
        cnt = jnp.minimum(t + 1, w).astype(jnp.float32)[None, :, None]
        outs.append((cg - c_prev) / cnt)
    mean = jnp.stack(outs, axis=2)
    return (mean - ug.astype(jnp.float32)).astype(u.dtype)


def forgetting_attention(q, k, v, log_f):
    B, S, H, Dh = q.shape
    q = q.transpose(0, 2, 1, 3)
    k = k.transpose(0, 2, 1, 3)
    v = v.transpose(0, 2, 1, 3)
    F = jnp.cumsum(log_f, axis=1).transpose(0, 2, 1)
    scale = HEAD_DIM ** -0.5
    outs = []
    for i in range(S // Q_BLOCK):
        lo, hi = i * Q_BLOCK, (i + 1) * Q_BLOCK
        s = jnp.einsum('bhqd,bhkd->bhqk', q[:, :, lo:hi], k[:, :, :hi]).astype(jnp.float32) * scale
        s = s + F[:, :, lo:hi, None] - F[:, :, None, :hi]
        mask = (lo + jnp.arange(Q_BLOCK))[:, None] >= jnp.arange(hi)[None, :]
        s = jnp.where(mask, s, -jnp.inf)
        p = jax.nn.softmax(s, axis=-1).astype(v.dtype)
        outs.append(jnp.einsum('bhqk,bhkd->bhqd', p, v[:, :, :hi]))
    o = jnp.concatenate(outs, axis=2)
    return o.transpose(0, 2, 1, 3).reshape(B, S, H * Dh)


def _fwd_setup_inputs(seed: int = 0) -> dict:
    key = jax.random.key(seed)
    ks = jax.random.split(key, 16)
    f32 = jnp.float32
    n = lambda k, shape, fan_in: jax.random.normal(k, shape, f32) * (fan_in ** -0.5)
    return {
        "x": jax.random.normal(ks[0], (BATCH, SEQ, D_MODEL), f32),
        "norm1_g": 1.0 + 0.02 * jax.random.normal(ks[1], (DEPTH, D_MODEL), f32),
        "w_in": n(ks[2], (DEPTH, D_MODEL, IN_WIDTH), D_MODEL),
        "b_forget": 2.0 + 0.5 * jax.random.normal(ks[3], (DEPTH, N_HEADS), f32),
        "pool_mix": n(ks[4], (DEPTH, POOL_GROUPS, POOL_GROUP_DIM, POOL_GROUP_DIM), POOL_GROUP_DIM),
        "pool_scale": 1.0 + 0.1 * jax.random.normal(ks[5], (DEPTH, POOL_WIDTH), f32),
        "w_pool_out": n(ks[6], (DEPTH, POOL_WIDTH, D_MODEL), POOL_WIDTH),
        "w_attn_out": n(ks[7], (DEPTH, ATTN_WIDTH, D_MODEL), ATTN_WIDTH),
        "w_out": n(ks[8], (DEPTH, D_MODEL, D_MODEL), D_MODEL),
        "norm2_g": 1.0 + 0.02 * jax.random.normal(ks[9], (DEPTH, D_MODEL), f32),
        "w_ffn_gate": n(ks[10], (DEPTH, D_MODEL, D_FF), D_MODEL),
        "w_ffn_up": n(ks[11], (DEPTH, D_MODEL, D_FF), D_MODEL),
        "w_ffn_down": n(ks[12], (DEPTH, D_FF, D_MODEL), D_FF),
        "norm_f_g": 1.0 + 0.02 * jax.random.normal(ks[13], (D_MODEL,), f32),
    }


def _fwd_reference(x, norm1_g, w_in, b_forget, pool_mix, pool_scale, w_pool_out, w_attn_out, w_out,
              norm2_g, w_ffn_gate, w_ffn_up, w_ffn_down, norm_f_g):
    B, S, _ = x.shape
    offs = np.cumsum((0,) + IN_SPLITS)
    for l in range(DEPTH):
        h = rmsnorm(x, norm1_g[l])
        z = h @ w_in[l]
        u, q, k, v, fl, gp, ga = [z[..., int(offs[i]):int(offs[i + 1])] for i in range(len(IN_SPLITS))]

        p = causal_multiscale_pool(u)
        p = jnp.einsum('bsgc,gcd->bsgd', p, pool_mix[l]).reshape(B, S, POOL_WIDTH) * pool_scale[l]
        pool_y = p @ w_pool_out[l]

        log_f = jax.nn.log_sigmoid(fl.astype(jnp.float32) + b_forget[l].astype(jnp.float32))
        a = forgetting_attention(q.reshape(B, S, N_HEADS, HEAD_DIM), k.reshape(B, S, N_HEADS, HEAD_DIM),
                                 v.reshape(B, S, N_HEADS, HEAD_DIM), log_f)
        attn_y = a @ w_attn_out[l]

        merged = jax.nn.sigmoid(gp) * pool_y + jax.nn.sigmoid(ga) * attn_y
        x = x + merged @ w_out[l]

        h2 = rmsnorm(x, norm2_g[l])
        x = x + (jax.nn.silu(h2 @ w_ffn_gate[l]) * (h2 @ w_ffn_up[l])) @ w_ffn_down[l]
    return rmsnorm(x, norm_f_g)


import jax as _jax
import jax.numpy as _jnp

TWIN_FORMAT = 'train_step'
FWD_PARAMS = ['x', 'norm1_g', 'w_in', 'b_forget', 'pool_mix', 'pool_scale', 'w_pool_out', 'w_attn_out', 'w_out', 'norm2_g', 'w_ffn_gate', 'w_ffn_up', 'w_ffn_down', 'norm_f_g']
TWIN_WEIGHTS = ['norm1_g', 'w_in', 'b_forget', 'pool_mix', 'pool_scale', 'w_pool_out', 'w_attn_out', 'w_out', 'norm2_g', 'w_ffn_gate', 'w_ffn_up', 'w_ffn_down', 'norm_f_g']
TWIN_DIFF_INPUT = 'x'
TWIN_INPUTS = ['x', 'norm1_g', 'w_in', 'b_forget', 'pool_mix', 'pool_scale', 'w_pool_out', 'w_attn_out', 'w_out', 'norm2_g', 'w_ffn_gate', 'w_ffn_up', 'w_ffn_down', 'norm_f_g', 'loss_target', 'm_norm1_g', 'm_w_in', 'm_b_forget', 'm_pool_mix', 'm_pool_scale', 'm_w_pool_out', 'm_w_attn_out', 'm_w_out', 'm_norm2_g', 'm_w_ffn_gate', 'm_w_ffn_up', 'm_w_ffn_down', 'm_norm_f_g', 'v_norm1_g', 'v_w_in', 'v_b_forget', 'v_pool_mix', 'v_pool_scale', 'v_w_pool_out', 'v_w_attn_out', 'v_w_out', 'v_norm2_g', 'v_w_ffn_gate', 'v_w_ffn_up', 'v_w_ffn_down', 'v_norm_f_g']
TWIN_OUTPUTS = ['loss', 'grad_x', 'grad_norm1_g', 'grad_w_in', 'grad_b_forget', 'grad_pool_mix', 'grad_pool_scale', 'grad_w_pool_out', 'grad_w_attn_out', 'grad_w_out', 'grad_norm2_g', 'grad_w_ffn_gate', 'grad_w_ffn_up', 'grad_w_ffn_down', 'grad_norm_f_g', 'delta_norm1_g', 'delta_w_in', 'delta_b_forget', 'delta_pool_mix', 'delta_pool_scale', 'delta_w_pool_out', 'delta_w_attn_out', 'delta_w_out', 'delta_norm2_g', 'delta_w_ffn_gate', 'delta_w_ffn_up', 'delta_w_ffn_down', 'delta_norm_f_g', 'new_m_norm1_g', 'new_m_w_in', 'new_m_b_forget', 'new_m_pool_mix', 'new_m_pool_scale', 'new_m_w_pool_out', 'new_m_w_attn_out', 'new_m_w_out', 'new_m_norm2_g', 'new_m_w_ffn_gate', 'new_m_w_ffn_up', 'new_m_w_ffn_down', 'new_m_norm_f_g', 'new_v_norm1_g', 'new_v_w_in', 'new_v_b_forget', 'new_v_pool_mix', 'new_v_pool_scale', 'new_v_w_pool_out', 'new_v_w_attn_out', 'new_v_w_out', 'new_v_norm2_g', 'new_v_w_ffn_gate', 'new_v_w_ffn_up', 'new_v_w_ffn_down', 'new_v_norm_f_g']
TWIN_LEAF_KINDS = {'loss': 'loss', 'grad_x': 'grad_x', 'grad_norm1_g': 'grad_w', 'grad_w_in': 'grad_w', 'grad_b_forget': 'grad_w', 'grad_pool_mix': 'grad_w', 'grad_pool_scale': 'grad_w', 'grad_w_pool_out': 'grad_w', 'grad_w_attn_out': 'grad_w', 'grad_w_out': 'grad_w', 'grad_norm2_g': 'grad_w', 'grad_w_ffn_gate': 'grad_w', 'grad_w_ffn_up': 'grad_w', 'grad_w_ffn_down': 'grad_w', 'grad_norm_f_g': 'grad_w', 'delta_norm1_g': 'delta_w', 'delta_w_in': 'delta_w', 'delta_b_forget': 'delta_w', 'delta_pool_mix': 'delta_w', 'delta_pool_scale': 'delta_w', 'delta_w_pool_out': 'delta_w', 'delta_w_attn_out': 'delta_w', 'delta_w_out': 'delta_w', 'delta_norm2_g': 'delta_w', 'delta_w_ffn_gate': 'delta_w', 'delta_w_ffn_up': 'delta_w', 'delta_w_ffn_down': 'delta_w', 'delta_norm_f_g': 'delta_w', 'new_m_norm1_g': 'new_m', 'new_m_w_in': 'new_m', 'new_m_b_forget': 'new_m', 'new_m_pool_mix': 'new_m', 'new_m_pool_scale': 'new_m', 'new_m_w_pool_out': 'new_m', 'new_m_w_attn_out': 'new_m', 'new_m_w_out': 'new_m', 'new_m_norm2_g': 'new_m', 'new_m_w_ffn_gate': 'new_m', 'new_m_w_ffn_up': 'new_m', 'new_m_w_ffn_down': 'new_m', 'new_m_norm_f_g': 'new_m', 'new_v_norm1_g': 'new_v', 'new_v_w_in': 'new_v', 'new_v_b_forget': 'new_v', 'new_v_pool_mix': 'new_v', 'new_v_pool_scale': 'new_v', 'new_v_w_pool_out': 'new_v', 'new_v_w_attn_out': 'new_v', 'new_v_w_out': 'new_v', 'new_v_norm2_g': 'new_v', 'new_v_w_ffn_gate': 'new_v', 'new_v_w_ffn_up': 'new_v', 'new_v_w_ffn_down': 'new_v', 'new_v_norm_f_g': 'new_v'}


def _forward(args):
    return _fwd_reference(*[args[k] for k in FWD_PARAMS])


def _output_shape():
    out = _jax.eval_shape(lambda: _forward(_fwd_setup_inputs(0)))
    return out.shape, out.dtype

N_MICROBATCH = 1
ADAM_LR = 0.001
ADAM_B1 = 0.9
ADAM_B2 = 0.999
ADAM_EPS = 1e-08
ADAM_WD = 0.01
ADAM_STEP = 10
PER_EXAMPLE_BATCH_AXIS = {'x': 0, 'loss_target': 0}
SHARED_INPUTS = []
_WEIGHT_DTYPES = {'norm1_g': _jnp.float32, 'w_in': _jnp.float32, 'b_forget': _jnp.float32, 'pool_mix': _jnp.float32, 'pool_scale': _jnp.float32, 'w_pool_out': _jnp.float32, 'w_attn_out': _jnp.float32, 'w_out': _jnp.float32, 'norm2_g': _jnp.float32, 'w_ffn_gate': _jnp.float32, 'w_ffn_up': _jnp.float32, 'w_ffn_down': _jnp.float32, 'norm_f_g': _jnp.float32}
MOMENT_SCALE = {'norm1_g': 1.262269e-01, 'w_in': 5.370638e-02, 'b_forget': 3.632744e-01, 'pool_mix': 1.182657e-01, 'pool_scale': 1.302165e-01, 'w_pool_out': 8.408672e-02, 'w_attn_out': 4.018798e-02, 'w_out': 9.194831e-02, 'norm2_g': 1.228831e-01, 'w_ffn_gate': 5.142384e-02, 'w_ffn_up': 4.978921e-02, 'w_ffn_down': 8.252188e-02, 'norm_f_g': 3.197834e+01}


def _to_microbatches(a, axis):
    t = _jnp.moveaxis(a, axis, 0)
    t = t.reshape((N_MICROBATCH, t.shape[0] // N_MICROBATCH) + t.shape[1:])
    return _jnp.moveaxis(t, 1, axis + 1)


def setup_inputs(seed: int = 0) -> dict:
    inp = _fwd_setup_inputs(seed)
    key = _jax.random.fold_in(_jax.random.key(seed), 7919)
    shape, _ = _output_shape()
    out = dict(inp)
    out["loss_target"] = _jax.random.normal(_jax.random.fold_in(key, 0), shape, _jnp.float32)
    for i, name in enumerate(TWIN_WEIGHTS):
        w = inp[name].astype(_jnp.float32)
        if MOMENT_SCALE is None:
            s = _jnp.sqrt(_jnp.mean(_jnp.square(w)) + 1e-30)
        else:
            s = MOMENT_SCALE[name]
        km, kv = _jax.random.split(_jax.random.fold_in(key, i + 1))
        out[name] = w
        out["m_" + name] = s * _jax.random.normal(km, w.shape, _jnp.float32)
        out["v_" + name] = (s * s) * _jax.random.uniform(kv, w.shape, _jnp.float32, 0.5, 1.5)
    if N_MICROBATCH > 1:
        for name, axis in PER_EXAMPLE_BATCH_AXIS.items():
            out[name] = _to_microbatches(out[name], axis)
    return {'x': out['x'], 'norm1_g': out['norm1_g'], 'w_in': out['w_in'], 'b_forget': out['b_forget'], 'pool_mix': out['pool_mix'], 'pool_scale': out['pool_scale'], 'w_pool_out': out['w_pool_out'], 'w_attn_out': out['w_attn_out'], 'w_out': out['w_out'], 'norm2_g': out['norm2_g'], 'w_ffn_gate': out['w_ffn_gate'], 'w_ffn_up': out['w_ffn_up'], 'w_ffn_down': out['w_ffn_down'], 'norm_f_g': out['norm_f_g'], 'loss_target': out['loss_target'], 'm_norm1_g': out['m_norm1_g'], 'm_w_in': out['m_w_in'], 'm_b_forget': out['m_b_forget'], 'm_pool_mix': out['m_pool_mix'], 'm_pool_scale': out['m_pool_scale'], 'm_w_pool_out': out['m_w_pool_out'], 'm_w_attn_out': out['m_w_attn_out'], 'm_w_out': out['m_w_out'], 'm_norm2_g': out['m_norm2_g'], 'm_w_ffn_gate': out['m_w_ffn_gate'], 'm_w_ffn_up': out['m_w_ffn_up'], 'm_w_ffn_down': out['m_w_ffn_down'], 'm_norm_f_g': out['m_norm_f_g'], 'v_norm1_g': out['v_norm1_g'], 'v_w_in': out['v_w_in'], 'v_b_forget': out['v_b_forget'], 'v_pool_mix': out['v_pool_mix'], 'v_pool_scale': out['v_pool_scale'], 'v_w_pool_out': out['v_w_pool_out'], 'v_w_attn_out': out['v_w_attn_out'], 'v_w_out': out['v_w_out'], 'v_norm2_g': out['v_norm2_g'], 'v_w_ffn_gate': out['v_w_ffn_gate'], 'v_w_ffn_up': out['v_w_ffn_up'], 'v_w_ffn_down': out['v_w_ffn_down'], 'v_norm_f_g': out['v_norm_f_g']}


def _loss(weights, diff, rest, loss_target):
    with _jax.named_scope("forward"):
        args = {**rest, TWIN_DIFF_INPUT: diff, **{k: w.astype(_WEIGHT_DTYPES[k]) for k, w in weights.items()}}
        y = _forward(args)
    with _jax.named_scope("loss_head"):
        err = _jnp.square(y.astype(_jnp.float32) - loss_target)
        return 0.5 * _jnp.sum(_jnp.mean(err, axis=-1)) if err.ndim else 0.5 * err


def _adamw(w, g, m, v):
    m = ADAM_B1 * m + (1.0 - ADAM_B1) * g
    v = ADAM_B2 * v + (1.0 - ADAM_B2) * _jnp.square(g)
    m_hat = m / (1.0 - ADAM_B1 ** ADAM_STEP)
    v_hat = v / (1.0 - ADAM_B2 ** ADAM_STEP)
    delta = -ADAM_LR * (m_hat / (_jnp.sqrt(v_hat) + ADAM_EPS) + ADAM_WD * w)
    return delta, m, v


def reference(x, norm1_g, w_in, b_forget, pool_mix, pool_scale, w_pool_out, w_attn_out, w_out, norm2_g, w_ffn_gate, w_ffn_up, w_ffn_down, norm_f_g, loss_target, m_norm1_g, m_w_in, m_b_forget, m_pool_mix, m_pool_scale, m_w_pool_out, m_w_attn_out, m_w_out, m_norm2_g, m_w_ffn_gate, m_w_ffn_up, m_w_ffn_down, m_norm_f_g, v_norm1_g, v_w_in, v_b_forget, v_pool_mix, v_pool_scale, v_w_pool_out, v_w_attn_out, v_w_out, v_norm2_g, v_w_ffn_gate, v_w_ffn_up, v_w_ffn_down, v_norm_f_g):
    given = dict(x=x, norm1_g=norm1_g, w_in=w_in, b_forget=b_forget, pool_mix=pool_mix, pool_scale=pool_scale, w_pool_out=w_pool_out, w_attn_out=w_attn_out, w_out=w_out, norm2_g=norm2_g, w_ffn_gate=w_ffn_gate, w_ffn_up=w_ffn_up, w_ffn_down=w_ffn_down, norm_f_g=norm_f_g, loss_target=loss_target, m_norm1_g=m_norm1_g, m_w_in=m_w_in, m_b_forget=m_b_forget, m_pool_mix=m_pool_mix, m_pool_scale=m_pool_scale, m_w_pool_out=m_w_pool_out, m_w_attn_out=m_w_attn_out, m_w_out=m_w_out, m_norm2_g=m_norm2_g, m_w_ffn_gate=m_w_ffn_gate, m_w_ffn_up=m_w_ffn_up, m_w_ffn_down=m_w_ffn_down, m_norm_f_g=m_norm_f_g, v_norm1_g=v_norm1_g, v_w_in=v_w_in, v_b_forget=v_b_forget, v_pool_mix=v_pool_mix, v_pool_scale=v_pool_scale, v_w_pool_out=v_w_pool_out, v_w_attn_out=v_w_attn_out, v_w_out=v_w_out, v_norm2_g=v_norm2_g, v_w_ffn_gate=v_w_ffn_gate, v_w_ffn_up=v_w_ffn_up, v_w_ffn_down=v_w_ffn_down, v_norm_f_g=v_norm_f_g)
    weights = {n: given[n] for n in TWIN_WEIGHTS}
    shared = {n: given[n] for n in SHARED_INPUTS}
    per_example = {n: given[n] for n in ['x']}
    grad_fn = _jax.value_and_grad(_loss, argnums=(0, 1))

    def one_microbatch(ex, loss_target):
        ex = dict(ex)
        diff = ex.pop(TWIN_DIFF_INPUT)
        return grad_fn(weights, diff, {**shared, **ex}, loss_target)

    if N_MICROBATCH == 1:
        loss, (grad_w, grad_x) = one_microbatch(per_example, given["loss_target"])
    else:
        def body(carry, xs):
            loss_sum, grad_sum = carry
            l_k, (gw_k, gx_k) = one_microbatch(xs[0], xs[1])
            with _jax.named_scope("update"):
                return (loss_sum + l_k, _jax.tree.map(_jnp.add, grad_sum, gw_k)), gx_k

        init = (_jnp.zeros((), _jnp.float32), _jax.tree.map(_jnp.zeros_like, weights))
        (loss, grad_w), grad_x = _jax.lax.scan(body, init, (per_example, given["loss_target"]))
    with _jax.named_scope("update"):
        delta_w, new_m, new_v = {}, {}, {}
        for n in TWIN_WEIGHTS:
            delta_w[n], new_m[n], new_v[n] = _adamw(weights[n], grad_w[n], given["m_" + n], given["v_" + n])
    return (loss, grad_x, *[grad_w[n] for n in TWIN_WEIGHTS], *[delta_w[n] for n in TWIN_WEIGHTS],
            *[new_m[n] for n in TWIN_WEIGHTS], *[new_v[n] for n in TWIN_WEIGHTS])
```

```python
import functools

import jax
import jax.numpy as jnp
from jax import lax
from jax.experimental import pallas as pl
from jax.experimental.pallas import tpu as pltpu

F32 = jnp.float32
BF16 = jnp.bfloat16
MESH = pl.DeviceIdType.MESH

D_MODEL = 1024
POOL_WINDOWS = (2, 4, 8, 16)
POOL_WIDTH = 512
GROUP_DIM = 128
ATTN_WIDTH = 512
HEAD_DIM = 64
N_HEADS = 8
N_PAIRS = 4
D_FF = 2816
RMS_EPS = 1e-6
N_DEV = 8
LANES = 128
FL_PAD = 128

ADAM_LR = 0.001
ADAM_B1 = 0.9
ADAM_B2 = 0.999
ADAM_EPS = 1e-08
ADAM_WD = 0.01
ADAM_STEP = 10

VMEM_LIMIT = 56 * 1024 * 1024
ROW_TILE = 256
ATTN_BLOCK = 256
FF_CHUNK = 256


def _mm(a, b):
    return jnp.dot(a, b, preferred_element_type=F32)


def _mm_nt(a, b):
    return lax.dot_general(a, b, (((1,), (1,)), ((), ())), preferred_element_type=F32)


def _mm_tn(a, b):
    return lax.dot_general(a, b, (((0,), (0,)), ((), ())), preferred_element_type=F32)


def _sigmoid(x):
    return 1.0 / (1.0 + jnp.exp(-x))


def _params(sem, vmem=VMEM_LIMIT):
    return pltpu.CompilerParams(dimension_semantics=sem, vmem_limit_bytes=vmem)


def _const_spec(shape):
    nd = len(shape)
    return pl.BlockSpec(shape, lambda *_: (0,) * nd, pipeline_mode=pl.Buffered(1))


def _rms_fwd(x, g):
    r = lax.rsqrt(jnp.mean(x * x, axis=-1, keepdims=True) + RMS_EPS)
    xh = x * r
    return xh * g, xh, r


def _rms_bwd(dy, xh, r, g):
    dxh = dy * g
    dx = r * (dxh - xh * jnp.mean(dxh * xh, axis=-1, keepdims=True))
    return dx, dy * xh


def _in_proj(x, g1, w_uqkv, w_fl, w_g):
    T = x.shape[0]
    tm = ROW_TILE

    def body(x_ref, g_ref, wa_ref, wf_ref, wg_ref, h_ref, u_ref, qkv_ref, fl_ref, gt_ref):
        h, _, _ = _rms_fwd(x_ref[...], g_ref[...])
        hb = h.astype(BF16)
        h_ref[...] = hb
        z = _mm(hb, wa_ref[...])
        u_ref[...] = z[:, :POOL_WIDTH]
        qkv_ref[...] = z[:, POOL_WIDTH:].astype(BF16)
        fl_ref[...] = _mm(hb, wf_ref[...])
        gt_ref[...] = _mm(hb, wg_ref[...])

    row = lambda n: pl.BlockSpec((tm, n), lambda i: (i, 0))
    return pl.pallas_call(
        body,
        name="in_proj",
        grid=(T // tm,),
        in_specs=[row(D_MODEL), _const_spec((1, D_MODEL)), _const_spec(w_uqkv.shape), _const_spec(w_fl.shape), _const_spec(w_g.shape)],
        out_specs=[row(D_MODEL), row(POOL_WIDTH), row(3 * ATTN_WIDTH), row(FL_PAD), row(2 * D_MODEL)],
        out_shape=[
            jax.ShapeDtypeStruct((T, D_MODEL), BF16),
            jax.ShapeDtypeStruct((T, POOL_WIDTH), F32),
            jax.ShapeDtypeStruct((T, 3 * ATTN_WIDTH), BF16),
            jax.ShapeDtypeStruct((T, FL_PAD), F32),
            jax.ShapeDtypeStruct((T, 2 * D_MODEL), F32),
        ],
        compiler_params=_params(("parallel",)),
    )(x, g1, w_uqkv, w_fl, w_g)


def _log_sigmoid(x):
    return jnp.minimum(x, 0.0) - jnp.log(1.0 + jnp.exp(-jnp.abs(x)))


def _forget_fwd(fl, b_pad, n_seq, S):
    def body(fl_ref, b_ref, fcol_ref, frow_ref):
        lf = _log_sigmoid(fl_ref[...] + b_ref[...])
        t = lf.T
        lane = lax.broadcasted_iota(jnp.int32, t.shape, 1)
        k = 1
        while k < S:
            t = t + jnp.where(lane >= k, pltpu.roll(t, k, 1), 0.0)
            k *= 2
        frow_ref[...] = t[:N_HEADS, :]
        fcol_ref[...] = t.T

    return pl.pallas_call(
        body,
        name="forget_fwd",
        grid=(n_seq,),
        in_specs=[pl.BlockSpec((S, FL_PAD), lambda s: (s, 0)), _const_spec((1, FL_PAD))],
        out_specs=[pl.BlockSpec((S, FL_PAD), lambda s: (s, 0)), pl.BlockSpec((None, N_HEADS, S), lambda s: (s, 0, 0))],
        out_shape=[jax.ShapeDtypeStruct((n_seq * S, FL_PAD), F32), jax.ShapeDtypeStruct((n_seq, N_HEADS, S), F32)],
        compiler_params=_params(("parallel",)),
    )(fl, b_pad)


def _window_pick(g, v2, v4, v8, v16):
    return jnp.where(g == 0, v2, jnp.where(g == 1, v4, jnp.where(g == 2, v8, v16)))


def _pool_fwd(u, mix_b, scale, w_po, n_seq, S):
    T = n_seq * S

    def body(u_ref, mix_ref, sc_ref, wpo_ref, pm_ref, p2_ref, p3_ref, py_ref):
        g = pl.program_id(1)
        uu = u_ref[...]
        row = lax.broadcasted_iota(jnp.int32, uu.shape, 0)

        def back(a, k):
            return jnp.where(row >= k, pltpu.roll(a, k, 0), 0.0)

        s2 = uu + back(uu, 1)
        s4 = s2 + back(s2, 2)
        s8 = s4 + back(s4, 4)
        s16 = s8 + back(s8, 8)
        w = _window_pick(g, 2.0, 4.0, 8.0, 16.0)
        cnt = jnp.minimum((row + 1).astype(F32), w)
        pm = _window_pick(g, s2, s4, s8, s16) / cnt - uu
        pmb = pm.astype(BF16)
        pm_ref[...] = pmb
        p2 = _mm(pmb, mix_ref[...])
        p2_ref[...] = p2
        p3 = (p2 * sc_ref[...]).astype(BF16)
        p3_ref[...] = p3

        @pl.when(g == 0)
        def _():
            py_ref[...] = jnp.zeros_like(py_ref)

        py_ref[...] += _mm(p3, wpo_ref[...])

    grp = pl.BlockSpec((S, GROUP_DIM), lambda s, g: (s, g))
    return pl.pallas_call(
        body,
        name="pool_fwd",
        grid=(n_seq, len(POOL_WINDOWS)),
        in_specs=[
            grp,
            pl.BlockSpec((None, GROUP_DIM, GROUP_DIM), lambda s, g: (g, 0, 0)),
            pl.BlockSpec((1, GROUP_DIM), lambda s, g: (0, g)),
            pl.BlockSpec((GROUP_DIM, D_MODEL), lambda s, g: (g, 0)),
        ],
        out_specs=[grp, grp, grp, pl.BlockSpec((S, D_MODEL), lambda s, g: (s, 0))],
        out_shape=[
            jax.ShapeDtypeStruct((T, POOL_WIDTH), BF16),
            jax.ShapeDtypeStruct((T, POOL_WIDTH), F32),
            jax.ShapeDtypeStruct((T, POOL_WIDTH), BF16),
            jax.ShapeDtypeStruct((T, D_MODEL), F32),
        ],
        compiler_params=_params(("parallel", "arbitrary")),
    )(u, mix_b, scale, w_po)


def _head_lanes(hh):
    lane = lax.broadcasted_iota(jnp.int32, (1, LANES), 1)
    return (lane >= HEAD_DIM * hh) & (lane < HEAD_DIM * (hh + 1))


def _lane_pick(a, idx):
    lane = lax.broadcasted_iota(jnp.int32, (1, LANES), 1)
    return jnp.sum(jnp.where(lane == idx, a, 0.0), axis=1, keepdims=True)


def _attn_fwd(qkv, fcol, frow4, n_seq, S):
    T = n_seq * S
    tb = ATTN_BLOCK
    nq = S // tb
    scale = HEAD_DIM ** -0.5

    def body(q_ref, k_ref, v_ref, fc_ref, fr_ref, o_ref, st_ref):
        p = pl.program_id(1)
        i = pl.program_id(2)
        q = q_ref[...]
        fc = fc_ref[...]
        rows = i * tb + lax.broadcasted_iota(jnp.int32, (tb, tb), 0)
        cols = lax.broadcasted_iota(jnp.int32, (tb, tb), 1)
        lane = lax.broadcasted_iota(jnp.int32, (1, LANES), 1)
        o_pair = jnp.zeros((tb, LANES), F32)
        st = jnp.zeros((tb, LANES), F32)
        for hh in range(2):
            hm = _head_lanes(hh)
            qm = jnp.where(hm, q, jnp.zeros_like(q))
            fq = _lane_pick(fc, 2 * p + hh)

            def kstep(j, carry, qm=qm, fq=fq, hh=hh):
                m, l, acc = carry
                c0 = pl.multiple_of(j * tb, tb)
                kb = k_ref[pl.ds(c0, tb), :]
                vb = v_ref[pl.ds(c0, tb), :]
                s = _mm_nt(qm, kb) * scale + fq - fr_ref[hh : hh + 1, pl.ds(c0, tb)]
                s = jnp.where(cols + c0 <= rows, s, -jnp.inf)
                m_new = jnp.maximum(m, jnp.max(s, axis=1, keepdims=True))
                alpha = jnp.exp(m - m_new)
                pe = jnp.exp(s - m_new)
                l = alpha * l + jnp.sum(pe, axis=1, keepdims=True)
                acc = alpha * acc + _mm(pe.astype(BF16), vb)
                return m_new, l, acc

            init = (jnp.full((tb, 1), -jnp.inf, F32), jnp.zeros((tb, 1), F32), jnp.zeros((tb, LANES), F32))
            m, l, acc = lax.fori_loop(0, i + 1, kstep, init)
            o_pair = jnp.where(hm, acc / l, o_pair)
            st = jnp.where(lane == hh, m + jnp.log(l), st)
        o_ref[...] = o_pair.astype(BF16)
        st_ref[...] = st

    return pl.pallas_call(
        body,
        name="attn_fwd",
        grid=(n_seq, N_PAIRS, nq),
        in_specs=[
            pl.BlockSpec((tb, LANES), lambda s, p, i: (s * nq + i, p)),
            pl.BlockSpec((S, LANES), lambda s, p, i: (s, N_PAIRS + p)),
            pl.BlockSpec((S, LANES), lambda s, p, i: (s, 2 * N_PAIRS + p)),
            pl.BlockSpec((tb, LANES), lambda s, p, i: (s * nq + i, 0)),
            pl.BlockSpec((None, None, 2, S), lambda s, p, i: (s, p, 0, 0)),
        ],
        out_specs=[
            pl.BlockSpec((tb, LANES), lambda s, p, i: (s * nq + i, p)),
            pl.BlockSpec((None, tb, LANES), lambda s, p, i: (p, s * nq + i, 0)),
        ],
        out_shape=[jax.ShapeDtypeStruct((T, ATTN_WIDTH), BF16), jax.ShapeDtypeStruct((N_PAIRS, T, LANES), F32)],
        compiler_params=_params(("parallel", "parallel", "parallel")),
    )(qkv, qkv, qkv, fcol, frow4)


def _mix_out(a, pool_y, gates, x, w_ao, w_out):
    T = x.shape[0]
    tm = ROW_TILE

    def body(a_ref, py_ref, gt_ref, x_ref, wao_ref, wout_ref, mg_ref, x1_ref, ay_ref):
        ay = _mm(a_ref[...], wao_ref[...])
        ay_ref[...] = ay
        sp = _sigmoid(gt_ref[:, :D_MODEL])
        sa = _sigmoid(gt_ref[:, D_MODEL:])
        mb = (sp * py_ref[...] + sa * ay).astype(BF16)
        mg_ref[...] = mb
        x1_ref[...] = x_ref[...] + _mm(mb, wout_ref[...])

    row = lambda n: pl.BlockSpec((tm, n), lambda i: (i, 0))
    return pl.pallas_call(
        body,
        name="mix_out",
        grid=(T // tm,),
        in_specs=[row(ATTN_WIDTH), row(D_MODEL), row(2 * D_MODEL), row(D_MODEL), _const_spec(w_ao.shape), _const_spec(w_out.shape)],
        out_specs=[row(D_MODEL), row(D_MODEL), row(D_MODEL)],
        out_shape=[jax.ShapeDtypeStruct((T, D_MODEL), BF16), jax.ShapeDtypeStruct((T, D_MODEL), F32), jax.ShapeDtypeStruct((T, D_MODEL), F32)],
        compiler_params=_params(("parallel",)),
    )(a, pool_y, gates, x, w_ao, w_out)


def _ffn_fwd(x1, g2, gf, tgt, w_gate, w_up, w_down):
    T = x1.shape[0]
    tm = ROW_TILE
    nt = T // tm
    nc = D_FF // FF_CHUNK

    def body(x1_ref, g2_ref, gf_ref, tg_ref, wg_ref, wu_ref, wd_ref, h2_ref, gate_ref, up_ref, act_ref, dx2_ref, loss_ref, dgf_ref):
        x1v = x1_ref[...]
        h2, _, _ = _rms_fwd(x1v, g2_ref[...])
        h2b = h2.astype(BF16)
        h2_ref[...] = h2b
        acc = x1v
        for c in range(nc):
            sl = slice(c * FF_CHUNK, (c + 1) * FF_CHUNK)
            gate = _mm(h2b, wg_ref[:, sl])
            up = _mm(h2b, wu_ref[:, sl])
            gate_ref[:, sl] = gate
            up_ref[:, sl] = up
            act = (gate * _sigmoid(gate) * up).astype(BF16)
            act_ref[:, sl] = act
            acc = acc + _mm(act, wd_ref[sl, :])
        gfv = gf_ref[...]
        y, xh, r = _rms_fwd(acc, gfv)
        err = y - tg_ref[...]
        part = 0.5 * jnp.sum(jnp.mean(err * err, axis=-1, keepdims=True), axis=0, keepdims=True)
        dx2, dgrow = _rms_bwd(err * (1.0 / D_MODEL), xh, r, gfv)
        dx2_ref[...] = dx2

        @pl.when(pl.program_id(0) == 0)
        def _():
            dgf_ref[...] = jnp.zeros_like(dgf_ref)
            loss_ref[...] = jnp.zeros_like(loss_ref)

        dgf_ref[...] += jnp.sum(dgrow, axis=0, keepdims=True)
        loss_ref[...] += jnp.broadcast_to(part, loss_ref.shape)

    row = lambda n: pl.BlockSpec((tm, n), lambda i: (i, 0))
    return pl.pallas_call(
        body,
        name="ffn_fwd",
        grid=(nt,),
        in_specs=[
            row(D_MODEL), _const_spec((1, D_MODEL)), _const_spec((1, D_MODEL)), row(D_MODEL),
            _const_spec(w_gate.shape), _const_spec(w_up.shape), _const_spec(w_down.shape),
        ],
        out_specs=[
            row(D_MODEL), row(D_FF), row(D_FF), row(D_FF), row(D_MODEL),
            pl.BlockSpec((8, LANES), lambda i: (0, 0)),
            pl.BlockSpec((1, D_MODEL), lambda i: (0, 0)),
        ],
        out_shape=[
            jax.ShapeDtypeStruct((T, D_MODEL), BF16),
            jax.ShapeDtypeStruct((T, D_FF), F32),
            jax.ShapeDtypeStruct((T, D_FF), F32),
            jax.ShapeDtypeStruct((T, D_FF), BF16),
            jax.ShapeDtypeStruct((T, D_MODEL), F32),
            jax.ShapeDtypeStruct((8, LANES), F32),
            jax.ShapeDtypeStruct((1, D_MODEL), F32),
        ],
        compiler_params=_params(("arbitrary",)),
    )(x1, g2, gf, tgt, w_gate, w_up, w_down)


def _ffn_bwd(dx2, gate, up, x1, g2, w_gate, w_up, w_down):
    T = x1.shape[0]
    tm = ROW_TILE
    nc = D_FF // FF_CHUNK

    def body(dx2_ref, gate_ref, up_ref, x1_ref, g2_ref, wg_ref, wu_ref, wd_ref, dgate_ref, dup_ref, dx1_ref, dg2_ref):
        dx2v = dx2_ref[...]
        dx2b = dx2v.astype(BF16)
        dh2 = jnp.zeros((tm, D_MODEL), F32)
        for c in range(nc):
            sl = slice(c * FF_CHUNK, (c + 1) * FF_CHUNK)
            dact = _mm_nt(dx2b, wd_ref[sl, :])
            gate = gate_ref[:, sl]
            sg = _sigmoid(gate)
            silu = gate * sg
            dgate = (dact * up_ref[:, sl] * (sg * (1.0 + gate * (1.0 - sg)))).astype(BF16)
            dup = (dact * silu).astype(BF16)
            dgate_ref[:, sl] = dgate
            dup_ref[:, sl] = dup
            dh2 = dh2 + _mm_nt(dgate, wg_ref[:, sl]) + _mm_nt(dup, wu_ref[:, sl])
        g2v = g2_ref[...]
        _, xh, r = _rms_fwd(x1_ref[...], g2v)
        dxn, dgrow = _rms_bwd(dh2, xh, r, g2v)
        dx1_ref[...] = dx2v + dxn

        @pl.when(pl.program_id(0) == 0)
        def _():
            dg2_ref[...] = jnp.zeros_like(dg2_ref)

        dg2_ref[...] += jnp.sum(dgrow, axis=0, keepdims=True)

    row = lambda n: pl.BlockSpec((tm, n), lambda i: (i, 0))
    return pl.pallas_call(
        body,
        name="ffn_bwd",
        grid=(T // tm,),
        in_specs=[
            row(D_MODEL), row(D_FF), row(D_FF), row(D_MODEL), _const_spec((1, D_MODEL)),
            _const_spec(w_gate.shape), _const_spec(w_up.shape), _const_spec(w_down.shape),
        ],
        out_specs=[row(D_FF), row(D_FF), row(D_MODEL), pl.BlockSpec((1, D_MODEL), lambda i: (0, 0))],
        out_shape=[
            jax.ShapeDtypeStruct((T, D_FF), BF16),
            jax.ShapeDtypeStruct((T, D_FF), BF16),
            jax.ShapeDtypeStruct((T, D_MODEL), F32),
            jax.ShapeDtypeStruct((1, D_MODEL), F32),
        ],
        compiler_params=_params(("arbitrary",)),
    )(dx2, gate, up, x1, g2, w_gate, w_up, w_down)


def _mix_bwd(dx1, gates, pool_y, attn_y, p2, scale, w_out, w_ao, w_po):
    T = dx1.shape[0]
    tm = ROW_TILE

    def body(dx1_ref, gt_ref, py_ref, ay_ref, p2_ref, sc_ref, wout_ref, wao_ref, wpo_ref, dgt_ref, dpy_ref, day_ref, da_ref, dp2_ref, dsc_ref):
        dm = _mm_nt(dx1_ref[...].astype(BF16), wout_ref[...])
        sp = _sigmoid(gt_ref[:, :D_MODEL])
        sa = _sigmoid(gt_ref[:, D_MODEL:])
        dgt_ref[:, :D_MODEL] = (dm * py_ref[...] * (sp * (1.0 - sp))).astype(BF16)
        dgt_ref[:, D_MODEL:] = (dm * ay_ref[...] * (sa * (1.0 - sa))).astype(BF16)
        dpy = (dm * sp).astype(BF16)
        day = (dm * sa).astype(BF16)
        dpy_ref[...] = dpy
        day_ref[...] = day
        da_ref[...] = _mm_nt(day, wao_ref[...]).astype(BF16)
        dp3 = _mm_nt(dpy, wpo_ref[...])
        dp2_ref[...] = (dp3 * sc_ref[...]).astype(BF16)

        @pl.when(pl.program_id(0) == 0)
        def _():
            dsc_ref[...] = jnp.zeros_like(dsc_ref)

        dsc_ref[...] += jnp.sum(dp3 * p2_ref[...], axis=0, keepdims=True)

    row = lambda n: pl.BlockSpec((tm, n), lambda i: (i, 0))
    return pl.pallas_call(
        body,
        name="mix_bwd",
        grid=(T // tm,),
        in_specs=[
            row(D_MODEL), row(2 * D_MODEL), row(D_MODEL), row(D_MODEL), row(POOL_WIDTH), _const_spec((1, POOL_WIDTH)),
            _const_spec(w_out.shape), _const_spec(w_ao.shape), _const_spec(w_po.shape),
        ],
        out_specs=[row(2 * D_MODEL), row(D_MODEL), row(D_MODEL), row(ATTN_WIDTH), row(POOL_WIDTH), pl.BlockSpec((1, POOL_WIDTH), lambda i: (0, 0))],
        out_shape=[
            jax.ShapeDtypeStruct((T, 2 * D_MODEL), BF16),
            jax.ShapeDtypeStruct((T, D_MODEL), BF16),
            jax.ShapeDtypeStruct((T, D_MODEL), BF16),
            jax.ShapeDtypeStruct((T, ATTN_WIDTH), BF16),
            jax.ShapeDtypeStruct((T, POOL_WIDTH), BF16),
            jax.ShapeDtypeStruct((1, POOL_WIDTH), F32),
        ],
        compiler_params=_params(("arbitrary",)),
    )(dx1, gates, pool_y, attn_y, p2, scale, w_out, w_ao, w_po)


def _pool_bwd(dp2, pm, mix_b, n_seq, S):
    T = n_seq * S

    def body(dp2_ref, pm_ref, mix_ref, du_ref, dmix_ref):
        g = pl.program_id(0)
        dp2v = dp2_ref[...]
        dpm = _mm_nt(dp2v, mix_ref[...])
        row = lax.broadcasted_iota(jnp.int32, dpm.shape, 0)
        w = _window_pick(g, 2.0, 4.0, 8.0, 16.0)
        e = dpm / jnp.minimum((row + 1).astype(F32), w)

        def ahead(a, k):
            return jnp.where(row < S - k, pltpu.roll(a, S - k, 0), 0.0)

        r2 = e + ahead(e, 1)
        r4 = r2 + ahead(r2, 2)
        r8 = r4 + ahead(r4, 4)
        r16 = r8 + ahead(r8, 8)
        du_ref[...] = (_window_pick(g, r2, r4, r8, r16) - dpm).astype(BF16)

        @pl.when(pl.program_id(1) == 0)
        def _():
            dmix_ref[...] = jnp.zeros_like(dmix_ref)

        dmix_ref[...] += _mm_tn(pm_ref[...], dp2v)

    grp = pl.BlockSpec((S, GROUP_DIM), lambda g, s: (s, g))
    mixs = pl.BlockSpec((None, GROUP_DIM, GROUP_DIM), lambda g, s: (g, 0, 0))
    return pl.pallas_call(
        body,
        name="pool_bwd",
        grid=(len(POOL_WINDOWS), n_seq),
        in_specs=[grp, grp, mixs],
        out_specs=[grp, mixs],
        out_shape=[jax.ShapeDtypeStruct((T, POOL_WIDTH), BF16), jax.ShapeDtypeStruct((len(POOL_WINDOWS), GROUP_DIM, GROUP_DIM), F32)],
        compiler_params=_params(("parallel", "arbitrary")),
    )(dp2, pm, mix_b)


def _attn_bwd(qkv, da, a, fcol, frow4, lse, n_seq, S):
    T = n_seq * S
    tb = ATTN_BLOCK
    nb = S // tb
    scale = HEAD_DIM ** -0.5

    def body(q_ref, k_ref, v_ref, do_ref, o_ref, fc_ref, fr_ref, st_ref, dq_ref, dk_ref, dv_ref, dfk_ref, dfq_ref, dq_acc):
        p = pl.program_id(1)
        j = pl.program_id(2)
        lane = lax.broadcasted_iota(jnp.int32, (1, LANES), 1)

        @pl.when(j == 0)
        def _():
            dq_acc[...] = jnp.zeros_like(dq_acc)
            dfq_ref[...] = jnp.zeros_like(dfq_ref)

        kb = k_ref[...]
        vb = v_ref[...]
        c0 = pl.multiple_of(j * tb, tb)
        rows0 = lax.broadcasted_iota(jnp.int32, (tb, tb), 0)
        cols = c0 + lax.broadcasted_iota(jnp.int32, (tb, tb), 1)
        dk_pair = jnp.zeros((tb, LANES), F32)
        dv_pair = jnp.zeros((tb, LANES), F32)
        for hh in range(2):
            hm = _head_lanes(hh)
            km = jnp.where(hm, kb, jnp.zeros_like(kb))
            vm = jnp.where(hm, vb, jnp.zeros_like(vb))
            fk = fr_ref[hh : hh + 1, pl.ds(c0, tb)]

            def qstep(i, carry, hm=hm, km=km, vm=vm, fk=fk, hh=hh):
                dk_acc, dv_acc, dfk_acc = carry
                r0 = pl.multiple_of(i * tb, tb)
                qb = q_ref[pl.ds(r0, tb), :]
                dob = do_ref[pl.ds(r0, tb), :]
                ob = o_ref[pl.ds(r0, tb), :]
                fq = _lane_pick(fc_ref[pl.ds(r0, tb), :], 2 * p + hh)
                lse_q = _lane_pick(st_ref[pl.ds(r0, tb), :], hh)
                delta = jnp.sum(jnp.where(hm, dob.astype(F32) * ob.astype(F32), 0.0), axis=1, keepdims=True)
                s = _mm_nt(qb, km) * scale + fq - fk
                s = jnp.where(cols <= rows0 + r0, s, -jnp.inf)
                pr = jnp.exp(s - lse_q)
                dv_acc = dv_acc + _mm_tn(pr.astype(BF16), dob)
                dp = _mm_nt(dob, vm)
                ds = pr * (dp - delta)
                dfk_acc = dfk_acc + jnp.sum(ds, axis=0, keepdims=True)
                dfq_ref[pl.ds(r0, tb), :] += jnp.where(lane == hh, jnp.sum(ds, axis=1, keepdims=True), 0.0)
                dsb = (ds * scale).astype(BF16)
                dq_acc[pl.ds(r0, tb), :] += _mm(dsb, km)
                dk_acc = dk_acc + _mm_tn(dsb, qb)
                return dk_acc, dv_acc, dfk_acc

            init = (jnp.zeros((tb, LANES), F32), jnp.zeros((tb, LANES), F32), jnp.zeros((1, tb), F32))
            dk_h, dv_h, dfk_h = lax.fori_loop(j, nb, qstep, init)
            dk_pair = jnp.where(hm, dk_h, dk_pair)
            dv_pair = jnp.where(hm, dv_h, dv_pair)
            dfk_ref[hh : hh + 1, pl.ds(c0, tb)] = -dfk_h
        dk_ref[...] = dk_pair.astype(BF16)
        dv_ref[...] = dv_pair.astype(BF16)

        @pl.when(j == nb - 1)
        def _():
            dq_ref[...] = dq_acc[...].astype(BF16)

    seq = lambda col: pl.BlockSpec((S, LANES), col)
    blk = lambda col: pl.BlockSpec((tb, LANES), col)
    return pl.pallas_call(
        body,
        name="attn_bwd",
        grid=(n_seq, N_PAIRS, nb),
        in_specs=[
            seq(lambda s, p, j: (s, p)),
            blk(lambda s, p, j: (s * nb + j, N_PAIRS + p)),
            blk(lambda s, p, j: (s * nb + j, 2 * N_PAIRS + p)),
            seq(lambda s, p, j: (s, p)),
            seq(lambda s, p, j: (s, p)),
            seq(lambda s, p, j: (s, 0)),
            pl.BlockSpec((None, None, 2, S), lambda s, p, j: (s, p, 0, 0)),
            pl.BlockSpec((None, S, LANES), lambda s, p, j: (p, s, 0)),
        ],
        out_specs=[
            seq(lambda s, p, j: (s, p)),
            blk(lambda s, p, j: (s * nb + j, p)),
            blk(lambda s, p, j: (s * nb + j, p)),
            pl.BlockSpec((None, None, 2, S), lambda s, p, j: (s, p, 0, 0)),
            pl.BlockSpec((None, S, LANES), lambda s, p, j: (p, s, 0)),
        ],
        out_shape=[
            jax.ShapeDtypeStruct((T, ATTN_WIDTH), BF16),
            jax.ShapeDtypeStruct((T, ATTN_WIDTH), BF16),
            jax.ShapeDtypeStruct((T, ATTN_WIDTH), BF16),
            jax.ShapeDtypeStruct((n_seq, N_PAIRS, 2, S), F32),
            jax.ShapeDtypeStruct((N_PAIRS, T, LANES), F32),
        ],
        scratch_shapes=[pltpu.VMEM((S, LANES), F32)],
        compiler_params=_params(("parallel", "parallel", "arbitrary")),
    )(qkv, qkv, qkv, da, a, fcol, frow4, lse)


def _forget_bwd(dfk, dfq, fl, b_pad, n_seq, S):
    def body(df_ref, dfq_ref, fl_ref, b_ref, dfl_ref, db_ref):
        lane1 = lax.broadcasted_iota(jnp.int32, (1, LANES), 1)
        qcol = jnp.zeros((S, LANES), F32)
        for p in range(N_PAIRS):
            pair = dfq_ref[p]
            moved = pair if p == 0 else pltpu.roll(pair, 2 * p, 1)
            qcol = jnp.where((lane1 == 2 * p) | (lane1 == 2 * p + 1), moved, qcol)
        t = jnp.concatenate([df_ref[...], jnp.zeros((FL_PAD - N_HEADS, S), F32)], axis=0) + qcol.T
        lane = lax.broadcasted_iota(jnp.int32, t.shape, 1)
        k = 1
        while k < S:
            t = t + jnp.where(lane < S - k, pltpu.roll(t, S - k, 1), 0.0)
            k *= 2
        dfl = t.T * _sigmoid(-(fl_ref[...] + b_ref[...]))
        dfl_ref[...] = dfl.astype(BF16)

        @pl.when(pl.program_id(0) == 0)
        def _():
            db_ref[...] = jnp.zeros_like(db_ref)

        db_ref[...] += jnp.sum(dfl, axis=0, keepdims=True)

    return pl.pallas_call(
        body,
        name="forget_bwd",
        grid=(n_seq,),
        in_specs=[
            pl.BlockSpec((None, N_HEADS, S), lambda s: (s, 0, 0)),
            pl.BlockSpec((N_PAIRS, S, LANES), lambda s: (0, s, 0)),
            pl.BlockSpec((S, FL_PAD), lambda s: (s, 0)),
            _const_spec((1, FL_PAD)),
        ],
        out_specs=[pl.BlockSpec((S, FL_PAD), lambda s: (s, 0)), pl.BlockSpec((1, FL_PAD), lambda s: (0, 0))],
        out_shape=[jax.ShapeDtypeStruct((n_seq * S, FL_PAD), BF16), jax.ShapeDtypeStruct((1, FL_PAD), F32)],
        compiler_params=_params(("arbitrary",)),
    )(dfk, dfq, fl, b_pad)


def _in_proj_bwd(du, dq, dk, dv, dfl, dgates, x, dx1, g1, w_uqkv, w_fl, w_g):
    T = x.shape[0]
    tm = ROW_TILE

    def body(du_ref, dq_ref, dk_ref, dv_ref, dfl_ref, dgt_ref, x_ref, dx1_ref, g_ref, wa_ref, wf_ref, wg_ref, dx_ref, dg_ref):
        dh = _mm_nt(dgt_ref[...], wg_ref[...]) + _mm_nt(dfl_ref[...], wf_ref[...])
        for n, ref in enumerate((du_ref, dq_ref, dk_ref, dv_ref)):
            dh = dh + _mm_nt(ref[...], wa_ref[:, n * 512 : (n + 1) * 512])
        gv = g_ref[...]
        _, xh, r = _rms_fwd(x_ref[...], gv)
        dxn, dgrow = _rms_bwd(dh, xh, r, gv)
        dx_ref[...] = dx1_ref[...] + dxn

        @pl.when(pl.program_id(0) == 0)
        def _():
            dg_ref[...] = jnp.zeros_like(dg_ref)

        dg_ref[...] += jnp.sum(dgrow, axis=0, keepdims=True)

    row = lambda n: pl.BlockSpec((tm, n), lambda i: (i, 0))
    return pl.pallas_call(
        body,
        name="in_proj_bwd",
        grid=(T // tm,),
        in_specs=[
            row(512), row(512), row(512), row(512), row(FL_PAD), row(2 * D_MODEL), row(D_MODEL), row(D_MODEL), _const_spec((1, D_MODEL)),
            _const_spec(w_uqkv.shape), _const_spec(w_fl.shape), _const_spec(w_g.shape),
        ],
        out_specs=[row(D_MODEL), pl.BlockSpec((1, D_MODEL), lambda i: (0, 0))],
        out_shape=[jax.ShapeDtypeStruct((T, D_MODEL), F32), jax.ShapeDtypeStruct((1, D_MODEL), F32)],
        compiler_params=_params(("arbitrary",)),
    )(du, dq, dk, dv, dfl, dgates, x, dx1, g1, w_uqkv, w_fl, w_g)


def _pick_block(n):
    for b in (512, 1408, 256, 128):
        if n % b == 0:
            return b
    raise ValueError(n)


def _matmul_tn(a, b, name):
    T, K = a.shape
    N = b.shape[1]
    bt, bk, bn = 512, _pick_block(K), _pick_block(N)

    def body(a_ref, b_ref, o_ref):
        @pl.when(pl.program_id(2) == 0)
        def _():
            o_ref[...] = jnp.zeros_like(o_ref)

        o_ref[...] += _mm_tn(a_ref[...].astype(BF16), b_ref[...].astype(BF16))

    return pl.pallas_call(
        body,
        name=name,
        grid=(K // bk, N // bn, T // bt),
        in_specs=[pl.BlockSpec((bt, bk), lambda k, n, t: (t, k)), pl.BlockSpec((bt, bn), lambda k, n, t: (t, n))],
        out_specs=pl.BlockSpec((bk, bn), lambda k, n, t: (k, n)),
        out_shape=jax.ShapeDtypeStruct((K, N), F32),
        compiler_params=_params(("parallel", "parallel", "arbitrary")),
    )(a, b)


def _position():
    return lax.axis_index("x"), lax.axis_index("y"), lax.axis_index("c")


def _all_gather(block, name):
    R = block.shape[0]

    def body(x_ref, out_ref, send_sems, recv_sems, local_sem):
        x, y, c = _position()
        me, sibling = (x, y, c), (x, y, 1 - c)
        chips = [(1 - x, y), (x, 1 - y), (1 - x, 1 - y)]

        def rows(px, py, pc):
            return out_ref.at[4 * px + 2 * py + pc]

        def copy(k, blk, to, src=None):
            return pltpu.make_async_remote_copy(
                src_ref=rows(*blk) if src is None else src, dst_ref=rows(*blk),
                send_sem=send_sems.at[k], recv_sem=recv_sems.at[k], device_id=to, device_id_type=MESH,
            )

        mine = pltpu.make_async_copy(x_ref, rows(*me), local_sem)
        mine.start()
        first = [copy(0, me, sibling, src=x_ref)]
        first += [copy(1 + n, me, (*chip, c), src=x_ref) for n, chip in enumerate(chips)]
        for cp in first:
            cp.start()
        passed = [copy(4 + n, (*chip, c), sibling) for n, chip in enumerate(chips)]
        for n, chip in enumerate(chips):
            copy(1 + n, (*chip, c), me).wait_recv()
            passed[n].start()
        copy(0, sibling, me).wait_recv()
        for n, chip in enumerate(chips):
            copy(4 + n, (*chip, 1 - c), me).wait_recv()
        for cp in first + passed:
            cp.wait_send()
        mine.wait()

    return pl.pallas_call(
        body,
        name=name,
        out_shape=jax.ShapeDtypeStruct((N_DEV, R, LANES), block.dtype),
        in_specs=[pl.BlockSpec(memory_space=pl.ANY)],
        out_specs=pl.BlockSpec(memory_space=pl.ANY),
        scratch_shapes=[pltpu.SemaphoreType.DMA((7,)), pltpu.SemaphoreType.DMA((7,)), pltpu.SemaphoreType.DMA],
    )(block)


def _sibling_exchange(send):
    _, _, R, _ = send.shape

    def body(s_ref, r_ref, send_sem, recv_sem):
        x, y, c = _position()
        cp = pltpu.make_async_remote_copy(
            src_ref=s_ref.at[1 - c], dst_ref=r_ref, send_sem=send_sem, recv_sem=recv_sem,
            device_id=(x, y, 1 - c), device_id_type=MESH,
        )
        cp.start()
        cp.wait()

    return pl.pallas_call(
        body,
        name="rs_sibling",
        out_shape=jax.ShapeDtypeStruct((4, R, LANES), send.dtype),
        in_specs=[pl.BlockSpec(memory_space=pl.ANY)],
        out_specs=pl.BlockSpec(memory_space=pl.ANY),
        scratch_shapes=[pltpu.SemaphoreType.DMA, pltpu.SemaphoreType.DMA],
    )(send)


def _pair_sum(send, got, core):
    _, _, R, _ = send.shape
    br = _row_block(R)

    def body(core_ref, a_ref, b_ref, o_ref):
        o_ref[...] = (a_ref[...].astype(F32) + b_ref[...].astype(F32)).astype(o_ref.dtype)

    return pl.pallas_call(
        body,
        name="rs_pair_sum",
        grid_spec=pltpu.PrefetchScalarGridSpec(
            num_scalar_prefetch=1,
            grid=(4, R // br),
            in_specs=[
                pl.BlockSpec((None, None, br, LANES), lambda n, i, core: (core[0], n, i, 0)),
                pl.BlockSpec((None, br, LANES), lambda n, i, core: (n, i, 0)),
            ],
            out_specs=pl.BlockSpec((None, br, LANES), lambda n, i, core: (n, i, 0)),
        ),
        out_shape=jax.ShapeDtypeStruct((4, R, LANES), send.dtype),
        compiler_params=_params(("parallel", "parallel")),
    )(core, send, got)


def _chip_exchange(pair):
    _, R, _ = pair.shape

    def body(p_ref, r_ref, send_sems, recv_sems):
        x, y, c = _position()
        chips = [(1 - x, y), (x, 1 - y), (1 - x, 1 - y)]
        cps = [
            pltpu.make_async_remote_copy(
                src_ref=p_ref.at[2 * cx + cy], dst_ref=r_ref.at[n], send_sem=send_sems.at[n], recv_sem=recv_sems.at[n],
                device_id=(cx, cy, c), device_id_type=MESH,
            )
            for n, (cx, cy) in enumerate(chips)
        ]
        for cp in cps:
            cp.start()
        for cp in cps:
            cp.wait()

    return pl.pallas_call(
        body,
        name="rs_chips",
        out_shape=jax.ShapeDtypeStruct((3, R, LANES), pair.dtype),
        in_specs=[pl.BlockSpec(memory_space=pl.ANY)],
        out_specs=pl.BlockSpec(memory_space=pl.ANY),
        scratch_shapes=[pltpu.SemaphoreType.DMA((3,)), pltpu.SemaphoreType.DMA((3,))],
    )(pair)


def _adamw(w, g, m, v):
    m = ADAM_B1 * m + (1.0 - ADAM_B1) * g
    v = ADAM_B2 * v + (1.0 - ADAM_B2) * (g * g)
    m_hat = m / (1.0 - ADAM_B1 ** ADAM_STEP)
    v_hat = v / (1.0 - ADAM_B2 ** ADAM_STEP)
    delta = -ADAM_LR * (m_hat / (jnp.sqrt(v_hat) + ADAM_EPS) + ADAM_WD * w)
    return delta, m, v


def _row_block(R):
    for b in range(min(R, 2048) // _PACK_ALIGN * _PACK_ALIGN, 0, -_PACK_ALIGN):
        if R % b == 0:
            return b
    raise ValueError(R)


def _shard_update(send, got, recv, w, m, v, pos):
    R = w.shape[0]
    br = _row_block(R)

    def body(pos_ref, a_ref, b_ref, r_ref, w_ref, m_ref, v_ref, g_ref, d_ref, nm_ref, nv_ref):
        g = a_ref[...].astype(F32) + b_ref[...].astype(F32)
        for n in range(3):
            g = g + r_ref[n].astype(F32)
        g_ref[...] = g
        d_ref[...], nm_ref[...], nv_ref[...] = _adamw(w_ref[...], g, m_ref[...], v_ref[...])

    flat = pl.BlockSpec((br, LANES), lambda i, pos: (i, 0))
    return pl.pallas_call(
        body,
        name="shard_update",
        grid_spec=pltpu.PrefetchScalarGridSpec(
            num_scalar_prefetch=1,
            grid=(R // br,),
            in_specs=[
                pl.BlockSpec((None, None, br, LANES), lambda i, pos: (pos[0], pos[1], i, 0)),
                pl.BlockSpec((None, br, LANES), lambda i, pos: (pos[1], i, 0)),
                pl.BlockSpec((3, br, LANES), lambda i, pos: (0, i, 0)),
                flat, flat, flat,
            ],
            out_specs=[flat, flat, flat, flat],
        ),
        out_shape=[jax.ShapeDtypeStruct((R, LANES), F32)] * 4,
        compiler_params=_params(("parallel",)),
    )(pos, send, got, recv, w, m, v)


def _small_update(parts, w, m, v):
    R = w.shape[0]

    def body(p_ref, w_ref, m_ref, v_ref, g_ref, d_ref, nm_ref, nv_ref):
        g = p_ref[0]
        for n in range(1, N_DEV):
            g = g + p_ref[n]
        g_ref[...] = g
        d_ref[...], nm_ref[...], nv_ref[...] = _adamw(w_ref[...], g, m_ref[...], v_ref[...])

    return pl.pallas_call(
        body,
        name="small_update",
        out_shape=[jax.ShapeDtypeStruct((R, LANES), F32)] * 4,
        compiler_params=pltpu.CompilerParams(vmem_limit_bytes=VMEM_LIMIT),
    )(parts, w, m, v)


_SHARDED = (
    ("w_in", (1024, 513), 1),
    ("w_pool_out", (512, 128), 1),
    ("w_attn_out", (512, 128), 1),
    ("w_out", (128, 1024), 0),
    ("w_ffn_gate", (1024, 352), 1),
    ("w_ffn_up", (1024, 352), 1),
    ("w_ffn_down", (352, 1024), 0),
)
_PACK_ALIGN = 16


def _packed_rows(shape):
    n = shape[0] * shape[1] // LANES
    return -(-n // _PACK_ALIGN) * _PACK_ALIGN


def _pack(shards, dtype):
    parts = []
    for (name, shape, _), t in zip(_SHARDED, shards):
        lead = t.shape[:-2]
        n = shape[0] * shape[1] // LANES
        f = t.astype(dtype).reshape(*lead, n, LANES)
        pad = _packed_rows(shape) - n
        if pad:
            f = jnp.concatenate([f, jnp.zeros((*lead, pad, LANES), dtype)], axis=-2)
        parts.append(f)
    return jnp.concatenate(parts, axis=-2)


def _unpack(packed):
    out, off = [], 0
    for name, shape, _ in _SHARDED:
        n = shape[0] * shape[1] // LANES
        out.append(packed[..., off : off + n, :].reshape(*packed.shape[:-2], *shape))
        off += _packed_rows(shape)
    return out


def _full_from_gathered(t, axis):
    if axis == 0:
        return t.reshape(N_DEV * t.shape[1], t.shape[2])
    return jnp.transpose(t, (1, 0, 2)).reshape(t.shape[1], N_DEV * t.shape[2])


def _shards_from_full(t, axis):
    if axis == 0:
        return t.reshape(N_DEV, t.shape[0] // N_DEV, t.shape[1])
    return jnp.transpose(t.reshape(t.shape[0], N_DEV, t.shape[1] // N_DEV), (1, 0, 2))


_SMALL = (("norm1_g", 8), ("norm2_g", 8), ("norm_f_g", 8), ("b_forget", 8), ("pool_scale", 8), ("pool_mix", 512))
_SMALL_ROWS = sum(r for _, r in _SMALL) + 8


def _pack_small(vals, loss_row):
    parts = []
    for (name, rows), t in zip(_SMALL, vals):
        f = t.astype(F32).reshape(-1)
        f = jnp.concatenate([f, jnp.zeros((rows * LANES - f.shape[0],), F32)]).reshape(rows, LANES)
        parts.append(f)
    parts.append(loss_row)
    return jnp.concatenate(parts, axis=0)


def _unpack_small(packed, shapes):
    out, off = [], 0
    for (name, rows), shape in zip(_SMALL, shapes):
        n = 1
        for s in shape:
            n *= s
        out.append(packed[off : off + rows].reshape(-1)[:n].reshape(shape))
        off += rows
    return out, packed[off, 0]


def _local_grads(x, tgt, g1, g2, gf, b_forget, pool_mix, pool_scale, w_in, w_po, w_ao, w_out, w_gate, w_up, w_down):
    n_seq, S, _ = x.shape
    T = n_seq * S
    x2 = x.reshape(T, D_MODEL)
    tg2 = tgt.reshape(T, D_MODEL)
    w_uqkv = w_in[:, : POOL_WIDTH + 3 * ATTN_WIDTH]
    w_fl = jnp.concatenate([w_in[:, 2048 : 2048 + N_HEADS], jnp.zeros((D_MODEL, FL_PAD - N_HEADS), BF16)], axis=1)
    w_g = w_in[:, 2048 + N_HEADS :]
    b_pad = jnp.concatenate([b_forget.reshape(1, N_HEADS), jnp.zeros((1, FL_PAD - N_HEADS), F32)], axis=1)
    mix_b = pool_mix.reshape(len(POOL_WINDOWS), GROUP_DIM, GROUP_DIM).astype(BF16)
    scale = pool_scale.reshape(1, POOL_WIDTH)
    g1 = g1.reshape(1, D_MODEL)
    g2 = g2.reshape(1, D_MODEL)
    gf = gf.reshape(1, D_MODEL)

    h, u, qkv, fl, gates = _in_proj(x2, g1, w_uqkv, w_fl, w_g)
    fcol, frow = _forget_fwd(fl, b_pad, n_seq, S)
    frow4 = frow.reshape(n_seq, N_PAIRS, 2, S)
    pm, p2, p3, pool_y = _pool_fwd(u, mix_b, scale, w_po, n_seq, S)
    a, lse = _attn_fwd(qkv, fcol, frow4, n_seq, S)
    merged, x1, attn_y = _mix_out(a, pool_y, gates, x2, w_ao, w_out)
    h2, gate, up, act, dx2, loss_rows, dgf = _ffn_fwd(x1, g2, gf, tg2, w_gate, w_up, w_down)

    dgate, dup, dx1, dg2 = _ffn_bwd(dx2, gate, up, x1, g2, w_gate, w_up, w_down)
    dgates, dpy, day, da, dp2, dscale = _mix_bwd(dx1, gates, pool_y, attn_y, p2, scale, w_out, w_ao, w_po)
    du, dmix = _pool_bwd(dp2, pm, mix_b, n_seq, S)
    dq, dk, dv, dfk, dfq = _attn_bwd(qkv, da, a, fcol, frow4, lse, n_seq, S)
    dfl, db = _forget_bwd(dfk.reshape(n_seq, N_HEADS, S), dfq, fl, b_pad, n_seq, S)
    dx, dg1 = _in_proj_bwd(du, dq, dk, dv, dfl, dgates, x2, dx1, g1, w_uqkv, w_fl, w_g)

    d_w_in = jnp.concatenate(
        [
            _matmul_tn(h, du, "dw_u"), _matmul_tn(h, dq, "dw_q"), _matmul_tn(h, dk, "dw_k"), _matmul_tn(h, dv, "dw_v"),
            _matmul_tn(h, dfl, "dw_fl")[:, :N_HEADS], _matmul_tn(h, dgates, "dw_gates"),
        ],
        axis=1,
    )
    d_full = (
        d_w_in,
        _matmul_tn(p3, dpy, "dw_pool_out"),
        _matmul_tn(a, day, "dw_attn_out"),
        _matmul_tn(merged, dx1, "dw_out"),
        _matmul_tn(h2, dgate, "dw_ffn_gate"),
        _matmul_tn(h2, dup, "dw_ffn_up"),
        _matmul_tn(act, dx2, "dw_ffn_down"),
    )
    small = (dg1, dg2, dgf, db[:, :N_HEADS], dscale, dmix)
    return loss_rows, dx.reshape(n_seq, S, D_MODEL), d_full, small


def kernel(x, norm1_g, w_in, b_forget, pool_mix, pool_scale, w_pool_out, w_attn_out, w_out, norm2_g, w_ffn_gate, w_ffn_up, w_ffn_down, norm_f_g, loss_target, m_norm1_g, m_w_in, m_b_forget, m_pool_mix, m_pool_scale, m_w_pool_out, m_w_attn_out, m_w_out, m_norm2_g, m_w_ffn_gate, m_w_ffn_up, m_w_ffn_down, m_norm_f_g, v_norm1_g, v_w_in, v_b_forget, v_pool_mix, v_pool_scale, v_w_pool_out, v_w_attn_out, v_w_out, v_norm2_g, v_w_ffn_gate, v_w_ffn_up, v_w_ffn_down, v_norm_f_g):
    w_sh = (w_in[0], w_pool_out[0], w_attn_out[0], w_out[0], w_ffn_gate[0], w_ffn_up[0], w_ffn_down[0])
    m_sh = (m_w_in[0], m_w_pool_out[0], m_w_attn_out[0], m_w_out[0], m_w_ffn_gate[0], m_w_ffn_up[0], m_w_ffn_down[0])
    v_sh = (v_w_in[0], v_w_pool_out[0], v_w_attn_out[0], v_w_out[0], v_w_ffn_gate[0], v_w_ffn_up[0], v_w_ffn_down[0])

    gathered = _unpack(_all_gather(_pack(w_sh, BF16), "weights_all_gather"))
    whole = [_full_from_gathered(t, axis) for t, (_, _, axis) in zip(gathered, _SHARDED)]

    loss_rows, grad_x, d_full, small = _local_grads(x, loss_target, norm1_g, norm2_g, norm_f_g, b_forget, pool_mix, pool_scale, *whole)

    chunks = _pack([_shards_from_full(t, axis) for t, (_, _, axis) in zip(d_full, _SHARDED)], BF16)
    R = chunks.shape[1]
    send = jnp.transpose(chunks.reshape(4, 2, R, LANES), (1, 0, 2, 3))
    cx, cy, cc = _position()
    core = jnp.reshape(cc, (1,)).astype(jnp.int32)
    pos = jnp.stack([cc, 2 * cx + cy]).astype(jnp.int32)
    got = _sibling_exchange(send)
    pair = _pair_sum(send, got, core)
    recv = _chip_exchange(pair)
    g_p, d_p, nm_p, nv_p = _shard_update(send, got, recv, _pack(w_sh, F32), _pack(m_sh, F32), _pack(v_sh, F32), pos)
    lead = lambda ts: [t[None] for t in ts]
    g_w, d_w, nm_w, nv_w = lead(_unpack(g_p)), lead(_unpack(d_p)), lead(_unpack(nm_p)), lead(_unpack(nv_p))

    small_w = (norm1_g, norm2_g, norm_f_g, b_forget, pool_scale, pool_mix)
    small_m = (m_norm1_g, m_norm2_g, m_norm_f_g, m_b_forget, m_pool_scale, m_pool_mix)
    small_v = (v_norm1_g, v_norm2_g, v_norm_f_g, v_b_forget, v_pool_scale, v_pool_mix)
    zero_row = jnp.zeros((8, LANES), F32)
    parts = _all_gather(_pack_small(small, loss_rows), "small_all_gather")
    g_s, d_s, nm_s, nv_s = _small_update(parts, _pack_small(small_w, zero_row), _pack_small(small_m, zero_row), _pack_small(small_v, zero_row))
    shapes = [t.shape for t in small_w]
    (g1, g2, gf, gb, gsc, gmix), loss = _unpack_small(g_s, shapes)
    (d1, d2, df, db_, dsc, dmx), _ = _unpack_small(d_s, shapes)
    (m1, m2, mf, mb, msc, mmx), _ = _unpack_small(nm_s, shapes)
    (v1, v2, vf, vb, vsc, vmx), _ = _unpack_small(nv_s, shapes)

    def ordered(n1, win, b, mix, sc, wpo, wao, wout, n2, wg, wu, wd, nf):
        return (n1, win, b, mix, sc, wpo, wao, wout, n2, wg, wu, wd, nf)

    grads = ordered(g1, g_w[0], gb, gmix, gsc, g_w[1], g_w[2], g_w[3], g2, g_w[4], g_w[5], g_w[6], gf)
    deltas = ordered(d1, d_w[0], db_, dmx, dsc, d_w[1], d_w[2], d_w[3], d2, d_w[4], d_w[5], d_w[6], df)
    new_m = ordered(m1, nm_w[0], mb, mmx, msc, nm_w[1], nm_w[2], nm_w[3], m2, nm_w[4], nm_w[5], nm_w[6], mf)
    new_v = ordered(v1, nv_w[0], vb, vmx, vsc, nv_w[1], nv_w[2], nv_w[3], v2, nv_w[4], nv_w[5], nv_w[6], vf)
    return (loss, grad_x, *grads, *deltas, *new_m, *new_v)
```

```python
import functools

import jax
import jax.numpy as jnp
from jax import lax
from jax.experimental import pallas as pl
from jax.experimental.pallas import tpu as pltpu

F32 = jnp.float32
BF16 = jnp.bfloat16
MESH = pl.DeviceIdType.MESH

D_MODEL = 1024
POOL_WINDOWS = (2, 4, 8, 16)
POOL_WIDTH = 512
GROUP_DIM = 128
ATTN_WIDTH = 512
HEAD_DIM = 64
N_HEADS = 8
N_PAIRS = 4
D_FF = 2816
RMS_EPS = 1e-6
N_DEV = 8
LANES = 128
FL_PAD = 128

ADAM_LR = 0.001
ADAM_B1 = 0.9
ADAM_B2 = 0.999
ADAM_EPS = 1e-08
ADAM_WD = 0.01
ADAM_STEP = 10

VMEM_LIMIT = 56 * 1024 * 1024
ROW_TILE = 256
ATTN_BLOCK = 256
FF_CHUNK = 256


def _mm(a, b):
    return jnp.dot(a, b, preferred_element_type=F32)


def _mm_nt(a, b):
    return lax.dot_general(a, b, (((1,), (1,)), ((), ())), preferred_element_type=F32)


def _mm_tn(a, b):
    return lax.dot_general(a, b, (((0,), (0,)), ((), ())), preferred_element_type=F32)


def _sigmoid(x):
    return 1.0 / (1.0 + jnp.exp(-x))


def _params(sem, vmem=VMEM_LIMIT):
    return pltpu.CompilerParams(dimension_semantics=sem, vmem_limit_bytes=vmem)


def _const_spec(shape):
    nd = len(shape)
    return pl.BlockSpec(shape, lambda *_: (0,) * nd, pipeline_mode=pl.Buffered(1))


def _rms_fwd(x, g):
    r = lax.rsqrt(jnp.mean(x * x, axis=-1, keepdims=True) + RMS_EPS)
    xh = x * r
    return xh * g, xh, r


def _rms_bwd(dy, xh, r, g):
    dxh = dy * g
    dx = r * (dxh - xh * jnp.mean(dxh * xh, axis=-1, keepdims=True))
    return dx, dy * xh


def _in_proj(x, g1, w_uqkv, w_fl, w_g):
    T = x.shape[0]
    tm = ROW_TILE

    def body(x_ref, g_ref, wa_ref, wf_ref, wg_ref, h_ref, u_ref, qkv_ref, fl_ref, gt_ref):
        h, _, _ = _rms_fwd(x_ref[...], g_ref[...])
        hb = h.astype(BF16)
        h_ref[...] = hb
        z = _mm(hb, wa_ref[...])
        u_ref[...] = z[:, :POOL_WIDTH]
        qkv_ref[...] = z[:, POOL_WIDTH:].astype(BF16)
        fl_ref[...] = _mm(hb, wf_ref[...])
        gt_ref[...] = _mm(hb, wg_ref[...])

    row = lambda n: pl.BlockSpec((tm, n), lambda i: (i, 0))
    return pl.pallas_call(
        body,
        name="in_proj",
        grid=(T // tm,),
        in_specs=[row(D_MODEL), _const_spec((1, D_MODEL)), _const_spec(w_uqkv.shape), _const_spec(w_fl.shape), _const_spec(w_g.shape)],
        out_specs=[row(D_MODEL), row(POOL_WIDTH), row(3 * ATTN_WIDTH), row(FL_PAD), row(2 * D_MODEL)],
        out_shape=[
            jax.ShapeDtypeStruct((T, D_MODEL), BF16),
            jax.ShapeDtypeStruct((T, POOL_WIDTH), F32),
            jax.ShapeDtypeStruct((T, 3 * ATTN_WIDTH), BF16),
            jax.ShapeDtypeStruct((T, FL_PAD), F32),
            jax.ShapeDtypeStruct((T, 2 * D_MODEL), F32),
        ],
        compiler_params=_params(("parallel",)),
    )(x, g1, w_uqkv, w_fl, w_g)


def _log_sigmoid(x):
    return jnp.minimum(x, 0.0) - jnp.log(1.0 + jnp.exp(-jnp.abs(x)))


def _forget_fwd(fl, b_pad, n_seq, S):
    def body(fl_ref, b_ref, fcol_ref):
        lf = _log_sigmoid(fl_ref[...] + b_ref[...])
        t = lf.T
        lane = lax.broadcasted_iota(jnp.int32, t.shape, 1)
        k = 1
        while k < S:
            t = t + jnp.where(lane >= k, pltpu.roll(t, k, 1), 0.0)
            k *= 2
        fcol_ref[...] = t.T

    return pl.pallas_call(
        body,
        name="forget_fwd",
        grid=(n_seq,),
        in_specs=[pl.BlockSpec((S, FL_PAD), lambda s: (s, 0)), _const_spec((1, FL_PAD))],
        out_specs=pl.BlockSpec((S, FL_PAD), lambda s: (s, 0)),
        out_shape=jax.ShapeDtypeStruct((n_seq * S, FL_PAD), F32),
        compiler_params=_params(("parallel",)),
    )(fl, b_pad)


def _window_pick(g, v2, v4, v8, v16):
    return jnp.where(g == 0, v2, jnp.where(g == 1, v4, jnp.where(g == 2, v8, v16)))


def _pool_fwd(u, mix_b, scale, w_po, n_seq, S):
    T = n_seq * S

    def body(u_ref, mix_ref, sc_ref, wpo_ref, pm_ref, p2_ref, p3_ref, py_ref):
        g = pl.program_id(1)
        uu = u_ref[...]
        row = lax.broadcasted_iota(jnp.int32, uu.shape, 0)

        def back(a, k):
            return jnp.where(row >= k, pltpu.roll(a, k, 0), 0.0)

        s2 = uu + back(uu, 1)
        s4 = s2 + back(s2, 2)
        s8 = s4 + back(s4, 4)
        s16 = s8 + back(s8, 8)
        w = _window_pick(g, 2.0, 4.0, 8.0, 16.0)
        cnt = jnp.minimum((row + 1).astype(F32), w)
        pm = _window_pick(g, s2, s4, s8, s16) / cnt - uu
        pmb = pm.astype(BF16)
        pm_ref[...] = pmb
        p2 = _mm(pmb, mix_ref[...])
        p2_ref[...] = p2
        p3 = (p2 * sc_ref[...]).astype(BF16)
        p3_ref[...] = p3

        @pl.when(g == 0)
        def _():
            py_ref[...] = jnp.zeros_like(py_ref)

        py_ref[...] += _mm(p3, wpo_ref[...])

    grp = pl.BlockSpec((S, GROUP_DIM), lambda s, g: (s, g))
    return pl.pallas_call(
        body,
        name="pool_fwd",
        grid=(n_seq, len(POOL_WINDOWS)),
        in_specs=[
            grp,
            pl.BlockSpec((None, GROUP_DIM, GROUP_DIM), lambda s, g: (g, 0, 0)),
            pl.BlockSpec((1, GROUP_DIM), lambda s, g: (0, g)),
            pl.BlockSpec((GROUP_DIM, D_MODEL), lambda s, g: (g, 0)),
        ],
        out_specs=[grp, grp, grp, pl.BlockSpec((S, D_MODEL), lambda s, g: (s, 0))],
        out_shape=[
            jax.ShapeDtypeStruct((T, POOL_WIDTH), BF16),
            jax.ShapeDtypeStruct((T, POOL_WIDTH), F32),
            jax.ShapeDtypeStruct((T, POOL_WIDTH), BF16),
            jax.ShapeDtypeStruct((T, D_MODEL), F32),
        ],
        compiler_params=_params(("parallel", "arbitrary")),
    )(u, mix_b, scale, w_po)


def _split3(v):
    hi = v.astype(BF16).astype(F32)
    r = v - hi
    mid = r.astype(BF16).astype(F32)
    lo = (r - mid).astype(BF16).astype(F32)
    return hi, mid, lo


def _augment(xp, hh, first, second):
    lane = lax.broadcasted_iota(jnp.int32, (1, LANES), 1)
    head = (lane >= HEAD_DIM * hh) & (lane < HEAD_DIM * (hh + 1))
    b = HEAD_DIM * (1 - hh)
    out = jnp.where(head, xp.astype(F32), 0.0)
    for n, col in enumerate(tuple(first) + tuple(second)):
        out = jnp.where(lane == b + n, col, out)
    return out.astype(BF16)


def _attn_fwd(qkv, fcol, n_seq, S):
    T = n_seq * S
    tb = ATTN_BLOCK
    nq = S // tb
    scale = HEAD_DIM ** -0.5

    def body(q_ref, k_ref, v_ref, fc_ref, o_ref, st_ref, qa_sc, ka_sc, m_sc, l_sc, acc_sc):
        i = pl.program_id(1)
        lane = lax.broadcasted_iota(jnp.int32, (1, LANES), 1)
        low = lane < HEAD_DIM
        ones = (1.0, 1.0, 1.0)

        @pl.when(i == 0)
        def _():
            def rows_ka(r, carry):
                r0 = pl.multiple_of(r * tb, tb)
                for h in range(N_HEADS):
                    kp = k_ref[pl.ds(r0, tb), (h // 2) * LANES : (h // 2 + 1) * LANES] * scale
                    fk = fc_ref[pl.ds(r0, tb), h : h + 1]
                    ka_sc[h, pl.ds(r0, tb), :] = _augment(kp, h % 2, ones, _split3(-fk))
                return carry

            lax.fori_loop(0, nq, rows_ka, 0)

        q0 = pl.multiple_of(i * tb, tb)
        for h in range(N_HEADS):
            qp = q_ref[:, (h // 2) * LANES : (h // 2 + 1) * LANES]
            qa_sc[h] = _augment(qp, h % 2, _split3(fc_ref[pl.ds(q0, tb), h : h + 1]), ones)
        m_sc[...] = jnp.full(m_sc.shape, -jnp.inf, F32)
        l_sc[...] = jnp.zeros_like(l_sc)
        acc_sc[...] = jnp.zeros_like(acc_sc)
        causal = lax.broadcasted_iota(jnp.int32, (tb, tb), 1) <= lax.broadcasted_iota(jnp.int32, (tb, tb), 0)

        def step(j, masked):
            c0 = pl.multiple_of(j * tb, tb)
            for p in range(N_PAIRS):
                vb = v_ref[pl.ds(c0, tb), p * LANES : (p + 1) * LANES]
                pv, al = [], []
                for hh in range(2):
                    h = 2 * p + hh
                    s = _mm_nt(qa_sc[h], ka_sc[h, pl.ds(c0, tb), :])
                    if masked:
                        s = jnp.where(causal, s, -jnp.inf)
                    m_old = m_sc[h]
                    m_new = jnp.maximum(m_old, jnp.max(s, axis=1, keepdims=True))
                    alpha = jnp.exp(m_old - m_new)
                    pe = jnp.exp(s - jnp.concatenate([m_new] * (tb // LANES), axis=1))
                    l_sc[h] = alpha * l_sc[h] + jnp.sum(pe, axis=1, keepdims=True)
                    m_sc[h] = m_new
                    pv.append(_mm(pe.astype(BF16), vb))
                    al.append(alpha)
                acc_sc[p] = jnp.where(low, al[0], al[1]) * acc_sc[p] + jnp.where(low, pv[0], pv[1])

        def loop_body(j, carry):
            step(j, False)
            return carry

        lax.fori_loop(0, i, loop_body, 0)
        step(i, True)
        st = jnp.zeros((tb, LANES), F32)
        for p in range(N_PAIRS):
            lp = jnp.where(low, l_sc[2 * p], l_sc[2 * p + 1])
            o_ref[:, p * LANES : (p + 1) * LANES] = (acc_sc[p] / lp).astype(BF16)
            for h in (2 * p, 2 * p + 1):
                st = jnp.where(lane == h, m_sc[h] + jnp.log(l_sc[h]), st)
        st_ref[...] = st

    return pl.pallas_call(
        body,
        name="attn_fwd",
        grid=(n_seq, nq),
        in_specs=[
            pl.BlockSpec((tb, ATTN_WIDTH), lambda s, i: (s * nq + i, 0)),
            pl.BlockSpec((S, ATTN_WIDTH), lambda s, i: (s, 1)),
            pl.BlockSpec((S, ATTN_WIDTH), lambda s, i: (s, 2)),
            pl.BlockSpec((S, LANES), lambda s, i: (s, 0)),
        ],
        out_specs=[
            pl.BlockSpec((tb, ATTN_WIDTH), lambda s, i: (s * nq + i, 0)),
            pl.BlockSpec((tb, LANES), lambda s, i: (s * nq + i, 0)),
        ],
        out_shape=[jax.ShapeDtypeStruct((T, ATTN_WIDTH), BF16), jax.ShapeDtypeStruct((T, LANES), F32)],
        scratch_shapes=[
            pltpu.VMEM((N_HEADS, tb, LANES), BF16),
            pltpu.VMEM((N_HEADS, S, LANES), BF16),
            pltpu.VMEM((N_HEADS, tb, LANES), F32),
            pltpu.VMEM((N_HEADS, tb, LANES), F32),
            pltpu.VMEM((N_PAIRS, tb, LANES), F32),
        ],
        compiler_params=_params(("parallel", "arbitrary")),
    )(qkv, qkv, qkv, fcol)


def _mix_out(a, pool_y, gates, x, w_ao, w_out):
    T = x.shape[0]
    tm = ROW_TILE

    def body(a_ref, py_ref, gt_ref, x_ref, wao_ref, wout_ref, mg_ref, x1_ref, ay_ref):
        ay = _mm(a_ref[...], wao_ref[...])
        ay_ref[...] = ay
        sp = _sigmoid(gt_ref[:, :D_MODEL])
        sa = _sigmoid(gt_ref[:, D_MODEL:])
        mb = (sp * py_ref[...] + sa * ay).astype(BF16)
        mg_ref[...] = mb
        x1_ref[...] = x_ref[...] + _mm(mb, wout_ref[...])

    row = lambda n: pl.BlockSpec((tm, n), lambda i: (i, 0))
    return pl.pallas_call(
        body,
        name="mix_out",
        grid=(T // tm,),
        in_specs=[row(ATTN_WIDTH), row(D_MODEL), row(2 * D_MODEL), row(D_MODEL), _const_spec(w_ao.shape), _const_spec(w_out.shape)],
        out_specs=[row(D_MODEL), row(D_MODEL), row(D_MODEL)],
        out_shape=[jax.ShapeDtypeStruct((T, D_MODEL), BF16), jax.ShapeDtypeStruct((T, D_MODEL), F32), jax.ShapeDtypeStruct((T, D_MODEL), F32)],
        compiler_params=_params(("parallel",)),
    )(a, pool_y, gates, x, w_ao, w_out)


def _ffn_fwd(x1, g2, gf, tgt, w_gate, w_up, w_down):
    T = x1.shape[0]
    tm = ROW_TILE
    nt = T // tm
    nc = D_FF // FF_CHUNK

    def body(x1_ref, g2_ref, gf_ref, tg_ref, wg_ref, wu_ref, wd_ref, h2_ref, gate_ref, up_ref, act_ref, dx2_ref, loss_ref, dgf_ref):
        x1v = x1_ref[...]
        h2, _, _ = _rms_fwd(x1v, g2_ref[...])
        h2b = h2.astype(BF16)
        h2_ref[...] = h2b
        acc = x1v
        for c in range(nc):
            sl = slice(c * FF_CHUNK, (c + 1) * FF_CHUNK)
            gate = _mm(h2b, wg_ref[:, sl])
            up = _mm(h2b, wu_ref[:, sl])
            gate_ref[:, sl] = gate
            up_ref[:, sl] = up
            act = (gate * _sigmoid(gate) * up).astype(BF16)
            act_ref[:, sl] = act
            acc = acc + _mm(act, wd_ref[sl, :])
        gfv = gf_ref[...]
        y, xh, r = _rms_fwd(acc, gfv)
        err = y - tg_ref[...]
        part = 0.5 * jnp.sum(jnp.mean(err * err, axis=-1, keepdims=True), axis=0, keepdims=True)
        dx2, dgrow = _rms_bwd(err * (1.0 / D_MODEL), xh, r, gfv)
        dx2_ref[...] = dx2

        @pl.when(pl.program_id(0) == 0)
        def _():
            dgf_ref[...] = jnp.zeros_like(dgf_ref)
            loss_ref[...] = jnp.zeros_like(loss_ref)

        dgf_ref[...] += jnp.sum(dgrow, axis=0, keepdims=True)
        loss_ref[...] += jnp.broadcast_to(part, loss_ref.shape)

    row = lambda n: pl.BlockSpec((tm, n), lambda i: (i, 0))
    return pl.pallas_call(
        body,
        name="ffn_fwd",
        grid=(nt,),
        in_specs=[
            row(D_MODEL), _const_spec((1, D_MODEL)), _const_spec((1, D_MODEL)), row(D_MODEL),
            _const_spec(w_gate.shape), _const_spec(w_up.shape), _const_spec(w_down.shape),
        ],
        out_specs=[
            row(D_MODEL), row(D_FF), row(D_FF), row(D_FF), row(D_MODEL),
            pl.BlockSpec((8, LANES), lambda i: (0, 0)),
            pl.BlockSpec((1, D_MODEL), lambda i: (0, 0)),
        ],
        out_shape=[
            jax.ShapeDtypeStruct((T, D_MODEL), BF16),
            jax.ShapeDtypeStruct((T, D_FF), F32),
            jax.ShapeDtypeStruct((T, D_FF), F32),
            jax.ShapeDtypeStruct((T, D_FF), BF16),
            jax.ShapeDtypeStruct((T, D_MODEL), F32),
            jax.ShapeDtypeStruct((8, LANES), F32),
            jax.ShapeDtypeStruct((1, D_MODEL), F32),
        ],
        compiler_params=_params(("arbitrary",)),
    )(x1, g2, gf, tgt, w_gate, w_up, w_down)


def _ffn_bwd(dx2, gate, up, x1, g2, w_gate, w_up, w_down):
    T = x1.shape[0]
    tm = ROW_TILE
    nc = D_FF // FF_CHUNK

    def body(dx2_ref, gate_ref, up_ref, x1_ref, g2_ref, wg_ref, wu_ref, wd_ref, dgate_ref, dup_ref, dx1_ref, dg2_ref):
        dx2v = dx2_ref[...]
        dx2b = dx2v.astype(BF16)
        dh2 = jnp.zeros((tm, D_MODEL), F32)
        for c in range(nc):
            sl = slice(c * FF_CHUNK, (c + 1) * FF_CHUNK)
            dact = _mm_nt(dx2b, wd_ref[sl, :])
            gate = gate_ref[:, sl]
            sg = _sigmoid(gate)
            silu = gate * sg
            dgate = (dact * up_ref[:, sl] * (sg * (1.0 + gate * (1.0 - sg)))).astype(BF16)
            dup = (dact * silu).astype(BF16)
            dgate_ref[:, sl] = dgate
            dup_ref[:, sl] = dup
            dh2 = dh2 + _mm_nt(dgate, wg_ref[:, sl]) + _mm_nt(dup, wu_ref[:, sl])
        g2v = g2_ref[...]
        _, xh, r = _rms_fwd(x1_ref[...], g2v)
        dxn, dgrow = _rms_bwd(dh2, xh, r, g2v)
        dx1_ref[...] = dx2v + dxn

        @pl.when(pl.program_id(0) == 0)
        def _():
            dg2_ref[...] = jnp.zeros_like(dg2_ref)

        dg2_ref[...] += jnp.sum(dgrow, axis=0, keepdims=True)

    row = lambda n: pl.BlockSpec((tm, n), lambda i: (i, 0))
    return pl.pallas_call(
        body,
        name="ffn_bwd",
        grid=(T // tm,),
        in_specs=[
            row(D_MODEL), row(D_FF), row(D_FF), row(D_MODEL), _const_spec((1, D_MODEL)),
            _const_spec(w_gate.shape), _const_spec(w_up.shape), _const_spec(w_down.shape),
        ],
        out_specs=[row(D_FF), row(D_FF), row(D_MODEL), pl.BlockSpec((1, D_MODEL), lambda i: (0, 0))],
        out_shape=[
            jax.ShapeDtypeStruct((T, D_FF), BF16),
            jax.ShapeDtypeStruct((T, D_FF), BF16),
            jax.ShapeDtypeStruct((T, D_MODEL), F32),
            jax.ShapeDtypeStruct((1, D_MODEL), F32),
        ],
        compiler_params=_params(("arbitrary",)),
    )(dx2, gate, up, x1, g2, w_gate, w_up, w_down)


def _mix_bwd(dx1, gates, pool_y, attn_y, p2, scale, w_out, w_ao, w_po):
    T = dx1.shape[0]
    tm = ROW_TILE

    def body(dx1_ref, gt_ref, py_ref, ay_ref, p2_ref, sc_ref, wout_ref, wao_ref, wpo_ref, dgt_ref, dpy_ref, day_ref, da_ref, dp2_ref, dsc_ref):
        dm = _mm_nt(dx1_ref[...].astype(BF16), wout_ref[...])
        sp = _sigmoid(gt_ref[:, :D_MODEL])
        sa = _sigmoid(gt_ref[:, D_MODEL:])
        dgt_ref[:, :D_MODEL] = (dm * py_ref[...] * (sp * (1.0 - sp))).astype(BF16)
        dgt_ref[:, D_MODEL:] = (dm * ay_ref[...] * (sa * (1.0 - sa))).astype(BF16)
        dpy = (dm * sp).astype(BF16)
        day = (dm * sa).astype(BF16)
        dpy_ref[...] = dpy
        day_ref[...] = day
        da_ref[...] = _mm_nt(day, wao_ref[...]).astype(BF16)
        dp3 = _mm_nt(dpy, wpo_ref[...])
        dp2_ref[...] = (dp3 * sc_ref[...]).astype(BF16)

        @pl.when(pl.program_id(0) == 0)
        def _():
            dsc_ref[...] = jnp.zeros_like(dsc_ref)

        dsc_ref[...] += jnp.sum(dp3 * p2_ref[...], axis=0, keepdims=True)

    row = lambda n: pl.BlockSpec((tm, n), lambda i: (i, 0))
    return pl.pallas_call(
        body,
        name="mix_bwd",
        grid=(T // tm,),
        in_specs=[
            row(D_MODEL), row(2 * D_MODEL), row(D_MODEL), row(D_MODEL), row(POOL_WIDTH), _const_spec((1, POOL_WIDTH)),
            _const_spec(w_out.shape), _const_spec(w_ao.shape), _const_spec(w_po.shape),
        ],
        out_specs=[row(2 * D_MODEL), row(D_MODEL), row(D_MODEL), row(ATTN_WIDTH), row(POOL_WIDTH), pl.BlockSpec((1, POOL_WIDTH), lambda i: (0, 0))],
        out_shape=[
            jax.ShapeDtypeStruct((T, 2 * D_MODEL), BF16),
            jax.ShapeDtypeStruct((T, D_MODEL), BF16),
            jax.ShapeDtypeStruct((T, D_MODEL), BF16),
            jax.ShapeDtypeStruct((T, ATTN_WIDTH), BF16),
            jax.ShapeDtypeStruct((T, POOL_WIDTH), BF16),
            jax.ShapeDtypeStruct((1, POOL_WIDTH), F32),
        ],
        compiler_params=_params(("arbitrary",)),
    )(dx1, gates, pool_y, attn_y, p2, scale, w_out, w_ao, w_po)


def _pool_bwd(dp2, pm, mix_b, n_seq, S):
    T = n_seq * S

    def body(dp2_ref, pm_ref, mix_ref, du_ref, dmix_ref):
        g = pl.program_id(0)
        dp2v = dp2_ref[...]
        dpm = _mm_nt(dp2v, mix_ref[...])
        row = lax.broadcasted_iota(jnp.int32, dpm.shape, 0)
        w = _window_pick(g, 2.0, 4.0, 8.0, 16.0)
        e = dpm / jnp.minimum((row + 1).astype(F32), w)

        def ahead(a, k):
            return jnp.where(row < S - k, pltpu.roll(a, S - k, 0), 0.0)

        r2 = e + ahead(e, 1)
        r4 = r2 + ahead(r2, 2)
        r8 = r4 + ahead(r4, 4)
        r16 = r8 + ahead(r8, 8)
        du_ref[...] = (_window_pick(g, r2, r4, r8, r16) - dpm).astype(BF16)

        @pl.when(pl.program_id(1) == 0)
        def _():
            dmix_ref[...] = jnp.zeros_like(dmix_ref)

        dmix_ref[...] += _mm_tn(pm_ref[...], dp2v)

    grp = pl.BlockSpec((S, GROUP_DIM), lambda g, s: (s, g))
    mixs = pl.BlockSpec((None, GROUP_DIM, GROUP_DIM), lambda g, s: (g, 0, 0))
    return pl.pallas_call(
        body,
        name="pool_bwd",
        grid=(len(POOL_WINDOWS), n_seq),
        in_specs=[grp, grp, mixs],
        out_specs=[grp, mixs],
        out_shape=[jax.ShapeDtypeStruct((T, POOL_WIDTH), BF16), jax.ShapeDtypeStruct((len(POOL_WINDOWS), GROUP_DIM, GROUP_DIM), F32)],
        compiler_params=_params(("parallel", "arbitrary")),
    )(dp2, pm, mix_b)


def _attn_bwd(qkv, da, a, fcol, lse, n_seq, S):
    T = n_seq * S
    tb = ATTN_BLOCK
    nb = S // tb
    scale = HEAD_DIM ** -0.5

    def body(q_ref, k_ref, v_ref, do_ref, o_ref, fc_ref, st_ref, dq_ref, dk_ref, dv_ref, dfk_ref, dfq_ref,
             qa_sc, doa_sc, dq_acc, ka_sc, va_sc, dk_sc, dv_sc):
        j = pl.program_id(1)
        lane = lax.broadcasted_iota(jnp.int32, (1, LANES), 1)
        low = lane < HEAD_DIM
        ones = (1.0, 1.0, 1.0)
        zeros = (0.0, 0.0, 0.0)

        @pl.when(j == 0)
        def _():
            dq_acc[...] = jnp.zeros_like(dq_acc)

            def rows_q(i, carry):
                r0 = pl.multiple_of(i * tb, tb)
                for h in range(N_HEADS):
                    pair = slice((h // 2) * LANES, (h // 2 + 1) * LANES)
                    qp = q_ref[pl.ds(r0, tb), pair]
                    dop = do_ref[pl.ds(r0, tb), pair]
                    prod = dop.astype(F32) * o_ref[pl.ds(r0, tb), pair].astype(F32)
                    head = (lane >= HEAD_DIM * (h % 2)) & (lane < HEAD_DIM * (h % 2 + 1))
                    delta = jnp.sum(jnp.where(head, prod, 0.0), axis=1, keepdims=True)
                    cq = fc_ref[pl.ds(r0, tb), h : h + 1] - st_ref[pl.ds(r0, tb), h : h + 1]
                    qa_sc[h, pl.ds(r0, tb), :] = _augment(qp, h % 2, _split3(cq), ones)
                    doa_sc[h, pl.ds(r0, tb), :] = _augment(dop, h % 2, _split3(-delta), zeros)
                return carry

            lax.fori_loop(0, nb, rows_q, 0)

        c0 = pl.multiple_of(j * tb, tb)
        for h in range(N_HEADS):
            pair = slice((h // 2) * LANES, (h // 2 + 1) * LANES)
            kp = k_ref[:, pair] * scale
            ka_sc[h] = _augment(kp, h % 2, ones, _split3(-fc_ref[pl.ds(c0, tb), h : h + 1]))
            va_sc[h] = _augment(v_ref[:, pair], h % 2, ones, zeros)
        dk_sc[...] = jnp.zeros_like(dk_sc)
        dv_sc[...] = jnp.zeros_like(dv_sc)
        causal = lax.broadcasted_iota(jnp.int32, (tb, tb), 1) <= lax.broadcasted_iota(jnp.int32, (tb, tb), 0)

        def step(i, masked):
            r0 = pl.multiple_of(i * tb, tb)
            for h in range(N_HEADS):
                dob = do_ref[pl.ds(r0, tb), (h // 2) * LANES : (h // 2 + 1) * LANES]
                qa = qa_sc[h, pl.ds(r0, tb), :]
                s = _mm_nt(qa, ka_sc[h])
                if masked:
                    s = jnp.where(causal, s, -jnp.inf)
                pr = jnp.exp(s)
                dv_sc[h] += _mm_tn(pr.astype(BF16), dob)
                dsb = (pr * _mm_nt(doa_sc[h, pl.ds(r0, tb), :], va_sc[h])).astype(BF16)
                dk_sc[h] += _mm_tn(dsb, qa)
                dq_acc[h, pl.ds(r0, tb), :] += _mm(dsb, ka_sc[h])

        step(j, True)

        def loop_body(i, carry):
            step(i, False)
            return carry

        lax.fori_loop(j + 1, nb, loop_body, 0)
        dfk = jnp.zeros((tb, LANES), F32)
        for p in range(N_PAIRS):
            dk_ref[:, p * LANES : (p + 1) * LANES] = (jnp.where(low, dk_sc[2 * p], dk_sc[2 * p + 1]) * scale).astype(BF16)
            dv_ref[:, p * LANES : (p + 1) * LANES] = jnp.where(low, dv_sc[2 * p], dv_sc[2 * p + 1]).astype(BF16)
            for hh in range(2):
                b = HEAD_DIM * (1 - hh) + 3
                dfk = jnp.where(lane == 2 * p + hh, -dk_sc[2 * p + hh][:, b : b + 1], dfk)
        dfk_ref[...] = dfk

        @pl.when(j == nb - 1)
        def _():
            def rows_dq(i, carry):
                r0 = pl.multiple_of(i * tb, tb)
                dfq = jnp.zeros((tb, LANES), F32)
                for p in range(N_PAIRS):
                    parts = [dq_acc[2 * p + hh, pl.ds(r0, tb), :] for hh in range(2)]
                    dq_ref[pl.ds(r0, tb), p * LANES : (p + 1) * LANES] = jnp.where(low, parts[0], parts[1]).astype(BF16)
                    for hh in range(2):
                        b = HEAD_DIM * (1 - hh)
                        dfq = jnp.where(lane == 2 * p + hh, parts[hh][:, b : b + 1], dfq)
                dfq_ref[pl.ds(r0, tb), :] = dfq
                return carry

            lax.fori_loop(0, nb, rows_dq, 0)

    seq = lambda w, col: pl.BlockSpec((S, w), lambda s, j: (s, col))
    blk = lambda w, col: pl.BlockSpec((tb, w), lambda s, j: (s * nb + j, col))
    return pl.pallas_call(
        body,
        name="attn_bwd",
        grid=(n_seq, nb),
        in_specs=[seq(ATTN_WIDTH, 0), blk(ATTN_WIDTH, 1), blk(ATTN_WIDTH, 2), seq(ATTN_WIDTH, 0), seq(ATTN_WIDTH, 0), seq(LANES, 0), seq(LANES, 0)],
        out_specs=[seq(ATTN_WIDTH, 0), blk(ATTN_WIDTH, 0), blk(ATTN_WIDTH, 0), blk(LANES, 0), seq(LANES, 0)],
        out_shape=[
            jax.ShapeDtypeStruct((T, ATTN_WIDTH), BF16),
            jax.ShapeDtypeStruct((T, ATTN_WIDTH), BF16),
            jax.ShapeDtypeStruct((T, ATTN_WIDTH), BF16),
            jax.ShapeDtypeStruct((T, LANES), F32),
            jax.ShapeDtypeStruct((T, LANES), F32),
        ],
        scratch_shapes=[
            pltpu.VMEM((N_HEADS, S, LANES), BF16),
            pltpu.VMEM((N_HEADS, S, LANES), BF16),
            pltpu.VMEM((N_HEADS, S, LANES), F32),
            pltpu.VMEM((N_HEADS, tb, LANES), BF16),
            pltpu.VMEM((N_HEADS, tb, LANES), BF16),
            pltpu.VMEM((N_HEADS, tb, LANES), F32),
            pltpu.VMEM((N_HEADS, tb, LANES), F32),
        ],
        compiler_params=_params(("parallel", "arbitrary")),
    )(qkv, qkv, qkv, da, a, fcol, lse)


def _forget_bwd(dfk, dfq, fl, b_pad, n_seq, S):
    def body(df_ref, dfq_ref, fl_ref, b_ref, dfl_ref, db_ref):
        t = (df_ref[...] + dfq_ref[...]).T
        lane = lax.broadcasted_iota(jnp.int32, t.shape, 1)
        k = 1
        while k < S:
            t = t + jnp.where(lane < S - k, pltpu.roll(t, S - k, 1), 0.0)
            k *= 2
        dfl = t.T * _sigmoid(-(fl_ref[...] + b_ref[...]))
        dfl_ref[...] = dfl.astype(BF16)

        @pl.when(pl.program_id(0) == 0)
        def _():
            db_ref[...] = jnp.zeros_like(db_ref)

        db_ref[...] += jnp.sum(dfl, axis=0, keepdims=True)

    return pl.pallas_call(
        body,
        name="forget_bwd",
        grid=(n_seq,),
        in_specs=[
            pl.BlockSpec((S, LANES), lambda s: (s, 0)),
            pl.BlockSpec((S, LANES), lambda s: (s, 0)),
            pl.BlockSpec((S, FL_PAD), lambda s: (s, 0)),
            _const_spec((1, FL_PAD)),
        ],
        out_specs=[pl.BlockSpec((S, FL_PAD), lambda s: (s, 0)), pl.BlockSpec((1, FL_PAD), lambda s: (0, 0))],
        out_shape=[jax.ShapeDtypeStruct((n_seq * S, FL_PAD), BF16), jax.ShapeDtypeStruct((1, FL_PAD), F32)],
        compiler_params=_params(("arbitrary",)),
    )(dfk, dfq, fl, b_pad)


def _in_proj_bwd(du, dq, dk, dv, dfl, dgates, x, dx1, g1, w_uqkv, w_fl, w_g):
    T = x.shape[0]
    tm = ROW_TILE

    def body(du_ref, dq_ref, dk_ref, dv_ref, dfl_ref, dgt_ref, x_ref, dx1_ref, g_ref, wa_ref, wf_ref, wg_ref, dx_ref, dg_ref):
        dh = _mm_nt(dgt_ref[...], wg_ref[...]) + _mm_nt(dfl_ref[...], wf_ref[...])
        for n, ref in enumerate((du_ref, dq_ref, dk_ref, dv_ref)):
            dh = dh + _mm_nt(ref[...], wa_ref[:, n * 512 : (n + 1) * 512])
        gv = g_ref[...]
        _, xh, r = _rms_fwd(x_ref[...], gv)
        dxn, dgrow = _rms_bwd(dh, xh, r, gv)
        dx_ref[...] = dx1_ref[...] + dxn

        @pl.when(pl.program_id(0) == 0)
        def _():
            dg_ref[...] = jnp.zeros_like(dg_ref)

        dg_ref[...] += jnp.sum(dgrow, axis=0, keepdims=True)

    row = lambda n: pl.BlockSpec((tm, n), lambda i: (i, 0))
    return pl.pallas_call(
        body,
        name="in_proj_bwd",
        grid=(T // tm,),
        in_specs=[
            row(512), row(512), row(512), row(512), row(FL_PAD), row(2 * D_MODEL), row(D_MODEL), row(D_MODEL), _const_spec((1, D_MODEL)),
            _const_spec(w_uqkv.shape), _const_spec(w_fl.shape), _const_spec(w_g.shape),
        ],
        out_specs=[row(D_MODEL), pl.BlockSpec((1, D_MODEL), lambda i: (0, 0))],
        out_shape=[jax.ShapeDtypeStruct((T, D_MODEL), F32), jax.ShapeDtypeStruct((1, D_MODEL), F32)],
        compiler_params=_params(("arbitrary",)),
    )(du, dq, dk, dv, dfl, dgates, x, dx1, g1, w_uqkv, w_fl, w_g)


def _pick_block(n):
    for b in (512, 1408, 256, 128):
        if n % b == 0:
            return b
    raise ValueError(n)


def _matmul_tn(a, b, name):
    T, K = a.shape
    N = b.shape[1]
    bt, bk, bn = 512, _pick_block(K), _pick_block(N)

    def body(a_ref, b_ref, o_ref):
        @pl.when(pl.program_id(2) == 0)
        def _():
            o_ref[...] = jnp.zeros_like(o_ref)

        o_ref[...] += _mm_tn(a_ref[...].astype(BF16), b_ref[...].astype(BF16))

    return pl.pallas_call(
        body,
        name=name,
        grid=(K // bk, N // bn, T // bt),
        in_specs=[pl.BlockSpec((bt, bk), lambda k, n, t: (t, k)), pl.BlockSpec((bt, bn), lambda k, n, t: (t, n))],
        out_specs=pl.BlockSpec((bk, bn), lambda k, n, t: (k, n)),
        out_shape=jax.ShapeDtypeStruct((K, N), F32),
        compiler_params=_params(("parallel", "parallel", "arbitrary")),
    )(a, b)


def _position():
    return lax.axis_index("x"), lax.axis_index("y"), lax.axis_index("c")


def _all_gather(block, name):
    R = block.shape[0]

    def body(x_ref, out_ref, send_sems, recv_sems, local_sem):
        x, y, c = _position()
        me, sibling = (x, y, c), (x, y, 1 - c)
        chips = [(1 - x, y), (x, 1 - y), (1 - x, 1 - y)]

        def rows(px, py, pc):
            return out_ref.at[4 * px + 2 * py + pc]

        def copy(k, blk, to, src=None):
            return pltpu.make_async_remote_copy(
                src_ref=rows(*blk) if src is None else src, dst_ref=rows(*blk),
                send_sem=send_sems.at[k], recv_sem=recv_sems.at[k], device_id=to, device_id_type=MESH,
            )

        mine = pltpu.make_async_copy(x_ref, rows(*me), local_sem)
        mine.start()
        first = [copy(0, me, sibling, src=x_ref)]
        first += [copy(1 + n, me, (*chip, c), src=x_ref) for n, chip in enumerate(chips)]
        for cp in first:
            cp.start()
        passed = [copy(4 + n, (*chip, c), sibling) for n, chip in enumerate(chips)]
        for n, chip in enumerate(chips):
            copy(1 + n, (*chip, c), me).wait_recv()
            passed[n].start()
        copy(0, sibling, me).wait_recv()
        for n, chip in enumerate(chips):
            copy(4 + n, (*chip, 1 - c), me).wait_recv()
        for cp in first + passed:
            cp.wait_send()
        mine.wait()

    return pl.pallas_call(
        body,
        name=name,
        out_shape=jax.ShapeDtypeStruct((N_DEV, R, LANES), block.dtype),
        in_specs=[pl.BlockSpec(memory_space=pl.ANY)],
        out_specs=pl.BlockSpec(memory_space=pl.ANY),
        scratch_shapes=[pltpu.SemaphoreType.DMA((7,)), pltpu.SemaphoreType.DMA((7,)), pltpu.SemaphoreType.DMA],
    )(block)


def _sibling_exchange(send):
    _, _, R, _ = send.shape

    def body(s_ref, r_ref, send_sem, recv_sem):
        x, y, c = _position()
        cp = pltpu.make_async_remote_copy(
            src_ref=s_ref.at[1 - c], dst_ref=r_ref, send_sem=send_sem, recv_sem=recv_sem,
            device_id=(x, y, 1 - c), device_id_type=MESH,
        )
        cp.start()
        cp.wait()

    return pl.pallas_call(
        body,
        name="rs_sibling",
        out_shape=jax.ShapeDtypeStruct((4, R, LANES), send.dtype),
        in_specs=[pl.BlockSpec(memory_space=pl.ANY)],
        out_specs=pl.BlockSpec(memory_space=pl.ANY),
        scratch_shapes=[pltpu.SemaphoreType.DMA, pltpu.SemaphoreType.DMA],
    )(send)


def _pair_sum(send, got, core):
    _, _, R, _ = send.shape
    br = _row_block(R)

    def body(core_ref, a_ref, b_ref, o_ref):
        o_ref[...] = (a_ref[...].astype(F32) + b_ref[...].astype(F32)).astype(o_ref.dtype)

    return pl.pallas_call(
        body,
        name="rs_pair_sum",
        grid_spec=pltpu.PrefetchScalarGridSpec(
            num_scalar_prefetch=1,
            grid=(4, R // br),
            in_specs=[
                pl.BlockSpec((None, None, br, LANES), lambda n, i, core: (core[0], n, i, 0)),
                pl.BlockSpec((None, br, LANES), lambda n, i, core: (n, i, 0)),
            ],
            out_specs=pl.BlockSpec((None, br, LANES), lambda n, i, core: (n, i, 0)),
        ),
        out_shape=jax.ShapeDtypeStruct((4, R, LANES), send.dtype),
        compiler_params=_params(("parallel", "parallel")),
    )(core, send, got)


def _chip_exchange(pair):
    _, R, _ = pair.shape

    def body(p_ref, r_ref, send_sems, recv_sems):
        x, y, c = _position()
        chips = [(1 - x, y), (x, 1 - y), (1 - x, 1 - y)]
        cps = [
            pltpu.make_async_remote_copy(
                src_ref=p_ref.at[2 * cx + cy], dst_ref=r_ref.at[n], send_sem=send_sems.at[n], recv_sem=recv_sems.at[n],
                device_id=(cx, cy, c), device_id_type=MESH,
            )
            for n, (cx, cy) in enumerate(chips)
        ]
        for cp in cps:
            cp.start()
        for cp in cps:
            cp.wait()

    return pl.pallas_call(
        body,
        name="rs_chips",
        out_shape=jax.ShapeDtypeStruct((3, R, LANES), pair.dtype),
        in_specs=[pl.BlockSpec(memory_space=pl.ANY)],
        out_specs=pl.BlockSpec(memory_space=pl.ANY),
        scratch_shapes=[pltpu.SemaphoreType.DMA((3,)), pltpu.SemaphoreType.DMA((3,))],
    )(pair)


def _adamw(w, g, m, v):
    m = ADAM_B1 * m + (1.0 - ADAM_B1) * g
    v = ADAM_B2 * v + (1.0 - ADAM_B2) * (g * g)
    m_hat = m / (1.0 - ADAM_B1 ** ADAM_STEP)
    v_hat = v / (1.0 - ADAM_B2 ** ADAM_STEP)
    delta = -ADAM_LR * (m_hat / (jnp.sqrt(v_hat) + ADAM_EPS) + ADAM_WD * w)
    return delta, m, v


def _row_block(R):
    for b in range(min(R, 2048) // _PACK_ALIGN * _PACK_ALIGN, 0, -_PACK_ALIGN):
        if R % b == 0:
            return b
    raise ValueError(R)


def _shard_update(send, got, recv, w, m, v, pos):
    R = w.shape[0]
    br = _row_block(R)

    def body(pos_ref, a_ref, b_ref, r_ref, w_ref, m_ref, v_ref, g_ref, d_ref, nm_ref, nv_ref):
        g = a_ref[...].astype(F32) + b_ref[...].astype(F32)
        for n in range(3):
            g = g + r_ref[n].astype(F32)
        g_ref[...] = g
        d_ref[...], nm_ref[...], nv_ref[...] = _adamw(w_ref[...], g, m_ref[...], v_ref[...])

    flat = pl.BlockSpec((br, LANES), lambda i, pos: (i, 0))
    return pl.pallas_call(
        body,
        name="shard_update",
        grid_spec=pltpu.PrefetchScalarGridSpec(
            num_scalar_prefetch=1,
            grid=(R // br,),
            in_specs=[
                pl.BlockSpec((None, None, br, LANES), lambda i, pos: (pos[0], pos[1], i, 0)),
                pl.BlockSpec((None, br, LANES), lambda i, pos: (pos[1], i, 0)),
                pl.BlockSpec((3, br, LANES), lambda i, pos: (0, i, 0)),
                flat, flat, flat,
            ],
            out_specs=[flat, flat, flat, flat],
        ),
        out_shape=[jax.ShapeDtypeStruct((R, LANES), F32)] * 4,
        compiler_params=_params(("parallel",)),
    )(pos, send, got, recv, w, m, v)


def _small_update(parts, w, m, v):
    R = w.shape[0]

    def body(p_ref, w_ref, m_ref, v_ref, g_ref, d_ref, nm_ref, nv_ref):
        g = p_ref[0]
        for n in range(1, N_DEV):
            g = g + p_ref[n]
        g_ref[...] = g
        d_ref[...], nm_ref[...], nv_ref[...] = _adamw(w_ref[...], g, m_ref[...], v_ref[...])

    return pl.pallas_call(
        body,
        name="small_update",
        out_shape=[jax.ShapeDtypeStruct((R, LANES), F32)] * 4,
        compiler_params=pltpu.CompilerParams(vmem_limit_bytes=VMEM_LIMIT),
    )(parts, w, m, v)


_SHARDED = (
    ("w_in", (1024, 513), 1),
    ("w_pool_out", (512, 128), 1),
    ("w_attn_out", (512, 128), 1),
    ("w_out", (128, 1024), 0),
    ("w_ffn_gate", (1024, 352), 1),
    ("w_ffn_up", (1024, 352), 1),
    ("w_ffn_down", (352, 1024), 0),
)
_PACK_ALIGN = 16


def _packed_rows(shape):
    n = shape[0] * shape[1] // LANES
    return -(-n // _PACK_ALIGN) * _PACK_ALIGN


def _pack(shards, dtype):
    parts = []
    for (name, shape, _), t in zip(_SHARDED, shards):
        lead = t.shape[:-2]
        n = shape[0] * shape[1] // LANES
        f = t.astype(dtype).reshape(*lead, n, LANES)
        pad = _packed_rows(shape) - n
        if pad:
            f = jnp.concatenate([f, jnp.zeros((*lead, pad, LANES), dtype)], axis=-2)
        parts.append(f)
    return jnp.concatenate(parts, axis=-2)


def _unpack(packed):
    out, off = [], 0
    for name, shape, _ in _SHARDED:
        n = shape[0] * shape[1] // LANES
        out.append(packed[..., off : off + n, :].reshape(*packed.shape[:-2], *shape))
        off += _packed_rows(shape)
    return out


def _full_from_gathered(t, axis):
    if axis == 0:
        return t.reshape(N_DEV * t.shape[1], t.shape[2])
    return jnp.transpose(t, (1, 0, 2)).reshape(t.shape[1], N_DEV * t.shape[2])


def _shards_from_full(t, axis):
    if axis == 0:
        return t.reshape(N_DEV, t.shape[0] // N_DEV, t.shape[1])
    return jnp.transpose(t.reshape(t.shape[0], N_DEV, t.shape[1] // N_DEV), (1, 0, 2))


_SMALL = (("norm1_g", 8), ("norm2_g", 8), ("norm_f_g", 8), ("b_forget", 8), ("pool_scale", 8), ("pool_mix", 512))
_SMALL_ROWS = sum(r for _, r in _SMALL) + 8


def _pack_small(vals, loss_row):
    parts = []
    for (name, rows), t in zip(_SMALL, vals):
        f = t.astype(F32).reshape(-1)
        f = jnp.concatenate([f, jnp.zeros((rows * LANES - f.shape[0],), F32)]).reshape(rows, LANES)
        parts.append(f)
    parts.append(loss_row)
    return jnp.concatenate(parts, axis=0)


def _unpack_small(packed, shapes):
    out, off = [], 0
    for (name, rows), shape in zip(_SMALL, shapes):
        n = 1
        for s in shape:
            n *= s
        out.append(packed[off : off + rows].reshape(-1)[:n].reshape(shape))
        off += rows
    return out, packed[off, 0]


def _local_grads(x, tgt, g1, g2, gf, b_forget, pool_mix, pool_scale, w_in, w_po, w_ao, w_out, w_gate, w_up, w_down):
    n_seq, S, _ = x.shape
    T = n_seq * S
    x2 = x.reshape(T, D_MODEL)
    tg2 = tgt.reshape(T, D_MODEL)
    w_uqkv = w_in[:, : POOL_WIDTH + 3 * ATTN_WIDTH]
    w_fl = jnp.concatenate([w_in[:, 2048 : 2048 + N_HEADS], jnp.zeros((D_MODEL, FL_PAD - N_HEADS), BF16)], axis=1)
    w_g = w_in[:, 2048 + N_HEADS :]
    b_pad = jnp.concatenate([b_forget.reshape(1, N_HEADS), jnp.zeros((1, FL_PAD - N_HEADS), F32)], axis=1)
    mix_b = pool_mix.reshape(len(POOL_WINDOWS), GROUP_DIM, GROUP_DIM).astype(BF16)
    scale = pool_scale.reshape(1, POOL_WIDTH)
    g1 = g1.reshape(1, D_MODEL)
    g2 = g2.reshape(1, D_MODEL)
    gf = gf.reshape(1, D_MODEL)

    h, u, qkv, fl, gates = _in_proj(x2, g1, w_uqkv, w_fl, w_g)
    fcol = _forget_fwd(fl, b_pad, n_seq, S)
    pm, p2, p3, pool_y = _pool_fwd(u, mix_b, scale, w_po, n_seq, S)
    a, lse = _attn_fwd(qkv, fcol, n_seq, S)
    merged, x1, attn_y = _mix_out(a, pool_y, gates, x2, w_ao, w_out)
    h2, gate, up, act, dx2, loss_rows, dgf = _ffn_fwd(x1, g2, gf, tg2, w_gate, w_up, w_down)

    dgate, dup, dx1, dg2 = _ffn_bwd(dx2, gate, up, x1, g2, w_gate, w_up, w_down)
    dgates, dpy, day, da, dp2, dscale = _mix_bwd(dx1, gates, pool_y, attn_y, p2, scale, w_out, w_ao, w_po)
    du, dmix = _pool_bwd(dp2, pm, mix_b, n_seq, S)
    dq, dk, dv, dfk, dfq = _attn_bwd(qkv, da, a, fcol, lse, n_seq, S)
    dfl, db = _forget_bwd(dfk, dfq, fl, b_pad, n_seq, S)
    dx, dg1 = _in_proj_bwd(du, dq, dk, dv, dfl, dgates, x2, dx1, g1, w_uqkv, w_fl, w_g)

    d_w_in = jnp.concatenate(
        [
            _matmul_tn(h, du, "dw_u"), _matmul_tn(h, dq, "dw_q"), _matmul_tn(h, dk, "dw_k"), _matmul_tn(h, dv, "dw_v"),
            _matmul_tn(h, dfl, "dw_fl")[:, :N_HEADS], _matmul_tn(h, dgates, "dw_gates"),
        ],
        axis=1,
    )
    d_full = (
        d_w_in,
        _matmul_tn(p3, dpy, "dw_pool_out"),
        _matmul_tn(a, day, "dw_attn_out"),
        _matmul_tn(merged, dx1, "dw_out"),
        _matmul_tn(h2, dgate, "dw_ffn_gate"),
        _matmul_tn(h2, dup, "dw_ffn_up"),
        _matmul_tn(act, dx2, "dw_ffn_down"),
    )
    small = (dg1, dg2, dgf, db[:, :N_HEADS], dscale, dmix)
    return loss_rows, dx.reshape(n_seq, S, D_MODEL), d_full, small


def kernel(x, norm1_g, w_in, b_forget, pool_mix, pool_scale, w_pool_out, w_attn_out, w_out, norm2_g, w_ffn_gate, w_ffn_up, w_ffn_down, norm_f_g, loss_target, m_norm1_g, m_w_in, m_b_forget, m_pool_mix, m_pool_scale, m_w_pool_out, m_w_attn_out, m_w_out, m_norm2_g, m_w_ffn_gate, m_w_ffn_up, m_w_ffn_down, m_norm_f_g, v_norm1_g, v_w_in, v_b_forget, v_pool_mix, v_pool_scale, v_w_pool_out, v_w_attn_out, v_w_out, v_norm2_g, v_w_ffn_gate, v_w_ffn_up, v_w_ffn_down, v_norm_f_g):
    w_sh = (w_in[0], w_pool_out[0], w_attn_out[0], w_out[0], w_ffn_gate[0], w_ffn_up[0], w_ffn_down[0])
    m_sh = (m_w_in[0], m_w_pool_out[0], m_w_attn_out[0], m_w_out[0], m_w_ffn_gate[0], m_w_ffn_up[0], m_w_ffn_down[0])
    v_sh = (v_w_in[0], v_w_pool_out[0], v_w_attn_out[0], v_w_out[0], v_w_ffn_gate[0], v_w_ffn_up[0], v_w_ffn_down[0])

    gathered = _unpack(_all_gather(_pack(w_sh, BF16), "weights_all_gather"))
    whole = [_full_from_gathered(t, axis) for t, (_, _, axis) in zip(gathered, _SHARDED)]

    loss_rows, grad_x, d_full, small = _local_grads(x, loss_target, norm1_g, norm2_g, norm_f_g, b_forget, pool_mix, pool_scale, *whole)

    chunks = _pack([_shards_from_full(t, axis) for t, (_, _, axis) in zip(d_full, _SHARDED)], BF16)
    R = chunks.shape[1]
    send = jnp.transpose(chunks.reshape(4, 2, R, LANES), (1, 0, 2, 3))
    cx, cy, cc = _position()
    core = jnp.reshape(cc, (1,)).astype(jnp.int32)
    pos = jnp.stack([cc, 2 * cx + cy]).astype(jnp.int32)
    got = _sibling_exchange(send)
    pair = _pair_sum(send, got, core)
    recv = _chip_exchange(pair)
    g_p, d_p, nm_p, nv_p = _shard_update(send, got, recv, _pack(w_sh, F32), _pack(m_sh, F32), _pack(v_sh, F32), pos)
    lead = lambda ts: [t[None] for t in ts]
    g_w, d_w, nm_w, nv_w = lead(_unpack(g_p)), lead(_unpack(d_p)), lead(_unpack(nm_p)), lead(_unpack(nv_p))

    small_w = (norm1_g, norm2_g, norm_f_g, b_forget, pool_scale, pool_mix)
    small_m = (m_norm1_g, m_norm2_g, m_norm_f_g, m_b_forget, m_pool_scale, m_pool_mix)
    small_v = (v_norm1_g, v_norm2_g, v_norm_f_g, v_b_forget, v_pool_scale, v_pool_mix)
    zero_row = jnp.zeros((8, LANES), F32)
    parts = _all_gather(_pack_small(small, loss_rows), "small_all_gather")
    g_s, d_s, nm_s, nv_s = _small_update(parts, _pack_small(small_w, zero_row), _pack_small(small_m, zero_row), _pack_small(small_v, zero_row))
    shapes = [t.shape for t in small_w]
    (g1, g2, gf, gb, gsc, gmix), loss = _unpack_small(g_s, shapes)
    (d1, d2, df, db_, dsc, dmx), _ = _unpack_small(d_s, shapes)
    (m1, m2, mf, mb, msc, mmx), _ = _unpack_small(nm_s, shapes)
    (v1, v2, vf, vb, vsc, vmx), _ = _unpack_small(nv_s, shapes)

    def ordered(n1, win, b, mix, sc, wpo, wao, wout, n2, wg, wu, wd, nf):
        return (n1, win, b, mix, sc, wpo, wao, wout, n2, wg, wu, wd, nf)

    grads = ordered(g1, g_w[0], gb, gmix, gsc, g_w[1], g_w[2], g_w[3], g2, g_w[4], g_w[5], g_w[6], gf)
    deltas = ordered(d1, d_w[0], db_, dmx, dsc, d_w[1], d_w[2], d_w[3], d2, d_w[4], d_w[5], d_w[6], df)
    new_m = ordered(m1, nm_w[0], mb, mmx, msc, nm_w[1], nm_w[2], nm_w[3], m2, nm_w[4], nm_w[5], nm_w[6], mf)
    new_v = ordered(v1, nv_w[0], vb, vmx, vsc, nv_w[1], nv_w[2], nv_w[3], v2, nv_w[4], nv_w[5], nv_w[6], vf)
    return (loss, grad_x, *grads, *deltas, *new_m, *new_v)
```

```python
import functools

import jax
import jax.numpy as jnp
from jax import lax
from jax.experimental import pallas as pl
from jax.experimental.pallas import tpu as pltpu

F32 = jnp.float32
BF16 = jnp.bfloat16
MESH = pl.DeviceIdType.MESH

D_MODEL = 1024
POOL_WINDOWS = (2, 4, 8, 16)
POOL_WIDTH = 512
GROUP_DIM = 128
ATTN_WIDTH = 512
HEAD_DIM = 64
N_HEADS = 8
N_PAIRS = 4
D_FF = 2816
RMS_EPS = 1e-6
N_DEV = 8
LANES = 128
FL_PAD = 128

ADAM_LR = 0.001
ADAM_B1 = 0.9
ADAM_B2 = 0.999
ADAM_EPS = 1e-08
ADAM_WD = 0.01
ADAM_STEP = 10

VMEM_LIMIT = 56 * 1024 * 1024
ROW_TILE = 256
ATTN_BLOCK = 256
FF_CHUNK = 256


def _mm(a, b):
    return jnp.dot(a, b, preferred_element_type=F32)


def _mm_nt(a, b):
    return lax.dot_general(a, b, (((1,), (1,)), ((), ())), preferred_element_type=F32)


def _mm_tn(a, b):
    return lax.dot_general(a, b, (((0,), (0,)), ((), ())), preferred_element_type=F32)


def _sigmoid(x):
    return 1.0 / (1.0 + jnp.exp(-x))


def _params(sem, vmem=VMEM_LIMIT):
    return pltpu.CompilerParams(dimension_semantics=sem, vmem_limit_bytes=vmem)


def _const_spec(shape):
    nd = len(shape)
    return pl.BlockSpec(shape, lambda *_: (0,) * nd, pipeline_mode=pl.Buffered(1))


def _rms_fwd(x, g):
    r = lax.rsqrt(jnp.mean(x * x, axis=-1, keepdims=True) + RMS_EPS)
    xh = x * r
    return xh * g, xh, r


def _rms_bwd(dy, xh, r, g):
    dxh = dy * g
    dx = r * (dxh - xh * jnp.mean(dxh * xh, axis=-1, keepdims=True))
    return dx, dy * xh


def _in_proj(x, g1, w_uqkv, w_fl, w_g):
    T = x.shape[0]
    tm = ROW_TILE

    def body(x_ref, g_ref, wa_ref, wf_ref, wg_ref, h_ref, u_ref, qkv_ref, fl_ref, gt_ref):
        h, _, _ = _rms_fwd(x_ref[...], g_ref[...])
        hb = h.astype(BF16)
        h_ref[...] = hb
        z = _mm(hb, wa_ref[...])
        u_ref[...] = z[:, :POOL_WIDTH]
        qkv_ref[...] = z[:, POOL_WIDTH:].astype(BF16)
        fl_ref[...] = _mm(hb, wf_ref[...])
        gt_ref[...] = _mm(hb, wg_ref[...])

    row = lambda n: pl.BlockSpec((tm, n), lambda i: (i, 0))
    return pl.pallas_call(
        body,
        name="in_proj",
        grid=(T // tm,),
        in_specs=[row(D_MODEL), _const_spec((1, D_MODEL)), _const_spec(w_uqkv.shape), _const_spec(w_fl.shape), _const_spec(w_g.shape)],
        out_specs=[row(D_MODEL), row(POOL_WIDTH), row(3 * ATTN_WIDTH), row(FL_PAD), row(2 * D_MODEL)],
        out_shape=[
            jax.ShapeDtypeStruct((T, D_MODEL), BF16),
            jax.ShapeDtypeStruct((T, POOL_WIDTH), F32),
            jax.ShapeDtypeStruct((T, 3 * ATTN_WIDTH), BF16),
            jax.ShapeDtypeStruct((T, FL_PAD), F32),
            jax.ShapeDtypeStruct((T, 2 * D_MODEL), F32),
        ],
        compiler_params=_params(("parallel",)),
    )(x, g1, w_uqkv, w_fl, w_g)


def _log_sigmoid(x):
    return jnp.minimum(x, 0.0) - jnp.log(1.0 + jnp.exp(-jnp.abs(x)))


def _forget_fwd(fl, b_pad, n_seq, S):
    def body(fl_ref, b_ref, fcol_ref):
        lf = _log_sigmoid(fl_ref[...] + b_ref[...])
        t = lf.T
        lane = lax.broadcasted_iota(jnp.int32, t.shape, 1)
        k = 1
        while k < S:
            t = t + jnp.where(lane >= k, pltpu.roll(t, k, 1), 0.0)
            k *= 2
        fcol_ref[...] = t.T

    return pl.pallas_call(
        body,
        name="forget_fwd",
        grid=(n_seq,),
        in_specs=[pl.BlockSpec((S, FL_PAD), lambda s: (s, 0)), _const_spec((1, FL_PAD))],
        out_specs=pl.BlockSpec((S, FL_PAD), lambda s: (s, 0)),
        out_shape=jax.ShapeDtypeStruct((n_seq * S, FL_PAD), F32),
        compiler_params=_params(("parallel",)),
    )(fl, b_pad)


def _window_pick(g, v2, v4, v8, v16):
    return jnp.where(g == 0, v2, jnp.where(g == 1, v4, jnp.where(g == 2, v8, v16)))


def _pool_fwd(u, mix_b, scale, w_po, n_seq, S):
    T = n_seq * S

    def body(u_ref, mix_ref, sc_ref, wpo_ref, pm_ref, p2_ref, p3_ref, py_ref):
        g = pl.program_id(1)
        uu = u_ref[...]
        row = lax.broadcasted_iota(jnp.int32, uu.shape, 0)

        def back(a, k):
            return jnp.where(row >= k, pltpu.roll(a, k, 0), 0.0)

        s2 = uu + back(uu, 1)
        s4 = s2 + back(s2, 2)
        s8 = s4 + back(s4, 4)
        s16 = s8 + back(s8, 8)
        w = _window_pick(g, 2.0, 4.0, 8.0, 16.0)
        cnt = jnp.minimum((row + 1).astype(F32), w)
        pm = _window_pick(g, s2, s4, s8, s16) / cnt - uu
        pmb = pm.astype(BF16)
        pm_ref[...] = pmb
        p2 = _mm(pmb, mix_ref[...])
        p2_ref[...] = p2
        p3 = (p2 * sc_ref[...]).astype(BF16)
        p3_ref[...] = p3

        @pl.when(g == 0)
        def _():
            py_ref[...] = jnp.zeros_like(py_ref)

        py_ref[...] += _mm(p3, wpo_ref[...])

    grp = pl.BlockSpec((S, GROUP_DIM), lambda s, g: (s, g))
    return pl.pallas_call(
        body,
        name="pool_fwd",
        grid=(n_seq, len(POOL_WINDOWS)),
        in_specs=[
            grp,
            pl.BlockSpec((None, GROUP_DIM, GROUP_DIM), lambda s, g: (g, 0, 0)),
            pl.BlockSpec((1, GROUP_DIM), lambda s, g: (0, g)),
            pl.BlockSpec((GROUP_DIM, D_MODEL), lambda s, g: (g, 0)),
        ],
        out_specs=[grp, grp, grp, pl.BlockSpec((S, D_MODEL), lambda s, g: (s, 0))],
        out_shape=[
            jax.ShapeDtypeStruct((T, POOL_WIDTH), BF16),
            jax.ShapeDtypeStruct((T, POOL_WIDTH), F32),
            jax.ShapeDtypeStruct((T, POOL_WIDTH), BF16),
            jax.ShapeDtypeStruct((T, D_MODEL), F32),
        ],
        compiler_params=_params(("parallel", "arbitrary")),
    )(u, mix_b, scale, w_po)


def _split3(v):
    hi = v.astype(BF16).astype(F32)
    r = v - hi
    mid = r.astype(BF16).astype(F32)
    lo = (r - mid).astype(BF16).astype(F32)
    return hi, mid, lo


def _augment(xp, hh, first, second):
    lane = lax.broadcasted_iota(jnp.int32, (1, LANES), 1)
    head = (lane >= HEAD_DIM * hh) & (lane < HEAD_DIM * (hh + 1))
    b = HEAD_DIM * (1 - hh)
    out = jnp.where(head, xp.astype(F32), 0.0)
    for n, col in enumerate(tuple(first) + tuple(second)):
        out = jnp.where(lane == b + n, col, out)
    return out.astype(BF16)


def _attn_fwd(qkv, fcol, n_seq, S):
    T = n_seq * S
    tb = ATTN_BLOCK
    nq = S // tb
    scale = HEAD_DIM ** -0.5

    def body(q_ref, k_ref, v_ref, fc_ref, o_ref, st_ref, qa_sc, ka_sc, m_sc, l_sc, acc_sc):
        i = pl.program_id(1)
        lane = lax.broadcasted_iota(jnp.int32, (1, LANES), 1)
        low = lane < HEAD_DIM
        ones = (1.0, 1.0, 1.0)

        @pl.when(i == 0)
        def _():
            def rows_ka(r, carry):
                r0 = pl.multiple_of(r * tb, tb)
                for h in range(N_HEADS):
                    kp = k_ref[pl.ds(r0, tb), (h // 2) * LANES : (h // 2 + 1) * LANES] * scale
                    fk = fc_ref[pl.ds(r0, tb), h : h + 1]
                    ka_sc[h, pl.ds(r0, tb), :] = _augment(kp, h % 2, ones, _split3(-fk))
                return carry

            lax.fori_loop(0, nq, rows_ka, 0)

        q0 = pl.multiple_of(i * tb, tb)
        for h in range(N_HEADS):
            qp = q_ref[:, (h // 2) * LANES : (h // 2 + 1) * LANES]
            qa_sc[h] = _augment(qp, h % 2, _split3(fc_ref[pl.ds(q0, tb), h : h + 1]), ones)
        m_sc[...] = jnp.full(m_sc.shape, -jnp.inf, F32)
        l_sc[...] = jnp.zeros_like(l_sc)
        acc_sc[...] = jnp.zeros_like(acc_sc)
        causal = lax.broadcasted_iota(jnp.int32, (tb, tb), 1) <= lax.broadcasted_iota(jnp.int32, (tb, tb), 0)

        def step(j, masked):
            c0 = pl.multiple_of(j * tb, tb)
            for p in range(N_PAIRS):
                vb = v_ref[pl.ds(c0, tb), p * LANES : (p + 1) * LANES]
                pv, al = [], []
                for hh in range(2):
                    h = 2 * p + hh
                    s = _mm_nt(qa_sc[h], ka_sc[h, pl.ds(c0, tb), :])
                    if masked:
                        s = jnp.where(causal, s, -jnp.inf)
                    m_old = m_sc[h]
                    m_new = jnp.maximum(m_old, jnp.max(s, axis=1, keepdims=True))
                    alpha = jnp.exp(m_old - m_new)
                    pe = jnp.exp(s - jnp.concatenate([m_new] * (tb // LANES), axis=1))
                    l_sc[h] = alpha * l_sc[h] + jnp.sum(pe, axis=1, keepdims=True)
                    m_sc[h] = m_new
                    pv.append(_mm(pe.astype(BF16), vb))
                    al.append(alpha)
                acc_sc[p] = jnp.where(low, al[0], al[1]) * acc_sc[p] + jnp.where(low, pv[0], pv[1])

        def loop_body(j, carry):
            step(j, False)
            return carry

        lax.fori_loop(0, i, loop_body, 0)
        step(i, True)
        st = jnp.zeros((tb, LANES), F32)
        for p in range(N_PAIRS):
            lp = jnp.where(low, l_sc[2 * p], l_sc[2 * p + 1])
            o_ref[:, p * LANES : (p + 1) * LANES] = (acc_sc[p] / lp).astype(BF16)
            for h in (2 * p, 2 * p + 1):
                st = jnp.where(lane == h, m_sc[h] + jnp.log(l_sc[h]), st)
        st_ref[...] = st

    return pl.pallas_call(
        body,
        name="attn_fwd",
        grid=(n_seq, nq),
        in_specs=[
            pl.BlockSpec((tb, ATTN_WIDTH), lambda s, i: (s * nq + i, 0)),
            pl.BlockSpec((S, ATTN_WIDTH), lambda s, i: (s, 1)),
            pl.BlockSpec((S, ATTN_WIDTH), lambda s, i: (s, 2)),
            pl.BlockSpec((S, LANES), lambda s, i: (s, 0)),
        ],
        out_specs=[
            pl.BlockSpec((tb, ATTN_WIDTH), lambda s, i: (s * nq + i, 0)),
            pl.BlockSpec((tb, LANES), lambda s, i: (s * nq + i, 0)),
        ],
        out_shape=[jax.ShapeDtypeStruct((T, ATTN_WIDTH), BF16), jax.ShapeDtypeStruct((T, LANES), F32)],
        scratch_shapes=[
            pltpu.VMEM((N_HEADS, tb, LANES), BF16),
            pltpu.VMEM((N_HEADS, S, LANES), BF16),
            pltpu.VMEM((N_HEADS, tb, LANES), F32),
            pltpu.VMEM((N_HEADS, tb, LANES), F32),
            pltpu.VMEM((N_PAIRS, tb, LANES), F32),
        ],
        compiler_params=_params(("parallel", "arbitrary")),
    )(qkv, qkv, qkv, fcol)


def _mix_out(a, pool_y, gates, x, w_ao, w_out):
    T = x.shape[0]
    tm = ROW_TILE

    def body(a_ref, py_ref, gt_ref, x_ref, wao_ref, wout_ref, mg_ref, x1_ref, ay_ref):
        ay = _mm(a_ref[...], wao_ref[...])
        ay_ref[...] = ay
        sp = _sigmoid(gt_ref[:, :D_MODEL])
        sa = _sigmoid(gt_ref[:, D_MODEL:])
        mb = (sp * py_ref[...] + sa * ay).astype(BF16)
        mg_ref[...] = mb
        x1_ref[...] = x_ref[...] + _mm(mb, wout_ref[...])

    row = lambda n: pl.BlockSpec((tm, n), lambda i: (i, 0))
    return pl.pallas_call(
        body,
        name="mix_out",
        grid=(T // tm,),
        in_specs=[row(ATTN_WIDTH), row(D_MODEL), row(2 * D_MODEL), row(D_MODEL), _const_spec(w_ao.shape), _const_spec(w_out.shape)],
        out_specs=[row(D_MODEL), row(D_MODEL), row(D_MODEL)],
        out_shape=[jax.ShapeDtypeStruct((T, D_MODEL), BF16), jax.ShapeDtypeStruct((T, D_MODEL), F32), jax.ShapeDtypeStruct((T, D_MODEL), F32)],
        compiler_params=_params(("parallel",)),
    )(a, pool_y, gates, x, w_ao, w_out)


def _ffn_fwd(x1, g2, gf, tgt, w_gate, w_up, w_down):
    T = x1.shape[0]
    tm = ROW_TILE
    nt = T // tm
    nc = D_FF // FF_CHUNK

    def body(x1_ref, g2_ref, gf_ref, tg_ref, wg_ref, wu_ref, wd_ref, h2_ref, gate_ref, up_ref, act_ref, dx2_ref, loss_ref, dgf_ref):
        x1v = x1_ref[...]
        h2, _, _ = _rms_fwd(x1v, g2_ref[...])
        h2b = h2.astype(BF16)
        h2_ref[...] = h2b
        acc = x1v
        for c in range(nc):
            sl = slice(c * FF_CHUNK, (c + 1) * FF_CHUNK)
            gate = _mm(h2b, wg_ref[:, sl])
            up = _mm(h2b, wu_ref[:, sl])
            gate_ref[:, sl] = gate
            up_ref[:, sl] = up
            act = (gate * _sigmoid(gate) * up).astype(BF16)
            act_ref[:, sl] = act
            acc = acc + _mm(act, wd_ref[sl, :])
        gfv = gf_ref[...]
        y, xh, r = _rms_fwd(acc, gfv)
        err = y - tg_ref[...]
        part = 0.5 * jnp.sum(jnp.mean(err * err, axis=-1, keepdims=True), axis=0, keepdims=True)
        dx2, dgrow = _rms_bwd(err * (1.0 / D_MODEL), xh, r, gfv)
        dx2_ref[...] = dx2

        @pl.when(pl.program_id(0) == 0)
        def _():
            dgf_ref[...] = jnp.zeros_like(dgf_ref)
            loss_ref[...] = jnp.zeros_like(loss_ref)

        dgf_ref[...] += jnp.sum(dgrow, axis=0, keepdims=True)
        loss_ref[...] += jnp.broadcast_to(part, loss_ref.shape)

    row = lambda n: pl.BlockSpec((tm, n), lambda i: (i, 0))
    return pl.pallas_call(
        body,
        name="ffn_fwd",
        grid=(nt,),
        in_specs=[
            row(D_MODEL), _const_spec((1, D_MODEL)), _const_spec((1, D_MODEL)), row(D_MODEL),
            _const_spec(w_gate.shape), _const_spec(w_up.shape), _const_spec(w_down.shape),
        ],
        out_specs=[
            row(D_MODEL), row(D_FF), row(D_FF), row(D_FF), row(D_MODEL),
            pl.BlockSpec((8, LANES), lambda i: (0, 0)),
            pl.BlockSpec((1, D_MODEL), lambda i: (0, 0)),
        ],
        out_shape=[
            jax.ShapeDtypeStruct((T, D_MODEL), BF16),
            jax.ShapeDtypeStruct((T, D_FF), F32),
            jax.ShapeDtypeStruct((T, D_FF), F32),
            jax.ShapeDtypeStruct((T, D_FF), BF16),
            jax.ShapeDtypeStruct((T, D_MODEL), F32),
            jax.ShapeDtypeStruct((8, LANES), F32),
            jax.ShapeDtypeStruct((1, D_MODEL), F32),
        ],
        compiler_params=_params(("arbitrary",)),
    )(x1, g2, gf, tgt, w_gate, w_up, w_down)


def _ffn_bwd(dx2, gate, up, x1, g2, w_gate, w_up, w_down):
    T = x1.shape[0]
    tm = ROW_TILE
    nc = D_FF // FF_CHUNK

    def body(dx2_ref, gate_ref, up_ref, x1_ref, g2_ref, wg_ref, wu_ref, wd_ref, dgate_ref, dup_ref, dx1_ref, dg2_ref):
        dx2v = dx2_ref[...]
        dx2b = dx2v.astype(BF16)
        dh2 = jnp.zeros((tm, D_MODEL), F32)
        for c in range(nc):
            sl = slice(c * FF_CHUNK, (c + 1) * FF_CHUNK)
            dact = _mm_nt(dx2b, wd_ref[sl, :])
            gate = gate_ref[:, sl]
            sg = _sigmoid(gate)
            silu = gate * sg
            dgate = (dact * up_ref[:, sl] * (sg * (1.0 + gate * (1.0 - sg)))).astype(BF16)
            dup = (dact * silu).astype(BF16)
            dgate_ref[:, sl] = dgate
            dup_ref[:, sl] = dup
            dh2 = dh2 + _mm_nt(dgate, wg_ref[:, sl]) + _mm_nt(dup, wu_ref[:, sl])
        g2v = g2_ref[...]
        _, xh, r = _rms_fwd(x1_ref[...], g2v)
        dxn, dgrow = _rms_bwd(dh2, xh, r, g2v)
        dx1_ref[...] = dx2v + dxn

        @pl.when(pl.program_id(0) == 0)
        def _():
            dg2_ref[...] = jnp.zeros_like(dg2_ref)

        dg2_ref[...] += jnp.sum(dgrow, axis=0, keepdims=True)

    row = lambda n: pl.BlockSpec((tm, n), lambda i: (i, 0))
    return pl.pallas_call(
        body,
        name="ffn_bwd",
        grid=(T // tm,),
        in_specs=[
            row(D_MODEL), row(D_FF), row(D_FF), row(D_MODEL), _const_spec((1, D_MODEL)),
            _const_spec(w_gate.shape), _const_spec(w_up.shape), _const_spec(w_down.shape),
        ],
        out_specs=[row(D_FF), row(D_FF), row(D_MODEL), pl.BlockSpec((1, D_MODEL), lambda i: (0, 0))],
        out_shape=[
            jax.ShapeDtypeStruct((T, D_FF), BF16),
            jax.ShapeDtypeStruct((T, D_FF), BF16),
            jax.ShapeDtypeStruct((T, D_MODEL), F32),
            jax.ShapeDtypeStruct((1, D_MODEL), F32),
        ],
        compiler_params=_params(("arbitrary",)),
    )(dx2, gate, up, x1, g2, w_gate, w_up, w_down)


def _mix_bwd(dx1, gates, pool_y, attn_y, p2, scale, w_out, w_ao, w_po):
    T = dx1.shape[0]
    tm = ROW_TILE

    def body(dx1_ref, gt_ref, py_ref, ay_ref, p2_ref, sc_ref, wout_ref, wao_ref, wpo_ref, dgt_ref, dpy_ref, day_ref, da_ref, dp2_ref, dsc_ref):
        dm = _mm_nt(dx1_ref[...].astype(BF16), wout_ref[...])
        sp = _sigmoid(gt_ref[:, :D_MODEL])
        sa = _sigmoid(gt_ref[:, D_MODEL:])
        dgt_ref[:, :D_MODEL] = (dm * py_ref[...] * (sp * (1.0 - sp))).astype(BF16)
        dgt_ref[:, D_MODEL:] = (dm * ay_ref[...] * (sa * (1.0 - sa))).astype(BF16)
        dpy = (dm * sp).astype(BF16)
        day = (dm * sa).astype(BF16)
        dpy_ref[...] = dpy
        day_ref[...] = day
        da_ref[...] = _mm_nt(day, wao_ref[...]).astype(BF16)
        dp3 = _mm_nt(dpy, wpo_ref[...])
        dp2_ref[...] = (dp3 * sc_ref[...]).astype(BF16)

        @pl.when(pl.program_id(0) == 0)
        def _():
            dsc_ref[...] = jnp.zeros_like(dsc_ref)

        dsc_ref[...] += jnp.sum(dp3 * p2_ref[...], axis=0, keepdims=True)

    row = lambda n: pl.BlockSpec((tm, n), lambda i: (i, 0))
    return pl.pallas_call(
        body,
        name="mix_bwd",
        grid=(T // tm,),
        in_specs=[
            row(D_MODEL), row(2 * D_MODEL), row(D_MODEL), row(D_MODEL), row(POOL_WIDTH), _const_spec((1, POOL_WIDTH)),
            _const_spec(w_out.shape), _const_spec(w_ao.shape), _const_spec(w_po.shape),
        ],
        out_specs=[row(2 * D_MODEL), row(D_MODEL), row(D_MODEL), row(ATTN_WIDTH), row(POOL_WIDTH), pl.BlockSpec((1, POOL_WIDTH), lambda i: (0, 0))],
        out_shape=[
            jax.ShapeDtypeStruct((T, 2 * D_MODEL), BF16),
            jax.ShapeDtypeStruct((T, D_MODEL), BF16),
            jax.ShapeDtypeStruct((T, D_MODEL), BF16),
            jax.ShapeDtypeStruct((T, ATTN_WIDTH), BF16),
            jax.ShapeDtypeStruct((T, POOL_WIDTH), BF16),
            jax.ShapeDtypeStruct((1, POOL_WIDTH), F32),
        ],
        compiler_params=_params(("arbitrary",)),
    )(dx1, gates, pool_y, attn_y, p2, scale, w_out, w_ao, w_po)


def _pool_bwd(dp2, pm, mix_b, n_seq, S):
    T = n_seq * S

    def body(dp2_ref, pm_ref, mix_ref, du_ref, dmix_ref):
        g = pl.program_id(0)
        dp2v = dp2_ref[...]
        dpm = _mm_nt(dp2v, mix_ref[...])
        row = lax.broadcasted_iota(jnp.int32, dpm.shape, 0)
        w = _window_pick(g, 2.0, 4.0, 8.0, 16.0)
        e = dpm / jnp.minimum((row + 1).astype(F32), w)

        def ahead(a, k):
            return jnp.where(row < S - k, pltpu.roll(a, S - k, 0), 0.0)

        r2 = e + ahead(e, 1)
        r4 = r2 + ahead(r2, 2)
        r8 = r4 + ahead(r4, 4)
        r16 = r8 + ahead(r8, 8)
        du_ref[...] = (_window_pick(g, r2, r4, r8, r16) - dpm).astype(BF16)

        @pl.when(pl.program_id(1) == 0)
        def _():
            dmix_ref[...] = jnp.zeros_like(dmix_ref)

        dmix_ref[...] += _mm_tn(pm_ref[...], dp2v)

    grp = pl.BlockSpec((S, GROUP_DIM), lambda g, s: (s, g))
    mixs = pl.BlockSpec((None, GROUP_DIM, GROUP_DIM), lambda g, s: (g, 0, 0))
    return pl.pallas_call(
        body,
        name="pool_bwd",
        grid=(len(POOL_WINDOWS), n_seq),
        in_specs=[grp, grp, mixs],
        out_specs=[grp, mixs],
        out_shape=[jax.ShapeDtypeStruct((T, POOL_WIDTH), BF16), jax.ShapeDtypeStruct((len(POOL_WINDOWS), GROUP_DIM, GROUP_DIM), F32)],
        compiler_params=_params(("parallel", "arbitrary")),
    )(dp2, pm, mix_b)


def _attn_bwd(qkv, da, a, fcol, lse, n_seq, S):
    T = n_seq * S
    tb = ATTN_BLOCK
    nb = S // tb
    scale = HEAD_DIM ** -0.5

    def body(q_ref, k_ref, v_ref, do_ref, o_ref, fc_ref, st_ref, dq_ref, dk_ref, dv_ref, dfk_ref, dfq_ref,
             qa_sc, doa_sc, dq_acc, ka_sc, va_sc, dk_sc, dv_sc):
        j = pl.program_id(1)
        lane = lax.broadcasted_iota(jnp.int32, (1, LANES), 1)
        low = lane < HEAD_DIM
        ones = (1.0, 1.0, 1.0)
        zeros = (0.0, 0.0, 0.0)

        @pl.when(j == 0)
        def _():
            dq_acc[...] = jnp.zeros_like(dq_acc)

            def rows_q(i, carry):
                r0 = pl.multiple_of(i * tb, tb)
                for h in range(N_HEADS):
                    pair = slice((h // 2) * LANES, (h // 2 + 1) * LANES)
                    qp = q_ref[pl.ds(r0, tb), pair]
                    dop = do_ref[pl.ds(r0, tb), pair]
                    prod = dop.astype(F32) * o_ref[pl.ds(r0, tb), pair].astype(F32)
                    head = (lane >= HEAD_DIM * (h % 2)) & (lane < HEAD_DIM * (h % 2 + 1))
                    delta = jnp.sum(jnp.where(head, prod, 0.0), axis=1, keepdims=True)
                    cq = fc_ref[pl.ds(r0, tb), h : h + 1] - st_ref[pl.ds(r0, tb), h : h + 1]
                    qa_sc[h, pl.ds(r0, tb), :] = _augment(qp, h % 2, _split3(cq), ones)
                    doa_sc[h, pl.ds(r0, tb), :] = _augment(dop, h % 2, _split3(-delta), zeros)
                return carry

            lax.fori_loop(0, nb, rows_q, 0)

        c0 = pl.multiple_of(j * tb, tb)
        for h in range(N_HEADS):
            pair = slice((h // 2) * LANES, (h // 2 + 1) * LANES)
            kp = k_ref[:, pair] * scale
            ka_sc[h] = _augment(kp, h % 2, ones, _split3(-fc_ref[pl.ds(c0, tb), h : h + 1]))
            va_sc[h] = _augment(v_ref[:, pair], h % 2, ones, zeros)
        dk_sc[...] = jnp.zeros_like(dk_sc)
        dv_sc[...] = jnp.zeros_like(dv_sc)
        causal = lax.broadcasted_iota(jnp.int32, (tb, tb), 1) <= lax.broadcasted_iota(jnp.int32, (tb, tb), 0)

        def step(i, masked):
            r0 = pl.multiple_of(i * tb, tb)
            for h in range(N_HEADS):
                dob = do_ref[pl.ds(r0, tb), (h // 2) * LANES : (h // 2 + 1) * LANES]
                qa = qa_sc[h, pl.ds(r0, tb), :]
                s = _mm_nt(qa, ka_sc[h])
                if masked:
                    s = jnp.where(causal, s, -jnp.inf)
                pr = jnp.exp(s)
                dv_sc[h] += _mm_tn(pr.astype(BF16), dob)
                dsb = (pr * _mm_nt(doa_sc[h, pl.ds(r0, tb), :], va_sc[h])).astype(BF16)
                dk_sc[h] += _mm_tn(dsb, qa)
                dq_acc[h, pl.ds(r0, tb), :] += _mm(dsb, ka_sc[h])

        step(j, True)

        def loop_body(i, carry):
            step(i, False)
            return carry

        lax.fori_loop(j + 1, nb, loop_body, 0)
        dfk = jnp.zeros((tb, LANES), F32)
        for p in range(N_PAIRS):
            dk_ref[:, p * LANES : (p + 1) * LANES] = (jnp.where(low, dk_sc[2 * p], dk_sc[2 * p + 1]) * scale).astype(BF16)
            dv_ref[:, p * LANES : (p + 1) * LANES] = jnp.where(low, dv_sc[2 * p], dv_sc[2 * p + 1]).astype(BF16)
            for hh in range(2):
                b = HEAD_DIM * (1 - hh) + 3
                dfk = jnp.where(lane == 2 * p + hh, -dk_sc[2 * p + hh][:, b : b + 1], dfk)
        dfk_ref[...] = dfk

        @pl.when(j == nb - 1)
        def _():
            def rows_dq(i, carry):
                r0 = pl.multiple_of(i * tb, tb)
                dfq = jnp.zeros((tb, LANES), F32)
                for p in range(N_PAIRS):
                    parts = [dq_acc[2 * p + hh, pl.ds(r0, tb), :] for hh in range(2)]
                    dq_ref[pl.ds(r0, tb), p * LANES : (p + 1) * LANES] = jnp.where(low, parts[0], parts[1]).astype(BF16)
                    for hh in range(2):
                        b = HEAD_DIM * (1 - hh)
                        dfq = jnp.where(lane == 2 * p + hh, parts[hh][:, b : b + 1], dfq)
                dfq_ref[pl.ds(r0, tb), :] = dfq
                return carry

            lax.fori_loop(0, nb, rows_dq, 0)

    seq = lambda w, col: pl.BlockSpec((S, w), lambda s, j: (s, col))
    blk = lambda w, col: pl.BlockSpec((tb, w), lambda s, j: (s * nb + j, col))
    return pl.pallas_call(
        body,
        name="attn_bwd",
        grid=(n_seq, nb),
        in_specs=[seq(ATTN_WIDTH, 0), blk(ATTN_WIDTH, 1), blk(ATTN_WIDTH, 2), seq(ATTN_WIDTH, 0), seq(ATTN_WIDTH, 0), seq(LANES, 0), seq(LANES, 0)],
        out_specs=[seq(ATTN_WIDTH, 0), blk(ATTN_WIDTH, 0), blk(ATTN_WIDTH, 0), blk(LANES, 0), seq(LANES, 0)],
        out_shape=[
            jax.ShapeDtypeStruct((T, ATTN_WIDTH), BF16),
            jax.ShapeDtypeStruct((T, ATTN_WIDTH), BF16),
            jax.ShapeDtypeStruct((T, ATTN_WIDTH), BF16),
            jax.ShapeDtypeStruct((T, LANES), F32),
            jax.ShapeDtypeStruct((T, LANES), F32),
        ],
        scratch_shapes=[
            pltpu.VMEM((N_HEADS, S, LANES), BF16),
            pltpu.VMEM((N_HEADS, S, LANES), BF16),
            pltpu.VMEM((N_HEADS, S, LANES), F32),
            pltpu.VMEM((N_HEADS, tb, LANES), BF16),
            pltpu.VMEM((N_HEADS, tb, LANES), BF16),
            pltpu.VMEM((N_HEADS, tb, LANES), F32),
            pltpu.VMEM((N_HEADS, tb, LANES), F32),
        ],
        compiler_params=_params(("parallel", "arbitrary")),
    )(qkv, qkv, qkv, da, a, fcol, lse)


def _forget_bwd(dfk, dfq, fl, b_pad, n_seq, S):
    def body(df_ref, dfq_ref, fl_ref, b_ref, dfl_ref, db_ref):
        t = (df_ref[...] + dfq_ref[...]).T
        lane = lax.broadcasted_iota(jnp.int32, t.shape, 1)
        k = 1
        while k < S:
            t = t + jnp.where(lane < S - k, pltpu.roll(t, S - k, 1), 0.0)
            k *= 2
        dfl = t.T * _sigmoid(-(fl_ref[...] + b_ref[...]))
        dfl_ref[...] = dfl.astype(BF16)

        @pl.when(pl.program_id(0) == 0)
        def _():
            db_ref[...] = jnp.zeros_like(db_ref)

        db_ref[...] += jnp.sum(dfl, axis=0, keepdims=True)

    return pl.pallas_call(
        body,
        name="forget_bwd",
        grid=(n_seq,),
        in_specs=[
            pl.BlockSpec((S, LANES), lambda s: (s, 0)),
            pl.BlockSpec((S, LANES), lambda s: (s, 0)),
            pl.BlockSpec((S, FL_PAD), lambda s: (s, 0)),
            _const_spec((1, FL_PAD)),
        ],
        out_specs=[pl.BlockSpec((S, FL_PAD), lambda s: (s, 0)), pl.BlockSpec((1, FL_PAD), lambda s: (0, 0))],
        out_shape=[jax.ShapeDtypeStruct((n_seq * S, FL_PAD), BF16), jax.ShapeDtypeStruct((1, FL_PAD), F32)],
        compiler_params=_params(("arbitrary",)),
    )(dfk, dfq, fl, b_pad)


def _in_proj_bwd(du, dq, dk, dv, dfl, dgates, x, dx1, g1, w_uqkv, w_fl, w_g):
    T = x.shape[0]
    tm = ROW_TILE

    def body(du_ref, dq_ref, dk_ref, dv_ref, dfl_ref, dgt_ref, x_ref, dx1_ref, g_ref, wa_ref, wf_ref, wg_ref, dx_ref, dg_ref):
        dh = _mm_nt(dgt_ref[...], wg_ref[...]) + _mm_nt(dfl_ref[...], wf_ref[...])
        for n, ref in enumerate((du_ref, dq_ref, dk_ref, dv_ref)):
            dh = dh + _mm_nt(ref[...], wa_ref[:, n * 512 : (n + 1) * 512])
        gv = g_ref[...]
        _, xh, r = _rms_fwd(x_ref[...], gv)
        dxn, dgrow = _rms_bwd(dh, xh, r, gv)
        dx_ref[...] = dx1_ref[...] + dxn

        @pl.when(pl.program_id(0) == 0)
        def _():
            dg_ref[...] = jnp.zeros_like(dg_ref)

        dg_ref[...] += jnp.sum(dgrow, axis=0, keepdims=True)

    row = lambda n: pl.BlockSpec((tm, n), lambda i: (i, 0))
    return pl.pallas_call(
        body,
        name="in_proj_bwd",
        grid=(T // tm,),
        in_specs=[
            row(512), row(512), row(512), row(512), row(FL_PAD), row(2 * D_MODEL), row(D_MODEL), row(D_MODEL), _const_spec((1, D_MODEL)),
            _const_spec(w_uqkv.shape), _const_spec(w_fl.shape), _const_spec(w_g.shape),
        ],
        out_specs=[row(D_MODEL), pl.BlockSpec((1, D_MODEL), lambda i: (0, 0))],
        out_shape=[jax.ShapeDtypeStruct((T, D_MODEL), F32), jax.ShapeDtypeStruct((1, D_MODEL), F32)],
        compiler_params=_params(("arbitrary",)),
    )(du, dq, dk, dv, dfl, dgates, x, dx1, g1, w_uqkv, w_fl, w_g)


def _pick_block(n):
    for b in (512, 1408, 256, 128):
        if n % b == 0:
            return b
    raise ValueError(n)


def _matmul_tn(a, b, name, row_sharded=False):
    T, K = a.shape
    N = b.shape[1]
    bt, bk, bn = 512, _pick_block(K), _pick_block(N)
    nt = T // bt
    r = K // N_DEV
    assert not row_sharded or bk == 4 * r

    def body(a_ref, b_ref, o_ref, acc):
        @pl.when(pl.program_id(2) == 0)
        def _():
            acc[...] = jnp.zeros_like(acc)

        acc[...] += _mm_tn(a_ref[...].astype(BF16), b_ref[...].astype(BF16))

        @pl.when(pl.program_id(2) == nt - 1)
        def _():
            if row_sharded:
                for chip in range(2):
                    for core in range(2):
                        d = 2 * chip + core
                        o_ref[core, chip] = acc[d * r : (d + 1) * r, :].astype(BF16)
            else:
                o_ref[...] = acc[...].astype(BF16)

    if row_sharded:
        out_spec = pl.BlockSpec((2, 2, r, bn), lambda k, n, t: (0, k, 0, n))
        out_shape = jax.ShapeDtypeStruct((2, 4, r, N), BF16)
    else:
        out_spec = pl.BlockSpec((bk, bn), lambda k, n, t: (k, n))
        out_shape = jax.ShapeDtypeStruct((K, N), BF16)
    return pl.pallas_call(
        body,
        name=name,
        grid=(K // bk, N // bn, nt),
        in_specs=[pl.BlockSpec((bt, bk), lambda k, n, t: (t, k)), pl.BlockSpec((bt, bn), lambda k, n, t: (t, n))],
        out_specs=out_spec,
        out_shape=out_shape,
        scratch_shapes=[pltpu.VMEM((bk, bn), F32)],
        compiler_params=_params(("parallel", "parallel", "arbitrary")),
    )(a, b)


def _position():
    return lax.axis_index("x"), lax.axis_index("y"), lax.axis_index("c")


_HBM = pl.BlockSpec(memory_space=pl.ANY)


def _all_gather(blocks, name):
    n = len(blocks)

    def body(*refs):
        xs, outs = refs[:n], refs[n : 2 * n]
        send_sems, recv_sems, local_sems = refs[2 * n :]
        x, y, c = _position()
        me, sibling = (x, y, c), (x, y, 1 - c)
        chips = [(1 - x, y), (x, 1 - y), (1 - x, 1 - y)]

        def rows(a, px, py, pc):
            return outs[a].at[4 * px + 2 * py + pc]

        def copy(a, k, blk, to, src=None):
            return pltpu.make_async_remote_copy(
                src_ref=rows(a, *blk) if src is None else src, dst_ref=rows(a, *blk),
                send_sem=send_sems.at[7 * a + k], recv_sem=recv_sems.at[7 * a + k], device_id=to, device_id_type=MESH,
            )

        mine = [pltpu.make_async_copy(xs[a], rows(a, *me), local_sems.at[a]) for a in range(n)]
        for cp in mine:
            cp.start()
        first = []
        for a in range(n):
            first.append(copy(a, 0, me, sibling, src=xs[a]))
            first += [copy(a, 1 + j, me, (*chip, c), src=xs[a]) for j, chip in enumerate(chips)]
        for cp in first:
            cp.start()
        passed = []
        for j, chip in enumerate(chips):
            for a in range(n):
                copy(a, 1 + j, (*chip, c), me).wait_recv()
                passed.append(copy(a, 4 + j, (*chip, c), sibling))
                passed[-1].start()
        for a in range(n):
            copy(a, 0, sibling, me).wait_recv()
        for j, chip in enumerate(chips):
            for a in range(n):
                copy(a, 4 + j, (*chip, 1 - c), me).wait_recv()
        for cp in first + passed:
            cp.wait_send()
        for cp in mine:
            cp.wait()

    return pl.pallas_call(
        body,
        name=name,
        out_shape=[jax.ShapeDtypeStruct((N_DEV, *b.shape), b.dtype) for b in blocks],
        in_specs=[_HBM] * n,
        out_specs=[_HBM] * n,
        scratch_shapes=[pltpu.SemaphoreType.DMA((7 * n,)), pltpu.SemaphoreType.DMA((7 * n,)), pltpu.SemaphoreType.DMA((n,))],
    )(*blocks)


def _sibling_exchange(sends):
    n = len(sends)

    def body(*refs):
        srcs, dsts = refs[:n], refs[n : 2 * n]
        send_sems, recv_sems = refs[2 * n :]
        x, y, c = _position()
        cps = [
            pltpu.make_async_remote_copy(
                src_ref=srcs[a].at[1 - c], dst_ref=dsts[a], send_sem=send_sems.at[a], recv_sem=recv_sems.at[a],
                device_id=(x, y, 1 - c), device_id_type=MESH,
            )
            for a in range(n)
        ]
        for cp in cps:
            cp.start()
        for cp in cps:
            cp.wait()

    return pl.pallas_call(
        body,
        name="rs_sibling",
        out_shape=[jax.ShapeDtypeStruct(s.shape[1:], s.dtype) for s in sends],
        in_specs=[_HBM] * n,
        out_specs=[_HBM] * n,
        scratch_shapes=[pltpu.SemaphoreType.DMA((n,)), pltpu.SemaphoreType.DMA((n,))],
    )(*sends)


def _rows_tile(r):
    return ROW_TILE if r % ROW_TILE == 0 else r


def _pair_sum(send, got, core, name):
    _, _, r, c = send.shape
    br = _rows_tile(r)

    def body(core_ref, a_ref, b_ref, o_ref):
        o_ref[...] = (a_ref[...].astype(F32) + b_ref[...].astype(F32)).astype(o_ref.dtype)

    return pl.pallas_call(
        body,
        name=name,
        grid_spec=pltpu.PrefetchScalarGridSpec(
            num_scalar_prefetch=1,
            grid=(4, r // br),
            in_specs=[
                pl.BlockSpec((None, None, br, c), lambda n, i, core: (core[0], n, i, 0)),
                pl.BlockSpec((None, br, c), lambda n, i, core: (n, i, 0)),
            ],
            out_specs=pl.BlockSpec((None, br, c), lambda n, i, core: (n, i, 0)),
        ),
        out_shape=jax.ShapeDtypeStruct((4, r, c), send.dtype),
        compiler_params=_params(("parallel", "parallel")),
    )(core, send, got)


def _chip_exchange(pairs):
    n = len(pairs)

    def body(*refs):
        srcs, dsts = refs[:n], refs[n : 2 * n]
        send_sems, recv_sems = refs[2 * n :]
        x, y, c = _position()
        chips = [(1 - x, y), (x, 1 - y), (1 - x, 1 - y)]
        cps = [
            pltpu.make_async_remote_copy(
                src_ref=srcs[a].at[2 * cx + cy], dst_ref=dsts[a].at[j], send_sem=send_sems.at[3 * a + j], recv_sem=recv_sems.at[3 * a + j],
                device_id=(cx, cy, c), device_id_type=MESH,
            )
            for a in range(n)
            for j, (cx, cy) in enumerate(chips)
        ]
        for cp in cps:
            cp.start()
        for cp in cps:
            cp.wait()

    return pl.pallas_call(
        body,
        name="rs_chips",
        out_shape=[jax.ShapeDtypeStruct((3, *p.shape[1:]), p.dtype) for p in pairs],
        in_specs=[_HBM] * n,
        out_specs=[_HBM] * n,
        scratch_shapes=[pltpu.SemaphoreType.DMA((3 * n,)), pltpu.SemaphoreType.DMA((3 * n,))],
    )(*pairs)


def _adamw(w, g, m, v):
    m = ADAM_B1 * m + (1.0 - ADAM_B1) * g
    v = ADAM_B2 * v + (1.0 - ADAM_B2) * (g * g)
    m_hat = m / (1.0 - ADAM_B1 ** ADAM_STEP)
    v_hat = v / (1.0 - ADAM_B2 ** ADAM_STEP)
    delta = -ADAM_LR * (m_hat / (jnp.sqrt(v_hat) + ADAM_EPS) + ADAM_WD * w)
    return delta, m, v


def _shard_update(send, got, recv, w, m, v, pos, name):
    _, r, c = w.shape
    br = _rows_tile(r)

    def body(pos_ref, a_ref, b_ref, r_ref, w_ref, m_ref, v_ref, g_ref, d_ref, nm_ref, nv_ref):
        g = a_ref[...].astype(F32) + b_ref[...].astype(F32)
        for n in range(3):
            g = g + r_ref[n].astype(F32)
        g_ref[...] = g
        d_ref[...], nm_ref[...], nv_ref[...] = _adamw(w_ref[...], g, m_ref[...], v_ref[...])

    own = pl.BlockSpec((None, br, c), lambda i, pos: (0, i, 0))
    return pl.pallas_call(
        body,
        name=name,
        grid_spec=pltpu.PrefetchScalarGridSpec(
            num_scalar_prefetch=1,
            grid=(r // br,),
            in_specs=[
                pl.BlockSpec((None, None, br, c), lambda i, pos: (pos[0], pos[1], i, 0)),
                pl.BlockSpec((None, br, c), lambda i, pos: (pos[1], i, 0)),
                pl.BlockSpec((3, br, c), lambda i, pos: (0, i, 0)),
                own, own, own,
            ],
            out_specs=[own, own, own, own],
        ),
        out_shape=[jax.ShapeDtypeStruct((1, r, c), F32)] * 4,
        compiler_params=_params(("parallel",)),
    )(pos, send, got, recv, w, m, v)


def _small_update(parts, w, m, v):
    R = w.shape[0]

    def body(p_ref, w_ref, m_ref, v_ref, g_ref, d_ref, nm_ref, nv_ref):
        g = p_ref[0]
        for n in range(1, N_DEV):
            g = g + p_ref[n]
        g_ref[...] = g
        d_ref[...], nm_ref[...], nv_ref[...] = _adamw(w_ref[...], g, m_ref[...], v_ref[...])

    return pl.pallas_call(
        body,
        name="small_update",
        out_shape=[jax.ShapeDtypeStruct((R, LANES), F32)] * 4,
        compiler_params=pltpu.CompilerParams(vmem_limit_bytes=VMEM_LIMIT),
    )(parts, w, m, v)


_SHARD_AXIS = (1, 1, 1, 0, 1, 1, 0)


def _full_from_gathered(t, axis):
    if axis == 0:
        return t.reshape(N_DEV * t.shape[1], t.shape[2])
    return jnp.transpose(t, (1, 0, 2)).reshape(t.shape[1], N_DEV * t.shape[2])


def _send_from_cols(t):
    k, n = t.shape
    return jnp.transpose(t.reshape(k, 4, 2, n // N_DEV), (2, 1, 0, 3))


_SMALL = (("norm1_g", 8), ("norm2_g", 8), ("norm_f_g", 8), ("b_forget", 8), ("pool_scale", 8), ("pool_mix", 512))
_SMALL_ROWS = sum(r for _, r in _SMALL) + 8


def _pack_small(vals, loss_row):
    parts = []
    for (name, rows), t in zip(_SMALL, vals):
        f = t.astype(F32).reshape(-1)
        f = jnp.concatenate([f, jnp.zeros((rows * LANES - f.shape[0],), F32)]).reshape(rows, LANES)
        parts.append(f)
    parts.append(loss_row)
    return jnp.concatenate(parts, axis=0)


def _unpack_small(packed, shapes):
    out, off = [], 0
    for (name, rows), shape in zip(_SMALL, shapes):
        n = 1
        for s in shape:
            n *= s
        out.append(packed[off : off + rows].reshape(-1)[:n].reshape(shape))
        off += rows
    return out, packed[off, 0]


def _local_grads(x, tgt, g1, g2, gf, b_forget, pool_mix, pool_scale, w_in, w_po, w_ao, w_out, w_gate, w_up, w_down):
    n_seq, S, _ = x.shape
    T = n_seq * S
    x2 = x.reshape(T, D_MODEL)
    tg2 = tgt.reshape(T, D_MODEL)
    w_uqkv = w_in[:, : POOL_WIDTH + 3 * ATTN_WIDTH]
    w_fl = jnp.concatenate([w_in[:, 2048 : 2048 + N_HEADS], jnp.zeros((D_MODEL, FL_PAD - N_HEADS), BF16)], axis=1)
    w_g = w_in[:, 2048 + N_HEADS :]
    b_pad = jnp.concatenate([b_forget.reshape(1, N_HEADS), jnp.zeros((1, FL_PAD - N_HEADS), F32)], axis=1)
    mix_b = pool_mix.reshape(len(POOL_WINDOWS), GROUP_DIM, GROUP_DIM).astype(BF16)
    scale = pool_scale.reshape(1, POOL_WIDTH)
    g1 = g1.reshape(1, D_MODEL)
    g2 = g2.reshape(1, D_MODEL)
    gf = gf.reshape(1, D_MODEL)

    h, u, qkv, fl, gates = _in_proj(x2, g1, w_uqkv, w_fl, w_g)
    fcol = _forget_fwd(fl, b_pad, n_seq, S)
    pm, p2, p3, pool_y = _pool_fwd(u, mix_b, scale, w_po, n_seq, S)
    a, lse = _attn_fwd(qkv, fcol, n_seq, S)
    merged, x1, attn_y = _mix_out(a, pool_y, gates, x2, w_ao, w_out)
    h2, gate, up, act, dx2, loss_rows, dgf = _ffn_fwd(x1, g2, gf, tg2, w_gate, w_up, w_down)

    dgate, dup, dx1, dg2 = _ffn_bwd(dx2, gate, up, x1, g2, w_gate, w_up, w_down)
    dgates, dpy, day, da, dp2, dscale = _mix_bwd(dx1, gates, pool_y, attn_y, p2, scale, w_out, w_ao, w_po)
    du, dmix = _pool_bwd(dp2, pm, mix_b, n_seq, S)
    dq, dk, dv, dfk, dfq = _attn_bwd(qkv, da, a, fcol, lse, n_seq, S)
    dfl, db = _forget_bwd(dfk, dfq, fl, b_pad, n_seq, S)
    dx, dg1 = _in_proj_bwd(du, dq, dk, dv, dfl, dgates, x2, dx1, g1, w_uqkv, w_fl, w_g)

    d_w_in = jnp.concatenate(
        [
            _matmul_tn(h, du, "dw_u"), _matmul_tn(h, dq, "dw_q"), _matmul_tn(h, dk, "dw_k"), _matmul_tn(h, dv, "dw_v"),
            _matmul_tn(h, dfl, "dw_fl")[:, :N_HEADS], _matmul_tn(h, dgates, "dw_gates"),
        ],
        axis=1,
    )
    sends = [
        _send_from_cols(d_w_in),
        _send_from_cols(_matmul_tn(p3, dpy, "dw_pool_out")),
        _send_from_cols(_matmul_tn(a, day, "dw_attn_out")),
        _matmul_tn(merged, dx1, "dw_out", row_sharded=True),
        _send_from_cols(_matmul_tn(h2, dgate, "dw_ffn_gate")),
        _send_from_cols(_matmul_tn(h2, dup, "dw_ffn_up")),
        _matmul_tn(act, dx2, "dw_ffn_down", row_sharded=True),
    ]
    small = (dg1, dg2, dgf, db[:, :N_HEADS], dscale, dmix)
    return loss_rows, dx.reshape(n_seq, S, D_MODEL), sends, small


def kernel(x, norm1_g, w_in, b_forget, pool_mix, pool_scale, w_pool_out, w_attn_out, w_out, norm2_g, w_ffn_gate, w_ffn_up, w_ffn_down, norm_f_g, loss_target, m_norm1_g, m_w_in, m_b_forget, m_pool_mix, m_pool_scale, m_w_pool_out, m_w_attn_out, m_w_out, m_norm2_g, m_w_ffn_gate, m_w_ffn_up, m_w_ffn_down, m_norm_f_g, v_norm1_g, v_w_in, v_b_forget, v_pool_mix, v_pool_scale, v_w_pool_out, v_w_attn_out, v_w_out, v_norm2_g, v_w_ffn_gate, v_w_ffn_up, v_w_ffn_down, v_norm_f_g):
    names = ("w_in", "w_pool_out", "w_attn_out", "w_out", "w_ffn_gate", "w_ffn_up", "w_ffn_down")
    w_sh = (w_in, w_pool_out, w_attn_out, w_out, w_ffn_gate, w_ffn_up, w_ffn_down)
    m_sh = (m_w_in, m_w_pool_out, m_w_attn_out, m_w_out, m_w_ffn_gate, m_w_ffn_up, m_w_ffn_down)
    v_sh = (v_w_in, v_w_pool_out, v_w_attn_out, v_w_out, v_w_ffn_gate, v_w_ffn_up, v_w_ffn_down)

    gathered = _all_gather([w[0].astype(BF16) for w in w_sh], "weights_all_gather")
    whole = [_full_from_gathered(t, axis) for t, axis in zip(gathered, _SHARD_AXIS)]

    loss_rows, grad_x, sends, small = _local_grads(x, loss_target, norm1_g, norm2_g, norm_f_g, b_forget, pool_mix, pool_scale, *whole)

    cx, cy, cc = _position()
    core = jnp.reshape(cc, (1,)).astype(jnp.int32)
    pos = jnp.stack([cc, 2 * cx + cy]).astype(jnp.int32)
    gots = _sibling_exchange(sends)
    pairs = [_pair_sum(s, g, core, "pair_sum_" + n) for s, g, n in zip(sends, gots, names)]
    recvs = _chip_exchange(pairs)
    updates = [
        _shard_update(s, g, r, w, m, v, pos, "update_" + n)
        for s, g, r, w, m, v, n in zip(sends, gots, recvs, w_sh, m_sh, v_sh, names)
    ]
    g_w, d_w, nm_w, nv_w = zip(*updates)

    small_w = (norm1_g, norm2_g, norm_f_g, b_forget, pool_scale, pool_mix)
    small_m = (m_norm1_g, m_norm2_g, m_norm_f_g, m_b_forget, m_pool_scale, m_pool_mix)
    small_v = (v_norm1_g, v_norm2_g, v_norm_f_g, v_b_forget, v_pool_scale, v_pool_mix)
    zero_row = jnp.zeros((8, LANES), F32)
    (parts,) = _all_gather([_pack_small(small, loss_rows)], "small_all_gather")
    g_s, d_s, nm_s, nv_s = _small_update(parts, _pack_small(small_w, zero_row), _pack_small(small_m, zero_row), _pack_small(small_v, zero_row))
    shapes = [t.shape for t in small_w]
    (g1, g2, gf, gb, gsc, gmix), loss = _unpack_small(g_s, shapes)
    (d1, d2, df, db_, dsc, dmx), _ = _unpack_small(d_s, shapes)
    (m1, m2, mf, mb, msc, mmx), _ = _unpack_small(nm_s, shapes)
    (v1, v2, vf, vb, vsc, vmx), _ = _unpack_small(nv_s, shapes)

    def ordered(n1, win, b, mix, sc, wpo, wao, wout, n2, wg, wu, wd, nf):
        return (n1, win, b, mix, sc, wpo, wao, wout, n2, wg, wu, wd, nf)

    grads = ordered(g1, g_w[0], gb, gmix, gsc, g_w[1], g_w[2], g_w[3], g2, g_w[4], g_w[5], g_w[6], gf)
    deltas = ordered(d1, d_w[0], db_, dmx, dsc, d_w[1], d_w[2], d_w[3], d2, d_w[4], d_w[5], d_w[6], df)
    new_m = ordered(m1, nm_w[0], mb, mmx, msc, nm_w[1], nm_w[2], nm_w[3], m2, nm_w[4], nm_w[5], nm_w[6], mf)
    new_v = ordered(v1, nv_w[0], vb, vmx, vsc, nv_w[1], nv_w[2], nv_w[3], v2, nv_w[4], nv_w[5], nv_w[6], vf)
    return (loss, grad_x, *grads, *deltas, *new_m, *new_v)
```

```python
import functools

import jax
import jax.numpy as jnp
from jax import lax
from jax.experimental import pallas as pl
from jax.experimental.pallas import tpu as pltpu

F32 = jnp.float32
BF16 = jnp.bfloat16
MESH = pl.DeviceIdType.MESH

D_MODEL = 1024
POOL_WINDOWS = (2, 4, 8, 16)
POOL_WIDTH = 512
GROUP_DIM = 128
ATTN_WIDTH = 512
HEAD_DIM = 64
N_HEADS = 8
N_PAIRS = 4
D_FF = 2816
RMS_EPS = 1e-6
N_DEV = 8
LANES = 128
FL_PAD = 128

ADAM_LR = 0.001
ADAM_B1 = 0.9
ADAM_B2 = 0.999
ADAM_EPS = 1e-08
ADAM_WD = 0.01
ADAM_STEP = 10

VMEM_LIMIT = 56 * 1024 * 1024
VMEM_LIMIT_MAX = 60 * 1024 * 1024
ROW_TILE = 256
ATTN_BLOCK = 256
FF_CHUNK = 256
FF_ROW_TILE = 512
DW_TOKENS = 2048


def _mm(a, b):
    return jnp.dot(a, b, preferred_element_type=F32)


def _mm_nt(a, b):
    return lax.dot_general(a, b, (((1,), (1,)), ((), ())), preferred_element_type=F32)


def _mm_tn(a, b):
    return lax.dot_general(a, b, (((0,), (0,)), ((), ())), preferred_element_type=F32)


def _sigmoid(x):
    return 1.0 / (1.0 + jnp.exp(-x))


def _params(sem, vmem=VMEM_LIMIT):
    return pltpu.CompilerParams(dimension_semantics=sem, vmem_limit_bytes=vmem)


def _const_spec(shape):
    nd = len(shape)
    return pl.BlockSpec(shape, lambda *_: (0,) * nd, pipeline_mode=pl.Buffered(1))


def _rms_fwd(x, g):
    r = lax.rsqrt(jnp.mean(x * x, axis=-1, keepdims=True) + RMS_EPS)
    xh = x * r
    return xh * g, xh, r


def _rms_bwd(dy, xh, r, g):
    dxh = dy * g
    dx = r * (dxh - xh * jnp.mean(dxh * xh, axis=-1, keepdims=True))
    return dx, dy * xh


def _in_proj(x, g1, w_uqkv, w_fl, w_g):
    T = x.shape[0]
    tm = ROW_TILE

    def body(x_ref, g_ref, wa_ref, wf_ref, wg_ref, h_ref, u_ref, qkv_ref, fl_ref, gt_ref):
        h, _, _ = _rms_fwd(x_ref[...], g_ref[...])
        hb = h.astype(BF16)
        h_ref[...] = hb
        z = _mm(hb, wa_ref[...])
        u_ref[...] = z[:, :POOL_WIDTH]
        qkv_ref[...] = z[:, POOL_WIDTH:].astype(BF16)
        fl_ref[...] = _mm(hb, wf_ref[...])
        gt_ref[...] = _mm(hb, wg_ref[...])

    row = lambda n: pl.BlockSpec((tm, n), lambda i: (i, 0))
    return pl.pallas_call(
        body,
        name="in_proj",
        grid=(T // tm,),
        in_specs=[row(D_MODEL), _const_spec((1, D_MODEL)), _const_spec(w_uqkv.shape), _const_spec(w_fl.shape), _const_spec(w_g.shape)],
        out_specs=[row(D_MODEL), row(POOL_WIDTH), row(3 * ATTN_WIDTH), row(FL_PAD), row(2 * D_MODEL)],
        out_shape=[
            jax.ShapeDtypeStruct((T, D_MODEL), BF16),
            jax.ShapeDtypeStruct((T, POOL_WIDTH), F32),
            jax.ShapeDtypeStruct((T, 3 * ATTN_WIDTH), BF16),
            jax.ShapeDtypeStruct((T, FL_PAD), F32),
            jax.ShapeDtypeStruct((T, 2 * D_MODEL), F32),
        ],
        compiler_params=_params(("parallel",)),
    )(x, g1, w_uqkv, w_fl, w_g)


def _log_sigmoid(x):
    return jnp.minimum(x, 0.0) - jnp.log(1.0 + jnp.exp(-jnp.abs(x)))


def _forget_fwd(fl, b_pad, n_seq, S):
    def body(fl_ref, b_ref, fcol_ref):
        lf = _log_sigmoid(fl_ref[...] + b_ref[...])
        t = lf.T
        lane = lax.broadcasted_iota(jnp.int32, t.shape, 1)
        k = 1
        while k < S:
            t = t + jnp.where(lane >= k, pltpu.roll(t, k, 1), 0.0)
            k *= 2
        fcol_ref[...] = t.T

    return pl.pallas_call(
        body,
        name="forget_fwd",
        grid=(n_seq,),
        in_specs=[pl.BlockSpec((S, FL_PAD), lambda s: (s, 0)), _const_spec((1, FL_PAD))],
        out_specs=pl.BlockSpec((S, FL_PAD), lambda s: (s, 0)),
        out_shape=jax.ShapeDtypeStruct((n_seq * S, FL_PAD), F32),
        compiler_params=_params(("parallel",)),
    )(fl, b_pad)


def _window_pick(g, v2, v4, v8, v16):
    return jnp.where(g == 0, v2, jnp.where(g == 1, v4, jnp.where(g == 2, v8, v16)))


def _pool_fwd(u, mix_b, scale, w_po, n_seq, S):
    T = n_seq * S

    def body(u_ref, mix_ref, sc_ref, wpo_ref, pm_ref, p2_ref, p3_ref, py_ref):
        g = pl.program_id(1)
        uu = u_ref[...]
        row = lax.broadcasted_iota(jnp.int32, uu.shape, 0)

        def back(a, k):
            return jnp.where(row >= k, pltpu.roll(a, k, 0), 0.0)

        s2 = uu + back(uu, 1)
        s4 = s2 + back(s2, 2)
        s8 = s4 + back(s4, 4)
        s16 = s8 + back(s8, 8)
        w = _window_pick(g, 2.0, 4.0, 8.0, 16.0)
        cnt = jnp.minimum((row + 1).astype(F32), w)
        pm = _window_pick(g, s2, s4, s8, s16) / cnt - uu
        pmb = pm.astype(BF16)
        pm_ref[...] = pmb
        p2 = _mm(pmb, mix_ref[...])
        p2_ref[...] = p2
        p3 = (p2 * sc_ref[...]).astype(BF16)
        p3_ref[...] = p3

        @pl.when(g == 0)
        def _():
            py_ref[...] = jnp.zeros_like(py_ref)

        py_ref[...] += _mm(p3, wpo_ref[...])

    grp = pl.BlockSpec((S, GROUP_DIM), lambda s, g: (s, g))
    return pl.pallas_call(
        body,
        name="pool_fwd",
        grid=(n_seq, len(POOL_WINDOWS)),
        in_specs=[
            grp,
            pl.BlockSpec((None, GROUP_DIM, GROUP_DIM), lambda s, g: (g, 0, 0)),
            pl.BlockSpec((1, GROUP_DIM), lambda s, g: (0, g)),
            pl.BlockSpec((GROUP_DIM, D_MODEL), lambda s, g: (g, 0)),
        ],
        out_specs=[grp, grp, grp, pl.BlockSpec((S, D_MODEL), lambda s, g: (s, 0))],
        out_shape=[
            jax.ShapeDtypeStruct((T, POOL_WIDTH), BF16),
            jax.ShapeDtypeStruct((T, POOL_WIDTH), F32),
            jax.ShapeDtypeStruct((T, POOL_WIDTH), BF16),
            jax.ShapeDtypeStruct((T, D_MODEL), F32),
        ],
        compiler_params=_params(("parallel", "arbitrary")),
    )(u, mix_b, scale, w_po)


def _split3(v):
    hi = v.astype(BF16).astype(F32)
    r = v - hi
    mid = r.astype(BF16).astype(F32)
    lo = (r - mid).astype(BF16).astype(F32)
    return hi, mid, lo


def _augment(xp, hh, first, second):
    lane = lax.broadcasted_iota(jnp.int32, (1, LANES), 1)
    head = (lane >= HEAD_DIM * hh) & (lane < HEAD_DIM * (hh + 1))
    b = HEAD_DIM * (1 - hh)
    out = jnp.where(head, xp.astype(F32), 0.0)
    for n, col in enumerate(tuple(first) + tuple(second)):
        out = jnp.where(lane == b + n, col, out)
    return out.astype(BF16)


def _attn_fwd(qkv, fcol, n_seq, S):
    T = n_seq * S
    tb = ATTN_BLOCK
    nq = S // tb
    scale = HEAD_DIM ** -0.5

    def body(q_ref, k_ref, v_ref, fc_ref, o_ref, st_ref, qa_sc, ka_sc, m_sc, l_sc, acc_sc):
        i = pl.program_id(1)
        lane = lax.broadcasted_iota(jnp.int32, (1, LANES), 1)
        low = lane < HEAD_DIM
        ones = (1.0, 1.0, 1.0)

        @pl.when(i == 0)
        def _():
            def rows_ka(r, carry):
                r0 = pl.multiple_of(r * tb, tb)
                for h in range(N_HEADS):
                    kp = k_ref[pl.ds(r0, tb), (h // 2) * LANES : (h // 2 + 1) * LANES] * scale
                    fk = fc_ref[pl.ds(r0, tb), h : h + 1]
                    ka_sc[h, pl.ds(r0, tb), :] = _augment(kp, h % 2, ones, _split3(-fk))
                return carry

            lax.fori_loop(0, nq, rows_ka, 0)

        q0 = pl.multiple_of(i * tb, tb)
        for h in range(N_HEADS):
            qp = q_ref[:, (h // 2) * LANES : (h // 2 + 1) * LANES]
            qa_sc[h] = _augment(qp, h % 2, _split3(fc_ref[pl.ds(q0, tb), h : h + 1]), ones)
        m_sc[...] = jnp.full(m_sc.shape, -jnp.inf, F32)
        l_sc[...] = jnp.zeros_like(l_sc)
        acc_sc[...] = jnp.zeros_like(acc_sc)
        causal = lax.broadcasted_iota(jnp.int32, (tb, tb), 1) <= lax.broadcasted_iota(jnp.int32, (tb, tb), 0)

        def step(j, masked):
            c0 = pl.multiple_of(j * tb, tb)
            for p in range(N_PAIRS):
                vb = v_ref[pl.ds(c0, tb), p * LANES : (p + 1) * LANES]
                pv, al = [], []
                for hh in range(2):
                    h = 2 * p + hh
                    s = _mm_nt(qa_sc[h], ka_sc[h, pl.ds(c0, tb), :])
                    if masked:
                        s = jnp.where(causal, s, -jnp.inf)
                    m_old = m_sc[h]
                    m_new = jnp.maximum(m_old, jnp.max(s, axis=1, keepdims=True))
                    alpha = jnp.exp(m_old - m_new)
                    pe = jnp.exp(s - jnp.concatenate([m_new] * (tb // LANES), axis=1))
                    l_sc[h] = alpha * l_sc[h] + jnp.sum(pe, axis=1, keepdims=True)
                    m_sc[h] = m_new
                    pv.append(_mm(pe.astype(BF16), vb))
                    al.append(alpha)
                acc_sc[p] = jnp.where(low, al[0], al[1]) * acc_sc[p] + jnp.where(low, pv[0], pv[1])

        def loop_body(j, carry):
            step(j, False)
            return carry

        lax.fori_loop(0, i, loop_body, 0)
        step(i, True)
        st = jnp.zeros((tb, LANES), F32)
        for p in range(N_PAIRS):
            lp = jnp.where(low, l_sc[2 * p], l_sc[2 * p + 1])
            o_ref[:, p * LANES : (p + 1) * LANES] = (acc_sc[p] / lp).astype(BF16)
            for h in (2 * p, 2 * p + 1):
                st = jnp.where(lane == h, m_sc[h] + jnp.log(l_sc[h]), st)
        st_ref[...] = st

    return pl.pallas_call(
        body,
        name="attn_fwd",
        grid=(n_seq, nq),
        in_specs=[
            pl.BlockSpec((tb, ATTN_WIDTH), lambda s, i: (s * nq + i, 0)),
            pl.BlockSpec((S, ATTN_WIDTH), lambda s, i: (s, 1)),
            pl.BlockSpec((S, ATTN_WIDTH), lambda s, i: (s, 2)),
            pl.BlockSpec((S, LANES), lambda s, i: (s, 0)),
        ],
        out_specs=[
            pl.BlockSpec((tb, ATTN_WIDTH), lambda s, i: (s * nq + i, 0)),
            pl.BlockSpec((tb, LANES), lambda s, i: (s * nq + i, 0)),
        ],
        out_shape=[jax.ShapeDtypeStruct((T, ATTN_WIDTH), BF16), jax.ShapeDtypeStruct((T, LANES), F32)],
        scratch_shapes=[
            pltpu.VMEM((N_HEADS, tb, LANES), BF16),
            pltpu.VMEM((N_HEADS, S, LANES), BF16),
            pltpu.VMEM((N_HEADS, tb, LANES), F32),
            pltpu.VMEM((N_HEADS, tb, LANES), F32),
            pltpu.VMEM((N_PAIRS, tb, LANES), F32),
        ],
        compiler_params=_params(("parallel", "arbitrary")),
    )(qkv, qkv, qkv, fcol)


def _mix_out(a, pool_y, gates, x, w_ao, w_out):
    T = x.shape[0]
    tm = ROW_TILE

    def body(a_ref, py_ref, gt_ref, x_ref, wao_ref, wout_ref, mg_ref, x1_ref, ay_ref):
        ay = _mm(a_ref[...], wao_ref[...])
        ay_ref[...] = ay
        sp = _sigmoid(gt_ref[:, :D_MODEL])
        sa = _sigmoid(gt_ref[:, D_MODEL:])
        mb = (sp * py_ref[...] + sa * ay).astype(BF16)
        mg_ref[...] = mb
        x1_ref[...] = x_ref[...] + _mm(mb, wout_ref[...])

    row = lambda n: pl.BlockSpec((tm, n), lambda i: (i, 0))
    return pl.pallas_call(
        body,
        name="mix_out",
        grid=(T // tm,),
        in_specs=[row(ATTN_WIDTH), row(D_MODEL), row(2 * D_MODEL), row(D_MODEL), _const_spec(w_ao.shape), _const_spec(w_out.shape)],
        out_specs=[row(D_MODEL), row(D_MODEL), row(D_MODEL)],
        out_shape=[jax.ShapeDtypeStruct((T, D_MODEL), BF16), jax.ShapeDtypeStruct((T, D_MODEL), F32), jax.ShapeDtypeStruct((T, D_MODEL), F32)],
        compiler_params=_params(("parallel",)),
    )(a, pool_y, gates, x, w_ao, w_out)


def _ffn_fwd(x1, g2, gf, tgt, w_gate, w_up, w_down):
    T = x1.shape[0]
    tm = min(T, FF_ROW_TILE)
    nt = T // tm
    nc = D_FF // FF_CHUNK

    def body(x1_ref, g2_ref, gf_ref, tg_ref, wg_ref, wu_ref, wd_ref, h2_ref, gate_ref, up_ref, act_ref, dx2_ref, loss_ref, dgf_ref):
        x1v = x1_ref[...]
        h2, _, _ = _rms_fwd(x1v, g2_ref[...])
        h2b = h2.astype(BF16)
        h2_ref[...] = h2b
        acc = x1v
        for c in range(nc):
            sl = slice(c * FF_CHUNK, (c + 1) * FF_CHUNK)
            gate = _mm(h2b, wg_ref[:, sl])
            up = _mm(h2b, wu_ref[:, sl])
            gate_ref[:, sl] = gate.astype(BF16)
            up_ref[:, sl] = up.astype(BF16)
            act = (gate * _sigmoid(gate) * up).astype(BF16)
            act_ref[:, sl] = act
            acc = acc + _mm(act, wd_ref[sl, :])
        gfv = gf_ref[...]
        y, xh, r = _rms_fwd(acc, gfv)
        err = y - tg_ref[...]
        part = 0.5 * jnp.sum(jnp.mean(err * err, axis=-1, keepdims=True), axis=0, keepdims=True)
        dx2, dgrow = _rms_bwd(err * (1.0 / D_MODEL), xh, r, gfv)
        dx2_ref[...] = dx2

        @pl.when(pl.program_id(0) == 0)
        def _():
            dgf_ref[...] = jnp.zeros_like(dgf_ref)
            loss_ref[...] = jnp.zeros_like(loss_ref)

        dgf_ref[...] += jnp.sum(dgrow, axis=0, keepdims=True)
        loss_ref[...] += jnp.broadcast_to(part, loss_ref.shape)

    row = lambda n: pl.BlockSpec((tm, n), lambda i: (i, 0))
    return pl.pallas_call(
        body,
        name="ffn_fwd",
        grid=(nt,),
        in_specs=[
            row(D_MODEL), _const_spec((1, D_MODEL)), _const_spec((1, D_MODEL)), row(D_MODEL),
            _const_spec(w_gate.shape), _const_spec(w_up.shape), _const_spec(w_down.shape),
        ],
        out_specs=[
            row(D_MODEL), row(D_FF), row(D_FF), row(D_FF), row(D_MODEL),
            pl.BlockSpec((8, LANES), lambda i: (0, 0)),
            pl.BlockSpec((1, D_MODEL), lambda i: (0, 0)),
        ],
        out_shape=[
            jax.ShapeDtypeStruct((T, D_MODEL), BF16),
            jax.ShapeDtypeStruct((T, D_FF), BF16),
            jax.ShapeDtypeStruct((T, D_FF), BF16),
            jax.ShapeDtypeStruct((T, D_FF), BF16),
            jax.ShapeDtypeStruct((T, D_MODEL), F32),
            jax.ShapeDtypeStruct((8, LANES), F32),
            jax.ShapeDtypeStruct((1, D_MODEL), F32),
        ],
        compiler_params=_params(("arbitrary",)),
    )(x1, g2, gf, tgt, w_gate, w_up, w_down)


def _ffn_bwd(dx2, gate, up, x1, g2, w_gate, w_up, w_down):
    T = x1.shape[0]
    tm = min(T, FF_ROW_TILE)
    nc = D_FF // FF_CHUNK

    def body(dx2_ref, gate_ref, up_ref, x1_ref, g2_ref, wg_ref, wu_ref, wd_ref, dgate_ref, dup_ref, dx1_ref, dg2_ref):
        dx2v = dx2_ref[...]
        dx2b = dx2v.astype(BF16)
        dh2 = jnp.zeros((tm, D_MODEL), F32)
        for c in range(nc):
            sl = slice(c * FF_CHUNK, (c + 1) * FF_CHUNK)
            dact = _mm_nt(dx2b, wd_ref[sl, :])
            gate = gate_ref[:, sl].astype(F32)
            sg = _sigmoid(gate)
            silu = gate * sg
            dgate = (dact * up_ref[:, sl].astype(F32) * (sg * (1.0 + gate * (1.0 - sg)))).astype(BF16)
            dup = (dact * silu).astype(BF16)
            dgate_ref[:, sl] = dgate
            dup_ref[:, sl] = dup
            dh2 = dh2 + _mm_nt(dgate, wg_ref[:, sl]) + _mm_nt(dup, wu_ref[:, sl])
        g2v = g2_ref[...]
        _, xh, r = _rms_fwd(x1_ref[...], g2v)
        dxn, dgrow = _rms_bwd(dh2, xh, r, g2v)
        dx1_ref[...] = dx2v + dxn

        @pl.when(pl.program_id(0) == 0)
        def _():
            dg2_ref[...] = jnp.zeros_like(dg2_ref)

        dg2_ref[...] += jnp.sum(dgrow, axis=0, keepdims=True)

    row = lambda n: pl.BlockSpec((tm, n), lambda i: (i, 0))
    return pl.pallas_call(
        body,
        name="ffn_bwd",
        grid=(T // tm,),
        in_specs=[
            row(D_MODEL), row(D_FF), row(D_FF), row(D_MODEL), _const_spec((1, D_MODEL)),
            _const_spec(w_gate.shape), _const_spec(w_up.shape), _const_spec(w_down.shape),
        ],
        out_specs=[row(D_FF), row(D_FF), row(D_MODEL), pl.BlockSpec((1, D_MODEL), lambda i: (0, 0))],
        out_shape=[
            jax.ShapeDtypeStruct((T, D_FF), BF16),
            jax.ShapeDtypeStruct((T, D_FF), BF16),
            jax.ShapeDtypeStruct((T, D_MODEL), F32),
            jax.ShapeDtypeStruct((1, D_MODEL), F32),
        ],
        compiler_params=_params(("arbitrary",), VMEM_LIMIT_MAX),
    )(dx2, gate, up, x1, g2, w_gate, w_up, w_down)


def _mix_bwd(dx1, gates, pool_y, attn_y, p2, scale, w_out, w_ao, w_po):
    T = dx1.shape[0]
    tm = ROW_TILE

    def body(dx1_ref, gt_ref, py_ref, ay_ref, p2_ref, sc_ref, wout_ref, wao_ref, wpo_ref, dgt_ref, dpy_ref, day_ref, da_ref, dp2_ref, dsc_ref):
        dm = _mm_nt(dx1_ref[...].astype(BF16), wout_ref[...])
        sp = _sigmoid(gt_ref[:, :D_MODEL])
        sa = _sigmoid(gt_ref[:, D_MODEL:])
        dgt_ref[:, :D_MODEL] = (dm * py_ref[...] * (sp * (1.0 - sp))).astype(BF16)
        dgt_ref[:, D_MODEL:] = (dm * ay_ref[...] * (sa * (1.0 - sa))).astype(BF16)
        dpy = (dm * sp).astype(BF16)
        day = (dm * sa).astype(BF16)
        dpy_ref[...] = dpy
        day_ref[...] = day
        da_ref[...] = _mm_nt(day, wao_ref[...]).astype(BF16)
        dp3 = _mm_nt(dpy, wpo_ref[...])
        dp2_ref[...] = (dp3 * sc_ref[...]).astype(BF16)

        @pl.when(pl.program_id(0) == 0)
        def _():
            dsc_ref[...] = jnp.zeros_like(dsc_ref)

        dsc_ref[...] += jnp.sum(dp3 * p2_ref[...], axis=0, keepdims=True)

    row = lambda n: pl.BlockSpec((tm, n), lambda i: (i, 0))
    return pl.pallas_call(
        body,
        name="mix_bwd",
        grid=(T // tm,),
        in_specs=[
            row(D_MODEL), row(2 * D_MODEL), row(D_MODEL), row(D_MODEL), row(POOL_WIDTH), _const_spec((1, POOL_WIDTH)),
            _const_spec(w_out.shape), _const_spec(w_ao.shape), _const_spec(w_po.shape),
        ],
        out_specs=[row(2 * D_MODEL), row(D_MODEL), row(D_MODEL), row(ATTN_WIDTH), row(POOL_WIDTH), pl.BlockSpec((1, POOL_WIDTH), lambda i: (0, 0))],
        out_shape=[
            jax.ShapeDtypeStruct((T, 2 * D_MODEL), BF16),
            jax.ShapeDtypeStruct((T, D_MODEL), BF16),
            jax.ShapeDtypeStruct((T, D_MODEL), BF16),
            jax.ShapeDtypeStruct((T, ATTN_WIDTH), BF16),
            jax.ShapeDtypeStruct((T, POOL_WIDTH), BF16),
            jax.ShapeDtypeStruct((1, POOL_WIDTH), F32),
        ],
        compiler_params=_params(("arbitrary",)),
    )(dx1, gates, pool_y, attn_y, p2, scale, w_out, w_ao, w_po)


def _pool_bwd(dp2, pm, mix_b, n_seq, S):
    T = n_seq * S

    def body(dp2_ref, pm_ref, mix_ref, du_ref, dmix_ref):
        g = pl.program_id(0)
        dp2v = dp2_ref[...]
        dpm = _mm_nt(dp2v, mix_ref[...])
        row = lax.broadcasted_iota(jnp.int32, dpm.shape, 0)
        w = _window_pick(g, 2.0, 4.0, 8.0, 16.0)
        e = dpm / jnp.minimum((row + 1).astype(F32), w)

        def ahead(a, k):
            return jnp.where(row < S - k, pltpu.roll(a, S - k, 0), 0.0)

        r2 = e + ahead(e, 1)
        r4 = r2 + ahead(r2, 2)
        r8 = r4 + ahead(r4, 4)
        r16 = r8 + ahead(r8, 8)
        du_ref[...] = (_window_pick(g, r2, r4, r8, r16) - dpm).astype(BF16)

        @pl.when(pl.program_id(1) == 0)
        def _():
            dmix_ref[...] = jnp.zeros_like(dmix_ref)

        dmix_ref[...] += _mm_tn(pm_ref[...], dp2v)

    grp = pl.BlockSpec((S, GROUP_DIM), lambda g, s: (s, g))
    mixs = pl.BlockSpec((None, GROUP_DIM, GROUP_DIM), lambda g, s: (g, 0, 0))
    return pl.pallas_call(
        body,
        name="pool_bwd",
        grid=(len(POOL_WINDOWS), n_seq),
        in_specs=[grp, grp, mixs],
        out_specs=[grp, mixs],
        out_shape=[jax.ShapeDtypeStruct((T, POOL_WIDTH), BF16), jax.ShapeDtypeStruct((len(POOL_WINDOWS), GROUP_DIM, GROUP_DIM), F32)],
        compiler_params=_params(("parallel", "arbitrary")),
    )(dp2, pm, mix_b)


def _attn_bwd(qkv, da, a, fcol, lse, n_seq, S):
    T = n_seq * S
    tb = ATTN_BLOCK
    nb = S // tb
    scale = HEAD_DIM ** -0.5

    def body(q_ref, k_ref, v_ref, do_ref, o_ref, fc_ref, st_ref, dq_ref, dk_ref, dv_ref, dfk_ref, dfq_ref,
             qa_sc, doa_sc, dq_acc, ka_sc, va_sc, dk_sc, dv_sc):
        j = pl.program_id(1)
        lane = lax.broadcasted_iota(jnp.int32, (1, LANES), 1)
        low = lane < HEAD_DIM
        ones = (1.0, 1.0, 1.0)
        zeros = (0.0, 0.0, 0.0)

        @pl.when(j == 0)
        def _():
            dq_acc[...] = jnp.zeros_like(dq_acc)

            def rows_q(i, carry):
                r0 = pl.multiple_of(i * tb, tb)
                for h in range(N_HEADS):
                    pair = slice((h // 2) * LANES, (h // 2 + 1) * LANES)
                    qp = q_ref[pl.ds(r0, tb), pair]
                    dop = do_ref[pl.ds(r0, tb), pair]
                    prod = dop.astype(F32) * o_ref[pl.ds(r0, tb), pair].astype(F32)
                    head = (lane >= HEAD_DIM * (h % 2)) & (lane < HEAD_DIM * (h % 2 + 1))
                    delta = jnp.sum(jnp.where(head, prod, 0.0), axis=1, keepdims=True)
                    cq = fc_ref[pl.ds(r0, tb), h : h + 1] - st_ref[pl.ds(r0, tb), h : h + 1]
                    qa_sc[h, pl.ds(r0, tb), :] = _augment(qp, h % 2, _split3(cq), ones)
                    doa_sc[h, pl.ds(r0, tb), :] = _augment(dop, h % 2, _split3(-delta), zeros)
                return carry

            lax.fori_loop(0, nb, rows_q, 0)

        c0 = pl.multiple_of(j * tb, tb)
        for h in range(N_HEADS):
            pair = slice((h // 2) * LANES, (h // 2 + 1) * LANES)
            kp = k_ref[:, pair] * scale
            ka_sc[h] = _augment(kp, h % 2, ones, _split3(-fc_ref[pl.ds(c0, tb), h : h + 1]))
            va_sc[h] = _augment(v_ref[:, pair], h % 2, ones, zeros)
        dk_sc[...] = jnp.zeros_like(dk_sc)
        dv_sc[...] = jnp.zeros_like(dv_sc)
        causal = lax.broadcasted_iota(jnp.int32, (tb, tb), 1) <= lax.broadcasted_iota(jnp.int32, (tb, tb), 0)

        def step(i, masked):
            r0 = pl.multiple_of(i * tb, tb)
            for h in range(N_HEADS):
                dob = do_ref[pl.ds(r0, tb), (h // 2) * LANES : (h // 2 + 1) * LANES]
                qa = qa_sc[h, pl.ds(r0, tb), :]
                s = _mm_nt(qa, ka_sc[h])
                if masked:
                    s = jnp.where(causal, s, -jnp.inf)
                pr = jnp.exp(s)
                dv_sc[h] += _mm_tn(pr.astype(BF16), dob)
                dsb = (pr * _mm_nt(doa_sc[h, pl.ds(r0, tb), :], va_sc[h])).astype(BF16)
                dk_sc[h] += _mm_tn(dsb, qa)
                dq_acc[h, pl.ds(r0, tb), :] += _mm(dsb, ka_sc[h])

        step(j, True)

        def loop_body(i, carry):
            step(i, False)
            return carry

        lax.fori_loop(j + 1, nb, loop_body, 0)
        dfk = jnp.zeros((tb, LANES), F32)
        for p in range(N_PAIRS):
            dk_ref[:, p * LANES : (p + 1) * LANES] = (jnp.where(low, dk_sc[2 * p], dk_sc[2 * p + 1]) * scale).astype(BF16)
            dv_ref[:, p * LANES : (p + 1) * LANES] = jnp.where(low, dv_sc[2 * p], dv_sc[2 * p + 1]).astype(BF16)
            for hh in range(2):
                b = HEAD_DIM * (1 - hh) + 3
                dfk = jnp.where(lane == 2 * p + hh, -dk_sc[2 * p + hh][:, b : b + 1], dfk)
        dfk_ref[...] = dfk

        @pl.when(j == nb - 1)
        def _():
            def rows_dq(i, carry):
                r0 = pl.multiple_of(i * tb, tb)
                dfq = jnp.zeros((tb, LANES), F32)
                for p in range(N_PAIRS):
                    parts = [dq_acc[2 * p + hh, pl.ds(r0, tb), :] for hh in range(2)]
                    dq_ref[pl.ds(r0, tb), p * LANES : (p + 1) * LANES] = jnp.where(low, parts[0], parts[1]).astype(BF16)
                    for hh in range(2):
                        b = HEAD_DIM * (1 - hh)
                        dfq = jnp.where(lane == 2 * p + hh, parts[hh][:, b : b + 1], dfq)
                dfq_ref[pl.ds(r0, tb), :] = dfq
                return carry

            lax.fori_loop(0, nb, rows_dq, 0)

    seq = lambda w, col: pl.BlockSpec((S, w), lambda s, j: (s, col))
    blk = lambda w, col: pl.BlockSpec((tb, w), lambda s, j: (s * nb + j, col))
    return pl.pallas_call(
        body,
        name="attn_bwd",
        grid=(n_seq, nb),
        in_specs=[seq(ATTN_WIDTH, 0), blk(ATTN_WIDTH, 1), blk(ATTN_WIDTH, 2), seq(ATTN_WIDTH, 0), seq(ATTN_WIDTH, 0), seq(LANES, 0), seq(LANES, 0)],
        out_specs=[seq(ATTN_WIDTH, 0), blk(ATTN_WIDTH, 0), blk(ATTN_WIDTH, 0), blk(LANES, 0), seq(LANES, 0)],
        out_shape=[
            jax.ShapeDtypeStruct((T, ATTN_WIDTH), BF16),
            jax.ShapeDtypeStruct((T, ATTN_WIDTH), BF16),
            jax.ShapeDtypeStruct((T, ATTN_WIDTH), BF16),
            jax.ShapeDtypeStruct((T, LANES), F32),
            jax.ShapeDtypeStruct((T, LANES), F32),
        ],
        scratch_shapes=[
            pltpu.VMEM((N_HEADS, S, LANES), BF16),
            pltpu.VMEM((N_HEADS, S, LANES), BF16),
            pltpu.VMEM((N_HEADS, S, LANES), F32),
            pltpu.VMEM((N_HEADS, tb, LANES), BF16),
            pltpu.VMEM((N_HEADS, tb, LANES), BF16),
            pltpu.VMEM((N_HEADS, tb, LANES), F32),
            pltpu.VMEM((N_HEADS, tb, LANES), F32),
        ],
        compiler_params=_params(("parallel", "arbitrary")),
    )(qkv, qkv, qkv, da, a, fcol, lse)


def _forget_bwd(dfk, dfq, fl, b_pad, n_seq, S):
    def body(df_ref, dfq_ref, fl_ref, b_ref, dfl_ref, db_ref):
        t = (df_ref[...] + dfq_ref[...]).T
        lane = lax.broadcasted_iota(jnp.int32, t.shape, 1)
        k = 1
        while k < S:
            t = t + jnp.where(lane < S - k, pltpu.roll(t, S - k, 1), 0.0)
            k *= 2
        dfl = t.T * _sigmoid(-(fl_ref[...] + b_ref[...]))
        dfl_ref[...] = dfl.astype(BF16)

        @pl.when(pl.program_id(0) == 0)
        def _():
            db_ref[...] = jnp.zeros_like(db_ref)

        db_ref[...] += jnp.sum(dfl, axis=0, keepdims=True)

    return pl.pallas_call(
        body,
        name="forget_bwd",
        grid=(n_seq,),
        in_specs=[
            pl.BlockSpec((S, LANES), lambda s: (s, 0)),
            pl.BlockSpec((S, LANES), lambda s: (s, 0)),
            pl.BlockSpec((S, FL_PAD), lambda s: (s, 0)),
            _const_spec((1, FL_PAD)),
        ],
        out_specs=[pl.BlockSpec((S, FL_PAD), lambda s: (s, 0)), pl.BlockSpec((1, FL_PAD), lambda s: (0, 0))],
        out_shape=[jax.ShapeDtypeStruct((n_seq * S, FL_PAD), BF16), jax.ShapeDtypeStruct((1, FL_PAD), F32)],
        compiler_params=_params(("arbitrary",)),
    )(dfk, dfq, fl, b_pad)


def _in_proj_bwd(du, dq, dk, dv, dfl, dgates, x, dx1, g1, w_uqkv, w_fl, w_g):
    T = x.shape[0]
    tm = ROW_TILE

    def body(du_ref, dq_ref, dk_ref, dv_ref, dfl_ref, dgt_ref, x_ref, dx1_ref, g_ref, wa_ref, wf_ref, wg_ref, dx_ref, dg_ref):
        dh = _mm_nt(dgt_ref[...], wg_ref[...]) + _mm_nt(dfl_ref[...], wf_ref[...])
        for n, ref in enumerate((du_ref, dq_ref, dk_ref, dv_ref)):
            dh = dh + _mm_nt(ref[...], wa_ref[:, n * 512 : (n + 1) * 512])
        gv = g_ref[...]
        _, xh, r = _rms_fwd(x_ref[...], gv)
        dxn, dgrow = _rms_bwd(dh, xh, r, gv)
        dx_ref[...] = dx1_ref[...] + dxn

        @pl.when(pl.program_id(0) == 0)
        def _():
            dg_ref[...] = jnp.zeros_like(dg_ref)

        dg_ref[...] += jnp.sum(dgrow, axis=0, keepdims=True)

    row = lambda n: pl.BlockSpec((tm, n), lambda i: (i, 0))
    return pl.pallas_call(
        body,
        name="in_proj_bwd",
        grid=(T // tm,),
        in_specs=[
            row(512), row(512), row(512), row(512), row(FL_PAD), row(2 * D_MODEL), row(D_MODEL), row(D_MODEL), _const_spec((1, D_MODEL)),
            _const_spec(w_uqkv.shape), _const_spec(w_fl.shape), _const_spec(w_g.shape),
        ],
        out_specs=[row(D_MODEL), pl.BlockSpec((1, D_MODEL), lambda i: (0, 0))],
        out_shape=[jax.ShapeDtypeStruct((T, D_MODEL), F32), jax.ShapeDtypeStruct((1, D_MODEL), F32)],
        compiler_params=_params(("arbitrary",)),
    )(du, dq, dk, dv, dfl, dgates, x, dx1, g1, w_uqkv, w_fl, w_g)


def _pick_block(n):
    for b in (512, 1408, 256, 128):
        if n % b == 0:
            return b
    raise ValueError(n)


def _matmul_tn(a, b, name, row_sharded=False):
    T, K = a.shape
    N = b.shape[1]
    bt, bk, bn = min(T, DW_TOKENS), _pick_block(K), _pick_block(N)
    nt = T // bt
    r = K // N_DEV
    assert not row_sharded or bk == 4 * r

    def body(a_ref, b_ref, o_ref, acc):
        @pl.when(pl.program_id(2) == 0)
        def _():
            acc[...] = jnp.zeros_like(acc)

        acc[...] += _mm_tn(a_ref[...].astype(BF16), b_ref[...].astype(BF16))

        @pl.when(pl.program_id(2) == nt - 1)
        def _():
            if row_sharded:
                for chip in range(2):
                    for core in range(2):
                        d = 2 * chip + core
                        o_ref[core, chip] = acc[d * r : (d + 1) * r, :].astype(BF16)
            else:
                o_ref[...] = acc[...].astype(BF16)

    if row_sharded:
        out_spec = pl.BlockSpec((2, 2, r, bn), lambda k, n, t: (0, k, 0, n))
        out_shape = jax.ShapeDtypeStruct((2, 4, r, N), BF16)
    else:
        out_spec = pl.BlockSpec((bk, bn), lambda k, n, t: (k, n))
        out_shape = jax.ShapeDtypeStruct((K, N), BF16)
    return pl.pallas_call(
        body,
        name=name,
        grid=(K // bk, N // bn, nt),
        in_specs=[pl.BlockSpec((bt, bk), lambda k, n, t: (t, k)), pl.BlockSpec((bt, bn), lambda k, n, t: (t, n))],
        out_specs=out_spec,
        out_shape=out_shape,
        scratch_shapes=[pltpu.VMEM((bk, bn), F32)],
        compiler_params=_params(("parallel", "parallel", "arbitrary")),
    )(a, b)


def _position():
    return lax.axis_index("x"), lax.axis_index("y"), lax.axis_index("c")


_HBM = pl.BlockSpec(memory_space=pl.ANY)


def _all_gather(blocks, name):
    n = len(blocks)

    def body(*refs):
        xs, outs = refs[:n], refs[n : 2 * n]
        send_sems, recv_sems, local_sems = refs[2 * n :]
        x, y, c = _position()
        me, sibling = (x, y, c), (x, y, 1 - c)
        chips = [(1 - x, y), (x, 1 - y), (1 - x, 1 - y)]

        def rows(a, px, py, pc):
            return outs[a].at[4 * px + 2 * py + pc]

        def copy(a, k, blk, to, src=None):
            return pltpu.make_async_remote_copy(
                src_ref=rows(a, *blk) if src is None else src, dst_ref=rows(a, *blk),
                send_sem=send_sems.at[7 * a + k], recv_sem=recv_sems.at[7 * a + k], device_id=to, device_id_type=MESH,
            )

        mine = [pltpu.make_async_copy(xs[a], rows(a, *me), local_sems.at[a]) for a in range(n)]
        for cp in mine:
            cp.start()
        first = []
        for a in range(n):
            first.append(copy(a, 0, me, sibling, src=xs[a]))
            first += [copy(a, 1 + j, me, (*chip, c), src=xs[a]) for j, chip in enumerate(chips)]
        for cp in first:
            cp.start()
        passed = []
        for j, chip in enumerate(chips):
            for a in range(n):
                copy(a, 1 + j, (*chip, c), me).wait_recv()
                passed.append(copy(a, 4 + j, (*chip, c), sibling))
                passed[-1].start()
        for a in range(n):
            copy(a, 0, sibling, me).wait_recv()
        for j, chip in enumerate(chips):
            for a in range(n):
                copy(a, 4 + j, (*chip, 1 - c), me).wait_recv()
        for cp in first + passed:
            cp.wait_send()
        for cp in mine:
            cp.wait()

    return pl.pallas_call(
        body,
        name=name,
        out_shape=[jax.ShapeDtypeStruct((N_DEV, *b.shape), b.dtype) for b in blocks],
        in_specs=[_HBM] * n,
        out_specs=[_HBM] * n,
        scratch_shapes=[pltpu.SemaphoreType.DMA((7 * n,)), pltpu.SemaphoreType.DMA((7 * n,)), pltpu.SemaphoreType.DMA((n,))],
    )(*blocks)


def _sibling_exchange(sends):
    n = len(sends)

    def body(*refs):
        srcs, dsts = refs[:n], refs[n : 2 * n]
        send_sems, recv_sems = refs[2 * n :]
        x, y, c = _position()
        cps = [
            pltpu.make_async_remote_copy(
                src_ref=srcs[a].at[1 - c], dst_ref=dsts[a], send_sem=send_sems.at[a], recv_sem=recv_sems.at[a],
                device_id=(x, y, 1 - c), device_id_type=MESH,
            )
            for a in range(n)
        ]
        for cp in cps:
            cp.start()
        for cp in cps:
            cp.wait()

    return pl.pallas_call(
        body,
        name="rs_sibling",
        out_shape=[jax.ShapeDtypeStruct(s.shape[1:], s.dtype) for s in sends],
        in_specs=[_HBM] * n,
        out_specs=[_HBM] * n,
        scratch_shapes=[pltpu.SemaphoreType.DMA((n,)), pltpu.SemaphoreType.DMA((n,))],
    )(*sends)


def _rows_tile(r):
    return ROW_TILE if r % ROW_TILE == 0 else r


def _pair_sum(send, got, core, name):
    _, _, r, c = send.shape
    br = _rows_tile(r)

    def body(core_ref, a_ref, b_ref, o_ref):
        o_ref[...] = (a_ref[...].astype(F32) + b_ref[...].astype(F32)).astype(o_ref.dtype)

    return pl.pallas_call(
        body,
        name=name,
        grid_spec=pltpu.PrefetchScalarGridSpec(
            num_scalar_prefetch=1,
            grid=(4, r // br),
            in_specs=[
                pl.BlockSpec((None, None, br, c), lambda n, i, core: (core[0], n, i, 0)),
                pl.BlockSpec((None, br, c), lambda n, i, core: (n, i, 0)),
            ],
            out_specs=pl.BlockSpec((None, br, c), lambda n, i, core: (n, i, 0)),
        ),
        out_shape=jax.ShapeDtypeStruct((4, r, c), send.dtype),
        compiler_params=_params(("parallel", "parallel")),
    )(core, send, got)


def _chip_exchange(pairs):
    n = len(pairs)

    def body(*refs):
        srcs, dsts = refs[:n], refs[n : 2 * n]
        send_sems, recv_sems = refs[2 * n :]
        x, y, c = _position()
        chips = [(1 - x, y), (x, 1 - y), (1 - x, 1 - y)]
        cps = [
            pltpu.make_async_remote_copy(
                src_ref=srcs[a].at[2 * cx + cy], dst_ref=dsts[a].at[j], send_sem=send_sems.at[3 * a + j], recv_sem=recv_sems.at[3 * a + j],
                device_id=(cx, cy, c), device_id_type=MESH,
            )
            for a in range(n)
            for j, (cx, cy) in enumerate(chips)
        ]
        for cp in cps:
            cp.start()
        for cp in cps:
            cp.wait()

    return pl.pallas_call(
        body,
        name="rs_chips",
        out_shape=[jax.ShapeDtypeStruct((3, *p.shape[1:]), p.dtype) for p in pairs],
        in_specs=[_HBM] * n,
        out_specs=[_HBM] * n,
        scratch_shapes=[pltpu.SemaphoreType.DMA((3 * n,)), pltpu.SemaphoreType.DMA((3 * n,))],
    )(*pairs)


def _adamw(w, g, m, v):
    m = ADAM_B1 * m + (1.0 - ADAM_B1) * g
    v = ADAM_B2 * v + (1.0 - ADAM_B2) * (g * g)
    m_hat = m / (1.0 - ADAM_B1 ** ADAM_STEP)
    v_hat = v / (1.0 - ADAM_B2 ** ADAM_STEP)
    delta = -ADAM_LR * (m_hat / (jnp.sqrt(v_hat) + ADAM_EPS) + ADAM_WD * w)
    return delta, m, v


def _shard_update(send, got, recv, w, m, v, pos, name):
    _, r, c = w.shape
    br = _rows_tile(r)

    def body(pos_ref, a_ref, b_ref, r_ref, w_ref, m_ref, v_ref, g_ref, d_ref, nm_ref, nv_ref):
        g = a_ref[...].astype(F32) + b_ref[...].astype(F32)
        for n in range(3):
            g = g + r_ref[n].astype(F32)
        g_ref[...] = g
        d_ref[...], nm_ref[...], nv_ref[...] = _adamw(w_ref[...], g, m_ref[...], v_ref[...])

    own = pl.BlockSpec((None, br, c), lambda i, pos: (0, i, 0))
    return pl.pallas_call(
        body,
        name=name,
        grid_spec=pltpu.PrefetchScalarGridSpec(
            num_scalar_prefetch=1,
            grid=(r // br,),
            in_specs=[
                pl.BlockSpec((None, None, br, c), lambda i, pos: (pos[0], pos[1], i, 0)),
                pl.BlockSpec((None, br, c), lambda i, pos: (pos[1], i, 0)),
                pl.BlockSpec((3, br, c), lambda i, pos: (0, i, 0)),
                own, own, own,
            ],
            out_specs=[own, own, own, own],
        ),
        out_shape=[jax.ShapeDtypeStruct((1, r, c), F32)] * 4,
        compiler_params=_params(("parallel",)),
    )(pos, send, got, recv, w, m, v)


def _small_update(parts, w, m, v):
    R = w.shape[0]

    def body(p_ref, w_ref, m_ref, v_ref, g_ref, d_ref, nm_ref, nv_ref):
        g = p_ref[0]
        for n in range(1, N_DEV):
            g = g + p_ref[n]
        g_ref[...] = g
        d_ref[...], nm_ref[...], nv_ref[...] = _adamw(w_ref[...], g, m_ref[...], v_ref[...])

    return pl.pallas_call(
        body,
        name="small_update",
        out_shape=[jax.ShapeDtypeStruct((R, LANES), F32)] * 4,
        compiler_params=pltpu.CompilerParams(vmem_limit_bytes=VMEM_LIMIT),
    )(parts, w, m, v)


_SHARD_AXIS = (1, 1, 1, 0, 1, 1, 0)


def _full_from_gathered(t, axis):
    if axis == 0:
        return t.reshape(N_DEV * t.shape[1], t.shape[2])
    return jnp.concatenate([t[d] for d in range(N_DEV)], axis=1)


def _send_from_cols(t):
    c = t.shape[1] // N_DEV
    return jnp.stack([jnp.stack([t[:, (2 * chip + core) * c : (2 * chip + core + 1) * c] for chip in range(4)]) for core in range(2)])


_SMALL = (("norm1_g", 8), ("norm2_g", 8), ("norm_f_g", 8), ("b_forget", 8), ("pool_scale", 8), ("pool_mix", 512))
_SMALL_ROWS = sum(r for _, r in _SMALL) + 8


def _pack_small(vals, loss_row):
    parts = []
    for (name, rows), t in zip(_SMALL, vals):
        f = t.astype(F32).reshape(-1)
        f = jnp.concatenate([f, jnp.zeros((rows * LANES - f.shape[0],), F32)]).reshape(rows, LANES)
        parts.append(f)
    parts.append(loss_row)
    return jnp.concatenate(parts, axis=0)


def _unpack_small(packed, shapes):
    out, off = [], 0
    for (name, rows), shape in zip(_SMALL, shapes):
        n = 1
        for s in shape:
            n *= s
        out.append(packed[off : off + rows].reshape(-1)[:n].reshape(shape))
        off += rows
    return out, packed[off, 0]


def _local_grads(x, tgt, g1, g2, gf, b_forget, pool_mix, pool_scale, w_in, w_po, w_ao, w_out, w_gate, w_up, w_down):
    n_seq, S, _ = x.shape
    T = n_seq * S
    x2 = x.reshape(T, D_MODEL)
    tg2 = tgt.reshape(T, D_MODEL)
    w_uqkv = w_in[:, : POOL_WIDTH + 3 * ATTN_WIDTH]
    w_fl = jnp.concatenate([w_in[:, 2048 : 2048 + N_HEADS], jnp.zeros((D_MODEL, FL_PAD - N_HEADS), BF16)], axis=1)
    w_g = w_in[:, 2048 + N_HEADS :]
    b_pad = jnp.concatenate([b_forget.reshape(1, N_HEADS), jnp.zeros((1, FL_PAD - N_HEADS), F32)], axis=1)
    mix_b = pool_mix.reshape(len(POOL_WINDOWS), GROUP_DIM, GROUP_DIM).astype(BF16)
    scale = pool_scale.reshape(1, POOL_WIDTH)
    g1 = g1.reshape(1, D_MODEL)
    g2 = g2.reshape(1, D_MODEL)
    gf = gf.reshape(1, D_MODEL)

    h, u, qkv, fl, gates = _in_proj(x2, g1, w_uqkv, w_fl, w_g)
    fcol = _forget_fwd(fl, b_pad, n_seq, S)
    pm, p2, p3, pool_y = _pool_fwd(u, mix_b, scale, w_po, n_seq, S)
    a, lse = _attn_fwd(qkv, fcol, n_seq, S)
    merged, x1, attn_y = _mix_out(a, pool_y, gates, x2, w_ao, w_out)
    h2, gate, up, act, dx2, loss_rows, dgf = _ffn_fwd(x1, g2, gf, tg2, w_gate, w_up, w_down)

    dgate, dup, dx1, dg2 = _ffn_bwd(dx2, gate, up, x1, g2, w_gate, w_up, w_down)
    dgates, dpy, day, da, dp2, dscale = _mix_bwd(dx1, gates, pool_y, attn_y, p2, scale, w_out, w_ao, w_po)
    du, dmix = _pool_bwd(dp2, pm, mix_b, n_seq, S)
    dq, dk, dv, dfk, dfq = _attn_bwd(qkv, da, a, fcol, lse, n_seq, S)
    dfl, db = _forget_bwd(dfk, dfq, fl, b_pad, n_seq, S)
    dx, dg1 = _in_proj_bwd(du, dq, dk, dv, dfl, dgates, x2, dx1, g1, w_uqkv, w_fl, w_g)

    d_w_in = jnp.concatenate(
        [
            _matmul_tn(h, du, "dw_u"), _matmul_tn(h, dq, "dw_q"), _matmul_tn(h, dk, "dw_k"), _matmul_tn(h, dv, "dw_v"),
            _matmul_tn(h, dfl, "dw_fl")[:, :N_HEADS], _matmul_tn(h, dgates, "dw_gates"),
        ],
        axis=1,
    )
    sends = [
        _send_from_cols(d_w_in),
        _send_from_cols(_matmul_tn(p3, dpy, "dw_pool_out")),
        _send_from_cols(_matmul_tn(a, day, "dw_attn_out")),
        _matmul_tn(merged, dx1, "dw_out", row_sharded=True),
        _send_from_cols(_matmul_tn(h2, dgate, "dw_ffn_gate")),
        _send_from_cols(_matmul_tn(h2, dup, "dw_ffn_up")),
        _matmul_tn(act, dx2, "dw_ffn_down", row_sharded=True),
    ]
    small = (dg1, dg2, dgf, db[:, :N_HEADS], dscale, dmix)
    return loss_rows, dx.reshape(n_seq, S, D_MODEL), sends, small


def kernel(x, norm1_g, w_in, b_forget, pool_mix, pool_scale, w_pool_out, w_attn_out, w_out, norm2_g, w_ffn_gate, w_ffn_up, w_ffn_down, norm_f_g, loss_target, m_norm1_g, m_w_in, m_b_forget, m_pool_mix, m_pool_scale, m_w_pool_out, m_w_attn_out, m_w_out, m_norm2_g, m_w_ffn_gate, m_w_ffn_up, m_w_ffn_down, m_norm_f_g, v_norm1_g, v_w_in, v_b_forget, v_pool_mix, v_pool_scale, v_w_pool_out, v_w_attn_out, v_w_out, v_norm2_g, v_w_ffn_gate, v_w_ffn_up, v_w_ffn_down, v_norm_f_g):
    names = ("w_in", "w_pool_out", "w_attn_out", "w_out", "w_ffn_gate", "w_ffn_up", "w_ffn_down")
    w_sh = (w_in, w_pool_out, w_attn_out, w_out, w_ffn_gate, w_ffn_up, w_ffn_down)
    m_sh = (m_w_in, m_w_pool_out, m_w_attn_out, m_w_out, m_w_ffn_gate, m_w_ffn_up, m_w_ffn_down)
    v_sh = (v_w_in, v_w_pool_out, v_w_attn_out, v_w_out, v_w_ffn_gate, v_w_ffn_up, v_w_ffn_down)

    gathered = _all_gather([w[0].astype(BF16) for w in w_sh], "weights_all_gather")
    whole = [_full_from_gathered(t, axis) for t, axis in zip(gathered, _SHARD_AXIS)]

    loss_rows, grad_x, sends, small = _local_grads(x, loss_target, norm1_g, norm2_g, norm_f_g, b_forget, pool_mix, pool_scale, *whole)

    cx, cy, cc = _position()
    core = jnp.reshape(cc, (1,)).astype(jnp.int32)
    pos = jnp.stack([cc, 2 * cx + cy]).astype(jnp.int32)
    gots = _sibling_exchange(sends)
    pairs = [_pair_sum(s, g, core, "pair_sum_" + n) for s, g, n in zip(sends, gots, names)]
    recvs = _chip_exchange(pairs)
    updates = [
        _shard_update(s, g, r, w, m, v, pos, "update_" + n)
        for s, g, r, w, m, v, n in zip(sends, gots, recvs, w_sh, m_sh, v_sh, names)
    ]
    g_w, d_w, nm_w, nv_w = zip(*updates)

    small_w = (norm1_g, norm2_g, norm_f_g, b_forget, pool_scale, pool_mix)
    small_m = (m_norm1_g, m_norm2_g, m_norm_f_g, m_b_forget, m_pool_scale, m_pool_mix)
    small_v = (v_norm1_g, v_norm2_g, v_norm_f_g, v_b_forget, v_pool_scale, v_pool_mix)
    zero_row = jnp.zeros((8, LANES), F32)
    (parts,) = _all_gather([_pack_small(small, loss_rows)], "small_all_gather")
    g_s, d_s, nm_s, nv_s = _small_update(parts, _pack_small(small_w, zero_row), _pack_small(small_m, zero_row), _pack_small(small_v, zero_row))
    shapes = [t.shape for t in small_w]
    (g1, g2, gf, gb, gsc, gmix), loss = _unpack_small(g_s, shapes)
    (d1, d2, df, db_, dsc, dmx), _ = _unpack_small(d_s, shapes)
    (m1, m2, mf, mb, msc, mmx), _ = _unpack_small(nm_s, shapes)
    (v1, v2, vf, vb, vsc, vmx), _ = _unpack_small(nv_s, shapes)

    def ordered(n1, win, b, mix, sc, wpo, wao, wout, n2, wg, wu, wd, nf):
        return (n1, win, b, mix, sc, wpo, wao, wout, n2, wg, wu, wd, nf)

    grads = ordered(g1, g_w[0], gb, gmix, gsc, g_w[1], g_w[2], g_w[3], g2, g_w[4], g_w[5], g_w[6], gf)
    deltas = ordered(d1, d_w[0], db_, dmx, dsc, d_w[1], d_w[2], d_w[3], d2, d_w[4], d_w[5], d_w[6], df)
    new_m = ordered(m1, nm_w[0], mb, mmx, msc, nm_w[1], nm_w[2], nm_w[3], m2, nm_w[4], nm_w[5], nm_w[6], mf)
    new_v = ordered(v1, nv_w[0], vb, vmx, vsc, nv_w[1], nv_w[2], nv_w[3], v2, nv_w[4], nv_w[5], nv_w[6], vf)
    return (loss, grad_x, *grads, *deltas, *new_m, *new_v)
```

```python
import functools

import jax
import jax.numpy as jnp
from jax import lax
from jax.experimental import pallas as pl
from jax.experimental.pallas import tpu as pltpu

F32 = jnp.float32
BF16 = jnp.bfloat16
MESH = pl.DeviceIdType.MESH

D_MODEL = 1024
POOL_WINDOWS = (2, 4, 8, 16)
POOL_WIDTH = 512
GROUP_DIM = 128
ATTN_WIDTH = 512
HEAD_DIM = 64
N_HEADS = 8
N_PAIRS = 4
D_FF = 2816
RMS_EPS = 1e-6
N_DEV = 8
LANES = 128
FL_PAD = 128

ADAM_LR = 0.001
ADAM_B1 = 0.9
ADAM_B2 = 0.999
ADAM_EPS = 1e-08
ADAM_WD = 0.01
ADAM_STEP = 10

VMEM_LIMIT = 56 * 1024 * 1024
VMEM_LIMIT_MAX = 60 * 1024 * 1024
ROW_TILE = 256
ATTN_BLOCK = 256
FF_CHUNK = 256
FF_ROW_TILE = 512
DW_TOKENS = 2048


def _mm(a, b):
    return jnp.dot(a, b, preferred_element_type=F32)


def _mm_nt(a, b):
    return lax.dot_general(a, b, (((1,), (1,)), ((), ())), preferred_element_type=F32)


def _mm_tn(a, b):
    return lax.dot_general(a, b, (((0,), (0,)), ((), ())), preferred_element_type=F32)


def _sigmoid(x):
    return 1.0 / (1.0 + jnp.exp(-x))


def _params(sem, vmem=VMEM_LIMIT):
    return pltpu.CompilerParams(dimension_semantics=sem, vmem_limit_bytes=vmem)


def _const_spec(shape):
    nd = len(shape)
    return pl.BlockSpec(shape, lambda *_: (0,) * nd, pipeline_mode=pl.Buffered(1))


def _rms_fwd(x, g):
    r = lax.rsqrt(jnp.mean(x * x, axis=-1, keepdims=True) + RMS_EPS)
    xh = x * r
    return xh * g, xh, r


def _rms_bwd(dy, xh, r, g):
    dxh = dy * g
    dx = r * (dxh - xh * jnp.mean(dxh * xh, axis=-1, keepdims=True))
    return dx, dy * xh


def _in_proj(x, g1, w_uqkv, w_fl, w_g, token):
    T = x.shape[0]
    tm = ROW_TILE

    def body(x_ref, g_ref, wa_ref, wf_ref, wg_ref, token_ref, h_ref, u_ref, qkv_ref, fl_ref, gt_ref):
        h, _, _ = _rms_fwd(x_ref[...], g_ref[...])
        hb = h.astype(BF16)
        h_ref[...] = hb
        z = _mm(hb, wa_ref[...])
        u_ref[...] = z[:, :POOL_WIDTH]
        qkv_ref[...] = z[:, POOL_WIDTH:].astype(BF16)
        fl_ref[...] = _mm(hb, wf_ref[...])
        gt_ref[...] = _mm(hb, wg_ref[...])

    row = lambda n: pl.BlockSpec((tm, n), lambda i: (i, 0))
    return pl.pallas_call(
        body,
        name="in_proj",
        grid=(T // tm,),
        in_specs=[row(D_MODEL), _const_spec((1, D_MODEL)), _const_spec(w_uqkv.shape), _const_spec(w_fl.shape), _const_spec(w_g.shape), _HBM],
        out_specs=[row(D_MODEL), row(POOL_WIDTH), row(3 * ATTN_WIDTH), row(FL_PAD), row(2 * D_MODEL)],
        out_shape=[
            jax.ShapeDtypeStruct((T, D_MODEL), BF16),
            jax.ShapeDtypeStruct((T, POOL_WIDTH), F32),
            jax.ShapeDtypeStruct((T, 3 * ATTN_WIDTH), BF16),
            jax.ShapeDtypeStruct((T, FL_PAD), F32),
            jax.ShapeDtypeStruct((T, 2 * D_MODEL), F32),
        ],
        compiler_params=_params(("parallel",)),
    )(x, g1, w_uqkv, w_fl, w_g, token)


def _log_sigmoid(x):
    return jnp.minimum(x, 0.0) - jnp.log(1.0 + jnp.exp(-jnp.abs(x)))


def _forget_fwd(fl, b_pad, n_seq, S):
    def body(fl_ref, b_ref, fcol_ref):
        lf = _log_sigmoid(fl_ref[...] + b_ref[...])
        t = lf.T
        lane = lax.broadcasted_iota(jnp.int32, t.shape, 1)
        k = 1
        while k < S:
            t = t + jnp.where(lane >= k, pltpu.roll(t, k, 1), 0.0)
            k *= 2
        fcol_ref[...] = t.T

    return pl.pallas_call(
        body,
        name="forget_fwd",
        grid=(n_seq,),
        in_specs=[pl.BlockSpec((S, FL_PAD), lambda s: (s, 0)), _const_spec((1, FL_PAD))],
        out_specs=pl.BlockSpec((S, FL_PAD), lambda s: (s, 0)),
        out_shape=jax.ShapeDtypeStruct((n_seq * S, FL_PAD), F32),
        compiler_params=_params(("parallel",)),
    )(fl, b_pad)


def _window_pick(g, v2, v4, v8, v16):
    return jnp.where(g == 0, v2, jnp.where(g == 1, v4, jnp.where(g == 2, v8, v16)))


def _pool_fwd(u, mix_b, scale, w_po, n_seq, S):
    T = n_seq * S

    def body(u_ref, mix_ref, sc_ref, wpo_ref, pm_ref, p2_ref, p3_ref, py_ref):
        g = pl.program_id(1)
        uu = u_ref[...]
        row = lax.broadcasted_iota(jnp.int32, uu.shape, 0)

        def back(a, k):
            return jnp.where(row >= k, pltpu.roll(a, k, 0), 0.0)

        s2 = uu + back(uu, 1)
        s4 = s2 + back(s2, 2)
        s8 = s4 + back(s4, 4)
        s16 = s8 + back(s8, 8)
        w = _window_pick(g, 2.0, 4.0, 8.0, 16.0)
        cnt = jnp.minimum((row + 1).astype(F32), w)
        pm = _window_pick(g, s2, s4, s8, s16) / cnt - uu
        pmb = pm.astype(BF16)
        pm_ref[...] = pmb
        p2 = _mm(pmb, mix_ref[...])
        p2_ref[...] = p2
        p3 = (p2 * sc_ref[...]).astype(BF16)
        p3_ref[...] = p3

        @pl.when(g == 0)
        def _():
            py_ref[...] = jnp.zeros_like(py_ref)

        py_ref[...] += _mm(p3, wpo_ref[...])

    grp = pl.BlockSpec((S, GROUP_DIM), lambda s, g: (s, g))
    return pl.pallas_call(
        body,
        name="pool_fwd",
        grid=(n_seq, len(POOL_WINDOWS)),
        in_specs=[
            grp,
            pl.BlockSpec((None, GROUP_DIM, GROUP_DIM), lambda s, g: (g, 0, 0)),
            pl.BlockSpec((1, GROUP_DIM), lambda s, g: (0, g)),
            pl.BlockSpec((GROUP_DIM, D_MODEL), lambda s, g: (g, 0)),
        ],
        out_specs=[grp, grp, grp, pl.BlockSpec((S, D_MODEL), lambda s, g: (s, 0))],
        out_shape=[
            jax.ShapeDtypeStruct((T, POOL_WIDTH), BF16),
            jax.ShapeDtypeStruct((T, POOL_WIDTH), F32),
            jax.ShapeDtypeStruct((T, POOL_WIDTH), BF16),
            jax.ShapeDtypeStruct((T, D_MODEL), F32),
        ],
        compiler_params=_params(("parallel", "arbitrary")),
    )(u, mix_b, scale, w_po)


def _split3(v):
    hi = v.astype(BF16).astype(F32)
    r = v - hi
    mid = r.astype(BF16).astype(F32)
    lo = (r - mid).astype(BF16).astype(F32)
    return hi, mid, lo


def _augment(xp, hh, first, second):
    lane = lax.broadcasted_iota(jnp.int32, (1, LANES), 1)
    head = (lane >= HEAD_DIM * hh) & (lane < HEAD_DIM * (hh + 1))
    b = HEAD_DIM * (1 - hh)
    out = jnp.where(head, xp.astype(F32), 0.0)
    for n, col in enumerate(tuple(first) + tuple(second)):
        out = jnp.where(lane == b + n, col, out)
    return out.astype(BF16)


def _attn_fwd(qkv, fcol, n_seq, S):
    T = n_seq * S
    tb = ATTN_BLOCK
    nq = S // tb
    scale = HEAD_DIM ** -0.5

    def body(q_ref, k_ref, v_ref, fc_ref, o_ref, st_ref, qa_sc, ka_sc, m_sc, l_sc, acc_sc):
        i = pl.program_id(1)
        lane = lax.broadcasted_iota(jnp.int32, (1, LANES), 1)
        low = lane < HEAD_DIM
        ones = (1.0, 1.0, 1.0)

        @pl.when(i == 0)
        def _():
            def rows_ka(r, carry):
                r0 = pl.multiple_of(r * tb, tb)
                for h in range(N_HEADS):
                    kp = k_ref[pl.ds(r0, tb), (h // 2) * LANES : (h // 2 + 1) * LANES] * scale
                    fk = fc_ref[pl.ds(r0, tb), h : h + 1]
                    ka_sc[h, pl.ds(r0, tb), :] = _augment(kp, h % 2, ones, _split3(-fk))
                return carry

            lax.fori_loop(0, nq, rows_ka, 0)

        q0 = pl.multiple_of(i * tb, tb)
        for h in range(N_HEADS):
            qp = q_ref[:, (h // 2) * LANES : (h // 2 + 1) * LANES]
            qa_sc[h] = _augment(qp, h % 2, _split3(fc_ref[pl.ds(q0, tb), h : h + 1]), ones)
        m_sc[...] = jnp.full(m_sc.shape, -jnp.inf, F32)
        l_sc[...] = jnp.zeros_like(l_sc)
        acc_sc[...] = jnp.zeros_like(acc_sc)
        causal = lax.broadcasted_iota(jnp.int32, (tb, tb), 1) <= lax.broadcasted_iota(jnp.int32, (tb, tb), 0)

        def step(j, masked):
            c0 = pl.multiple_of(j * tb, tb)
            for p in range(N_PAIRS):
                vb = v_ref[pl.ds(c0, tb), p * LANES : (p + 1) * LANES]
                pv, al = [], []
                for hh in range(2):
                    h = 2 * p + hh
                    s = _mm_nt(qa_sc[h], ka_sc[h, pl.ds(c0, tb), :])
                    if masked:
                        s = jnp.where(causal, s, -jnp.inf)
                    m_old = m_sc[h]
                    m_new = jnp.maximum(m_old, jnp.max(s, axis=1, keepdims=True))
                    alpha = jnp.exp(m_old - m_new)
                    pe = jnp.exp(s - jnp.concatenate([m_new] * (tb // LANES), axis=1))
                    l_sc[h] = alpha * l_sc[h] + jnp.sum(pe, axis=1, keepdims=True)
                    m_sc[h] = m_new
                    pv.append(_mm(pe.astype(BF16), vb))
                    al.append(alpha)
                acc_sc[p] = jnp.where(low, al[0], al[1]) * acc_sc[p] + jnp.where(low, pv[0], pv[1])

        def loop_body(j, carry):
            step(j, False)
            return carry

        lax.fori_loop(0, i, loop_body, 0)
        step(i, True)
        st = jnp.zeros((tb, LANES), F32)
        for p in range(N_PAIRS):
            lp = jnp.where(low, l_sc[2 * p], l_sc[2 * p + 1])
            o_ref[:, p * LANES : (p + 1) * LANES] = (acc_sc[p] / lp).astype(BF16)
            for h in (2 * p, 2 * p + 1):
                st = jnp.where(lane == h, m_sc[h] + jnp.log(l_sc[h]), st)
        st_ref[...] = st

    return pl.pallas_call(
        body,
        name="attn_fwd",
        grid=(n_seq, nq),
        in_specs=[
            pl.BlockSpec((tb, ATTN_WIDTH), lambda s, i: (s * nq + i, 0)),
            pl.BlockSpec((S, ATTN_WIDTH), lambda s, i: (s, 1)),
            pl.BlockSpec((S, ATTN_WIDTH), lambda s, i: (s, 2)),
            pl.BlockSpec((S, LANES), lambda s, i: (s, 0)),
        ],
        out_specs=[
            pl.BlockSpec((tb, ATTN_WIDTH), lambda s, i: (s * nq + i, 0)),
            pl.BlockSpec((tb, LANES), lambda s, i: (s * nq + i, 0)),
        ],
        out_shape=[jax.ShapeDtypeStruct((T, ATTN_WIDTH), BF16), jax.ShapeDtypeStruct((T, LANES), F32)],
        scratch_shapes=[
            pltpu.VMEM((N_HEADS, tb, LANES), BF16),
            pltpu.VMEM((N_HEADS, S, LANES), BF16),
            pltpu.VMEM((N_HEADS, tb, LANES), F32),
            pltpu.VMEM((N_HEADS, tb, LANES), F32),
            pltpu.VMEM((N_PAIRS, tb, LANES), F32),
        ],
        compiler_params=_params(("parallel", "arbitrary")),
    )(qkv, qkv, qkv, fcol)


def _mix_out(a, pool_y, gates, x, w_ao, w_out):
    T = x.shape[0]
    tm = ROW_TILE

    def body(a_ref, py_ref, gt_ref, x_ref, wao_ref, wout_ref, mg_ref, x1_ref, ay_ref):
        ay = _mm(a_ref[...], wao_ref[...])
        ay_ref[...] = ay
        sp = _sigmoid(gt_ref[:, :D_MODEL])
        sa = _sigmoid(gt_ref[:, D_MODEL:])
        mb = (sp * py_ref[...] + sa * ay).astype(BF16)
        mg_ref[...] = mb
        x1_ref[...] = x_ref[...] + _mm(mb, wout_ref[...])

    row = lambda n: pl.BlockSpec((tm, n), lambda i: (i, 0))
    return pl.pallas_call(
        body,
        name="mix_out",
        grid=(T // tm,),
        in_specs=[row(ATTN_WIDTH), row(D_MODEL), row(2 * D_MODEL), row(D_MODEL), _const_spec(w_ao.shape), _const_spec(w_out.shape)],
        out_specs=[row(D_MODEL), row(D_MODEL), row(D_MODEL)],
        out_shape=[jax.ShapeDtypeStruct((T, D_MODEL), BF16), jax.ShapeDtypeStruct((T, D_MODEL), F32), jax.ShapeDtypeStruct((T, D_MODEL), F32)],
        compiler_params=_params(("parallel",)),
    )(a, pool_y, gates, x, w_ao, w_out)


def _ffn_fwd(x1, g2, gf, tgt, w_gate, w_up, w_down):
    T = x1.shape[0]
    tm = min(T, FF_ROW_TILE)
    nt = T // tm
    nc = D_FF // FF_CHUNK

    def body(x1_ref, g2_ref, gf_ref, tg_ref, wg_ref, wu_ref, wd_ref, h2_ref, gate_ref, up_ref, act_ref, dx2_ref, loss_ref, dgf_ref):
        x1v = x1_ref[...]
        h2, _, _ = _rms_fwd(x1v, g2_ref[...])
        h2b = h2.astype(BF16)
        h2_ref[...] = h2b
        acc = x1v
        for c in range(nc):
            sl = slice(c * FF_CHUNK, (c + 1) * FF_CHUNK)
            gate = _mm(h2b, wg_ref[:, sl])
            up = _mm(h2b, wu_ref[:, sl])
            gate_ref[:, sl] = gate.astype(BF16)
            up_ref[:, sl] = up.astype(BF16)
            act = (gate * _sigmoid(gate) * up).astype(BF16)
            act_ref[:, sl] = act
            acc = acc + _mm(act, wd_ref[sl, :])
        gfv = gf_ref[...]
        y, xh, r = _rms_fwd(acc, gfv)
        err = y - tg_ref[...]
        part = 0.5 * jnp.sum(jnp.mean(err * err, axis=-1, keepdims=True), axis=0, keepdims=True)
        dx2, dgrow = _rms_bwd(err * (1.0 / D_MODEL), xh, r, gfv)
        dx2_ref[...] = dx2

        @pl.when(pl.program_id(0) == 0)
        def _():
            dgf_ref[...] = jnp.zeros_like(dgf_ref)
            loss_ref[...] = jnp.zeros_like(loss_ref)

        dgf_ref[...] += jnp.sum(dgrow, axis=0, keepdims=True)
        loss_ref[...] += jnp.broadcast_to(part, loss_ref.shape)

    row = lambda n: pl.BlockSpec((tm, n), lambda i: (i, 0))
    return pl.pallas_call(
        body,
        name="ffn_fwd",
        grid=(nt,),
        in_specs=[
            row(D_MODEL), _const_spec((1, D_MODEL)), _const_spec((1, D_MODEL)), row(D_MODEL),
            _const_spec(w_gate.shape), _const_spec(w_up.shape), _const_spec(w_down.shape),
        ],
        out_specs=[
            row(D_MODEL), row(D_FF), row(D_FF), row(D_FF), row(D_MODEL),
            pl.BlockSpec((8, LANES), lambda i: (0, 0)),
            pl.BlockSpec((1, D_MODEL), lambda i: (0, 0)),
        ],
        out_shape=[
            jax.ShapeDtypeStruct((T, D_MODEL), BF16),
            jax.ShapeDtypeStruct((T, D_FF), BF16),
            jax.ShapeDtypeStruct((T, D_FF), BF16),
            jax.ShapeDtypeStruct((T, D_FF), BF16),
            jax.ShapeDtypeStruct((T, D_MODEL), F32),
            jax.ShapeDtypeStruct((8, LANES), F32),
            jax.ShapeDtypeStruct((1, D_MODEL), F32),
        ],
        compiler_params=_params(("arbitrary",)),
    )(x1, g2, gf, tgt, w_gate, w_up, w_down)


def _ffn_bwd(dx2, gate, up, x1, g2, w_gate, w_up, w_down):
    T = x1.shape[0]
    tm = min(T, FF_ROW_TILE)
    nc = D_FF // FF_CHUNK

    def body(dx2_ref, gate_ref, up_ref, x1_ref, g2_ref, wg_ref, wu_ref, wd_ref, dgate_ref, dup_ref, dx1_ref, dg2_ref):
        dx2v = dx2_ref[...]
        dx2b = dx2v.astype(BF16)
        dh2 = jnp.zeros((tm, D_MODEL), F32)
        for c in range(nc):
            sl = slice(c * FF_CHUNK, (c + 1) * FF_CHUNK)
            dact = _mm_nt(dx2b, wd_ref[sl, :])
            gate = gate_ref[:, sl].astype(F32)
            sg = _sigmoid(gate)
            silu = gate * sg
            dgate = (dact * up_ref[:, sl].astype(F32) * (sg * (1.0 + gate * (1.0 - sg)))).astype(BF16)
            dup = (dact * silu).astype(BF16)
            dgate_ref[:, sl] = dgate
            dup_ref[:, sl] = dup
            dh2 = dh2 + _mm_nt(dgate, wg_ref[:, sl]) + _mm_nt(dup, wu_ref[:, sl])
        g2v = g2_ref[...]
        _, xh, r = _rms_fwd(x1_ref[...], g2v)
        dxn, dgrow = _rms_bwd(dh2, xh, r, g2v)
        dx1_ref[...] = dx2v + dxn

        @pl.when(pl.program_id(0) == 0)
        def _():
            dg2_ref[...] = jnp.zeros_like(dg2_ref)

        dg2_ref[...] += jnp.sum(dgrow, axis=0, keepdims=True)

    row = lambda n: pl.BlockSpec((tm, n), lambda i: (i, 0))
    return pl.pallas_call(
        body,
        name="ffn_bwd",
        grid=(T // tm,),
        in_specs=[
            row(D_MODEL), row(D_FF), row(D_FF), row(D_MODEL), _const_spec((1, D_MODEL)),
            _const_spec(w_gate.shape), _const_spec(w_up.shape), _const_spec(w_down.shape),
        ],
        out_specs=[row(D_FF), row(D_FF), row(D_MODEL), pl.BlockSpec((1, D_MODEL), lambda i: (0, 0))],
        out_shape=[
            jax.ShapeDtypeStruct((T, D_FF), BF16),
            jax.ShapeDtypeStruct((T, D_FF), BF16),
            jax.ShapeDtypeStruct((T, D_MODEL), F32),
            jax.ShapeDtypeStruct((1, D_MODEL), F32),
        ],
        compiler_params=_params(("arbitrary",), VMEM_LIMIT_MAX),
    )(dx2, gate, up, x1, g2, w_gate, w_up, w_down)


def _mix_bwd(dx1, gates, pool_y, attn_y, p2, scale, w_out, w_ao, w_po, token):
    T = dx1.shape[0]
    tm = ROW_TILE

    def body(dx1_ref, gt_ref, py_ref, ay_ref, p2_ref, sc_ref, wout_ref, wao_ref, wpo_ref, token_ref, dgt_ref, dpy_ref, day_ref, da_ref, dp2_ref, dsc_ref):
        dm = _mm_nt(dx1_ref[...].astype(BF16), wout_ref[...])
        sp = _sigmoid(gt_ref[:, :D_MODEL])
        sa = _sigmoid(gt_ref[:, D_MODEL:])
        dgt_ref[:, :D_MODEL] = (dm * py_ref[...] * (sp * (1.0 - sp))).astype(BF16)
        dgt_ref[:, D_MODEL:] = (dm * ay_ref[...] * (sa * (1.0 - sa))).astype(BF16)
        dpy = (dm * sp).astype(BF16)
        day = (dm * sa).astype(BF16)
        dpy_ref[...] = dpy
        day_ref[...] = day
        da_ref[...] = _mm_nt(day, wao_ref[...]).astype(BF16)
        dp3 = _mm_nt(dpy, wpo_ref[...])
        dp2_ref[...] = (dp3 * sc_ref[...]).astype(BF16)

        @pl.when(pl.program_id(0) == 0)
        def _():
            dsc_ref[...] = jnp.zeros_like(dsc_ref)

        dsc_ref[...] += jnp.sum(dp3 * p2_ref[...], axis=0, keepdims=True)

    row = lambda n: pl.BlockSpec((tm, n), lambda i: (i, 0))
    return pl.pallas_call(
        body,
        name="mix_bwd",
        grid=(T // tm,),
        in_specs=[
            row(D_MODEL), row(2 * D_MODEL), row(D_MODEL), row(D_MODEL), row(POOL_WIDTH), _const_spec((1, POOL_WIDTH)),
            _const_spec(w_out.shape), _const_spec(w_ao.shape), _const_spec(w_po.shape), _HBM,
        ],
        out_specs=[row(2 * D_MODEL), row(D_MODEL), row(D_MODEL), row(ATTN_WIDTH), row(POOL_WIDTH), pl.BlockSpec((1, POOL_WIDTH), lambda i: (0, 0))],
        out_shape=[
            jax.ShapeDtypeStruct((T, 2 * D_MODEL), BF16),
            jax.ShapeDtypeStruct((T, D_MODEL), BF16),
            jax.ShapeDtypeStruct((T, D_MODEL), BF16),
            jax.ShapeDtypeStruct((T, ATTN_WIDTH), BF16),
            jax.ShapeDtypeStruct((T, POOL_WIDTH), BF16),
            jax.ShapeDtypeStruct((1, POOL_WIDTH), F32),
        ],
        compiler_params=_params(("arbitrary",)),
    )(dx1, gates, pool_y, attn_y, p2, scale, w_out, w_ao, w_po, token)


def _pool_bwd(dp2, pm, mix_b, n_seq, S):
    T = n_seq * S

    def body(dp2_ref, pm_ref, mix_ref, du_ref, dmix_ref):
        g = pl.program_id(0)
        dp2v = dp2_ref[...]
        dpm = _mm_nt(dp2v, mix_ref[...])
        row = lax.broadcasted_iota(jnp.int32, dpm.shape, 0)
        w = _window_pick(g, 2.0, 4.0, 8.0, 16.0)
        e = dpm / jnp.minimum((row + 1).astype(F32), w)

        def ahead(a, k):
            return jnp.where(row < S - k, pltpu.roll(a, S - k, 0), 0.0)

        r2 = e + ahead(e, 1)
        r4 = r2 + ahead(r2, 2)
        r8 = r4 + ahead(r4, 4)
        r16 = r8 + ahead(r8, 8)
        du_ref[...] = (_window_pick(g, r2, r4, r8, r16) - dpm).astype(BF16)

        @pl.when(pl.program_id(1) == 0)
        def _():
            dmix_ref[...] = jnp.zeros_like(dmix_ref)

        dmix_ref[...] += _mm_tn(pm_ref[...], dp2v)

    grp = pl.BlockSpec((S, GROUP_DIM), lambda g, s: (s, g))
    mixs = pl.BlockSpec((None, GROUP_DIM, GROUP_DIM), lambda g, s: (g, 0, 0))
    return pl.pallas_call(
        body,
        name="pool_bwd",
        grid=(len(POOL_WINDOWS), n_seq),
        in_specs=[grp, grp, mixs],
        out_specs=[grp, mixs],
        out_shape=[jax.ShapeDtypeStruct((T, POOL_WIDTH), BF16), jax.ShapeDtypeStruct((len(POOL_WINDOWS), GROUP_DIM, GROUP_DIM), F32)],
        compiler_params=_params(("parallel", "arbitrary")),
    )(dp2, pm, mix_b)


def _attn_bwd(qkv, da, a, fcol, lse, n_seq, S):
    T = n_seq * S
    tb = ATTN_BLOCK
    nb = S // tb
    scale = HEAD_DIM ** -0.5

    def body(q_ref, k_ref, v_ref, do_ref, o_ref, fc_ref, st_ref, dq_ref, dk_ref, dv_ref, dfk_ref, dfq_ref,
             qa_sc, doa_sc, dq_acc, ka_sc, va_sc, dk_sc, dv_sc):
        j = pl.program_id(1)
        lane = lax.broadcasted_iota(jnp.int32, (1, LANES), 1)
        low = lane < HEAD_DIM
        ones = (1.0, 1.0, 1.0)
        zeros = (0.0, 0.0, 0.0)

        @pl.when(j == 0)
        def _():
            dq_acc[...] = jnp.zeros_like(dq_acc)

            def rows_q(i, carry):
                r0 = pl.multiple_of(i * tb, tb)
                for h in range(N_HEADS):
                    pair = slice((h // 2) * LANES, (h // 2 + 1) * LANES)
                    qp = q_ref[pl.ds(r0, tb), pair]
                    dop = do_ref[pl.ds(r0, tb), pair]
                    prod = dop.astype(F32) * o_ref[pl.ds(r0, tb), pair].astype(F32)
                    head = (lane >= HEAD_DIM * (h % 2)) & (lane < HEAD_DIM * (h % 2 + 1))
                    delta = jnp.sum(jnp.where(head, prod, 0.0), axis=1, keepdims=True)
                    cq = fc_ref[pl.ds(r0, tb), h : h + 1] - st_ref[pl.ds(r0, tb), h : h + 1]
                    qa_sc[h, pl.ds(r0, tb), :] = _augment(qp, h % 2, _split3(cq), ones)
                    doa_sc[h, pl.ds(r0, tb), :] = _augment(dop, h % 2, _split3(-delta), zeros)
                return carry

            lax.fori_loop(0, nb, rows_q, 0)

        c0 = pl.multiple_of(j * tb, tb)
        for h in range(N_HEADS):
            pair = slice((h // 2) * LANES, (h // 2 + 1) * LANES)
            kp = k_ref[:, pair] * scale
            ka_sc[h] = _augment(kp, h % 2, ones, _split3(-fc_ref[pl.ds(c0, tb), h : h + 1]))
            va_sc[h] = _augment(v_ref[:, pair], h % 2, ones, zeros)
        dk_sc[...] = jnp.zeros_like(dk_sc)
        dv_sc[...] = jnp.zeros_like(dv_sc)
        causal = lax.broadcasted_iota(jnp.int32, (tb, tb), 1) <= lax.broadcasted_iota(jnp.int32, (tb, tb), 0)

        def step(i, masked):
            r0 = pl.multiple_of(i * tb, tb)
            for h in range(N_HEADS):
                dob = do_ref[pl.ds(r0, tb), (h // 2) * LANES : (h // 2 + 1) * LANES]
                qa = qa_sc[h, pl.ds(r0, tb), :]
                s = _mm_nt(qa, ka_sc[h])
                if masked:
                    s = jnp.where(causal, s, -jnp.inf)
                pr = jnp.exp(s)
                dv_sc[h] += _mm_tn(pr.astype(BF16), dob)
                dsb = (pr * _mm_nt(doa_sc[h, pl.ds(r0, tb), :], va_sc[h])).astype(BF16)
                dk_sc[h] += _mm_tn(dsb, qa)
                dq_acc[h, pl.ds(r0, tb), :] += _mm(dsb, ka_sc[h])

        step(j, True)

        def loop_body(i, carry):
            step(i, False)
            return carry

        lax.fori_loop(j + 1, nb, loop_body, 0)
        dfk = jnp.zeros((tb, LANES), F32)
        for p in range(N_PAIRS):
            dk_ref[:, p * LANES : (p + 1) * LANES] = (jnp.where(low, dk_sc[2 * p], dk_sc[2 * p + 1]) * scale).astype(BF16)
            dv_ref[:, p * LANES : (p + 1) * LANES] = jnp.where(low, dv_sc[2 * p], dv_sc[2 * p + 1]).astype(BF16)
            for hh in range(2):
                b = HEAD_DIM * (1 - hh) + 3
                dfk = jnp.where(lane == 2 * p + hh, -dk_sc[2 * p + hh][:, b : b + 1], dfk)
        dfk_ref[...] = dfk

        @pl.when(j == nb - 1)
        def _():
            def rows_dq(i, carry):
                r0 = pl.multiple_of(i * tb, tb)
                dfq = jnp.zeros((tb, LANES), F32)
                for p in range(N_PAIRS):
                    parts = [dq_acc[2 * p + hh, pl.ds(r0, tb), :] for hh in range(2)]
                    dq_ref[pl.ds(r0, tb), p * LANES : (p + 1) * LANES] = jnp.where(low, parts[0], parts[1]).astype(BF16)
                    for hh in range(2):
                        b = HEAD_DIM * (1 - hh)
                        dfq = jnp.where(lane == 2 * p + hh, parts[hh][:, b : b + 1], dfq)
                dfq_ref[pl.ds(r0, tb), :] = dfq
                return carry

            lax.fori_loop(0, nb, rows_dq, 0)

    seq = lambda w, col: pl.BlockSpec((S, w), lambda s, j: (s, col))
    blk = lambda w, col: pl.BlockSpec((tb, w), lambda s, j: (s * nb + j, col))
    return pl.pallas_call(
        body,
        name="attn_bwd",
        grid=(n_seq, nb),
        in_specs=[seq(ATTN_WIDTH, 0), blk(ATTN_WIDTH, 1), blk(ATTN_WIDTH, 2), seq(ATTN_WIDTH, 0), seq(ATTN_WIDTH, 0), seq(LANES, 0), seq(LANES, 0)],
        out_specs=[seq(ATTN_WIDTH, 0), blk(ATTN_WIDTH, 0), blk(ATTN_WIDTH, 0), blk(LANES, 0), seq(LANES, 0)],
        out_shape=[
            jax.ShapeDtypeStruct((T, ATTN_WIDTH), BF16),
            jax.ShapeDtypeStruct((T, ATTN_WIDTH), BF16),
            jax.ShapeDtypeStruct((T, ATTN_WIDTH), BF16),
            jax.ShapeDtypeStruct((T, LANES), F32),
            jax.ShapeDtypeStruct((T, LANES), F32),
        ],
        scratch_shapes=[
            pltpu.VMEM((N_HEADS, S, LANES), BF16),
            pltpu.VMEM((N_HEADS, S, LANES), BF16),
            pltpu.VMEM((N_HEADS, S, LANES), F32),
            pltpu.VMEM((N_HEADS, tb, LANES), BF16),
            pltpu.VMEM((N_HEADS, tb, LANES), BF16),
            pltpu.VMEM((N_HEADS, tb, LANES), F32),
            pltpu.VMEM((N_HEADS, tb, LANES), F32),
        ],
        compiler_params=_params(("parallel", "arbitrary")),
    )(qkv, qkv, qkv, da, a, fcol, lse)


def _forget_bwd(dfk, dfq, fl, b_pad, n_seq, S):
    def body(df_ref, dfq_ref, fl_ref, b_ref, dfl_ref, db_ref):
        t = (df_ref[...] + dfq_ref[...]).T
        lane = lax.broadcasted_iota(jnp.int32, t.shape, 1)
        k = 1
        while k < S:
            t = t + jnp.where(lane < S - k, pltpu.roll(t, S - k, 1), 0.0)
            k *= 2
        dfl = t.T * _sigmoid(-(fl_ref[...] + b_ref[...]))
        dfl_ref[...] = dfl.astype(BF16)

        @pl.when(pl.program_id(0) == 0)
        def _():
            db_ref[...] = jnp.zeros_like(db_ref)

        db_ref[...] += jnp.sum(dfl, axis=0, keepdims=True)

    return pl.pallas_call(
        body,
        name="forget_bwd",
        grid=(n_seq,),
        in_specs=[
            pl.BlockSpec((S, LANES), lambda s: (s, 0)),
            pl.BlockSpec((S, LANES), lambda s: (s, 0)),
            pl.BlockSpec((S, FL_PAD), lambda s: (s, 0)),
            _const_spec((1, FL_PAD)),
        ],
        out_specs=[pl.BlockSpec((S, FL_PAD), lambda s: (s, 0)), pl.BlockSpec((1, FL_PAD), lambda s: (0, 0))],
        out_shape=[jax.ShapeDtypeStruct((n_seq * S, FL_PAD), BF16), jax.ShapeDtypeStruct((1, FL_PAD), F32)],
        compiler_params=_params(("arbitrary",)),
    )(dfk, dfq, fl, b_pad)


def _in_proj_bwd(du, dq, dk, dv, dfl, dgates, x, dx1, g1, w_uqkv, w_fl, w_g):
    T = x.shape[0]
    tm = ROW_TILE

    def body(du_ref, dq_ref, dk_ref, dv_ref, dfl_ref, dgt_ref, x_ref, dx1_ref, g_ref, wa_ref, wf_ref, wg_ref, dx_ref, dg_ref):
        dh = _mm_nt(dgt_ref[...], wg_ref[...]) + _mm_nt(dfl_ref[...], wf_ref[...])
        for n, ref in enumerate((du_ref, dq_ref, dk_ref, dv_ref)):
            dh = dh + _mm_nt(ref[...], wa_ref[:, n * 512 : (n + 1) * 512])
        gv = g_ref[...]
        _, xh, r = _rms_fwd(x_ref[...], gv)
        dxn, dgrow = _rms_bwd(dh, xh, r, gv)
        dx_ref[...] = dx1_ref[...] + dxn

        @pl.when(pl.program_id(0) == 0)
        def _():
            dg_ref[...] = jnp.zeros_like(dg_ref)

        dg_ref[...] += jnp.sum(dgrow, axis=0, keepdims=True)

    row = lambda n: pl.BlockSpec((tm, n), lambda i: (i, 0))
    return pl.pallas_call(
        body,
        name="in_proj_bwd",
        grid=(T // tm,),
        in_specs=[
            row(512), row(512), row(512), row(512), row(FL_PAD), row(2 * D_MODEL), row(D_MODEL), row(D_MODEL), _const_spec((1, D_MODEL)),
            _const_spec(w_uqkv.shape), _const_spec(w_fl.shape), _const_spec(w_g.shape),
        ],
        out_specs=[row(D_MODEL), pl.BlockSpec((1, D_MODEL), lambda i: (0, 0))],
        out_shape=[jax.ShapeDtypeStruct((T, D_MODEL), F32), jax.ShapeDtypeStruct((1, D_MODEL), F32)],
        compiler_params=_params(("arbitrary",)),
    )(du, dq, dk, dv, dfl, dgates, x, dx1, g1, w_uqkv, w_fl, w_g)


def _pick_block(n):
    for b in (512, 1408, 256, 128):
        if n % b == 0:
            return b
    raise ValueError(n)


def _matmul_tn(a, b, name, row_sharded=False):
    T, K = a.shape
    N = b.shape[1]
    bt, bk, bn = min(T, DW_TOKENS), _pick_block(K), _pick_block(N)
    nt = T // bt
    r = K // N_DEV
    assert not row_sharded or bk == 4 * r

    def body(a_ref, b_ref, o_ref, acc):
        @pl.when(pl.program_id(2) == 0)
        def _():
            acc[...] = jnp.zeros_like(acc)

        acc[...] += _mm_tn(a_ref[...].astype(BF16), b_ref[...].astype(BF16))

        @pl.when(pl.program_id(2) == nt - 1)
        def _():
            if row_sharded:
                for chip in range(2):
                    for core in range(2):
                        d = 2 * chip + core
                        o_ref[core, chip] = acc[d * r : (d + 1) * r, :].astype(BF16)
            else:
                o_ref[...] = acc[...].astype(BF16)

    if row_sharded:
        out_spec = pl.BlockSpec((2, 2, r, bn), lambda k, n, t: (0, k, 0, n))
        out_shape = jax.ShapeDtypeStruct((2, 4, r, N), BF16)
    else:
        out_spec = pl.BlockSpec((bk, bn), lambda k, n, t: (k, n))
        out_shape = jax.ShapeDtypeStruct((K, N), BF16)
    return pl.pallas_call(
        body,
        name=name,
        grid=(K // bk, N // bn, nt),
        in_specs=[pl.BlockSpec((bt, bk), lambda k, n, t: (t, k)), pl.BlockSpec((bt, bn), lambda k, n, t: (t, n))],
        out_specs=out_spec,
        out_shape=out_shape,
        scratch_shapes=[pltpu.VMEM((bk, bn), F32)],
        compiler_params=_params(("parallel", "parallel", "arbitrary")),
    )(a, b)


def _position():
    return lax.axis_index("x"), lax.axis_index("y"), lax.axis_index("c")


_HBM = pl.BlockSpec(memory_space=pl.ANY)


def _all_gather(blocks, name):
    n = len(blocks)

    def body(*refs):
        xs, outs = refs[:n], refs[n : 2 * n]
        send_sems, recv_sems, local_sems = refs[2 * n :]
        x, y, c = _position()
        me, sibling = (x, y, c), (x, y, 1 - c)
        chips = [(1 - x, y), (x, 1 - y), (1 - x, 1 - y)]

        def rows(a, px, py, pc):
            return outs[a].at[4 * px + 2 * py + pc]

        def copy(a, k, blk, to, src=None):
            return pltpu.make_async_remote_copy(
                src_ref=rows(a, *blk) if src is None else src, dst_ref=rows(a, *blk),
                send_sem=send_sems.at[7 * a + k], recv_sem=recv_sems.at[7 * a + k], device_id=to, device_id_type=MESH,
            )

        mine = [pltpu.make_async_copy(xs[a], rows(a, *me), local_sems.at[a]) for a in range(n)]
        for cp in mine:
            cp.start()
        first = []
        for a in range(n):
            first.append(copy(a, 0, me, sibling, src=xs[a]))
            first += [copy(a, 1 + j, me, (*chip, c), src=xs[a]) for j, chip in enumerate(chips)]
        for cp in first:
            cp.start()
        passed = []
        for j, chip in enumerate(chips):
            for a in range(n):
                copy(a, 1 + j, (*chip, c), me).wait_recv()
                passed.append(copy(a, 4 + j, (*chip, c), sibling))
                passed[-1].start()
        for a in range(n):
            copy(a, 0, sibling, me).wait_recv()
        for j, chip in enumerate(chips):
            for a in range(n):
                copy(a, 4 + j, (*chip, 1 - c), me).wait_recv()
        for cp in first + passed:
            cp.wait_send()
        for cp in mine:
            cp.wait()

    return pl.pallas_call(
        body,
        name=name,
        out_shape=[jax.ShapeDtypeStruct((N_DEV, *b.shape), b.dtype) for b in blocks],
        in_specs=[_HBM] * n,
        out_specs=[_HBM] * n,
        scratch_shapes=[pltpu.SemaphoreType.DMA((7 * n,)), pltpu.SemaphoreType.DMA((7 * n,)), pltpu.SemaphoreType.DMA((n,))],
    )(*blocks)


_SEM = pl.BlockSpec(memory_space=pltpu.SEMAPHORE)
_HBM_ONLY = pl.BlockSpec(memory_space=pltpu.HBM)
_SIDE_EFFECT = pltpu.SideEffectType.DATAFLOW_SIDE_EFFECTING


def _peer(x, y, c, k):
    return (1 - x if k & 4 else x, 1 - y if k & 2 else y, 1 - c if k & 1 else c)


def _exchange_copies(src_refs, land_refs, send_sems, recv_sems, scatter, receive_side):
    x, y, c = _position()
    me = 4 * x + 2 * y + c
    cps = []
    for k in range(1, N_DEV):
        px, py, pc = _peer(x, y, c, k)
        peer = 4 * px + 2 * py + pc
        for a, (src, land) in enumerate(zip(src_refs, land_refs)):
            cps.append(pltpu.make_async_remote_copy(
                src_ref=src.at[peer] if scatter else src, dst_ref=land.at[peer if receive_side else me],
                send_sem=send_sems.at[7 * a + k - 1], recv_sem=recv_sems.at[7 * a + k - 1],
                device_id=(px, py, pc), device_id_type=MESH,
            ))
    return cps


def _exchange_start(srcs, after, name, scatter):
    n = len(srcs)
    lands = [jax.ShapeDtypeStruct((N_DEV, *s.shape[-2:]), s.dtype) for s in srcs]

    def body(*refs):
        src_refs, land_refs = refs[1 : 1 + n], refs[1 + n : 1 + 2 * n]
        send_sems, recv_sems = refs[1 + 2 * n], refs[2 + 2 * n]
        token = refs[-1]
        for cp in _exchange_copies(src_refs, land_refs, send_sems, recv_sems, scatter, receive_side=False):
            cp.start()
        token[...] = jnp.zeros_like(token)

    hbm = lambda t: pltpu.with_memory_space_constraint(t, pltpu.HBM)
    out = pl.pallas_call(
        body,
        name=name,
        out_shape=(
            pltpu.SemaphoreType.DMA((7 * n,)), pltpu.SemaphoreType.DMA((7 * n,)),
            *[pltpu.HBM(s.shape, s.dtype) for s in srcs], *[pltpu.HBM(l.shape, l.dtype) for l in lands],
            jax.ShapeDtypeStruct((8, LANES), F32),
        ),
        in_specs=(_HBM, *[_HBM_ONLY] * (2 * n)),
        out_specs=(_SEM, _SEM, *[_HBM_ONLY] * (2 * n), pl.BlockSpec(memory_space=pltpu.VMEM)),
        input_output_aliases={1 + i: 2 + i for i in range(2 * n)},
        compiler_params=pltpu.CompilerParams(has_side_effects=_SIDE_EFFECT),
    )(after, *[hbm(s) for s in srcs], *[hbm(lax.empty(l.shape, l.dtype)) for l in lands])
    return out[0], out[1], out[2 : 2 + n], out[2 + n : 2 + 2 * n], out[-1]


def _exchange_wait(send_sems, recv_sems, srcs, lands, after, name, scatter):
    n = len(srcs)

    def body(*refs):
        src_refs, land_refs = refs[:n], refs[n : 2 * n]
        for cp in _exchange_copies(src_refs, land_refs, refs[2 * n], refs[2 * n + 1], scatter, receive_side=True):
            cp.wait_send()
            cp.wait_recv()

    out = pl.pallas_call(
        body,
        name=name,
        out_shape=(*[pltpu.HBM(s.shape, s.dtype) for s in srcs], *[pltpu.HBM(l.shape, l.dtype) for l in lands]),
        in_specs=(*[_HBM_ONLY] * (2 * n), _SEM, _SEM, _HBM),
        out_specs=tuple([_HBM_ONLY] * (2 * n)),
        input_output_aliases={i: i for i in range(2 * n)},
        compiler_params=pltpu.CompilerParams(has_side_effects=_SIDE_EFFECT),
    )(*srcs, *lands, send_sems, recv_sems, after)
    return out[:n], out[n:]


def _sibling_exchange(sends):
    n = len(sends)

    def body(*refs):
        srcs, dsts = refs[:n], refs[n : 2 * n]
        send_sems, recv_sems = refs[2 * n :]
        x, y, c = _position()
        cps = [
            pltpu.make_async_remote_copy(
                src_ref=srcs[a].at[1 - c], dst_ref=dsts[a], send_sem=send_sems.at[a], recv_sem=recv_sems.at[a],
                device_id=(x, y, 1 - c), device_id_type=MESH,
            )
            for a in range(n)
        ]
        for cp in cps:
            cp.start()
        for cp in cps:
            cp.wait()

    return pl.pallas_call(
        body,
        name="rs_sibling",
        out_shape=[jax.ShapeDtypeStruct(s.shape[1:], s.dtype) for s in sends],
        in_specs=[_HBM] * n,
        out_specs=[_HBM] * n,
        scratch_shapes=[pltpu.SemaphoreType.DMA((n,)), pltpu.SemaphoreType.DMA((n,))],
    )(*sends)


def _rows_tile(r):
    return ROW_TILE if r % ROW_TILE == 0 else r


def _pair_sum(send, got, core, name):
    _, _, r, c = send.shape
    br = _rows_tile(r)

    def body(core_ref, a_ref, b_ref, o_ref):
        o_ref[...] = (a_ref[...].astype(F32) + b_ref[...].astype(F32)).astype(o_ref.dtype)

    return pl.pallas_call(
        body,
        name=name,
        grid_spec=pltpu.PrefetchScalarGridSpec(
            num_scalar_prefetch=1,
            grid=(4, r // br),
            in_specs=[
                pl.BlockSpec((None, None, br, c), lambda n, i, core: (core[0], n, i, 0)),
                pl.BlockSpec((None, br, c), lambda n, i, core: (n, i, 0)),
            ],
            out_specs=pl.BlockSpec((None, br, c), lambda n, i, core: (n, i, 0)),
        ),
        out_shape=jax.ShapeDtypeStruct((4, r, c), send.dtype),
        compiler_params=_params(("parallel", "parallel")),
    )(core, send, got)


def _chip_exchange(pairs):
    n = len(pairs)

    def body(*refs):
        srcs, dsts = refs[:n], refs[n : 2 * n]
        send_sems, recv_sems = refs[2 * n :]
        x, y, c = _position()
        chips = [(1 - x, y), (x, 1 - y), (1 - x, 1 - y)]
        cps = [
            pltpu.make_async_remote_copy(
                src_ref=srcs[a].at[2 * cx + cy], dst_ref=dsts[a].at[j], send_sem=send_sems.at[3 * a + j], recv_sem=recv_sems.at[3 * a + j],
                device_id=(cx, cy, c), device_id_type=MESH,
            )
            for a in range(n)
            for j, (cx, cy) in enumerate(chips)
        ]
        for cp in cps:
            cp.start()
        for cp in cps:
            cp.wait()

    return pl.pallas_call(
        body,
        name="rs_chips",
        out_shape=[jax.ShapeDtypeStruct((3, *p.shape[1:]), p.dtype) for p in pairs],
        in_specs=[_HBM] * n,
        out_specs=[_HBM] * n,
        scratch_shapes=[pltpu.SemaphoreType.DMA((3 * n,)), pltpu.SemaphoreType.DMA((3 * n,))],
    )(*pairs)


def _adamw(w, g, m, v):
    m = ADAM_B1 * m + (1.0 - ADAM_B1) * g
    v = ADAM_B2 * v + (1.0 - ADAM_B2) * (g * g)
    m_hat = m / (1.0 - ADAM_B1 ** ADAM_STEP)
    v_hat = v / (1.0 - ADAM_B2 ** ADAM_STEP)
    delta = -ADAM_LR * (m_hat / (jnp.sqrt(v_hat) + ADAM_EPS) + ADAM_WD * w)
    return delta, m, v


def _shard_update(send, got, recv, w, m, v, pos, name):
    _, r, c = w.shape
    br = _rows_tile(r)

    def body(pos_ref, a_ref, b_ref, r_ref, w_ref, m_ref, v_ref, g_ref, d_ref, nm_ref, nv_ref):
        g = a_ref[...].astype(F32) + b_ref[...].astype(F32)
        for n in range(3):
            g = g + r_ref[n].astype(F32)
        g_ref[...] = g
        d_ref[...], nm_ref[...], nv_ref[...] = _adamw(w_ref[...], g, m_ref[...], v_ref[...])

    own = pl.BlockSpec((None, br, c), lambda i, pos: (0, i, 0))
    return pl.pallas_call(
        body,
        name=name,
        grid_spec=pltpu.PrefetchScalarGridSpec(
            num_scalar_prefetch=1,
            grid=(r // br,),
            in_specs=[
                pl.BlockSpec((None, None, br, c), lambda i, pos: (pos[0], pos[1], i, 0)),
                pl.BlockSpec((None, br, c), lambda i, pos: (pos[1], i, 0)),
                pl.BlockSpec((3, br, c), lambda i, pos: (0, i, 0)),
                own, own, own,
            ],
            out_specs=[own, own, own, own],
        ),
        out_shape=[jax.ShapeDtypeStruct((1, r, c), F32)] * 4,
        compiler_params=_params(("parallel",)),
    )(pos, send, got, recv, w, m, v)


def _shard_update_direct(parts, w, m, v, name):
    _, r, c = w.shape
    br = _rows_tile(r)

    def body(p_ref, w_ref, m_ref, v_ref, g_ref, d_ref, nm_ref, nv_ref):
        g = p_ref[0].astype(F32)
        for n in range(1, N_DEV):
            g = g + p_ref[n].astype(F32)
        g_ref[...] = g
        d_ref[...], nm_ref[...], nv_ref[...] = _adamw(w_ref[...], g, m_ref[...], v_ref[...])

    own = pl.BlockSpec((None, br, c), lambda i: (0, i, 0))
    return pl.pallas_call(
        body,
        name=name,
        grid=(r // br,),
        in_specs=[pl.BlockSpec((N_DEV, br, c), lambda i: (0, i, 0)), own, own, own],
        out_specs=[own, own, own, own],
        out_shape=[jax.ShapeDtypeStruct((1, r, c), F32)] * 4,
        compiler_params=_params(("parallel",)),
    )(parts, w, m, v)


def _small_update(parts, w, m, v):
    R = w.shape[0]

    def body(p_ref, w_ref, m_ref, v_ref, g_ref, d_ref, nm_ref, nv_ref):
        g = p_ref[0]
        for n in range(1, N_DEV):
            g = g + p_ref[n]
        g_ref[...] = g
        d_ref[...], nm_ref[...], nv_ref[...] = _adamw(w_ref[...], g, m_ref[...], v_ref[...])

    return pl.pallas_call(
        body,
        name="small_update",
        out_shape=[jax.ShapeDtypeStruct((R, LANES), F32)] * 4,
        compiler_params=pltpu.CompilerParams(vmem_limit_bytes=VMEM_LIMIT),
    )(parts, w, m, v)


_SHARD_AXIS = (1, 1, 1, 0, 1, 1, 0)


def _full_from_gathered(t, axis):
    if axis == 0:
        return t.reshape(N_DEV * t.shape[1], t.shape[2])
    return jnp.concatenate([t[d] for d in range(N_DEV)], axis=1)


def _chunks_from_cols(t):
    c = t.shape[1] // N_DEV
    return jnp.stack([t[:, d * c : (d + 1) * c] for d in range(N_DEV)])


def _send_from_cols(t):
    c = t.shape[1] // N_DEV
    return jnp.stack([jnp.stack([t[:, (2 * chip + core) * c : (2 * chip + core + 1) * c] for chip in range(4)]) for core in range(2)])


_SMALL = (("norm1_g", 8), ("norm2_g", 8), ("norm_f_g", 8), ("b_forget", 8), ("pool_scale", 8), ("pool_mix", 512))
_SMALL_ROWS = sum(r for _, r in _SMALL) + 8


def _pack_small(vals, loss_row):
    parts = []
    for (name, rows), t in zip(_SMALL, vals):
        f = t.astype(F32).reshape(-1)
        f = jnp.concatenate([f, jnp.zeros((rows * LANES - f.shape[0],), F32)]).reshape(rows, LANES)
        parts.append(f)
    parts.append(loss_row)
    return jnp.concatenate(parts, axis=0)


def _unpack_small(packed, shapes):
    out, off = [], 0
    for (name, rows), shape in zip(_SMALL, shapes):
        n = 1
        for s in shape:
            n *= s
        out.append(packed[off : off + rows].reshape(-1)[:n].reshape(shape))
        off += rows
    return out, packed[off, 0]


def _local_grads(x, tgt, g1, g2, gf, b_forget, pool_mix, pool_scale, w_in, w_po, w_ao, w_out, fwd_token, ffn_weights, ffn_grads_out):
    n_seq, S, _ = x.shape
    T = n_seq * S
    x2 = x.reshape(T, D_MODEL)
    tg2 = tgt.reshape(T, D_MODEL)
    w_uqkv = w_in[:, : POOL_WIDTH + 3 * ATTN_WIDTH]
    w_fl = jnp.concatenate([w_in[:, 2048 : 2048 + N_HEADS], jnp.zeros((D_MODEL, FL_PAD - N_HEADS), BF16)], axis=1)
    w_g = w_in[:, 2048 + N_HEADS :]
    b_pad = jnp.concatenate([b_forget.reshape(1, N_HEADS), jnp.zeros((1, FL_PAD - N_HEADS), F32)], axis=1)
    mix_b = pool_mix.reshape(len(POOL_WINDOWS), GROUP_DIM, GROUP_DIM).astype(BF16)
    scale = pool_scale.reshape(1, POOL_WIDTH)
    g1 = g1.reshape(1, D_MODEL)
    g2 = g2.reshape(1, D_MODEL)
    gf = gf.reshape(1, D_MODEL)

    h, u, qkv, fl, gates = _in_proj(x2, g1, w_uqkv, w_fl, w_g, fwd_token)
    fcol = _forget_fwd(fl, b_pad, n_seq, S)
    pm, p2, p3, pool_y = _pool_fwd(u, mix_b, scale, w_po, n_seq, S)
    a, lse = _attn_fwd(qkv, fcol, n_seq, S)
    merged, x1, attn_y = _mix_out(a, pool_y, gates, x2, w_ao, w_out)
    w_gate, w_up, w_down = ffn_weights(x1)
    h2, gate, up, act, dx2, loss_rows, dgf = _ffn_fwd(x1, g2, gf, tg2, w_gate, w_up, w_down)

    dgate, dup, dx1, dg2 = _ffn_bwd(dx2, gate, up, x1, g2, w_gate, w_up, w_down)
    bwd_token = ffn_grads_out(_matmul_tn(h2, dgate, "dw_ffn_gate"), _matmul_tn(h2, dup, "dw_ffn_up"), _matmul_tn(act, dx2, "dw_ffn_down"))
    dgates, dpy, day, da, dp2, dscale = _mix_bwd(dx1, gates, pool_y, attn_y, p2, scale, w_out, w_ao, w_po, bwd_token)
    du, dmix = _pool_bwd(dp2, pm, mix_b, n_seq, S)
    dq, dk, dv, dfk, dfq = _attn_bwd(qkv, da, a, fcol, lse, n_seq, S)
    dfl, db = _forget_bwd(dfk, dfq, fl, b_pad, n_seq, S)
    dx, dg1 = _in_proj_bwd(du, dq, dk, dv, dfl, dgates, x2, dx1, g1, w_uqkv, w_fl, w_g)

    d_w_in = jnp.concatenate(
        [
            _matmul_tn(h, du, "dw_u"), _matmul_tn(h, dq, "dw_q"), _matmul_tn(h, dk, "dw_k"), _matmul_tn(h, dv, "dw_v"),
            _matmul_tn(h, dfl, "dw_fl")[:, :N_HEADS], _matmul_tn(h, dgates, "dw_gates"),
        ],
        axis=1,
    )
    sends = [
        _send_from_cols(d_w_in),
        _send_from_cols(_matmul_tn(p3, dpy, "dw_pool_out")),
        _send_from_cols(_matmul_tn(a, day, "dw_attn_out")),
        _matmul_tn(merged, dx1, "dw_out", row_sharded=True),
    ]
    small = (dg1, dg2, dgf, db[:, :N_HEADS], dscale, dmix)
    return loss_rows, dx.reshape(n_seq, S, D_MODEL), sends, small


def kernel(x, norm1_g, w_in, b_forget, pool_mix, pool_scale, w_pool_out, w_attn_out, w_out, norm2_g, w_ffn_gate, w_ffn_up, w_ffn_down, norm_f_g, loss_target, m_norm1_g, m_w_in, m_b_forget, m_pool_mix, m_pool_scale, m_w_pool_out, m_w_attn_out, m_w_out, m_norm2_g, m_w_ffn_gate, m_w_ffn_up, m_w_ffn_down, m_norm_f_g, v_norm1_g, v_w_in, v_b_forget, v_pool_mix, v_pool_scale, v_w_pool_out, v_w_attn_out, v_w_out, v_norm2_g, v_w_ffn_gate, v_w_ffn_up, v_w_ffn_down, v_norm_f_g):
    names = ("w_in", "w_pool_out", "w_attn_out", "w_out", "w_ffn_gate", "w_ffn_up", "w_ffn_down")
    w_sh = (w_in, w_pool_out, w_attn_out, w_out, w_ffn_gate, w_ffn_up, w_ffn_down)
    m_sh = (m_w_in, m_w_pool_out, m_w_attn_out, m_w_out, m_w_ffn_gate, m_w_ffn_up, m_w_ffn_down)
    v_sh = (v_w_in, v_w_pool_out, v_w_attn_out, v_w_out, v_w_ffn_gate, v_w_ffn_up, v_w_ffn_down)

    cx, cy, cc = _position()
    me = 4 * cx + 2 * cy + cc
    n_mix = 4
    shards = [w[0].astype(BF16) for w in w_sh]

    gathered = _all_gather(shards[:n_mix], "mixer_weights_all_gather")
    whole = [_full_from_gathered(t, axis) for t, axis in zip(gathered, _SHARD_AXIS)]
    ffn_sems = _exchange_start(shards[n_mix:], gathered[0], "ffn_weights_gather_start", scatter=False)

    def with_own(lands, own):
        return [lax.dynamic_update_slice(l, o[None], (me, 0, 0)) for l, o in zip(lands, own)]

    def ffn_weights(after):
        send_sems, recv_sems, srcs, lands, _ = ffn_sems
        srcs, lands = _exchange_wait(send_sems, recv_sems, srcs, lands, after, "ffn_weights_gather_wait", scatter=False)
        return [_full_from_gathered(t, axis) for t, axis in zip(with_own(lands, srcs), _SHARD_AXIS[n_mix:])]

    rs_ffn = []

    def ffn_grads_out(d_gate, d_up, d_down):
        chunks = [_chunks_from_cols(d_gate), _chunks_from_cols(d_up), d_down.reshape(N_DEV, -1, d_down.shape[1])]
        rs_ffn.append(_exchange_start(chunks, jnp.zeros((8, LANES), F32), "ffn_grads_scatter_start", scatter=True))
        return rs_ffn[0][4]

    loss_rows, grad_x, sends, small = _local_grads(
        x, loss_target, norm1_g, norm2_g, norm_f_g, b_forget, pool_mix, pool_scale, *whole, ffn_sems[4], ffn_weights, ffn_grads_out)

    send_sems, recv_sems, srcs, lands, _ = rs_ffn[0]
    srcs, lands = _exchange_wait(send_sems, recv_sems, srcs, lands, grad_x, "ffn_grads_scatter_wait", scatter=True)
    own = [lax.dynamic_index_in_dim(s, me, 0, keepdims=False) for s in srcs]
    updates_ffn = [
        _shard_update_direct(p, w, m, v, "update_" + n)
        for p, w, m, v, n in zip(with_own(lands, own), w_sh[n_mix:], m_sh[n_mix:], v_sh[n_mix:], names[n_mix:])
    ]

    core = jnp.reshape(cc, (1,)).astype(jnp.int32)
    pos = jnp.stack([cc, 2 * cx + cy]).astype(jnp.int32)
    gots = _sibling_exchange(sends)
    pairs = [_pair_sum(s, g, core, "pair_sum_" + n) for s, g, n in zip(sends, gots, names)]
    recvs = _chip_exchange(pairs)
    updates = [
        _shard_update(s, g, r, w, m, v, pos, "update_" + n)
        for s, g, r, w, m, v, n in zip(sends, gots, recvs, w_sh, m_sh, v_sh, names)
    ]
    g_w, d_w, nm_w, nv_w = zip(*(updates + updates_ffn))

    small_w = (norm1_g, norm2_g, norm_f_g, b_forget, pool_scale, pool_mix)
    small_m = (m_norm1_g, m_norm2_g, m_norm_f_g, m_b_forget, m_pool_scale, m_pool_mix)
    small_v = (v_norm1_g, v_norm2_g, v_norm_f_g, v_b_forget, v_pool_scale, v_pool_mix)
    zero_row = jnp.zeros((8, LANES), F32)
    (parts,) = _all_gather([_pack_small(small, loss_rows)], "small_all_gather")
    g_s, d_s, nm_s, nv_s = _small_update(parts, _pack_small(small_w, zero_row), _pack_small(small_m, zero_row), _pack_small(small_v, zero_row))
    shapes = [t.shape for t in small_w]
    (g1, g2, gf, gb, gsc, gmix), loss = _unpack_small(g_s, shapes)
    (d1, d2, df, db_, dsc, dmx), _ = _unpack_small(d_s, shapes)
    (m1, m2, mf, mb, msc, mmx), _ = _unpack_small(nm_s, shapes)
    (v1, v2, vf, vb, vsc, vmx), _ = _unpack_small(nv_s, shapes)

    def ordered(n1, win, b, mix, sc, wpo, wao, wout, n2, wg, wu, wd, nf):
        return (n1, win, b, mix, sc, wpo, wao, wout, n2, wg, wu, wd, nf)

    grads = ordered(g1, g_w[0], gb, gmix, gsc, g_w[1], g_w[2], g_w[3], g2, g_w[4], g_w[5], g_w[6], gf)
    deltas = ordered(d1, d_w[0], db_, dmx, dsc, d_w[1], d_w[2], d_w[3], d2, d_w[4], d_w[5], d_w[6], df)
    new_m = ordered(m1, nm_w[0], mb, mmx, msc, nm_w[1], nm_w[2], nm_w[3], m2, nm_w[4], nm_w[5], nm_w[6], mf)
    new_v = ordered(v1, nv_w[0], vb, vmx, vsc, nv_w[1], nv_w[2], nv_w[3], v2, nv_w[4], nv_w[5], nv_w[6], vf)
    return (loss, grad_x, *grads, *deltas, *new_m, *new_v)
```

```python
import functools

import jax
import jax.numpy as jnp
from jax import lax
from jax.experimental import pallas as pl
from jax.experimental.pallas import tpu as pltpu

F32 = jnp.float32
BF16 = jnp.bfloat16
MESH = pl.DeviceIdType.MESH

D_MODEL = 1024
POOL_WINDOWS = (2, 4, 8, 16)
POOL_WIDTH = 512
GROUP_DIM = 128
ATTN_WIDTH = 512
HEAD_DIM = 64
N_HEADS = 8
N_PAIRS = 4
D_FF = 2816
RMS_EPS = 1e-6
N_DEV = 8
LANES = 128
FL_PAD = 128

ADAM_LR = 0.001
ADAM_B1 = 0.9
ADAM_B2 = 0.999
ADAM_EPS = 1e-08
ADAM_WD = 0.01
ADAM_STEP = 10

VMEM_LIMIT = 56 * 1024 * 1024
VMEM_LIMIT_MAX = 60 * 1024 * 1024
ROW_TILE = 256
ATTN_BLOCK = 512
FF_CHUNK = 256
FF_ROW_TILE = 512
DW_TOKENS = 2048


def _mm(a, b):
    return jnp.dot(a, b, preferred_element_type=F32)


def _mm_nt(a, b):
    return lax.dot_general(a, b, (((1,), (1,)), ((), ())), preferred_element_type=F32)


def _mm_tn(a, b):
    return lax.dot_general(a, b, (((0,), (0,)), ((), ())), preferred_element_type=F32)


def _sigmoid(x):
    return 1.0 / (1.0 + jnp.exp(-x))


def _params(sem, vmem=VMEM_LIMIT):
    return pltpu.CompilerParams(dimension_semantics=sem, vmem_limit_bytes=vmem)


def _const_spec(shape):
    nd = len(shape)
    return pl.BlockSpec(shape, lambda *_: (0,) * nd, pipeline_mode=pl.Buffered(1))


def _rms_fwd(x, g):
    r = lax.rsqrt(jnp.mean(x * x, axis=-1, keepdims=True) + RMS_EPS)
    xh = x * r
    return xh * g, xh, r


def _rms_bwd(dy, xh, r, g):
    dxh = dy * g
    dx = r * (dxh - xh * jnp.mean(dxh * xh, axis=-1, keepdims=True))
    return dx, dy * xh


def _in_proj(x, g1, w_uqkv, w_fl, w_g, token):
    T = x.shape[0]
    tm = ROW_TILE

    def body(x_ref, g_ref, wa_ref, wf_ref, wg_ref, token_ref, h_ref, u_ref, qkv_ref, fl_ref, gt_ref):
        h, _, _ = _rms_fwd(x_ref[...], g_ref[...])
        hb = h.astype(BF16)
        h_ref[...] = hb
        z = _mm(hb, wa_ref[...])
        u_ref[...] = z[:, :POOL_WIDTH]
        qkv_ref[...] = z[:, POOL_WIDTH:].astype(BF16)
        fl_ref[...] = _mm(hb, wf_ref[...])
        gt_ref[...] = _mm(hb, wg_ref[...])

    row = lambda n: pl.BlockSpec((tm, n), lambda i: (i, 0))
    return pl.pallas_call(
        body,
        name="in_proj",
        grid=(T // tm,),
        in_specs=[row(D_MODEL), _const_spec((1, D_MODEL)), _const_spec(w_uqkv.shape), _const_spec(w_fl.shape), _const_spec(w_g.shape), _HBM],
        out_specs=[row(D_MODEL), row(POOL_WIDTH), row(3 * ATTN_WIDTH), row(FL_PAD), row(2 * D_MODEL)],
        out_shape=[
            jax.ShapeDtypeStruct((T, D_MODEL), BF16),
            jax.ShapeDtypeStruct((T, POOL_WIDTH), F32),
            jax.ShapeDtypeStruct((T, 3 * ATTN_WIDTH), BF16),
            jax.ShapeDtypeStruct((T, FL_PAD), F32),
            jax.ShapeDtypeStruct((T, 2 * D_MODEL), F32),
        ],
        compiler_params=_params(("parallel",)),
    )(x, g1, w_uqkv, w_fl, w_g, token)


def _log_sigmoid(x):
    return jnp.minimum(x, 0.0) - jnp.log(1.0 + jnp.exp(-jnp.abs(x)))


def _forget_fwd(fl, b_pad, n_seq, S):
    def body(fl_ref, b_ref, fcol_ref):
        lf = _log_sigmoid(fl_ref[...] + b_ref[...])
        t = lf.T
        lane = lax.broadcasted_iota(jnp.int32, t.shape, 1)
        k = 1
        while k < S:
            t = t + jnp.where(lane >= k, pltpu.roll(t, k, 1), 0.0)
            k *= 2
        fcol_ref[...] = t.T

    return pl.pallas_call(
        body,
        name="forget_fwd",
        grid=(n_seq,),
        in_specs=[pl.BlockSpec((S, FL_PAD), lambda s: (s, 0)), _const_spec((1, FL_PAD))],
        out_specs=pl.BlockSpec((S, FL_PAD), lambda s: (s, 0)),
        out_shape=jax.ShapeDtypeStruct((n_seq * S, FL_PAD), F32),
        compiler_params=_params(("parallel",)),
    )(fl, b_pad)


def _window_pick(g, v2, v4, v8, v16):
    return jnp.where(g == 0, v2, jnp.where(g == 1, v4, jnp.where(g == 2, v8, v16)))


def _pool_fwd(u, mix_b, scale, w_po, n_seq, S):
    T = n_seq * S

    def body(u_ref, mix_ref, sc_ref, wpo_ref, pm_ref, p2_ref, p3_ref, py_ref):
        g = pl.program_id(1)
        uu = u_ref[...]
        row = lax.broadcasted_iota(jnp.int32, uu.shape, 0)

        def back(a, k):
            return jnp.where(row >= k, pltpu.roll(a, k, 0), 0.0)

        s2 = uu + back(uu, 1)
        s4 = s2 + back(s2, 2)
        s8 = s4 + back(s4, 4)
        s16 = s8 + back(s8, 8)
        w = _window_pick(g, 2.0, 4.0, 8.0, 16.0)
        cnt = jnp.minimum((row + 1).astype(F32), w)
        pm = _window_pick(g, s2, s4, s8, s16) / cnt - uu
        pmb = pm.astype(BF16)
        pm_ref[...] = pmb
        p2 = _mm(pmb, mix_ref[...])
        p2_ref[...] = p2
        p3 = (p2 * sc_ref[...]).astype(BF16)
        p3_ref[...] = p3

        @pl.when(g == 0)
        def _():
            py_ref[...] = jnp.zeros_like(py_ref)

        py_ref[...] += _mm(p3, wpo_ref[...])

    grp = pl.BlockSpec((S, GROUP_DIM), lambda s, g: (s, g))
    return pl.pallas_call(
        body,
        name="pool_fwd",
        grid=(n_seq, len(POOL_WINDOWS)),
        in_specs=[
            grp,
            pl.BlockSpec((None, GROUP_DIM, GROUP_DIM), lambda s, g: (g, 0, 0)),
            pl.BlockSpec((1, GROUP_DIM), lambda s, g: (0, g)),
            pl.BlockSpec((GROUP_DIM, D_MODEL), lambda s, g: (g, 0)),
        ],
        out_specs=[grp, grp, grp, pl.BlockSpec((S, D_MODEL), lambda s, g: (s, 0))],
        out_shape=[
            jax.ShapeDtypeStruct((T, POOL_WIDTH), BF16),
            jax.ShapeDtypeStruct((T, POOL_WIDTH), F32),
            jax.ShapeDtypeStruct((T, POOL_WIDTH), BF16),
            jax.ShapeDtypeStruct((T, D_MODEL), F32),
        ],
        compiler_params=_params(("parallel", "arbitrary")),
    )(u, mix_b, scale, w_po)


def _split3(v):
    hi = v.astype(BF16).astype(F32)
    r = v - hi
    mid = r.astype(BF16).astype(F32)
    lo = (r - mid).astype(BF16).astype(F32)
    return hi, mid, lo


def _augment(xp, hh, first, second):
    lane = lax.broadcasted_iota(jnp.int32, (1, LANES), 1)
    head = (lane >= HEAD_DIM * hh) & (lane < HEAD_DIM * (hh + 1))
    b = HEAD_DIM * (1 - hh)
    out = jnp.where(head, xp.astype(F32), 0.0)
    for n, col in enumerate(tuple(first) + tuple(second)):
        out = jnp.where(lane == b + n, col, out)
    return out.astype(BF16)


def _attn_fwd(qkv, fcol, n_seq, S):
    T = n_seq * S
    tb = ATTN_BLOCK
    nq = S // tb
    scale = HEAD_DIM ** -0.5

    def body(q_ref, k_ref, v_ref, fc_ref, o_ref, st_ref, qa_sc, ka_sc, m_sc, l_sc, acc_sc):
        i = pl.program_id(1)
        lane = lax.broadcasted_iota(jnp.int32, (1, LANES), 1)
        low = lane < HEAD_DIM
        ones = (1.0, 1.0, 1.0)

        @pl.when(i == 0)
        def _():
            def rows_ka(r, carry):
                r0 = pl.multiple_of(r * tb, tb)
                for h in range(N_HEADS):
                    kp = k_ref[pl.ds(r0, tb), (h // 2) * LANES : (h // 2 + 1) * LANES] * scale
                    fk = fc_ref[pl.ds(r0, tb), h : h + 1]
                    ka_sc[h, pl.ds(r0, tb), :] = _augment(kp, h % 2, ones, _split3(-fk))
                return carry

            lax.fori_loop(0, nq, rows_ka, 0)

        q0 = pl.multiple_of(i * tb, tb)
        for h in range(N_HEADS):
            qp = q_ref[:, (h // 2) * LANES : (h // 2 + 1) * LANES]
            qa_sc[h] = _augment(qp, h % 2, _split3(fc_ref[pl.ds(q0, tb), h : h + 1]), ones)
        m_sc[...] = jnp.full(m_sc.shape, -jnp.inf, F32)
        l_sc[...] = jnp.zeros_like(l_sc)
        acc_sc[...] = jnp.zeros_like(acc_sc)
        causal = lax.broadcasted_iota(jnp.int32, (tb, tb), 1) <= lax.broadcasted_iota(jnp.int32, (tb, tb), 0)

        def step(j, masked):
            c0 = pl.multiple_of(j * tb, tb)
            for p in range(N_PAIRS):
                vb = v_ref[pl.ds(c0, tb), p * LANES : (p + 1) * LANES]
                pv, al = [], []
                for hh in range(2):
                    h = 2 * p + hh
                    s = _mm_nt(qa_sc[h], ka_sc[h, pl.ds(c0, tb), :])
                    if masked:
                        s = jnp.where(causal, s, -jnp.inf)
                    m_old = m_sc[h]
                    m_new = jnp.maximum(m_old, jnp.max(s, axis=1, keepdims=True))
                    alpha = jnp.exp(m_old - m_new)
                    pe = jnp.exp(s - jnp.concatenate([m_new] * (tb // LANES), axis=1))
                    l_sc[h] = alpha * l_sc[h] + jnp.sum(pe, axis=1, keepdims=True)
                    m_sc[h] = m_new
                    pv.append(_mm(pe.astype(BF16), vb))
                    al.append(alpha)
                acc_sc[p] = jnp.where(low, al[0], al[1]) * acc_sc[p] + jnp.where(low, pv[0], pv[1])

        def loop_body(j, carry):
            step(j, False)
            return carry

        lax.fori_loop(0, i, loop_body, 0)
        step(i, True)
        st = jnp.zeros((tb, LANES), F32)
        for p in range(N_PAIRS):
            lp = jnp.where(low, l_sc[2 * p], l_sc[2 * p + 1])
            o_ref[:, p * LANES : (p + 1) * LANES] = (acc_sc[p] / lp).astype(BF16)
            for h in (2 * p, 2 * p + 1):
                st = jnp.where(lane == h, m_sc[h] + jnp.log(l_sc[h]), st)
        st_ref[...] = st

    return pl.pallas_call(
        body,
        name="attn_fwd",
        grid=(n_seq, nq),
        in_specs=[
            pl.BlockSpec((tb, ATTN_WIDTH), lambda s, i: (s * nq + i, 0)),
            pl.BlockSpec((S, ATTN_WIDTH), lambda s, i: (s, 1)),
            pl.BlockSpec((S, ATTN_WIDTH), lambda s, i: (s, 2)),
            pl.BlockSpec((S, LANES), lambda s, i: (s, 0)),
        ],
        out_specs=[
            pl.BlockSpec((tb, ATTN_WIDTH), lambda s, i: (s * nq + i, 0)),
            pl.BlockSpec((tb, LANES), lambda s, i: (s * nq + i, 0)),
        ],
        out_shape=[jax.ShapeDtypeStruct((T, ATTN_WIDTH), BF16), jax.ShapeDtypeStruct((T, LANES), F32)],
        scratch_shapes=[
            pltpu.VMEM((N_HEADS, tb, LANES), BF16),
            pltpu.VMEM((N_HEADS, S, LANES), BF16),
            pltpu.VMEM((N_HEADS, tb, LANES), F32),
            pltpu.VMEM((N_HEADS, tb, LANES), F32),
            pltpu.VMEM((N_PAIRS, tb, LANES), F32),
        ],
        compiler_params=_params(("parallel", "arbitrary")),
    )(qkv, qkv, qkv, fcol)


def _mix_out(a, pool_y, gates, x, w_ao, w_out):
    T = x.shape[0]
    tm = ROW_TILE

    def body(a_ref, py_ref, gt_ref, x_ref, wao_ref, wout_ref, mg_ref, x1_ref, ay_ref):
        ay = _mm(a_ref[...], wao_ref[...])
        ay_ref[...] = ay
        sp = _sigmoid(gt_ref[:, :D_MODEL])
        sa = _sigmoid(gt_ref[:, D_MODEL:])
        mb = (sp * py_ref[...] + sa * ay).astype(BF16)
        mg_ref[...] = mb
        x1_ref[...] = x_ref[...] + _mm(mb, wout_ref[...])

    row = lambda n: pl.BlockSpec((tm, n), lambda i: (i, 0))
    return pl.pallas_call(
        body,
        name="mix_out",
        grid=(T // tm,),
        in_specs=[row(ATTN_WIDTH), row(D_MODEL), row(2 * D_MODEL), row(D_MODEL), _const_spec(w_ao.shape), _const_spec(w_out.shape)],
        out_specs=[row(D_MODEL), row(D_MODEL), row(D_MODEL)],
        out_shape=[jax.ShapeDtypeStruct((T, D_MODEL), BF16), jax.ShapeDtypeStruct((T, D_MODEL), F32), jax.ShapeDtypeStruct((T, D_MODEL), F32)],
        compiler_params=_params(("parallel",)),
    )(a, pool_y, gates, x, w_ao, w_out)


def _ffn_fwd(x1, g2, gf, tgt, w_gate, w_up, w_down):
    T = x1.shape[0]
    tm = min(T, FF_ROW_TILE)
    nt = T // tm
    nc = D_FF // FF_CHUNK

    def body(x1_ref, g2_ref, gf_ref, tg_ref, wg_ref, wu_ref, wd_ref, h2_ref, gate_ref, up_ref, act_ref, dx2_ref, loss_ref, dgf_ref):
        x1v = x1_ref[...]
        h2, _, _ = _rms_fwd(x1v, g2_ref[...])
        h2b = h2.astype(BF16)
        h2_ref[...] = h2b
        acc = x1v
        for c in range(nc):
            sl = slice(c * FF_CHUNK, (c + 1) * FF_CHUNK)
            gate = _mm(h2b, wg_ref[:, sl])
            up = _mm(h2b, wu_ref[:, sl])
            gate_ref[:, sl] = gate.astype(BF16)
            up_ref[:, sl] = up.astype(BF16)
            act = (gate * _sigmoid(gate) * up).astype(BF16)
            act_ref[:, sl] = act
            acc = acc + _mm(act, wd_ref[sl, :])
        gfv = gf_ref[...]
        y, xh, r = _rms_fwd(acc, gfv)
        err = y - tg_ref[...]
        part = 0.5 * jnp.sum(jnp.mean(err * err, axis=-1, keepdims=True), axis=0, keepdims=True)
        dx2, dgrow = _rms_bwd(err * (1.0 / D_MODEL), xh, r, gfv)
        dx2_ref[...] = dx2

        @pl.when(pl.program_id(0) == 0)
        def _():
            dgf_ref[...] = jnp.zeros_like(dgf_ref)
            loss_ref[...] = jnp.zeros_like(loss_ref)

        dgf_ref[...] += jnp.sum(dgrow, axis=0, keepdims=True)
        loss_ref[...] += jnp.broadcast_to(part, loss_ref.shape)

    row = lambda n: pl.BlockSpec((tm, n), lambda i: (i, 0))
    return pl.pallas_call(
        body,
        name="ffn_fwd",
        grid=(nt,),
        in_specs=[
            row(D_MODEL), _const_spec((1, D_MODEL)), _const_spec((1, D_MODEL)), row(D_MODEL),
            _const_spec(w_gate.shape), _const_spec(w_up.shape), _const_spec(w_down.shape),
        ],
        out_specs=[
            row(D_MODEL), row(D_FF), row(D_FF), row(D_FF), row(D_MODEL),
            pl.BlockSpec((8, LANES), lambda i: (0, 0)),
            pl.BlockSpec((1, D_MODEL), lambda i: (0, 0)),
        ],
        out_shape=[
            jax.ShapeDtypeStruct((T, D_MODEL), BF16),
            jax.ShapeDtypeStruct((T, D_FF), BF16),
            jax.ShapeDtypeStruct((T, D_FF), BF16),
            jax.ShapeDtypeStruct((T, D_FF), BF16),
            jax.ShapeDtypeStruct((T, D_MODEL), F32),
            jax.ShapeDtypeStruct((8, LANES), F32),
            jax.ShapeDtypeStruct((1, D_MODEL), F32),
        ],
        compiler_params=_params(("arbitrary",)),
    )(x1, g2, gf, tgt, w_gate, w_up, w_down)


def _ffn_bwd(dx2, gate, up, x1, g2, w_gate, w_up, w_down):
    T = x1.shape[0]
    tm = min(T, FF_ROW_TILE)
    nc = D_FF // FF_CHUNK

    def body(dx2_ref, gate_ref, up_ref, x1_ref, g2_ref, wg_ref, wu_ref, wd_ref, dgate_ref, dup_ref, dx1_ref, dg2_ref):
        dx2v = dx2_ref[...]
        dx2b = dx2v.astype(BF16)
        dh2 = jnp.zeros((tm, D_MODEL), F32)
        for c in range(nc):
            sl = slice(c * FF_CHUNK, (c + 1) * FF_CHUNK)
            dact = _mm_nt(dx2b, wd_ref[sl, :])
            gate = gate_ref[:, sl].astype(F32)
            sg = _sigmoid(gate)
            silu = gate * sg
            dgate = (dact * up_ref[:, sl].astype(F32) * (sg * (1.0 + gate * (1.0 - sg)))).astype(BF16)
            dup = (dact * silu).astype(BF16)
            dgate_ref[:, sl] = dgate
            dup_ref[:, sl] = dup
            dh2 = dh2 + _mm_nt(dgate, wg_ref[:, sl]) + _mm_nt(dup, wu_ref[:, sl])
        g2v = g2_ref[...]
        _, xh, r = _rms_fwd(x1_ref[...], g2v)
        dxn, dgrow = _rms_bwd(dh2, xh, r, g2v)
        dx1_ref[...] = dx2v + dxn

        @pl.when(pl.program_id(0) == 0)
        def _():
            dg2_ref[...] = jnp.zeros_like(dg2_ref)

        dg2_ref[...] += jnp.sum(dgrow, axis=0, keepdims=True)

    row = lambda n: pl.BlockSpec((tm, n), lambda i: (i, 0))
    return pl.pallas_call(
        body,
        name="ffn_bwd",
        grid=(T // tm,),
        in_specs=[
            row(D_MODEL), row(D_FF), row(D_FF), row(D_MODEL), _const_spec((1, D_MODEL)),
            _const_spec(w_gate.shape), _const_spec(w_up.shape), _const_spec(w_down.shape),
        ],
        out_specs=[row(D_FF), row(D_FF), row(D_MODEL), pl.BlockSpec((1, D_MODEL), lambda i: (0, 0))],
        out_shape=[
            jax.ShapeDtypeStruct((T, D_FF), BF16),
            jax.ShapeDtypeStruct((T, D_FF), BF16),
            jax.ShapeDtypeStruct((T, D_MODEL), F32),
            jax.ShapeDtypeStruct((1, D_MODEL), F32),
        ],
        compiler_params=_params(("arbitrary",), VMEM_LIMIT_MAX),
    )(dx2, gate, up, x1, g2, w_gate, w_up, w_down)


def _mix_bwd(dx1, gates, pool_y, attn_y, p2, scale, w_out, w_ao, w_po, token):
    T = dx1.shape[0]
    tm = ROW_TILE

    def body(dx1_ref, gt_ref, py_ref, ay_ref, p2_ref, sc_ref, wout_ref, wao_ref, wpo_ref, token_ref, dgt_ref, dpy_ref, day_ref, da_ref, dp2_ref, dsc_ref):
        dm = _mm_nt(dx1_ref[...].astype(BF16), wout_ref[...])
        sp = _sigmoid(gt_ref[:, :D_MODEL])
        sa = _sigmoid(gt_ref[:, D_MODEL:])
        dgt_ref[:, :D_MODEL] = (dm * py_ref[...] * (sp * (1.0 - sp))).astype(BF16)
        dgt_ref[:, D_MODEL:] = (dm * ay_ref[...] * (sa * (1.0 - sa))).astype(BF16)
        dpy = (dm * sp).astype(BF16)
        day = (dm * sa).astype(BF16)
        dpy_ref[...] = dpy
        day_ref[...] = day
        da_ref[...] = _mm_nt(day, wao_ref[...]).astype(BF16)
        dp3 = _mm_nt(dpy, wpo_ref[...])
        dp2_ref[...] = (dp3 * sc_ref[...]).astype(BF16)

        @pl.when(pl.program_id(0) == 0)
        def _():
            dsc_ref[...] = jnp.zeros_like(dsc_ref)

        dsc_ref[...] += jnp.sum(dp3 * p2_ref[...], axis=0, keepdims=True)

    row = lambda n: pl.BlockSpec((tm, n), lambda i: (i, 0))
    return pl.pallas_call(
        body,
        name="mix_bwd",
        grid=(T // tm,),
        in_specs=[
            row(D_MODEL), row(2 * D_MODEL), row(D_MODEL), row(D_MODEL), row(POOL_WIDTH), _const_spec((1, POOL_WIDTH)),
            _const_spec(w_out.shape), _const_spec(w_ao.shape), _const_spec(w_po.shape), _HBM,
        ],
        out_specs=[row(2 * D_MODEL), row(D_MODEL), row(D_MODEL), row(ATTN_WIDTH), row(POOL_WIDTH), pl.BlockSpec((1, POOL_WIDTH), lambda i: (0, 0))],
        out_shape=[
            jax.ShapeDtypeStruct((T, 2 * D_MODEL), BF16),
            jax.ShapeDtypeStruct((T, D_MODEL), BF16),
            jax.ShapeDtypeStruct((T, D_MODEL), BF16),
            jax.ShapeDtypeStruct((T, ATTN_WIDTH), BF16),
            jax.ShapeDtypeStruct((T, POOL_WIDTH), BF16),
            jax.ShapeDtypeStruct((1, POOL_WIDTH), F32),
        ],
        compiler_params=_params(("arbitrary",)),
    )(dx1, gates, pool_y, attn_y, p2, scale, w_out, w_ao, w_po, token)


def _pool_bwd(dp2, pm, mix_b, n_seq, S):
    T = n_seq * S

    def body(dp2_ref, pm_ref, mix_ref, du_ref, dmix_ref):
        g = pl.program_id(0)
        dp2v = dp2_ref[...]
        dpm = _mm_nt(dp2v, mix_ref[...])
        row = lax.broadcasted_iota(jnp.int32, dpm.shape, 0)
        w = _window_pick(g, 2.0, 4.0, 8.0, 16.0)
        e = dpm / jnp.minimum((row + 1).astype(F32), w)

        def ahead(a, k):
            return jnp.where(row < S - k, pltpu.roll(a, S - k, 0), 0.0)

        r2 = e + ahead(e, 1)
        r4 = r2 + ahead(r2, 2)
        r8 = r4 + ahead(r4, 4)
        r16 = r8 + ahead(r8, 8)
        du_ref[...] = (_window_pick(g, r2, r4, r8, r16) - dpm).astype(BF16)

        @pl.when(pl.program_id(1) == 0)
        def _():
            dmix_ref[...] = jnp.zeros_like(dmix_ref)

        dmix_ref[...] += _mm_tn(pm_ref[...], dp2v)

    grp = pl.BlockSpec((S, GROUP_DIM), lambda g, s: (s, g))
    mixs = pl.BlockSpec((None, GROUP_DIM, GROUP_DIM), lambda g, s: (g, 0, 0))
    return pl.pallas_call(
        body,
        name="pool_bwd",
        grid=(len(POOL_WINDOWS), n_seq),
        in_specs=[grp, grp, mixs],
        out_specs=[grp, mixs],
        out_shape=[jax.ShapeDtypeStruct((T, POOL_WIDTH), BF16), jax.ShapeDtypeStruct((len(POOL_WINDOWS), GROUP_DIM, GROUP_DIM), F32)],
        compiler_params=_params(("parallel", "arbitrary")),
    )(dp2, pm, mix_b)


def _attn_bwd(qkv, da, a, fcol, lse, n_seq, S):
    T = n_seq * S
    tb = ATTN_BLOCK
    nb = S // tb
    scale = HEAD_DIM ** -0.5

    def body(q_ref, k_ref, v_ref, do_ref, o_ref, fc_ref, st_ref, dq_ref, dk_ref, dv_ref, dfk_ref, dfq_ref,
             qa_sc, doa_sc, dq_acc, ka_sc, va_sc, dk_sc, dv_sc):
        j = pl.program_id(1)
        lane = lax.broadcasted_iota(jnp.int32, (1, LANES), 1)
        low = lane < HEAD_DIM
        ones = (1.0, 1.0, 1.0)
        zeros = (0.0, 0.0, 0.0)

        @pl.when(j == 0)
        def _():
            dq_acc[...] = jnp.zeros_like(dq_acc)

            def rows_q(i, carry):
                r0 = pl.multiple_of(i * tb, tb)
                for h in range(N_HEADS):
                    pair = slice((h // 2) * LANES, (h // 2 + 1) * LANES)
                    qp = q_ref[pl.ds(r0, tb), pair]
                    dop = do_ref[pl.ds(r0, tb), pair]
                    prod = dop.astype(F32) * o_ref[pl.ds(r0, tb), pair].astype(F32)
                    head = (lane >= HEAD_DIM * (h % 2)) & (lane < HEAD_DIM * (h % 2 + 1))
                    delta = jnp.sum(jnp.where(head, prod, 0.0), axis=1, keepdims=True)
                    cq = fc_ref[pl.ds(r0, tb), h : h + 1] - st_ref[pl.ds(r0, tb), h : h + 1]
                    qa_sc[h, pl.ds(r0, tb), :] = _augment(qp, h % 2, _split3(cq), ones)
                    doa_sc[h, pl.ds(r0, tb), :] = _augment(dop, h % 2, _split3(-delta), zeros)
                return carry

            lax.fori_loop(0, nb, rows_q, 0)

        c0 = pl.multiple_of(j * tb, tb)
        for h in range(N_HEADS):
            pair = slice((h // 2) * LANES, (h // 2 + 1) * LANES)
            kp = k_ref[:, pair] * scale
            ka_sc[h] = _augment(kp, h % 2, ones, _split3(-fc_ref[pl.ds(c0, tb), h : h + 1]))
            va_sc[h] = _augment(v_ref[:, pair], h % 2, ones, zeros)
        dk_sc[...] = jnp.zeros_like(dk_sc)
        dv_sc[...] = jnp.zeros_like(dv_sc)
        causal = lax.broadcasted_iota(jnp.int32, (tb, tb), 1) <= lax.broadcasted_iota(jnp.int32, (tb, tb), 0)

        def step(i, masked):
            r0 = pl.multiple_of(i * tb, tb)
            for h in range(N_HEADS):
                dob = do_ref[pl.ds(r0, tb), (h // 2) * LANES : (h // 2 + 1) * LANES]
                qa = qa_sc[h, pl.ds(r0, tb), :]
                s = _mm_nt(qa, ka_sc[h])
                if masked:
                    s = jnp.where(causal, s, -jnp.inf)
                pr = jnp.exp(s)
                dv_sc[h] += _mm_tn(pr.astype(BF16), dob)
                dsb = (pr * _mm_nt(doa_sc[h, pl.ds(r0, tb), :], va_sc[h])).astype(BF16)
                dk_sc[h] += _mm_tn(dsb, qa)
                dq_acc[h, pl.ds(r0, tb), :] += _mm(dsb, ka_sc[h])

        step(j, True)

        def loop_body(i, carry):
            step(i, False)
            return carry

        lax.fori_loop(j + 1, nb, loop_body, 0)
        dfk = jnp.zeros((tb, LANES), F32)
        for p in range(N_PAIRS):
            dk_ref[:, p * LANES : (p + 1) * LANES] = (jnp.where(low, dk_sc[2 * p], dk_sc[2 * p + 1]) * scale).astype(BF16)
            dv_ref[:, p * LANES : (p + 1) * LANES] = jnp.where(low, dv_sc[2 * p], dv_sc[2 * p + 1]).astype(BF16)
            for hh in range(2):
                b = HEAD_DIM * (1 - hh) + 3
                dfk = jnp.where(lane == 2 * p + hh, -dk_sc[2 * p + hh][:, b : b + 1], dfk)
        dfk_ref[...] = dfk

        @pl.when(j == nb - 1)
        def _():
            def rows_dq(i, carry):
                r0 = pl.multiple_of(i * tb, tb)
                dfq = jnp.zeros((tb, LANES), F32)
                for p in range(N_PAIRS):
                    parts = [dq_acc[2 * p + hh, pl.ds(r0, tb), :] for hh in range(2)]
                    dq_ref[pl.ds(r0, tb), p * LANES : (p + 1) * LANES] = jnp.where(low, parts[0], parts[1]).astype(BF16)
                    for hh in range(2):
                        b = HEAD_DIM * (1 - hh)
                        dfq = jnp.where(lane == 2 * p + hh, parts[hh][:, b : b + 1], dfq)
                dfq_ref[pl.ds(r0, tb), :] = dfq
                return carry

            lax.fori_loop(0, nb, rows_dq, 0)

    seq = lambda w, col: pl.BlockSpec((S, w), lambda s, j: (s, col))
    blk = lambda w, col: pl.BlockSpec((tb, w), lambda s, j: (s * nb + j, col))
    return pl.pallas_call(
        body,
        name="attn_bwd",
        grid=(n_seq, nb),
        in_specs=[seq(ATTN_WIDTH, 0), blk(ATTN_WIDTH, 1), blk(ATTN_WIDTH, 2), seq(ATTN_WIDTH, 0), seq(ATTN_WIDTH, 0), seq(LANES, 0), seq(LANES, 0)],
        out_specs=[seq(ATTN_WIDTH, 0), blk(ATTN_WIDTH, 0), blk(ATTN_WIDTH, 0), blk(LANES, 0), seq(LANES, 0)],
        out_shape=[
            jax.ShapeDtypeStruct((T, ATTN_WIDTH), BF16),
            jax.ShapeDtypeStruct((T, ATTN_WIDTH), BF16),
            jax.ShapeDtypeStruct((T, ATTN_WIDTH), BF16),
            jax.ShapeDtypeStruct((T, LANES), F32),
            jax.ShapeDtypeStruct((T, LANES), F32),
        ],
        scratch_shapes=[
            pltpu.VMEM((N_HEADS, S, LANES), BF16),
            pltpu.VMEM((N_HEADS, S, LANES), BF16),
            pltpu.VMEM((N_HEADS, S, LANES), F32),
            pltpu.VMEM((N_HEADS, tb, LANES), BF16),
            pltpu.VMEM((N_HEADS, tb, LANES), BF16),
            pltpu.VMEM((N_HEADS, tb, LANES), F32),
            pltpu.VMEM((N_HEADS, tb, LANES), F32),
        ],
        compiler_params=_params(("parallel", "arbitrary")),
    )(qkv, qkv, qkv, da, a, fcol, lse)


def _forget_bwd(dfk, dfq, fl, b_pad, n_seq, S):
    def body(df_ref, dfq_ref, fl_ref, b_ref, dfl_ref, db_ref):
        t = (df_ref[...] + dfq_ref[...]).T
        lane = lax.broadcasted_iota(jnp.int32, t.shape, 1)
        k = 1
        while k < S:
            t = t + jnp.where(lane < S - k, pltpu.roll(t, S - k, 1), 0.0)
            k *= 2
        dfl = t.T * _sigmoid(-(fl_ref[...] + b_ref[...]))
        dfl_ref[...] = dfl.astype(BF16)

        @pl.when(pl.program_id(0) == 0)
        def _():
            db_ref[...] = jnp.zeros_like(db_ref)

        db_ref[...] += jnp.sum(dfl, axis=0, keepdims=True)

    return pl.pallas_call(
        body,
        name="forget_bwd",
        grid=(n_seq,),
        in_specs=[
            pl.BlockSpec((S, LANES), lambda s: (s, 0)),
            pl.BlockSpec((S, LANES), lambda s: (s, 0)),
            pl.BlockSpec((S, FL_PAD), lambda s: (s, 0)),
            _const_spec((1, FL_PAD)),
        ],
        out_specs=[pl.BlockSpec((S, FL_PAD), lambda s: (s, 0)), pl.BlockSpec((1, FL_PAD), lambda s: (0, 0))],
        out_shape=[jax.ShapeDtypeStruct((n_seq * S, FL_PAD), BF16), jax.ShapeDtypeStruct((1, FL_PAD), F32)],
        compiler_params=_params(("arbitrary",)),
    )(dfk, dfq, fl, b_pad)


def _in_proj_bwd(du, dq, dk, dv, dfl, dgates, x, dx1, g1, w_uqkv, w_fl, w_g):
    T = x.shape[0]
    tm = ROW_TILE

    def body(du_ref, dq_ref, dk_ref, dv_ref, dfl_ref, dgt_ref, x_ref, dx1_ref, g_ref, wa_ref, wf_ref, wg_ref, dx_ref, dg_ref):
        dh = _mm_nt(dgt_ref[...], wg_ref[...]) + _mm_nt(dfl_ref[...], wf_ref[...])
        for n, ref in enumerate((du_ref, dq_ref, dk_ref, dv_ref)):
            dh = dh + _mm_nt(ref[...], wa_ref[:, n * 512 : (n + 1) * 512])
        gv = g_ref[...]
        _, xh, r = _rms_fwd(x_ref[...], gv)
        dxn, dgrow = _rms_bwd(dh, xh, r, gv)
        dx_ref[...] = dx1_ref[...] + dxn

        @pl.when(pl.program_id(0) == 0)
        def _():
            dg_ref[...] = jnp.zeros_like(dg_ref)

        dg_ref[...] += jnp.sum(dgrow, axis=0, keepdims=True)

    row = lambda n: pl.BlockSpec((tm, n), lambda i: (i, 0))
    return pl.pallas_call(
        body,
        name="in_proj_bwd",
        grid=(T // tm,),
        in_specs=[
            row(512), row(512), row(512), row(512), row(FL_PAD), row(2 * D_MODEL), row(D_MODEL), row(D_MODEL), _const_spec((1, D_MODEL)),
            _const_spec(w_uqkv.shape), _const_spec(w_fl.shape), _const_spec(w_g.shape),
        ],
        out_specs=[row(D_MODEL), pl.BlockSpec((1, D_MODEL), lambda i: (0, 0))],
        out_shape=[jax.ShapeDtypeStruct((T, D_MODEL), F32), jax.ShapeDtypeStruct((1, D_MODEL), F32)],
        compiler_params=_params(("arbitrary",)),
    )(du, dq, dk, dv, dfl, dgates, x, dx1, g1, w_uqkv, w_fl, w_g)


def _pick_block(n):
    for b in (512, 1408, 256, 128):
        if n % b == 0:
            return b
    raise ValueError(n)


def _matmul_tn(a, b, name, row_sharded=False):
    T, K = a.shape
    N = b.shape[1]
    bt, bk, bn = min(T, DW_TOKENS), _pick_block(K), _pick_block(N)
    nt = T // bt
    r = K // N_DEV
    assert not row_sharded or bk == 4 * r

    def body(a_ref, b_ref, o_ref, acc):
        @pl.when(pl.program_id(2) == 0)
        def _():
            acc[...] = jnp.zeros_like(acc)

        acc[...] += _mm_tn(a_ref[...].astype(BF16), b_ref[...].astype(BF16))

        @pl.when(pl.program_id(2) == nt - 1)
        def _():
            if row_sharded:
                for chip in range(2):
                    for core in range(2):
                        d = 2 * chip + core
                        o_ref[core, chip] = acc[d * r : (d + 1) * r, :].astype(BF16)
            else:
                o_ref[...] = acc[...].astype(BF16)

    if row_sharded:
        out_spec = pl.BlockSpec((2, 2, r, bn), lambda k, n, t: (0, k, 0, n))
        out_shape = jax.ShapeDtypeStruct((2, 4, r, N), BF16)
    else:
        out_spec = pl.BlockSpec((bk, bn), lambda k, n, t: (k, n))
        out_shape = jax.ShapeDtypeStruct((K, N), BF16)
    return pl.pallas_call(
        body,
        name=name,
        grid=(K // bk, N // bn, nt),
        in_specs=[pl.BlockSpec((bt, bk), lambda k, n, t: (t, k)), pl.BlockSpec((bt, bn), lambda k, n, t: (t, n))],
        out_specs=out_spec,
        out_shape=out_shape,
        scratch_shapes=[pltpu.VMEM((bk, bn), F32)],
        compiler_params=_params(("parallel", "parallel", "arbitrary")),
    )(a, b)


def _position():
    return lax.axis_index("x"), lax.axis_index("y"), lax.axis_index("c")


_HBM = pl.BlockSpec(memory_space=pl.ANY)


def _all_gather(blocks, name):
    n = len(blocks)

    def body(*refs):
        xs, outs = refs[:n], refs[n : 2 * n]
        send_sems, recv_sems, local_sems = refs[2 * n :]
        x, y, c = _position()
        me, sibling = (x, y, c), (x, y, 1 - c)
        chips = [(1 - x, y), (x, 1 - y), (1 - x, 1 - y)]

        def rows(a, px, py, pc):
            return outs[a].at[4 * px + 2 * py + pc]

        def copy(a, k, blk, to, src=None):
            return pltpu.make_async_remote_copy(
                src_ref=rows(a, *blk) if src is None else src, dst_ref=rows(a, *blk),
                send_sem=send_sems.at[7 * a + k], recv_sem=recv_sems.at[7 * a + k], device_id=to, device_id_type=MESH,
            )

        mine = [pltpu.make_async_copy(xs[a], rows(a, *me), local_sems.at[a]) for a in range(n)]
        for cp in mine:
            cp.start()
        first = []
        for a in range(n):
            first.append(copy(a, 0, me, sibling, src=xs[a]))
            first += [copy(a, 1 + j, me, (*chip, c), src=xs[a]) for j, chip in enumerate(chips)]
        for cp in first:
            cp.start()
        passed = []
        for j, chip in enumerate(chips):
            for a in range(n):
                copy(a, 1 + j, (*chip, c), me).wait_recv()
                passed.append(copy(a, 4 + j, (*chip, c), sibling))
                passed[-1].start()
        for a in range(n):
            copy(a, 0, sibling, me).wait_recv()
        for j, chip in enumerate(chips):
            for a in range(n):
                copy(a, 4 + j, (*chip, 1 - c), me).wait_recv()
        for cp in first + passed:
            cp.wait_send()
        for cp in mine:
            cp.wait()

    return pl.pallas_call(
        body,
        name=name,
        out_shape=[jax.ShapeDtypeStruct((N_DEV, *b.shape), b.dtype) for b in blocks],
        in_specs=[_HBM] * n,
        out_specs=[_HBM] * n,
        scratch_shapes=[pltpu.SemaphoreType.DMA((7 * n,)), pltpu.SemaphoreType.DMA((7 * n,)), pltpu.SemaphoreType.DMA((n,))],
    )(*blocks)


_SEM = pl.BlockSpec(memory_space=pltpu.SEMAPHORE)
_HBM_ONLY = pl.BlockSpec(memory_space=pltpu.HBM)
_SIDE_EFFECT = pltpu.SideEffectType.DATAFLOW_SIDE_EFFECTING


def _peer(x, y, c, k):
    return (1 - x if k & 4 else x, 1 - y if k & 2 else y, 1 - c if k & 1 else c)


def _exchange_copies(src_refs, land_refs, send_sems, recv_sems, scatter, receive_side):
    x, y, c = _position()
    me = 4 * x + 2 * y + c
    cps = []
    for k in range(1, N_DEV):
        px, py, pc = _peer(x, y, c, k)
        peer = 4 * px + 2 * py + pc
        for a, (src, land) in enumerate(zip(src_refs, land_refs)):
            cps.append(pltpu.make_async_remote_copy(
                src_ref=src.at[peer] if scatter else src, dst_ref=land.at[peer if receive_side else me],
                send_sem=send_sems.at[7 * a + k - 1], recv_sem=recv_sems.at[7 * a + k - 1],
                device_id=(px, py, pc), device_id_type=MESH,
            ))
    return cps


def _exchange_start(srcs, after, name, scatter):
    n = len(srcs)
    lands = [jax.ShapeDtypeStruct((N_DEV, *s.shape[-2:]), s.dtype) for s in srcs]

    def body(*refs):
        src_refs, land_refs = refs[1 : 1 + n], refs[1 + n : 1 + 2 * n]
        send_sems, recv_sems = refs[1 + 2 * n], refs[2 + 2 * n]
        token = refs[-1]
        for cp in _exchange_copies(src_refs, land_refs, send_sems, recv_sems, scatter, receive_side=False):
            cp.start()
        token[...] = jnp.zeros_like(token)

    hbm = lambda t: pltpu.with_memory_space_constraint(t, pltpu.HBM)
    out = pl.pallas_call(
        body,
        name=name,
        out_shape=(
            pltpu.SemaphoreType.DMA((7 * n,)), pltpu.SemaphoreType.DMA((7 * n,)),
            *[pltpu.HBM(s.shape, s.dtype) for s in srcs], *[pltpu.HBM(l.shape, l.dtype) for l in lands],
            jax.ShapeDtypeStruct((8, LANES), F32),
        ),
        in_specs=(_HBM, *[_HBM_ONLY] * (2 * n)),
        out_specs=(_SEM, _SEM, *[_HBM_ONLY] * (2 * n), pl.BlockSpec(memory_space=pltpu.VMEM)),
        input_output_aliases={1 + i: 2 + i for i in range(2 * n)},
        compiler_params=pltpu.CompilerParams(has_side_effects=_SIDE_EFFECT),
    )(after, *[hbm(s) for s in srcs], *[hbm(lax.empty(l.shape, l.dtype)) for l in lands])
    return out[0], out[1], out[2 : 2 + n], out[2 + n : 2 + 2 * n], out[-1]


def _exchange_wait(send_sems, recv_sems, srcs, lands, after, name, scatter):
    n = len(srcs)

    def body(*refs):
        src_refs, land_refs = refs[:n], refs[n : 2 * n]
        for cp in _exchange_copies(src_refs, land_refs, refs[2 * n], refs[2 * n + 1], scatter, receive_side=True):
            cp.wait_send()
            cp.wait_recv()

    out = pl.pallas_call(
        body,
        name=name,
        out_shape=(*[pltpu.HBM(s.shape, s.dtype) for s in srcs], *[pltpu.HBM(l.shape, l.dtype) for l in lands]),
        in_specs=(*[_HBM_ONLY] * (2 * n), _SEM, _SEM, _HBM),
        out_specs=tuple([_HBM_ONLY] * (2 * n)),
        input_output_aliases={i: i for i in range(2 * n)},
        compiler_params=pltpu.CompilerParams(has_side_effects=_SIDE_EFFECT),
    )(*srcs, *lands, send_sems, recv_sems, after)
    return out[:n], out[n:]


def _sibling_exchange(sends):
    n = len(sends)

    def body(*refs):
        srcs, dsts = refs[:n], refs[n : 2 * n]
        send_sems, recv_sems = refs[2 * n :]
        x, y, c = _position()
        cps = [
            pltpu.make_async_remote_copy(
                src_ref=srcs[a].at[1 - c], dst_ref=dsts[a], send_sem=send_sems.at[a], recv_sem=recv_sems.at[a],
                device_id=(x, y, 1 - c), device_id_type=MESH,
            )
            for a in range(n)
        ]
        for cp in cps:
            cp.start()
        for cp in cps:
            cp.wait()

    return pl.pallas_call(
        body,
        name="rs_sibling",
        out_shape=[jax.ShapeDtypeStruct(s.shape[1:], s.dtype) for s in sends],
        in_specs=[_HBM] * n,
        out_specs=[_HBM] * n,
        scratch_shapes=[pltpu.SemaphoreType.DMA((n,)), pltpu.SemaphoreType.DMA((n,))],
    )(*sends)


def _rows_tile(r):
    return ROW_TILE if r % ROW_TILE == 0 else r


def _pair_sum(send, got, core, name):
    _, _, r, c = send.shape
    br = _rows_tile(r)

    def body(core_ref, a_ref, b_ref, o_ref):
        o_ref[...] = (a_ref[...].astype(F32) + b_ref[...].astype(F32)).astype(o_ref.dtype)

    return pl.pallas_call(
        body,
        name=name,
        grid_spec=pltpu.PrefetchScalarGridSpec(
            num_scalar_prefetch=1,
            grid=(4, r // br),
            in_specs=[
                pl.BlockSpec((None, None, br, c), lambda n, i, core: (core[0], n, i, 0)),
                pl.BlockSpec((None, br, c), lambda n, i, core: (n, i, 0)),
            ],
            out_specs=pl.BlockSpec((None, br, c), lambda n, i, core: (n, i, 0)),
        ),
        out_shape=jax.ShapeDtypeStruct((4, r, c), send.dtype),
        compiler_params=_params(("parallel", "parallel")),
    )(core, send, got)


def _chip_exchange(pairs):
    n = len(pairs)

    def body(*refs):
        srcs, dsts = refs[:n], refs[n : 2 * n]
        send_sems, recv_sems = refs[2 * n :]
        x, y, c = _position()
        chips = [(1 - x, y), (x, 1 - y), (1 - x, 1 - y)]
        cps = [
            pltpu.make_async_remote_copy(
                src_ref=srcs[a].at[2 * cx + cy], dst_ref=dsts[a].at[j], send_sem=send_sems.at[3 * a + j], recv_sem=recv_sems.at[3 * a + j],
                device_id=(cx, cy, c), device_id_type=MESH,
            )
            for a in range(n)
            for j, (cx, cy) in enumerate(chips)
        ]
        for cp in cps:
            cp.start()
        for cp in cps:
            cp.wait()

    return pl.pallas_call(
        body,
        name="rs_chips",
        out_shape=[jax.ShapeDtypeStruct((3, *p.shape[1:]), p.dtype) for p in pairs],
        in_specs=[_HBM] * n,
        out_specs=[_HBM] * n,
        scratch_shapes=[pltpu.SemaphoreType.DMA((3 * n,)), pltpu.SemaphoreType.DMA((3 * n,))],
    )(*pairs)


def _adamw(w, g, m, v):
    m = ADAM_B1 * m + (1.0 - ADAM_B1) * g
    v = ADAM_B2 * v + (1.0 - ADAM_B2) * (g * g)
    m_hat = m / (1.0 - ADAM_B1 ** ADAM_STEP)
    v_hat = v / (1.0 - ADAM_B2 ** ADAM_STEP)
    delta = -ADAM_LR * (m_hat / (jnp.sqrt(v_hat) + ADAM_EPS) + ADAM_WD * w)
    return delta, m, v


def _shard_update(send, got, recv, w, m, v, pos, name):
    _, r, c = w.shape
    br = _rows_tile(r)

    def body(pos_ref, a_ref, b_ref, r_ref, w_ref, m_ref, v_ref, g_ref, d_ref, nm_ref, nv_ref):
        g = a_ref[...].astype(F32) + b_ref[...].astype(F32)
        for n in range(3):
            g = g + r_ref[n].astype(F32)
        g_ref[...] = g
        d_ref[...], nm_ref[...], nv_ref[...] = _adamw(w_ref[...], g, m_ref[...], v_ref[...])

    own = pl.BlockSpec((None, br, c), lambda i, pos: (0, i, 0))
    return pl.pallas_call(
        body,
        name=name,
        grid_spec=pltpu.PrefetchScalarGridSpec(
            num_scalar_prefetch=1,
            grid=(r // br,),
            in_specs=[
                pl.BlockSpec((None, None, br, c), lambda i, pos: (pos[0], pos[1], i, 0)),
                pl.BlockSpec((None, br, c), lambda i, pos: (pos[1], i, 0)),
                pl.BlockSpec((3, br, c), lambda i, pos: (0, i, 0)),
                own, own, own,
            ],
            out_specs=[own, own, own, own],
        ),
        out_shape=[jax.ShapeDtypeStruct((1, r, c), F32)] * 4,
        compiler_params=_params(("parallel",)),
    )(pos, send, got, recv, w, m, v)


def _shard_update_direct(parts, w, m, v, name):
    _, r, c = w.shape
    br = _rows_tile(r)

    def body(p_ref, w_ref, m_ref, v_ref, g_ref, d_ref, nm_ref, nv_ref):
        g = p_ref[0].astype(F32)
        for n in range(1, N_DEV):
            g = g + p_ref[n].astype(F32)
        g_ref[...] = g
        d_ref[...], nm_ref[...], nv_ref[...] = _adamw(w_ref[...], g, m_ref[...], v_ref[...])

    own = pl.BlockSpec((None, br, c), lambda i: (0, i, 0))
    return pl.pallas_call(
        body,
        name=name,
        grid=(r // br,),
        in_specs=[pl.BlockSpec((N_DEV, br, c), lambda i: (0, i, 0)), own, own, own],
        out_specs=[own, own, own, own],
        out_shape=[jax.ShapeDtypeStruct((1, r, c), F32)] * 4,
        compiler_params=_params(("parallel",)),
    )(parts, w, m, v)


def _small_update(parts, w, m, v):
    R = w.shape[0]

    def body(p_ref, w_ref, m_ref, v_ref, g_ref, d_ref, nm_ref, nv_ref):
        g = p_ref[0]
        for n in range(1, N_DEV):
            g = g + p_ref[n]
        g_ref[...] = g
        d_ref[...], nm_ref[...], nv_ref[...] = _adamw(w_ref[...], g, m_ref[...], v_ref[...])

    return pl.pallas_call(
        body,
        name="small_update",
        out_shape=[jax.ShapeDtypeStruct((R, LANES), F32)] * 4,
        compiler_params=pltpu.CompilerParams(vmem_limit_bytes=VMEM_LIMIT),
    )(parts, w, m, v)


_SHARD_AXIS = (1, 1, 1, 0, 1, 1, 0)


def _full_from_gathered(t, axis):
    if axis == 0:
        return t.reshape(N_DEV * t.shape[1], t.shape[2])
    return jnp.concatenate([t[d] for d in range(N_DEV)], axis=1)


def _chunks_from_cols(t):
    c = t.shape[1] // N_DEV
    return jnp.stack([t[:, d * c : (d + 1) * c] for d in range(N_DEV)])


def _send_from_cols(t):
    c = t.shape[1] // N_DEV
    return jnp.stack([jnp.stack([t[:, (2 * chip + core) * c : (2 * chip + core + 1) * c] for chip in range(4)]) for core in range(2)])


_SMALL = (("norm1_g", 8), ("norm2_g", 8), ("norm_f_g", 8), ("b_forget", 8), ("pool_scale", 8), ("pool_mix", 512))
_SMALL_ROWS = sum(r for _, r in _SMALL) + 8


def _pack_small(vals, loss_row):
    parts = []
    for (name, rows), t in zip(_SMALL, vals):
        f = t.astype(F32).reshape(-1)
        f = jnp.concatenate([f, jnp.zeros((rows * LANES - f.shape[0],), F32)]).reshape(rows, LANES)
        parts.append(f)
    parts.append(loss_row)
    return jnp.concatenate(parts, axis=0)


def _unpack_small(packed, shapes):
    out, off = [], 0
    for (name, rows), shape in zip(_SMALL, shapes):
        n = 1
        for s in shape:
            n *= s
        out.append(packed[off : off + rows].reshape(-1)[:n].reshape(shape))
        off += rows
    return out, packed[off, 0]


def _local_grads(x, tgt, g1, g2, gf, b_forget, pool_mix, pool_scale, w_in, w_po, w_ao, w_out, fwd_token, ffn_weights, ffn_grads_out):
    n_seq, S, _ = x.shape
    T = n_seq * S
    x2 = x.reshape(T, D_MODEL)
    tg2 = tgt.reshape(T, D_MODEL)
    w_uqkv = w_in[:, : POOL_WIDTH + 3 * ATTN_WIDTH]
    w_fl = jnp.concatenate([w_in[:, 2048 : 2048 + N_HEADS], jnp.zeros((D_MODEL, FL_PAD - N_HEADS), BF16)], axis=1)
    w_g = w_in[:, 2048 + N_HEADS :]
    b_pad = jnp.concatenate([b_forget.reshape(1, N_HEADS), jnp.zeros((1, FL_PAD - N_HEADS), F32)], axis=1)
    mix_b = pool_mix.reshape(len(POOL_WINDOWS), GROUP_DIM, GROUP_DIM).astype(BF16)
    scale = pool_scale.reshape(1, POOL_WIDTH)
    g1 = g1.reshape(1, D_MODEL)
    g2 = g2.reshape(1, D_MODEL)
    gf = gf.reshape(1, D_MODEL)

    h, u, qkv, fl, gates = _in_proj(x2, g1, w_uqkv, w_fl, w_g, fwd_token)
    fcol = _forget_fwd(fl, b_pad, n_seq, S)
    pm, p2, p3, pool_y = _pool_fwd(u, mix_b, scale, w_po, n_seq, S)
    a, lse = _attn_fwd(qkv, fcol, n_seq, S)
    merged, x1, attn_y = _mix_out(a, pool_y, gates, x2, w_ao, w_out)
    w_gate, w_up, w_down = ffn_weights(x1)
    h2, gate, up, act, dx2, loss_rows, dgf = _ffn_fwd(x1, g2, gf, tg2, w_gate, w_up, w_down)

    dgate, dup, dx1, dg2 = _ffn_bwd(dx2, gate, up, x1, g2, w_gate, w_up, w_down)
    bwd_token = ffn_grads_out(_matmul_tn(h2, dgate, "dw_ffn_gate"), _matmul_tn(h2, dup, "dw_ffn_up"), _matmul_tn(act, dx2, "dw_ffn_down"))
    dgates, dpy, day, da, dp2, dscale = _mix_bwd(dx1, gates, pool_y, attn_y, p2, scale, w_out, w_ao, w_po, bwd_token)
    du, dmix = _pool_bwd(dp2, pm, mix_b, n_seq, S)
    dq, dk, dv, dfk, dfq = _attn_bwd(qkv, da, a, fcol, lse, n_seq, S)
    dfl, db = _forget_bwd(dfk, dfq, fl, b_pad, n_seq, S)
    dx, dg1 = _in_proj_bwd(du, dq, dk, dv, dfl, dgates, x2, dx1, g1, w_uqkv, w_fl, w_g)

    d_w_in = jnp.concatenate(
        [
            _matmul_tn(h, du, "dw_u"), _matmul_tn(h, dq, "dw_q"), _matmul_tn(h, dk, "dw_k"), _matmul_tn(h, dv, "dw_v"),
            _matmul_tn(h, dfl, "dw_fl")[:, :N_HEADS], _matmul_tn(h, dgates, "dw_gates"),
        ],
        axis=1,
    )
    sends = [
        _send_from_cols(d_w_in),
        _send_from_cols(_matmul_tn(p3, dpy, "dw_pool_out")),
        _send_from_cols(_matmul_tn(a, day, "dw_attn_out")),
        _matmul_tn(merged, dx1, "dw_out", row_sharded=True),
    ]
    small = (dg1, dg2, dgf, db[:, :N_HEADS], dscale, dmix)
    return loss_rows, dx.reshape(n_seq, S, D_MODEL), sends, small


def kernel(x, norm1_g, w_in, b_forget, pool_mix, pool_scale, w_pool_out, w_attn_out, w_out, norm2_g, w_ffn_gate, w_ffn_up, w_ffn_down, norm_f_g, loss_target, m_norm1_g, m_w_in, m_b_forget, m_pool_mix, m_pool_scale, m_w_pool_out, m_w_attn_out, m_w_out, m_norm2_g, m_w_ffn_gate, m_w_ffn_up, m_w_ffn_down, m_norm_f_g, v_norm1_g, v_w_in, v_b_forget, v_pool_mix, v_pool_scale, v_w_pool_out, v_w_attn_out, v_w_out, v_norm2_g, v_w_ffn_gate, v_w_ffn_up, v_w_ffn_down, v_norm_f_g):
    names = ("w_in", "w_pool_out", "w_attn_out", "w_out", "w_ffn_gate", "w_ffn_up", "w_ffn_down")
    w_sh = (w_in, w_pool_out, w_attn_out, w_out, w_ffn_gate, w_ffn_up, w_ffn_down)
    m_sh = (m_w_in, m_w_pool_out, m_w_attn_out, m_w_out, m_w_ffn_gate, m_w_ffn_up, m_w_ffn_down)
    v_sh = (v_w_in, v_w_pool_out, v_w_attn_out, v_w_out, v_w_ffn_gate, v_w_ffn_up, v_w_ffn_down)

    cx, cy, cc = _position()
    me = 4 * cx + 2 * cy + cc
    n_mix = 4
    shards = [w[0].astype(BF16) for w in w_sh]

    gathered = _all_gather(shards[:n_mix], "mixer_weights_all_gather")
    whole = [_full_from_gathered(t, axis) for t, axis in zip(gathered, _SHARD_AXIS)]
    ffn_sems = _exchange_start(shards[n_mix:], gathered[0], "ffn_weights_gather_start", scatter=False)

    def with_own(lands, own):
        return [lax.dynamic_update_slice(l, o[None], (me, 0, 0)) for l, o in zip(lands, own)]

    def ffn_weights(after):
        send_sems, recv_sems, srcs, lands, _ = ffn_sems
        srcs, lands = _exchange_wait(send_sems, recv_sems, srcs, lands, after, "ffn_weights_gather_wait", scatter=False)
        return [_full_from_gathered(t, axis) for t, axis in zip(with_own(lands, srcs), _SHARD_AXIS[n_mix:])]

    rs_ffn = []

    def ffn_grads_out(d_gate, d_up, d_down):
        chunks = [_chunks_from_cols(d_gate), _chunks_from_cols(d_up), d_down.reshape(N_DEV, -1, d_down.shape[1])]
        rs_ffn.append(_exchange_start(chunks, jnp.zeros((8, LANES), F32), "ffn_grads_scatter_start", scatter=True))
        return rs_ffn[0][4]

    loss_rows, grad_x, sends, small = _local_grads(
        x, loss_target, norm1_g, norm2_g, norm_f_g, b_forget, pool_mix, pool_scale, *whole, ffn_sems[4], ffn_weights, ffn_grads_out)

    send_sems, recv_sems, srcs, lands, _ = rs_ffn[0]
    srcs, lands = _exchange_wait(send_sems, recv_sems, srcs, lands, grad_x, "ffn_grads_scatter_wait", scatter=True)
    own = [lax.dynamic_index_in_dim(s, me, 0, keepdims=False) for s in srcs]
    updates_ffn = [
        _shard_update_direct(p, w, m, v, "update_" + n)
        for p, w, m, v, n in zip(with_own(lands, own), w_sh[n_mix:], m_sh[n_mix:], v_sh[n_mix:], names[n_mix:])
    ]

    core = jnp.reshape(cc, (1,)).astype(jnp.int32)
    pos = jnp.stack([cc, 2 * cx + cy]).astype(jnp.int32)
    gots = _sibling_exchange(sends)
    pairs = [_pair_sum(s, g, core, "pair_sum_" + n) for s, g, n in zip(sends, gots, names)]
    recvs = _chip_exchange(pairs)
    updates = [
        _shard_update(s, g, r, w, m, v, pos, "update_" + n)
        for s, g, r, w, m, v, n in zip(sends, gots, recvs, w_sh, m_sh, v_sh, names)
    ]
    g_w, d_w, nm_w, nv_w = zip(*(updates + updates_ffn))

    small_w = (norm1_g, norm2_g, norm_f_g, b_forget, pool_scale, pool_mix)
    small_m = (m_norm1_g, m_norm2_g, m_norm_f_g, m_b_forget, m_pool_scale, m_pool_mix)
    small_v = (v_norm1_g, v_norm2_g, v_norm_f_g, v_b_forget, v_pool_scale, v_pool_mix)
    zero_row = jnp.zeros((8, LANES), F32)
    (parts,) = _all_gather([_pack_small(small, loss_rows)], "small_all_gather")
    g_s, d_s, nm_s, nv_s = _small_update(parts, _pack_small(small_w, zero_row), _pack_small(small_m, zero_row), _pack_small(small_v, zero_row))
    shapes = [t.shape for t in small_w]
    (g1, g2, gf, gb, gsc, gmix), loss = _unpack_small(g_s, shapes)
    (d1, d2, df, db_, dsc, dmx), _ = _unpack_small(d_s, shapes)
    (m1, m2, mf, mb, msc, mmx), _ = _unpack_small(nm_s, shapes)
    (v1, v2, vf, vb, vsc, vmx), _ = _unpack_small(nv_s, shapes)

    def ordered(n1, win, b, mix, sc, wpo, wao, wout, n2, wg, wu, wd, nf):
        return (n1, win, b, mix, sc, wpo, wao, wout, n2, wg, wu, wd, nf)

    grads = ordered(g1, g_w[0], gb, gmix, gsc, g_w[1], g_w[2], g_w[3], g2, g_w[4], g_w[5], g_w[6], gf)
    deltas = ordered(d1, d_w[0], db_, dmx, dsc, d_w[1], d_w[2], d_w[3], d2, d_w[4], d_w[5], d_w[6], df)
    new_m = ordered(m1, nm_w[0], mb, mmx, msc, nm_w[1], nm_w[2], nm_w[3], m2, nm_w[4], nm_w[5], nm_w[6], mf)
    new_v = ordered(v1, nv_w[0], vb, vmx, vsc, nv_w[1], nv_w[2], nv_w[3], v2, nv_w[4], nv_w[5], nv_w[6], vf)
    return (loss, grad_x, *grads, *deltas, *new_m, *new_v)
```

```python
import functools

import jax
import jax.numpy as jnp
from jax import lax
from jax.experimental import pallas as pl
from jax.experimental.pallas import tpu as pltpu

F32 = jnp.float32
BF16 = jnp.bfloat16
MESH = pl.DeviceIdType.MESH

D_MODEL = 1024
POOL_WINDOWS = (2, 4, 8, 16)
POOL_WIDTH = 512
GROUP_DIM = 128
ATTN_WIDTH = 512
HEAD_DIM = 64
N_HEADS = 8
N_PAIRS = 4
D_FF = 2816
RMS_EPS = 1e-6
N_DEV = 8
LANES = 128
FL_PAD = 128

ADAM_LR = 0.001
ADAM_B1 = 0.9
ADAM_B2 = 0.999
ADAM_EPS = 1e-08
ADAM_WD = 0.01
ADAM_STEP = 10

VMEM_LIMIT = 56 * 1024 * 1024
VMEM_LIMIT_MAX = 60 * 1024 * 1024
ROW_TILE = 512
ATTN_BLOCK = 512
FF_CHUNK = 256
FF_ROW_TILE = 512
DW_TOKENS = 2048


def _mm(a, b):
    return jnp.dot(a, b, preferred_element_type=F32)


def _mm_nt(a, b):
    return lax.dot_general(a, b, (((1,), (1,)), ((), ())), preferred_element_type=F32)


def _mm_tn(a, b):
    return lax.dot_general(a, b, (((0,), (0,)), ((), ())), preferred_element_type=F32)


def _sigmoid(x):
    return 1.0 / (1.0 + jnp.exp(-x))


def _params(sem, vmem=VMEM_LIMIT):
    return pltpu.CompilerParams(dimension_semantics=sem, vmem_limit_bytes=vmem)


def _const_spec(shape):
    nd = len(shape)
    return pl.BlockSpec(shape, lambda *_: (0,) * nd, pipeline_mode=pl.Buffered(1))


def _rms_fwd(x, g):
    r = lax.rsqrt(jnp.mean(x * x, axis=-1, keepdims=True) + RMS_EPS)
    xh = x * r
    return xh * g, xh, r


def _rms_bwd(dy, xh, r, g):
    dxh = dy * g
    dx = r * (dxh - xh * jnp.mean(dxh * xh, axis=-1, keepdims=True))
    return dx, dy * xh


def _in_proj(x, g1, w_uqkv, w_fl, w_g, token):
    T = x.shape[0]
    tm = ROW_TILE

    def body(x_ref, g_ref, wa_ref, wf_ref, wg_ref, token_ref, h_ref, u_ref, qkv_ref, fl_ref, gt_ref):
        h, _, _ = _rms_fwd(x_ref[...], g_ref[...])
        hb = h.astype(BF16)
        h_ref[...] = hb
        z = _mm(hb, wa_ref[...])
        u_ref[...] = z[:, :POOL_WIDTH]
        qkv_ref[...] = z[:, POOL_WIDTH:].astype(BF16)
        fl_ref[...] = _mm(hb, wf_ref[...])
        gt_ref[...] = _mm(hb, wg_ref[...]).astype(BF16)

    row = lambda n: pl.BlockSpec((tm, n), lambda i: (i, 0))
    return pl.pallas_call(
        body,
        name="in_proj",
        grid=(T // tm,),
        in_specs=[row(D_MODEL), _const_spec((1, D_MODEL)), _const_spec(w_uqkv.shape), _const_spec(w_fl.shape), _const_spec(w_g.shape), _HBM],
        out_specs=[row(D_MODEL), row(POOL_WIDTH), row(3 * ATTN_WIDTH), row(FL_PAD), row(2 * D_MODEL)],
        out_shape=[
            jax.ShapeDtypeStruct((T, D_MODEL), BF16),
            jax.ShapeDtypeStruct((T, POOL_WIDTH), F32),
            jax.ShapeDtypeStruct((T, 3 * ATTN_WIDTH), BF16),
            jax.ShapeDtypeStruct((T, FL_PAD), F32),
            jax.ShapeDtypeStruct((T, 2 * D_MODEL), BF16),
        ],
        compiler_params=_params(("parallel",)),
    )(x, g1, w_uqkv, w_fl, w_g, token)


def _log_sigmoid(x):
    return jnp.minimum(x, 0.0) - jnp.log(1.0 + jnp.exp(-jnp.abs(x)))


def _forget_fwd(fl, b_pad, n_seq, S):
    def body(fl_ref, b_ref, fcol_ref):
        lf = _log_sigmoid(fl_ref[...] + b_ref[...])
        t = lf.T
        lane = lax.broadcasted_iota(jnp.int32, t.shape, 1)
        k = 1
        while k < S:
            t = t + jnp.where(lane >= k, pltpu.roll(t, k, 1), 0.0)
            k *= 2
        fcol_ref[...] = t.T

    return pl.pallas_call(
        body,
        name="forget_fwd",
        grid=(n_seq,),
        in_specs=[pl.BlockSpec((S, FL_PAD), lambda s: (s, 0)), _const_spec((1, FL_PAD))],
        out_specs=pl.BlockSpec((S, FL_PAD), lambda s: (s, 0)),
        out_shape=jax.ShapeDtypeStruct((n_seq * S, FL_PAD), F32),
        compiler_params=_params(("parallel",)),
    )(fl, b_pad)


def _window_pick(g, v2, v4, v8, v16):
    return jnp.where(g == 0, v2, jnp.where(g == 1, v4, jnp.where(g == 2, v8, v16)))


def _pool_fwd(u, mix_b, scale, n_seq, S):
    T = n_seq * S

    def body(u_ref, mix_ref, sc_ref, pm_ref, p2_ref, p3_ref):
        g = pl.program_id(1)
        uu = u_ref[...]
        row = lax.broadcasted_iota(jnp.int32, uu.shape, 0)

        def back(a, k):
            return jnp.where(row >= k, pltpu.roll(a, k, 0), 0.0)

        s2 = uu + back(uu, 1)
        s4 = s2 + back(s2, 2)
        s8 = s4 + back(s4, 4)
        s16 = s8 + back(s8, 8)
        w = _window_pick(g, 2.0, 4.0, 8.0, 16.0)
        cnt = jnp.minimum((row + 1).astype(F32), w)
        pm = _window_pick(g, s2, s4, s8, s16) / cnt - uu
        pmb = pm.astype(BF16)
        pm_ref[...] = pmb
        p2 = _mm(pmb, mix_ref[...])
        p2_ref[...] = p2
        p3_ref[...] = (p2 * sc_ref[...]).astype(BF16)

    grp = pl.BlockSpec((S, GROUP_DIM), lambda s, g: (s, g))
    return pl.pallas_call(
        body,
        name="pool_fwd",
        grid=(n_seq, len(POOL_WINDOWS)),
        in_specs=[
            grp,
            pl.BlockSpec((None, GROUP_DIM, GROUP_DIM), lambda s, g: (g, 0, 0)),
            pl.BlockSpec((1, GROUP_DIM), lambda s, g: (0, g)),
        ],
        out_specs=[grp, grp, grp],
        out_shape=[
            jax.ShapeDtypeStruct((T, POOL_WIDTH), BF16),
            jax.ShapeDtypeStruct((T, POOL_WIDTH), F32),
            jax.ShapeDtypeStruct((T, POOL_WIDTH), BF16),
        ],
        compiler_params=_params(("parallel", "parallel")),
    )(u, mix_b, scale)


def _split3(v):
    hi = v.astype(BF16).astype(F32)
    r = v - hi
    mid = r.astype(BF16).astype(F32)
    lo = (r - mid).astype(BF16).astype(F32)
    return hi, mid, lo


def _augment(xp, hh, first, second):
    lane = lax.broadcasted_iota(jnp.int32, (1, LANES), 1)
    head = (lane >= HEAD_DIM * hh) & (lane < HEAD_DIM * (hh + 1))
    b = HEAD_DIM * (1 - hh)
    out = jnp.where(head, xp.astype(F32), 0.0)
    for n, col in enumerate(tuple(first) + tuple(second)):
        out = jnp.where(lane == b + n, col, out)
    return out.astype(BF16)


def _attn_fwd(qkv, fcol, n_seq, S):
    T = n_seq * S
    tb = ATTN_BLOCK
    nq = S // tb
    scale = HEAD_DIM ** -0.5

    def body(q_ref, k_ref, v_ref, fc_ref, o_ref, st_ref, qa_sc, ka_sc, m_sc, l_sc, acc_sc):
        i = pl.program_id(1)
        lane = lax.broadcasted_iota(jnp.int32, (1, LANES), 1)
        low = lane < HEAD_DIM
        ones = (1.0, 1.0, 1.0)

        @pl.when(i == 0)
        def _():
            def rows_ka(r, carry):
                r0 = pl.multiple_of(r * tb, tb)
                for h in range(N_HEADS):
                    kp = k_ref[pl.ds(r0, tb), (h // 2) * LANES : (h // 2 + 1) * LANES] * scale
                    fk = fc_ref[pl.ds(r0, tb), h : h + 1]
                    ka_sc[h, pl.ds(r0, tb), :] = _augment(kp, h % 2, ones, _split3(-fk))
                return carry

            lax.fori_loop(0, nq, rows_ka, 0)

        q0 = pl.multiple_of(i * tb, tb)
        for h in range(N_HEADS):
            qp = q_ref[:, (h // 2) * LANES : (h // 2 + 1) * LANES]
            qa_sc[h] = _augment(qp, h % 2, _split3(fc_ref[pl.ds(q0, tb), h : h + 1]), ones)
        m_sc[...] = jnp.full(m_sc.shape, -jnp.inf, F32)
        l_sc[...] = jnp.zeros_like(l_sc)
        acc_sc[...] = jnp.zeros_like(acc_sc)
        causal = lax.broadcasted_iota(jnp.int32, (tb, tb), 1) <= lax.broadcasted_iota(jnp.int32, (tb, tb), 0)

        def step(j, masked):
            c0 = pl.multiple_of(j * tb, tb)
            for p in range(N_PAIRS):
                vb = v_ref[pl.ds(c0, tb), p * LANES : (p + 1) * LANES]
                pv, al = [], []
                for hh in range(2):
                    h = 2 * p + hh
                    s = _mm_nt(qa_sc[h], ka_sc[h, pl.ds(c0, tb), :])
                    if masked:
                        s = jnp.where(causal, s, -jnp.inf)
                    m_old = m_sc[h]
                    m_new = jnp.maximum(m_old, jnp.max(s, axis=1, keepdims=True))
                    alpha = jnp.exp(m_old - m_new)
                    pe = jnp.exp(s - jnp.concatenate([m_new] * (tb // LANES), axis=1))
                    l_sc[h] = alpha * l_sc[h] + jnp.sum(pe, axis=1, keepdims=True)
                    m_sc[h] = m_new
                    pv.append(_mm(pe.astype(BF16), vb))
                    al.append(alpha)
                acc_sc[p] = jnp.where(low, al[0], al[1]) * acc_sc[p] + jnp.where(low, pv[0], pv[1])

        def loop_body(j, carry):
            step(j, False)
            return carry

        lax.fori_loop(0, i, loop_body, 0)
        step(i, True)
        st = jnp.zeros((tb, LANES), F32)
        for p in range(N_PAIRS):
            lp = jnp.where(low, l_sc[2 * p], l_sc[2 * p + 1])
            o_ref[:, p * LANES : (p + 1) * LANES] = (acc_sc[p] / lp).astype(BF16)
            for h in (2 * p, 2 * p + 1):
                st = jnp.where(lane == h, m_sc[h] + jnp.log(l_sc[h]), st)
        st_ref[...] = st

    return pl.pallas_call(
        body,
        name="attn_fwd",
        grid=(n_seq, nq),
        in_specs=[
            pl.BlockSpec((tb, ATTN_WIDTH), lambda s, i: (s * nq + i, 0)),
            pl.BlockSpec((S, ATTN_WIDTH), lambda s, i: (s, 1)),
            pl.BlockSpec((S, ATTN_WIDTH), lambda s, i: (s, 2)),
            pl.BlockSpec((S, LANES), lambda s, i: (s, 0)),
        ],
        out_specs=[
            pl.BlockSpec((tb, ATTN_WIDTH), lambda s, i: (s * nq + i, 0)),
            pl.BlockSpec((tb, LANES), lambda s, i: (s * nq + i, 0)),
        ],
        out_shape=[jax.ShapeDtypeStruct((T, ATTN_WIDTH), BF16), jax.ShapeDtypeStruct((T, LANES), F32)],
        scratch_shapes=[
            pltpu.VMEM((N_HEADS, tb, LANES), BF16),
            pltpu.VMEM((N_HEADS, S, LANES), BF16),
            pltpu.VMEM((N_HEADS, tb, LANES), F32),
            pltpu.VMEM((N_HEADS, tb, LANES), F32),
            pltpu.VMEM((N_PAIRS, tb, LANES), F32),
        ],
        compiler_params=_params(("parallel", "arbitrary")),
    )(qkv, qkv, qkv, fcol)


def _mix_out(a, p3, gates, x, w_ao, w_po, w_out):
    T = x.shape[0]
    tm = ROW_TILE

    def body(a_ref, p3_ref, gt_ref, x_ref, wao_ref, wpo_ref, wout_ref, mg_ref, x1_ref, ay_ref, py_ref):
        ay = _mm(a_ref[...], wao_ref[...])
        py = _mm(p3_ref[...], wpo_ref[...])
        ay_ref[...] = ay.astype(BF16)
        py_ref[...] = py.astype(BF16)
        sp = _sigmoid(gt_ref[:, :D_MODEL].astype(F32))
        sa = _sigmoid(gt_ref[:, D_MODEL:].astype(F32))
        mb = (sp * py + sa * ay).astype(BF16)
        mg_ref[...] = mb
        x1_ref[...] = x_ref[...] + _mm(mb, wout_ref[...])

    row = lambda n: pl.BlockSpec((tm, n), lambda i: (i, 0))
    return pl.pallas_call(
        body,
        name="mix_out",
        grid=(T // tm,),
        in_specs=[
            row(ATTN_WIDTH), row(POOL_WIDTH), row(2 * D_MODEL), row(D_MODEL),
            _const_spec(w_ao.shape), _const_spec(w_po.shape), _const_spec(w_out.shape),
        ],
        out_specs=[row(D_MODEL), row(D_MODEL), row(D_MODEL), row(D_MODEL)],
        out_shape=[
            jax.ShapeDtypeStruct((T, D_MODEL), BF16), jax.ShapeDtypeStruct((T, D_MODEL), F32),
            jax.ShapeDtypeStruct((T, D_MODEL), BF16), jax.ShapeDtypeStruct((T, D_MODEL), BF16),
        ],
        compiler_params=_params(("parallel",)),
    )(a, p3, gates, x, w_ao, w_po, w_out)


def _ffn_fwd(x1, g2, gf, tgt, w_gate, w_up, w_down):
    T = x1.shape[0]
    tm = min(T, FF_ROW_TILE)
    nt = T // tm
    nc = D_FF // FF_CHUNK

    def body(x1_ref, g2_ref, gf_ref, tg_ref, wg_ref, wu_ref, wd_ref, h2_ref, gate_ref, up_ref, act_ref, dx2_ref, loss_ref, dgf_ref):
        x1v = x1_ref[...]
        h2, _, _ = _rms_fwd(x1v, g2_ref[...])
        h2b = h2.astype(BF16)
        h2_ref[...] = h2b
        acc = x1v
        for c in range(nc):
            sl = slice(c * FF_CHUNK, (c + 1) * FF_CHUNK)
            gate = _mm(h2b, wg_ref[:, sl])
            up = _mm(h2b, wu_ref[:, sl])
            gate_ref[:, sl] = gate.astype(BF16)
            up_ref[:, sl] = up.astype(BF16)
            act = (gate * _sigmoid(gate) * up).astype(BF16)
            act_ref[:, sl] = act
            acc = acc + _mm(act, wd_ref[sl, :])
        gfv = gf_ref[...]
        y, xh, r = _rms_fwd(acc, gfv)
        err = y - tg_ref[...]
        part = 0.5 * jnp.sum(jnp.mean(err * err, axis=-1, keepdims=True), axis=0, keepdims=True)
        dx2, dgrow = _rms_bwd(err * (1.0 / D_MODEL), xh, r, gfv)
        dx2_ref[...] = dx2

        @pl.when(pl.program_id(0) == 0)
        def _():
            dgf_ref[...] = jnp.zeros_like(dgf_ref)
            loss_ref[...] = jnp.zeros_like(loss_ref)

        dgf_ref[...] += jnp.sum(dgrow, axis=0, keepdims=True)
        loss_ref[...] += jnp.broadcast_to(part, loss_ref.shape)

    row = lambda n: pl.BlockSpec((tm, n), lambda i: (i, 0))
    return pl.pallas_call(
        body,
        name="ffn_fwd",
        grid=(nt,),
        in_specs=[
            row(D_MODEL), _const_spec((1, D_MODEL)), _const_spec((1, D_MODEL)), row(D_MODEL),
            _const_spec(w_gate.shape), _const_spec(w_up.shape), _const_spec(w_down.shape),
        ],
        out_specs=[
            row(D_MODEL), row(D_FF), row(D_FF), row(D_FF), row(D_MODEL),
            pl.BlockSpec((8, LANES), lambda i: (0, 0)),
            pl.BlockSpec((1, D_MODEL), lambda i: (0, 0)),
        ],
        out_shape=[
            jax.ShapeDtypeStruct((T, D_MODEL), BF16),
            jax.ShapeDtypeStruct((T, D_FF), BF16),
            jax.ShapeDtypeStruct((T, D_FF), BF16),
            jax.ShapeDtypeStruct((T, D_FF), BF16),
            jax.ShapeDtypeStruct((T, D_MODEL), F32),
            jax.ShapeDtypeStruct((8, LANES), F32),
            jax.ShapeDtypeStruct((1, D_MODEL), F32),
        ],
        compiler_params=_params(("arbitrary",)),
    )(x1, g2, gf, tgt, w_gate, w_up, w_down)


def _ffn_bwd(dx2, gate, up, x1, g2, w_gate, w_up, w_down):
    T = x1.shape[0]
    tm = min(T, FF_ROW_TILE)
    nc = D_FF // FF_CHUNK

    def body(dx2_ref, gate_ref, up_ref, x1_ref, g2_ref, wg_ref, wu_ref, wd_ref, dgate_ref, dup_ref, dx1_ref, dg2_ref):
        dx2v = dx2_ref[...]
        dx2b = dx2v.astype(BF16)
        dh2 = jnp.zeros((tm, D_MODEL), F32)
        for c in range(nc):
            sl = slice(c * FF_CHUNK, (c + 1) * FF_CHUNK)
            dact = _mm_nt(dx2b, wd_ref[sl, :])
            gate = gate_ref[:, sl].astype(F32)
            sg = _sigmoid(gate)
            silu = gate * sg
            dgate = (dact * up_ref[:, sl].astype(F32) * (sg * (1.0 + gate * (1.0 - sg)))).astype(BF16)
            dup = (dact * silu).astype(BF16)
            dgate_ref[:, sl] = dgate
            dup_ref[:, sl] = dup
            dh2 = dh2 + _mm_nt(dgate, wg_ref[:, sl]) + _mm_nt(dup, wu_ref[:, sl])
        g2v = g2_ref[...]
        _, xh, r = _rms_fwd(x1_ref[...], g2v)
        dxn, dgrow = _rms_bwd(dh2, xh, r, g2v)
        dx1_ref[...] = dx2v + dxn

        @pl.when(pl.program_id(0) == 0)
        def _():
            dg2_ref[...] = jnp.zeros_like(dg2_ref)

        dg2_ref[...] += jnp.sum(dgrow, axis=0, keepdims=True)

    row = lambda n: pl.BlockSpec((tm, n), lambda i: (i, 0))
    return pl.pallas_call(
        body,
        name="ffn_bwd",
        grid=(T // tm,),
        in_specs=[
            row(D_MODEL), row(D_FF), row(D_FF), row(D_MODEL), _const_spec((1, D_MODEL)),
            _const_spec(w_gate.shape), _const_spec(w_up.shape), _const_spec(w_down.shape),
        ],
        out_specs=[row(D_FF), row(D_FF), row(D_MODEL), pl.BlockSpec((1, D_MODEL), lambda i: (0, 0))],
        out_shape=[
            jax.ShapeDtypeStruct((T, D_FF), BF16),
            jax.ShapeDtypeStruct((T, D_FF), BF16),
            jax.ShapeDtypeStruct((T, D_MODEL), F32),
            jax.ShapeDtypeStruct((1, D_MODEL), F32),
        ],
        compiler_params=_params(("arbitrary",), VMEM_LIMIT_MAX),
    )(dx2, gate, up, x1, g2, w_gate, w_up, w_down)


def _mix_bwd(dx1, gates, pool_y, attn_y, p2, scale, w_out, w_ao, w_po, token):
    T = dx1.shape[0]
    tm = ROW_TILE

    def body(dx1_ref, gt_ref, py_ref, ay_ref, p2_ref, sc_ref, wout_ref, wao_ref, wpo_ref, token_ref, dgt_ref, dpy_ref, day_ref, da_ref, dp2_ref, dsc_ref):
        dm = _mm_nt(dx1_ref[...].astype(BF16), wout_ref[...])
        sp = _sigmoid(gt_ref[:, :D_MODEL].astype(F32))
        sa = _sigmoid(gt_ref[:, D_MODEL:].astype(F32))
        dgt_ref[:, :D_MODEL] = (dm * py_ref[...].astype(F32) * (sp * (1.0 - sp))).astype(BF16)
        dgt_ref[:, D_MODEL:] = (dm * ay_ref[...].astype(F32) * (sa * (1.0 - sa))).astype(BF16)
        dpy = (dm * sp).astype(BF16)
        day = (dm * sa).astype(BF16)
        dpy_ref[...] = dpy
        day_ref[...] = day
        da_ref[...] = _mm_nt(day, wao_ref[...]).astype(BF16)
        dp3 = _mm_nt(dpy, wpo_ref[...])
        dp2_ref[...] = (dp3 * sc_ref[...]).astype(BF16)

        @pl.when(pl.program_id(0) == 0)
        def _():
            dsc_ref[...] = jnp.zeros_like(dsc_ref)

        dsc_ref[...] += jnp.sum(dp3 * p2_ref[...], axis=0, keepdims=True)

    row = lambda n: pl.BlockSpec((tm, n), lambda i: (i, 0))
    return pl.pallas_call(
        body,
        name="mix_bwd",
        grid=(T // tm,),
        in_specs=[
            row(D_MODEL), row(2 * D_MODEL), row(D_MODEL), row(D_MODEL), row(POOL_WIDTH), _const_spec((1, POOL_WIDTH)),
            _const_spec(w_out.shape), _const_spec(w_ao.shape), _const_spec(w_po.shape), _HBM,
        ],
        out_specs=[row(2 * D_MODEL), row(D_MODEL), row(D_MODEL), row(ATTN_WIDTH), row(POOL_WIDTH), pl.BlockSpec((1, POOL_WIDTH), lambda i: (0, 0))],
        out_shape=[
            jax.ShapeDtypeStruct((T, 2 * D_MODEL), BF16),
            jax.ShapeDtypeStruct((T, D_MODEL), BF16),
            jax.ShapeDtypeStruct((T, D_MODEL), BF16),
            jax.ShapeDtypeStruct((T, ATTN_WIDTH), BF16),
            jax.ShapeDtypeStruct((T, POOL_WIDTH), BF16),
            jax.ShapeDtypeStruct((1, POOL_WIDTH), F32),
        ],
        compiler_params=_params(("arbitrary",)),
    )(dx1, gates, pool_y, attn_y, p2, scale, w_out, w_ao, w_po, token)


def _pool_bwd(dp2, pm, mix_b, token, n_seq, S):
    T = n_seq * S

    def body(dp2_ref, pm_ref, mix_ref, token_ref, du_ref, dmix_ref):
        g = pl.program_id(0)
        dp2v = dp2_ref[...]
        dpm = _mm_nt(dp2v, mix_ref[...])
        row = lax.broadcasted_iota(jnp.int32, dpm.shape, 0)
        w = _window_pick(g, 2.0, 4.0, 8.0, 16.0)
        e = dpm / jnp.minimum((row + 1).astype(F32), w)

        def ahead(a, k):
            return jnp.where(row < S - k, pltpu.roll(a, S - k, 0), 0.0)

        r2 = e + ahead(e, 1)
        r4 = r2 + ahead(r2, 2)
        r8 = r4 + ahead(r4, 4)
        r16 = r8 + ahead(r8, 8)
        du_ref[...] = (_window_pick(g, r2, r4, r8, r16) - dpm).astype(BF16)

        @pl.when(pl.program_id(1) == 0)
        def _():
            dmix_ref[...] = jnp.zeros_like(dmix_ref)

        dmix_ref[...] += _mm_tn(pm_ref[...], dp2v)

    grp = pl.BlockSpec((S, GROUP_DIM), lambda g, s: (s, g))
    mixs = pl.BlockSpec((None, GROUP_DIM, GROUP_DIM), lambda g, s: (g, 0, 0))
    return pl.pallas_call(
        body,
        name="pool_bwd",
        grid=(len(POOL_WINDOWS), n_seq),
        in_specs=[grp, grp, mixs, _HBM],
        out_specs=[grp, mixs],
        out_shape=[jax.ShapeDtypeStruct((T, POOL_WIDTH), BF16), jax.ShapeDtypeStruct((len(POOL_WINDOWS), GROUP_DIM, GROUP_DIM), F32)],
        compiler_params=_params(("parallel", "arbitrary")),
    )(dp2, pm, mix_b, token)


def _attn_bwd(qkv, da, a, fcol, lse, n_seq, S):
    T = n_seq * S
    tb = ATTN_BLOCK
    nb = S // tb
    scale = HEAD_DIM ** -0.5

    def body(q_ref, k_ref, v_ref, do_ref, o_ref, fc_ref, st_ref, dq_ref, dk_ref, dv_ref, dfk_ref, dfq_ref,
             qa_sc, doa_sc, dq_acc, ka_sc, va_sc, dk_sc, dv_sc):
        j = pl.program_id(1)
        lane = lax.broadcasted_iota(jnp.int32, (1, LANES), 1)
        low = lane < HEAD_DIM
        ones = (1.0, 1.0, 1.0)
        zeros = (0.0, 0.0, 0.0)

        @pl.when(j == 0)
        def _():
            dq_acc[...] = jnp.zeros_like(dq_acc)

            def rows_q(i, carry):
                r0 = pl.multiple_of(i * tb, tb)
                for h in range(N_HEADS):
                    pair = slice((h // 2) * LANES, (h // 2 + 1) * LANES)
                    qp = q_ref[pl.ds(r0, tb), pair]
                    dop = do_ref[pl.ds(r0, tb), pair]
                    prod = dop.astype(F32) * o_ref[pl.ds(r0, tb), pair].astype(F32)
                    head = (lane >= HEAD_DIM * (h % 2)) & (lane < HEAD_DIM * (h % 2 + 1))
                    delta = jnp.sum(jnp.where(head, prod, 0.0), axis=1, keepdims=True)
                    cq = fc_ref[pl.ds(r0, tb), h : h + 1] - st_ref[pl.ds(r0, tb), h : h + 1]
                    qa_sc[h, pl.ds(r0, tb), :] = _augment(qp, h % 2, _split3(cq), ones)
                    doa_sc[h, pl.ds(r0, tb), :] = _augment(dop, h % 2, _split3(-delta), zeros)
                return carry

            lax.fori_loop(0, nb, rows_q, 0)

        c0 = pl.multiple_of(j * tb, tb)
        for h in range(N_HEADS):
            pair = slice((h // 2) * LANES, (h // 2 + 1) * LANES)
            kp = k_ref[:, pair] * scale
            ka_sc[h] = _augment(kp, h % 2, ones, _split3(-fc_ref[pl.ds(c0, tb), h : h + 1]))
            va_sc[h] = _augment(v_ref[:, pair], h % 2, ones, zeros)
        dk_sc[...] = jnp.zeros_like(dk_sc)
        dv_sc[...] = jnp.zeros_like(dv_sc)
        causal = lax.broadcasted_iota(jnp.int32, (tb, tb), 1) <= lax.broadcasted_iota(jnp.int32, (tb, tb), 0)

        def step(i, masked):
            r0 = pl.multiple_of(i * tb, tb)
            for h in range(N_HEADS):
                dob = do_ref[pl.ds(r0, tb), (h // 2) * LANES : (h // 2 + 1) * LANES]
                qa = qa_sc[h, pl.ds(r0, tb), :]
                s = _mm_nt(qa, ka_sc[h])
                if masked:
                    s = jnp.where(causal, s, -jnp.inf)
                pr = jnp.exp(s)
                dv_sc[h] += _mm_tn(pr.astype(BF16), dob)
                dsb = (pr * _mm_nt(doa_sc[h, pl.ds(r0, tb), :], va_sc[h])).astype(BF16)
                dk_sc[h] += _mm_tn(dsb, qa)
                dq_acc[h, pl.ds(r0, tb), :] += _mm(dsb, ka_sc[h])

        step(j, True)

        def loop_body(i, carry):
            step(i, False)
            return carry

        lax.fori_loop(j + 1, nb, loop_body, 0)
        dfk = jnp.zeros((tb, LANES), F32)
        for p in range(N_PAIRS):
            dk_ref[:, p * LANES : (p + 1) * LANES] = (jnp.where(low, dk_sc[2 * p], dk_sc[2 * p + 1]) * scale).astype(BF16)
            dv_ref[:, p * LANES : (p + 1) * LANES] = jnp.where(low, dv_sc[2 * p], dv_sc[2 * p + 1]).astype(BF16)
            for hh in range(2):
                b = HEAD_DIM * (1 - hh) + 3
                dfk = jnp.where(lane == 2 * p + hh, -dk_sc[2 * p + hh][:, b : b + 1], dfk)
        dfk_ref[...] = dfk

        @pl.when(j == nb - 1)
        def _():
            def rows_dq(i, carry):
                r0 = pl.multiple_of(i * tb, tb)
                dfq = jnp.zeros((tb, LANES), F32)
                for p in range(N_PAIRS):
                    parts = [dq_acc[2 * p + hh, pl.ds(r0, tb), :] for hh in range(2)]
                    dq_ref[pl.ds(r0, tb), p * LANES : (p + 1) * LANES] = jnp.where(low, parts[0], parts[1]).astype(BF16)
                    for hh in range(2):
                        b = HEAD_DIM * (1 - hh)
                        dfq = jnp.where(lane == 2 * p + hh, parts[hh][:, b : b + 1], dfq)
                dfq_ref[pl.ds(r0, tb), :] = dfq
                return carry

            lax.fori_loop(0, nb, rows_dq, 0)

    seq = lambda w, col: pl.BlockSpec((S, w), lambda s, j: (s, col))
    blk = lambda w, col: pl.BlockSpec((tb, w), lambda s, j: (s * nb + j, col))
    return pl.pallas_call(
        body,
        name="attn_bwd",
        grid=(n_seq, nb),
        in_specs=[seq(ATTN_WIDTH, 0), blk(ATTN_WIDTH, 1), blk(ATTN_WIDTH, 2), seq(ATTN_WIDTH, 0), seq(ATTN_WIDTH, 0), seq(LANES, 0), seq(LANES, 0)],
        out_specs=[seq(ATTN_WIDTH, 0), blk(ATTN_WIDTH, 0), blk(ATTN_WIDTH, 0), blk(LANES, 0), seq(LANES, 0)],
        out_shape=[
            jax.ShapeDtypeStruct((T, ATTN_WIDTH), BF16),
            jax.ShapeDtypeStruct((T, ATTN_WIDTH), BF16),
            jax.ShapeDtypeStruct((T, ATTN_WIDTH), BF16),
            jax.ShapeDtypeStruct((T, LANES), F32),
            jax.ShapeDtypeStruct((T, LANES), F32),
        ],
        scratch_shapes=[
            pltpu.VMEM((N_HEADS, S, LANES), BF16),
            pltpu.VMEM((N_HEADS, S, LANES), BF16),
            pltpu.VMEM((N_HEADS, S, LANES), F32),
            pltpu.VMEM((N_HEADS, tb, LANES), BF16),
            pltpu.VMEM((N_HEADS, tb, LANES), BF16),
            pltpu.VMEM((N_HEADS, tb, LANES), F32),
            pltpu.VMEM((N_HEADS, tb, LANES), F32),
        ],
        compiler_params=_params(("parallel", "arbitrary")),
    )(qkv, qkv, qkv, da, a, fcol, lse)


def _forget_bwd(dfk, dfq, fl, b_pad, n_seq, S):
    def body(df_ref, dfq_ref, fl_ref, b_ref, dfl_ref, db_ref):
        t = (df_ref[...] + dfq_ref[...]).T
        lane = lax.broadcasted_iota(jnp.int32, t.shape, 1)
        k = 1
        while k < S:
            t = t + jnp.where(lane < S - k, pltpu.roll(t, S - k, 1), 0.0)
            k *= 2
        dfl = t.T * _sigmoid(-(fl_ref[...] + b_ref[...]))
        dfl_ref[...] = dfl.astype(BF16)

        @pl.when(pl.program_id(0) == 0)
        def _():
            db_ref[...] = jnp.zeros_like(db_ref)

        db_ref[...] += jnp.sum(dfl, axis=0, keepdims=True)

    return pl.pallas_call(
        body,
        name="forget_bwd",
        grid=(n_seq,),
        in_specs=[
            pl.BlockSpec((S, LANES), lambda s: (s, 0)),
            pl.BlockSpec((S, LANES), lambda s: (s, 0)),
            pl.BlockSpec((S, FL_PAD), lambda s: (s, 0)),
            _const_spec((1, FL_PAD)),
        ],
        out_specs=[pl.BlockSpec((S, FL_PAD), lambda s: (s, 0)), pl.BlockSpec((1, FL_PAD), lambda s: (0, 0))],
        out_shape=[jax.ShapeDtypeStruct((n_seq * S, FL_PAD), BF16), jax.ShapeDtypeStruct((1, FL_PAD), F32)],
        compiler_params=_params(("arbitrary",)),
    )(dfk, dfq, fl, b_pad)


def _in_proj_bwd(du, dq, dk, dv, dfl, dgates, x, dx1, g1, w_uqkv, w_fl, w_g):
    T = x.shape[0]
    tm = ROW_TILE

    def body(du_ref, dq_ref, dk_ref, dv_ref, dfl_ref, dgt_ref, x_ref, dx1_ref, g_ref, wa_ref, wf_ref, wg_ref, dx_ref, dg_ref):
        dh = _mm_nt(dgt_ref[...], wg_ref[...]) + _mm_nt(dfl_ref[...], wf_ref[...])
        for n, ref in enumerate((du_ref, dq_ref, dk_ref, dv_ref)):
            dh = dh + _mm_nt(ref[...], wa_ref[:, n * 512 : (n + 1) * 512])
        gv = g_ref[...]
        _, xh, r = _rms_fwd(x_ref[...], gv)
        dxn, dgrow = _rms_bwd(dh, xh, r, gv)
        dx_ref[...] = dx1_ref[...] + dxn

        @pl.when(pl.program_id(0) == 0)
        def _():
            dg_ref[...] = jnp.zeros_like(dg_ref)

        dg_ref[...] += jnp.sum(dgrow, axis=0, keepdims=True)

    row = lambda n: pl.BlockSpec((tm, n), lambda i: (i, 0))
    return pl.pallas_call(
        body,
        name="in_proj_bwd",
        grid=(T // tm,),
        in_specs=[
            row(512), row(512), row(512), row(512), row(FL_PAD), row(2 * D_MODEL), row(D_MODEL), row(D_MODEL), _const_spec((1, D_MODEL)),
            _const_spec(w_uqkv.shape), _const_spec(w_fl.shape), _const_spec(w_g.shape),
        ],
        out_specs=[row(D_MODEL), pl.BlockSpec((1, D_MODEL), lambda i: (0, 0))],
        out_shape=[jax.ShapeDtypeStruct((T, D_MODEL), F32), jax.ShapeDtypeStruct((1, D_MODEL), F32)],
        compiler_params=_params(("arbitrary",)),
    )(du, dq, dk, dv, dfl, dgates, x, dx1, g1, w_uqkv, w_fl, w_g)


def _pick_block(n):
    for b in (512, 1408, 256, 128):
        if n % b == 0:
            return b
    raise ValueError(n)


def _matmul_tn(a, b, name, row_sharded=False):
    T, K = a.shape
    N = b.shape[1]
    bt, bk, bn = min(T, DW_TOKENS), _pick_block(K), _pick_block(N)
    nt = T // bt
    r = K // N_DEV
    assert not row_sharded or bk == 4 * r

    def body(a_ref, b_ref, o_ref, acc):
        @pl.when(pl.program_id(2) == 0)
        def _():
            acc[...] = jnp.zeros_like(acc)

        acc[...] += _mm_tn(a_ref[...].astype(BF16), b_ref[...].astype(BF16))

        @pl.when(pl.program_id(2) == nt - 1)
        def _():
            if row_sharded:
                for chip in range(2):
                    for core in range(2):
                        d = 2 * chip + core
                        o_ref[core, chip] = acc[d * r : (d + 1) * r, :].astype(BF16)
            else:
                o_ref[...] = acc[...].astype(BF16)

    if row_sharded:
        out_spec = pl.BlockSpec((2, 2, r, bn), lambda k, n, t: (0, k, 0, n))
        out_shape = jax.ShapeDtypeStruct((2, 4, r, N), BF16)
    else:
        out_spec = pl.BlockSpec((bk, bn), lambda k, n, t: (k, n))
        out_shape = jax.ShapeDtypeStruct((K, N), BF16)
    return pl.pallas_call(
        body,
        name=name,
        grid=(K // bk, N // bn, nt),
        in_specs=[pl.BlockSpec((bt, bk), lambda k, n, t: (t, k)), pl.BlockSpec((bt, bn), lambda k, n, t: (t, n))],
        out_specs=out_spec,
        out_shape=out_shape,
        scratch_shapes=[pltpu.VMEM((bk, bn), F32)],
        compiler_params=_params(("parallel", "parallel", "arbitrary")),
    )(a, b)


def _position():
    return lax.axis_index("x"), lax.axis_index("y"), lax.axis_index("c")


_HBM = pl.BlockSpec(memory_space=pl.ANY)


def _all_gather(blocks, name):
    n = len(blocks)

    def body(*refs):
        xs, outs = refs[:n], refs[n : 2 * n]
        send_sems, recv_sems, local_sems = refs[2 * n :]
        x, y, c = _position()
        me, sibling = (x, y, c), (x, y, 1 - c)
        chips = [(1 - x, y), (x, 1 - y), (1 - x, 1 - y)]

        def rows(a, px, py, pc):
            return outs[a].at[4 * px + 2 * py + pc]

        def copy(a, k, blk, to, src=None):
            return pltpu.make_async_remote_copy(
                src_ref=rows(a, *blk) if src is None else src, dst_ref=rows(a, *blk),
                send_sem=send_sems.at[7 * a + k], recv_sem=recv_sems.at[7 * a + k], device_id=to, device_id_type=MESH,
            )

        mine = [pltpu.make_async_copy(xs[a], rows(a, *me), local_sems.at[a]) for a in range(n)]
        for cp in mine:
            cp.start()
        first = []
        for a in range(n):
            first.append(copy(a, 0, me, sibling, src=xs[a]))
            first += [copy(a, 1 + j, me, (*chip, c), src=xs[a]) for j, chip in enumerate(chips)]
        for cp in first:
            cp.start()
        passed = []
        for j, chip in enumerate(chips):
            for a in range(n):
                copy(a, 1 + j, (*chip, c), me).wait_recv()
                passed.append(copy(a, 4 + j, (*chip, c), sibling))
                passed[-1].start()
        for a in range(n):
            copy(a, 0, sibling, me).wait_recv()
        for j, chip in enumerate(chips):
            for a in range(n):
                copy(a, 4 + j, (*chip, 1 - c), me).wait_recv()
        for cp in first + passed:
            cp.wait_send()
        for cp in mine:
            cp.wait()

    return pl.pallas_call(
        body,
        name=name,
        out_shape=[jax.ShapeDtypeStruct((N_DEV, *b.shape), b.dtype) for b in blocks],
        in_specs=[_HBM] * n,
        out_specs=[_HBM] * n,
        scratch_shapes=[pltpu.SemaphoreType.DMA((7 * n,)), pltpu.SemaphoreType.DMA((7 * n,)), pltpu.SemaphoreType.DMA((n,))],
    )(*blocks)


_SEM = pl.BlockSpec(memory_space=pltpu.SEMAPHORE)
_HBM_ONLY = pl.BlockSpec(memory_space=pltpu.HBM)
_SIDE_EFFECT = pltpu.SideEffectType.DATAFLOW_SIDE_EFFECTING


def _peer(x, y, c, k):
    return (1 - x if k & 4 else x, 1 - y if k & 2 else y, 1 - c if k & 1 else c)


def _exchange_copies(src_refs, land_refs, send_sems, recv_sems, scatter, receive_side):
    x, y, c = _position()
    me = 4 * x + 2 * y + c
    cps = []
    for k in range(1, N_DEV):
        px, py, pc = _peer(x, y, c, k)
        peer = 4 * px + 2 * py + pc
        for a, (src, land) in enumerate(zip(src_refs, land_refs)):
            cps.append(pltpu.make_async_remote_copy(
                src_ref=src.at[peer] if scatter else src, dst_ref=land.at[peer if receive_side else me],
                send_sem=send_sems.at[7 * a + k - 1], recv_sem=recv_sems.at[7 * a + k - 1],
                device_id=(px, py, pc), device_id_type=MESH,
            ))
    return cps


def _exchange_start(srcs, after, name, scatter):
    n = len(srcs)
    lands = [jax.ShapeDtypeStruct((N_DEV, *s.shape[-2:]), s.dtype) for s in srcs]

    def body(*refs):
        src_refs, land_refs = refs[1 : 1 + n], refs[1 + n : 1 + 2 * n]
        send_sems, recv_sems = refs[1 + 2 * n], refs[2 + 2 * n]
        token = refs[-1]
        for cp in _exchange_copies(src_refs, land_refs, send_sems, recv_sems, scatter, receive_side=False):
            cp.start()
        token[...] = jnp.zeros_like(token)

    hbm = lambda t: pltpu.with_memory_space_constraint(t, pltpu.HBM)
    out = pl.pallas_call(
        body,
        name=name,
        out_shape=(
            pltpu.SemaphoreType.DMA((7 * n,)), pltpu.SemaphoreType.DMA((7 * n,)),
            *[pltpu.HBM(s.shape, s.dtype) for s in srcs], *[pltpu.HBM(l.shape, l.dtype) for l in lands],
            jax.ShapeDtypeStruct((8, LANES), F32),
        ),
        in_specs=(_HBM, *[_HBM_ONLY] * (2 * n)),
        out_specs=(_SEM, _SEM, *[_HBM_ONLY] * (2 * n), pl.BlockSpec(memory_space=pltpu.VMEM)),
        input_output_aliases={1 + i: 2 + i for i in range(2 * n)},
        compiler_params=pltpu.CompilerParams(has_side_effects=_SIDE_EFFECT),
    )(after, *[hbm(s) for s in srcs], *[hbm(lax.empty(l.shape, l.dtype)) for l in lands])
    return out[0], out[1], out[2 : 2 + n], out[2 + n : 2 + 2 * n], out[-1]


def _exchange_wait(send_sems, recv_sems, srcs, lands, after, name, scatter):
    n = len(srcs)

    def body(*refs):
        src_refs, land_refs = refs[:n], refs[n : 2 * n]
        for cp in _exchange_copies(src_refs, land_refs, refs[2 * n], refs[2 * n + 1], scatter, receive_side=True):
            cp.wait_send()
            cp.wait_recv()

    out = pl.pallas_call(
        body,
        name=name,
        out_shape=(*[pltpu.HBM(s.shape, s.dtype) for s in srcs], *[pltpu.HBM(l.shape, l.dtype) for l in lands]),
        in_specs=(*[_HBM_ONLY] * (2 * n), _SEM, _SEM, _HBM),
        out_specs=tuple([_HBM_ONLY] * (2 * n)),
        input_output_aliases={i: i for i in range(2 * n)},
        compiler_params=pltpu.CompilerParams(has_side_effects=_SIDE_EFFECT),
    )(*srcs, *lands, send_sems, recv_sems, after)
    return out[:n], out[n:]


def _sibling_exchange(sends):
    n = len(sends)

    def body(*refs):
        srcs, dsts = refs[:n], refs[n : 2 * n]
        send_sems, recv_sems = refs[2 * n :]
        x, y, c = _position()
        cps = [
            pltpu.make_async_remote_copy(
                src_ref=srcs[a].at[1 - c], dst_ref=dsts[a], send_sem=send_sems.at[a], recv_sem=recv_sems.at[a],
                device_id=(x, y, 1 - c), device_id_type=MESH,
            )
            for a in range(n)
        ]
        for cp in cps:
            cp.start()
        for cp in cps:
            cp.wait()

    return pl.pallas_call(
        body,
        name="rs_sibling",
        out_shape=[jax.ShapeDtypeStruct(s.shape[1:], s.dtype) for s in sends],
        in_specs=[_HBM] * n,
        out_specs=[_HBM] * n,
        scratch_shapes=[pltpu.SemaphoreType.DMA((n,)), pltpu.SemaphoreType.DMA((n,))],
    )(*sends)


def _rows_tile(r):
    return ROW_TILE if r % ROW_TILE == 0 else r


def _pair_sum(send, got, core, name):
    _, _, r, c = send.shape
    br = _rows_tile(r)

    def body(core_ref, a_ref, b_ref, o_ref):
        o_ref[...] = (a_ref[...].astype(F32) + b_ref[...].astype(F32)).astype(o_ref.dtype)

    return pl.pallas_call(
        body,
        name=name,
        grid_spec=pltpu.PrefetchScalarGridSpec(
            num_scalar_prefetch=1,
            grid=(4, r // br),
            in_specs=[
                pl.BlockSpec((None, None, br, c), lambda n, i, core: (core[0], n, i, 0)),
                pl.BlockSpec((None, br, c), lambda n, i, core: (n, i, 0)),
            ],
            out_specs=pl.BlockSpec((None, br, c), lambda n, i, core: (n, i, 0)),
        ),
        out_shape=jax.ShapeDtypeStruct((4, r, c), send.dtype),
        compiler_params=_params(("parallel", "parallel")),
    )(core, send, got)


def _chip_exchange(pairs):
    n = len(pairs)

    def body(*refs):
        srcs, dsts = refs[:n], refs[n : 2 * n]
        send_sems, recv_sems = refs[2 * n :]
        x, y, c = _position()
        chips = [(1 - x, y), (x, 1 - y), (1 - x, 1 - y)]
        cps = [
            pltpu.make_async_remote_copy(
                src_ref=srcs[a].at[2 * cx + cy], dst_ref=dsts[a].at[j], send_sem=send_sems.at[3 * a + j], recv_sem=recv_sems.at[3 * a + j],
                device_id=(cx, cy, c), device_id_type=MESH,
            )
            for a in range(n)
            for j, (cx, cy) in enumerate(chips)
        ]
        for cp in cps:
            cp.start()
        for cp in cps:
            cp.wait()

    return pl.pallas_call(
        body,
        name="rs_chips",
        out_shape=[jax.ShapeDtypeStruct((3, *p.shape[1:]), p.dtype) for p in pairs],
        in_specs=[_HBM] * n,
        out_specs=[_HBM] * n,
        scratch_shapes=[pltpu.SemaphoreType.DMA((3 * n,)), pltpu.SemaphoreType.DMA((3 * n,))],
    )(*pairs)


def _adamw(w, g, m, v):
    m = ADAM_B1 * m + (1.0 - ADAM_B1) * g
    v = ADAM_B2 * v + (1.0 - ADAM_B2) * (g * g)
    m_hat = m / (1.0 - ADAM_B1 ** ADAM_STEP)
    v_hat = v / (1.0 - ADAM_B2 ** ADAM_STEP)
    delta = -ADAM_LR * (m_hat / (jnp.sqrt(v_hat) + ADAM_EPS) + ADAM_WD * w)
    return delta, m, v


def _shard_update(send, got, recv, w, m, v, pos, name):
    _, r, c = w.shape
    br = _rows_tile(r)

    def body(pos_ref, a_ref, b_ref, r_ref, w_ref, m_ref, v_ref, g_ref, d_ref, nm_ref, nv_ref):
        g = a_ref[...].astype(F32) + b_ref[...].astype(F32)
        for n in range(3):
            g = g + r_ref[n].astype(F32)
        g_ref[...] = g
        d_ref[...], nm_ref[...], nv_ref[...] = _adamw(w_ref[...], g, m_ref[...], v_ref[...])

    own = pl.BlockSpec((None, br, c), lambda i, pos: (0, i, 0))
    return pl.pallas_call(
        body,
        name=name,
        grid_spec=pltpu.PrefetchScalarGridSpec(
            num_scalar_prefetch=1,
            grid=(r // br,),
            in_specs=[
                pl.BlockSpec((None, None, br, c), lambda i, pos: (pos[0], pos[1], i, 0)),
                pl.BlockSpec((None, br, c), lambda i, pos: (pos[1], i, 0)),
                pl.BlockSpec((3, br, c), lambda i, pos: (0, i, 0)),
                own, own, own,
            ],
            out_specs=[own, own, own, own],
        ),
        out_shape=[jax.ShapeDtypeStruct((1, r, c), F32)] * 4,
        compiler_params=_params(("parallel",)),
    )(pos, send, got, recv, w, m, v)


def _shard_update_direct(parts, w, m, v, name):
    _, r, c = w.shape
    br = _rows_tile(r)

    def body(p_ref, w_ref, m_ref, v_ref, g_ref, d_ref, nm_ref, nv_ref):
        g = p_ref[0].astype(F32)
        for n in range(1, N_DEV):
            g = g + p_ref[n].astype(F32)
        g_ref[...] = g
        d_ref[...], nm_ref[...], nv_ref[...] = _adamw(w_ref[...], g, m_ref[...], v_ref[...])

    own = pl.BlockSpec((None, br, c), lambda i: (0, i, 0))
    return pl.pallas_call(
        body,
        name=name,
        grid=(r // br,),
        in_specs=[pl.BlockSpec((N_DEV, br, c), lambda i: (0, i, 0)), own, own, own],
        out_specs=[own, own, own, own],
        out_shape=[jax.ShapeDtypeStruct((1, r, c), F32)] * 4,
        compiler_params=_params(("parallel",)),
    )(parts, w, m, v)


def _small_update(parts, w, m, v):
    R = w.shape[0]

    def body(p_ref, w_ref, m_ref, v_ref, g_ref, d_ref, nm_ref, nv_ref):
        g = p_ref[0]
        for n in range(1, N_DEV):
            g = g + p_ref[n]
        g_ref[...] = g
        d_ref[...], nm_ref[...], nv_ref[...] = _adamw(w_ref[...], g, m_ref[...], v_ref[...])

    return pl.pallas_call(
        body,
        name="small_update",
        out_shape=[jax.ShapeDtypeStruct((R, LANES), F32)] * 4,
        compiler_params=pltpu.CompilerParams(vmem_limit_bytes=VMEM_LIMIT),
    )(parts, w, m, v)


_SHARD_AXIS = (1, 1, 1, 0, 1, 1, 0)


def _full_from_gathered(t, axis):
    if axis == 0:
        return t.reshape(N_DEV * t.shape[1], t.shape[2])
    return jnp.concatenate([t[d] for d in range(N_DEV)], axis=1)


def _chunks_from_cols(t):
    c = t.shape[1] // N_DEV
    return jnp.stack([t[:, d * c : (d + 1) * c] for d in range(N_DEV)])


def _send_from_cols(t):
    c = t.shape[1] // N_DEV
    return jnp.stack([jnp.stack([t[:, (2 * chip + core) * c : (2 * chip + core + 1) * c] for chip in range(4)]) for core in range(2)])


_SMALL = (("norm1_g", 8), ("norm2_g", 8), ("norm_f_g", 8), ("b_forget", 8), ("pool_scale", 8), ("pool_mix", 512))
_SMALL_ROWS = sum(r for _, r in _SMALL) + 8


def _pack_small(vals, loss_row):
    parts = []
    for (name, rows), t in zip(_SMALL, vals):
        f = t.astype(F32).reshape(-1)
        f = jnp.concatenate([f, jnp.zeros((rows * LANES - f.shape[0],), F32)]).reshape(rows, LANES)
        parts.append(f)
    parts.append(loss_row)
    return jnp.concatenate(parts, axis=0)


def _unpack_small(packed, shapes):
    out, off = [], 0
    for (name, rows), shape in zip(_SMALL, shapes):
        n = 1
        for s in shape:
            n *= s
        out.append(packed[off : off + rows].reshape(-1)[:n].reshape(shape))
        off += rows
    return out, packed[off, 0]


def _local_grads(x, tgt, g1, g2, gf, b_forget, pool_mix, pool_scale, w_in, fwd_token, out_weights, ffn_weights, ffn_grads_out, out_grads_out):
    n_seq, S, _ = x.shape
    T = n_seq * S
    x2 = x.reshape(T, D_MODEL)
    tg2 = tgt.reshape(T, D_MODEL)
    w_uqkv = w_in[:, : POOL_WIDTH + 3 * ATTN_WIDTH]
    w_fl = jnp.concatenate([w_in[:, 2048 : 2048 + N_HEADS], jnp.zeros((D_MODEL, FL_PAD - N_HEADS), BF16)], axis=1)
    w_g = w_in[:, 2048 + N_HEADS :]
    b_pad = jnp.concatenate([b_forget.reshape(1, N_HEADS), jnp.zeros((1, FL_PAD - N_HEADS), F32)], axis=1)
    mix_b = pool_mix.reshape(len(POOL_WINDOWS), GROUP_DIM, GROUP_DIM).astype(BF16)
    scale = pool_scale.reshape(1, POOL_WIDTH)
    g1 = g1.reshape(1, D_MODEL)
    g2 = g2.reshape(1, D_MODEL)
    gf = gf.reshape(1, D_MODEL)

    h, u, qkv, fl, gates = _in_proj(x2, g1, w_uqkv, w_fl, w_g, fwd_token)
    fcol = _forget_fwd(fl, b_pad, n_seq, S)
    pm, p2, p3 = _pool_fwd(u, mix_b, scale, n_seq, S)
    a, lse = _attn_fwd(qkv, fcol, n_seq, S)
    w_po, w_ao, w_out = out_weights(a)
    merged, x1, attn_y, pool_y = _mix_out(a, p3, gates, x2, w_ao, w_po, w_out)
    w_gate, w_up, w_down = ffn_weights(x1)
    h2, gate, up, act, dx2, loss_rows, dgf = _ffn_fwd(x1, g2, gf, tg2, w_gate, w_up, w_down)

    dgate, dup, dx1, dg2 = _ffn_bwd(dx2, gate, up, x1, g2, w_gate, w_up, w_down)
    bwd_token = ffn_grads_out(_matmul_tn(h2, dgate, "dw_ffn_gate"), _matmul_tn(h2, dup, "dw_ffn_up"), _matmul_tn(act, dx2, "dw_ffn_down"))
    dgates, dpy, day, da, dp2, dscale = _mix_bwd(dx1, gates, pool_y, attn_y, p2, scale, w_out, w_ao, w_po, bwd_token)
    out_token = out_grads_out(_matmul_tn(p3, dpy, "dw_pool_out"), _matmul_tn(a, day, "dw_attn_out"), _matmul_tn(merged, dx1, "dw_out"))
    du, dmix = _pool_bwd(dp2, pm, mix_b, out_token, n_seq, S)
    dq, dk, dv, dfk, dfq = _attn_bwd(qkv, da, a, fcol, lse, n_seq, S)
    dfl, db = _forget_bwd(dfk, dfq, fl, b_pad, n_seq, S)
    dx, dg1 = _in_proj_bwd(du, dq, dk, dv, dfl, dgates, x2, dx1, g1, w_uqkv, w_fl, w_g)

    d_w_in = jnp.concatenate(
        [
            _matmul_tn(h, du, "dw_u"), _matmul_tn(h, dq, "dw_q"), _matmul_tn(h, dk, "dw_k"), _matmul_tn(h, dv, "dw_v"),
            _matmul_tn(h, dfl, "dw_fl")[:, :N_HEADS], _matmul_tn(h, dgates, "dw_gates"),
        ],
        axis=1,
    )
    small = (dg1, dg2, dgf, db[:, :N_HEADS], dscale, dmix)
    return loss_rows, dx.reshape(n_seq, S, D_MODEL), _send_from_cols(d_w_in), small


def kernel(x, norm1_g, w_in, b_forget, pool_mix, pool_scale, w_pool_out, w_attn_out, w_out, norm2_g, w_ffn_gate, w_ffn_up, w_ffn_down, norm_f_g, loss_target, m_norm1_g, m_w_in, m_b_forget, m_pool_mix, m_pool_scale, m_w_pool_out, m_w_attn_out, m_w_out, m_norm2_g, m_w_ffn_gate, m_w_ffn_up, m_w_ffn_down, m_norm_f_g, v_norm1_g, v_w_in, v_b_forget, v_pool_mix, v_pool_scale, v_w_pool_out, v_w_attn_out, v_w_out, v_norm2_g, v_w_ffn_gate, v_w_ffn_up, v_w_ffn_down, v_norm_f_g):
    names = ("w_in", "w_pool_out", "w_attn_out", "w_out", "w_ffn_gate", "w_ffn_up", "w_ffn_down")
    w_sh = (w_in, w_pool_out, w_attn_out, w_out, w_ffn_gate, w_ffn_up, w_ffn_down)
    m_sh = (m_w_in, m_w_pool_out, m_w_attn_out, m_w_out, m_w_ffn_gate, m_w_ffn_up, m_w_ffn_down)
    v_sh = (v_w_in, v_w_pool_out, v_w_attn_out, v_w_out, v_w_ffn_gate, v_w_ffn_up, v_w_ffn_down)

    cx, cy, cc = _position()
    me = 4 * cx + 2 * cy + cc
    shards = [w[0].astype(BF16) for w in w_sh]
    (gathered_in,) = _all_gather(shards[:1], "w_in_all_gather")
    out_sems = _exchange_start(shards[1:4], gathered_in, "out_weights_gather_start", scatter=False)
    ffn_sems = _exchange_start(shards[4:], out_sems[4], "ffn_weights_gather_start", scatter=False)

    def with_own(lands, own):
        return [lax.dynamic_update_slice(l, o[None], (me, 0, 0)) for l, o in zip(lands, own)]

    def gathered_weights(sems, axes, name):
        def wait(after):
            send_sems, recv_sems, srcs, lands, _ = sems
            srcs, lands = _exchange_wait(send_sems, recv_sems, srcs, lands, after, name, scatter=False)
            return [_full_from_gathered(t, axis) for t, axis in zip(with_own(lands, srcs), axes)]

        return wait

    scatters = {}

    def scatter_grads(key, name):
        def start(*whole_grads):
            chunks = [
                _chunks_from_cols(t) if axis == 1 else t.reshape(N_DEV, -1, t.shape[1])
                for t, axis in zip(whole_grads, _SHARD_AXIS[key])
            ]
            scatters[key] = _exchange_start(chunks, jnp.zeros((8, LANES), F32), name, scatter=True)
            return scatters[key][4]

        return start

    ffn, out = slice(4, 7), slice(1, 4)
    loss_rows, grad_x, send_in, small = _local_grads(
        x, loss_target, norm1_g, norm2_g, norm_f_g, b_forget, pool_mix, pool_scale, _full_from_gathered(gathered_in, 1), ffn_sems[4],
        gathered_weights(out_sems, _SHARD_AXIS[out], "out_weights_gather_wait"),
        gathered_weights(ffn_sems, _SHARD_AXIS[ffn], "ffn_weights_gather_wait"),
        scatter_grads(ffn, "ffn_grads_scatter_start"), scatter_grads(out, "out_grads_scatter_start"),
    )

    def scattered_updates(key, name):
        send_sems, recv_sems, srcs, lands, _ = scatters[key]
        srcs, lands = _exchange_wait(send_sems, recv_sems, srcs, lands, grad_x, name, scatter=True)
        own = [lax.dynamic_index_in_dim(s, me, 0, keepdims=False) for s in srcs]
        return [
            _shard_update_direct(p, w, m, v, "update_" + n)
            for p, w, m, v, n in zip(with_own(lands, own), w_sh[key], m_sh[key], v_sh[key], names[key])
        ]

    updates_out = scattered_updates(out, "out_grads_scatter_wait")
    updates_ffn = scattered_updates(ffn, "ffn_grads_scatter_wait")

    core = jnp.reshape(cc, (1,)).astype(jnp.int32)
    pos = jnp.stack([cc, 2 * cx + cy]).astype(jnp.int32)
    (got_in,) = _sibling_exchange([send_in])
    pair_in = _pair_sum(send_in, got_in, core, "pair_sum_w_in")
    (recv_in,) = _chip_exchange([pair_in])
    update_in = _shard_update(send_in, got_in, recv_in, w_in, m_w_in, v_w_in, pos, "update_w_in")
    g_w, d_w, nm_w, nv_w = zip(*([update_in] + updates_out + updates_ffn))

    small_w = (norm1_g, norm2_g, norm_f_g, b_forget, pool_scale, pool_mix)
    small_m = (m_norm1_g, m_norm2_g, m_norm_f_g, m_b_forget, m_pool_scale, m_pool_mix)
    small_v = (v_norm1_g, v_norm2_g, v_norm_f_g, v_b_forget, v_pool_scale, v_pool_mix)
    zero_row = jnp.zeros((8, LANES), F32)
    (parts,) = _all_gather([_pack_small(small, loss_rows)], "small_all_gather")
    g_s, d_s, nm_s, nv_s = _small_update(parts, _pack_small(small_w, zero_row), _pack_small(small_m, zero_row), _pack_small(small_v, zero_row))
    shapes = [t.shape for t in small_w]
    (g1, g2, gf, gb, gsc, gmix), loss = _unpack_small(g_s, shapes)
    (d1, d2, df, db_, dsc, dmx), _ = _unpack_small(d_s, shapes)
    (m1, m2, mf, mb, msc, mmx), _ = _unpack_small(nm_s, shapes)
    (v1, v2, vf, vb, vsc, vmx), _ = _unpack_small(nv_s, shapes)

    def ordered(n1, win, b, mix, sc, wpo, wao, wout, n2, wg, wu, wd, nf):
        return (n1, win, b, mix, sc, wpo, wao, wout, n2, wg, wu, wd, nf)

    grads = ordered(g1, g_w[0], gb, gmix, gsc, g_w[1], g_w[2], g_w[3], g2, g_w[4], g_w[5], g_w[6], gf)
    deltas = ordered(d1, d_w[0], db_, dmx, dsc, d_w[1], d_w[2], d_w[3], d2, d_w[4], d_w[5], d_w[6], df)
    new_m = ordered(m1, nm_w[0], mb, mmx, msc, nm_w[1], nm_w[2], nm_w[3], m2, nm_w[4], nm_w[5], nm_w[6], mf)
    new_v = ordered(v1, nv_w[0], vb, vmx, vsc, nv_w[1], nv_w[2], nv_w[3], v2, nv_w[4], nv_w[5], nv_w[6], vf)
    return (loss, grad_x, *grads, *deltas, *new_m, *new_v)
```

```python
import functools

import jax
import jax.numpy as jnp
from jax import lax
from jax.experimental import pallas as pl
from jax.experimental.pallas import tpu as pltpu

F32 = jnp.float32
BF16 = jnp.bfloat16
MESH = pl.DeviceIdType.MESH

D_MODEL = 1024
POOL_WINDOWS = (2, 4, 8, 16)
POOL_WIDTH = 512
GROUP_DIM = 128
ATTN_WIDTH = 512
HEAD_DIM = 64
N_HEADS = 8
N_PAIRS = 4
D_FF = 2816
RMS_EPS = 1e-6
N_DEV = 8
LANES = 128
FL_PAD = 128

ADAM_LR = 0.001
ADAM_B1 = 0.9
ADAM_B2 = 0.999
ADAM_EPS = 1e-08
ADAM_WD = 0.01
ADAM_STEP = 10

VMEM_LIMIT = 56 * 1024 * 1024
VMEM_LIMIT_MAX = 60 * 1024 * 1024
ROW_TILE = 512
ATTN_BLOCK = 512
FF_CHUNK = 256
FF_ROW_TILE = 512
DW_TOKENS = 2048


def _mm(a, b):
    return jnp.dot(a, b, preferred_element_type=F32)


def _mm_nt(a, b):
    return lax.dot_general(a, b, (((1,), (1,)), ((), ())), preferred_element_type=F32)


def _mm_tn(a, b):
    return lax.dot_general(a, b, (((0,), (0,)), ((), ())), preferred_element_type=F32)


def _sigmoid(x):
    return 1.0 / (1.0 + jnp.exp(-x))


def _params(sem, vmem=VMEM_LIMIT):
    return pltpu.CompilerParams(dimension_semantics=sem, vmem_limit_bytes=vmem)


def _const_spec(shape):
    nd = len(shape)
    return pl.BlockSpec(shape, lambda *_: (0,) * nd, pipeline_mode=pl.Buffered(1))


def _rms_fwd(x, g):
    r = lax.rsqrt(jnp.mean(x * x, axis=-1, keepdims=True) + RMS_EPS)
    xh = x * r
    return xh * g, xh, r


def _rms_bwd(dy, xh, r, g):
    dxh = dy * g
    dx = r * (dxh - xh * jnp.mean(dxh * xh, axis=-1, keepdims=True))
    return dx, dy * xh


def _in_proj(x, g1, w_uqkv, w_fl, w_g, token):
    T = x.shape[0]
    tm = ROW_TILE

    def body(x_ref, g_ref, wa_ref, wf_ref, wg_ref, token_ref, h_ref, u_ref, qkv_ref, fl_ref, gt_ref):
        h, _, _ = _rms_fwd(x_ref[...], g_ref[...])
        hb = h.astype(BF16)
        h_ref[...] = hb
        z = _mm(hb, wa_ref[...])
        u_ref[...] = z[:, :POOL_WIDTH]
        qkv_ref[...] = z[:, POOL_WIDTH:].astype(BF16)
        fl_ref[...] = _mm(hb, wf_ref[...])
        gt_ref[...] = _mm(hb, wg_ref[...]).astype(BF16)

    row = lambda n: pl.BlockSpec((tm, n), lambda i: (i, 0))
    return pl.pallas_call(
        body,
        name="in_proj",
        grid=(T // tm,),
        in_specs=[row(D_MODEL), _const_spec((1, D_MODEL)), _const_spec(w_uqkv.shape), _const_spec(w_fl.shape), _const_spec(w_g.shape), _HBM],
        out_specs=[row(D_MODEL), row(POOL_WIDTH), row(3 * ATTN_WIDTH), row(FL_PAD), row(2 * D_MODEL)],
        out_shape=[
            jax.ShapeDtypeStruct((T, D_MODEL), BF16),
            jax.ShapeDtypeStruct((T, POOL_WIDTH), F32),
            jax.ShapeDtypeStruct((T, 3 * ATTN_WIDTH), BF16),
            jax.ShapeDtypeStruct((T, FL_PAD), F32),
            jax.ShapeDtypeStruct((T, 2 * D_MODEL), BF16),
        ],
        compiler_params=_params(("parallel",)),
    )(x, g1, w_uqkv, w_fl, w_g, token)


def _log_sigmoid(x):
    return jnp.minimum(x, 0.0) - jnp.log(1.0 + jnp.exp(-jnp.abs(x)))


def _forget_fwd(fl, b_pad, n_seq, S):
    def body(fl_ref, b_ref, fcol_ref):
        lf = _log_sigmoid(fl_ref[...] + b_ref[...])
        t = lf.T
        lane = lax.broadcasted_iota(jnp.int32, t.shape, 1)
        k = 1
        while k < S:
            t = t + jnp.where(lane >= k, pltpu.roll(t, k, 1), 0.0)
            k *= 2
        fcol_ref[...] = t.T

    return pl.pallas_call(
        body,
        name="forget_fwd",
        grid=(n_seq,),
        in_specs=[pl.BlockSpec((S, FL_PAD), lambda s: (s, 0)), _const_spec((1, FL_PAD))],
        out_specs=pl.BlockSpec((S, FL_PAD), lambda s: (s, 0)),
        out_shape=jax.ShapeDtypeStruct((n_seq * S, FL_PAD), F32),
        compiler_params=_params(("parallel",)),
    )(fl, b_pad)


def _window_pick(g, v2, v4, v8, v16):
    return jnp.where(g == 0, v2, jnp.where(g == 1, v4, jnp.where(g == 2, v8, v16)))


def _pool_fwd(u, mix_b, scale, n_seq, S):
    T = n_seq * S

    def body(u_ref, mix_ref, sc_ref, pm_ref, p2_ref, p3_ref):
        g = pl.program_id(1)
        uu = u_ref[...]
        row = lax.broadcasted_iota(jnp.int32, uu.shape, 0)

        def back(a, k):
            return jnp.where(row >= k, pltpu.roll(a, k, 0), 0.0)

        s2 = uu + back(uu, 1)
        s4 = s2 + back(s2, 2)
        s8 = s4 + back(s4, 4)
        s16 = s8 + back(s8, 8)
        w = _window_pick(g, 2.0, 4.0, 8.0, 16.0)
        cnt = jnp.minimum((row + 1).astype(F32), w)
        pm = _window_pick(g, s2, s4, s8, s16) / cnt - uu
        pmb = pm.astype(BF16)
        pm_ref[...] = pmb
        p2 = _mm(pmb, mix_ref[...])
        p2_ref[...] = p2
        p3_ref[...] = (p2 * sc_ref[...]).astype(BF16)

    grp = pl.BlockSpec((S, GROUP_DIM), lambda s, g: (s, g))
    return pl.pallas_call(
        body,
        name="pool_fwd",
        grid=(n_seq, len(POOL_WINDOWS)),
        in_specs=[
            grp,
            pl.BlockSpec((None, GROUP_DIM, GROUP_DIM), lambda s, g: (g, 0, 0)),
            pl.BlockSpec((1, GROUP_DIM), lambda s, g: (0, g)),
        ],
        out_specs=[grp, grp, grp],
        out_shape=[
            jax.ShapeDtypeStruct((T, POOL_WIDTH), BF16),
            jax.ShapeDtypeStruct((T, POOL_WIDTH), F32),
            jax.ShapeDtypeStruct((T, POOL_WIDTH), BF16),
        ],
        compiler_params=_params(("parallel", "parallel")),
    )(u, mix_b, scale)


def _split3(v):
    hi = v.astype(BF16).astype(F32)
    r = v - hi
    mid = r.astype(BF16).astype(F32)
    lo = (r - mid).astype(BF16).astype(F32)
    return hi, mid, lo


def _augment(xp, hh, first, second):
    lane = lax.broadcasted_iota(jnp.int32, (1, LANES), 1)
    head = (lane >= HEAD_DIM * hh) & (lane < HEAD_DIM * (hh + 1))
    b = HEAD_DIM * (1 - hh)
    out = jnp.where(head, xp.astype(F32), 0.0)
    for n, col in enumerate(tuple(first) + tuple(second)):
        out = jnp.where(lane == b + n, col, out)
    return out.astype(BF16)


def _attn_fwd(qkv, fcol, n_seq, S):
    T = n_seq * S
    tb = ATTN_BLOCK
    nq = S // tb
    scale = HEAD_DIM ** -0.5

    def body(q_ref, k_ref, v_ref, fc_ref, o_ref, st_ref, qa_sc, ka_sc, m_sc, l_sc, acc_sc):
        i = pl.program_id(1)
        lane = lax.broadcasted_iota(jnp.int32, (1, LANES), 1)
        low = lane < HEAD_DIM
        ones = (1.0, 1.0, 1.0)

        @pl.when(i == 0)
        def _():
            def rows_ka(r, carry):
                r0 = pl.multiple_of(r * tb, tb)
                for h in range(N_HEADS):
                    kp = k_ref[pl.ds(r0, tb), (h // 2) * LANES : (h // 2 + 1) * LANES] * scale
                    fk = fc_ref[pl.ds(r0, tb), h : h + 1]
                    ka_sc[h, pl.ds(r0, tb), :] = _augment(kp, h % 2, ones, _split3(-fk))
                return carry

            lax.fori_loop(0, nq, rows_ka, 0)

        q0 = pl.multiple_of(i * tb, tb)
        for h in range(N_HEADS):
            qp = q_ref[:, (h // 2) * LANES : (h // 2 + 1) * LANES]
            qa_sc[h] = _augment(qp, h % 2, _split3(fc_ref[pl.ds(q0, tb), h : h + 1]), ones)
        m_sc[...] = jnp.full(m_sc.shape, -jnp.inf, F32)
        l_sc[...] = jnp.zeros_like(l_sc)
        acc_sc[...] = jnp.zeros_like(acc_sc)
        causal = lax.broadcasted_iota(jnp.int32, (tb, tb), 1) <= lax.broadcasted_iota(jnp.int32, (tb, tb), 0)

        def step(j, masked):
            c0 = pl.multiple_of(j * tb, tb)
            for p in range(N_PAIRS):
                vb = v_ref[pl.ds(c0, tb), p * LANES : (p + 1) * LANES]
                pv, al = [], []
                for hh in range(2):
                    h = 2 * p + hh
                    s = _mm_nt(qa_sc[h], ka_sc[h, pl.ds(c0, tb), :])
                    if masked:
                        s = jnp.where(causal, s, -jnp.inf)
                    m_old = m_sc[h]
                    m_new = jnp.maximum(m_old, jnp.max(s, axis=1, keepdims=True))
                    alpha = jnp.exp(m_old - m_new)
                    pe = jnp.exp(s - jnp.concatenate([m_new] * (tb // LANES), axis=1))
                    l_sc[h] = alpha * l_sc[h] + jnp.sum(pe, axis=1, keepdims=True)
                    m_sc[h] = m_new
                    pv.append(_mm(pe.astype(BF16), vb))
                    al.append(alpha)
                acc_sc[p] = jnp.where(low, al[0], al[1]) * acc_sc[p] + jnp.where(low, pv[0], pv[1])

        def loop_body(j, carry):
            step(j, False)
            return carry

        lax.fori_loop(0, i, loop_body, 0)
        step(i, True)
        st = jnp.zeros((tb, LANES), F32)
        for p in range(N_PAIRS):
            lp = jnp.where(low, l_sc[2 * p], l_sc[2 * p + 1])
            o_ref[:, p * LANES : (p + 1) * LANES] = (acc_sc[p] / lp).astype(BF16)
            for h in (2 * p, 2 * p + 1):
                st = jnp.where(lane == h, m_sc[h] + jnp.log(l_sc[h]), st)
        st_ref[...] = st

    return pl.pallas_call(
        body,
        name="attn_fwd",
        grid=(n_seq, nq),
        in_specs=[
            pl.BlockSpec((tb, ATTN_WIDTH), lambda s, i: (s * nq + i, 0)),
            pl.BlockSpec((S, ATTN_WIDTH), lambda s, i: (s, 1)),
            pl.BlockSpec((S, ATTN_WIDTH), lambda s, i: (s, 2)),
            pl.BlockSpec((S, LANES), lambda s, i: (s, 0)),
        ],
        out_specs=[
            pl.BlockSpec((tb, ATTN_WIDTH), lambda s, i: (s * nq + i, 0)),
            pl.BlockSpec((tb, LANES), lambda s, i: (s * nq + i, 0)),
        ],
        out_shape=[jax.ShapeDtypeStruct((T, ATTN_WIDTH), BF16), jax.ShapeDtypeStruct((T, LANES), F32)],
        scratch_shapes=[
            pltpu.VMEM((N_HEADS, tb, LANES), BF16),
            pltpu.VMEM((N_HEADS, S, LANES), BF16),
            pltpu.VMEM((N_HEADS, tb, LANES), F32),
            pltpu.VMEM((N_HEADS, tb, LANES), F32),
            pltpu.VMEM((N_PAIRS, tb, LANES), F32),
        ],
        compiler_params=_params(("parallel", "arbitrary")),
    )(qkv, qkv, qkv, fcol)


def _mix_out(a, p3, gates, x, w_ao, w_po, w_out):
    T = x.shape[0]
    tm = ROW_TILE

    def body(a_ref, p3_ref, gt_ref, x_ref, wao_ref, wpo_ref, wout_ref, mg_ref, x1_ref, ay_ref, py_ref):
        ay = _mm(a_ref[...], wao_ref[...])
        py = _mm(p3_ref[...], wpo_ref[...])
        ay_ref[...] = ay.astype(BF16)
        py_ref[...] = py.astype(BF16)
        sp = _sigmoid(gt_ref[:, :D_MODEL].astype(F32))
        sa = _sigmoid(gt_ref[:, D_MODEL:].astype(F32))
        mb = (sp * py + sa * ay).astype(BF16)
        mg_ref[...] = mb
        x1_ref[...] = x_ref[...] + _mm(mb, wout_ref[...])

    row = lambda n: pl.BlockSpec((tm, n), lambda i: (i, 0))
    return pl.pallas_call(
        body,
        name="mix_out",
        grid=(T // tm,),
        in_specs=[
            row(ATTN_WIDTH), row(POOL_WIDTH), row(2 * D_MODEL), row(D_MODEL),
            _const_spec(w_ao.shape), _const_spec(w_po.shape), _const_spec(w_out.shape),
        ],
        out_specs=[row(D_MODEL), row(D_MODEL), row(D_MODEL), row(D_MODEL)],
        out_shape=[
            jax.ShapeDtypeStruct((T, D_MODEL), BF16), jax.ShapeDtypeStruct((T, D_MODEL), F32),
            jax.ShapeDtypeStruct((T, D_MODEL), BF16), jax.ShapeDtypeStruct((T, D_MODEL), BF16),
        ],
        compiler_params=_params(("parallel",)),
    )(a, p3, gates, x, w_ao, w_po, w_out)


def _ffn_fwd(x1, g2, gf, tgt, w_gate, w_up, w_down):
    T = x1.shape[0]
    tm = min(T, FF_ROW_TILE)
    nt = T // tm
    nc = D_FF // FF_CHUNK

    def body(x1_ref, g2_ref, gf_ref, tg_ref, wg_ref, wu_ref, wd_ref, h2_ref, gate_ref, up_ref, act_ref, dx2_ref, loss_ref, dgf_ref):
        x1v = x1_ref[...]
        h2, _, _ = _rms_fwd(x1v, g2_ref[...])
        h2b = h2.astype(BF16)
        h2_ref[...] = h2b
        acc = x1v
        for c in range(nc):
            sl = slice(c * FF_CHUNK, (c + 1) * FF_CHUNK)
            gate = _mm(h2b, wg_ref[:, sl])
            up = _mm(h2b, wu_ref[:, sl])
            gate_ref[:, sl] = gate.astype(BF16)
            up_ref[:, sl] = up.astype(BF16)
            act = (gate * _sigmoid(gate) * up).astype(BF16)
            act_ref[:, sl] = act
            acc = acc + _mm(act, wd_ref[sl, :])
        gfv = gf_ref[...]
        y, xh, r = _rms_fwd(acc, gfv)
        err = y - tg_ref[...]
        part = 0.5 * jnp.sum(jnp.mean(err * err, axis=-1, keepdims=True), axis=0, keepdims=True)
        dx2, dgrow = _rms_bwd(err * (1.0 / D_MODEL), xh, r, gfv)
        dx2_ref[...] = dx2

        @pl.when(pl.program_id(0) == 0)
        def _():
            dgf_ref[...] = jnp.zeros_like(dgf_ref)
            loss_ref[...] = jnp.zeros_like(loss_ref)

        dgf_ref[...] += jnp.sum(dgrow, axis=0, keepdims=True)
        loss_ref[...] += jnp.broadcast_to(part, loss_ref.shape)

    row = lambda n: pl.BlockSpec((tm, n), lambda i: (i, 0))
    return pl.pallas_call(
        body,
        name="ffn_fwd",
        grid=(nt,),
        in_specs=[
            row(D_MODEL), _const_spec((1, D_MODEL)), _const_spec((1, D_MODEL)), row(D_MODEL),
            _const_spec(w_gate.shape), _const_spec(w_up.shape), _const_spec(w_down.shape),
        ],
        out_specs=[
            row(D_MODEL), row(D_FF), row(D_FF), row(D_FF), row(D_MODEL),
            pl.BlockSpec((8, LANES), lambda i: (0, 0)),
            pl.BlockSpec((1, D_MODEL), lambda i: (0, 0)),
        ],
        out_shape=[
            jax.ShapeDtypeStruct((T, D_MODEL), BF16),
            jax.ShapeDtypeStruct((T, D_FF), BF16),
            jax.ShapeDtypeStruct((T, D_FF), BF16),
            jax.ShapeDtypeStruct((T, D_FF), BF16),
            jax.ShapeDtypeStruct((T, D_MODEL), F32),
            jax.ShapeDtypeStruct((8, LANES), F32),
            jax.ShapeDtypeStruct((1, D_MODEL), F32),
        ],
        compiler_params=_params(("arbitrary",)),
    )(x1, g2, gf, tgt, w_gate, w_up, w_down)


def _ffn_bwd(dx2, gate, up, x1, g2, w_gate, w_up, w_down):
    T = x1.shape[0]
    tm = min(T, FF_ROW_TILE)
    nc = D_FF // FF_CHUNK

    def body(dx2_ref, gate_ref, up_ref, x1_ref, g2_ref, wg_ref, wu_ref, wd_ref, dgate_ref, dup_ref, dx1_ref, dg2_ref):
        dx2v = dx2_ref[...]
        dx2b = dx2v.astype(BF16)
        dh2 = jnp.zeros((tm, D_MODEL), F32)
        for c in range(nc):
            sl = slice(c * FF_CHUNK, (c + 1) * FF_CHUNK)
            dact = _mm_nt(dx2b, wd_ref[sl, :])
            gate = gate_ref[:, sl].astype(F32)
            sg = _sigmoid(gate)
            silu = gate * sg
            dgate = (dact * up_ref[:, sl].astype(F32) * (sg * (1.0 + gate * (1.0 - sg)))).astype(BF16)
            dup = (dact * silu).astype(BF16)
            dgate_ref[:, sl] = dgate
            dup_ref[:, sl] = dup
            dh2 = dh2 + _mm_nt(dgate, wg_ref[:, sl]) + _mm_nt(dup, wu_ref[:, sl])
        g2v = g2_ref[...]
        _, xh, r = _rms_fwd(x1_ref[...], g2v)
        dxn, dgrow = _rms_bwd(dh2, xh, r, g2v)
        dx1_ref[...] = dx2v + dxn

        @pl.when(pl.program_id(0) == 0)
        def _():
            dg2_ref[...] = jnp.zeros_like(dg2_ref)

        dg2_ref[...] += jnp.sum(dgrow, axis=0, keepdims=True)

    row = lambda n: pl.BlockSpec((tm, n), lambda i: (i, 0))
    return pl.pallas_call(
        body,
        name="ffn_bwd",
        grid=(T // tm,),
        in_specs=[
            row(D_MODEL), row(D_FF), row(D_FF), row(D_MODEL), _const_spec((1, D_MODEL)),
            _const_spec(w_gate.shape), _const_spec(w_up.shape), _const_spec(w_down.shape),
        ],
        out_specs=[row(D_FF), row(D_FF), row(D_MODEL), pl.BlockSpec((1, D_MODEL), lambda i: (0, 0))],
        out_shape=[
            jax.ShapeDtypeStruct((T, D_FF), BF16),
            jax.ShapeDtypeStruct((T, D_FF), BF16),
            jax.ShapeDtypeStruct((T, D_MODEL), F32),
            jax.ShapeDtypeStruct((1, D_MODEL), F32),
        ],
        compiler_params=_params(("arbitrary",), VMEM_LIMIT_MAX),
    )(dx2, gate, up, x1, g2, w_gate, w_up, w_down)


def _mix_bwd(dx1, gates, pool_y, attn_y, p2, scale, w_out, w_ao, w_po, token):
    T = dx1.shape[0]
    tm = ROW_TILE

    def body(dx1_ref, gt_ref, py_ref, ay_ref, p2_ref, sc_ref, wout_ref, wao_ref, wpo_ref, token_ref, dgt_ref, dpy_ref, day_ref, da_ref, dp2_ref, dsc_ref):
        dm = _mm_nt(dx1_ref[...].astype(BF16), wout_ref[...])
        sp = _sigmoid(gt_ref[:, :D_MODEL].astype(F32))
        sa = _sigmoid(gt_ref[:, D_MODEL:].astype(F32))
        dgt_ref[:, :D_MODEL] = (dm * py_ref[...].astype(F32) * (sp * (1.0 - sp))).astype(BF16)
        dgt_ref[:, D_MODEL:] = (dm * ay_ref[...].astype(F32) * (sa * (1.0 - sa))).astype(BF16)
        dpy = (dm * sp).astype(BF16)
        day = (dm * sa).astype(BF16)
        dpy_ref[...] = dpy
        day_ref[...] = day
        da_ref[...] = _mm_nt(day, wao_ref[...]).astype(BF16)
        dp3 = _mm_nt(dpy, wpo_ref[...])
        dp2_ref[...] = (dp3 * sc_ref[...]).astype(BF16)

        @pl.when(pl.program_id(0) == 0)
        def _():
            dsc_ref[...] = jnp.zeros_like(dsc_ref)

        dsc_ref[...] += jnp.sum(dp3 * p2_ref[...], axis=0, keepdims=True)

    row = lambda n: pl.BlockSpec((tm, n), lambda i: (i, 0))
    return pl.pallas_call(
        body,
        name="mix_bwd",
        grid=(T // tm,),
        in_specs=[
            row(D_MODEL), row(2 * D_MODEL), row(D_MODEL), row(D_MODEL), row(POOL_WIDTH), _const_spec((1, POOL_WIDTH)),
            _const_spec(w_out.shape), _const_spec(w_ao.shape), _const_spec(w_po.shape), _HBM,
        ],
        out_specs=[row(2 * D_MODEL), row(D_MODEL), row(D_MODEL), row(ATTN_WIDTH), row(POOL_WIDTH), pl.BlockSpec((1, POOL_WIDTH), lambda i: (0, 0))],
        out_shape=[
            jax.ShapeDtypeStruct((T, 2 * D_MODEL), BF16),
            jax.ShapeDtypeStruct((T, D_MODEL), BF16),
            jax.ShapeDtypeStruct((T, D_MODEL), BF16),
            jax.ShapeDtypeStruct((T, ATTN_WIDTH), BF16),
            jax.ShapeDtypeStruct((T, POOL_WIDTH), BF16),
            jax.ShapeDtypeStruct((1, POOL_WIDTH), F32),
        ],
        compiler_params=_params(("arbitrary",)),
    )(dx1, gates, pool_y, attn_y, p2, scale, w_out, w_ao, w_po, token)


def _pool_bwd(dp2, pm, mix_b, token, n_seq, S):
    T = n_seq * S

    def body(dp2_ref, pm_ref, mix_ref, token_ref, du_ref, dmix_ref):
        g = pl.program_id(0)
        dp2v = dp2_ref[...]
        dpm = _mm_nt(dp2v, mix_ref[...])
        row = lax.broadcasted_iota(jnp.int32, dpm.shape, 0)
        w = _window_pick(g, 2.0, 4.0, 8.0, 16.0)
        e = dpm / jnp.minimum((row + 1).astype(F32), w)

        def ahead(a, k):
            return jnp.where(row < S - k, pltpu.roll(a, S - k, 0), 0.0)

        r2 = e + ahead(e, 1)
        r4 = r2 + ahead(r2, 2)
        r8 = r4 + ahead(r4, 4)
        r16 = r8 + ahead(r8, 8)
        du_ref[...] = (_window_pick(g, r2, r4, r8, r16) - dpm).astype(BF16)

        @pl.when(pl.program_id(1) == 0)
        def _():
            dmix_ref[...] = jnp.zeros_like(dmix_ref)

        dmix_ref[...] += _mm_tn(pm_ref[...], dp2v)

    grp = pl.BlockSpec((S, GROUP_DIM), lambda g, s: (s, g))
    mixs = pl.BlockSpec((None, GROUP_DIM, GROUP_DIM), lambda g, s: (g, 0, 0))
    return pl.pallas_call(
        body,
        name="pool_bwd",
        grid=(len(POOL_WINDOWS), n_seq),
        in_specs=[grp, grp, mixs, _HBM],
        out_specs=[grp, mixs],
        out_shape=[jax.ShapeDtypeStruct((T, POOL_WIDTH), BF16), jax.ShapeDtypeStruct((len(POOL_WINDOWS), GROUP_DIM, GROUP_DIM), F32)],
        compiler_params=_params(("parallel", "arbitrary")),
    )(dp2, pm, mix_b, token)


def _attn_bwd(qkv, da, a, fcol, lse, n_seq, S):
    T = n_seq * S
    tb = ATTN_BLOCK
    nb = S // tb
    scale = HEAD_DIM ** -0.5

    def body(q_ref, k_ref, v_ref, do_ref, o_ref, fc_ref, st_ref, dq_ref, dk_ref, dv_ref, dfk_ref, dfq_ref,
             qa_sc, doa_sc, dq_acc, ka_sc, va_sc, dk_sc, dv_sc):
        j = pl.program_id(1)
        lane = lax.broadcasted_iota(jnp.int32, (1, LANES), 1)
        low = lane < HEAD_DIM
        ones = (1.0, 1.0, 1.0)
        zeros = (0.0, 0.0, 0.0)

        @pl.when(j == 0)
        def _():
            dq_acc[...] = jnp.zeros_like(dq_acc)

            def rows_q(i, carry):
                r0 = pl.multiple_of(i * tb, tb)
                for h in range(N_HEADS):
                    pair = slice((h // 2) * LANES, (h // 2 + 1) * LANES)
                    qp = q_ref[pl.ds(r0, tb), pair]
                    dop = do_ref[pl.ds(r0, tb), pair]
                    prod = dop.astype(F32) * o_ref[pl.ds(r0, tb), pair].astype(F32)
                    head = (lane >= HEAD_DIM * (h % 2)) & (lane < HEAD_DIM * (h % 2 + 1))
                    delta = jnp.sum(jnp.where(head, prod, 0.0), axis=1, keepdims=True)
                    cq = fc_ref[pl.ds(r0, tb), h : h + 1] - st_ref[pl.ds(r0, tb), h : h + 1]
                    qa_sc[h, pl.ds(r0, tb), :] = _augment(qp, h % 2, _split3(cq), ones)
                    doa_sc[h, pl.ds(r0, tb), :] = _augment(dop, h % 2, _split3(-delta), zeros)
                return carry

            lax.fori_loop(0, nb, rows_q, 0)

        c0 = pl.multiple_of(j * tb, tb)
        for h in range(N_HEADS):
            pair = slice((h // 2) * LANES, (h // 2 + 1) * LANES)
            kp = k_ref[:, pair] * scale
            ka_sc[h] = _augment(kp, h % 2, ones, _split3(-fc_ref[pl.ds(c0, tb), h : h + 1]))
            va_sc[h] = _augment(v_ref[:, pair], h % 2, ones, zeros)
        dk_sc[...] = jnp.zeros_like(dk_sc)
        dv_sc[...] = jnp.zeros_like(dv_sc)
        causal = lax.broadcasted_iota(jnp.int32, (tb, tb), 1) <= lax.broadcasted_iota(jnp.int32, (tb, tb), 0)

        def step(i, masked):
            r0 = pl.multiple_of(i * tb, tb)
            for h in range(N_HEADS):
                dob = do_ref[pl.ds(r0, tb), (h // 2) * LANES : (h // 2 + 1) * LANES]
                qa = qa_sc[h, pl.ds(r0, tb), :]
                s = _mm_nt(qa, ka_sc[h])
                if masked:
                    s = jnp.where(causal, s, -jnp.inf)
                pr = jnp.exp(s)
                dv_sc[h] += _mm_tn(pr.astype(BF16), dob)
                dsb = (pr * _mm_nt(doa_sc[h, pl.ds(r0, tb), :], va_sc[h])).astype(BF16)
                dk_sc[h] += _mm_tn(dsb, qa)
                dq_acc[h, pl.ds(r0, tb), :] += _mm(dsb, ka_sc[h])

        step(j, True)

        def loop_body(i, carry):
            step(i, False)
            return carry

        lax.fori_loop(j + 1, nb, loop_body, 0)
        dfk = jnp.zeros((tb, LANES), F32)
        for p in range(N_PAIRS):
            dk_ref[:, p * LANES : (p + 1) * LANES] = (jnp.where(low, dk_sc[2 * p], dk_sc[2 * p + 1]) * scale).astype(BF16)
            dv_ref[:, p * LANES : (p + 1) * LANES] = jnp.where(low, dv_sc[2 * p], dv_sc[2 * p + 1]).astype(BF16)
            for hh in range(2):
                b = HEAD_DIM * (1 - hh) + 3
                dfk = jnp.where(lane == 2 * p + hh, -dk_sc[2 * p + hh][:, b : b + 1], dfk)
        dfk_ref[...] = dfk

        @pl.when(j == nb - 1)
        def _():
            def rows_dq(i, carry):
                r0 = pl.multiple_of(i * tb, tb)
                dfq = jnp.zeros((tb, LANES), F32)
                for p in range(N_PAIRS):
                    parts = [dq_acc[2 * p + hh, pl.ds(r0, tb), :] for hh in range(2)]
                    dq_ref[pl.ds(r0, tb), p * LANES : (p + 1) * LANES] = jnp.where(low, parts[0], parts[1]).astype(BF16)
                    for hh in range(2):
                        b = HEAD_DIM * (1 - hh)
                        dfq = jnp.where(lane == 2 * p + hh, parts[hh][:, b : b + 1], dfq)
                dfq_ref[pl.ds(r0, tb), :] = dfq
                return carry

            lax.fori_loop(0, nb, rows_dq, 0)

    seq = lambda w, col: pl.BlockSpec((S, w), lambda s, j: (s, col))
    blk = lambda w, col: pl.BlockSpec((tb, w), lambda s, j: (s * nb + j, col))
    return pl.pallas_call(
        body,
        name="attn_bwd",
        grid=(n_seq, nb),
        in_specs=[seq(ATTN_WIDTH, 0), blk(ATTN_WIDTH, 1), blk(ATTN_WIDTH, 2), seq(ATTN_WIDTH, 0), seq(ATTN_WIDTH, 0), seq(LANES, 0), seq(LANES, 0)],
        out_specs=[seq(ATTN_WIDTH, 0), blk(ATTN_WIDTH, 0), blk(ATTN_WIDTH, 0), blk(LANES, 0), seq(LANES, 0)],
        out_shape=[
            jax.ShapeDtypeStruct((T, ATTN_WIDTH), BF16),
            jax.ShapeDtypeStruct((T, ATTN_WIDTH), BF16),
            jax.ShapeDtypeStruct((T, ATTN_WIDTH), BF16),
            jax.ShapeDtypeStruct((T, LANES), F32),
            jax.ShapeDtypeStruct((T, LANES), F32),
        ],
        scratch_shapes=[
            pltpu.VMEM((N_HEADS, S, LANES), BF16),
            pltpu.VMEM((N_HEADS, S, LANES), BF16),
            pltpu.VMEM((N_HEADS, S, LANES), F32),
            pltpu.VMEM((N_HEADS, tb, LANES), BF16),
            pltpu.VMEM((N_HEADS, tb, LANES), BF16),
            pltpu.VMEM((N_HEADS, tb, LANES), F32),
            pltpu.VMEM((N_HEADS, tb, LANES), F32),
        ],
        compiler_params=_params(("parallel", "arbitrary")),
    )(qkv, qkv, qkv, da, a, fcol, lse)


def _forget_bwd(dfk, dfq, fl, b_pad, n_seq, S):
    def body(df_ref, dfq_ref, fl_ref, b_ref, dfl_ref, db_ref):
        t = (df_ref[...] + dfq_ref[...]).T
        lane = lax.broadcasted_iota(jnp.int32, t.shape, 1)
        k = 1
        while k < S:
            t = t + jnp.where(lane < S - k, pltpu.roll(t, S - k, 1), 0.0)
            k *= 2
        dfl = t.T * _sigmoid(-(fl_ref[...] + b_ref[...]))
        dfl_ref[...] = dfl.astype(BF16)

        @pl.when(pl.program_id(0) == 0)
        def _():
            db_ref[...] = jnp.zeros_like(db_ref)

        db_ref[...] += jnp.sum(dfl, axis=0, keepdims=True)

    return pl.pallas_call(
        body,
        name="forget_bwd",
        grid=(n_seq,),
        in_specs=[
            pl.BlockSpec((S, LANES), lambda s: (s, 0)),
            pl.BlockSpec((S, LANES), lambda s: (s, 0)),
            pl.BlockSpec((S, FL_PAD), lambda s: (s, 0)),
            _const_spec((1, FL_PAD)),
        ],
        out_specs=[pl.BlockSpec((S, FL_PAD), lambda s: (s, 0)), pl.BlockSpec((1, FL_PAD), lambda s: (0, 0))],
        out_shape=[jax.ShapeDtypeStruct((n_seq * S, FL_PAD), BF16), jax.ShapeDtypeStruct((1, FL_PAD), F32)],
        compiler_params=_params(("arbitrary",)),
    )(dfk, dfq, fl, b_pad)


def _in_proj_bwd(du, dq, dk, dv, dfl, dgates, x, dx1, g1, w_uqkv, w_fl, w_g):
    T = x.shape[0]
    tm = ROW_TILE

    def body(du_ref, dq_ref, dk_ref, dv_ref, dfl_ref, dgt_ref, x_ref, dx1_ref, g_ref, wa_ref, wf_ref, wg_ref, dx_ref, dg_ref):
        dh = _mm_nt(dgt_ref[...], wg_ref[...]) + _mm_nt(dfl_ref[...], wf_ref[...])
        for n, ref in enumerate((du_ref, dq_ref, dk_ref, dv_ref)):
            dh = dh + _mm_nt(ref[...], wa_ref[:, n * 512 : (n + 1) * 512])
        gv = g_ref[...]
        _, xh, r = _rms_fwd(x_ref[...], gv)
        dxn, dgrow = _rms_bwd(dh, xh, r, gv)
        dx_ref[...] = dx1_ref[...] + dxn

        @pl.when(pl.program_id(0) == 0)
        def _():
            dg_ref[...] = jnp.zeros_like(dg_ref)

        dg_ref[...] += jnp.sum(dgrow, axis=0, keepdims=True)

    row = lambda n: pl.BlockSpec((tm, n), lambda i: (i, 0))
    return pl.pallas_call(
        body,
        name="in_proj_bwd",
        grid=(T // tm,),
        in_specs=[
            row(512), row(512), row(512), row(512), row(FL_PAD), row(2 * D_MODEL), row(D_MODEL), row(D_MODEL), _const_spec((1, D_MODEL)),
            _const_spec(w_uqkv.shape), _const_spec(w_fl.shape), _const_spec(w_g.shape),
        ],
        out_specs=[row(D_MODEL), pl.BlockSpec((1, D_MODEL), lambda i: (0, 0))],
        out_shape=[jax.ShapeDtypeStruct((T, D_MODEL), F32), jax.ShapeDtypeStruct((1, D_MODEL), F32)],
        compiler_params=_params(("arbitrary",)),
    )(du, dq, dk, dv, dfl, dgates, x, dx1, g1, w_uqkv, w_fl, w_g)


def _pick_block(n):
    for b in (512, 1408, 256, 128):
        if n % b == 0:
            return b
    raise ValueError(n)


def _matmul_tn(a, b, name, token=None):
    T, K = a.shape
    N = b.shape[1]
    bt, bk, bn = min(T, DW_TOKENS), _pick_block(K), _pick_block(N)
    nt = T // bt

    def body(a_ref, b_ref, *rest):
        o_ref, acc = rest[-2:]

        @pl.when(pl.program_id(2) == 0)
        def _():
            acc[...] = jnp.zeros_like(acc)

        acc[...] += _mm_tn(a_ref[...].astype(BF16), b_ref[...].astype(BF16))

        @pl.when(pl.program_id(2) == nt - 1)
        def _():
            o_ref[...] = acc[...].astype(BF16)

    ordering = [] if token is None else [token]
    return pl.pallas_call(
        body,
        name=name,
        grid=(K // bk, N // bn, nt),
        in_specs=[pl.BlockSpec((bt, bk), lambda k, n, t: (t, k)), pl.BlockSpec((bt, bn), lambda k, n, t: (t, n))] + [_HBM] * len(ordering),
        out_specs=pl.BlockSpec((bk, bn), lambda k, n, t: (k, n)),
        out_shape=jax.ShapeDtypeStruct((K, N), BF16),
        scratch_shapes=[pltpu.VMEM((bk, bn), F32)],
        compiler_params=_params(("parallel", "parallel", "arbitrary")),
    )(a, b, *ordering)


def _position():
    return lax.axis_index("x"), lax.axis_index("y"), lax.axis_index("c")


_HBM = pl.BlockSpec(memory_space=pl.ANY)


def _all_gather(blocks, name):
    n = len(blocks)

    def body(*refs):
        xs, outs = refs[:n], refs[n : 2 * n]
        send_sems, recv_sems, local_sems = refs[2 * n :]
        x, y, c = _position()
        me, sibling = (x, y, c), (x, y, 1 - c)
        chips = [(1 - x, y), (x, 1 - y), (1 - x, 1 - y)]

        def rows(a, px, py, pc):
            return outs[a].at[4 * px + 2 * py + pc]

        def copy(a, k, blk, to, src=None):
            return pltpu.make_async_remote_copy(
                src_ref=rows(a, *blk) if src is None else src, dst_ref=rows(a, *blk),
                send_sem=send_sems.at[7 * a + k], recv_sem=recv_sems.at[7 * a + k], device_id=to, device_id_type=MESH,
            )

        mine = [pltpu.make_async_copy(xs[a], rows(a, *me), local_sems.at[a]) for a in range(n)]
        for cp in mine:
            cp.start()
        first = []
        for a in range(n):
            first.append(copy(a, 0, me, sibling, src=xs[a]))
            first += [copy(a, 1 + j, me, (*chip, c), src=xs[a]) for j, chip in enumerate(chips)]
        for cp in first:
            cp.start()
        passed = []
        for j, chip in enumerate(chips):
            for a in range(n):
                copy(a, 1 + j, (*chip, c), me).wait_recv()
                passed.append(copy(a, 4 + j, (*chip, c), sibling))
                passed[-1].start()
        for a in range(n):
            copy(a, 0, sibling, me).wait_recv()
        for j, chip in enumerate(chips):
            for a in range(n):
                copy(a, 4 + j, (*chip, 1 - c), me).wait_recv()
        for cp in first + passed:
            cp.wait_send()
        for cp in mine:
            cp.wait()

    return pl.pallas_call(
        body,
        name=name,
        out_shape=[jax.ShapeDtypeStruct((N_DEV, *b.shape), b.dtype) for b in blocks],
        in_specs=[_HBM] * n,
        out_specs=[_HBM] * n,
        scratch_shapes=[pltpu.SemaphoreType.DMA((7 * n,)), pltpu.SemaphoreType.DMA((7 * n,)), pltpu.SemaphoreType.DMA((n,))],
    )(*blocks)


_SEM = pl.BlockSpec(memory_space=pltpu.SEMAPHORE)
_HBM_ONLY = pl.BlockSpec(memory_space=pltpu.HBM)
_SIDE_EFFECT = pltpu.SideEffectType.DATAFLOW_SIDE_EFFECTING


def _peer(x, y, c, k):
    return (1 - x if k & 4 else x, 1 - y if k & 2 else y, 1 - c if k & 1 else c)


_PEER_BITS = {"gather": range(1, N_DEV), "scatter": range(1, N_DEV), "chips": (4, 2, 6)}
_LAND_SLOTS = {"gather": N_DEV, "scatter": N_DEV, "chips": 3}


def _exchange_copies(src_refs, land_refs, send_sems, recv_sems, pattern, receive_side):
    x, y, c = _position()
    me = 4 * x + 2 * y + c
    bits = _PEER_BITS[pattern]
    cps = []
    for j, k in enumerate(bits):
        px, py, pc = _peer(x, y, c, k)
        peer = 4 * px + 2 * py + pc
        for a, (src, land) in enumerate(zip(src_refs, land_refs)):
            if pattern == "chips":
                s, slot = src.at[2 * px + py], j
            else:
                s, slot = (src if pattern == "gather" else src.at[peer]), (peer if receive_side else me)
            cps.append(pltpu.make_async_remote_copy(
                src_ref=s, dst_ref=land.at[slot],
                send_sem=send_sems.at[len(bits) * a + j], recv_sem=recv_sems.at[len(bits) * a + j],
                device_id=(px, py, pc), device_id_type=MESH,
            ))
    return cps


def _exchange_start(srcs, after, name, pattern):
    n = len(srcs)
    m = len(_PEER_BITS[pattern])
    lands = [jax.ShapeDtypeStruct((_LAND_SLOTS[pattern], *s.shape[-2:]), s.dtype) for s in srcs]

    def body(*refs):
        src_refs, land_refs = refs[1 : 1 + n], refs[1 + n : 1 + 2 * n]
        send_sems, recv_sems = refs[1 + 2 * n], refs[2 + 2 * n]
        token = refs[-1]
        for cp in _exchange_copies(src_refs, land_refs, send_sems, recv_sems, pattern, receive_side=False):
            cp.start()
        token[...] = jnp.zeros_like(token)

    hbm = lambda t: pltpu.with_memory_space_constraint(t, pltpu.HBM)
    out = pl.pallas_call(
        body,
        name=name,
        out_shape=(
            pltpu.SemaphoreType.DMA((m * n,)), pltpu.SemaphoreType.DMA((m * n,)),
            *[pltpu.HBM(s.shape, s.dtype) for s in srcs], *[pltpu.HBM(l.shape, l.dtype) for l in lands],
            jax.ShapeDtypeStruct((8, LANES), F32),
        ),
        in_specs=(_HBM, *[_HBM_ONLY] * (2 * n)),
        out_specs=(_SEM, _SEM, *[_HBM_ONLY] * (2 * n), pl.BlockSpec(memory_space=pltpu.VMEM)),
        input_output_aliases={1 + i: 2 + i for i in range(2 * n)},
        compiler_params=pltpu.CompilerParams(has_side_effects=_SIDE_EFFECT),
    )(after, *[hbm(s) for s in srcs], *[hbm(lax.empty(l.shape, l.dtype)) for l in lands])
    return out[0], out[1], out[2 : 2 + n], out[2 + n : 2 + 2 * n], out[-1]


def _exchange_wait(send_sems, recv_sems, srcs, lands, after, name, pattern):
    n = len(srcs)

    def body(*refs):
        src_refs, land_refs = refs[:n], refs[n : 2 * n]
        for cp in _exchange_copies(src_refs, land_refs, refs[2 * n], refs[2 * n + 1], pattern, receive_side=True):
            cp.wait_send()
            cp.wait_recv()

    out = pl.pallas_call(
        body,
        name=name,
        out_shape=(*[pltpu.HBM(s.shape, s.dtype) for s in srcs], *[pltpu.HBM(l.shape, l.dtype) for l in lands]),
        in_specs=(*[_HBM_ONLY] * (2 * n), _SEM, _SEM, _HBM),
        out_specs=tuple([_HBM_ONLY] * (2 * n)),
        input_output_aliases={i: i for i in range(2 * n)},
        compiler_params=pltpu.CompilerParams(has_side_effects=_SIDE_EFFECT),
    )(*srcs, *lands, send_sems, recv_sems, after)
    return out[:n], out[n:]


def _sibling_exchange(sends):
    n = len(sends)

    def body(*refs):
        srcs, dsts = refs[:n], refs[n : 2 * n]
        send_sems, recv_sems = refs[2 * n :]
        x, y, c = _position()
        cps = [
            pltpu.make_async_remote_copy(
                src_ref=srcs[a].at[1 - c], dst_ref=dsts[a], send_sem=send_sems.at[a], recv_sem=recv_sems.at[a],
                device_id=(x, y, 1 - c), device_id_type=MESH,
            )
            for a in range(n)
        ]
        for cp in cps:
            cp.start()
        for cp in cps:
            cp.wait()

    return pl.pallas_call(
        body,
        name="rs_sibling",
        out_shape=[jax.ShapeDtypeStruct(s.shape[1:], s.dtype) for s in sends],
        in_specs=[_HBM] * n,
        out_specs=[_HBM] * n,
        scratch_shapes=[pltpu.SemaphoreType.DMA((n,)), pltpu.SemaphoreType.DMA((n,))],
    )(*sends)


def _rows_tile(r):
    return ROW_TILE if r % ROW_TILE == 0 else r


def _pair_sum(send, got, core, name):
    _, _, r, c = send.shape
    br = _rows_tile(r)

    def body(core_ref, a_ref, b_ref, o_ref):
        o_ref[...] = (a_ref[...].astype(F32) + b_ref[...].astype(F32)).astype(o_ref.dtype)

    return pl.pallas_call(
        body,
        name=name,
        grid_spec=pltpu.PrefetchScalarGridSpec(
            num_scalar_prefetch=1,
            grid=(4, r // br),
            in_specs=[
                pl.BlockSpec((None, None, br, c), lambda n, i, core: (core[0], n, i, 0)),
                pl.BlockSpec((None, br, c), lambda n, i, core: (n, i, 0)),
            ],
            out_specs=pl.BlockSpec((None, br, c), lambda n, i, core: (n, i, 0)),
        ),
        out_shape=jax.ShapeDtypeStruct((4, r, c), send.dtype),
        compiler_params=_params(("parallel", "parallel")),
    )(core, send, got)


def _adamw(w, g, m, v):
    m = ADAM_B1 * m + (1.0 - ADAM_B1) * g
    v = ADAM_B2 * v + (1.0 - ADAM_B2) * (g * g)
    m_hat = m / (1.0 - ADAM_B1 ** ADAM_STEP)
    v_hat = v / (1.0 - ADAM_B2 ** ADAM_STEP)
    delta = -ADAM_LR * (m_hat / (jnp.sqrt(v_hat) + ADAM_EPS) + ADAM_WD * w)
    return delta, m, v


def _shard_update(send, got, recv, w, m, v, pos, name):
    _, r, c = w.shape
    br = _rows_tile(r)

    def body(pos_ref, a_ref, b_ref, r_ref, w_ref, m_ref, v_ref, g_ref, d_ref, nm_ref, nv_ref):
        g = a_ref[...].astype(F32) + b_ref[...].astype(F32)
        for n in range(3):
            g = g + r_ref[n].astype(F32)
        g_ref[...] = g
        d_ref[...], nm_ref[...], nv_ref[...] = _adamw(w_ref[...], g, m_ref[...], v_ref[...])

    own = pl.BlockSpec((None, br, c), lambda i, pos: (0, i, 0))
    return pl.pallas_call(
        body,
        name=name,
        grid_spec=pltpu.PrefetchScalarGridSpec(
            num_scalar_prefetch=1,
            grid=(r // br,),
            in_specs=[
                pl.BlockSpec((None, None, br, c), lambda i, pos: (pos[0], pos[1], i, 0)),
                pl.BlockSpec((None, br, c), lambda i, pos: (pos[1], i, 0)),
                pl.BlockSpec((3, br, c), lambda i, pos: (0, i, 0)),
                own, own, own,
            ],
            out_specs=[own, own, own, own],
        ),
        out_shape=[jax.ShapeDtypeStruct((1, r, c), F32)] * 4,
        compiler_params=_params(("parallel",)),
    )(pos, send, got, recv, w, m, v)


def _shard_update_direct(parts, w, m, v, name):
    _, r, c = w.shape
    br = _rows_tile(r)

    def body(p_ref, w_ref, m_ref, v_ref, g_ref, d_ref, nm_ref, nv_ref):
        g = p_ref[0].astype(F32)
        for n in range(1, N_DEV):
            g = g + p_ref[n].astype(F32)
        g_ref[...] = g
        d_ref[...], nm_ref[...], nv_ref[...] = _adamw(w_ref[...], g, m_ref[...], v_ref[...])

    own = pl.BlockSpec((None, br, c), lambda i: (0, i, 0))
    return pl.pallas_call(
        body,
        name=name,
        grid=(r // br,),
        in_specs=[pl.BlockSpec((N_DEV, br, c), lambda i: (0, i, 0)), own, own, own],
        out_specs=[own, own, own, own],
        out_shape=[jax.ShapeDtypeStruct((1, r, c), F32)] * 4,
        compiler_params=_params(("parallel",)),
    )(parts, w, m, v)


def _small_update(parts, w, m, v):
    R = w.shape[0]

    def body(p_ref, w_ref, m_ref, v_ref, g_ref, d_ref, nm_ref, nv_ref):
        g = p_ref[0]
        for n in range(1, N_DEV):
            g = g + p_ref[n]
        g_ref[...] = g
        d_ref[...], nm_ref[...], nv_ref[...] = _adamw(w_ref[...], g, m_ref[...], v_ref[...])

    return pl.pallas_call(
        body,
        name="small_update",
        out_shape=[jax.ShapeDtypeStruct((R, LANES), F32)] * 4,
        compiler_params=pltpu.CompilerParams(vmem_limit_bytes=VMEM_LIMIT),
    )(parts, w, m, v)


_SHARD_AXIS = (1, 1, 1, 0, 1, 1, 0)


def _full_from_gathered(t, axis):
    if axis == 0:
        return t.reshape(N_DEV * t.shape[1], t.shape[2])
    return jnp.concatenate([t[d] for d in range(N_DEV)], axis=1)


def _chunks_from_cols(t):
    c = t.shape[1] // N_DEV
    return jnp.stack([t[:, d * c : (d + 1) * c] for d in range(N_DEV)])


def _send_from_cols(t):
    c = t.shape[1] // N_DEV
    return jnp.stack([jnp.stack([t[:, (2 * chip + core) * c : (2 * chip + core + 1) * c] for chip in range(4)]) for core in range(2)])


_SMALL = (("norm1_g", 8), ("norm2_g", 8), ("norm_f_g", 8), ("b_forget", 8), ("pool_scale", 8), ("pool_mix", 512))
_SMALL_ROWS = sum(r for _, r in _SMALL) + 8


def _pack_small(vals, loss_row):
    parts = []
    for (name, rows), t in zip(_SMALL, vals):
        f = t.astype(F32).reshape(-1)
        f = jnp.concatenate([f, jnp.zeros((rows * LANES - f.shape[0],), F32)]).reshape(rows, LANES)
        parts.append(f)
    parts.append(loss_row)
    return jnp.concatenate(parts, axis=0)


def _unpack_small(packed, shapes):
    out, off = [], 0
    for (name, rows), shape in zip(_SMALL, shapes):
        n = 1
        for s in shape:
            n *= s
        out.append(packed[off : off + rows].reshape(-1)[:n].reshape(shape))
        off += rows
    return out, packed[off, 0]


def _local_grads(x, tgt, g1, g2, gf, b_forget, pool_mix, pool_scale, w_in, fwd_token, out_weights, ffn_weights, ffn_grads_out, out_grads_out, small_grads_out):
    n_seq, S, _ = x.shape
    T = n_seq * S
    x2 = x.reshape(T, D_MODEL)
    tg2 = tgt.reshape(T, D_MODEL)
    w_uqkv = w_in[:, : POOL_WIDTH + 3 * ATTN_WIDTH]
    w_fl = jnp.concatenate([w_in[:, 2048 : 2048 + N_HEADS], jnp.zeros((D_MODEL, FL_PAD - N_HEADS), BF16)], axis=1)
    w_g = w_in[:, 2048 + N_HEADS :]
    b_pad = jnp.concatenate([b_forget.reshape(1, N_HEADS), jnp.zeros((1, FL_PAD - N_HEADS), F32)], axis=1)
    mix_b = pool_mix.reshape(len(POOL_WINDOWS), GROUP_DIM, GROUP_DIM).astype(BF16)
    scale = pool_scale.reshape(1, POOL_WIDTH)
    g1 = g1.reshape(1, D_MODEL)
    g2 = g2.reshape(1, D_MODEL)
    gf = gf.reshape(1, D_MODEL)

    h, u, qkv, fl, gates = _in_proj(x2, g1, w_uqkv, w_fl, w_g, fwd_token)
    fcol = _forget_fwd(fl, b_pad, n_seq, S)
    pm, p2, p3 = _pool_fwd(u, mix_b, scale, n_seq, S)
    a, lse = _attn_fwd(qkv, fcol, n_seq, S)
    w_po, w_ao, w_out = out_weights(a)
    merged, x1, attn_y, pool_y = _mix_out(a, p3, gates, x2, w_ao, w_po, w_out)
    w_gate, w_up, w_down = ffn_weights(x1)
    h2, gate, up, act, dx2, loss_rows, dgf = _ffn_fwd(x1, g2, gf, tg2, w_gate, w_up, w_down)

    dgate, dup, dx1, dg2 = _ffn_bwd(dx2, gate, up, x1, g2, w_gate, w_up, w_down)
    bwd_token = ffn_grads_out(_matmul_tn(h2, dgate, "dw_ffn_gate"), _matmul_tn(h2, dup, "dw_ffn_up"), _matmul_tn(act, dx2, "dw_ffn_down"))
    dgates, dpy, day, da, dp2, dscale = _mix_bwd(dx1, gates, pool_y, attn_y, p2, scale, w_out, w_ao, w_po, bwd_token)
    out_token = out_grads_out(_matmul_tn(p3, dpy, "dw_pool_out"), _matmul_tn(a, day, "dw_attn_out"), _matmul_tn(merged, dx1, "dw_out"))
    du, dmix = _pool_bwd(dp2, pm, mix_b, out_token, n_seq, S)
    dq, dk, dv, dfk, dfq = _attn_bwd(qkv, da, a, fcol, lse, n_seq, S)
    dfl, db = _forget_bwd(dfk, dfq, fl, b_pad, n_seq, S)
    dx, dg1 = _in_proj_bwd(du, dq, dk, dv, dfl, dgates, x2, dx1, g1, w_uqkv, w_fl, w_g)
    small_token = small_grads_out((dg1, dg2, dgf, db[:, :N_HEADS], dscale, dmix), loss_rows)

    d_w_in = jnp.concatenate(
        [
            _matmul_tn(h, du, "dw_u", small_token), _matmul_tn(h, dq, "dw_q"), _matmul_tn(h, dk, "dw_k"), _matmul_tn(h, dv, "dw_v"),
            _matmul_tn(h, dfl, "dw_fl")[:, :N_HEADS], _matmul_tn(h, dgates, "dw_gates"),
        ],
        axis=1,
    )
    return dx.reshape(n_seq, S, D_MODEL), _send_from_cols(d_w_in)


def kernel(x, norm1_g, w_in, b_forget, pool_mix, pool_scale, w_pool_out, w_attn_out, w_out, norm2_g, w_ffn_gate, w_ffn_up, w_ffn_down, norm_f_g, loss_target, m_norm1_g, m_w_in, m_b_forget, m_pool_mix, m_pool_scale, m_w_pool_out, m_w_attn_out, m_w_out, m_norm2_g, m_w_ffn_gate, m_w_ffn_up, m_w_ffn_down, m_norm_f_g, v_norm1_g, v_w_in, v_b_forget, v_pool_mix, v_pool_scale, v_w_pool_out, v_w_attn_out, v_w_out, v_norm2_g, v_w_ffn_gate, v_w_ffn_up, v_w_ffn_down, v_norm_f_g):
    names = ("w_in", "w_pool_out", "w_attn_out", "w_out", "w_ffn_gate", "w_ffn_up", "w_ffn_down")
    w_sh = (w_in, w_pool_out, w_attn_out, w_out, w_ffn_gate, w_ffn_up, w_ffn_down)
    m_sh = (m_w_in, m_w_pool_out, m_w_attn_out, m_w_out, m_w_ffn_gate, m_w_ffn_up, m_w_ffn_down)
    v_sh = (v_w_in, v_w_pool_out, v_w_attn_out, v_w_out, v_w_ffn_gate, v_w_ffn_up, v_w_ffn_down)

    cx, cy, cc = _position()
    me = 4 * cx + 2 * cy + cc
    shards = [w[0].astype(BF16) for w in w_sh]
    (gathered_in,) = _all_gather(shards[:1], "w_in_all_gather")
    out_sems = _exchange_start(shards[1:4], gathered_in, "out_weights_gather_start", "gather")
    ffn_sems = _exchange_start(shards[4:], out_sems[4], "ffn_weights_gather_start", "gather")
    no_order = jnp.zeros((8, LANES), F32)

    def with_own(lands, own):
        return [lax.dynamic_update_slice(l, o[None], (me, 0, 0)) for l, o in zip(lands, own)]

    def gathered_weights(sems, axes, name):
        def wait(after):
            send_sems, recv_sems, srcs, lands, _ = sems
            srcs, lands = _exchange_wait(send_sems, recv_sems, srcs, lands, after, name, "gather")
            return [_full_from_gathered(t, axis) for t, axis in zip(with_own(lands, srcs), axes)]

        return wait

    started = {}

    def scatter_grads(key, name):
        def start(*whole_grads):
            chunks = [
                _chunks_from_cols(t) if axis == 1 else t.reshape(N_DEV, -1, t.shape[1])
                for t, axis in zip(whole_grads, _SHARD_AXIS[key])
            ]
            started[key] = _exchange_start(chunks, no_order, name, "scatter")
            return started[key][4]

        return start

    def gather_small(small, loss_rows):
        started["small"] = _exchange_start([_pack_small(small, loss_rows)], no_order, "small_grads_gather_start", "gather")
        return started["small"][4]

    ffn, out = slice(4, 7), slice(1, 4)
    grad_x, send_in = _local_grads(
        x, loss_target, norm1_g, norm2_g, norm_f_g, b_forget, pool_mix, pool_scale, _full_from_gathered(gathered_in, 1), ffn_sems[4],
        gathered_weights(out_sems, _SHARD_AXIS[out], "out_weights_gather_wait"),
        gathered_weights(ffn_sems, _SHARD_AXIS[ffn], "ffn_weights_gather_wait"),
        scatter_grads(ffn, "ffn_grads_scatter_start"), scatter_grads(out, "out_grads_scatter_start"), gather_small,
    )

    core = jnp.reshape(cc, (1,)).astype(jnp.int32)
    pos = jnp.stack([cc, 2 * cx + cy]).astype(jnp.int32)
    (got_in,) = _sibling_exchange([send_in])
    pair_in = _pair_sum(send_in, got_in, core, "pair_sum_w_in")
    chip_sems = _exchange_start([pair_in], no_order, "w_in_grads_chips_start", "chips")

    def scattered_updates(key, after, name):
        send_sems, recv_sems, srcs, lands, _ = started[key]
        srcs, lands = _exchange_wait(send_sems, recv_sems, srcs, lands, after, name, "scatter")
        own = [lax.dynamic_index_in_dim(s, me, 0, keepdims=False) for s in srcs]
        return [
            _shard_update_direct(p, w, m, v, "update_" + n)
            for p, w, m, v, n in zip(with_own(lands, own), w_sh[key], m_sh[key], v_sh[key], names[key])
        ]

    updates_out = scattered_updates(out, chip_sems[4], "out_grads_scatter_wait")
    updates_ffn = scattered_updates(ffn, chip_sems[4], "ffn_grads_scatter_wait")

    small_w = (norm1_g, norm2_g, norm_f_g, b_forget, pool_scale, pool_mix)
    small_m = (m_norm1_g, m_norm2_g, m_norm_f_g, m_b_forget, m_pool_scale, m_pool_mix)
    small_v = (v_norm1_g, v_norm2_g, v_norm_f_g, v_b_forget, v_pool_scale, v_pool_mix)
    zero_row = jnp.zeros((8, LANES), F32)
    send_sems, recv_sems, srcs, lands, _ = started["small"]
    srcs, lands = _exchange_wait(send_sems, recv_sems, srcs, lands, updates_ffn[-1][0], "small_grads_gather_wait", "gather")
    (parts,) = with_own(lands, srcs)
    g_s, d_s, nm_s, nv_s = _small_update(parts, _pack_small(small_w, zero_row), _pack_small(small_m, zero_row), _pack_small(small_v, zero_row))

    send_sems, recv_sems, srcs, lands, _ = chip_sems
    _, (recv_in,) = _exchange_wait(send_sems, recv_sems, srcs, lands, g_s, "w_in_grads_chips_wait", "chips")
    update_in = _shard_update(send_in, got_in, recv_in, w_in, m_w_in, v_w_in, pos, "update_w_in")
    g_w, d_w, nm_w, nv_w = zip(*([update_in] + updates_out + updates_ffn))
    shapes = [t.shape for t in small_w]
    (g1, g2, gf, gb, gsc, gmix), loss = _unpack_small(g_s, shapes)
    (d1, d2, df, db_, dsc, dmx), _ = _unpack_small(d_s, shapes)
    (m1, m2, mf, mb, msc, mmx), _ = _unpack_small(nm_s, shapes)
    (v1, v2, vf, vb, vsc, vmx), _ = _unpack_small(nv_s, shapes)

    def ordered(n1, win, b, mix, sc, wpo, wao, wout, n2, wg, wu, wd, nf):
        return (n1, win, b, mix, sc, wpo, wao, wout, n2, wg, wu, wd, nf)

    grads = ordered(g1, g_w[0], gb, gmix, gsc, g_w[1], g_w[2], g_w[3], g2, g_w[4], g_w[5], g_w[6], gf)
    deltas = ordered(d1, d_w[0], db_, dmx, dsc, d_w[1], d_w[2], d_w[3], d2, d_w[4], d_w[5], d_w[6], df)
    new_m = ordered(m1, nm_w[0], mb, mmx, msc, nm_w[1], nm_w[2], nm_w[3], m2, nm_w[4], nm_w[5], nm_w[6], mf)
    new_v = ordered(v1, nv_w[0], vb, vmx, vsc, nv_w[1], nv_w[2], nv_w[3], v2, nv_w[4], nv_w[5], nv_w[6], vf)
    return (loss, grad_x, *grads, *deltas, *new_m, *new_v)
```

```python
import functools

import jax
import jax.numpy as jnp
from jax import lax
from jax.experimental import pallas as pl
from jax.experimental.pallas import tpu as pltpu

F32 = jnp.float32
BF16 = jnp.bfloat16
MESH = pl.DeviceIdType.MESH

D_MODEL = 1024
POOL_WINDOWS = (2, 4, 8, 16)
POOL_WIDTH = 512
GROUP_DIM = 128
ATTN_WIDTH = 512
HEAD_DIM = 64
N_HEADS = 8
N_PAIRS = 4
D_FF = 2816
RMS_EPS = 1e-6
N_DEV = 8
LANES = 128
FL_PAD = 128

ADAM_LR = 0.001
ADAM_B1 = 0.9
ADAM_B2 = 0.999
ADAM_EPS = 1e-08
ADAM_WD = 0.01
ADAM_STEP = 10

VMEM_LIMIT = 56 * 1024 * 1024
VMEM_LIMIT_MAX = 60 * 1024 * 1024
ROW_TILE = 512
ATTN_BLOCK = 512
FF_CHUNK = 256
FF_ROW_TILE = 512
DW_TOKENS = 2048


def _mm(a, b):
    return jnp.dot(a, b, preferred_element_type=F32)


def _mm_nt(a, b):
    return lax.dot_general(a, b, (((1,), (1,)), ((), ())), preferred_element_type=F32)


def _mm_tn(a, b):
    return lax.dot_general(a, b, (((0,), (0,)), ((), ())), preferred_element_type=F32)


def _sigmoid(x):
    return 1.0 / (1.0 + jnp.exp(-x))


def _params(sem, vmem=VMEM_LIMIT):
    return pltpu.CompilerParams(dimension_semantics=sem, vmem_limit_bytes=vmem)


def _const_spec(shape):
    nd = len(shape)
    return pl.BlockSpec(shape, lambda *_: (0,) * nd, pipeline_mode=pl.Buffered(1))


def _rms_fwd(x, g):
    r = lax.rsqrt(jnp.mean(x * x, axis=-1, keepdims=True) + RMS_EPS)
    xh = x * r
    return xh * g, xh, r


def _rms_bwd(dy, xh, r, g):
    dxh = dy * g
    dx = r * (dxh - xh * jnp.mean(dxh * xh, axis=-1, keepdims=True))
    return dx, dy * xh


def _in_proj(x, g1, w_uqkv, w_fl, w_g, token):
    T = x.shape[0]
    tm = ROW_TILE

    def body(x_ref, g_ref, wa_ref, wf_ref, wg_ref, token_ref, h_ref, u_ref, qkv_ref, fl_ref, gt_ref):
        h, _, _ = _rms_fwd(x_ref[...], g_ref[...])
        hb = h.astype(BF16)
        h_ref[...] = hb
        z = _mm(hb, wa_ref[...])
        u_ref[...] = z[:, :POOL_WIDTH]
        qkv_ref[...] = z[:, POOL_WIDTH:].astype(BF16)
        fl_ref[...] = _mm(hb, wf_ref[...])
        gt_ref[...] = _mm(hb, wg_ref[...]).astype(BF16)

    row = lambda n: pl.BlockSpec((tm, n), lambda i: (i, 0))
    return pl.pallas_call(
        body,
        name="in_proj",
        grid=(T // tm,),
        in_specs=[row(D_MODEL), _const_spec((1, D_MODEL)), _const_spec(w_uqkv.shape), _const_spec(w_fl.shape), _const_spec(w_g.shape), _HBM],
        out_specs=[row(D_MODEL), row(POOL_WIDTH), row(3 * ATTN_WIDTH), row(FL_PAD), row(2 * D_MODEL)],
        out_shape=[
            jax.ShapeDtypeStruct((T, D_MODEL), BF16),
            jax.ShapeDtypeStruct((T, POOL_WIDTH), F32),
            jax.ShapeDtypeStruct((T, 3 * ATTN_WIDTH), BF16),
            jax.ShapeDtypeStruct((T, FL_PAD), F32),
            jax.ShapeDtypeStruct((T, 2 * D_MODEL), BF16),
        ],
        compiler_params=_params(("parallel",)),
    )(x, g1, w_uqkv, w_fl, w_g, token)


def _log_sigmoid(x):
    return jnp.minimum(x, 0.0) - jnp.log(1.0 + jnp.exp(-jnp.abs(x)))


def _forget_fwd(fl, b_pad, n_seq, S):
    def body(fl_ref, b_ref, fcol_ref):
        lf = _log_sigmoid(fl_ref[...] + b_ref[...])
        t = lf.T
        lane = lax.broadcasted_iota(jnp.int32, t.shape, 1)
        k = 1
        while k < S:
            t = t + jnp.where(lane >= k, pltpu.roll(t, k, 1), 0.0)
            k *= 2
        fcol_ref[...] = t.T

    return pl.pallas_call(
        body,
        name="forget_fwd",
        grid=(n_seq,),
        in_specs=[pl.BlockSpec((S, FL_PAD), lambda s: (s, 0)), _const_spec((1, FL_PAD))],
        out_specs=pl.BlockSpec((S, FL_PAD), lambda s: (s, 0)),
        out_shape=jax.ShapeDtypeStruct((n_seq * S, FL_PAD), F32),
        compiler_params=_params(("parallel",)),
    )(fl, b_pad)


def _window_pick(g, v2, v4, v8, v16):
    return jnp.where(g == 0, v2, jnp.where(g == 1, v4, jnp.where(g == 2, v8, v16)))


def _pool_fwd(u, mix_b, scale, n_seq, S):
    T = n_seq * S

    def body(u_ref, mix_ref, sc_ref, pm_ref, p2_ref, p3_ref):
        g = pl.program_id(1)
        uu = u_ref[...]
        row = lax.broadcasted_iota(jnp.int32, uu.shape, 0)

        def back(a, k):
            return jnp.where(row >= k, pltpu.roll(a, k, 0), 0.0)

        s2 = uu + back(uu, 1)
        s4 = s2 + back(s2, 2)
        s8 = s4 + back(s4, 4)
        s16 = s8 + back(s8, 8)
        w = _window_pick(g, 2.0, 4.0, 8.0, 16.0)
        cnt = jnp.minimum((row + 1).astype(F32), w)
        pm = _window_pick(g, s2, s4, s8, s16) / cnt - uu
        pmb = pm.astype(BF16)
        pm_ref[...] = pmb
        p2 = _mm(pmb, mix_ref[...])
        p2_ref[...] = p2
        p3_ref[...] = (p2 * sc_ref[...]).astype(BF16)

    grp = pl.BlockSpec((S, GROUP_DIM), lambda s, g: (s, g))
    return pl.pallas_call(
        body,
        name="pool_fwd",
        grid=(n_seq, len(POOL_WINDOWS)),
        in_specs=[
            grp,
            pl.BlockSpec((None, GROUP_DIM, GROUP_DIM), lambda s, g: (g, 0, 0)),
            pl.BlockSpec((1, GROUP_DIM), lambda s, g: (0, g)),
        ],
        out_specs=[grp, grp, grp],
        out_shape=[
            jax.ShapeDtypeStruct((T, POOL_WIDTH), BF16),
            jax.ShapeDtypeStruct((T, POOL_WIDTH), F32),
            jax.ShapeDtypeStruct((T, POOL_WIDTH), BF16),
        ],
        compiler_params=_params(("parallel", "parallel")),
    )(u, mix_b, scale)


def _split3(v):
    hi = v.astype(BF16).astype(F32)
    r = v - hi
    mid = r.astype(BF16).astype(F32)
    lo = (r - mid).astype(BF16).astype(F32)
    return hi, mid, lo


def _augment(xp, hh, first, second):
    lane = lax.broadcasted_iota(jnp.int32, (1, LANES), 1)
    head = (lane >= HEAD_DIM * hh) & (lane < HEAD_DIM * (hh + 1))
    b = HEAD_DIM * (1 - hh)
    out = jnp.where(head, xp.astype(F32), 0.0)
    for n, col in enumerate(tuple(first) + tuple(second)):
        out = jnp.where(lane == b + n, col, out)
    return out.astype(BF16)


def _attn_fwd(qkv, fcol, n_seq, S):
    T = n_seq * S
    tb = ATTN_BLOCK
    nq = S // tb
    scale = HEAD_DIM ** -0.5

    def body(q_ref, k_ref, v_ref, fc_ref, o_ref, st_ref, qa_sc, ka_sc, m_sc, l_sc, acc_sc):
        i = pl.program_id(1)
        lane = lax.broadcasted_iota(jnp.int32, (1, LANES), 1)
        low = lane < HEAD_DIM
        ones = (1.0, 1.0, 1.0)

        @pl.when(i == 0)
        def _():
            def rows_ka(r, carry):
                r0 = pl.multiple_of(r * tb, tb)
                for h in range(N_HEADS):
                    kp = k_ref[pl.ds(r0, tb), (h // 2) * LANES : (h // 2 + 1) * LANES] * scale
                    fk = fc_ref[pl.ds(r0, tb), h : h + 1]
                    ka_sc[h, pl.ds(r0, tb), :] = _augment(kp, h % 2, ones, _split3(-fk))
                return carry

            lax.fori_loop(0, nq, rows_ka, 0)

        q0 = pl.multiple_of(i * tb, tb)
        for h in range(N_HEADS):
            qp = q_ref[:, (h // 2) * LANES : (h // 2 + 1) * LANES]
            qa_sc[h] = _augment(qp, h % 2, _split3(fc_ref[pl.ds(q0, tb), h : h + 1]), ones)
        m_sc[...] = jnp.full(m_sc.shape, -jnp.inf, F32)
        l_sc[...] = jnp.zeros_like(l_sc)
        acc_sc[...] = jnp.zeros_like(acc_sc)
        causal = lax.broadcasted_iota(jnp.int32, (tb, tb), 1) <= lax.broadcasted_iota(jnp.int32, (tb, tb), 0)

        def step(j, masked):
            c0 = pl.multiple_of(j * tb, tb)
            for p in range(N_PAIRS):
                vb = v_ref[pl.ds(c0, tb), p * LANES : (p + 1) * LANES]
                pv, al = [], []
                for hh in range(2):
                    h = 2 * p + hh
                    s = _mm_nt(qa_sc[h], ka_sc[h, pl.ds(c0, tb), :])
                    if masked:
                        s = jnp.where(causal, s, -jnp.inf)
                    m_old = m_sc[h]
                    m_new = jnp.maximum(m_old, jnp.max(s, axis=1, keepdims=True))
                    alpha = jnp.exp(m_old - m_new)
                    pe = jnp.exp(s - jnp.concatenate([m_new] * (tb // LANES), axis=1))
                    l_sc[h] = alpha * l_sc[h] + jnp.sum(pe, axis=1, keepdims=True)
                    m_sc[h] = m_new
                    pv.append(_mm(pe.astype(BF16), vb))
                    al.append(alpha)
                acc_sc[p] = jnp.where(low, al[0], al[1]) * acc_sc[p] + jnp.where(low, pv[0], pv[1])

        def loop_body(j, carry):
            step(j, False)
            return carry

        lax.fori_loop(0, i, loop_body, 0)
        step(i, True)
        st = jnp.zeros((tb, LANES), F32)
        for p in range(N_PAIRS):
            lp = jnp.where(low, l_sc[2 * p], l_sc[2 * p + 1])
            o_ref[:, p * LANES : (p + 1) * LANES] = (acc_sc[p] / lp).astype(BF16)
            for h in (2 * p, 2 * p + 1):
                st = jnp.where(lane == h, m_sc[h] + jnp.log(l_sc[h]), st)
        st_ref[...] = st

    return pl.pallas_call(
        body,
        name="attn_fwd",
        grid=(n_seq, nq),
        in_specs=[
            pl.BlockSpec((tb, ATTN_WIDTH), lambda s, i: (s * nq + i, 0)),
            pl.BlockSpec((S, ATTN_WIDTH), lambda s, i: (s, 1)),
            pl.BlockSpec((S, ATTN_WIDTH), lambda s, i: (s, 2)),
            pl.BlockSpec((S, LANES), lambda s, i: (s, 0)),
        ],
        out_specs=[
            pl.BlockSpec((tb, ATTN_WIDTH), lambda s, i: (s * nq + i, 0)),
            pl.BlockSpec((tb, LANES), lambda s, i: (s * nq + i, 0)),
        ],
        out_shape=[jax.ShapeDtypeStruct((T, ATTN_WIDTH), BF16), jax.ShapeDtypeStruct((T, LANES), F32)],
        scratch_shapes=[
            pltpu.VMEM((N_HEADS, tb, LANES), BF16),
            pltpu.VMEM((N_HEADS, S, LANES), BF16),
            pltpu.VMEM((N_HEADS, tb, LANES), F32),
            pltpu.VMEM((N_HEADS, tb, LANES), F32),
            pltpu.VMEM((N_PAIRS, tb, LANES), F32),
        ],
        compiler_params=_params(("parallel", "arbitrary")),
    )(qkv, qkv, qkv, fcol)


def _mix_out(a, p3, gates, x, w_ao, w_po, w_out):
    T = x.shape[0]
    tm = ROW_TILE

    def body(a_ref, p3_ref, gt_ref, x_ref, wao_ref, wpo_ref, wout_ref, mg_ref, x1_ref, ay_ref, py_ref):
        ay = _mm(a_ref[...], wao_ref[...])
        py = _mm(p3_ref[...], wpo_ref[...])
        ay_ref[...] = ay.astype(BF16)
        py_ref[...] = py.astype(BF16)
        sp = _sigmoid(gt_ref[:, :D_MODEL].astype(F32))
        sa = _sigmoid(gt_ref[:, D_MODEL:].astype(F32))
        mb = (sp * py + sa * ay).astype(BF16)
        mg_ref[...] = mb
        x1_ref[...] = x_ref[...] + _mm(mb, wout_ref[...])

    row = lambda n: pl.BlockSpec((tm, n), lambda i: (i, 0))
    return pl.pallas_call(
        body,
        name="mix_out",
        grid=(T // tm,),
        in_specs=[
            row(ATTN_WIDTH), row(POOL_WIDTH), row(2 * D_MODEL), row(D_MODEL),
            _const_spec(w_ao.shape), _const_spec(w_po.shape), _const_spec(w_out.shape),
        ],
        out_specs=[row(D_MODEL), row(D_MODEL), row(D_MODEL), row(D_MODEL)],
        out_shape=[
            jax.ShapeDtypeStruct((T, D_MODEL), BF16), jax.ShapeDtypeStruct((T, D_MODEL), F32),
            jax.ShapeDtypeStruct((T, D_MODEL), BF16), jax.ShapeDtypeStruct((T, D_MODEL), BF16),
        ],
        compiler_params=_params(("parallel",)),
    )(a, p3, gates, x, w_ao, w_po, w_out)


def _ffn_fwd(x1, g2, gf, tgt, w_gate_t, w_up_t, w_down):
    T = x1.shape[0]
    tm = min(T, FF_ROW_TILE)
    nt = T // tm
    nc = D_FF // FF_CHUNK

    def body(x1_ref, g2_ref, gf_ref, tg_ref, wg_ref, wu_ref, wd_ref, h2_ref, gate_ref, up_ref, act_ref, dx2_ref, loss_ref, dgf_ref):
        x1v = x1_ref[...]
        h2, _, _ = _rms_fwd(x1v, g2_ref[...])
        h2b = h2.astype(BF16)
        h2_ref[...] = h2b
        acc = x1v
        for c in range(nc):
            sl = slice(c * FF_CHUNK, (c + 1) * FF_CHUNK)
            gate = _mm_nt(h2b, wg_ref[sl, :])
            up = _mm_nt(h2b, wu_ref[sl, :])
            gate_ref[:, sl] = gate.astype(BF16)
            up_ref[:, sl] = up.astype(BF16)
            act = (gate * _sigmoid(gate) * up).astype(BF16)
            act_ref[:, sl] = act
            acc = acc + _mm(act, wd_ref[sl, :])
        gfv = gf_ref[...]
        y, xh, r = _rms_fwd(acc, gfv)
        err = y - tg_ref[...]
        part = 0.5 * jnp.sum(jnp.mean(err * err, axis=-1, keepdims=True), axis=0, keepdims=True)
        dx2, dgrow = _rms_bwd(err * (1.0 / D_MODEL), xh, r, gfv)
        dx2_ref[...] = dx2

        @pl.when(pl.program_id(0) == 0)
        def _():
            dgf_ref[...] = jnp.zeros_like(dgf_ref)
            loss_ref[...] = jnp.zeros_like(loss_ref)

        dgf_ref[...] += jnp.sum(dgrow, axis=0, keepdims=True)
        loss_ref[...] += jnp.broadcast_to(part, loss_ref.shape)

    row = lambda n: pl.BlockSpec((tm, n), lambda i: (i, 0))
    return pl.pallas_call(
        body,
        name="ffn_fwd",
        grid=(nt,),
        in_specs=[
            row(D_MODEL), _const_spec((1, D_MODEL)), _const_spec((1, D_MODEL)), row(D_MODEL),
            _const_spec(w_gate_t.shape), _const_spec(w_up_t.shape), _const_spec(w_down.shape),
        ],
        out_specs=[
            row(D_MODEL), row(D_FF), row(D_FF), row(D_FF), row(D_MODEL),
            pl.BlockSpec((8, LANES), lambda i: (0, 0)),
            pl.BlockSpec((1, D_MODEL), lambda i: (0, 0)),
        ],
        out_shape=[
            jax.ShapeDtypeStruct((T, D_MODEL), BF16),
            jax.ShapeDtypeStruct((T, D_FF), BF16),
            jax.ShapeDtypeStruct((T, D_FF), BF16),
            jax.ShapeDtypeStruct((T, D_FF), BF16),
            jax.ShapeDtypeStruct((T, D_MODEL), F32),
            jax.ShapeDtypeStruct((8, LANES), F32),
            jax.ShapeDtypeStruct((1, D_MODEL), F32),
        ],
        compiler_params=_params(("arbitrary",)),
    )(x1, g2, gf, tgt, w_gate_t, w_up_t, w_down)


def _ffn_bwd(dx2, gate, up, x1, g2, w_gate_t, w_up_t, w_down):
    T = x1.shape[0]
    tm = min(T, FF_ROW_TILE)
    nc = D_FF // FF_CHUNK

    def body(dx2_ref, gate_ref, up_ref, x1_ref, g2_ref, wg_ref, wu_ref, wd_ref, dgate_ref, dup_ref, dx1_ref, dg2_ref):
        dx2v = dx2_ref[...]
        dx2b = dx2v.astype(BF16)
        dh2 = jnp.zeros((tm, D_MODEL), F32)
        for c in range(nc):
            sl = slice(c * FF_CHUNK, (c + 1) * FF_CHUNK)
            dact = _mm_nt(dx2b, wd_ref[sl, :])
            gate = gate_ref[:, sl].astype(F32)
            sg = _sigmoid(gate)
            silu = gate * sg
            dgate = (dact * up_ref[:, sl].astype(F32) * (sg * (1.0 + gate * (1.0 - sg)))).astype(BF16)
            dup = (dact * silu).astype(BF16)
            dgate_ref[:, sl] = dgate
            dup_ref[:, sl] = dup
            dh2 = dh2 + _mm(dgate, wg_ref[sl, :]) + _mm(dup, wu_ref[sl, :])
        g2v = g2_ref[...]
        _, xh, r = _rms_fwd(x1_ref[...], g2v)
        dxn, dgrow = _rms_bwd(dh2, xh, r, g2v)
        dx1_ref[...] = dx2v + dxn

        @pl.when(pl.program_id(0) == 0)
        def _():
            dg2_ref[...] = jnp.zeros_like(dg2_ref)

        dg2_ref[...] += jnp.sum(dgrow, axis=0, keepdims=True)

    row = lambda n: pl.BlockSpec((tm, n), lambda i: (i, 0))
    return pl.pallas_call(
        body,
        name="ffn_bwd",
        grid=(T // tm,),
        in_specs=[
            row(D_MODEL), row(D_FF), row(D_FF), row(D_MODEL), _const_spec((1, D_MODEL)),
            _const_spec(w_gate_t.shape), _const_spec(w_up_t.shape), _const_spec(w_down.shape),
        ],
        out_specs=[row(D_FF), row(D_FF), row(D_MODEL), pl.BlockSpec((1, D_MODEL), lambda i: (0, 0))],
        out_shape=[
            jax.ShapeDtypeStruct((T, D_FF), BF16),
            jax.ShapeDtypeStruct((T, D_FF), BF16),
            jax.ShapeDtypeStruct((T, D_MODEL), F32),
            jax.ShapeDtypeStruct((1, D_MODEL), F32),
        ],
        compiler_params=_params(("arbitrary",), VMEM_LIMIT_MAX),
    )(dx2, gate, up, x1, g2, w_gate_t, w_up_t, w_down)


def _mix_bwd(dx1, gates, pool_y, attn_y, p2, scale, w_out, w_ao, w_po, token):
    T = dx1.shape[0]
    tm = ROW_TILE

    def body(dx1_ref, gt_ref, py_ref, ay_ref, p2_ref, sc_ref, wout_ref, wao_ref, wpo_ref, token_ref, dgt_ref, dpy_ref, day_ref, da_ref, dp2_ref, dsc_ref):
        dm = _mm_nt(dx1_ref[...].astype(BF16), wout_ref[...])
        sp = _sigmoid(gt_ref[:, :D_MODEL].astype(F32))
        sa = _sigmoid(gt_ref[:, D_MODEL:].astype(F32))
        dgt_ref[:, :D_MODEL] = (dm * py_ref[...].astype(F32) * (sp * (1.0 - sp))).astype(BF16)
        dgt_ref[:, D_MODEL:] = (dm * ay_ref[...].astype(F32) * (sa * (1.0 - sa))).astype(BF16)
        dpy = (dm * sp).astype(BF16)
        day = (dm * sa).astype(BF16)
        dpy_ref[...] = dpy
        day_ref[...] = day
        da_ref[...] = _mm_nt(day, wao_ref[...]).astype(BF16)
        dp3 = _mm_nt(dpy, wpo_ref[...])
        dp2_ref[...] = (dp3 * sc_ref[...]).astype(BF16)

        @pl.when(pl.program_id(0) == 0)
        def _():
            dsc_ref[...] = jnp.zeros_like(dsc_ref)

        dsc_ref[...] += jnp.sum(dp3 * p2_ref[...], axis=0, keepdims=True)

    row = lambda n: pl.BlockSpec((tm, n), lambda i: (i, 0))
    return pl.pallas_call(
        body,
        name="mix_bwd",
        grid=(T // tm,),
        in_specs=[
            row(D_MODEL), row(2 * D_MODEL), row(D_MODEL), row(D_MODEL), row(POOL_WIDTH), _const_spec((1, POOL_WIDTH)),
            _const_spec(w_out.shape), _const_spec(w_ao.shape), _const_spec(w_po.shape), _HBM,
        ],
        out_specs=[row(2 * D_MODEL), row(D_MODEL), row(D_MODEL), row(ATTN_WIDTH), row(POOL_WIDTH), pl.BlockSpec((1, POOL_WIDTH), lambda i: (0, 0))],
        out_shape=[
            jax.ShapeDtypeStruct((T, 2 * D_MODEL), BF16),
            jax.ShapeDtypeStruct((T, D_MODEL), BF16),
            jax.ShapeDtypeStruct((T, D_MODEL), BF16),
            jax.ShapeDtypeStruct((T, ATTN_WIDTH), BF16),
            jax.ShapeDtypeStruct((T, POOL_WIDTH), BF16),
            jax.ShapeDtypeStruct((1, POOL_WIDTH), F32),
        ],
        compiler_params=_params(("arbitrary",)),
    )(dx1, gates, pool_y, attn_y, p2, scale, w_out, w_ao, w_po, token)


def _pool_bwd(dp2, pm, mix_b, token, n_seq, S):
    T = n_seq * S

    def body(dp2_ref, pm_ref, mix_ref, token_ref, du_ref, dmix_ref):
        g = pl.program_id(0)
        dp2v = dp2_ref[...]
        dpm = _mm_nt(dp2v, mix_ref[...])
        row = lax.broadcasted_iota(jnp.int32, dpm.shape, 0)
        w = _window_pick(g, 2.0, 4.0, 8.0, 16.0)
        e = dpm / jnp.minimum((row + 1).astype(F32), w)

        def ahead(a, k):
            return jnp.where(row < S - k, pltpu.roll(a, S - k, 0), 0.0)

        r2 = e + ahead(e, 1)
        r4 = r2 + ahead(r2, 2)
        r8 = r4 + ahead(r4, 4)
        r16 = r8 + ahead(r8, 8)
        du_ref[...] = (_window_pick(g, r2, r4, r8, r16) - dpm).astype(BF16)

        @pl.when(pl.program_id(1) == 0)
        def _():
            dmix_ref[...] = jnp.zeros_like(dmix_ref)

        dmix_ref[...] += _mm_tn(pm_ref[...], dp2v)

    grp = pl.BlockSpec((S, GROUP_DIM), lambda g, s: (s, g))
    mixs = pl.BlockSpec((None, GROUP_DIM, GROUP_DIM), lambda g, s: (g, 0, 0))
    return pl.pallas_call(
        body,
        name="pool_bwd",
        grid=(len(POOL_WINDOWS), n_seq),
        in_specs=[grp, grp, mixs, _HBM],
        out_specs=[grp, mixs],
        out_shape=[jax.ShapeDtypeStruct((T, POOL_WIDTH), BF16), jax.ShapeDtypeStruct((len(POOL_WINDOWS), GROUP_DIM, GROUP_DIM), F32)],
        compiler_params=_params(("parallel", "arbitrary")),
    )(dp2, pm, mix_b, token)


def _attn_bwd(qkv, da, a, fcol, lse, n_seq, S):
    T = n_seq * S
    tb = ATTN_BLOCK
    nb = S // tb
    scale = HEAD_DIM ** -0.5

    def body(q_ref, k_ref, v_ref, do_ref, o_ref, fc_ref, st_ref, dq_ref, dk_ref, dv_ref, dfk_ref, dfq_ref,
             qa_sc, doa_sc, dq_acc, ka_sc, va_sc, dk_sc, dv_sc):
        j = pl.program_id(1)
        lane = lax.broadcasted_iota(jnp.int32, (1, LANES), 1)
        low = lane < HEAD_DIM
        ones = (1.0, 1.0, 1.0)
        zeros = (0.0, 0.0, 0.0)

        @pl.when(j == 0)
        def _():
            dq_acc[...] = jnp.zeros_like(dq_acc)

            def rows_q(i, carry):
                r0 = pl.multiple_of(i * tb, tb)
                for h in range(N_HEADS):
                    pair = slice((h // 2) * LANES, (h // 2 + 1) * LANES)
                    qp = q_ref[pl.ds(r0, tb), pair]
                    dop = do_ref[pl.ds(r0, tb), pair]
                    prod = dop.astype(F32) * o_ref[pl.ds(r0, tb), pair].astype(F32)
                    head = (lane >= HEAD_DIM * (h % 2)) & (lane < HEAD_DIM * (h % 2 + 1))
                    delta = jnp.sum(jnp.where(head, prod, 0.0), axis=1, keepdims=True)
                    cq = fc_ref[pl.ds(r0, tb), h : h + 1] - st_ref[pl.ds(r0, tb), h : h + 1]
                    qa_sc[h, pl.ds(r0, tb), :] = _augment(qp, h % 2, _split3(cq), ones)
                    doa_sc[h, pl.ds(r0, tb), :] = _augment(dop, h % 2, _split3(-delta), zeros)
                return carry

            lax.fori_loop(0, nb, rows_q, 0)

        c0 = pl.multiple_of(j * tb, tb)
        for h in range(N_HEADS):
            pair = slice((h // 2) * LANES, (h // 2 + 1) * LANES)
            kp = k_ref[:, pair] * scale
            ka_sc[h] = _augment(kp, h % 2, ones, _split3(-fc_ref[pl.ds(c0, tb), h : h + 1]))
            va_sc[h] = _augment(v_ref[:, pair], h % 2, ones, zeros)
        dk_sc[...] = jnp.zeros_like(dk_sc)
        dv_sc[...] = jnp.zeros_like(dv_sc)
        causal = lax.broadcasted_iota(jnp.int32, (tb, tb), 1) <= lax.broadcasted_iota(jnp.int32, (tb, tb), 0)

        def step(i, masked):
            r0 = pl.multiple_of(i * tb, tb)
            for h in range(N_HEADS):
                dob = do_ref[pl.ds(r0, tb), (h // 2) * LANES : (h // 2 + 1) * LANES]
                qa = qa_sc[h, pl.ds(r0, tb), :]
                s = _mm_nt(qa, ka_sc[h])
                if masked:
                    s = jnp.where(causal, s, -jnp.inf)
                pr = jnp.exp(s)
                dv_sc[h] += _mm_tn(pr.astype(BF16), dob)
                dsb = (pr * _mm_nt(doa_sc[h, pl.ds(r0, tb), :], va_sc[h])).astype(BF16)
                dk_sc[h] += _mm_tn(dsb, qa)
                dq_acc[h, pl.ds(r0, tb), :] += _mm(dsb, ka_sc[h])

        step(j, True)

        def loop_body(i, carry):
            step(i, False)
            return carry

        lax.fori_loop(j + 1, nb, loop_body, 0)
        dfk = jnp.zeros((tb, LANES), F32)
        for p in range(N_PAIRS):
            dk_ref[:, p * LANES : (p + 1) * LANES] = (jnp.where(low, dk_sc[2 * p], dk_sc[2 * p + 1]) * scale).astype(BF16)
            dv_ref[:, p * LANES : (p + 1) * LANES] = jnp.where(low, dv_sc[2 * p], dv_sc[2 * p + 1]).astype(BF16)
            for hh in range(2):
                b = HEAD_DIM * (1 - hh) + 3
                dfk = jnp.where(lane == 2 * p + hh, -dk_sc[2 * p + hh][:, b : b + 1], dfk)
        dfk_ref[...] = dfk

        @pl.when(j == nb - 1)
        def _():
            def rows_dq(i, carry):
                r0 = pl.multiple_of(i * tb, tb)
                dfq = jnp.zeros((tb, LANES), F32)
                for p in range(N_PAIRS):
                    parts = [dq_acc[2 * p + hh, pl.ds(r0, tb), :] for hh in range(2)]
                    dq_ref[pl.ds(r0, tb), p * LANES : (p + 1) * LANES] = jnp.where(low, parts[0], parts[1]).astype(BF16)
                    for hh in range(2):
                        b = HEAD_DIM * (1 - hh)
                        dfq = jnp.where(lane == 2 * p + hh, parts[hh][:, b : b + 1], dfq)
                dfq_ref[pl.ds(r0, tb), :] = dfq
                return carry

            lax.fori_loop(0, nb, rows_dq, 0)

    seq = lambda w, col: pl.BlockSpec((S, w), lambda s, j: (s, col))
    blk = lambda w, col: pl.BlockSpec((tb, w), lambda s, j: (s * nb + j, col))
    return pl.pallas_call(
        body,
        name="attn_bwd",
        grid=(n_seq, nb),
        in_specs=[seq(ATTN_WIDTH, 0), blk(ATTN_WIDTH, 1), blk(ATTN_WIDTH, 2), seq(ATTN_WIDTH, 0), seq(ATTN_WIDTH, 0), seq(LANES, 0), seq(LANES, 0)],
        out_specs=[seq(ATTN_WIDTH, 0), blk(ATTN_WIDTH, 0), blk(ATTN_WIDTH, 0), blk(LANES, 0), seq(LANES, 0)],
        out_shape=[
            jax.ShapeDtypeStruct((T, ATTN_WIDTH), BF16),
            jax.ShapeDtypeStruct((T, ATTN_WIDTH), BF16),
            jax.ShapeDtypeStruct((T, ATTN_WIDTH), BF16),
            jax.ShapeDtypeStruct((T, LANES), F32),
            jax.ShapeDtypeStruct((T, LANES), F32),
        ],
        scratch_shapes=[
            pltpu.VMEM((N_HEADS, S, LANES), BF16),
            pltpu.VMEM((N_HEADS, S, LANES), BF16),
            pltpu.VMEM((N_HEADS, S, LANES), F32),
            pltpu.VMEM((N_HEADS, tb, LANES), BF16),
            pltpu.VMEM((N_HEADS, tb, LANES), BF16),
            pltpu.VMEM((N_HEADS, tb, LANES), F32),
            pltpu.VMEM((N_HEADS, tb, LANES), F32),
        ],
        compiler_params=_params(("parallel", "arbitrary")),
    )(qkv, qkv, qkv, da, a, fcol, lse)


def _forget_bwd(dfk, dfq, fl, b_pad, n_seq, S):
    def body(df_ref, dfq_ref, fl_ref, b_ref, dfl_ref, db_ref):
        t = (df_ref[...] + dfq_ref[...]).T
        lane = lax.broadcasted_iota(jnp.int32, t.shape, 1)
        k = 1
        while k < S:
            t = t + jnp.where(lane < S - k, pltpu.roll(t, S - k, 1), 0.0)
            k *= 2
        dfl = t.T * _sigmoid(-(fl_ref[...] + b_ref[...]))
        dfl_ref[...] = dfl.astype(BF16)

        @pl.when(pl.program_id(0) == 0)
        def _():
            db_ref[...] = jnp.zeros_like(db_ref)

        db_ref[...] += jnp.sum(dfl, axis=0, keepdims=True)

    return pl.pallas_call(
        body,
        name="forget_bwd",
        grid=(n_seq,),
        in_specs=[
            pl.BlockSpec((S, LANES), lambda s: (s, 0)),
            pl.BlockSpec((S, LANES), lambda s: (s, 0)),
            pl.BlockSpec((S, FL_PAD), lambda s: (s, 0)),
            _const_spec((1, FL_PAD)),
        ],
        out_specs=[pl.BlockSpec((S, FL_PAD), lambda s: (s, 0)), pl.BlockSpec((1, FL_PAD), lambda s: (0, 0))],
        out_shape=[jax.ShapeDtypeStruct((n_seq * S, FL_PAD), BF16), jax.ShapeDtypeStruct((1, FL_PAD), F32)],
        compiler_params=_params(("arbitrary",)),
    )(dfk, dfq, fl, b_pad)


def _in_proj_bwd(du, dq, dk, dv, dfl, dgates, x, dx1, g1, w_uqkv, w_fl, w_g):
    T = x.shape[0]
    tm = ROW_TILE

    def body(du_ref, dq_ref, dk_ref, dv_ref, dfl_ref, dgt_ref, x_ref, dx1_ref, g_ref, wa_ref, wf_ref, wg_ref, dx_ref, dg_ref):
        dh = _mm_nt(dgt_ref[...], wg_ref[...]) + _mm_nt(dfl_ref[...], wf_ref[...])
        for n, ref in enumerate((du_ref, dq_ref, dk_ref, dv_ref)):
            dh = dh + _mm_nt(ref[...], wa_ref[:, n * 512 : (n + 1) * 512])
        gv = g_ref[...]
        _, xh, r = _rms_fwd(x_ref[...], gv)
        dxn, dgrow = _rms_bwd(dh, xh, r, gv)
        dx_ref[...] = dx1_ref[...] + dxn

        @pl.when(pl.program_id(0) == 0)
        def _():
            dg_ref[...] = jnp.zeros_like(dg_ref)

        dg_ref[...] += jnp.sum(dgrow, axis=0, keepdims=True)

    row = lambda n: pl.BlockSpec((tm, n), lambda i: (i, 0))
    return pl.pallas_call(
        body,
        name="in_proj_bwd",
        grid=(T // tm,),
        in_specs=[
            row(512), row(512), row(512), row(512), row(FL_PAD), row(2 * D_MODEL), row(D_MODEL), row(D_MODEL), _const_spec((1, D_MODEL)),
            _const_spec(w_uqkv.shape), _const_spec(w_fl.shape), _const_spec(w_g.shape),
        ],
        out_specs=[row(D_MODEL), pl.BlockSpec((1, D_MODEL), lambda i: (0, 0))],
        out_shape=[jax.ShapeDtypeStruct((T, D_MODEL), F32), jax.ShapeDtypeStruct((1, D_MODEL), F32)],
        compiler_params=_params(("arbitrary",)),
    )(du, dq, dk, dv, dfl, dgates, x, dx1, g1, w_uqkv, w_fl, w_g)


def _pick_block(n):
    for b in (512, 1408, 256, 128):
        if n % b == 0:
            return b
    raise ValueError(n)


def _matmul_tn(a, b, name, token=None):
    T, K = a.shape
    N = b.shape[1]
    bt, bk, bn = min(T, DW_TOKENS), _pick_block(K), _pick_block(N)
    nt = T // bt

    def body(a_ref, b_ref, *rest):
        o_ref, acc = rest[-2:]

        @pl.when(pl.program_id(2) == 0)
        def _():
            acc[...] = jnp.zeros_like(acc)

        acc[...] += _mm_tn(a_ref[...].astype(BF16), b_ref[...].astype(BF16))

        @pl.when(pl.program_id(2) == nt - 1)
        def _():
            o_ref[...] = acc[...].astype(BF16)

    ordering = [] if token is None else [token]
    return pl.pallas_call(
        body,
        name=name,
        grid=(K // bk, N // bn, nt),
        in_specs=[pl.BlockSpec((bt, bk), lambda k, n, t: (t, k)), pl.BlockSpec((bt, bn), lambda k, n, t: (t, n))] + [_HBM] * len(ordering),
        out_specs=pl.BlockSpec((bk, bn), lambda k, n, t: (k, n)),
        out_shape=jax.ShapeDtypeStruct((K, N), BF16),
        scratch_shapes=[pltpu.VMEM((bk, bn), F32)],
        compiler_params=_params(("parallel", "parallel", "arbitrary")),
    )(a, b, *ordering)


def _position():
    return lax.axis_index("x"), lax.axis_index("y"), lax.axis_index("c")


_HBM = pl.BlockSpec(memory_space=pl.ANY)


def _all_gather(blocks, name):
    n = len(blocks)

    def body(*refs):
        xs, outs = refs[:n], refs[n : 2 * n]
        send_sems, recv_sems, local_sems = refs[2 * n :]
        x, y, c = _position()
        me, sibling = (x, y, c), (x, y, 1 - c)
        chips = [(1 - x, y), (x, 1 - y), (1 - x, 1 - y)]

        def rows(a, px, py, pc):
            return outs[a].at[4 * px + 2 * py + pc]

        def copy(a, k, blk, to, src=None):
            return pltpu.make_async_remote_copy(
                src_ref=rows(a, *blk) if src is None else src, dst_ref=rows(a, *blk),
                send_sem=send_sems.at[7 * a + k], recv_sem=recv_sems.at[7 * a + k], device_id=to, device_id_type=MESH,
            )

        mine = [pltpu.make_async_copy(xs[a], rows(a, *me), local_sems.at[a]) for a in range(n)]
        for cp in mine:
            cp.start()
        first = []
        for a in range(n):
            first.append(copy(a, 0, me, sibling, src=xs[a]))
            first += [copy(a, 1 + j, me, (*chip, c), src=xs[a]) for j, chip in enumerate(chips)]
        for cp in first:
            cp.start()
        passed = []
        for j, chip in enumerate(chips):
            for a in range(n):
                copy(a, 1 + j, (*chip, c), me).wait_recv()
                passed.append(copy(a, 4 + j, (*chip, c), sibling))
                passed[-1].start()
        for a in range(n):
            copy(a, 0, sibling, me).wait_recv()
        for j, chip in enumerate(chips):
            for a in range(n):
                copy(a, 4 + j, (*chip, 1 - c), me).wait_recv()
        for cp in first + passed:
            cp.wait_send()
        for cp in mine:
            cp.wait()

    return pl.pallas_call(
        body,
        name=name,
        out_shape=[jax.ShapeDtypeStruct((N_DEV, *b.shape), b.dtype) for b in blocks],
        in_specs=[_HBM] * n,
        out_specs=[_HBM] * n,
        scratch_shapes=[pltpu.SemaphoreType.DMA((7 * n,)), pltpu.SemaphoreType.DMA((7 * n,)), pltpu.SemaphoreType.DMA((n,))],
    )(*blocks)


_SEM = pl.BlockSpec(memory_space=pltpu.SEMAPHORE)
_HBM_ONLY = pl.BlockSpec(memory_space=pltpu.HBM)
_SIDE_EFFECT = pltpu.SideEffectType.DATAFLOW_SIDE_EFFECTING


def _peer(x, y, c, k):
    return (1 - x if k & 4 else x, 1 - y if k & 2 else y, 1 - c if k & 1 else c)


_PEER_BITS = {"gather": range(1, N_DEV), "scatter": range(1, N_DEV), "chips": (4, 2, 6)}
_LAND_SLOTS = {"gather": N_DEV, "scatter": N_DEV, "chips": 3}


def _exchange_copies(src_refs, land_refs, send_sems, recv_sems, pattern, receive_side):
    x, y, c = _position()
    me = 4 * x + 2 * y + c
    bits = _PEER_BITS[pattern]
    cps = []
    for j, k in enumerate(bits):
        px, py, pc = _peer(x, y, c, k)
        peer = 4 * px + 2 * py + pc
        for a, (src, land) in enumerate(zip(src_refs, land_refs)):
            if pattern == "chips":
                s, slot = src.at[2 * px + py], j
            else:
                s, slot = (src if pattern == "gather" else src.at[peer]), (peer if receive_side else me)
            cps.append(pltpu.make_async_remote_copy(
                src_ref=s, dst_ref=land.at[slot],
                send_sem=send_sems.at[len(bits) * a + j], recv_sem=recv_sems.at[len(bits) * a + j],
                device_id=(px, py, pc), device_id_type=MESH,
            ))
    return cps


def _exchange_start(srcs, after, name, pattern):
    n = len(srcs)
    m = len(_PEER_BITS[pattern])
    lands = [jax.ShapeDtypeStruct((_LAND_SLOTS[pattern], *s.shape[-2:]), s.dtype) for s in srcs]

    def body(*refs):
        src_refs, land_refs = refs[1 : 1 + n], refs[1 + n : 1 + 2 * n]
        send_sems, recv_sems = refs[1 + 2 * n], refs[2 + 2 * n]
        token = refs[-1]
        for cp in _exchange_copies(src_refs, land_refs, send_sems, recv_sems, pattern, receive_side=False):
            cp.start()
        token[...] = jnp.zeros_like(token)

    hbm = lambda t: pltpu.with_memory_space_constraint(t, pltpu.HBM)
    out = pl.pallas_call(
        body,
        name=name,
        out_shape=(
            pltpu.SemaphoreType.DMA((m * n,)), pltpu.SemaphoreType.DMA((m * n,)),
            *[pltpu.HBM(s.shape, s.dtype) for s in srcs], *[pltpu.HBM(l.shape, l.dtype) for l in lands],
            jax.ShapeDtypeStruct((8, LANES), F32),
        ),
        in_specs=(_HBM, *[_HBM_ONLY] * (2 * n)),
        out_specs=(_SEM, _SEM, *[_HBM_ONLY] * (2 * n), pl.BlockSpec(memory_space=pltpu.VMEM)),
        input_output_aliases={1 + i: 2 + i for i in range(2 * n)},
        compiler_params=pltpu.CompilerParams(has_side_effects=_SIDE_EFFECT),
    )(after, *[hbm(s) for s in srcs], *[hbm(lax.empty(l.shape, l.dtype)) for l in lands])
    return out[0], out[1], out[2 : 2 + n], out[2 + n : 2 + 2 * n], out[-1]


def _exchange_wait(send_sems, recv_sems, srcs, lands, after, name, pattern):
    n = len(srcs)

    def body(*refs):
        src_refs, land_refs = refs[:n], refs[n : 2 * n]
        for cp in _exchange_copies(src_refs, land_refs, refs[2 * n], refs[2 * n + 1], pattern, receive_side=True):
            cp.wait_send()
            cp.wait_recv()

    out = pl.pallas_call(
        body,
        name=name,
        out_shape=(*[pltpu.HBM(s.shape, s.dtype) for s in srcs], *[pltpu.HBM(l.shape, l.dtype) for l in lands]),
        in_specs=(*[_HBM_ONLY] * (2 * n), _SEM, _SEM, _HBM),
        out_specs=tuple([_HBM_ONLY] * (2 * n)),
        input_output_aliases={i: i for i in range(2 * n)},
        compiler_params=pltpu.CompilerParams(has_side_effects=_SIDE_EFFECT),
    )(*srcs, *lands, send_sems, recv_sems, after)
    return out[:n], out[n:]


def _sibling_exchange(sends):
    n = len(sends)

    def body(*refs):
        srcs, dsts = refs[:n], refs[n : 2 * n]
        send_sems, recv_sems = refs[2 * n :]
        x, y, c = _position()
        cps = [
            pltpu.make_async_remote_copy(
                src_ref=srcs[a].at[1 - c], dst_ref=dsts[a], send_sem=send_sems.at[a], recv_sem=recv_sems.at[a],
                device_id=(x, y, 1 - c), device_id_type=MESH,
            )
            for a in range(n)
        ]
        for cp in cps:
            cp.start()
        for cp in cps:
            cp.wait()

    return pl.pallas_call(
        body,
        name="rs_sibling",
        out_shape=[jax.ShapeDtypeStruct(s.shape[1:], s.dtype) for s in sends],
        in_specs=[_HBM] * n,
        out_specs=[_HBM] * n,
        scratch_shapes=[pltpu.SemaphoreType.DMA((n,)), pltpu.SemaphoreType.DMA((n,))],
    )(*sends)


def _rows_tile(r):
    return ROW_TILE if r % ROW_TILE == 0 else r


def _pair_sum(send, got, core, name):
    _, _, r, c = send.shape
    br = _rows_tile(r)

    def body(core_ref, a_ref, b_ref, o_ref):
        o_ref[...] = (a_ref[...].astype(F32) + b_ref[...].astype(F32)).astype(o_ref.dtype)

    return pl.pallas_call(
        body,
        name=name,
        grid_spec=pltpu.PrefetchScalarGridSpec(
            num_scalar_prefetch=1,
            grid=(4, r // br),
            in_specs=[
                pl.BlockSpec((None, None, br, c), lambda n, i, core: (core[0], n, i, 0)),
                pl.BlockSpec((None, br, c), lambda n, i, core: (n, i, 0)),
            ],
            out_specs=pl.BlockSpec((None, br, c), lambda n, i, core: (n, i, 0)),
        ),
        out_shape=jax.ShapeDtypeStruct((4, r, c), send.dtype),
        compiler_params=_params(("parallel", "parallel")),
    )(core, send, got)


def _adamw(w, g, m, v):
    m = ADAM_B1 * m + (1.0 - ADAM_B1) * g
    v = ADAM_B2 * v + (1.0 - ADAM_B2) * (g * g)
    m_hat = m / (1.0 - ADAM_B1 ** ADAM_STEP)
    v_hat = v / (1.0 - ADAM_B2 ** ADAM_STEP)
    delta = -ADAM_LR * (m_hat / (jnp.sqrt(v_hat) + ADAM_EPS) + ADAM_WD * w)
    return delta, m, v


def _shard_update(send, got, recv, w, m, v, pos, name):
    _, r, c = w.shape
    br = _rows_tile(r)

    def body(pos_ref, a_ref, b_ref, r_ref, w_ref, m_ref, v_ref, g_ref, d_ref, nm_ref, nv_ref):
        g = a_ref[...].astype(F32) + b_ref[...].astype(F32)
        for n in range(3):
            g = g + r_ref[n].astype(F32)
        g_ref[...] = g
        d_ref[...], nm_ref[...], nv_ref[...] = _adamw(w_ref[...], g, m_ref[...], v_ref[...])

    own = pl.BlockSpec((None, br, c), lambda i, pos: (0, i, 0))
    return pl.pallas_call(
        body,
        name=name,
        grid_spec=pltpu.PrefetchScalarGridSpec(
            num_scalar_prefetch=1,
            grid=(r // br,),
            in_specs=[
                pl.BlockSpec((None, None, br, c), lambda i, pos: (pos[0], pos[1], i, 0)),
                pl.BlockSpec((None, br, c), lambda i, pos: (pos[1], i, 0)),
                pl.BlockSpec((3, br, c), lambda i, pos: (0, i, 0)),
                own, own, own,
            ],
            out_specs=[own, own, own, own],
        ),
        out_shape=[jax.ShapeDtypeStruct((1, r, c), F32)] * 4,
        compiler_params=_params(("parallel",)),
    )(pos, send, got, recv, w, m, v)


def _shard_update_direct(parts, chunks, w, m, v, me, name):
    _, r, c = w.shape
    br = _rows_tile(r)

    def body(me_ref, p_ref, own_ref, w_ref, m_ref, v_ref, g_ref, d_ref, nm_ref, nv_ref):
        g = None
        for n in range(N_DEV):
            part = jnp.where(me_ref[0] == n, own_ref[...], p_ref[n]).astype(F32)
            g = part if g is None else g + part
        g_ref[...] = g
        d_ref[...], nm_ref[...], nv_ref[...] = _adamw(w_ref[...], g, m_ref[...], v_ref[...])

    shard = pl.BlockSpec((None, br, c), lambda i, me: (0, i, 0))
    return pl.pallas_call(
        body,
        name=name,
        grid_spec=pltpu.PrefetchScalarGridSpec(
            num_scalar_prefetch=1,
            grid=(r // br,),
            in_specs=[
                pl.BlockSpec((N_DEV, br, c), lambda i, me: (0, i, 0)),
                pl.BlockSpec((None, br, c), lambda i, me: (me[0], i, 0)),
                shard, shard, shard,
            ],
            out_specs=[shard, shard, shard, shard],
        ),
        out_shape=[jax.ShapeDtypeStruct((1, r, c), F32)] * 4,
        compiler_params=_params(("parallel",)),
    )(me, parts, chunks, w, m, v)


def _small_update(parts, w, m, v):
    R = w.shape[0]

    def body(p_ref, w_ref, m_ref, v_ref, g_ref, d_ref, nm_ref, nv_ref):
        g = p_ref[0]
        for n in range(1, N_DEV):
            g = g + p_ref[n]
        g_ref[...] = g
        d_ref[...], nm_ref[...], nv_ref[...] = _adamw(w_ref[...], g, m_ref[...], v_ref[...])

    return pl.pallas_call(
        body,
        name="small_update",
        out_shape=[jax.ShapeDtypeStruct((R, LANES), F32)] * 4,
        compiler_params=pltpu.CompilerParams(vmem_limit_bytes=VMEM_LIMIT),
    )(parts, w, m, v)


_SHARD_AXIS = (1, 1, 1, 0, 0, 0, 0)
_TRANSPOSED = (False, False, False, False, True, True, False)


def _full_from_gathered(t, axis):
    if axis == 0:
        return t.reshape(N_DEV * t.shape[1], t.shape[2])
    return jnp.concatenate([t[d] for d in range(N_DEV)], axis=1)


def _chunks_from_cols(t):
    c = t.shape[1] // N_DEV
    return jnp.stack([t[:, d * c : (d + 1) * c] for d in range(N_DEV)])


def _send_from_cols(t):
    c = t.shape[1] // N_DEV
    return jnp.stack([jnp.stack([t[:, (2 * chip + core) * c : (2 * chip + core + 1) * c] for chip in range(4)]) for core in range(2)])


_SMALL = (("norm1_g", 8), ("norm2_g", 8), ("norm_f_g", 8), ("b_forget", 8), ("pool_scale", 8), ("pool_mix", 512))
_SMALL_ROWS = sum(r for _, r in _SMALL) + 8


def _pack_small(vals, loss_row):
    parts = []
    for (name, rows), t in zip(_SMALL, vals):
        f = t.astype(F32).reshape(-1)
        f = jnp.concatenate([f, jnp.zeros((rows * LANES - f.shape[0],), F32)]).reshape(rows, LANES)
        parts.append(f)
    parts.append(loss_row)
    return jnp.concatenate(parts, axis=0)


def _unpack_small(packed, shapes):
    out, off = [], 0
    for (name, rows), shape in zip(_SMALL, shapes):
        n = 1
        for s in shape:
            n *= s
        out.append(packed[off : off + rows].reshape(-1)[:n].reshape(shape))
        off += rows
    return out, packed[off, 0]


def _local_grads(x, tgt, g1, g2, gf, b_forget, pool_mix, pool_scale, w_in, fwd_token, out_weights, ffn_weights, ffn_grads_out, out_grads_out, small_grads_out):
    n_seq, S, _ = x.shape
    T = n_seq * S
    x2 = x.reshape(T, D_MODEL)
    tg2 = tgt.reshape(T, D_MODEL)
    w_uqkv = w_in[:, : POOL_WIDTH + 3 * ATTN_WIDTH]
    w_fl = jnp.concatenate([w_in[:, 2048 : 2048 + N_HEADS], jnp.zeros((D_MODEL, FL_PAD - N_HEADS), BF16)], axis=1)
    w_g = w_in[:, 2048 + N_HEADS :]
    b_pad = jnp.concatenate([b_forget.reshape(1, N_HEADS), jnp.zeros((1, FL_PAD - N_HEADS), F32)], axis=1)
    mix_b = pool_mix.reshape(len(POOL_WINDOWS), GROUP_DIM, GROUP_DIM).astype(BF16)
    scale = pool_scale.reshape(1, POOL_WIDTH)
    g1 = g1.reshape(1, D_MODEL)
    g2 = g2.reshape(1, D_MODEL)
    gf = gf.reshape(1, D_MODEL)

    h, u, qkv, fl, gates = _in_proj(x2, g1, w_uqkv, w_fl, w_g, fwd_token)
    fcol = _forget_fwd(fl, b_pad, n_seq, S)
    pm, p2, p3 = _pool_fwd(u, mix_b, scale, n_seq, S)
    a, lse = _attn_fwd(qkv, fcol, n_seq, S)
    w_po, w_ao, w_out = out_weights(a)
    merged, x1, attn_y, pool_y = _mix_out(a, p3, gates, x2, w_ao, w_po, w_out)
    w_gate_t, w_up_t, w_down = ffn_weights(x1)
    h2, gate, up, act, dx2, loss_rows, dgf = _ffn_fwd(x1, g2, gf, tg2, w_gate_t, w_up_t, w_down)

    dgate, dup, dx1, dg2 = _ffn_bwd(dx2, gate, up, x1, g2, w_gate_t, w_up_t, w_down)
    bwd_token = ffn_grads_out(_matmul_tn(dgate, h2, "dw_ffn_gate"), _matmul_tn(dup, h2, "dw_ffn_up"), _matmul_tn(act, dx2, "dw_ffn_down"))
    dgates, dpy, day, da, dp2, dscale = _mix_bwd(dx1, gates, pool_y, attn_y, p2, scale, w_out, w_ao, w_po, bwd_token)
    out_token = out_grads_out(_matmul_tn(p3, dpy, "dw_pool_out"), _matmul_tn(a, day, "dw_attn_out"), _matmul_tn(merged, dx1, "dw_out"))
    du, dmix = _pool_bwd(dp2, pm, mix_b, out_token, n_seq, S)
    dq, dk, dv, dfk, dfq = _attn_bwd(qkv, da, a, fcol, lse, n_seq, S)
    dfl, db = _forget_bwd(dfk, dfq, fl, b_pad, n_seq, S)
    dx, dg1 = _in_proj_bwd(du, dq, dk, dv, dfl, dgates, x2, dx1, g1, w_uqkv, w_fl, w_g)
    small_token = small_grads_out((dg1, dg2, dgf, db[:, :N_HEADS], dscale, dmix), loss_rows)

    d_w_in = jnp.concatenate(
        [
            _matmul_tn(h, du, "dw_u", small_token), _matmul_tn(h, dq, "dw_q"), _matmul_tn(h, dk, "dw_k"), _matmul_tn(h, dv, "dw_v"),
            _matmul_tn(h, dfl, "dw_fl")[:, :N_HEADS], _matmul_tn(h, dgates, "dw_gates"),
        ],
        axis=1,
    )
    return dx.reshape(n_seq, S, D_MODEL), _send_from_cols(d_w_in)


def kernel(x, norm1_g, w_in, b_forget, pool_mix, pool_scale, w_pool_out, w_attn_out, w_out, norm2_g, w_ffn_gate, w_ffn_up, w_ffn_down, norm_f_g, loss_target, m_norm1_g, m_w_in, m_b_forget, m_pool_mix, m_pool_scale, m_w_pool_out, m_w_attn_out, m_w_out, m_norm2_g, m_w_ffn_gate, m_w_ffn_up, m_w_ffn_down, m_norm_f_g, v_norm1_g, v_w_in, v_b_forget, v_pool_mix, v_pool_scale, v_w_pool_out, v_w_attn_out, v_w_out, v_norm2_g, v_w_ffn_gate, v_w_ffn_up, v_w_ffn_down, v_norm_f_g):
    names = ("w_in", "w_pool_out", "w_attn_out", "w_out", "w_ffn_gate", "w_ffn_up", "w_ffn_down")
    w_sh = (w_in, w_pool_out, w_attn_out, w_out, w_ffn_gate, w_ffn_up, w_ffn_down)
    m_sh = (m_w_in, m_w_pool_out, m_w_attn_out, m_w_out, m_w_ffn_gate, m_w_ffn_up, m_w_ffn_down)
    v_sh = (v_w_in, v_w_pool_out, v_w_attn_out, v_w_out, v_w_ffn_gate, v_w_ffn_up, v_w_ffn_down)

    cx, cy, cc = _position()
    me = 4 * cx + 2 * cy + cc
    def stored(t, transposed):
        return jnp.transpose(t, (0, 2, 1)) if transposed else t

    w_sh, m_sh, v_sh = ([stored(t, tr) for t, tr in zip(ts, _TRANSPOSED)] for ts in (w_sh, m_sh, v_sh))
    shards = [w[0].astype(BF16) for w in w_sh]
    (gathered_in,) = _all_gather(shards[:1], "w_in_all_gather")
    out_sems = _exchange_start(shards[1:4], gathered_in, "out_weights_gather_start", "gather")
    ffn_sems = _exchange_start(shards[4:], out_sems[4], "ffn_weights_gather_start", "gather")
    no_order = jnp.zeros((8, LANES), F32)

    def with_own(lands, own):
        return [lax.dynamic_update_slice(l, o[None], (me, 0, 0)) for l, o in zip(lands, own)]

    def gathered_weights(sems, axes, name):
        def wait(after):
            send_sems, recv_sems, srcs, lands, _ = sems
            srcs, lands = _exchange_wait(send_sems, recv_sems, srcs, lands, after, name, "gather")
            return [_full_from_gathered(t, axis) for t, axis in zip(with_own(lands, srcs), axes)]

        return wait

    started = {}

    def scatter_grads(key, name):
        def start(*whole_grads):
            chunks = [
                _chunks_from_cols(t) if axis == 1 else t.reshape(N_DEV, -1, t.shape[1])
                for t, axis in zip(whole_grads, _SHARD_AXIS[key])
            ]
            started[key] = _exchange_start(chunks, no_order, name, "scatter")
            return started[key][4]

        return start

    def gather_small(small, loss_rows):
        started["small"] = _exchange_start([_pack_small(small, loss_rows)], no_order, "small_grads_gather_start", "gather")
        return started["small"][4]

    ffn, out = slice(4, 7), slice(1, 4)
    grad_x, send_in = _local_grads(
        x, loss_target, norm1_g, norm2_g, norm_f_g, b_forget, pool_mix, pool_scale, _full_from_gathered(gathered_in, 1), ffn_sems[4],
        gathered_weights(out_sems, _SHARD_AXIS[out], "out_weights_gather_wait"),
        gathered_weights(ffn_sems, _SHARD_AXIS[ffn], "ffn_weights_gather_wait"),
        scatter_grads(ffn, "ffn_grads_scatter_start"), scatter_grads(out, "out_grads_scatter_start"), gather_small,
    )

    core = jnp.reshape(cc, (1,)).astype(jnp.int32)
    pos = jnp.stack([cc, 2 * cx + cy]).astype(jnp.int32)
    (got_in,) = _sibling_exchange([send_in])
    pair_in = _pair_sum(send_in, got_in, core, "pair_sum_w_in")
    chip_sems = _exchange_start([pair_in], no_order, "w_in_grads_chips_start", "chips")

    def scattered_updates(key, after, name):
        send_sems, recv_sems, srcs, lands, _ = started[key]
        srcs, lands = _exchange_wait(send_sems, recv_sems, srcs, lands, after, name, "scatter")
        return [
            _shard_update_direct(p, s, w, m, v, jnp.reshape(me, (1,)).astype(jnp.int32), "update_" + n)
            for p, s, w, m, v, n in zip(lands, srcs, w_sh[key], m_sh[key], v_sh[key], names[key])
        ]

    updates_out = scattered_updates(out, chip_sems[4], "out_grads_scatter_wait")
    updates_ffn = scattered_updates(ffn, chip_sems[4], "ffn_grads_scatter_wait")

    small_w = (norm1_g, norm2_g, norm_f_g, b_forget, pool_scale, pool_mix)
    small_m = (m_norm1_g, m_norm2_g, m_norm_f_g, m_b_forget, m_pool_scale, m_pool_mix)
    small_v = (v_norm1_g, v_norm2_g, v_norm_f_g, v_b_forget, v_pool_scale, v_pool_mix)
    zero_row = jnp.zeros((8, LANES), F32)
    send_sems, recv_sems, srcs, lands, _ = started["small"]
    srcs, lands = _exchange_wait(send_sems, recv_sems, srcs, lands, updates_ffn[-1][0], "small_grads_gather_wait", "gather")
    (parts,) = with_own(lands, srcs)
    g_s, d_s, nm_s, nv_s = _small_update(parts, _pack_small(small_w, zero_row), _pack_small(small_m, zero_row), _pack_small(small_v, zero_row))

    send_sems, recv_sems, srcs, lands, _ = chip_sems
    _, (recv_in,) = _exchange_wait(send_sems, recv_sems, srcs, lands, g_s, "w_in_grads_chips_wait", "chips")
    update_in = _shard_update(send_in, got_in, recv_in, w_in, m_w_in, v_w_in, pos, "update_w_in")
    g_w, d_w, nm_w, nv_w = zip(*(
        [stored(t, tr) for t in u] for u, tr in zip([update_in] + updates_out + updates_ffn, _TRANSPOSED)
    ))
    shapes = [t.shape for t in small_w]
    (g1, g2, gf, gb, gsc, gmix), loss = _unpack_small(g_s, shapes)
    (d1, d2, df, db_, dsc, dmx), _ = _unpack_small(d_s, shapes)
    (m1, m2, mf, mb, msc, mmx), _ = _unpack_small(nm_s, shapes)
    (v1, v2, vf, vb, vsc, vmx), _ = _unpack_small(nv_s, shapes)

    def ordered(n1, win, b, mix, sc, wpo, wao, wout, n2, wg, wu, wd, nf):
        return (n1, win, b, mix, sc, wpo, wao, wout, n2, wg, wu, wd, nf)

    grads = ordered(g1, g_w[0], gb, gmix, gsc, g_w[1], g_w[2], g_w[3], g2, g_w[4], g_w[5], g_w[6], gf)
    deltas = ordered(d1, d_w[0], db_, dmx, dsc, d_w[1], d_w[2], d_w[3], d2, d_w[4], d_w[5], d_w[6], df)
    new_m = ordered(m1, nm_w[0], mb, mmx, msc, nm_w[1], nm_w[2], nm_w[3], m2, nm_w[4], nm_w[5], nm_w[6], mf)
    new_v = ordered(v1, nv_w[0], vb, vmx, vsc, nv_w[1], nv_w[2], nv_w[3], v2, nv_w[4], nv_w[5], nv_w[6], vf)
    return (loss, grad_x, *grads, *deltas, *new_m, *new_v)
```

```python
import functools

import jax
import jax.numpy as jnp
from jax import lax
from jax.experimental import pallas as pl
from jax.experimental.pallas import tpu as pltpu

F32 = jnp.float32
BF16 = jnp.bfloat16
MESH = pl.DeviceIdType.MESH

D_MODEL = 1024
POOL_WINDOWS = (2, 4, 8, 16)
POOL_WIDTH = 512
GROUP_DIM = 128
ATTN_WIDTH = 512
HEAD_DIM = 64
N_HEADS = 8
N_PAIRS = 4
D_FF = 2816
RMS_EPS = 1e-6
N_DEV = 8
LANES = 128
FL_PAD = 128

ADAM_LR = 0.001
ADAM_B1 = 0.9
ADAM_B2 = 0.999
ADAM_EPS = 1e-08
ADAM_WD = 0.01
ADAM_STEP = 10

VMEM_LIMIT = 56 * 1024 * 1024
VMEM_LIMIT_MAX = 60 * 1024 * 1024
ROW_TILE = 512
ATTN_BLOCK = 512
FF_CHUNK = 256
FF_ROW_TILE = 512
DW_TOKENS = 2048


def _mm(a, b):
    return jnp.dot(a, b, preferred_element_type=F32)


def _mm_nt(a, b):
    return lax.dot_general(a, b, (((1,), (1,)), ((), ())), preferred_element_type=F32)


def _mm_tn(a, b):
    return lax.dot_general(a, b, (((0,), (0,)), ((), ())), preferred_element_type=F32)


def _sigmoid(x):
    return 1.0 / (1.0 + jnp.exp(-x))


def _params(sem, vmem=VMEM_LIMIT):
    return pltpu.CompilerParams(dimension_semantics=sem, vmem_limit_bytes=vmem)


def _const_spec(shape):
    nd = len(shape)
    return pl.BlockSpec(shape, lambda *_: (0,) * nd, pipeline_mode=pl.Buffered(1))


def _rms_fwd(x, g):
    r = lax.rsqrt(jnp.mean(x * x, axis=-1, keepdims=True) + RMS_EPS)
    xh = x * r
    return xh * g, xh, r


def _rms_bwd(dy, xh, r, g):
    dxh = dy * g
    dx = r * (dxh - xh * jnp.mean(dxh * xh, axis=-1, keepdims=True))
    return dx, dy * xh


def _in_proj(x, g1, w_uqkv, w_fl, w_g, token):
    T = x.shape[0]
    tm = ROW_TILE

    def body(x_ref, g_ref, wa_ref, wf_ref, wg_ref, token_ref, h_ref, u_ref, qkv_ref, fl_ref, gt_ref):
        h, _, _ = _rms_fwd(x_ref[...], g_ref[...])
        hb = h.astype(BF16)
        h_ref[...] = hb
        z = _mm(hb, wa_ref[...])
        u_ref[...] = z[:, :POOL_WIDTH]
        qkv_ref[...] = z[:, POOL_WIDTH:].astype(BF16)
        fl_ref[...] = _mm(hb, wf_ref[...])
        gt_ref[...] = _mm(hb, wg_ref[...]).astype(BF16)

    row = lambda n: pl.BlockSpec((tm, n), lambda i: (i, 0))
    return pl.pallas_call(
        body,
        name="in_proj",
        grid=(T // tm,),
        in_specs=[row(D_MODEL), _const_spec((1, D_MODEL)), _const_spec(w_uqkv.shape), _const_spec(w_fl.shape), _const_spec(w_g.shape), _HBM],
        out_specs=[row(D_MODEL), row(POOL_WIDTH), row(3 * ATTN_WIDTH), row(FL_PAD), row(2 * D_MODEL)],
        out_shape=[
            jax.ShapeDtypeStruct((T, D_MODEL), BF16),
            jax.ShapeDtypeStruct((T, POOL_WIDTH), F32),
            jax.ShapeDtypeStruct((T, 3 * ATTN_WIDTH), BF16),
            jax.ShapeDtypeStruct((T, FL_PAD), F32),
            jax.ShapeDtypeStruct((T, 2 * D_MODEL), BF16),
        ],
        compiler_params=_params(("parallel",)),
    )(x, g1, w_uqkv, w_fl, w_g, token)


def _log_sigmoid(x):
    return jnp.minimum(x, 0.0) - jnp.log(1.0 + jnp.exp(-jnp.abs(x)))


def _forget_fwd(fl, b_pad, n_seq, S):
    def body(fl_ref, b_ref, fcol_ref):
        lf = _log_sigmoid(fl_ref[...] + b_ref[...])
        t = lf.T
        lane = lax.broadcasted_iota(jnp.int32, t.shape, 1)
        k = 1
        while k < S:
            t = t + jnp.where(lane >= k, pltpu.roll(t, k, 1), 0.0)
            k *= 2
        fcol_ref[...] = t.T

    return pl.pallas_call(
        body,
        name="forget_fwd",
        grid=(n_seq,),
        in_specs=[pl.BlockSpec((S, FL_PAD), lambda s: (s, 0)), _const_spec((1, FL_PAD))],
        out_specs=pl.BlockSpec((S, FL_PAD), lambda s: (s, 0)),
        out_shape=jax.ShapeDtypeStruct((n_seq * S, FL_PAD), F32),
        compiler_params=_params(("parallel",)),
    )(fl, b_pad)


def _window_pick(g, v2, v4, v8, v16):
    return jnp.where(g == 0, v2, jnp.where(g == 1, v4, jnp.where(g == 2, v8, v16)))


def _pool_fwd(u, mix_b, scale, n_seq, S):
    T = n_seq * S

    def body(u_ref, mix_ref, sc_ref, pm_ref, p2_ref, p3_ref):
        g = pl.program_id(1)
        uu = u_ref[...]
        row = lax.broadcasted_iota(jnp.int32, uu.shape, 0)

        def back(a, k):
            return jnp.where(row >= k, pltpu.roll(a, k, 0), 0.0)

        s2 = uu + back(uu, 1)
        s4 = s2 + back(s2, 2)
        s8 = s4 + back(s4, 4)
        s16 = s8 + back(s8, 8)
        w = _window_pick(g, 2.0, 4.0, 8.0, 16.0)
        cnt = jnp.minimum((row + 1).astype(F32), w)
        pm = _window_pick(g, s2, s4, s8, s16) / cnt - uu
        pmb = pm.astype(BF16)
        pm_ref[...] = pmb
        p2 = _mm(pmb, mix_ref[...])
        p2_ref[...] = p2
        p3_ref[...] = (p2 * sc_ref[...]).astype(BF16)

    grp = pl.BlockSpec((S, GROUP_DIM), lambda s, g: (s, g))
    return pl.pallas_call(
        body,
        name="pool_fwd",
        grid=(n_seq, len(POOL_WINDOWS)),
        in_specs=[
            grp,
            pl.BlockSpec((None, GROUP_DIM, GROUP_DIM), lambda s, g: (g, 0, 0)),
            pl.BlockSpec((1, GROUP_DIM), lambda s, g: (0, g)),
        ],
        out_specs=[grp, grp, grp],
        out_shape=[
            jax.ShapeDtypeStruct((T, POOL_WIDTH), BF16),
            jax.ShapeDtypeStruct((T, POOL_WIDTH), F32),
            jax.ShapeDtypeStruct((T, POOL_WIDTH), BF16),
        ],
        compiler_params=_params(("parallel", "parallel")),
    )(u, mix_b, scale)


def _split3(v):
    hi = v.astype(BF16).astype(F32)
    r = v - hi
    mid = r.astype(BF16).astype(F32)
    lo = (r - mid).astype(BF16).astype(F32)
    return hi, mid, lo


def _augment(xp, hh, first, second):
    lane = lax.broadcasted_iota(jnp.int32, (1, LANES), 1)
    head = (lane >= HEAD_DIM * hh) & (lane < HEAD_DIM * (hh + 1))
    b = HEAD_DIM * (1 - hh)
    out = jnp.where(head, xp.astype(F32), 0.0)
    for n, col in enumerate(tuple(first) + tuple(second)):
        out = jnp.where(lane == b + n, col, out)
    return out.astype(BF16)


def _attn_fwd(qkv, fcol, n_seq, S):
    T = n_seq * S
    tb = ATTN_BLOCK
    nq = S // tb
    scale = HEAD_DIM ** -0.5

    def body(q_ref, k_ref, v_ref, fc_ref, o_ref, st_ref, qa_sc, ka_sc, m_sc, l_sc, acc_sc):
        i = pl.program_id(1)
        lane = lax.broadcasted_iota(jnp.int32, (1, LANES), 1)
        low = lane < HEAD_DIM
        ones = (1.0, 1.0, 1.0)

        @pl.when(i == 0)
        def _():
            def rows_ka(r, carry):
                r0 = pl.multiple_of(r * tb, tb)
                for h in range(N_HEADS):
                    kp = k_ref[pl.ds(r0, tb), (h // 2) * LANES : (h // 2 + 1) * LANES] * scale
                    fk = fc_ref[pl.ds(r0, tb), h : h + 1]
                    ka_sc[h, pl.ds(r0, tb), :] = _augment(kp, h % 2, ones, _split3(-fk))
                return carry

            lax.fori_loop(0, nq, rows_ka, 0)

        q0 = pl.multiple_of(i * tb, tb)
        for h in range(N_HEADS):
            qp = q_ref[:, (h // 2) * LANES : (h // 2 + 1) * LANES]
            qa_sc[h] = _augment(qp, h % 2, _split3(fc_ref[pl.ds(q0, tb), h : h + 1]), ones)
        m_sc[...] = jnp.full(m_sc.shape, -jnp.inf, F32)
        l_sc[...] = jnp.zeros_like(l_sc)
        acc_sc[...] = jnp.zeros_like(acc_sc)
        causal = lax.broadcasted_iota(jnp.int32, (tb, tb), 1) <= lax.broadcasted_iota(jnp.int32, (tb, tb), 0)

        def step(j, masked):
            c0 = pl.multiple_of(j * tb, tb)
            for p in range(N_PAIRS):
                vb = v_ref[pl.ds(c0, tb), p * LANES : (p + 1) * LANES]
                pv, al = [], []
                for hh in range(2):
                    h = 2 * p + hh
                    s = _mm_nt(qa_sc[h], ka_sc[h, pl.ds(c0, tb), :])
                    if masked:
                        s = jnp.where(causal, s, -jnp.inf)
                    m_old = m_sc[h]
                    m_new = jnp.maximum(m_old, jnp.max(s, axis=1, keepdims=True))
                    alpha = jnp.exp(m_old - m_new)
                    pe = jnp.exp(s - jnp.concatenate([m_new] * (tb // LANES), axis=1))
                    l_sc[h] = alpha * l_sc[h] + jnp.sum(pe, axis=1, keepdims=True)
                    m_sc[h] = m_new
                    pv.append(_mm(pe.astype(BF16), vb))
                    al.append(alpha)
                acc_sc[p] = jnp.where(low, al[0], al[1]) * acc_sc[p] + jnp.where(low, pv[0], pv[1])

        def loop_body(j, carry):
            step(j, False)
            return carry

        lax.fori_loop(0, i, loop_body, 0)
        step(i, True)
        st = jnp.zeros((tb, LANES), F32)
        for p in range(N_PAIRS):
            lp = jnp.where(low, l_sc[2 * p], l_sc[2 * p + 1])
            o_ref[:, p * LANES : (p + 1) * LANES] = (acc_sc[p] / lp).astype(BF16)
            for h in (2 * p, 2 * p + 1):
                st = jnp.where(lane == h, m_sc[h] + jnp.log(l_sc[h]), st)
        st_ref[...] = st

    return pl.pallas_call(
        body,
        name="attn_fwd",
        grid=(n_seq, nq),
        in_specs=[
            pl.BlockSpec((tb, ATTN_WIDTH), lambda s, i: (s * nq + i, 0)),
            pl.BlockSpec((S, ATTN_WIDTH), lambda s, i: (s, 1)),
            pl.BlockSpec((S, ATTN_WIDTH), lambda s, i: (s, 2)),
            pl.BlockSpec((S, LANES), lambda s, i: (s, 0)),
        ],
        out_specs=[
            pl.BlockSpec((tb, ATTN_WIDTH), lambda s, i: (s * nq + i, 0)),
            pl.BlockSpec((tb, LANES), lambda s, i: (s * nq + i, 0)),
        ],
        out_shape=[jax.ShapeDtypeStruct((T, ATTN_WIDTH), BF16), jax.ShapeDtypeStruct((T, LANES), F32)],
        scratch_shapes=[
            pltpu.VMEM((N_HEADS, tb, LANES), BF16),
            pltpu.VMEM((N_HEADS, S, LANES), BF16),
            pltpu.VMEM((N_HEADS, tb, LANES), F32),
            pltpu.VMEM((N_HEADS, tb, LANES), F32),
            pltpu.VMEM((N_PAIRS, tb, LANES), F32),
        ],
        compiler_params=_params(("parallel", "arbitrary")),
    )(qkv, qkv, qkv, fcol)


def _mix_out(a, p3, gates, x, w_ao, w_po, w_out):
    T = x.shape[0]
    tm = ROW_TILE

    def body(a_ref, p3_ref, gt_ref, x_ref, wao_ref, wpo_ref, wout_ref, mg_ref, x1_ref, ay_ref, py_ref):
        ay = _mm(a_ref[...], wao_ref[...])
        py = _mm(p3_ref[...], wpo_ref[...])
        ay_ref[...] = ay.astype(BF16)
        py_ref[...] = py.astype(BF16)
        sp = _sigmoid(gt_ref[:, :D_MODEL].astype(F32))
        sa = _sigmoid(gt_ref[:, D_MODEL:].astype(F32))
        mb = (sp * py + sa * ay).astype(BF16)
        mg_ref[...] = mb
        x1_ref[...] = x_ref[...] + _mm(mb, wout_ref[...])

    row = lambda n: pl.BlockSpec((tm, n), lambda i: (i, 0))
    return pl.pallas_call(
        body,
        name="mix_out",
        grid=(T // tm,),
        in_specs=[
            row(ATTN_WIDTH), row(POOL_WIDTH), row(2 * D_MODEL), row(D_MODEL),
            _const_spec(w_ao.shape), _const_spec(w_po.shape), _const_spec(w_out.shape),
        ],
        out_specs=[row(D_MODEL), row(D_MODEL), row(D_MODEL), row(D_MODEL)],
        out_shape=[
            jax.ShapeDtypeStruct((T, D_MODEL), BF16), jax.ShapeDtypeStruct((T, D_MODEL), F32),
            jax.ShapeDtypeStruct((T, D_MODEL), BF16), jax.ShapeDtypeStruct((T, D_MODEL), BF16),
        ],
        compiler_params=_params(("parallel",)),
    )(a, p3, gates, x, w_ao, w_po, w_out)


def _ffn_fwd(x1, g2, gf, tgt, w_gate_t, w_up_t, w_down):
    T = x1.shape[0]
    tm = min(T, FF_ROW_TILE)
    nt = T // tm
    nc = D_FF // FF_CHUNK

    def body(x1_ref, g2_ref, gf_ref, tg_ref, wg_ref, wu_ref, wd_ref, h2_ref, gate_ref, up_ref, act_ref, dx2_ref, loss_ref, dgf_ref):
        x1v = x1_ref[...]
        h2, _, _ = _rms_fwd(x1v, g2_ref[...])
        h2b = h2.astype(BF16)
        h2_ref[...] = h2b
        for c in range(nc):
            sl = slice(c * FF_CHUNK, (c + 1) * FF_CHUNK)
            gate = _mm_nt(h2b, wg_ref[sl, :])
            up = _mm_nt(h2b, wu_ref[sl, :])
            gate_ref[:, sl] = gate.astype(BF16)
            up_ref[:, sl] = up.astype(BF16)
            act_ref[:, sl] = (gate * _sigmoid(gate) * up).astype(BF16)
        acc = x1v + _mm(act_ref[...], wd_ref[...])
        gfv = gf_ref[...]
        y, xh, r = _rms_fwd(acc, gfv)
        err = y - tg_ref[...]
        part = 0.5 * jnp.sum(jnp.mean(err * err, axis=-1, keepdims=True), axis=0, keepdims=True)
        dx2, dgrow = _rms_bwd(err * (1.0 / D_MODEL), xh, r, gfv)
        dx2_ref[...] = dx2

        @pl.when(pl.program_id(0) == 0)
        def _():
            dgf_ref[...] = jnp.zeros_like(dgf_ref)
            loss_ref[...] = jnp.zeros_like(loss_ref)

        dgf_ref[...] += jnp.sum(dgrow, axis=0, keepdims=True)
        loss_ref[...] += jnp.broadcast_to(part, loss_ref.shape)

    row = lambda n: pl.BlockSpec((tm, n), lambda i: (i, 0))
    return pl.pallas_call(
        body,
        name="ffn_fwd",
        grid=(nt,),
        in_specs=[
            row(D_MODEL), _const_spec((1, D_MODEL)), _const_spec((1, D_MODEL)), row(D_MODEL),
            _const_spec(w_gate_t.shape), _const_spec(w_up_t.shape), _const_spec(w_down.shape),
        ],
        out_specs=[
            row(D_MODEL), row(D_FF), row(D_FF), row(D_FF), row(D_MODEL),
            pl.BlockSpec((8, LANES), lambda i: (0, 0)),
            pl.BlockSpec((1, D_MODEL), lambda i: (0, 0)),
        ],
        out_shape=[
            jax.ShapeDtypeStruct((T, D_MODEL), BF16),
            jax.ShapeDtypeStruct((T, D_FF), BF16),
            jax.ShapeDtypeStruct((T, D_FF), BF16),
            jax.ShapeDtypeStruct((T, D_FF), BF16),
            jax.ShapeDtypeStruct((T, D_MODEL), F32),
            jax.ShapeDtypeStruct((8, LANES), F32),
            jax.ShapeDtypeStruct((1, D_MODEL), F32),
        ],
        compiler_params=_params(("arbitrary",)),
    )(x1, g2, gf, tgt, w_gate_t, w_up_t, w_down)


def _ffn_bwd(dx2, gate, up, x1, g2, w_gate_t, w_up_t, w_down):
    T = x1.shape[0]
    tm = min(T, FF_ROW_TILE)
    nc = D_FF // FF_CHUNK

    def body(dx2_ref, gate_ref, up_ref, x1_ref, g2_ref, wg_ref, wu_ref, wd_ref, dgate_ref, dup_ref, dx1_ref, dg2_ref):
        dx2v = dx2_ref[...]
        dx2b = dx2v.astype(BF16)
        for c in range(nc):
            sl = slice(c * FF_CHUNK, (c + 1) * FF_CHUNK)
            dact = _mm_nt(dx2b, wd_ref[sl, :])
            gate = gate_ref[:, sl].astype(F32)
            sg = _sigmoid(gate)
            silu = gate * sg
            dgate = (dact * up_ref[:, sl].astype(F32) * (sg * (1.0 + gate * (1.0 - sg)))).astype(BF16)
            dup = (dact * silu).astype(BF16)
            dgate_ref[:, sl] = dgate
            dup_ref[:, sl] = dup
        dh2 = _mm(dgate_ref[...], wg_ref[...]) + _mm(dup_ref[...], wu_ref[...])
        g2v = g2_ref[...]
        _, xh, r = _rms_fwd(x1_ref[...], g2v)
        dxn, dgrow = _rms_bwd(dh2, xh, r, g2v)
        dx1_ref[...] = dx2v + dxn

        @pl.when(pl.program_id(0) == 0)
        def _():
            dg2_ref[...] = jnp.zeros_like(dg2_ref)

        dg2_ref[...] += jnp.sum(dgrow, axis=0, keepdims=True)

    row = lambda n: pl.BlockSpec((tm, n), lambda i: (i, 0))
    return pl.pallas_call(
        body,
        name="ffn_bwd",
        grid=(T // tm,),
        in_specs=[
            row(D_MODEL), row(D_FF), row(D_FF), row(D_MODEL), _const_spec((1, D_MODEL)),
            _const_spec(w_gate_t.shape), _const_spec(w_up_t.shape), _const_spec(w_down.shape),
        ],
        out_specs=[row(D_FF), row(D_FF), row(D_MODEL), pl.BlockSpec((1, D_MODEL), lambda i: (0, 0))],
        out_shape=[
            jax.ShapeDtypeStruct((T, D_FF), BF16),
            jax.ShapeDtypeStruct((T, D_FF), BF16),
            jax.ShapeDtypeStruct((T, D_MODEL), F32),
            jax.ShapeDtypeStruct((1, D_MODEL), F32),
        ],
        compiler_params=_params(("arbitrary",), VMEM_LIMIT_MAX),
    )(dx2, gate, up, x1, g2, w_gate_t, w_up_t, w_down)


def _mix_bwd(dx1, gates, pool_y, attn_y, p2, scale, w_out, w_ao, w_po, token):
    T = dx1.shape[0]
    tm = ROW_TILE

    def body(dx1_ref, gt_ref, py_ref, ay_ref, p2_ref, sc_ref, wout_ref, wao_ref, wpo_ref, token_ref, dgt_ref, dpy_ref, day_ref, da_ref, dp2_ref, dsc_ref):
        dm = _mm_nt(dx1_ref[...].astype(BF16), wout_ref[...])
        sp = _sigmoid(gt_ref[:, :D_MODEL].astype(F32))
        sa = _sigmoid(gt_ref[:, D_MODEL:].astype(F32))
        dgt_ref[:, :D_MODEL] = (dm * py_ref[...].astype(F32) * (sp * (1.0 - sp))).astype(BF16)
        dgt_ref[:, D_MODEL:] = (dm * ay_ref[...].astype(F32) * (sa * (1.0 - sa))).astype(BF16)
        dpy = (dm * sp).astype(BF16)
        day = (dm * sa).astype(BF16)
        dpy_ref[...] = dpy
        day_ref[...] = day
        da_ref[...] = _mm_nt(day, wao_ref[...]).astype(BF16)
        dp3 = _mm_nt(dpy, wpo_ref[...])
        dp2_ref[...] = (dp3 * sc_ref[...]).astype(BF16)

        @pl.when(pl.program_id(0) == 0)
        def _():
            dsc_ref[...] = jnp.zeros_like(dsc_ref)

        dsc_ref[...] += jnp.sum(dp3 * p2_ref[...], axis=0, keepdims=True)

    row = lambda n: pl.BlockSpec((tm, n), lambda i: (i, 0))
    return pl.pallas_call(
        body,
        name="mix_bwd",
        grid=(T // tm,),
        in_specs=[
            row(D_MODEL), row(2 * D_MODEL), row(D_MODEL), row(D_MODEL), row(POOL_WIDTH), _const_spec((1, POOL_WIDTH)),
            _const_spec(w_out.shape), _const_spec(w_ao.shape), _const_spec(w_po.shape), _HBM,
        ],
        out_specs=[row(2 * D_MODEL), row(D_MODEL), row(D_MODEL), row(ATTN_WIDTH), row(POOL_WIDTH), pl.BlockSpec((1, POOL_WIDTH), lambda i: (0, 0))],
        out_shape=[
            jax.ShapeDtypeStruct((T, 2 * D_MODEL), BF16),
            jax.ShapeDtypeStruct((T, D_MODEL), BF16),
            jax.ShapeDtypeStruct((T, D_MODEL), BF16),
            jax.ShapeDtypeStruct((T, ATTN_WIDTH), BF16),
            jax.ShapeDtypeStruct((T, POOL_WIDTH), BF16),
            jax.ShapeDtypeStruct((1, POOL_WIDTH), F32),
        ],
        compiler_params=_params(("arbitrary",)),
    )(dx1, gates, pool_y, attn_y, p2, scale, w_out, w_ao, w_po, token)


def _pool_bwd(dp2, pm, mix_b, token, n_seq, S):
    T = n_seq * S

    def body(dp2_ref, pm_ref, mix_ref, token_ref, du_ref, dmix_ref):
        g = pl.program_id(0)
        dp2v = dp2_ref[...]
        dpm = _mm_nt(dp2v, mix_ref[...])
        row = lax.broadcasted_iota(jnp.int32, dpm.shape, 0)
        w = _window_pick(g, 2.0, 4.0, 8.0, 16.0)
        e = dpm / jnp.minimum((row + 1).astype(F32), w)

        def ahead(a, k):
            return jnp.where(row < S - k, pltpu.roll(a, S - k, 0), 0.0)

        r2 = e + ahead(e, 1)
        r4 = r2 + ahead(r2, 2)
        r8 = r4 + ahead(r4, 4)
        r16 = r8 + ahead(r8, 8)
        du_ref[...] = (_window_pick(g, r2, r4, r8, r16) - dpm).astype(BF16)

        @pl.when(pl.program_id(1) == 0)
        def _():
            dmix_ref[...] = jnp.zeros_like(dmix_ref)

        dmix_ref[...] += _mm_tn(pm_ref[...], dp2v)

    grp = pl.BlockSpec((S, GROUP_DIM), lambda g, s: (s, g))
    mixs = pl.BlockSpec((None, GROUP_DIM, GROUP_DIM), lambda g, s: (g, 0, 0))
    return pl.pallas_call(
        body,
        name="pool_bwd",
        grid=(len(POOL_WINDOWS), n_seq),
        in_specs=[grp, grp, mixs, _HBM],
        out_specs=[grp, mixs],
        out_shape=[jax.ShapeDtypeStruct((T, POOL_WIDTH), BF16), jax.ShapeDtypeStruct((len(POOL_WINDOWS), GROUP_DIM, GROUP_DIM), F32)],
        compiler_params=_params(("parallel", "arbitrary")),
    )(dp2, pm, mix_b, token)


def _attn_bwd(qkv, da, a, fcol, lse, n_seq, S):
    T = n_seq * S
    tb = ATTN_BLOCK
    nb = S // tb
    scale = HEAD_DIM ** -0.5

    def body(q_ref, k_ref, v_ref, do_ref, o_ref, fc_ref, st_ref, dq_ref, dk_ref, dv_ref, dfk_ref, dfq_ref,
             qa_sc, doa_sc, dq_acc, ka_sc, va_sc, dk_sc, dv_sc):
        j = pl.program_id(1)
        lane = lax.broadcasted_iota(jnp.int32, (1, LANES), 1)
        low = lane < HEAD_DIM
        ones = (1.0, 1.0, 1.0)
        zeros = (0.0, 0.0, 0.0)

        @pl.when(j == 0)
        def _():
            dq_acc[...] = jnp.zeros_like(dq_acc)

            def rows_q(i, carry):
                r0 = pl.multiple_of(i * tb, tb)
                for h in range(N_HEADS):
                    pair = slice((h // 2) * LANES, (h // 2 + 1) * LANES)
                    qp = q_ref[pl.ds(r0, tb), pair]
                    dop = do_ref[pl.ds(r0, tb), pair]
                    prod = dop.astype(F32) * o_ref[pl.ds(r0, tb), pair].astype(F32)
                    head = (lane >= HEAD_DIM * (h % 2)) & (lane < HEAD_DIM * (h % 2 + 1))
                    delta = jnp.sum(jnp.where(head, prod, 0.0), axis=1, keepdims=True)
                    cq = fc_ref[pl.ds(r0, tb), h : h + 1] - st_ref[pl.ds(r0, tb), h : h + 1]
                    qa_sc[h, pl.ds(r0, tb), :] = _augment(qp, h % 2, _split3(cq), ones)
                    doa_sc[h, pl.ds(r0, tb), :] = _augment(dop, h % 2, _split3(-delta), zeros)
                return carry

            lax.fori_loop(0, nb, rows_q, 0)

        c0 = pl.multiple_of(j * tb, tb)
        for h in range(N_HEADS):
            pair = slice((h // 2) * LANES, (h // 2 + 1) * LANES)
            kp = k_ref[:, pair] * scale
            ka_sc[h] = _augment(kp, h % 2, ones, _split3(-fc_ref[pl.ds(c0, tb), h : h + 1]))
            va_sc[h] = _augment(v_ref[:, pair], h % 2, ones, zeros)
        dk_sc[...] = jnp.zeros_like(dk_sc)
        dv_sc[...] = jnp.zeros_like(dv_sc)
        causal = lax.broadcasted_iota(jnp.int32, (tb, tb), 1) <= lax.broadcasted_iota(jnp.int32, (tb, tb), 0)

        def step(i, masked):
            r0 = pl.multiple_of(i * tb, tb)
            for h in range(N_HEADS):
                dob = do_ref[pl.ds(r0, tb), (h // 2) * LANES : (h // 2 + 1) * LANES]
                qa = qa_sc[h, pl.ds(r0, tb), :]
                s = _mm_nt(qa, ka_sc[h])
                if masked:
                    s = jnp.where(causal, s, -jnp.inf)
                pr = jnp.exp(s)
                dv_sc[h] += _mm_tn(pr.astype(BF16), dob)
                dsb = (pr * _mm_nt(doa_sc[h, pl.ds(r0, tb), :], va_sc[h])).astype(BF16)
                dk_sc[h] += _mm_tn(dsb, qa)
                dq_acc[h, pl.ds(r0, tb), :] += _mm(dsb, ka_sc[h])

        step(j, True)

        def loop_body(i, carry):
            step(i, False)
            return carry

        lax.fori_loop(j + 1, nb, loop_body, 0)
        dfk = jnp.zeros((tb, LANES), F32)
        for p in range(N_PAIRS):
            dk_ref[:, p * LANES : (p + 1) * LANES] = (jnp.where(low, dk_sc[2 * p], dk_sc[2 * p + 1]) * scale).astype(BF16)
            dv_ref[:, p * LANES : (p + 1) * LANES] = jnp.where(low, dv_sc[2 * p], dv_sc[2 * p + 1]).astype(BF16)
            for hh in range(2):
                b = HEAD_DIM * (1 - hh) + 3
                dfk = jnp.where(lane == 2 * p + hh, -dk_sc[2 * p + hh][:, b : b + 1], dfk)
        dfk_ref[...] = dfk

        @pl.when(j == nb - 1)
        def _():
            def rows_dq(i, carry):
                r0 = pl.multiple_of(i * tb, tb)
                dfq = jnp.zeros((tb, LANES), F32)
                for p in range(N_PAIRS):
                    parts = [dq_acc[2 * p + hh, pl.ds(r0, tb), :] for hh in range(2)]
                    dq_ref[pl.ds(r0, tb), p * LANES : (p + 1) * LANES] = jnp.where(low, parts[0], parts[1]).astype(BF16)
                    for hh in range(2):
                        b = HEAD_DIM * (1 - hh)
                        dfq = jnp.where(lane == 2 * p + hh, parts[hh][:, b : b + 1], dfq)
                dfq_ref[pl.ds(r0, tb), :] = dfq
                return carry

            lax.fori_loop(0, nb, rows_dq, 0)

    seq = lambda w, col: pl.BlockSpec((S, w), lambda s, j: (s, col))
    blk = lambda w, col: pl.BlockSpec((tb, w), lambda s, j: (s * nb + j, col))
    return pl.pallas_call(
        body,
        name="attn_bwd",
        grid=(n_seq, nb),
        in_specs=[seq(ATTN_WIDTH, 0), blk(ATTN_WIDTH, 1), blk(ATTN_WIDTH, 2), seq(ATTN_WIDTH, 0), seq(ATTN_WIDTH, 0), seq(LANES, 0), seq(LANES, 0)],
        out_specs=[seq(ATTN_WIDTH, 0), blk(ATTN_WIDTH, 0), blk(ATTN_WIDTH, 0), blk(LANES, 0), seq(LANES, 0)],
        out_shape=[
            jax.ShapeDtypeStruct((T, ATTN_WIDTH), BF16),
            jax.ShapeDtypeStruct((T, ATTN_WIDTH), BF16),
            jax.ShapeDtypeStruct((T, ATTN_WIDTH), BF16),
            jax.ShapeDtypeStruct((T, LANES), F32),
            jax.ShapeDtypeStruct((T, LANES), F32),
        ],
        scratch_shapes=[
            pltpu.VMEM((N_HEADS, S, LANES), BF16),
            pltpu.VMEM((N_HEADS, S, LANES), BF16),
            pltpu.VMEM((N_HEADS, S, LANES), F32),
            pltpu.VMEM((N_HEADS, tb, LANES), BF16),
            pltpu.VMEM((N_HEADS, tb, LANES), BF16),
            pltpu.VMEM((N_HEADS, tb, LANES), F32),
            pltpu.VMEM((N_HEADS, tb, LANES), F32),
        ],
        compiler_params=_params(("parallel", "arbitrary")),
    )(qkv, qkv, qkv, da, a, fcol, lse)


def _forget_bwd(dfk, dfq, fl, b_pad, n_seq, S):
    def body(df_ref, dfq_ref, fl_ref, b_ref, dfl_ref, db_ref):
        t = (df_ref[...] + dfq_ref[...]).T
        lane = lax.broadcasted_iota(jnp.int32, t.shape, 1)
        k = 1
        while k < S:
            t = t + jnp.where(lane < S - k, pltpu.roll(t, S - k, 1), 0.0)
            k *= 2
        dfl = t.T * _sigmoid(-(fl_ref[...] + b_ref[...]))
        dfl_ref[...] = dfl.astype(BF16)

        @pl.when(pl.program_id(0) == 0)
        def _():
            db_ref[...] = jnp.zeros_like(db_ref)

        db_ref[...] += jnp.sum(dfl, axis=0, keepdims=True)

    return pl.pallas_call(
        body,
        name="forget_bwd",
        grid=(n_seq,),
        in_specs=[
            pl.BlockSpec((S, LANES), lambda s: (s, 0)),
            pl.BlockSpec((S, LANES), lambda s: (s, 0)),
            pl.BlockSpec((S, FL_PAD), lambda s: (s, 0)),
            _const_spec((1, FL_PAD)),
        ],
        out_specs=[pl.BlockSpec((S, FL_PAD), lambda s: (s, 0)), pl.BlockSpec((1, FL_PAD), lambda s: (0, 0))],
        out_shape=[jax.ShapeDtypeStruct((n_seq * S, FL_PAD), BF16), jax.ShapeDtypeStruct((1, FL_PAD), F32)],
        compiler_params=_params(("arbitrary",)),
    )(dfk, dfq, fl, b_pad)


def _in_proj_bwd(du, dq, dk, dv, dfl, dgates, x, dx1, g1, w_uqkv, w_fl, w_g):
    T = x.shape[0]
    tm = ROW_TILE

    def body(du_ref, dq_ref, dk_ref, dv_ref, dfl_ref, dgt_ref, x_ref, dx1_ref, g_ref, wa_ref, wf_ref, wg_ref, dx_ref, dg_ref):
        dz = jnp.concatenate([du_ref[...], dq_ref[...], dk_ref[...], dv_ref[...]], axis=1)
        dh = _mm_nt(dz, wa_ref[...]) + _mm_nt(dgt_ref[...], wg_ref[...]) + _mm_nt(dfl_ref[...], wf_ref[...])
        gv = g_ref[...]
        _, xh, r = _rms_fwd(x_ref[...], gv)
        dxn, dgrow = _rms_bwd(dh, xh, r, gv)
        dx_ref[...] = dx1_ref[...] + dxn

        @pl.when(pl.program_id(0) == 0)
        def _():
            dg_ref[...] = jnp.zeros_like(dg_ref)

        dg_ref[...] += jnp.sum(dgrow, axis=0, keepdims=True)

    row = lambda n: pl.BlockSpec((tm, n), lambda i: (i, 0))
    return pl.pallas_call(
        body,
        name="in_proj_bwd",
        grid=(T // tm,),
        in_specs=[
            row(512), row(512), row(512), row(512), row(FL_PAD), row(2 * D_MODEL), row(D_MODEL), row(D_MODEL), _const_spec((1, D_MODEL)),
            _const_spec(w_uqkv.shape), _const_spec(w_fl.shape), _const_spec(w_g.shape),
        ],
        out_specs=[row(D_MODEL), pl.BlockSpec((1, D_MODEL), lambda i: (0, 0))],
        out_shape=[jax.ShapeDtypeStruct((T, D_MODEL), F32), jax.ShapeDtypeStruct((1, D_MODEL), F32)],
        compiler_params=_params(("arbitrary",)),
    )(du, dq, dk, dv, dfl, dgates, x, dx1, g1, w_uqkv, w_fl, w_g)


def _pick_block(n):
    for b in (512, 1408, 256, 128):
        if n % b == 0:
            return b
    raise ValueError(n)


def _matmul_tn(a, b, name, token=None):
    T, K = a.shape
    N = b.shape[1]
    bt, bk, bn = min(T, DW_TOKENS), _pick_block(K), _pick_block(N)
    nt = T // bt

    def body(a_ref, b_ref, *rest):
        o_ref, acc = rest[-2:]

        @pl.when(pl.program_id(2) == 0)
        def _():
            acc[...] = jnp.zeros_like(acc)

        acc[...] += _mm_tn(a_ref[...].astype(BF16), b_ref[...].astype(BF16))

        @pl.when(pl.program_id(2) == nt - 1)
        def _():
            o_ref[...] = acc[...].astype(BF16)

    ordering = [] if token is None else [token]
    return pl.pallas_call(
        body,
        name=name,
        grid=(K // bk, N // bn, nt),
        in_specs=[pl.BlockSpec((bt, bk), lambda k, n, t: (t, k)), pl.BlockSpec((bt, bn), lambda k, n, t: (t, n))] + [_HBM] * len(ordering),
        out_specs=pl.BlockSpec((bk, bn), lambda k, n, t: (k, n)),
        out_shape=jax.ShapeDtypeStruct((K, N), BF16),
        scratch_shapes=[pltpu.VMEM((bk, bn), F32)],
        compiler_params=_params(("parallel", "parallel", "arbitrary")),
    )(a, b, *ordering)


def _position():
    return lax.axis_index("x"), lax.axis_index("y"), lax.axis_index("c")


_HBM = pl.BlockSpec(memory_space=pl.ANY)


def _all_gather(blocks, name):
    n = len(blocks)

    def body(*refs):
        xs, outs = refs[:n], refs[n : 2 * n]
        send_sems, recv_sems, local_sems = refs[2 * n :]
        x, y, c = _position()
        me, sibling = (x, y, c), (x, y, 1 - c)
        chips = [(1 - x, y), (x, 1 - y), (1 - x, 1 - y)]

        def rows(a, px, py, pc):
            return outs[a].at[4 * px + 2 * py + pc]

        def copy(a, k, blk, to, src=None):
            return pltpu.make_async_remote_copy(
                src_ref=rows(a, *blk) if src is None else src, dst_ref=rows(a, *blk),
                send_sem=send_sems.at[7 * a + k], recv_sem=recv_sems.at[7 * a + k], device_id=to, device_id_type=MESH,
            )

        mine = [pltpu.make_async_copy(xs[a], rows(a, *me), local_sems.at[a]) for a in range(n)]
        for cp in mine:
            cp.start()
        first = []
        for a in range(n):
            first.append(copy(a, 0, me, sibling, src=xs[a]))
            first += [copy(a, 1 + j, me, (*chip, c), src=xs[a]) for j, chip in enumerate(chips)]
        for cp in first:
            cp.start()
        passed = []
        for j, chip in enumerate(chips):
            for a in range(n):
                copy(a, 1 + j, (*chip, c), me).wait_recv()
                passed.append(copy(a, 4 + j, (*chip, c), sibling))
                passed[-1].start()
        for a in range(n):
            copy(a, 0, sibling, me).wait_recv()
        for j, chip in enumerate(chips):
            for a in range(n):
                copy(a, 4 + j, (*chip, 1 - c), me).wait_recv()
        for cp in first + passed:
            cp.wait_send()
        for cp in mine:
            cp.wait()

    return pl.pallas_call(
        body,
        name=name,
        out_shape=[jax.ShapeDtypeStruct((N_DEV, *b.shape), b.dtype) for b in blocks],
        in_specs=[_HBM] * n,
        out_specs=[_HBM] * n,
        scratch_shapes=[pltpu.SemaphoreType.DMA((7 * n,)), pltpu.SemaphoreType.DMA((7 * n,)), pltpu.SemaphoreType.DMA((n,))],
    )(*blocks)


_SEM = pl.BlockSpec(memory_space=pltpu.SEMAPHORE)
_HBM_ONLY = pl.BlockSpec(memory_space=pltpu.HBM)
_SIDE_EFFECT = pltpu.SideEffectType.DATAFLOW_SIDE_EFFECTING


def _peer(x, y, c, k):
    return (1 - x if k & 4 else x, 1 - y if k & 2 else y, 1 - c if k & 1 else c)


_PEER_BITS = {"gather": range(1, N_DEV), "scatter": range(1, N_DEV), "chips": (4, 2, 6)}
_LAND_SLOTS = {"gather": N_DEV, "scatter": N_DEV, "chips": 3}


def _exchange_copies(src_refs, land_refs, send_sems, recv_sems, pattern, receive_side):
    x, y, c = _position()
    me = 4 * x + 2 * y + c
    bits = _PEER_BITS[pattern]
    cps = []
    for j, k in enumerate(bits):
        px, py, pc = _peer(x, y, c, k)
        peer = 4 * px + 2 * py + pc
        for a, (src, land) in enumerate(zip(src_refs, land_refs)):
            if pattern == "chips":
                s, slot = src.at[2 * px + py], j
            else:
                s, slot = (src if pattern == "gather" else src.at[peer]), (peer if receive_side else me)
            cps.append(pltpu.make_async_remote_copy(
                src_ref=s, dst_ref=land.at[slot],
                send_sem=send_sems.at[len(bits) * a + j], recv_sem=recv_sems.at[len(bits) * a + j],
                device_id=(px, py, pc), device_id_type=MESH,
            ))
    return cps


def _exchange_start(srcs, after, name, pattern):
    n = len(srcs)
    m = len(_PEER_BITS[pattern])
    lands = [jax.ShapeDtypeStruct((_LAND_SLOTS[pattern], *s.shape[-2:]), s.dtype) for s in srcs]

    def body(*refs):
        src_refs, land_refs = refs[1 : 1 + n], refs[1 + n : 1 + 2 * n]
        send_sems, recv_sems = refs[1 + 2 * n], refs[2 + 2 * n]
        token = refs[-1]
        for cp in _exchange_copies(src_refs, land_refs, send_sems, recv_sems, pattern, receive_side=False):
            cp.start()
        token[...] = jnp.zeros_like(token)

    hbm = lambda t: pltpu.with_memory_space_constraint(t, pltpu.HBM)
    out = pl.pallas_call(
        body,
        name=name,
        out_shape=(
            pltpu.SemaphoreType.DMA((m * n,)), pltpu.SemaphoreType.DMA((m * n,)),
            *[pltpu.HBM(s.shape, s.dtype) for s in srcs], *[pltpu.HBM(l.shape, l.dtype) for l in lands],
            jax.ShapeDtypeStruct((8, LANES), F32),
        ),
        in_specs=(_HBM, *[_HBM_ONLY] * (2 * n)),
        out_specs=(_SEM, _SEM, *[_HBM_ONLY] * (2 * n), pl.BlockSpec(memory_space=pltpu.VMEM)),
        input_output_aliases={1 + i: 2 + i for i in range(2 * n)},
        compiler_params=pltpu.CompilerParams(has_side_effects=_SIDE_EFFECT),
    )(after, *[hbm(s) for s in srcs], *[hbm(lax.empty(l.shape, l.dtype)) for l in lands])
    return out[0], out[1], out[2 : 2 + n], out[2 + n : 2 + 2 * n], out[-1]


def _exchange_wait(send_sems, recv_sems, srcs, lands, after, name, pattern):
    n = len(srcs)

    def body(*refs):
        src_refs, land_refs = refs[:n], refs[n : 2 * n]
        for cp in _exchange_copies(src_refs, land_refs, refs[2 * n], refs[2 * n + 1], pattern, receive_side=True):
            cp.wait_send()
            cp.wait_recv()

    out = pl.pallas_call(
        body,
        name=name,
        out_shape=(*[pltpu.HBM(s.shape, s.dtype) for s in srcs], *[pltpu.HBM(l.shape, l.dtype) for l in lands]),
        in_specs=(*[_HBM_ONLY] * (2 * n), _SEM, _SEM, _HBM),
        out_specs=tuple([_HBM_ONLY] * (2 * n)),
        input_output_aliases={i: i for i in range(2 * n)},
        compiler_params=pltpu.CompilerParams(has_side_effects=_SIDE_EFFECT),
    )(*srcs, *lands, send_sems, recv_sems, after)
    return out[:n], out[n:]


def _sibling_exchange(sends):
    n = len(sends)

    def body(*refs):
        srcs, dsts = refs[:n], refs[n : 2 * n]
        send_sems, recv_sems = refs[2 * n :]
        x, y, c = _position()
        cps = [
            pltpu.make_async_remote_copy(
                src_ref=srcs[a].at[1 - c], dst_ref=dsts[a], send_sem=send_sems.at[a], recv_sem=recv_sems.at[a],
                device_id=(x, y, 1 - c), device_id_type=MESH,
            )
            for a in range(n)
        ]
        for cp in cps:
            cp.start()
        for cp in cps:
            cp.wait()

    return pl.pallas_call(
        body,
        name="rs_sibling",
        out_shape=[jax.ShapeDtypeStruct(s.shape[1:], s.dtype) for s in sends],
        in_specs=[_HBM] * n,
        out_specs=[_HBM] * n,
        scratch_shapes=[pltpu.SemaphoreType.DMA((n,)), pltpu.SemaphoreType.DMA((n,))],
    )(*sends)


def _rows_tile(r):
    return ROW_TILE if r % ROW_TILE == 0 else r


def _pair_sum(send, got, core, name):
    _, _, r, c = send.shape
    br = _rows_tile(r)

    def body(core_ref, a_ref, b_ref, o_ref):
        o_ref[...] = (a_ref[...].astype(F32) + b_ref[...].astype(F32)).astype(o_ref.dtype)

    return pl.pallas_call(
        body,
        name=name,
        grid_spec=pltpu.PrefetchScalarGridSpec(
            num_scalar_prefetch=1,
            grid=(4, r // br),
            in_specs=[
                pl.BlockSpec((None, None, br, c), lambda n, i, core: (core[0], n, i, 0)),
                pl.BlockSpec((None, br, c), lambda n, i, core: (n, i, 0)),
            ],
            out_specs=pl.BlockSpec((None, br, c), lambda n, i, core: (n, i, 0)),
        ),
        out_shape=jax.ShapeDtypeStruct((4, r, c), send.dtype),
        compiler_params=_params(("parallel", "parallel")),
    )(core, send, got)


def _adamw(w, g, m, v):
    m = ADAM_B1 * m + (1.0 - ADAM_B1) * g
    v = ADAM_B2 * v + (1.0 - ADAM_B2) * (g * g)
    m_hat = m / (1.0 - ADAM_B1 ** ADAM_STEP)
    v_hat = v / (1.0 - ADAM_B2 ** ADAM_STEP)
    delta = -ADAM_LR * (m_hat / (jnp.sqrt(v_hat) + ADAM_EPS) + ADAM_WD * w)
    return delta, m, v


def _shard_update(send, got, recv, w, m, v, pos, name):
    _, r, c = w.shape
    br = _rows_tile(r)

    def body(pos_ref, a_ref, b_ref, r_ref, w_ref, m_ref, v_ref, g_ref, d_ref, nm_ref, nv_ref):
        g = a_ref[...].astype(F32) + b_ref[...].astype(F32)
        for n in range(3):
            g = g + r_ref[n].astype(F32)
        g_ref[...] = g
        d_ref[...], nm_ref[...], nv_ref[...] = _adamw(w_ref[...], g, m_ref[...], v_ref[...])

    own = pl.BlockSpec((None, br, c), lambda i, pos: (0, i, 0))
    return pl.pallas_call(
        body,
        name=name,
        grid_spec=pltpu.PrefetchScalarGridSpec(
            num_scalar_prefetch=1,
            grid=(r // br,),
            in_specs=[
                pl.BlockSpec((None, None, br, c), lambda i, pos: (pos[0], pos[1], i, 0)),
                pl.BlockSpec((None, br, c), lambda i, pos: (pos[1], i, 0)),
                pl.BlockSpec((3, br, c), lambda i, pos: (0, i, 0)),
                own, own, own,
            ],
            out_specs=[own, own, own, own],
        ),
        out_shape=[jax.ShapeDtypeStruct((1, r, c), F32)] * 4,
        compiler_params=_params(("parallel",)),
    )(pos, send, got, recv, w, m, v)


def _shard_update_direct(parts, chunks, w, m, v, me, name):
    _, r, c = w.shape
    br = _rows_tile(r)

    def body(me_ref, p_ref, own_ref, w_ref, m_ref, v_ref, g_ref, d_ref, nm_ref, nv_ref):
        g = None
        for n in range(N_DEV):
            part = jnp.where(me_ref[0] == n, own_ref[...], p_ref[n]).astype(F32)
            g = part if g is None else g + part
        g_ref[...] = g
        d_ref[...], nm_ref[...], nv_ref[...] = _adamw(w_ref[...], g, m_ref[...], v_ref[...])

    shard = pl.BlockSpec((None, br, c), lambda i, me: (0, i, 0))
    return pl.pallas_call(
        body,
        name=name,
        grid_spec=pltpu.PrefetchScalarGridSpec(
            num_scalar_prefetch=1,
            grid=(r // br,),
            in_specs=[
                pl.BlockSpec((N_DEV, br, c), lambda i, me: (0, i, 0)),
                pl.BlockSpec((None, br, c), lambda i, me: (me[0], i, 0)),
                shard, shard, shard,
            ],
            out_specs=[shard, shard, shard, shard],
        ),
        out_shape=[jax.ShapeDtypeStruct((1, r, c), F32)] * 4,
        compiler_params=_params(("parallel",)),
    )(me, parts, chunks, w, m, v)


def _small_update(parts, w, m, v):
    R = w.shape[0]

    def body(p_ref, w_ref, m_ref, v_ref, g_ref, d_ref, nm_ref, nv_ref):
        g = p_ref[0]
        for n in range(1, N_DEV):
            g = g + p_ref[n]
        g_ref[...] = g
        d_ref[...], nm_ref[...], nv_ref[...] = _adamw(w_ref[...], g, m_ref[...], v_ref[...])

    return pl.pallas_call(
        body,
        name="small_update",
        out_shape=[jax.ShapeDtypeStruct((R, LANES), F32)] * 4,
        compiler_params=pltpu.CompilerParams(vmem_limit_bytes=VMEM_LIMIT),
    )(parts, w, m, v)


_SHARD_AXIS = (1, 1, 1, 0, 0, 0, 0)
_TRANSPOSED = (False, False, False, False, True, True, False)


def _full_from_gathered(t, axis):
    if axis == 0:
        return t.reshape(N_DEV * t.shape[1], t.shape[2])
    return jnp.concatenate([t[d] for d in range(N_DEV)], axis=1)


def _chunks_from_cols(t):
    c = t.shape[1] // N_DEV
    return jnp.stack([t[:, d * c : (d + 1) * c] for d in range(N_DEV)])


def _send_from_cols(t):
    c = t.shape[1] // N_DEV
    return jnp.stack([jnp.stack([t[:, (2 * chip + core) * c : (2 * chip + core + 1) * c] for chip in range(4)]) for core in range(2)])


_SMALL = (("norm1_g", 8), ("norm2_g", 8), ("norm_f_g", 8), ("b_forget", 8), ("pool_scale", 8), ("pool_mix", 512))
_SMALL_ROWS = sum(r for _, r in _SMALL) + 8


def _pack_small(vals, loss_row):
    parts = []
    for (name, rows), t in zip(_SMALL, vals):
        f = t.astype(F32).reshape(-1)
        f = jnp.concatenate([f, jnp.zeros((rows * LANES - f.shape[0],), F32)]).reshape(rows, LANES)
        parts.append(f)
    parts.append(loss_row)
    return jnp.concatenate(parts, axis=0)


def _unpack_small(packed, shapes):
    out, off = [], 0
    for (name, rows), shape in zip(_SMALL, shapes):
        n = 1
        for s in shape:
            n *= s
        out.append(packed[off : off + rows].reshape(-1)[:n].reshape(shape))
        off += rows
    return out, packed[off, 0]


def _local_grads(x, tgt, g1, g2, gf, b_forget, pool_mix, pool_scale, w_in, fwd_token, out_weights, ffn_weights, ffn_grads_out, out_grads_out, small_grads_out):
    n_seq, S, _ = x.shape
    T = n_seq * S
    x2 = x.reshape(T, D_MODEL)
    tg2 = tgt.reshape(T, D_MODEL)
    w_uqkv = w_in[:, : POOL_WIDTH + 3 * ATTN_WIDTH]
    w_fl = jnp.concatenate([w_in[:, 2048 : 2048 + N_HEADS], jnp.zeros((D_MODEL, FL_PAD - N_HEADS), BF16)], axis=1)
    w_g = w_in[:, 2048 + N_HEADS :]
    b_pad = jnp.concatenate([b_forget.reshape(1, N_HEADS), jnp.zeros((1, FL_PAD - N_HEADS), F32)], axis=1)
    mix_b = pool_mix.reshape(len(POOL_WINDOWS), GROUP_DIM, GROUP_DIM).astype(BF16)
    scale = pool_scale.reshape(1, POOL_WIDTH)
    g1 = g1.reshape(1, D_MODEL)
    g2 = g2.reshape(1, D_MODEL)
    gf = gf.reshape(1, D_MODEL)

    h, u, qkv, fl, gates = _in_proj(x2, g1, w_uqkv, w_fl, w_g, fwd_token)
    fcol = _forget_fwd(fl, b_pad, n_seq, S)
    pm, p2, p3 = _pool_fwd(u, mix_b, scale, n_seq, S)
    a, lse = _attn_fwd(qkv, fcol, n_seq, S)
    w_po, w_ao, w_out = out_weights(a)
    merged, x1, attn_y, pool_y = _mix_out(a, p3, gates, x2, w_ao, w_po, w_out)
    w_gate_t, w_up_t, w_down = ffn_weights(x1)
    h2, gate, up, act, dx2, loss_rows, dgf = _ffn_fwd(x1, g2, gf, tg2, w_gate_t, w_up_t, w_down)

    dgate, dup, dx1, dg2 = _ffn_bwd(dx2, gate, up, x1, g2, w_gate_t, w_up_t, w_down)
    bwd_token = ffn_grads_out(_matmul_tn(dgate, h2, "dw_ffn_gate"), _matmul_tn(dup, h2, "dw_ffn_up"), _matmul_tn(act, dx2, "dw_ffn_down"))
    dgates, dpy, day, da, dp2, dscale = _mix_bwd(dx1, gates, pool_y, attn_y, p2, scale, w_out, w_ao, w_po, bwd_token)
    out_token = out_grads_out(_matmul_tn(p3, dpy, "dw_pool_out"), _matmul_tn(a, day, "dw_attn_out"), _matmul_tn(merged, dx1, "dw_out"))
    du, dmix = _pool_bwd(dp2, pm, mix_b, out_token, n_seq, S)
    dq, dk, dv, dfk, dfq = _attn_bwd(qkv, da, a, fcol, lse, n_seq, S)
    dfl, db = _forget_bwd(dfk, dfq, fl, b_pad, n_seq, S)
    dx, dg1 = _in_proj_bwd(du, dq, dk, dv, dfl, dgates, x2, dx1, g1, w_uqkv, w_fl, w_g)
    small_token = small_grads_out((dg1, dg2, dgf, db[:, :N_HEADS], dscale, dmix), loss_rows)

    d_w_in = jnp.concatenate(
        [
            _matmul_tn(h, du, "dw_u", small_token), _matmul_tn(h, dq, "dw_q"), _matmul_tn(h, dk, "dw_k"), _matmul_tn(h, dv, "dw_v"),
            _matmul_tn(h, dfl, "dw_fl")[:, :N_HEADS], _matmul_tn(h, dgates, "dw_gates"),
        ],
        axis=1,
    )
    return dx.reshape(n_seq, S, D_MODEL), _send_from_cols(d_w_in)


def kernel(x, norm1_g, w_in, b_forget, pool_mix, pool_scale, w_pool_out, w_attn_out, w_out, norm2_g, w_ffn_gate, w_ffn_up, w_ffn_down, norm_f_g, loss_target, m_norm1_g, m_w_in, m_b_forget, m_pool_mix, m_pool_scale, m_w_pool_out, m_w_attn_out, m_w_out, m_norm2_g, m_w_ffn_gate, m_w_ffn_up, m_w_ffn_down, m_norm_f_g, v_norm1_g, v_w_in, v_b_forget, v_pool_mix, v_pool_scale, v_w_pool_out, v_w_attn_out, v_w_out, v_norm2_g, v_w_ffn_gate, v_w_ffn_up, v_w_ffn_down, v_norm_f_g):
    names = ("w_in", "w_pool_out", "w_attn_out", "w_out", "w_ffn_gate", "w_ffn_up", "w_ffn_down")
    w_sh = (w_in, w_pool_out, w_attn_out, w_out, w_ffn_gate, w_ffn_up, w_ffn_down)
    m_sh = (m_w_in, m_w_pool_out, m_w_attn_out, m_w_out, m_w_ffn_gate, m_w_ffn_up, m_w_ffn_down)
    v_sh = (v_w_in, v_w_pool_out, v_w_attn_out, v_w_out, v_w_ffn_gate, v_w_ffn_up, v_w_ffn_down)

    cx, cy, cc = _position()
    me = 4 * cx + 2 * cy + cc
    def stored(t, transposed):
        return jnp.transpose(t, (0, 2, 1)) if transposed else t

    w_sh, m_sh, v_sh = ([stored(t, tr) for t, tr in zip(ts, _TRANSPOSED)] for ts in (w_sh, m_sh, v_sh))
    shards = [w[0].astype(BF16) for w in w_sh]
    (gathered_in,) = _all_gather(shards[:1], "w_in_all_gather")
    out_sems = _exchange_start(shards[1:4], gathered_in, "out_weights_gather_start", "gather")
    ffn_sems = _exchange_start(shards[4:], out_sems[4], "ffn_weights_gather_start", "gather")
    no_order = jnp.zeros((8, LANES), F32)

    def with_own(lands, own):
        return [lax.dynamic_update_slice(l, o[None], (me, 0, 0)) for l, o in zip(lands, own)]

    def gathered_weights(sems, axes, name):
        def wait(after):
            send_sems, recv_sems, srcs, lands, _ = sems
            srcs, lands = _exchange_wait(send_sems, recv_sems, srcs, lands, after, name, "gather")
            return [_full_from_gathered(t, axis) for t, axis in zip(with_own(lands, srcs), axes)]

        return wait

    started = {}

    def scatter_grads(key, name):
        def start(*whole_grads):
            chunks = [
                _chunks_from_cols(t) if axis == 1 else t.reshape(N_DEV, -1, t.shape[1])
                for t, axis in zip(whole_grads, _SHARD_AXIS[key])
            ]
            started[key] = _exchange_start(chunks, no_order, name, "scatter")
            return started[key][4]

        return start

    def gather_small(small, loss_rows):
        started["small"] = _exchange_start([_pack_small(small, loss_rows)], no_order, "small_grads_gather_start", "gather")
        return started["small"][4]

    ffn, out = slice(4, 7), slice(1, 4)
    grad_x, send_in = _local_grads(
        x, loss_target, norm1_g, norm2_g, norm_f_g, b_forget, pool_mix, pool_scale, _full_from_gathered(gathered_in, 1), ffn_sems[4],
        gathered_weights(out_sems, _SHARD_AXIS[out], "out_weights_gather_wait"),
        gathered_weights(ffn_sems, _SHARD_AXIS[ffn], "ffn_weights_gather_wait"),
        scatter_grads(ffn, "ffn_grads_scatter_start"), scatter_grads(out, "out_grads_scatter_start"), gather_small,
    )

    core = jnp.reshape(cc, (1,)).astype(jnp.int32)
    pos = jnp.stack([cc, 2 * cx + cy]).astype(jnp.int32)
    (got_in,) = _sibling_exchange([send_in])
    pair_in = _pair_sum(send_in, got_in, core, "pair_sum_w_in")
    chip_sems = _exchange_start([pair_in], no_order, "w_in_grads_chips_start", "chips")

    def scattered_updates(key, after, name):
        send_sems, recv_sems, srcs, lands, _ = started[key]
        srcs, lands = _exchange_wait(send_sems, recv_sems, srcs, lands, after, name, "scatter")
        return [
            _shard_update_direct(p, s, w, m, v, jnp.reshape(me, (1,)).astype(jnp.int32), "update_" + n)
            for p, s, w, m, v, n in zip(lands, srcs, w_sh[key], m_sh[key], v_sh[key], names[key])
        ]

    updates_out = scattered_updates(out, chip_sems[4], "out_grads_scatter_wait")
    updates_ffn = scattered_updates(ffn, chip_sems[4], "ffn_grads_scatter_wait")

    small_w = (norm1_g, norm2_g, norm_f_g, b_forget, pool_scale, pool_mix)
    small_m = (m_norm1_g, m_norm2_g, m_norm_f_g, m_b_forget, m_pool_scale, m_pool_mix)
    small_v = (v_norm1_g, v_norm2_g, v_norm_f_g, v_b_forget, v_pool_scale, v_pool_mix)
    zero_row = jnp.zeros((8, LANES), F32)
    send_sems, recv_sems, srcs, lands, _ = started["small"]
    srcs, lands = _exchange_wait(send_sems, recv_sems, srcs, lands, updates_ffn[-1][0], "small_grads_gather_wait", "gather")
    (parts,) = with_own(lands, srcs)
    g_s, d_s, nm_s, nv_s = _small_update(parts, _pack_small(small_w, zero_row), _pack_small(small_m, zero_row), _pack_small(small_v, zero_row))

    send_sems, recv_sems, srcs, lands, _ = chip_sems
    _, (recv_in,) = _exchange_wait(send_sems, recv_sems, srcs, lands, g_s, "w_in_grads_chips_wait", "chips")
    update_in = _shard_update(send_in, got_in, recv_in, w_in, m_w_in, v_w_in, pos, "update_w_in")
    g_w, d_w, nm_w, nv_w = zip(*(
        [stored(t, tr) for t in u] for u, tr in zip([update_in] + updates_out + updates_ffn, _TRANSPOSED)
    ))
    shapes = [t.shape for t in small_w]
    (g1, g2, gf, gb, gsc, gmix), loss = _unpack_small(g_s, shapes)
    (d1, d2, df, db_, dsc, dmx), _ = _unpack_small(d_s, shapes)
    (m1, m2, mf, mb, msc, mmx), _ = _unpack_small(nm_s, shapes)
    (v1, v2, vf, vb, vsc, vmx), _ = _unpack_small(nv_s, shapes)

    def ordered(n1, win, b, mix, sc, wpo, wao, wout, n2, wg, wu, wd, nf):
        return (n1, win, b, mix, sc, wpo, wao, wout, n2, wg, wu, wd, nf)

    grads = ordered(g1, g_w[0], gb, gmix, gsc, g_w[1], g_w[2], g_w[3], g2, g_w[4], g_w[5], g_w[6], gf)
    deltas = ordered(d1, d_w[0], db_, dmx, dsc, d_w[1], d_w[2], d_w[3], d2, d_w[4], d_w[5], d_w[6], df)
    new_m = ordered(m1, nm_w[0], mb, mmx, msc, nm_w[1], nm_w[2], nm_w[3], m2, nm_w[4], nm_w[5], nm_w[6], mf)
    new_v = ordered(v1, nv_w[0], vb, vmx, vsc, nv_w[1], nv_w[2], nv_w[3], v2, nv_w[4], nv_w[5], nv_w[6], vf)
    return (loss, grad_x, *grads, *deltas, *new_m, *new_v)
```

```python
import functools

import jax
import jax.numpy as jnp
from jax import lax
from jax.experimental import pallas as pl
from jax.experimental.pallas import tpu as pltpu

F32 = jnp.float32
BF16 = jnp.bfloat16
MESH = pl.DeviceIdType.MESH

D_MODEL = 1024
POOL_WINDOWS = (2, 4, 8, 16)
POOL_WIDTH = 512
GROUP_DIM = 128
ATTN_WIDTH = 512
HEAD_DIM = 64
N_HEADS = 8
N_PAIRS = 4
D_FF = 2816
RMS_EPS = 1e-6
N_DEV = 8
LANES = 128
FL_PAD = 128

ADAM_LR = 0.001
ADAM_B1 = 0.9
ADAM_B2 = 0.999
ADAM_EPS = 1e-08
ADAM_WD = 0.01
ADAM_STEP = 10

VMEM_LIMIT = 56 * 1024 * 1024
VMEM_LIMIT_MAX = 60 * 1024 * 1024
ROW_TILE = 512
ATTN_BLOCK = 512
FF_CHUNK = 256
FF_ROW_TILE = 512
DW_TOKENS = 2048


def _mm(a, b):
    return jnp.dot(a, b, preferred_element_type=F32)


def _mm_nt(a, b):
    return lax.dot_general(a, b, (((1,), (1,)), ((), ())), preferred_element_type=F32)


def _mm_tn(a, b):
    return lax.dot_general(a, b, (((0,), (0,)), ((), ())), preferred_element_type=F32)


def _sigmoid(x):
    return 1.0 / (1.0 + jnp.exp(-x))


def _params(sem, vmem=VMEM_LIMIT):
    return pltpu.CompilerParams(dimension_semantics=sem, vmem_limit_bytes=vmem)


def _const_spec(shape):
    nd = len(shape)
    return pl.BlockSpec(shape, lambda *_: (0,) * nd, pipeline_mode=pl.Buffered(1))


def _rms_fwd(x, g):
    r = lax.rsqrt(jnp.mean(x * x, axis=-1, keepdims=True) + RMS_EPS)
    xh = x * r
    return xh * g, xh, r


def _rms_bwd(dy, xh, r, g):
    dxh = dy * g
    dx = r * (dxh - xh * jnp.mean(dxh * xh, axis=-1, keepdims=True))
    return dx, dy * xh


def _in_proj(x, g1, w_uqkv, w_fl, w_g, token):
    T = x.shape[0]
    tm = ROW_TILE

    def body(x_ref, g_ref, wa_ref, wf_ref, wg_ref, token_ref, h_ref, u_ref, qkv_ref, fl_ref, gt_ref):
        h, _, _ = _rms_fwd(x_ref[...], g_ref[...])
        hb = h.astype(BF16)
        h_ref[...] = hb
        z = _mm(hb, wa_ref[...])
        u_ref[...] = z[:, :POOL_WIDTH]
        qkv_ref[...] = z[:, POOL_WIDTH:].astype(BF16)
        fl_ref[...] = _mm(hb, wf_ref[...])
        gt_ref[...] = _mm(hb, wg_ref[...]).astype(BF16)

    row = lambda n: pl.BlockSpec((tm, n), lambda i: (i, 0))
    return pl.pallas_call(
        body,
        name="in_proj",
        grid=(T // tm,),
        in_specs=[row(D_MODEL), _const_spec((1, D_MODEL)), _const_spec(w_uqkv.shape), _const_spec(w_fl.shape), _const_spec(w_g.shape), _HBM],
        out_specs=[row(D_MODEL), row(POOL_WIDTH), row(3 * ATTN_WIDTH), row(FL_PAD), row(2 * D_MODEL)],
        out_shape=[
            jax.ShapeDtypeStruct((T, D_MODEL), BF16),
            jax.ShapeDtypeStruct((T, POOL_WIDTH), F32),
            jax.ShapeDtypeStruct((T, 3 * ATTN_WIDTH), BF16),
            jax.ShapeDtypeStruct((T, FL_PAD), F32),
            jax.ShapeDtypeStruct((T, 2 * D_MODEL), BF16),
        ],
        compiler_params=_params(("parallel",)),
    )(x, g1, w_uqkv, w_fl, w_g, token)


def _log_sigmoid(x):
    return jnp.minimum(x, 0.0) - jnp.log(1.0 + jnp.exp(-jnp.abs(x)))


def _forget_fwd(fl, b_pad, n_seq, S):
    def body(fl_ref, b_ref, fcol_ref):
        lf = _log_sigmoid(fl_ref[...] + b_ref[...])
        t = lf.T
        lane = lax.broadcasted_iota(jnp.int32, t.shape, 1)
        k = 1
        while k < S:
            t = t + jnp.where(lane >= k, pltpu.roll(t, k, 1), 0.0)
            k *= 2
        fcol_ref[...] = t.T

    return pl.pallas_call(
        body,
        name="forget_fwd",
        grid=(n_seq,),
        in_specs=[pl.BlockSpec((S, FL_PAD), lambda s: (s, 0)), _const_spec((1, FL_PAD))],
        out_specs=pl.BlockSpec((S, FL_PAD), lambda s: (s, 0)),
        out_shape=jax.ShapeDtypeStruct((n_seq * S, FL_PAD), F32),
        compiler_params=_params(("parallel",)),
    )(fl, b_pad)


def _window_pick(g, v2, v4, v8, v16):
    return jnp.where(g == 0, v2, jnp.where(g == 1, v4, jnp.where(g == 2, v8, v16)))


def _pool_fwd(u, mix_b, scale, n_seq, S):
    T = n_seq * S

    def body(u_ref, mix_ref, sc_ref, pm_ref, p2_ref, p3_ref):
        g = pl.program_id(1)
        uu = u_ref[...]
        row = lax.broadcasted_iota(jnp.int32, uu.shape, 0)

        def back(a, k):
            return jnp.where(row >= k, pltpu.roll(a, k, 0), 0.0)

        s2 = uu + back(uu, 1)
        s4 = s2 + back(s2, 2)
        s8 = s4 + back(s4, 4)
        s16 = s8 + back(s8, 8)
        w = _window_pick(g, 2.0, 4.0, 8.0, 16.0)
        cnt = jnp.minimum((row + 1).astype(F32), w)
        pm = _window_pick(g, s2, s4, s8, s16) / cnt - uu
        pmb = pm.astype(BF16)
        pm_ref[...] = pmb
        p2 = _mm(pmb, mix_ref[...])
        p2_ref[...] = p2
        p3_ref[...] = (p2 * sc_ref[...]).astype(BF16)

    grp = pl.BlockSpec((S, GROUP_DIM), lambda s, g: (s, g))
    return pl.pallas_call(
        body,
        name="pool_fwd",
        grid=(n_seq, len(POOL_WINDOWS)),
        in_specs=[
            grp,
            pl.BlockSpec((None, GROUP_DIM, GROUP_DIM), lambda s, g: (g, 0, 0)),
            pl.BlockSpec((1, GROUP_DIM), lambda s, g: (0, g)),
        ],
        out_specs=[grp, grp, grp],
        out_shape=[
            jax.ShapeDtypeStruct((T, POOL_WIDTH), BF16),
            jax.ShapeDtypeStruct((T, POOL_WIDTH), F32),
            jax.ShapeDtypeStruct((T, POOL_WIDTH), BF16),
        ],
        compiler_params=_params(("parallel", "parallel")),
    )(u, mix_b, scale)


def _split3(v):
    hi = v.astype(BF16).astype(F32)
    r = v - hi
    mid = r.astype(BF16).astype(F32)
    lo = (r - mid).astype(BF16).astype(F32)
    return hi, mid, lo


def _augment(xp, hh, first, second):
    lane = lax.broadcasted_iota(jnp.int32, (1, LANES), 1)
    head = (lane >= HEAD_DIM * hh) & (lane < HEAD_DIM * (hh + 1))
    b = HEAD_DIM * (1 - hh)
    out = jnp.where(head, xp.astype(F32), 0.0)
    for n, col in enumerate(tuple(first) + tuple(second)):
        out = jnp.where(lane == b + n, col, out)
    return out.astype(BF16)


def _attn_fwd(qkv, fcol, n_seq, S):
    T = n_seq * S
    tb = ATTN_BLOCK
    nq = S // tb
    scale = HEAD_DIM ** -0.5

    def body(q_ref, k_ref, v_ref, fc_ref, o_ref, st_ref, qa_sc, ka_sc, m_sc, l_sc, acc_sc):
        i = pl.program_id(1)
        lane = lax.broadcasted_iota(jnp.int32, (1, LANES), 1)
        low = lane < HEAD_DIM
        ones = (1.0, 1.0, 1.0)

        @pl.when(i == 0)
        def _():
            def rows_ka(r, carry):
                r0 = pl.multiple_of(r * tb, tb)
                for h in range(N_HEADS):
                    kp = k_ref[pl.ds(r0, tb), (h // 2) * LANES : (h // 2 + 1) * LANES] * scale
                    fk = fc_ref[pl.ds(r0, tb), h : h + 1]
                    ka_sc[h, pl.ds(r0, tb), :] = _augment(kp, h % 2, ones, _split3(-fk))
                return carry

            lax.fori_loop(0, nq, rows_ka, 0)

        q0 = pl.multiple_of(i * tb, tb)
        for h in range(N_HEADS):
            qp = q_ref[:, (h // 2) * LANES : (h // 2 + 1) * LANES]
            qa_sc[h] = _augment(qp, h % 2, _split3(fc_ref[pl.ds(q0, tb), h : h + 1]), ones)
        m_sc[...] = jnp.full(m_sc.shape, -jnp.inf, F32)
        l_sc[...] = jnp.zeros_like(l_sc)
        acc_sc[...] = jnp.zeros_like(acc_sc)
        causal = lax.broadcasted_iota(jnp.int32, (tb, tb), 1) <= lax.broadcasted_iota(jnp.int32, (tb, tb), 0)

        def step(j, masked):
            c0 = pl.multiple_of(j * tb, tb)
            for p in range(N_PAIRS):
                vb = v_ref[pl.ds(c0, tb), p * LANES : (p + 1) * LANES]
                pv, al = [], []
                for hh in range(2):
                    h = 2 * p + hh
                    s = _mm_nt(qa_sc[h], ka_sc[h, pl.ds(c0, tb), :])
                    if masked:
                        s = jnp.where(causal, s, -jnp.inf)
                    m_old = m_sc[h]
                    m_new = jnp.maximum(m_old, jnp.max(s, axis=1, keepdims=True))
                    alpha = jnp.exp(m_old - m_new)
                    pe = jnp.exp(s - jnp.concatenate([m_new] * (tb // LANES), axis=1))
                    l_sc[h] = alpha * l_sc[h] + jnp.sum(pe, axis=1, keepdims=True)
                    m_sc[h] = m_new
                    pv.append(_mm(pe.astype(BF16), vb))
                    al.append(alpha)
                acc_sc[p] = jnp.where(low, al[0], al[1]) * acc_sc[p] + jnp.where(low, pv[0], pv[1])

        def loop_body(j, carry):
            step(j, False)
            return carry

        lax.fori_loop(0, i, loop_body, 0)
        step(i, True)
        st = jnp.zeros((tb, LANES), F32)
        for p in range(N_PAIRS):
            lp = jnp.where(low, l_sc[2 * p], l_sc[2 * p + 1])
            o_ref[:, p * LANES : (p + 1) * LANES] = (acc_sc[p] / lp).astype(BF16)
            for h in (2 * p, 2 * p + 1):
                st = jnp.where(lane == h, m_sc[h] + jnp.log(l_sc[h]), st)
        st_ref[...] = st

    return pl.pallas_call(
        body,
        name="attn_fwd",
        grid=(n_seq, nq),
        in_specs=[
            pl.BlockSpec((tb, ATTN_WIDTH), lambda s, i: (s * nq + i, 0)),
            pl.BlockSpec((S, ATTN_WIDTH), lambda s, i: (s, 1)),
            pl.BlockSpec((S, ATTN_WIDTH), lambda s, i: (s, 2)),
            pl.BlockSpec((S, LANES), lambda s, i: (s, 0)),
        ],
        out_specs=[
            pl.BlockSpec((tb, ATTN_WIDTH), lambda s, i: (s * nq + i, 0)),
            pl.BlockSpec((tb, LANES), lambda s, i: (s * nq + i, 0)),
        ],
        out_shape=[jax.ShapeDtypeStruct((T, ATTN_WIDTH), BF16), jax.ShapeDtypeStruct((T, LANES), F32)],
        scratch_shapes=[
            pltpu.VMEM((N_HEADS, tb, LANES), BF16),
            pltpu.VMEM((N_HEADS, S, LANES), BF16),
            pltpu.VMEM((N_HEADS, tb, LANES), F32),
            pltpu.VMEM((N_HEADS, tb, LANES), F32),
            pltpu.VMEM((N_PAIRS, tb, LANES), F32),
        ],
        compiler_params=_params(("parallel", "arbitrary")),
    )(qkv, qkv, qkv, fcol)


def _mix_out(a, p3, gates, x, w_ao, w_po, w_out):
    T = x.shape[0]
    tm = ROW_TILE

    def body(a_ref, p3_ref, gt_ref, x_ref, wao_ref, wpo_ref, wout_ref, mg_ref, x1_ref, ay_ref, py_ref):
        ay = _mm(a_ref[...], wao_ref[...])
        py = _mm(p3_ref[...], wpo_ref[...])
        ay_ref[...] = ay.astype(BF16)
        py_ref[...] = py.astype(BF16)
        sp = _sigmoid(gt_ref[:, :D_MODEL].astype(F32))
        sa = _sigmoid(gt_ref[:, D_MODEL:].astype(F32))
        mb = (sp * py + sa * ay).astype(BF16)
        mg_ref[...] = mb
        x1_ref[...] = x_ref[...] + _mm(mb, wout_ref[...])

    row = lambda n: pl.BlockSpec((tm, n), lambda i: (i, 0))
    return pl.pallas_call(
        body,
        name="mix_out",
        grid=(T // tm,),
        in_specs=[
            row(ATTN_WIDTH), row(POOL_WIDTH), row(2 * D_MODEL), row(D_MODEL),
            _const_spec(w_ao.shape), _const_spec(w_po.shape), _const_spec(w_out.shape),
        ],
        out_specs=[row(D_MODEL), row(D_MODEL), row(D_MODEL), row(D_MODEL)],
        out_shape=[
            jax.ShapeDtypeStruct((T, D_MODEL), BF16), jax.ShapeDtypeStruct((T, D_MODEL), F32),
            jax.ShapeDtypeStruct((T, D_MODEL), BF16), jax.ShapeDtypeStruct((T, D_MODEL), BF16),
        ],
        compiler_params=_params(("parallel",)),
    )(a, p3, gates, x, w_ao, w_po, w_out)


def _ffn_fwd(x1, g2, gf, tgt, w_gate_t, w_up_t, w_down):
    T = x1.shape[0]
    tm = min(T, FF_ROW_TILE)
    nt = T // tm
    nc = D_FF // FF_CHUNK

    def body(x1_ref, g2_ref, gf_ref, tg_ref, wg_ref, wu_ref, wd_ref, h2_ref, gate_ref, up_ref, act_ref, dx2_ref, loss_ref, dgf_ref):
        x1v = x1_ref[...]
        h2, _, _ = _rms_fwd(x1v, g2_ref[...])
        h2b = h2.astype(BF16)
        h2_ref[...] = h2b
        for c in range(nc):
            sl = slice(c * FF_CHUNK, (c + 1) * FF_CHUNK)
            gate = _mm_nt(h2b, wg_ref[sl, :])
            up = _mm_nt(h2b, wu_ref[sl, :])
            gate_ref[:, sl] = gate.astype(BF16)
            up_ref[:, sl] = up.astype(BF16)
            act_ref[:, sl] = (gate * _sigmoid(gate) * up).astype(BF16)
        acc = x1v + _mm(act_ref[...], wd_ref[...])
        gfv = gf_ref[...]
        y, xh, r = _rms_fwd(acc, gfv)
        err = y - tg_ref[...]
        part = 0.5 * jnp.sum(jnp.mean(err * err, axis=-1, keepdims=True), axis=0, keepdims=True)
        dx2, dgrow = _rms_bwd(err * (1.0 / D_MODEL), xh, r, gfv)
        dx2_ref[...] = dx2

        @pl.when(pl.program_id(0) == 0)
        def _():
            dgf_ref[...] = jnp.zeros_like(dgf_ref)
            loss_ref[...] = jnp.zeros_like(loss_ref)

        dgf_ref[...] += jnp.sum(dgrow, axis=0, keepdims=True)
        loss_ref[...] += jnp.broadcast_to(part, loss_ref.shape)

    row = lambda n: pl.BlockSpec((tm, n), lambda i: (i, 0))
    return pl.pallas_call(
        body,
        name="ffn_fwd",
        grid=(nt,),
        in_specs=[
            row(D_MODEL), _const_spec((1, D_MODEL)), _const_spec((1, D_MODEL)), row(D_MODEL),
            _const_spec(w_gate_t.shape), _const_spec(w_up_t.shape), _const_spec(w_down.shape),
        ],
        out_specs=[
            row(D_MODEL), row(D_FF), row(D_FF), row(D_FF), row(D_MODEL),
            pl.BlockSpec((8, LANES), lambda i: (0, 0)),
            pl.BlockSpec((1, D_MODEL), lambda i: (0, 0)),
        ],
        out_shape=[
            jax.ShapeDtypeStruct((T, D_MODEL), BF16),
            jax.ShapeDtypeStruct((T, D_FF), BF16),
            jax.ShapeDtypeStruct((T, D_FF), BF16),
            jax.ShapeDtypeStruct((T, D_FF), BF16),
            jax.ShapeDtypeStruct((T, D_MODEL), F32),
            jax.ShapeDtypeStruct((8, LANES), F32),
            jax.ShapeDtypeStruct((1, D_MODEL), F32),
        ],
        compiler_params=_params(("arbitrary",)),
    )(x1, g2, gf, tgt, w_gate_t, w_up_t, w_down)


def _ffn_bwd(dx2, gate, up, x1, g2, w_gate_t, w_up_t, w_down):
    T = x1.shape[0]
    tm = min(T, FF_ROW_TILE)
    nc = D_FF // FF_CHUNK

    def body(dx2_ref, gate_ref, up_ref, x1_ref, g2_ref, wg_ref, wu_ref, wd_ref, dgate_ref, dup_ref, dx1_ref, dg2_ref):
        dx2v = dx2_ref[...]
        dx2b = dx2v.astype(BF16)
        for c in range(nc):
            sl = slice(c * FF_CHUNK, (c + 1) * FF_CHUNK)
            dact = _mm_nt(dx2b, wd_ref[sl, :])
            gate = gate_ref[:, sl].astype(F32)
            sg = _sigmoid(gate)
            silu = gate * sg
            dgate = (dact * up_ref[:, sl].astype(F32) * (sg * (1.0 + gate * (1.0 - sg)))).astype(BF16)
            dup = (dact * silu).astype(BF16)
            dgate_ref[:, sl] = dgate
            dup_ref[:, sl] = dup
        dh2 = _mm(dgate_ref[...], wg_ref[...]) + _mm(dup_ref[...], wu_ref[...])
        g2v = g2_ref[...]
        _, xh, r = _rms_fwd(x1_ref[...], g2v)
        dxn, dgrow = _rms_bwd(dh2, xh, r, g2v)
        dx1_ref[...] = dx2v + dxn

        @pl.when(pl.program_id(0) == 0)
        def _():
            dg2_ref[...] = jnp.zeros_like(dg2_ref)

        dg2_ref[...] += jnp.sum(dgrow, axis=0, keepdims=True)

    row = lambda n: pl.BlockSpec((tm, n), lambda i: (i, 0))
    return pl.pallas_call(
        body,
        name="ffn_bwd",
        grid=(T // tm,),
        in_specs=[
            row(D_MODEL), row(D_FF), row(D_FF), row(D_MODEL), _const_spec((1, D_MODEL)),
            _const_spec(w_gate_t.shape), _const_spec(w_up_t.shape), _const_spec(w_down.shape),
        ],
        out_specs=[row(D_FF), row(D_FF), row(D_MODEL), pl.BlockSpec((1, D_MODEL), lambda i: (0, 0))],
        out_shape=[
            jax.ShapeDtypeStruct((T, D_FF), BF16),
            jax.ShapeDtypeStruct((T, D_FF), BF16),
            jax.ShapeDtypeStruct((T, D_MODEL), F32),
            jax.ShapeDtypeStruct((1, D_MODEL), F32),
        ],
        compiler_params=_params(("arbitrary",), VMEM_LIMIT_MAX),
    )(dx2, gate, up, x1, g2, w_gate_t, w_up_t, w_down)


def _mix_bwd(dx1, gates, pool_y, attn_y, p2, scale, w_out, w_ao, w_po, token):
    T = dx1.shape[0]
    tm = ROW_TILE

    def body(dx1_ref, gt_ref, py_ref, ay_ref, p2_ref, sc_ref, wout_ref, wao_ref, wpo_ref, token_ref, dgt_ref, dpy_ref, day_ref, da_ref, dp2_ref, dsc_ref):
        dm = _mm_nt(dx1_ref[...].astype(BF16), wout_ref[...])
        sp = _sigmoid(gt_ref[:, :D_MODEL].astype(F32))
        sa = _sigmoid(gt_ref[:, D_MODEL:].astype(F32))
        dgt_ref[:, :D_MODEL] = (dm * py_ref[...].astype(F32) * (sp * (1.0 - sp))).astype(BF16)
        dgt_ref[:, D_MODEL:] = (dm * ay_ref[...].astype(F32) * (sa * (1.0 - sa))).astype(BF16)
        dpy = (dm * sp).astype(BF16)
        day = (dm * sa).astype(BF16)
        dpy_ref[...] = dpy
        day_ref[...] = day
        da_ref[...] = _mm_nt(day, wao_ref[...]).astype(BF16)
        dp3 = _mm_nt(dpy, wpo_ref[...])
        dp2_ref[...] = (dp3 * sc_ref[...]).astype(BF16)

        @pl.when(pl.program_id(0) == 0)
        def _():
            dsc_ref[...] = jnp.zeros_like(dsc_ref)

        dsc_ref[...] += jnp.sum(dp3 * p2_ref[...], axis=0, keepdims=True)

    row = lambda n: pl.BlockSpec((tm, n), lambda i: (i, 0))
    return pl.pallas_call(
        body,
        name="mix_bwd",
        grid=(T // tm,),
        in_specs=[
            row(D_MODEL), row(2 * D_MODEL), row(D_MODEL), row(D_MODEL), row(POOL_WIDTH), _const_spec((1, POOL_WIDTH)),
            _const_spec(w_out.shape), _const_spec(w_ao.shape), _const_spec(w_po.shape), _HBM,
        ],
        out_specs=[row(2 * D_MODEL), row(D_MODEL), row(D_MODEL), row(ATTN_WIDTH), row(POOL_WIDTH), pl.BlockSpec((1, POOL_WIDTH), lambda i: (0, 0))],
        out_shape=[
            jax.ShapeDtypeStruct((T, 2 * D_MODEL), BF16),
            jax.ShapeDtypeStruct((T, D_MODEL), BF16),
            jax.ShapeDtypeStruct((T, D_MODEL), BF16),
            jax.ShapeDtypeStruct((T, ATTN_WIDTH), BF16),
            jax.ShapeDtypeStruct((T, POOL_WIDTH), BF16),
            jax.ShapeDtypeStruct((1, POOL_WIDTH), F32),
        ],
        compiler_params=_params(("arbitrary",)),
    )(dx1, gates, pool_y, attn_y, p2, scale, w_out, w_ao, w_po, token)


def _pool_bwd(dp2, pm, mix_b, token, n_seq, S):
    T = n_seq * S

    def body(dp2_ref, pm_ref, mix_ref, token_ref, du_ref, dmix_ref):
        g = pl.program_id(0)
        dp2v = dp2_ref[...]
        dpm = _mm_nt(dp2v, mix_ref[...])
        row = lax.broadcasted_iota(jnp.int32, dpm.shape, 0)
        w = _window_pick(g, 2.0, 4.0, 8.0, 16.0)
        e = dpm / jnp.minimum((row + 1).astype(F32), w)

        def ahead(a, k):
            return jnp.where(row < S - k, pltpu.roll(a, S - k, 0), 0.0)

        r2 = e + ahead(e, 1)
        r4 = r2 + ahead(r2, 2)
        r8 = r4 + ahead(r4, 4)
        r16 = r8 + ahead(r8, 8)
        du_ref[...] = (_window_pick(g, r2, r4, r8, r16) - dpm).astype(BF16)

        @pl.when(pl.program_id(1) == 0)
        def _():
            dmix_ref[...] = jnp.zeros_like(dmix_ref)

        dmix_ref[...] += _mm_tn(pm_ref[...], dp2v)

    grp = pl.BlockSpec((S, GROUP_DIM), lambda g, s: (s, g))
    mixs = pl.BlockSpec((None, GROUP_DIM, GROUP_DIM), lambda g, s: (g, 0, 0))
    return pl.pallas_call(
        body,
        name="pool_bwd",
        grid=(len(POOL_WINDOWS), n_seq),
        in_specs=[grp, grp, mixs, _HBM],
        out_specs=[grp, mixs],
        out_shape=[jax.ShapeDtypeStruct((T, POOL_WIDTH), BF16), jax.ShapeDtypeStruct((len(POOL_WINDOWS), GROUP_DIM, GROUP_DIM), F32)],
        compiler_params=_params(("parallel", "arbitrary")),
    )(dp2, pm, mix_b, token)


def _attn_bwd(qkv, da, a, fcol, lse, n_seq, S):
    T = n_seq * S
    tb = ATTN_BLOCK
    nb = S // tb
    scale = HEAD_DIM ** -0.5

    def body(q_ref, k_ref, v_ref, do_ref, o_ref, fc_ref, st_ref, dq_ref, dk_ref, dv_ref, dfk_ref, dfq_ref,
             qa_sc, doa_sc, dq_acc, ka_sc, va_sc, dk_sc, dv_sc):
        j = pl.program_id(1)
        lane = lax.broadcasted_iota(jnp.int32, (1, LANES), 1)
        low = lane < HEAD_DIM
        ones = (1.0, 1.0, 1.0)
        zeros = (0.0, 0.0, 0.0)

        @pl.when(j == 0)
        def _():
            dq_acc[...] = jnp.zeros_like(dq_acc)

            def rows_q(i, carry):
                r0 = pl.multiple_of(i * tb, tb)
                for h in range(N_HEADS):
                    pair = slice((h // 2) * LANES, (h // 2 + 1) * LANES)
                    qp = q_ref[pl.ds(r0, tb), pair]
                    dop = do_ref[pl.ds(r0, tb), pair]
                    prod = dop.astype(F32) * o_ref[pl.ds(r0, tb), pair].astype(F32)
                    head = (lane >= HEAD_DIM * (h % 2)) & (lane < HEAD_DIM * (h % 2 + 1))
                    delta = jnp.sum(jnp.where(head, prod, 0.0), axis=1, keepdims=True)
                    cq = fc_ref[pl.ds(r0, tb), h : h + 1] - st_ref[pl.ds(r0, tb), h : h + 1]
                    qa_sc[h, pl.ds(r0, tb), :] = _augment(qp, h % 2, _split3(cq), ones)
                    doa_sc[h, pl.ds(r0, tb), :] = _augment(dop, h % 2, _split3(-delta), zeros)
                return carry

            lax.fori_loop(0, nb, rows_q, 0)

        c0 = pl.multiple_of(j * tb, tb)
        for h in range(N_HEADS):
            pair = slice((h // 2) * LANES, (h // 2 + 1) * LANES)
            kp = k_ref[:, pair] * scale
            ka_sc[h] = _augment(kp, h % 2, ones, _split3(-fc_ref[pl.ds(c0, tb), h : h + 1]))
            va_sc[h] = _augment(v_ref[:, pair], h % 2, ones, zeros)
        dk_sc[...] = jnp.zeros_like(dk_sc)
        dv_sc[...] = jnp.zeros_like(dv_sc)
        causal = lax.broadcasted_iota(jnp.int32, (tb, tb), 1) <= lax.broadcasted_iota(jnp.int32, (tb, tb), 0)

        def step(i, masked):
            r0 = pl.multiple_of(i * tb, tb)
            for h in range(N_HEADS):
                dob = do_ref[pl.ds(r0, tb), (h // 2) * LANES : (h // 2 + 1) * LANES]
                qa = qa_sc[h, pl.ds(r0, tb), :]
                s = _mm_nt(qa, ka_sc[h])
                if masked:
                    s = jnp.where(causal, s, -jnp.inf)
                pr = jnp.exp(s)
                dv_sc[h] += _mm_tn(pr.astype(BF16), dob)
                dsb = (pr * _mm_nt(doa_sc[h, pl.ds(r0, tb), :], va_sc[h])).astype(BF16)
                dk_sc[h] += _mm_tn(dsb, qa)
                dq_acc[h, pl.ds(r0, tb), :] += _mm(dsb, ka_sc[h])

        step(j, True)

        def loop_body(i, carry):
            step(i, False)
            return carry

        lax.fori_loop(j + 1, nb, loop_body, 0)
        dfk = jnp.zeros((tb, LANES), F32)
        for p in range(N_PAIRS):
            dk_ref[:, p * LANES : (p + 1) * LANES] = (jnp.where(low, dk_sc[2 * p], dk_sc[2 * p + 1]) * scale).astype(BF16)
            dv_ref[:, p * LANES : (p + 1) * LANES] = jnp.where(low, dv_sc[2 * p], dv_sc[2 * p + 1]).astype(BF16)
            for hh in range(2):
                b = HEAD_DIM * (1 - hh) + 3
                dfk = jnp.where(lane == 2 * p + hh, -dk_sc[2 * p + hh][:, b : b + 1], dfk)
        dfk_ref[...] = dfk

        @pl.when(j == nb - 1)
        def _():
            def rows_dq(i, carry):
                r0 = pl.multiple_of(i * tb, tb)
                dfq = jnp.zeros((tb, LANES), F32)
                for p in range(N_PAIRS):
                    parts = [dq_acc[2 * p + hh, pl.ds(r0, tb), :] for hh in range(2)]
                    dq_ref[pl.ds(r0, tb), p * LANES : (p + 1) * LANES] = jnp.where(low, parts[0], parts[1]).astype(BF16)
                    for hh in range(2):
                        b = HEAD_DIM * (1 - hh)
                        dfq = jnp.where(lane == 2 * p + hh, parts[hh][:, b : b + 1], dfq)
                dfq_ref[pl.ds(r0, tb), :] = dfq
                return carry

            lax.fori_loop(0, nb, rows_dq, 0)

    seq = lambda w, col: pl.BlockSpec((S, w), lambda s, j: (s, col))
    blk = lambda w, col: pl.BlockSpec((tb, w), lambda s, j: (s * nb + j, col))
    return pl.pallas_call(
        body,
        name="attn_bwd",
        grid=(n_seq, nb),
        in_specs=[seq(ATTN_WIDTH, 0), blk(ATTN_WIDTH, 1), blk(ATTN_WIDTH, 2), seq(ATTN_WIDTH, 0), seq(ATTN_WIDTH, 0), seq(LANES, 0), seq(LANES, 0)],
        out_specs=[seq(ATTN_WIDTH, 0), blk(ATTN_WIDTH, 0), blk(ATTN_WIDTH, 0), blk(LANES, 0), seq(LANES, 0)],
        out_shape=[
            jax.ShapeDtypeStruct((T, ATTN_WIDTH), BF16),
            jax.ShapeDtypeStruct((T, ATTN_WIDTH), BF16),
            jax.ShapeDtypeStruct((T, ATTN_WIDTH), BF16),
            jax.ShapeDtypeStruct((T, LANES), F32),
            jax.ShapeDtypeStruct((T, LANES), F32),
        ],
        scratch_shapes=[
            pltpu.VMEM((N_HEADS, S, LANES), BF16),
            pltpu.VMEM((N_HEADS, S, LANES), BF16),
            pltpu.VMEM((N_HEADS, S, LANES), F32),
            pltpu.VMEM((N_HEADS, tb, LANES), BF16),
            pltpu.VMEM((N_HEADS, tb, LANES), BF16),
            pltpu.VMEM((N_HEADS, tb, LANES), F32),
            pltpu.VMEM((N_HEADS, tb, LANES), F32),
        ],
        compiler_params=_params(("parallel", "arbitrary")),
    )(qkv, qkv, qkv, da, a, fcol, lse)


def _forget_bwd(dfk, dfq, fl, b_pad, n_seq, S):
    def body(df_ref, dfq_ref, fl_ref, b_ref, dfl_ref, db_ref):
        t = (df_ref[...] + dfq_ref[...]).T
        lane = lax.broadcasted_iota(jnp.int32, t.shape, 1)
        k = 1
        while k < S:
            t = t + jnp.where(lane < S - k, pltpu.roll(t, S - k, 1), 0.0)
            k *= 2
        dfl = t.T * _sigmoid(-(fl_ref[...] + b_ref[...]))
        dfl_ref[...] = dfl.astype(BF16)

        @pl.when(pl.program_id(0) == 0)
        def _():
            db_ref[...] = jnp.zeros_like(db_ref)

        db_ref[...] += jnp.sum(dfl, axis=0, keepdims=True)

    return pl.pallas_call(
        body,
        name="forget_bwd",
        grid=(n_seq,),
        in_specs=[
            pl.BlockSpec((S, LANES), lambda s: (s, 0)),
            pl.BlockSpec((S, LANES), lambda s: (s, 0)),
            pl.BlockSpec((S, FL_PAD), lambda s: (s, 0)),
            _const_spec((1, FL_PAD)),
        ],
        out_specs=[pl.BlockSpec((S, FL_PAD), lambda s: (s, 0)), pl.BlockSpec((1, FL_PAD), lambda s: (0, 0))],
        out_shape=[jax.ShapeDtypeStruct((n_seq * S, FL_PAD), BF16), jax.ShapeDtypeStruct((1, FL_PAD), F32)],
        compiler_params=_params(("arbitrary",)),
    )(dfk, dfq, fl, b_pad)


def _in_proj_bwd(du, dq, dk, dv, dfl, dgates, x, dx1, g1, w_uqkv, w_fl, w_g):
    T = x.shape[0]
    tm = ROW_TILE

    def body(du_ref, dq_ref, dk_ref, dv_ref, dfl_ref, dgt_ref, x_ref, dx1_ref, g_ref, wa_ref, wf_ref, wg_ref, dx_ref, dg_ref):
        dz = jnp.concatenate([du_ref[...], dq_ref[...], dk_ref[...], dv_ref[...]], axis=1)
        dh = _mm_nt(dz, wa_ref[...]) + _mm_nt(dgt_ref[...], wg_ref[...]) + _mm_nt(dfl_ref[...], wf_ref[...])
        gv = g_ref[...]
        _, xh, r = _rms_fwd(x_ref[...], gv)
        dxn, dgrow = _rms_bwd(dh, xh, r, gv)
        dx_ref[...] = dx1_ref[...] + dxn

        @pl.when(pl.program_id(0) == 0)
        def _():
            dg_ref[...] = jnp.zeros_like(dg_ref)

        dg_ref[...] += jnp.sum(dgrow, axis=0, keepdims=True)

    row = lambda n: pl.BlockSpec((tm, n), lambda i: (i, 0))
    return pl.pallas_call(
        body,
        name="in_proj_bwd",
        grid=(T // tm,),
        in_specs=[
            row(512), row(512), row(512), row(512), row(FL_PAD), row(2 * D_MODEL), row(D_MODEL), row(D_MODEL), _const_spec((1, D_MODEL)),
            _const_spec(w_uqkv.shape), _const_spec(w_fl.shape), _const_spec(w_g.shape),
        ],
        out_specs=[row(D_MODEL), pl.BlockSpec((1, D_MODEL), lambda i: (0, 0))],
        out_shape=[jax.ShapeDtypeStruct((T, D_MODEL), F32), jax.ShapeDtypeStruct((1, D_MODEL), F32)],
        compiler_params=_params(("arbitrary",)),
    )(du, dq, dk, dv, dfl, dgates, x, dx1, g1, w_uqkv, w_fl, w_g)


def _pick_block(n):
    for b in (512, 1408, 256, 128):
        if n % b == 0:
            return b
    raise ValueError(n)


def _matmul_tn(a, b, name):
    T, K = a.shape
    N = b.shape[1]
    bt, bk, bn = min(T, DW_TOKENS), _pick_block(K), _pick_block(N)
    nt = T // bt

    def body(a_ref, b_ref, o_ref, acc):
        @pl.when(pl.program_id(2) == 0)
        def _():
            acc[...] = jnp.zeros_like(acc)

        acc[...] += _mm_tn(a_ref[...].astype(BF16), b_ref[...].astype(BF16))

        @pl.when(pl.program_id(2) == nt - 1)
        def _():
            o_ref[...] = acc[...].astype(BF16)

    return pl.pallas_call(
        body,
        name=name,
        grid=(K // bk, N // bn, nt),
        in_specs=[pl.BlockSpec((bt, bk), lambda k, n, t: (t, k)), pl.BlockSpec((bt, bn), lambda k, n, t: (t, n))],
        out_specs=pl.BlockSpec((bk, bn), lambda k, n, t: (k, n)),
        out_shape=jax.ShapeDtypeStruct((K, N), BF16),
        scratch_shapes=[pltpu.VMEM((bk, bn), F32)],
        compiler_params=_params(("parallel", "parallel", "arbitrary")),
    )(a, b)


W_IN_A = POOL_WIDTH + 3 * ATTN_WIDTH
W_IN_SHARD = (W_IN_A + N_HEADS + 2 * D_MODEL) // N_DEV
_W_IN_PIECES = ((0, W_IN_A), (W_IN_A, W_IN_A + N_HEADS), (W_IN_A + N_HEADS, W_IN_A + N_HEADS + 2 * D_MODEL))


def _w_in_segments(d):
    lo, hi = d * W_IN_SHARD, (d + 1) * W_IN_SHARD
    out = []
    for p, (a, b) in enumerate(_W_IN_PIECES):
        s, e = max(lo, a), min(hi, b)
        if s < e:
            out.append((p, s - a, s - lo, e - s))
    return out


def _w_in_pieces(gathered):
    tm = ROW_TILE // 2

    def body(g_ref, wa_ref, wf_ref, wg_ref):
        outs = (wa_ref, wf_ref, wg_ref)
        wf_ref[...] = jnp.zeros_like(wf_ref)
        for d in range(N_DEV):
            for p, at, frm, n in _w_in_segments(d):
                outs[p][:, at : at + n] = g_ref[d, :, frm : frm + n]

    return pl.pallas_call(
        body,
        name="w_in_pieces",
        grid=(D_MODEL // tm,),
        in_specs=[pl.BlockSpec((N_DEV, tm, W_IN_SHARD), lambda i: (0, i, 0))],
        out_specs=[pl.BlockSpec((tm, W_IN_A), lambda i: (i, 0)), pl.BlockSpec((tm, FL_PAD), lambda i: (i, 0)), pl.BlockSpec((tm, 2 * D_MODEL), lambda i: (i, 0))],
        out_shape=[
            jax.ShapeDtypeStruct((D_MODEL, W_IN_A), gathered.dtype),
            jax.ShapeDtypeStruct((D_MODEL, FL_PAD), gathered.dtype),
            jax.ShapeDtypeStruct((D_MODEL, 2 * D_MODEL), gathered.dtype),
        ],
        compiler_params=_params(("parallel",)),
    )(gathered)


def _dw_in(h, du, dq, dk, dv, dfl, dgates, token):
    T = h.shape[0]
    bt, bk = min(T, DW_TOKENS // 2), 512
    nt = T // bt
    pieces = (du, dq, dk, dv, dfl, dgates)
    offs = [0]
    for p in pieces:
        offs.append(offs[-1] + p.shape[1])

    def body(h_ref, *rest):
        refs, o_ref, acc = rest[: len(pieces)], rest[-2], rest[-1]

        @pl.when(pl.program_id(1) == 0)
        def _():
            acc[...] = jnp.zeros_like(acc)

        ht = h_ref[...].T
        for ref, at in zip(refs, offs):
            acc[:, at : at + ref.shape[1]] += _mm(ht, ref[...])

        @pl.when(pl.program_id(1) == nt - 1)
        def _():
            starts = (0, W_IN_A, W_IN_A + FL_PAD)
            for d in range(N_DEV):
                for p, at, to, n in _w_in_segments(d):
                    o_ref[d % 2, d // 2, :, to : to + n] = acc[:, starts[p] + at : starts[p] + at + n].astype(BF16)

    return pl.pallas_call(
        body,
        name="dw_in",
        grid=(D_MODEL // bk, nt),
        in_specs=[pl.BlockSpec((bt, bk), lambda k, t: (t, k))] + [pl.BlockSpec((bt, p.shape[1]), lambda k, t: (t, 0)) for p in pieces] + [_HBM],
        out_specs=pl.BlockSpec((2, 4, bk, W_IN_SHARD), lambda k, t: (0, 0, k, 0)),
        out_shape=jax.ShapeDtypeStruct((2, 4, D_MODEL, W_IN_SHARD), BF16),
        scratch_shapes=[pltpu.VMEM((bk, offs[-1]), F32)],
        compiler_params=_params(("parallel", "arbitrary")),
    )(h, *pieces, token)


def _position():
    return lax.axis_index("x"), lax.axis_index("y"), lax.axis_index("c")


_HBM = pl.BlockSpec(memory_space=pl.ANY)


def _all_gather(blocks, name):
    n = len(blocks)

    def body(*refs):
        xs, outs = refs[:n], refs[n : 2 * n]
        send_sems, recv_sems, local_sems = refs[2 * n :]
        x, y, c = _position()
        me, sibling = (x, y, c), (x, y, 1 - c)
        chips = [(1 - x, y), (x, 1 - y), (1 - x, 1 - y)]

        def rows(a, px, py, pc):
            return outs[a].at[4 * px + 2 * py + pc]

        def copy(a, k, blk, to, src=None):
            return pltpu.make_async_remote_copy(
                src_ref=rows(a, *blk) if src is None else src, dst_ref=rows(a, *blk),
                send_sem=send_sems.at[7 * a + k], recv_sem=recv_sems.at[7 * a + k], device_id=to, device_id_type=MESH,
            )

        mine = [pltpu.make_async_copy(xs[a], rows(a, *me), local_sems.at[a]) for a in range(n)]
        for cp in mine:
            cp.start()
        first = []
        for a in range(n):
            first.append(copy(a, 0, me, sibling, src=xs[a]))
            first += [copy(a, 1 + j, me, (*chip, c), src=xs[a]) for j, chip in enumerate(chips)]
        for cp in first:
            cp.start()
        passed = []
        for j, chip in enumerate(chips):
            for a in range(n):
                copy(a, 1 + j, (*chip, c), me).wait_recv()
                passed.append(copy(a, 4 + j, (*chip, c), sibling))
                passed[-1].start()
        for a in range(n):
            copy(a, 0, sibling, me).wait_recv()
        for j, chip in enumerate(chips):
            for a in range(n):
                copy(a, 4 + j, (*chip, 1 - c), me).wait_recv()
        for cp in first + passed:
            cp.wait_send()
        for cp in mine:
            cp.wait()

    return pl.pallas_call(
        body,
        name=name,
        out_shape=[jax.ShapeDtypeStruct((N_DEV, *b.shape), b.dtype) for b in blocks],
        in_specs=[_HBM] * n,
        out_specs=[_HBM] * n,
        scratch_shapes=[pltpu.SemaphoreType.DMA((7 * n,)), pltpu.SemaphoreType.DMA((7 * n,)), pltpu.SemaphoreType.DMA((n,))],
    )(*blocks)


_SEM = pl.BlockSpec(memory_space=pltpu.SEMAPHORE)
_HBM_ONLY = pl.BlockSpec(memory_space=pltpu.HBM)
_SIDE_EFFECT = pltpu.SideEffectType.DATAFLOW_SIDE_EFFECTING


def _peer(x, y, c, k):
    return (1 - x if k & 4 else x, 1 - y if k & 2 else y, 1 - c if k & 1 else c)


_PEER_BITS = {"gather": range(1, N_DEV), "scatter": range(1, N_DEV), "chips": (4, 2, 6)}
_LAND_SLOTS = {"gather": N_DEV, "scatter": N_DEV, "chips": 3}


def _exchange_copies(src_refs, land_refs, send_sems, recv_sems, pattern, receive_side):
    x, y, c = _position()
    me = 4 * x + 2 * y + c
    bits = _PEER_BITS[pattern]
    cps = []
    for j, k in enumerate(bits):
        px, py, pc = _peer(x, y, c, k)
        peer = 4 * px + 2 * py + pc
        for a, (src, land) in enumerate(zip(src_refs, land_refs)):
            if pattern == "chips":
                s, slot = src.at[2 * px + py], j
            else:
                s, slot = (src if pattern == "gather" else src.at[peer]), (peer if receive_side else me)
            cps.append(pltpu.make_async_remote_copy(
                src_ref=s, dst_ref=land.at[slot],
                send_sem=send_sems.at[len(bits) * a + j], recv_sem=recv_sems.at[len(bits) * a + j],
                device_id=(px, py, pc), device_id_type=MESH,
            ))
    return cps


def _exchange_start(srcs, after, name, pattern):
    n = len(srcs)
    m = len(_PEER_BITS[pattern])
    lands = [jax.ShapeDtypeStruct((_LAND_SLOTS[pattern], *s.shape[-2:]), s.dtype) for s in srcs]

    def body(*refs):
        src_refs, land_refs = refs[1 : 1 + n], refs[1 + n : 1 + 2 * n]
        send_sems, recv_sems = refs[1 + 2 * n], refs[2 + 2 * n]
        token = refs[-1]
        for cp in _exchange_copies(src_refs, land_refs, send_sems, recv_sems, pattern, receive_side=False):
            cp.start()
        token[...] = jnp.zeros_like(token)

    hbm = lambda t: pltpu.with_memory_space_constraint(t, pltpu.HBM)
    out = pl.pallas_call(
        body,
        name=name,
        out_shape=(
            pltpu.SemaphoreType.DMA((m * n,)), pltpu.SemaphoreType.DMA((m * n,)),
            *[pltpu.HBM(s.shape, s.dtype) for s in srcs], *[pltpu.HBM(l.shape, l.dtype) for l in lands],
            jax.ShapeDtypeStruct((8, LANES), F32),
        ),
        in_specs=(_HBM, *[_HBM_ONLY] * (2 * n)),
        out_specs=(_SEM, _SEM, *[_HBM_ONLY] * (2 * n), pl.BlockSpec(memory_space=pltpu.VMEM)),
        input_output_aliases={1 + i: 2 + i for i in range(2 * n)},
        compiler_params=pltpu.CompilerParams(has_side_effects=_SIDE_EFFECT),
    )(after, *[hbm(s) for s in srcs], *[hbm(lax.empty(l.shape, l.dtype)) for l in lands])
    return out[0], out[1], out[2 : 2 + n], out[2 + n : 2 + 2 * n], out[-1]


def _exchange_wait(send_sems, recv_sems, srcs, lands, after, name, pattern):
    n = len(srcs)

    def body(*refs):
        src_refs, land_refs = refs[:n], refs[n : 2 * n]
        for cp in _exchange_copies(src_refs, land_refs, refs[2 * n], refs[2 * n + 1], pattern, receive_side=True):
            cp.wait_send()
            cp.wait_recv()

    out = pl.pallas_call(
        body,
        name=name,
        out_shape=(*[pltpu.HBM(s.shape, s.dtype) for s in srcs], *[pltpu.HBM(l.shape, l.dtype) for l in lands]),
        in_specs=(*[_HBM_ONLY] * (2 * n), _SEM, _SEM, _HBM),
        out_specs=tuple([_HBM_ONLY] * (2 * n)),
        input_output_aliases={i: i for i in range(2 * n)},
        compiler_params=pltpu.CompilerParams(has_side_effects=_SIDE_EFFECT),
    )(*srcs, *lands, send_sems, recv_sems, after)
    return out[:n], out[n:]


def _sibling_exchange(sends):
    n = len(sends)

    def body(*refs):
        srcs, dsts = refs[:n], refs[n : 2 * n]
        send_sems, recv_sems = refs[2 * n :]
        x, y, c = _position()
        cps = [
            pltpu.make_async_remote_copy(
                src_ref=srcs[a].at[1 - c], dst_ref=dsts[a], send_sem=send_sems.at[a], recv_sem=recv_sems.at[a],
                device_id=(x, y, 1 - c), device_id_type=MESH,
            )
            for a in range(n)
        ]
        for cp in cps:
            cp.start()
        for cp in cps:
            cp.wait()

    return pl.pallas_call(
        body,
        name="rs_sibling",
        out_shape=[jax.ShapeDtypeStruct(s.shape[1:], s.dtype) for s in sends],
        in_specs=[_HBM] * n,
        out_specs=[_HBM] * n,
        scratch_shapes=[pltpu.SemaphoreType.DMA((n,)), pltpu.SemaphoreType.DMA((n,))],
    )(*sends)


def _rows_tile(r):
    return ROW_TILE if r % ROW_TILE == 0 else r


def _pair_sum(send, got, core, name):
    _, _, r, c = send.shape
    br = _rows_tile(r)

    def body(core_ref, a_ref, b_ref, o_ref):
        o_ref[...] = (a_ref[...].astype(F32) + b_ref[...].astype(F32)).astype(o_ref.dtype)

    return pl.pallas_call(
        body,
        name=name,
        grid_spec=pltpu.PrefetchScalarGridSpec(
            num_scalar_prefetch=1,
            grid=(4, r // br),
            in_specs=[
                pl.BlockSpec((None, None, br, c), lambda n, i, core: (core[0], n, i, 0)),
                pl.BlockSpec((None, br, c), lambda n, i, core: (n, i, 0)),
            ],
            out_specs=pl.BlockSpec((None, br, c), lambda n, i, core: (n, i, 0)),
        ),
        out_shape=jax.ShapeDtypeStruct((4, r, c), send.dtype),
        compiler_params=_params(("parallel", "parallel")),
    )(core, send, got)


def _adamw(w, g, m, v):
    m = ADAM_B1 * m + (1.0 - ADAM_B1) * g
    v = ADAM_B2 * v + (1.0 - ADAM_B2) * (g * g)
    m_hat = m / (1.0 - ADAM_B1 ** ADAM_STEP)
    v_hat = v / (1.0 - ADAM_B2 ** ADAM_STEP)
    delta = -ADAM_LR * (m_hat / (jnp.sqrt(v_hat) + ADAM_EPS) + ADAM_WD * w)
    return delta, m, v


def _shard_update(send, got, recv, w, m, v, pos, name):
    _, r, c = w.shape
    br = _rows_tile(r)

    def body(pos_ref, a_ref, b_ref, r_ref, w_ref, m_ref, v_ref, g_ref, d_ref, nm_ref, nv_ref):
        g = a_ref[...].astype(F32) + b_ref[...].astype(F32)
        for n in range(3):
            g = g + r_ref[n].astype(F32)
        g_ref[...] = g
        d_ref[...], nm_ref[...], nv_ref[...] = _adamw(w_ref[...], g, m_ref[...], v_ref[...])

    own = pl.BlockSpec((None, br, c), lambda i, pos: (0, i, 0))
    return pl.pallas_call(
        body,
        name=name,
        grid_spec=pltpu.PrefetchScalarGridSpec(
            num_scalar_prefetch=1,
            grid=(r // br,),
            in_specs=[
                pl.BlockSpec((None, None, br, c), lambda i, pos: (pos[0], pos[1], i, 0)),
                pl.BlockSpec((None, br, c), lambda i, pos: (pos[1], i, 0)),
                pl.BlockSpec((3, br, c), lambda i, pos: (0, i, 0)),
                own, own, own,
            ],
            out_specs=[own, own, own, own],
        ),
        out_shape=[jax.ShapeDtypeStruct((1, r, c), F32)] * 4,
        compiler_params=_params(("parallel",)),
    )(pos, send, got, recv, w, m, v)


def _shard_update_direct(parts, chunks, w, m, v, me, name):
    _, r, c = w.shape
    br = _rows_tile(r)

    def body(me_ref, p_ref, own_ref, w_ref, m_ref, v_ref, g_ref, d_ref, nm_ref, nv_ref):
        g = None
        for n in range(N_DEV):
            part = jnp.where(me_ref[0] == n, own_ref[...], p_ref[n]).astype(F32)
            g = part if g is None else g + part
        g_ref[...] = g
        d_ref[...], nm_ref[...], nv_ref[...] = _adamw(w_ref[...], g, m_ref[...], v_ref[...])

    shard = pl.BlockSpec((None, br, c), lambda i, me: (0, i, 0))
    return pl.pallas_call(
        body,
        name=name,
        grid_spec=pltpu.PrefetchScalarGridSpec(
            num_scalar_prefetch=1,
            grid=(r // br,),
            in_specs=[
                pl.BlockSpec((N_DEV, br, c), lambda i, me: (0, i, 0)),
                pl.BlockSpec((None, br, c), lambda i, me: (me[0], i, 0)),
                shard, shard, shard,
            ],
            out_specs=[shard, shard, shard, shard],
        ),
        out_shape=[jax.ShapeDtypeStruct((1, r, c), F32)] * 4,
        compiler_params=_params(("parallel",)),
    )(me, parts, chunks, w, m, v)


def _small_update(parts, w, m, v):
    R = w.shape[0]

    def body(p_ref, w_ref, m_ref, v_ref, g_ref, d_ref, nm_ref, nv_ref):
        g = p_ref[0]
        for n in range(1, N_DEV):
            g = g + p_ref[n]
        g_ref[...] = g
        d_ref[...], nm_ref[...], nv_ref[...] = _adamw(w_ref[...], g, m_ref[...], v_ref[...])

    return pl.pallas_call(
        body,
        name="small_update",
        out_shape=[jax.ShapeDtypeStruct((R, LANES), F32)] * 4,
        compiler_params=pltpu.CompilerParams(vmem_limit_bytes=VMEM_LIMIT),
    )(parts, w, m, v)


_SHARD_AXIS = (1, 1, 1, 0, 0, 0, 0)
_TRANSPOSED = (False, False, False, False, True, True, False)


def _full_from_gathered(t, axis):
    if axis == 0:
        return t.reshape(N_DEV * t.shape[1], t.shape[2])
    return jnp.concatenate([t[d] for d in range(N_DEV)], axis=1)


def _chunks_from_cols(t):
    c = t.shape[1] // N_DEV
    return jnp.stack([t[:, d * c : (d + 1) * c] for d in range(N_DEV)])


_SMALL = (("norm1_g", 8), ("norm2_g", 8), ("norm_f_g", 8), ("b_forget", 8), ("pool_scale", 8), ("pool_mix", 512))
_SMALL_ROWS = sum(r for _, r in _SMALL) + 8


def _pack_small(vals, loss_row):
    parts = []
    for (name, rows), t in zip(_SMALL, vals):
        f = t.astype(F32).reshape(-1)
        f = jnp.concatenate([f, jnp.zeros((rows * LANES - f.shape[0],), F32)]).reshape(rows, LANES)
        parts.append(f)
    parts.append(loss_row)
    return jnp.concatenate(parts, axis=0)


def _unpack_small(packed, shapes):
    out, off = [], 0
    for (name, rows), shape in zip(_SMALL, shapes):
        n = 1
        for s in shape:
            n *= s
        out.append(packed[off : off + rows].reshape(-1)[:n].reshape(shape))
        off += rows
    return out, packed[off, 0]


def _local_grads(x, tgt, g1, g2, gf, b_forget, pool_mix, pool_scale, w_in, fwd_token, out_weights, ffn_weights, ffn_grads_out, out_grads_out, small_grads_out):
    n_seq, S, _ = x.shape
    T = n_seq * S
    x2 = x.reshape(T, D_MODEL)
    tg2 = tgt.reshape(T, D_MODEL)
    w_uqkv, w_fl, w_g = w_in
    b_pad = jnp.concatenate([b_forget.reshape(1, N_HEADS), jnp.zeros((1, FL_PAD - N_HEADS), F32)], axis=1)
    mix_b = pool_mix.reshape(len(POOL_WINDOWS), GROUP_DIM, GROUP_DIM).astype(BF16)
    scale = pool_scale.reshape(1, POOL_WIDTH)
    g1 = g1.reshape(1, D_MODEL)
    g2 = g2.reshape(1, D_MODEL)
    gf = gf.reshape(1, D_MODEL)

    h, u, qkv, fl, gates = _in_proj(x2, g1, w_uqkv, w_fl, w_g, fwd_token)
    fcol = _forget_fwd(fl, b_pad, n_seq, S)
    pm, p2, p3 = _pool_fwd(u, mix_b, scale, n_seq, S)
    a, lse = _attn_fwd(qkv, fcol, n_seq, S)
    w_po, w_ao, w_out = out_weights(a)
    merged, x1, attn_y, pool_y = _mix_out(a, p3, gates, x2, w_ao, w_po, w_out)
    w_gate_t, w_up_t, w_down = ffn_weights(x1)
    h2, gate, up, act, dx2, loss_rows, dgf = _ffn_fwd(x1, g2, gf, tg2, w_gate_t, w_up_t, w_down)

    dgate, dup, dx1, dg2 = _ffn_bwd(dx2, gate, up, x1, g2, w_gate_t, w_up_t, w_down)
    bwd_token = ffn_grads_out(_matmul_tn(dgate, h2, "dw_ffn_gate"), _matmul_tn(dup, h2, "dw_ffn_up"), _matmul_tn(act, dx2, "dw_ffn_down"))
    dgates, dpy, day, da, dp2, dscale = _mix_bwd(dx1, gates, pool_y, attn_y, p2, scale, w_out, w_ao, w_po, bwd_token)
    out_token = out_grads_out(_matmul_tn(p3, dpy, "dw_pool_out"), _matmul_tn(a, day, "dw_attn_out"), _matmul_tn(merged, dx1, "dw_out"))
    du, dmix = _pool_bwd(dp2, pm, mix_b, out_token, n_seq, S)
    dq, dk, dv, dfk, dfq = _attn_bwd(qkv, da, a, fcol, lse, n_seq, S)
    dfl, db = _forget_bwd(dfk, dfq, fl, b_pad, n_seq, S)
    dx, dg1 = _in_proj_bwd(du, dq, dk, dv, dfl, dgates, x2, dx1, g1, w_uqkv, w_fl, w_g)
    small_token = small_grads_out((dg1, dg2, dgf, db[:, :N_HEADS], dscale, dmix), loss_rows)

    return dx.reshape(n_seq, S, D_MODEL), _dw_in(h, du, dq, dk, dv, dfl, dgates, small_token)


def kernel(x, norm1_g, w_in, b_forget, pool_mix, pool_scale, w_pool_out, w_attn_out, w_out, norm2_g, w_ffn_gate, w_ffn_up, w_ffn_down, norm_f_g, loss_target, m_norm1_g, m_w_in, m_b_forget, m_pool_mix, m_pool_scale, m_w_pool_out, m_w_attn_out, m_w_out, m_norm2_g, m_w_ffn_gate, m_w_ffn_up, m_w_ffn_down, m_norm_f_g, v_norm1_g, v_w_in, v_b_forget, v_pool_mix, v_pool_scale, v_w_pool_out, v_w_attn_out, v_w_out, v_norm2_g, v_w_ffn_gate, v_w_ffn_up, v_w_ffn_down, v_norm_f_g):
    names = ("w_in", "w_pool_out", "w_attn_out", "w_out", "w_ffn_gate", "w_ffn_up", "w_ffn_down")
    w_sh = (w_in, w_pool_out, w_attn_out, w_out, w_ffn_gate, w_ffn_up, w_ffn_down)
    m_sh = (m_w_in, m_w_pool_out, m_w_attn_out, m_w_out, m_w_ffn_gate, m_w_ffn_up, m_w_ffn_down)
    v_sh = (v_w_in, v_w_pool_out, v_w_attn_out, v_w_out, v_w_ffn_gate, v_w_ffn_up, v_w_ffn_down)

    cx, cy, cc = _position()
    me = 4 * cx + 2 * cy + cc
    def stored(t, transposed):
        return jnp.transpose(t, (0, 2, 1)) if transposed else t

    w_sh, m_sh, v_sh = ([stored(t, tr) for t, tr in zip(ts, _TRANSPOSED)] for ts in (w_sh, m_sh, v_sh))
    shards = [w[0].astype(BF16) for w in w_sh]
    (gathered_in,) = _all_gather(shards[:1], "w_in_all_gather")
    out_sems = _exchange_start(shards[1:4], gathered_in, "out_weights_gather_start", "gather")
    ffn_sems = _exchange_start(shards[4:], out_sems[4], "ffn_weights_gather_start", "gather")
    no_order = jnp.zeros((8, LANES), F32)

    def with_own(lands, own):
        return [lax.dynamic_update_slice(l, o[None], (me, 0, 0)) for l, o in zip(lands, own)]

    def gathered_weights(sems, axes, name):
        def wait(after):
            send_sems, recv_sems, srcs, lands, _ = sems
            srcs, lands = _exchange_wait(send_sems, recv_sems, srcs, lands, after, name, "gather")
            return [_full_from_gathered(t, axis) for t, axis in zip(with_own(lands, srcs), axes)]

        return wait

    started = {}

    def scatter_grads(key, name):
        def start(*whole_grads):
            chunks = [
                _chunks_from_cols(t) if axis == 1 else t.reshape(N_DEV, -1, t.shape[1])
                for t, axis in zip(whole_grads, _SHARD_AXIS[key])
            ]
            started[key] = _exchange_start(chunks, no_order, name, "scatter")
            return started[key][4]

        return start

    def gather_small(small, loss_rows):
        started["small"] = _exchange_start([_pack_small(small, loss_rows)], no_order, "small_grads_gather_start", "gather")
        return started["small"][4]

    ffn, out = slice(4, 7), slice(1, 4)
    grad_x, send_in = _local_grads(
        x, loss_target, norm1_g, norm2_g, norm_f_g, b_forget, pool_mix, pool_scale, _w_in_pieces(gathered_in), ffn_sems[4],
        gathered_weights(out_sems, _SHARD_AXIS[out], "out_weights_gather_wait"),
        gathered_weights(ffn_sems, _SHARD_AXIS[ffn], "ffn_weights_gather_wait"),
        scatter_grads(ffn, "ffn_grads_scatter_start"), scatter_grads(out, "out_grads_scatter_start"), gather_small,
    )

    core = jnp.reshape(cc, (1,)).astype(jnp.int32)
    pos = jnp.stack([cc, 2 * cx + cy]).astype(jnp.int32)
    (got_in,) = _sibling_exchange([send_in])
    pair_in = _pair_sum(send_in, got_in, core, "pair_sum_w_in")
    chip_sems = _exchange_start([pair_in], no_order, "w_in_grads_chips_start", "chips")

    def scattered_updates(key, after, name):
        send_sems, recv_sems, srcs, lands, _ = started[key]
        srcs, lands = _exchange_wait(send_sems, recv_sems, srcs, lands, after, name, "scatter")
        return [
            _shard_update_direct(p, s, w, m, v, jnp.reshape(me, (1,)).astype(jnp.int32), "update_" + n)
            for p, s, w, m, v, n in zip(lands, srcs, w_sh[key], m_sh[key], v_sh[key], names[key])
        ]

    updates_out = scattered_updates(out, chip_sems[4], "out_grads_scatter_wait")
    updates_ffn = scattered_updates(ffn, chip_sems[4], "ffn_grads_scatter_wait")

    small_w = (norm1_g, norm2_g, norm_f_g, b_forget, pool_scale, pool_mix)
    small_m = (m_norm1_g, m_norm2_g, m_norm_f_g, m_b_forget, m_pool_scale, m_pool_mix)
    small_v = (v_norm1_g, v_norm2_g, v_norm_f_g, v_b_forget, v_pool_scale, v_pool_mix)
    zero_row = jnp.zeros((8, LANES), F32)
    send_sems, recv_sems, srcs, lands, _ = started["small"]
    srcs, lands = _exchange_wait(send_sems, recv_sems, srcs, lands, updates_ffn[-1][0], "small_grads_gather_wait", "gather")
    (parts,) = with_own(lands, srcs)
    g_s, d_s, nm_s, nv_s = _small_update(parts, _pack_small(small_w, zero_row), _pack_small(small_m, zero_row), _pack_small(small_v, zero_row))

    send_sems, recv_sems, srcs, lands, _ = chip_sems
    _, (recv_in,) = _exchange_wait(send_sems, recv_sems, srcs, lands, g_s, "w_in_grads_chips_wait", "chips")
    update_in = _shard_update(send_in, got_in, recv_in, w_in, m_w_in, v_w_in, pos, "update_w_in")
    g_w, d_w, nm_w, nv_w = zip(*(
        [stored(t, tr) for t in u] for u, tr in zip([update_in] + updates_out + updates_ffn, _TRANSPOSED)
    ))
    shapes = [t.shape for t in small_w]
    (g1, g2, gf, gb, gsc, gmix), loss = _unpack_small(g_s, shapes)
    (d1, d2, df, db_, dsc, dmx), _ = _unpack_small(d_s, shapes)
    (m1, m2, mf, mb, msc, mmx), _ = _unpack_small(nm_s, shapes)
    (v1, v2, vf, vb, vsc, vmx), _ = _unpack_small(nv_s, shapes)

    def ordered(n1, win, b, mix, sc, wpo, wao, wout, n2, wg, wu, wd, nf):
        return (n1, win, b, mix, sc, wpo, wao, wout, n2, wg, wu, wd, nf)

    grads = ordered(g1, g_w[0], gb, gmix, gsc, g_w[1], g_w[2], g_w[3], g2, g_w[4], g_w[5], g_w[6], gf)
    deltas = ordered(d1, d_w[0], db_, dmx, dsc, d_w[1], d_w[2], d_w[3], d2, d_w[4], d_w[5], d_w[6], df)
    new_m = ordered(m1, nm_w[0], mb, mmx, msc, nm_w[1], nm_w[2], nm_w[3], m2, nm_w[4], nm_w[5], nm_w[6], mf)
    new_v = ordered(v1, nv_w[0], vb, vmx, vsc, nv_w[1], nv_w[2], nv_w[3], v2, nv_w[4], nv_w[5], nv_w[6], vf)
    return (loss, grad_x, *grads, *deltas, *new_m, *new_v)
```

```python
import functools

import jax
import jax.numpy as jnp
from jax import lax
from jax.experimental import pallas as pl
from jax.experimental.pallas import tpu as pltpu

F32 = jnp.float32
BF16 = jnp.bfloat16
MESH = pl.DeviceIdType.MESH

D_MODEL = 1024
POOL_WINDOWS = (2, 4, 8, 16)
POOL_WIDTH = 512
GROUP_DIM = 128
ATTN_WIDTH = 512
HEAD_DIM = 64
N_HEADS = 8
N_PAIRS = 4
D_FF = 2816
RMS_EPS = 1e-6
N_DEV = 8
LANES = 128
FL_PAD = 128

ADAM_LR = 0.001
ADAM_B1 = 0.9
ADAM_B2 = 0.999
ADAM_EPS = 1e-08
ADAM_WD = 0.01
ADAM_STEP = 10

VMEM_LIMIT = 56 * 1024 * 1024
VMEM_LIMIT_MAX = 60 * 1024 * 1024
ROW_TILE = 512
ATTN_BLOCK = 512
FF_CHUNK = 256
FF_ROW_TILE = 512
DW_TOKENS = 2048


def _mm(a, b):
    return jnp.dot(a, b, preferred_element_type=F32)


def _mm_nt(a, b):
    return lax.dot_general(a, b, (((1,), (1,)), ((), ())), preferred_element_type=F32)


def _mm_tn(a, b):
    return lax.dot_general(a, b, (((0,), (0,)), ((), ())), preferred_element_type=F32)


def _sigmoid(x):
    return 1.0 / (1.0 + jnp.exp(-x))


def _params(sem, vmem=VMEM_LIMIT):
    return pltpu.CompilerParams(dimension_semantics=sem, vmem_limit_bytes=vmem)


def _const_spec(shape):
    nd = len(shape)
    return pl.BlockSpec(shape, lambda *_: (0,) * nd, pipeline_mode=pl.Buffered(1))


def _rms_fwd(x, g):
    r = lax.rsqrt(jnp.mean(x * x, axis=-1, keepdims=True) + RMS_EPS)
    xh = x * r
    return xh * g, xh, r


def _rms_bwd(dy, xh, r, g):
    dxh = dy * g
    dx = r * (dxh - xh * jnp.mean(dxh * xh, axis=-1, keepdims=True))
    return dx, dy * xh


def _in_proj(x, g1, w_uqkv, w_fl, w_g, token):
    T = x.shape[0]
    tm = ROW_TILE

    def body(x_ref, g_ref, wa_ref, wf_ref, wg_ref, token_ref, h_ref, u_ref, qkv_ref, fl_ref, gt_ref):
        h, _, _ = _rms_fwd(x_ref[...], g_ref[...])
        hb = h.astype(BF16)
        h_ref[...] = hb
        z = _mm(hb, wa_ref[...])
        u_ref[...] = z[:, :POOL_WIDTH]
        qkv_ref[...] = z[:, POOL_WIDTH:].astype(BF16)
        fl_ref[...] = _mm(hb, wf_ref[...])
        gt_ref[...] = _mm(hb, wg_ref[...]).astype(BF16)

    row = lambda n: pl.BlockSpec((tm, n), lambda i: (i, 0))
    return pl.pallas_call(
        body,
        name="in_proj",
        grid=(T // tm,),
        in_specs=[row(D_MODEL), _const_spec((1, D_MODEL)), _const_spec(w_uqkv.shape), _const_spec(w_fl.shape), _const_spec(w_g.shape), _HBM],
        out_specs=[row(D_MODEL), row(POOL_WIDTH), row(3 * ATTN_WIDTH), row(FL_PAD), row(2 * D_MODEL)],
        out_shape=[
            jax.ShapeDtypeStruct((T, D_MODEL), BF16),
            jax.ShapeDtypeStruct((T, POOL_WIDTH), F32),
            jax.ShapeDtypeStruct((T, 3 * ATTN_WIDTH), BF16),
            jax.ShapeDtypeStruct((T, FL_PAD), F32),
            jax.ShapeDtypeStruct((T, 2 * D_MODEL), BF16),
        ],
        compiler_params=_params(("parallel",)),
    )(x, g1, w_uqkv, w_fl, w_g, token)


def _log_sigmoid(x):
    return jnp.minimum(x, 0.0) - jnp.log(1.0 + jnp.exp(-jnp.abs(x)))


def _forget_fwd(fl, b_pad, n_seq, S):
    def body(fl_ref, b_ref, fcol_ref):
        lf = _log_sigmoid(fl_ref[...] + b_ref[...])
        t = lf.T
        lane = lax.broadcasted_iota(jnp.int32, t.shape, 1)
        k = 1
        while k < S:
            t = t + jnp.where(lane >= k, pltpu.roll(t, k, 1), 0.0)
            k *= 2
        fcol_ref[...] = t.T

    return pl.pallas_call(
        body,
        name="forget_fwd",
        grid=(n_seq,),
        in_specs=[pl.BlockSpec((S, FL_PAD), lambda s: (s, 0)), _const_spec((1, FL_PAD))],
        out_specs=pl.BlockSpec((S, FL_PAD), lambda s: (s, 0)),
        out_shape=jax.ShapeDtypeStruct((n_seq * S, FL_PAD), F32),
        compiler_params=_params(("parallel",)),
    )(fl, b_pad)


def _window_pick(g, v2, v4, v8, v16):
    return jnp.where(g == 0, v2, jnp.where(g == 1, v4, jnp.where(g == 2, v8, v16)))


def _pool_fwd(u, mix_b, scale, n_seq, S):
    T = n_seq * S

    def body(u_ref, mix_ref, sc_ref, pm_ref, p2_ref, p3_ref):
        g = pl.program_id(1)
        uu = u_ref[...]
        row = lax.broadcasted_iota(jnp.int32, uu.shape, 0)

        def back(a, k):
            return jnp.where(row >= k, pltpu.roll(a, k, 0), 0.0)

        s2 = uu + back(uu, 1)
        s4 = s2 + back(s2, 2)
        s8 = s4 + back(s4, 4)
        s16 = s8 + back(s8, 8)
        w = _window_pick(g, 2.0, 4.0, 8.0, 16.0)
        cnt = jnp.minimum((row + 1).astype(F32), w)
        pm = _window_pick(g, s2, s4, s8, s16) / cnt - uu
        pmb = pm.astype(BF16)
        pm_ref[...] = pmb
        p2 = _mm(pmb, mix_ref[...])
        p2_ref[...] = p2
        p3_ref[...] = (p2 * sc_ref[...]).astype(BF16)

    grp = pl.BlockSpec((S, GROUP_DIM), lambda s, g: (s, g))
    return pl.pallas_call(
        body,
        name="pool_fwd",
        grid=(n_seq, len(POOL_WINDOWS)),
        in_specs=[
            grp,
            pl.BlockSpec((None, GROUP_DIM, GROUP_DIM), lambda s, g: (g, 0, 0)),
            pl.BlockSpec((1, GROUP_DIM), lambda s, g: (0, g)),
        ],
        out_specs=[grp, grp, grp],
        out_shape=[
            jax.ShapeDtypeStruct((T, POOL_WIDTH), BF16),
            jax.ShapeDtypeStruct((T, POOL_WIDTH), F32),
            jax.ShapeDtypeStruct((T, POOL_WIDTH), BF16),
        ],
        compiler_params=_params(("parallel", "parallel")),
    )(u, mix_b, scale)


def _split3(v):
    hi = v.astype(BF16).astype(F32)
    r = v - hi
    mid = r.astype(BF16).astype(F32)
    lo = (r - mid).astype(BF16).astype(F32)
    return hi, mid, lo


def _augment(xp, hh, first, second):
    lane = lax.broadcasted_iota(jnp.int32, (1, LANES), 1)
    head = (lane >= HEAD_DIM * hh) & (lane < HEAD_DIM * (hh + 1))
    b = HEAD_DIM * (1 - hh)
    out = jnp.where(head, xp.astype(F32), 0.0)
    for n, col in enumerate(tuple(first) + tuple(second)):
        out = jnp.where(lane == b + n, col, out)
    return out.astype(BF16)


def _attn_fwd(qkv, fcol, n_seq, S):
    T = n_seq * S
    tb = ATTN_BLOCK
    nq = S // tb
    scale = HEAD_DIM ** -0.5

    def body(q_ref, k_ref, v_ref, fc_ref, o_ref, st_ref, qa_sc, ka_sc, m_sc, l_sc, acc_sc):
        i = pl.program_id(1)
        lane = lax.broadcasted_iota(jnp.int32, (1, LANES), 1)
        low = lane < HEAD_DIM
        ones = (1.0, 1.0, 1.0)

        @pl.when(i == 0)
        def _():
            def rows_ka(r, carry):
                r0 = pl.multiple_of(r * tb, tb)
                for h in range(N_HEADS):
                    kp = k_ref[pl.ds(r0, tb), (h // 2) * LANES : (h // 2 + 1) * LANES] * scale
                    fk = fc_ref[pl.ds(r0, tb), h : h + 1]
                    ka_sc[h, pl.ds(r0, tb), :] = _augment(kp, h % 2, ones, _split3(-fk))
                return carry

            lax.fori_loop(0, nq, rows_ka, 0)

        q0 = pl.multiple_of(i * tb, tb)
        for h in range(N_HEADS):
            qp = q_ref[:, (h // 2) * LANES : (h // 2 + 1) * LANES]
            qa_sc[h] = _augment(qp, h % 2, _split3(fc_ref[pl.ds(q0, tb), h : h + 1]), ones)
        m_sc[...] = jnp.full(m_sc.shape, -jnp.inf, F32)
        l_sc[...] = jnp.zeros_like(l_sc)
        acc_sc[...] = jnp.zeros_like(acc_sc)
        causal = lax.broadcasted_iota(jnp.int32, (tb, tb), 1) <= lax.broadcasted_iota(jnp.int32, (tb, tb), 0)

        def step(j, masked):
            c0 = pl.multiple_of(j * tb, tb)
            for p in range(N_PAIRS):
                vb = v_ref[pl.ds(c0, tb), p * LANES : (p + 1) * LANES]
                pv, al = [], []
                for hh in range(2):
                    h = 2 * p + hh
                    s = _mm_nt(qa_sc[h], ka_sc[h, pl.ds(c0, tb), :])
                    if masked:
                        s = jnp.where(causal, s, -jnp.inf)
                    m_old = m_sc[h]
                    m_new = jnp.maximum(m_old, jnp.max(s, axis=1, keepdims=True))
                    alpha = jnp.exp(m_old - m_new)
                    pe = jnp.exp(s - jnp.concatenate([m_new] * (tb // LANES), axis=1))
                    l_sc[h] = alpha * l_sc[h] + jnp.sum(pe, axis=1, keepdims=True)
                    m_sc[h] = m_new
                    pv.append(_mm(pe.astype(BF16), vb))
                    al.append(alpha)
                acc_sc[p] = jnp.where(low, al[0], al[1]) * acc_sc[p] + jnp.where(low, pv[0], pv[1])

        def loop_body(j, carry):
            step(j, False)
            return carry

        lax.fori_loop(0, i, loop_body, 0)
        step(i, True)
        st = jnp.zeros((tb, LANES), F32)
        for p in range(N_PAIRS):
            lp = jnp.where(low, l_sc[2 * p], l_sc[2 * p + 1])
            o_ref[:, p * LANES : (p + 1) * LANES] = (acc_sc[p] / lp).astype(BF16)
            for h in (2 * p, 2 * p + 1):
                st = jnp.where(lane == h, m_sc[h] + jnp.log(l_sc[h]), st)
        st_ref[...] = st

    return pl.pallas_call(
        body,
        name="attn_fwd",
        grid=(n_seq, nq),
        in_specs=[
            pl.BlockSpec((tb, ATTN_WIDTH), lambda s, i: (s * nq + i, 0)),
            pl.BlockSpec((S, ATTN_WIDTH), lambda s, i: (s, 1)),
            pl.BlockSpec((S, ATTN_WIDTH), lambda s, i: (s, 2)),
            pl.BlockSpec((S, LANES), lambda s, i: (s, 0)),
        ],
        out_specs=[
            pl.BlockSpec((tb, ATTN_WIDTH), lambda s, i: (s * nq + i, 0)),
            pl.BlockSpec((tb, LANES), lambda s, i: (s * nq + i, 0)),
        ],
        out_shape=[jax.ShapeDtypeStruct((T, ATTN_WIDTH), BF16), jax.ShapeDtypeStruct((T, LANES), F32)],
        scratch_shapes=[
            pltpu.VMEM((N_HEADS, tb, LANES), BF16),
            pltpu.VMEM((N_HEADS, S, LANES), BF16),
            pltpu.VMEM((N_HEADS, tb, LANES), F32),
            pltpu.VMEM((N_HEADS, tb, LANES), F32),
            pltpu.VMEM((N_PAIRS, tb, LANES), F32),
        ],
        compiler_params=_params(("parallel", "arbitrary")),
    )(qkv, qkv, qkv, fcol)


def _mix_out(a, p3, gates, x, w_ao, w_po, w_out):
    T = x.shape[0]
    tm = ROW_TILE

    def body(a_ref, p3_ref, gt_ref, x_ref, wao_ref, wpo_ref, wout_ref, mg_ref, x1_ref, ay_ref, py_ref):
        ay = _mm(a_ref[...], wao_ref[...])
        py = _mm(p3_ref[...], wpo_ref[...])
        ay_ref[...] = ay.astype(BF16)
        py_ref[...] = py.astype(BF16)
        sp = _sigmoid(gt_ref[:, :D_MODEL].astype(F32))
        sa = _sigmoid(gt_ref[:, D_MODEL:].astype(F32))
        mb = (sp * py + sa * ay).astype(BF16)
        mg_ref[...] = mb
        x1_ref[...] = x_ref[...] + _mm(mb, wout_ref[...])

    row = lambda n: pl.BlockSpec((tm, n), lambda i: (i, 0))
    return pl.pallas_call(
        body,
        name="mix_out",
        grid=(T // tm,),
        in_specs=[
            row(ATTN_WIDTH), row(POOL_WIDTH), row(2 * D_MODEL), row(D_MODEL),
            _const_spec(w_ao.shape), _const_spec(w_po.shape), _const_spec(w_out.shape),
        ],
        out_specs=[row(D_MODEL), row(D_MODEL), row(D_MODEL), row(D_MODEL)],
        out_shape=[
            jax.ShapeDtypeStruct((T, D_MODEL), BF16), jax.ShapeDtypeStruct((T, D_MODEL), F32),
            jax.ShapeDtypeStruct((T, D_MODEL), BF16), jax.ShapeDtypeStruct((T, D_MODEL), BF16),
        ],
        compiler_params=_params(("parallel",)),
    )(a, p3, gates, x, w_ao, w_po, w_out)


def _ffn_fwd(x1, g2, gf, tgt, w_gate_t, w_up_t, w_down):
    T = x1.shape[0]
    tm = min(T, FF_ROW_TILE)
    nt = T // tm
    nc = D_FF // FF_CHUNK

    def body(x1_ref, g2_ref, gf_ref, tg_ref, wg_ref, wu_ref, wd_ref, h2_ref, gate_ref, up_ref, act_ref, dx2_ref, loss_ref, dgf_ref):
        x1v = x1_ref[...]
        h2, _, _ = _rms_fwd(x1v, g2_ref[...])
        h2b = h2.astype(BF16)
        h2_ref[...] = h2b
        for c in range(nc):
            sl = slice(c * FF_CHUNK, (c + 1) * FF_CHUNK)
            gate = _mm_nt(h2b, wg_ref[sl, :])
            up = _mm_nt(h2b, wu_ref[sl, :])
            gate_ref[:, sl] = gate.astype(BF16)
            up_ref[:, sl] = up.astype(BF16)
            act_ref[:, sl] = (gate * _sigmoid(gate) * up).astype(BF16)
        acc = x1v + _mm(act_ref[...], wd_ref[...])
        gfv = gf_ref[...]
        y, xh, r = _rms_fwd(acc, gfv)
        err = y - tg_ref[...]
        part = 0.5 * jnp.sum(jnp.mean(err * err, axis=-1, keepdims=True), axis=0, keepdims=True)
        dx2, dgrow = _rms_bwd(err * (1.0 / D_MODEL), xh, r, gfv)
        dx2_ref[...] = dx2

        @pl.when(pl.program_id(0) == 0)
        def _():
            dgf_ref[...] = jnp.zeros_like(dgf_ref)
            loss_ref[...] = jnp.zeros_like(loss_ref)

        dgf_ref[...] += jnp.sum(dgrow, axis=0, keepdims=True)
        loss_ref[...] += jnp.broadcast_to(part, loss_ref.shape)

    row = lambda n: pl.BlockSpec((tm, n), lambda i: (i, 0))
    return pl.pallas_call(
        body,
        name="ffn_fwd",
        grid=(nt,),
        in_specs=[
            row(D_MODEL), _const_spec((1, D_MODEL)), _const_spec((1, D_MODEL)), row(D_MODEL),
            _const_spec(w_gate_t.shape), _const_spec(w_up_t.shape), _const_spec(w_down.shape),
        ],
        out_specs=[
            row(D_MODEL), row(D_FF), row(D_FF), row(D_FF), row(D_MODEL),
            pl.BlockSpec((8, LANES), lambda i: (0, 0)),
            pl.BlockSpec((1, D_MODEL), lambda i: (0, 0)),
        ],
        out_shape=[
            jax.ShapeDtypeStruct((T, D_MODEL), BF16),
            jax.ShapeDtypeStruct((T, D_FF), BF16),
            jax.ShapeDtypeStruct((T, D_FF), BF16),
            jax.ShapeDtypeStruct((T, D_FF), BF16),
            jax.ShapeDtypeStruct((T, D_MODEL), F32),
            jax.ShapeDtypeStruct((8, LANES), F32),
            jax.ShapeDtypeStruct((1, D_MODEL), F32),
        ],
        compiler_params=_params(("arbitrary",)),
    )(x1, g2, gf, tgt, w_gate_t, w_up_t, w_down)


def _ffn_bwd(dx2, gate, up, x1, g2, w_gate_t, w_up_t, w_down):
    T = x1.shape[0]
    tm = min(T, FF_ROW_TILE)
    nc = D_FF // FF_CHUNK

    def body(dx2_ref, gate_ref, up_ref, x1_ref, g2_ref, wg_ref, wu_ref, wd_ref, dgate_ref, dup_ref, dx1_ref, dg2_ref):
        dx2v = dx2_ref[...]
        dx2b = dx2v.astype(BF16)
        for c in range(nc):
            sl = slice(c * FF_CHUNK, (c + 1) * FF_CHUNK)
            dact = _mm_nt(dx2b, wd_ref[sl, :])
            gate = gate_ref[:, sl].astype(F32)
            sg = _sigmoid(gate)
            silu = gate * sg
            dgate = (dact * up_ref[:, sl].astype(F32) * (sg * (1.0 + gate * (1.0 - sg)))).astype(BF16)
            dup = (dact * silu).astype(BF16)
            dgate_ref[:, sl] = dgate
            dup_ref[:, sl] = dup
        dh2 = _mm(dgate_ref[...], wg_ref[...]) + _mm(dup_ref[...], wu_ref[...])
        g2v = g2_ref[...]
        _, xh, r = _rms_fwd(x1_ref[...], g2v)
        dxn, dgrow = _rms_bwd(dh2, xh, r, g2v)
        dx1_ref[...] = dx2v + dxn

        @pl.when(pl.program_id(0) == 0)
        def _():
            dg2_ref[...] = jnp.zeros_like(dg2_ref)

        dg2_ref[...] += jnp.sum(dgrow, axis=0, keepdims=True)

    row = lambda n: pl.BlockSpec((tm, n), lambda i: (i, 0))
    return pl.pallas_call(
        body,
        name="ffn_bwd",
        grid=(T // tm,),
        in_specs=[
            row(D_MODEL), row(D_FF), row(D_FF), row(D_MODEL), _const_spec((1, D_MODEL)),
            _const_spec(w_gate_t.shape), _const_spec(w_up_t.shape), _const_spec(w_down.shape),
        ],
        out_specs=[row(D_FF), row(D_FF), row(D_MODEL), pl.BlockSpec((1, D_MODEL), lambda i: (0, 0))],
        out_shape=[
            jax.ShapeDtypeStruct((T, D_FF), BF16),
            jax.ShapeDtypeStruct((T, D_FF), BF16),
            jax.ShapeDtypeStruct((T, D_MODEL), F32),
            jax.ShapeDtypeStruct((1, D_MODEL), F32),
        ],
        compiler_params=_params(("arbitrary",), VMEM_LIMIT_MAX),
    )(dx2, gate, up, x1, g2, w_gate_t, w_up_t, w_down)


def _mix_bwd(dx1, gates, pool_y, attn_y, p2, scale, w_out, w_ao, w_po, token):
    T = dx1.shape[0]
    tm = ROW_TILE

    def body(dx1_ref, gt_ref, py_ref, ay_ref, p2_ref, sc_ref, wout_ref, wao_ref, wpo_ref, token_ref, dgt_ref, dpy_ref, day_ref, da_ref, dp2_ref, dsc_ref):
        dm = _mm_nt(dx1_ref[...].astype(BF16), wout_ref[...])
        sp = _sigmoid(gt_ref[:, :D_MODEL].astype(F32))
        sa = _sigmoid(gt_ref[:, D_MODEL:].astype(F32))
        dgt_ref[:, :D_MODEL] = (dm * py_ref[...].astype(F32) * (sp * (1.0 - sp))).astype(BF16)
        dgt_ref[:, D_MODEL:] = (dm * ay_ref[...].astype(F32) * (sa * (1.0 - sa))).astype(BF16)
        dpy = (dm * sp).astype(BF16)
        day = (dm * sa).astype(BF16)
        dpy_ref[...] = dpy
        day_ref[...] = day
        da_ref[...] = _mm_nt(day, wao_ref[...]).astype(BF16)
        dp3 = _mm_nt(dpy, wpo_ref[...])
        dp2_ref[...] = (dp3 * sc_ref[...]).astype(BF16)

        @pl.when(pl.program_id(0) == 0)
        def _():
            dsc_ref[...] = jnp.zeros_like(dsc_ref)

        dsc_ref[...] += jnp.sum(dp3 * p2_ref[...], axis=0, keepdims=True)

    row = lambda n: pl.BlockSpec((tm, n), lambda i: (i, 0))
    return pl.pallas_call(
        body,
        name="mix_bwd",
        grid=(T // tm,),
        in_specs=[
            row(D_MODEL), row(2 * D_MODEL), row(D_MODEL), row(D_MODEL), row(POOL_WIDTH), _const_spec((1, POOL_WIDTH)),
            _const_spec(w_out.shape), _const_spec(w_ao.shape), _const_spec(w_po.shape), _HBM,
        ],
        out_specs=[row(2 * D_MODEL), row(D_MODEL), row(D_MODEL), row(ATTN_WIDTH), row(POOL_WIDTH), pl.BlockSpec((1, POOL_WIDTH), lambda i: (0, 0))],
        out_shape=[
            jax.ShapeDtypeStruct((T, 2 * D_MODEL), BF16),
            jax.ShapeDtypeStruct((T, D_MODEL), BF16),
            jax.ShapeDtypeStruct((T, D_MODEL), BF16),
            jax.ShapeDtypeStruct((T, ATTN_WIDTH), BF16),
            jax.ShapeDtypeStruct((T, POOL_WIDTH), BF16),
            jax.ShapeDtypeStruct((1, POOL_WIDTH), F32),
        ],
        compiler_params=_params(("arbitrary",)),
    )(dx1, gates, pool_y, attn_y, p2, scale, w_out, w_ao, w_po, token)


def _pool_bwd(dp2, pm, mix_b, token, n_seq, S):
    T = n_seq * S

    def body(dp2_ref, pm_ref, mix_ref, token_ref, du_ref, dmix_ref):
        g = pl.program_id(0)
        dp2v = dp2_ref[...]
        dpm = _mm_nt(dp2v, mix_ref[...])
        row = lax.broadcasted_iota(jnp.int32, dpm.shape, 0)
        w = _window_pick(g, 2.0, 4.0, 8.0, 16.0)
        e = dpm / jnp.minimum((row + 1).astype(F32), w)

        def ahead(a, k):
            return jnp.where(row < S - k, pltpu.roll(a, S - k, 0), 0.0)

        r2 = e + ahead(e, 1)
        r4 = r2 + ahead(r2, 2)
        r8 = r4 + ahead(r4, 4)
        r16 = r8 + ahead(r8, 8)
        du_ref[...] = (_window_pick(g, r2, r4, r8, r16) - dpm).astype(BF16)

        @pl.when(pl.program_id(1) == 0)
        def _():
            dmix_ref[...] = jnp.zeros_like(dmix_ref)

        dmix_ref[...] += _mm_tn(pm_ref[...], dp2v)

    grp = pl.BlockSpec((S, GROUP_DIM), lambda g, s: (s, g))
    mixs = pl.BlockSpec((None, GROUP_DIM, GROUP_DIM), lambda g, s: (g, 0, 0))
    return pl.pallas_call(
        body,
        name="pool_bwd",
        grid=(len(POOL_WINDOWS), n_seq),
        in_specs=[grp, grp, mixs, _HBM],
        out_specs=[grp, mixs],
        out_shape=[jax.ShapeDtypeStruct((T, POOL_WIDTH), BF16), jax.ShapeDtypeStruct((len(POOL_WINDOWS), GROUP_DIM, GROUP_DIM), F32)],
        compiler_params=_params(("parallel", "arbitrary")),
    )(dp2, pm, mix_b, token)


def _attn_bwd(qkv, da, a, fcol, lse, n_seq, S):
    T = n_seq * S
    tb = ATTN_BLOCK
    nb = S // tb
    scale = HEAD_DIM ** -0.5

    def body(q_ref, k_ref, v_ref, do_ref, o_ref, fc_ref, st_ref, dq_ref, dk_ref, dv_ref, dfk_ref, dfq_ref,
             qa_sc, doa_sc, dq_acc, ka_sc, va_sc, dk_sc, dv_sc):
        j = pl.program_id(1)
        lane = lax.broadcasted_iota(jnp.int32, (1, LANES), 1)
        low = lane < HEAD_DIM
        ones = (1.0, 1.0, 1.0)
        zeros = (0.0, 0.0, 0.0)

        @pl.when(j == 0)
        def _():
            dq_acc[...] = jnp.zeros_like(dq_acc)

            def rows_q(i, carry):
                r0 = pl.multiple_of(i * tb, tb)
                for h in range(N_HEADS):
                    pair = slice((h // 2) * LANES, (h // 2 + 1) * LANES)
                    qp = q_ref[pl.ds(r0, tb), pair]
                    dop = do_ref[pl.ds(r0, tb), pair]
                    prod = dop.astype(F32) * o_ref[pl.ds(r0, tb), pair].astype(F32)
                    head = (lane >= HEAD_DIM * (h % 2)) & (lane < HEAD_DIM * (h % 2 + 1))
                    delta = jnp.sum(jnp.where(head, prod, 0.0), axis=1, keepdims=True)
                    cq = fc_ref[pl.ds(r0, tb), h : h + 1] - st_ref[pl.ds(r0, tb), h : h + 1]
                    qa_sc[h, pl.ds(r0, tb), :] = _augment(qp, h % 2, _split3(cq), ones)
                    doa_sc[h, pl.ds(r0, tb), :] = _augment(dop, h % 2, _split3(-delta), zeros)
                return carry

            lax.fori_loop(0, nb, rows_q, 0)

        c0 = pl.multiple_of(j * tb, tb)
        for h in range(N_HEADS):
            pair = slice((h // 2) * LANES, (h // 2 + 1) * LANES)
            kp = k_ref[:, pair] * scale
            ka_sc[h] = _augment(kp, h % 2, ones, _split3(-fc_ref[pl.ds(c0, tb), h : h + 1]))
            va_sc[h] = _augment(v_ref[:, pair], h % 2, ones, zeros)
        dk_sc[...] = jnp.zeros_like(dk_sc)
        dv_sc[...] = jnp.zeros_like(dv_sc)
        causal = lax.broadcasted_iota(jnp.int32, (tb, tb), 1) <= lax.broadcasted_iota(jnp.int32, (tb, tb), 0)

        def step(i, masked):
            r0 = pl.multiple_of(i * tb, tb)
            for h in range(N_HEADS):
                dob = do_ref[pl.ds(r0, tb), (h // 2) * LANES : (h // 2 + 1) * LANES]
                qa = qa_sc[h, pl.ds(r0, tb), :]
                s = _mm_nt(qa, ka_sc[h])
                if masked:
                    s = jnp.where(causal, s, -jnp.inf)
                pr = jnp.exp(s)
                dv_sc[h] += _mm_tn(pr.astype(BF16), dob)
                dsb = (pr * _mm_nt(doa_sc[h, pl.ds(r0, tb), :], va_sc[h])).astype(BF16)
                dk_sc[h] += _mm_tn(dsb, qa)
                dq_acc[h, pl.ds(r0, tb), :] += _mm(dsb, ka_sc[h])

        step(j, True)

        def loop_body(i, carry):
            step(i, False)
            return carry

        lax.fori_loop(j + 1, nb, loop_body, 0)
        dfk = jnp.zeros((tb, LANES), F32)
        for p in range(N_PAIRS):
            dk_ref[:, p * LANES : (p + 1) * LANES] = (jnp.where(low, dk_sc[2 * p], dk_sc[2 * p + 1]) * scale).astype(BF16)
            dv_ref[:, p * LANES : (p + 1) * LANES] = jnp.where(low, dv_sc[2 * p], dv_sc[2 * p + 1]).astype(BF16)
            for hh in range(2):
                b = HEAD_DIM * (1 - hh) + 3
                dfk = jnp.where(lane == 2 * p + hh, -dk_sc[2 * p + hh][:, b : b + 1], dfk)
        dfk_ref[...] = dfk

        @pl.when(j == nb - 1)
        def _():
            def rows_dq(i, carry):
                r0 = pl.multiple_of(i * tb, tb)
                dfq = jnp.zeros((tb, LANES), F32)
                for p in range(N_PAIRS):
                    parts = [dq_acc[2 * p + hh, pl.ds(r0, tb), :] for hh in range(2)]
                    dq_ref[pl.ds(r0, tb), p * LANES : (p + 1) * LANES] = jnp.where(low, parts[0], parts[1]).astype(BF16)
                    for hh in range(2):
                        b = HEAD_DIM * (1 - hh)
                        dfq = jnp.where(lane == 2 * p + hh, parts[hh][:, b : b + 1], dfq)
                dfq_ref[pl.ds(r0, tb), :] = dfq
                return carry

            lax.fori_loop(0, nb, rows_dq, 0)

    seq = lambda w, col: pl.BlockSpec((S, w), lambda s, j: (s, col))
    blk = lambda w, col: pl.BlockSpec((tb, w), lambda s, j: (s * nb + j, col))
    return pl.pallas_call(
        body,
        name="attn_bwd",
        grid=(n_seq, nb),
        in_specs=[seq(ATTN_WIDTH, 0), blk(ATTN_WIDTH, 1), blk(ATTN_WIDTH, 2), seq(ATTN_WIDTH, 0), seq(ATTN_WIDTH, 0), seq(LANES, 0), seq(LANES, 0)],
        out_specs=[seq(ATTN_WIDTH, 0), blk(ATTN_WIDTH, 0), blk(ATTN_WIDTH, 0), blk(LANES, 0), seq(LANES, 0)],
        out_shape=[
            jax.ShapeDtypeStruct((T, ATTN_WIDTH), BF16),
            jax.ShapeDtypeStruct((T, ATTN_WIDTH), BF16),
            jax.ShapeDtypeStruct((T, ATTN_WIDTH), BF16),
            jax.ShapeDtypeStruct((T, LANES), F32),
            jax.ShapeDtypeStruct((T, LANES), F32),
        ],
        scratch_shapes=[
            pltpu.VMEM((N_HEADS, S, LANES), BF16),
            pltpu.VMEM((N_HEADS, S, LANES), BF16),
            pltpu.VMEM((N_HEADS, S, LANES), F32),
            pltpu.VMEM((N_HEADS, tb, LANES), BF16),
            pltpu.VMEM((N_HEADS, tb, LANES), BF16),
            pltpu.VMEM((N_HEADS, tb, LANES), F32),
            pltpu.VMEM((N_HEADS, tb, LANES), F32),
        ],
        compiler_params=_params(("parallel", "arbitrary")),
    )(qkv, qkv, qkv, da, a, fcol, lse)


def _forget_bwd(dfk, dfq, fl, b_pad, n_seq, S):
    def body(df_ref, dfq_ref, fl_ref, b_ref, dfl_ref, db_ref):
        t = (df_ref[...] + dfq_ref[...]).T
        lane = lax.broadcasted_iota(jnp.int32, t.shape, 1)
        k = 1
        while k < S:
            t = t + jnp.where(lane < S - k, pltpu.roll(t, S - k, 1), 0.0)
            k *= 2
        dfl = t.T * _sigmoid(-(fl_ref[...] + b_ref[...]))
        dfl_ref[...] = dfl.astype(BF16)

        @pl.when(pl.program_id(0) == 0)
        def _():
            db_ref[...] = jnp.zeros_like(db_ref)

        db_ref[...] += jnp.sum(dfl, axis=0, keepdims=True)

    return pl.pallas_call(
        body,
        name="forget_bwd",
        grid=(n_seq,),
        in_specs=[
            pl.BlockSpec((S, LANES), lambda s: (s, 0)),
            pl.BlockSpec((S, LANES), lambda s: (s, 0)),
            pl.BlockSpec((S, FL_PAD), lambda s: (s, 0)),
            _const_spec((1, FL_PAD)),
        ],
        out_specs=[pl.BlockSpec((S, FL_PAD), lambda s: (s, 0)), pl.BlockSpec((1, FL_PAD), lambda s: (0, 0))],
        out_shape=[jax.ShapeDtypeStruct((n_seq * S, FL_PAD), BF16), jax.ShapeDtypeStruct((1, FL_PAD), F32)],
        compiler_params=_params(("arbitrary",)),
    )(dfk, dfq, fl, b_pad)


def _in_proj_bwd(du, dq, dk, dv, dfl, dgates, x, dx1, g1, w_uqkv, w_fl, w_g, token):
    T = x.shape[0]
    tm = ROW_TILE

    def body(du_ref, dq_ref, dk_ref, dv_ref, dfl_ref, dgt_ref, x_ref, dx1_ref, g_ref, wa_ref, wf_ref, wg_ref, token_ref, dx_ref, dg_ref):
        dz = jnp.concatenate([du_ref[...], dq_ref[...], dk_ref[...], dv_ref[...]], axis=1)
        dh = _mm_nt(dz, wa_ref[...]) + _mm_nt(dgt_ref[...], wg_ref[...]) + _mm_nt(dfl_ref[...], wf_ref[...])
        gv = g_ref[...]
        _, xh, r = _rms_fwd(x_ref[...], gv)
        dxn, dgrow = _rms_bwd(dh, xh, r, gv)
        dx_ref[...] = dx1_ref[...] + dxn

        @pl.when(pl.program_id(0) == 0)
        def _():
            dg_ref[...] = jnp.zeros_like(dg_ref)

        dg_ref[...] += jnp.sum(dgrow, axis=0, keepdims=True)

    row = lambda n: pl.BlockSpec((tm, n), lambda i: (i, 0))
    return pl.pallas_call(
        body,
        name="in_proj_bwd",
        grid=(T // tm,),
        in_specs=[
            row(512), row(512), row(512), row(512), row(FL_PAD), row(2 * D_MODEL), row(D_MODEL), row(D_MODEL), _const_spec((1, D_MODEL)),
            _const_spec(w_uqkv.shape), _const_spec(w_fl.shape), _const_spec(w_g.shape), _HBM,
        ],
        out_specs=[row(D_MODEL), pl.BlockSpec((1, D_MODEL), lambda i: (0, 0))],
        out_shape=[jax.ShapeDtypeStruct((T, D_MODEL), F32), jax.ShapeDtypeStruct((1, D_MODEL), F32)],
        compiler_params=_params(("arbitrary",)),
    )(du, dq, dk, dv, dfl, dgates, x, dx1, g1, w_uqkv, w_fl, w_g, token)


def _pick_block(n):
    for b in (512, 1408, 256, 128):
        if n % b == 0:
            return b
    raise ValueError(n)


def _matmul_tn(a, b, name):
    T, K = a.shape
    N = b.shape[1]
    bt, bk, bn = min(T, DW_TOKENS), _pick_block(K), _pick_block(N)
    nt = T // bt

    def body(a_ref, b_ref, o_ref, acc):
        @pl.when(pl.program_id(2) == 0)
        def _():
            acc[...] = jnp.zeros_like(acc)

        acc[...] += _mm_tn(a_ref[...].astype(BF16), b_ref[...].astype(BF16))

        @pl.when(pl.program_id(2) == nt - 1)
        def _():
            o_ref[...] = acc[...].astype(BF16)

    return pl.pallas_call(
        body,
        name=name,
        grid=(K // bk, N // bn, nt),
        in_specs=[pl.BlockSpec((bt, bk), lambda k, n, t: (t, k)), pl.BlockSpec((bt, bn), lambda k, n, t: (t, n))],
        out_specs=pl.BlockSpec((bk, bn), lambda k, n, t: (k, n)),
        out_shape=jax.ShapeDtypeStruct((K, N), BF16),
        scratch_shapes=[pltpu.VMEM((bk, bn), F32)],
        compiler_params=_params(("parallel", "parallel", "arbitrary")),
    )(a, b)


W_IN_A = POOL_WIDTH + 3 * ATTN_WIDTH
W_IN_SHARD = (W_IN_A + N_HEADS + 2 * D_MODEL) // N_DEV
_W_IN_PIECES = ((0, W_IN_A), (W_IN_A, W_IN_A + N_HEADS), (W_IN_A + N_HEADS, W_IN_A + N_HEADS + 2 * D_MODEL))


def _w_in_segments(d):
    lo, hi = d * W_IN_SHARD, (d + 1) * W_IN_SHARD
    out = []
    for p, (a, b) in enumerate(_W_IN_PIECES):
        s, e = max(lo, a), min(hi, b)
        if s < e:
            out.append((p, s - a, s - lo, e - s))
    return out


def _w_in_pieces(gathered):
    tm = ROW_TILE // 2

    def body(g_ref, wa_ref, wf_ref, wg_ref):
        outs = (wa_ref, wf_ref, wg_ref)
        wf_ref[...] = jnp.zeros_like(wf_ref)
        for d in range(N_DEV):
            for p, at, frm, n in _w_in_segments(d):
                outs[p][:, at : at + n] = g_ref[d, :, frm : frm + n]

    return pl.pallas_call(
        body,
        name="w_in_pieces",
        grid=(D_MODEL // tm,),
        in_specs=[pl.BlockSpec((N_DEV, tm, W_IN_SHARD), lambda i: (0, i, 0))],
        out_specs=[pl.BlockSpec((tm, W_IN_A), lambda i: (i, 0)), pl.BlockSpec((tm, FL_PAD), lambda i: (i, 0)), pl.BlockSpec((tm, 2 * D_MODEL), lambda i: (i, 0))],
        out_shape=[
            jax.ShapeDtypeStruct((D_MODEL, W_IN_A), gathered.dtype),
            jax.ShapeDtypeStruct((D_MODEL, FL_PAD), gathered.dtype),
            jax.ShapeDtypeStruct((D_MODEL, 2 * D_MODEL), gathered.dtype),
        ],
        compiler_params=_params(("parallel",)),
    )(gathered)


def _dw_in(h, du, dq, dk, dv, dfl, dgates):
    T = h.shape[0]
    bt, bk = min(T, DW_TOKENS // 2), 512
    nt = T // bt
    pieces = (du, dq, dk, dv, dfl, dgates)
    offs = [0]
    for p in pieces:
        offs.append(offs[-1] + p.shape[1])

    def body(h_ref, *rest):
        refs, o_ref, acc = rest[: len(pieces)], rest[-2], rest[-1]

        @pl.when(pl.program_id(1) == 0)
        def _():
            acc[...] = jnp.zeros_like(acc)

        ht = h_ref[...].T
        for ref, at in zip(refs, offs):
            acc[:, at : at + ref.shape[1]] += _mm(ht, ref[...])

        @pl.when(pl.program_id(1) == nt - 1)
        def _():
            starts = (0, W_IN_A, W_IN_A + FL_PAD)
            for d in range(N_DEV):
                for p, at, to, n in _w_in_segments(d):
                    o_ref[d % 2, d // 2, :, to : to + n] = acc[:, starts[p] + at : starts[p] + at + n].astype(BF16)

    return pl.pallas_call(
        body,
        name="dw_in",
        grid=(D_MODEL // bk, nt),
        in_specs=[pl.BlockSpec((bt, bk), lambda k, t: (t, k))] + [pl.BlockSpec((bt, p.shape[1]), lambda k, t: (t, 0)) for p in pieces],
        out_specs=pl.BlockSpec((2, 4, bk, W_IN_SHARD), lambda k, t: (0, 0, k, 0)),
        out_shape=jax.ShapeDtypeStruct((2, 4, D_MODEL, W_IN_SHARD), BF16),
        scratch_shapes=[pltpu.VMEM((bk, offs[-1]), F32)],
        compiler_params=_params(("parallel", "arbitrary")),
    )(h, *pieces)


def _position():
    return lax.axis_index("x"), lax.axis_index("y"), lax.axis_index("c")


_HBM = pl.BlockSpec(memory_space=pl.ANY)


def _all_gather(blocks, name):
    n = len(blocks)

    def body(*refs):
        xs, outs = refs[:n], refs[n : 2 * n]
        send_sems, recv_sems, local_sems = refs[2 * n :]
        x, y, c = _position()
        me, sibling = (x, y, c), (x, y, 1 - c)
        chips = [(1 - x, y), (x, 1 - y), (1 - x, 1 - y)]

        def rows(a, px, py, pc):
            return outs[a].at[4 * px + 2 * py + pc]

        def copy(a, k, blk, to, src=None):
            return pltpu.make_async_remote_copy(
                src_ref=rows(a, *blk) if src is None else src, dst_ref=rows(a, *blk),
                send_sem=send_sems.at[7 * a + k], recv_sem=recv_sems.at[7 * a + k], device_id=to, device_id_type=MESH,
            )

        mine = [pltpu.make_async_copy(xs[a], rows(a, *me), local_sems.at[a]) for a in range(n)]
        for cp in mine:
            cp.start()
        first = []
        for a in range(n):
            first.append(copy(a, 0, me, sibling, src=xs[a]))
            first += [copy(a, 1 + j, me, (*chip, c), src=xs[a]) for j, chip in enumerate(chips)]
        for cp in first:
            cp.start()
        passed = []
        for j, chip in enumerate(chips):
            for a in range(n):
                copy(a, 1 + j, (*chip, c), me).wait_recv()
                passed.append(copy(a, 4 + j, (*chip, c), sibling))
                passed[-1].start()
        for a in range(n):
            copy(a, 0, sibling, me).wait_recv()
        for j, chip in enumerate(chips):
            for a in range(n):
                copy(a, 4 + j, (*chip, 1 - c), me).wait_recv()
        for cp in first + passed:
            cp.wait_send()
        for cp in mine:
            cp.wait()

    return pl.pallas_call(
        body,
        name=name,
        out_shape=[jax.ShapeDtypeStruct((N_DEV, *b.shape), b.dtype) for b in blocks],
        in_specs=[_HBM] * n,
        out_specs=[_HBM] * n,
        scratch_shapes=[pltpu.SemaphoreType.DMA((7 * n,)), pltpu.SemaphoreType.DMA((7 * n,)), pltpu.SemaphoreType.DMA((n,))],
    )(*blocks)


_SEM = pl.BlockSpec(memory_space=pltpu.SEMAPHORE)
_HBM_ONLY = pl.BlockSpec(memory_space=pltpu.HBM)
_SIDE_EFFECT = pltpu.SideEffectType.DATAFLOW_SIDE_EFFECTING


def _peer(x, y, c, k):
    return (1 - x if k & 4 else x, 1 - y if k & 2 else y, 1 - c if k & 1 else c)


_PEER_BITS = {"gather": range(1, N_DEV), "scatter": range(1, N_DEV), "chips": (4, 2, 6)}
_LAND_SLOTS = {"gather": N_DEV, "scatter": N_DEV, "chips": 3}


def _exchange_copies(src_refs, land_refs, send_sems, recv_sems, pattern, receive_side):
    x, y, c = _position()
    me = 4 * x + 2 * y + c
    bits = _PEER_BITS[pattern]
    cps = []
    for j, k in enumerate(bits):
        px, py, pc = _peer(x, y, c, k)
        peer = 4 * px + 2 * py + pc
        for a, (src, land) in enumerate(zip(src_refs, land_refs)):
            if pattern == "chips":
                s, slot = src.at[2 * px + py], j
            else:
                s, slot = (src if pattern == "gather" else src.at[peer]), (peer if receive_side else me)
            cps.append(pltpu.make_async_remote_copy(
                src_ref=s, dst_ref=land.at[slot],
                send_sem=send_sems.at[len(bits) * a + j], recv_sem=recv_sems.at[len(bits) * a + j],
                device_id=(px, py, pc), device_id_type=MESH,
            ))
    return cps


def _exchange_start(srcs, after, name, pattern):
    n = len(srcs)
    m = len(_PEER_BITS[pattern])
    lands = [jax.ShapeDtypeStruct((_LAND_SLOTS[pattern], *s.shape[-2:]), s.dtype) for s in srcs]

    def body(*refs):
        src_refs, land_refs = refs[1 : 1 + n], refs[1 + n : 1 + 2 * n]
        send_sems, recv_sems = refs[1 + 2 * n], refs[2 + 2 * n]
        token = refs[-1]
        for cp in _exchange_copies(src_refs, land_refs, send_sems, recv_sems, pattern, receive_side=False):
            cp.start()
        token[...] = jnp.zeros_like(token)

    hbm = lambda t: pltpu.with_memory_space_constraint(t, pltpu.HBM)
    out = pl.pallas_call(
        body,
        name=name,
        out_shape=(
            pltpu.SemaphoreType.DMA((m * n,)), pltpu.SemaphoreType.DMA((m * n,)),
            *[pltpu.HBM(s.shape, s.dtype) for s in srcs], *[pltpu.HBM(l.shape, l.dtype) for l in lands],
            jax.ShapeDtypeStruct((8, LANES), F32),
        ),
        in_specs=(_HBM, *[_HBM_ONLY] * (2 * n)),
        out_specs=(_SEM, _SEM, *[_HBM_ONLY] * (2 * n), pl.BlockSpec(memory_space=pltpu.VMEM)),
        input_output_aliases={1 + i: 2 + i for i in range(2 * n)},
        compiler_params=pltpu.CompilerParams(has_side_effects=_SIDE_EFFECT),
    )(after, *[hbm(s) for s in srcs], *[hbm(lax.empty(l.shape, l.dtype)) for l in lands])
    return out[0], out[1], out[2 : 2 + n], out[2 + n : 2 + 2 * n], out[-1]


def _exchange_wait(send_sems, recv_sems, srcs, lands, after, name, pattern):
    n = len(srcs)

    def body(*refs):
        src_refs, land_refs = refs[:n], refs[n : 2 * n]
        for cp in _exchange_copies(src_refs, land_refs, refs[2 * n], refs[2 * n + 1], pattern, receive_side=True):
            cp.wait_send()
            cp.wait_recv()

    out = pl.pallas_call(
        body,
        name=name,
        out_shape=(*[pltpu.HBM(s.shape, s.dtype) for s in srcs], *[pltpu.HBM(l.shape, l.dtype) for l in lands]),
        in_specs=(*[_HBM_ONLY] * (2 * n), _SEM, _SEM, _HBM),
        out_specs=tuple([_HBM_ONLY] * (2 * n)),
        input_output_aliases={i: i for i in range(2 * n)},
        compiler_params=pltpu.CompilerParams(has_side_effects=_SIDE_EFFECT),
    )(*srcs, *lands, send_sems, recv_sems, after)
    return out[:n], out[n:]


def _sibling_exchange(sends):
    n = len(sends)

    def body(*refs):
        srcs, dsts = refs[:n], refs[n : 2 * n]
        send_sems, recv_sems = refs[2 * n :]
        x, y, c = _position()
        cps = [
            pltpu.make_async_remote_copy(
                src_ref=srcs[a].at[1 - c], dst_ref=dsts[a], send_sem=send_sems.at[a], recv_sem=recv_sems.at[a],
                device_id=(x, y, 1 - c), device_id_type=MESH,
            )
            for a in range(n)
        ]
        for cp in cps:
            cp.start()
        for cp in cps:
            cp.wait()

    return pl.pallas_call(
        body,
        name="rs_sibling",
        out_shape=[jax.ShapeDtypeStruct(s.shape[1:], s.dtype) for s in sends],
        in_specs=[_HBM] * n,
        out_specs=[_HBM] * n,
        scratch_shapes=[pltpu.SemaphoreType.DMA((n,)), pltpu.SemaphoreType.DMA((n,))],
    )(*sends)


def _rows_tile(r):
    return ROW_TILE if r % ROW_TILE == 0 else r


def _pair_sum(send, got, core, name):
    _, _, r, c = send.shape
    br = _rows_tile(r)

    def body(core_ref, a_ref, b_ref, o_ref):
        o_ref[...] = (a_ref[...].astype(F32) + b_ref[...].astype(F32)).astype(o_ref.dtype)

    return pl.pallas_call(
        body,
        name=name,
        grid_spec=pltpu.PrefetchScalarGridSpec(
            num_scalar_prefetch=1,
            grid=(4, r // br),
            in_specs=[
                pl.BlockSpec((None, None, br, c), lambda n, i, core: (core[0], n, i, 0)),
                pl.BlockSpec((None, br, c), lambda n, i, core: (n, i, 0)),
            ],
            out_specs=pl.BlockSpec((None, br, c), lambda n, i, core: (n, i, 0)),
        ),
        out_shape=jax.ShapeDtypeStruct((4, r, c), send.dtype),
        compiler_params=_params(("parallel", "parallel")),
    )(core, send, got)


def _adamw(w, g, m, v):
    m = ADAM_B1 * m + (1.0 - ADAM_B1) * g
    v = ADAM_B2 * v + (1.0 - ADAM_B2) * (g * g)
    m_hat = m / (1.0 - ADAM_B1 ** ADAM_STEP)
    v_hat = v / (1.0 - ADAM_B2 ** ADAM_STEP)
    delta = -ADAM_LR * (m_hat / (jnp.sqrt(v_hat) + ADAM_EPS) + ADAM_WD * w)
    return delta, m, v


def _shard_update(send, got, recv, w, m, v, pos, name):
    _, r, c = w.shape
    br = _rows_tile(r)

    def body(pos_ref, a_ref, b_ref, r_ref, w_ref, m_ref, v_ref, g_ref, d_ref, nm_ref, nv_ref):
        g = a_ref[...].astype(F32) + b_ref[...].astype(F32)
        for n in range(3):
            g = g + r_ref[n].astype(F32)
        g_ref[...] = g
        d_ref[...], nm_ref[...], nv_ref[...] = _adamw(w_ref[...], g, m_ref[...], v_ref[...])

    own = pl.BlockSpec((None, br, c), lambda i, pos: (0, i, 0))
    return pl.pallas_call(
        body,
        name=name,
        grid_spec=pltpu.PrefetchScalarGridSpec(
            num_scalar_prefetch=1,
            grid=(r // br,),
            in_specs=[
                pl.BlockSpec((None, None, br, c), lambda i, pos: (pos[0], pos[1], i, 0)),
                pl.BlockSpec((None, br, c), lambda i, pos: (pos[1], i, 0)),
                pl.BlockSpec((3, br, c), lambda i, pos: (0, i, 0)),
                own, own, own,
            ],
            out_specs=[own, own, own, own],
        ),
        out_shape=[jax.ShapeDtypeStruct((1, r, c), F32)] * 4,
        compiler_params=_params(("parallel",)),
    )(pos, send, got, recv, w, m, v)


def _shard_update_direct(parts, chunks, w, m, v, me, name):
    _, r, c = w.shape
    br = _rows_tile(r)

    def body(me_ref, p_ref, own_ref, w_ref, m_ref, v_ref, g_ref, d_ref, nm_ref, nv_ref):
        g = None
        for n in range(N_DEV):
            part = jnp.where(me_ref[0] == n, own_ref[...], p_ref[n]).astype(F32)
            g = part if g is None else g + part
        g_ref[...] = g
        d_ref[...], nm_ref[...], nv_ref[...] = _adamw(w_ref[...], g, m_ref[...], v_ref[...])

    shard = pl.BlockSpec((None, br, c), lambda i, me: (0, i, 0))
    return pl.pallas_call(
        body,
        name=name,
        grid_spec=pltpu.PrefetchScalarGridSpec(
            num_scalar_prefetch=1,
            grid=(r // br,),
            in_specs=[
                pl.BlockSpec((N_DEV, br, c), lambda i, me: (0, i, 0)),
                pl.BlockSpec((None, br, c), lambda i, me: (me[0], i, 0)),
                shard, shard, shard,
            ],
            out_specs=[shard, shard, shard, shard],
        ),
        out_shape=[jax.ShapeDtypeStruct((1, r, c), F32)] * 4,
        compiler_params=_params(("parallel",)),
    )(me, parts, chunks, w, m, v)


def _small_update(parts, w, m, v):
    R = w.shape[0]

    def body(p_ref, w_ref, m_ref, v_ref, g_ref, d_ref, nm_ref, nv_ref):
        g = p_ref[0]
        for n in range(1, N_DEV):
            g = g + p_ref[n]
        g_ref[...] = g
        d_ref[...], nm_ref[...], nv_ref[...] = _adamw(w_ref[...], g, m_ref[...], v_ref[...])

    return pl.pallas_call(
        body,
        name="small_update",
        out_shape=[jax.ShapeDtypeStruct((R, LANES), F32)] * 4,
        compiler_params=pltpu.CompilerParams(vmem_limit_bytes=VMEM_LIMIT),
    )(parts, w, m, v)


_SHARD_AXIS = (1, 1, 1, 0, 0, 0, 0)
_TRANSPOSED = (False, False, False, False, True, True, False)


def _full_from_gathered(t, axis):
    if axis == 0:
        return t.reshape(N_DEV * t.shape[1], t.shape[2])
    return jnp.concatenate([t[d] for d in range(N_DEV)], axis=1)


def _chunks_from_cols(t):
    c = t.shape[1] // N_DEV
    return jnp.stack([t[:, d * c : (d + 1) * c] for d in range(N_DEV)])


_SMALL = (("norm1_g", 8), ("norm2_g", 8), ("norm_f_g", 8), ("b_forget", 8), ("pool_scale", 8), ("pool_mix", 512))
_SMALL_ROWS = sum(r for _, r in _SMALL) + 8


def _pack_small(vals, loss_row):
    parts = []
    for (name, rows), t in zip(_SMALL, vals):
        f = t.astype(F32).reshape(-1)
        f = jnp.concatenate([f, jnp.zeros((rows * LANES - f.shape[0],), F32)]).reshape(rows, LANES)
        parts.append(f)
    parts.append(loss_row)
    return jnp.concatenate(parts, axis=0)


def _unpack_small(packed, shapes):
    out, off = [], 0
    for (name, rows), shape in zip(_SMALL, shapes):
        n = 1
        for s in shape:
            n *= s
        out.append(packed[off : off + rows].reshape(-1)[:n].reshape(shape))
        off += rows
    return out, packed[off, 0]


def _local_grads(x, tgt, g1, g2, gf, b_forget, pool_mix, pool_scale, w_in, fwd_token, out_weights, ffn_weights, ffn_grads_out, out_grads_out, in_grads_out, small_grads_out):
    n_seq, S, _ = x.shape
    T = n_seq * S
    x2 = x.reshape(T, D_MODEL)
    tg2 = tgt.reshape(T, D_MODEL)
    w_uqkv, w_fl, w_g = w_in
    b_pad = jnp.concatenate([b_forget.reshape(1, N_HEADS), jnp.zeros((1, FL_PAD - N_HEADS), F32)], axis=1)
    mix_b = pool_mix.reshape(len(POOL_WINDOWS), GROUP_DIM, GROUP_DIM).astype(BF16)
    scale = pool_scale.reshape(1, POOL_WIDTH)
    g1 = g1.reshape(1, D_MODEL)
    g2 = g2.reshape(1, D_MODEL)
    gf = gf.reshape(1, D_MODEL)

    h, u, qkv, fl, gates = _in_proj(x2, g1, w_uqkv, w_fl, w_g, fwd_token)
    fcol = _forget_fwd(fl, b_pad, n_seq, S)
    pm, p2, p3 = _pool_fwd(u, mix_b, scale, n_seq, S)
    a, lse = _attn_fwd(qkv, fcol, n_seq, S)
    w_po, w_ao, w_out = out_weights(a)
    merged, x1, attn_y, pool_y = _mix_out(a, p3, gates, x2, w_ao, w_po, w_out)
    w_gate_t, w_up_t, w_down = ffn_weights(x1)
    h2, gate, up, act, dx2, loss_rows, dgf = _ffn_fwd(x1, g2, gf, tg2, w_gate_t, w_up_t, w_down)

    dgate, dup, dx1, dg2 = _ffn_bwd(dx2, gate, up, x1, g2, w_gate_t, w_up_t, w_down)
    bwd_token = ffn_grads_out(_matmul_tn(dgate, h2, "dw_ffn_gate"), _matmul_tn(dup, h2, "dw_ffn_up"), _matmul_tn(act, dx2, "dw_ffn_down"))
    dgates, dpy, day, da, dp2, dscale = _mix_bwd(dx1, gates, pool_y, attn_y, p2, scale, w_out, w_ao, w_po, bwd_token)
    out_token = out_grads_out(_matmul_tn(p3, dpy, "dw_pool_out"), _matmul_tn(a, day, "dw_attn_out"), _matmul_tn(merged, dx1, "dw_out"))
    du, dmix = _pool_bwd(dp2, pm, mix_b, out_token, n_seq, S)
    dq, dk, dv, dfk, dfq = _attn_bwd(qkv, da, a, fcol, lse, n_seq, S)
    dfl, db = _forget_bwd(dfk, dfq, fl, b_pad, n_seq, S)
    in_token = in_grads_out(_dw_in(h, du, dq, dk, dv, dfl, dgates))
    dx, dg1 = _in_proj_bwd(du, dq, dk, dv, dfl, dgates, x2, dx1, g1, w_uqkv, w_fl, w_g, in_token)
    small_grads_out((dg1, dg2, dgf, db[:, :N_HEADS], dscale, dmix), loss_rows)
    return dx.reshape(n_seq, S, D_MODEL)


def kernel(x, norm1_g, w_in, b_forget, pool_mix, pool_scale, w_pool_out, w_attn_out, w_out, norm2_g, w_ffn_gate, w_ffn_up, w_ffn_down, norm_f_g, loss_target, m_norm1_g, m_w_in, m_b_forget, m_pool_mix, m_pool_scale, m_w_pool_out, m_w_attn_out, m_w_out, m_norm2_g, m_w_ffn_gate, m_w_ffn_up, m_w_ffn_down, m_norm_f_g, v_norm1_g, v_w_in, v_b_forget, v_pool_mix, v_pool_scale, v_w_pool_out, v_w_attn_out, v_w_out, v_norm2_g, v_w_ffn_gate, v_w_ffn_up, v_w_ffn_down, v_norm_f_g):
    names = ("w_in", "w_pool_out", "w_attn_out", "w_out", "w_ffn_gate", "w_ffn_up", "w_ffn_down")
    w_sh = (w_in, w_pool_out, w_attn_out, w_out, w_ffn_gate, w_ffn_up, w_ffn_down)
    m_sh = (m_w_in, m_w_pool_out, m_w_attn_out, m_w_out, m_w_ffn_gate, m_w_ffn_up, m_w_ffn_down)
    v_sh = (v_w_in, v_w_pool_out, v_w_attn_out, v_w_out, v_w_ffn_gate, v_w_ffn_up, v_w_ffn_down)

    cx, cy, cc = _position()
    me = 4 * cx + 2 * cy + cc
    def stored(t, transposed):
        return jnp.transpose(t, (0, 2, 1)) if transposed else t

    w_sh, m_sh, v_sh = ([stored(t, tr) for t, tr in zip(ts, _TRANSPOSED)] for ts in (w_sh, m_sh, v_sh))
    shards = [w[0].astype(BF16) for w in w_sh]
    (gathered_in,) = _all_gather(shards[:1], "w_in_all_gather")
    out_sems = _exchange_start(shards[1:4], gathered_in, "out_weights_gather_start", "gather")
    ffn_sems = _exchange_start(shards[4:], out_sems[4], "ffn_weights_gather_start", "gather")
    no_order = jnp.zeros((8, LANES), F32)

    def with_own(lands, own):
        return [lax.dynamic_update_slice(l, o[None], (me, 0, 0)) for l, o in zip(lands, own)]

    def gathered_weights(sems, axes, name):
        def wait(after):
            send_sems, recv_sems, srcs, lands, _ = sems
            srcs, lands = _exchange_wait(send_sems, recv_sems, srcs, lands, after, name, "gather")
            return [_full_from_gathered(t, axis) for t, axis in zip(with_own(lands, srcs), axes)]

        return wait

    started = {}

    def scatter_grads(key, name):
        def start(*whole_grads):
            chunks = [
                _chunks_from_cols(t) if axis == 1 else t.reshape(N_DEV, -1, t.shape[1])
                for t, axis in zip(whole_grads, _SHARD_AXIS[key])
            ]
            started[key] = _exchange_start(chunks, no_order, name, "scatter")
            return started[key][4]

        return start

    def gather_small(small, loss_rows):
        started["small"] = _exchange_start([_pack_small(small, loss_rows)], no_order, "small_grads_gather_start", "gather")

    core = jnp.reshape(cc, (1,)).astype(jnp.int32)
    pos = jnp.stack([cc, 2 * cx + cy]).astype(jnp.int32)

    def reduce_w_in(send_in):
        (got_in,) = _sibling_exchange([send_in])
        pair_in = _pair_sum(send_in, got_in, core, "pair_sum_w_in")
        started["in"] = (send_in, got_in, _exchange_start([pair_in], no_order, "w_in_grads_chips_start", "chips"))
        return started["in"][2][4]

    ffn, out = slice(4, 7), slice(1, 4)
    grad_x = _local_grads(
        x, loss_target, norm1_g, norm2_g, norm_f_g, b_forget, pool_mix, pool_scale, _w_in_pieces(gathered_in), ffn_sems[4],
        gathered_weights(out_sems, _SHARD_AXIS[out], "out_weights_gather_wait"),
        gathered_weights(ffn_sems, _SHARD_AXIS[ffn], "ffn_weights_gather_wait"),
        scatter_grads(ffn, "ffn_grads_scatter_start"), scatter_grads(out, "out_grads_scatter_start"), reduce_w_in, gather_small,
    )
    send_in, got_in, chip_sems = started["in"]

    def scattered_updates(key, after, name):
        send_sems, recv_sems, srcs, lands, _ = started[key]
        srcs, lands = _exchange_wait(send_sems, recv_sems, srcs, lands, after, name, "scatter")
        return [
            _shard_update_direct(p, s, w, m, v, jnp.reshape(me, (1,)).astype(jnp.int32), "update_" + n)
            for p, s, w, m, v, n in zip(lands, srcs, w_sh[key], m_sh[key], v_sh[key], names[key])
        ]

    updates_out = scattered_updates(out, grad_x, "out_grads_scatter_wait")
    updates_ffn = scattered_updates(ffn, grad_x, "ffn_grads_scatter_wait")

    small_w = (norm1_g, norm2_g, norm_f_g, b_forget, pool_scale, pool_mix)
    small_m = (m_norm1_g, m_norm2_g, m_norm_f_g, m_b_forget, m_pool_scale, m_pool_mix)
    small_v = (v_norm1_g, v_norm2_g, v_norm_f_g, v_b_forget, v_pool_scale, v_pool_mix)
    zero_row = jnp.zeros((8, LANES), F32)
    send_sems, recv_sems, srcs, lands, _ = started["small"]
    srcs, lands = _exchange_wait(send_sems, recv_sems, srcs, lands, updates_ffn[-1][0], "small_grads_gather_wait", "gather")
    (parts,) = with_own(lands, srcs)
    g_s, d_s, nm_s, nv_s = _small_update(parts, _pack_small(small_w, zero_row), _pack_small(small_m, zero_row), _pack_small(small_v, zero_row))

    send_sems, recv_sems, srcs, lands, _ = chip_sems
    _, (recv_in,) = _exchange_wait(send_sems, recv_sems, srcs, lands, g_s, "w_in_grads_chips_wait", "chips")
    update_in = _shard_update(send_in, got_in, recv_in, w_in, m_w_in, v_w_in, pos, "update_w_in")
    g_w, d_w, nm_w, nv_w = zip(*(
        [stored(t, tr) for t in u] for u, tr in zip([update_in] + updates_out + updates_ffn, _TRANSPOSED)
    ))
    shapes = [t.shape for t in small_w]
    (g1, g2, gf, gb, gsc, gmix), loss = _unpack_small(g_s, shapes)
    (d1, d2, df, db_, dsc, dmx), _ = _unpack_small(d_s, shapes)
    (m1, m2, mf, mb, msc, mmx), _ = _unpack_small(nm_s, shapes)
    (v1, v2, vf, vb, vsc, vmx), _ = _unpack_small(nv_s, shapes)

    def ordered(n1, win, b, mix, sc, wpo, wao, wout, n2, wg, wu, wd, nf):
        return (n1, win, b, mix, sc, wpo, wao, wout, n2, wg, wu, wd, nf)

    grads = ordered(g1, g_w[0], gb, gmix, gsc, g_w[1], g_w[2], g_w[3], g2, g_w[4], g_w[5], g_w[6], gf)
    deltas = ordered(d1, d_w[0], db_, dmx, dsc, d_w[1], d_w[2], d_w[3], d2, d_w[4], d_w[5], d_w[6], df)
    new_m = ordered(m1, nm_w[0], mb, mmx, msc, nm_w[1], nm_w[2], nm_w[3], m2, nm_w[4], nm_w[5], nm_w[6], mf)
    new_v = ordered(v1, nv_w[0], vb, vmx, vsc, nv_w[1], nv_w[2], nv_w[3], v2, nv_w[4], nv_w[5], nv_w[6], vf)
    return (loss, grad_x, *grads, *deltas, *new_m, *new_v)
```

```python
import functools

import jax
import jax.numpy as jnp
from jax import lax
from jax.experimental import pallas as pl
from jax.experimental.pallas import tpu as pltpu

F32 = jnp.float32
BF16 = jnp.bfloat16
MESH = pl.DeviceIdType.MESH

D_MODEL = 1024
POOL_WINDOWS = (2, 4, 8, 16)
POOL_WIDTH = 512
GROUP_DIM = 128
ATTN_WIDTH = 512
HEAD_DIM = 64
N_HEADS = 8
N_PAIRS = 4
D_FF = 2816
RMS_EPS = 1e-6
N_DEV = 8
LANES = 128
FL_PAD = 128

ADAM_LR = 0.001
ADAM_B1 = 0.9
ADAM_B2 = 0.999
ADAM_EPS = 1e-08
ADAM_WD = 0.01
ADAM_STEP = 10

VMEM_LIMIT = 56 * 1024 * 1024
VMEM_LIMIT_MAX = 60 * 1024 * 1024
ROW_TILE = 512
ATTN_BLOCK = 512
FF_CHUNK = 256
FF_ROW_TILE = 512
DW_TOKENS = 2048


def _mm(a, b):
    return jnp.dot(a, b, preferred_element_type=F32)


def _mm_nt(a, b):
    return lax.dot_general(a, b, (((1,), (1,)), ((), ())), preferred_element_type=F32)


def _mm_tn(a, b):
    return lax.dot_general(a, b, (((0,), (0,)), ((), ())), preferred_element_type=F32)


def _sigmoid(x):
    return 1.0 / (1.0 + jnp.exp(-x))


def _params(sem, vmem=VMEM_LIMIT):
    return pltpu.CompilerParams(dimension_semantics=sem, vmem_limit_bytes=vmem)


def _const_spec(shape):
    nd = len(shape)
    return pl.BlockSpec(shape, lambda *_: (0,) * nd, pipeline_mode=pl.Buffered(1))


def _rms_fwd(x, g):
    r = lax.rsqrt(jnp.mean(x * x, axis=-1, keepdims=True) + RMS_EPS)
    xh = x * r
    return xh * g, xh, r


def _rms_bwd(dy, xh, r, g):
    dxh = dy * g
    dx = r * (dxh - xh * jnp.mean(dxh * xh, axis=-1, keepdims=True))
    return dx, dy * xh


def _in_proj(x, g1, w_uqkv, w_fl, w_g, token):
    T = x.shape[0]
    tm = ROW_TILE

    def body(x_ref, g_ref, wa_ref, wf_ref, wg_ref, token_ref, h_ref, u_ref, qkv_ref, fl_ref, gt_ref):
        h, _, _ = _rms_fwd(x_ref[...], g_ref[...])
        hb = h.astype(BF16)
        h_ref[...] = hb
        z = _mm(hb, wa_ref[...])
        u_ref[...] = z[:, :POOL_WIDTH]
        qkv_ref[...] = z[:, POOL_WIDTH:].astype(BF16)
        fl_ref[...] = _mm(hb, wf_ref[...])
        gt_ref[...] = _mm(hb, wg_ref[...]).astype(BF16)

    row = lambda n: pl.BlockSpec((tm, n), lambda i: (i, 0))
    return pl.pallas_call(
        body,
        name="in_proj",
        grid=(T // tm,),
        in_specs=[row(D_MODEL), _const_spec((1, D_MODEL)), _const_spec(w_uqkv.shape), _const_spec(w_fl.shape), _const_spec(w_g.shape), _HBM],
        out_specs=[row(D_MODEL), row(POOL_WIDTH), row(3 * ATTN_WIDTH), row(FL_PAD), row(2 * D_MODEL)],
        out_shape=[
            jax.ShapeDtypeStruct((T, D_MODEL), BF16),
            jax.ShapeDtypeStruct((T, POOL_WIDTH), F32),
            jax.ShapeDtypeStruct((T, 3 * ATTN_WIDTH), BF16),
            jax.ShapeDtypeStruct((T, FL_PAD), F32),
            jax.ShapeDtypeStruct((T, 2 * D_MODEL), BF16),
        ],
        compiler_params=_params(("parallel",)),
    )(x, g1, w_uqkv, w_fl, w_g, token)


def _log_sigmoid(x):
    return jnp.minimum(x, 0.0) - jnp.log(1.0 + jnp.exp(-jnp.abs(x)))


def _forget_fwd(fl, b_pad, n_seq, S):
    def body(fl_ref, b_ref, fcol_ref):
        lf = _log_sigmoid(fl_ref[...] + b_ref[...])
        t = lf.T
        lane = lax.broadcasted_iota(jnp.int32, t.shape, 1)
        k = 1
        while k < S:
            t = t + jnp.where(lane >= k, pltpu.roll(t, k, 1), 0.0)
            k *= 2
        fcol_ref[...] = t.T

    return pl.pallas_call(
        body,
        name="forget_fwd",
        grid=(n_seq,),
        in_specs=[pl.BlockSpec((S, FL_PAD), lambda s: (s, 0)), _const_spec((1, FL_PAD))],
        out_specs=pl.BlockSpec((S, FL_PAD), lambda s: (s, 0)),
        out_shape=jax.ShapeDtypeStruct((n_seq * S, FL_PAD), F32),
        compiler_params=_params(("parallel",)),
    )(fl, b_pad)


def _window_pick(g, v2, v4, v8, v16):
    return jnp.where(g == 0, v2, jnp.where(g == 1, v4, jnp.where(g == 2, v8, v16)))


def _pool_fwd(u, mix_b, scale, n_seq, S):
    T = n_seq * S

    def body(u_ref, mix_ref, sc_ref, pm_ref, p2_ref, p3_ref):
        g = pl.program_id(1)
        uu = u_ref[...]
        row = lax.broadcasted_iota(jnp.int32, uu.shape, 0)

        def back(a, k):
            return jnp.where(row >= k, pltpu.roll(a, k, 0), 0.0)

        s2 = uu + back(uu, 1)
        s4 = s2 + back(s2, 2)
        s8 = s4 + back(s4, 4)
        s16 = s8 + back(s8, 8)
        w = _window_pick(g, 2.0, 4.0, 8.0, 16.0)
        cnt = jnp.minimum((row + 1).astype(F32), w)
        pm = _window_pick(g, s2, s4, s8, s16) / cnt - uu
        pmb = pm.astype(BF16)
        pm_ref[...] = pmb
        p2 = _mm(pmb, mix_ref[...])
        p2_ref[...] = p2
        p3_ref[...] = (p2 * sc_ref[...]).astype(BF16)

    grp = pl.BlockSpec((S, GROUP_DIM), lambda s, g: (s, g))
    return pl.pallas_call(
        body,
        name="pool_fwd",
        grid=(n_seq, len(POOL_WINDOWS)),
        in_specs=[
            grp,
            pl.BlockSpec((None, GROUP_DIM, GROUP_DIM), lambda s, g: (g, 0, 0)),
            pl.BlockSpec((1, GROUP_DIM), lambda s, g: (0, g)),
        ],
        out_specs=[grp, grp, grp],
        out_shape=[
            jax.ShapeDtypeStruct((T, POOL_WIDTH), BF16),
            jax.ShapeDtypeStruct((T, POOL_WIDTH), F32),
            jax.ShapeDtypeStruct((T, POOL_WIDTH), BF16),
        ],
        compiler_params=_params(("parallel", "parallel")),
    )(u, mix_b, scale)


def _split3(v):
    hi = v.astype(BF16).astype(F32)
    r = v - hi
    mid = r.astype(BF16).astype(F32)
    lo = (r - mid).astype(BF16).astype(F32)
    return hi, mid, lo


def _augment(xp, hh, first, second):
    lane = lax.broadcasted_iota(jnp.int32, (1, LANES), 1)
    head = (lane >= HEAD_DIM * hh) & (lane < HEAD_DIM * (hh + 1))
    b = HEAD_DIM * (1 - hh)
    out = jnp.where(head, xp.astype(F32), 0.0)
    for n, col in enumerate(tuple(first) + tuple(second)):
        out = jnp.where(lane == b + n, col, out)
    return out.astype(BF16)


def _attn_fwd(qkv, fcol, n_seq, S):
    T = n_seq * S
    tb = ATTN_BLOCK
    nq = S // tb
    scale = HEAD_DIM ** -0.5

    def body(q_ref, k_ref, v_ref, fc_ref, o_ref, st_ref, qa_sc, ka_sc, m_sc, l_sc, acc_sc):
        i = pl.program_id(1)
        lane = lax.broadcasted_iota(jnp.int32, (1, LANES), 1)
        low = lane < HEAD_DIM
        ones = (1.0, 1.0, 1.0)

        @pl.when(i == 0)
        def _():
            def rows_ka(r, carry):
                r0 = pl.multiple_of(r * tb, tb)
                for h in range(N_HEADS):
                    kp = k_ref[pl.ds(r0, tb), (h // 2) * LANES : (h // 2 + 1) * LANES] * scale
                    fk = fc_ref[pl.ds(r0, tb), h : h + 1]
                    ka_sc[h, pl.ds(r0, tb), :] = _augment(kp, h % 2, ones, _split3(-fk))
                return carry

            lax.fori_loop(0, nq, rows_ka, 0)

        q0 = pl.multiple_of(i * tb, tb)
        for h in range(N_HEADS):
            qp = q_ref[:, (h // 2) * LANES : (h // 2 + 1) * LANES]
            qa_sc[h] = _augment(qp, h % 2, _split3(fc_ref[pl.ds(q0, tb), h : h + 1]), ones)
        m_sc[...] = jnp.full(m_sc.shape, -jnp.inf, F32)
        l_sc[...] = jnp.zeros_like(l_sc)
        acc_sc[...] = jnp.zeros_like(acc_sc)
        causal = lax.broadcasted_iota(jnp.int32, (tb, tb), 1) <= lax.broadcasted_iota(jnp.int32, (tb, tb), 0)

        def step(j, masked):
            c0 = pl.multiple_of(j * tb, tb)
            for p in range(N_PAIRS):
                vb = v_ref[pl.ds(c0, tb), p * LANES : (p + 1) * LANES]
                pv, al = [], []
                for hh in range(2):
                    h = 2 * p + hh
                    s = _mm_nt(qa_sc[h], ka_sc[h, pl.ds(c0, tb), :])
                    if masked:
                        s = jnp.where(causal, s, -jnp.inf)
                    m_old = m_sc[h]
                    m_new = jnp.maximum(m_old, jnp.max(s, axis=1, keepdims=True))
                    alpha = jnp.exp(m_old - m_new)
                    pe = jnp.exp(s - jnp.concatenate([m_new] * (tb // LANES), axis=1))
                    l_sc[h] = alpha * l_sc[h] + jnp.sum(pe, axis=1, keepdims=True)
                    m_sc[h] = m_new
                    pv.append(_mm(pe.astype(BF16), vb))
                    al.append(alpha)
                acc_sc[p] = jnp.where(low, al[0], al[1]) * acc_sc[p] + jnp.where(low, pv[0], pv[1])

        def loop_body(j, carry):
            step(j, False)
            return carry

        lax.fori_loop(0, i, loop_body, 0)
        step(i, True)
        st = jnp.zeros((tb, LANES), F32)
        for p in range(N_PAIRS):
            lp = jnp.where(low, l_sc[2 * p], l_sc[2 * p + 1])
            o_ref[:, p * LANES : (p + 1) * LANES] = (acc_sc[p] / lp).astype(BF16)
            for h in (2 * p, 2 * p + 1):
                st = jnp.where(lane == h, m_sc[h] + jnp.log(l_sc[h]), st)
        st_ref[...] = st

    return pl.pallas_call(
        body,
        name="attn_fwd",
        grid=(n_seq, nq),
        in_specs=[
            pl.BlockSpec((tb, ATTN_WIDTH), lambda s, i: (s * nq + i, 0)),
            pl.BlockSpec((S, ATTN_WIDTH), lambda s, i: (s, 1)),
            pl.BlockSpec((S, ATTN_WIDTH), lambda s, i: (s, 2)),
            pl.BlockSpec((S, LANES), lambda s, i: (s, 0)),
        ],
        out_specs=[
            pl.BlockSpec((tb, ATTN_WIDTH), lambda s, i: (s * nq + i, 0)),
            pl.BlockSpec((tb, LANES), lambda s, i: (s * nq + i, 0)),
        ],
        out_shape=[jax.ShapeDtypeStruct((T, ATTN_WIDTH), BF16), jax.ShapeDtypeStruct((T, LANES), F32)],
        scratch_shapes=[
            pltpu.VMEM((N_HEADS, tb, LANES), BF16),
            pltpu.VMEM((N_HEADS, S, LANES), BF16),
            pltpu.VMEM((N_HEADS, tb, LANES), F32),
            pltpu.VMEM((N_HEADS, tb, LANES), F32),
            pltpu.VMEM((N_PAIRS, tb, LANES), F32),
        ],
        compiler_params=_params(("parallel", "arbitrary")),
    )(qkv, qkv, qkv, fcol)


def _mix_out(a, p3, gates, x, w_ao, w_po, w_out):
    T = x.shape[0]
    tm = ROW_TILE

    def body(a_ref, p3_ref, gt_ref, x_ref, wao_ref, wpo_ref, wout_ref, mg_ref, x1_ref, ay_ref, py_ref):
        ay = _mm(a_ref[...], wao_ref[...])
        py = _mm(p3_ref[...], wpo_ref[...])
        ay_ref[...] = ay.astype(BF16)
        py_ref[...] = py.astype(BF16)
        sp = _sigmoid(gt_ref[:, :D_MODEL].astype(F32))
        sa = _sigmoid(gt_ref[:, D_MODEL:].astype(F32))
        mb = (sp * py + sa * ay).astype(BF16)
        mg_ref[...] = mb
        x1_ref[...] = x_ref[...] + _mm(mb, wout_ref[...])

    row = lambda n: pl.BlockSpec((tm, n), lambda i: (i, 0))
    return pl.pallas_call(
        body,
        name="mix_out",
        grid=(T // tm,),
        in_specs=[
            row(ATTN_WIDTH), row(POOL_WIDTH), row(2 * D_MODEL), row(D_MODEL),
            _const_spec(w_ao.shape), _const_spec(w_po.shape), _const_spec(w_out.shape),
        ],
        out_specs=[row(D_MODEL), row(D_MODEL), row(D_MODEL), row(D_MODEL)],
        out_shape=[
            jax.ShapeDtypeStruct((T, D_MODEL), BF16), jax.ShapeDtypeStruct((T, D_MODEL), F32),
            jax.ShapeDtypeStruct((T, D_MODEL), BF16), jax.ShapeDtypeStruct((T, D_MODEL), BF16),
        ],
        compiler_params=_params(("parallel",)),
    )(a, p3, gates, x, w_ao, w_po, w_out)


def _ffn_fwd(x1, g2, gf, tgt, w_gate_t, w_up_t, w_down):
    T = x1.shape[0]
    tm = min(T, FF_ROW_TILE)
    nt = T // tm
    nc = D_FF // FF_CHUNK

    def body(x1_ref, g2_ref, gf_ref, tg_ref, wg_ref, wu_ref, wd_ref, h2_ref, gate_ref, up_ref, act_ref, dx2_ref, loss_ref, dgf_ref):
        x1v = x1_ref[...]
        h2, _, _ = _rms_fwd(x1v, g2_ref[...])
        h2b = h2.astype(BF16)
        h2_ref[...] = h2b
        for c in range(nc):
            sl = slice(c * FF_CHUNK, (c + 1) * FF_CHUNK)
            gate = _mm_nt(h2b, wg_ref[sl, :])
            up = _mm_nt(h2b, wu_ref[sl, :])
            gate_ref[:, sl] = gate.astype(BF16)
            up_ref[:, sl] = up.astype(BF16)
            act_ref[:, sl] = (gate * _sigmoid(gate) * up).astype(BF16)
        acc = x1v + _mm(act_ref[...], wd_ref[...])
        gfv = gf_ref[...]
        y, xh, r = _rms_fwd(acc, gfv)
        err = y - tg_ref[...]
        part = 0.5 * jnp.sum(jnp.mean(err * err, axis=-1, keepdims=True), axis=0, keepdims=True)
        dx2, dgrow = _rms_bwd(err * (1.0 / D_MODEL), xh, r, gfv)
        dx2_ref[...] = dx2

        @pl.when(pl.program_id(0) == 0)
        def _():
            dgf_ref[...] = jnp.zeros_like(dgf_ref)
            loss_ref[...] = jnp.zeros_like(loss_ref)

        dgf_ref[...] += jnp.sum(dgrow, axis=0, keepdims=True)
        loss_ref[...] += jnp.broadcast_to(part, loss_ref.shape)

    row = lambda n: pl.BlockSpec((tm, n), lambda i: (i, 0))
    return pl.pallas_call(
        body,
        name="ffn_fwd",
        grid=(nt,),
        in_specs=[
            row(D_MODEL), _const_spec((1, D_MODEL)), _const_spec((1, D_MODEL)), row(D_MODEL),
            _const_spec(w_gate_t.shape), _const_spec(w_up_t.shape), _const_spec(w_down.shape),
        ],
        out_specs=[
            row(D_MODEL), row(D_FF), row(D_FF), row(D_FF), row(D_MODEL),
            pl.BlockSpec((8, LANES), lambda i: (0, 0)),
            pl.BlockSpec((1, D_MODEL), lambda i: (0, 0)),
        ],
        out_shape=[
            jax.ShapeDtypeStruct((T, D_MODEL), BF16),
            jax.ShapeDtypeStruct((T, D_FF), BF16),
            jax.ShapeDtypeStruct((T, D_FF), BF16),
            jax.ShapeDtypeStruct((T, D_FF), BF16),
            jax.ShapeDtypeStruct((T, D_MODEL), F32),
            jax.ShapeDtypeStruct((8, LANES), F32),
            jax.ShapeDtypeStruct((1, D_MODEL), F32),
        ],
        compiler_params=_params(("arbitrary",)),
    )(x1, g2, gf, tgt, w_gate_t, w_up_t, w_down)


def _ffn_bwd(dx2, gate, up, x1, g2, w_gate_t, w_up_t, w_down):
    T = x1.shape[0]
    tm = min(T, FF_ROW_TILE)
    nc = D_FF // FF_CHUNK

    def body(dx2_ref, gate_ref, up_ref, x1_ref, g2_ref, wg_ref, wu_ref, wd_ref, dgate_ref, dup_ref, dx1_ref, dg2_ref):
        dx2v = dx2_ref[...]
        dx2b = dx2v.astype(BF16)
        for c in range(nc):
            sl = slice(c * FF_CHUNK, (c + 1) * FF_CHUNK)
            dact = _mm_nt(dx2b, wd_ref[sl, :])
            gate = gate_ref[:, sl].astype(F32)
            sg = _sigmoid(gate)
            silu = gate * sg
            dgate = (dact * up_ref[:, sl].astype(F32) * (sg * (1.0 + gate * (1.0 - sg)))).astype(BF16)
            dup = (dact * silu).astype(BF16)
            dgate_ref[:, sl] = dgate
            dup_ref[:, sl] = dup
        dh2 = _mm(dgate_ref[...], wg_ref[...]) + _mm(dup_ref[...], wu_ref[...])
        g2v = g2_ref[...]
        _, xh, r = _rms_fwd(x1_ref[...], g2v)
        dxn, dgrow = _rms_bwd(dh2, xh, r, g2v)
        dx1_ref[...] = dx2v + dxn

        @pl.when(pl.program_id(0) == 0)
        def _():
            dg2_ref[...] = jnp.zeros_like(dg2_ref)

        dg2_ref[...] += jnp.sum(dgrow, axis=0, keepdims=True)

    row = lambda n: pl.BlockSpec((tm, n), lambda i: (i, 0))
    return pl.pallas_call(
        body,
        name="ffn_bwd",
        grid=(T // tm,),
        in_specs=[
            row(D_MODEL), row(D_FF), row(D_FF), row(D_MODEL), _const_spec((1, D_MODEL)),
            _const_spec(w_gate_t.shape), _const_spec(w_up_t.shape), _const_spec(w_down.shape),
        ],
        out_specs=[row(D_FF), row(D_FF), row(D_MODEL), pl.BlockSpec((1, D_MODEL), lambda i: (0, 0))],
        out_shape=[
            jax.ShapeDtypeStruct((T, D_FF), BF16),
            jax.ShapeDtypeStruct((T, D_FF), BF16),
            jax.ShapeDtypeStruct((T, D_MODEL), F32),
            jax.ShapeDtypeStruct((1, D_MODEL), F32),
        ],
        compiler_params=_params(("arbitrary",), VMEM_LIMIT_MAX),
    )(dx2, gate, up, x1, g2, w_gate_t, w_up_t, w_down)


def _mix_bwd(dx1, gates, pool_y, attn_y, p2, scale, w_out, w_ao, w_po, token):
    T = dx1.shape[0]
    tm = ROW_TILE

    def body(dx1_ref, gt_ref, py_ref, ay_ref, p2_ref, sc_ref, wout_ref, wao_ref, wpo_ref, token_ref, dgt_ref, dpy_ref, day_ref, da_ref, dp2_ref, dsc_ref):
        dm = _mm_nt(dx1_ref[...].astype(BF16), wout_ref[...])
        sp = _sigmoid(gt_ref[:, :D_MODEL].astype(F32))
        sa = _sigmoid(gt_ref[:, D_MODEL:].astype(F32))
        dgt_ref[:, :D_MODEL] = (dm * py_ref[...].astype(F32) * (sp * (1.0 - sp))).astype(BF16)
        dgt_ref[:, D_MODEL:] = (dm * ay_ref[...].astype(F32) * (sa * (1.0 - sa))).astype(BF16)
        dpy = (dm * sp).astype(BF16)
        day = (dm * sa).astype(BF16)
        dpy_ref[...] = dpy
        day_ref[...] = day
        da_ref[...] = _mm_nt(day, wao_ref[...]).astype(BF16)
        dp3 = _mm_nt(dpy, wpo_ref[...])
        dp2_ref[...] = (dp3 * sc_ref[...]).astype(BF16)

        @pl.when(pl.program_id(0) == 0)
        def _():
            dsc_ref[...] = jnp.zeros_like(dsc_ref)

        dsc_ref[...] += jnp.sum(dp3 * p2_ref[...], axis=0, keepdims=True)

    row = lambda n: pl.BlockSpec((tm, n), lambda i: (i, 0))
    return pl.pallas_call(
        body,
        name="mix_bwd",
        grid=(T // tm,),
        in_specs=[
            row(D_MODEL), row(2 * D_MODEL), row(D_MODEL), row(D_MODEL), row(POOL_WIDTH), _const_spec((1, POOL_WIDTH)),
            _const_spec(w_out.shape), _const_spec(w_ao.shape), _const_spec(w_po.shape), _HBM,
        ],
        out_specs=[row(2 * D_MODEL), row(D_MODEL), row(D_MODEL), row(ATTN_WIDTH), row(POOL_WIDTH), pl.BlockSpec((1, POOL_WIDTH), lambda i: (0, 0))],
        out_shape=[
            jax.ShapeDtypeStruct((T, 2 * D_MODEL), BF16),
            jax.ShapeDtypeStruct((T, D_MODEL), BF16),
            jax.ShapeDtypeStruct((T, D_MODEL), BF16),
            jax.ShapeDtypeStruct((T, ATTN_WIDTH), BF16),
            jax.ShapeDtypeStruct((T, POOL_WIDTH), BF16),
            jax.ShapeDtypeStruct((1, POOL_WIDTH), F32),
        ],
        compiler_params=_params(("arbitrary",)),
    )(dx1, gates, pool_y, attn_y, p2, scale, w_out, w_ao, w_po, token)


def _pool_bwd(dp2, pm, mix_b, token, n_seq, S):
    T = n_seq * S

    def body(dp2_ref, pm_ref, mix_ref, token_ref, du_ref, dmix_ref):
        g = pl.program_id(0)
        dp2v = dp2_ref[...]
        dpm = _mm_nt(dp2v, mix_ref[...])
        row = lax.broadcasted_iota(jnp.int32, dpm.shape, 0)
        w = _window_pick(g, 2.0, 4.0, 8.0, 16.0)
        e = dpm / jnp.minimum((row + 1).astype(F32), w)

        def ahead(a, k):
            return jnp.where(row < S - k, pltpu.roll(a, S - k, 0), 0.0)

        r2 = e + ahead(e, 1)
        r4 = r2 + ahead(r2, 2)
        r8 = r4 + ahead(r4, 4)
        r16 = r8 + ahead(r8, 8)
        du_ref[...] = (_window_pick(g, r2, r4, r8, r16) - dpm).astype(BF16)

        @pl.when(pl.program_id(1) == 0)
        def _():
            dmix_ref[...] = jnp.zeros_like(dmix_ref)

        dmix_ref[...] += _mm_tn(pm_ref[...], dp2v)

    grp = pl.BlockSpec((S, GROUP_DIM), lambda g, s: (s, g))
    mixs = pl.BlockSpec((None, GROUP_DIM, GROUP_DIM), lambda g, s: (g, 0, 0))
    return pl.pallas_call(
        body,
        name="pool_bwd",
        grid=(len(POOL_WINDOWS), n_seq),
        in_specs=[grp, grp, mixs, _HBM],
        out_specs=[grp, mixs],
        out_shape=[jax.ShapeDtypeStruct((T, POOL_WIDTH), BF16), jax.ShapeDtypeStruct((len(POOL_WINDOWS), GROUP_DIM, GROUP_DIM), F32)],
        compiler_params=_params(("parallel", "arbitrary")),
    )(dp2, pm, mix_b, token)


def _attn_bwd(qkv, da, a, fcol, lse, n_seq, S):
    T = n_seq * S
    tb = ATTN_BLOCK
    nb = S // tb
    scale = HEAD_DIM ** -0.5

    def body(q_ref, k_ref, v_ref, do_ref, o_ref, fc_ref, st_ref, dq_ref, dk_ref, dv_ref, dfk_ref, dfq_ref,
             qa_sc, doa_sc, dq_acc, ka_sc, va_sc, dk_sc, dv_sc):
        j = pl.program_id(1)
        lane = lax.broadcasted_iota(jnp.int32, (1, LANES), 1)
        low = lane < HEAD_DIM
        ones = (1.0, 1.0, 1.0)
        zeros = (0.0, 0.0, 0.0)

        @pl.when(j == 0)
        def _():
            dq_acc[...] = jnp.zeros_like(dq_acc)

            def rows_q(i, carry):
                r0 = pl.multiple_of(i * tb, tb)
                for h in range(N_HEADS):
                    pair = slice((h // 2) * LANES, (h // 2 + 1) * LANES)
                    qp = q_ref[pl.ds(r0, tb), pair]
                    dop = do_ref[pl.ds(r0, tb), pair]
                    prod = dop.astype(F32) * o_ref[pl.ds(r0, tb), pair].astype(F32)
                    head = (lane >= HEAD_DIM * (h % 2)) & (lane < HEAD_DIM * (h % 2 + 1))
                    delta = jnp.sum(jnp.where(head, prod, 0.0), axis=1, keepdims=True)
                    cq = fc_ref[pl.ds(r0, tb), h : h + 1] - st_ref[pl.ds(r0, tb), h : h + 1]
                    qa_sc[h, pl.ds(r0, tb), :] = _augment(qp, h % 2, _split3(cq), ones)
                    doa_sc[h, pl.ds(r0, tb), :] = _augment(dop, h % 2, _split3(-delta), zeros)
                return carry

            lax.fori_loop(0, nb, rows_q, 0)

        c0 = pl.multiple_of(j * tb, tb)
        for h in range(N_HEADS):
            pair = slice((h // 2) * LANES, (h // 2 + 1) * LANES)
            kp = k_ref[:, pair] * scale
            ka_sc[h] = _augment(kp, h % 2, ones, _split3(-fc_ref[pl.ds(c0, tb), h : h + 1]))
            va_sc[h] = _augment(v_ref[:, pair], h % 2, ones, zeros)
        dk_sc[...] = jnp.zeros_like(dk_sc)
        dv_sc[...] = jnp.zeros_like(dv_sc)
        causal = lax.broadcasted_iota(jnp.int32, (tb, tb), 1) <= lax.broadcasted_iota(jnp.int32, (tb, tb), 0)

        def step(i, masked):
            r0 = pl.multiple_of(i * tb, tb)
            for h in range(N_HEADS):
                dob = do_ref[pl.ds(r0, tb), (h // 2) * LANES : (h // 2 + 1) * LANES]
                qa = qa_sc[h, pl.ds(r0, tb), :]
                s = _mm_nt(qa, ka_sc[h])
                if masked:
                    s = jnp.where(causal, s, -jnp.inf)
                pr = jnp.exp(s)
                dv_sc[h] += _mm_tn(pr.astype(BF16), dob)
                dsb = (pr * _mm_nt(doa_sc[h, pl.ds(r0, tb), :], va_sc[h])).astype(BF16)
                dk_sc[h] += _mm_tn(dsb, qa)
                dq_acc[h, pl.ds(r0, tb), :] += _mm(dsb, ka_sc[h])

        step(j, True)

        def loop_body(i, carry):
            step(i, False)
            return carry

        lax.fori_loop(j + 1, nb, loop_body, 0)
        dfk = jnp.zeros((tb, LANES), F32)
        for p in range(N_PAIRS):
            dk_ref[:, p * LANES : (p + 1) * LANES] = (jnp.where(low, dk_sc[2 * p], dk_sc[2 * p + 1]) * scale).astype(BF16)
            dv_ref[:, p * LANES : (p + 1) * LANES] = jnp.where(low, dv_sc[2 * p], dv_sc[2 * p + 1]).astype(BF16)
            for hh in range(2):
                b = HEAD_DIM * (1 - hh) + 3
                dfk = jnp.where(lane == 2 * p + hh, -dk_sc[2 * p + hh][:, b : b + 1], dfk)
        dfk_ref[...] = dfk

        @pl.when(j == nb - 1)
        def _():
            def rows_dq(i, carry):
                r0 = pl.multiple_of(i * tb, tb)
                dfq = jnp.zeros((tb, LANES), F32)
                for p in range(N_PAIRS):
                    parts = [dq_acc[2 * p + hh, pl.ds(r0, tb), :] for hh in range(2)]
                    dq_ref[pl.ds(r0, tb), p * LANES : (p + 1) * LANES] = jnp.where(low, parts[0], parts[1]).astype(BF16)
                    for hh in range(2):
                        b = HEAD_DIM * (1 - hh)
                        dfq = jnp.where(lane == 2 * p + hh, parts[hh][:, b : b + 1], dfq)
                dfq_ref[pl.ds(r0, tb), :] = dfq
                return carry

            lax.fori_loop(0, nb, rows_dq, 0)

    seq = lambda w, col: pl.BlockSpec((S, w), lambda s, j: (s, col))
    blk = lambda w, col: pl.BlockSpec((tb, w), lambda s, j: (s * nb + j, col))
    return pl.pallas_call(
        body,
        name="attn_bwd",
        grid=(n_seq, nb),
        in_specs=[seq(ATTN_WIDTH, 0), blk(ATTN_WIDTH, 1), blk(ATTN_WIDTH, 2), seq(ATTN_WIDTH, 0), seq(ATTN_WIDTH, 0), seq(LANES, 0), seq(LANES, 0)],
        out_specs=[seq(ATTN_WIDTH, 0), blk(ATTN_WIDTH, 0), blk(ATTN_WIDTH, 0), blk(LANES, 0), seq(LANES, 0)],
        out_shape=[
            jax.ShapeDtypeStruct((T, ATTN_WIDTH), BF16),
            jax.ShapeDtypeStruct((T, ATTN_WIDTH), BF16),
            jax.ShapeDtypeStruct((T, ATTN_WIDTH), BF16),
            jax.ShapeDtypeStruct((T, LANES), F32),
            jax.ShapeDtypeStruct((T, LANES), F32),
        ],
        scratch_shapes=[
            pltpu.VMEM((N_HEADS, S, LANES), BF16),
            pltpu.VMEM((N_HEADS, S, LANES), BF16),
            pltpu.VMEM((N_HEADS, S, LANES), F32),
            pltpu.VMEM((N_HEADS, tb, LANES), BF16),
            pltpu.VMEM((N_HEADS, tb, LANES), BF16),
            pltpu.VMEM((N_HEADS, tb, LANES), F32),
            pltpu.VMEM((N_HEADS, tb, LANES), F32),
        ],
        compiler_params=_params(("parallel", "arbitrary")),
    )(qkv, qkv, qkv, da, a, fcol, lse)


def _forget_bwd(dfk, dfq, fl, b_pad, n_seq, S):
    def body(df_ref, dfq_ref, fl_ref, b_ref, dfl_ref, db_ref):
        t = (df_ref[...] + dfq_ref[...]).T
        lane = lax.broadcasted_iota(jnp.int32, t.shape, 1)
        k = 1
        while k < S:
            t = t + jnp.where(lane < S - k, pltpu.roll(t, S - k, 1), 0.0)
            k *= 2
        dfl = t.T * _sigmoid(-(fl_ref[...] + b_ref[...]))
        dfl_ref[...] = dfl.astype(BF16)

        @pl.when(pl.program_id(0) == 0)
        def _():
            db_ref[...] = jnp.zeros_like(db_ref)

        db_ref[...] += jnp.sum(dfl, axis=0, keepdims=True)

    return pl.pallas_call(
        body,
        name="forget_bwd",
        grid=(n_seq,),
        in_specs=[
            pl.BlockSpec((S, LANES), lambda s: (s, 0)),
            pl.BlockSpec((S, LANES), lambda s: (s, 0)),
            pl.BlockSpec((S, FL_PAD), lambda s: (s, 0)),
            _const_spec((1, FL_PAD)),
        ],
        out_specs=[pl.BlockSpec((S, FL_PAD), lambda s: (s, 0)), pl.BlockSpec((1, FL_PAD), lambda s: (0, 0))],
        out_shape=[jax.ShapeDtypeStruct((n_seq * S, FL_PAD), BF16), jax.ShapeDtypeStruct((1, FL_PAD), F32)],
        compiler_params=_params(("arbitrary",)),
    )(dfk, dfq, fl, b_pad)


def _in_proj_bwd(du, dq, dk, dv, dfl, dgates, x, dx1, g1, w_uqkv, w_fl, w_g, token):
    T = x.shape[0]
    tm = ROW_TILE

    def body(du_ref, dq_ref, dk_ref, dv_ref, dfl_ref, dgt_ref, x_ref, dx1_ref, g_ref, wa_ref, wf_ref, wg_ref, token_ref, dx_ref, dg_ref):
        dz = jnp.concatenate([du_ref[...], dq_ref[...], dk_ref[...], dv_ref[...]], axis=1)
        dh = _mm_nt(dz, wa_ref[...]) + _mm_nt(dgt_ref[...], wg_ref[...]) + _mm_nt(dfl_ref[...], wf_ref[...])
        gv = g_ref[...]
        _, xh, r = _rms_fwd(x_ref[...], gv)
        dxn, dgrow = _rms_bwd(dh, xh, r, gv)
        dx_ref[...] = dx1_ref[...] + dxn

        @pl.when(pl.program_id(0) == 0)
        def _():
            dg_ref[...] = jnp.zeros_like(dg_ref)

        dg_ref[...] += jnp.sum(dgrow, axis=0, keepdims=True)

    row = lambda n: pl.BlockSpec((tm, n), lambda i: (i, 0))
    return pl.pallas_call(
        body,
        name="in_proj_bwd",
        grid=(T // tm,),
        in_specs=[
            row(512), row(512), row(512), row(512), row(FL_PAD), row(2 * D_MODEL), row(D_MODEL), row(D_MODEL), _const_spec((1, D_MODEL)),
            _const_spec(w_uqkv.shape), _const_spec(w_fl.shape), _const_spec(w_g.shape), _HBM,
        ],
        out_specs=[row(D_MODEL), pl.BlockSpec((1, D_MODEL), lambda i: (0, 0))],
        out_shape=[jax.ShapeDtypeStruct((T, D_MODEL), F32), jax.ShapeDtypeStruct((1, D_MODEL), F32)],
        compiler_params=_params(("arbitrary",)),
    )(du, dq, dk, dv, dfl, dgates, x, dx1, g1, w_uqkv, w_fl, w_g, token)


def _pick_block(n):
    for b in (512, 1408, 256, 128):
        if n % b == 0:
            return b
    raise ValueError(n)


def _matmul_tn(a, b, name):
    T, K = a.shape
    N = b.shape[1]
    bt, bk, bn = min(T, DW_TOKENS), _pick_block(K), _pick_block(N)
    nt = T // bt

    def body(a_ref, b_ref, o_ref, acc):
        @pl.when(pl.program_id(2) == 0)
        def _():
            acc[...] = jnp.zeros_like(acc)

        acc[...] += _mm_tn(a_ref[...].astype(BF16), b_ref[...].astype(BF16))

        @pl.when(pl.program_id(2) == nt - 1)
        def _():
            o_ref[...] = acc[...].astype(BF16)

    return pl.pallas_call(
        body,
        name=name,
        grid=(K // bk, N // bn, nt),
        in_specs=[pl.BlockSpec((bt, bk), lambda k, n, t: (t, k)), pl.BlockSpec((bt, bn), lambda k, n, t: (t, n))],
        out_specs=pl.BlockSpec((bk, bn), lambda k, n, t: (k, n)),
        out_shape=jax.ShapeDtypeStruct((K, N), BF16),
        scratch_shapes=[pltpu.VMEM((bk, bn), F32)],
        compiler_params=_params(("parallel", "parallel", "arbitrary")),
    )(a, b)


W_IN_A = POOL_WIDTH + 3 * ATTN_WIDTH
W_IN_SHARD = (W_IN_A + N_HEADS + 2 * D_MODEL) // N_DEV
_W_IN_PIECES = ((0, W_IN_A), (W_IN_A, W_IN_A + N_HEADS), (W_IN_A + N_HEADS, W_IN_A + N_HEADS + 2 * D_MODEL))


def _w_in_segments(d):
    lo, hi = d * W_IN_SHARD, (d + 1) * W_IN_SHARD
    out = []
    for p, (a, b) in enumerate(_W_IN_PIECES):
        s, e = max(lo, a), min(hi, b)
        if s < e:
            out.append((p, s - a, s - lo, e - s))
    return out


def _w_in_pieces(gathered):
    tm = ROW_TILE // 2

    def body(g_ref, wa_ref, wf_ref, wg_ref):
        outs = (wa_ref, wf_ref, wg_ref)
        wf_ref[...] = jnp.zeros_like(wf_ref)
        for d in range(N_DEV):
            for p, at, frm, n in _w_in_segments(d):
                outs[p][:, at : at + n] = g_ref[d, :, frm : frm + n]

    return pl.pallas_call(
        body,
        name="w_in_pieces",
        grid=(D_MODEL // tm,),
        in_specs=[pl.BlockSpec((N_DEV, tm, W_IN_SHARD), lambda i: (0, i, 0))],
        out_specs=[pl.BlockSpec((tm, W_IN_A), lambda i: (i, 0)), pl.BlockSpec((tm, FL_PAD), lambda i: (i, 0)), pl.BlockSpec((tm, 2 * D_MODEL), lambda i: (i, 0))],
        out_shape=[
            jax.ShapeDtypeStruct((D_MODEL, W_IN_A), gathered.dtype),
            jax.ShapeDtypeStruct((D_MODEL, FL_PAD), gathered.dtype),
            jax.ShapeDtypeStruct((D_MODEL, 2 * D_MODEL), gathered.dtype),
        ],
        compiler_params=_params(("parallel",)),
    )(gathered)


def _dw_in(h, du, dq, dk, dv, dfl, dgates):
    T = h.shape[0]
    bt, bk = min(T, DW_TOKENS // 2), 512
    nt = T // bt
    pieces = (du, dq, dk, dv, dfl, dgates)
    offs = [0]
    for p in pieces:
        offs.append(offs[-1] + p.shape[1])

    def body(h_ref, *rest):
        refs, o_ref, acc = rest[: len(pieces)], rest[-2], rest[-1]

        @pl.when(pl.program_id(1) == 0)
        def _():
            acc[...] = jnp.zeros_like(acc)

        ht = h_ref[...].T
        for ref, at in zip(refs, offs):
            acc[:, at : at + ref.shape[1]] += _mm(ht, ref[...])

        @pl.when(pl.program_id(1) == nt - 1)
        def _():
            starts = (0, W_IN_A, W_IN_A + FL_PAD)
            for d in range(N_DEV):
                for p, at, to, n in _w_in_segments(d):
                    o_ref[d % 2, d // 2, :, to : to + n] = acc[:, starts[p] + at : starts[p] + at + n].astype(BF16)

    return pl.pallas_call(
        body,
        name="dw_in",
        grid=(D_MODEL // bk, nt),
        in_specs=[pl.BlockSpec((bt, bk), lambda k, t: (t, k))] + [pl.BlockSpec((bt, p.shape[1]), lambda k, t: (t, 0)) for p in pieces],
        out_specs=pl.BlockSpec((2, 4, bk, W_IN_SHARD), lambda k, t: (0, 0, k, 0)),
        out_shape=jax.ShapeDtypeStruct((2, 4, D_MODEL, W_IN_SHARD), BF16),
        scratch_shapes=[pltpu.VMEM((bk, offs[-1]), F32)],
        compiler_params=_params(("parallel", "arbitrary")),
    )(h, *pieces)


def _position():
    return lax.axis_index("x"), lax.axis_index("y"), lax.axis_index("c")


_HBM = pl.BlockSpec(memory_space=pl.ANY)


def _all_gather(blocks, name):
    n = len(blocks)

    def body(*refs):
        xs, outs = refs[:n], refs[n : 2 * n]
        send_sems, recv_sems, local_sems = refs[2 * n :]
        x, y, c = _position()
        me, sibling = (x, y, c), (x, y, 1 - c)
        chips = [(1 - x, y), (x, 1 - y), (1 - x, 1 - y)]

        def rows(a, px, py, pc):
            return outs[a].at[4 * px + 2 * py + pc]

        def copy(a, k, blk, to, src=None):
            return pltpu.make_async_remote_copy(
                src_ref=rows(a, *blk) if src is None else src, dst_ref=rows(a, *blk),
                send_sem=send_sems.at[7 * a + k], recv_sem=recv_sems.at[7 * a + k], device_id=to, device_id_type=MESH,
            )

        mine = [pltpu.make_async_copy(xs[a], rows(a, *me), local_sems.at[a]) for a in range(n)]
        for cp in mine:
            cp.start()
        first = []
        for a in range(n):
            first.append(copy(a, 0, me, sibling, src=xs[a]))
            first += [copy(a, 1 + j, me, (*chip, c), src=xs[a]) for j, chip in enumerate(chips)]
        for cp in first:
            cp.start()
        passed = []
        for j, chip in enumerate(chips):
            for a in range(n):
                copy(a, 1 + j, (*chip, c), me).wait_recv()
                passed.append(copy(a, 4 + j, (*chip, c), sibling))
                passed[-1].start()
        for a in range(n):
            copy(a, 0, sibling, me).wait_recv()
        for j, chip in enumerate(chips):
            for a in range(n):
                copy(a, 4 + j, (*chip, 1 - c), me).wait_recv()
        for cp in first + passed:
            cp.wait_send()
        for cp in mine:
            cp.wait()

    return pl.pallas_call(
        body,
        name=name,
        out_shape=[jax.ShapeDtypeStruct((N_DEV, *b.shape), b.dtype) for b in blocks],
        in_specs=[_HBM] * n,
        out_specs=[_HBM] * n,
        scratch_shapes=[pltpu.SemaphoreType.DMA((7 * n,)), pltpu.SemaphoreType.DMA((7 * n,)), pltpu.SemaphoreType.DMA((n,))],
    )(*blocks)


_SEM = pl.BlockSpec(memory_space=pltpu.SEMAPHORE)
_HBM_ONLY = pl.BlockSpec(memory_space=pltpu.HBM)
_SIDE_EFFECT = pltpu.SideEffectType.DATAFLOW_SIDE_EFFECTING


def _peer(x, y, c, k):
    return (1 - x if k & 4 else x, 1 - y if k & 2 else y, 1 - c if k & 1 else c)


_PEER_BITS = {"gather": range(1, N_DEV), "scatter": range(1, N_DEV), "chips": (4, 2, 6)}
_LAND_SLOTS = {"gather": N_DEV, "scatter": N_DEV, "chips": 3}


def _exchange_copies(src_refs, land_refs, send_sems, recv_sems, pattern, receive_side):
    x, y, c = _position()
    me = 4 * x + 2 * y + c
    bits = _PEER_BITS[pattern]
    cps = []
    for j, k in enumerate(bits):
        px, py, pc = _peer(x, y, c, k)
        peer = 4 * px + 2 * py + pc
        for a, (src, land) in enumerate(zip(src_refs, land_refs)):
            if pattern == "chips":
                s, slot = src.at[2 * px + py], j
            else:
                s, slot = (src if pattern == "gather" else src.at[peer]), (peer if receive_side else me)
            cps.append(pltpu.make_async_remote_copy(
                src_ref=s, dst_ref=land.at[slot],
                send_sem=send_sems.at[len(bits) * a + j], recv_sem=recv_sems.at[len(bits) * a + j],
                device_id=(px, py, pc), device_id_type=MESH,
            ))
    return cps


def _exchange_start(srcs, after, name, pattern):
    n = len(srcs)
    m = len(_PEER_BITS[pattern])
    lands = [jax.ShapeDtypeStruct((_LAND_SLOTS[pattern], *s.shape[-2:]), s.dtype) for s in srcs]

    def body(*refs):
        src_refs, land_refs = refs[1 : 1 + n], refs[1 + n : 1 + 2 * n]
        send_sems, recv_sems = refs[1 + 2 * n], refs[2 + 2 * n]
        token = refs[-1]
        for cp in _exchange_copies(src_refs, land_refs, send_sems, recv_sems, pattern, receive_side=False):
            cp.start()
        token[...] = jnp.zeros_like(token)

    hbm = lambda t: pltpu.with_memory_space_constraint(t, pltpu.HBM)
    out = pl.pallas_call(
        body,
        name=name,
        out_shape=(
            pltpu.SemaphoreType.DMA((m * n,)), pltpu.SemaphoreType.DMA((m * n,)),
            *[pltpu.HBM(s.shape, s.dtype) for s in srcs], *[pltpu.HBM(l.shape, l.dtype) for l in lands],
            jax.ShapeDtypeStruct((8, LANES), F32),
        ),
        in_specs=(_HBM, *[_HBM_ONLY] * (2 * n)),
        out_specs=(_SEM, _SEM, *[_HBM_ONLY] * (2 * n), pl.BlockSpec(memory_space=pltpu.VMEM)),
        input_output_aliases={1 + i: 2 + i for i in range(2 * n)},
        compiler_params=pltpu.CompilerParams(has_side_effects=_SIDE_EFFECT),
    )(after, *[hbm(s) for s in srcs], *[hbm(lax.empty(l.shape, l.dtype)) for l in lands])
    return out[0], out[1], out[2 : 2 + n], out[2 + n : 2 + 2 * n], out[-1]


def _exchange_wait(send_sems, recv_sems, srcs, lands, after, name, pattern):
    n = len(srcs)

    def body(*refs):
        src_refs, land_refs = refs[:n], refs[n : 2 * n]
        for cp in _exchange_copies(src_refs, land_refs, refs[2 * n], refs[2 * n + 1], pattern, receive_side=True):
            cp.wait_send()
            cp.wait_recv()

    out = pl.pallas_call(
        body,
        name=name,
        out_shape=(*[pltpu.HBM(s.shape, s.dtype) for s in srcs], *[pltpu.HBM(l.shape, l.dtype) for l in lands]),
        in_specs=(*[_HBM_ONLY] * (2 * n), _SEM, _SEM, _HBM),
        out_specs=tuple([_HBM_ONLY] * (2 * n)),
        input_output_aliases={i: i for i in range(2 * n)},
        compiler_params=pltpu.CompilerParams(has_side_effects=_SIDE_EFFECT),
    )(*srcs, *lands, send_sems, recv_sems, after)
    return out[:n], out[n:]


def _sibling_exchange(sends):
    n = len(sends)

    def body(*refs):
        srcs, dsts = refs[:n], refs[n : 2 * n]
        send_sems, recv_sems = refs[2 * n :]
        x, y, c = _position()
        cps = [
            pltpu.make_async_remote_copy(
                src_ref=srcs[a].at[1 - c], dst_ref=dsts[a], send_sem=send_sems.at[a], recv_sem=recv_sems.at[a],
                device_id=(x, y, 1 - c), device_id_type=MESH,
            )
            for a in range(n)
        ]
        for cp in cps:
            cp.start()
        for cp in cps:
            cp.wait()

    return pl.pallas_call(
        body,
        name="rs_sibling",
        out_shape=[jax.ShapeDtypeStruct(s.shape[1:], s.dtype) for s in sends],
        in_specs=[_HBM] * n,
        out_specs=[_HBM] * n,
        scratch_shapes=[pltpu.SemaphoreType.DMA((n,)), pltpu.SemaphoreType.DMA((n,))],
    )(*sends)


def _rows_tile(r):
    return ROW_TILE if r % ROW_TILE == 0 else r


def _pair_sum(send, got, core, name):
    _, _, r, c = send.shape
    br = _rows_tile(r)

    def body(core_ref, a_ref, b_ref, o_ref):
        o_ref[...] = (a_ref[...].astype(F32) + b_ref[...].astype(F32)).astype(o_ref.dtype)

    return pl.pallas_call(
        body,
        name=name,
        grid_spec=pltpu.PrefetchScalarGridSpec(
            num_scalar_prefetch=1,
            grid=(4, r // br),
            in_specs=[
                pl.BlockSpec((None, None, br, c), lambda n, i, core: (core[0], n, i, 0)),
                pl.BlockSpec((None, br, c), lambda n, i, core: (n, i, 0)),
            ],
            out_specs=pl.BlockSpec((None, br, c), lambda n, i, core: (n, i, 0)),
        ),
        out_shape=jax.ShapeDtypeStruct((4, r, c), send.dtype),
        compiler_params=_params(("parallel", "parallel")),
    )(core, send, got)


def _adamw(w, g, m, v):
    m = ADAM_B1 * m + (1.0 - ADAM_B1) * g
    v = ADAM_B2 * v + (1.0 - ADAM_B2) * (g * g)
    m_hat = m / (1.0 - ADAM_B1 ** ADAM_STEP)
    v_hat = v / (1.0 - ADAM_B2 ** ADAM_STEP)
    delta = -ADAM_LR * (m_hat / (jnp.sqrt(v_hat) + ADAM_EPS) + ADAM_WD * w)
    return delta, m, v


def _shard_update(send, got, recv, w, m, v, pos, name):
    _, r, c = w.shape
    br = _rows_tile(r)

    def body(pos_ref, a_ref, b_ref, r_ref, w_ref, m_ref, v_ref, g_ref, d_ref, nm_ref, nv_ref):
        g = a_ref[...].astype(F32) + b_ref[...].astype(F32)
        for n in range(3):
            g = g + r_ref[n].astype(F32)
        g_ref[...] = g
        d_ref[...], nm_ref[...], nv_ref[...] = _adamw(w_ref[...], g, m_ref[...], v_ref[...])

    own = pl.BlockSpec((None, br, c), lambda i, pos: (0, i, 0))
    return pl.pallas_call(
        body,
        name=name,
        grid_spec=pltpu.PrefetchScalarGridSpec(
            num_scalar_prefetch=1,
            grid=(r // br,),
            in_specs=[
                pl.BlockSpec((None, None, br, c), lambda i, pos: (pos[0], pos[1], i, 0)),
                pl.BlockSpec((None, br, c), lambda i, pos: (pos[1], i, 0)),
                pl.BlockSpec((3, br, c), lambda i, pos: (0, i, 0)),
                own, own, own,
            ],
            out_specs=[own, own, own, own],
        ),
        out_shape=[jax.ShapeDtypeStruct((1, r, c), F32)] * 4,
        compiler_params=_params(("parallel",)),
    )(pos, send, got, recv, w, m, v)


def _shard_update_direct(parts, chunks, w, m, v, me, name):
    _, r, c = w.shape
    br = _rows_tile(r)

    def body(me_ref, p_ref, own_ref, w_ref, m_ref, v_ref, g_ref, d_ref, nm_ref, nv_ref):
        g = None
        for n in range(N_DEV):
            part = jnp.where(me_ref[0] == n, own_ref[...], p_ref[n]).astype(F32)
            g = part if g is None else g + part
        g_ref[...] = g
        d_ref[...], nm_ref[...], nv_ref[...] = _adamw(w_ref[...], g, m_ref[...], v_ref[...])

    shard = pl.BlockSpec((None, br, c), lambda i, me: (0, i, 0))
    return pl.pallas_call(
        body,
        name=name,
        grid_spec=pltpu.PrefetchScalarGridSpec(
            num_scalar_prefetch=1,
            grid=(r // br,),
            in_specs=[
                pl.BlockSpec((N_DEV, br, c), lambda i, me: (0, i, 0)),
                pl.BlockSpec((None, br, c), lambda i, me: (me[0], i, 0)),
                shard, shard, shard,
            ],
            out_specs=[shard, shard, shard, shard],
        ),
        out_shape=[jax.ShapeDtypeStruct((1, r, c), F32)] * 4,
        compiler_params=_params(("parallel",)),
    )(me, parts, chunks, w, m, v)


def _small_update(parts, w, m, v):
    R = w.shape[0]

    def body(p_ref, w_ref, m_ref, v_ref, g_ref, d_ref, nm_ref, nv_ref):
        g = p_ref[0]
        for n in range(1, N_DEV):
            g = g + p_ref[n]
        g_ref[...] = g
        d_ref[...], nm_ref[...], nv_ref[...] = _adamw(w_ref[...], g, m_ref[...], v_ref[...])

    return pl.pallas_call(
        body,
        name="small_update",
        out_shape=[jax.ShapeDtypeStruct((R, LANES), F32)] * 4,
        compiler_params=pltpu.CompilerParams(vmem_limit_bytes=VMEM_LIMIT),
    )(parts, w, m, v)


_SHARD_AXIS = (1, 1, 1, 0, 0, 0, 0)
_TRANSPOSED = (False, False, False, False, True, True, False)


def _full_from_gathered(t, axis):
    if axis == 0:
        return t.reshape(N_DEV * t.shape[1], t.shape[2])
    return jnp.concatenate([t[d] for d in range(N_DEV)], axis=1)


def _chunks_from_cols(t):
    c = t.shape[1] // N_DEV
    return jnp.stack([t[:, d * c : (d + 1) * c] for d in range(N_DEV)])


_SMALL = (("norm1_g", 8), ("norm2_g", 8), ("norm_f_g", 8), ("b_forget", 8), ("pool_scale", 8), ("pool_mix", 512))
_SMALL_ROWS = sum(r for _, r in _SMALL) + 8


def _pack_small(vals, loss_row):
    parts = []
    for (name, rows), t in zip(_SMALL, vals):
        f = t.astype(F32).reshape(-1)
        f = jnp.concatenate([f, jnp.zeros((rows * LANES - f.shape[0],), F32)]).reshape(rows, LANES)
        parts.append(f)
    parts.append(loss_row)
    return jnp.concatenate(parts, axis=0)


def _unpack_small(packed, shapes):
    out, off = [], 0
    for (name, rows), shape in zip(_SMALL, shapes):
        n = 1
        for s in shape:
            n *= s
        out.append(packed[off : off + rows].reshape(-1)[:n].reshape(shape))
        off += rows
    return out, packed[off, 0]


def _local_grads(x, tgt, g1, g2, gf, b_forget, pool_mix, pool_scale, w_in, fwd_token, out_weights, ffn_weights, ffn_grads_out, out_grads_out, in_grads_out, small_grads_out):
    n_seq, S, _ = x.shape
    T = n_seq * S
    x2 = x.reshape(T, D_MODEL)
    tg2 = tgt.reshape(T, D_MODEL)
    w_uqkv, w_fl, w_g = w_in
    b_pad = jnp.concatenate([b_forget.reshape(1, N_HEADS), jnp.zeros((1, FL_PAD - N_HEADS), F32)], axis=1)
    mix_b = pool_mix.reshape(len(POOL_WINDOWS), GROUP_DIM, GROUP_DIM).astype(BF16)
    scale = pool_scale.reshape(1, POOL_WIDTH)
    g1 = g1.reshape(1, D_MODEL)
    g2 = g2.reshape(1, D_MODEL)
    gf = gf.reshape(1, D_MODEL)

    h, u, qkv, fl, gates = _in_proj(x2, g1, w_uqkv, w_fl, w_g, fwd_token)
    fcol = _forget_fwd(fl, b_pad, n_seq, S)
    pm, p2, p3 = _pool_fwd(u, mix_b, scale, n_seq, S)
    a, lse = _attn_fwd(qkv, fcol, n_seq, S)
    w_po, w_ao, w_out = out_weights(a)
    merged, x1, attn_y, pool_y = _mix_out(a, p3, gates, x2, w_ao, w_po, w_out)
    w_gate_t, w_up_t, w_down = ffn_weights(x1)
    h2, gate, up, act, dx2, loss_rows, dgf = _ffn_fwd(x1, g2, gf, tg2, w_gate_t, w_up_t, w_down)

    dgate, dup, dx1, dg2 = _ffn_bwd(dx2, gate, up, x1, g2, w_gate_t, w_up_t, w_down)
    bwd_token = ffn_grads_out(_matmul_tn(dgate, h2, "dw_ffn_gate"), _matmul_tn(dup, h2, "dw_ffn_up"), _matmul_tn(act, dx2, "dw_ffn_down"))
    dgates, dpy, day, da, dp2, dscale = _mix_bwd(dx1, gates, pool_y, attn_y, p2, scale, w_out, w_ao, w_po, bwd_token)
    out_token = out_grads_out(_matmul_tn(p3, dpy, "dw_pool_out"), _matmul_tn(a, day, "dw_attn_out"), _matmul_tn(merged, dx1, "dw_out"))
    du, dmix = _pool_bwd(dp2, pm, mix_b, out_token, n_seq, S)
    dq, dk, dv, dfk, dfq = _attn_bwd(qkv, da, a, fcol, lse, n_seq, S)
    dfl, db = _forget_bwd(dfk, dfq, fl, b_pad, n_seq, S)
    in_token = in_grads_out(_dw_in(h, du, dq, dk, dv, dfl, dgates))
    dx, dg1 = _in_proj_bwd(du, dq, dk, dv, dfl, dgates, x2, dx1, g1, w_uqkv, w_fl, w_g, in_token)
    small_grads_out((dg1, dg2, dgf, db[:, :N_HEADS], dscale, dmix), loss_rows)
    return dx.reshape(n_seq, S, D_MODEL)


def kernel(x, norm1_g, w_in, b_forget, pool_mix, pool_scale, w_pool_out, w_attn_out, w_out, norm2_g, w_ffn_gate, w_ffn_up, w_ffn_down, norm_f_g, loss_target, m_norm1_g, m_w_in, m_b_forget, m_pool_mix, m_pool_scale, m_w_pool_out, m_w_attn_out, m_w_out, m_norm2_g, m_w_ffn_gate, m_w_ffn_up, m_w_ffn_down, m_norm_f_g, v_norm1_g, v_w_in, v_b_forget, v_pool_mix, v_pool_scale, v_w_pool_out, v_w_attn_out, v_w_out, v_norm2_g, v_w_ffn_gate, v_w_ffn_up, v_w_ffn_down, v_norm_f_g):
    names = ("w_in", "w_pool_out", "w_attn_out", "w_out", "w_ffn_gate", "w_ffn_up", "w_ffn_down")
    w_sh = (w_in, w_pool_out, w_attn_out, w_out, w_ffn_gate, w_ffn_up, w_ffn_down)
    m_sh = (m_w_in, m_w_pool_out, m_w_attn_out, m_w_out, m_w_ffn_gate, m_w_ffn_up, m_w_ffn_down)
    v_sh = (v_w_in, v_w_pool_out, v_w_attn_out, v_w_out, v_w_ffn_gate, v_w_ffn_up, v_w_ffn_down)

    cx, cy, cc = _position()
    me = 4 * cx + 2 * cy + cc
    def stored(t, transposed):
        return jnp.transpose(t, (0, 2, 1)) if transposed else t

    w_sh, m_sh, v_sh = ([stored(t, tr) for t, tr in zip(ts, _TRANSPOSED)] for ts in (w_sh, m_sh, v_sh))
    shards = [w[0].astype(BF16) for w in w_sh]
    (gathered_in,) = _all_gather(shards[:1], "w_in_all_gather")
    out_sems = _exchange_start(shards[1:4], gathered_in, "out_weights_gather_start", "gather")
    ffn_sems = _exchange_start(shards[4:], out_sems[4], "ffn_weights_gather_start", "gather")
    no_order = jnp.zeros((8, LANES), F32)

    def with_own(lands, own):
        return [lax.dynamic_update_slice(l, o[None], (me, 0, 0)) for l, o in zip(lands, own)]

    def gathered_weights(sems, axes, name):
        def wait(after):
            send_sems, recv_sems, srcs, lands, _ = sems
            srcs, lands = _exchange_wait(send_sems, recv_sems, srcs, lands, after, name, "gather")
            return [_full_from_gathered(t, axis) for t, axis in zip(with_own(lands, srcs), axes)]

        return wait

    started = {}

    def scatter_grads(key, name):
        def start(*whole_grads):
            chunks = [
                _chunks_from_cols(t) if axis == 1 else t.reshape(N_DEV, -1, t.shape[1])
                for t, axis in zip(whole_grads, _SHARD_AXIS[key])
            ]
            started[key] = _exchange_start(chunks, no_order, name, "scatter")
            return started[key][4]

        return start

    def gather_small(small, loss_rows):
        started["small"] = _exchange_start([_pack_small(small, loss_rows)], no_order, "small_grads_gather_start", "gather")

    core = jnp.reshape(cc, (1,)).astype(jnp.int32)
    pos = jnp.stack([cc, 2 * cx + cy]).astype(jnp.int32)

    def reduce_w_in(send_in):
        (got_in,) = _sibling_exchange([send_in])
        pair_in = _pair_sum(send_in, got_in, core, "pair_sum_w_in")
        started["in"] = (send_in, got_in, _exchange_start([pair_in], no_order, "w_in_grads_chips_start", "chips"))
        return started["in"][2][4]

    ffn, out = slice(4, 7), slice(1, 4)
    grad_x = _local_grads(
        x, loss_target, norm1_g, norm2_g, norm_f_g, b_forget, pool_mix, pool_scale, _w_in_pieces(gathered_in), ffn_sems[4],
        gathered_weights(out_sems, _SHARD_AXIS[out], "out_weights_gather_wait"),
        gathered_weights(ffn_sems, _SHARD_AXIS[ffn], "ffn_weights_gather_wait"),
        scatter_grads(ffn, "ffn_grads_scatter_start"), scatter_grads(out, "out_grads_scatter_start"), reduce_w_in, gather_small,
    )
    send_in, got_in, chip_sems = started["in"]

    def scattered_updates(key, after, name):
        send_sems, recv_sems, srcs, lands, _ = started[key]
        srcs, lands = _exchange_wait(send_sems, recv_sems, srcs, lands, after, name, "scatter")
        return [
            _shard_update_direct(p, s, w, m, v, jnp.reshape(me, (1,)).astype(jnp.int32), "update_" + n)
            for p, s, w, m, v, n in zip(lands, srcs, w_sh[key], m_sh[key], v_sh[key], names[key])
        ]

    updates_out = scattered_updates(out, grad_x, "out_grads_scatter_wait")
    updates_ffn = scattered_updates(ffn, grad_x, "ffn_grads_scatter_wait")

    small_w = (norm1_g, norm2_g, norm_f_g, b_forget, pool_scale, pool_mix)
    small_m = (m_norm1_g, m_norm2_g, m_norm_f_g, m_b_forget, m_pool_scale, m_pool_mix)
    small_v = (v_norm1_g, v_norm2_g, v_norm_f_g, v_b_forget, v_pool_scale, v_pool_mix)
    zero_row = jnp.zeros((8, LANES), F32)
    send_sems, recv_sems, srcs, lands, _ = chip_sems
    _, (recv_in,) = _exchange_wait(send_sems, recv_sems, srcs, lands, updates_ffn[-1][0], "w_in_grads_chips_wait", "chips")
    update_in = _shard_update(send_in, got_in, recv_in, w_in, m_w_in, v_w_in, pos, "update_w_in")

    send_sems, recv_sems, srcs, lands, _ = started["small"]
    srcs, lands = _exchange_wait(send_sems, recv_sems, srcs, lands, update_in[0], "small_grads_gather_wait", "gather")
    (parts,) = with_own(lands, srcs)
    g_s, d_s, nm_s, nv_s = _small_update(parts, _pack_small(small_w, zero_row), _pack_small(small_m, zero_row), _pack_small(small_v, zero_row))
    g_w, d_w, nm_w, nv_w = zip(*(
        [stored(t, tr) for t in u] for u, tr in zip([update_in] + updates_out + updates_ffn, _TRANSPOSED)
    ))
    shapes = [t.shape for t in small_w]
    (g1, g2, gf, gb, gsc, gmix), loss = _unpack_small(g_s, shapes)
    (d1, d2, df, db_, dsc, dmx), _ = _unpack_small(d_s, shapes)
    (m1, m2, mf, mb, msc, mmx), _ = _unpack_small(nm_s, shapes)
    (v1, v2, vf, vb, vsc, vmx), _ = _unpack_small(nv_s, shapes)

    def ordered(n1, win, b, mix, sc, wpo, wao, wout, n2, wg, wu, wd, nf):
        return (n1, win, b, mix, sc, wpo, wao, wout, n2, wg, wu, wd, nf)

    grads = ordered(g1, g_w[0], gb, gmix, gsc, g_w[1], g_w[2], g_w[3], g2, g_w[4], g_w[5], g_w[6], gf)
    deltas = ordered(d1, d_w[0], db_, dmx, dsc, d_w[1], d_w[2], d_w[3], d2, d_w[4], d_w[5], d_w[6], df)
    new_m = ordered(m1, nm_w[0], mb, mmx, msc, nm_w[1], nm_w[2], nm_w[3], m2, nm_w[4], nm_w[5], nm_w[6], mf)
    new_v = ordered(v1, nv_w[0], vb, vmx, vsc, nv_w[1], nv_w[2], nv_w[3], v2, nv_w[4], nv_w[5], nv_w[6], vf)
    return (loss, grad_x, *grads, *deltas, *new_m, *new_v)
```

```python
import functools

import jax
import jax.numpy as jnp
from jax import lax
from jax.experimental import pallas as pl
from jax.experimental.pallas import tpu as pltpu

F32 = jnp.float32
BF16 = jnp.bfloat16
MESH = pl.DeviceIdType.MESH

D_MODEL = 1024
POOL_WINDOWS = (2, 4, 8, 16)
POOL_WIDTH = 512
GROUP_DIM = 128
ATTN_WIDTH = 512
HEAD_DIM = 64
N_HEADS = 8
N_PAIRS = 4
D_FF = 2816
RMS_EPS = 1e-6
N_DEV = 8
LANES = 128
FL_PAD = 128

ADAM_LR = 0.001
ADAM_B1 = 0.9
ADAM_B2 = 0.999
ADAM_EPS = 1e-08
ADAM_WD = 0.01
ADAM_STEP = 10

VMEM_LIMIT = 56 * 1024 * 1024
VMEM_LIMIT_MAX = 60 * 1024 * 1024
ROW_TILE = 512
ATTN_BLOCK = 512
FF_CHUNK = 256
FF_ROW_TILE = 512
DW_TOKENS = 2048


def _mm(a, b):
    return jnp.dot(a, b, preferred_element_type=F32)


def _mm_nt(a, b):
    return lax.dot_general(a, b, (((1,), (1,)), ((), ())), preferred_element_type=F32)


def _mm_tn(a, b):
    return lax.dot_general(a, b, (((0,), (0,)), ((), ())), preferred_element_type=F32)


def _sigmoid(x):
    return 1.0 / (1.0 + jnp.exp(-x))


def _params(sem, vmem=VMEM_LIMIT):
    return pltpu.CompilerParams(dimension_semantics=sem, vmem_limit_bytes=vmem)


def _const_spec(shape):
    nd = len(shape)
    return pl.BlockSpec(shape, lambda *_: (0,) * nd, pipeline_mode=pl.Buffered(1))


def _rms_fwd(x, g):
    r = lax.rsqrt(jnp.mean(x * x, axis=-1, keepdims=True) + RMS_EPS)
    xh = x * r
    return xh * g, xh, r


def _rms_bwd(dy, xh, r, g):
    dxh = dy * g
    dx = r * (dxh - xh * jnp.mean(dxh * xh, axis=-1, keepdims=True))
    return dx, dy * xh


def _in_proj(x, g1, w_uqkv, w_fl, w_g, token):
    T = x.shape[0]
    tm = ROW_TILE

    def body(x_ref, g_ref, wa_ref, wf_ref, wg_ref, token_ref, h_ref, u_ref, qkv_ref, fl_ref, gt_ref):
        h, _, _ = _rms_fwd(x_ref[...], g_ref[...])
        hb = h.astype(BF16)
        h_ref[...] = hb
        z = _mm(hb, wa_ref[...])
        u_ref[...] = z[:, :POOL_WIDTH]
        qkv_ref[...] = z[:, POOL_WIDTH:].astype(BF16)
        fl_ref[...] = _mm(hb, wf_ref[...])
        gt_ref[...] = _mm(hb, wg_ref[...]).astype(BF16)

    row = lambda n: pl.BlockSpec((tm, n), lambda i: (i, 0))
    return pl.pallas_call(
        body,
        name="in_proj",
        grid=(T // tm,),
        in_specs=[row(D_MODEL), _const_spec((1, D_MODEL)), _const_spec(w_uqkv.shape), _const_spec(w_fl.shape), _const_spec(w_g.shape), _HBM],
        out_specs=[row(D_MODEL), row(POOL_WIDTH), row(3 * ATTN_WIDTH), row(FL_PAD), row(2 * D_MODEL)],
        out_shape=[
            jax.ShapeDtypeStruct((T, D_MODEL), BF16),
            jax.ShapeDtypeStruct((T, POOL_WIDTH), F32),
            jax.ShapeDtypeStruct((T, 3 * ATTN_WIDTH), BF16),
            jax.ShapeDtypeStruct((T, FL_PAD), F32),
            jax.ShapeDtypeStruct((T, 2 * D_MODEL), BF16),
        ],
        compiler_params=_params(("parallel",)),
    )(x, g1, w_uqkv, w_fl, w_g, token)


def _log_sigmoid(x):
    return jnp.minimum(x, 0.0) - jnp.log(1.0 + jnp.exp(-jnp.abs(x)))


def _forget_fwd(fl, b_pad, n_seq, S):
    def body(fl_ref, b_ref, fcol_ref):
        lf = _log_sigmoid(fl_ref[...] + b_ref[...])
        t = lf.T
        lane = lax.broadcasted_iota(jnp.int32, t.shape, 1)
        k = 1
        while k < S:
            t = t + jnp.where(lane >= k, pltpu.roll(t, k, 1), 0.0)
            k *= 2
        fcol_ref[...] = t.T

    return pl.pallas_call(
        body,
        name="forget_fwd",
        grid=(n_seq,),
        in_specs=[pl.BlockSpec((S, FL_PAD), lambda s: (s, 0)), _const_spec((1, FL_PAD))],
        out_specs=pl.BlockSpec((S, FL_PAD), lambda s: (s, 0)),
        out_shape=jax.ShapeDtypeStruct((n_seq * S, FL_PAD), F32),
        compiler_params=_params(("parallel",)),
    )(fl, b_pad)


def _window_pick(g, v2, v4, v8, v16):
    return jnp.where(g == 0, v2, jnp.where(g == 1, v4, jnp.where(g == 2, v8, v16)))


def _pool_fwd(u, mix_b, scale, n_seq, S):
    T = n_seq * S

    def body(u_ref, mix_ref, sc_ref, pm_ref, p2_ref, p3_ref):
        g = pl.program_id(1)
        uu = u_ref[...]
        row = lax.broadcasted_iota(jnp.int32, uu.shape, 0)

        def back(a, k):
            return jnp.where(row >= k, pltpu.roll(a, k, 0), 0.0)

        s2 = uu + back(uu, 1)
        s4 = s2 + back(s2, 2)
        s8 = s4 + back(s4, 4)
        s16 = s8 + back(s8, 8)
        w = _window_pick(g, 2.0, 4.0, 8.0, 16.0)
        cnt = jnp.minimum((row + 1).astype(F32), w)
        pm = _window_pick(g, s2, s4, s8, s16) / cnt - uu
        pmb = pm.astype(BF16)
        pm_ref[...] = pmb
        p2 = _mm(pmb, mix_ref[...])
        p2_ref[...] = p2
        p3_ref[...] = (p2 * sc_ref[...]).astype(BF16)

    grp = pl.BlockSpec((S, GROUP_DIM), lambda s, g: (s, g))
    return pl.pallas_call(
        body,
        name="pool_fwd",
        grid=(n_seq, len(POOL_WINDOWS)),
        in_specs=[
            grp,
            pl.BlockSpec((None, GROUP_DIM, GROUP_DIM), lambda s, g: (g, 0, 0)),
            pl.BlockSpec((1, GROUP_DIM), lambda s, g: (0, g)),
        ],
        out_specs=[grp, grp, grp],
        out_shape=[
            jax.ShapeDtypeStruct((T, POOL_WIDTH), BF16),
            jax.ShapeDtypeStruct((T, POOL_WIDTH), F32),
            jax.ShapeDtypeStruct((T, POOL_WIDTH), BF16),
        ],
        compiler_params=_params(("parallel", "parallel")),
    )(u, mix_b, scale)


def _split3(v):
    hi = v.astype(BF16).astype(F32)
    r = v - hi
    mid = r.astype(BF16).astype(F32)
    lo = (r - mid).astype(BF16).astype(F32)
    return hi, mid, lo


def _bias_lanes(v):
    hi, mid, lo = _split3(v)
    lane = lax.broadcasted_iota(jnp.int32, (1, LANES), 1)
    packed = jnp.where(lane < N_HEADS, hi, jnp.where(lane < 2 * N_HEADS, pltpu.roll(mid, N_HEADS, 1), pltpu.roll(lo, 2 * N_HEADS, 1)))
    return jnp.where(lane < 3 * N_HEADS, packed, 0.0).astype(BF16)


def _bias_placement(slot):
    row = lax.broadcasted_iota(jnp.int32, (LANES, N_HEADS * LANES), 0)
    col = lax.broadcasted_iota(jnp.int32, (LANES, N_HEADS * LANES), 1)
    h = col // LANES
    n = col % LANES - jnp.where(h % 2 == 0, HEAD_DIM, 0) - 3 * slot
    return ((n >= 0) & (n < 3) & (row == N_HEADS * n + h)).astype(BF16)


def _augment(xp, h, bias, ones_slot):
    lane = lax.broadcasted_iota(jnp.int32, (1, LANES), 1)
    hh = h % 2
    head = (lane >= HEAD_DIM * hh) & (lane < HEAD_DIM * (hh + 1))
    b = HEAD_DIM * (1 - hh)
    rest = jnp.zeros_like(xp) if bias is None else bias[:, h * LANES : (h + 1) * LANES]
    out = jnp.where(head, xp, rest)
    if ones_slot is not None:
        out = jnp.where((lane >= b + 3 * ones_slot) & (lane < b + 3 * ones_slot + 3), jnp.ones_like(xp), out)
    return out


def _attn_fwd(qkv, fcol, n_seq, S):
    T = n_seq * S
    tb = ATTN_BLOCK
    nq = S // tb
    scale = HEAD_DIM ** -0.5

    def body(q_ref, k_ref, v_ref, fc_ref, o_ref, st_ref, qa_sc, ka_sc, m_sc, l_sc, acc_sc):
        i = pl.program_id(1)
        lane = lax.broadcasted_iota(jnp.int32, (1, LANES), 1)
        low = lane < HEAD_DIM

        @pl.when(i == 0)
        def _():
            place = _bias_placement(1)

            def rows_ka(r, carry):
                r0 = pl.multiple_of(r * tb, tb)
                bias = _mm(_bias_lanes(-fc_ref[pl.ds(r0, tb), :]), place).astype(BF16)
                for h in range(N_HEADS):
                    kp = k_ref[pl.ds(r0, tb), (h // 2) * LANES : (h // 2 + 1) * LANES] * scale
                    ka_sc[h, pl.ds(r0, tb), :] = _augment(kp, h, bias, 0)
                return carry

            lax.fori_loop(0, nq, rows_ka, 0)

        q0 = pl.multiple_of(i * tb, tb)
        bias = _mm(_bias_lanes(fc_ref[pl.ds(q0, tb), :]), _bias_placement(0)).astype(BF16)
        for h in range(N_HEADS):
            qa_sc[h] = _augment(q_ref[:, (h // 2) * LANES : (h // 2 + 1) * LANES], h, bias, 1)
        m_sc[...] = jnp.full(m_sc.shape, -jnp.inf, F32)
        l_sc[...] = jnp.zeros_like(l_sc)
        acc_sc[...] = jnp.zeros_like(acc_sc)
        causal = lax.broadcasted_iota(jnp.int32, (tb, tb), 1) <= lax.broadcasted_iota(jnp.int32, (tb, tb), 0)

        def step(j, masked):
            c0 = pl.multiple_of(j * tb, tb)
            for p in range(N_PAIRS):
                vb = v_ref[pl.ds(c0, tb), p * LANES : (p + 1) * LANES]
                pv, al = [], []
                for hh in range(2):
                    h = 2 * p + hh
                    s = _mm_nt(qa_sc[h], ka_sc[h, pl.ds(c0, tb), :])
                    if masked:
                        s = jnp.where(causal, s, -jnp.inf)
                    m_old = m_sc[h]
                    m_new = jnp.maximum(m_old, jnp.max(s, axis=1, keepdims=True))
                    alpha = jnp.exp(m_old - m_new)
                    pe = jnp.exp(s - jnp.concatenate([m_new] * (tb // LANES), axis=1))
                    l_sc[h] = alpha * l_sc[h] + jnp.sum(pe, axis=1, keepdims=True)
                    m_sc[h] = m_new
                    pv.append(_mm(pe.astype(BF16), vb))
                    al.append(alpha)
                acc_sc[p] = jnp.where(low, al[0], al[1]) * acc_sc[p] + jnp.where(low, pv[0], pv[1])

        def loop_body(j, carry):
            step(j, False)
            return carry

        lax.fori_loop(0, i, loop_body, 0)
        step(i, True)
        st = jnp.zeros((tb, LANES), F32)
        for p in range(N_PAIRS):
            lp = jnp.where(low, l_sc[2 * p], l_sc[2 * p + 1])
            o_ref[:, p * LANES : (p + 1) * LANES] = (acc_sc[p] / lp).astype(BF16)
            for h in (2 * p, 2 * p + 1):
                st = jnp.where(lane == h, m_sc[h] + jnp.log(l_sc[h]), st)
        st_ref[...] = st

    return pl.pallas_call(
        body,
        name="attn_fwd",
        grid=(n_seq, nq),
        in_specs=[
            pl.BlockSpec((tb, ATTN_WIDTH), lambda s, i: (s * nq + i, 0)),
            pl.BlockSpec((S, ATTN_WIDTH), lambda s, i: (s, 1)),
            pl.BlockSpec((S, ATTN_WIDTH), lambda s, i: (s, 2)),
            pl.BlockSpec((S, LANES), lambda s, i: (s, 0)),
        ],
        out_specs=[
            pl.BlockSpec((tb, ATTN_WIDTH), lambda s, i: (s * nq + i, 0)),
            pl.BlockSpec((tb, LANES), lambda s, i: (s * nq + i, 0)),
        ],
        out_shape=[jax.ShapeDtypeStruct((T, ATTN_WIDTH), BF16), jax.ShapeDtypeStruct((T, LANES), F32)],
        scratch_shapes=[
            pltpu.VMEM((N_HEADS, tb, LANES), BF16),
            pltpu.VMEM((N_HEADS, S, LANES), BF16),
            pltpu.VMEM((N_HEADS, tb, LANES), F32),
            pltpu.VMEM((N_HEADS, tb, LANES), F32),
            pltpu.VMEM((N_PAIRS, tb, LANES), F32),
        ],
        compiler_params=_params(("parallel", "arbitrary")),
    )(qkv, qkv, qkv, fcol)


def _mix_out(a, p3, gates, x, w_ao, w_po, w_out):
    T = x.shape[0]
    tm = ROW_TILE

    def body(a_ref, p3_ref, gt_ref, x_ref, wao_ref, wpo_ref, wout_ref, mg_ref, x1_ref, ay_ref, py_ref):
        ay = _mm(a_ref[...], wao_ref[...])
        py = _mm(p3_ref[...], wpo_ref[...])
        ay_ref[...] = ay.astype(BF16)
        py_ref[...] = py.astype(BF16)
        sp = _sigmoid(gt_ref[:, :D_MODEL].astype(F32))
        sa = _sigmoid(gt_ref[:, D_MODEL:].astype(F32))
        mb = (sp * py + sa * ay).astype(BF16)
        mg_ref[...] = mb
        x1_ref[...] = x_ref[...] + _mm(mb, wout_ref[...])

    row = lambda n: pl.BlockSpec((tm, n), lambda i: (i, 0))
    return pl.pallas_call(
        body,
        name="mix_out",
        grid=(T // tm,),
        in_specs=[
            row(ATTN_WIDTH), row(POOL_WIDTH), row(2 * D_MODEL), row(D_MODEL),
            _const_spec(w_ao.shape), _const_spec(w_po.shape), _const_spec(w_out.shape),
        ],
        out_specs=[row(D_MODEL), row(D_MODEL), row(D_MODEL), row(D_MODEL)],
        out_shape=[
            jax.ShapeDtypeStruct((T, D_MODEL), BF16), jax.ShapeDtypeStruct((T, D_MODEL), F32),
            jax.ShapeDtypeStruct((T, D_MODEL), BF16), jax.ShapeDtypeStruct((T, D_MODEL), BF16),
        ],
        compiler_params=_params(("parallel",)),
    )(a, p3, gates, x, w_ao, w_po, w_out)


def _ffn_fwd(x1, g2, gf, tgt, w_gate_t, w_up_t, w_down):
    T = x1.shape[0]
    tm = min(T, FF_ROW_TILE)
    nt = T // tm
    nc = D_FF // FF_CHUNK

    def body(x1_ref, g2_ref, gf_ref, tg_ref, wg_ref, wu_ref, wd_ref, h2_ref, gate_ref, up_ref, act_ref, dx2_ref, loss_ref, dgf_ref):
        x1v = x1_ref[...]
        h2, _, _ = _rms_fwd(x1v, g2_ref[...])
        h2b = h2.astype(BF16)
        h2_ref[...] = h2b
        for c in range(nc):
            sl = slice(c * FF_CHUNK, (c + 1) * FF_CHUNK)
            gate = _mm_nt(h2b, wg_ref[sl, :])
            up = _mm_nt(h2b, wu_ref[sl, :])
            gate_ref[:, sl] = gate.astype(BF16)
            up_ref[:, sl] = up.astype(BF16)
            act_ref[:, sl] = (gate * _sigmoid(gate) * up).astype(BF16)
        acc = x1v + _mm(act_ref[...], wd_ref[...])
        gfv = gf_ref[...]
        y, xh, r = _rms_fwd(acc, gfv)
        err = y - tg_ref[...]
        part = 0.5 * jnp.sum(jnp.mean(err * err, axis=-1, keepdims=True), axis=0, keepdims=True)
        dx2, dgrow = _rms_bwd(err * (1.0 / D_MODEL), xh, r, gfv)
        dx2_ref[...] = dx2

        @pl.when(pl.program_id(0) == 0)
        def _():
            dgf_ref[...] = jnp.zeros_like(dgf_ref)
            loss_ref[...] = jnp.zeros_like(loss_ref)

        dgf_ref[...] += jnp.sum(dgrow, axis=0, keepdims=True)
        loss_ref[...] += jnp.broadcast_to(part, loss_ref.shape)

    row = lambda n: pl.BlockSpec((tm, n), lambda i: (i, 0))
    return pl.pallas_call(
        body,
        name="ffn_fwd",
        grid=(nt,),
        in_specs=[
            row(D_MODEL), _const_spec((1, D_MODEL)), _const_spec((1, D_MODEL)), row(D_MODEL),
            _const_spec(w_gate_t.shape), _const_spec(w_up_t.shape), _const_spec(w_down.shape),
        ],
        out_specs=[
            row(D_MODEL), row(D_FF), row(D_FF), row(D_FF), row(D_MODEL),
            pl.BlockSpec((8, LANES), lambda i: (0, 0)),
            pl.BlockSpec((1, D_MODEL), lambda i: (0, 0)),
        ],
        out_shape=[
            jax.ShapeDtypeStruct((T, D_MODEL), BF16),
            jax.ShapeDtypeStruct((T, D_FF), BF16),
            jax.ShapeDtypeStruct((T, D_FF), BF16),
            jax.ShapeDtypeStruct((T, D_FF), BF16),
            jax.ShapeDtypeStruct((T, D_MODEL), F32),
            jax.ShapeDtypeStruct((8, LANES), F32),
            jax.ShapeDtypeStruct((1, D_MODEL), F32),
        ],
        compiler_params=_params(("arbitrary",)),
    )(x1, g2, gf, tgt, w_gate_t, w_up_t, w_down)


def _ffn_bwd(dx2, gate, up, x1, g2, w_gate_t, w_up_t, w_down):
    T = x1.shape[0]
    tm = min(T, FF_ROW_TILE)
    nc = D_FF // FF_CHUNK

    def body(dx2_ref, gate_ref, up_ref, x1_ref, g2_ref, wg_ref, wu_ref, wd_ref, dgate_ref, dup_ref, dx1_ref, dg2_ref):
        dx2v = dx2_ref[...]
        dx2b = dx2v.astype(BF16)
        for c in range(nc):
            sl = slice(c * FF_CHUNK, (c + 1) * FF_CHUNK)
            dact = _mm_nt(dx2b, wd_ref[sl, :])
            gate = gate_ref[:, sl].astype(F32)
            sg = _sigmoid(gate)
            silu = gate * sg
            dgate = (dact * up_ref[:, sl].astype(F32) * (sg * (1.0 + gate * (1.0 - sg)))).astype(BF16)
            dup = (dact * silu).astype(BF16)
            dgate_ref[:, sl] = dgate
            dup_ref[:, sl] = dup
        dh2 = _mm(dgate_ref[...], wg_ref[...]) + _mm(dup_ref[...], wu_ref[...])
        g2v = g2_ref[...]
        _, xh, r = _rms_fwd(x1_ref[...], g2v)
        dxn, dgrow = _rms_bwd(dh2, xh, r, g2v)
        dx1_ref[...] = dx2v + dxn

        @pl.when(pl.program_id(0) == 0)
        def _():
            dg2_ref[...] = jnp.zeros_like(dg2_ref)

        dg2_ref[...] += jnp.sum(dgrow, axis=0, keepdims=True)

    row = lambda n: pl.BlockSpec((tm, n), lambda i: (i, 0))
    return pl.pallas_call(
        body,
        name="ffn_bwd",
        grid=(T // tm,),
        in_specs=[
            row(D_MODEL), row(D_FF), row(D_FF), row(D_MODEL), _const_spec((1, D_MODEL)),
            _const_spec(w_gate_t.shape), _const_spec(w_up_t.shape), _const_spec(w_down.shape),
        ],
        out_specs=[row(D_FF), row(D_FF), row(D_MODEL), pl.BlockSpec((1, D_MODEL), lambda i: (0, 0))],
        out_shape=[
            jax.ShapeDtypeStruct((T, D_FF), BF16),
            jax.ShapeDtypeStruct((T, D_FF), BF16),
            jax.ShapeDtypeStruct((T, D_MODEL), F32),
            jax.ShapeDtypeStruct((1, D_MODEL), F32),
        ],
        compiler_params=_params(("arbitrary",), VMEM_LIMIT_MAX),
    )(dx2, gate, up, x1, g2, w_gate_t, w_up_t, w_down)


def _mix_bwd(dx1, gates, pool_y, attn_y, p2, scale, w_out, w_ao, w_po, token):
    T = dx1.shape[0]
    tm = ROW_TILE

    def body(dx1_ref, gt_ref, py_ref, ay_ref, p2_ref, sc_ref, wout_ref, wao_ref, wpo_ref, token_ref, dgt_ref, dpy_ref, day_ref, da_ref, dp2_ref, dsc_ref):
        dm = _mm_nt(dx1_ref[...].astype(BF16), wout_ref[...])
        sp = _sigmoid(gt_ref[:, :D_MODEL].astype(F32))
        sa = _sigmoid(gt_ref[:, D_MODEL:].astype(F32))
        dgt_ref[:, :D_MODEL] = (dm * py_ref[...].astype(F32) * (sp * (1.0 - sp))).astype(BF16)
        dgt_ref[:, D_MODEL:] = (dm * ay_ref[...].astype(F32) * (sa * (1.0 - sa))).astype(BF16)
        dpy = (dm * sp).astype(BF16)
        day = (dm * sa).astype(BF16)
        dpy_ref[...] = dpy
        day_ref[...] = day
        da_ref[...] = _mm_nt(day, wao_ref[...]).astype(BF16)
        dp3 = _mm_nt(dpy, wpo_ref[...])
        dp2_ref[...] = (dp3 * sc_ref[...]).astype(BF16)

        @pl.when(pl.program_id(0) == 0)
        def _():
            dsc_ref[...] = jnp.zeros_like(dsc_ref)

        dsc_ref[...] += jnp.sum(dp3 * p2_ref[...], axis=0, keepdims=True)

    row = lambda n: pl.BlockSpec((tm, n), lambda i: (i, 0))
    return pl.pallas_call(
        body,
        name="mix_bwd",
        grid=(T // tm,),
        in_specs=[
            row(D_MODEL), row(2 * D_MODEL), row(D_MODEL), row(D_MODEL), row(POOL_WIDTH), _const_spec((1, POOL_WIDTH)),
            _const_spec(w_out.shape), _const_spec(w_ao.shape), _const_spec(w_po.shape), _HBM,
        ],
        out_specs=[row(2 * D_MODEL), row(D_MODEL), row(D_MODEL), row(ATTN_WIDTH), row(POOL_WIDTH), pl.BlockSpec((1, POOL_WIDTH), lambda i: (0, 0))],
        out_shape=[
            jax.ShapeDtypeStruct((T, 2 * D_MODEL), BF16),
            jax.ShapeDtypeStruct((T, D_MODEL), BF16),
            jax.ShapeDtypeStruct((T, D_MODEL), BF16),
            jax.ShapeDtypeStruct((T, ATTN_WIDTH), BF16),
            jax.ShapeDtypeStruct((T, POOL_WIDTH), BF16),
            jax.ShapeDtypeStruct((1, POOL_WIDTH), F32),
        ],
        compiler_params=_params(("arbitrary",)),
    )(dx1, gates, pool_y, attn_y, p2, scale, w_out, w_ao, w_po, token)


def _pool_bwd(dp2, pm, mix_b, token, n_seq, S):
    T = n_seq * S

    def body(dp2_ref, pm_ref, mix_ref, token_ref, du_ref, dmix_ref):
        g = pl.program_id(0)
        dp2v = dp2_ref[...]
        dpm = _mm_nt(dp2v, mix_ref[...])
        row = lax.broadcasted_iota(jnp.int32, dpm.shape, 0)
        w = _window_pick(g, 2.0, 4.0, 8.0, 16.0)
        e = dpm / jnp.minimum((row + 1).astype(F32), w)

        def ahead(a, k):
            return jnp.where(row < S - k, pltpu.roll(a, S - k, 0), 0.0)

        r2 = e + ahead(e, 1)
        r4 = r2 + ahead(r2, 2)
        r8 = r4 + ahead(r4, 4)
        r16 = r8 + ahead(r8, 8)
        du_ref[...] = (_window_pick(g, r2, r4, r8, r16) - dpm).astype(BF16)

        @pl.when(pl.program_id(1) == 0)
        def _():
            dmix_ref[...] = jnp.zeros_like(dmix_ref)

        dmix_ref[...] += _mm_tn(pm_ref[...], dp2v)

    grp = pl.BlockSpec((S, GROUP_DIM), lambda g, s: (s, g))
    mixs = pl.BlockSpec((None, GROUP_DIM, GROUP_DIM), lambda g, s: (g, 0, 0))
    return pl.pallas_call(
        body,
        name="pool_bwd",
        grid=(len(POOL_WINDOWS), n_seq),
        in_specs=[grp, grp, mixs, _HBM],
        out_specs=[grp, mixs],
        out_shape=[jax.ShapeDtypeStruct((T, POOL_WIDTH), BF16), jax.ShapeDtypeStruct((len(POOL_WINDOWS), GROUP_DIM, GROUP_DIM), F32)],
        compiler_params=_params(("parallel", "arbitrary")),
    )(dp2, pm, mix_b, token)


def _attn_bwd(qkv, da, a, fcol, lse, n_seq, S):
    T = n_seq * S
    tb = ATTN_BLOCK
    nb = S // tb
    scale = HEAD_DIM ** -0.5

    def body(q_ref, k_ref, v_ref, do_ref, o_ref, fc_ref, st_ref, dq_ref, dk_ref, dv_ref, dfk_ref, dfq_ref,
             qa_sc, doa_sc, dq_acc, ka_sc, va_sc, dk_sc, dv_sc):
        j = pl.program_id(1)
        lane = lax.broadcasted_iota(jnp.int32, (1, LANES), 1)
        low = lane < HEAD_DIM

        @pl.when(j == 0)
        def _():
            dq_acc[...] = jnp.zeros_like(dq_acc)
            place = _bias_placement(0)

            def rows_q(i, carry):
                r0 = pl.multiple_of(i * tb, tb)
                delta = jnp.zeros((tb, LANES), F32)
                for h in range(N_HEADS):
                    pair = slice((h // 2) * LANES, (h // 2 + 1) * LANES)
                    prod = do_ref[pl.ds(r0, tb), pair].astype(F32) * o_ref[pl.ds(r0, tb), pair].astype(F32)
                    head = (lane >= HEAD_DIM * (h % 2)) & (lane < HEAD_DIM * (h % 2 + 1))
                    delta = jnp.where(lane == h, jnp.sum(jnp.where(head, prod, 0.0), axis=1, keepdims=True), delta)
                cq = fc_ref[pl.ds(r0, tb), :] - st_ref[pl.ds(r0, tb), :]
                q_bias = _mm(_bias_lanes(cq), place).astype(BF16)
                do_bias = _mm(_bias_lanes(-delta), place).astype(BF16)
                for h in range(N_HEADS):
                    pair = slice((h // 2) * LANES, (h // 2 + 1) * LANES)
                    qa_sc[h, pl.ds(r0, tb), :] = _augment(q_ref[pl.ds(r0, tb), pair], h, q_bias, 1)
                    doa_sc[h, pl.ds(r0, tb), :] = _augment(do_ref[pl.ds(r0, tb), pair], h, do_bias, None)
                return carry

            lax.fori_loop(0, nb, rows_q, 0)

        c0 = pl.multiple_of(j * tb, tb)
        k_bias = _mm(_bias_lanes(-fc_ref[pl.ds(c0, tb), :]), _bias_placement(1)).astype(BF16)
        for h in range(N_HEADS):
            pair = slice((h // 2) * LANES, (h // 2 + 1) * LANES)
            ka_sc[h] = _augment(k_ref[:, pair] * scale, h, k_bias, 0)
            va_sc[h] = _augment(v_ref[:, pair], h, None, 0)
        dk_sc[...] = jnp.zeros_like(dk_sc)
        dv_sc[...] = jnp.zeros_like(dv_sc)
        causal = lax.broadcasted_iota(jnp.int32, (tb, tb), 1) <= lax.broadcasted_iota(jnp.int32, (tb, tb), 0)

        def step(i, masked):
            r0 = pl.multiple_of(i * tb, tb)
            for h in range(N_HEADS):
                dob = do_ref[pl.ds(r0, tb), (h // 2) * LANES : (h // 2 + 1) * LANES]
                qa = qa_sc[h, pl.ds(r0, tb), :]
                s = _mm_nt(qa, ka_sc[h])
                if masked:
                    s = jnp.where(causal, s, -jnp.inf)
                pr = jnp.exp(s)
                dv_sc[h] += _mm_tn(pr.astype(BF16), dob)
                dsb = (pr * _mm_nt(doa_sc[h, pl.ds(r0, tb), :], va_sc[h])).astype(BF16)
                dk_sc[h] += _mm_tn(dsb, qa)
                dq_acc[h, pl.ds(r0, tb), :] += _mm(dsb, ka_sc[h])

        step(j, True)

        def loop_body(i, carry):
            step(i, False)
            return carry

        lax.fori_loop(j + 1, nb, loop_body, 0)
        dfk = jnp.zeros((tb, LANES), F32)
        for p in range(N_PAIRS):
            dk_ref[:, p * LANES : (p + 1) * LANES] = (jnp.where(low, dk_sc[2 * p], dk_sc[2 * p + 1]) * scale).astype(BF16)
            dv_ref[:, p * LANES : (p + 1) * LANES] = jnp.where(low, dv_sc[2 * p], dv_sc[2 * p + 1]).astype(BF16)
            for hh in range(2):
                b = HEAD_DIM * (1 - hh) + 3
                dfk = jnp.where(lane == 2 * p + hh, -dk_sc[2 * p + hh][:, b : b + 1], dfk)
        dfk_ref[...] = dfk

        @pl.when(j == nb - 1)
        def _():
            def rows_dq(i, carry):
                r0 = pl.multiple_of(i * tb, tb)
                dfq = jnp.zeros((tb, LANES), F32)
                for p in range(N_PAIRS):
                    parts = [dq_acc[2 * p + hh, pl.ds(r0, tb), :] for hh in range(2)]
                    dq_ref[pl.ds(r0, tb), p * LANES : (p + 1) * LANES] = jnp.where(low, parts[0], parts[1]).astype(BF16)
                    for hh in range(2):
                        b = HEAD_DIM * (1 - hh)
                        dfq = jnp.where(lane == 2 * p + hh, parts[hh][:, b : b + 1], dfq)
                dfq_ref[pl.ds(r0, tb), :] = dfq
                return carry

            lax.fori_loop(0, nb, rows_dq, 0)

    seq = lambda w, col: pl.BlockSpec((S, w), lambda s, j: (s, col))
    blk = lambda w, col: pl.BlockSpec((tb, w), lambda s, j: (s * nb + j, col))
    return pl.pallas_call(
        body,
        name="attn_bwd",
        grid=(n_seq, nb),
        in_specs=[seq(ATTN_WIDTH, 0), blk(ATTN_WIDTH, 1), blk(ATTN_WIDTH, 2), seq(ATTN_WIDTH, 0), seq(ATTN_WIDTH, 0), seq(LANES, 0), seq(LANES, 0)],
        out_specs=[seq(ATTN_WIDTH, 0), blk(ATTN_WIDTH, 0), blk(ATTN_WIDTH, 0), blk(LANES, 0), seq(LANES, 0)],
        out_shape=[
            jax.ShapeDtypeStruct((T, ATTN_WIDTH), BF16),
            jax.ShapeDtypeStruct((T, ATTN_WIDTH), BF16),
            jax.ShapeDtypeStruct((T, ATTN_WIDTH), BF16),
            jax.ShapeDtypeStruct((T, LANES), F32),
            jax.ShapeDtypeStruct((T, LANES), F32),
        ],
        scratch_shapes=[
            pltpu.VMEM((N_HEADS, S, LANES), BF16),
            pltpu.VMEM((N_HEADS, S, LANES), BF16),
            pltpu.VMEM((N_HEADS, S, LANES), F32),
            pltpu.VMEM((N_HEADS, tb, LANES), BF16),
            pltpu.VMEM((N_HEADS, tb, LANES), BF16),
            pltpu.VMEM((N_HEADS, tb, LANES), F32),
            pltpu.VMEM((N_HEADS, tb, LANES), F32),
        ],
        compiler_params=_params(("parallel", "arbitrary")),
    )(qkv, qkv, qkv, da, a, fcol, lse)


def _forget_bwd(dfk, dfq, fl, b_pad, n_seq, S):
    def body(df_ref, dfq_ref, fl_ref, b_ref, dfl_ref, db_ref):
        t = (df_ref[...] + dfq_ref[...]).T
        lane = lax.broadcasted_iota(jnp.int32, t.shape, 1)
        k = 1
        while k < S:
            t = t + jnp.where(lane < S - k, pltpu.roll(t, S - k, 1), 0.0)
            k *= 2
        dfl = t.T * _sigmoid(-(fl_ref[...] + b_ref[...]))
        dfl_ref[...] = dfl.astype(BF16)

        @pl.when(pl.program_id(0) == 0)
        def _():
            db_ref[...] = jnp.zeros_like(db_ref)

        db_ref[...] += jnp.sum(dfl, axis=0, keepdims=True)

    return pl.pallas_call(
        body,
        name="forget_bwd",
        grid=(n_seq,),
        in_specs=[
            pl.BlockSpec((S, LANES), lambda s: (s, 0)),
            pl.BlockSpec((S, LANES), lambda s: (s, 0)),
            pl.BlockSpec((S, FL_PAD), lambda s: (s, 0)),
            _const_spec((1, FL_PAD)),
        ],
        out_specs=[pl.BlockSpec((S, FL_PAD), lambda s: (s, 0)), pl.BlockSpec((1, FL_PAD), lambda s: (0, 0))],
        out_shape=[jax.ShapeDtypeStruct((n_seq * S, FL_PAD), BF16), jax.ShapeDtypeStruct((1, FL_PAD), F32)],
        compiler_params=_params(("arbitrary",)),
    )(dfk, dfq, fl, b_pad)


def _in_proj_bwd(du, dq, dk, dv, dfl, dgates, x, dx1, g1, w_uqkv, w_fl, w_g):
    T = x.shape[0]
    tm = ROW_TILE

    def body(du_ref, dq_ref, dk_ref, dv_ref, dfl_ref, dgt_ref, x_ref, dx1_ref, g_ref, wa_ref, wf_ref, wg_ref, dx_ref, dg_ref):
        dz = jnp.concatenate([du_ref[...], dq_ref[...], dk_ref[...], dv_ref[...]], axis=1)
        dh = _mm_nt(dz, wa_ref[...]) + _mm_nt(dgt_ref[...], wg_ref[...]) + _mm_nt(dfl_ref[...], wf_ref[...])
        gv = g_ref[...]
        _, xh, r = _rms_fwd(x_ref[...], gv)
        dxn, dgrow = _rms_bwd(dh, xh, r, gv)
        dx_ref[...] = dx1_ref[...] + dxn

        @pl.when(pl.program_id(0) == 0)
        def _():
            dg_ref[...] = jnp.zeros_like(dg_ref)

        dg_ref[...] += jnp.sum(dgrow, axis=0, keepdims=True)

    row = lambda n: pl.BlockSpec((tm, n), lambda i: (i, 0))
    return pl.pallas_call(
        body,
        name="in_proj_bwd",
        grid=(T // tm,),
        in_specs=[
            row(512), row(512), row(512), row(512), row(FL_PAD), row(2 * D_MODEL), row(D_MODEL), row(D_MODEL), _const_spec((1, D_MODEL)),
            _const_spec(w_uqkv.shape), _const_spec(w_fl.shape), _const_spec(w_g.shape),
        ],
        out_specs=[row(D_MODEL), pl.BlockSpec((1, D_MODEL), lambda i: (0, 0))],
        out_shape=[jax.ShapeDtypeStruct((T, D_MODEL), F32), jax.ShapeDtypeStruct((1, D_MODEL), F32)],
        compiler_params=_params(("arbitrary",)),
    )(du, dq, dk, dv, dfl, dgates, x, dx1, g1, w_uqkv, w_fl, w_g)


def _pick_block(n):
    for b in (512, 1408, 256, 128):
        if n % b == 0:
            return b
    raise ValueError(n)


def _matmul_tn(a, b, name):
    T, K = a.shape
    N = b.shape[1]
    bt, bk, bn = min(T, DW_TOKENS), _pick_block(K), _pick_block(N)
    nt = T // bt

    def body(a_ref, b_ref, o_ref, acc):
        @pl.when(pl.program_id(2) == 0)
        def _():
            acc[...] = jnp.zeros_like(acc)

        acc[...] += _mm_tn(a_ref[...].astype(BF16), b_ref[...].astype(BF16))

        @pl.when(pl.program_id(2) == nt - 1)
        def _():
            o_ref[...] = acc[...].astype(BF16)

    return pl.pallas_call(
        body,
        name=name,
        grid=(K // bk, N // bn, nt),
        in_specs=[pl.BlockSpec((bt, bk), lambda k, n, t: (t, k)), pl.BlockSpec((bt, bn), lambda k, n, t: (t, n))],
        out_specs=pl.BlockSpec((bk, bn), lambda k, n, t: (k, n)),
        out_shape=jax.ShapeDtypeStruct((K, N), BF16),
        scratch_shapes=[pltpu.VMEM((bk, bn), F32)],
        compiler_params=_params(("parallel", "parallel", "arbitrary")),
    )(a, b)


W_IN_A = POOL_WIDTH + 3 * ATTN_WIDTH
W_IN_SHARD = (W_IN_A + N_HEADS + 2 * D_MODEL) // N_DEV
_W_IN_PIECES = ((0, W_IN_A), (W_IN_A, W_IN_A + N_HEADS), (W_IN_A + N_HEADS, W_IN_A + N_HEADS + 2 * D_MODEL))


def _w_in_segments(d):
    lo, hi = d * W_IN_SHARD, (d + 1) * W_IN_SHARD
    out = []
    for p, (a, b) in enumerate(_W_IN_PIECES):
        s, e = max(lo, a), min(hi, b)
        if s < e:
            out.append((p, s - a, s - lo, e - s))
    return out


def _w_in_pieces(gathered):
    tm = ROW_TILE // 2

    def body(g_ref, wa_ref, wf_ref, wg_ref):
        outs = (wa_ref, wf_ref, wg_ref)
        wf_ref[...] = jnp.zeros_like(wf_ref)
        for d in range(N_DEV):
            for p, at, frm, n in _w_in_segments(d):
                outs[p][:, at : at + n] = g_ref[d, :, frm : frm + n]

    return pl.pallas_call(
        body,
        name="w_in_pieces",
        grid=(D_MODEL // tm,),
        in_specs=[pl.BlockSpec((N_DEV, tm, W_IN_SHARD), lambda i: (0, i, 0))],
        out_specs=[pl.BlockSpec((tm, W_IN_A), lambda i: (i, 0)), pl.BlockSpec((tm, FL_PAD), lambda i: (i, 0)), pl.BlockSpec((tm, 2 * D_MODEL), lambda i: (i, 0))],
        out_shape=[
            jax.ShapeDtypeStruct((D_MODEL, W_IN_A), gathered.dtype),
            jax.ShapeDtypeStruct((D_MODEL, FL_PAD), gathered.dtype),
            jax.ShapeDtypeStruct((D_MODEL, 2 * D_MODEL), gathered.dtype),
        ],
        compiler_params=_params(("parallel",)),
    )(gathered)


def _dw_in(h, du, dq, dk, dv, dfl, dgates, token):
    T = h.shape[0]
    bt, bk = min(T, DW_TOKENS // 2), 512
    nt = T // bt
    pieces = (du, dq, dk, dv, dfl, dgates)
    offs = [0]
    for p in pieces:
        offs.append(offs[-1] + p.shape[1])

    def body(h_ref, *rest):
        refs, o_ref, acc = rest[: len(pieces)], rest[-2], rest[-1]

        @pl.when(pl.program_id(1) == 0)
        def _():
            acc[...] = jnp.zeros_like(acc)

        ht = h_ref[...].T
        for ref, at in zip(refs, offs):
            acc[:, at : at + ref.shape[1]] += _mm(ht, ref[...])

        @pl.when(pl.program_id(1) == nt - 1)
        def _():
            starts = (0, W_IN_A, W_IN_A + FL_PAD)
            for d in range(N_DEV):
                for p, at, to, n in _w_in_segments(d):
                    o_ref[d % 2, d // 2, :, to : to + n] = acc[:, starts[p] + at : starts[p] + at + n].astype(BF16)

    return pl.pallas_call(
        body,
        name="dw_in",
        grid=(D_MODEL // bk, nt),
        in_specs=[pl.BlockSpec((bt, bk), lambda k, t: (t, k))] + [pl.BlockSpec((bt, p.shape[1]), lambda k, t: (t, 0)) for p in pieces] + [_HBM],
        out_specs=pl.BlockSpec((2, 4, bk, W_IN_SHARD), lambda k, t: (0, 0, k, 0)),
        out_shape=jax.ShapeDtypeStruct((2, 4, D_MODEL, W_IN_SHARD), BF16),
        scratch_shapes=[pltpu.VMEM((bk, offs[-1]), F32)],
        compiler_params=_params(("parallel", "arbitrary")),
    )(h, *pieces, token)


def _position():
    return lax.axis_index("x"), lax.axis_index("y"), lax.axis_index("c")


_HBM = pl.BlockSpec(memory_space=pl.ANY)


def _all_gather(blocks, name):
    n = len(blocks)

    def body(*refs):
        xs, outs = refs[:n], refs[n : 2 * n]
        send_sems, recv_sems, local_sems = refs[2 * n :]
        x, y, c = _position()
        me, sibling = (x, y, c), (x, y, 1 - c)
        chips = [(1 - x, y), (x, 1 - y), (1 - x, 1 - y)]

        def rows(a, px, py, pc):
            return outs[a].at[4 * px + 2 * py + pc]

        def copy(a, k, blk, to, src=None):
            return pltpu.make_async_remote_copy(
                src_ref=rows(a, *blk) if src is None else src, dst_ref=rows(a, *blk),
                send_sem=send_sems.at[7 * a + k], recv_sem=recv_sems.at[7 * a + k], device_id=to, device_id_type=MESH,
            )

        mine = [pltpu.make_async_copy(xs[a], rows(a, *me), local_sems.at[a]) for a in range(n)]
        for cp in mine:
            cp.start()
        first = []
        for a in range(n):
            first.append(copy(a, 0, me, sibling, src=xs[a]))
            first += [copy(a, 1 + j, me, (*chip, c), src=xs[a]) for j, chip in enumerate(chips)]
        for cp in first:
            cp.start()
        passed = []
        for j, chip in enumerate(chips):
            for a in range(n):
                copy(a, 1 + j, (*chip, c), me).wait_recv()
                passed.append(copy(a, 4 + j, (*chip, c), sibling))
                passed[-1].start()
        for a in range(n):
            copy(a, 0, sibling, me).wait_recv()
        for j, chip in enumerate(chips):
            for a in range(n):
                copy(a, 4 + j, (*chip, 1 - c), me).wait_recv()
        for cp in first + passed:
            cp.wait_send()
        for cp in mine:
            cp.wait()

    return pl.pallas_call(
        body,
        name=name,
        out_shape=[jax.ShapeDtypeStruct((N_DEV, *b.shape), b.dtype) for b in blocks],
        in_specs=[_HBM] * n,
        out_specs=[_HBM] * n,
        scratch_shapes=[pltpu.SemaphoreType.DMA((7 * n,)), pltpu.SemaphoreType.DMA((7 * n,)), pltpu.SemaphoreType.DMA((n,))],
    )(*blocks)


_SEM = pl.BlockSpec(memory_space=pltpu.SEMAPHORE)
_HBM_ONLY = pl.BlockSpec(memory_space=pltpu.HBM)
_SIDE_EFFECT = pltpu.SideEffectType.DATAFLOW_SIDE_EFFECTING


def _peer(x, y, c, k):
    return (1 - x if k & 4 else x, 1 - y if k & 2 else y, 1 - c if k & 1 else c)


_PEER_BITS = {"gather": range(1, N_DEV), "scatter": range(1, N_DEV), "chips": (4, 2, 6)}
_LAND_SLOTS = {"gather": N_DEV, "scatter": N_DEV, "chips": 3}


def _exchange_copies(src_refs, land_refs, send_sems, recv_sems, pattern, receive_side):
    x, y, c = _position()
    me = 4 * x + 2 * y + c
    bits = _PEER_BITS[pattern]
    cps = []
    for j, k in enumerate(bits):
        px, py, pc = _peer(x, y, c, k)
        peer = 4 * px + 2 * py + pc
        for a, (src, land) in enumerate(zip(src_refs, land_refs)):
            if pattern == "chips":
                s, slot = src.at[2 * px + py], j
            else:
                s, slot = (src if pattern == "gather" else src.at[peer]), (peer if receive_side else me)
            cps.append(pltpu.make_async_remote_copy(
                src_ref=s, dst_ref=land.at[slot],
                send_sem=send_sems.at[len(bits) * a + j], recv_sem=recv_sems.at[len(bits) * a + j],
                device_id=(px, py, pc), device_id_type=MESH,
            ))
    return cps


def _exchange_start(srcs, after, name, pattern):
    n = len(srcs)
    m = len(_PEER_BITS[pattern])
    lands = [jax.ShapeDtypeStruct((_LAND_SLOTS[pattern], *s.shape[-2:]), s.dtype) for s in srcs]

    def body(*refs):
        src_refs, land_refs = refs[1 : 1 + n], refs[1 + n : 1 + 2 * n]
        send_sems, recv_sems = refs[1 + 2 * n], refs[2 + 2 * n]
        token = refs[-1]
        for cp in _exchange_copies(src_refs, land_refs, send_sems, recv_sems, pattern, receive_side=False):
            cp.start()
        token[...] = jnp.zeros_like(token)

    hbm = lambda t: pltpu.with_memory_space_constraint(t, pltpu.HBM)
    out = pl.pallas_call(
        body,
        name=name,
        out_shape=(
            pltpu.SemaphoreType.DMA((m * n,)), pltpu.SemaphoreType.DMA((m * n,)),
            *[pltpu.HBM(s.shape, s.dtype) for s in srcs], *[pltpu.HBM(l.shape, l.dtype) for l in lands],
            jax.ShapeDtypeStruct((8, LANES), F32),
        ),
        in_specs=(_HBM, *[_HBM_ONLY] * (2 * n)),
        out_specs=(_SEM, _SEM, *[_HBM_ONLY] * (2 * n), pl.BlockSpec(memory_space=pltpu.VMEM)),
        input_output_aliases={1 + i: 2 + i for i in range(2 * n)},
        compiler_params=pltpu.CompilerParams(has_side_effects=_SIDE_EFFECT),
    )(after, *[hbm(s) for s in srcs], *[hbm(lax.empty(l.shape, l.dtype)) for l in lands])
    return out[0], out[1], out[2 : 2 + n], out[2 + n : 2 + 2 * n], out[-1]


def _exchange_wait(send_sems, recv_sems, srcs, lands, after, name, pattern):
    n = len(srcs)

    def body(*refs):
        src_refs, land_refs = refs[:n], refs[n : 2 * n]
        for cp in _exchange_copies(src_refs, land_refs, refs[2 * n], refs[2 * n + 1], pattern, receive_side=True):
            cp.wait_send()
            cp.wait_recv()

    out = pl.pallas_call(
        body,
        name=name,
        out_shape=(*[pltpu.HBM(s.shape, s.dtype) for s in srcs], *[pltpu.HBM(l.shape, l.dtype) for l in lands]),
        in_specs=(*[_HBM_ONLY] * (2 * n), _SEM, _SEM, _HBM),
        out_specs=tuple([_HBM_ONLY] * (2 * n)),
        input_output_aliases={i: i for i in range(2 * n)},
        compiler_params=pltpu.CompilerParams(has_side_effects=_SIDE_EFFECT),
    )(*srcs, *lands, send_sems, recv_sems, after)
    return out[:n], out[n:]


def _sibling_exchange(sends):
    n = len(sends)

    def body(*refs):
        srcs, dsts = refs[:n], refs[n : 2 * n]
        send_sems, recv_sems = refs[2 * n :]
        x, y, c = _position()
        cps = [
            pltpu.make_async_remote_copy(
                src_ref=srcs[a].at[1 - c], dst_ref=dsts[a], send_sem=send_sems.at[a], recv_sem=recv_sems.at[a],
                device_id=(x, y, 1 - c), device_id_type=MESH,
            )
            for a in range(n)
        ]
        for cp in cps:
            cp.start()
        for cp in cps:
            cp.wait()

    return pl.pallas_call(
        body,
        name="rs_sibling",
        out_shape=[jax.ShapeDtypeStruct(s.shape[1:], s.dtype) for s in sends],
        in_specs=[_HBM] * n,
        out_specs=[_HBM] * n,
        scratch_shapes=[pltpu.SemaphoreType.DMA((n,)), pltpu.SemaphoreType.DMA((n,))],
    )(*sends)


def _rows_tile(r):
    return ROW_TILE if r % ROW_TILE == 0 else r


def _pair_sum(send, got, core, name):
    _, _, r, c = send.shape
    br = _rows_tile(r)

    def body(core_ref, a_ref, b_ref, o_ref):
        o_ref[...] = (a_ref[...].astype(F32) + b_ref[...].astype(F32)).astype(o_ref.dtype)

    return pl.pallas_call(
        body,
        name=name,
        grid_spec=pltpu.PrefetchScalarGridSpec(
            num_scalar_prefetch=1,
            grid=(4, r // br),
            in_specs=[
                pl.BlockSpec((None, None, br, c), lambda n, i, core: (core[0], n, i, 0)),
                pl.BlockSpec((None, br, c), lambda n, i, core: (n, i, 0)),
            ],
            out_specs=pl.BlockSpec((None, br, c), lambda n, i, core: (n, i, 0)),
        ),
        out_shape=jax.ShapeDtypeStruct((4, r, c), send.dtype),
        compiler_params=_params(("parallel", "parallel")),
    )(core, send, got)


def _adamw(w, g, m, v):
    m = ADAM_B1 * m + (1.0 - ADAM_B1) * g
    v = ADAM_B2 * v + (1.0 - ADAM_B2) * (g * g)
    m_hat = m / (1.0 - ADAM_B1 ** ADAM_STEP)
    v_hat = v / (1.0 - ADAM_B2 ** ADAM_STEP)
    delta = -ADAM_LR * (m_hat / (jnp.sqrt(v_hat) + ADAM_EPS) + ADAM_WD * w)
    return delta, m, v


def _shard_update(send, got, recv, w, m, v, pos, name):
    _, r, c = w.shape
    br = _rows_tile(r)

    def body(pos_ref, a_ref, b_ref, r_ref, w_ref, m_ref, v_ref, g_ref, d_ref, nm_ref, nv_ref):
        g = a_ref[...].astype(F32) + b_ref[...].astype(F32)
        for n in range(3):
            g = g + r_ref[n].astype(F32)
        g_ref[...] = g
        d_ref[...], nm_ref[...], nv_ref[...] = _adamw(w_ref[...], g, m_ref[...], v_ref[...])

    own = pl.BlockSpec((None, br, c), lambda i, pos: (0, i, 0))
    return pl.pallas_call(
        body,
        name=name,
        grid_spec=pltpu.PrefetchScalarGridSpec(
            num_scalar_prefetch=1,
            grid=(r // br,),
            in_specs=[
                pl.BlockSpec((None, None, br, c), lambda i, pos: (pos[0], pos[1], i, 0)),
                pl.BlockSpec((None, br, c), lambda i, pos: (pos[1], i, 0)),
                pl.BlockSpec((3, br, c), lambda i, pos: (0, i, 0)),
                own, own, own,
            ],
            out_specs=[own, own, own, own],
        ),
        out_shape=[jax.ShapeDtypeStruct((1, r, c), F32)] * 4,
        compiler_params=_params(("parallel",)),
    )(pos, send, got, recv, w, m, v)


def _shard_update_direct(parts, chunks, w, m, v, me, name):
    _, r, c = w.shape
    br = _rows_tile(r)

    def body(me_ref, p_ref, own_ref, w_ref, m_ref, v_ref, g_ref, d_ref, nm_ref, nv_ref):
        g = None
        for n in range(N_DEV):
            part = jnp.where(me_ref[0] == n, own_ref[...], p_ref[n]).astype(F32)
            g = part if g is None else g + part
        g_ref[...] = g
        d_ref[...], nm_ref[...], nv_ref[...] = _adamw(w_ref[...], g, m_ref[...], v_ref[...])

    shard = pl.BlockSpec((None, br, c), lambda i, me: (0, i, 0))
    return pl.pallas_call(
        body,
        name=name,
        grid_spec=pltpu.PrefetchScalarGridSpec(
            num_scalar_prefetch=1,
            grid=(r // br,),
            in_specs=[
                pl.BlockSpec((N_DEV, br, c), lambda i, me: (0, i, 0)),
                pl.BlockSpec((None, br, c), lambda i, me: (me[0], i, 0)),
                shard, shard, shard,
            ],
            out_specs=[shard, shard, shard, shard],
        ),
        out_shape=[jax.ShapeDtypeStruct((1, r, c), F32)] * 4,
        compiler_params=_params(("parallel",)),
    )(me, parts, chunks, w, m, v)


def _small_update(parts, w, m, v):
    R = w.shape[0]

    def body(p_ref, w_ref, m_ref, v_ref, g_ref, d_ref, nm_ref, nv_ref):
        g = p_ref[0]
        for n in range(1, N_DEV):
            g = g + p_ref[n]
        g_ref[...] = g
        d_ref[...], nm_ref[...], nv_ref[...] = _adamw(w_ref[...], g, m_ref[...], v_ref[...])

    return pl.pallas_call(
        body,
        name="small_update",
        out_shape=[jax.ShapeDtypeStruct((R, LANES), F32)] * 4,
        compiler_params=pltpu.CompilerParams(vmem_limit_bytes=VMEM_LIMIT),
    )(parts, w, m, v)


_SHARD_AXIS = (1, 1, 1, 0, 0, 0, 0)
_TRANSPOSED = (False, False, False, False, True, True, False)


def _full_from_gathered(t, axis):
    if axis == 0:
        return t.reshape(N_DEV * t.shape[1], t.shape[2])
    return jnp.concatenate([t[d] for d in range(N_DEV)], axis=1)


def _chunks_from_cols(t):
    c = t.shape[1] // N_DEV
    return jnp.stack([t[:, d * c : (d + 1) * c] for d in range(N_DEV)])


_SMALL = (("norm1_g", 8), ("norm2_g", 8), ("norm_f_g", 8), ("b_forget", 8), ("pool_scale", 8), ("pool_mix", 512))
_SMALL_ROWS = sum(r for _, r in _SMALL) + 8


def _pack_small(vals, loss_row):
    parts = []
    for (name, rows), t in zip(_SMALL, vals):
        f = t.astype(F32).reshape(-1)
        f = jnp.concatenate([f, jnp.zeros((rows * LANES - f.shape[0],), F32)]).reshape(rows, LANES)
        parts.append(f)
    parts.append(loss_row)
    return jnp.concatenate(parts, axis=0)


def _unpack_small(packed, shapes):
    out, off = [], 0
    for (name, rows), shape in zip(_SMALL, shapes):
        n = 1
        for s in shape:
            n *= s
        out.append(packed[off : off + rows].reshape(-1)[:n].reshape(shape))
        off += rows
    return out, packed[off, 0]


def _local_grads(x, tgt, g1, g2, gf, b_forget, pool_mix, pool_scale, w_in, fwd_token, out_weights, ffn_weights, ffn_grads_out, out_grads_out, in_grads_out, small_grads_out):
    n_seq, S, _ = x.shape
    T = n_seq * S
    x2 = x.reshape(T, D_MODEL)
    tg2 = tgt.reshape(T, D_MODEL)
    w_uqkv, w_fl, w_g = w_in
    b_pad = jnp.concatenate([b_forget.reshape(1, N_HEADS), jnp.zeros((1, FL_PAD - N_HEADS), F32)], axis=1)
    mix_b = pool_mix.reshape(len(POOL_WINDOWS), GROUP_DIM, GROUP_DIM).astype(BF16)
    scale = pool_scale.reshape(1, POOL_WIDTH)
    g1 = g1.reshape(1, D_MODEL)
    g2 = g2.reshape(1, D_MODEL)
    gf = gf.reshape(1, D_MODEL)

    h, u, qkv, fl, gates = _in_proj(x2, g1, w_uqkv, w_fl, w_g, fwd_token)
    fcol = _forget_fwd(fl, b_pad, n_seq, S)
    pm, p2, p3 = _pool_fwd(u, mix_b, scale, n_seq, S)
    a, lse = _attn_fwd(qkv, fcol, n_seq, S)
    w_po, w_ao, w_out = out_weights(a)
    merged, x1, attn_y, pool_y = _mix_out(a, p3, gates, x2, w_ao, w_po, w_out)
    w_gate_t, w_up_t, w_down = ffn_weights(x1)
    h2, gate, up, act, dx2, loss_rows, dgf = _ffn_fwd(x1, g2, gf, tg2, w_gate_t, w_up_t, w_down)

    dgate, dup, dx1, dg2 = _ffn_bwd(dx2, gate, up, x1, g2, w_gate_t, w_up_t, w_down)
    bwd_token = ffn_grads_out(_matmul_tn(dgate, h2, "dw_ffn_gate"), _matmul_tn(dup, h2, "dw_ffn_up"), _matmul_tn(act, dx2, "dw_ffn_down"))
    dgates, dpy, day, da, dp2, dscale = _mix_bwd(dx1, gates, pool_y, attn_y, p2, scale, w_out, w_ao, w_po, bwd_token)
    out_token = out_grads_out(_matmul_tn(p3, dpy, "dw_pool_out"), _matmul_tn(a, day, "dw_attn_out"), _matmul_tn(merged, dx1, "dw_out"))
    du, dmix = _pool_bwd(dp2, pm, mix_b, out_token, n_seq, S)
    dq, dk, dv, dfk, dfq = _attn_bwd(qkv, da, a, fcol, lse, n_seq, S)
    dfl, db = _forget_bwd(dfk, dfq, fl, b_pad, n_seq, S)
    dx, dg1 = _in_proj_bwd(du, dq, dk, dv, dfl, dgates, x2, dx1, g1, w_uqkv, w_fl, w_g)
    small_token = small_grads_out((dg1, dg2, dgf, db[:, :N_HEADS], dscale, dmix), loss_rows)
    in_grads_out(_dw_in(h, du, dq, dk, dv, dfl, dgates, small_token))
    return dx.reshape(n_seq, S, D_MODEL)


def kernel(x, norm1_g, w_in, b_forget, pool_mix, pool_scale, w_pool_out, w_attn_out, w_out, norm2_g, w_ffn_gate, w_ffn_up, w_ffn_down, norm_f_g, loss_target, m_norm1_g, m_w_in, m_b_forget, m_pool_mix, m_pool_scale, m_w_pool_out, m_w_attn_out, m_w_out, m_norm2_g, m_w_ffn_gate, m_w_ffn_up, m_w_ffn_down, m_norm_f_g, v_norm1_g, v_w_in, v_b_forget, v_pool_mix, v_pool_scale, v_w_pool_out, v_w_attn_out, v_w_out, v_norm2_g, v_w_ffn_gate, v_w_ffn_up, v_w_ffn_down, v_norm_f_g):
    names = ("w_in", "w_pool_out", "w_attn_out", "w_out", "w_ffn_gate", "w_ffn_up", "w_ffn_down")
    w_sh = (w_in, w_pool_out, w_attn_out, w_out, w_ffn_gate, w_ffn_up, w_ffn_down)
    m_sh = (m_w_in, m_w_pool_out, m_w_attn_out, m_w_out, m_w_ffn_gate, m_w_ffn_up, m_w_ffn_down)
    v_sh = (v_w_in, v_w_pool_out, v_w_attn_out, v_w_out, v_w_ffn_gate, v_w_ffn_up, v_w_ffn_down)

    cx, cy, cc = _position()
    me = 4 * cx + 2 * cy + cc
    def stored(t, transposed):
        return jnp.transpose(t, (0, 2, 1)) if transposed else t

    w_sh, m_sh, v_sh = ([stored(t, tr) for t, tr in zip(ts, _TRANSPOSED)] for ts in (w_sh, m_sh, v_sh))
    shards = [w[0].astype(BF16) for w in w_sh]
    (gathered_in,) = _all_gather(shards[:1], "w_in_all_gather")
    out_sems = _exchange_start(shards[1:4], gathered_in, "out_weights_gather_start", "gather")
    ffn_sems = _exchange_start(shards[4:], out_sems[4], "ffn_weights_gather_start", "gather")
    no_order = jnp.zeros((8, LANES), F32)

    def with_own(lands, own):
        return [lax.dynamic_update_slice(l, o[None], (me, 0, 0)) for l, o in zip(lands, own)]

    def gathered_weights(sems, axes, name):
        def wait(after):
            send_sems, recv_sems, srcs, lands, _ = sems
            srcs, lands = _exchange_wait(send_sems, recv_sems, srcs, lands, after, name, "gather")
            return [_full_from_gathered(t, axis) for t, axis in zip(with_own(lands, srcs), axes)]

        return wait

    started = {}

    def scatter_grads(key, name):
        def start(*whole_grads):
            chunks = [
                _chunks_from_cols(t) if axis == 1 else t.reshape(N_DEV, -1, t.shape[1])
                for t, axis in zip(whole_grads, _SHARD_AXIS[key])
            ]
            started[key] = _exchange_start(chunks, no_order, name, "scatter")
            return started[key][4]

        return start

    def gather_small(small, loss_rows):
        started["small"] = _exchange_start([_pack_small(small, loss_rows)], no_order, "small_grads_gather_start", "gather")
        return started["small"][4]

    core = jnp.reshape(cc, (1,)).astype(jnp.int32)
    pos = jnp.stack([cc, 2 * cx + cy]).astype(jnp.int32)

    def reduce_w_in(send_in):
        (got_in,) = _sibling_exchange([send_in])
        pair_in = _pair_sum(send_in, got_in, core, "pair_sum_w_in")
        started["in"] = (send_in, got_in, _exchange_start([pair_in], no_order, "w_in_grads_chips_start", "chips"))

    ffn, out = slice(4, 7), slice(1, 4)
    grad_x = _local_grads(
        x, loss_target, norm1_g, norm2_g, norm_f_g, b_forget, pool_mix, pool_scale, _w_in_pieces(gathered_in), ffn_sems[4],
        gathered_weights(out_sems, _SHARD_AXIS[out], "out_weights_gather_wait"),
        gathered_weights(ffn_sems, _SHARD_AXIS[ffn], "ffn_weights_gather_wait"),
        scatter_grads(ffn, "ffn_grads_scatter_start"), scatter_grads(out, "out_grads_scatter_start"), reduce_w_in, gather_small,
    )
    send_in, got_in, chip_sems = started["in"]

    def scattered_updates(key, after, name):
        send_sems, recv_sems, srcs, lands, _ = started[key]
        srcs, lands = _exchange_wait(send_sems, recv_sems, srcs, lands, after, name, "scatter")
        return [
            _shard_update_direct(p, s, w, m, v, jnp.reshape(me, (1,)).astype(jnp.int32), "update_" + n)
            for p, s, w, m, v, n in zip(lands, srcs, w_sh[key], m_sh[key], v_sh[key], names[key])
        ]

    updates_out = scattered_updates(out, chip_sems[4], "out_grads_scatter_wait")
    updates_ffn = scattered_updates(ffn, chip_sems[4], "ffn_grads_scatter_wait")

    small_w = (norm1_g, norm2_g, norm_f_g, b_forget, pool_scale, pool_mix)
    small_m = (m_norm1_g, m_norm2_g, m_norm_f_g, m_b_forget, m_pool_scale, m_pool_mix)
    small_v = (v_norm1_g, v_norm2_g, v_norm_f_g, v_b_forget, v_pool_scale, v_pool_mix)
    zero_row = jnp.zeros((8, LANES), F32)
    send_sems, recv_sems, srcs, lands, _ = started["small"]
    srcs, lands = _exchange_wait(send_sems, recv_sems, srcs, lands, updates_ffn[-1][0], "small_grads_gather_wait", "gather")
    (parts,) = with_own(lands, srcs)
    g_s, d_s, nm_s, nv_s = _small_update(parts, _pack_small(small_w, zero_row), _pack_small(small_m, zero_row), _pack_small(small_v, zero_row))

    send_sems, recv_sems, srcs, lands, _ = chip_sems
    _, (recv_in,) = _exchange_wait(send_sems, recv_sems, srcs, lands, g_s, "w_in_grads_chips_wait", "chips")
    update_in = _shard_update(send_in, got_in, recv_in, w_in, m_w_in, v_w_in, pos, "update_w_in")
    g_w, d_w, nm_w, nv_w = zip(*(
        [stored(t, tr) for t in u] for u, tr in zip([update_in] + updates_out + updates_ffn, _TRANSPOSED)
    ))
    shapes = [t.shape for t in small_w]
    (g1, g2, gf, gb, gsc, gmix), loss = _unpack_small(g_s, shapes)
    (d1, d2, df, db_, dsc, dmx), _ = _unpack_small(d_s, shapes)
    (m1, m2, mf, mb, msc, mmx), _ = _unpack_small(nm_s, shapes)
    (v1, v2, vf, vb, vsc, vmx), _ = _unpack_small(nv_s, shapes)

    def ordered(n1, win, b, mix, sc, wpo, wao, wout, n2, wg, wu, wd, nf):
        return (n1, win, b, mix, sc, wpo, wao, wout, n2, wg, wu, wd, nf)

    grads = ordered(g1, g_w[0], gb, gmix, gsc, g_w[1], g_w[2], g_w[3], g2, g_w[4], g_w[5], g_w[6], gf)
    deltas = ordered(d1, d_w[0], db_, dmx, dsc, d_w[1], d_w[2], d_w[3], d2, d_w[4], d_w[5], d_w[6], df)
    new_m = ordered(m1, nm_w[0], mb, mmx, msc, nm_w[1], nm_w[2], nm_w[3], m2, nm_w[4], nm_w[5], nm_w[6], mf)
    new_v = ordered(v1, nv_w[0], vb, vmx, vsc, nv_w[1], nv_w[2], nv_w[3], v2, nv_w[4], nv_w[5], nv_w[6], vf)
    return (loss, grad_x, *grads, *deltas, *new_m, *new_v)
```

```python
import functools

import jax
import jax.numpy as jnp
from jax import lax
from jax.experimental import pallas as pl
from jax.experimental.pallas import tpu as pltpu

F32 = jnp.float32
BF16 = jnp.bfloat16
MESH = pl.DeviceIdType.MESH

D_MODEL = 1024
POOL_WINDOWS = (2, 4, 8, 16)
POOL_WIDTH = 512
GROUP_DIM = 128
ATTN_WIDTH = 512
HEAD_DIM = 64
N_HEADS = 8
N_PAIRS = 4
D_FF = 2816
RMS_EPS = 1e-6
N_DEV = 8
LANES = 128
FL_PAD = 128

ADAM_LR = 0.001
ADAM_B1 = 0.9
ADAM_B2 = 0.999
ADAM_EPS = 1e-08
ADAM_WD = 0.01
ADAM_STEP = 10

VMEM_LIMIT = 56 * 1024 * 1024
VMEM_LIMIT_MAX = 60 * 1024 * 1024
ROW_TILE = 512
ATTN_BLOCK = 512
FF_CHUNK = 256
FF_ROW_TILE = 512
DW_TOKENS = 2048


def _mm(a, b):
    return jnp.dot(a, b, preferred_element_type=F32)


def _mm_nt(a, b):
    return lax.dot_general(a, b, (((1,), (1,)), ((), ())), preferred_element_type=F32)


def _mm_tn(a, b):
    return lax.dot_general(a, b, (((0,), (0,)), ((), ())), preferred_element_type=F32)


def _sigmoid(x):
    return 1.0 / (1.0 + jnp.exp(-x))


def _params(sem, vmem=VMEM_LIMIT):
    return pltpu.CompilerParams(dimension_semantics=sem, vmem_limit_bytes=vmem)


def _const_spec(shape):
    nd = len(shape)
    return pl.BlockSpec(shape, lambda *_: (0,) * nd, pipeline_mode=pl.Buffered(1))


def _rms_fwd(x, g):
    r = lax.rsqrt(jnp.mean(x * x, axis=-1, keepdims=True) + RMS_EPS)
    xh = x * r
    return xh * g, xh, r


def _rms_bwd(dy, xh, r, g):
    dxh = dy * g
    dx = r * (dxh - xh * jnp.mean(dxh * xh, axis=-1, keepdims=True))
    return dx, dy * xh


def _in_proj(x, g1, w_uqkv, w_fl, w_g, token):
    T = x.shape[0]
    tm = ROW_TILE

    def body(x_ref, g_ref, wa_ref, wf_ref, wg_ref, token_ref, h_ref, u_ref, qkv_ref, fl_ref, gt_ref):
        h, _, _ = _rms_fwd(x_ref[...], g_ref[...])
        hb = h.astype(BF16)
        h_ref[...] = hb
        z = _mm(hb, wa_ref[...])
        u_ref[...] = z[:, :POOL_WIDTH]
        qkv_ref[...] = z[:, POOL_WIDTH:].astype(BF16)
        fl_ref[...] = _mm(hb, wf_ref[...])
        gt_ref[...] = _mm(hb, wg_ref[...]).astype(BF16)

    row = lambda n: pl.BlockSpec((tm, n), lambda i: (i, 0))
    return pl.pallas_call(
        body,
        name="in_proj",
        grid=(T // tm,),
        in_specs=[row(D_MODEL), _const_spec((1, D_MODEL)), _const_spec(w_uqkv.shape), _const_spec(w_fl.shape), _const_spec(w_g.shape), _HBM],
        out_specs=[row(D_MODEL), row(POOL_WIDTH), row(3 * ATTN_WIDTH), row(FL_PAD), row(2 * D_MODEL)],
        out_shape=[
            jax.ShapeDtypeStruct((T, D_MODEL), BF16),
            jax.ShapeDtypeStruct((T, POOL_WIDTH), F32),
            jax.ShapeDtypeStruct((T, 3 * ATTN_WIDTH), BF16),
            jax.ShapeDtypeStruct((T, FL_PAD), F32),
            jax.ShapeDtypeStruct((T, 2 * D_MODEL), BF16),
        ],
        compiler_params=_params(("parallel",)),
    )(x, g1, w_uqkv, w_fl, w_g, token)


def _log_sigmoid(x):
    return jnp.minimum(x, 0.0) - jnp.log(1.0 + jnp.exp(-jnp.abs(x)))


def _forget_fwd(fl, b_pad, n_seq, S):
    def body(fl_ref, b_ref, fcol_ref):
        lf = _log_sigmoid(fl_ref[...] + b_ref[...])
        t = lf.T
        lane = lax.broadcasted_iota(jnp.int32, t.shape, 1)
        k = 1
        while k < S:
            t = t + jnp.where(lane >= k, pltpu.roll(t, k, 1), 0.0)
            k *= 2
        fcol_ref[...] = t.T

    return pl.pallas_call(
        body,
        name="forget_fwd",
        grid=(n_seq,),
        in_specs=[pl.BlockSpec((S, FL_PAD), lambda s: (s, 0)), _const_spec((1, FL_PAD))],
        out_specs=pl.BlockSpec((S, FL_PAD), lambda s: (s, 0)),
        out_shape=jax.ShapeDtypeStruct((n_seq * S, FL_PAD), F32),
        compiler_params=_params(("parallel",)),
    )(fl, b_pad)


def _window_pick(g, v2, v4, v8, v16):
    return jnp.where(g == 0, v2, jnp.where(g == 1, v4, jnp.where(g == 2, v8, v16)))


def _pool_fwd(u, mix_b, scale, n_seq, S):
    T = n_seq * S

    def body(u_ref, mix_ref, sc_ref, pm_ref, p2_ref, p3_ref):
        g = pl.program_id(1)
        uu = u_ref[...]
        row = lax.broadcasted_iota(jnp.int32, uu.shape, 0)

        def back(a, k):
            return jnp.where(row >= k, pltpu.roll(a, k, 0), 0.0)

        s2 = uu + back(uu, 1)
        s4 = s2 + back(s2, 2)
        s8 = s4 + back(s4, 4)
        s16 = s8 + back(s8, 8)
        w = _window_pick(g, 2.0, 4.0, 8.0, 16.0)
        cnt = jnp.minimum((row + 1).astype(F32), w)
        pm = _window_pick(g, s2, s4, s8, s16) / cnt - uu
        pmb = pm.astype(BF16)
        pm_ref[...] = pmb
        p2 = _mm(pmb, mix_ref[...])
        p2_ref[...] = p2
        p3_ref[...] = (p2 * sc_ref[...]).astype(BF16)

    grp = pl.BlockSpec((S, GROUP_DIM), lambda s, g: (s, g))
    return pl.pallas_call(
        body,
        name="pool_fwd",
        grid=(n_seq, len(POOL_WINDOWS)),
        in_specs=[
            grp,
            pl.BlockSpec((None, GROUP_DIM, GROUP_DIM), lambda s, g: (g, 0, 0)),
            pl.BlockSpec((1, GROUP_DIM), lambda s, g: (0, g)),
        ],
        out_specs=[grp, grp, grp],
        out_shape=[
            jax.ShapeDtypeStruct((T, POOL_WIDTH), BF16),
            jax.ShapeDtypeStruct((T, POOL_WIDTH), F32),
            jax.ShapeDtypeStruct((T, POOL_WIDTH), BF16),
        ],
        compiler_params=_params(("parallel", "parallel")),
    )(u, mix_b, scale)


def _split3(v):
    hi = v.astype(BF16).astype(F32)
    r = v - hi
    mid = r.astype(BF16).astype(F32)
    lo = (r - mid).astype(BF16).astype(F32)
    return hi, mid, lo


def _bias_lanes(v):
    hi, mid, lo = _split3(v)
    lane = lax.broadcasted_iota(jnp.int32, (1, LANES), 1)
    packed = jnp.where(lane < N_HEADS, hi, jnp.where(lane < 2 * N_HEADS, pltpu.roll(mid, N_HEADS, 1), pltpu.roll(lo, 2 * N_HEADS, 1)))
    return jnp.where(lane < 3 * N_HEADS, packed, 0.0).astype(BF16)


def _bias_placement(slot):
    row = lax.broadcasted_iota(jnp.int32, (LANES, N_HEADS * LANES), 0)
    col = lax.broadcasted_iota(jnp.int32, (LANES, N_HEADS * LANES), 1)
    h = col // LANES
    n = col % LANES - jnp.where(h % 2 == 0, HEAD_DIM, 0) - 3 * slot
    return ((n >= 0) & (n < 3) & (row == N_HEADS * n + h)).astype(BF16)


def _augment(xp, h, bias, ones_slot):
    lane = lax.broadcasted_iota(jnp.int32, (1, LANES), 1)
    hh = h % 2
    head = (lane >= HEAD_DIM * hh) & (lane < HEAD_DIM * (hh + 1))
    b = HEAD_DIM * (1 - hh)
    rest = jnp.zeros_like(xp) if bias is None else bias[:, h * LANES : (h + 1) * LANES]
    out = jnp.where(head, xp, rest)
    if ones_slot is not None:
        out = jnp.where((lane >= b + 3 * ones_slot) & (lane < b + 3 * ones_slot + 3), jnp.ones_like(xp), out)
    return out


def _attn_fwd(qkv, fcol, n_seq, S):
    T = n_seq * S
    tb = ATTN_BLOCK
    nq = S // tb
    scale = HEAD_DIM ** -0.5

    def body(q_ref, k_ref, v_ref, fc_ref, o_ref, st_ref, qa_sc, ka_sc, m_sc, l_sc, acc_sc):
        i = pl.program_id(1)
        lane = lax.broadcasted_iota(jnp.int32, (1, LANES), 1)
        low = lane < HEAD_DIM

        @pl.when(i == 0)
        def _():
            place = _bias_placement(1)

            def rows_ka(r, carry):
                r0 = pl.multiple_of(r * tb, tb)
                bias = _mm(_bias_lanes(-fc_ref[pl.ds(r0, tb), :]), place).astype(BF16)
                for h in range(N_HEADS):
                    kp = k_ref[pl.ds(r0, tb), (h // 2) * LANES : (h // 2 + 1) * LANES] * scale
                    ka_sc[h, pl.ds(r0, tb), :] = _augment(kp, h, bias, 0)
                return carry

            lax.fori_loop(0, nq, rows_ka, 0)

        q0 = pl.multiple_of(i * tb, tb)
        bias = _mm(_bias_lanes(fc_ref[pl.ds(q0, tb), :]), _bias_placement(0)).astype(BF16)
        for h in range(N_HEADS):
            qa_sc[h] = _augment(q_ref[:, (h // 2) * LANES : (h // 2 + 1) * LANES], h, bias, 1)
        m_sc[...] = jnp.full(m_sc.shape, -jnp.inf, F32)
        l_sc[...] = jnp.zeros_like(l_sc)
        acc_sc[...] = jnp.zeros_like(acc_sc)
        causal = lax.broadcasted_iota(jnp.int32, (tb, tb), 1) <= lax.broadcasted_iota(jnp.int32, (tb, tb), 0)

        def step(j, masked):
            c0 = pl.multiple_of(j * tb, tb)
            for p in range(N_PAIRS):
                vb = v_ref[pl.ds(c0, tb), p * LANES : (p + 1) * LANES]
                pv, al = [], []
                for hh in range(2):
                    h = 2 * p + hh
                    s = _mm_nt(qa_sc[h], ka_sc[h, pl.ds(c0, tb), :])
                    if masked:
                        s = jnp.where(causal, s, -jnp.inf)
                    m_old = m_sc[h]
                    m_new = jnp.maximum(m_old, jnp.max(s, axis=1, keepdims=True))
                    alpha = jnp.exp(m_old - m_new)
                    pe = jnp.exp(s - jnp.concatenate([m_new] * (tb // LANES), axis=1))
                    l_sc[h] = alpha * l_sc[h] + jnp.sum(pe, axis=1, keepdims=True)
                    m_sc[h] = m_new
                    pv.append(_mm(pe.astype(BF16), vb))
                    al.append(alpha)
                acc_sc[p] = jnp.where(low, al[0], al[1]) * acc_sc[p] + jnp.where(low, pv[0], pv[1])

        def loop_body(j, carry):
            step(j, False)
            return carry

        lax.fori_loop(0, i, loop_body, 0)
        step(i, True)
        st = jnp.zeros((tb, LANES), F32)
        for p in range(N_PAIRS):
            lp = jnp.where(low, l_sc[2 * p], l_sc[2 * p + 1])
            o_ref[:, p * LANES : (p + 1) * LANES] = (acc_sc[p] / lp).astype(BF16)
            for h in (2 * p, 2 * p + 1):
                st = jnp.where(lane == h, m_sc[h] + jnp.log(l_sc[h]), st)
        st_ref[...] = st

    return pl.pallas_call(
        body,
        name="attn_fwd",
        grid=(n_seq, nq),
        in_specs=[
            pl.BlockSpec((tb, ATTN_WIDTH), lambda s, i: (s * nq + i, 0)),
            pl.BlockSpec((S, ATTN_WIDTH), lambda s, i: (s, 1)),
            pl.BlockSpec((S, ATTN_WIDTH), lambda s, i: (s, 2)),
            pl.BlockSpec((S, LANES), lambda s, i: (s, 0)),
        ],
        out_specs=[
            pl.BlockSpec((tb, ATTN_WIDTH), lambda s, i: (s * nq + i, 0)),
            pl.BlockSpec((tb, LANES), lambda s, i: (s * nq + i, 0)),
        ],
        out_shape=[jax.ShapeDtypeStruct((T, ATTN_WIDTH), BF16), jax.ShapeDtypeStruct((T, LANES), F32)],
        scratch_shapes=[
            pltpu.VMEM((N_HEADS, tb, LANES), BF16),
            pltpu.VMEM((N_HEADS, S, LANES), BF16),
            pltpu.VMEM((N_HEADS, tb, LANES), F32),
            pltpu.VMEM((N_HEADS, tb, LANES), F32),
            pltpu.VMEM((N_PAIRS, tb, LANES), F32),
        ],
        compiler_params=_params(("parallel", "arbitrary")),
    )(qkv, qkv, qkv, fcol)


def _mix_out(a, p3, gates, x, w_ao, w_po, w_out):
    T = x.shape[0]
    tm = ROW_TILE

    def body(a_ref, p3_ref, gt_ref, x_ref, wao_ref, wpo_ref, wout_ref, mg_ref, x1_ref, ay_ref, py_ref):
        ay = _mm(a_ref[...], wao_ref[...])
        py = _mm(p3_ref[...], wpo_ref[...])
        ay_ref[...] = ay.astype(BF16)
        py_ref[...] = py.astype(BF16)
        sp = _sigmoid(gt_ref[:, :D_MODEL].astype(F32))
        sa = _sigmoid(gt_ref[:, D_MODEL:].astype(F32))
        mb = (sp * py + sa * ay).astype(BF16)
        mg_ref[...] = mb
        x1_ref[...] = x_ref[...] + _mm(mb, wout_ref[...])

    row = lambda n: pl.BlockSpec((tm, n), lambda i: (i, 0))
    return pl.pallas_call(
        body,
        name="mix_out",
        grid=(T // tm,),
        in_specs=[
            row(ATTN_WIDTH), row(POOL_WIDTH), row(2 * D_MODEL), row(D_MODEL),
            _const_spec(w_ao.shape), _const_spec(w_po.shape), _const_spec(w_out.shape),
        ],
        out_specs=[row(D_MODEL), row(D_MODEL), row(D_MODEL), row(D_MODEL)],
        out_shape=[
            jax.ShapeDtypeStruct((T, D_MODEL), BF16), jax.ShapeDtypeStruct((T, D_MODEL), F32),
            jax.ShapeDtypeStruct((T, D_MODEL), BF16), jax.ShapeDtypeStruct((T, D_MODEL), BF16),
        ],
        compiler_params=_params(("parallel",)),
    )(a, p3, gates, x, w_ao, w_po, w_out)


def _ffn_fwd(x1, g2, gf, tgt, w_gate_t, w_up_t, w_down):
    T = x1.shape[0]
    tm = min(T, FF_ROW_TILE)
    nt = T // tm
    nc = D_FF // FF_CHUNK

    def body(x1_ref, g2_ref, gf_ref, tg_ref, wg_ref, wu_ref, wd_ref, h2_ref, gate_ref, up_ref, act_ref, dx2_ref, loss_ref, dgf_ref):
        x1v = x1_ref[...]
        h2, _, _ = _rms_fwd(x1v, g2_ref[...])
        h2b = h2.astype(BF16)
        h2_ref[...] = h2b
        for c in range(nc):
            sl = slice(c * FF_CHUNK, (c + 1) * FF_CHUNK)
            gate = _mm_nt(h2b, wg_ref[sl, :])
            up = _mm_nt(h2b, wu_ref[sl, :])
            gate_ref[:, sl] = gate.astype(BF16)
            up_ref[:, sl] = up.astype(BF16)
            act_ref[:, sl] = (gate * _sigmoid(gate) * up).astype(BF16)
        acc = x1v + _mm(act_ref[...], wd_ref[...])
        gfv = gf_ref[...]
        y, xh, r = _rms_fwd(acc, gfv)
        err = y - tg_ref[...]
        part = 0.5 * jnp.sum(jnp.mean(err * err, axis=-1, keepdims=True), axis=0, keepdims=True)
        dx2, dgrow = _rms_bwd(err * (1.0 / D_MODEL), xh, r, gfv)
        dx2_ref[...] = dx2

        @pl.when(pl.program_id(0) == 0)
        def _():
            dgf_ref[...] = jnp.zeros_like(dgf_ref)
            loss_ref[...] = jnp.zeros_like(loss_ref)

        dgf_ref[...] += jnp.sum(dgrow, axis=0, keepdims=True)
        loss_ref[...] += jnp.broadcast_to(part, loss_ref.shape)

    row = lambda n: pl.BlockSpec((tm, n), lambda i: (i, 0))
    return pl.pallas_call(
        body,
        name="ffn_fwd",
        grid=(nt,),
        in_specs=[
            row(D_MODEL), _const_spec((1, D_MODEL)), _const_spec((1, D_MODEL)), row(D_MODEL),
            _const_spec(w_gate_t.shape), _const_spec(w_up_t.shape), _const_spec(w_down.shape),
        ],
        out_specs=[
            row(D_MODEL), row(D_FF), row(D_FF), row(D_FF), row(D_MODEL),
            pl.BlockSpec((8, LANES), lambda i: (0, 0)),
            pl.BlockSpec((1, D_MODEL), lambda i: (0, 0)),
        ],
        out_shape=[
            jax.ShapeDtypeStruct((T, D_MODEL), BF16),
            jax.ShapeDtypeStruct((T, D_FF), BF16),
            jax.ShapeDtypeStruct((T, D_FF), BF16),
            jax.ShapeDtypeStruct((T, D_FF), BF16),
            jax.ShapeDtypeStruct((T, D_MODEL), F32),
            jax.ShapeDtypeStruct((8, LANES), F32),
            jax.ShapeDtypeStruct((1, D_MODEL), F32),
        ],
        compiler_params=_params(("arbitrary",)),
    )(x1, g2, gf, tgt, w_gate_t, w_up_t, w_down)


def _ffn_bwd(dx2, gate, up, x1, g2, w_gate_t, w_up_t, w_down):
    T = x1.shape[0]
    tm = min(T, FF_ROW_TILE)
    nc = D_FF // FF_CHUNK

    def body(dx2_ref, gate_ref, up_ref, x1_ref, g2_ref, wg_ref, wu_ref, wd_ref, dgate_ref, dup_ref, dx1_ref, dg2_ref):
        dx2v = dx2_ref[...]
        dx2b = dx2v.astype(BF16)
        for c in range(nc):
            sl = slice(c * FF_CHUNK, (c + 1) * FF_CHUNK)
            dact = _mm_nt(dx2b, wd_ref[sl, :])
            gate = gate_ref[:, sl].astype(F32)
            sg = _sigmoid(gate)
            silu = gate * sg
            dgate = (dact * up_ref[:, sl].astype(F32) * (sg * (1.0 + gate * (1.0 - sg)))).astype(BF16)
            dup = (dact * silu).astype(BF16)
            dgate_ref[:, sl] = dgate
            dup_ref[:, sl] = dup
        dh2 = _mm(dgate_ref[...], wg_ref[...]) + _mm(dup_ref[...], wu_ref[...])
        g2v = g2_ref[...]
        _, xh, r = _rms_fwd(x1_ref[...], g2v)
        dxn, dgrow = _rms_bwd(dh2, xh, r, g2v)
        dx1_ref[...] = dx2v + dxn

        @pl.when(pl.program_id(0) == 0)
        def _():
            dg2_ref[...] = jnp.zeros_like(dg2_ref)

        dg2_ref[...] += jnp.sum(dgrow, axis=0, keepdims=True)

    row = lambda n: pl.BlockSpec((tm, n), lambda i: (i, 0))
    return pl.pallas_call(
        body,
        name="ffn_bwd",
        grid=(T // tm,),
        in_specs=[
            row(D_MODEL), row(D_FF), row(D_FF), row(D_MODEL), _const_spec((1, D_MODEL)),
            _const_spec(w_gate_t.shape), _const_spec(w_up_t.shape), _const_spec(w_down.shape),
        ],
        out_specs=[row(D_FF), row(D_FF), row(D_MODEL), pl.BlockSpec((1, D_MODEL), lambda i: (0, 0))],
        out_shape=[
            jax.ShapeDtypeStruct((T, D_FF), BF16),
            jax.ShapeDtypeStruct((T, D_FF), BF16),
            jax.ShapeDtypeStruct((T, D_MODEL), F32),
            jax.ShapeDtypeStruct((1, D_MODEL), F32),
        ],
        compiler_params=_params(("arbitrary",), VMEM_LIMIT_MAX),
    )(dx2, gate, up, x1, g2, w_gate_t, w_up_t, w_down)


def _mix_bwd(dx1, gates, pool_y, attn_y, p2, scale, w_out, w_ao, w_po, token):
    T = dx1.shape[0]
    tm = ROW_TILE

    def body(dx1_ref, gt_ref, py_ref, ay_ref, p2_ref, sc_ref, wout_ref, wao_ref, wpo_ref, token_ref, dgt_ref, dpy_ref, day_ref, da_ref, dp2_ref, dsc_ref):
        dm = _mm_nt(dx1_ref[...].astype(BF16), wout_ref[...])
        sp = _sigmoid(gt_ref[:, :D_MODEL].astype(F32))
        sa = _sigmoid(gt_ref[:, D_MODEL:].astype(F32))
        dgt_ref[:, :D_MODEL] = (dm * py_ref[...].astype(F32) * (sp * (1.0 - sp))).astype(BF16)
        dgt_ref[:, D_MODEL:] = (dm * ay_ref[...].astype(F32) * (sa * (1.0 - sa))).astype(BF16)
        dpy = (dm * sp).astype(BF16)
        day = (dm * sa).astype(BF16)
        dpy_ref[...] = dpy
        day_ref[...] = day
        da_ref[...] = _mm_nt(day, wao_ref[...]).astype(BF16)
        dp3 = _mm_nt(dpy, wpo_ref[...])
        dp2_ref[...] = (dp3 * sc_ref[...]).astype(BF16)

        @pl.when(pl.program_id(0) == 0)
        def _():
            dsc_ref[...] = jnp.zeros_like(dsc_ref)

        dsc_ref[...] += jnp.sum(dp3 * p2_ref[...], axis=0, keepdims=True)

    row = lambda n: pl.BlockSpec((tm, n), lambda i: (i, 0))
    return pl.pallas_call(
        body,
        name="mix_bwd",
        grid=(T // tm,),
        in_specs=[
            row(D_MODEL), row(2 * D_MODEL), row(D_MODEL), row(D_MODEL), row(POOL_WIDTH), _const_spec((1, POOL_WIDTH)),
            _const_spec(w_out.shape), _const_spec(w_ao.shape), _const_spec(w_po.shape), _HBM,
        ],
        out_specs=[row(2 * D_MODEL), row(D_MODEL), row(D_MODEL), row(ATTN_WIDTH), row(POOL_WIDTH), pl.BlockSpec((1, POOL_WIDTH), lambda i: (0, 0))],
        out_shape=[
            jax.ShapeDtypeStruct((T, 2 * D_MODEL), BF16),
            jax.ShapeDtypeStruct((T, D_MODEL), BF16),
            jax.ShapeDtypeStruct((T, D_MODEL), BF16),
            jax.ShapeDtypeStruct((T, ATTN_WIDTH), BF16),
            jax.ShapeDtypeStruct((T, POOL_WIDTH), BF16),
            jax.ShapeDtypeStruct((1, POOL_WIDTH), F32),
        ],
        compiler_params=_params(("arbitrary",)),
    )(dx1, gates, pool_y, attn_y, p2, scale, w_out, w_ao, w_po, token)


def _pool_bwd(dp2, pm, mix_b, token, n_seq, S):
    T = n_seq * S

    def body(dp2_ref, pm_ref, mix_ref, token_ref, du_ref, dmix_ref):
        g = pl.program_id(0)
        dp2v = dp2_ref[...]
        dpm = _mm_nt(dp2v, mix_ref[...])
        row = lax.broadcasted_iota(jnp.int32, dpm.shape, 0)
        w = _window_pick(g, 2.0, 4.0, 8.0, 16.0)
        e = dpm / jnp.minimum((row + 1).astype(F32), w)

        def ahead(a, k):
            return jnp.where(row < S - k, pltpu.roll(a, S - k, 0), 0.0)

        r2 = e + ahead(e, 1)
        r4 = r2 + ahead(r2, 2)
        r8 = r4 + ahead(r4, 4)
        r16 = r8 + ahead(r8, 8)
        du_ref[...] = (_window_pick(g, r2, r4, r8, r16) - dpm).astype(BF16)

        @pl.when(pl.program_id(1) == 0)
        def _():
            dmix_ref[...] = jnp.zeros_like(dmix_ref)

        dmix_ref[...] += _mm_tn(pm_ref[...], dp2v)

    grp = pl.BlockSpec((S, GROUP_DIM), lambda g, s: (s, g))
    mixs = pl.BlockSpec((None, GROUP_DIM, GROUP_DIM), lambda g, s: (g, 0, 0))
    return pl.pallas_call(
        body,
        name="pool_bwd",
        grid=(len(POOL_WINDOWS), n_seq),
        in_specs=[grp, grp, mixs, _HBM],
        out_specs=[grp, mixs],
        out_shape=[jax.ShapeDtypeStruct((T, POOL_WIDTH), BF16), jax.ShapeDtypeStruct((len(POOL_WINDOWS), GROUP_DIM, GROUP_DIM), F32)],
        compiler_params=_params(("parallel", "arbitrary")),
    )(dp2, pm, mix_b, token)


def _attn_bwd(qkv, da, a, fcol, lse, n_seq, S):
    T = n_seq * S
    tb = ATTN_BLOCK
    nb = S // tb
    scale = HEAD_DIM ** -0.5

    def body(q_ref, k_ref, v_ref, do_ref, o_ref, fc_ref, st_ref, dq_ref, dk_ref, dv_ref, dfk_ref, dfq_ref,
             qa_sc, doa_sc, dq_acc, ka_sc, va_sc, dk_sc, dv_sc):
        j = pl.program_id(1)
        lane = lax.broadcasted_iota(jnp.int32, (1, LANES), 1)
        low = lane < HEAD_DIM

        @pl.when(j == 0)
        def _():
            dq_acc[...] = jnp.zeros_like(dq_acc)
            place = _bias_placement(0)

            def rows_q(i, carry):
                r0 = pl.multiple_of(i * tb, tb)
                delta = jnp.zeros((tb, LANES), F32)
                for h in range(N_HEADS):
                    pair = slice((h // 2) * LANES, (h // 2 + 1) * LANES)
                    prod = do_ref[pl.ds(r0, tb), pair].astype(F32) * o_ref[pl.ds(r0, tb), pair].astype(F32)
                    head = (lane >= HEAD_DIM * (h % 2)) & (lane < HEAD_DIM * (h % 2 + 1))
                    delta = jnp.where(lane == h, jnp.sum(jnp.where(head, prod, 0.0), axis=1, keepdims=True), delta)
                cq = fc_ref[pl.ds(r0, tb), :] - st_ref[pl.ds(r0, tb), :]
                q_bias = _mm(_bias_lanes(cq), place).astype(BF16)
                do_bias = _mm(_bias_lanes(-delta), place).astype(BF16)
                for h in range(N_HEADS):
                    pair = slice((h // 2) * LANES, (h // 2 + 1) * LANES)
                    qa_sc[h, pl.ds(r0, tb), :] = _augment(q_ref[pl.ds(r0, tb), pair], h, q_bias, 1)
                    doa_sc[h, pl.ds(r0, tb), :] = _augment(do_ref[pl.ds(r0, tb), pair], h, do_bias, None)
                return carry

            lax.fori_loop(0, nb, rows_q, 0)

        c0 = pl.multiple_of(j * tb, tb)
        k_bias = _mm(_bias_lanes(-fc_ref[pl.ds(c0, tb), :]), _bias_placement(1)).astype(BF16)
        for h in range(N_HEADS):
            pair = slice((h // 2) * LANES, (h // 2 + 1) * LANES)
            ka_sc[h] = _augment(k_ref[:, pair] * scale, h, k_bias, 0)
            va_sc[h] = _augment(v_ref[:, pair], h, None, 0)
        dk_sc[...] = jnp.zeros_like(dk_sc)
        dv_sc[...] = jnp.zeros_like(dv_sc)
        causal = lax.broadcasted_iota(jnp.int32, (tb, tb), 1) <= lax.broadcasted_iota(jnp.int32, (tb, tb), 0)

        def step(i, masked):
            r0 = pl.multiple_of(i * tb, tb)
            for h in range(N_HEADS):
                dob = do_ref[pl.ds(r0, tb), (h // 2) * LANES : (h // 2 + 1) * LANES]
                qa = qa_sc[h, pl.ds(r0, tb), :]
                s = _mm_nt(qa, ka_sc[h])
                if masked:
                    s = jnp.where(causal, s, -jnp.inf)
                pr = jnp.exp(s)
                dv_sc[h] += _mm_tn(pr.astype(BF16), dob)
                dsb = (pr * _mm_nt(doa_sc[h, pl.ds(r0, tb), :], va_sc[h])).astype(BF16)
                dk_sc[h] += _mm_tn(dsb, qa)
                dq_acc[h, pl.ds(r0, tb), :] += _mm(dsb, ka_sc[h])

        step(j, True)

        def loop_body(i, carry):
            step(i, False)
            return carry

        lax.fori_loop(j + 1, nb, loop_body, 0)
        dfk = jnp.zeros((tb, LANES), F32)
        for p in range(N_PAIRS):
            dk_ref[:, p * LANES : (p + 1) * LANES] = (jnp.where(low, dk_sc[2 * p], dk_sc[2 * p + 1]) * scale).astype(BF16)
            dv_ref[:, p * LANES : (p + 1) * LANES] = jnp.where(low, dv_sc[2 * p], dv_sc[2 * p + 1]).astype(BF16)
            for hh in range(2):
                b = HEAD_DIM * (1 - hh) + 3
                dfk = jnp.where(lane == 2 * p + hh, -dk_sc[2 * p + hh][:, b : b + 1], dfk)
        dfk_ref[...] = dfk

        @pl.when(j == nb - 1)
        def _():
            def rows_dq(i, carry):
                r0 = pl.multiple_of(i * tb, tb)
                dfq = jnp.zeros((tb, LANES), F32)
                for p in range(N_PAIRS):
                    parts = [dq_acc[2 * p + hh, pl.ds(r0, tb), :] for hh in range(2)]
                    dq_ref[pl.ds(r0, tb), p * LANES : (p + 1) * LANES] = jnp.where(low, parts[0], parts[1]).astype(BF16)
                    for hh in range(2):
                        b = HEAD_DIM * (1 - hh)
                        dfq = jnp.where(lane == 2 * p + hh, parts[hh][:, b : b + 1], dfq)
                dfq_ref[pl.ds(r0, tb), :] = dfq
                return carry

            lax.fori_loop(0, nb, rows_dq, 0)

    seq = lambda w, col: pl.BlockSpec((S, w), lambda s, j: (s, col))
    blk = lambda w, col: pl.BlockSpec((tb, w), lambda s, j: (s * nb + j, col))
    return pl.pallas_call(
        body,
        name="attn_bwd",
        grid=(n_seq, nb),
        in_specs=[seq(ATTN_WIDTH, 0), blk(ATTN_WIDTH, 1), blk(ATTN_WIDTH, 2), seq(ATTN_WIDTH, 0), seq(ATTN_WIDTH, 0), seq(LANES, 0), seq(LANES, 0)],
        out_specs=[seq(ATTN_WIDTH, 0), blk(ATTN_WIDTH, 0), blk(ATTN_WIDTH, 0), blk(LANES, 0), seq(LANES, 0)],
        out_shape=[
            jax.ShapeDtypeStruct((T, ATTN_WIDTH), BF16),
            jax.ShapeDtypeStruct((T, ATTN_WIDTH), BF16),
            jax.ShapeDtypeStruct((T, ATTN_WIDTH), BF16),
            jax.ShapeDtypeStruct((T, LANES), F32),
            jax.ShapeDtypeStruct((T, LANES), F32),
        ],
        scratch_shapes=[
            pltpu.VMEM((N_HEADS, S, LANES), BF16),
            pltpu.VMEM((N_HEADS, S, LANES), BF16),
            pltpu.VMEM((N_HEADS, S, LANES), F32),
            pltpu.VMEM((N_HEADS, tb, LANES), BF16),
            pltpu.VMEM((N_HEADS, tb, LANES), BF16),
            pltpu.VMEM((N_HEADS, tb, LANES), F32),
            pltpu.VMEM((N_HEADS, tb, LANES), F32),
        ],
        compiler_params=_params(("parallel", "arbitrary")),
    )(qkv, qkv, qkv, da, a, fcol, lse)


def _forget_bwd(dfk, dfq, fl, b_pad, n_seq, S):
    def body(df_ref, dfq_ref, fl_ref, b_ref, dfl_ref, db_ref):
        t = (df_ref[...] + dfq_ref[...]).T
        lane = lax.broadcasted_iota(jnp.int32, t.shape, 1)
        k = 1
        while k < S:
            t = t + jnp.where(lane < S - k, pltpu.roll(t, S - k, 1), 0.0)
            k *= 2
        dfl = t.T * _sigmoid(-(fl_ref[...] + b_ref[...]))
        dfl_ref[...] = dfl.astype(BF16)

        @pl.when(pl.program_id(0) == 0)
        def _():
            db_ref[...] = jnp.zeros_like(db_ref)

        db_ref[...] += jnp.sum(dfl, axis=0, keepdims=True)

    return pl.pallas_call(
        body,
        name="forget_bwd",
        grid=(n_seq,),
        in_specs=[
            pl.BlockSpec((S, LANES), lambda s: (s, 0)),
            pl.BlockSpec((S, LANES), lambda s: (s, 0)),
            pl.BlockSpec((S, FL_PAD), lambda s: (s, 0)),
            _const_spec((1, FL_PAD)),
        ],
        out_specs=[pl.BlockSpec((S, FL_PAD), lambda s: (s, 0)), pl.BlockSpec((1, FL_PAD), lambda s: (0, 0))],
        out_shape=[jax.ShapeDtypeStruct((n_seq * S, FL_PAD), BF16), jax.ShapeDtypeStruct((1, FL_PAD), F32)],
        compiler_params=_params(("arbitrary",)),
    )(dfk, dfq, fl, b_pad)


def _in_proj_bwd(du, dq, dk, dv, dfl, dgates, x, dx1, g1, w_uqkv, w_fl, w_g, token):
    T = x.shape[0]
    tm = ROW_TILE

    def body(du_ref, dq_ref, dk_ref, dv_ref, dfl_ref, dgt_ref, x_ref, dx1_ref, g_ref, wa_ref, wf_ref, wg_ref, token_ref, dx_ref, dg_ref):
        dz = jnp.concatenate([du_ref[...], dq_ref[...], dk_ref[...], dv_ref[...]], axis=1)
        dh = _mm_nt(dz, wa_ref[...]) + _mm_nt(dgt_ref[...], wg_ref[...]) + _mm_nt(dfl_ref[...], wf_ref[...])
        gv = g_ref[...]
        _, xh, r = _rms_fwd(x_ref[...], gv)
        dxn, dgrow = _rms_bwd(dh, xh, r, gv)
        dx_ref[...] = dx1_ref[...] + dxn

        @pl.when(pl.program_id(0) == 0)
        def _():
            dg_ref[...] = jnp.zeros_like(dg_ref)

        dg_ref[...] += jnp.sum(dgrow, axis=0, keepdims=True)

    row = lambda n: pl.BlockSpec((tm, n), lambda i: (i, 0))
    return pl.pallas_call(
        body,
        name="in_proj_bwd",
        grid=(T // tm,),
        in_specs=[
            row(512), row(512), row(512), row(512), row(FL_PAD), row(2 * D_MODEL), row(D_MODEL), row(D_MODEL), _const_spec((1, D_MODEL)),
            _const_spec(w_uqkv.shape), _const_spec(w_fl.shape), _const_spec(w_g.shape), _HBM,
        ],
        out_specs=[row(D_MODEL), pl.BlockSpec((1, D_MODEL), lambda i: (0, 0))],
        out_shape=[jax.ShapeDtypeStruct((T, D_MODEL), F32), jax.ShapeDtypeStruct((1, D_MODEL), F32)],
        compiler_params=_params(("arbitrary",)),
    )(du, dq, dk, dv, dfl, dgates, x, dx1, g1, w_uqkv, w_fl, w_g, token)


def _pick_block(n):
    for b in (512, 1408, 256, 128):
        if n % b == 0:
            return b
    raise ValueError(n)


def _matmul_tn(a, b, name):
    T, K = a.shape
    N = b.shape[1]
    bt, bk, bn = min(T, DW_TOKENS), _pick_block(K), _pick_block(N)
    nt = T // bt

    def body(a_ref, b_ref, o_ref, acc):
        @pl.when(pl.program_id(2) == 0)
        def _():
            acc[...] = jnp.zeros_like(acc)

        acc[...] += _mm_tn(a_ref[...].astype(BF16), b_ref[...].astype(BF16))

        @pl.when(pl.program_id(2) == nt - 1)
        def _():
            o_ref[...] = acc[...].astype(BF16)

    return pl.pallas_call(
        body,
        name=name,
        grid=(K // bk, N // bn, nt),
        in_specs=[pl.BlockSpec((bt, bk), lambda k, n, t: (t, k)), pl.BlockSpec((bt, bn), lambda k, n, t: (t, n))],
        out_specs=pl.BlockSpec((bk, bn), lambda k, n, t: (k, n)),
        out_shape=jax.ShapeDtypeStruct((K, N), BF16),
        scratch_shapes=[pltpu.VMEM((bk, bn), F32)],
        compiler_params=_params(("parallel", "parallel", "arbitrary")),
    )(a, b)


W_IN_A = POOL_WIDTH + 3 * ATTN_WIDTH
W_IN_SHARD = (W_IN_A + N_HEADS + 2 * D_MODEL) // N_DEV
_W_IN_PIECES = ((0, W_IN_A), (W_IN_A, W_IN_A + N_HEADS), (W_IN_A + N_HEADS, W_IN_A + N_HEADS + 2 * D_MODEL))


def _w_in_segments(d):
    lo, hi = d * W_IN_SHARD, (d + 1) * W_IN_SHARD
    out = []
    for p, (a, b) in enumerate(_W_IN_PIECES):
        s, e = max(lo, a), min(hi, b)
        if s < e:
            out.append((p, s - a, s - lo, e - s))
    return out


def _w_in_pieces(gathered):
    tm = ROW_TILE // 2

    def body(g_ref, wa_ref, wf_ref, wg_ref):
        outs = (wa_ref, wf_ref, wg_ref)
        wf_ref[...] = jnp.zeros_like(wf_ref)
        for d in range(N_DEV):
            for p, at, frm, n in _w_in_segments(d):
                outs[p][:, at : at + n] = g_ref[d, :, frm : frm + n]

    return pl.pallas_call(
        body,
        name="w_in_pieces",
        grid=(D_MODEL // tm,),
        in_specs=[pl.BlockSpec((N_DEV, tm, W_IN_SHARD), lambda i: (0, i, 0))],
        out_specs=[pl.BlockSpec((tm, W_IN_A), lambda i: (i, 0)), pl.BlockSpec((tm, FL_PAD), lambda i: (i, 0)), pl.BlockSpec((tm, 2 * D_MODEL), lambda i: (i, 0))],
        out_shape=[
            jax.ShapeDtypeStruct((D_MODEL, W_IN_A), gathered.dtype),
            jax.ShapeDtypeStruct((D_MODEL, FL_PAD), gathered.dtype),
            jax.ShapeDtypeStruct((D_MODEL, 2 * D_MODEL), gathered.dtype),
        ],
        compiler_params=_params(("parallel",)),
    )(gathered)


def _dw_in(h, du, dq, dk, dv, dfl, dgates, token):
    T = h.shape[0]
    bt, bk = min(T, DW_TOKENS // 2), 512
    nt = T // bt
    pieces = (du, dq, dk, dv, dfl, dgates)
    offs = [0]
    for p in pieces:
        offs.append(offs[-1] + p.shape[1])

    def body(h_ref, *rest):
        refs, o_ref, acc = rest[: len(pieces)], rest[-2], rest[-1]

        @pl.when(pl.program_id(1) == 0)
        def _():
            acc[...] = jnp.zeros_like(acc)

        ht = h_ref[...].T
        for ref, at in zip(refs, offs):
            acc[:, at : at + ref.shape[1]] += _mm(ht, ref[...])

        @pl.when(pl.program_id(1) == nt - 1)
        def _():
            starts = (0, W_IN_A, W_IN_A + FL_PAD)
            for d in range(N_DEV):
                for p, at, to, n in _w_in_segments(d):
                    o_ref[d % 2, d // 2, :, to : to + n] = acc[:, starts[p] + at : starts[p] + at + n].astype(BF16)

    return pl.pallas_call(
        body,
        name="dw_in",
        grid=(D_MODEL // bk, nt),
        in_specs=[pl.BlockSpec((bt, bk), lambda k, t: (t, k))] + [pl.BlockSpec((bt, p.shape[1]), lambda k, t: (t, 0)) for p in pieces] + [_HBM],
        out_specs=pl.BlockSpec((2, 4, bk, W_IN_SHARD), lambda k, t: (0, 0, k, 0)),
        out_shape=jax.ShapeDtypeStruct((2, 4, D_MODEL, W_IN_SHARD), BF16),
        scratch_shapes=[pltpu.VMEM((bk, offs[-1]), F32)],
        compiler_params=_params(("parallel", "arbitrary")),
    )(h, *pieces, token)


def _position():
    return lax.axis_index("x"), lax.axis_index("y"), lax.axis_index("c")


_HBM = pl.BlockSpec(memory_space=pl.ANY)


def _all_gather(blocks, name):
    n = len(blocks)

    def body(*refs):
        xs, outs = refs[:n], refs[n : 2 * n]
        send_sems, recv_sems, local_sems = refs[2 * n :]
        x, y, c = _position()
        me, sibling = (x, y, c), (x, y, 1 - c)
        chips = [(1 - x, y), (x, 1 - y), (1 - x, 1 - y)]

        def rows(a, px, py, pc):
            return outs[a].at[4 * px + 2 * py + pc]

        def copy(a, k, blk, to, src=None):
            return pltpu.make_async_remote_copy(
                src_ref=rows(a, *blk) if src is None else src, dst_ref=rows(a, *blk),
                send_sem=send_sems.at[7 * a + k], recv_sem=recv_sems.at[7 * a + k], device_id=to, device_id_type=MESH,
            )

        mine = [pltpu.make_async_copy(xs[a], rows(a, *me), local_sems.at[a]) for a in range(n)]
        for cp in mine:
            cp.start()
        first = []
        for a in range(n):
            first.append(copy(a, 0, me, sibling, src=xs[a]))
            first += [copy(a, 1 + j, me, (*chip, c), src=xs[a]) for j, chip in enumerate(chips)]
        for cp in first:
            cp.start()
        passed = []
        for j, chip in enumerate(chips):
            for a in range(n):
                copy(a, 1 + j, (*chip, c), me).wait_recv()
                passed.append(copy(a, 4 + j, (*chip, c), sibling))
                passed[-1].start()
        for a in range(n):
            copy(a, 0, sibling, me).wait_recv()
        for j, chip in enumerate(chips):
            for a in range(n):
                copy(a, 4 + j, (*chip, 1 - c), me).wait_recv()
        for cp in first + passed:
            cp.wait_send()
        for cp in mine:
            cp.wait()

    return pl.pallas_call(
        body,
        name=name,
        out_shape=[jax.ShapeDtypeStruct((N_DEV, *b.shape), b.dtype) for b in blocks],
        in_specs=[_HBM] * n,
        out_specs=[_HBM] * n,
        scratch_shapes=[pltpu.SemaphoreType.DMA((7 * n,)), pltpu.SemaphoreType.DMA((7 * n,)), pltpu.SemaphoreType.DMA((n,))],
    )(*blocks)


_SEM = pl.BlockSpec(memory_space=pltpu.SEMAPHORE)
_HBM_ONLY = pl.BlockSpec(memory_space=pltpu.HBM)
_SIDE_EFFECT = pltpu.SideEffectType.DATAFLOW_SIDE_EFFECTING


def _peer(x, y, c, k):
    return (1 - x if k & 4 else x, 1 - y if k & 2 else y, 1 - c if k & 1 else c)


_PEER_BITS = {"gather": range(1, N_DEV), "scatter": range(1, N_DEV), "chips": (4, 2, 6)}
_LAND_SLOTS = {"gather": N_DEV, "scatter": N_DEV, "chips": 3}


def _exchange_copies(src_refs, land_refs, send_sems, recv_sems, pattern, receive_side):
    x, y, c = _position()
    me = 4 * x + 2 * y + c
    bits = _PEER_BITS[pattern]
    cps = []
    for j, k in enumerate(bits):
        px, py, pc = _peer(x, y, c, k)
        peer = 4 * px + 2 * py + pc
        for a, (src, land) in enumerate(zip(src_refs, land_refs)):
            if pattern == "chips":
                s, slot = src.at[2 * px + py], j
            else:
                s, slot = (src if pattern == "gather" else src.at[peer]), (peer if receive_side else me)
            cps.append(pltpu.make_async_remote_copy(
                src_ref=s, dst_ref=land.at[slot],
                send_sem=send_sems.at[len(bits) * a + j], recv_sem=recv_sems.at[len(bits) * a + j],
                device_id=(px, py, pc), device_id_type=MESH,
            ))
    return cps


def _exchange_start(srcs, after, name, pattern):
    n = len(srcs)
    m = len(_PEER_BITS[pattern])
    lands = [jax.ShapeDtypeStruct((_LAND_SLOTS[pattern], *s.shape[-2:]), s.dtype) for s in srcs]

    def body(*refs):
        src_refs, land_refs = refs[1 : 1 + n], refs[1 + n : 1 + 2 * n]
        send_sems, recv_sems = refs[1 + 2 * n], refs[2 + 2 * n]
        token = refs[-1]
        for cp in _exchange_copies(src_refs, land_refs, send_sems, recv_sems, pattern, receive_side=False):
            cp.start()
        token[...] = jnp.zeros_like(token)

    hbm = lambda t: pltpu.with_memory_space_constraint(t, pltpu.HBM)
    out = pl.pallas_call(
        body,
        name=name,
        out_shape=(
            pltpu.SemaphoreType.DMA((m * n,)), pltpu.SemaphoreType.DMA((m * n,)),
            *[pltpu.HBM(s.shape, s.dtype) for s in srcs], *[pltpu.HBM(l.shape, l.dtype) for l in lands],
            jax.ShapeDtypeStruct((8, LANES), F32),
        ),
        in_specs=(_HBM, *[_HBM_ONLY] * (2 * n)),
        out_specs=(_SEM, _SEM, *[_HBM_ONLY] * (2 * n), pl.BlockSpec(memory_space=pltpu.VMEM)),
        input_output_aliases={1 + i: 2 + i for i in range(2 * n)},
        compiler_params=pltpu.CompilerParams(has_side_effects=_SIDE_EFFECT),
    )(after, *[hbm(s) for s in srcs], *[hbm(lax.empty(l.shape, l.dtype)) for l in lands])
    return out[0], out[1], out[2 : 2 + n], out[2 + n : 2 + 2 * n], out[-1]


def _exchange_wait(send_sems, recv_sems, srcs, lands, after, name, pattern):
    n = len(srcs)

    def body(*refs):
        src_refs, land_refs = refs[:n], refs[n : 2 * n]
        for cp in _exchange_copies(src_refs, land_refs, refs[2 * n], refs[2 * n + 1], pattern, receive_side=True):
            cp.wait_send()
            cp.wait_recv()

    out = pl.pallas_call(
        body,
        name=name,
        out_shape=(*[pltpu.HBM(s.shape, s.dtype) for s in srcs], *[pltpu.HBM(l.shape, l.dtype) for l in lands]),
        in_specs=(*[_HBM_ONLY] * (2 * n), _SEM, _SEM, _HBM),
        out_specs=tuple([_HBM_ONLY] * (2 * n)),
        input_output_aliases={i: i for i in range(2 * n)},
        compiler_params=pltpu.CompilerParams(has_side_effects=_SIDE_EFFECT),
    )(*srcs, *lands, send_sems, recv_sems, after)
    return out[:n], out[n:]


def _sibling_exchange(sends):
    n = len(sends)

    def body(*refs):
        srcs, dsts = refs[:n], refs[n : 2 * n]
        send_sems, recv_sems = refs[2 * n :]
        x, y, c = _position()
        cps = [
            pltpu.make_async_remote_copy(
                src_ref=srcs[a].at[1 - c], dst_ref=dsts[a], send_sem=send_sems.at[a], recv_sem=recv_sems.at[a],
                device_id=(x, y, 1 - c), device_id_type=MESH,
            )
            for a in range(n)
        ]
        for cp in cps:
            cp.start()
        for cp in cps:
            cp.wait()

    return pl.pallas_call(
        body,
        name="rs_sibling",
        out_shape=[jax.ShapeDtypeStruct(s.shape[1:], s.dtype) for s in sends],
        in_specs=[_HBM] * n,
        out_specs=[_HBM] * n,
        scratch_shapes=[pltpu.SemaphoreType.DMA((n,)), pltpu.SemaphoreType.DMA((n,))],
    )(*sends)


def _rows_tile(r):
    return ROW_TILE if r % ROW_TILE == 0 else r


def _pair_sum(send, got, core, name):
    _, _, r, c = send.shape
    br = _rows_tile(r)

    def body(core_ref, a_ref, b_ref, o_ref):
        o_ref[...] = (a_ref[...].astype(F32) + b_ref[...].astype(F32)).astype(o_ref.dtype)

    return pl.pallas_call(
        body,
        name=name,
        grid_spec=pltpu.PrefetchScalarGridSpec(
            num_scalar_prefetch=1,
            grid=(4, r // br),
            in_specs=[
                pl.BlockSpec((None, None, br, c), lambda n, i, core: (core[0], n, i, 0)),
                pl.BlockSpec((None, br, c), lambda n, i, core: (n, i, 0)),
            ],
            out_specs=pl.BlockSpec((None, br, c), lambda n, i, core: (n, i, 0)),
        ),
        out_shape=jax.ShapeDtypeStruct((4, r, c), send.dtype),
        compiler_params=_params(("parallel", "parallel")),
    )(core, send, got)


def _adamw(w, g, m, v):
    m = ADAM_B1 * m + (1.0 - ADAM_B1) * g
    v = ADAM_B2 * v + (1.0 - ADAM_B2) * (g * g)
    m_hat = m / (1.0 - ADAM_B1 ** ADAM_STEP)
    v_hat = v / (1.0 - ADAM_B2 ** ADAM_STEP)
    delta = -ADAM_LR * (m_hat / (jnp.sqrt(v_hat) + ADAM_EPS) + ADAM_WD * w)
    return delta, m, v


def _shard_update(send, got, recv, w, m, v, pos, name):
    _, r, c = w.shape
    br = _rows_tile(r)

    def body(pos_ref, a_ref, b_ref, r_ref, w_ref, m_ref, v_ref, g_ref, d_ref, nm_ref, nv_ref):
        g = a_ref[...].astype(F32) + b_ref[...].astype(F32)
        for n in range(3):
            g = g + r_ref[n].astype(F32)
        g_ref[...] = g
        d_ref[...], nm_ref[...], nv_ref[...] = _adamw(w_ref[...], g, m_ref[...], v_ref[...])

    own = pl.BlockSpec((None, br, c), lambda i, pos: (0, i, 0))
    return pl.pallas_call(
        body,
        name=name,
        grid_spec=pltpu.PrefetchScalarGridSpec(
            num_scalar_prefetch=1,
            grid=(r // br,),
            in_specs=[
                pl.BlockSpec((None, None, br, c), lambda i, pos: (pos[0], pos[1], i, 0)),
                pl.BlockSpec((None, br, c), lambda i, pos: (pos[1], i, 0)),
                pl.BlockSpec((3, br, c), lambda i, pos: (0, i, 0)),
                own, own, own,
            ],
            out_specs=[own, own, own, own],
        ),
        out_shape=[jax.ShapeDtypeStruct((1, r, c), F32)] * 4,
        compiler_params=_params(("parallel",)),
    )(pos, send, got, recv, w, m, v)


def _shard_update_direct(parts, chunks, w, m, v, me, name):
    _, r, c = w.shape
    br = _rows_tile(r)

    def body(me_ref, p_ref, own_ref, w_ref, m_ref, v_ref, g_ref, d_ref, nm_ref, nv_ref):
        g = None
        for n in range(N_DEV):
            part = jnp.where(me_ref[0] == n, own_ref[...], p_ref[n]).astype(F32)
            g = part if g is None else g + part
        g_ref[...] = g
        d_ref[...], nm_ref[...], nv_ref[...] = _adamw(w_ref[...], g, m_ref[...], v_ref[...])

    shard = pl.BlockSpec((None, br, c), lambda i, me: (0, i, 0))
    return pl.pallas_call(
        body,
        name=name,
        grid_spec=pltpu.PrefetchScalarGridSpec(
            num_scalar_prefetch=1,
            grid=(r // br,),
            in_specs=[
                pl.BlockSpec((N_DEV, br, c), lambda i, me: (0, i, 0)),
                pl.BlockSpec((None, br, c), lambda i, me: (me[0], i, 0)),
                shard, shard, shard,
            ],
            out_specs=[shard, shard, shard, shard],
        ),
        out_shape=[jax.ShapeDtypeStruct((1, r, c), F32)] * 4,
        compiler_params=_params(("parallel",)),
    )(me, parts, chunks, w, m, v)


def _small_update(parts, first_rows, w, m, v):
    R = w.shape[0]

    def body(p_ref, f_ref, w_ref, m_ref, v_ref, g_ref, d_ref, nm_ref, nv_ref):
        g, first = p_ref[0], f_ref[0]
        for n in range(1, N_DEV):
            g = g + p_ref[n]
            first = first + f_ref[n]
        g = jnp.concatenate([g[:8] + first, g[8:]], axis=0)
        g_ref[...] = g
        d_ref[...], nm_ref[...], nv_ref[...] = _adamw(w_ref[...], g, m_ref[...], v_ref[...])

    return pl.pallas_call(
        body,
        name="small_update",
        out_shape=[jax.ShapeDtypeStruct((R, LANES), F32)] * 4,
        compiler_params=pltpu.CompilerParams(vmem_limit_bytes=VMEM_LIMIT),
    )(parts, first_rows, w, m, v)


_SHARD_AXIS = (1, 1, 1, 0, 0, 0, 0)
_TRANSPOSED = (False, False, False, False, True, True, False)


def _full_from_gathered(t, axis):
    if axis == 0:
        return t.reshape(N_DEV * t.shape[1], t.shape[2])
    return jnp.concatenate([t[d] for d in range(N_DEV)], axis=1)


def _chunks_from_cols(t):
    c = t.shape[1] // N_DEV
    return jnp.stack([t[:, d * c : (d + 1) * c] for d in range(N_DEV)])


_SMALL = (("norm1_g", 8), ("norm2_g", 8), ("norm_f_g", 8), ("b_forget", 8), ("pool_scale", 8), ("pool_mix", 512))
_SMALL_ROWS = sum(r for _, r in _SMALL) + 8


def _pack_small(vals, loss_row):
    parts = []
    for (name, rows), t in zip(_SMALL, vals):
        f = t.astype(F32).reshape(-1)
        f = jnp.concatenate([f, jnp.zeros((rows * LANES - f.shape[0],), F32)]).reshape(rows, LANES)
        parts.append(f)
    parts.append(loss_row)
    return jnp.concatenate(parts, axis=0)


def _unpack_small(packed, shapes):
    out, off = [], 0
    for (name, rows), shape in zip(_SMALL, shapes):
        n = 1
        for s in shape:
            n *= s
        out.append(packed[off : off + rows].reshape(-1)[:n].reshape(shape))
        off += rows
    return out, packed[off, 0]


def _local_grads(x, tgt, g1, g2, gf, b_forget, pool_mix, pool_scale, w_in, fwd_token, out_weights, ffn_weights, ffn_grads_out, out_grads_out, small_grads_out, in_grads_out, norm1_grad_out):
    n_seq, S, _ = x.shape
    T = n_seq * S
    x2 = x.reshape(T, D_MODEL)
    tg2 = tgt.reshape(T, D_MODEL)
    w_uqkv, w_fl, w_g = w_in
    b_pad = jnp.concatenate([b_forget.reshape(1, N_HEADS), jnp.zeros((1, FL_PAD - N_HEADS), F32)], axis=1)
    mix_b = pool_mix.reshape(len(POOL_WINDOWS), GROUP_DIM, GROUP_DIM).astype(BF16)
    scale = pool_scale.reshape(1, POOL_WIDTH)
    g1 = g1.reshape(1, D_MODEL)
    g2 = g2.reshape(1, D_MODEL)
    gf = gf.reshape(1, D_MODEL)

    h, u, qkv, fl, gates = _in_proj(x2, g1, w_uqkv, w_fl, w_g, fwd_token)
    fcol = _forget_fwd(fl, b_pad, n_seq, S)
    pm, p2, p3 = _pool_fwd(u, mix_b, scale, n_seq, S)
    a, lse = _attn_fwd(qkv, fcol, n_seq, S)
    w_po, w_ao, w_out = out_weights(a)
    merged, x1, attn_y, pool_y = _mix_out(a, p3, gates, x2, w_ao, w_po, w_out)
    w_gate_t, w_up_t, w_down = ffn_weights(x1)
    h2, gate, up, act, dx2, loss_rows, dgf = _ffn_fwd(x1, g2, gf, tg2, w_gate_t, w_up_t, w_down)

    dgate, dup, dx1, dg2 = _ffn_bwd(dx2, gate, up, x1, g2, w_gate_t, w_up_t, w_down)
    bwd_token = ffn_grads_out(_matmul_tn(dgate, h2, "dw_ffn_gate"), _matmul_tn(dup, h2, "dw_ffn_up"), _matmul_tn(act, dx2, "dw_ffn_down"))
    dgates, dpy, day, da, dp2, dscale = _mix_bwd(dx1, gates, pool_y, attn_y, p2, scale, w_out, w_ao, w_po, bwd_token)
    out_token = out_grads_out(_matmul_tn(p3, dpy, "dw_pool_out"), _matmul_tn(a, day, "dw_attn_out"), _matmul_tn(merged, dx1, "dw_out"))
    du, dmix = _pool_bwd(dp2, pm, mix_b, out_token, n_seq, S)
    dq, dk, dv, dfk, dfq = _attn_bwd(qkv, da, a, fcol, lse, n_seq, S)
    dfl, db = _forget_bwd(dfk, dfq, fl, b_pad, n_seq, S)
    small_token = small_grads_out((jnp.zeros_like(g1), dg2, dgf, db[:, :N_HEADS], dscale, dmix), loss_rows)
    in_token = in_grads_out(_dw_in(h, du, dq, dk, dv, dfl, dgates, small_token))
    dx, dg1 = _in_proj_bwd(du, dq, dk, dv, dfl, dgates, x2, dx1, g1, w_uqkv, w_fl, w_g, in_token)
    norm1_grad_out(dg1)
    return dx.reshape(n_seq, S, D_MODEL)


def kernel(x, norm1_g, w_in, b_forget, pool_mix, pool_scale, w_pool_out, w_attn_out, w_out, norm2_g, w_ffn_gate, w_ffn_up, w_ffn_down, norm_f_g, loss_target, m_norm1_g, m_w_in, m_b_forget, m_pool_mix, m_pool_scale, m_w_pool_out, m_w_attn_out, m_w_out, m_norm2_g, m_w_ffn_gate, m_w_ffn_up, m_w_ffn_down, m_norm_f_g, v_norm1_g, v_w_in, v_b_forget, v_pool_mix, v_pool_scale, v_w_pool_out, v_w_attn_out, v_w_out, v_norm2_g, v_w_ffn_gate, v_w_ffn_up, v_w_ffn_down, v_norm_f_g):
    names = ("w_in", "w_pool_out", "w_attn_out", "w_out", "w_ffn_gate", "w_ffn_up", "w_ffn_down")
    w_sh = (w_in, w_pool_out, w_attn_out, w_out, w_ffn_gate, w_ffn_up, w_ffn_down)
    m_sh = (m_w_in, m_w_pool_out, m_w_attn_out, m_w_out, m_w_ffn_gate, m_w_ffn_up, m_w_ffn_down)
    v_sh = (v_w_in, v_w_pool_out, v_w_attn_out, v_w_out, v_w_ffn_gate, v_w_ffn_up, v_w_ffn_down)

    cx, cy, cc = _position()
    me = 4 * cx + 2 * cy + cc
    def stored(t, transposed):
        return jnp.transpose(t, (0, 2, 1)) if transposed else t

    w_sh, m_sh, v_sh = ([stored(t, tr) for t, tr in zip(ts, _TRANSPOSED)] for ts in (w_sh, m_sh, v_sh))
    shards = [w[0].astype(BF16) for w in w_sh]
    (gathered_in,) = _all_gather(shards[:1], "w_in_all_gather")
    out_sems = _exchange_start(shards[1:4], gathered_in, "out_weights_gather_start", "gather")
    ffn_sems = _exchange_start(shards[4:], out_sems[4], "ffn_weights_gather_start", "gather")
    no_order = jnp.zeros((8, LANES), F32)

    def with_own(lands, own):
        return [lax.dynamic_update_slice(l, o[None], (me, 0, 0)) for l, o in zip(lands, own)]

    def gathered_weights(sems, axes, name):
        def wait(after):
            send_sems, recv_sems, srcs, lands, _ = sems
            srcs, lands = _exchange_wait(send_sems, recv_sems, srcs, lands, after, name, "gather")
            return [_full_from_gathered(t, axis) for t, axis in zip(with_own(lands, srcs), axes)]

        return wait

    started = {}

    def scatter_grads(key, name):
        def start(*whole_grads):
            chunks = [
                _chunks_from_cols(t) if axis == 1 else t.reshape(N_DEV, -1, t.shape[1])
                for t, axis in zip(whole_grads, _SHARD_AXIS[key])
            ]
            started[key] = _exchange_start(chunks, no_order, name, "scatter")
            return started[key][4]

        return start

    def gather_small(small, loss_rows):
        started["small"] = _exchange_start([_pack_small(small, loss_rows)], no_order, "small_grads_gather_start", "gather")
        return started["small"][4]

    core = jnp.reshape(cc, (1,)).astype(jnp.int32)
    pos = jnp.stack([cc, 2 * cx + cy]).astype(jnp.int32)

    def reduce_w_in(send_in):
        (got_in,) = _sibling_exchange([send_in])
        pair_in = _pair_sum(send_in, got_in, core, "pair_sum_w_in")
        started["in"] = (send_in, got_in, _exchange_start([pair_in], no_order, "w_in_grads_chips_start", "chips"))
        return started["in"][2][4]

    def gather_norm1(dg1):
        rows = jnp.reshape(dg1, (8, LANES))
        started["norm1"] = _exchange_start([rows], no_order, "norm1_grad_gather_start", "gather")

    ffn, out = slice(4, 7), slice(1, 4)
    grad_x = _local_grads(
        x, loss_target, norm1_g, norm2_g, norm_f_g, b_forget, pool_mix, pool_scale, _w_in_pieces(gathered_in), ffn_sems[4],
        gathered_weights(out_sems, _SHARD_AXIS[out], "out_weights_gather_wait"),
        gathered_weights(ffn_sems, _SHARD_AXIS[ffn], "ffn_weights_gather_wait"),
        scatter_grads(ffn, "ffn_grads_scatter_start"), scatter_grads(out, "out_grads_scatter_start"), gather_small, reduce_w_in, gather_norm1,
    )
    send_in, got_in, chip_sems = started["in"]

    def scattered_updates(key, after, name):
        send_sems, recv_sems, srcs, lands, _ = started[key]
        srcs, lands = _exchange_wait(send_sems, recv_sems, srcs, lands, after, name, "scatter")
        return [
            _shard_update_direct(p, s, w, m, v, jnp.reshape(me, (1,)).astype(jnp.int32), "update_" + n)
            for p, s, w, m, v, n in zip(lands, srcs, w_sh[key], m_sh[key], v_sh[key], names[key])
        ]

    updates_out = scattered_updates(out, grad_x, "out_grads_scatter_wait")
    updates_ffn = scattered_updates(ffn, grad_x, "ffn_grads_scatter_wait")

    small_w = (norm1_g, norm2_g, norm_f_g, b_forget, pool_scale, pool_mix)
    small_m = (m_norm1_g, m_norm2_g, m_norm_f_g, m_b_forget, m_pool_scale, m_pool_mix)
    small_v = (v_norm1_g, v_norm2_g, v_norm_f_g, v_b_forget, v_pool_scale, v_pool_mix)
    zero_row = jnp.zeros((8, LANES), F32)
    send_sems, recv_sems, srcs, lands, _ = chip_sems
    _, (recv_in,) = _exchange_wait(send_sems, recv_sems, srcs, lands, updates_ffn[-1][0], "w_in_grads_chips_wait", "chips")
    update_in = _shard_update(send_in, got_in, recv_in, w_in, m_w_in, v_w_in, pos, "update_w_in")

    def gathered_small(key, after, name):
        send_sems, recv_sems, srcs, lands, _ = started[key]
        srcs, lands = _exchange_wait(send_sems, recv_sems, srcs, lands, after, name, "gather")
        return with_own(lands, srcs)[0]

    parts = gathered_small("small", update_in[0], "small_grads_gather_wait")
    first_rows = gathered_small("norm1", parts, "norm1_grad_gather_wait")
    g_s, d_s, nm_s, nv_s = _small_update(parts, first_rows, _pack_small(small_w, zero_row), _pack_small(small_m, zero_row), _pack_small(small_v, zero_row))
    g_w, d_w, nm_w, nv_w = zip(*(
        [stored(t, tr) for t in u] for u, tr in zip([update_in] + updates_out + updates_ffn, _TRANSPOSED)
    ))
    shapes = [t.shape for t in small_w]
    (g1, g2, gf, gb, gsc, gmix), loss = _unpack_small(g_s, shapes)
    (d1, d2, df, db_, dsc, dmx), _ = _unpack_small(d_s, shapes)
    (m1, m2, mf, mb, msc, mmx), _ = _unpack_small(nm_s, shapes)
    (v1, v2, vf, vb, vsc, vmx), _ = _unpack_small(nv_s, shapes)

    def ordered(n1, win, b, mix, sc, wpo, wao, wout, n2, wg, wu, wd, nf):
        return (n1, win, b, mix, sc, wpo, wao, wout, n2, wg, wu, wd, nf)

    grads = ordered(g1, g_w[0], gb, gmix, gsc, g_w[1], g_w[2], g_w[3], g2, g_w[4], g_w[5], g_w[6], gf)
    deltas = ordered(d1, d_w[0], db_, dmx, dsc, d_w[1], d_w[2], d_w[3], d2, d_w[4], d_w[5], d_w[6], df)
    new_m = ordered(m1, nm_w[0], mb, mmx, msc, nm_w[1], nm_w[2], nm_w[3], m2, nm_w[4], nm_w[5], nm_w[6], mf)
    new_v = ordered(v1, nv_w[0], vb, vmx, vsc, nv_w[1], nv_w[2], nv_w[3], v2, nv_w[4], nv_w[5], nv_w[6], vf)
    return (loss, grad_x, *grads, *deltas, *new_m, *new_v)
```

```python
import jax
import jax.numpy as jnp
from jax import lax
from jax.experimental import pallas as pl
from jax.experimental.pallas import tpu as pltpu

F32 = jnp.float32
BF16 = jnp.bfloat16
MESH = pl.DeviceIdType.MESH

D_MODEL = 1024
POOL_WINDOWS = (2, 4, 8, 16)
POOL_WIDTH = 512
GROUP_DIM = 128
ATTN_WIDTH = 512
HEAD_DIM = 64
N_HEADS = 8
N_PAIRS = 4
D_FF = 2816
RMS_EPS = 1e-6
N_DEV = 8
LANES = 128
FL_PAD = 128

ADAM_LR = 0.001
ADAM_B1 = 0.9
ADAM_B2 = 0.999
ADAM_EPS = 1e-08
ADAM_WD = 0.01
ADAM_STEP = 10

VMEM_LIMIT = 56 * 1024 * 1024
VMEM_LIMIT_MAX = 60 * 1024 * 1024
ROW_TILE = 512
ATTN_BLOCK = 512
FF_CHUNK = 256
FF_ROW_TILE = 512
DW_TOKENS = 2048


def _mm(a, b):
    return jnp.dot(a, b, preferred_element_type=F32)


def _mm_nt(a, b):
    return lax.dot_general(a, b, (((1,), (1,)), ((), ())), preferred_element_type=F32)


def _mm_tn(a, b):
    return lax.dot_general(a, b, (((0,), (0,)), ((), ())), preferred_element_type=F32)


def _sigmoid(x):
    return 1.0 / (1.0 + jnp.exp(-x))


def _params(sem, vmem=VMEM_LIMIT):
    return pltpu.CompilerParams(dimension_semantics=sem, vmem_limit_bytes=vmem)


def _const_spec(shape):
    nd = len(shape)
    return pl.BlockSpec(shape, lambda *_: (0,) * nd, pipeline_mode=pl.Buffered(1))


def _rms_fwd(x, g):
    r = lax.rsqrt(jnp.mean(x * x, axis=-1, keepdims=True) + RMS_EPS)
    xh = x * r
    return xh * g, xh, r


def _rms_bwd(dy, xh, r, g):
    dxh = dy * g
    dx = r * (dxh - xh * jnp.mean(dxh * xh, axis=-1, keepdims=True))
    return dx, dy * xh


def _in_proj(x, g1, w_uqkv, w_fl, w_g, token):
    T = x.shape[0]
    tm = ROW_TILE

    def body(x_ref, g_ref, wa_ref, wf_ref, wg_ref, token_ref, h_ref, u_ref, qkv_ref, fl_ref, gt_ref):
        h, _, _ = _rms_fwd(x_ref[...], g_ref[...])
        hb = h.astype(BF16)
        h_ref[...] = hb
        z = _mm(hb, wa_ref[...])
        u_ref[...] = z[:, :POOL_WIDTH]
        qkv_ref[...] = z[:, POOL_WIDTH:].astype(BF16)
        fl_ref[...] = _mm(hb, wf_ref[...])
        gt_ref[...] = _mm(hb, wg_ref[...]).astype(BF16)

    row = lambda n: pl.BlockSpec((tm, n), lambda i: (i, 0))
    return pl.pallas_call(
        body,
        name="in_proj",
        grid=(T // tm,),
        in_specs=[row(D_MODEL), _const_spec((1, D_MODEL)), _const_spec(w_uqkv.shape), _const_spec(w_fl.shape), _const_spec(w_g.shape), _HBM],
        out_specs=[row(D_MODEL), row(POOL_WIDTH), row(3 * ATTN_WIDTH), row(FL_PAD), row(2 * D_MODEL)],
        out_shape=[
            jax.ShapeDtypeStruct((T, D_MODEL), BF16),
            jax.ShapeDtypeStruct((T, POOL_WIDTH), F32),
            jax.ShapeDtypeStruct((T, 3 * ATTN_WIDTH), BF16),
            jax.ShapeDtypeStruct((T, FL_PAD), F32),
            jax.ShapeDtypeStruct((T, 2 * D_MODEL), BF16),
        ],
        compiler_params=_params(("parallel",)),
    )(x, g1, w_uqkv, w_fl, w_g, token)


def _log_sigmoid(x):
    return jnp.minimum(x, 0.0) - jnp.log(1.0 + jnp.exp(-jnp.abs(x)))


def _forget_fwd(fl, b_pad, n_seq, S):
    def body(fl_ref, b_ref, fcol_ref):
        lf = _log_sigmoid(fl_ref[...] + b_ref[...])
        t = lf.T
        lane = lax.broadcasted_iota(jnp.int32, t.shape, 1)
        k = 1
        while k < S:
            t = t + jnp.where(lane >= k, pltpu.roll(t, k, 1), 0.0)
            k *= 2
        fcol_ref[...] = t.T

    return pl.pallas_call(
        body,
        name="forget_fwd",
        grid=(n_seq,),
        in_specs=[pl.BlockSpec((S, FL_PAD), lambda s: (s, 0)), _const_spec((1, FL_PAD))],
        out_specs=pl.BlockSpec((S, FL_PAD), lambda s: (s, 0)),
        out_shape=jax.ShapeDtypeStruct((n_seq * S, FL_PAD), F32),
        compiler_params=_params(("parallel",)),
    )(fl, b_pad)


def _window_pick(g, v2, v4, v8, v16):
    return jnp.where(g == 0, v2, jnp.where(g == 1, v4, jnp.where(g == 2, v8, v16)))


def _pool_fwd(u, mix_b, scale, n_seq, S):
    T = n_seq * S

    def body(u_ref, mix_ref, sc_ref, pm_ref, p2_ref, p3_ref):
        g = pl.program_id(1)
        uu = u_ref[...]
        row = lax.broadcasted_iota(jnp.int32, uu.shape, 0)

        def back(a, k):
            return jnp.where(row >= k, pltpu.roll(a, k, 0), 0.0)

        s2 = uu + back(uu, 1)
        s4 = s2 + back(s2, 2)
        s8 = s4 + back(s4, 4)
        s16 = s8 + back(s8, 8)
        w = _window_pick(g, 2.0, 4.0, 8.0, 16.0)
        cnt = jnp.minimum((row + 1).astype(F32), w)
        pm = _window_pick(g, s2, s4, s8, s16) / cnt - uu
        pmb = pm.astype(BF16)
        pm_ref[...] = pmb
        p2 = _mm(pmb, mix_ref[...])
        p2_ref[...] = p2
        p3_ref[...] = (p2 * sc_ref[...]).astype(BF16)

    grp = pl.BlockSpec((S, GROUP_DIM), lambda s, g: (s, g))
    return pl.pallas_call(
        body,
        name="pool_fwd",
        grid=(n_seq, len(POOL_WINDOWS)),
        in_specs=[
            grp,
            pl.BlockSpec((None, GROUP_DIM, GROUP_DIM), lambda s, g: (g, 0, 0)),
            pl.BlockSpec((1, GROUP_DIM), lambda s, g: (0, g)),
        ],
        out_specs=[grp, grp, grp],
        out_shape=[
            jax.ShapeDtypeStruct((T, POOL_WIDTH), BF16),
            jax.ShapeDtypeStruct((T, POOL_WIDTH), F32),
            jax.ShapeDtypeStruct((T, POOL_WIDTH), BF16),
        ],
        compiler_params=_params(("parallel", "parallel")),
    )(u, mix_b, scale)


def _split3(v):
    hi = v.astype(BF16).astype(F32)
    r = v - hi
    mid = r.astype(BF16).astype(F32)
    lo = (r - mid).astype(BF16).astype(F32)
    return hi, mid, lo


def _bias_lanes(v):
    hi, mid, lo = _split3(v)
    lane = lax.broadcasted_iota(jnp.int32, (1, LANES), 1)
    packed = jnp.where(lane < N_HEADS, hi, jnp.where(lane < 2 * N_HEADS, pltpu.roll(mid, N_HEADS, 1), pltpu.roll(lo, 2 * N_HEADS, 1)))
    return jnp.where(lane < 3 * N_HEADS, packed, 0.0).astype(BF16)


def _bias_placement(slot):
    row = lax.broadcasted_iota(jnp.int32, (LANES, N_HEADS * LANES), 0)
    col = lax.broadcasted_iota(jnp.int32, (LANES, N_HEADS * LANES), 1)
    h = col // LANES
    n = col % LANES - jnp.where(h % 2 == 0, HEAD_DIM, 0) - 3 * slot
    return ((n >= 0) & (n < 3) & (row == N_HEADS * n + h)).astype(BF16)


def _augment(xp, h, bias, ones_slot):
    lane = lax.broadcasted_iota(jnp.int32, (1, LANES), 1)
    hh = h % 2
    head = (lane >= HEAD_DIM * hh) & (lane < HEAD_DIM * (hh + 1))
    b = HEAD_DIM * (1 - hh)
    rest = jnp.zeros_like(xp) if bias is None else bias[:, h * LANES : (h + 1) * LANES]
    out = jnp.where(head, xp, rest)
    if ones_slot is not None:
        out = jnp.where((lane >= b + 3 * ones_slot) & (lane < b + 3 * ones_slot + 3), jnp.ones_like(xp), out)
    return out


def _attn_fwd(qkv, fcol, n_seq, S):
    T = n_seq * S
    tb = ATTN_BLOCK
    nq = S // tb
    scale = HEAD_DIM ** -0.5

    def body(q_ref, k_ref, v_ref, fc_ref, o_ref, st_ref, qa_sc, ka_sc, m_sc, l_sc, acc_sc):
        i = pl.program_id(1)
        lane = lax.broadcasted_iota(jnp.int32, (1, LANES), 1)
        low = lane < HEAD_DIM

        @pl.when(i == 0)
        def _():
            place = _bias_placement(1)

            def rows_ka(r, carry):
                r0 = pl.multiple_of(r * tb, tb)
                bias = _mm(_bias_lanes(-fc_ref[pl.ds(r0, tb), :]), place).astype(BF16)
                for h in range(N_HEADS):
                    kp = k_ref[pl.ds(r0, tb), (h // 2) * LANES : (h // 2 + 1) * LANES] * scale
                    ka_sc[h, pl.ds(r0, tb), :] = _augment(kp, h, bias, 0)
                return carry

            lax.fori_loop(0, nq, rows_ka, 0)

        q0 = pl.multiple_of(i * tb, tb)
        bias = _mm(_bias_lanes(fc_ref[pl.ds(q0, tb), :]), _bias_placement(0)).astype(BF16)
        for h in range(N_HEADS):
            qa_sc[h] = _augment(q_ref[:, (h // 2) * LANES : (h // 2 + 1) * LANES], h, bias, 1)
        m_sc[...] = jnp.full(m_sc.shape, -jnp.inf, F32)
        l_sc[...] = jnp.zeros_like(l_sc)
        acc_sc[...] = jnp.zeros_like(acc_sc)
        causal = lax.broadcasted_iota(jnp.int32, (tb, tb), 1) <= lax.broadcasted_iota(jnp.int32, (tb, tb), 0)

        def step(j, masked):
            c0 = pl.multiple_of(j * tb, tb)
            for p in range(N_PAIRS):
                vb = v_ref[pl.ds(c0, tb), p * LANES : (p + 1) * LANES]
                pv, al = [], []
                for hh in range(2):
                    h = 2 * p + hh
                    s = _mm_nt(qa_sc[h], ka_sc[h, pl.ds(c0, tb), :])
                    if masked:
                        s = jnp.where(causal, s, -jnp.inf)
                    m_old = m_sc[h]
                    m_new = jnp.maximum(m_old, jnp.max(s, axis=1, keepdims=True))
                    alpha = jnp.exp(m_old - m_new)
                    pe = jnp.exp(s - jnp.concatenate([m_new] * (tb // LANES), axis=1))
                    l_sc[h] = alpha * l_sc[h] + jnp.sum(pe, axis=1, keepdims=True)
                    m_sc[h] = m_new
                    pv.append(_mm(pe.astype(BF16), vb))
                    al.append(alpha)
                acc_sc[p] = jnp.where(low, al[0], al[1]) * acc_sc[p] + jnp.where(low, pv[0], pv[1])

        def loop_body(j, carry):
            step(j, False)
            return carry

        lax.fori_loop(0, i, loop_body, 0)
        step(i, True)
        st = jnp.zeros((tb, LANES), F32)
        for p in range(N_PAIRS):
            lp = jnp.where(low, l_sc[2 * p], l_sc[2 * p + 1])
            o_ref[:, p * LANES : (p + 1) * LANES] = (acc_sc[p] / lp).astype(BF16)
            for h in (2 * p, 2 * p + 1):
                st = jnp.where(lane == h, m_sc[h] + jnp.log(l_sc[h]), st)
        st_ref[...] = st

    return pl.pallas_call(
        body,
        name="attn_fwd",
        grid=(n_seq, nq),
        in_specs=[
            pl.BlockSpec((tb, ATTN_WIDTH), lambda s, i: (s * nq + i, 0)),
            pl.BlockSpec((S, ATTN_WIDTH), lambda s, i: (s, 1)),
            pl.BlockSpec((S, ATTN_WIDTH), lambda s, i: (s, 2)),
            pl.BlockSpec((S, LANES), lambda s, i: (s, 0)),
        ],
        out_specs=[
            pl.BlockSpec((tb, ATTN_WIDTH), lambda s, i: (s * nq + i, 0)),
            pl.BlockSpec((tb, LANES), lambda s, i: (s * nq + i, 0)),
        ],
        out_shape=[jax.ShapeDtypeStruct((T, ATTN_WIDTH), BF16), jax.ShapeDtypeStruct((T, LANES), F32)],
        scratch_shapes=[
            pltpu.VMEM((N_HEADS, tb, LANES), BF16),
            pltpu.VMEM((N_HEADS, S, LANES), BF16),
            pltpu.VMEM((N_HEADS, tb, LANES), F32),
            pltpu.VMEM((N_HEADS, tb, LANES), F32),
            pltpu.VMEM((N_PAIRS, tb, LANES), F32),
        ],
        compiler_params=_params(("parallel", "arbitrary")),
    )(qkv, qkv, qkv, fcol)


def _mix_out(a, p3, gates, x, w_ao, w_po, w_out):
    T = x.shape[0]
    tm = ROW_TILE

    def body(a_ref, p3_ref, gt_ref, x_ref, wao_ref, wpo_ref, wout_ref, mg_ref, x1_ref, ay_ref, py_ref):
        ay = _mm(a_ref[...], wao_ref[...])
        py = _mm(p3_ref[...], wpo_ref[...])
        ay_ref[...] = ay.astype(BF16)
        py_ref[...] = py.astype(BF16)
        sp = _sigmoid(gt_ref[:, :D_MODEL].astype(F32))
        sa = _sigmoid(gt_ref[:, D_MODEL:].astype(F32))
        mb = (sp * py + sa * ay).astype(BF16)
        mg_ref[...] = mb
        x1_ref[...] = x_ref[...] + _mm(mb, wout_ref[...])

    row = lambda n: pl.BlockSpec((tm, n), lambda i: (i, 0))
    return pl.pallas_call(
        body,
        name="mix_out",
        grid=(T // tm,),
        in_specs=[
            row(ATTN_WIDTH), row(POOL_WIDTH), row(2 * D_MODEL), row(D_MODEL),
            _const_spec(w_ao.shape), _const_spec(w_po.shape), _const_spec(w_out.shape),
        ],
        out_specs=[row(D_MODEL), row(D_MODEL), row(D_MODEL), row(D_MODEL)],
        out_shape=[
            jax.ShapeDtypeStruct((T, D_MODEL), BF16), jax.ShapeDtypeStruct((T, D_MODEL), F32),
            jax.ShapeDtypeStruct((T, D_MODEL), BF16), jax.ShapeDtypeStruct((T, D_MODEL), BF16),
        ],
        compiler_params=_params(("parallel",)),
    )(a, p3, gates, x, w_ao, w_po, w_out)


def _ffn_fwd(x1, g2, gf, tgt, w_gate_t, w_up_t, w_down):
    T = x1.shape[0]
    tm = min(T, FF_ROW_TILE)
    nt = T // tm
    nc = D_FF // FF_CHUNK

    def body(x1_ref, g2_ref, gf_ref, tg_ref, wg_ref, wu_ref, wd_ref, h2_ref, gate_ref, up_ref, act_ref, dx2_ref, loss_ref, dgf_ref):
        x1v = x1_ref[...]
        h2, _, _ = _rms_fwd(x1v, g2_ref[...])
        h2b = h2.astype(BF16)
        h2_ref[...] = h2b
        for c in range(nc):
            sl = slice(c * FF_CHUNK, (c + 1) * FF_CHUNK)
            gate = _mm_nt(h2b, wg_ref[sl, :])
            up = _mm_nt(h2b, wu_ref[sl, :])
            gate_ref[:, sl] = gate.astype(BF16)
            up_ref[:, sl] = up.astype(BF16)
            act_ref[:, sl] = (gate * _sigmoid(gate) * up).astype(BF16)
        acc = x1v + _mm(act_ref[...], wd_ref[...])
        gfv = gf_ref[...]
        y, xh, r = _rms_fwd(acc, gfv)
        err = y - tg_ref[...]
        part = 0.5 * jnp.sum(jnp.mean(err * err, axis=-1, keepdims=True), axis=0, keepdims=True)
        dx2, dgrow = _rms_bwd(err * (1.0 / D_MODEL), xh, r, gfv)
        dx2_ref[...] = dx2

        @pl.when(pl.program_id(0) == 0)
        def _():
            dgf_ref[...] = jnp.zeros_like(dgf_ref)
            loss_ref[...] = jnp.zeros_like(loss_ref)

        dgf_ref[...] += jnp.sum(dgrow, axis=0, keepdims=True)
        loss_ref[...] += jnp.broadcast_to(part, loss_ref.shape)

    row = lambda n: pl.BlockSpec((tm, n), lambda i: (i, 0))
    return pl.pallas_call(
        body,
        name="ffn_fwd",
        grid=(nt,),
        in_specs=[
            row(D_MODEL), _const_spec((1, D_MODEL)), _const_spec((1, D_MODEL)), row(D_MODEL),
            _const_spec(w_gate_t.shape), _const_spec(w_up_t.shape), _const_spec(w_down.shape),
        ],
        out_specs=[
            row(D_MODEL), row(D_FF), row(D_FF), row(D_FF), row(D_MODEL),
            pl.BlockSpec((8, LANES), lambda i: (0, 0)),
            pl.BlockSpec((1, D_MODEL), lambda i: (0, 0)),
        ],
        out_shape=[
            jax.ShapeDtypeStruct((T, D_MODEL), BF16),
            jax.ShapeDtypeStruct((T, D_FF), BF16),
            jax.ShapeDtypeStruct((T, D_FF), BF16),
            jax.ShapeDtypeStruct((T, D_FF), BF16),
            jax.ShapeDtypeStruct((T, D_MODEL), F32),
            jax.ShapeDtypeStruct((8, LANES), F32),
            jax.ShapeDtypeStruct((1, D_MODEL), F32),
        ],
        compiler_params=_params(("arbitrary",)),
    )(x1, g2, gf, tgt, w_gate_t, w_up_t, w_down)


def _ffn_bwd(dx2, gate, up, x1, g2, w_gate_t, w_up_t, w_down):
    T = x1.shape[0]
    tm = min(T, FF_ROW_TILE)
    nc = D_FF // FF_CHUNK

    def body(dx2_ref, gate_ref, up_ref, x1_ref, g2_ref, wg_ref, wu_ref, wd_ref, dgate_ref, dup_ref, dx1_ref, dg2_ref):
        dx2v = dx2_ref[...]
        dx2b = dx2v.astype(BF16)
        for c in range(nc):
            sl = slice(c * FF_CHUNK, (c + 1) * FF_CHUNK)
            dact = _mm_nt(dx2b, wd_ref[sl, :])
            gate = gate_ref[:, sl].astype(F32)
            sg = _sigmoid(gate)
            silu = gate * sg
            dgate = (dact * up_ref[:, sl].astype(F32) * (sg * (1.0 + gate * (1.0 - sg)))).astype(BF16)
            dup = (dact * silu).astype(BF16)
            dgate_ref[:, sl] = dgate
            dup_ref[:, sl] = dup
        dh2 = _mm(dgate_ref[...], wg_ref[...]) + _mm(dup_ref[...], wu_ref[...])
        g2v = g2_ref[...]
        _, xh, r = _rms_fwd(x1_ref[...], g2v)
        dxn, dgrow = _rms_bwd(dh2, xh, r, g2v)
        dx1_ref[...] = dx2v + dxn

        @pl.when(pl.program_id(0) == 0)
        def _():
            dg2_ref[...] = jnp.zeros_like(dg2_ref)

        dg2_ref[...] += jnp.sum(dgrow, axis=0, keepdims=True)

    row = lambda n: pl.BlockSpec((tm, n), lambda i: (i, 0))
    return pl.pallas_call(
        body,
        name="ffn_bwd",
        grid=(T // tm,),
        in_specs=[
            row(D_MODEL), row(D_FF), row(D_FF), row(D_MODEL), _const_spec((1, D_MODEL)),
            _const_spec(w_gate_t.shape), _const_spec(w_up_t.shape), _const_spec(w_down.shape),
        ],
        out_specs=[row(D_FF), row(D_FF), row(D_MODEL), pl.BlockSpec((1, D_MODEL), lambda i: (0, 0))],
        out_shape=[
            jax.ShapeDtypeStruct((T, D_FF), BF16),
            jax.ShapeDtypeStruct((T, D_FF), BF16),
            jax.ShapeDtypeStruct((T, D_MODEL), F32),
            jax.ShapeDtypeStruct((1, D_MODEL), F32),
        ],
        compiler_params=_params(("arbitrary",), VMEM_LIMIT_MAX),
    )(dx2, gate, up, x1, g2, w_gate_t, w_up_t, w_down)


def _mix_bwd(dx1, gates, pool_y, attn_y, p2, scale, w_out, w_ao, w_po, token):
    T = dx1.shape[0]
    tm = ROW_TILE

    def body(dx1_ref, gt_ref, py_ref, ay_ref, p2_ref, sc_ref, wout_ref, wao_ref, wpo_ref, token_ref, dgt_ref, dpy_ref, day_ref, da_ref, dp2_ref, dsc_ref):
        dm = _mm_nt(dx1_ref[...].astype(BF16), wout_ref[...])
        sp = _sigmoid(gt_ref[:, :D_MODEL].astype(F32))
        sa = _sigmoid(gt_ref[:, D_MODEL:].astype(F32))
        dgt_ref[:, :D_MODEL] = (dm * py_ref[...].astype(F32) * (sp * (1.0 - sp))).astype(BF16)
        dgt_ref[:, D_MODEL:] = (dm * ay_ref[...].astype(F32) * (sa * (1.0 - sa))).astype(BF16)
        dpy = (dm * sp).astype(BF16)
        day = (dm * sa).astype(BF16)
        dpy_ref[...] = dpy
        day_ref[...] = day
        da_ref[...] = _mm_nt(day, wao_ref[...]).astype(BF16)
        dp3 = _mm_nt(dpy, wpo_ref[...])
        dp2_ref[...] = (dp3 * sc_ref[...]).astype(BF16)

        @pl.when(pl.program_id(0) == 0)
        def _():
            dsc_ref[...] = jnp.zeros_like(dsc_ref)

        dsc_ref[...] += jnp.sum(dp3 * p2_ref[...], axis=0, keepdims=True)

    row = lambda n: pl.BlockSpec((tm, n), lambda i: (i, 0))
    return pl.pallas_call(
        body,
        name="mix_bwd",
        grid=(T // tm,),
        in_specs=[
            row(D_MODEL), row(2 * D_MODEL), row(D_MODEL), row(D_MODEL), row(POOL_WIDTH), _const_spec((1, POOL_WIDTH)),
            _const_spec(w_out.shape), _const_spec(w_ao.shape), _const_spec(w_po.shape), _HBM,
        ],
        out_specs=[row(2 * D_MODEL), row(D_MODEL), row(D_MODEL), row(ATTN_WIDTH), row(POOL_WIDTH), pl.BlockSpec((1, POOL_WIDTH), lambda i: (0, 0))],
        out_shape=[
            jax.ShapeDtypeStruct((T, 2 * D_MODEL), BF16),
            jax.ShapeDtypeStruct((T, D_MODEL), BF16),
            jax.ShapeDtypeStruct((T, D_MODEL), BF16),
            jax.ShapeDtypeStruct((T, ATTN_WIDTH), BF16),
            jax.ShapeDtypeStruct((T, POOL_WIDTH), BF16),
            jax.ShapeDtypeStruct((1, POOL_WIDTH), F32),
        ],
        compiler_params=_params(("arbitrary",)),
    )(dx1, gates, pool_y, attn_y, p2, scale, w_out, w_ao, w_po, token)


def _pool_bwd(dp2, pm, mix_b, token, n_seq, S):
    T = n_seq * S

    def body(dp2_ref, pm_ref, mix_ref, token_ref, du_ref, dmix_ref):
        g = pl.program_id(0)
        dp2v = dp2_ref[...]
        dpm = _mm_nt(dp2v, mix_ref[...])
        row = lax.broadcasted_iota(jnp.int32, dpm.shape, 0)
        w = _window_pick(g, 2.0, 4.0, 8.0, 16.0)
        e = dpm / jnp.minimum((row + 1).astype(F32), w)

        def ahead(a, k):
            return jnp.where(row < S - k, pltpu.roll(a, S - k, 0), 0.0)

        r2 = e + ahead(e, 1)
        r4 = r2 + ahead(r2, 2)
        r8 = r4 + ahead(r4, 4)
        r16 = r8 + ahead(r8, 8)
        du_ref[...] = (_window_pick(g, r2, r4, r8, r16) - dpm).astype(BF16)

        @pl.when(pl.program_id(1) == 0)
        def _():
            dmix_ref[...] = jnp.zeros_like(dmix_ref)

        dmix_ref[...] += _mm_tn(pm_ref[...], dp2v)

    grp = pl.BlockSpec((S, GROUP_DIM), lambda g, s: (s, g))
    mixs = pl.BlockSpec((None, GROUP_DIM, GROUP_DIM), lambda g, s: (g, 0, 0))
    return pl.pallas_call(
        body,
        name="pool_bwd",
        grid=(len(POOL_WINDOWS), n_seq),
        in_specs=[grp, grp, mixs, _HBM],
        out_specs=[grp, mixs],
        out_shape=[jax.ShapeDtypeStruct((T, POOL_WIDTH), BF16), jax.ShapeDtypeStruct((len(POOL_WINDOWS), GROUP_DIM, GROUP_DIM), F32)],
        compiler_params=_params(("parallel", "arbitrary")),
    )(dp2, pm, mix_b, token)


def _attn_bwd(qkv, da, a, fcol, lse, n_seq, S):
    T = n_seq * S
    tb = ATTN_BLOCK
    nb = S // tb
    scale = HEAD_DIM ** -0.5

    def body(q_ref, k_ref, v_ref, do_ref, o_ref, fc_ref, st_ref, dq_ref, dk_ref, dv_ref, dfk_ref, dfq_ref,
             qa_sc, doa_sc, dq_acc, ka_sc, va_sc, dk_sc, dv_sc):
        j = pl.program_id(1)
        lane = lax.broadcasted_iota(jnp.int32, (1, LANES), 1)
        low = lane < HEAD_DIM

        @pl.when(j == 0)
        def _():
            dq_acc[...] = jnp.zeros_like(dq_acc)
            place = _bias_placement(0)

            def rows_q(i, carry):
                r0 = pl.multiple_of(i * tb, tb)
                delta = jnp.zeros((tb, LANES), F32)
                for h in range(N_HEADS):
                    pair = slice((h // 2) * LANES, (h // 2 + 1) * LANES)
                    prod = do_ref[pl.ds(r0, tb), pair].astype(F32) * o_ref[pl.ds(r0, tb), pair].astype(F32)
                    head = (lane >= HEAD_DIM * (h % 2)) & (lane < HEAD_DIM * (h % 2 + 1))
                    delta = jnp.where(lane == h, jnp.sum(jnp.where(head, prod, 0.0), axis=1, keepdims=True), delta)
                cq = fc_ref[pl.ds(r0, tb), :] - st_ref[pl.ds(r0, tb), :]
                q_bias = _mm(_bias_lanes(cq), place).astype(BF16)
                do_bias = _mm(_bias_lanes(-delta), place).astype(BF16)
                for h in range(N_HEADS):
                    pair = slice((h // 2) * LANES, (h // 2 + 1) * LANES)
                    qa_sc[h, pl.ds(r0, tb), :] = _augment(q_ref[pl.ds(r0, tb), pair], h, q_bias, 1)
                    doa_sc[h, pl.ds(r0, tb), :] = _augment(do_ref[pl.ds(r0, tb), pair], h, do_bias, None)
                return carry

            lax.fori_loop(0, nb, rows_q, 0)

        c0 = pl.multiple_of(j * tb, tb)
        k_bias = _mm(_bias_lanes(-fc_ref[pl.ds(c0, tb), :]), _bias_placement(1)).astype(BF16)
        for h in range(N_HEADS):
            pair = slice((h // 2) * LANES, (h // 2 + 1) * LANES)
            ka_sc[h] = _augment(k_ref[:, pair] * scale, h, k_bias, 0)
            va_sc[h] = _augment(v_ref[:, pair], h, None, 0)
        dk_sc[...] = jnp.zeros_like(dk_sc)
        dv_sc[...] = jnp.zeros_like(dv_sc)
        causal = lax.broadcasted_iota(jnp.int32, (tb, tb), 1) <= lax.broadcasted_iota(jnp.int32, (tb, tb), 0)

        def step(i, masked):
            r0 = pl.multiple_of(i * tb, tb)
            for h in range(N_HEADS):
                dob = do_ref[pl.ds(r0, tb), (h // 2) * LANES : (h // 2 + 1) * LANES]
                qa = qa_sc[h, pl.ds(r0, tb), :]
                s = _mm_nt(qa, ka_sc[h])
                if masked:
                    s = jnp.where(causal, s, -jnp.inf)
                pr = jnp.exp(s)
                dv_sc[h] += _mm_tn(pr.astype(BF16), dob)
                dsb = (pr * _mm_nt(doa_sc[h, pl.ds(r0, tb), :], va_sc[h])).astype(BF16)
                dk_sc[h] += _mm_tn(dsb, qa)
                dq_acc[h, pl.ds(r0, tb), :] += _mm(dsb, ka_sc[h])

        step(j, True)

        def loop_body(i, carry):
            step(i, False)
            return carry

        lax.fori_loop(j + 1, nb, loop_body, 0)
        dfk = jnp.zeros((tb, LANES), F32)
        for p in range(N_PAIRS):
            dk_ref[:, p * LANES : (p + 1) * LANES] = (jnp.where(low, dk_sc[2 * p], dk_sc[2 * p + 1]) * scale).astype(BF16)
            dv_ref[:, p * LANES : (p + 1) * LANES] = jnp.where(low, dv_sc[2 * p], dv_sc[2 * p + 1]).astype(BF16)
            for hh in range(2):
                b = HEAD_DIM * (1 - hh) + 3
                dfk = jnp.where(lane == 2 * p + hh, -dk_sc[2 * p + hh][:, b : b + 1], dfk)
        dfk_ref[...] = dfk

        @pl.when(j == nb - 1)
        def _():
            def rows_dq(i, carry):
                r0 = pl.multiple_of(i * tb, tb)
                dfq = jnp.zeros((tb, LANES), F32)
                for p in range(N_PAIRS):
                    parts = [dq_acc[2 * p + hh, pl.ds(r0, tb), :] for hh in range(2)]
                    dq_ref[pl.ds(r0, tb), p * LANES : (p + 1) * LANES] = jnp.where(low, parts[0], parts[1]).astype(BF16)
                    for hh in range(2):
                        b = HEAD_DIM * (1 - hh)
                        dfq = jnp.where(lane == 2 * p + hh, parts[hh][:, b : b + 1], dfq)
                dfq_ref[pl.ds(r0, tb), :] = dfq
                return carry

            lax.fori_loop(0, nb, rows_dq, 0)

    seq = lambda w, col: pl.BlockSpec((S, w), lambda s, j: (s, col))
    blk = lambda w, col: pl.BlockSpec((tb, w), lambda s, j: (s * nb + j, col))
    return pl.pallas_call(
        body,
        name="attn_bwd",
        grid=(n_seq, nb),
        in_specs=[seq(ATTN_WIDTH, 0), blk(ATTN_WIDTH, 1), blk(ATTN_WIDTH, 2), seq(ATTN_WIDTH, 0), seq(ATTN_WIDTH, 0), seq(LANES, 0), seq(LANES, 0)],
        out_specs=[seq(ATTN_WIDTH, 0), blk(ATTN_WIDTH, 0), blk(ATTN_WIDTH, 0), blk(LANES, 0), seq(LANES, 0)],
        out_shape=[
            jax.ShapeDtypeStruct((T, ATTN_WIDTH), BF16),
            jax.ShapeDtypeStruct((T, ATTN_WIDTH), BF16),
            jax.ShapeDtypeStruct((T, ATTN_WIDTH), BF16),
            jax.ShapeDtypeStruct((T, LANES), F32),
            jax.ShapeDtypeStruct((T, LANES), F32),
        ],
        scratch_shapes=[
            pltpu.VMEM((N_HEADS, S, LANES), BF16),
            pltpu.VMEM((N_HEADS, S, LANES), BF16),
            pltpu.VMEM((N_HEADS, S, LANES), F32),
            pltpu.VMEM((N_HEADS, tb, LANES), BF16),
            pltpu.VMEM((N_HEADS, tb, LANES), BF16),
            pltpu.VMEM((N_HEADS, tb, LANES), F32),
            pltpu.VMEM((N_HEADS, tb, LANES), F32),
        ],
        compiler_params=_params(("parallel", "arbitrary")),
    )(qkv, qkv, qkv, da, a, fcol, lse)


def _forget_bwd(dfk, dfq, fl, b_pad, n_seq, S):
    def body(df_ref, dfq_ref, fl_ref, b_ref, dfl_ref, db_ref):
        t = (df_ref[...] + dfq_ref[...]).T
        lane = lax.broadcasted_iota(jnp.int32, t.shape, 1)
        k = 1
        while k < S:
            t = t + jnp.where(lane < S - k, pltpu.roll(t, S - k, 1), 0.0)
            k *= 2
        dfl = t.T * _sigmoid(-(fl_ref[...] + b_ref[...]))
        dfl_ref[...] = dfl.astype(BF16)

        @pl.when(pl.program_id(0) == 0)
        def _():
            db_ref[...] = jnp.zeros_like(db_ref)

        db_ref[...] += jnp.sum(dfl, axis=0, keepdims=True)

    return pl.pallas_call(
        body,
        name="forget_bwd",
        grid=(n_seq,),
        in_specs=[
            pl.BlockSpec((S, LANES), lambda s: (s, 0)),
            pl.BlockSpec((S, LANES), lambda s: (s, 0)),
            pl.BlockSpec((S, FL_PAD), lambda s: (s, 0)),
            _const_spec((1, FL_PAD)),
        ],
        out_specs=[pl.BlockSpec((S, FL_PAD), lambda s: (s, 0)), pl.BlockSpec((1, FL_PAD), lambda s: (0, 0))],
        out_shape=[jax.ShapeDtypeStruct((n_seq * S, FL_PAD), BF16), jax.ShapeDtypeStruct((1, FL_PAD), F32)],
        compiler_params=_params(("arbitrary",)),
    )(dfk, dfq, fl, b_pad)


def _in_proj_bwd(du, dq, dk, dv, dfl, dgates, x, dx1, g1, w_uqkv, w_fl, w_g, token):
    T = x.shape[0]
    tm = ROW_TILE

    def body(du_ref, dq_ref, dk_ref, dv_ref, dfl_ref, dgt_ref, x_ref, dx1_ref, g_ref, wa_ref, wf_ref, wg_ref, token_ref, dx_ref, dg_ref):
        dz = jnp.concatenate([du_ref[...], dq_ref[...], dk_ref[...], dv_ref[...]], axis=1)
        dh = _mm_nt(dz, wa_ref[...]) + _mm_nt(dgt_ref[...], wg_ref[...]) + _mm_nt(dfl_ref[...], wf_ref[...])
        gv = g_ref[...]
        _, xh, r = _rms_fwd(x_ref[...], gv)
        dxn, dgrow = _rms_bwd(dh, xh, r, gv)
        dx_ref[...] = dx1_ref[...] + dxn

        @pl.when(pl.program_id(0) == 0)
        def _():
            dg_ref[...] = jnp.zeros_like(dg_ref)

        dg_ref[...] += jnp.sum(dgrow, axis=0, keepdims=True)

    row = lambda n: pl.BlockSpec((tm, n), lambda i: (i, 0))
    return pl.pallas_call(
        body,
        name="in_proj_bwd",
        grid=(T // tm,),
        in_specs=[
            row(512), row(512), row(512), row(512), row(FL_PAD), row(2 * D_MODEL), row(D_MODEL), row(D_MODEL), _const_spec((1, D_MODEL)),
            _const_spec(w_uqkv.shape), _const_spec(w_fl.shape), _const_spec(w_g.shape), _HBM,
        ],
        out_specs=[row(D_MODEL), pl.BlockSpec((1, D_MODEL), lambda i: (0, 0))],
        out_shape=[jax.ShapeDtypeStruct((T, D_MODEL), F32), jax.ShapeDtypeStruct((1, D_MODEL), F32)],
        compiler_params=_params(("arbitrary",)),
    )(du, dq, dk, dv, dfl, dgates, x, dx1, g1, w_uqkv, w_fl, w_g, token)


def _pick_block(n):
    for b in (1024, 512, 1408, 256, 128):
        if n % b == 0:
            return b
    raise ValueError(n)


def _matmul_tn(a, b, name):
    T, K = a.shape
    N = b.shape[1]
    bt, bk, bn = min(T, DW_TOKENS), _pick_block(K), _pick_block(N)
    nt = T // bt

    def body(a_ref, b_ref, o_ref, acc):
        @pl.when(pl.program_id(2) == 0)
        def _():
            acc[...] = jnp.zeros_like(acc)

        acc[...] += _mm_tn(a_ref[...].astype(BF16), b_ref[...].astype(BF16))

        @pl.when(pl.program_id(2) == nt - 1)
        def _():
            o_ref[...] = acc[...].astype(BF16)

    return pl.pallas_call(
        body,
        name=name,
        grid=(K // bk, N // bn, nt),
        in_specs=[pl.BlockSpec((bt, bk), lambda k, n, t: (t, k)), pl.BlockSpec((bt, bn), lambda k, n, t: (t, n))],
        out_specs=pl.BlockSpec((bk, bn), lambda k, n, t: (k, n)),
        out_shape=jax.ShapeDtypeStruct((K, N), BF16),
        scratch_shapes=[pltpu.VMEM((bk, bn), F32)],
        compiler_params=_params(("parallel", "parallel", "arbitrary")),
    )(a, b)


W_IN_A = POOL_WIDTH + 3 * ATTN_WIDTH
W_IN_SHARD = (W_IN_A + N_HEADS + 2 * D_MODEL) // N_DEV
_W_IN_PIECES = ((0, W_IN_A), (W_IN_A, W_IN_A + N_HEADS), (W_IN_A + N_HEADS, W_IN_A + N_HEADS + 2 * D_MODEL))


def _w_in_segments(d):
    lo, hi = d * W_IN_SHARD, (d + 1) * W_IN_SHARD
    out = []
    for p, (a, b) in enumerate(_W_IN_PIECES):
        s, e = max(lo, a), min(hi, b)
        if s < e:
            out.append((p, s - a, s - lo, e - s))
    return out


def _w_in_pieces(gathered):
    tm = ROW_TILE // 2

    def body(g_ref, wa_ref, wf_ref, wg_ref):
        outs = (wa_ref, wf_ref, wg_ref)
        wf_ref[...] = jnp.zeros_like(wf_ref)
        for d in range(N_DEV):
            for p, at, frm, n in _w_in_segments(d):
                outs[p][:, at : at + n] = g_ref[d, :, frm : frm + n]

    return pl.pallas_call(
        body,
        name="w_in_pieces",
        grid=(D_MODEL // tm,),
        in_specs=[pl.BlockSpec((N_DEV, tm, W_IN_SHARD), lambda i: (0, i, 0))],
        out_specs=[pl.BlockSpec((tm, W_IN_A), lambda i: (i, 0)), pl.BlockSpec((tm, FL_PAD), lambda i: (i, 0)), pl.BlockSpec((tm, 2 * D_MODEL), lambda i: (i, 0))],
        out_shape=[
            jax.ShapeDtypeStruct((D_MODEL, W_IN_A), gathered.dtype),
            jax.ShapeDtypeStruct((D_MODEL, FL_PAD), gathered.dtype),
            jax.ShapeDtypeStruct((D_MODEL, 2 * D_MODEL), gathered.dtype),
        ],
        compiler_params=_params(("parallel",)),
    )(gathered)


def _dw_in(h, du, dq, dk, dv, dfl, dgates, token):
    T = h.shape[0]
    bt, bk = min(T, DW_TOKENS // 2), 512
    nt = T // bt
    pieces = (du, dq, dk, dv, dfl, dgates)
    offs = [0]
    for p in pieces:
        offs.append(offs[-1] + p.shape[1])

    def body(h_ref, *rest):
        refs, o_ref, acc = rest[: len(pieces)], rest[-2], rest[-1]

        @pl.when(pl.program_id(1) == 0)
        def _():
            acc[...] = jnp.zeros_like(acc)

        ht = h_ref[...].T
        for ref, at in zip(refs, offs):
            acc[:, at : at + ref.shape[1]] += _mm(ht, ref[...])

        @pl.when(pl.program_id(1) == nt - 1)
        def _():
            starts = (0, W_IN_A, W_IN_A + FL_PAD)
            for d in range(N_DEV):
                for p, at, to, n in _w_in_segments(d):
                    o_ref[d % 2, d // 2, :, to : to + n] = acc[:, starts[p] + at : starts[p] + at + n].astype(BF16)

    return pl.pallas_call(
        body,
        name="dw_in",
        grid=(D_MODEL // bk, nt),
        in_specs=[pl.BlockSpec((bt, bk), lambda k, t: (t, k))] + [pl.BlockSpec((bt, p.shape[1]), lambda k, t: (t, 0)) for p in pieces] + [_HBM],
        out_specs=pl.BlockSpec((2, 4, bk, W_IN_SHARD), lambda k, t: (0, 0, k, 0)),
        out_shape=jax.ShapeDtypeStruct((2, 4, D_MODEL, W_IN_SHARD), BF16),
        scratch_shapes=[pltpu.VMEM((bk, offs[-1]), F32)],
        compiler_params=_params(("parallel", "arbitrary")),
    )(h, *pieces, token)


def _position():
    return lax.axis_index("x"), lax.axis_index("y"), lax.axis_index("c")


_HBM = pl.BlockSpec(memory_space=pl.ANY)


def _all_gather(blocks, name):
    n = len(blocks)

    def body(*refs):
        xs, outs = refs[:n], refs[n : 2 * n]
        send_sems, recv_sems, local_sems = refs[2 * n :]
        x, y, c = _position()
        me, sibling = (x, y, c), (x, y, 1 - c)
        chips = [(1 - x, y), (x, 1 - y), (1 - x, 1 - y)]

        def rows(a, px, py, pc):
            return outs[a].at[4 * px + 2 * py + pc]

        def copy(a, k, blk, to, src=None):
            return pltpu.make_async_remote_copy(
                src_ref=rows(a, *blk) if src is None else src, dst_ref=rows(a, *blk),
                send_sem=send_sems.at[7 * a + k], recv_sem=recv_sems.at[7 * a + k], device_id=to, device_id_type=MESH,
            )

        mine = [pltpu.make_async_copy(xs[a], rows(a, *me), local_sems.at[a]) for a in range(n)]
        for cp in mine:
            cp.start()
        first = []
        for a in range(n):
            first.append(copy(a, 0, me, sibling, src=xs[a]))
            first += [copy(a, 1 + j, me, (*chip, c), src=xs[a]) for j, chip in enumerate(chips)]
        for cp in first:
            cp.start()
        passed = []
        for j, chip in enumerate(chips):
            for a in range(n):
                copy(a, 1 + j, (*chip, c), me).wait_recv()
                passed.append(copy(a, 4 + j, (*chip, c), sibling))
                passed[-1].start()
        for a in range(n):
            copy(a, 0, sibling, me).wait_recv()
        for j, chip in enumerate(chips):
            for a in range(n):
                copy(a, 4 + j, (*chip, 1 - c), me).wait_recv()
        for cp in first + passed:
            cp.wait_send()
        for cp in mine:
            cp.wait()

    return pl.pallas_call(
        body,
        name=name,
        out_shape=[jax.ShapeDtypeStruct((N_DEV, *b.shape), b.dtype) for b in blocks],
        in_specs=[_HBM] * n,
        out_specs=[_HBM] * n,
        scratch_shapes=[pltpu.SemaphoreType.DMA((7 * n,)), pltpu.SemaphoreType.DMA((7 * n,)), pltpu.SemaphoreType.DMA((n,))],
    )(*blocks)


_SEM = pl.BlockSpec(memory_space=pltpu.SEMAPHORE)
_HBM_ONLY = pl.BlockSpec(memory_space=pltpu.HBM)
_SIDE_EFFECT = pltpu.SideEffectType.DATAFLOW_SIDE_EFFECTING


def _peer(x, y, c, k):
    return (1 - x if k & 4 else x, 1 - y if k & 2 else y, 1 - c if k & 1 else c)


_PEER_BITS = {"gather": range(1, N_DEV), "scatter": range(1, N_DEV), "chips": (4, 2, 6)}
_LAND_SLOTS = {"gather": N_DEV, "scatter": N_DEV, "chips": 3}


def _exchange_copies(src_refs, land_refs, send_sems, recv_sems, pattern, receive_side):
    x, y, c = _position()
    me = 4 * x + 2 * y + c
    bits = _PEER_BITS[pattern]
    cps = []
    for j, k in enumerate(bits):
        px, py, pc = _peer(x, y, c, k)
        peer = 4 * px + 2 * py + pc
        for a, (src, land) in enumerate(zip(src_refs, land_refs)):
            if pattern == "chips":
                s, slot = src.at[2 * px + py], j
            else:
                s, slot = (src if pattern == "gather" else src.at[peer]), (peer if receive_side else me)
            cps.append(pltpu.make_async_remote_copy(
                src_ref=s, dst_ref=land.at[slot],
                send_sem=send_sems.at[len(bits) * a + j], recv_sem=recv_sems.at[len(bits) * a + j],
                device_id=(px, py, pc), device_id_type=MESH,
            ))
    return cps


def _exchange_start(srcs, after, name, pattern):
    n = len(srcs)
    m = len(_PEER_BITS[pattern])
    lands = [jax.ShapeDtypeStruct((_LAND_SLOTS[pattern], *s.shape[-2:]), s.dtype) for s in srcs]

    def body(*refs):
        src_refs, land_refs = refs[1 : 1 + n], refs[1 + n : 1 + 2 * n]
        send_sems, recv_sems = refs[1 + 2 * n], refs[2 + 2 * n]
        token = refs[-1]
        for cp in _exchange_copies(src_refs, land_refs, send_sems, recv_sems, pattern, receive_side=False):
            cp.start()
        token[...] = jnp.zeros_like(token)

    hbm = lambda t: pltpu.with_memory_space_constraint(t, pltpu.HBM)
    out = pl.pallas_call(
        body,
        name=name,
        out_shape=(
            pltpu.SemaphoreType.DMA((m * n,)), pltpu.SemaphoreType.DMA((m * n,)),
            *[pltpu.HBM(s.shape, s.dtype) for s in srcs], *[pltpu.HBM(l.shape, l.dtype) for l in lands],
            jax.ShapeDtypeStruct((8, LANES), F32),
        ),
        in_specs=(_HBM, *[_HBM_ONLY] * (2 * n)),
        out_specs=(_SEM, _SEM, *[_HBM_ONLY] * (2 * n), pl.BlockSpec(memory_space=pltpu.VMEM)),
        input_output_aliases={1 + i: 2 + i for i in range(2 * n)},
        compiler_params=pltpu.CompilerParams(has_side_effects=_SIDE_EFFECT),
    )(after, *[hbm(s) for s in srcs], *[hbm(lax.empty(l.shape, l.dtype)) for l in lands])
    return out[0], out[1], out[2 : 2 + n], out[2 + n : 2 + 2 * n], out[-1]


def _exchange_wait(send_sems, recv_sems, srcs, lands, after, name, pattern):
    n = len(srcs)

    def body(*refs):
        src_refs, land_refs = refs[:n], refs[n : 2 * n]
        for cp in _exchange_copies(src_refs, land_refs, refs[2 * n], refs[2 * n + 1], pattern, receive_side=True):
            cp.wait_send()
            cp.wait_recv()

    out = pl.pallas_call(
        body,
        name=name,
        out_shape=(*[pltpu.HBM(s.shape, s.dtype) for s in srcs], *[pltpu.HBM(l.shape, l.dtype) for l in lands]),
        in_specs=(*[_HBM_ONLY] * (2 * n), _SEM, _SEM, _HBM),
        out_specs=tuple([_HBM_ONLY] * (2 * n)),
        input_output_aliases={i: i for i in range(2 * n)},
        compiler_params=pltpu.CompilerParams(has_side_effects=_SIDE_EFFECT),
    )(*srcs, *lands, send_sems, recv_sems, after)
    return out[:n], out[n:]


def _sibling_exchange(sends):
    n = len(sends)

    def body(*refs):
        srcs, dsts = refs[:n], refs[n : 2 * n]
        send_sems, recv_sems = refs[2 * n :]
        x, y, c = _position()
        cps = [
            pltpu.make_async_remote_copy(
                src_ref=srcs[a].at[1 - c], dst_ref=dsts[a], send_sem=send_sems.at[a], recv_sem=recv_sems.at[a],
                device_id=(x, y, 1 - c), device_id_type=MESH,
            )
            for a in range(n)
        ]
        for cp in cps:
            cp.start()
        for cp in cps:
            cp.wait()

    return pl.pallas_call(
        body,
        name="rs_sibling",
        out_shape=[jax.ShapeDtypeStruct(s.shape[1:], s.dtype) for s in sends],
        in_specs=[_HBM] * n,
        out_specs=[_HBM] * n,
        scratch_shapes=[pltpu.SemaphoreType.DMA((n,)), pltpu.SemaphoreType.DMA((n,))],
    )(*sends)


def _rows_tile(r):
    return ROW_TILE if r % ROW_TILE == 0 else r


def _pair_sum(send, got, core, name):
    _, _, r, c = send.shape
    br = _rows_tile(r)

    def body(core_ref, a_ref, b_ref, o_ref):
        o_ref[...] = (a_ref[...].astype(F32) + b_ref[...].astype(F32)).astype(o_ref.dtype)

    return pl.pallas_call(
        body,
        name=name,
        grid_spec=pltpu.PrefetchScalarGridSpec(
            num_scalar_prefetch=1,
            grid=(4, r // br),
            in_specs=[
                pl.BlockSpec((None, None, br, c), lambda n, i, core: (core[0], n, i, 0)),
                pl.BlockSpec((None, br, c), lambda n, i, core: (n, i, 0)),
            ],
            out_specs=pl.BlockSpec((None, br, c), lambda n, i, core: (n, i, 0)),
        ),
        out_shape=jax.ShapeDtypeStruct((4, r, c), send.dtype),
        compiler_params=_params(("parallel", "parallel")),
    )(core, send, got)


def _adamw(w, g, m, v):
    m = ADAM_B1 * m + (1.0 - ADAM_B1) * g
    v = ADAM_B2 * v + (1.0 - ADAM_B2) * (g * g)
    m_hat = m / (1.0 - ADAM_B1 ** ADAM_STEP)
    v_hat = v / (1.0 - ADAM_B2 ** ADAM_STEP)
    delta = -ADAM_LR * (m_hat / (jnp.sqrt(v_hat) + ADAM_EPS) + ADAM_WD * w)
    return delta, m, v


def _shard_update(send, got, recv, w, m, v, pos, name):
    _, r, c = w.shape
    br = _rows_tile(r)

    def body(pos_ref, a_ref, b_ref, r_ref, w_ref, m_ref, v_ref, g_ref, d_ref, nm_ref, nv_ref):
        g = a_ref[...].astype(F32) + b_ref[...].astype(F32)
        for n in range(3):
            g = g + r_ref[n].astype(F32)
        g_ref[...] = g
        d_ref[...], nm_ref[...], nv_ref[...] = _adamw(w_ref[...], g, m_ref[...], v_ref[...])

    own = pl.BlockSpec((None, br, c), lambda i, pos: (0, i, 0))
    return pl.pallas_call(
        body,
        name=name,
        grid_spec=pltpu.PrefetchScalarGridSpec(
            num_scalar_prefetch=1,
            grid=(r // br,),
            in_specs=[
                pl.BlockSpec((None, None, br, c), lambda i, pos: (pos[0], pos[1], i, 0)),
                pl.BlockSpec((None, br, c), lambda i, pos: (pos[1], i, 0)),
                pl.BlockSpec((3, br, c), lambda i, pos: (0, i, 0)),
                own, own, own,
            ],
            out_specs=[own, own, own, own],
        ),
        out_shape=[jax.ShapeDtypeStruct((1, r, c), F32)] * 4,
        compiler_params=_params(("parallel",)),
    )(pos, send, got, recv, w, m, v)


def _shard_update_direct(parts, chunks, w, m, v, me, name):
    _, r, c = w.shape
    br = _rows_tile(r)

    def body(me_ref, p_ref, own_ref, w_ref, m_ref, v_ref, g_ref, d_ref, nm_ref, nv_ref):
        g = None
        for n in range(N_DEV):
            part = jnp.where(me_ref[0] == n, own_ref[...], p_ref[n]).astype(F32)
            g = part if g is None else g + part
        g_ref[...] = g
        d_ref[...], nm_ref[...], nv_ref[...] = _adamw(w_ref[...], g, m_ref[...], v_ref[...])

    shard = pl.BlockSpec((None, br, c), lambda i, me: (0, i, 0))
    return pl.pallas_call(
        body,
        name=name,
        grid_spec=pltpu.PrefetchScalarGridSpec(
            num_scalar_prefetch=1,
            grid=(r // br,),
            in_specs=[
                pl.BlockSpec((N_DEV, br, c), lambda i, me: (0, i, 0)),
                pl.BlockSpec((None, br, c), lambda i, me: (me[0], i, 0)),
                shard, shard, shard,
            ],
            out_specs=[shard, shard, shard, shard],
        ),
        out_shape=[jax.ShapeDtypeStruct((1, r, c), F32)] * 4,
        compiler_params=_params(("parallel",)),
    )(me, parts, chunks, w, m, v)


def _small_update(parts, first_rows, w, m, v):
    R = w.shape[0]

    def body(p_ref, f_ref, w_ref, m_ref, v_ref, g_ref, d_ref, nm_ref, nv_ref):
        g, first = p_ref[0], f_ref[0]
        for n in range(1, N_DEV):
            g = g + p_ref[n]
            first = first + f_ref[n]
        g = jnp.concatenate([g[:8] + first, g[8:]], axis=0)
        g_ref[...] = g
        d_ref[...], nm_ref[...], nv_ref[...] = _adamw(w_ref[...], g, m_ref[...], v_ref[...])

    return pl.pallas_call(
        body,
        name="small_update",
        out_shape=[jax.ShapeDtypeStruct((R, LANES), F32)] * 4,
        compiler_params=pltpu.CompilerParams(vmem_limit_bytes=VMEM_LIMIT),
    )(parts, first_rows, w, m, v)


_SHARD_AXIS = (1, 1, 1, 0, 0, 0, 0)
_TRANSPOSED = (False, False, False, False, True, True, False)


def _full_from_gathered(t, axis):
    if axis == 0:
        return t.reshape(N_DEV * t.shape[1], t.shape[2])
    return jnp.concatenate([t[d] for d in range(N_DEV)], axis=1)


def _chunks_from_cols(t):
    c = t.shape[1] // N_DEV
    return jnp.stack([t[:, d * c : (d + 1) * c] for d in range(N_DEV)])


_SMALL = (("norm1_g", 8), ("norm2_g", 8), ("norm_f_g", 8), ("b_forget", 8), ("pool_scale", 8), ("pool_mix", 512))


def _pack_small(vals, loss_row):
    parts = []
    for (name, rows), t in zip(_SMALL, vals):
        f = t.astype(F32).reshape(-1)
        f = jnp.concatenate([f, jnp.zeros((rows * LANES - f.shape[0],), F32)]).reshape(rows, LANES)
        parts.append(f)
    parts.append(loss_row)
    return jnp.concatenate(parts, axis=0)


def _unpack_small(packed, shapes):
    out, off = [], 0
    for (name, rows), shape in zip(_SMALL, shapes):
        n = 1
        for s in shape:
            n *= s
        out.append(packed[off : off + rows].reshape(-1)[:n].reshape(shape))
        off += rows
    return out, packed[off, 0]


def _local_grads(x, tgt, g1, g2, gf, b_forget, pool_mix, pool_scale, w_in, fwd_token, out_weights, ffn_weights, ffn_grads_out, out_grads_out, small_grads_out, in_grads_out, norm1_grad_out):
    n_seq, S, _ = x.shape
    T = n_seq * S
    x2 = x.reshape(T, D_MODEL)
    tg2 = tgt.reshape(T, D_MODEL)
    w_uqkv, w_fl, w_g = w_in
    b_pad = jnp.concatenate([b_forget.reshape(1, N_HEADS), jnp.zeros((1, FL_PAD - N_HEADS), F32)], axis=1)
    mix_b = pool_mix.reshape(len(POOL_WINDOWS), GROUP_DIM, GROUP_DIM).astype(BF16)
    scale = pool_scale.reshape(1, POOL_WIDTH)
    g1 = g1.reshape(1, D_MODEL)
    g2 = g2.reshape(1, D_MODEL)
    gf = gf.reshape(1, D_MODEL)

    h, u, qkv, fl, gates = _in_proj(x2, g1, w_uqkv, w_fl, w_g, fwd_token)
    fcol = _forget_fwd(fl, b_pad, n_seq, S)
    pm, p2, p3 = _pool_fwd(u, mix_b, scale, n_seq, S)
    a, lse = _attn_fwd(qkv, fcol, n_seq, S)
    w_po, w_ao, w_out = out_weights(a)
    merged, x1, attn_y, pool_y = _mix_out(a, p3, gates, x2, w_ao, w_po, w_out)
    w_gate_t, w_up_t, w_down = ffn_weights(x1)
    h2, gate, up, act, dx2, loss_rows, dgf = _ffn_fwd(x1, g2, gf, tg2, w_gate_t, w_up_t, w_down)

    dgate, dup, dx1, dg2 = _ffn_bwd(dx2, gate, up, x1, g2, w_gate_t, w_up_t, w_down)
    bwd_token = ffn_grads_out(_matmul_tn(dgate, h2, "dw_ffn_gate"), _matmul_tn(dup, h2, "dw_ffn_up"), _matmul_tn(act, dx2, "dw_ffn_down"))
    dgates, dpy, day, da, dp2, dscale = _mix_bwd(dx1, gates, pool_y, attn_y, p2, scale, w_out, w_ao, w_po, bwd_token)
    out_token = out_grads_out(_matmul_tn(p3, dpy, "dw_pool_out"), _matmul_tn(a, day, "dw_attn_out"), _matmul_tn(merged, dx1, "dw_out"))
    du, dmix = _pool_bwd(dp2, pm, mix_b, out_token, n_seq, S)
    dq, dk, dv, dfk, dfq = _attn_bwd(qkv, da, a, fcol, lse, n_seq, S)
    dfl, db = _forget_bwd(dfk, dfq, fl, b_pad, n_seq, S)
    small_token = small_grads_out((jnp.zeros_like(g1), dg2, dgf, db[:, :N_HEADS], dscale, dmix), loss_rows)
    in_token = in_grads_out(_dw_in(h, du, dq, dk, dv, dfl, dgates, small_token))
    dx, dg1 = _in_proj_bwd(du, dq, dk, dv, dfl, dgates, x2, dx1, g1, w_uqkv, w_fl, w_g, in_token)
    norm1_grad_out(dg1)
    return dx.reshape(n_seq, S, D_MODEL)


def kernel(x, norm1_g, w_in, b_forget, pool_mix, pool_scale, w_pool_out, w_attn_out, w_out, norm2_g, w_ffn_gate, w_ffn_up, w_ffn_down, norm_f_g, loss_target, m_norm1_g, m_w_in, m_b_forget, m_pool_mix, m_pool_scale, m_w_pool_out, m_w_attn_out, m_w_out, m_norm2_g, m_w_ffn_gate, m_w_ffn_up, m_w_ffn_down, m_norm_f_g, v_norm1_g, v_w_in, v_b_forget, v_pool_mix, v_pool_scale, v_w_pool_out, v_w_attn_out, v_w_out, v_norm2_g, v_w_ffn_gate, v_w_ffn_up, v_w_ffn_down, v_norm_f_g):
    names = ("w_in", "w_pool_out", "w_attn_out", "w_out", "w_ffn_gate", "w_ffn_up", "w_ffn_down")
    w_sh = (w_in, w_pool_out, w_attn_out, w_out, w_ffn_gate, w_ffn_up, w_ffn_down)
    m_sh = (m_w_in, m_w_pool_out, m_w_attn_out, m_w_out, m_w_ffn_gate, m_w_ffn_up, m_w_ffn_down)
    v_sh = (v_w_in, v_w_pool_out, v_w_attn_out, v_w_out, v_w_ffn_gate, v_w_ffn_up, v_w_ffn_down)

    cx, cy, cc = _position()
    me = 4 * cx + 2 * cy + cc
    def stored(t, transposed):
        return jnp.transpose(t, (0, 2, 1)) if transposed else t

    w_sh, m_sh, v_sh = ([stored(t, tr) for t, tr in zip(ts, _TRANSPOSED)] for ts in (w_sh, m_sh, v_sh))
    shards = [w[0].astype(BF16) for w in w_sh]
    (gathered_in,) = _all_gather(shards[:1], "w_in_all_gather")
    out_sems = _exchange_start(shards[1:4], gathered_in, "out_weights_gather_start", "gather")
    ffn_sems = _exchange_start(shards[4:], out_sems[4], "ffn_weights_gather_start", "gather")
    no_order = jnp.zeros((8, LANES), F32)

    def with_own(lands, own):
        return [lax.dynamic_update_slice(l, o[None], (me, 0, 0)) for l, o in zip(lands, own)]

    def gathered_weights(sems, axes, name):
        def wait(after):
            send_sems, recv_sems, srcs, lands, _ = sems
            srcs, lands = _exchange_wait(send_sems, recv_sems, srcs, lands, after, name, "gather")
            return [_full_from_gathered(t, axis) for t, axis in zip(with_own(lands, srcs), axes)]

        return wait

    started = {}

    def scatter_grads(key, name):
        def start(*whole_grads):
            chunks = [
                _chunks_from_cols(t) if axis == 1 else t.reshape(N_DEV, -1, t.shape[1])
                for t, axis in zip(whole_grads, _SHARD_AXIS[key])
            ]
            started[key] = _exchange_start(chunks, no_order, name, "scatter")
            return started[key][4]

        return start

    def gather_small(small, loss_rows):
        started["small"] = _exchange_start([_pack_small(small, loss_rows)], no_order, "small_grads_gather_start", "gather")
        return started["small"][4]

    core = jnp.reshape(cc, (1,)).astype(jnp.int32)
    pos = jnp.stack([cc, 2 * cx + cy]).astype(jnp.int32)

    def reduce_w_in(send_in):
        (got_in,) = _sibling_exchange([send_in])
        pair_in = _pair_sum(send_in, got_in, core, "pair_sum_w_in")
        started["in"] = (send_in, got_in, _exchange_start([pair_in], no_order, "w_in_grads_chips_start", "chips"))
        return started["in"][2][4]

    def gather_norm1(dg1):
        rows = jnp.reshape(dg1, (8, LANES))
        started["norm1"] = _exchange_start([rows], no_order, "norm1_grad_gather_start", "gather")

    ffn, out = slice(4, 7), slice(1, 4)
    grad_x = _local_grads(
        x, loss_target, norm1_g, norm2_g, norm_f_g, b_forget, pool_mix, pool_scale, _w_in_pieces(gathered_in), ffn_sems[4],
        gathered_weights(out_sems, _SHARD_AXIS[out], "out_weights_gather_wait"),
        gathered_weights(ffn_sems, _SHARD_AXIS[ffn], "ffn_weights_gather_wait"),
        scatter_grads(ffn, "ffn_grads_scatter_start"), scatter_grads(out, "out_grads_scatter_start"), gather_small, reduce_w_in, gather_norm1,
    )
    send_in, got_in, chip_sems = started["in"]

    def scattered_updates(key, after, name):
        send_sems, recv_sems, srcs, lands, _ = started[key]
        srcs, lands = _exchange_wait(send_sems, recv_sems, srcs, lands, after, name, "scatter")
        return [
            _shard_update_direct(p, s, w, m, v, jnp.reshape(me, (1,)).astype(jnp.int32), "update_" + n)
            for p, s, w, m, v, n in zip(lands, srcs, w_sh[key], m_sh[key], v_sh[key], names[key])
        ]

    updates_out = scattered_updates(out, grad_x, "out_grads_scatter_wait")
    updates_ffn = scattered_updates(ffn, grad_x, "ffn_grads_scatter_wait")

    small_w = (norm1_g, norm2_g, norm_f_g, b_forget, pool_scale, pool_mix)
    small_m = (m_norm1_g, m_norm2_g, m_norm_f_g, m_b_forget, m_pool_scale, m_pool_mix)
    small_v = (v_norm1_g, v_norm2_g, v_norm_f_g, v_b_forget, v_pool_scale, v_pool_mix)
    zero_row = jnp.zeros((8, LANES), F32)
    send_sems, recv_sems, srcs, lands, _ = chip_sems
    _, (recv_in,) = _exchange_wait(send_sems, recv_sems, srcs, lands, updates_ffn[-1][0], "w_in_grads_chips_wait", "chips")
    update_in = _shard_update(send_in, got_in, recv_in, w_in, m_w_in, v_w_in, pos, "update_w_in")

    def gathered_small(key, after, name):
        send_sems, recv_sems, srcs, lands, _ = started[key]
        srcs, lands = _exchange_wait(send_sems, recv_sems, srcs, lands, after, name, "gather")
        return with_own(lands, srcs)[0]

    parts = gathered_small("small", update_in[0], "small_grads_gather_wait")
    first_rows = gathered_small("norm1", parts, "norm1_grad_gather_wait")
    g_s, d_s, nm_s, nv_s = _small_update(parts, first_rows, _pack_small(small_w, zero_row), _pack_small(small_m, zero_row), _pack_small(small_v, zero_row))
    g_w, d_w, nm_w, nv_w = zip(*(
        [stored(t, tr) for t in u] for u, tr in zip([update_in] + updates_out + updates_ffn, _TRANSPOSED)
    ))
    shapes = [t.shape for t in small_w]
    (g1, g2, gf, gb, gsc, gmix), loss = _unpack_small(g_s, shapes)
    (d1, d2, df, db_, dsc, dmx), _ = _unpack_small(d_s, shapes)
    (m1, m2, mf, mb, msc, mmx), _ = _unpack_small(nm_s, shapes)
    (v1, v2, vf, vb, vsc, vmx), _ = _unpack_small(nv_s, shapes)

    def ordered(n1, win, b, mix, sc, wpo, wao, wout, n2, wg, wu, wd, nf):
        return (n1, win, b, mix, sc, wpo, wao, wout, n2, wg, wu, wd, nf)

    grads = ordered(g1, g_w[0], gb, gmix, gsc, g_w[1], g_w[2], g_w[3], g2, g_w[4], g_w[5], g_w[6], gf)
    deltas = ordered(d1, d_w[0], db_, dmx, dsc, d_w[1], d_w[2], d_w[3], d2, d_w[4], d_w[5], d_w[6], df)
    new_m = ordered(m1, nm_w[0], mb, mmx, msc, nm_w[1], nm_w[2], nm_w[3], m2, nm_w[4], nm_w[5], nm_w[6], mf)
    new_v = ordered(v1, nv_w[0], vb, vmx, vsc, nv_w[1], nv_w[2], nv_w[3], v2, nv_w[4], nv_w[5], nv_w[6], vf)
    return (loss, grad_x, *grads, *deltas, *new_m, *new_v)
```

```python
import jax
import jax.numpy as jnp
from jax import lax
from jax.experimental import pallas as pl
from jax.experimental.pallas import tpu as pltpu

F32 = jnp.float32
BF16 = jnp.bfloat16
MESH = pl.DeviceIdType.MESH

D_MODEL = 1024
POOL_WINDOWS = (2, 4, 8, 16)
POOL_WIDTH = 512
GROUP_DIM = 128
ATTN_WIDTH = 512
HEAD_DIM = 64
N_HEADS = 8
N_PAIRS = 4
D_FF = 2816
RMS_EPS = 1e-6
N_DEV = 8
LANES = 128
FL_PAD = 128

ADAM_LR = 0.001
ADAM_B1 = 0.9
ADAM_B2 = 0.999
ADAM_EPS = 1e-08
ADAM_WD = 0.01
ADAM_STEP = 10

VMEM_LIMIT = 56 * 1024 * 1024
VMEM_LIMIT_MAX = 60 * 1024 * 1024
ROW_TILE = 512
ATTN_BLOCK = 512
FF_CHUNK = 256
FF_ROW_TILE = 512
DW_TOKENS = 2048


def _mm(a, b):
    return jnp.dot(a, b, preferred_element_type=F32)


def _mm_nt(a, b):
    return lax.dot_general(a, b, (((1,), (1,)), ((), ())), preferred_element_type=F32)


def _mm_tn(a, b):
    return lax.dot_general(a, b, (((0,), (0,)), ((), ())), preferred_element_type=F32)


def _sigmoid(x):
    return 1.0 / (1.0 + jnp.exp(-x))


def _params(sem, vmem=VMEM_LIMIT):
    return pltpu.CompilerParams(dimension_semantics=sem, vmem_limit_bytes=vmem)


def _const_spec(shape):
    nd = len(shape)
    return pl.BlockSpec(shape, lambda *_: (0,) * nd, pipeline_mode=pl.Buffered(1))


def _rms_fwd(x, g):
    r = lax.rsqrt(jnp.mean(x * x, axis=-1, keepdims=True) + RMS_EPS)
    xh = x * r
    return xh * g, xh, r


def _rms_bwd(dy, xh, r, g):
    dxh = dy * g
    dx = r * (dxh - xh * jnp.mean(dxh * xh, axis=-1, keepdims=True))
    return dx, dy * xh


def _in_proj(x, g1, w_uqkv, w_fl, w_g, token):
    T = x.shape[0]
    tm = ROW_TILE

    def body(x_ref, g_ref, wa_ref, wf_ref, wg_ref, token_ref, h_ref, u_ref, qkv_ref, fl_ref, gt_ref):
        h, _, _ = _rms_fwd(x_ref[...], g_ref[...])
        hb = h.astype(BF16)
        h_ref[...] = hb
        z = _mm(hb, wa_ref[...])
        u_ref[...] = z[:, :POOL_WIDTH]
        qkv_ref[...] = z[:, POOL_WIDTH:].astype(BF16)
        fl_ref[...] = _mm(hb, wf_ref[...])
        gt_ref[...] = _mm(hb, wg_ref[...]).astype(BF16)

    row = lambda n: pl.BlockSpec((tm, n), lambda i: (i, 0))
    return pl.pallas_call(
        body,
        name="in_proj",
        grid=(T // tm,),
        in_specs=[row(D_MODEL), _const_spec((1, D_MODEL)), _const_spec(w_uqkv.shape), _const_spec(w_fl.shape), _const_spec(w_g.shape), _HBM],
        out_specs=[row(D_MODEL), row(POOL_WIDTH), row(3 * ATTN_WIDTH), row(FL_PAD), row(2 * D_MODEL)],
        out_shape=[
            jax.ShapeDtypeStruct((T, D_MODEL), BF16),
            jax.ShapeDtypeStruct((T, POOL_WIDTH), F32),
            jax.ShapeDtypeStruct((T, 3 * ATTN_WIDTH), BF16),
            jax.ShapeDtypeStruct((T, FL_PAD), F32),
            jax.ShapeDtypeStruct((T, 2 * D_MODEL), BF16),
        ],
        compiler_params=_params(("parallel",)),
    )(x, g1, w_uqkv, w_fl, w_g, token)


def _log_sigmoid(x):
    return jnp.minimum(x, 0.0) - jnp.log(1.0 + jnp.exp(-jnp.abs(x)))


def _forget_fwd(fl, b_pad, n_seq, S):
    def body(fl_ref, b_ref, fcol_ref):
        lf = _log_sigmoid(fl_ref[...] + b_ref[...])
        t = lf.T
        lane = lax.broadcasted_iota(jnp.int32, t.shape, 1)
        k = 1
        while k < S:
            t = t + jnp.where(lane >= k, pltpu.roll(t, k, 1), 0.0)
            k *= 2
        fcol_ref[...] = t.T

    return pl.pallas_call(
        body,
        name="forget_fwd",
        grid=(n_seq,),
        in_specs=[pl.BlockSpec((S, FL_PAD), lambda s: (s, 0)), _const_spec((1, FL_PAD))],
        out_specs=pl.BlockSpec((S, FL_PAD), lambda s: (s, 0)),
        out_shape=jax.ShapeDtypeStruct((n_seq * S, FL_PAD), F32),
        compiler_params=_params(("parallel",)),
    )(fl, b_pad)


def _window_pick(g, v2, v4, v8, v16):
    return jnp.where(g == 0, v2, jnp.where(g == 1, v4, jnp.where(g == 2, v8, v16)))


def _pool_fwd(u, mix_b, scale, n_seq, S):
    T = n_seq * S

    def body(u_ref, mix_ref, sc_ref, pm_ref, p2_ref, p3_ref):
        g = pl.program_id(1)
        uu = u_ref[...]
        row = lax.broadcasted_iota(jnp.int32, uu.shape, 0)

        def back(a, k):
            return jnp.where(row >= k, pltpu.roll(a, k, 0), 0.0)

        s2 = uu + back(uu, 1)
        s4 = s2 + back(s2, 2)
        s8 = s4 + back(s4, 4)
        s16 = s8 + back(s8, 8)
        w = _window_pick(g, 2.0, 4.0, 8.0, 16.0)
        cnt = jnp.minimum((row + 1).astype(F32), w)
        pm = _window_pick(g, s2, s4, s8, s16) / cnt - uu
        pmb = pm.astype(BF16)
        pm_ref[...] = pmb
        p2 = _mm(pmb, mix_ref[...])
        p2_ref[...] = p2
        p3_ref[...] = (p2 * sc_ref[...]).astype(BF16)

    grp = pl.BlockSpec((S, GROUP_DIM), lambda s, g: (s, g))
    return pl.pallas_call(
        body,
        name="pool_fwd",
        grid=(n_seq, len(POOL_WINDOWS)),
        in_specs=[
            grp,
            pl.BlockSpec((None, GROUP_DIM, GROUP_DIM), lambda s, g: (g, 0, 0)),
            pl.BlockSpec((1, GROUP_DIM), lambda s, g: (0, g)),
        ],
        out_specs=[grp, grp, grp],
        out_shape=[
            jax.ShapeDtypeStruct((T, POOL_WIDTH), BF16),
            jax.ShapeDtypeStruct((T, POOL_WIDTH), F32),
            jax.ShapeDtypeStruct((T, POOL_WIDTH), BF16),
        ],
        compiler_params=_params(("parallel", "parallel")),
    )(u, mix_b, scale)


def _split3(v):
    hi = v.astype(BF16).astype(F32)
    r = v - hi
    mid = r.astype(BF16).astype(F32)
    lo = (r - mid).astype(BF16).astype(F32)
    return hi, mid, lo


def _bias_lanes(v):
    hi, mid, lo = _split3(v)
    lane = lax.broadcasted_iota(jnp.int32, (1, LANES), 1)
    packed = jnp.where(lane < N_HEADS, hi, jnp.where(lane < 2 * N_HEADS, pltpu.roll(mid, N_HEADS, 1), pltpu.roll(lo, 2 * N_HEADS, 1)))
    return jnp.where(lane < 3 * N_HEADS, packed, 0.0).astype(BF16)


def _bias_placement(slot):
    row = lax.broadcasted_iota(jnp.int32, (LANES, N_HEADS * LANES), 0)
    col = lax.broadcasted_iota(jnp.int32, (LANES, N_HEADS * LANES), 1)
    h = col // LANES
    n = col % LANES - jnp.where(h % 2 == 0, HEAD_DIM, 0) - 3 * slot
    return ((n >= 0) & (n < 3) & (row == N_HEADS * n + h)).astype(BF16)


def _augment(xp, h, bias, ones_slot):
    lane = lax.broadcasted_iota(jnp.int32, (1, LANES), 1)
    hh = h % 2
    head = (lane >= HEAD_DIM * hh) & (lane < HEAD_DIM * (hh + 1))
    b = HEAD_DIM * (1 - hh)
    rest = jnp.zeros_like(xp) if bias is None else bias[:, h * LANES : (h + 1) * LANES]
    out = jnp.where(head, xp, rest)
    if ones_slot is not None:
        out = jnp.where((lane >= b + 3 * ones_slot) & (lane < b + 3 * ones_slot + 3), jnp.ones_like(xp), out)
    return out


def _attn_fwd(qkv, fcol, n_seq, S):
    T = n_seq * S
    tb = ATTN_BLOCK
    nq = S // tb
    scale = HEAD_DIM ** -0.5

    def body(q_ref, k_ref, v_ref, fc_ref, o_ref, st_ref, qa_sc, ka_sc, m_sc, l_sc, acc_sc):
        i = pl.program_id(1)
        lane = lax.broadcasted_iota(jnp.int32, (1, LANES), 1)
        low = lane < HEAD_DIM

        @pl.when(i == 0)
        def _():
            place = _bias_placement(1)

            def rows_ka(r, carry):
                r0 = pl.multiple_of(r * tb, tb)
                bias = _mm(_bias_lanes(-fc_ref[pl.ds(r0, tb), :]), place).astype(BF16)
                for h in range(N_HEADS):
                    kp = k_ref[pl.ds(r0, tb), (h // 2) * LANES : (h // 2 + 1) * LANES] * scale
                    ka_sc[h, pl.ds(r0, tb), :] = _augment(kp, h, bias, 0)
                return carry

            lax.fori_loop(0, nq, rows_ka, 0)

        q0 = pl.multiple_of(i * tb, tb)
        bias = _mm(_bias_lanes(fc_ref[pl.ds(q0, tb), :]), _bias_placement(0)).astype(BF16)
        for h in range(N_HEADS):
            qa_sc[h] = _augment(q_ref[:, (h // 2) * LANES : (h // 2 + 1) * LANES], h, bias, 1)
        m_sc[...] = jnp.full(m_sc.shape, -jnp.inf, F32)
        l_sc[...] = jnp.zeros_like(l_sc)
        acc_sc[...] = jnp.zeros_like(acc_sc)
        causal = lax.broadcasted_iota(jnp.int32, (tb, tb), 1) <= lax.broadcasted_iota(jnp.int32, (tb, tb), 0)

        def step(j, masked):
            c0 = pl.multiple_of(j * tb, tb)
            for p in range(N_PAIRS):
                vb = v_ref[pl.ds(c0, tb), p * LANES : (p + 1) * LANES]
                pv, al = [], []
                for hh in range(2):
                    h = 2 * p + hh
                    s = _mm_nt(qa_sc[h], ka_sc[h, pl.ds(c0, tb), :])
                    if masked:
                        s = jnp.where(causal, s, -jnp.inf)
                    m_old = m_sc[h]
                    m_new = jnp.maximum(m_old, jnp.max(s, axis=1, keepdims=True))
                    alpha = jnp.exp(m_old - m_new)
                    pe = jnp.exp(s - jnp.concatenate([m_new] * (tb // LANES), axis=1))
                    l_sc[h] = alpha * l_sc[h] + jnp.sum(pe, axis=1, keepdims=True)
                    m_sc[h] = m_new
                    pv.append(_mm(pe.astype(BF16), vb))
                    al.append(alpha)
                acc_sc[p] = jnp.where(low, al[0], al[1]) * acc_sc[p] + jnp.where(low, pv[0], pv[1])

        def loop_body(j, carry):
            step(j, False)
            return carry

        lax.fori_loop(0, i, loop_body, 0)
        step(i, True)
        st = jnp.zeros((tb, LANES), F32)
        for p in range(N_PAIRS):
            lp = jnp.where(low, l_sc[2 * p], l_sc[2 * p + 1])
            o_ref[:, p * LANES : (p + 1) * LANES] = (acc_sc[p] / lp).astype(BF16)
            for h in (2 * p, 2 * p + 1):
                st = jnp.where(lane == h, m_sc[h] + jnp.log(l_sc[h]), st)
        st_ref[...] = st

    return pl.pallas_call(
        body,
        name="attn_fwd",
        grid=(n_seq, nq),
        in_specs=[
            pl.BlockSpec((tb, ATTN_WIDTH), lambda s, i: (s * nq + i, 0)),
            pl.BlockSpec((S, ATTN_WIDTH), lambda s, i: (s, 1)),
            pl.BlockSpec((S, ATTN_WIDTH), lambda s, i: (s, 2)),
            pl.BlockSpec((S, LANES), lambda s, i: (s, 0)),
        ],
        out_specs=[
            pl.BlockSpec((tb, ATTN_WIDTH), lambda s, i: (s * nq + i, 0)),
            pl.BlockSpec((tb, LANES), lambda s, i: (s * nq + i, 0)),
        ],
        out_shape=[jax.ShapeDtypeStruct((T, ATTN_WIDTH), BF16), jax.ShapeDtypeStruct((T, LANES), F32)],
        scratch_shapes=[
            pltpu.VMEM((N_HEADS, tb, LANES), BF16),
            pltpu.VMEM((N_HEADS, S, LANES), BF16),
            pltpu.VMEM((N_HEADS, tb, LANES), F32),
            pltpu.VMEM((N_HEADS, tb, LANES), F32),
            pltpu.VMEM((N_PAIRS, tb, LANES), F32),
        ],
        compiler_params=_params(("parallel", "arbitrary")),
    )(qkv, qkv, qkv, fcol)


def _mix_out(a, p3, gates, x, w_ao, w_po, w_out):
    T = x.shape[0]
    tm = ROW_TILE

    def body(a_ref, p3_ref, gt_ref, x_ref, wao_ref, wpo_ref, wout_ref, mg_ref, x1_ref, ay_ref, py_ref):
        ay = _mm(a_ref[...], wao_ref[...])
        py = _mm(p3_ref[...], wpo_ref[...])
        ay_ref[...] = ay.astype(BF16)
        py_ref[...] = py.astype(BF16)
        sp = _sigmoid(gt_ref[:, :D_MODEL].astype(F32))
        sa = _sigmoid(gt_ref[:, D_MODEL:].astype(F32))
        mb = (sp * py + sa * ay).astype(BF16)
        mg_ref[...] = mb
        x1_ref[...] = x_ref[...] + _mm(mb, wout_ref[...])

    row = lambda n: pl.BlockSpec((tm, n), lambda i: (i, 0))
    return pl.pallas_call(
        body,
        name="mix_out",
        grid=(T // tm,),
        in_specs=[
            row(ATTN_WIDTH), row(POOL_WIDTH), row(2 * D_MODEL), row(D_MODEL),
            _const_spec(w_ao.shape), _const_spec(w_po.shape), _const_spec(w_out.shape),
        ],
        out_specs=[row(D_MODEL), row(D_MODEL), row(D_MODEL), row(D_MODEL)],
        out_shape=[
            jax.ShapeDtypeStruct((T, D_MODEL), BF16), jax.ShapeDtypeStruct((T, D_MODEL), F32),
            jax.ShapeDtypeStruct((T, D_MODEL), BF16), jax.ShapeDtypeStruct((T, D_MODEL), BF16),
        ],
        compiler_params=_params(("parallel",)),
    )(a, p3, gates, x, w_ao, w_po, w_out)


def _ffn_fwd(x1, g2, gf, tgt, w_gate_t, w_up_t, w_down):
    T = x1.shape[0]
    tm = min(T, FF_ROW_TILE)
    nt = T // tm
    nc = D_FF // FF_CHUNK

    def body(x1_ref, g2_ref, gf_ref, tg_ref, wg_ref, wu_ref, wd_ref, h2_ref, gate_ref, up_ref, act_ref, dx2_ref, loss_ref, dgf_ref):
        x1v = x1_ref[...]
        h2, _, _ = _rms_fwd(x1v, g2_ref[...])
        h2b = h2.astype(BF16)
        h2_ref[...] = h2b
        for c in range(nc):
            sl = slice(c * FF_CHUNK, (c + 1) * FF_CHUNK)
            gate = _mm_nt(h2b, wg_ref[sl, :])
            up = _mm_nt(h2b, wu_ref[sl, :])
            gate_ref[:, sl] = gate.astype(BF16)
            up_ref[:, sl] = up.astype(BF16)
            act_ref[:, sl] = (gate * _sigmoid(gate) * up).astype(BF16)
        acc = x1v + _mm(act_ref[...], wd_ref[...])
        gfv = gf_ref[...]
        y, xh, r = _rms_fwd(acc, gfv)
        err = y - tg_ref[...]
        part = 0.5 * jnp.sum(jnp.mean(err * err, axis=-1, keepdims=True), axis=0, keepdims=True)
        dx2, dgrow = _rms_bwd(err * (1.0 / D_MODEL), xh, r, gfv)
        dx2_ref[...] = dx2

        @pl.when(pl.program_id(0) == 0)
        def _():
            dgf_ref[...] = jnp.zeros_like(dgf_ref)
            loss_ref[...] = jnp.zeros_like(loss_ref)

        dgf_ref[...] += jnp.sum(dgrow, axis=0, keepdims=True)
        loss_ref[...] += jnp.broadcast_to(part, loss_ref.shape)

    row = lambda n: pl.BlockSpec((tm, n), lambda i: (i, 0))
    return pl.pallas_call(
        body,
        name="ffn_fwd",
        grid=(nt,),
        in_specs=[
            row(D_MODEL), _const_spec((1, D_MODEL)), _const_spec((1, D_MODEL)), row(D_MODEL),
            _const_spec(w_gate_t.shape), _const_spec(w_up_t.shape), _const_spec(w_down.shape),
        ],
        out_specs=[
            row(D_MODEL), row(D_FF), row(D_FF), row(D_FF), row(D_MODEL),
            pl.BlockSpec((8, LANES), lambda i: (0, 0)),
            pl.BlockSpec((1, D_MODEL), lambda i: (0, 0)),
        ],
        out_shape=[
            jax.ShapeDtypeStruct((T, D_MODEL), BF16),
            jax.ShapeDtypeStruct((T, D_FF), BF16),
            jax.ShapeDtypeStruct((T, D_FF), BF16),
            jax.ShapeDtypeStruct((T, D_FF), BF16),
            jax.ShapeDtypeStruct((T, D_MODEL), F32),
            jax.ShapeDtypeStruct((8, LANES), F32),
            jax.ShapeDtypeStruct((1, D_MODEL), F32),
        ],
        compiler_params=_params(("arbitrary",)),
    )(x1, g2, gf, tgt, w_gate_t, w_up_t, w_down)


def _ffn_bwd(dx2, gate, up, x1, g2, w_gate_t, w_up_t, w_down):
    T = x1.shape[0]
    tm = min(T, FF_ROW_TILE)
    nc = D_FF // FF_CHUNK

    def body(dx2_ref, gate_ref, up_ref, x1_ref, g2_ref, wg_ref, wu_ref, wd_ref, dgate_ref, dup_ref, dx1_ref, dg2_ref):
        dx2v = dx2_ref[...]
        dx2b = dx2v.astype(BF16)
        for c in range(nc):
            sl = slice(c * FF_CHUNK, (c + 1) * FF_CHUNK)
            dact = _mm_nt(dx2b, wd_ref[sl, :])
            gate = gate_ref[:, sl].astype(F32)
            sg = _sigmoid(gate)
            silu = gate * sg
            dgate = (dact * up_ref[:, sl].astype(F32) * (sg * (1.0 + gate * (1.0 - sg)))).astype(BF16)
            dup = (dact * silu).astype(BF16)
            dgate_ref[:, sl] = dgate
            dup_ref[:, sl] = dup
        dh2 = _mm(dgate_ref[...], wg_ref[...]) + _mm(dup_ref[...], wu_ref[...])
        g2v = g2_ref[...]
        _, xh, r = _rms_fwd(x1_ref[...], g2v)
        dxn, dgrow = _rms_bwd(dh2, xh, r, g2v)
        dx1_ref[...] = dx2v + dxn

        @pl.when(pl.program_id(0) == 0)
        def _():
            dg2_ref[...] = jnp.zeros_like(dg2_ref)

        dg2_ref[...] += jnp.sum(dgrow, axis=0, keepdims=True)

    row = lambda n: pl.BlockSpec((tm, n), lambda i: (i, 0))
    return pl.pallas_call(
        body,
        name="ffn_bwd",
        grid=(T // tm,),
        in_specs=[
            row(D_MODEL), row(D_FF), row(D_FF), row(D_MODEL), _const_spec((1, D_MODEL)),
            _const_spec(w_gate_t.shape), _const_spec(w_up_t.shape), _const_spec(w_down.shape),
        ],
        out_specs=[row(D_FF), row(D_FF), row(D_MODEL), pl.BlockSpec((1, D_MODEL), lambda i: (0, 0))],
        out_shape=[
            jax.ShapeDtypeStruct((T, D_FF), BF16),
            jax.ShapeDtypeStruct((T, D_FF), BF16),
            jax.ShapeDtypeStruct((T, D_MODEL), F32),
            jax.ShapeDtypeStruct((1, D_MODEL), F32),
        ],
        compiler_params=_params(("arbitrary",), VMEM_LIMIT_MAX),
    )(dx2, gate, up, x1, g2, w_gate_t, w_up_t, w_down)


def _mix_bwd(dx1, gates, pool_y, attn_y, p2, scale, w_out, w_ao, w_po, token):
    T = dx1.shape[0]
    tm = ROW_TILE

    def body(dx1_ref, gt_ref, py_ref, ay_ref, p2_ref, sc_ref, wout_ref, wao_ref, wpo_ref, token_ref, dgt_ref, dpy_ref, day_ref, da_ref, dp2_ref, dsc_ref):
        dm = _mm_nt(dx1_ref[...].astype(BF16), wout_ref[...])
        sp = _sigmoid(gt_ref[:, :D_MODEL].astype(F32))
        sa = _sigmoid(gt_ref[:, D_MODEL:].astype(F32))
        dgt_ref[:, :D_MODEL] = (dm * py_ref[...].astype(F32) * (sp * (1.0 - sp))).astype(BF16)
        dgt_ref[:, D_MODEL:] = (dm * ay_ref[...].astype(F32) * (sa * (1.0 - sa))).astype(BF16)
        dpy = (dm * sp).astype(BF16)
        day = (dm * sa).astype(BF16)
        dpy_ref[...] = dpy
        day_ref[...] = day
        da_ref[...] = _mm_nt(day, wao_ref[...]).astype(BF16)
        dp3 = _mm_nt(dpy, wpo_ref[...])
        dp2_ref[...] = (dp3 * sc_ref[...]).astype(BF16)

        @pl.when(pl.program_id(0) == 0)
        def _():
            dsc_ref[...] = jnp.zeros_like(dsc_ref)

        dsc_ref[...] += jnp.sum(dp3 * p2_ref[...], axis=0, keepdims=True)

    row = lambda n: pl.BlockSpec((tm, n), lambda i: (i, 0))
    return pl.pallas_call(
        body,
        name="mix_bwd",
        grid=(T // tm,),
        in_specs=[
            row(D_MODEL), row(2 * D_MODEL), row(D_MODEL), row(D_MODEL), row(POOL_WIDTH), _const_spec((1, POOL_WIDTH)),
            _const_spec(w_out.shape), _const_spec(w_ao.shape), _const_spec(w_po.shape), _HBM,
        ],
        out_specs=[row(2 * D_MODEL), row(D_MODEL), row(D_MODEL), row(ATTN_WIDTH), row(POOL_WIDTH), pl.BlockSpec((1, POOL_WIDTH), lambda i: (0, 0))],
        out_shape=[
            jax.ShapeDtypeStruct((T, 2 * D_MODEL), BF16),
            jax.ShapeDtypeStruct((T, D_MODEL), BF16),
            jax.ShapeDtypeStruct((T, D_MODEL), BF16),
            jax.ShapeDtypeStruct((T, ATTN_WIDTH), BF16),
            jax.ShapeDtypeStruct((T, POOL_WIDTH), BF16),
            jax.ShapeDtypeStruct((1, POOL_WIDTH), F32),
        ],
        compiler_params=_params(("arbitrary",)),
    )(dx1, gates, pool_y, attn_y, p2, scale, w_out, w_ao, w_po, token)


def _pool_bwd(dp2, pm, mix_b, token, n_seq, S):
    T = n_seq * S

    def body(dp2_ref, pm_ref, mix_ref, token_ref, du_ref, dmix_ref):
        g = pl.program_id(0)
        dp2v = dp2_ref[...]
        dpm = _mm_nt(dp2v, mix_ref[...])
        row = lax.broadcasted_iota(jnp.int32, dpm.shape, 0)
        w = _window_pick(g, 2.0, 4.0, 8.0, 16.0)
        e = dpm / jnp.minimum((row + 1).astype(F32), w)

        def ahead(a, k):
            return jnp.where(row < S - k, pltpu.roll(a, S - k, 0), 0.0)

        r2 = e + ahead(e, 1)
        r4 = r2 + ahead(r2, 2)
        r8 = r4 + ahead(r4, 4)
        r16 = r8 + ahead(r8, 8)
        du_ref[...] = (_window_pick(g, r2, r4, r8, r16) - dpm).astype(BF16)

        @pl.when(pl.program_id(1) == 0)
        def _():
            dmix_ref[...] = jnp.zeros_like(dmix_ref)

        dmix_ref[...] += _mm_tn(pm_ref[...], dp2v)

    grp = pl.BlockSpec((S, GROUP_DIM), lambda g, s: (s, g))
    mixs = pl.BlockSpec((None, GROUP_DIM, GROUP_DIM), lambda g, s: (g, 0, 0))
    return pl.pallas_call(
        body,
        name="pool_bwd",
        grid=(len(POOL_WINDOWS), n_seq),
        in_specs=[grp, grp, mixs, _HBM],
        out_specs=[grp, mixs],
        out_shape=[jax.ShapeDtypeStruct((T, POOL_WIDTH), BF16), jax.ShapeDtypeStruct((len(POOL_WINDOWS), GROUP_DIM, GROUP_DIM), F32)],
        compiler_params=_params(("parallel", "arbitrary")),
    )(dp2, pm, mix_b, token)


def _attn_bwd(qkv, da, a, fcol, lse, n_seq, S):
    T = n_seq * S
    tb = ATTN_BLOCK
    nb = S // tb
    scale = HEAD_DIM ** -0.5

    def body(q_ref, k_ref, v_ref, do_ref, o_ref, fc_ref, st_ref, dq_ref, dk_ref, dv_ref, dfk_ref, dfq_ref,
             qa_sc, doa_sc, qat_sc, doat_sc, dq_acc, ka_sc, va_sc, dkt_sc, dvt_sc):
        j = pl.program_id(1)
        lane = lax.broadcasted_iota(jnp.int32, (1, LANES), 1)
        low = lane < HEAD_DIM

        @pl.when(j == 0)
        def _():
            dq_acc[...] = jnp.zeros_like(dq_acc)
            place = _bias_placement(0)

            def rows_q(i, carry):
                r0 = pl.multiple_of(i * tb, tb)
                delta = jnp.zeros((tb, LANES), F32)
                for h in range(N_HEADS):
                    pair = slice((h // 2) * LANES, (h // 2 + 1) * LANES)
                    prod = do_ref[pl.ds(r0, tb), pair].astype(F32) * o_ref[pl.ds(r0, tb), pair].astype(F32)
                    head = (lane >= HEAD_DIM * (h % 2)) & (lane < HEAD_DIM * (h % 2 + 1))
                    delta = jnp.where(lane == h, jnp.sum(jnp.where(head, prod, 0.0), axis=1, keepdims=True), delta)
                cq = fc_ref[pl.ds(r0, tb), :] - st_ref[pl.ds(r0, tb), :]
                q_bias = _mm(_bias_lanes(cq), place).astype(BF16)
                do_bias = _mm(_bias_lanes(-delta), place).astype(BF16)
                for h in range(N_HEADS):
                    pair = slice((h // 2) * LANES, (h // 2 + 1) * LANES)
                    qa = _augment(q_ref[pl.ds(r0, tb), pair], h, q_bias, 1)
                    doa = _augment(do_ref[pl.ds(r0, tb), pair], h, do_bias, None)
                    qa_sc[h, pl.ds(r0, tb), :] = qa
                    doa_sc[h, pl.ds(r0, tb), :] = doa
                    qat_sc[h, :, pl.ds(r0, tb)] = qa.astype(F32).T.astype(BF16)
                    doat_sc[h, :, pl.ds(r0, tb)] = doa.astype(F32).T.astype(BF16)
                return carry

            lax.fori_loop(0, nb, rows_q, 0)

        c0 = pl.multiple_of(j * tb, tb)
        k_bias = _mm(_bias_lanes(-fc_ref[pl.ds(c0, tb), :]), _bias_placement(1)).astype(BF16)
        for h in range(N_HEADS):
            pair = slice((h // 2) * LANES, (h // 2 + 1) * LANES)
            ka_sc[h] = _augment(k_ref[:, pair] * scale, h, k_bias, 0)
            va_sc[h] = _augment(v_ref[:, pair], h, None, 0)
        dkt_sc[...] = jnp.zeros_like(dkt_sc)
        dvt_sc[...] = jnp.zeros_like(dvt_sc)
        causal = lax.broadcasted_iota(jnp.int32, (tb, tb), 1) <= lax.broadcasted_iota(jnp.int32, (tb, tb), 0)

        def step(i, masked):
            r0 = pl.multiple_of(i * tb, tb)
            for h in range(N_HEADS):
                s = _mm_nt(qa_sc[h, pl.ds(r0, tb), :], ka_sc[h])
                if masked:
                    s = jnp.where(causal, s, -jnp.inf)
                pr = jnp.exp(s)
                dvt_sc[h] += _mm(doat_sc[h, :, pl.ds(r0, tb)], pr.astype(BF16))
                dsb = (pr * _mm_nt(doa_sc[h, pl.ds(r0, tb), :], va_sc[h])).astype(BF16)
                dkt_sc[h] += _mm(qat_sc[h, :, pl.ds(r0, tb)], dsb)
                dq_acc[h, pl.ds(r0, tb), :] += _mm(dsb, ka_sc[h])

        step(j, True)

        def loop_body(i, carry):
            step(i, False)
            return carry

        lax.fori_loop(j + 1, nb, loop_body, 0)
        dfk = jnp.zeros((tb, LANES), F32)
        for p in range(N_PAIRS):
            dk = [dkt_sc[2 * p + hh].T for hh in range(2)]
            dv = [dvt_sc[2 * p + hh].T for hh in range(2)]
            dk_ref[:, p * LANES : (p + 1) * LANES] = (jnp.where(low, dk[0], dk[1]) * scale).astype(BF16)
            dv_ref[:, p * LANES : (p + 1) * LANES] = jnp.where(low, dv[0], dv[1]).astype(BF16)
            for hh in range(2):
                b = HEAD_DIM * (1 - hh) + 3
                dfk = jnp.where(lane == 2 * p + hh, -dk[hh][:, b : b + 1], dfk)
        dfk_ref[...] = dfk

        @pl.when(j == nb - 1)
        def _():
            def rows_dq(i, carry):
                r0 = pl.multiple_of(i * tb, tb)
                dfq = jnp.zeros((tb, LANES), F32)
                for p in range(N_PAIRS):
                    parts = [dq_acc[2 * p + hh, pl.ds(r0, tb), :] for hh in range(2)]
                    dq_ref[pl.ds(r0, tb), p * LANES : (p + 1) * LANES] = jnp.where(low, parts[0], parts[1]).astype(BF16)
                    for hh in range(2):
                        b = HEAD_DIM * (1 - hh)
                        dfq = jnp.where(lane == 2 * p + hh, parts[hh][:, b : b + 1], dfq)
                dfq_ref[pl.ds(r0, tb), :] = dfq
                return carry

            lax.fori_loop(0, nb, rows_dq, 0)

    seq = lambda w, col: pl.BlockSpec((S, w), lambda s, j: (s, col))
    seq_in = lambda w, col: pl.BlockSpec((S, w), lambda s, j: (s, col), pipeline_mode=pl.Buffered(1))
    blk = lambda w, col: pl.BlockSpec((tb, w), lambda s, j: (s * nb + j, col))
    return pl.pallas_call(
        body,
        name="attn_bwd",
        grid=(n_seq, nb),
        in_specs=[seq_in(ATTN_WIDTH, 0), blk(ATTN_WIDTH, 1), blk(ATTN_WIDTH, 2), seq_in(ATTN_WIDTH, 0), seq_in(ATTN_WIDTH, 0), seq_in(LANES, 0), seq_in(LANES, 0)],
        out_specs=[seq(ATTN_WIDTH, 0), blk(ATTN_WIDTH, 0), blk(ATTN_WIDTH, 0), blk(LANES, 0), seq(LANES, 0)],
        out_shape=[
            jax.ShapeDtypeStruct((T, ATTN_WIDTH), BF16),
            jax.ShapeDtypeStruct((T, ATTN_WIDTH), BF16),
            jax.ShapeDtypeStruct((T, ATTN_WIDTH), BF16),
            jax.ShapeDtypeStruct((T, LANES), F32),
            jax.ShapeDtypeStruct((T, LANES), F32),
        ],
        scratch_shapes=[
            pltpu.VMEM((N_HEADS, S, LANES), BF16),
            pltpu.VMEM((N_HEADS, S, LANES), BF16),
            pltpu.VMEM((N_HEADS, LANES, S), BF16),
            pltpu.VMEM((N_HEADS, LANES, S), BF16),
            pltpu.VMEM((N_HEADS, S, LANES), F32),
            pltpu.VMEM((N_HEADS, tb, LANES), BF16),
            pltpu.VMEM((N_HEADS, tb, LANES), BF16),
            pltpu.VMEM((N_HEADS, LANES, tb), F32),
            pltpu.VMEM((N_HEADS, LANES, tb), F32),
        ],
        compiler_params=_params(("parallel", "arbitrary"), VMEM_LIMIT_MAX),
    )(qkv, qkv, qkv, da, a, fcol, lse)


def _forget_bwd(dfk, dfq, fl, b_pad, n_seq, S):
    def body(df_ref, dfq_ref, fl_ref, b_ref, dfl_ref, db_ref):
        t = (df_ref[...] + dfq_ref[...]).T
        lane = lax.broadcasted_iota(jnp.int32, t.shape, 1)
        k = 1
        while k < S:
            t = t + jnp.where(lane < S - k, pltpu.roll(t, S - k, 1), 0.0)
            k *= 2
        dfl = t.T * _sigmoid(-(fl_ref[...] + b_ref[...]))
        dfl_ref[...] = dfl.astype(BF16)

        @pl.when(pl.program_id(0) == 0)
        def _():
            db_ref[...] = jnp.zeros_like(db_ref)

        db_ref[...] += jnp.sum(dfl, axis=0, keepdims=True)

    return pl.pallas_call(
        body,
        name="forget_bwd",
        grid=(n_seq,),
        in_specs=[
            pl.BlockSpec((S, LANES), lambda s: (s, 0)),
            pl.BlockSpec((S, LANES), lambda s: (s, 0)),
            pl.BlockSpec((S, FL_PAD), lambda s: (s, 0)),
            _const_spec((1, FL_PAD)),
        ],
        out_specs=[pl.BlockSpec((S, FL_PAD), lambda s: (s, 0)), pl.BlockSpec((1, FL_PAD), lambda s: (0, 0))],
        out_shape=[jax.ShapeDtypeStruct((n_seq * S, FL_PAD), BF16), jax.ShapeDtypeStruct((1, FL_PAD), F32)],
        compiler_params=_params(("arbitrary",)),
    )(dfk, dfq, fl, b_pad)


def _in_proj_bwd(du, dq, dk, dv, dfl, dgates, x, dx1, g1, w_uqkv, w_fl, w_g, token):
    T = x.shape[0]
    tm = ROW_TILE

    def body(du_ref, dq_ref, dk_ref, dv_ref, dfl_ref, dgt_ref, x_ref, dx1_ref, g_ref, wa_ref, wf_ref, wg_ref, token_ref, dx_ref, dg_ref):
        dz = jnp.concatenate([du_ref[...], dq_ref[...], dk_ref[...], dv_ref[...]], axis=1)
        dh = _mm_nt(dz, wa_ref[...]) + _mm_nt(dgt_ref[...], wg_ref[...]) + _mm_nt(dfl_ref[...], wf_ref[...])
        gv = g_ref[...]
        _, xh, r = _rms_fwd(x_ref[...], gv)
        dxn, dgrow = _rms_bwd(dh, xh, r, gv)
        dx_ref[...] = dx1_ref[...] + dxn

        @pl.when(pl.program_id(0) == 0)
        def _():
            dg_ref[...] = jnp.zeros_like(dg_ref)

        dg_ref[...] += jnp.sum(dgrow, axis=0, keepdims=True)

    row = lambda n: pl.BlockSpec((tm, n), lambda i: (i, 0))
    return pl.pallas_call(
        body,
        name="in_proj_bwd",
        grid=(T // tm,),
        in_specs=[
            row(512), row(512), row(512), row(512), row(FL_PAD), row(2 * D_MODEL), row(D_MODEL), row(D_MODEL), _const_spec((1, D_MODEL)),
            _const_spec(w_uqkv.shape), _const_spec(w_fl.shape), _const_spec(w_g.shape), _HBM,
        ],
        out_specs=[row(D_MODEL), pl.BlockSpec((1, D_MODEL), lambda i: (0, 0))],
        out_shape=[jax.ShapeDtypeStruct((T, D_MODEL), F32), jax.ShapeDtypeStruct((1, D_MODEL), F32)],
        compiler_params=_params(("arbitrary",)),
    )(du, dq, dk, dv, dfl, dgates, x, dx1, g1, w_uqkv, w_fl, w_g, token)


def _pick_block(n):
    for b in (1024, 512, 1408, 256, 128):
        if n % b == 0:
            return b
    raise ValueError(n)


def _matmul_tn(a, b, name):
    T, K = a.shape
    N = b.shape[1]
    bt, bk, bn = min(T, DW_TOKENS), _pick_block(K), _pick_block(N)
    nt = T // bt

    def body(a_ref, b_ref, o_ref, acc):
        @pl.when(pl.program_id(2) == 0)
        def _():
            acc[...] = jnp.zeros_like(acc)

        acc[...] += _mm_tn(a_ref[...].astype(BF16), b_ref[...].astype(BF16))

        @pl.when(pl.program_id(2) == nt - 1)
        def _():
            o_ref[...] = acc[...].astype(BF16)

    return pl.pallas_call(
        body,
        name=name,
        grid=(K // bk, N // bn, nt),
        in_specs=[pl.BlockSpec((bt, bk), lambda k, n, t: (t, k)), pl.BlockSpec((bt, bn), lambda k, n, t: (t, n))],
        out_specs=pl.BlockSpec((bk, bn), lambda k, n, t: (k, n)),
        out_shape=jax.ShapeDtypeStruct((K, N), BF16),
        scratch_shapes=[pltpu.VMEM((bk, bn), F32)],
        compiler_params=_params(("parallel", "parallel", "arbitrary")),
    )(a, b)


W_IN_A = POOL_WIDTH + 3 * ATTN_WIDTH
W_IN_SHARD = (W_IN_A + N_HEADS + 2 * D_MODEL) // N_DEV
_W_IN_PIECES = ((0, W_IN_A), (W_IN_A, W_IN_A + N_HEADS), (W_IN_A + N_HEADS, W_IN_A + N_HEADS + 2 * D_MODEL))


def _w_in_segments(d):
    lo, hi = d * W_IN_SHARD, (d + 1) * W_IN_SHARD
    out = []
    for p, (a, b) in enumerate(_W_IN_PIECES):
        s, e = max(lo, a), min(hi, b)
        if s < e:
            out.append((p, s - a, s - lo, e - s))
    return out


def _w_in_pieces(gathered):
    tm = ROW_TILE // 2

    def body(g_ref, wa_ref, wf_ref, wg_ref):
        outs = (wa_ref, wf_ref, wg_ref)
        wf_ref[...] = jnp.zeros_like(wf_ref)
        for d in range(N_DEV):
            for p, at, frm, n in _w_in_segments(d):
                outs[p][:, at : at + n] = g_ref[d, :, frm : frm + n]

    return pl.pallas_call(
        body,
        name="w_in_pieces",
        grid=(D_MODEL // tm,),
        in_specs=[pl.BlockSpec((N_DEV, tm, W_IN_SHARD), lambda i: (0, i, 0))],
        out_specs=[pl.BlockSpec((tm, W_IN_A), lambda i: (i, 0)), pl.BlockSpec((tm, FL_PAD), lambda i: (i, 0)), pl.BlockSpec((tm, 2 * D_MODEL), lambda i: (i, 0))],
        out_shape=[
            jax.ShapeDtypeStruct((D_MODEL, W_IN_A), gathered.dtype),
            jax.ShapeDtypeStruct((D_MODEL, FL_PAD), gathered.dtype),
            jax.ShapeDtypeStruct((D_MODEL, 2 * D_MODEL), gathered.dtype),
        ],
        compiler_params=_params(("parallel",)),
    )(gathered)


def _dw_in(h, du, dq, dk, dv, dfl, dgates, token):
    T = h.shape[0]
    bt, bk = min(T, DW_TOKENS // 2), 512
    nt = T // bt
    pieces = (du, dq, dk, dv, dfl, dgates)
    offs = [0]
    for p in pieces:
        offs.append(offs[-1] + p.shape[1])

    def body(h_ref, *rest):
        refs, o_ref, acc = rest[: len(pieces)], rest[-2], rest[-1]

        @pl.when(pl.program_id(1) == 0)
        def _():
            acc[...] = jnp.zeros_like(acc)

        ht = h_ref[...].T
        for ref, at in zip(refs, offs):
            acc[:, at : at + ref.shape[1]] += _mm(ht, ref[...])

        @pl.when(pl.program_id(1) == nt - 1)
        def _():
            starts = (0, W_IN_A, W_IN_A + FL_PAD)
            for d in range(N_DEV):
                for p, at, to, n in _w_in_segments(d):
                    o_ref[d % 2, d // 2, :, to : to + n] = acc[:, starts[p] + at : starts[p] + at + n].astype(BF16)

    return pl.pallas_call(
        body,
        name="dw_in",
        grid=(D_MODEL // bk, nt),
        in_specs=[pl.BlockSpec((bt, bk), lambda k, t: (t, k))] + [pl.BlockSpec((bt, p.shape[1]), lambda k, t: (t, 0)) for p in pieces] + [_HBM],
        out_specs=pl.BlockSpec((2, 4, bk, W_IN_SHARD), lambda k, t: (0, 0, k, 0)),
        out_shape=jax.ShapeDtypeStruct((2, 4, D_MODEL, W_IN_SHARD), BF16),
        scratch_shapes=[pltpu.VMEM((bk, offs[-1]), F32)],
        compiler_params=_params(("parallel", "arbitrary")),
    )(h, *pieces, token)


def _position():
    return lax.axis_index("x"), lax.axis_index("y"), lax.axis_index("c")


_HBM = pl.BlockSpec(memory_space=pl.ANY)


def _all_gather(blocks, name):
    n = len(blocks)

    def body(*refs):
        xs, outs = refs[:n], refs[n : 2 * n]
        send_sems, recv_sems, local_sems = refs[2 * n :]
        x, y, c = _position()
        me, sibling = (x, y, c), (x, y, 1 - c)
        chips = [(1 - x, y), (x, 1 - y), (1 - x, 1 - y)]

        def rows(a, px, py, pc):
            return outs[a].at[4 * px + 2 * py + pc]

        def copy(a, k, blk, to, src=None):
            return pltpu.make_async_remote_copy(
                src_ref=rows(a, *blk) if src is None else src, dst_ref=rows(a, *blk),
                send_sem=send_sems.at[7 * a + k], recv_sem=recv_sems.at[7 * a + k], device_id=to, device_id_type=MESH,
            )

        mine = [pltpu.make_async_copy(xs[a], rows(a, *me), local_sems.at[a]) for a in range(n)]
        for cp in mine:
            cp.start()
        first = []
        for a in range(n):
            first.append(copy(a, 0, me, sibling, src=xs[a]))
            first += [copy(a, 1 + j, me, (*chip, c), src=xs[a]) for j, chip in enumerate(chips)]
        for cp in first:
            cp.start()
        passed = []
        for j, chip in enumerate(chips):
            for a in range(n):
                copy(a, 1 + j, (*chip, c), me).wait_recv()
                passed.append(copy(a, 4 + j, (*chip, c), sibling))
                passed[-1].start()
        for a in range(n):
            copy(a, 0, sibling, me).wait_recv()
        for j, chip in enumerate(chips):
            for a in range(n):
                copy(a, 4 + j, (*chip, 1 - c), me).wait_recv()
        for cp in first + passed:
            cp.wait_send()
        for cp in mine:
            cp.wait()

    return pl.pallas_call(
        body,
        name=name,
        out_shape=[jax.ShapeDtypeStruct((N_DEV, *b.shape), b.dtype) for b in blocks],
        in_specs=[_HBM] * n,
        out_specs=[_HBM] * n,
        scratch_shapes=[pltpu.SemaphoreType.DMA((7 * n,)), pltpu.SemaphoreType.DMA((7 * n,)), pltpu.SemaphoreType.DMA((n,))],
    )(*blocks)


_SEM = pl.BlockSpec(memory_space=pltpu.SEMAPHORE)
_HBM_ONLY = pl.BlockSpec(memory_space=pltpu.HBM)
_SIDE_EFFECT = pltpu.SideEffectType.DATAFLOW_SIDE_EFFECTING


def _peer(x, y, c, k):
    return (1 - x if k & 4 else x, 1 - y if k & 2 else y, 1 - c if k & 1 else c)


_PEER_BITS = {"gather": range(1, N_DEV), "scatter": range(1, N_DEV), "chips": (4, 2, 6)}
_LAND_SLOTS = {"gather": N_DEV, "scatter": N_DEV, "chips": 3}


def _exchange_copies(src_refs, land_refs, send_sems, recv_sems, pattern, receive_side):
    x, y, c = _position()
    me = 4 * x + 2 * y + c
    bits = _PEER_BITS[pattern]
    cps = []
    for j, k in enumerate(bits):
        px, py, pc = _peer(x, y, c, k)
        peer = 4 * px + 2 * py + pc
        for a, (src, land) in enumerate(zip(src_refs, land_refs)):
            if pattern == "chips":
                s, slot = src.at[2 * px + py], j
            else:
                s, slot = (src if pattern == "gather" else src.at[peer]), (peer if receive_side else me)
            cps.append(pltpu.make_async_remote_copy(
                src_ref=s, dst_ref=land.at[slot],
                send_sem=send_sems.at[len(bits) * a + j], recv_sem=recv_sems.at[len(bits) * a + j],
                device_id=(px, py, pc), device_id_type=MESH,
            ))
    return cps


def _exchange_start(srcs, after, name, pattern):
    n = len(srcs)
    m = len(_PEER_BITS[pattern])
    lands = [jax.ShapeDtypeStruct((_LAND_SLOTS[pattern], *s.shape[-2:]), s.dtype) for s in srcs]

    def body(*refs):
        src_refs, land_refs = refs[1 : 1 + n], refs[1 + n : 1 + 2 * n]
        send_sems, recv_sems = refs[1 + 2 * n], refs[2 + 2 * n]
        token = refs[-1]
        for cp in _exchange_copies(src_refs, land_refs, send_sems, recv_sems, pattern, receive_side=False):
            cp.start()
        token[...] = jnp.zeros_like(token)

    hbm = lambda t: pltpu.with_memory_space_constraint(t, pltpu.HBM)
    out = pl.pallas_call(
        body,
        name=name,
        out_shape=(
            pltpu.SemaphoreType.DMA((m * n,)), pltpu.SemaphoreType.DMA((m * n,)),
            *[pltpu.HBM(s.shape, s.dtype) for s in srcs], *[pltpu.HBM(l.shape, l.dtype) for l in lands],
            jax.ShapeDtypeStruct((8, LANES), F32),
        ),
        in_specs=(_HBM, *[_HBM_ONLY] * (2 * n)),
        out_specs=(_SEM, _SEM, *[_HBM_ONLY] * (2 * n), pl.BlockSpec(memory_space=pltpu.VMEM)),
        input_output_aliases={1 + i: 2 + i for i in range(2 * n)},
        compiler_params=pltpu.CompilerParams(has_side_effects=_SIDE_EFFECT),
    )(after, *[hbm(s) for s in srcs], *[hbm(lax.empty(l.shape, l.dtype)) for l in lands])
    return out[0], out[1], out[2 : 2 + n], out[2 + n : 2 + 2 * n], out[-1]


def _exchange_wait(send_sems, recv_sems, srcs, lands, after, name, pattern):
    n = len(srcs)

    def body(*refs):
        src_refs, land_refs = refs[:n], refs[n : 2 * n]
        for cp in _exchange_copies(src_refs, land_refs, refs[2 * n], refs[2 * n + 1], pattern, receive_side=True):
            cp.wait_send()
            cp.wait_recv()

    out = pl.pallas_call(
        body,
        name=name,
        out_shape=(*[pltpu.HBM(s.shape, s.dtype) for s in srcs], *[pltpu.HBM(l.shape, l.dtype) for l in lands]),
        in_specs=(*[_HBM_ONLY] * (2 * n), _SEM, _SEM, _HBM),
        out_specs=tuple([_HBM_ONLY] * (2 * n)),
        input_output_aliases={i: i for i in range(2 * n)},
        compiler_params=pltpu.CompilerParams(has_side_effects=_SIDE_EFFECT),
    )(*srcs, *lands, send_sems, recv_sems, after)
    return out[:n], out[n:]


def _sibling_exchange(sends):
    n = len(sends)

    def body(*refs):
        srcs, dsts = refs[:n], refs[n : 2 * n]
        send_sems, recv_sems = refs[2 * n :]
        x, y, c = _position()
        cps = [
            pltpu.make_async_remote_copy(
                src_ref=srcs[a].at[1 - c], dst_ref=dsts[a], send_sem=send_sems.at[a], recv_sem=recv_sems.at[a],
                device_id=(x, y, 1 - c), device_id_type=MESH,
            )
            for a in range(n)
        ]
        for cp in cps:
            cp.start()
        for cp in cps:
            cp.wait()

    return pl.pallas_call(
        body,
        name="rs_sibling",
        out_shape=[jax.ShapeDtypeStruct(s.shape[1:], s.dtype) for s in sends],
        in_specs=[_HBM] * n,
        out_specs=[_HBM] * n,
        scratch_shapes=[pltpu.SemaphoreType.DMA((n,)), pltpu.SemaphoreType.DMA((n,))],
    )(*sends)


def _rows_tile(r):
    return ROW_TILE if r % ROW_TILE == 0 else r


def _pair_sum(send, got, core, name):
    _, _, r, c = send.shape
    br = _rows_tile(r)

    def body(core_ref, a_ref, b_ref, o_ref):
        o_ref[...] = (a_ref[...].astype(F32) + b_ref[...].astype(F32)).astype(o_ref.dtype)

    return pl.pallas_call(
        body,
        name=name,
        grid_spec=pltpu.PrefetchScalarGridSpec(
            num_scalar_prefetch=1,
            grid=(4, r // br),
            in_specs=[
                pl.BlockSpec((None, None, br, c), lambda n, i, core: (core[0], n, i, 0)),
                pl.BlockSpec((None, br, c), lambda n, i, core: (n, i, 0)),
            ],
            out_specs=pl.BlockSpec((None, br, c), lambda n, i, core: (n, i, 0)),
        ),
        out_shape=jax.ShapeDtypeStruct((4, r, c), send.dtype),
        compiler_params=_params(("parallel", "parallel")),
    )(core, send, got)


def _adamw(w, g, m, v):
    m = ADAM_B1 * m + (1.0 - ADAM_B1) * g
    v = ADAM_B2 * v + (1.0 - ADAM_B2) * (g * g)
    m_hat = m / (1.0 - ADAM_B1 ** ADAM_STEP)
    v_hat = v / (1.0 - ADAM_B2 ** ADAM_STEP)
    delta = -ADAM_LR * (m_hat / (jnp.sqrt(v_hat) + ADAM_EPS) + ADAM_WD * w)
    return delta, m, v


def _shard_update(send, got, recv, w, m, v, pos, name):
    _, r, c = w.shape
    br = _rows_tile(r)

    def body(pos_ref, a_ref, b_ref, r_ref, w_ref, m_ref, v_ref, g_ref, d_ref, nm_ref, nv_ref):
        g = a_ref[...].astype(F32) + b_ref[...].astype(F32)
        for n in range(3):
            g = g + r_ref[n].astype(F32)
        g_ref[...] = g
        d_ref[...], nm_ref[...], nv_ref[...] = _adamw(w_ref[...], g, m_ref[...], v_ref[...])

    own = pl.BlockSpec((None, br, c), lambda i, pos: (0, i, 0))
    return pl.pallas_call(
        body,
        name=name,
        grid_spec=pltpu.PrefetchScalarGridSpec(
            num_scalar_prefetch=1,
            grid=(r // br,),
            in_specs=[
                pl.BlockSpec((None, None, br, c), lambda i, pos: (pos[0], pos[1], i, 0)),
                pl.BlockSpec((None, br, c), lambda i, pos: (pos[1], i, 0)),
                pl.BlockSpec((3, br, c), lambda i, pos: (0, i, 0)),
                own, own, own,
            ],
            out_specs=[own, own, own, own],
        ),
        out_shape=[jax.ShapeDtypeStruct((1, r, c), F32)] * 4,
        compiler_params=_params(("parallel",)),
    )(pos, send, got, recv, w, m, v)


def _shard_update_direct(parts, chunks, w, m, v, me, name):
    _, r, c = w.shape
    br = _rows_tile(r)

    def body(me_ref, p_ref, own_ref, w_ref, m_ref, v_ref, g_ref, d_ref, nm_ref, nv_ref):
        g = None
        for n in range(N_DEV):
            part = jnp.where(me_ref[0] == n, own_ref[...], p_ref[n]).astype(F32)
            g = part if g is None else g + part
        g_ref[...] = g
        d_ref[...], nm_ref[...], nv_ref[...] = _adamw(w_ref[...], g, m_ref[...], v_ref[...])

    shard = pl.BlockSpec((None, br, c), lambda i, me: (0, i, 0))
    return pl.pallas_call(
        body,
        name=name,
        grid_spec=pltpu.PrefetchScalarGridSpec(
            num_scalar_prefetch=1,
            grid=(r // br,),
            in_specs=[
                pl.BlockSpec((N_DEV, br, c), lambda i, me: (0, i, 0)),
                pl.BlockSpec((None, br, c), lambda i, me: (me[0], i, 0)),
                shard, shard, shard,
            ],
            out_specs=[shard, shard, shard, shard],
        ),
        out_shape=[jax.ShapeDtypeStruct((1, r, c), F32)] * 4,
        compiler_params=_params(("parallel",)),
    )(me, parts, chunks, w, m, v)


def _small_update(parts, first_rows, w, m, v):
    R = w.shape[0]

    def body(p_ref, f_ref, w_ref, m_ref, v_ref, g_ref, d_ref, nm_ref, nv_ref):
        g, first = p_ref[0], f_ref[0]
        for n in range(1, N_DEV):
            g = g + p_ref[n]
            first = first + f_ref[n]
        g = jnp.concatenate([g[:8] + first, g[8:]], axis=0)
        g_ref[...] = g
        d_ref[...], nm_ref[...], nv_ref[...] = _adamw(w_ref[...], g, m_ref[...], v_ref[...])

    return pl.pallas_call(
        body,
        name="small_update",
        out_shape=[jax.ShapeDtypeStruct((R, LANES), F32)] * 4,
        compiler_params=pltpu.CompilerParams(vmem_limit_bytes=VMEM_LIMIT),
    )(parts, first_rows, w, m, v)


_SHARD_AXIS = (1, 1, 1, 0, 0, 0, 0)
_TRANSPOSED = (False, False, False, False, True, True, False)


def _full_from_gathered(t, axis):
    if axis == 0:
        return t.reshape(N_DEV * t.shape[1], t.shape[2])
    return jnp.concatenate([t[d] for d in range(N_DEV)], axis=1)


def _chunks_from_cols(t):
    c = t.shape[1] // N_DEV
    return jnp.stack([t[:, d * c : (d + 1) * c] for d in range(N_DEV)])


_SMALL = (("norm1_g", 8), ("norm2_g", 8), ("norm_f_g", 8), ("b_forget", 8), ("pool_scale", 8), ("pool_mix", 512))


def _pack_small(vals, loss_row):
    parts = []
    for (name, rows), t in zip(_SMALL, vals):
        f = t.astype(F32).reshape(-1)
        f = jnp.concatenate([f, jnp.zeros((rows * LANES - f.shape[0],), F32)]).reshape(rows, LANES)
        parts.append(f)
    parts.append(loss_row)
    return jnp.concatenate(parts, axis=0)


def _unpack_small(packed, shapes):
    out, off = [], 0
    for (name, rows), shape in zip(_SMALL, shapes):
        n = 1
        for s in shape:
            n *= s
        out.append(packed[off : off + rows].reshape(-1)[:n].reshape(shape))
        off += rows
    return out, packed[off, 0]


def _local_grads(x, tgt, g1, g2, gf, b_forget, pool_mix, pool_scale, w_in, fwd_token, out_weights, ffn_weights, ffn_grads_out, out_grads_out, small_grads_out, in_grads_out, norm1_grad_out):
    n_seq, S, _ = x.shape
    T = n_seq * S
    x2 = x.reshape(T, D_MODEL)
    tg2 = tgt.reshape(T, D_MODEL)
    w_uqkv, w_fl, w_g = w_in
    b_pad = jnp.concatenate([b_forget.reshape(1, N_HEADS), jnp.zeros((1, FL_PAD - N_HEADS), F32)], axis=1)
    mix_b = pool_mix.reshape(len(POOL_WINDOWS), GROUP_DIM, GROUP_DIM).astype(BF16)
    scale = pool_scale.reshape(1, POOL_WIDTH)
    g1 = g1.reshape(1, D_MODEL)
    g2 = g2.reshape(1, D_MODEL)
    gf = gf.reshape(1, D_MODEL)

    h, u, qkv, fl, gates = _in_proj(x2, g1, w_uqkv, w_fl, w_g, fwd_token)
    fcol = _forget_fwd(fl, b_pad, n_seq, S)
    pm, p2, p3 = _pool_fwd(u, mix_b, scale, n_seq, S)
    a, lse = _attn_fwd(qkv, fcol, n_seq, S)
    w_po, w_ao, w_out = out_weights(a)
    merged, x1, attn_y, pool_y = _mix_out(a, p3, gates, x2, w_ao, w_po, w_out)
    w_gate_t, w_up_t, w_down = ffn_weights(x1)
    h2, gate, up, act, dx2, loss_rows, dgf = _ffn_fwd(x1, g2, gf, tg2, w_gate_t, w_up_t, w_down)

    dgate, dup, dx1, dg2 = _ffn_bwd(dx2, gate, up, x1, g2, w_gate_t, w_up_t, w_down)
    bwd_token = ffn_grads_out(_matmul_tn(dgate, h2, "dw_ffn_gate"), _matmul_tn(dup, h2, "dw_ffn_up"), _matmul_tn(act, dx2, "dw_ffn_down"))
    dgates, dpy, day, da, dp2, dscale = _mix_bwd(dx1, gates, pool_y, attn_y, p2, scale, w_out, w_ao, w_po, bwd_token)
    out_token = out_grads_out(_matmul_tn(p3, dpy, "dw_pool_out"), _matmul_tn(a, day, "dw_attn_out"), _matmul_tn(merged, dx1, "dw_out"))
    du, dmix = _pool_bwd(dp2, pm, mix_b, out_token, n_seq, S)
    dq, dk, dv, dfk, dfq = _attn_bwd(qkv, da, a, fcol, lse, n_seq, S)
    dfl, db = _forget_bwd(dfk, dfq, fl, b_pad, n_seq, S)
    small_token = small_grads_out((jnp.zeros_like(g1), dg2, dgf, db[:, :N_HEADS], dscale, dmix), loss_rows)
    in_token = in_grads_out(_dw_in(h, du, dq, dk, dv, dfl, dgates, small_token))
    dx, dg1 = _in_proj_bwd(du, dq, dk, dv, dfl, dgates, x2, dx1, g1, w_uqkv, w_fl, w_g, in_token)
    norm1_grad_out(dg1)
    return dx.reshape(n_seq, S, D_MODEL)


def kernel(x, norm1_g, w_in, b_forget, pool_mix, pool_scale, w_pool_out, w_attn_out, w_out, norm2_g, w_ffn_gate, w_ffn_up, w_ffn_down, norm_f_g, loss_target, m_norm1_g, m_w_in, m_b_forget, m_pool_mix, m_pool_scale, m_w_pool_out, m_w_attn_out, m_w_out, m_norm2_g, m_w_ffn_gate, m_w_ffn_up, m_w_ffn_down, m_norm_f_g, v_norm1_g, v_w_in, v_b_forget, v_pool_mix, v_pool_scale, v_w_pool_out, v_w_attn_out, v_w_out, v_norm2_g, v_w_ffn_gate, v_w_ffn_up, v_w_ffn_down, v_norm_f_g):
    names = ("w_in", "w_pool_out", "w_attn_out", "w_out", "w_ffn_gate", "w_ffn_up", "w_ffn_down")
    w_sh = (w_in, w_pool_out, w_attn_out, w_out, w_ffn_gate, w_ffn_up, w_ffn_down)
    m_sh = (m_w_in, m_w_pool_out, m_w_attn_out, m_w_out, m_w_ffn_gate, m_w_ffn_up, m_w_ffn_down)
    v_sh = (v_w_in, v_w_pool_out, v_w_attn_out, v_w_out, v_w_ffn_gate, v_w_ffn_up, v_w_ffn_down)

    cx, cy, cc = _position()
    me = 4 * cx + 2 * cy + cc
    def stored(t, transposed):
        return jnp.transpose(t, (0, 2, 1)) if transposed else t

    w_sh, m_sh, v_sh = ([stored(t, tr) for t, tr in zip(ts, _TRANSPOSED)] for ts in (w_sh, m_sh, v_sh))
    shards = [w[0].astype(BF16) for w in w_sh]
    (gathered_in,) = _all_gather(shards[:1], "w_in_all_gather")
    out_sems = _exchange_start(shards[1:4], gathered_in, "out_weights_gather_start", "gather")
    ffn_sems = _exchange_start(shards[4:], out_sems[4], "ffn_weights_gather_start", "gather")
    no_order = jnp.zeros((8, LANES), F32)

    def with_own(lands, own):
        return [lax.dynamic_update_slice(l, o[None], (me, 0, 0)) for l, o in zip(lands, own)]

    def gathered_weights(sems, axes, name):
        def wait(after):
            send_sems, recv_sems, srcs, lands, _ = sems
            srcs, lands = _exchange_wait(send_sems, recv_sems, srcs, lands, after, name, "gather")
            return [_full_from_gathered(t, axis) for t, axis in zip(with_own(lands, srcs), axes)]

        return wait

    started = {}

    def scatter_grads(key, name):
        def start(*whole_grads):
            chunks = [
                _chunks_from_cols(t) if axis == 1 else t.reshape(N_DEV, -1, t.shape[1])
                for t, axis in zip(whole_grads, _SHARD_AXIS[key])
            ]
            started[key] = _exchange_start(chunks, no_order, name, "scatter")
            return started[key][4]

        return start

    def gather_small(small, loss_rows):
        started["small"] = _exchange_start([_pack_small(small, loss_rows)], no_order, "small_grads_gather_start", "gather")
        return started["small"][4]

    core = jnp.reshape(cc, (1,)).astype(jnp.int32)
    pos = jnp.stack([cc, 2 * cx + cy]).astype(jnp.int32)

    def reduce_w_in(send_in):
        (got_in,) = _sibling_exchange([send_in])
        pair_in = _pair_sum(send_in, got_in, core, "pair_sum_w_in")
        started["in"] = (send_in, got_in, _exchange_start([pair_in], no_order, "w_in_grads_chips_start", "chips"))
        return started["in"][2][4]

    def gather_norm1(dg1):
        rows = jnp.reshape(dg1, (8, LANES))
        started["norm1"] = _exchange_start([rows], no_order, "norm1_grad_gather_start", "gather")

    ffn, out = slice(4, 7), slice(1, 4)
    grad_x = _local_grads(
        x, loss_target, norm1_g, norm2_g, norm_f_g, b_forget, pool_mix, pool_scale, _w_in_pieces(gathered_in), ffn_sems[4],
        gathered_weights(out_sems, _SHARD_AXIS[out], "out_weights_gather_wait"),
        gathered_weights(ffn_sems, _SHARD_AXIS[ffn], "ffn_weights_gather_wait"),
        scatter_grads(ffn, "ffn_grads_scatter_start"), scatter_grads(out, "out_grads_scatter_start"), gather_small, reduce_w_in, gather_norm1,
    )
    send_in, got_in, chip_sems = started["in"]

    def scattered_updates(key, after, name):
        send_sems, recv_sems, srcs, lands, _ = started[key]
        srcs, lands = _exchange_wait(send_sems, recv_sems, srcs, lands, after, name, "scatter")
        return [
            _shard_update_direct(p, s, w, m, v, jnp.reshape(me, (1,)).astype(jnp.int32), "update_" + n)
            for p, s, w, m, v, n in zip(lands, srcs, w_sh[key], m_sh[key], v_sh[key], names[key])
        ]

    updates_out = scattered_updates(out, grad_x, "out_grads_scatter_wait")
    updates_ffn = scattered_updates(ffn, grad_x, "ffn_grads_scatter_wait")

    small_w = (norm1_g, norm2_g, norm_f_g, b_forget, pool_scale, pool_mix)
    small_m = (m_norm1_g, m_norm2_g, m_norm_f_g, m_b_forget, m_pool_scale, m_pool_mix)
    small_v = (v_norm1_g, v_norm2_g, v_norm_f_g, v_b_forget, v_pool_scale, v_pool_mix)
    zero_row = jnp.zeros((8, LANES), F32)
    send_sems, recv_sems, srcs, lands, _ = chip_sems
    _, (recv_in,) = _exchange_wait(send_sems, recv_sems, srcs, lands, updates_ffn[-1][0], "w_in_grads_chips_wait", "chips")
    update_in = _shard_update(send_in, got_in, recv_in, w_in, m_w_in, v_w_in, pos, "update_w_in")

    def gathered_small(key, after, name):
        send_sems, recv_sems, srcs, lands, _ = started[key]
        srcs, lands = _exchange_wait(send_sems, recv_sems, srcs, lands, after, name, "gather")
        return with_own(lands, srcs)[0]

    parts = gathered_small("small", update_in[0], "small_grads_gather_wait")
    first_rows = gathered_small("norm1", parts, "norm1_grad_gather_wait")
    g_s, d_s, nm_s, nv_s = _small_update(parts, first_rows, _pack_small(small_w, zero_row), _pack_small(small_m, zero_row), _pack_small(small_v, zero_row))
    g_w, d_w, nm_w, nv_w = zip(*(
        [stored(t, tr) for t in u] for u, tr in zip([update_in] + updates_out + updates_ffn, _TRANSPOSED)
    ))
    shapes = [t.shape for t in small_w]
    (g1, g2, gf, gb, gsc, gmix), loss = _unpack_small(g_s, shapes)
    (d1, d2, df, db_, dsc, dmx), _ = _unpack_small(d_s, shapes)
    (m1, m2, mf, mb, msc, mmx), _ = _unpack_small(nm_s, shapes)
    (v1, v2, vf, vb, vsc, vmx), _ = _unpack_small(nv_s, shapes)

    def ordered(n1, win, b, mix, sc, wpo, wao, wout, n2, wg, wu, wd, nf):
        return (n1, win, b, mix, sc, wpo, wao, wout, n2, wg, wu, wd, nf)

    grads = ordered(g1, g_w[0], gb, gmix, gsc, g_w[1], g_w[2], g_w[3], g2, g_w[4], g_w[5], g_w[6], gf)
    deltas = ordered(d1, d_w[0], db_, dmx, dsc, d_w[1], d_w[2], d_w[3], d2, d_w[4], d_w[5], d_w[6], df)
    new_m = ordered(m1, nm_w[0], mb, mmx, msc, nm_w[1], nm_w[2], nm_w[3], m2, nm_w[4], nm_w[5], nm_w[6], mf)
    new_v = ordered(v1, nv_w[0], vb, vmx, vsc, nv_w[1], nv_w[2], nv_w[3], v2, nv_w[4], nv_w[5], nv_w[6], vf)
    return (loss, grad_x, *grads, *deltas, *new_m, *new_v)
```

```python
import jax
import jax.numpy as jnp
from jax import lax
from jax.experimental import pallas as pl
from jax.experimental.pallas import tpu as pltpu

F32 = jnp.float32
BF16 = jnp.bfloat16
MESH = pl.DeviceIdType.MESH

D_MODEL = 1024
POOL_WINDOWS = (2, 4, 8, 16)
POOL_WIDTH = 512
GROUP_DIM = 128
ATTN_WIDTH = 512
HEAD_DIM = 64
N_HEADS = 8
N_PAIRS = 4
D_FF = 2816
RMS_EPS = 1e-6
N_DEV = 8
LANES = 128
FL_PAD = 128

ADAM_LR = 0.001
ADAM_B1 = 0.9
ADAM_B2 = 0.999
ADAM_EPS = 1e-08
ADAM_WD = 0.01
ADAM_STEP = 10

VMEM_LIMIT = 56 * 1024 * 1024
VMEM_LIMIT_MAX = 60 * 1024 * 1024
ROW_TILE = 512
ATTN_BLOCK = 512
FF_CHUNK = 256
FF_ROW_TILE = 512
DW_TOKENS = 2048


def _mm(a, b):
    return jnp.dot(a, b, preferred_element_type=F32)


def _mm_nt(a, b):
    return lax.dot_general(a, b, (((1,), (1,)), ((), ())), preferred_element_type=F32)


def _mm_tn(a, b):
    return lax.dot_general(a, b, (((0,), (0,)), ((), ())), preferred_element_type=F32)


def _sigmoid(x):
    return 1.0 / (1.0 + jnp.exp(-x))


def _params(sem, vmem=VMEM_LIMIT):
    return pltpu.CompilerParams(dimension_semantics=sem, vmem_limit_bytes=vmem)


def _const_spec(shape):
    nd = len(shape)
    return pl.BlockSpec(shape, lambda *_: (0,) * nd, pipeline_mode=pl.Buffered(1))


def _rms_fwd(x, g):
    r = lax.rsqrt(jnp.mean(x * x, axis=-1, keepdims=True) + RMS_EPS)
    xh = x * r
    return xh * g, xh, r


def _rms_bwd(dy, xh, r, g):
    dxh = dy * g
    dx = r * (dxh - xh * jnp.mean(dxh * xh, axis=-1, keepdims=True))
    return dx, dy * xh


def _in_proj(x, g1, w_uqkv, w_fl, w_g, token):
    T = x.shape[0]
    tm = ROW_TILE

    def body(x_ref, g_ref, wa_ref, wf_ref, wg_ref, token_ref, h_ref, u_ref, qkv_ref, fl_ref, gt_ref):
        h, _, _ = _rms_fwd(x_ref[...], g_ref[...])
        hb = h.astype(BF16)
        h_ref[...] = hb
        z = _mm(hb, wa_ref[...])
        u_ref[...] = z[:, :POOL_WIDTH]
        qkv_ref[...] = z[:, POOL_WIDTH:].astype(BF16)
        fl_ref[...] = _mm(hb, wf_ref[...])
        gt_ref[...] = _mm(hb, wg_ref[...]).astype(BF16)

    row = lambda n: pl.BlockSpec((tm, n), lambda i: (i, 0))
    return pl.pallas_call(
        body,
        name="in_proj",
        grid=(T // tm,),
        in_specs=[row(D_MODEL), _const_spec((1, D_MODEL)), _const_spec(w_uqkv.shape), _const_spec(w_fl.shape), _const_spec(w_g.shape), _HBM],
        out_specs=[row(D_MODEL), row(POOL_WIDTH), row(3 * ATTN_WIDTH), row(FL_PAD), row(2 * D_MODEL)],
        out_shape=[
            jax.ShapeDtypeStruct((T, D_MODEL), BF16),
            jax.ShapeDtypeStruct((T, POOL_WIDTH), F32),
            jax.ShapeDtypeStruct((T, 3 * ATTN_WIDTH), BF16),
            jax.ShapeDtypeStruct((T, FL_PAD), F32),
            jax.ShapeDtypeStruct((T, 2 * D_MODEL), BF16),
        ],
        compiler_params=_params(("parallel",)),
    )(x, g1, w_uqkv, w_fl, w_g, token)


def _log_sigmoid(x):
    return jnp.minimum(x, 0.0) - jnp.log(1.0 + jnp.exp(-jnp.abs(x)))


def _forget_fwd(fl, b_pad, n_seq, S):
    def body(fl_ref, b_ref, fcol_ref):
        lf = _log_sigmoid(fl_ref[...] + b_ref[...])
        t = lf.T
        lane = lax.broadcasted_iota(jnp.int32, t.shape, 1)
        k = 1
        while k < S:
            t = t + jnp.where(lane >= k, pltpu.roll(t, k, 1), 0.0)
            k *= 2
        fcol_ref[...] = t.T

    return pl.pallas_call(
        body,
        name="forget_fwd",
        grid=(n_seq,),
        in_specs=[pl.BlockSpec((S, FL_PAD), lambda s: (s, 0)), _const_spec((1, FL_PAD))],
        out_specs=pl.BlockSpec((S, FL_PAD), lambda s: (s, 0)),
        out_shape=jax.ShapeDtypeStruct((n_seq * S, FL_PAD), F32),
        compiler_params=_params(("parallel",)),
    )(fl, b_pad)


def _window_pick(g, v2, v4, v8, v16):
    return jnp.where(g == 0, v2, jnp.where(g == 1, v4, jnp.where(g == 2, v8, v16)))


def _pool_fwd(u, mix_b, scale, n_seq, S):
    T = n_seq * S

    def body(u_ref, mix_ref, sc_ref, pm_ref, p2_ref, p3_ref):
        g = pl.program_id(1)
        uu = u_ref[...]
        row = lax.broadcasted_iota(jnp.int32, uu.shape, 0)

        def back(a, k):
            return jnp.where(row >= k, pltpu.roll(a, k, 0), 0.0)

        s2 = uu + back(uu, 1)
        s4 = s2 + back(s2, 2)
        s8 = s4 + back(s4, 4)
        s16 = s8 + back(s8, 8)
        w = _window_pick(g, 2.0, 4.0, 8.0, 16.0)
        cnt = jnp.minimum((row + 1).astype(F32), w)
        pm = _window_pick(g, s2, s4, s8, s16) / cnt - uu
        pmb = pm.astype(BF16)
        pm_ref[...] = pmb
        p2 = _mm(pmb, mix_ref[...])
        p2_ref[...] = p2
        p3_ref[...] = (p2 * sc_ref[...]).astype(BF16)

    grp = pl.BlockSpec((S, GROUP_DIM), lambda s, g: (s, g))
    return pl.pallas_call(
        body,
        name="pool_fwd",
        grid=(n_seq, len(POOL_WINDOWS)),
        in_specs=[
            grp,
            pl.BlockSpec((None, GROUP_DIM, GROUP_DIM), lambda s, g: (g, 0, 0)),
            pl.BlockSpec((1, GROUP_DIM), lambda s, g: (0, g)),
        ],
        out_specs=[grp, grp, grp],
        out_shape=[
            jax.ShapeDtypeStruct((T, POOL_WIDTH), BF16),
            jax.ShapeDtypeStruct((T, POOL_WIDTH), F32),
            jax.ShapeDtypeStruct((T, POOL_WIDTH), BF16),
        ],
        compiler_params=_params(("parallel", "parallel")),
    )(u, mix_b, scale)


def _split3(v):
    hi = v.astype(BF16).astype(F32)
    r = v - hi
    mid = r.astype(BF16).astype(F32)
    lo = (r - mid).astype(BF16).astype(F32)
    return hi, mid, lo


def _bias_lanes(v):
    hi, mid, lo = _split3(v)
    lane = lax.broadcasted_iota(jnp.int32, (1, LANES), 1)
    packed = jnp.where(lane < N_HEADS, hi, jnp.where(lane < 2 * N_HEADS, pltpu.roll(mid, N_HEADS, 1), pltpu.roll(lo, 2 * N_HEADS, 1)))
    return jnp.where(lane < 3 * N_HEADS, packed, 0.0).astype(BF16)


def _bias_placement(slot):
    row = lax.broadcasted_iota(jnp.int32, (LANES, N_HEADS * LANES), 0)
    col = lax.broadcasted_iota(jnp.int32, (LANES, N_HEADS * LANES), 1)
    h = col // LANES
    n = col % LANES - jnp.where(h % 2 == 0, HEAD_DIM, 0) - 3 * slot
    return ((n >= 0) & (n < 3) & (row == N_HEADS * n + h)).astype(BF16)


def _augment(xp, h, bias, ones_slot):
    lane = lax.broadcasted_iota(jnp.int32, (1, LANES), 1)
    hh = h % 2
    head = (lane >= HEAD_DIM * hh) & (lane < HEAD_DIM * (hh + 1))
    b = HEAD_DIM * (1 - hh)
    rest = jnp.zeros_like(xp) if bias is None else bias[:, h * LANES : (h + 1) * LANES]
    out = jnp.where(head, xp, rest)
    if ones_slot is not None:
        out = jnp.where((lane >= b + 3 * ones_slot) & (lane < b + 3 * ones_slot + 3), jnp.ones_like(xp), out)
    return out


def _attn_fwd(qkv, fcol, n_seq, S):
    T = n_seq * S
    tb = ATTN_BLOCK
    nq = S // tb
    scale = HEAD_DIM ** -0.5

    def body(q_ref, k_ref, v_ref, fc_ref, o_ref, st_ref, qa_sc, ka_sc, m_sc, l_sc, acc_sc):
        i = pl.program_id(1)
        lane = lax.broadcasted_iota(jnp.int32, (1, LANES), 1)
        low = lane < HEAD_DIM

        @pl.when(i == 0)
        def _():
            place = _bias_placement(1)

            def rows_ka(r, carry):
                r0 = pl.multiple_of(r * tb, tb)
                bias = _mm(_bias_lanes(-fc_ref[pl.ds(r0, tb), :]), place).astype(BF16)
                for h in range(N_HEADS):
                    kp = k_ref[pl.ds(r0, tb), (h // 2) * LANES : (h // 2 + 1) * LANES] * scale
                    ka_sc[h, pl.ds(r0, tb), :] = _augment(kp, h, bias, 0)
                return carry

            lax.fori_loop(0, nq, rows_ka, 0)

        q0 = pl.multiple_of(i * tb, tb)
        bias = _mm(_bias_lanes(fc_ref[pl.ds(q0, tb), :]), _bias_placement(0)).astype(BF16)
        for h in range(N_HEADS):
            qa_sc[h] = _augment(q_ref[:, (h // 2) * LANES : (h // 2 + 1) * LANES], h, bias, 1)
        m_sc[...] = jnp.full(m_sc.shape, -jnp.inf, F32)
        l_sc[...] = jnp.zeros_like(l_sc)
        acc_sc[...] = jnp.zeros_like(acc_sc)
        causal = lax.broadcasted_iota(jnp.int32, (tb, tb), 1) <= lax.broadcasted_iota(jnp.int32, (tb, tb), 0)

        def step(j, masked):
            c0 = pl.multiple_of(j * tb, tb)
            for p in range(N_PAIRS):
                vb = v_ref[pl.ds(c0, tb), p * LANES : (p + 1) * LANES]
                pv, al = [], []
                for hh in range(2):
                    h = 2 * p + hh
                    s = _mm_nt(qa_sc[h], ka_sc[h, pl.ds(c0, tb), :])
                    if masked:
                        s = jnp.where(causal, s, -jnp.inf)
                    m_old = m_sc[h]
                    m_new = jnp.maximum(m_old, jnp.max(s, axis=1, keepdims=True))
                    alpha = jnp.exp(m_old - m_new)
                    pe = jnp.exp(s - jnp.concatenate([m_new] * (tb // LANES), axis=1))
                    l_sc[h] = alpha * l_sc[h] + jnp.sum(pe, axis=1, keepdims=True)
                    m_sc[h] = m_new
                    pv.append(_mm(pe.astype(BF16), vb))
                    al.append(alpha)
                acc_sc[p] = jnp.where(low, al[0], al[1]) * acc_sc[p] + jnp.where(low, pv[0], pv[1])

        def loop_body(j, carry):
            step(j, False)
            return carry

        lax.fori_loop(0, i, loop_body, 0)
        step(i, True)
        st = jnp.zeros((tb, LANES), F32)
        for p in range(N_PAIRS):
            lp = jnp.where(low, l_sc[2 * p], l_sc[2 * p + 1])
            o_ref[:, p * LANES : (p + 1) * LANES] = (acc_sc[p] / lp).astype(BF16)
            for h in (2 * p, 2 * p + 1):
                st = jnp.where(lane == h, m_sc[h] + jnp.log(l_sc[h]), st)
        st_ref[...] = st

    return pl.pallas_call(
        body,
        name="attn_fwd",
        grid=(n_seq, nq),
        in_specs=[
            pl.BlockSpec((tb, ATTN_WIDTH), lambda s, i: (s * nq + i, 0)),
            pl.BlockSpec((S, ATTN_WIDTH), lambda s, i: (s, 1)),
            pl.BlockSpec((S, ATTN_WIDTH), lambda s, i: (s, 2)),
            pl.BlockSpec((S, LANES), lambda s, i: (s, 0)),
        ],
        out_specs=[
            pl.BlockSpec((tb, ATTN_WIDTH), lambda s, i: (s * nq + i, 0)),
            pl.BlockSpec((tb, LANES), lambda s, i: (s * nq + i, 0)),
        ],
        out_shape=[jax.ShapeDtypeStruct((T, ATTN_WIDTH), BF16), jax.ShapeDtypeStruct((T, LANES), F32)],
        scratch_shapes=[
            pltpu.VMEM((N_HEADS, tb, LANES), BF16),
            pltpu.VMEM((N_HEADS, S, LANES), BF16),
            pltpu.VMEM((N_HEADS, tb, LANES), F32),
            pltpu.VMEM((N_HEADS, tb, LANES), F32),
            pltpu.VMEM((N_PAIRS, tb, LANES), F32),
        ],
        compiler_params=_params(("parallel", "arbitrary")),
    )(qkv, qkv, qkv, fcol)


def _mix_out(a, p3, gates, x, w_ao, w_po, w_out):
    T = x.shape[0]
    tm = ROW_TILE

    def body(a_ref, p3_ref, gt_ref, x_ref, wao_ref, wpo_ref, wout_ref, mg_ref, x1_ref, ay_ref, py_ref):
        ay = _mm(a_ref[...], wao_ref[...])
        py = _mm(p3_ref[...], wpo_ref[...])
        ay_ref[...] = ay.astype(BF16)
        py_ref[...] = py.astype(BF16)
        sp = _sigmoid(gt_ref[:, :D_MODEL].astype(F32))
        sa = _sigmoid(gt_ref[:, D_MODEL:].astype(F32))
        mb = (sp * py + sa * ay).astype(BF16)
        mg_ref[...] = mb
        x1_ref[...] = x_ref[...] + _mm(mb, wout_ref[...])

    row = lambda n: pl.BlockSpec((tm, n), lambda i: (i, 0))
    return pl.pallas_call(
        body,
        name="mix_out",
        grid=(T // tm,),
        in_specs=[
            row(ATTN_WIDTH), row(POOL_WIDTH), row(2 * D_MODEL), row(D_MODEL),
            _const_spec(w_ao.shape), _const_spec(w_po.shape), _const_spec(w_out.shape),
        ],
        out_specs=[row(D_MODEL), row(D_MODEL), row(D_MODEL), row(D_MODEL)],
        out_shape=[
            jax.ShapeDtypeStruct((T, D_MODEL), BF16), jax.ShapeDtypeStruct((T, D_MODEL), F32),
            jax.ShapeDtypeStruct((T, D_MODEL), BF16), jax.ShapeDtypeStruct((T, D_MODEL), BF16),
        ],
        compiler_params=_params(("parallel",)),
    )(a, p3, gates, x, w_ao, w_po, w_out)


def _ffn_fwd(x1, g2, gf, tgt, w_gate_t, w_up_t, w_down):
    T = x1.shape[0]
    tm = min(T, FF_ROW_TILE)
    nt = T // tm
    nc = D_FF // FF_CHUNK

    def body(x1_ref, g2_ref, gf_ref, tg_ref, wg_ref, wu_ref, wd_ref, h2_ref, gate_ref, up_ref, act_ref, dx2_ref, loss_ref, dgf_ref):
        x1v = x1_ref[...]
        h2, _, _ = _rms_fwd(x1v, g2_ref[...])
        h2b = h2.astype(BF16)
        h2_ref[...] = h2b
        for c in range(nc):
            sl = slice(c * FF_CHUNK, (c + 1) * FF_CHUNK)
            gate = _mm_nt(h2b, wg_ref[sl, :])
            up = _mm_nt(h2b, wu_ref[sl, :])
            gate_ref[:, sl] = gate.astype(BF16)
            up_ref[:, sl] = up.astype(BF16)
            act_ref[:, sl] = (gate * _sigmoid(gate) * up).astype(BF16)
        acc = x1v + _mm(act_ref[...], wd_ref[...])
        gfv = gf_ref[...]
        y, xh, r = _rms_fwd(acc, gfv)
        err = y - tg_ref[...]
        part = 0.5 * jnp.sum(jnp.mean(err * err, axis=-1, keepdims=True), axis=0, keepdims=True)
        dx2, dgrow = _rms_bwd(err * (1.0 / D_MODEL), xh, r, gfv)
        dx2_ref[...] = dx2

        @pl.when(pl.program_id(0) == 0)
        def _():
            dgf_ref[...] = jnp.zeros_like(dgf_ref)
            loss_ref[...] = jnp.zeros_like(loss_ref)

        dgf_ref[...] += jnp.sum(dgrow, axis=0, keepdims=True)
        loss_ref[...] += jnp.broadcast_to(part, loss_ref.shape)

    row = lambda n: pl.BlockSpec((tm, n), lambda i: (i, 0))
    return pl.pallas_call(
        body,
        name="ffn_fwd",
        grid=(nt,),
        in_specs=[
            row(D_MODEL), _const_spec((1, D_MODEL)), _const_spec((1, D_MODEL)), row(D_MODEL),
            _const_spec(w_gate_t.shape), _const_spec(w_up_t.shape), _const_spec(w_down.shape),
        ],
        out_specs=[
            row(D_MODEL), row(D_FF), row(D_FF), row(D_FF), row(D_MODEL),
            pl.BlockSpec((8, LANES), lambda i: (0, 0)),
            pl.BlockSpec((1, D_MODEL), lambda i: (0, 0)),
        ],
        out_shape=[
            jax.ShapeDtypeStruct((T, D_MODEL), BF16),
            jax.ShapeDtypeStruct((T, D_FF), BF16),
            jax.ShapeDtypeStruct((T, D_FF), BF16),
            jax.ShapeDtypeStruct((T, D_FF), BF16),
            jax.ShapeDtypeStruct((T, D_MODEL), F32),
            jax.ShapeDtypeStruct((8, LANES), F32),
            jax.ShapeDtypeStruct((1, D_MODEL), F32),
        ],
        compiler_params=_params(("arbitrary",)),
    )(x1, g2, gf, tgt, w_gate_t, w_up_t, w_down)


def _ffn_bwd(dx2, gate, up, x1, g2, w_gate_t, w_up_t, w_down):
    T = x1.shape[0]
    tm = min(T, FF_ROW_TILE)
    nc = D_FF // FF_CHUNK

    def body(dx2_ref, gate_ref, up_ref, x1_ref, g2_ref, wg_ref, wu_ref, wd_ref, dgate_ref, dup_ref, dx1_ref, dg2_ref):
        dx2v = dx2_ref[...]
        dx2b = dx2v.astype(BF16)
        for c in range(nc):
            sl = slice(c * FF_CHUNK, (c + 1) * FF_CHUNK)
            dact = _mm_nt(dx2b, wd_ref[sl, :])
            gate = gate_ref[:, sl].astype(F32)
            sg = _sigmoid(gate)
            silu = gate * sg
            dgate = (dact * up_ref[:, sl].astype(F32) * (sg * (1.0 + gate * (1.0 - sg)))).astype(BF16)
            dup = (dact * silu).astype(BF16)
            dgate_ref[:, sl] = dgate
            dup_ref[:, sl] = dup
        dh2 = _mm(dgate_ref[...], wg_ref[...]) + _mm(dup_ref[...], wu_ref[...])
        g2v = g2_ref[...]
        _, xh, r = _rms_fwd(x1_ref[...], g2v)
        dxn, dgrow = _rms_bwd(dh2, xh, r, g2v)
        dx1_ref[...] = dx2v + dxn

        @pl.when(pl.program_id(0) == 0)
        def _():
            dg2_ref[...] = jnp.zeros_like(dg2_ref)

        dg2_ref[...] += jnp.sum(dgrow, axis=0, keepdims=True)

    row = lambda n: pl.BlockSpec((tm, n), lambda i: (i, 0))
    return pl.pallas_call(
        body,
        name="ffn_bwd",
        grid=(T // tm,),
        in_specs=[
            row(D_MODEL), row(D_FF), row(D_FF), row(D_MODEL), _const_spec((1, D_MODEL)),
            _const_spec(w_gate_t.shape), _const_spec(w_up_t.shape), _const_spec(w_down.shape),
        ],
        out_specs=[row(D_FF), row(D_FF), row(D_MODEL), pl.BlockSpec((1, D_MODEL), lambda i: (0, 0))],
        out_shape=[
            jax.ShapeDtypeStruct((T, D_FF), BF16),
            jax.ShapeDtypeStruct((T, D_FF), BF16),
            jax.ShapeDtypeStruct((T, D_MODEL), F32),
            jax.ShapeDtypeStruct((1, D_MODEL), F32),
        ],
        compiler_params=_params(("arbitrary",), VMEM_LIMIT_MAX),
    )(dx2, gate, up, x1, g2, w_gate_t, w_up_t, w_down)


def _mix_bwd(dx1, gates, pool_y, attn_y, p2, scale, w_out, w_ao, w_po, token):
    T = dx1.shape[0]
    tm = ROW_TILE

    def body(dx1_ref, gt_ref, py_ref, ay_ref, p2_ref, sc_ref, wout_ref, wao_ref, wpo_ref, token_ref, dgt_ref, dpy_ref, day_ref, da_ref, dp2_ref, dsc_ref):
        dm = _mm_nt(dx1_ref[...].astype(BF16), wout_ref[...])
        sp = _sigmoid(gt_ref[:, :D_MODEL].astype(F32))
        sa = _sigmoid(gt_ref[:, D_MODEL:].astype(F32))
        dgt_ref[:, :D_MODEL] = (dm * py_ref[...].astype(F32) * (sp * (1.0 - sp))).astype(BF16)
        dgt_ref[:, D_MODEL:] = (dm * ay_ref[...].astype(F32) * (sa * (1.0 - sa))).astype(BF16)
        dpy = (dm * sp).astype(BF16)
        day = (dm * sa).astype(BF16)
        dpy_ref[...] = dpy
        day_ref[...] = day
        da_ref[...] = _mm_nt(day, wao_ref[...]).astype(BF16)
        dp3 = _mm_nt(dpy, wpo_ref[...])
        dp2_ref[...] = (dp3 * sc_ref[...]).astype(BF16)

        @pl.when(pl.program_id(0) == 0)
        def _():
            dsc_ref[...] = jnp.zeros_like(dsc_ref)

        dsc_ref[...] += jnp.sum(dp3 * p2_ref[...], axis=0, keepdims=True)

    row = lambda n: pl.BlockSpec((tm, n), lambda i: (i, 0))
    return pl.pallas_call(
        body,
        name="mix_bwd",
        grid=(T // tm,),
        in_specs=[
            row(D_MODEL), row(2 * D_MODEL), row(D_MODEL), row(D_MODEL), row(POOL_WIDTH), _const_spec((1, POOL_WIDTH)),
            _const_spec(w_out.shape), _const_spec(w_ao.shape), _const_spec(w_po.shape), _HBM,
        ],
        out_specs=[row(2 * D_MODEL), row(D_MODEL), row(D_MODEL), row(ATTN_WIDTH), row(POOL_WIDTH), pl.BlockSpec((1, POOL_WIDTH), lambda i: (0, 0))],
        out_shape=[
            jax.ShapeDtypeStruct((T, 2 * D_MODEL), BF16),
            jax.ShapeDtypeStruct((T, D_MODEL), BF16),
            jax.ShapeDtypeStruct((T, D_MODEL), BF16),
            jax.ShapeDtypeStruct((T, ATTN_WIDTH), BF16),
            jax.ShapeDtypeStruct((T, POOL_WIDTH), BF16),
            jax.ShapeDtypeStruct((1, POOL_WIDTH), F32),
        ],
        compiler_params=_params(("arbitrary",)),
    )(dx1, gates, pool_y, attn_y, p2, scale, w_out, w_ao, w_po, token)


def _pool_bwd(dp2, pm, mix_b, token, n_seq, S):
    T = n_seq * S

    def body(dp2_ref, pm_ref, mix_ref, token_ref, du_ref, dmix_ref):
        g = pl.program_id(0)
        dp2v = dp2_ref[...]
        dpm = _mm_nt(dp2v, mix_ref[...])
        row = lax.broadcasted_iota(jnp.int32, dpm.shape, 0)
        w = _window_pick(g, 2.0, 4.0, 8.0, 16.0)
        e = dpm / jnp.minimum((row + 1).astype(F32), w)

        def ahead(a, k):
            return jnp.where(row < S - k, pltpu.roll(a, S - k, 0), 0.0)

        r2 = e + ahead(e, 1)
        r4 = r2 + ahead(r2, 2)
        r8 = r4 + ahead(r4, 4)
        r16 = r8 + ahead(r8, 8)
        du_ref[...] = (_window_pick(g, r2, r4, r8, r16) - dpm).astype(BF16)

        @pl.when(pl.program_id(1) == 0)
        def _():
            dmix_ref[...] = jnp.zeros_like(dmix_ref)

        dmix_ref[...] += _mm_tn(pm_ref[...], dp2v)

    grp = pl.BlockSpec((S, GROUP_DIM), lambda g, s: (s, g))
    mixs = pl.BlockSpec((None, GROUP_DIM, GROUP_DIM), lambda g, s: (g, 0, 0))
    return pl.pallas_call(
        body,
        name="pool_bwd",
        grid=(len(POOL_WINDOWS), n_seq),
        in_specs=[grp, grp, mixs, _HBM],
        out_specs=[grp, mixs],
        out_shape=[jax.ShapeDtypeStruct((T, POOL_WIDTH), BF16), jax.ShapeDtypeStruct((len(POOL_WINDOWS), GROUP_DIM, GROUP_DIM), F32)],
        compiler_params=_params(("parallel", "arbitrary")),
    )(dp2, pm, mix_b, token)


def _attn_bwd(qkv, da, a, fcol, lse, n_seq, S):
    T = n_seq * S
    tb = ATTN_BLOCK
    nb = S // tb
    scale = HEAD_DIM ** -0.5

    def body(q_ref, k_ref, v_ref, do_ref, o_ref, fc_ref, st_ref, dq_ref, dk_ref, dv_ref, dfk_ref, dfq_ref,
             qa_sc, doa_sc, qat_sc, doat_sc, dq_acc, ka_sc, va_sc, dkt_sc, dvt_sc):
        j = pl.program_id(1)
        lane = lax.broadcasted_iota(jnp.int32, (1, LANES), 1)
        low = lane < HEAD_DIM

        @pl.when(j == 0)
        def _():
            dq_acc[...] = jnp.zeros_like(dq_acc)
            place = _bias_placement(0)

            def rows_q(i, carry):
                r0 = pl.multiple_of(i * tb, tb)
                delta = jnp.zeros((tb, LANES), F32)
                for h in range(N_HEADS):
                    pair = slice((h // 2) * LANES, (h // 2 + 1) * LANES)
                    prod = do_ref[pl.ds(r0, tb), pair].astype(F32) * o_ref[pl.ds(r0, tb), pair].astype(F32)
                    head = (lane >= HEAD_DIM * (h % 2)) & (lane < HEAD_DIM * (h % 2 + 1))
                    delta = jnp.where(lane == h, jnp.sum(jnp.where(head, prod, 0.0), axis=1, keepdims=True), delta)
                cq = fc_ref[pl.ds(r0, tb), :] - st_ref[pl.ds(r0, tb), :]
                q_bias = _mm(_bias_lanes(cq), place).astype(BF16)
                do_bias = _mm(_bias_lanes(-delta), place).astype(BF16)
                for h in range(N_HEADS):
                    pair = slice((h // 2) * LANES, (h // 2 + 1) * LANES)
                    qa = _augment(q_ref[pl.ds(r0, tb), pair], h, q_bias, 1)
                    doa = _augment(do_ref[pl.ds(r0, tb), pair], h, do_bias, None)
                    qa_sc[h, pl.ds(r0, tb), :] = qa
                    doa_sc[h, pl.ds(r0, tb), :] = doa
                    qat_sc[h, i] = qa.astype(F32).T.astype(BF16)
                    doat_sc[h, i] = doa.astype(F32).T.astype(BF16)
                return carry

            lax.fori_loop(0, nb, rows_q, 0)

        c0 = pl.multiple_of(j * tb, tb)
        k_bias = _mm(_bias_lanes(-fc_ref[pl.ds(c0, tb), :]), _bias_placement(1)).astype(BF16)
        for h in range(N_HEADS):
            pair = slice((h // 2) * LANES, (h // 2 + 1) * LANES)
            ka_sc[h] = _augment(k_ref[:, pair] * scale, h, k_bias, 0)
            va_sc[h] = _augment(v_ref[:, pair], h, None, 0)
        dkt_sc[...] = jnp.zeros_like(dkt_sc)
        dvt_sc[...] = jnp.zeros_like(dvt_sc)
        causal = lax.broadcasted_iota(jnp.int32, (tb, tb), 1) <= lax.broadcasted_iota(jnp.int32, (tb, tb), 0)

        def step(i, masked):
            r0 = pl.multiple_of(i * tb, tb)
            for h in range(N_HEADS):
                s = _mm_nt(qa_sc[h, pl.ds(r0, tb), :], ka_sc[h])
                if masked:
                    s = jnp.where(causal, s, -jnp.inf)
                pr = jnp.exp(s)
                dvt_sc[h] += _mm(doat_sc[h, i], pr.astype(BF16))
                dsb = (pr * _mm_nt(doa_sc[h, pl.ds(r0, tb), :], va_sc[h])).astype(BF16)
                dkt_sc[h] += _mm(qat_sc[h, i], dsb)
                dq_acc[h, pl.ds(r0, tb), :] += _mm(dsb, ka_sc[h])

        step(j, True)

        def loop_body(i, carry):
            step(i, False)
            return carry

        lax.fori_loop(j + 1, nb, loop_body, 0)
        dfk = jnp.zeros((tb, LANES), F32)
        for p in range(N_PAIRS):
            dk = [dkt_sc[2 * p + hh].T for hh in range(2)]
            dv = [dvt_sc[2 * p + hh].T for hh in range(2)]
            dk_ref[:, p * LANES : (p + 1) * LANES] = (jnp.where(low, dk[0], dk[1]) * scale).astype(BF16)
            dv_ref[:, p * LANES : (p + 1) * LANES] = jnp.where(low, dv[0], dv[1]).astype(BF16)
            for hh in range(2):
                b = HEAD_DIM * (1 - hh) + 3
                dfk = jnp.where(lane == 2 * p + hh, -dk[hh][:, b : b + 1], dfk)
        dfk_ref[...] = dfk

        @pl.when(j == nb - 1)
        def _():
            def rows_dq(i, carry):
                r0 = pl.multiple_of(i * tb, tb)
                dfq = jnp.zeros((tb, LANES), F32)
                for p in range(N_PAIRS):
                    parts = [dq_acc[2 * p + hh, pl.ds(r0, tb), :] for hh in range(2)]
                    dq_ref[pl.ds(r0, tb), p * LANES : (p + 1) * LANES] = jnp.where(low, parts[0], parts[1]).astype(BF16)
                    for hh in range(2):
                        b = HEAD_DIM * (1 - hh)
                        dfq = jnp.where(lane == 2 * p + hh, parts[hh][:, b : b + 1], dfq)
                dfq_ref[pl.ds(r0, tb), :] = dfq
                return carry

            lax.fori_loop(0, nb, rows_dq, 0)

    seq = lambda w, col: pl.BlockSpec((S, w), lambda s, j: (s, col))
    seq_in = lambda w, col: pl.BlockSpec((S, w), lambda s, j: (s, col), pipeline_mode=pl.Buffered(1))
    blk = lambda w, col: pl.BlockSpec((tb, w), lambda s, j: (s * nb + j, col))
    return pl.pallas_call(
        body,
        name="attn_bwd",
        grid=(n_seq, nb),
        in_specs=[seq_in(ATTN_WIDTH, 0), blk(ATTN_WIDTH, 1), blk(ATTN_WIDTH, 2), seq_in(ATTN_WIDTH, 0), seq_in(ATTN_WIDTH, 0), seq_in(LANES, 0), seq_in(LANES, 0)],
        out_specs=[seq(ATTN_WIDTH, 0), blk(ATTN_WIDTH, 0), blk(ATTN_WIDTH, 0), blk(LANES, 0), seq(LANES, 0)],
        out_shape=[
            jax.ShapeDtypeStruct((T, ATTN_WIDTH), BF16),
            jax.ShapeDtypeStruct((T, ATTN_WIDTH), BF16),
            jax.ShapeDtypeStruct((T, ATTN_WIDTH), BF16),
            jax.ShapeDtypeStruct((T, LANES), F32),
            jax.ShapeDtypeStruct((T, LANES), F32),
        ],
        scratch_shapes=[
            pltpu.VMEM((N_HEADS, S, LANES), BF16),
            pltpu.VMEM((N_HEADS, S, LANES), BF16),
            pltpu.VMEM((N_HEADS, nb, LANES, tb), BF16),
            pltpu.VMEM((N_HEADS, nb, LANES, tb), BF16),
            pltpu.VMEM((N_HEADS, S, LANES), F32),
            pltpu.VMEM((N_HEADS, tb, LANES), BF16),
            pltpu.VMEM((N_HEADS, tb, LANES), BF16),
            pltpu.VMEM((N_HEADS, LANES, tb), F32),
            pltpu.VMEM((N_HEADS, LANES, tb), F32),
        ],
        compiler_params=_params(("parallel", "arbitrary"), VMEM_LIMIT_MAX),
    )(qkv, qkv, qkv, da, a, fcol, lse)


def _forget_bwd(dfk, dfq, fl, b_pad, n_seq, S):
    def body(df_ref, dfq_ref, fl_ref, b_ref, dfl_ref, db_ref):
        t = (df_ref[...] + dfq_ref[...]).T
        lane = lax.broadcasted_iota(jnp.int32, t.shape, 1)
        k = 1
        while k < S:
            t = t + jnp.where(lane < S - k, pltpu.roll(t, S - k, 1), 0.0)
            k *= 2
        dfl = t.T * _sigmoid(-(fl_ref[...] + b_ref[...]))
        dfl_ref[...] = dfl.astype(BF16)

        @pl.when(pl.program_id(0) == 0)
        def _():
            db_ref[...] = jnp.zeros_like(db_ref)

        db_ref[...] += jnp.sum(dfl, axis=0, keepdims=True)

    return pl.pallas_call(
        body,
        name="forget_bwd",
        grid=(n_seq,),
        in_specs=[
            pl.BlockSpec((S, LANES), lambda s: (s, 0)),
            pl.BlockSpec((S, LANES), lambda s: (s, 0)),
            pl.BlockSpec((S, FL_PAD), lambda s: (s, 0)),
            _const_spec((1, FL_PAD)),
        ],
        out_specs=[pl.BlockSpec((S, FL_PAD), lambda s: (s, 0)), pl.BlockSpec((1, FL_PAD), lambda s: (0, 0))],
        out_shape=[jax.ShapeDtypeStruct((n_seq * S, FL_PAD), BF16), jax.ShapeDtypeStruct((1, FL_PAD), F32)],
        compiler_params=_params(("arbitrary",)),
    )(dfk, dfq, fl, b_pad)


def _in_proj_bwd(du, dq, dk, dv, dfl, dgates, x, dx1, g1, w_uqkv, w_fl, w_g, token):
    T = x.shape[0]
    tm = ROW_TILE

    def body(du_ref, dq_ref, dk_ref, dv_ref, dfl_ref, dgt_ref, x_ref, dx1_ref, g_ref, wa_ref, wf_ref, wg_ref, token_ref, dx_ref, dg_ref):
        dz = jnp.concatenate([du_ref[...], dq_ref[...], dk_ref[...], dv_ref[...]], axis=1)
        dh = _mm_nt(dz, wa_ref[...]) + _mm_nt(dgt_ref[...], wg_ref[...]) + _mm_nt(dfl_ref[...], wf_ref[...])
        gv = g_ref[...]
        _, xh, r = _rms_fwd(x_ref[...], gv)
        dxn, dgrow = _rms_bwd(dh, xh, r, gv)
        dx_ref[...] = dx1_ref[...] + dxn

        @pl.when(pl.program_id(0) == 0)
        def _():
            dg_ref[...] = jnp.zeros_like(dg_ref)

        dg_ref[...] += jnp.sum(dgrow, axis=0, keepdims=True)

    row = lambda n: pl.BlockSpec((tm, n), lambda i: (i, 0))
    return pl.pallas_call(
        body,
        name="in_proj_bwd",
        grid=(T // tm,),
        in_specs=[
            row(512), row(512), row(512), row(512), row(FL_PAD), row(2 * D_MODEL), row(D_MODEL), row(D_MODEL), _const_spec((1, D_MODEL)),
            _const_spec(w_uqkv.shape), _const_spec(w_fl.shape), _const_spec(w_g.shape), _HBM,
        ],
        out_specs=[row(D_MODEL), pl.BlockSpec((1, D_MODEL), lambda i: (0, 0))],
        out_shape=[jax.ShapeDtypeStruct((T, D_MODEL), F32), jax.ShapeDtypeStruct((1, D_MODEL), F32)],
        compiler_params=_params(("arbitrary",)),
    )(du, dq, dk, dv, dfl, dgates, x, dx1, g1, w_uqkv, w_fl, w_g, token)


def _pick_block(n):
    for b in (1024, 512, 1408, 256, 128):
        if n % b == 0:
            return b
    raise ValueError(n)


def _matmul_tn(a, b, name):
    T, K = a.shape
    N = b.shape[1]
    bt, bk, bn = min(T, DW_TOKENS), _pick_block(K), _pick_block(N)
    nt = T // bt

    def body(a_ref, b_ref, o_ref, acc):
        @pl.when(pl.program_id(2) == 0)
        def _():
            acc[...] = jnp.zeros_like(acc)

        acc[...] += _mm_tn(a_ref[...].astype(BF16), b_ref[...].astype(BF16))

        @pl.when(pl.program_id(2) == nt - 1)
        def _():
            o_ref[...] = acc[...].astype(BF16)

    return pl.pallas_call(
        body,
        name=name,
        grid=(K // bk, N // bn, nt),
        in_specs=[pl.BlockSpec((bt, bk), lambda k, n, t: (t, k)), pl.BlockSpec((bt, bn), lambda k, n, t: (t, n))],
        out_specs=pl.BlockSpec((bk, bn), lambda k, n, t: (k, n)),
        out_shape=jax.ShapeDtypeStruct((K, N), BF16),
        scratch_shapes=[pltpu.VMEM((bk, bn), F32)],
        compiler_params=_params(("parallel", "parallel", "arbitrary")),
    )(a, b)


W_IN_A = POOL_WIDTH + 3 * ATTN_WIDTH
W_IN_SHARD = (W_IN_A + N_HEADS + 2 * D_MODEL) // N_DEV
_W_IN_PIECES = ((0, W_IN_A), (W_IN_A, W_IN_A + N_HEADS), (W_IN_A + N_HEADS, W_IN_A + N_HEADS + 2 * D_MODEL))


def _w_in_segments(d):
    lo, hi = d * W_IN_SHARD, (d + 1) * W_IN_SHARD
    out = []
    for p, (a, b) in enumerate(_W_IN_PIECES):
        s, e = max(lo, a), min(hi, b)
        if s < e:
            out.append((p, s - a, s - lo, e - s))
    return out


def _w_in_pieces(gathered):
    tm = ROW_TILE // 2

    def body(g_ref, wa_ref, wf_ref, wg_ref):
        outs = (wa_ref, wf_ref, wg_ref)
        wf_ref[...] = jnp.zeros_like(wf_ref)
        for d in range(N_DEV):
            for p, at, frm, n in _w_in_segments(d):
                outs[p][:, at : at + n] = g_ref[d, :, frm : frm + n]

    return pl.pallas_call(
        body,
        name="w_in_pieces",
        grid=(D_MODEL // tm,),
        in_specs=[pl.BlockSpec((N_DEV, tm, W_IN_SHARD), lambda i: (0, i, 0))],
        out_specs=[pl.BlockSpec((tm, W_IN_A), lambda i: (i, 0)), pl.BlockSpec((tm, FL_PAD), lambda i: (i, 0)), pl.BlockSpec((tm, 2 * D_MODEL), lambda i: (i, 0))],
        out_shape=[
            jax.ShapeDtypeStruct((D_MODEL, W_IN_A), gathered.dtype),
            jax.ShapeDtypeStruct((D_MODEL, FL_PAD), gathered.dtype),
            jax.ShapeDtypeStruct((D_MODEL, 2 * D_MODEL), gathered.dtype),
        ],
        compiler_params=_params(("parallel",)),
    )(gathered)


def _dw_in(h, du, dq, dk, dv, dfl, dgates, token):
    T = h.shape[0]
    bt, bk = min(T, DW_TOKENS // 2), 512
    nt = T // bt
    pieces = (du, dq, dk, dv, dfl, dgates)
    offs = [0]
    for p in pieces:
        offs.append(offs[-1] + p.shape[1])

    def body(h_ref, *rest):
        refs, o_ref, acc = rest[: len(pieces)], rest[-2], rest[-1]

        @pl.when(pl.program_id(1) == 0)
        def _():
            acc[...] = jnp.zeros_like(acc)

        ht = h_ref[...].T
        for ref, at in zip(refs, offs):
            acc[:, at : at + ref.shape[1]] += _mm(ht, ref[...])

        @pl.when(pl.program_id(1) == nt - 1)
        def _():
            starts = (0, W_IN_A, W_IN_A + FL_PAD)
            for d in range(N_DEV):
                for p, at, to, n in _w_in_segments(d):
                    o_ref[d % 2, d // 2, :, to : to + n] = acc[:, starts[p] + at : starts[p] + at + n].astype(BF16)

    return pl.pallas_call(
        body,
        name="dw_in",
        grid=(D_MODEL // bk, nt),
        in_specs=[pl.BlockSpec((bt, bk), lambda k, t: (t, k))] + [pl.BlockSpec((bt, p.shape[1]), lambda k, t: (t, 0)) for p in pieces] + [_HBM],
        out_specs=pl.BlockSpec((2, 4, bk, W_IN_SHARD), lambda k, t: (0, 0, k, 0)),
        out_shape=jax.ShapeDtypeStruct((2, 4, D_MODEL, W_IN_SHARD), BF16),
        scratch_shapes=[pltpu.VMEM((bk, offs[-1]), F32)],
        compiler_params=_params(("parallel", "arbitrary")),
    )(h, *pieces, token)


def _position():
    return lax.axis_index("x"), lax.axis_index("y"), lax.axis_index("c")


_HBM = pl.BlockSpec(memory_space=pl.ANY)


def _all_gather(blocks, name):
    n = len(blocks)

    def body(*refs):
        xs, outs = refs[:n], refs[n : 2 * n]
        send_sems, recv_sems, local_sems = refs[2 * n :]
        x, y, c = _position()
        me, sibling = (x, y, c), (x, y, 1 - c)
        chips = [(1 - x, y), (x, 1 - y), (1 - x, 1 - y)]

        def rows(a, px, py, pc):
            return outs[a].at[4 * px + 2 * py + pc]

        def copy(a, k, blk, to, src=None):
            return pltpu.make_async_remote_copy(
                src_ref=rows(a, *blk) if src is None else src, dst_ref=rows(a, *blk),
                send_sem=send_sems.at[7 * a + k], recv_sem=recv_sems.at[7 * a + k], device_id=to, device_id_type=MESH,
            )

        mine = [pltpu.make_async_copy(xs[a], rows(a, *me), local_sems.at[a]) for a in range(n)]
        for cp in mine:
            cp.start()
        first = []
        for a in range(n):
            first.append(copy(a, 0, me, sibling, src=xs[a]))
            first += [copy(a, 1 + j, me, (*chip, c), src=xs[a]) for j, chip in enumerate(chips)]
        for cp in first:
            cp.start()
        passed = []
        for j, chip in enumerate(chips):
            for a in range(n):
                copy(a, 1 + j, (*chip, c), me).wait_recv()
                passed.append(copy(a, 4 + j, (*chip, c), sibling))
                passed[-1].start()
        for a in range(n):
            copy(a, 0, sibling, me).wait_recv()
        for j, chip in enumerate(chips):
            for a in range(n):
                copy(a, 4 + j, (*chip, 1 - c), me).wait_recv()
        for cp in first + passed:
            cp.wait_send()
        for cp in mine:
            cp.wait()

    return pl.pallas_call(
        body,
        name=name,
        out_shape=[jax.ShapeDtypeStruct((N_DEV, *b.shape), b.dtype) for b in blocks],
        in_specs=[_HBM] * n,
        out_specs=[_HBM] * n,
        scratch_shapes=[pltpu.SemaphoreType.DMA((7 * n,)), pltpu.SemaphoreType.DMA((7 * n,)), pltpu.SemaphoreType.DMA((n,))],
    )(*blocks)


_SEM = pl.BlockSpec(memory_space=pltpu.SEMAPHORE)
_HBM_ONLY = pl.BlockSpec(memory_space=pltpu.HBM)
_SIDE_EFFECT = pltpu.SideEffectType.DATAFLOW_SIDE_EFFECTING


def _peer(x, y, c, k):
    return (1 - x if k & 4 else x, 1 - y if k & 2 else y, 1 - c if k & 1 else c)


_PEER_BITS = {"gather": range(1, N_DEV), "scatter": range(1, N_DEV), "chips": (4, 2, 6)}
_LAND_SLOTS = {"gather": N_DEV, "scatter": N_DEV, "chips": 3}


def _exchange_copies(src_refs, land_refs, send_sems, recv_sems, pattern, receive_side):
    x, y, c = _position()
    me = 4 * x + 2 * y + c
    bits = _PEER_BITS[pattern]
    cps = []
    for j, k in enumerate(bits):
        px, py, pc = _peer(x, y, c, k)
        peer = 4 * px + 2 * py + pc
        for a, (src, land) in enumerate(zip(src_refs, land_refs)):
            if pattern == "chips":
                s, slot = src.at[2 * px + py], j
            else:
                s, slot = (src if pattern == "gather" else src.at[peer]), (peer if receive_side else me)
            cps.append(pltpu.make_async_remote_copy(
                src_ref=s, dst_ref=land.at[slot],
                send_sem=send_sems.at[len(bits) * a + j], recv_sem=recv_sems.at[len(bits) * a + j],
                device_id=(px, py, pc), device_id_type=MESH,
            ))
    return cps


def _exchange_start(srcs, after, name, pattern):
    n = len(srcs)
    m = len(_PEER_BITS[pattern])
    lands = [jax.ShapeDtypeStruct((_LAND_SLOTS[pattern], *s.shape[-2:]), s.dtype) for s in srcs]

    def body(*refs):
        src_refs, land_refs = refs[1 : 1 + n], refs[1 + n : 1 + 2 * n]
        send_sems, recv_sems = refs[1 + 2 * n], refs[2 + 2 * n]
        token = refs[-1]
        for cp in _exchange_copies(src_refs, land_refs, send_sems, recv_sems, pattern, receive_side=False):
            cp.start()
        token[...] = jnp.zeros_like(token)

    hbm = lambda t: pltpu.with_memory_space_constraint(t, pltpu.HBM)
    out = pl.pallas_call(
        body,
        name=name,
        out_shape=(
            pltpu.SemaphoreType.DMA((m * n,)), pltpu.SemaphoreType.DMA((m * n,)),
            *[pltpu.HBM(s.shape, s.dtype) for s in srcs], *[pltpu.HBM(l.shape, l.dtype) for l in lands],
            jax.ShapeDtypeStruct((8, LANES), F32),
        ),
        in_specs=(_HBM, *[_HBM_ONLY] * (2 * n)),
        out_specs=(_SEM, _SEM, *[_HBM_ONLY] * (2 * n), pl.BlockSpec(memory_space=pltpu.VMEM)),
        input_output_aliases={1 + i: 2 + i for i in range(2 * n)},
        compiler_params=pltpu.CompilerParams(has_side_effects=_SIDE_EFFECT),
    )(after, *[hbm(s) for s in srcs], *[hbm(lax.empty(l.shape, l.dtype)) for l in lands])
    return out[0], out[1], out[2 : 2 + n], out[2 + n : 2 + 2 * n], out[-1]


def _exchange_wait(send_sems, recv_sems, srcs, lands, after, name, pattern):
    n = len(srcs)

    def body(*refs):
        src_refs, land_refs = refs[:n], refs[n : 2 * n]
        for cp in _exchange_copies(src_refs, land_refs, refs[2 * n], refs[2 * n + 1], pattern, receive_side=True):
            cp.wait_send()
            cp.wait_recv()

    out = pl.pallas_call(
        body,
        name=name,
        out_shape=(*[pltpu.HBM(s.shape, s.dtype) for s in srcs], *[pltpu.HBM(l.shape, l.dtype) for l in lands]),
        in_specs=(*[_HBM_ONLY] * (2 * n), _SEM, _SEM, _HBM),
        out_specs=tuple([_HBM_ONLY] * (2 * n)),
        input_output_aliases={i: i for i in range(2 * n)},
        compiler_params=pltpu.CompilerParams(has_side_effects=_SIDE_EFFECT),
    )(*srcs, *lands, send_sems, recv_sems, after)
    return out[:n], out[n:]


def _sibling_exchange(sends):
    n = len(sends)

    def body(*refs):
        srcs, dsts = refs[:n], refs[n : 2 * n]
        send_sems, recv_sems = refs[2 * n :]
        x, y, c = _position()
        cps = [
            pltpu.make_async_remote_copy(
                src_ref=srcs[a].at[1 - c], dst_ref=dsts[a], send_sem=send_sems.at[a], recv_sem=recv_sems.at[a],
                device_id=(x, y, 1 - c), device_id_type=MESH,
            )
            for a in range(n)
        ]
        for cp in cps:
            cp.start()
        for cp in cps:
            cp.wait()

    return pl.pallas_call(
        body,
        name="rs_sibling",
        out_shape=[jax.ShapeDtypeStruct(s.shape[1:], s.dtype) for s in sends],
        in_specs=[_HBM] * n,
        out_specs=[_HBM] * n,
        scratch_shapes=[pltpu.SemaphoreType.DMA((n,)), pltpu.SemaphoreType.DMA((n,))],
    )(*sends)


def _rows_tile(r):
    return ROW_TILE if r % ROW_TILE == 0 else r


def _pair_sum(send, got, core, name):
    _, _, r, c = send.shape
    br = _rows_tile(r)

    def body(core_ref, a_ref, b_ref, o_ref):
        o_ref[...] = (a_ref[...].astype(F32) + b_ref[...].astype(F32)).astype(o_ref.dtype)

    return pl.pallas_call(
        body,
        name=name,
        grid_spec=pltpu.PrefetchScalarGridSpec(
            num_scalar_prefetch=1,
            grid=(4, r // br),
            in_specs=[
                pl.BlockSpec((None, None, br, c), lambda n, i, core: (core[0], n, i, 0)),
                pl.BlockSpec((None, br, c), lambda n, i, core: (n, i, 0)),
            ],
            out_specs=pl.BlockSpec((None, br, c), lambda n, i, core: (n, i, 0)),
        ),
        out_shape=jax.ShapeDtypeStruct((4, r, c), send.dtype),
        compiler_params=_params(("parallel", "parallel")),
    )(core, send, got)


def _adamw(w, g, m, v):
    m = ADAM_B1 * m + (1.0 - ADAM_B1) * g
    v = ADAM_B2 * v + (1.0 - ADAM_B2) * (g * g)
    m_hat = m / (1.0 - ADAM_B1 ** ADAM_STEP)
    v_hat = v / (1.0 - ADAM_B2 ** ADAM_STEP)
    delta = -ADAM_LR * (m_hat / (jnp.sqrt(v_hat) + ADAM_EPS) + ADAM_WD * w)
    return delta, m, v


def _shard_update(send, got, recv, w, m, v, pos, name):
    _, r, c = w.shape
    br = _rows_tile(r)

    def body(pos_ref, a_ref, b_ref, r_ref, w_ref, m_ref, v_ref, g_ref, d_ref, nm_ref, nv_ref):
        g = a_ref[...].astype(F32) + b_ref[...].astype(F32)
        for n in range(3):
            g = g + r_ref[n].astype(F32)
        g_ref[...] = g
        d_ref[...], nm_ref[...], nv_ref[...] = _adamw(w_ref[...], g, m_ref[...], v_ref[...])

    own = pl.BlockSpec((None, br, c), lambda i, pos: (0, i, 0))
    return pl.pallas_call(
        body,
        name=name,
        grid_spec=pltpu.PrefetchScalarGridSpec(
            num_scalar_prefetch=1,
            grid=(r // br,),
            in_specs=[
                pl.BlockSpec((None, None, br, c), lambda i, pos: (pos[0], pos[1], i, 0)),
                pl.BlockSpec((None, br, c), lambda i, pos: (pos[1], i, 0)),
                pl.BlockSpec((3, br, c), lambda i, pos: (0, i, 0)),
                own, own, own,
            ],
            out_specs=[own, own, own, own],
        ),
        out_shape=[jax.ShapeDtypeStruct((1, r, c), F32)] * 4,
        compiler_params=_params(("parallel",)),
    )(pos, send, got, recv, w, m, v)


def _shard_update_direct(parts, chunks, w, m, v, me, name):
    _, r, c = w.shape
    br = _rows_tile(r)

    def body(me_ref, p_ref, own_ref, w_ref, m_ref, v_ref, g_ref, d_ref, nm_ref, nv_ref):
        g = None
        for n in range(N_DEV):
            part = jnp.where(me_ref[0] == n, own_ref[...], p_ref[n]).astype(F32)
            g = part if g is None else g + part
        g_ref[...] = g
        d_ref[...], nm_ref[...], nv_ref[...] = _adamw(w_ref[...], g, m_ref[...], v_ref[...])

    shard = pl.BlockSpec((None, br, c), lambda i, me: (0, i, 0))
    return pl.pallas_call(
        body,
        name=name,
        grid_spec=pltpu.PrefetchScalarGridSpec(
            num_scalar_prefetch=1,
            grid=(r // br,),
            in_specs=[
                pl.BlockSpec((N_DEV, br, c), lambda i, me: (0, i, 0)),
                pl.BlockSpec((None, br, c), lambda i, me: (me[0], i, 0)),
                shard, shard, shard,
            ],
            out_specs=[shard, shard, shard, shard],
        ),
        out_shape=[jax.ShapeDtypeStruct((1, r, c), F32)] * 4,
        compiler_params=_params(("parallel",)),
    )(me, parts, chunks, w, m, v)


def _small_update(parts, first_rows, w, m, v):
    R = w.shape[0]

    def body(p_ref, f_ref, w_ref, m_ref, v_ref, g_ref, d_ref, nm_ref, nv_ref):
        g, first = p_ref[0], f_ref[0]
        for n in range(1, N_DEV):
            g = g + p_ref[n]
            first = first + f_ref[n]
        g = jnp.concatenate([g[:8] + first, g[8:]], axis=0)
        g_ref[...] = g
        d_ref[...], nm_ref[...], nv_ref[...] = _adamw(w_ref[...], g, m_ref[...], v_ref[...])

    return pl.pallas_call(
        body,
        name="small_update",
        out_shape=[jax.ShapeDtypeStruct((R, LANES), F32)] * 4,
        compiler_params=pltpu.CompilerParams(vmem_limit_bytes=VMEM_LIMIT),
    )(parts, first_rows, w, m, v)


_SHARD_AXIS = (1, 1, 1, 0, 0, 0, 0)
_TRANSPOSED = (False, False, False, False, True, True, False)


def _full_from_gathered(t, axis):
    if axis == 0:
        return t.reshape(N_DEV * t.shape[1], t.shape[2])
    return jnp.concatenate([t[d] for d in range(N_DEV)], axis=1)


def _chunks_from_cols(t):
    c = t.shape[1] // N_DEV
    return jnp.stack([t[:, d * c : (d + 1) * c] for d in range(N_DEV)])


_SMALL = (("norm1_g", 8), ("norm2_g", 8), ("norm_f_g", 8), ("b_forget", 8), ("pool_scale", 8), ("pool_mix", 512))


def _pack_small(vals, loss_row):
    parts = []
    for (name, rows), t in zip(_SMALL, vals):
        f = t.astype(F32).reshape(-1)
        f = jnp.concatenate([f, jnp.zeros((rows * LANES - f.shape[0],), F32)]).reshape(rows, LANES)
        parts.append(f)
    parts.append(loss_row)
    return jnp.concatenate(parts, axis=0)


def _unpack_small(packed, shapes):
    out, off = [], 0
    for (name, rows), shape in zip(_SMALL, shapes):
        n = 1
        for s in shape:
            n *= s
        out.append(packed[off : off + rows].reshape(-1)[:n].reshape(shape))
        off += rows
    return out, packed[off, 0]


def _local_grads(x, tgt, g1, g2, gf, b_forget, pool_mix, pool_scale, w_in, fwd_token, out_weights, ffn_weights, ffn_grads_out, out_grads_out, small_grads_out, in_grads_out, norm1_grad_out):
    n_seq, S, _ = x.shape
    T = n_seq * S
    x2 = x.reshape(T, D_MODEL)
    tg2 = tgt.reshape(T, D_MODEL)
    w_uqkv, w_fl, w_g = w_in
    b_pad = jnp.concatenate([b_forget.reshape(1, N_HEADS), jnp.zeros((1, FL_PAD - N_HEADS), F32)], axis=1)
    mix_b = pool_mix.reshape(len(POOL_WINDOWS), GROUP_DIM, GROUP_DIM).astype(BF16)
    scale = pool_scale.reshape(1, POOL_WIDTH)
    g1 = g1.reshape(1, D_MODEL)
    g2 = g2.reshape(1, D_MODEL)
    gf = gf.reshape(1, D_MODEL)

    h, u, qkv, fl, gates = _in_proj(x2, g1, w_uqkv, w_fl, w_g, fwd_token)
    fcol = _forget_fwd(fl, b_pad, n_seq, S)
    pm, p2, p3 = _pool_fwd(u, mix_b, scale, n_seq, S)
    a, lse = _attn_fwd(qkv, fcol, n_seq, S)
    w_po, w_ao, w_out = out_weights(a)
    merged, x1, attn_y, pool_y = _mix_out(a, p3, gates, x2, w_ao, w_po, w_out)
    w_gate_t, w_up_t, w_down = ffn_weights(x1)
    h2, gate, up, act, dx2, loss_rows, dgf = _ffn_fwd(x1, g2, gf, tg2, w_gate_t, w_up_t, w_down)

    dgate, dup, dx1, dg2 = _ffn_bwd(dx2, gate, up, x1, g2, w_gate_t, w_up_t, w_down)
    bwd_token = ffn_grads_out(_matmul_tn(dgate, h2, "dw_ffn_gate"), _matmul_tn(dup, h2, "dw_ffn_up"), _matmul_tn(act, dx2, "dw_ffn_down"))
    dgates, dpy, day, da, dp2, dscale = _mix_bwd(dx1, gates, pool_y, attn_y, p2, scale, w_out, w_ao, w_po, bwd_token)
    out_token = out_grads_out(_matmul_tn(p3, dpy, "dw_pool_out"), _matmul_tn(a, day, "dw_attn_out"), _matmul_tn(merged, dx1, "dw_out"))
    du, dmix = _pool_bwd(dp2, pm, mix_b, out_token, n_seq, S)
    dq, dk, dv, dfk, dfq = _attn_bwd(qkv, da, a, fcol, lse, n_seq, S)
    dfl, db = _forget_bwd(dfk, dfq, fl, b_pad, n_seq, S)
    small_token = small_grads_out((jnp.zeros_like(g1), dg2, dgf, db[:, :N_HEADS], dscale, dmix), loss_rows)
    in_token = in_grads_out(_dw_in(h, du, dq, dk, dv, dfl, dgates, small_token))
    dx, dg1 = _in_proj_bwd(du, dq, dk, dv, dfl, dgates, x2, dx1, g1, w_uqkv, w_fl, w_g, in_token)
    norm1_grad_out(dg1)
    return dx.reshape(n_seq, S, D_MODEL)


def kernel(x, norm1_g, w_in, b_forget, pool_mix, pool_scale, w_pool_out, w_attn_out, w_out, norm2_g, w_ffn_gate, w_ffn_up, w_ffn_down, norm_f_g, loss_target, m_norm1_g, m_w_in, m_b_forget, m_pool_mix, m_pool_scale, m_w_pool_out, m_w_attn_out, m_w_out, m_norm2_g, m_w_ffn_gate, m_w_ffn_up, m_w_ffn_down, m_norm_f_g, v_norm1_g, v_w_in, v_b_forget, v_pool_mix, v_pool_scale, v_w_pool_out, v_w_attn_out, v_w_out, v_norm2_g, v_w_ffn_gate, v_w_ffn_up, v_w_ffn_down, v_norm_f_g):
    names = ("w_in", "w_pool_out", "w_attn_out", "w_out", "w_ffn_gate", "w_ffn_up", "w_ffn_down")
    w_sh = (w_in, w_pool_out, w_attn_out, w_out, w_ffn_gate, w_ffn_up, w_ffn_down)
    m_sh = (m_w_in, m_w_pool_out, m_w_attn_out, m_w_out, m_w_ffn_gate, m_w_ffn_up, m_w_ffn_down)
    v_sh = (v_w_in, v_w_pool_out, v_w_attn_out, v_w_out, v_w_ffn_gate, v_w_ffn_up, v_w_ffn_down)

    cx, cy, cc = _position()
    me = 4 * cx + 2 * cy + cc
    def stored(t, transposed):
        return jnp.transpose(t, (0, 2, 1)) if transposed else t

    w_sh, m_sh, v_sh = ([stored(t, tr) for t, tr in zip(ts, _TRANSPOSED)] for ts in (w_sh, m_sh, v_sh))
    shards = [w[0].astype(BF16) for w in w_sh]
    (gathered_in,) = _all_gather(shards[:1], "w_in_all_gather")
    out_sems = _exchange_start(shards[1:4], gathered_in, "out_weights_gather_start", "gather")
    ffn_sems = _exchange_start(shards[4:], out_sems[4], "ffn_weights_gather_start", "gather")
    no_order = jnp.zeros((8, LANES), F32)

    def with_own(lands, own):
        return [lax.dynamic_update_slice(l, o[None], (me, 0, 0)) for l, o in zip(lands, own)]

    def gathered_weights(sems, axes, name):
        def wait(after):
            send_sems, recv_sems, srcs, lands, _ = sems
            srcs, lands = _exchange_wait(send_sems, recv_sems, srcs, lands, after, name, "gather")
            return [_full_from_gathered(t, axis) for t, axis in zip(with_own(lands, srcs), axes)]

        return wait

    started = {}

    def scatter_grads(key, name):
        def start(*whole_grads):
            chunks = [
                _chunks_from_cols(t) if axis == 1 else t.reshape(N_DEV, -1, t.shape[1])
                for t, axis in zip(whole_grads, _SHARD_AXIS[key])
            ]
            started[key] = _exchange_start(chunks, no_order, name, "scatter")
            return started[key][4]

        return start

    def gather_small(small, loss_rows):
        started["small"] = _exchange_start([_pack_small(small, loss_rows)], no_order, "small_grads_gather_start", "gather")
        return started["small"][4]

    core = jnp.reshape(cc, (1,)).astype(jnp.int32)
    pos = jnp.stack([cc, 2 * cx + cy]).astype(jnp.int32)

    def reduce_w_in(send_in):
        (got_in,) = _sibling_exchange([send_in])
        pair_in = _pair_sum(send_in, got_in, core, "pair_sum_w_in")
        started["in"] = (send_in, got_in, _exchange_start([pair_in], no_order, "w_in_grads_chips_start", "chips"))
        return started["in"][2][4]

    def gather_norm1(dg1):
        rows = jnp.reshape(dg1, (8, LANES))
        started["norm1"] = _exchange_start([rows], no_order, "norm1_grad_gather_start", "gather")

    ffn, out = slice(4, 7), slice(1, 4)
    grad_x = _local_grads(
        x, loss_target, norm1_g, norm2_g, norm_f_g, b_forget, pool_mix, pool_scale, _w_in_pieces(gathered_in), ffn_sems[4],
        gathered_weights(out_sems, _SHARD_AXIS[out], "out_weights_gather_wait"),
        gathered_weights(ffn_sems, _SHARD_AXIS[ffn], "ffn_weights_gather_wait"),
        scatter_grads(ffn, "ffn_grads_scatter_start"), scatter_grads(out, "out_grads_scatter_start"), gather_small, reduce_w_in, gather_norm1,
    )
    send_in, got_in, chip_sems = started["in"]

    def scattered_updates(key, after, name):
        send_sems, recv_sems, srcs, lands, _ = started[key]
        srcs, lands = _exchange_wait(send_sems, recv_sems, srcs, lands, after, name, "scatter")
        return [
            _shard_update_direct(p, s, w, m, v, jnp.reshape(me, (1,)).astype(jnp.int32), "update_" + n)
            for p, s, w, m, v, n in zip(lands, srcs, w_sh[key], m_sh[key], v_sh[key], names[key])
        ]

    updates_out = scattered_updates(out, grad_x, "out_grads_scatter_wait")
    updates_ffn = scattered_updates(ffn, grad_x, "ffn_grads_scatter_wait")

    small_w = (norm1_g, norm2_g, norm_f_g, b_forget, pool_scale, pool_mix)
    small_m = (m_norm1_g, m_norm2_g, m_norm_f_g, m_b_forget, m_pool_scale, m_pool_mix)
    small_v = (v_norm1_g, v_norm2_g, v_norm_f_g, v_b_forget, v_pool_scale, v_pool_mix)
    zero_row = jnp.zeros((8, LANES), F32)
    send_sems, recv_sems, srcs, lands, _ = chip_sems
    _, (recv_in,) = _exchange_wait(send_sems, recv_sems, srcs, lands, updates_ffn[-1][0], "w_in_grads_chips_wait", "chips")
    update_in = _shard_update(send_in, got_in, recv_in, w_in, m_w_in, v_w_in, pos, "update_w_in")

    def gathered_small(key, after, name):
        send_sems, recv_sems, srcs, lands, _ = started[key]
        srcs, lands = _exchange_wait(send_sems, recv_sems, srcs, lands, after, name, "gather")
        return with_own(lands, srcs)[0]

    parts = gathered_small("small", update_in[0], "small_grads_gather_wait")
    first_rows = gathered_small("norm1", parts, "norm1_grad_gather_wait")
    g_s, d_s, nm_s, nv_s = _small_update(parts, first_rows, _pack_small(small_w, zero_row), _pack_small(small_m, zero_row), _pack_small(small_v, zero_row))
    g_w, d_w, nm_w, nv_w = zip(*(
        [stored(t, tr) for t in u] for u, tr in zip([update_in] + updates_out + updates_ffn, _TRANSPOSED)
    ))
    shapes = [t.shape for t in small_w]
    (g1, g2, gf, gb, gsc, gmix), loss = _unpack_small(g_s, shapes)
    (d1, d2, df, db_, dsc, dmx), _ = _unpack_small(d_s, shapes)
    (m1, m2, mf, mb, msc, mmx), _ = _unpack_small(nm_s, shapes)
    (v1, v2, vf, vb, vsc, vmx), _ = _unpack_small(nv_s, shapes)

    def ordered(n1, win, b, mix, sc, wpo, wao, wout, n2, wg, wu, wd, nf):
        return (n1, win, b, mix, sc, wpo, wao, wout, n2, wg, wu, wd, nf)

    grads = ordered(g1, g_w[0], gb, gmix, gsc, g_w[1], g_w[2], g_w[3], g2, g_w[4], g_w[5], g_w[6], gf)
    deltas = ordered(d1, d_w[0], db_, dmx, dsc, d_w[1], d_w[2], d_w[3], d2, d_w[4], d_w[5], d_w[6], df)
    new_m = ordered(m1, nm_w[0], mb, mmx, msc, nm_w[1], nm_w[2], nm_w[3], m2, nm_w[4], nm_w[5], nm_w[6], mf)
    new_v = ordered(v1, nv_w[0], vb, vmx, vsc, nv_w[1], nv_w[2], nv_w[3], v2, nv_w[4], nv_w[5], nv_w[6], vf)
    return (loss, grad_x, *grads, *deltas, *new_m, *new_v)
```

```python
import jax
import jax.numpy as jnp
from jax import lax
from jax.experimental import pallas as pl
from jax.experimental.pallas import tpu as pltpu

F32 = jnp.float32
BF16 = jnp.bfloat16
MESH = pl.DeviceIdType.MESH

D_MODEL = 1024
POOL_WINDOWS = (2, 4, 8, 16)
POOL_WIDTH = 512
GROUP_DIM = 128
ATTN_WIDTH = 512
HEAD_DIM = 64
N_HEADS = 8
N_PAIRS = 4
D_FF = 2816
RMS_EPS = 1e-6
N_DEV = 8
LANES = 128
FL_PAD = 128

ADAM_LR = 0.001
ADAM_B1 = 0.9
ADAM_B2 = 0.999
ADAM_EPS = 1e-08
ADAM_WD = 0.01
ADAM_STEP = 10

VMEM_LIMIT = 56 * 1024 * 1024
VMEM_LIMIT_MAX = 60 * 1024 * 1024
ROW_TILE = 512
ATTN_BLOCK = 512
FF_CHUNK = 256
FF_ROW_TILE = 512
DW_TOKENS = 2048


def _mm(a, b):
    return jnp.dot(a, b, preferred_element_type=F32)


def _mm_nt(a, b):
    return lax.dot_general(a, b, (((1,), (1,)), ((), ())), preferred_element_type=F32)


def _mm_tn(a, b):
    return lax.dot_general(a, b, (((0,), (0,)), ((), ())), preferred_element_type=F32)


def _sigmoid(x):
    return 1.0 / (1.0 + jnp.exp(-x))


def _params(sem, vmem=VMEM_LIMIT):
    return pltpu.CompilerParams(dimension_semantics=sem, vmem_limit_bytes=vmem)


def _const_spec(shape):
    nd = len(shape)
    return pl.BlockSpec(shape, lambda *_: (0,) * nd, pipeline_mode=pl.Buffered(1))


def _rms_fwd(x, g):
    r = lax.rsqrt(jnp.mean(x * x, axis=-1, keepdims=True) + RMS_EPS)
    xh = x * r
    return xh * g, xh, r


def _rms_bwd(dy, xh, r, g):
    dxh = dy * g
    dx = r * (dxh - xh * jnp.mean(dxh * xh, axis=-1, keepdims=True))
    return dx, dy * xh


def _in_proj(x, g1, w_uqkv, w_fl, w_g, token):
    T = x.shape[0]
    tm = ROW_TILE

    def body(x_ref, g_ref, wa_ref, wf_ref, wg_ref, token_ref, h_ref, u_ref, qkv_ref, fl_ref, gt_ref):
        h, _, _ = _rms_fwd(x_ref[...], g_ref[...])
        hb = h.astype(BF16)
        h_ref[...] = hb
        z = _mm(hb, wa_ref[...])
        u_ref[...] = z[:, :POOL_WIDTH]
        qkv_ref[...] = z[:, POOL_WIDTH:].astype(BF16)
        fl_ref[...] = _mm(hb, wf_ref[...])
        gt_ref[...] = _mm(hb, wg_ref[...]).astype(BF16)

    row = lambda n: pl.BlockSpec((tm, n), lambda i: (i, 0))
    return pl.pallas_call(
        body,
        name="in_proj",
        grid=(T // tm,),
        in_specs=[row(D_MODEL), _const_spec((1, D_MODEL)), _const_spec(w_uqkv.shape), _const_spec(w_fl.shape), _const_spec(w_g.shape), _HBM],
        out_specs=[row(D_MODEL), row(POOL_WIDTH), row(3 * ATTN_WIDTH), row(FL_PAD), row(2 * D_MODEL)],
        out_shape=[
            jax.ShapeDtypeStruct((T, D_MODEL), BF16),
            jax.ShapeDtypeStruct((T, POOL_WIDTH), F32),
            jax.ShapeDtypeStruct((T, 3 * ATTN_WIDTH), BF16),
            jax.ShapeDtypeStruct((T, FL_PAD), F32),
            jax.ShapeDtypeStruct((T, 2 * D_MODEL), BF16),
        ],
        compiler_params=_params(("parallel",)),
    )(x, g1, w_uqkv, w_fl, w_g, token)


def _log_sigmoid(x):
    return jnp.minimum(x, 0.0) - jnp.log(1.0 + jnp.exp(-jnp.abs(x)))


def _forget_fwd(fl, b_pad, n_seq, S):
    def body(fl_ref, b_ref, fcol_ref):
        lf = _log_sigmoid(fl_ref[...] + b_ref[...])
        t = lf.T
        lane = lax.broadcasted_iota(jnp.int32, t.shape, 1)
        k = 1
        while k < S:
            t = t + jnp.where(lane >= k, pltpu.roll(t, k, 1), 0.0)
            k *= 2
        fcol_ref[...] = t.T

    return pl.pallas_call(
        body,
        name="forget_fwd",
        grid=(n_seq,),
        in_specs=[pl.BlockSpec((S, FL_PAD), lambda s: (s, 0)), _const_spec((1, FL_PAD))],
        out_specs=pl.BlockSpec((S, FL_PAD), lambda s: (s, 0)),
        out_shape=jax.ShapeDtypeStruct((n_seq * S, FL_PAD), F32),
        compiler_params=_params(("parallel",)),
    )(fl, b_pad)


def _window_pick(g, v2, v4, v8, v16):
    return jnp.where(g == 0, v2, jnp.where(g == 1, v4, jnp.where(g == 2, v8, v16)))


def _pool_fwd(u, mix_b, scale, n_seq, S):
    T = n_seq * S

    def body(u_ref, mix_ref, sc_ref, pm_ref, p2_ref, p3_ref):
        g = pl.program_id(1)
        uu = u_ref[...]
        row = lax.broadcasted_iota(jnp.int32, uu.shape, 0)

        def back(a, k):
            return jnp.where(row >= k, pltpu.roll(a, k, 0), 0.0)

        s2 = uu + back(uu, 1)
        s4 = s2 + back(s2, 2)
        s8 = s4 + back(s4, 4)
        s16 = s8 + back(s8, 8)
        w = _window_pick(g, 2.0, 4.0, 8.0, 16.0)
        cnt = jnp.minimum((row + 1).astype(F32), w)
        pm = _window_pick(g, s2, s4, s8, s16) / cnt - uu
        pmb = pm.astype(BF16)
        pm_ref[...] = pmb
        p2 = _mm(pmb, mix_ref[...])
        p2_ref[...] = p2
        p3_ref[...] = (p2 * sc_ref[...]).astype(BF16)

    grp = pl.BlockSpec((S, GROUP_DIM), lambda s, g: (s, g))
    return pl.pallas_call(
        body,
        name="pool_fwd",
        grid=(n_seq, len(POOL_WINDOWS)),
        in_specs=[
            grp,
            pl.BlockSpec((None, GROUP_DIM, GROUP_DIM), lambda s, g: (g, 0, 0)),
            pl.BlockSpec((1, GROUP_DIM), lambda s, g: (0, g)),
        ],
        out_specs=[grp, grp, grp],
        out_shape=[
            jax.ShapeDtypeStruct((T, POOL_WIDTH), BF16),
            jax.ShapeDtypeStruct((T, POOL_WIDTH), F32),
            jax.ShapeDtypeStruct((T, POOL_WIDTH), BF16),
        ],
        compiler_params=_params(("parallel", "parallel")),
    )(u, mix_b, scale)


def _split3(v):
    hi = v.astype(BF16).astype(F32)
    r = v - hi
    mid = r.astype(BF16).astype(F32)
    lo = (r - mid).astype(BF16).astype(F32)
    return hi, mid, lo


def _bias_lanes(v):
    hi, mid, lo = _split3(v)
    lane = lax.broadcasted_iota(jnp.int32, (1, LANES), 1)
    packed = jnp.where(lane < N_HEADS, hi, jnp.where(lane < 2 * N_HEADS, pltpu.roll(mid, N_HEADS, 1), pltpu.roll(lo, 2 * N_HEADS, 1)))
    return jnp.where(lane < 3 * N_HEADS, packed, 0.0).astype(BF16)


def _bias_placement(slot):
    row = lax.broadcasted_iota(jnp.int32, (LANES, N_HEADS * LANES), 0)
    col = lax.broadcasted_iota(jnp.int32, (LANES, N_HEADS * LANES), 1)
    h = col // LANES
    n = col % LANES - jnp.where(h % 2 == 0, HEAD_DIM, 0) - 3 * slot
    return ((n >= 0) & (n < 3) & (row == N_HEADS * n + h)).astype(BF16)


def _augment(xp, h, bias, ones_slot):
    lane = lax.broadcasted_iota(jnp.int32, (1, LANES), 1)
    hh = h % 2
    head = (lane >= HEAD_DIM * hh) & (lane < HEAD_DIM * (hh + 1))
    b = HEAD_DIM * (1 - hh)
    rest = jnp.zeros_like(xp) if bias is None else bias[:, h * LANES : (h + 1) * LANES]
    out = jnp.where(head, xp, rest)
    if ones_slot is not None:
        out = jnp.where((lane >= b + 3 * ones_slot) & (lane < b + 3 * ones_slot + 3), jnp.ones_like(xp), out)
    return out


def _attn_fwd(qkv, fcol, n_seq, S):
    T = n_seq * S
    tb = ATTN_BLOCK
    nq = S // tb
    scale = HEAD_DIM ** -0.5

    def body(q_ref, k_ref, v_ref, fc_ref, o_ref, st_ref, qa_sc, ka_sc, m_sc, l_sc, acc_sc):
        i = pl.program_id(1)
        lane = lax.broadcasted_iota(jnp.int32, (1, LANES), 1)
        low = lane < HEAD_DIM

        @pl.when(i == 0)
        def _():
            place = _bias_placement(1)

            def rows_ka(r, carry):
                r0 = pl.multiple_of(r * tb, tb)
                bias = _mm(_bias_lanes(-fc_ref[pl.ds(r0, tb), :]), place).astype(BF16)
                for h in range(N_HEADS):
                    kp = k_ref[pl.ds(r0, tb), (h // 2) * LANES : (h // 2 + 1) * LANES] * scale
                    ka_sc[h, pl.ds(r0, tb), :] = _augment(kp, h, bias, 0)
                return carry

            lax.fori_loop(0, nq, rows_ka, 0)

        q0 = pl.multiple_of(i * tb, tb)
        bias = _mm(_bias_lanes(fc_ref[pl.ds(q0, tb), :]), _bias_placement(0)).astype(BF16)
        for h in range(N_HEADS):
            qa_sc[h] = _augment(q_ref[:, (h // 2) * LANES : (h // 2 + 1) * LANES], h, bias, 1)
        m_sc[...] = jnp.full(m_sc.shape, -jnp.inf, F32)
        l_sc[...] = jnp.zeros_like(l_sc)
        acc_sc[...] = jnp.zeros_like(acc_sc)
        causal = lax.broadcasted_iota(jnp.int32, (tb, tb), 1) <= lax.broadcasted_iota(jnp.int32, (tb, tb), 0)

        def step(j, masked):
            c0 = pl.multiple_of(j * tb, tb)
            for p in range(N_PAIRS):
                vb = v_ref[pl.ds(c0, tb), p * LANES : (p + 1) * LANES]
                pv, al = [], []
                for hh in range(2):
                    h = 2 * p + hh
                    s = _mm_nt(qa_sc[h], ka_sc[h, pl.ds(c0, tb), :])
                    if masked:
                        s = jnp.where(causal, s, -jnp.inf)
                    m_old = m_sc[h]
                    m_new = jnp.maximum(m_old, jnp.max(s, axis=1, keepdims=True))
                    alpha = jnp.exp(m_old - m_new)
                    pe = jnp.exp(s - jnp.concatenate([m_new] * (tb // LANES), axis=1))
                    l_sc[h] = alpha * l_sc[h] + jnp.sum(pe, axis=1, keepdims=True)
                    m_sc[h] = m_new
                    pv.append(_mm(pe.astype(BF16), vb))
                    al.append(alpha)
                acc_sc[p] = jnp.where(low, al[0], al[1]) * acc_sc[p] + jnp.where(low, pv[0], pv[1])

        def loop_body(j, carry):
            step(j, False)
            return carry

        lax.fori_loop(0, i, loop_body, 0)
        step(i, True)
        st = jnp.zeros((tb, LANES), F32)
        for p in range(N_PAIRS):
            lp = jnp.where(low, l_sc[2 * p], l_sc[2 * p + 1])
            o_ref[:, p * LANES : (p + 1) * LANES] = (acc_sc[p] / lp).astype(BF16)
            for h in (2 * p, 2 * p + 1):
                st = jnp.where(lane == h, m_sc[h] + jnp.log(l_sc[h]), st)
        st_ref[...] = st

    return pl.pallas_call(
        body,
        name="attn_fwd",
        grid=(n_seq, nq),
        in_specs=[
            pl.BlockSpec((tb, ATTN_WIDTH), lambda s, i: (s * nq + i, 0)),
            pl.BlockSpec((S, ATTN_WIDTH), lambda s, i: (s, 1)),
            pl.BlockSpec((S, ATTN_WIDTH), lambda s, i: (s, 2)),
            pl.BlockSpec((S, LANES), lambda s, i: (s, 0)),
        ],
        out_specs=[
            pl.BlockSpec((tb, ATTN_WIDTH), lambda s, i: (s * nq + i, 0)),
            pl.BlockSpec((tb, LANES), lambda s, i: (s * nq + i, 0)),
        ],
        out_shape=[jax.ShapeDtypeStruct((T, ATTN_WIDTH), BF16), jax.ShapeDtypeStruct((T, LANES), F32)],
        scratch_shapes=[
            pltpu.VMEM((N_HEADS, tb, LANES), BF16),
            pltpu.VMEM((N_HEADS, S, LANES), BF16),
            pltpu.VMEM((N_HEADS, tb, LANES), F32),
            pltpu.VMEM((N_HEADS, tb, LANES), F32),
            pltpu.VMEM((N_PAIRS, tb, LANES), F32),
        ],
        compiler_params=_params(("parallel", "arbitrary")),
    )(qkv, qkv, qkv, fcol)


def _mix_out(a, p3, gates, x, w_ao, w_po, w_out):
    T = x.shape[0]
    tm = ROW_TILE

    def body(a_ref, p3_ref, gt_ref, x_ref, wao_ref, wpo_ref, wout_ref, mg_ref, x1_ref, ay_ref, py_ref):
        ay = _mm(a_ref[...], wao_ref[...])
        py = _mm(p3_ref[...], wpo_ref[...])
        ay_ref[...] = ay.astype(BF16)
        py_ref[...] = py.astype(BF16)
        sp = _sigmoid(gt_ref[:, :D_MODEL].astype(F32))
        sa = _sigmoid(gt_ref[:, D_MODEL:].astype(F32))
        mb = (sp * py + sa * ay).astype(BF16)
        mg_ref[...] = mb
        x1_ref[...] = x_ref[...] + _mm(mb, wout_ref[...])

    row = lambda n: pl.BlockSpec((tm, n), lambda i: (i, 0))
    return pl.pallas_call(
        body,
        name="mix_out",
        grid=(T // tm,),
        in_specs=[
            row(ATTN_WIDTH), row(POOL_WIDTH), row(2 * D_MODEL), row(D_MODEL),
            _const_spec(w_ao.shape), _const_spec(w_po.shape), _const_spec(w_out.shape),
        ],
        out_specs=[row(D_MODEL), row(D_MODEL), row(D_MODEL), row(D_MODEL)],
        out_shape=[
            jax.ShapeDtypeStruct((T, D_MODEL), BF16), jax.ShapeDtypeStruct((T, D_MODEL), F32),
            jax.ShapeDtypeStruct((T, D_MODEL), BF16), jax.ShapeDtypeStruct((T, D_MODEL), BF16),
        ],
        compiler_params=_params(("parallel",)),
    )(a, p3, gates, x, w_ao, w_po, w_out)


def _ffn_fwd(x1, g2, gf, tgt, w_gate_t, w_up_t, w_down):
    T = x1.shape[0]
    tm = min(T, FF_ROW_TILE)
    nt = T // tm
    nc = D_FF // FF_CHUNK

    def body(x1_ref, g2_ref, gf_ref, tg_ref, wg_ref, wu_ref, wd_ref, h2_ref, gate_ref, up_ref, act_ref, dx2_ref, loss_ref, dgf_ref):
        x1v = x1_ref[...]
        h2, _, _ = _rms_fwd(x1v, g2_ref[...])
        h2b = h2.astype(BF16)
        h2_ref[...] = h2b
        for c in range(nc):
            sl = slice(c * FF_CHUNK, (c + 1) * FF_CHUNK)
            gate = _mm_nt(h2b, wg_ref[sl, :])
            up = _mm_nt(h2b, wu_ref[sl, :])
            gate_ref[:, sl] = gate.astype(BF16)
            up_ref[:, sl] = up.astype(BF16)
            act_ref[:, sl] = (gate * _sigmoid(gate) * up).astype(BF16)
        acc = x1v + _mm(act_ref[...], wd_ref[...])
        gfv = gf_ref[...]
        y, xh, r = _rms_fwd(acc, gfv)
        err = y - tg_ref[...]
        part = 0.5 * jnp.sum(jnp.mean(err * err, axis=-1, keepdims=True), axis=0, keepdims=True)
        dx2, dgrow = _rms_bwd(err * (1.0 / D_MODEL), xh, r, gfv)
        dx2_ref[...] = dx2

        @pl.when(pl.program_id(0) == 0)
        def _():
            dgf_ref[...] = jnp.zeros_like(dgf_ref)
            loss_ref[...] = jnp.zeros_like(loss_ref)

        dgf_ref[...] += jnp.sum(dgrow, axis=0, keepdims=True)
        loss_ref[...] += jnp.broadcast_to(part, loss_ref.shape)

    row = lambda n: pl.BlockSpec((tm, n), lambda i: (i, 0))
    return pl.pallas_call(
        body,
        name="ffn_fwd",
        grid=(nt,),
        in_specs=[
            row(D_MODEL), _const_spec((1, D_MODEL)), _const_spec((1, D_MODEL)), row(D_MODEL),
            _const_spec(w_gate_t.shape), _const_spec(w_up_t.shape), _const_spec(w_down.shape),
        ],
        out_specs=[
            row(D_MODEL), row(D_FF), row(D_FF), row(D_FF), row(D_MODEL),
            pl.BlockSpec((8, LANES), lambda i: (0, 0)),
            pl.BlockSpec((1, D_MODEL), lambda i: (0, 0)),
        ],
        out_shape=[
            jax.ShapeDtypeStruct((T, D_MODEL), BF16),
            jax.ShapeDtypeStruct((T, D_FF), BF16),
            jax.ShapeDtypeStruct((T, D_FF), BF16),
            jax.ShapeDtypeStruct((T, D_FF), BF16),
            jax.ShapeDtypeStruct((T, D_MODEL), F32),
            jax.ShapeDtypeStruct((8, LANES), F32),
            jax.ShapeDtypeStruct((1, D_MODEL), F32),
        ],
        compiler_params=_params(("arbitrary",)),
    )(x1, g2, gf, tgt, w_gate_t, w_up_t, w_down)


def _ffn_bwd(dx2, gate, up, x1, g2, w_gate_t, w_up_t, w_down):
    T = x1.shape[0]
    tm = min(T, FF_ROW_TILE)
    nc = D_FF // FF_CHUNK

    def body(dx2_ref, gate_ref, up_ref, x1_ref, g2_ref, wg_ref, wu_ref, wd_ref, dgate_ref, dup_ref, dx1_ref, dg2_ref):
        dx2v = dx2_ref[...]
        dx2b = dx2v.astype(BF16)
        for c in range(nc):
            sl = slice(c * FF_CHUNK, (c + 1) * FF_CHUNK)
            dact = _mm_nt(dx2b, wd_ref[sl, :])
            gate = gate_ref[:, sl].astype(F32)
            sg = _sigmoid(gate)
            silu = gate * sg
            dgate = (dact * up_ref[:, sl].astype(F32) * (sg * (1.0 + gate * (1.0 - sg)))).astype(BF16)
            dup = (dact * silu).astype(BF16)
            dgate_ref[:, sl] = dgate
            dup_ref[:, sl] = dup
        dh2 = _mm(dgate_ref[...], wg_ref[...]) + _mm(dup_ref[...], wu_ref[...])
        g2v = g2_ref[...]
        _, xh, r = _rms_fwd(x1_ref[...], g2v)
        dxn, dgrow = _rms_bwd(dh2, xh, r, g2v)
        dx1_ref[...] = dx2v + dxn

        @pl.when(pl.program_id(0) == 0)
        def _():
            dg2_ref[...] = jnp.zeros_like(dg2_ref)

        dg2_ref[...] += jnp.sum(dgrow, axis=0, keepdims=True)

    row = lambda n: pl.BlockSpec((tm, n), lambda i: (i, 0))
    return pl.pallas_call(
        body,
        name="ffn_bwd",
        grid=(T // tm,),
        in_specs=[
            row(D_MODEL), row(D_FF), row(D_FF), row(D_MODEL), _const_spec((1, D_MODEL)),
            _const_spec(w_gate_t.shape), _const_spec(w_up_t.shape), _const_spec(w_down.shape),
        ],
        out_specs=[row(D_FF), row(D_FF), row(D_MODEL), pl.BlockSpec((1, D_MODEL), lambda i: (0, 0))],
        out_shape=[
            jax.ShapeDtypeStruct((T, D_FF), BF16),
            jax.ShapeDtypeStruct((T, D_FF), BF16),
            jax.ShapeDtypeStruct((T, D_MODEL), F32),
            jax.ShapeDtypeStruct((1, D_MODEL), F32),
        ],
        compiler_params=_params(("arbitrary",), VMEM_LIMIT_MAX),
    )(dx2, gate, up, x1, g2, w_gate_t, w_up_t, w_down)


def _mix_bwd(dx1, gates, pool_y, attn_y, p2, scale, w_out, w_ao, w_po, token):
    T = dx1.shape[0]
    tm = ROW_TILE

    def body(dx1_ref, gt_ref, py_ref, ay_ref, p2_ref, sc_ref, wout_ref, wao_ref, wpo_ref, token_ref, dgt_ref, dpy_ref, day_ref, da_ref, dp2_ref, dsc_ref):
        dm = _mm_nt(dx1_ref[...].astype(BF16), wout_ref[...])
        sp = _sigmoid(gt_ref[:, :D_MODEL].astype(F32))
        sa = _sigmoid(gt_ref[:, D_MODEL:].astype(F32))
        dgt_ref[:, :D_MODEL] = (dm * py_ref[...].astype(F32) * (sp * (1.0 - sp))).astype(BF16)
        dgt_ref[:, D_MODEL:] = (dm * ay_ref[...].astype(F32) * (sa * (1.0 - sa))).astype(BF16)
        dpy = (dm * sp).astype(BF16)
        day = (dm * sa).astype(BF16)
        dpy_ref[...] = dpy
        day_ref[...] = day
        da_ref[...] = _mm_nt(day, wao_ref[...]).astype(BF16)
        dp3 = _mm_nt(dpy, wpo_ref[...])
        dp2_ref[...] = (dp3 * sc_ref[...]).astype(BF16)

        @pl.when(pl.program_id(0) == 0)
        def _():
            dsc_ref[...] = jnp.zeros_like(dsc_ref)

        dsc_ref[...] += jnp.sum(dp3 * p2_ref[...], axis=0, keepdims=True)

    row = lambda n: pl.BlockSpec((tm, n), lambda i: (i, 0))
    return pl.pallas_call(
        body,
        name="mix_bwd",
        grid=(T // tm,),
        in_specs=[
            row(D_MODEL), row(2 * D_MODEL), row(D_MODEL), row(D_MODEL), row(POOL_WIDTH), _const_spec((1, POOL_WIDTH)),
            _const_spec(w_out.shape), _const_spec(w_ao.shape), _const_spec(w_po.shape), _HBM,
        ],
        out_specs=[row(2 * D_MODEL), row(D_MODEL), row(D_MODEL), row(ATTN_WIDTH), row(POOL_WIDTH), pl.BlockSpec((1, POOL_WIDTH), lambda i: (0, 0))],
        out_shape=[
            jax.ShapeDtypeStruct((T, 2 * D_MODEL), BF16),
            jax.ShapeDtypeStruct((T, D_MODEL), BF16),
            jax.ShapeDtypeStruct((T, D_MODEL), BF16),
            jax.ShapeDtypeStruct((T, ATTN_WIDTH), BF16),
            jax.ShapeDtypeStruct((T, POOL_WIDTH), BF16),
            jax.ShapeDtypeStruct((1, POOL_WIDTH), F32),
        ],
        compiler_params=_params(("arbitrary",)),
    )(dx1, gates, pool_y, attn_y, p2, scale, w_out, w_ao, w_po, token)


def _pool_bwd(dp2, pm, mix_b, token, n_seq, S):
    T = n_seq * S

    def body(dp2_ref, pm_ref, mix_ref, token_ref, du_ref, dmix_ref):
        g = pl.program_id(0)
        dp2v = dp2_ref[...]
        dpm = _mm_nt(dp2v, mix_ref[...])
        row = lax.broadcasted_iota(jnp.int32, dpm.shape, 0)
        w = _window_pick(g, 2.0, 4.0, 8.0, 16.0)
        e = dpm / jnp.minimum((row + 1).astype(F32), w)

        def ahead(a, k):
            return jnp.where(row < S - k, pltpu.roll(a, S - k, 0), 0.0)

        r2 = e + ahead(e, 1)
        r4 = r2 + ahead(r2, 2)
        r8 = r4 + ahead(r4, 4)
        r16 = r8 + ahead(r8, 8)
        du_ref[...] = (_window_pick(g, r2, r4, r8, r16) - dpm).astype(BF16)

        @pl.when(pl.program_id(1) == 0)
        def _():
            dmix_ref[...] = jnp.zeros_like(dmix_ref)

        dmix_ref[...] += _mm_tn(pm_ref[...], dp2v)

    grp = pl.BlockSpec((S, GROUP_DIM), lambda g, s: (s, g))
    mixs = pl.BlockSpec((None, GROUP_DIM, GROUP_DIM), lambda g, s: (g, 0, 0))
    return pl.pallas_call(
        body,
        name="pool_bwd",
        grid=(len(POOL_WINDOWS), n_seq),
        in_specs=[grp, grp, mixs, _HBM],
        out_specs=[grp, mixs],
        out_shape=[jax.ShapeDtypeStruct((T, POOL_WIDTH), BF16), jax.ShapeDtypeStruct((len(POOL_WINDOWS), GROUP_DIM, GROUP_DIM), F32)],
        compiler_params=_params(("parallel", "arbitrary")),
    )(dp2, pm, mix_b, token)


def _attn_bwd(qkv, da, a, fcol, lse, n_seq, S):
    T = n_seq * S
    tb = ATTN_BLOCK
    nb = S // tb
    scale = HEAD_DIM ** -0.5

    def body(q_ref, k_ref, v_ref, do_ref, o_ref, fc_ref, st_ref, dq_ref, dk_ref, dv_ref, dfk_ref, dfq_ref,
             qa_sc, doa_sc, qat_sc, doat_sc, dq_acc, ka_sc, va_sc, dkt_sc, dvt_sc):
        j = pl.program_id(1)
        lane = lax.broadcasted_iota(jnp.int32, (1, LANES), 1)
        low = lane < HEAD_DIM

        @pl.when(j == 0)
        def _():
            dq_acc[...] = jnp.zeros_like(dq_acc)
            place = _bias_placement(0)

            def rows_q(i, carry):
                r0 = pl.multiple_of(i * tb, tb)
                delta = jnp.zeros((tb, LANES), F32)
                for h in range(N_HEADS):
                    pair = slice((h // 2) * LANES, (h // 2 + 1) * LANES)
                    prod = do_ref[pl.ds(r0, tb), pair].astype(F32) * o_ref[pl.ds(r0, tb), pair].astype(F32)
                    head = (lane >= HEAD_DIM * (h % 2)) & (lane < HEAD_DIM * (h % 2 + 1))
                    delta = jnp.where(lane == h, jnp.sum(jnp.where(head, prod, 0.0), axis=1, keepdims=True), delta)
                cq = fc_ref[pl.ds(r0, tb), :] - st_ref[pl.ds(r0, tb), :]
                q_bias = _mm(_bias_lanes(cq), place).astype(BF16)
                do_bias = _mm(_bias_lanes(-delta), place).astype(BF16)
                for h in range(N_HEADS):
                    pair = slice((h // 2) * LANES, (h // 2 + 1) * LANES)
                    qa = _augment(q_ref[pl.ds(r0, tb), pair], h, q_bias, 1)
                    doa = _augment(do_ref[pl.ds(r0, tb), pair], h, do_bias, None)
                    qa_sc[h, pl.ds(r0, tb), :] = qa
                    doa_sc[h, pl.ds(r0, tb), :] = doa
                    qat_sc[h, i] = qa.astype(F32).T.astype(BF16)
                    doat_sc[h, i] = doa.astype(F32).T.astype(BF16)
                return carry

            lax.fori_loop(0, nb, rows_q, 0)

        c0 = pl.multiple_of(j * tb, tb)
        k_bias = _mm(_bias_lanes(-fc_ref[pl.ds(c0, tb), :]), _bias_placement(1)).astype(BF16)
        for h in range(N_HEADS):
            pair = slice((h // 2) * LANES, (h // 2 + 1) * LANES)
            ka_sc[h] = _augment(k_ref[:, pair] * scale, h, k_bias, 0)
            va_sc[h] = _augment(v_ref[:, pair], h, None, 0)
        dkt_sc[...] = jnp.zeros_like(dkt_sc)
        dvt_sc[...] = jnp.zeros_like(dvt_sc)
        causal = lax.broadcasted_iota(jnp.int32, (tb, tb), 1) <= lax.broadcasted_iota(jnp.int32, (tb, tb), 0)

        def step(i, masked):
            r0 = pl.multiple_of(i * tb, tb)
            for h in range(N_HEADS):
                s = _mm_nt(qa_sc[h, pl.ds(r0, tb), :], ka_sc[h])
                if masked:
                    s = jnp.where(causal, s, -jnp.inf)
                pr = jnp.exp(s)
                dvt_sc[h] += _mm(doat_sc[h, i], pr.astype(BF16))
                dsb = (pr * _mm_nt(doa_sc[h, pl.ds(r0, tb), :], va_sc[h])).astype(BF16)
                dkt_sc[h] += _mm(qat_sc[h, i], dsb)
                dq_acc[h, pl.ds(r0, tb), :] += _mm(dsb, ka_sc[h])

        step(j, True)

        def loop_body(i, carry):
            step(i, False)
            return carry

        lax.fori_loop(j + 1, nb, loop_body, 0)
        dfk = jnp.zeros((tb, LANES), F32)
        for p in range(N_PAIRS):
            dk = [dkt_sc[2 * p + hh].T for hh in range(2)]
            dv = [dvt_sc[2 * p + hh].T for hh in range(2)]
            dk_ref[:, p * LANES : (p + 1) * LANES] = (jnp.where(low, dk[0], dk[1]) * scale).astype(BF16)
            dv_ref[:, p * LANES : (p + 1) * LANES] = jnp.where(low, dv[0], dv[1]).astype(BF16)
            for hh in range(2):
                b = HEAD_DIM * (1 - hh) + 3
                dfk = jnp.where(lane == 2 * p + hh, -dk[hh][:, b : b + 1], dfk)
        dfk_ref[...] = dfk

        @pl.when(j == nb - 1)
        def _():
            def rows_dq(i, carry):
                r0 = pl.multiple_of(i * tb, tb)
                dfq = jnp.zeros((tb, LANES), F32)
                for p in range(N_PAIRS):
                    parts = [dq_acc[2 * p + hh, pl.ds(r0, tb), :] for hh in range(2)]
                    dq_ref[pl.ds(r0, tb), p * LANES : (p + 1) * LANES] = jnp.where(low, parts[0], parts[1]).astype(BF16)
                    for hh in range(2):
                        b = HEAD_DIM * (1 - hh)
                        dfq = jnp.where(lane == 2 * p + hh, parts[hh][:, b : b + 1], dfq)
                dfq_ref[pl.ds(r0, tb), :] = dfq
                return carry

            lax.fori_loop(0, nb, rows_dq, 0)

    seq = lambda w, col: pl.BlockSpec((S, w), lambda s, j: (s, col))
    seq_in = lambda w, col: pl.BlockSpec((S, w), lambda s, j: (s, col), pipeline_mode=pl.Buffered(1))
    blk = lambda w, col: pl.BlockSpec((tb, w), lambda s, j: (s * nb + j, col))
    return pl.pallas_call(
        body,
        name="attn_bwd",
        grid=(n_seq, nb),
        in_specs=[seq_in(ATTN_WIDTH, 0), blk(ATTN_WIDTH, 1), blk(ATTN_WIDTH, 2), seq_in(ATTN_WIDTH, 0), seq_in(ATTN_WIDTH, 0), seq_in(LANES, 0), seq_in(LANES, 0)],
        out_specs=[seq(ATTN_WIDTH, 0), blk(ATTN_WIDTH, 0), blk(ATTN_WIDTH, 0), blk(LANES, 0), seq(LANES, 0)],
        out_shape=[
            jax.ShapeDtypeStruct((T, ATTN_WIDTH), BF16),
            jax.ShapeDtypeStruct((T, ATTN_WIDTH), BF16),
            jax.ShapeDtypeStruct((T, ATTN_WIDTH), BF16),
            jax.ShapeDtypeStruct((T, LANES), F32),
            jax.ShapeDtypeStruct((T, LANES), F32),
        ],
        scratch_shapes=[
            pltpu.VMEM((N_HEADS, S, LANES), BF16),
            pltpu.VMEM((N_HEADS, S, LANES), BF16),
            pltpu.VMEM((N_HEADS, nb, LANES, tb), BF16),
            pltpu.VMEM((N_HEADS, nb, LANES, tb), BF16),
            pltpu.VMEM((N_HEADS, S, LANES), F32),
            pltpu.VMEM((N_HEADS, tb, LANES), BF16),
            pltpu.VMEM((N_HEADS, tb, LANES), BF16),
            pltpu.VMEM((N_HEADS, LANES, tb), F32),
            pltpu.VMEM((N_HEADS, LANES, tb), F32),
        ],
        compiler_params=_params(("parallel", "arbitrary"), VMEM_LIMIT_MAX),
    )(qkv, qkv, qkv, da, a, fcol, lse)


def _forget_bwd(dfk, dfq, fl, b_pad, n_seq, S):
    def body(df_ref, dfq_ref, fl_ref, b_ref, dfl_ref, db_ref):
        t = (df_ref[...] + dfq_ref[...]).T
        lane = lax.broadcasted_iota(jnp.int32, t.shape, 1)
        k = 1
        while k < S:
            t = t + jnp.where(lane < S - k, pltpu.roll(t, S - k, 1), 0.0)
            k *= 2
        dfl = t.T * _sigmoid(-(fl_ref[...] + b_ref[...]))
        dfl_ref[...] = dfl.astype(BF16)

        @pl.when(pl.program_id(0) == 0)
        def _():
            db_ref[...] = jnp.zeros_like(db_ref)

        db_ref[...] += jnp.sum(dfl, axis=0, keepdims=True)

    return pl.pallas_call(
        body,
        name="forget_bwd",
        grid=(n_seq,),
        in_specs=[
            pl.BlockSpec((S, LANES), lambda s: (s, 0)),
            pl.BlockSpec((S, LANES), lambda s: (s, 0)),
            pl.BlockSpec((S, FL_PAD), lambda s: (s, 0)),
            _const_spec((1, FL_PAD)),
        ],
        out_specs=[pl.BlockSpec((S, FL_PAD), lambda s: (s, 0)), pl.BlockSpec((1, FL_PAD), lambda s: (0, 0))],
        out_shape=[jax.ShapeDtypeStruct((n_seq * S, FL_PAD), BF16), jax.ShapeDtypeStruct((1, FL_PAD), F32)],
        compiler_params=_params(("arbitrary",)),
    )(dfk, dfq, fl, b_pad)


def _in_proj_bwd(du, dq, dk, dv, dfl, dgates, x, dx1, g1, w_uqkv, w_fl, w_g, token):
    T = x.shape[0]
    tm = ROW_TILE

    def body(du_ref, dq_ref, dk_ref, dv_ref, dfl_ref, dgt_ref, x_ref, dx1_ref, g_ref, wa_ref, wf_ref, wg_ref, token_ref, dx_ref, dg_ref):
        dz = jnp.concatenate([du_ref[...], dq_ref[...], dk_ref[...], dv_ref[...]], axis=1)
        dh = _mm_nt(dz, wa_ref[...]) + _mm_nt(dgt_ref[...], wg_ref[...]) + _mm_nt(dfl_ref[...], wf_ref[...])
        gv = g_ref[...]
        _, xh, r = _rms_fwd(x_ref[...], gv)
        dxn, dgrow = _rms_bwd(dh, xh, r, gv)
        dx_ref[...] = dx1_ref[...] + dxn

        @pl.when(pl.program_id(0) == 0)
        def _():
            dg_ref[...] = jnp.zeros_like(dg_ref)

        dg_ref[...] += jnp.sum(dgrow, axis=0, keepdims=True)

    row = lambda n: pl.BlockSpec((tm, n), lambda i: (i, 0))
    return pl.pallas_call(
        body,
        name="in_proj_bwd",
        grid=(T // tm,),
        in_specs=[
            row(512), row(512), row(512), row(512), row(FL_PAD), row(2 * D_MODEL), row(D_MODEL), row(D_MODEL), _const_spec((1, D_MODEL)),
            _const_spec(w_uqkv.shape), _const_spec(w_fl.shape), _const_spec(w_g.shape), _HBM,
        ],
        out_specs=[row(D_MODEL), pl.BlockSpec((1, D_MODEL), lambda i: (0, 0))],
        out_shape=[jax.ShapeDtypeStruct((T, D_MODEL), F32), jax.ShapeDtypeStruct((1, D_MODEL), F32)],
        compiler_params=_params(("arbitrary",)),
    )(du, dq, dk, dv, dfl, dgates, x, dx1, g1, w_uqkv, w_fl, w_g, token)


def _pick_block(n):
    for b in (1024, 512, 1408, 256, 128):
        if n % b == 0:
            return b
    raise ValueError(n)


def _matmul_tn(a, b, name):
    T, K = a.shape
    N = b.shape[1]
    bt, bk, bn = min(T, DW_TOKENS), _pick_block(K), _pick_block(N)
    nt = T // bt

    def body(a_ref, b_ref, o_ref, acc):
        @pl.when(pl.program_id(2) == 0)
        def _():
            acc[...] = jnp.zeros_like(acc)

        acc[...] += _mm_tn(a_ref[...].astype(BF16), b_ref[...].astype(BF16))

        @pl.when(pl.program_id(2) == nt - 1)
        def _():
            o_ref[...] = acc[...].astype(BF16)

    return pl.pallas_call(
        body,
        name=name,
        grid=(K // bk, N // bn, nt),
        in_specs=[pl.BlockSpec((bt, bk), lambda k, n, t: (t, k)), pl.BlockSpec((bt, bn), lambda k, n, t: (t, n))],
        out_specs=pl.BlockSpec((bk, bn), lambda k, n, t: (k, n)),
        out_shape=jax.ShapeDtypeStruct((K, N), BF16),
        scratch_shapes=[pltpu.VMEM((bk, bn), F32)],
        compiler_params=_params(("parallel", "parallel", "arbitrary")),
    )(a, b)


W_IN_A = POOL_WIDTH + 3 * ATTN_WIDTH
W_IN_SHARD = (W_IN_A + N_HEADS + 2 * D_MODEL) // N_DEV
_W_IN_PIECES = ((0, W_IN_A), (W_IN_A, W_IN_A + N_HEADS), (W_IN_A + N_HEADS, W_IN_A + N_HEADS + 2 * D_MODEL))


def _w_in_segments(d):
    lo, hi = d * W_IN_SHARD, (d + 1) * W_IN_SHARD
    out = []
    for p, (a, b) in enumerate(_W_IN_PIECES):
        s, e = max(lo, a), min(hi, b)
        if s < e:
            out.append((p, s - a, s - lo, e - s))
    return out


def _w_in_pieces(gathered):
    tm = ROW_TILE // 2

    def body(g_ref, wa_ref, wf_ref, wg_ref):
        outs = (wa_ref, wf_ref, wg_ref)
        wf_ref[...] = jnp.zeros_like(wf_ref)
        for d in range(N_DEV):
            for p, at, frm, n in _w_in_segments(d):
                outs[p][:, at : at + n] = g_ref[d, :, frm : frm + n]

    return pl.pallas_call(
        body,
        name="w_in_pieces",
        grid=(D_MODEL // tm,),
        in_specs=[pl.BlockSpec((N_DEV, tm, W_IN_SHARD), lambda i: (0, i, 0))],
        out_specs=[pl.BlockSpec((tm, W_IN_A), lambda i: (i, 0)), pl.BlockSpec((tm, FL_PAD), lambda i: (i, 0)), pl.BlockSpec((tm, 2 * D_MODEL), lambda i: (i, 0))],
        out_shape=[
            jax.ShapeDtypeStruct((D_MODEL, W_IN_A), gathered.dtype),
            jax.ShapeDtypeStruct((D_MODEL, FL_PAD), gathered.dtype),
            jax.ShapeDtypeStruct((D_MODEL, 2 * D_MODEL), gathered.dtype),
        ],
        compiler_params=_params(("parallel",)),
    )(gathered)


def _dw_in(h, du, dq, dk, dv, dfl, dgates, token):
    T = h.shape[0]
    bt, bk = min(T, DW_TOKENS // 2), 512
    nt = T // bt
    pieces = (du, dq, dk, dv, dfl, dgates)
    offs = [0]
    for p in pieces:
        offs.append(offs[-1] + p.shape[1])

    def body(h_ref, *rest):
        refs, o_ref, acc = rest[: len(pieces)], rest[-2], rest[-1]

        @pl.when(pl.program_id(1) == 0)
        def _():
            acc[...] = jnp.zeros_like(acc)

        ht = h_ref[...].T
        for ref, at in zip(refs, offs):
            acc[:, at : at + ref.shape[1]] += _mm(ht, ref[...])

        @pl.when(pl.program_id(1) == nt - 1)
        def _():
            starts = (0, W_IN_A, W_IN_A + FL_PAD)
            for d in range(N_DEV):
                for p, at, to, n in _w_in_segments(d):
                    o_ref[d % 2, d // 2, :, to : to + n] = acc[:, starts[p] + at : starts[p] + at + n].astype(BF16)

    return pl.pallas_call(
        body,
        name="dw_in",
        grid=(D_MODEL // bk, nt),
        in_specs=[pl.BlockSpec((bt, bk), lambda k, t: (t, k))] + [pl.BlockSpec((bt, p.shape[1]), lambda k, t: (t, 0)) for p in pieces] + [_HBM],
        out_specs=pl.BlockSpec((2, 4, bk, W_IN_SHARD), lambda k, t: (0, 0, k, 0)),
        out_shape=jax.ShapeDtypeStruct((2, 4, D_MODEL, W_IN_SHARD), BF16),
        scratch_shapes=[pltpu.VMEM((bk, offs[-1]), F32)],
        compiler_params=_params(("parallel", "arbitrary")),
    )(h, *pieces, token)


def _position():
    return lax.axis_index("x"), lax.axis_index("y"), lax.axis_index("c")


_HBM = pl.BlockSpec(memory_space=pl.ANY)


def _all_gather(blocks, name):
    n = len(blocks)

    def body(*refs):
        xs, outs = refs[:n], refs[n : 2 * n]
        send_sems, recv_sems, local_sems = refs[2 * n :]
        x, y, c = _position()
        me, sibling = (x, y, c), (x, y, 1 - c)
        chips = [(1 - x, y), (x, 1 - y), (1 - x, 1 - y)]

        def rows(a, px, py, pc):
            return outs[a].at[4 * px + 2 * py + pc]

        def copy(a, k, blk, to, src=None):
            return pltpu.make_async_remote_copy(
                src_ref=rows(a, *blk) if src is None else src, dst_ref=rows(a, *blk),
                send_sem=send_sems.at[7 * a + k], recv_sem=recv_sems.at[7 * a + k], device_id=to, device_id_type=MESH,
            )

        mine = [pltpu.make_async_copy(xs[a], rows(a, *me), local_sems.at[a]) for a in range(n)]
        for cp in mine:
            cp.start()
        first = []
        for a in range(n):
            first.append(copy(a, 0, me, sibling, src=xs[a]))
            first += [copy(a, 1 + j, me, (*chip, c), src=xs[a]) for j, chip in enumerate(chips)]
        for cp in first:
            cp.start()
        passed = []
        for j, chip in enumerate(chips):
            for a in range(n):
                copy(a, 1 + j, (*chip, c), me).wait_recv()
                passed.append(copy(a, 4 + j, (*chip, c), sibling))
                passed[-1].start()
        for a in range(n):
            copy(a, 0, sibling, me).wait_recv()
        for j, chip in enumerate(chips):
            for a in range(n):
                copy(a, 4 + j, (*chip, 1 - c), me).wait_recv()
        for cp in first + passed:
            cp.wait_send()
        for cp in mine:
            cp.wait()

    return pl.pallas_call(
        body,
        name=name,
        out_shape=[jax.ShapeDtypeStruct((N_DEV, *b.shape), b.dtype) for b in blocks],
        in_specs=[_HBM] * n,
        out_specs=[_HBM] * n,
        scratch_shapes=[pltpu.SemaphoreType.DMA((7 * n,)), pltpu.SemaphoreType.DMA((7 * n,)), pltpu.SemaphoreType.DMA((n,))],
    )(*blocks)


_SEM = pl.BlockSpec(memory_space=pltpu.SEMAPHORE)
_HBM_ONLY = pl.BlockSpec(memory_space=pltpu.HBM)
_SIDE_EFFECT = pltpu.SideEffectType.DATAFLOW_SIDE_EFFECTING


def _peer(x, y, c, k):
    return (1 - x if k & 4 else x, 1 - y if k & 2 else y, 1 - c if k & 1 else c)


_PEER_BITS = {"gather": range(1, N_DEV), "scatter": range(1, N_DEV), "chips": (4, 2, 6)}
_LAND_SLOTS = {"gather": N_DEV, "scatter": N_DEV, "chips": 3}


def _exchange_copies(src_refs, land_refs, send_sems, recv_sems, pattern, receive_side):
    x, y, c = _position()
    me = 4 * x + 2 * y + c
    bits = _PEER_BITS[pattern]
    cps = []
    for j, k in enumerate(bits):
        px, py, pc = _peer(x, y, c, k)
        peer = 4 * px + 2 * py + pc
        for a, (src, land) in enumerate(zip(src_refs, land_refs)):
            if pattern == "chips":
                s, slot = src.at[2 * px + py], j
            else:
                s, slot = (src if pattern == "gather" else src.at[peer]), (peer if receive_side else me)
            cps.append(pltpu.make_async_remote_copy(
                src_ref=s, dst_ref=land.at[slot],
                send_sem=send_sems.at[len(bits) * a + j], recv_sem=recv_sems.at[len(bits) * a + j],
                device_id=(px, py, pc), device_id_type=MESH,
            ))
    return cps


def _exchange_start(srcs, after, name, pattern):
    n = len(srcs)
    m = len(_PEER_BITS[pattern])
    lands = [jax.ShapeDtypeStruct((_LAND_SLOTS[pattern], *s.shape[-2:]), s.dtype) for s in srcs]

    def body(*refs):
        src_refs, land_refs = refs[1 : 1 + n], refs[1 + n : 1 + 2 * n]
        send_sems, recv_sems = refs[1 + 2 * n], refs[2 + 2 * n]
        token, own_sems = refs[3 + 4 * n], refs[4 + 4 * n]
        for cp in _exchange_copies(src_refs, land_refs, send_sems, recv_sems, pattern, receive_side=False):
            cp.start()
        if pattern == "gather":
            x, y, c = _position()
            own = [
                pltpu.make_async_copy(src, land.at[4 * x + 2 * y + c], own_sems.at[a])
                for a, (src, land) in enumerate(zip(src_refs, land_refs))
            ]
            for cp in own:
                cp.start()
            for cp in own:
                cp.wait()
        token[...] = jnp.zeros_like(token)

    hbm = lambda t: pltpu.with_memory_space_constraint(t, pltpu.HBM)
    out = pl.pallas_call(
        body,
        name=name,
        out_shape=(
            pltpu.SemaphoreType.DMA((m * n,)), pltpu.SemaphoreType.DMA((m * n,)),
            *[pltpu.HBM(s.shape, s.dtype) for s in srcs], *[pltpu.HBM(l.shape, l.dtype) for l in lands],
            jax.ShapeDtypeStruct((8, LANES), F32),
        ),
        in_specs=(_HBM, *[_HBM_ONLY] * (2 * n)),
        out_specs=(_SEM, _SEM, *[_HBM_ONLY] * (2 * n), pl.BlockSpec(memory_space=pltpu.VMEM)),
        input_output_aliases={1 + i: 2 + i for i in range(2 * n)},
        scratch_shapes=[pltpu.SemaphoreType.DMA((n,))],
        compiler_params=pltpu.CompilerParams(has_side_effects=_SIDE_EFFECT),
    )(after, *[hbm(s) for s in srcs], *[hbm(lax.empty(l.shape, l.dtype)) for l in lands])
    return out[0], out[1], out[2 : 2 + n], out[2 + n : 2 + 2 * n], out[-1]


def _exchange_wait(send_sems, recv_sems, srcs, lands, after, name, pattern):
    n = len(srcs)

    def body(*refs):
        src_refs, land_refs = refs[:n], refs[n : 2 * n]
        for cp in _exchange_copies(src_refs, land_refs, refs[2 * n], refs[2 * n + 1], pattern, receive_side=True):
            cp.wait_send()
            cp.wait_recv()

    out = pl.pallas_call(
        body,
        name=name,
        out_shape=(*[pltpu.HBM(s.shape, s.dtype) for s in srcs], *[pltpu.HBM(l.shape, l.dtype) for l in lands]),
        in_specs=(*[_HBM_ONLY] * (2 * n), _SEM, _SEM, _HBM),
        out_specs=tuple([_HBM_ONLY] * (2 * n)),
        input_output_aliases={i: i for i in range(2 * n)},
        compiler_params=pltpu.CompilerParams(has_side_effects=_SIDE_EFFECT),
    )(*srcs, *lands, send_sems, recv_sems, after)
    return out[:n], out[n:]


def _sibling_exchange(sends):
    n = len(sends)

    def body(*refs):
        srcs, dsts = refs[:n], refs[n : 2 * n]
        send_sems, recv_sems = refs[2 * n :]
        x, y, c = _position()
        cps = [
            pltpu.make_async_remote_copy(
                src_ref=srcs[a].at[1 - c], dst_ref=dsts[a], send_sem=send_sems.at[a], recv_sem=recv_sems.at[a],
                device_id=(x, y, 1 - c), device_id_type=MESH,
            )
            for a in range(n)
        ]
        for cp in cps:
            cp.start()
        for cp in cps:
            cp.wait()

    return pl.pallas_call(
        body,
        name="rs_sibling",
        out_shape=[jax.ShapeDtypeStruct(s.shape[1:], s.dtype) for s in sends],
        in_specs=[_HBM] * n,
        out_specs=[_HBM] * n,
        scratch_shapes=[pltpu.SemaphoreType.DMA((n,)), pltpu.SemaphoreType.DMA((n,))],
    )(*sends)


def _rows_tile(r):
    return ROW_TILE if r % ROW_TILE == 0 else r


def _pair_sum(send, got, core, name):
    _, _, r, c = send.shape
    br = _rows_tile(r)

    def body(core_ref, a_ref, b_ref, o_ref):
        o_ref[...] = (a_ref[...].astype(F32) + b_ref[...].astype(F32)).astype(o_ref.dtype)

    return pl.pallas_call(
        body,
        name=name,
        grid_spec=pltpu.PrefetchScalarGridSpec(
            num_scalar_prefetch=1,
            grid=(4, r // br),
            in_specs=[
                pl.BlockSpec((None, None, br, c), lambda n, i, core: (core[0], n, i, 0)),
                pl.BlockSpec((None, br, c), lambda n, i, core: (n, i, 0)),
            ],
            out_specs=pl.BlockSpec((None, br, c), lambda n, i, core: (n, i, 0)),
        ),
        out_shape=jax.ShapeDtypeStruct((4, r, c), send.dtype),
        compiler_params=_params(("parallel", "parallel")),
    )(core, send, got)


def _adamw(w, g, m, v):
    m = ADAM_B1 * m + (1.0 - ADAM_B1) * g
    v = ADAM_B2 * v + (1.0 - ADAM_B2) * (g * g)
    m_hat = m / (1.0 - ADAM_B1 ** ADAM_STEP)
    v_hat = v / (1.0 - ADAM_B2 ** ADAM_STEP)
    delta = -ADAM_LR * (m_hat / (jnp.sqrt(v_hat) + ADAM_EPS) + ADAM_WD * w)
    return delta, m, v


def _shard_update(send, got, recv, w, m, v, pos, name):
    _, r, c = w.shape
    br = _rows_tile(r)

    def body(pos_ref, a_ref, b_ref, r_ref, w_ref, m_ref, v_ref, g_ref, d_ref, nm_ref, nv_ref):
        g = a_ref[...].astype(F32) + b_ref[...].astype(F32)
        for n in range(3):
            g = g + r_ref[n].astype(F32)
        g_ref[...] = g
        d_ref[...], nm_ref[...], nv_ref[...] = _adamw(w_ref[...], g, m_ref[...], v_ref[...])

    own = pl.BlockSpec((None, br, c), lambda i, pos: (0, i, 0))
    return pl.pallas_call(
        body,
        name=name,
        grid_spec=pltpu.PrefetchScalarGridSpec(
            num_scalar_prefetch=1,
            grid=(r // br,),
            in_specs=[
                pl.BlockSpec((None, None, br, c), lambda i, pos: (pos[0], pos[1], i, 0)),
                pl.BlockSpec((None, br, c), lambda i, pos: (pos[1], i, 0)),
                pl.BlockSpec((3, br, c), lambda i, pos: (0, i, 0)),
                own, own, own,
            ],
            out_specs=[own, own, own, own],
        ),
        out_shape=[jax.ShapeDtypeStruct((1, r, c), F32)] * 4,
        compiler_params=_params(("parallel",)),
    )(pos, send, got, recv, w, m, v)


def _shard_update_direct(parts, chunks, w, m, v, me, name):
    _, r, c = w.shape
    br = _rows_tile(r)

    def body(me_ref, p_ref, own_ref, w_ref, m_ref, v_ref, g_ref, d_ref, nm_ref, nv_ref):
        g = None
        for n in range(N_DEV):
            part = jnp.where(me_ref[0] == n, own_ref[...], p_ref[n]).astype(F32)
            g = part if g is None else g + part
        g_ref[...] = g
        d_ref[...], nm_ref[...], nv_ref[...] = _adamw(w_ref[...], g, m_ref[...], v_ref[...])

    shard = pl.BlockSpec((None, br, c), lambda i, me: (0, i, 0))
    return pl.pallas_call(
        body,
        name=name,
        grid_spec=pltpu.PrefetchScalarGridSpec(
            num_scalar_prefetch=1,
            grid=(r // br,),
            in_specs=[
                pl.BlockSpec((N_DEV, br, c), lambda i, me: (0, i, 0)),
                pl.BlockSpec((None, br, c), lambda i, me: (me[0], i, 0)),
                shard, shard, shard,
            ],
            out_specs=[shard, shard, shard, shard],
        ),
        out_shape=[jax.ShapeDtypeStruct((1, r, c), F32)] * 4,
        compiler_params=_params(("parallel",)),
    )(me, parts, chunks, w, m, v)


def _small_update(parts, first_rows, ws, ms, vs):
    k = len(ws)

    def unpacked(rows, shape):
        if len(shape) == 2 and shape[1] <= LANES:
            return rows[0:1, : shape[1]]
        if len(shape) == 2:
            return jnp.concatenate([rows[r : r + 1] for r in range(shape[1] // LANES)], axis=1)
        return rows.reshape(shape)

    def body(p_ref, f_ref, *refs):
        w_refs, m_refs, v_refs = refs[:k], refs[k : 2 * k], refs[2 * k : 3 * k]
        outs, loss_ref = refs[3 * k : 7 * k], refs[7 * k]
        g, first = p_ref[0], f_ref[0]
        for n in range(1, N_DEV):
            g = g + p_ref[n]
            first = first + f_ref[n]
        g = jnp.concatenate([g[:8] + first, g[8:]], axis=0)
        off = 0
        for i, (_, rows) in enumerate(_SMALL):
            gi = unpacked(g[off : off + rows], w_refs[i].shape)
            off += rows
            outs[i][...] = gi
            outs[k + i][...], outs[2 * k + i][...], outs[3 * k + i][...] = _adamw(w_refs[i][...], gi, m_refs[i][...], v_refs[i][...])
        loss_ref[...] = g[off : off + 1, 0:1]

    out = pl.pallas_call(
        body,
        name="small_update",
        out_shape=[jax.ShapeDtypeStruct(w.shape, F32) for _ in range(4) for w in ws] + [jax.ShapeDtypeStruct((1, 1), F32)],
        compiler_params=pltpu.CompilerParams(vmem_limit_bytes=VMEM_LIMIT),
    )(parts, first_rows, *ws, *ms, *vs)
    return [out[a * k : (a + 1) * k] for a in range(4)], out[4 * k]


_SHARD_AXIS = (1, 1, 1, 0, 0, 0, 0)
_TRANSPOSED = (False, False, False, False, True, True, False)


def _full_from_gathered(t, axis):
    if axis == 0:
        return t.reshape(N_DEV * t.shape[1], t.shape[2])
    return jnp.concatenate([t[d] for d in range(N_DEV)], axis=1)


def _chunks_from_cols(t):
    c = t.shape[1] // N_DEV
    return jnp.stack([t[:, d * c : (d + 1) * c] for d in range(N_DEV)])


_SMALL = (("norm1_g", 8), ("norm2_g", 8), ("norm_f_g", 8), ("b_forget", 8), ("pool_scale", 8), ("pool_mix", 512))


def _pack_small(vals, loss_row):
    parts = []
    for (name, rows), t in zip(_SMALL, vals):
        f = t.astype(F32).reshape(-1)
        f = jnp.concatenate([f, jnp.zeros((rows * LANES - f.shape[0],), F32)]).reshape(rows, LANES)
        parts.append(f)
    parts.append(loss_row)
    return jnp.concatenate(parts, axis=0)


def _local_grads(x, tgt, g1, g2, gf, b_forget, pool_mix, pool_scale, w_in, fwd_token, out_weights, ffn_weights, ffn_grads_out, out_grads_out, small_grads_out, in_grads_out, norm1_grad_out):
    n_seq, S, _ = x.shape
    T = n_seq * S
    x2 = x.reshape(T, D_MODEL)
    tg2 = tgt.reshape(T, D_MODEL)
    w_uqkv, w_fl, w_g = w_in
    b_pad = jnp.concatenate([b_forget.reshape(1, N_HEADS), jnp.zeros((1, FL_PAD - N_HEADS), F32)], axis=1)
    mix_b = pool_mix.reshape(len(POOL_WINDOWS), GROUP_DIM, GROUP_DIM).astype(BF16)
    scale = pool_scale.reshape(1, POOL_WIDTH)
    g1 = g1.reshape(1, D_MODEL)
    g2 = g2.reshape(1, D_MODEL)
    gf = gf.reshape(1, D_MODEL)

    h, u, qkv, fl, gates = _in_proj(x2, g1, w_uqkv, w_fl, w_g, fwd_token)
    fcol = _forget_fwd(fl, b_pad, n_seq, S)
    pm, p2, p3 = _pool_fwd(u, mix_b, scale, n_seq, S)
    a, lse = _attn_fwd(qkv, fcol, n_seq, S)
    w_po, w_ao, w_out = out_weights(a)
    merged, x1, attn_y, pool_y = _mix_out(a, p3, gates, x2, w_ao, w_po, w_out)
    w_gate_t, w_up_t, w_down = ffn_weights(x1)
    h2, gate, up, act, dx2, loss_rows, dgf = _ffn_fwd(x1, g2, gf, tg2, w_gate_t, w_up_t, w_down)

    dgate, dup, dx1, dg2 = _ffn_bwd(dx2, gate, up, x1, g2, w_gate_t, w_up_t, w_down)
    bwd_token = ffn_grads_out(_matmul_tn(dgate, h2, "dw_ffn_gate"), _matmul_tn(dup, h2, "dw_ffn_up"), _matmul_tn(act, dx2, "dw_ffn_down"))
    dgates, dpy, day, da, dp2, dscale = _mix_bwd(dx1, gates, pool_y, attn_y, p2, scale, w_out, w_ao, w_po, bwd_token)
    out_token = out_grads_out(_matmul_tn(p3, dpy, "dw_pool_out"), _matmul_tn(a, day, "dw_attn_out"), _matmul_tn(merged, dx1, "dw_out"))
    du, dmix = _pool_bwd(dp2, pm, mix_b, out_token, n_seq, S)
    dq, dk, dv, dfk, dfq = _attn_bwd(qkv, da, a, fcol, lse, n_seq, S)
    dfl, db = _forget_bwd(dfk, dfq, fl, b_pad, n_seq, S)
    small_token = small_grads_out((jnp.zeros_like(g1), dg2, dgf, db[:, :N_HEADS], dscale, dmix), loss_rows)
    in_token = in_grads_out(_dw_in(h, du, dq, dk, dv, dfl, dgates, small_token))
    dx, dg1 = _in_proj_bwd(du, dq, dk, dv, dfl, dgates, x2, dx1, g1, w_uqkv, w_fl, w_g, in_token)
    norm1_grad_out(dg1)
    return dx.reshape(n_seq, S, D_MODEL)


def kernel(x, norm1_g, w_in, b_forget, pool_mix, pool_scale, w_pool_out, w_attn_out, w_out, norm2_g, w_ffn_gate, w_ffn_up, w_ffn_down, norm_f_g, loss_target, m_norm1_g, m_w_in, m_b_forget, m_pool_mix, m_pool_scale, m_w_pool_out, m_w_attn_out, m_w_out, m_norm2_g, m_w_ffn_gate, m_w_ffn_up, m_w_ffn_down, m_norm_f_g, v_norm1_g, v_w_in, v_b_forget, v_pool_mix, v_pool_scale, v_w_pool_out, v_w_attn_out, v_w_out, v_norm2_g, v_w_ffn_gate, v_w_ffn_up, v_w_ffn_down, v_norm_f_g):
    names = ("w_in", "w_pool_out", "w_attn_out", "w_out", "w_ffn_gate", "w_ffn_up", "w_ffn_down")
    w_sh = (w_in, w_pool_out, w_attn_out, w_out, w_ffn_gate, w_ffn_up, w_ffn_down)
    m_sh = (m_w_in, m_w_pool_out, m_w_attn_out, m_w_out, m_w_ffn_gate, m_w_ffn_up, m_w_ffn_down)
    v_sh = (v_w_in, v_w_pool_out, v_w_attn_out, v_w_out, v_w_ffn_gate, v_w_ffn_up, v_w_ffn_down)

    cx, cy, cc = _position()
    me = 4 * cx + 2 * cy + cc
    def stored(t, transposed):
        return jnp.transpose(t, (0, 2, 1)) if transposed else t

    w_sh, m_sh, v_sh = ([stored(t, tr) for t, tr in zip(ts, _TRANSPOSED)] for ts in (w_sh, m_sh, v_sh))
    shards = [w[0].astype(BF16) for w in w_sh]
    (gathered_in,) = _all_gather(shards[:1], "w_in_all_gather")
    out_sems = _exchange_start(shards[1:4], gathered_in, "out_weights_gather_start", "gather")
    ffn_sems = _exchange_start(shards[4:], out_sems[4], "ffn_weights_gather_start", "gather")
    no_order = jnp.zeros((8, LANES), F32)

    def gathered_weights(sems, axes, name):
        def wait(after):
            send_sems, recv_sems, srcs, lands, _ = sems
            _, lands = _exchange_wait(send_sems, recv_sems, srcs, lands, after, name, "gather")
            return [_full_from_gathered(t, axis) for t, axis in zip(lands, axes)]

        return wait

    started = {}

    def scatter_grads(key, name):
        def start(*whole_grads):
            chunks = [
                _chunks_from_cols(t) if axis == 1 else t.reshape(N_DEV, -1, t.shape[1])
                for t, axis in zip(whole_grads, _SHARD_AXIS[key])
            ]
            started[key] = _exchange_start(chunks, no_order, name, "scatter")
            return started[key][4]

        return start

    def gather_small(small, loss_rows):
        started["small"] = _exchange_start([_pack_small(small, loss_rows)], no_order, "small_grads_gather_start", "gather")
        return started["small"][4]

    core = jnp.reshape(cc, (1,)).astype(jnp.int32)
    pos = jnp.stack([cc, 2 * cx + cy]).astype(jnp.int32)

    def reduce_w_in(send_in):
        (got_in,) = _sibling_exchange([send_in])
        pair_in = _pair_sum(send_in, got_in, core, "pair_sum_w_in")
        started["in"] = (send_in, got_in, _exchange_start([pair_in], no_order, "w_in_grads_chips_start", "chips"))
        return started["in"][2][4]

    def gather_norm1(dg1):
        rows = jnp.reshape(dg1, (8, LANES))
        started["norm1"] = _exchange_start([rows], no_order, "norm1_grad_gather_start", "gather")

    ffn, out = slice(4, 7), slice(1, 4)
    grad_x = _local_grads(
        x, loss_target, norm1_g, norm2_g, norm_f_g, b_forget, pool_mix, pool_scale, _w_in_pieces(gathered_in), ffn_sems[4],
        gathered_weights(out_sems, _SHARD_AXIS[out], "out_weights_gather_wait"),
        gathered_weights(ffn_sems, _SHARD_AXIS[ffn], "ffn_weights_gather_wait"),
        scatter_grads(ffn, "ffn_grads_scatter_start"), scatter_grads(out, "out_grads_scatter_start"), gather_small, reduce_w_in, gather_norm1,
    )
    send_in, got_in, chip_sems = started["in"]

    def scattered_updates(key, after, name):
        send_sems, recv_sems, srcs, lands, _ = started[key]
        srcs, lands = _exchange_wait(send_sems, recv_sems, srcs, lands, after, name, "scatter")
        return [
            _shard_update_direct(p, s, w, m, v, jnp.reshape(me, (1,)).astype(jnp.int32), "update_" + n)
            for p, s, w, m, v, n in zip(lands, srcs, w_sh[key], m_sh[key], v_sh[key], names[key])
        ]

    updates_out = scattered_updates(out, grad_x, "out_grads_scatter_wait")
    updates_ffn = scattered_updates(ffn, grad_x, "ffn_grads_scatter_wait")

    small_w = (norm1_g, norm2_g, norm_f_g, b_forget, pool_scale, pool_mix)
    small_m = (m_norm1_g, m_norm2_g, m_norm_f_g, m_b_forget, m_pool_scale, m_pool_mix)
    small_v = (v_norm1_g, v_norm2_g, v_norm_f_g, v_b_forget, v_pool_scale, v_pool_mix)
    send_sems, recv_sems, srcs, lands, _ = chip_sems
    _, (recv_in,) = _exchange_wait(send_sems, recv_sems, srcs, lands, updates_ffn[-1][0], "w_in_grads_chips_wait", "chips")
    update_in = _shard_update(send_in, got_in, recv_in, w_in, m_w_in, v_w_in, pos, "update_w_in")

    def gathered_small(key, after, name):
        send_sems, recv_sems, srcs, lands, _ = started[key]
        _, lands = _exchange_wait(send_sems, recv_sems, srcs, lands, after, name, "gather")
        return lands[0]

    parts = gathered_small("small", update_in[0], "small_grads_gather_wait")
    first_rows = gathered_small("norm1", parts, "norm1_grad_gather_wait")
    (g_s, d_s, nm_s, nv_s), loss = _small_update(parts, first_rows, small_w, small_m, small_v)
    g_w, d_w, nm_w, nv_w = zip(*(
        [stored(t, tr) for t in u] for u, tr in zip([update_in] + updates_out + updates_ffn, _TRANSPOSED)
    ))
    loss = loss.reshape(())
    (g1, g2, gf, gb, gsc, gmix), (d1, d2, df, db_, dsc, dmx) = g_s, d_s
    (m1, m2, mf, mb, msc, mmx), (v1, v2, vf, vb, vsc, vmx) = nm_s, nv_s

    def ordered(n1, win, b, mix, sc, wpo, wao, wout, n2, wg, wu, wd, nf):
        return (n1, win, b, mix, sc, wpo, wao, wout, n2, wg, wu, wd, nf)

    grads = ordered(g1, g_w[0], gb, gmix, gsc, g_w[1], g_w[2], g_w[3], g2, g_w[4], g_w[5], g_w[6], gf)
    deltas = ordered(d1, d_w[0], db_, dmx, dsc, d_w[1], d_w[2], d_w[3], d2, d_w[4], d_w[5], d_w[6], df)
    new_m = ordered(m1, nm_w[0], mb, mmx, msc, nm_w[1], nm_w[2], nm_w[3], m2, nm_w[4], nm_w[5], nm_w[6], mf)
    new_v = ordered(v1, nv_w[0], vb, vmx, vsc, nv_w[1], nv_w[2], nv_w[3], v2, nv_w[4], nv_w[5], nv_w[6], vf)
    return (loss, grad_x, *grads, *deltas, *new_m, *new_v)
```

```python
import jax
import jax.numpy as jnp
from jax import lax
from jax.experimental import pallas as pl
from jax.experimental.pallas import tpu as pltpu

F32 = jnp.float32
BF16 = jnp.bfloat16
MESH = pl.DeviceIdType.MESH

D_MODEL = 1024
POOL_WINDOWS = (2, 4, 8, 16)
POOL_WIDTH = 512
GROUP_DIM = 128
ATTN_WIDTH = 512
HEAD_DIM = 64
N_HEADS = 8
N_PAIRS = 4
D_FF = 2816
RMS_EPS = 1e-6
N_DEV = 8
LANES = 128
FL_PAD = 128

ADAM_LR = 0.001
ADAM_B1 = 0.9
ADAM_B2 = 0.999
ADAM_EPS = 1e-08
ADAM_WD = 0.01
ADAM_STEP = 10

VMEM_LIMIT = 56 * 1024 * 1024
VMEM_LIMIT_MAX = 60 * 1024 * 1024
ROW_TILE = 512
ATTN_BLOCK = 512
FF_CHUNK = 256
FF_ROW_TILE = 512
DW_TOKENS = 2048


def _mm(a, b):
    return jnp.dot(a, b, preferred_element_type=F32)


def _mm_nt(a, b):
    return lax.dot_general(a, b, (((1,), (1,)), ((), ())), preferred_element_type=F32)


def _mm_tn(a, b):
    return lax.dot_general(a, b, (((0,), (0,)), ((), ())), preferred_element_type=F32)


def _sigmoid(x):
    return 1.0 / (1.0 + jnp.exp(-x))


def _params(sem, vmem=VMEM_LIMIT):
    return pltpu.CompilerParams(dimension_semantics=sem, vmem_limit_bytes=vmem)


def _const_spec(shape):
    nd = len(shape)
    return pl.BlockSpec(shape, lambda *_: (0,) * nd, pipeline_mode=pl.Buffered(1))


def _rms_fwd(x, g):
    r = lax.rsqrt(jnp.mean(x * x, axis=-1, keepdims=True) + RMS_EPS)
    xh = x * r
    return xh * g, xh, r


def _rms_bwd(dy, xh, r, g):
    dxh = dy * g
    dx = r * (dxh - xh * jnp.mean(dxh * xh, axis=-1, keepdims=True))
    return dx, dy * xh


def _in_proj(x, g1, w_uqkv, w_fl, w_g, token):
    T = x.shape[0]
    tm = ROW_TILE

    def body(x_ref, g_ref, wa_ref, wf_ref, wg_ref, token_ref, h_ref, u_ref, qkv_ref, fl_ref, gt_ref):
        h, _, _ = _rms_fwd(x_ref[...], g_ref[...])
        hb = h.astype(BF16)
        h_ref[...] = hb
        z = _mm(hb, wa_ref[...])
        u_ref[...] = z[:, :POOL_WIDTH]
        qkv_ref[...] = z[:, POOL_WIDTH:].astype(BF16)
        fl_ref[...] = _mm(hb, wf_ref[...])
        gt_ref[...] = _mm(hb, wg_ref[...]).astype(BF16)

    row = lambda n: pl.BlockSpec((tm, n), lambda i: (i, 0))
    return pl.pallas_call(
        body,
        name="in_proj",
        grid=(T // tm,),
        in_specs=[row(D_MODEL), _const_spec((1, D_MODEL)), _const_spec(w_uqkv.shape), _const_spec(w_fl.shape), _const_spec(w_g.shape), _HBM],
        out_specs=[row(D_MODEL), row(POOL_WIDTH), row(3 * ATTN_WIDTH), row(FL_PAD), row(2 * D_MODEL)],
        out_shape=[
            jax.ShapeDtypeStruct((T, D_MODEL), BF16),
            jax.ShapeDtypeStruct((T, POOL_WIDTH), F32),
            jax.ShapeDtypeStruct((T, 3 * ATTN_WIDTH), BF16),
            jax.ShapeDtypeStruct((T, FL_PAD), F32),
            jax.ShapeDtypeStruct((T, 2 * D_MODEL), BF16),
        ],
        compiler_params=_params(("parallel",)),
    )(x, g1, w_uqkv, w_fl, w_g, token)


def _log_sigmoid(x):
    return jnp.minimum(x, 0.0) - jnp.log(1.0 + jnp.exp(-jnp.abs(x)))


def _forget_fwd(fl, b_pad, n_seq, S):
    def body(fl_ref, b_ref, fcol_ref):
        lf = _log_sigmoid(fl_ref[...] + b_ref[...])
        t = lf.T
        lane = lax.broadcasted_iota(jnp.int32, t.shape, 1)
        k = 1
        while k < S:
            t = t + jnp.where(lane >= k, pltpu.roll(t, k, 1), 0.0)
            k *= 2
        fcol_ref[...] = t.T

    return pl.pallas_call(
        body,
        name="forget_fwd",
        grid=(n_seq,),
        in_specs=[pl.BlockSpec((S, FL_PAD), lambda s: (s, 0)), _const_spec((1, FL_PAD))],
        out_specs=pl.BlockSpec((S, FL_PAD), lambda s: (s, 0)),
        out_shape=jax.ShapeDtypeStruct((n_seq * S, FL_PAD), F32),
        compiler_params=_params(("parallel",)),
    )(fl, b_pad)


def _window_pick(g, v2, v4, v8, v16):
    return jnp.where(g == 0, v2, jnp.where(g == 1, v4, jnp.where(g == 2, v8, v16)))


def _pool_fwd(u, mix_b, scale, n_seq, S):
    T = n_seq * S

    def body(u_ref, mix_ref, sc_ref, pm_ref, p2_ref, p3_ref):
        g = pl.program_id(1)
        uu = u_ref[...]
        row = lax.broadcasted_iota(jnp.int32, uu.shape, 0)

        def back(a, k):
            return jnp.where(row >= k, pltpu.roll(a, k, 0), 0.0)

        s2 = uu + back(uu, 1)
        s4 = s2 + back(s2, 2)
        s8 = s4 + back(s4, 4)
        s16 = s8 + back(s8, 8)
        w = _window_pick(g, 2.0, 4.0, 8.0, 16.0)
        cnt = jnp.minimum((row + 1).astype(F32), w)
        pm = _window_pick(g, s2, s4, s8, s16) / cnt - uu
        pmb = pm.astype(BF16)
        pm_ref[...] = pmb
        p2 = _mm(pmb, mix_ref[...])
        p2_ref[...] = p2
        p3_ref[...] = (p2 * sc_ref[...]).astype(BF16)

    grp = pl.BlockSpec((S, GROUP_DIM), lambda s, g: (s, g))
    return pl.pallas_call(
        body,
        name="pool_fwd",
        grid=(n_seq, len(POOL_WINDOWS)),
        in_specs=[
            grp,
            pl.BlockSpec((None, GROUP_DIM, GROUP_DIM), lambda s, g: (g, 0, 0)),
            pl.BlockSpec((1, GROUP_DIM), lambda s, g: (0, g)),
        ],
        out_specs=[grp, grp, grp],
        out_shape=[
            jax.ShapeDtypeStruct((T, POOL_WIDTH), BF16),
            jax.ShapeDtypeStruct((T, POOL_WIDTH), F32),
            jax.ShapeDtypeStruct((T, POOL_WIDTH), BF16),
        ],
        compiler_params=_params(("parallel", "parallel")),
    )(u, mix_b, scale)


def _split3(v):
    hi = v.astype(BF16).astype(F32)
    r = v - hi
    mid = r.astype(BF16).astype(F32)
    lo = (r - mid).astype(BF16).astype(F32)
    return hi, mid, lo


def _bias_lanes(v):
    hi, mid, lo = _split3(v)
    lane = lax.broadcasted_iota(jnp.int32, (1, LANES), 1)
    packed = jnp.where(lane < N_HEADS, hi, jnp.where(lane < 2 * N_HEADS, pltpu.roll(mid, N_HEADS, 1), pltpu.roll(lo, 2 * N_HEADS, 1)))
    return jnp.where(lane < 3 * N_HEADS, packed, 0.0).astype(BF16)


def _bias_placement(slot):
    row = lax.broadcasted_iota(jnp.int32, (LANES, N_HEADS * LANES), 0)
    col = lax.broadcasted_iota(jnp.int32, (LANES, N_HEADS * LANES), 1)
    h = col // LANES
    n = col % LANES - jnp.where(h % 2 == 0, HEAD_DIM, 0) - 3 * slot
    return ((n >= 0) & (n < 3) & (row == N_HEADS * n + h)).astype(BF16)


def _augment(xp, h, bias, ones_slot):
    lane = lax.broadcasted_iota(jnp.int32, (1, LANES), 1)
    hh = h % 2
    head = (lane >= HEAD_DIM * hh) & (lane < HEAD_DIM * (hh + 1))
    b = HEAD_DIM * (1 - hh)
    rest = jnp.zeros_like(xp) if bias is None else bias[:, h * LANES : (h + 1) * LANES]
    out = jnp.where(head, xp, rest)
    if ones_slot is not None:
        out = jnp.where((lane >= b + 3 * ones_slot) & (lane < b + 3 * ones_slot + 3), jnp.ones_like(xp), out)
    return out


def _attn_fwd(qkv, fcol, n_seq, S):
    T = n_seq * S
    tb = ATTN_BLOCK
    nq = S // tb
    scale = HEAD_DIM ** -0.5

    def body(q_ref, k_ref, v_ref, fc_ref, o_ref, st_ref, qa_sc, ka_sc, m_sc, l_sc, acc_sc):
        i = pl.program_id(1)
        lane = lax.broadcasted_iota(jnp.int32, (1, LANES), 1)
        low = lane < HEAD_DIM

        @pl.when(i == 0)
        def _():
            place = _bias_placement(1)

            def rows_ka(r, carry):
                r0 = pl.multiple_of(r * tb, tb)
                bias = _mm(_bias_lanes(-fc_ref[pl.ds(r0, tb), :]), place).astype(BF16)
                for h in range(N_HEADS):
                    kp = k_ref[pl.ds(r0, tb), (h // 2) * LANES : (h // 2 + 1) * LANES] * scale
                    ka_sc[h, pl.ds(r0, tb), :] = _augment(kp, h, bias, 0)
                return carry

            lax.fori_loop(0, nq, rows_ka, 0)

        q0 = pl.multiple_of(i * tb, tb)
        bias = _mm(_bias_lanes(fc_ref[pl.ds(q0, tb), :]), _bias_placement(0)).astype(BF16)
        for h in range(N_HEADS):
            qa_sc[h] = _augment(q_ref[:, (h // 2) * LANES : (h // 2 + 1) * LANES], h, bias, 1)
        m_sc[...] = jnp.full(m_sc.shape, -jnp.inf, F32)
        l_sc[...] = jnp.zeros_like(l_sc)
        acc_sc[...] = jnp.zeros_like(acc_sc)
        causal = lax.broadcasted_iota(jnp.int32, (tb, tb), 1) <= lax.broadcasted_iota(jnp.int32, (tb, tb), 0)

        def step(j, masked):
            c0 = pl.multiple_of(j * tb, tb)
            for p in range(N_PAIRS):
                vb = v_ref[pl.ds(c0, tb), p * LANES : (p + 1) * LANES]
                pv, al = [], []
                for hh in range(2):
                    h = 2 * p + hh
                    s = _mm_nt(qa_sc[h], ka_sc[h, pl.ds(c0, tb), :])
                    if masked:
                        s = jnp.where(causal, s, -jnp.inf)
                    m_old = m_sc[h]
                    m_new = jnp.maximum(m_old, jnp.max(s, axis=1, keepdims=True))
                    alpha = jnp.exp(m_old - m_new)
                    pe = jnp.exp(s - jnp.concatenate([m_new] * (tb // LANES), axis=1))
                    l_sc[h] = alpha * l_sc[h] + jnp.sum(pe, axis=1, keepdims=True)
                    m_sc[h] = m_new
                    pv.append(_mm(pe.astype(BF16), vb))
                    al.append(alpha)
                acc_sc[p] = jnp.where(low, al[0], al[1]) * acc_sc[p] + jnp.where(low, pv[0], pv[1])

        def loop_body(j, carry):
            step(j, False)
            return carry

        lax.fori_loop(0, i, loop_body, 0)
        step(i, True)
        st = jnp.zeros((tb, LANES), F32)
        for p in range(N_PAIRS):
            lp = jnp.where(low, l_sc[2 * p], l_sc[2 * p + 1])
            o_ref[:, p * LANES : (p + 1) * LANES] = (acc_sc[p] / lp).astype(BF16)
            for h in (2 * p, 2 * p + 1):
                st = jnp.where(lane == h, m_sc[h] + jnp.log(l_sc[h]), st)
        st_ref[...] = st

    return pl.pallas_call(
        body,
        name="attn_fwd",
        grid=(n_seq, nq),
        in_specs=[
            pl.BlockSpec((tb, ATTN_WIDTH), lambda s, i: (s * nq + i, 0)),
            pl.BlockSpec((S, ATTN_WIDTH), lambda s, i: (s, 1)),
            pl.BlockSpec((S, ATTN_WIDTH), lambda s, i: (s, 2)),
            pl.BlockSpec((S, LANES), lambda s, i: (s, 0)),
        ],
        out_specs=[
            pl.BlockSpec((tb, ATTN_WIDTH), lambda s, i: (s * nq + i, 0)),
            pl.BlockSpec((tb, LANES), lambda s, i: (s * nq + i, 0)),
        ],
        out_shape=[jax.ShapeDtypeStruct((T, ATTN_WIDTH), BF16), jax.ShapeDtypeStruct((T, LANES), F32)],
        scratch_shapes=[
            pltpu.VMEM((N_HEADS, tb, LANES), BF16),
            pltpu.VMEM((N_HEADS, S, LANES), BF16),
            pltpu.VMEM((N_HEADS, tb, LANES), F32),
            pltpu.VMEM((N_HEADS, tb, LANES), F32),
            pltpu.VMEM((N_PAIRS, tb, LANES), F32),
        ],
        compiler_params=_params(("parallel", "arbitrary")),
    )(qkv, qkv, qkv, fcol)


def _mix_out(a, p3, gates, x, w_ao, w_po, w_out):
    T = x.shape[0]
    tm = ROW_TILE

    def body(a_ref, p3_ref, gt_ref, x_ref, wao_ref, wpo_ref, wout_ref, mg_ref, x1_ref, ay_ref, py_ref):
        ay = _mm(a_ref[...], wao_ref[...])
        py = _mm(p3_ref[...], wpo_ref[...])
        ay_ref[...] = ay.astype(BF16)
        py_ref[...] = py.astype(BF16)
        sp = _sigmoid(gt_ref[:, :D_MODEL].astype(F32))
        sa = _sigmoid(gt_ref[:, D_MODEL:].astype(F32))
        mb = (sp * py + sa * ay).astype(BF16)
        mg_ref[...] = mb
        x1_ref[...] = x_ref[...] + _mm(mb, wout_ref[...])

    row = lambda n: pl.BlockSpec((tm, n), lambda i: (i, 0))
    return pl.pallas_call(
        body,
        name="mix_out",
        grid=(T // tm,),
        in_specs=[
            row(ATTN_WIDTH), row(POOL_WIDTH), row(2 * D_MODEL), row(D_MODEL),
            _const_spec(w_ao.shape), _const_spec(w_po.shape), _const_spec(w_out.shape),
        ],
        out_specs=[row(D_MODEL), row(D_MODEL), row(D_MODEL), row(D_MODEL)],
        out_shape=[
            jax.ShapeDtypeStruct((T, D_MODEL), BF16), jax.ShapeDtypeStruct((T, D_MODEL), F32),
            jax.ShapeDtypeStruct((T, D_MODEL), BF16), jax.ShapeDtypeStruct((T, D_MODEL), BF16),
        ],
        compiler_params=_params(("parallel",)),
    )(a, p3, gates, x, w_ao, w_po, w_out)


def _ffn_fwd(x1, g2, gf, tgt, w_gate_t, w_up_t, w_down):
    T = x1.shape[0]
    tm = min(T, FF_ROW_TILE)
    nt = T // tm
    nc = D_FF // FF_CHUNK

    def body(x1_ref, g2_ref, gf_ref, tg_ref, wg_ref, wu_ref, wd_ref, h2_ref, gate_ref, up_ref, act_ref, dx2_ref, loss_ref, dgf_ref):
        x1v = x1_ref[...]
        h2, _, _ = _rms_fwd(x1v, g2_ref[...])
        h2b = h2.astype(BF16)
        h2_ref[...] = h2b
        for c in range(nc):
            sl = slice(c * FF_CHUNK, (c + 1) * FF_CHUNK)
            gate = _mm_nt(h2b, wg_ref[sl, :])
            up = _mm_nt(h2b, wu_ref[sl, :])
            gate_ref[:, sl] = gate.astype(BF16)
            up_ref[:, sl] = up.astype(BF16)
            act_ref[:, sl] = (gate * _sigmoid(gate) * up).astype(BF16)
        acc = x1v + _mm(act_ref[...], wd_ref[...])
        gfv = gf_ref[...]
        y, xh, r = _rms_fwd(acc, gfv)
        err = y - tg_ref[...]
        part = 0.5 * jnp.sum(jnp.mean(err * err, axis=-1, keepdims=True), axis=0, keepdims=True)
        dx2, dgrow = _rms_bwd(err * (1.0 / D_MODEL), xh, r, gfv)
        dx2_ref[...] = dx2

        @pl.when(pl.program_id(0) == 0)
        def _():
            dgf_ref[...] = jnp.zeros_like(dgf_ref)
            loss_ref[...] = jnp.zeros_like(loss_ref)

        dgf_ref[...] += jnp.sum(dgrow, axis=0, keepdims=True)
        loss_ref[...] += jnp.broadcast_to(part, loss_ref.shape)

    row = lambda n: pl.BlockSpec((tm, n), lambda i: (i, 0))
    return pl.pallas_call(
        body,
        name="ffn_fwd",
        grid=(nt,),
        in_specs=[
            row(D_MODEL), _const_spec((1, D_MODEL)), _const_spec((1, D_MODEL)), row(D_MODEL),
            _const_spec(w_gate_t.shape), _const_spec(w_up_t.shape), _const_spec(w_down.shape),
        ],
        out_specs=[
            row(D_MODEL), row(D_FF), row(D_FF), row(D_FF), row(D_MODEL),
            pl.BlockSpec((8, LANES), lambda i: (0, 0)),
            pl.BlockSpec((1, D_MODEL), lambda i: (0, 0)),
        ],
        out_shape=[
            jax.ShapeDtypeStruct((T, D_MODEL), BF16),
            jax.ShapeDtypeStruct((T, D_FF), BF16),
            jax.ShapeDtypeStruct((T, D_FF), BF16),
            jax.ShapeDtypeStruct((T, D_FF), BF16),
            jax.ShapeDtypeStruct((T, D_MODEL), F32),
            jax.ShapeDtypeStruct((8, LANES), F32),
            jax.ShapeDtypeStruct((1, D_MODEL), F32),
        ],
        compiler_params=_params(("arbitrary",)),
    )(x1, g2, gf, tgt, w_gate_t, w_up_t, w_down)


def _ffn_bwd(dx2, gate, up, x1, g2, w_gate_t, w_up_t, w_down):
    T = x1.shape[0]
    tm = min(T, FF_ROW_TILE)
    nc = D_FF // FF_CHUNK

    def body(dx2_ref, gate_ref, up_ref, x1_ref, g2_ref, wg_ref, wu_ref, wd_ref, dgate_ref, dup_ref, dx1_ref, dg2_ref):
        dx2v = dx2_ref[...]
        dx2b = dx2v.astype(BF16)
        for c in range(nc):
            sl = slice(c * FF_CHUNK, (c + 1) * FF_CHUNK)
            dact = _mm_nt(dx2b, wd_ref[sl, :])
            gate = gate_ref[:, sl].astype(F32)
            sg = _sigmoid(gate)
            silu = gate * sg
            dgate = (dact * up_ref[:, sl].astype(F32) * (sg * (1.0 + gate * (1.0 - sg)))).astype(BF16)
            dup = (dact * silu).astype(BF16)
            dgate_ref[:, sl] = dgate
            dup_ref[:, sl] = dup
        dh2 = _mm(dgate_ref[...], wg_ref[...]) + _mm(dup_ref[...], wu_ref[...])
        g2v = g2_ref[...]
        _, xh, r = _rms_fwd(x1_ref[...], g2v)
        dxn, dgrow = _rms_bwd(dh2, xh, r, g2v)
        dx1_ref[...] = dx2v + dxn

        @pl.when(pl.program_id(0) == 0)
        def _():
            dg2_ref[...] = jnp.zeros_like(dg2_ref)

        dg2_ref[...] += jnp.sum(dgrow, axis=0, keepdims=True)

    row = lambda n: pl.BlockSpec((tm, n), lambda i: (i, 0))
    return pl.pallas_call(
        body,
        name="ffn_bwd",
        grid=(T // tm,),
        in_specs=[
            row(D_MODEL), row(D_FF), row(D_FF), row(D_MODEL), _const_spec((1, D_MODEL)),
            _const_spec(w_gate_t.shape), _const_spec(w_up_t.shape), _const_spec(w_down.shape),
        ],
        out_specs=[row(D_FF), row(D_FF), row(D_MODEL), pl.BlockSpec((1, D_MODEL), lambda i: (0, 0))],
        out_shape=[
            jax.ShapeDtypeStruct((T, D_FF), BF16),
            jax.ShapeDtypeStruct((T, D_FF), BF16),
            jax.ShapeDtypeStruct((T, D_MODEL), F32),
            jax.ShapeDtypeStruct((1, D_MODEL), F32),
        ],
        compiler_params=_params(("arbitrary",), VMEM_LIMIT_MAX),
    )(dx2, gate, up, x1, g2, w_gate_t, w_up_t, w_down)


def _mix_bwd(dx1, gates, pool_y, attn_y, p2, scale, w_out, w_ao, w_po, token):
    T = dx1.shape[0]
    tm = ROW_TILE

    def body(dx1_ref, gt_ref, py_ref, ay_ref, p2_ref, sc_ref, wout_ref, wao_ref, wpo_ref, token_ref, dgt_ref, dpy_ref, day_ref, da_ref, dp2_ref, dsc_ref):
        dm = _mm_nt(dx1_ref[...].astype(BF16), wout_ref[...])
        sp = _sigmoid(gt_ref[:, :D_MODEL].astype(F32))
        sa = _sigmoid(gt_ref[:, D_MODEL:].astype(F32))
        dgt_ref[:, :D_MODEL] = (dm * py_ref[...].astype(F32) * (sp * (1.0 - sp))).astype(BF16)
        dgt_ref[:, D_MODEL:] = (dm * ay_ref[...].astype(F32) * (sa * (1.0 - sa))).astype(BF16)
        dpy = (dm * sp).astype(BF16)
        day = (dm * sa).astype(BF16)
        dpy_ref[...] = dpy
        day_ref[...] = day
        da_ref[...] = _mm_nt(day, wao_ref[...]).astype(BF16)
        dp3 = _mm_nt(dpy, wpo_ref[...])
        dp2_ref[...] = (dp3 * sc_ref[...]).astype(BF16)

        @pl.when(pl.program_id(0) == 0)
        def _():
            dsc_ref[...] = jnp.zeros_like(dsc_ref)

        dsc_ref[...] += jnp.sum(dp3 * p2_ref[...], axis=0, keepdims=True)

    row = lambda n: pl.BlockSpec((tm, n), lambda i: (i, 0))
    return pl.pallas_call(
        body,
        name="mix_bwd",
        grid=(T // tm,),
        in_specs=[
            row(D_MODEL), row(2 * D_MODEL), row(D_MODEL), row(D_MODEL), row(POOL_WIDTH), _const_spec((1, POOL_WIDTH)),
            _const_spec(w_out.shape), _const_spec(w_ao.shape), _const_spec(w_po.shape), _HBM,
        ],
        out_specs=[row(2 * D_MODEL), row(D_MODEL), row(D_MODEL), row(ATTN_WIDTH), row(POOL_WIDTH), pl.BlockSpec((1, POOL_WIDTH), lambda i: (0, 0))],
        out_shape=[
            jax.ShapeDtypeStruct((T, 2 * D_MODEL), BF16),
            jax.ShapeDtypeStruct((T, D_MODEL), BF16),
            jax.ShapeDtypeStruct((T, D_MODEL), BF16),
            jax.ShapeDtypeStruct((T, ATTN_WIDTH), BF16),
            jax.ShapeDtypeStruct((T, POOL_WIDTH), BF16),
            jax.ShapeDtypeStruct((1, POOL_WIDTH), F32),
        ],
        compiler_params=_params(("arbitrary",)),
    )(dx1, gates, pool_y, attn_y, p2, scale, w_out, w_ao, w_po, token)


def _pool_bwd(dp2, pm, mix_b, token, n_seq, S):
    T = n_seq * S

    def body(dp2_ref, pm_ref, mix_ref, token_ref, du_ref, dmix_ref):
        g = pl.program_id(0)
        dp2v = dp2_ref[...]
        dpm = _mm_nt(dp2v, mix_ref[...])
        row = lax.broadcasted_iota(jnp.int32, dpm.shape, 0)
        w = _window_pick(g, 2.0, 4.0, 8.0, 16.0)
        e = dpm / jnp.minimum((row + 1).astype(F32), w)

        def ahead(a, k):
            return jnp.where(row < S - k, pltpu.roll(a, S - k, 0), 0.0)

        r2 = e + ahead(e, 1)
        r4 = r2 + ahead(r2, 2)
        r8 = r4 + ahead(r4, 4)
        r16 = r8 + ahead(r8, 8)
        du_ref[...] = (_window_pick(g, r2, r4, r8, r16) - dpm).astype(BF16)

        @pl.when(pl.program_id(1) == 0)
        def _():
            dmix_ref[...] = jnp.zeros_like(dmix_ref)

        dmix_ref[...] += _mm_tn(pm_ref[...], dp2v)

    grp = pl.BlockSpec((S, GROUP_DIM), lambda g, s: (s, g))
    mixs = pl.BlockSpec((None, GROUP_DIM, GROUP_DIM), lambda g, s: (g, 0, 0))
    return pl.pallas_call(
        body,
        name="pool_bwd",
        grid=(len(POOL_WINDOWS), n_seq),
        in_specs=[grp, grp, mixs, _HBM],
        out_specs=[grp, mixs],
        out_shape=[jax.ShapeDtypeStruct((T, POOL_WIDTH), BF16), jax.ShapeDtypeStruct((len(POOL_WINDOWS), GROUP_DIM, GROUP_DIM), F32)],
        compiler_params=_params(("parallel", "arbitrary")),
    )(dp2, pm, mix_b, token)


def _attn_bwd(qkv, da, a, fcol, lse, n_seq, S):
    T = n_seq * S
    tb = ATTN_BLOCK
    nb = S // tb
    scale = HEAD_DIM ** -0.5

    def body(q_ref, k_ref, v_ref, do_ref, o_ref, fc_ref, st_ref, dq_ref, dk_ref, dv_ref, dfk_ref, dfq_ref,
             qa_sc, doa_sc, qat_sc, doat_sc, dq_acc, ka_sc, va_sc, dkt_sc, dvt_sc):
        j = pl.program_id(1)
        lane = lax.broadcasted_iota(jnp.int32, (1, LANES), 1)
        low = lane < HEAD_DIM

        @pl.when(j == 0)
        def _():
            dq_acc[...] = jnp.zeros_like(dq_acc)
            place = _bias_placement(0)

            def rows_q(i, carry):
                r0 = pl.multiple_of(i * tb, tb)
                delta = jnp.zeros((tb, LANES), F32)
                for h in range(N_HEADS):
                    pair = slice((h // 2) * LANES, (h // 2 + 1) * LANES)
                    prod = do_ref[pl.ds(r0, tb), pair].astype(F32) * o_ref[pl.ds(r0, tb), pair].astype(F32)
                    head = (lane >= HEAD_DIM * (h % 2)) & (lane < HEAD_DIM * (h % 2 + 1))
                    delta = jnp.where(lane == h, jnp.sum(jnp.where(head, prod, 0.0), axis=1, keepdims=True), delta)
                cq = fc_ref[pl.ds(r0, tb), :] - st_ref[pl.ds(r0, tb), :]
                q_bias = _mm(_bias_lanes(cq), place).astype(BF16)
                do_bias = _mm(_bias_lanes(-delta), place).astype(BF16)
                for h in range(N_HEADS):
                    pair = slice((h // 2) * LANES, (h // 2 + 1) * LANES)
                    qa = _augment(q_ref[pl.ds(r0, tb), pair], h, q_bias, 1)
                    doa = _augment(do_ref[pl.ds(r0, tb), pair], h, do_bias, None)
                    qa_sc[h, pl.ds(r0, tb), :] = qa
                    doa_sc[h, pl.ds(r0, tb), :] = doa
                    qat_sc[h, i] = qa.astype(F32).T.astype(BF16)
                    doat_sc[h, i] = doa.astype(F32).T.astype(BF16)
                return carry

            lax.fori_loop(0, nb, rows_q, 0)

        c0 = pl.multiple_of(j * tb, tb)
        k_bias = _mm(_bias_lanes(-fc_ref[pl.ds(c0, tb), :]), _bias_placement(1)).astype(BF16)
        for h in range(N_HEADS):
            pair = slice((h // 2) * LANES, (h // 2 + 1) * LANES)
            ka_sc[h] = _augment(k_ref[:, pair] * scale, h, k_bias, 0)
            va_sc[h] = _augment(v_ref[:, pair], h, None, 0)
        dkt_sc[...] = jnp.zeros_like(dkt_sc)
        dvt_sc[...] = jnp.zeros_like(dvt_sc)
        causal = lax.broadcasted_iota(jnp.int32, (tb, tb), 1) <= lax.broadcasted_iota(jnp.int32, (tb, tb), 0)

        def step(i, masked):
            r0 = pl.multiple_of(i * tb, tb)
            for h in range(N_HEADS):
                s = _mm_nt(qa_sc[h, pl.ds(r0, tb), :], ka_sc[h])
                if masked:
                    s = jnp.where(causal, s, -jnp.inf)
                pr = jnp.exp(s)
                dvt_sc[h] += _mm(doat_sc[h, i], pr.astype(BF16))
                dsb = (pr * _mm_nt(doa_sc[h, pl.ds(r0, tb), :], va_sc[h])).astype(BF16)
                dkt_sc[h] += _mm(qat_sc[h, i], dsb)
                dq_acc[h, pl.ds(r0, tb), :] += _mm(dsb, ka_sc[h])

        step(j, True)

        def loop_body(i, carry):
            step(i, False)
            return carry

        lax.fori_loop(j + 1, nb, loop_body, 0)
        dfk = jnp.zeros((tb, LANES), F32)
        for p in range(N_PAIRS):
            dk = [dkt_sc[2 * p + hh].T for hh in range(2)]
            dv = [dvt_sc[2 * p + hh].T for hh in range(2)]
            dk_ref[:, p * LANES : (p + 1) * LANES] = (jnp.where(low, dk[0], dk[1]) * scale).astype(BF16)
            dv_ref[:, p * LANES : (p + 1) * LANES] = jnp.where(low, dv[0], dv[1]).astype(BF16)
            for hh in range(2):
                b = HEAD_DIM * (1 - hh) + 3
                dfk = jnp.where(lane == 2 * p + hh, -dk[hh][:, b : b + 1], dfk)
        dfk_ref[...] = dfk

        @pl.when(j == nb - 1)
        def _():
            def rows_dq(i, carry):
                r0 = pl.multiple_of(i * tb, tb)
                dfq = jnp.zeros((tb, LANES), F32)
                for p in range(N_PAIRS):
                    parts = [dq_acc[2 * p + hh, pl.ds(r0, tb), :] for hh in range(2)]
                    dq_ref[pl.ds(r0, tb), p * LANES : (p + 1) * LANES] = jnp.where(low, parts[0], parts[1]).astype(BF16)
                    for hh in range(2):
                        b = HEAD_DIM * (1 - hh)
                        dfq = jnp.where(lane == 2 * p + hh, parts[hh][:, b : b + 1], dfq)
                dfq_ref[pl.ds(r0, tb), :] = dfq
                return carry

            lax.fori_loop(0, nb, rows_dq, 0)

    seq = lambda w, col: pl.BlockSpec((S, w), lambda s, j: (s, col))
    seq_in = lambda w, col: pl.BlockSpec((S, w), lambda s, j: (s, col), pipeline_mode=pl.Buffered(1))
    blk = lambda w, col: pl.BlockSpec((tb, w), lambda s, j: (s * nb + j, col))
    return pl.pallas_call(
        body,
        name="attn_bwd",
        grid=(n_seq, nb),
        in_specs=[seq_in(ATTN_WIDTH, 0), blk(ATTN_WIDTH, 1), blk(ATTN_WIDTH, 2), seq_in(ATTN_WIDTH, 0), seq_in(ATTN_WIDTH, 0), seq_in(LANES, 0), seq_in(LANES, 0)],
        out_specs=[seq(ATTN_WIDTH, 0), blk(ATTN_WIDTH, 0), blk(ATTN_WIDTH, 0), blk(LANES, 0), seq(LANES, 0)],
        out_shape=[
            jax.ShapeDtypeStruct((T, ATTN_WIDTH), BF16),
            jax.ShapeDtypeStruct((T, ATTN_WIDTH), BF16),
            jax.ShapeDtypeStruct((T, ATTN_WIDTH), BF16),
            jax.ShapeDtypeStruct((T, LANES), F32),
            jax.ShapeDtypeStruct((T, LANES), F32),
        ],
        scratch_shapes=[
            pltpu.VMEM((N_HEADS, S, LANES), BF16),
            pltpu.VMEM((N_HEADS, S, LANES), BF16),
            pltpu.VMEM((N_HEADS, nb, LANES, tb), BF16),
            pltpu.VMEM((N_HEADS, nb, LANES, tb), BF16),
            pltpu.VMEM((N_HEADS, S, LANES), F32),
            pltpu.VMEM((N_HEADS, tb, LANES), BF16),
            pltpu.VMEM((N_HEADS, tb, LANES), BF16),
            pltpu.VMEM((N_HEADS, LANES, tb), F32),
            pltpu.VMEM((N_HEADS, LANES, tb), F32),
        ],
        compiler_params=_params(("parallel", "arbitrary"), VMEM_LIMIT_MAX),
    )(qkv, qkv, qkv, da, a, fcol, lse)


def _forget_bwd(dfk, dfq, fl, b_pad, n_seq, S):
    def body(df_ref, dfq_ref, fl_ref, b_ref, dfl_ref, db_ref):
        t = (df_ref[...] + dfq_ref[...]).T
        lane = lax.broadcasted_iota(jnp.int32, t.shape, 1)
        k = 1
        while k < S:
            t = t + jnp.where(lane < S - k, pltpu.roll(t, S - k, 1), 0.0)
            k *= 2
        dfl = t.T * _sigmoid(-(fl_ref[...] + b_ref[...]))
        dfl_ref[...] = dfl.astype(BF16)

        @pl.when(pl.program_id(0) == 0)
        def _():
            db_ref[...] = jnp.zeros_like(db_ref)

        db_ref[...] += jnp.sum(dfl, axis=0, keepdims=True)

    return pl.pallas_call(
        body,
        name="forget_bwd",
        grid=(n_seq,),
        in_specs=[
            pl.BlockSpec((S, LANES), lambda s: (s, 0)),
            pl.BlockSpec((S, LANES), lambda s: (s, 0)),
            pl.BlockSpec((S, FL_PAD), lambda s: (s, 0)),
            _const_spec((1, FL_PAD)),
        ],
        out_specs=[pl.BlockSpec((S, FL_PAD), lambda s: (s, 0)), pl.BlockSpec((1, FL_PAD), lambda s: (0, 0))],
        out_shape=[jax.ShapeDtypeStruct((n_seq * S, FL_PAD), BF16), jax.ShapeDtypeStruct((1, FL_PAD), F32)],
        compiler_params=_params(("arbitrary",)),
    )(dfk, dfq, fl, b_pad)


def _in_proj_bwd(du, dq, dk, dv, dfl, dgates, x, dx1, g1, w_uqkv, w_fl, w_g, token):
    T = x.shape[0]
    tm = ROW_TILE

    def body(du_ref, dq_ref, dk_ref, dv_ref, dfl_ref, dgt_ref, x_ref, dx1_ref, g_ref, wa_ref, wf_ref, wg_ref, token_ref, dx_ref, dg_ref):
        dz = jnp.concatenate([du_ref[...], dq_ref[...], dk_ref[...], dv_ref[...]], axis=1)
        dh = _mm_nt(dz, wa_ref[...]) + _mm_nt(dgt_ref[...], wg_ref[...]) + _mm_nt(dfl_ref[...], wf_ref[...])
        gv = g_ref[...]
        _, xh, r = _rms_fwd(x_ref[...], gv)
        dxn, dgrow = _rms_bwd(dh, xh, r, gv)
        dx_ref[...] = dx1_ref[...] + dxn

        @pl.when(pl.program_id(0) == 0)
        def _():
            dg_ref[...] = jnp.zeros_like(dg_ref)

        dg_ref[...] += jnp.sum(dgrow, axis=0, keepdims=True)

    row = lambda n: pl.BlockSpec((tm, n), lambda i: (i, 0))
    return pl.pallas_call(
        body,
        name="in_proj_bwd",
        grid=(T // tm,),
        in_specs=[
            row(512), row(512), row(512), row(512), row(FL_PAD), row(2 * D_MODEL), row(D_MODEL), row(D_MODEL), _const_spec((1, D_MODEL)),
            _const_spec(w_uqkv.shape), _const_spec(w_fl.shape), _const_spec(w_g.shape), _HBM,
        ],
        out_specs=[row(D_MODEL), pl.BlockSpec((1, D_MODEL), lambda i: (0, 0))],
        out_shape=[jax.ShapeDtypeStruct((T, D_MODEL), F32), jax.ShapeDtypeStruct((1, D_MODEL), F32)],
        compiler_params=_params(("arbitrary",)),
    )(du, dq, dk, dv, dfl, dgates, x, dx1, g1, w_uqkv, w_fl, w_g, token)


def _pick_block(n):
    for b in (1024, 512, 1408, 256, 128):
        if n % b == 0:
            return b
    raise ValueError(n)


def _matmul_tn(a, b, name):
    T, K = a.shape
    N = b.shape[1]
    bt, bk, bn = min(T, DW_TOKENS), _pick_block(K), _pick_block(N)
    nt = T // bt

    def body(a_ref, b_ref, o_ref, acc):
        @pl.when(pl.program_id(2) == 0)
        def _():
            acc[...] = jnp.zeros_like(acc)

        acc[...] += _mm_tn(a_ref[...].astype(BF16), b_ref[...].astype(BF16))

        @pl.when(pl.program_id(2) == nt - 1)
        def _():
            o_ref[...] = acc[...].astype(BF16)

    return pl.pallas_call(
        body,
        name=name,
        grid=(K // bk, N // bn, nt),
        in_specs=[pl.BlockSpec((bt, bk), lambda k, n, t: (t, k)), pl.BlockSpec((bt, bn), lambda k, n, t: (t, n))],
        out_specs=pl.BlockSpec((bk, bn), lambda k, n, t: (k, n)),
        out_shape=jax.ShapeDtypeStruct((K, N), BF16),
        scratch_shapes=[pltpu.VMEM((bk, bn), F32)],
        compiler_params=_params(("parallel", "parallel", "arbitrary")),
    )(a, b)


W_IN_A = POOL_WIDTH + 3 * ATTN_WIDTH
W_IN_SHARD = (W_IN_A + N_HEADS + 2 * D_MODEL) // N_DEV
_W_IN_PIECES = ((0, W_IN_A), (W_IN_A, W_IN_A + N_HEADS), (W_IN_A + N_HEADS, W_IN_A + N_HEADS + 2 * D_MODEL))


def _w_in_segments(d):
    lo, hi = d * W_IN_SHARD, (d + 1) * W_IN_SHARD
    out = []
    for p, (a, b) in enumerate(_W_IN_PIECES):
        s, e = max(lo, a), min(hi, b)
        if s < e:
            out.append((p, s - a, s - lo, e - s))
    return out


def _w_in_pieces(gathered):
    tm = ROW_TILE // 2

    def body(g_ref, wa_ref, wf_ref, wg_ref):
        outs = (wa_ref, wf_ref, wg_ref)
        wf_ref[...] = jnp.zeros_like(wf_ref)
        for d in range(N_DEV):
            for p, at, frm, n in _w_in_segments(d):
                outs[p][:, at : at + n] = g_ref[d, :, frm : frm + n]

    return pl.pallas_call(
        body,
        name="w_in_pieces",
        grid=(D_MODEL // tm,),
        in_specs=[pl.BlockSpec((N_DEV, tm, W_IN_SHARD), lambda i: (0, i, 0))],
        out_specs=[pl.BlockSpec((tm, W_IN_A), lambda i: (i, 0)), pl.BlockSpec((tm, FL_PAD), lambda i: (i, 0)), pl.BlockSpec((tm, 2 * D_MODEL), lambda i: (i, 0))],
        out_shape=[
            jax.ShapeDtypeStruct((D_MODEL, W_IN_A), gathered.dtype),
            jax.ShapeDtypeStruct((D_MODEL, FL_PAD), gathered.dtype),
            jax.ShapeDtypeStruct((D_MODEL, 2 * D_MODEL), gathered.dtype),
        ],
        compiler_params=_params(("parallel",)),
    )(gathered)


def _dw_in(h, du, dq, dk, dv, dfl, dgates, token):
    T = h.shape[0]
    bt, bk = min(T, DW_TOKENS // 2), 512
    nt = T // bt
    pieces = (du, dq, dk, dv, dfl, dgates)
    offs = [0]
    for p in pieces:
        offs.append(offs[-1] + p.shape[1])

    def body(h_ref, *rest):
        refs, o_ref, acc = rest[: len(pieces)], rest[-2], rest[-1]

        @pl.when(pl.program_id(1) == 0)
        def _():
            acc[...] = jnp.zeros_like(acc)

        ht = h_ref[...].T
        for ref, at in zip(refs, offs):
            acc[:, at : at + ref.shape[1]] += _mm(ht, ref[...])

        @pl.when(pl.program_id(1) == nt - 1)
        def _():
            starts = (0, W_IN_A, W_IN_A + FL_PAD)
            for d in range(N_DEV):
                for p, at, to, n in _w_in_segments(d):
                    o_ref[d % 2, d // 2, :, to : to + n] = acc[:, starts[p] + at : starts[p] + at + n].astype(BF16)

    return pl.pallas_call(
        body,
        name="dw_in",
        grid=(D_MODEL // bk, nt),
        in_specs=[pl.BlockSpec((bt, bk), lambda k, t: (t, k))] + [pl.BlockSpec((bt, p.shape[1]), lambda k, t: (t, 0)) for p in pieces] + [_HBM],
        out_specs=pl.BlockSpec((2, 4, bk, W_IN_SHARD), lambda k, t: (0, 0, k, 0)),
        out_shape=jax.ShapeDtypeStruct((2, 4, D_MODEL, W_IN_SHARD), BF16),
        scratch_shapes=[pltpu.VMEM((bk, offs[-1]), F32)],
        compiler_params=_params(("parallel", "arbitrary")),
    )(h, *pieces, token)


def _position():
    return lax.axis_index("x"), lax.axis_index("y"), lax.axis_index("c")


_HBM = pl.BlockSpec(memory_space=pl.ANY)


def _all_gather(blocks, name):
    n = len(blocks)

    def body(*refs):
        xs, outs = refs[:n], refs[n : 2 * n]
        send_sems, recv_sems, local_sems = refs[2 * n :]
        x, y, c = _position()
        me, sibling = (x, y, c), (x, y, 1 - c)
        chips = [(1 - x, y), (x, 1 - y), (1 - x, 1 - y)]

        def rows(a, px, py, pc):
            return outs[a].at[4 * px + 2 * py + pc]

        def copy(a, k, blk, to, src=None):
            return pltpu.make_async_remote_copy(
                src_ref=rows(a, *blk) if src is None else src, dst_ref=rows(a, *blk),
                send_sem=send_sems.at[7 * a + k], recv_sem=recv_sems.at[7 * a + k], device_id=to, device_id_type=MESH,
            )

        mine = [pltpu.make_async_copy(xs[a], rows(a, *me), local_sems.at[a]) for a in range(n)]
        for cp in mine:
            cp.start()
        first = []
        for a in range(n):
            first.append(copy(a, 0, me, sibling, src=xs[a]))
            first += [copy(a, 1 + j, me, (*chip, c), src=xs[a]) for j, chip in enumerate(chips)]
        for cp in first:
            cp.start()
        passed = []
        for j, chip in enumerate(chips):
            for a in range(n):
                copy(a, 1 + j, (*chip, c), me).wait_recv()
                passed.append(copy(a, 4 + j, (*chip, c), sibling))
                passed[-1].start()
        for a in range(n):
            copy(a, 0, sibling, me).wait_recv()
        for j, chip in enumerate(chips):
            for a in range(n):
                copy(a, 4 + j, (*chip, 1 - c), me).wait_recv()
        for cp in first + passed:
            cp.wait_send()
        for cp in mine:
            cp.wait()

    return pl.pallas_call(
        body,
        name=name,
        out_shape=[jax.ShapeDtypeStruct((N_DEV, *b.shape), b.dtype) for b in blocks],
        in_specs=[_HBM] * n,
        out_specs=[_HBM] * n,
        scratch_shapes=[pltpu.SemaphoreType.DMA((7 * n,)), pltpu.SemaphoreType.DMA((7 * n,)), pltpu.SemaphoreType.DMA((n,))],
    )(*blocks)


_SEM = pl.BlockSpec(memory_space=pltpu.SEMAPHORE)
_HBM_ONLY = pl.BlockSpec(memory_space=pltpu.HBM)
_SIDE_EFFECT = pltpu.SideEffectType.DATAFLOW_SIDE_EFFECTING


def _peer(x, y, c, k):
    return (1 - x if k & 4 else x, 1 - y if k & 2 else y, 1 - c if k & 1 else c)


_PEER_BITS = {"gather": range(1, N_DEV), "scatter": range(1, N_DEV), "chips": (4, 2, 6)}
_LAND_SLOTS = {"gather": N_DEV, "scatter": N_DEV, "chips": 3}


def _exchange_copies(src_refs, land_refs, send_sems, recv_sems, pattern, receive_side):
    x, y, c = _position()
    me = 4 * x + 2 * y + c
    bits = _PEER_BITS[pattern]
    cps = []
    for j, k in enumerate(bits):
        px, py, pc = _peer(x, y, c, k)
        peer = 4 * px + 2 * py + pc
        for a, (src, land) in enumerate(zip(src_refs, land_refs)):
            if pattern == "chips":
                s, slot = src.at[2 * px + py], j
            else:
                s, slot = (src if pattern == "gather" else src.at[peer]), (peer if receive_side else me)
            cps.append(pltpu.make_async_remote_copy(
                src_ref=s, dst_ref=land.at[slot],
                send_sem=send_sems.at[len(bits) * a + j], recv_sem=recv_sems.at[len(bits) * a + j],
                device_id=(px, py, pc), device_id_type=MESH,
            ))
    return cps


def _own_copies(src_refs, land_refs, own_sems):
    x, y, c = _position()
    return [
        pltpu.make_async_copy(src, land.at[4 * x + 2 * y + c], own_sems.at[a])
        for a, (src, land) in enumerate(zip(src_refs, land_refs))
    ]


def _exchange_start(srcs, after, name, pattern):
    n = len(srcs)
    m = len(_PEER_BITS[pattern])
    lands = [jax.ShapeDtypeStruct((_LAND_SLOTS[pattern], *s.shape[-2:]), s.dtype) for s in srcs]

    def body(*refs):
        src_refs, land_refs = refs[1 : 1 + n], refs[1 + n : 1 + 2 * n]
        send_sems, recv_sems, own_sems = refs[1 + 2 * n : 4 + 2 * n]
        token = refs[-1]
        if pattern == "gather":
            for cp in _own_copies(src_refs, land_refs, own_sems):
                cp.start()
        for cp in _exchange_copies(src_refs, land_refs, send_sems, recv_sems, pattern, receive_side=False):
            cp.start()
        token[...] = jnp.zeros_like(token)

    hbm = lambda t: pltpu.with_memory_space_constraint(t, pltpu.HBM)
    out = pl.pallas_call(
        body,
        name=name,
        out_shape=(
            pltpu.SemaphoreType.DMA((m * n,)), pltpu.SemaphoreType.DMA((m * n,)), pltpu.SemaphoreType.DMA((n,)),
            *[pltpu.HBM(s.shape, s.dtype) for s in srcs], *[pltpu.HBM(l.shape, l.dtype) for l in lands],
            jax.ShapeDtypeStruct((8, LANES), F32),
        ),
        in_specs=(_HBM, *[_HBM_ONLY] * (2 * n)),
        out_specs=(_SEM, _SEM, _SEM, *[_HBM_ONLY] * (2 * n), pl.BlockSpec(memory_space=pltpu.VMEM)),
        input_output_aliases={1 + i: 3 + i for i in range(2 * n)},
        compiler_params=pltpu.CompilerParams(has_side_effects=_SIDE_EFFECT),
    )(after, *[hbm(s) for s in srcs], *[hbm(lax.empty(l.shape, l.dtype)) for l in lands])
    return out[:3], out[3 : 3 + n], out[3 + n : 3 + 2 * n], out[-1]


def _exchange_wait(sems, srcs, lands, after, name, pattern):
    n = len(srcs)

    def body(*refs):
        src_refs, land_refs = refs[:n], refs[n : 2 * n]
        send_sems, recv_sems, own_sems = refs[2 * n : 2 * n + 3]
        if pattern == "gather":
            for cp in _own_copies(src_refs, land_refs, own_sems):
                cp.wait()
        for cp in _exchange_copies(src_refs, land_refs, send_sems, recv_sems, pattern, receive_side=True):
            cp.wait_send()
            cp.wait_recv()

    out = pl.pallas_call(
        body,
        name=name,
        out_shape=(*[pltpu.HBM(s.shape, s.dtype) for s in srcs], *[pltpu.HBM(l.shape, l.dtype) for l in lands]),
        in_specs=(*[_HBM_ONLY] * (2 * n), _SEM, _SEM, _SEM, _HBM),
        out_specs=tuple([_HBM_ONLY] * (2 * n)),
        input_output_aliases={i: i for i in range(2 * n)},
        compiler_params=pltpu.CompilerParams(has_side_effects=_SIDE_EFFECT),
    )(*srcs, *lands, *sems, after)
    return out[:n], out[n:]


def _sibling_exchange(sends):
    n = len(sends)

    def body(*refs):
        srcs, dsts = refs[:n], refs[n : 2 * n]
        send_sems, recv_sems = refs[2 * n :]
        x, y, c = _position()
        cps = [
            pltpu.make_async_remote_copy(
                src_ref=srcs[a].at[1 - c], dst_ref=dsts[a], send_sem=send_sems.at[a], recv_sem=recv_sems.at[a],
                device_id=(x, y, 1 - c), device_id_type=MESH,
            )
            for a in range(n)
        ]
        for cp in cps:
            cp.start()
        for cp in cps:
            cp.wait()

    return pl.pallas_call(
        body,
        name="rs_sibling",
        out_shape=[jax.ShapeDtypeStruct(s.shape[1:], s.dtype) for s in sends],
        in_specs=[_HBM] * n,
        out_specs=[_HBM] * n,
        scratch_shapes=[pltpu.SemaphoreType.DMA((n,)), pltpu.SemaphoreType.DMA((n,))],
    )(*sends)


def _rows_tile(r):
    return ROW_TILE if r % ROW_TILE == 0 else r


def _pair_sum(send, got, core, name):
    _, _, r, c = send.shape
    br = _rows_tile(r)

    def body(core_ref, a_ref, b_ref, o_ref):
        o_ref[...] = (a_ref[...].astype(F32) + b_ref[...].astype(F32)).astype(o_ref.dtype)

    return pl.pallas_call(
        body,
        name=name,
        grid_spec=pltpu.PrefetchScalarGridSpec(
            num_scalar_prefetch=1,
            grid=(4, r // br),
            in_specs=[
                pl.BlockSpec((None, None, br, c), lambda n, i, core: (core[0], n, i, 0)),
                pl.BlockSpec((None, br, c), lambda n, i, core: (n, i, 0)),
            ],
            out_specs=pl.BlockSpec((None, br, c), lambda n, i, core: (n, i, 0)),
        ),
        out_shape=jax.ShapeDtypeStruct((4, r, c), send.dtype),
        compiler_params=_params(("parallel", "parallel")),
    )(core, send, got)


def _adamw(w, g, m, v):
    m = ADAM_B1 * m + (1.0 - ADAM_B1) * g
    v = ADAM_B2 * v + (1.0 - ADAM_B2) * (g * g)
    m_hat = m / (1.0 - ADAM_B1 ** ADAM_STEP)
    v_hat = v / (1.0 - ADAM_B2 ** ADAM_STEP)
    delta = -ADAM_LR * (m_hat / (jnp.sqrt(v_hat) + ADAM_EPS) + ADAM_WD * w)
    return delta, m, v


def _shard_update(send, got, recv, w, m, v, pos, name):
    _, r, c = w.shape
    br = _rows_tile(r)

    def body(pos_ref, a_ref, b_ref, r_ref, w_ref, m_ref, v_ref, g_ref, d_ref, nm_ref, nv_ref):
        g = a_ref[...].astype(F32) + b_ref[...].astype(F32)
        for n in range(3):
            g = g + r_ref[n].astype(F32)
        g_ref[...] = g
        d_ref[...], nm_ref[...], nv_ref[...] = _adamw(w_ref[...], g, m_ref[...], v_ref[...])

    own = pl.BlockSpec((None, br, c), lambda i, pos: (0, i, 0))
    return pl.pallas_call(
        body,
        name=name,
        grid_spec=pltpu.PrefetchScalarGridSpec(
            num_scalar_prefetch=1,
            grid=(r // br,),
            in_specs=[
                pl.BlockSpec((None, None, br, c), lambda i, pos: (pos[0], pos[1], i, 0)),
                pl.BlockSpec((None, br, c), lambda i, pos: (pos[1], i, 0)),
                pl.BlockSpec((3, br, c), lambda i, pos: (0, i, 0)),
                own, own, own,
            ],
            out_specs=[own, own, own, own],
        ),
        out_shape=[jax.ShapeDtypeStruct((1, r, c), F32)] * 4,
        compiler_params=_params(("parallel",)),
    )(pos, send, got, recv, w, m, v)


def _shard_update_direct(parts, chunks, w, m, v, me, name):
    _, r, c = w.shape
    br = _rows_tile(r)

    def body(me_ref, p_ref, own_ref, w_ref, m_ref, v_ref, g_ref, d_ref, nm_ref, nv_ref):
        g = None
        for n in range(N_DEV):
            part = jnp.where(me_ref[0] == n, own_ref[...], p_ref[n]).astype(F32)
            g = part if g is None else g + part
        g_ref[...] = g
        d_ref[...], nm_ref[...], nv_ref[...] = _adamw(w_ref[...], g, m_ref[...], v_ref[...])

    shard = pl.BlockSpec((None, br, c), lambda i, me: (0, i, 0))
    return pl.pallas_call(
        body,
        name=name,
        grid_spec=pltpu.PrefetchScalarGridSpec(
            num_scalar_prefetch=1,
            grid=(r // br,),
            in_specs=[
                pl.BlockSpec((N_DEV, br, c), lambda i, me: (0, i, 0)),
                pl.BlockSpec((None, br, c), lambda i, me: (me[0], i, 0)),
                shard, shard, shard,
            ],
            out_specs=[shard, shard, shard, shard],
        ),
        out_shape=[jax.ShapeDtypeStruct((1, r, c), F32)] * 4,
        compiler_params=_params(("parallel",)),
    )(me, parts, chunks, w, m, v)


def _small_update(parts, first_rows, ws, ms, vs):
    k = len(ws)

    def unpacked(rows, shape):
        if len(shape) == 2 and shape[1] <= LANES:
            return rows[0:1, : shape[1]]
        if len(shape) == 2:
            return jnp.concatenate([rows[r : r + 1] for r in range(shape[1] // LANES)], axis=1)
        return rows.reshape(shape)

    def body(p_ref, f_ref, *refs):
        w_refs, m_refs, v_refs = refs[:k], refs[k : 2 * k], refs[2 * k : 3 * k]
        outs, loss_ref = refs[3 * k : 7 * k], refs[7 * k]
        g, first = p_ref[0], f_ref[0]
        for n in range(1, N_DEV):
            g = g + p_ref[n]
            first = first + f_ref[n]
        g = jnp.concatenate([g[:8] + first, g[8:]], axis=0)
        off = 0
        for i, (_, rows) in enumerate(_SMALL):
            gi = unpacked(g[off : off + rows], w_refs[i].shape)
            off += rows
            outs[i][...] = gi
            outs[k + i][...], outs[2 * k + i][...], outs[3 * k + i][...] = _adamw(w_refs[i][...], gi, m_refs[i][...], v_refs[i][...])
        loss_ref[...] = g[off : off + 1, 0:1]

    out = pl.pallas_call(
        body,
        name="small_update",
        out_shape=[jax.ShapeDtypeStruct(w.shape, F32) for _ in range(4) for w in ws] + [jax.ShapeDtypeStruct((1, 1), F32)],
        compiler_params=pltpu.CompilerParams(vmem_limit_bytes=VMEM_LIMIT),
    )(parts, first_rows, *ws, *ms, *vs)
    return [out[a * k : (a + 1) * k] for a in range(4)], out[4 * k]


_SHARD_AXIS = (1, 1, 1, 0, 0, 0, 0)
_TRANSPOSED = (False, False, False, False, True, True, False)


def _full_from_gathered(t, axis):
    if axis == 0:
        return t.reshape(N_DEV * t.shape[1], t.shape[2])
    return jnp.concatenate([t[d] for d in range(N_DEV)], axis=1)


def _chunks_from_cols(t):
    c = t.shape[1] // N_DEV
    return jnp.stack([t[:, d * c : (d + 1) * c] for d in range(N_DEV)])


_SMALL = (("norm1_g", 8), ("norm2_g", 8), ("norm_f_g", 8), ("b_forget", 8), ("pool_scale", 8), ("pool_mix", 512))


def _pack_small(vals, loss_row):
    parts = []
    for (name, rows), t in zip(_SMALL, vals):
        f = t.astype(F32).reshape(-1)
        f = jnp.concatenate([f, jnp.zeros((rows * LANES - f.shape[0],), F32)]).reshape(rows, LANES)
        parts.append(f)
    parts.append(loss_row)
    return jnp.concatenate(parts, axis=0)


def _local_grads(x, tgt, g1, g2, gf, b_forget, pool_mix, pool_scale, w_in, fwd_token, out_weights, ffn_weights, ffn_grads_out, out_grads_out, small_grads_out, in_grads_out, norm1_grad_out):
    n_seq, S, _ = x.shape
    T = n_seq * S
    x2 = x.reshape(T, D_MODEL)
    tg2 = tgt.reshape(T, D_MODEL)
    w_uqkv, w_fl, w_g = w_in
    b_pad = jnp.concatenate([b_forget.reshape(1, N_HEADS), jnp.zeros((1, FL_PAD - N_HEADS), F32)], axis=1)
    mix_b = pool_mix.reshape(len(POOL_WINDOWS), GROUP_DIM, GROUP_DIM).astype(BF16)
    scale = pool_scale.reshape(1, POOL_WIDTH)
    g1 = g1.reshape(1, D_MODEL)
    g2 = g2.reshape(1, D_MODEL)
    gf = gf.reshape(1, D_MODEL)

    h, u, qkv, fl, gates = _in_proj(x2, g1, w_uqkv, w_fl, w_g, fwd_token)
    fcol = _forget_fwd(fl, b_pad, n_seq, S)
    pm, p2, p3 = _pool_fwd(u, mix_b, scale, n_seq, S)
    a, lse = _attn_fwd(qkv, fcol, n_seq, S)
    w_po, w_ao, w_out = out_weights(a)
    merged, x1, attn_y, pool_y = _mix_out(a, p3, gates, x2, w_ao, w_po, w_out)
    w_gate_t, w_up_t, w_down = ffn_weights(x1)
    h2, gate, up, act, dx2, loss_rows, dgf = _ffn_fwd(x1, g2, gf, tg2, w_gate_t, w_up_t, w_down)

    dgate, dup, dx1, dg2 = _ffn_bwd(dx2, gate, up, x1, g2, w_gate_t, w_up_t, w_down)
    bwd_token = ffn_grads_out(_matmul_tn(dgate, h2, "dw_ffn_gate"), _matmul_tn(dup, h2, "dw_ffn_up"), _matmul_tn(act, dx2, "dw_ffn_down"))
    dgates, dpy, day, da, dp2, dscale = _mix_bwd(dx1, gates, pool_y, attn_y, p2, scale, w_out, w_ao, w_po, bwd_token)
    out_token = out_grads_out(_matmul_tn(p3, dpy, "dw_pool_out"), _matmul_tn(a, day, "dw_attn_out"), _matmul_tn(merged, dx1, "dw_out"))
    du, dmix = _pool_bwd(dp2, pm, mix_b, out_token, n_seq, S)
    dq, dk, dv, dfk, dfq = _attn_bwd(qkv, da, a, fcol, lse, n_seq, S)
    dfl, db = _forget_bwd(dfk, dfq, fl, b_pad, n_seq, S)
    small_token = small_grads_out((jnp.zeros_like(g1), dg2, dgf, db[:, :N_HEADS], dscale, dmix), loss_rows)
    in_token = in_grads_out(_dw_in(h, du, dq, dk, dv, dfl, dgates, small_token))
    dx, dg1 = _in_proj_bwd(du, dq, dk, dv, dfl, dgates, x2, dx1, g1, w_uqkv, w_fl, w_g, in_token)
    norm1_grad_out(dg1)
    return dx.reshape(n_seq, S, D_MODEL)


def kernel(x, norm1_g, w_in, b_forget, pool_mix, pool_scale, w_pool_out, w_attn_out, w_out, norm2_g, w_ffn_gate, w_ffn_up, w_ffn_down, norm_f_g, loss_target, m_norm1_g, m_w_in, m_b_forget, m_pool_mix, m_pool_scale, m_w_pool_out, m_w_attn_out, m_w_out, m_norm2_g, m_w_ffn_gate, m_w_ffn_up, m_w_ffn_down, m_norm_f_g, v_norm1_g, v_w_in, v_b_forget, v_pool_mix, v_pool_scale, v_w_pool_out, v_w_attn_out, v_w_out, v_norm2_g, v_w_ffn_gate, v_w_ffn_up, v_w_ffn_down, v_norm_f_g):
    names = ("w_in", "w_pool_out", "w_attn_out", "w_out", "w_ffn_gate", "w_ffn_up", "w_ffn_down")
    w_sh = (w_in, w_pool_out, w_attn_out, w_out, w_ffn_gate, w_ffn_up, w_ffn_down)
    m_sh = (m_w_in, m_w_pool_out, m_w_attn_out, m_w_out, m_w_ffn_gate, m_w_ffn_up, m_w_ffn_down)
    v_sh = (v_w_in, v_w_pool_out, v_w_attn_out, v_w_out, v_w_ffn_gate, v_w_ffn_up, v_w_ffn_down)

    cx, cy, cc = _position()
    me = 4 * cx + 2 * cy + cc
    def stored(t, transposed):
        return jnp.transpose(t, (0, 2, 1)) if transposed else t

    w_sh, m_sh, v_sh = ([stored(t, tr) for t, tr in zip(ts, _TRANSPOSED)] for ts in (w_sh, m_sh, v_sh))
    shards = [w[0].astype(BF16) for w in w_sh]
    (gathered_in,) = _all_gather(shards[:1], "w_in_all_gather")
    out_sems = _exchange_start(shards[1:4], gathered_in, "out_weights_gather_start", "gather")
    ffn_sems = _exchange_start(shards[4:], out_sems[3], "ffn_weights_gather_start", "gather")
    no_order = jnp.zeros((8, LANES), F32)

    def gathered_weights(sems, axes, name):
        def wait(after):
            _, lands = _exchange_wait(*sems[:3], after, name, "gather")
            return [_full_from_gathered(t, axis) for t, axis in zip(lands, axes)]

        return wait

    started = {}

    def scatter_grads(key, name):
        def start(*whole_grads):
            chunks = [
                _chunks_from_cols(t) if axis == 1 else t.reshape(N_DEV, -1, t.shape[1])
                for t, axis in zip(whole_grads, _SHARD_AXIS[key])
            ]
            started[key] = _exchange_start(chunks, no_order, name, "scatter")
            return started[key][3]

        return start

    def gather_small(small, loss_rows):
        started["small"] = _exchange_start([_pack_small(small, loss_rows)], no_order, "small_grads_gather_start", "gather")
        return started["small"][3]

    core = jnp.reshape(cc, (1,)).astype(jnp.int32)
    pos = jnp.stack([cc, 2 * cx + cy]).astype(jnp.int32)

    def reduce_w_in(send_in):
        (got_in,) = _sibling_exchange([send_in])
        pair_in = _pair_sum(send_in, got_in, core, "pair_sum_w_in")
        started["in"] = (send_in, got_in, _exchange_start([pair_in], no_order, "w_in_grads_chips_start", "chips"))
        return started["in"][2][3]

    def gather_norm1(dg1):
        rows = jnp.reshape(dg1, (8, LANES))
        started["norm1"] = _exchange_start([rows], no_order, "norm1_grad_gather_start", "gather")

    ffn, out = slice(4, 7), slice(1, 4)
    grad_x = _local_grads(
        x, loss_target, norm1_g, norm2_g, norm_f_g, b_forget, pool_mix, pool_scale, _w_in_pieces(gathered_in), ffn_sems[3],
        gathered_weights(out_sems, _SHARD_AXIS[out], "out_weights_gather_wait"),
        gathered_weights(ffn_sems, _SHARD_AXIS[ffn], "ffn_weights_gather_wait"),
        scatter_grads(ffn, "ffn_grads_scatter_start"), scatter_grads(out, "out_grads_scatter_start"), gather_small, reduce_w_in, gather_norm1,
    )
    send_in, got_in, chip_sems = started["in"]

    def scattered_updates(key, after, name):
        srcs, lands = _exchange_wait(*started[key][:3], after, name, "scatter")
        return [
            _shard_update_direct(p, s, w, m, v, jnp.reshape(me, (1,)).astype(jnp.int32), "update_" + n)
            for p, s, w, m, v, n in zip(lands, srcs, w_sh[key], m_sh[key], v_sh[key], names[key])
        ]

    updates_out = scattered_updates(out, grad_x, "out_grads_scatter_wait")
    updates_ffn = scattered_updates(ffn, grad_x, "ffn_grads_scatter_wait")

    small_w = (norm1_g, norm2_g, norm_f_g, b_forget, pool_scale, pool_mix)
    small_m = (m_norm1_g, m_norm2_g, m_norm_f_g, m_b_forget, m_pool_scale, m_pool_mix)
    small_v = (v_norm1_g, v_norm2_g, v_norm_f_g, v_b_forget, v_pool_scale, v_pool_mix)
    _, (recv_in,) = _exchange_wait(*chip_sems[:3], updates_ffn[-1][0], "w_in_grads_chips_wait", "chips")
    update_in = _shard_update(send_in, got_in, recv_in, w_in, m_w_in, v_w_in, pos, "update_w_in")

    def gathered_small(key, after, name):
        _, lands = _exchange_wait(*started[key][:3], after, name, "gather")
        return lands[0]

    parts = gathered_small("small", update_in[0], "small_grads_gather_wait")
    first_rows = gathered_small("norm1", parts, "norm1_grad_gather_wait")
    (g_s, d_s, nm_s, nv_s), loss = _small_update(parts, first_rows, small_w, small_m, small_v)
    g_w, d_w, nm_w, nv_w = zip(*(
        [stored(t, tr) for t in u] for u, tr in zip([update_in] + updates_out + updates_ffn, _TRANSPOSED)
    ))
    loss = loss.reshape(())
    (g1, g2, gf, gb, gsc, gmix), (d1, d2, df, db_, dsc, dmx) = g_s, d_s
    (m1, m2, mf, mb, msc, mmx), (v1, v2, vf, vb, vsc, vmx) = nm_s, nv_s

    def ordered(n1, win, b, mix, sc, wpo, wao, wout, n2, wg, wu, wd, nf):
        return (n1, win, b, mix, sc, wpo, wao, wout, n2, wg, wu, wd, nf)

    grads = ordered(g1, g_w[0], gb, gmix, gsc, g_w[1], g_w[2], g_w[3], g2, g_w[4], g_w[5], g_w[6], gf)
    deltas = ordered(d1, d_w[0], db_, dmx, dsc, d_w[1], d_w[2], d_w[3], d2, d_w[4], d_w[5], d_w[6], df)
    new_m = ordered(m1, nm_w[0], mb, mmx, msc, nm_w[1], nm_w[2], nm_w[3], m2, nm_w[4], nm_w[5], nm_w[6], mf)
    new_v = ordered(v1, nv_w[0], vb, vmx, vsc, nv_w[1], nv_w[2], nv_w[3], v2, nv_w[4], nv_w[5], nv_w[6], vf)
    return (loss, grad_x, *grads, *deltas, *new_m, *new_v)
```

```python
import jax
import jax.numpy as jnp
from jax import lax
from jax.experimental import pallas as pl
from jax.experimental.pallas import tpu as pltpu

F32 = jnp.float32
BF16 = jnp.bfloat16
MESH = pl.DeviceIdType.MESH

D_MODEL = 1024
POOL_WINDOWS = (2, 4, 8, 16)
POOL_WIDTH = 512
GROUP_DIM = 128
ATTN_WIDTH = 512
HEAD_DIM = 64
N_HEADS = 8
N_PAIRS = 4
D_FF = 2816
RMS_EPS = 1e-6
N_DEV = 8
LANES = 128
FL_PAD = 128

ADAM_LR = 0.001
ADAM_B1 = 0.9
ADAM_B2 = 0.999
ADAM_EPS = 1e-08
ADAM_WD = 0.01
ADAM_STEP = 10

VMEM_LIMIT = 56 * 1024 * 1024
VMEM_LIMIT_MAX = 60 * 1024 * 1024
ROW_TILE = 512
ATTN_BLOCK = 512
FF_CHUNK = 256
FF_ROW_TILE = 512
DW_TOKENS = 2048


def _mm(a, b):
    return jnp.dot(a, b, preferred_element_type=F32)


def _mm_nt(a, b):
    return lax.dot_general(a, b, (((1,), (1,)), ((), ())), preferred_element_type=F32)


def _mm_tn(a, b):
    return lax.dot_general(a, b, (((0,), (0,)), ((), ())), preferred_element_type=F32)


def _whole_cols(w_ref):
    if len(w_ref.shape) == 2:
        return w_ref[...]
    return jnp.concatenate([w_ref[d] for d in range(w_ref.shape[0])], axis=1)


def _sigmoid(x):
    return 1.0 / (1.0 + jnp.exp(-x))


def _params(sem, vmem=VMEM_LIMIT):
    return pltpu.CompilerParams(dimension_semantics=sem, vmem_limit_bytes=vmem)


def _const_spec(shape):
    nd = len(shape)
    return pl.BlockSpec(shape, lambda *_: (0,) * nd, pipeline_mode=pl.Buffered(1))


def _rms_fwd(x, g):
    r = lax.rsqrt(jnp.mean(x * x, axis=-1, keepdims=True) + RMS_EPS)
    xh = x * r
    return xh * g, xh, r


def _rms_bwd(dy, xh, r, g):
    dxh = dy * g
    dx = r * (dxh - xh * jnp.mean(dxh * xh, axis=-1, keepdims=True))
    return dx, dy * xh


def _in_proj(x, g1, w_uqkv, w_fl, w_g, token):
    T = x.shape[0]
    tm = ROW_TILE

    def body(x_ref, g_ref, wa_ref, wf_ref, wg_ref, token_ref, h_ref, u_ref, qkv_ref, fl_ref, gt_ref):
        h, _, _ = _rms_fwd(x_ref[...], g_ref[...])
        hb = h.astype(BF16)
        h_ref[...] = hb
        z = _mm(hb, wa_ref[...])
        u_ref[...] = z[:, :POOL_WIDTH]
        qkv_ref[...] = z[:, POOL_WIDTH:].astype(BF16)
        fl_ref[...] = _mm(hb, wf_ref[...])
        gt_ref[...] = _mm(hb, wg_ref[...]).astype(BF16)

    row = lambda n: pl.BlockSpec((tm, n), lambda i: (i, 0))
    return pl.pallas_call(
        body,
        name="in_proj",
        grid=(T // tm,),
        in_specs=[row(D_MODEL), _const_spec((1, D_MODEL)), _const_spec(w_uqkv.shape), _const_spec(w_fl.shape), _const_spec(w_g.shape), _HBM],
        out_specs=[row(D_MODEL), row(POOL_WIDTH), row(3 * ATTN_WIDTH), row(FL_PAD), row(2 * D_MODEL)],
        out_shape=[
            jax.ShapeDtypeStruct((T, D_MODEL), BF16),
            jax.ShapeDtypeStruct((T, POOL_WIDTH), F32),
            jax.ShapeDtypeStruct((T, 3 * ATTN_WIDTH), BF16),
            jax.ShapeDtypeStruct((T, FL_PAD), F32),
            jax.ShapeDtypeStruct((T, 2 * D_MODEL), BF16),
        ],
        compiler_params=_params(("parallel",)),
    )(x, g1, w_uqkv, w_fl, w_g, token)


def _log_sigmoid(x):
    return jnp.minimum(x, 0.0) - jnp.log(1.0 + jnp.exp(-jnp.abs(x)))


def _forget_fwd(fl, b_pad, n_seq, S):
    def body(fl_ref, b_ref, fcol_ref):
        lf = _log_sigmoid(fl_ref[...] + b_ref[...])
        t = lf.T
        lane = lax.broadcasted_iota(jnp.int32, t.shape, 1)
        k = 1
        while k < S:
            t = t + jnp.where(lane >= k, pltpu.roll(t, k, 1), 0.0)
            k *= 2
        fcol_ref[...] = t.T

    return pl.pallas_call(
        body,
        name="forget_fwd",
        grid=(n_seq,),
        in_specs=[pl.BlockSpec((S, FL_PAD), lambda s: (s, 0)), _const_spec((1, FL_PAD))],
        out_specs=pl.BlockSpec((S, FL_PAD), lambda s: (s, 0)),
        out_shape=jax.ShapeDtypeStruct((n_seq * S, FL_PAD), F32),
        compiler_params=_params(("parallel",)),
    )(fl, b_pad)


def _window_pick(g, v2, v4, v8, v16):
    return jnp.where(g == 0, v2, jnp.where(g == 1, v4, jnp.where(g == 2, v8, v16)))


def _pool_fwd(u, mix_b, scale, n_seq, S):
    T = n_seq * S

    def body(u_ref, mix_ref, sc_ref, pm_ref, p2_ref, p3_ref):
        g = pl.program_id(1)
        uu = u_ref[...]
        row = lax.broadcasted_iota(jnp.int32, uu.shape, 0)

        def back(a, k):
            return jnp.where(row >= k, pltpu.roll(a, k, 0), 0.0)

        s2 = uu + back(uu, 1)
        s4 = s2 + back(s2, 2)
        s8 = s4 + back(s4, 4)
        s16 = s8 + back(s8, 8)
        w = _window_pick(g, 2.0, 4.0, 8.0, 16.0)
        cnt = jnp.minimum((row + 1).astype(F32), w)
        pm = _window_pick(g, s2, s4, s8, s16) / cnt - uu
        pmb = pm.astype(BF16)
        pm_ref[...] = pmb
        p2 = _mm(pmb, mix_ref[...])
        p2_ref[...] = p2
        p3_ref[...] = (p2 * sc_ref[...]).astype(BF16)

    grp = pl.BlockSpec((S, GROUP_DIM), lambda s, g: (s, g))
    return pl.pallas_call(
        body,
        name="pool_fwd",
        grid=(n_seq, len(POOL_WINDOWS)),
        in_specs=[
            grp,
            pl.BlockSpec((None, GROUP_DIM, GROUP_DIM), lambda s, g: (g, 0, 0)),
            pl.BlockSpec((1, GROUP_DIM), lambda s, g: (0, g)),
        ],
        out_specs=[grp, grp, grp],
        out_shape=[
            jax.ShapeDtypeStruct((T, POOL_WIDTH), BF16),
            jax.ShapeDtypeStruct((T, POOL_WIDTH), F32),
            jax.ShapeDtypeStruct((T, POOL_WIDTH), BF16),
        ],
        compiler_params=_params(("parallel", "parallel")),
    )(u, mix_b, scale)


def _split3(v):
    hi = v.astype(BF16).astype(F32)
    r = v - hi
    mid = r.astype(BF16).astype(F32)
    lo = (r - mid).astype(BF16).astype(F32)
    return hi, mid, lo


def _bias_lanes(v):
    hi, mid, lo = _split3(v)
    lane = lax.broadcasted_iota(jnp.int32, (1, LANES), 1)
    packed = jnp.where(lane < N_HEADS, hi, jnp.where(lane < 2 * N_HEADS, pltpu.roll(mid, N_HEADS, 1), pltpu.roll(lo, 2 * N_HEADS, 1)))
    return jnp.where(lane < 3 * N_HEADS, packed, 0.0).astype(BF16)


def _bias_placement(slot):
    row = lax.broadcasted_iota(jnp.int32, (LANES, N_HEADS * LANES), 0)
    col = lax.broadcasted_iota(jnp.int32, (LANES, N_HEADS * LANES), 1)
    h = col // LANES
    n = col % LANES - jnp.where(h % 2 == 0, HEAD_DIM, 0) - 3 * slot
    return ((n >= 0) & (n < 3) & (row == N_HEADS * n + h)).astype(BF16)


def _augment(xp, h, bias, ones_slot):
    lane = lax.broadcasted_iota(jnp.int32, (1, LANES), 1)
    hh = h % 2
    head = (lane >= HEAD_DIM * hh) & (lane < HEAD_DIM * (hh + 1))
    b = HEAD_DIM * (1 - hh)
    rest = jnp.zeros_like(xp) if bias is None else bias[:, h * LANES : (h + 1) * LANES]
    out = jnp.where(head, xp, rest)
    if ones_slot is not None:
        out = jnp.where((lane >= b + 3 * ones_slot) & (lane < b + 3 * ones_slot + 3), jnp.ones_like(xp), out)
    return out


def _attn_fwd(qkv, fcol, n_seq, S):
    T = n_seq * S
    tb = ATTN_BLOCK
    nq = S // tb
    scale = HEAD_DIM ** -0.5

    def body(q_ref, k_ref, v_ref, fc_ref, o_ref, st_ref, qa_sc, ka_sc, m_sc, l_sc, acc_sc):
        i = pl.program_id(1)
        lane = lax.broadcasted_iota(jnp.int32, (1, LANES), 1)
        low = lane < HEAD_DIM

        @pl.when(i == 0)
        def _():
            place = _bias_placement(1)

            def rows_ka(r, carry):
                r0 = pl.multiple_of(r * tb, tb)
                bias = _mm(_bias_lanes(-fc_ref[pl.ds(r0, tb), :]), place).astype(BF16)
                for h in range(N_HEADS):
                    kp = k_ref[pl.ds(r0, tb), (h // 2) * LANES : (h // 2 + 1) * LANES] * scale
                    ka_sc[h, pl.ds(r0, tb), :] = _augment(kp, h, bias, 0)
                return carry

            lax.fori_loop(0, nq, rows_ka, 0)

        q0 = pl.multiple_of(i * tb, tb)
        bias = _mm(_bias_lanes(fc_ref[pl.ds(q0, tb), :]), _bias_placement(0)).astype(BF16)
        for h in range(N_HEADS):
            qa_sc[h] = _augment(q_ref[:, (h // 2) * LANES : (h // 2 + 1) * LANES], h, bias, 1)
        m_sc[...] = jnp.full(m_sc.shape, -jnp.inf, F32)
        l_sc[...] = jnp.zeros_like(l_sc)
        acc_sc[...] = jnp.zeros_like(acc_sc)
        causal = lax.broadcasted_iota(jnp.int32, (tb, tb), 1) <= lax.broadcasted_iota(jnp.int32, (tb, tb), 0)

        def step(j, masked):
            c0 = pl.multiple_of(j * tb, tb)
            for p in range(N_PAIRS):
                vb = v_ref[pl.ds(c0, tb), p * LANES : (p + 1) * LANES]
                pv, al = [], []
                for hh in range(2):
                    h = 2 * p + hh
                    s = _mm_nt(qa_sc[h], ka_sc[h, pl.ds(c0, tb), :])
                    if masked:
                        s = jnp.where(causal, s, -jnp.inf)
                    m_old = m_sc[h]
                    m_new = jnp.maximum(m_old, jnp.max(s, axis=1, keepdims=True))
                    alpha = jnp.exp(m_old - m_new)
                    pe = jnp.exp(s - jnp.concatenate([m_new] * (tb // LANES), axis=1))
                    l_sc[h] = alpha * l_sc[h] + jnp.sum(pe, axis=1, keepdims=True)
                    m_sc[h] = m_new
                    pv.append(_mm(pe.astype(BF16), vb))
                    al.append(alpha)
                acc_sc[p] = jnp.where(low, al[0], al[1]) * acc_sc[p] + jnp.where(low, pv[0], pv[1])

        def loop_body(j, carry):
            step(j, False)
            return carry

        lax.fori_loop(0, i, loop_body, 0)
        step(i, True)
        st = jnp.zeros((tb, LANES), F32)
        for p in range(N_PAIRS):
            lp = jnp.where(low, l_sc[2 * p], l_sc[2 * p + 1])
            o_ref[:, p * LANES : (p + 1) * LANES] = (acc_sc[p] / lp).astype(BF16)
            for h in (2 * p, 2 * p + 1):
                st = jnp.where(lane == h, m_sc[h] + jnp.log(l_sc[h]), st)
        st_ref[...] = st

    return pl.pallas_call(
        body,
        name="attn_fwd",
        grid=(n_seq, nq),
        in_specs=[
            pl.BlockSpec((tb, ATTN_WIDTH), lambda s, i: (s * nq + i, 0)),
            pl.BlockSpec((S, ATTN_WIDTH), lambda s, i: (s, 1)),
            pl.BlockSpec((S, ATTN_WIDTH), lambda s, i: (s, 2)),
            pl.BlockSpec((S, LANES), lambda s, i: (s, 0)),
        ],
        out_specs=[
            pl.BlockSpec((tb, ATTN_WIDTH), lambda s, i: (s * nq + i, 0)),
            pl.BlockSpec((tb, LANES), lambda s, i: (s * nq + i, 0)),
        ],
        out_shape=[jax.ShapeDtypeStruct((T, ATTN_WIDTH), BF16), jax.ShapeDtypeStruct((T, LANES), F32)],
        scratch_shapes=[
            pltpu.VMEM((N_HEADS, tb, LANES), BF16),
            pltpu.VMEM((N_HEADS, S, LANES), BF16),
            pltpu.VMEM((N_HEADS, tb, LANES), F32),
            pltpu.VMEM((N_HEADS, tb, LANES), F32),
            pltpu.VMEM((N_PAIRS, tb, LANES), F32),
        ],
        compiler_params=_params(("parallel", "arbitrary")),
    )(qkv, qkv, qkv, fcol)


def _mix_out(a, p3, gates, x, w_ao, w_po, w_out):
    T = x.shape[0]
    tm = ROW_TILE

    def body(a_ref, p3_ref, gt_ref, x_ref, wao_ref, wpo_ref, wout_ref, mg_ref, x1_ref, ay_ref, py_ref):
        ay = _mm(a_ref[...], _whole_cols(wao_ref))
        py = _mm(p3_ref[...], _whole_cols(wpo_ref))
        ay_ref[...] = ay.astype(BF16)
        py_ref[...] = py.astype(BF16)
        sp = _sigmoid(gt_ref[:, :D_MODEL].astype(F32))
        sa = _sigmoid(gt_ref[:, D_MODEL:].astype(F32))
        mb = (sp * py + sa * ay).astype(BF16)
        mg_ref[...] = mb
        x1_ref[...] = x_ref[...] + _mm(mb, wout_ref[...])

    row = lambda n: pl.BlockSpec((tm, n), lambda i: (i, 0))
    return pl.pallas_call(
        body,
        name="mix_out",
        grid=(T // tm,),
        in_specs=[
            row(ATTN_WIDTH), row(POOL_WIDTH), row(2 * D_MODEL), row(D_MODEL),
            _const_spec(w_ao.shape), _const_spec(w_po.shape), _const_spec(w_out.shape),
        ],
        out_specs=[row(D_MODEL), row(D_MODEL), row(D_MODEL), row(D_MODEL)],
        out_shape=[
            jax.ShapeDtypeStruct((T, D_MODEL), BF16), jax.ShapeDtypeStruct((T, D_MODEL), F32),
            jax.ShapeDtypeStruct((T, D_MODEL), BF16), jax.ShapeDtypeStruct((T, D_MODEL), BF16),
        ],
        compiler_params=_params(("parallel",)),
    )(a, p3, gates, x, w_ao, w_po, w_out)


def _ffn_fwd(x1, g2, gf, tgt, w_gate_t, w_up_t, w_down):
    T = x1.shape[0]
    tm = min(T, FF_ROW_TILE)
    nt = T // tm
    nc = D_FF // FF_CHUNK

    def body(x1_ref, g2_ref, gf_ref, tg_ref, wg_ref, wu_ref, wd_ref, h2_ref, gate_ref, up_ref, act_ref, dx2_ref, loss_ref, dgf_ref):
        x1v = x1_ref[...]
        h2, _, _ = _rms_fwd(x1v, g2_ref[...])
        h2b = h2.astype(BF16)
        h2_ref[...] = h2b
        for c in range(nc):
            sl = slice(c * FF_CHUNK, (c + 1) * FF_CHUNK)
            gate = _mm_nt(h2b, wg_ref[sl, :])
            up = _mm_nt(h2b, wu_ref[sl, :])
            gate_ref[:, sl] = gate.astype(BF16)
            up_ref[:, sl] = up.astype(BF16)
            act_ref[:, sl] = (gate * _sigmoid(gate) * up).astype(BF16)
        acc = x1v + _mm(act_ref[...], wd_ref[...])
        gfv = gf_ref[...]
        y, xh, r = _rms_fwd(acc, gfv)
        err = y - tg_ref[...]
        part = 0.5 * jnp.sum(jnp.mean(err * err, axis=-1, keepdims=True), axis=0, keepdims=True)
        dx2, dgrow = _rms_bwd(err * (1.0 / D_MODEL), xh, r, gfv)
        dx2_ref[...] = dx2

        @pl.when(pl.program_id(0) == 0)
        def _():
            dgf_ref[...] = jnp.zeros_like(dgf_ref)
            loss_ref[...] = jnp.zeros_like(loss_ref)

        dgf_ref[...] += jnp.sum(dgrow, axis=0, keepdims=True)
        loss_ref[...] += jnp.broadcast_to(part, loss_ref.shape)

    row = lambda n: pl.BlockSpec((tm, n), lambda i: (i, 0))
    return pl.pallas_call(
        body,
        name="ffn_fwd",
        grid=(nt,),
        in_specs=[
            row(D_MODEL), _const_spec((1, D_MODEL)), _const_spec((1, D_MODEL)), row(D_MODEL),
            _const_spec(w_gate_t.shape), _const_spec(w_up_t.shape), _const_spec(w_down.shape),
        ],
        out_specs=[
            row(D_MODEL), row(D_FF), row(D_FF), row(D_FF), row(D_MODEL),
            pl.BlockSpec((8, LANES), lambda i: (0, 0)),
            pl.BlockSpec((1, D_MODEL), lambda i: (0, 0)),
        ],
        out_shape=[
            jax.ShapeDtypeStruct((T, D_MODEL), BF16),
            jax.ShapeDtypeStruct((T, D_FF), BF16),
            jax.ShapeDtypeStruct((T, D_FF), BF16),
            jax.ShapeDtypeStruct((T, D_FF), BF16),
            jax.ShapeDtypeStruct((T, D_MODEL), F32),
            jax.ShapeDtypeStruct((8, LANES), F32),
            jax.ShapeDtypeStruct((1, D_MODEL), F32),
        ],
        compiler_params=_params(("arbitrary",)),
    )(x1, g2, gf, tgt, w_gate_t, w_up_t, w_down)


def _ffn_bwd(dx2, gate, up, x1, g2, w_gate_t, w_up_t, w_down):
    T = x1.shape[0]
    tm = min(T, FF_ROW_TILE)
    nc = D_FF // FF_CHUNK

    def body(dx2_ref, gate_ref, up_ref, x1_ref, g2_ref, wg_ref, wu_ref, wd_ref, dgate_ref, dup_ref, dx1_ref, dg2_ref):
        dx2v = dx2_ref[...]
        dx2b = dx2v.astype(BF16)
        for c in range(nc):
            sl = slice(c * FF_CHUNK, (c + 1) * FF_CHUNK)
            dact = _mm_nt(dx2b, wd_ref[sl, :])
            gate = gate_ref[:, sl].astype(F32)
            sg = _sigmoid(gate)
            silu = gate * sg
            dgate = (dact * up_ref[:, sl].astype(F32) * (sg * (1.0 + gate * (1.0 - sg)))).astype(BF16)
            dup = (dact * silu).astype(BF16)
            dgate_ref[:, sl] = dgate
            dup_ref[:, sl] = dup
        dh2 = _mm(dgate_ref[...], wg_ref[...]) + _mm(dup_ref[...], wu_ref[...])
        g2v = g2_ref[...]
        _, xh, r = _rms_fwd(x1_ref[...], g2v)
        dxn, dgrow = _rms_bwd(dh2, xh, r, g2v)
        dx1_ref[...] = dx2v + dxn

        @pl.when(pl.program_id(0) == 0)
        def _():
            dg2_ref[...] = jnp.zeros_like(dg2_ref)

        dg2_ref[...] += jnp.sum(dgrow, axis=0, keepdims=True)

    row = lambda n: pl.BlockSpec((tm, n), lambda i: (i, 0))
    return pl.pallas_call(
        body,
        name="ffn_bwd",
        grid=(T // tm,),
        in_specs=[
            row(D_MODEL), row(D_FF), row(D_FF), row(D_MODEL), _const_spec((1, D_MODEL)),
            _const_spec(w_gate_t.shape), _const_spec(w_up_t.shape), _const_spec(w_down.shape),
        ],
        out_specs=[row(D_FF), row(D_FF), row(D_MODEL), pl.BlockSpec((1, D_MODEL), lambda i: (0, 0))],
        out_shape=[
            jax.ShapeDtypeStruct((T, D_FF), BF16),
            jax.ShapeDtypeStruct((T, D_FF), BF16),
            jax.ShapeDtypeStruct((T, D_MODEL), F32),
            jax.ShapeDtypeStruct((1, D_MODEL), F32),
        ],
        compiler_params=_params(("arbitrary",), VMEM_LIMIT_MAX),
    )(dx2, gate, up, x1, g2, w_gate_t, w_up_t, w_down)


def _mix_bwd(dx1, gates, pool_y, attn_y, p2, scale, w_out, w_ao, w_po, token):
    T = dx1.shape[0]
    tm = ROW_TILE

    def body(dx1_ref, gt_ref, py_ref, ay_ref, p2_ref, sc_ref, wout_ref, wao_ref, wpo_ref, token_ref, dgt_ref, dpy_ref, day_ref, da_ref, dp2_ref, dsc_ref):
        dm = _mm_nt(dx1_ref[...].astype(BF16), wout_ref[...])
        sp = _sigmoid(gt_ref[:, :D_MODEL].astype(F32))
        sa = _sigmoid(gt_ref[:, D_MODEL:].astype(F32))
        dgt_ref[:, :D_MODEL] = (dm * py_ref[...].astype(F32) * (sp * (1.0 - sp))).astype(BF16)
        dgt_ref[:, D_MODEL:] = (dm * ay_ref[...].astype(F32) * (sa * (1.0 - sa))).astype(BF16)
        dpy = (dm * sp).astype(BF16)
        day = (dm * sa).astype(BF16)
        dpy_ref[...] = dpy
        day_ref[...] = day
        da_ref[...] = _mm_nt(day, _whole_cols(wao_ref)).astype(BF16)
        dp3 = _mm_nt(dpy, _whole_cols(wpo_ref))
        dp2_ref[...] = (dp3 * sc_ref[...]).astype(BF16)

        @pl.when(pl.program_id(0) == 0)
        def _():
            dsc_ref[...] = jnp.zeros_like(dsc_ref)

        dsc_ref[...] += jnp.sum(dp3 * p2_ref[...], axis=0, keepdims=True)

    row = lambda n: pl.BlockSpec((tm, n), lambda i: (i, 0))
    return pl.pallas_call(
        body,
        name="mix_bwd",
        grid=(T // tm,),
        in_specs=[
            row(D_MODEL), row(2 * D_MODEL), row(D_MODEL), row(D_MODEL), row(POOL_WIDTH), _const_spec((1, POOL_WIDTH)),
            _const_spec(w_out.shape), _const_spec(w_ao.shape), _const_spec(w_po.shape), _HBM,
        ],
        out_specs=[row(2 * D_MODEL), row(D_MODEL), row(D_MODEL), row(ATTN_WIDTH), row(POOL_WIDTH), pl.BlockSpec((1, POOL_WIDTH), lambda i: (0, 0))],
        out_shape=[
            jax.ShapeDtypeStruct((T, 2 * D_MODEL), BF16),
            jax.ShapeDtypeStruct((T, D_MODEL), BF16),
            jax.ShapeDtypeStruct((T, D_MODEL), BF16),
            jax.ShapeDtypeStruct((T, ATTN_WIDTH), BF16),
            jax.ShapeDtypeStruct((T, POOL_WIDTH), BF16),
            jax.ShapeDtypeStruct((1, POOL_WIDTH), F32),
        ],
        compiler_params=_params(("arbitrary",)),
    )(dx1, gates, pool_y, attn_y, p2, scale, w_out, w_ao, w_po, token)


def _pool_bwd(dp2, pm, mix_b, token, n_seq, S):
    T = n_seq * S

    def body(dp2_ref, pm_ref, mix_ref, token_ref, du_ref, dmix_ref):
        g = pl.program_id(0)
        dp2v = dp2_ref[...]
        dpm = _mm_nt(dp2v, mix_ref[...])
        row = lax.broadcasted_iota(jnp.int32, dpm.shape, 0)
        w = _window_pick(g, 2.0, 4.0, 8.0, 16.0)
        e = dpm / jnp.minimum((row + 1).astype(F32), w)

        def ahead(a, k):
            return jnp.where(row < S - k, pltpu.roll(a, S - k, 0), 0.0)

        r2 = e + ahead(e, 1)
        r4 = r2 + ahead(r2, 2)
        r8 = r4 + ahead(r4, 4)
        r16 = r8 + ahead(r8, 8)
        du_ref[...] = (_window_pick(g, r2, r4, r8, r16) - dpm).astype(BF16)

        @pl.when(pl.program_id(1) == 0)
        def _():
            dmix_ref[...] = jnp.zeros_like(dmix_ref)

        dmix_ref[...] += _mm_tn(pm_ref[...], dp2v)

    grp = pl.BlockSpec((S, GROUP_DIM), lambda g, s: (s, g))
    mixs = pl.BlockSpec((None, GROUP_DIM, GROUP_DIM), lambda g, s: (g, 0, 0))
    return pl.pallas_call(
        body,
        name="pool_bwd",
        grid=(len(POOL_WINDOWS), n_seq),
        in_specs=[grp, grp, mixs, _HBM],
        out_specs=[grp, mixs],
        out_shape=[jax.ShapeDtypeStruct((T, POOL_WIDTH), BF16), jax.ShapeDtypeStruct((len(POOL_WINDOWS), GROUP_DIM, GROUP_DIM), F32)],
        compiler_params=_params(("parallel", "arbitrary")),
    )(dp2, pm, mix_b, token)


def _attn_bwd(qkv, da, a, fcol, lse, n_seq, S):
    T = n_seq * S
    tb = ATTN_BLOCK
    nb = S // tb
    scale = HEAD_DIM ** -0.5

    def body(q_ref, k_ref, v_ref, do_ref, o_ref, fc_ref, st_ref, dq_ref, dk_ref, dv_ref, dfk_ref, dfq_ref,
             qa_sc, doa_sc, qat_sc, doat_sc, dq_acc, ka_sc, va_sc, dkt_sc, dvt_sc):
        j = pl.program_id(1)
        lane = lax.broadcasted_iota(jnp.int32, (1, LANES), 1)
        low = lane < HEAD_DIM

        @pl.when(j == 0)
        def _():
            dq_acc[...] = jnp.zeros_like(dq_acc)
            place = _bias_placement(0)

            def rows_q(i, carry):
                r0 = pl.multiple_of(i * tb, tb)
                delta = jnp.zeros((tb, LANES), F32)
                for h in range(N_HEADS):
                    pair = slice((h // 2) * LANES, (h // 2 + 1) * LANES)
                    prod = do_ref[pl.ds(r0, tb), pair].astype(F32) * o_ref[pl.ds(r0, tb), pair].astype(F32)
                    head = (lane >= HEAD_DIM * (h % 2)) & (lane < HEAD_DIM * (h % 2 + 1))
                    delta = jnp.where(lane == h, jnp.sum(jnp.where(head, prod, 0.0), axis=1, keepdims=True), delta)
                cq = fc_ref[pl.ds(r0, tb), :] - st_ref[pl.ds(r0, tb), :]
                q_bias = _mm(_bias_lanes(cq), place).astype(BF16)
                do_bias = _mm(_bias_lanes(-delta), place).astype(BF16)
                for h in range(N_HEADS):
                    pair = slice((h // 2) * LANES, (h // 2 + 1) * LANES)
                    qa = _augment(q_ref[pl.ds(r0, tb), pair], h, q_bias, 1)
                    doa = _augment(do_ref[pl.ds(r0, tb), pair], h, do_bias, None)
                    qa_sc[h, pl.ds(r0, tb), :] = qa
                    doa_sc[h, pl.ds(r0, tb), :] = doa
                    qat_sc[h, i] = qa.astype(F32).T.astype(BF16)
                    doat_sc[h, i] = doa.astype(F32).T.astype(BF16)
                return carry

            lax.fori_loop(0, nb, rows_q, 0)

        c0 = pl.multiple_of(j * tb, tb)
        k_bias = _mm(_bias_lanes(-fc_ref[pl.ds(c0, tb), :]), _bias_placement(1)).astype(BF16)
        for h in range(N_HEADS):
            pair = slice((h // 2) * LANES, (h // 2 + 1) * LANES)
            ka_sc[h] = _augment(k_ref[:, pair] * scale, h, k_bias, 0)
            va_sc[h] = _augment(v_ref[:, pair], h, None, 0)
        dkt_sc[...] = jnp.zeros_like(dkt_sc)
        dvt_sc[...] = jnp.zeros_like(dvt_sc)
        causal = lax.broadcasted_iota(jnp.int32, (tb, tb), 1) <= lax.broadcasted_iota(jnp.int32, (tb, tb), 0)

        def step(i, masked):
            r0 = pl.multiple_of(i * tb, tb)
            for h in range(N_HEADS):
                s = _mm_nt(qa_sc[h, pl.ds(r0, tb), :], ka_sc[h])
                if masked:
                    s = jnp.where(causal, s, -jnp.inf)
                pr = jnp.exp(s)
                dvt_sc[h] += _mm(doat_sc[h, i], pr.astype(BF16))
                dsb = (pr * _mm_nt(doa_sc[h, pl.ds(r0, tb), :], va_sc[h])).astype(BF16)
                dkt_sc[h] += _mm(qat_sc[h, i], dsb)
                dq_acc[h, pl.ds(r0, tb), :] += _mm(dsb, ka_sc[h])

        step(j, True)

        def loop_body(i, carry):
            step(i, False)
            return carry

        lax.fori_loop(j + 1, nb, loop_body, 0)
        dfk = jnp.zeros((tb, LANES), F32)
        for p in range(N_PAIRS):
            dk = [dkt_sc[2 * p + hh].T for hh in range(2)]
            dv = [dvt_sc[2 * p + hh].T for hh in range(2)]
            dk_ref[:, p * LANES : (p + 1) * LANES] = (jnp.where(low, dk[0], dk[1]) * scale).astype(BF16)
            dv_ref[:, p * LANES : (p + 1) * LANES] = jnp.where(low, dv[0], dv[1]).astype(BF16)
            for hh in range(2):
                b = HEAD_DIM * (1 - hh) + 3
                dfk = jnp.where(lane == 2 * p + hh, -dk[hh][:, b : b + 1], dfk)
        dfk_ref[...] = dfk

        @pl.when(j == nb - 1)
        def _():
            def rows_dq(i, carry):
                r0 = pl.multiple_of(i * tb, tb)
                dfq = jnp.zeros((tb, LANES), F32)
                for p in range(N_PAIRS):
                    parts = [dq_acc[2 * p + hh, pl.ds(r0, tb), :] for hh in range(2)]
                    dq_ref[pl.ds(r0, tb), p * LANES : (p + 1) * LANES] = jnp.where(low, parts[0], parts[1]).astype(BF16)
                    for hh in range(2):
                        b = HEAD_DIM * (1 - hh)
                        dfq = jnp.where(lane == 2 * p + hh, parts[hh][:, b : b + 1], dfq)
                dfq_ref[pl.ds(r0, tb), :] = dfq
                return carry

            lax.fori_loop(0, nb, rows_dq, 0)

    seq = lambda w, col: pl.BlockSpec((S, w), lambda s, j: (s, col))
    seq_in = lambda w, col: pl.BlockSpec((S, w), lambda s, j: (s, col), pipeline_mode=pl.Buffered(1))
    blk = lambda w, col: pl.BlockSpec((tb, w), lambda s, j: (s * nb + j, col))
    return pl.pallas_call(
        body,
        name="attn_bwd",
        grid=(n_seq, nb),
        in_specs=[seq_in(ATTN_WIDTH, 0), blk(ATTN_WIDTH, 1), blk(ATTN_WIDTH, 2), seq_in(ATTN_WIDTH, 0), seq_in(ATTN_WIDTH, 0), seq_in(LANES, 0), seq_in(LANES, 0)],
        out_specs=[seq(ATTN_WIDTH, 0), blk(ATTN_WIDTH, 0), blk(ATTN_WIDTH, 0), blk(LANES, 0), seq(LANES, 0)],
        out_shape=[
            jax.ShapeDtypeStruct((T, ATTN_WIDTH), BF16),
            jax.ShapeDtypeStruct((T, ATTN_WIDTH), BF16),
            jax.ShapeDtypeStruct((T, ATTN_WIDTH), BF16),
            jax.ShapeDtypeStruct((T, LANES), F32),
            jax.ShapeDtypeStruct((T, LANES), F32),
        ],
        scratch_shapes=[
            pltpu.VMEM((N_HEADS, S, LANES), BF16),
            pltpu.VMEM((N_HEADS, S, LANES), BF16),
            pltpu.VMEM((N_HEADS, nb, LANES, tb), BF16),
            pltpu.VMEM((N_HEADS, nb, LANES, tb), BF16),
            pltpu.VMEM((N_HEADS, S, LANES), F32),
            pltpu.VMEM((N_HEADS, tb, LANES), BF16),
            pltpu.VMEM((N_HEADS, tb, LANES), BF16),
            pltpu.VMEM((N_HEADS, LANES, tb), F32),
            pltpu.VMEM((N_HEADS, LANES, tb), F32),
        ],
        compiler_params=_params(("parallel", "arbitrary"), VMEM_LIMIT_MAX),
    )(qkv, qkv, qkv, da, a, fcol, lse)


def _forget_bwd(dfk, dfq, fl, b_pad, n_seq, S):
    def body(df_ref, dfq_ref, fl_ref, b_ref, dfl_ref, db_ref):
        t = (df_ref[...] + dfq_ref[...]).T
        lane = lax.broadcasted_iota(jnp.int32, t.shape, 1)
        k = 1
        while k < S:
            t = t + jnp.where(lane < S - k, pltpu.roll(t, S - k, 1), 0.0)
            k *= 2
        dfl = t.T * _sigmoid(-(fl_ref[...] + b_ref[...]))
        dfl_ref[...] = dfl.astype(BF16)

        @pl.when(pl.program_id(0) == 0)
        def _():
            db_ref[...] = jnp.zeros_like(db_ref)

        db_ref[...] += jnp.sum(dfl, axis=0, keepdims=True)

    return pl.pallas_call(
        body,
        name="forget_bwd",
        grid=(n_seq,),
        in_specs=[
            pl.BlockSpec((S, LANES), lambda s: (s, 0)),
            pl.BlockSpec((S, LANES), lambda s: (s, 0)),
            pl.BlockSpec((S, FL_PAD), lambda s: (s, 0)),
            _const_spec((1, FL_PAD)),
        ],
        out_specs=[pl.BlockSpec((S, FL_PAD), lambda s: (s, 0)), pl.BlockSpec((1, FL_PAD), lambda s: (0, 0))],
        out_shape=[jax.ShapeDtypeStruct((n_seq * S, FL_PAD), BF16), jax.ShapeDtypeStruct((1, FL_PAD), F32)],
        compiler_params=_params(("arbitrary",)),
    )(dfk, dfq, fl, b_pad)


def _in_proj_bwd(du, dq, dk, dv, dfl, dgates, x, dx1, g1, w_uqkv, w_fl, w_g, token):
    T = x.shape[0]
    tm = ROW_TILE

    def body(du_ref, dq_ref, dk_ref, dv_ref, dfl_ref, dgt_ref, x_ref, dx1_ref, g_ref, wa_ref, wf_ref, wg_ref, token_ref, dx_ref, dg_ref):
        dz = jnp.concatenate([du_ref[...], dq_ref[...], dk_ref[...], dv_ref[...]], axis=1)
        dh = _mm_nt(dz, wa_ref[...]) + _mm_nt(dgt_ref[...], wg_ref[...]) + _mm_nt(dfl_ref[...], wf_ref[...])
        gv = g_ref[...]
        _, xh, r = _rms_fwd(x_ref[...], gv)
        dxn, dgrow = _rms_bwd(dh, xh, r, gv)
        dx_ref[...] = dx1_ref[...] + dxn

        @pl.when(pl.program_id(0) == 0)
        def _():
            dg_ref[...] = jnp.zeros_like(dg_ref)

        dg_ref[...] += jnp.sum(dgrow, axis=0, keepdims=True)

    row = lambda n: pl.BlockSpec((tm, n), lambda i: (i, 0))
    return pl.pallas_call(
        body,
        name="in_proj_bwd",
        grid=(T // tm,),
        in_specs=[
            row(512), row(512), row(512), row(512), row(FL_PAD), row(2 * D_MODEL), row(D_MODEL), row(D_MODEL), _const_spec((1, D_MODEL)),
            _const_spec(w_uqkv.shape), _const_spec(w_fl.shape), _const_spec(w_g.shape), _HBM,
        ],
        out_specs=[row(D_MODEL), pl.BlockSpec((1, D_MODEL), lambda i: (0, 0))],
        out_shape=[jax.ShapeDtypeStruct((T, D_MODEL), F32), jax.ShapeDtypeStruct((1, D_MODEL), F32)],
        compiler_params=_params(("arbitrary",)),
    )(du, dq, dk, dv, dfl, dgates, x, dx1, g1, w_uqkv, w_fl, w_g, token)


def _pick_block(n):
    for b in (1024, 512, 1408, 256, 128):
        if n % b == 0:
            return b
    raise ValueError(n)


def _matmul_tn(a, b, name, col_chunks=False):
    T, K = a.shape
    N = b.shape[1]
    bt, bk, bn = min(T, DW_TOKENS), _pick_block(K), _pick_block(N)
    nt = T // bt
    c = N // N_DEV
    assert not col_chunks or (bn == N and c % LANES == 0)

    def body(a_ref, b_ref, o_ref, acc):
        @pl.when(pl.program_id(2) == 0)
        def _():
            acc[...] = jnp.zeros_like(acc)

        acc[...] += _mm_tn(a_ref[...].astype(BF16), b_ref[...].astype(BF16))

        @pl.when(pl.program_id(2) == nt - 1)
        def _():
            if col_chunks:
                for d in range(N_DEV):
                    o_ref[d] = acc[:, d * c : (d + 1) * c].astype(BF16)
            else:
                o_ref[...] = acc[...].astype(BF16)

    if col_chunks:
        out_spec, out_shape = pl.BlockSpec((N_DEV, bk, c), lambda k, n, t: (0, k, 0)), (N_DEV, K, c)
    else:
        out_spec, out_shape = pl.BlockSpec((bk, bn), lambda k, n, t: (k, n)), (K, N)
    return pl.pallas_call(
        body,
        name=name,
        grid=(K // bk, N // bn, nt),
        in_specs=[pl.BlockSpec((bt, bk), lambda k, n, t: (t, k)), pl.BlockSpec((bt, bn), lambda k, n, t: (t, n))],
        out_specs=out_spec,
        out_shape=jax.ShapeDtypeStruct(out_shape, BF16),
        scratch_shapes=[pltpu.VMEM((bk, bn), F32)],
        compiler_params=_params(("parallel", "parallel", "arbitrary")),
    )(a, b)


W_IN_A = POOL_WIDTH + 3 * ATTN_WIDTH
W_IN_SHARD = (W_IN_A + N_HEADS + 2 * D_MODEL) // N_DEV
_W_IN_PIECES = ((0, W_IN_A), (W_IN_A, W_IN_A + N_HEADS), (W_IN_A + N_HEADS, W_IN_A + N_HEADS + 2 * D_MODEL))


def _w_in_segments(d):
    lo, hi = d * W_IN_SHARD, (d + 1) * W_IN_SHARD
    out = []
    for p, (a, b) in enumerate(_W_IN_PIECES):
        s, e = max(lo, a), min(hi, b)
        if s < e:
            out.append((p, s - a, s - lo, e - s))
    return out


def _w_in_pieces(gathered):
    tm = ROW_TILE // 2

    def body(g_ref, wa_ref, wf_ref, wg_ref):
        outs = (wa_ref, wf_ref, wg_ref)
        wf_ref[...] = jnp.zeros_like(wf_ref)
        for d in range(N_DEV):
            for p, at, frm, n in _w_in_segments(d):
                outs[p][:, at : at + n] = g_ref[d, :, frm : frm + n]

    return pl.pallas_call(
        body,
        name="w_in_pieces",
        grid=(D_MODEL // tm,),
        in_specs=[pl.BlockSpec((N_DEV, tm, W_IN_SHARD), lambda i: (0, i, 0))],
        out_specs=[pl.BlockSpec((tm, W_IN_A), lambda i: (i, 0)), pl.BlockSpec((tm, FL_PAD), lambda i: (i, 0)), pl.BlockSpec((tm, 2 * D_MODEL), lambda i: (i, 0))],
        out_shape=[
            jax.ShapeDtypeStruct((D_MODEL, W_IN_A), gathered.dtype),
            jax.ShapeDtypeStruct((D_MODEL, FL_PAD), gathered.dtype),
            jax.ShapeDtypeStruct((D_MODEL, 2 * D_MODEL), gathered.dtype),
        ],
        compiler_params=_params(("parallel",)),
    )(gathered)


def _dw_in(h, du, dq, dk, dv, dfl, dgates, token):
    T = h.shape[0]
    bt, bk = min(T, DW_TOKENS // 2), 512
    nt = T // bt
    pieces = (du, dq, dk, dv, dfl, dgates)
    offs = [0]
    for p in pieces:
        offs.append(offs[-1] + p.shape[1])

    def body(h_ref, *rest):
        refs, o_ref, acc = rest[: len(pieces)], rest[-2], rest[-1]

        @pl.when(pl.program_id(1) == 0)
        def _():
            acc[...] = jnp.zeros_like(acc)

        ht = h_ref[...].T
        for ref, at in zip(refs, offs):
            acc[:, at : at + ref.shape[1]] += _mm(ht, ref[...])

        @pl.when(pl.program_id(1) == nt - 1)
        def _():
            starts = (0, W_IN_A, W_IN_A + FL_PAD)
            for d in range(N_DEV):
                for p, at, to, n in _w_in_segments(d):
                    o_ref[d % 2, d // 2, :, to : to + n] = acc[:, starts[p] + at : starts[p] + at + n].astype(BF16)

    return pl.pallas_call(
        body,
        name="dw_in",
        grid=(D_MODEL // bk, nt),
        in_specs=[pl.BlockSpec((bt, bk), lambda k, t: (t, k))] + [pl.BlockSpec((bt, p.shape[1]), lambda k, t: (t, 0)) for p in pieces] + [_HBM],
        out_specs=pl.BlockSpec((2, 4, bk, W_IN_SHARD), lambda k, t: (0, 0, k, 0)),
        out_shape=jax.ShapeDtypeStruct((2, 4, D_MODEL, W_IN_SHARD), BF16),
        scratch_shapes=[pltpu.VMEM((bk, offs[-1]), F32)],
        compiler_params=_params(("parallel", "arbitrary")),
    )(h, *pieces, token)


def _position():
    return lax.axis_index("x"), lax.axis_index("y"), lax.axis_index("c")


_HBM = pl.BlockSpec(memory_space=pl.ANY)


def _all_gather(blocks, name):
    n = len(blocks)

    def body(*refs):
        xs, outs = refs[:n], refs[n : 2 * n]
        send_sems, recv_sems, local_sems = refs[2 * n :]
        x, y, c = _position()
        me, sibling = (x, y, c), (x, y, 1 - c)
        chips = [(1 - x, y), (x, 1 - y), (1 - x, 1 - y)]

        def rows(a, px, py, pc):
            return outs[a].at[4 * px + 2 * py + pc]

        def copy(a, k, blk, to, src=None):
            return pltpu.make_async_remote_copy(
                src_ref=rows(a, *blk) if src is None else src, dst_ref=rows(a, *blk),
                send_sem=send_sems.at[7 * a + k], recv_sem=recv_sems.at[7 * a + k], device_id=to, device_id_type=MESH,
            )

        mine = [pltpu.make_async_copy(xs[a], rows(a, *me), local_sems.at[a]) for a in range(n)]
        for cp in mine:
            cp.start()
        first = []
        for a in range(n):
            first.append(copy(a, 0, me, sibling, src=xs[a]))
            first += [copy(a, 1 + j, me, (*chip, c), src=xs[a]) for j, chip in enumerate(chips)]
        for cp in first:
            cp.start()
        passed = []
        for j, chip in enumerate(chips):
            for a in range(n):
                copy(a, 1 + j, (*chip, c), me).wait_recv()
                passed.append(copy(a, 4 + j, (*chip, c), sibling))
                passed[-1].start()
        for a in range(n):
            copy(a, 0, sibling, me).wait_recv()
        for j, chip in enumerate(chips):
            for a in range(n):
                copy(a, 4 + j, (*chip, 1 - c), me).wait_recv()
        for cp in first + passed:
            cp.wait_send()
        for cp in mine:
            cp.wait()

    return pl.pallas_call(
        body,
        name=name,
        out_shape=[jax.ShapeDtypeStruct((N_DEV, *b.shape), b.dtype) for b in blocks],
        in_specs=[_HBM] * n,
        out_specs=[_HBM] * n,
        scratch_shapes=[pltpu.SemaphoreType.DMA((7 * n,)), pltpu.SemaphoreType.DMA((7 * n,)), pltpu.SemaphoreType.DMA((n,))],
    )(*blocks)


_SEM = pl.BlockSpec(memory_space=pltpu.SEMAPHORE)
_HBM_ONLY = pl.BlockSpec(memory_space=pltpu.HBM)
_SIDE_EFFECT = pltpu.SideEffectType.DATAFLOW_SIDE_EFFECTING


def _peer(x, y, c, k):
    return (1 - x if k & 4 else x, 1 - y if k & 2 else y, 1 - c if k & 1 else c)


_PEER_BITS = {"gather": range(1, N_DEV), "scatter": range(1, N_DEV), "chips": (4, 2, 6)}
_LAND_SLOTS = {"gather": N_DEV, "scatter": N_DEV, "chips": 3}


def _exchange_copies(src_refs, land_refs, send_sems, recv_sems, pattern, receive_side):
    x, y, c = _position()
    me = 4 * x + 2 * y + c
    bits = _PEER_BITS[pattern]
    cps = []
    for j, k in enumerate(bits):
        px, py, pc = _peer(x, y, c, k)
        peer = 4 * px + 2 * py + pc
        for a, (src, land) in enumerate(zip(src_refs, land_refs)):
            if pattern == "chips":
                s, slot = src.at[2 * px + py], j
            else:
                s, slot = (src if pattern == "gather" else src.at[peer]), (peer if receive_side else me)
            cps.append(pltpu.make_async_remote_copy(
                src_ref=s, dst_ref=land.at[slot],
                send_sem=send_sems.at[len(bits) * a + j], recv_sem=recv_sems.at[len(bits) * a + j],
                device_id=(px, py, pc), device_id_type=MESH,
            ))
    return cps


def _own_copies(src_refs, land_refs, own_sems):
    x, y, c = _position()
    return [
        pltpu.make_async_copy(src, land.at[4 * x + 2 * y + c], own_sems.at[a])
        for a, (src, land) in enumerate(zip(src_refs, land_refs))
    ]


def _exchange_start(srcs, after, name, pattern):
    n = len(srcs)
    m = len(_PEER_BITS[pattern])
    lands = [jax.ShapeDtypeStruct((_LAND_SLOTS[pattern], *s.shape[-2:]), s.dtype) for s in srcs]

    def body(*refs):
        src_refs, land_refs = refs[1 : 1 + n], refs[1 + n : 1 + 2 * n]
        send_sems, recv_sems, own_sems = refs[1 + 2 * n : 4 + 2 * n]
        token = refs[-1]
        if pattern == "gather":
            for cp in _own_copies(src_refs, land_refs, own_sems):
                cp.start()
        for cp in _exchange_copies(src_refs, land_refs, send_sems, recv_sems, pattern, receive_side=False):
            cp.start()
        token[...] = jnp.zeros_like(token)

    hbm = lambda t: pltpu.with_memory_space_constraint(t, pltpu.HBM)
    out = pl.pallas_call(
        body,
        name=name,
        out_shape=(
            pltpu.SemaphoreType.DMA((m * n,)), pltpu.SemaphoreType.DMA((m * n,)), pltpu.SemaphoreType.DMA((n,)),
            *[pltpu.HBM(s.shape, s.dtype) for s in srcs], *[pltpu.HBM(l.shape, l.dtype) for l in lands],
            jax.ShapeDtypeStruct((8, LANES), F32),
        ),
        in_specs=(_HBM, *[_HBM_ONLY] * (2 * n)),
        out_specs=(_SEM, _SEM, _SEM, *[_HBM_ONLY] * (2 * n), pl.BlockSpec(memory_space=pltpu.VMEM)),
        input_output_aliases={1 + i: 3 + i for i in range(2 * n)},
        compiler_params=pltpu.CompilerParams(has_side_effects=_SIDE_EFFECT),
    )(after, *[hbm(s) for s in srcs], *[hbm(lax.empty(l.shape, l.dtype)) for l in lands])
    return out[:3], out[3 : 3 + n], out[3 + n : 3 + 2 * n], out[-1]


def _exchange_wait(sems, srcs, lands, after, name, pattern):
    n = len(srcs)

    def body(*refs):
        src_refs, land_refs = refs[:n], refs[n : 2 * n]
        send_sems, recv_sems, own_sems = refs[2 * n : 2 * n + 3]
        if pattern == "gather":
            for cp in _own_copies(src_refs, land_refs, own_sems):
                cp.wait()
        for cp in _exchange_copies(src_refs, land_refs, send_sems, recv_sems, pattern, receive_side=True):
            cp.wait_send()
            cp.wait_recv()

    out = pl.pallas_call(
        body,
        name=name,
        out_shape=(*[pltpu.HBM(s.shape, s.dtype) for s in srcs], *[pltpu.HBM(l.shape, l.dtype) for l in lands]),
        in_specs=(*[_HBM_ONLY] * (2 * n), _SEM, _SEM, _SEM, _HBM),
        out_specs=tuple([_HBM_ONLY] * (2 * n)),
        input_output_aliases={i: i for i in range(2 * n)},
        compiler_params=pltpu.CompilerParams(has_side_effects=_SIDE_EFFECT),
    )(*srcs, *lands, *sems, after)
    return out[:n], out[n:]


def _sibling_exchange(sends):
    n = len(sends)

    def body(*refs):
        srcs, dsts = refs[:n], refs[n : 2 * n]
        send_sems, recv_sems = refs[2 * n :]
        x, y, c = _position()
        cps = [
            pltpu.make_async_remote_copy(
                src_ref=srcs[a].at[1 - c], dst_ref=dsts[a], send_sem=send_sems.at[a], recv_sem=recv_sems.at[a],
                device_id=(x, y, 1 - c), device_id_type=MESH,
            )
            for a in range(n)
        ]
        for cp in cps:
            cp.start()
        for cp in cps:
            cp.wait()

    return pl.pallas_call(
        body,
        name="rs_sibling",
        out_shape=[jax.ShapeDtypeStruct(s.shape[1:], s.dtype) for s in sends],
        in_specs=[_HBM] * n,
        out_specs=[_HBM] * n,
        scratch_shapes=[pltpu.SemaphoreType.DMA((n,)), pltpu.SemaphoreType.DMA((n,))],
    )(*sends)


def _rows_tile(r):
    return ROW_TILE if r % ROW_TILE == 0 else r


def _pair_sum(send, got, core, name):
    _, _, r, c = send.shape
    br = _rows_tile(r)

    def body(core_ref, a_ref, b_ref, o_ref):
        o_ref[...] = (a_ref[...].astype(F32) + b_ref[...].astype(F32)).astype(o_ref.dtype)

    return pl.pallas_call(
        body,
        name=name,
        grid_spec=pltpu.PrefetchScalarGridSpec(
            num_scalar_prefetch=1,
            grid=(4, r // br),
            in_specs=[
                pl.BlockSpec((None, None, br, c), lambda n, i, core: (core[0], n, i, 0)),
                pl.BlockSpec((None, br, c), lambda n, i, core: (n, i, 0)),
            ],
            out_specs=pl.BlockSpec((None, br, c), lambda n, i, core: (n, i, 0)),
        ),
        out_shape=jax.ShapeDtypeStruct((4, r, c), send.dtype),
        compiler_params=_params(("parallel", "parallel")),
    )(core, send, got)


def _adamw(w, g, m, v):
    m = ADAM_B1 * m + (1.0 - ADAM_B1) * g
    v = ADAM_B2 * v + (1.0 - ADAM_B2) * (g * g)
    m_hat = m / (1.0 - ADAM_B1 ** ADAM_STEP)
    v_hat = v / (1.0 - ADAM_B2 ** ADAM_STEP)
    delta = -ADAM_LR * (m_hat / (jnp.sqrt(v_hat) + ADAM_EPS) + ADAM_WD * w)
    return delta, m, v


def _shard_update(send, got, recv, w, m, v, pos, name):
    _, r, c = w.shape
    br = _rows_tile(r)

    def body(pos_ref, a_ref, b_ref, r_ref, w_ref, m_ref, v_ref, g_ref, d_ref, nm_ref, nv_ref):
        g = a_ref[...].astype(F32) + b_ref[...].astype(F32)
        for n in range(3):
            g = g + r_ref[n].astype(F32)
        g_ref[...] = g
        d_ref[...], nm_ref[...], nv_ref[...] = _adamw(w_ref[...], g, m_ref[...], v_ref[...])

    own = pl.BlockSpec((None, br, c), lambda i, pos: (0, i, 0))
    return pl.pallas_call(
        body,
        name=name,
        grid_spec=pltpu.PrefetchScalarGridSpec(
            num_scalar_prefetch=1,
            grid=(r // br,),
            in_specs=[
                pl.BlockSpec((None, None, br, c), lambda i, pos: (pos[0], pos[1], i, 0)),
                pl.BlockSpec((None, br, c), lambda i, pos: (pos[1], i, 0)),
                pl.BlockSpec((3, br, c), lambda i, pos: (0, i, 0)),
                own, own, own,
            ],
            out_specs=[own, own, own, own],
        ),
        out_shape=[jax.ShapeDtypeStruct((1, r, c), F32)] * 4,
        compiler_params=_params(("parallel",)),
    )(pos, send, got, recv, w, m, v)


def _shard_update_direct(parts, chunks, w, m, v, me, name):
    _, r, c = w.shape
    br = _rows_tile(r)

    def body(me_ref, p_ref, own_ref, w_ref, m_ref, v_ref, g_ref, d_ref, nm_ref, nv_ref):
        g = None
        for n in range(N_DEV):
            part = jnp.where(me_ref[0] == n, own_ref[...], p_ref[n]).astype(F32)
            g = part if g is None else g + part
        g_ref[...] = g
        d_ref[...], nm_ref[...], nv_ref[...] = _adamw(w_ref[...], g, m_ref[...], v_ref[...])

    shard = pl.BlockSpec((None, br, c), lambda i, me: (0, i, 0))
    return pl.pallas_call(
        body,
        name=name,
        grid_spec=pltpu.PrefetchScalarGridSpec(
            num_scalar_prefetch=1,
            grid=(r // br,),
            in_specs=[
                pl.BlockSpec((N_DEV, br, c), lambda i, me: (0, i, 0)),
                pl.BlockSpec((None, br, c), lambda i, me: (me[0], i, 0)),
                shard, shard, shard,
            ],
            out_specs=[shard, shard, shard, shard],
        ),
        out_shape=[jax.ShapeDtypeStruct((1, r, c), F32)] * 4,
        compiler_params=_params(("parallel",)),
    )(me, parts, chunks, w, m, v)


def _small_update(parts, first_rows, ws, ms, vs):
    k = len(ws)

    def unpacked(rows, shape):
        if len(shape) == 2 and shape[1] <= LANES:
            return rows[0:1, : shape[1]]
        if len(shape) == 2:
            return jnp.concatenate([rows[r : r + 1] for r in range(shape[1] // LANES)], axis=1)
        return rows.reshape(shape)

    def body(p_ref, f_ref, *refs):
        w_refs, m_refs, v_refs = refs[:k], refs[k : 2 * k], refs[2 * k : 3 * k]
        outs, loss_ref = refs[3 * k : 7 * k], refs[7 * k]
        g, first = p_ref[0], f_ref[0]
        for n in range(1, N_DEV):
            g = g + p_ref[n]
            first = first + f_ref[n]
        g = jnp.concatenate([g[:8] + first, g[8:]], axis=0)
        off = 0
        for i, (_, rows) in enumerate(_SMALL):
            gi = unpacked(g[off : off + rows], w_refs[i].shape)
            off += rows
            outs[i][...] = gi
            outs[k + i][...], outs[2 * k + i][...], outs[3 * k + i][...] = _adamw(w_refs[i][...], gi, m_refs[i][...], v_refs[i][...])
        loss_ref[...] = g[off : off + 1, 0:1]

    out = pl.pallas_call(
        body,
        name="small_update",
        out_shape=[jax.ShapeDtypeStruct(w.shape, F32) for _ in range(4) for w in ws] + [jax.ShapeDtypeStruct((1, 1), F32)],
        compiler_params=pltpu.CompilerParams(vmem_limit_bytes=VMEM_LIMIT),
    )(parts, first_rows, *ws, *ms, *vs)
    return [out[a * k : (a + 1) * k] for a in range(4)], out[4 * k]


_SHARD_AXIS = (1, 1, 1, 0, 0, 0, 0)
_TRANSPOSED = (False, False, False, False, True, True, False)


def _full_from_gathered(t, axis):
    if axis == 0:
        return t.reshape(N_DEV * t.shape[1], t.shape[2])
    return t


_SMALL = (("norm1_g", 8), ("norm2_g", 8), ("norm_f_g", 8), ("b_forget", 8), ("pool_scale", 8), ("pool_mix", 512))


def _pack_small(vals, loss_row):
    parts = []
    for (name, rows), t in zip(_SMALL, vals):
        f = t.astype(F32).reshape(-1)
        f = jnp.concatenate([f, jnp.zeros((rows * LANES - f.shape[0],), F32)]).reshape(rows, LANES)
        parts.append(f)
    parts.append(loss_row)
    return jnp.concatenate(parts, axis=0)


def _local_grads(x, tgt, g1, g2, gf, b_forget, pool_mix, pool_scale, w_in, fwd_token, out_weights, ffn_weights, ffn_grads_out, out_grads_out, small_grads_out, in_grads_out, norm1_grad_out):
    n_seq, S, _ = x.shape
    T = n_seq * S
    x2 = x.reshape(T, D_MODEL)
    tg2 = tgt.reshape(T, D_MODEL)
    w_uqkv, w_fl, w_g = w_in
    b_pad = jnp.concatenate([b_forget.reshape(1, N_HEADS), jnp.zeros((1, FL_PAD - N_HEADS), F32)], axis=1)
    mix_b = pool_mix.reshape(len(POOL_WINDOWS), GROUP_DIM, GROUP_DIM).astype(BF16)
    scale = pool_scale.reshape(1, POOL_WIDTH)
    g1 = g1.reshape(1, D_MODEL)
    g2 = g2.reshape(1, D_MODEL)
    gf = gf.reshape(1, D_MODEL)

    h, u, qkv, fl, gates = _in_proj(x2, g1, w_uqkv, w_fl, w_g, fwd_token)
    fcol = _forget_fwd(fl, b_pad, n_seq, S)
    pm, p2, p3 = _pool_fwd(u, mix_b, scale, n_seq, S)
    a, lse = _attn_fwd(qkv, fcol, n_seq, S)
    w_po, w_ao, w_out = out_weights(a)
    merged, x1, attn_y, pool_y = _mix_out(a, p3, gates, x2, w_ao, w_po, w_out)
    w_gate_t, w_up_t, w_down = ffn_weights(x1)
    h2, gate, up, act, dx2, loss_rows, dgf = _ffn_fwd(x1, g2, gf, tg2, w_gate_t, w_up_t, w_down)

    dgate, dup, dx1, dg2 = _ffn_bwd(dx2, gate, up, x1, g2, w_gate_t, w_up_t, w_down)
    bwd_token = ffn_grads_out(_matmul_tn(dgate, h2, "dw_ffn_gate"), _matmul_tn(dup, h2, "dw_ffn_up"), _matmul_tn(act, dx2, "dw_ffn_down"))
    dgates, dpy, day, da, dp2, dscale = _mix_bwd(dx1, gates, pool_y, attn_y, p2, scale, w_out, w_ao, w_po, bwd_token)
    out_token = out_grads_out(
        _matmul_tn(p3, dpy, "dw_pool_out", col_chunks=True), _matmul_tn(a, day, "dw_attn_out", col_chunks=True), _matmul_tn(merged, dx1, "dw_out")
    )
    du, dmix = _pool_bwd(dp2, pm, mix_b, out_token, n_seq, S)
    dq, dk, dv, dfk, dfq = _attn_bwd(qkv, da, a, fcol, lse, n_seq, S)
    dfl, db = _forget_bwd(dfk, dfq, fl, b_pad, n_seq, S)
    small_token = small_grads_out((jnp.zeros_like(g1), dg2, dgf, db[:, :N_HEADS], dscale, dmix), loss_rows)
    in_token = in_grads_out(_dw_in(h, du, dq, dk, dv, dfl, dgates, small_token))
    dx, dg1 = _in_proj_bwd(du, dq, dk, dv, dfl, dgates, x2, dx1, g1, w_uqkv, w_fl, w_g, in_token)
    norm1_grad_out(dg1)
    return dx.reshape(n_seq, S, D_MODEL)


def kernel(x, norm1_g, w_in, b_forget, pool_mix, pool_scale, w_pool_out, w_attn_out, w_out, norm2_g, w_ffn_gate, w_ffn_up, w_ffn_down, norm_f_g, loss_target, m_norm1_g, m_w_in, m_b_forget, m_pool_mix, m_pool_scale, m_w_pool_out, m_w_attn_out, m_w_out, m_norm2_g, m_w_ffn_gate, m_w_ffn_up, m_w_ffn_down, m_norm_f_g, v_norm1_g, v_w_in, v_b_forget, v_pool_mix, v_pool_scale, v_w_pool_out, v_w_attn_out, v_w_out, v_norm2_g, v_w_ffn_gate, v_w_ffn_up, v_w_ffn_down, v_norm_f_g):
    names = ("w_in", "w_pool_out", "w_attn_out", "w_out", "w_ffn_gate", "w_ffn_up", "w_ffn_down")
    w_sh = (w_in, w_pool_out, w_attn_out, w_out, w_ffn_gate, w_ffn_up, w_ffn_down)
    m_sh = (m_w_in, m_w_pool_out, m_w_attn_out, m_w_out, m_w_ffn_gate, m_w_ffn_up, m_w_ffn_down)
    v_sh = (v_w_in, v_w_pool_out, v_w_attn_out, v_w_out, v_w_ffn_gate, v_w_ffn_up, v_w_ffn_down)

    cx, cy, cc = _position()
    me = 4 * cx + 2 * cy + cc
    def stored(t, transposed):
        return jnp.transpose(t, (0, 2, 1)) if transposed else t

    w_sh, m_sh, v_sh = ([stored(t, tr) for t, tr in zip(ts, _TRANSPOSED)] for ts in (w_sh, m_sh, v_sh))
    shards = [w[0].astype(BF16) for w in w_sh]
    (gathered_in,) = _all_gather(shards[:1], "w_in_all_gather")
    out_sems = _exchange_start(shards[1:4], gathered_in, "out_weights_gather_start", "gather")
    ffn_sems = _exchange_start(shards[4:], out_sems[3], "ffn_weights_gather_start", "gather")
    no_order = jnp.zeros((8, LANES), F32)

    def gathered_weights(sems, axes, name):
        def wait(after):
            _, lands = _exchange_wait(*sems[:3], after, name, "gather")
            return [_full_from_gathered(t, axis) for t, axis in zip(lands, axes)]

        return wait

    started = {}

    def scatter_grads(key, name):
        def start(*whole_grads):
            chunks = [
                t if axis == 1 else t.reshape(N_DEV, -1, t.shape[1])
                for t, axis in zip(whole_grads, _SHARD_AXIS[key])
            ]
            started[key] = _exchange_start(chunks, no_order, name, "scatter")
            return started[key][3]

        return start

    def gather_small(small, loss_rows):
        started["small"] = _exchange_start([_pack_small(small, loss_rows)], no_order, "small_grads_gather_start", "gather")
        return started["small"][3]

    core = jnp.reshape(cc, (1,)).astype(jnp.int32)
    pos = jnp.stack([cc, 2 * cx + cy]).astype(jnp.int32)

    def reduce_w_in(send_in):
        (got_in,) = _sibling_exchange([send_in])
        pair_in = _pair_sum(send_in, got_in, core, "pair_sum_w_in")
        started["in"] = (send_in, got_in, _exchange_start([pair_in], no_order, "w_in_grads_chips_start", "chips"))
        return started["in"][2][3]

    def gather_norm1(dg1):
        rows = jnp.reshape(dg1, (8, LANES))
        started["norm1"] = _exchange_start([rows], no_order, "norm1_grad_gather_start", "gather")

    ffn, out = slice(4, 7), slice(1, 4)
    grad_x = _local_grads(
        x, loss_target, norm1_g, norm2_g, norm_f_g, b_forget, pool_mix, pool_scale, _w_in_pieces(gathered_in), ffn_sems[3],
        gathered_weights(out_sems, _SHARD_AXIS[out], "out_weights_gather_wait"),
        gathered_weights(ffn_sems, _SHARD_AXIS[ffn], "ffn_weights_gather_wait"),
        scatter_grads(ffn, "ffn_grads_scatter_start"), scatter_grads(out, "out_grads_scatter_start"), gather_small, reduce_w_in, gather_norm1,
    )
    send_in, got_in, chip_sems = started["in"]

    def scattered_updates(key, after, name):
        srcs, lands = _exchange_wait(*started[key][:3], after, name, "scatter")
        return [
            _shard_update_direct(p, s, w, m, v, jnp.reshape(me, (1,)).astype(jnp.int32), "update_" + n)
            for p, s, w, m, v, n in zip(lands, srcs, w_sh[key], m_sh[key], v_sh[key], names[key])
        ]

    updates_out = scattered_updates(out, grad_x, "out_grads_scatter_wait")
    updates_ffn = scattered_updates(ffn, grad_x, "ffn_grads_scatter_wait")

    small_w = (norm1_g, norm2_g, norm_f_g, b_forget, pool_scale, pool_mix)
    small_m = (m_norm1_g, m_norm2_g, m_norm_f_g, m_b_forget, m_pool_scale, m_pool_mix)
    small_v = (v_norm1_g, v_norm2_g, v_norm_f_g, v_b_forget, v_pool_scale, v_pool_mix)
    _, (recv_in,) = _exchange_wait(*chip_sems[:3], updates_ffn[-1][0], "w_in_grads_chips_wait", "chips")
    update_in = _shard_update(send_in, got_in, recv_in, w_in, m_w_in, v_w_in, pos, "update_w_in")

    def gathered_small(key, after, name):
        _, lands = _exchange_wait(*started[key][:3], after, name, "gather")
        return lands[0]

    parts = gathered_small("small", update_in[0], "small_grads_gather_wait")
    first_rows = gathered_small("norm1", parts, "norm1_grad_gather_wait")
    (g_s, d_s, nm_s, nv_s), loss = _small_update(parts, first_rows, small_w, small_m, small_v)
    g_w, d_w, nm_w, nv_w = zip(*(
        [stored(t, tr) for t in u] for u, tr in zip([update_in] + updates_out + updates_ffn, _TRANSPOSED)
    ))
    loss = loss.reshape(())
    (g1, g2, gf, gb, gsc, gmix), (d1, d2, df, db_, dsc, dmx) = g_s, d_s
    (m1, m2, mf, mb, msc, mmx), (v1, v2, vf, vb, vsc, vmx) = nm_s, nv_s

    def ordered(n1, win, b, mix, sc, wpo, wao, wout, n2, wg, wu, wd, nf):
        return (n1, win, b, mix, sc, wpo, wao, wout, n2, wg, wu, wd, nf)

    grads = ordered(g1, g_w[0], gb, gmix, gsc, g_w[1], g_w[2], g_w[3], g2, g_w[4], g_w[5], g_w[6], gf)
    deltas = ordered(d1, d_w[0], db_, dmx, dsc, d_w[1], d_w[2], d_w[3], d2, d_w[4], d_w[5], d_w[6], df)
    new_m = ordered(m1, nm_w[0], mb, mmx, msc, nm_w[1], nm_w[2], nm_w[3], m2, nm_w[4], nm_w[5], nm_w[6], mf)
    new_v = ordered(v1, nv_w[0], vb, vmx, vsc, nv_w[1], nv_w[2], nv_w[3], v2, nv_w[4], nv_w[5], nv_w[6], vf)
    return (loss, grad_x, *grads, *deltas, *new_m, *new_v)
```

```python
import jax
import jax.numpy as jnp
from jax import lax
from jax.experimental import pallas as pl
from jax.experimental.pallas import tpu as pltpu

F32 = jnp.float32
BF16 = jnp.bfloat16
MESH = pl.DeviceIdType.MESH

D_MODEL = 1024
POOL_WINDOWS = (2, 4, 8, 16)
POOL_WIDTH = 512
GROUP_DIM = 128
ATTN_WIDTH = 512
HEAD_DIM = 64
N_HEADS = 8
N_PAIRS = 4
D_FF = 2816
RMS_EPS = 1e-6
N_DEV = 8
LANES = 128
FL_PAD = 128

ADAM_LR = 0.001
ADAM_B1 = 0.9
ADAM_B2 = 0.999
ADAM_EPS = 1e-08
ADAM_WD = 0.01
ADAM_STEP = 10

VMEM_LIMIT = 56 * 1024 * 1024
VMEM_LIMIT_MAX = 60 * 1024 * 1024
ROW_TILE = 512
ATTN_BLOCK = 512
FF_CHUNK = 256
FF_ROW_TILE = 512
DW_TOKENS = 2048


def _mm(a, b):
    return jnp.dot(a, b, preferred_element_type=F32)


def _mm_nt(a, b):
    return lax.dot_general(a, b, (((1,), (1,)), ((), ())), preferred_element_type=F32)


def _mm_tn(a, b):
    return lax.dot_general(a, b, (((0,), (0,)), ((), ())), preferred_element_type=F32)


def _whole_cols(w_ref):
    if len(w_ref.shape) == 2:
        return w_ref[...]
    return jnp.concatenate([w_ref[d] for d in range(w_ref.shape[0])], axis=1)


def _sigmoid(x):
    return 1.0 / (1.0 + jnp.exp(-x))


def _params(sem, vmem=VMEM_LIMIT):
    return pltpu.CompilerParams(dimension_semantics=sem, vmem_limit_bytes=vmem)


def _const_spec(shape):
    nd = len(shape)
    return pl.BlockSpec(shape, lambda *_: (0,) * nd, pipeline_mode=pl.Buffered(1))


def _rms_fwd(x, g):
    r = lax.rsqrt(jnp.mean(x * x, axis=-1, keepdims=True) + RMS_EPS)
    xh = x * r
    return xh * g, xh, r


def _rms_bwd(dy, xh, r, g):
    dxh = dy * g
    dx = r * (dxh - xh * jnp.mean(dxh * xh, axis=-1, keepdims=True))
    return dx, dy * xh


def _in_proj(x, g1, w_uqkv, w_fl, w_g, token):
    T = x.shape[0]
    tm = ROW_TILE

    def body(x_ref, g_ref, wa_ref, wf_ref, wg_ref, token_ref, h_ref, u_ref, qkv_ref, fl_ref, gt_ref):
        h, _, _ = _rms_fwd(x_ref[...], g_ref[...])
        hb = h.astype(BF16)
        h_ref[...] = hb
        z = _mm(hb, wa_ref[...])
        u_ref[...] = z[:, :POOL_WIDTH]
        qkv_ref[...] = z[:, POOL_WIDTH:].astype(BF16)
        fl_ref[...] = _mm(hb, wf_ref[...])
        gt_ref[...] = _mm(hb, wg_ref[...]).astype(BF16)

    row = lambda n: pl.BlockSpec((tm, n), lambda i: (i, 0))
    return pl.pallas_call(
        body,
        name="in_proj",
        grid=(T // tm,),
        in_specs=[row(D_MODEL), _const_spec((1, D_MODEL)), _const_spec(w_uqkv.shape), _const_spec(w_fl.shape), _const_spec(w_g.shape), _HBM],
        out_specs=[row(D_MODEL), row(POOL_WIDTH), row(3 * ATTN_WIDTH), row(FL_PAD), row(2 * D_MODEL)],
        out_shape=[
            jax.ShapeDtypeStruct((T, D_MODEL), BF16),
            jax.ShapeDtypeStruct((T, POOL_WIDTH), F32),
            jax.ShapeDtypeStruct((T, 3 * ATTN_WIDTH), BF16),
            jax.ShapeDtypeStruct((T, FL_PAD), F32),
            jax.ShapeDtypeStruct((T, 2 * D_MODEL), BF16),
        ],
        compiler_params=_params(("parallel",)),
    )(x, g1, w_uqkv, w_fl, w_g, token)


def _log_sigmoid(x):
    return jnp.minimum(x, 0.0) - jnp.log(1.0 + jnp.exp(-jnp.abs(x)))


def _forget_fwd(fl, b_pad, n_seq, S):
    def body(fl_ref, b_ref, fcol_ref):
        lf = _log_sigmoid(fl_ref[...] + b_ref[...])
        t = lf.T
        lane = lax.broadcasted_iota(jnp.int32, t.shape, 1)
        k = 1
        while k < S:
            t = t + jnp.where(lane >= k, pltpu.roll(t, k, 1), 0.0)
            k *= 2
        fcol_ref[...] = t.T

    return pl.pallas_call(
        body,
        name="forget_fwd",
        grid=(n_seq,),
        in_specs=[pl.BlockSpec((S, FL_PAD), lambda s: (s, 0)), _const_spec((1, FL_PAD))],
        out_specs=pl.BlockSpec((S, FL_PAD), lambda s: (s, 0)),
        out_shape=jax.ShapeDtypeStruct((n_seq * S, FL_PAD), F32),
        compiler_params=_params(("parallel",)),
    )(fl, b_pad)


def _window_pick(g, v2, v4, v8, v16):
    return jnp.where(g == 0, v2, jnp.where(g == 1, v4, jnp.where(g == 2, v8, v16)))


def _pool_fwd(u, mix_b, scale, n_seq, S):
    T = n_seq * S

    def body(u_ref, mix_ref, sc_ref, pm_ref, p2_ref, p3_ref):
        g = pl.program_id(1)
        uu = u_ref[...]
        row = lax.broadcasted_iota(jnp.int32, uu.shape, 0)

        def back(a, k):
            return jnp.where(row >= k, pltpu.roll(a, k, 0), 0.0)

        s2 = uu + back(uu, 1)
        s4 = s2 + back(s2, 2)
        s8 = s4 + back(s4, 4)
        s16 = s8 + back(s8, 8)
        w = _window_pick(g, 2.0, 4.0, 8.0, 16.0)
        cnt = jnp.minimum((row + 1).astype(F32), w)
        pm = _window_pick(g, s2, s4, s8, s16) / cnt - uu
        pmb = pm.astype(BF16)
        pm_ref[...] = pmb
        p2 = _mm(pmb, mix_ref[...])
        p2_ref[...] = p2
        p3_ref[...] = (p2 * sc_ref[...]).astype(BF16)

    grp = pl.BlockSpec((S, GROUP_DIM), lambda s, g: (s, g))
    return pl.pallas_call(
        body,
        name="pool_fwd",
        grid=(n_seq, len(POOL_WINDOWS)),
        in_specs=[
            grp,
            pl.BlockSpec((None, GROUP_DIM, GROUP_DIM), lambda s, g: (g, 0, 0)),
            pl.BlockSpec((1, GROUP_DIM), lambda s, g: (0, g)),
        ],
        out_specs=[grp, grp, grp],
        out_shape=[
            jax.ShapeDtypeStruct((T, POOL_WIDTH), BF16),
            jax.ShapeDtypeStruct((T, POOL_WIDTH), F32),
            jax.ShapeDtypeStruct((T, POOL_WIDTH), BF16),
        ],
        compiler_params=_params(("parallel", "parallel")),
    )(u, mix_b, scale)


def _split3(v):
    hi = v.astype(BF16).astype(F32)
    r = v - hi
    mid = r.astype(BF16).astype(F32)
    lo = (r - mid).astype(BF16).astype(F32)
    return hi, mid, lo


def _bias_lanes(v):
    hi, mid, lo = _split3(v)
    lane = lax.broadcasted_iota(jnp.int32, (1, LANES), 1)
    packed = jnp.where(lane < N_HEADS, hi, jnp.where(lane < 2 * N_HEADS, pltpu.roll(mid, N_HEADS, 1), pltpu.roll(lo, 2 * N_HEADS, 1)))
    return jnp.where(lane < 3 * N_HEADS, packed, 0.0).astype(BF16)


def _bias_placement(slot):
    row = lax.broadcasted_iota(jnp.int32, (LANES, N_HEADS * LANES), 0)
    col = lax.broadcasted_iota(jnp.int32, (LANES, N_HEADS * LANES), 1)
    h = col // LANES
    n = col % LANES - jnp.where(h % 2 == 0, HEAD_DIM, 0) - 3 * slot
    return ((n >= 0) & (n < 3) & (row == N_HEADS * n + h)).astype(BF16)


def _augment(xp, h, bias, ones_slot):
    lane = lax.broadcasted_iota(jnp.int32, (1, LANES), 1)
    hh = h % 2
    head = (lane >= HEAD_DIM * hh) & (lane < HEAD_DIM * (hh + 1))
    b = HEAD_DIM * (1 - hh)
    rest = jnp.zeros_like(xp) if bias is None else bias[:, h * LANES : (h + 1) * LANES]
    out = jnp.where(head, xp, rest)
    if ones_slot is not None:
        out = jnp.where((lane >= b + 3 * ones_slot) & (lane < b + 3 * ones_slot + 3), jnp.ones_like(xp), out)
    return out


def _attn_fwd(qkv, fcol, n_seq, S):
    T = n_seq * S
    tb = ATTN_BLOCK
    nq = S // tb
    scale = HEAD_DIM ** -0.5

    def body(q_ref, k_ref, v_ref, fc_ref, o_ref, st_ref, qa_sc, ka_sc, m_sc, l_sc, acc_sc):
        i = pl.program_id(1)
        lane = lax.broadcasted_iota(jnp.int32, (1, LANES), 1)
        low = lane < HEAD_DIM

        @pl.when(i == 0)
        def _():
            place = _bias_placement(1)

            def rows_ka(r, carry):
                r0 = pl.multiple_of(r * tb, tb)
                bias = _mm(_bias_lanes(-fc_ref[pl.ds(r0, tb), :]), place).astype(BF16)
                for h in range(N_HEADS):
                    kp = k_ref[pl.ds(r0, tb), (h // 2) * LANES : (h // 2 + 1) * LANES] * scale
                    ka_sc[h, pl.ds(r0, tb), :] = _augment(kp, h, bias, 0)
                return carry

            lax.fori_loop(0, nq, rows_ka, 0)

        q0 = pl.multiple_of(i * tb, tb)
        bias = _mm(_bias_lanes(fc_ref[pl.ds(q0, tb), :]), _bias_placement(0)).astype(BF16)
        for h in range(N_HEADS):
            qa_sc[h] = _augment(q_ref[:, (h // 2) * LANES : (h // 2 + 1) * LANES], h, bias, 1)
        m_sc[...] = jnp.full(m_sc.shape, -jnp.inf, F32)
        l_sc[...] = jnp.zeros_like(l_sc)
        acc_sc[...] = jnp.zeros_like(acc_sc)
        causal = lax.broadcasted_iota(jnp.int32, (tb, tb), 1) <= lax.broadcasted_iota(jnp.int32, (tb, tb), 0)

        def step(j, masked):
            c0 = pl.multiple_of(j * tb, tb)
            for p in range(N_PAIRS):
                vb = v_ref[pl.ds(c0, tb), p * LANES : (p + 1) * LANES]
                pv, al = [], []
                for hh in range(2):
                    h = 2 * p + hh
                    s = _mm_nt(qa_sc[h], ka_sc[h, pl.ds(c0, tb), :])
                    if masked:
                        s = jnp.where(causal, s, -jnp.inf)
                    m_old = m_sc[h]
                    m_new = jnp.maximum(m_old, jnp.max(s, axis=1, keepdims=True))
                    alpha = jnp.exp(m_old - m_new)
                    pe = jnp.exp(s - jnp.concatenate([m_new] * (tb // LANES), axis=1))
                    l_sc[h] = alpha * l_sc[h] + jnp.sum(pe, axis=1, keepdims=True)
                    m_sc[h] = m_new
                    pv.append(_mm(pe.astype(BF16), vb))
                    al.append(alpha)
                acc_sc[p] = jnp.where(low, al[0], al[1]) * acc_sc[p] + jnp.where(low, pv[0], pv[1])

        def loop_body(j, carry):
            step(j, False)
            return carry

        lax.fori_loop(0, i, loop_body, 0)
        step(i, True)
        st = jnp.zeros((tb, LANES), F32)
        for p in range(N_PAIRS):
            lp = jnp.where(low, l_sc[2 * p], l_sc[2 * p + 1])
            o_ref[:, p * LANES : (p + 1) * LANES] = (acc_sc[p] / lp).astype(BF16)
            for h in (2 * p, 2 * p + 1):
                st = jnp.where(lane == h, m_sc[h] + jnp.log(l_sc[h]), st)
        st_ref[...] = st

    return pl.pallas_call(
        body,
        name="attn_fwd",
        grid=(n_seq, nq),
        in_specs=[
            pl.BlockSpec((tb, ATTN_WIDTH), lambda s, i: (s * nq + i, 0)),
            pl.BlockSpec((S, ATTN_WIDTH), lambda s, i: (s, 1)),
            pl.BlockSpec((S, ATTN_WIDTH), lambda s, i: (s, 2)),
            pl.BlockSpec((S, LANES), lambda s, i: (s, 0)),
        ],
        out_specs=[
            pl.BlockSpec((tb, ATTN_WIDTH), lambda s, i: (s * nq + i, 0)),
            pl.BlockSpec((tb, LANES), lambda s, i: (s * nq + i, 0)),
        ],
        out_shape=[jax.ShapeDtypeStruct((T, ATTN_WIDTH), BF16), jax.ShapeDtypeStruct((T, LANES), F32)],
        scratch_shapes=[
            pltpu.VMEM((N_HEADS, tb, LANES), BF16),
            pltpu.VMEM((N_HEADS, S, LANES), BF16),
            pltpu.VMEM((N_HEADS, tb, LANES), F32),
            pltpu.VMEM((N_HEADS, tb, LANES), F32),
            pltpu.VMEM((N_PAIRS, tb, LANES), F32),
        ],
        compiler_params=_params(("parallel", "arbitrary")),
    )(qkv, qkv, qkv, fcol)


def _mix_out(a, p3, gates, x, w_ao, w_po, w_out):
    T = x.shape[0]
    tm = ROW_TILE

    def body(a_ref, p3_ref, gt_ref, x_ref, wao_ref, wpo_ref, wout_ref, mg_ref, x1_ref, ay_ref, py_ref):
        ay = _mm(a_ref[...], _whole_cols(wao_ref))
        py = _mm(p3_ref[...], _whole_cols(wpo_ref))
        ay_ref[...] = ay.astype(BF16)
        py_ref[...] = py.astype(BF16)
        sp = _sigmoid(gt_ref[:, :D_MODEL].astype(F32))
        sa = _sigmoid(gt_ref[:, D_MODEL:].astype(F32))
        mb = (sp * py + sa * ay).astype(BF16)
        mg_ref[...] = mb
        x1_ref[...] = x_ref[...] + _mm(mb, wout_ref[...])

    row = lambda n: pl.BlockSpec((tm, n), lambda i: (i, 0))
    return pl.pallas_call(
        body,
        name="mix_out",
        grid=(T // tm,),
        in_specs=[
            row(ATTN_WIDTH), row(POOL_WIDTH), row(2 * D_MODEL), row(D_MODEL),
            _const_spec(w_ao.shape), _const_spec(w_po.shape), _const_spec(w_out.shape),
        ],
        out_specs=[row(D_MODEL), row(D_MODEL), row(D_MODEL), row(D_MODEL)],
        out_shape=[
            jax.ShapeDtypeStruct((T, D_MODEL), BF16), jax.ShapeDtypeStruct((T, D_MODEL), F32),
            jax.ShapeDtypeStruct((T, D_MODEL), BF16), jax.ShapeDtypeStruct((T, D_MODEL), BF16),
        ],
        compiler_params=_params(("parallel",)),
    )(a, p3, gates, x, w_ao, w_po, w_out)


def _ffn_fwd(x1, g2, gf, tgt, w_gate_t, w_up_t, w_down):
    T = x1.shape[0]
    tm = min(T, FF_ROW_TILE)
    nt = T // tm
    nc = D_FF // FF_CHUNK

    def body(x1_ref, g2_ref, gf_ref, tg_ref, wg_ref, wu_ref, wd_ref, h2_ref, gate_ref, up_ref, act_ref, dx2_ref, loss_ref, dgf_ref):
        x1v = x1_ref[...]
        h2, _, _ = _rms_fwd(x1v, g2_ref[...])
        h2b = h2.astype(BF16)
        h2_ref[...] = h2b
        for c in range(nc):
            sl = slice(c * FF_CHUNK, (c + 1) * FF_CHUNK)
            gate = _mm_nt(h2b, wg_ref[sl, :])
            up = _mm_nt(h2b, wu_ref[sl, :])
            gate_ref[:, sl] = gate.astype(BF16)
            up_ref[:, sl] = up.astype(BF16)
            act_ref[:, sl] = (gate * _sigmoid(gate) * up).astype(BF16)
        acc = x1v + _mm(act_ref[...], wd_ref[...])
        gfv = gf_ref[...]
        y, xh, r = _rms_fwd(acc, gfv)
        err = y - tg_ref[...]
        part = 0.5 * jnp.sum(jnp.mean(err * err, axis=-1, keepdims=True), axis=0, keepdims=True)
        dx2, dgrow = _rms_bwd(err * (1.0 / D_MODEL), xh, r, gfv)
        dx2_ref[...] = dx2

        @pl.when(pl.program_id(0) == 0)
        def _():
            dgf_ref[...] = jnp.zeros_like(dgf_ref)
            loss_ref[...] = jnp.zeros_like(loss_ref)

        dgf_ref[...] += jnp.sum(dgrow, axis=0, keepdims=True)
        loss_ref[...] += jnp.broadcast_to(part, loss_ref.shape)

    row = lambda n: pl.BlockSpec((tm, n), lambda i: (i, 0))
    return pl.pallas_call(
        body,
        name="ffn_fwd",
        grid=(nt,),
        in_specs=[
            row(D_MODEL), _const_spec((1, D_MODEL)), _const_spec((1, D_MODEL)), row(D_MODEL),
            _const_spec(w_gate_t.shape), _const_spec(w_up_t.shape), _const_spec(w_down.shape),
        ],
        out_specs=[
            row(D_MODEL), row(D_FF), row(D_FF), row(D_FF), row(D_MODEL),
            pl.BlockSpec((8, LANES), lambda i: (0, 0)),
            pl.BlockSpec((1, D_MODEL), lambda i: (0, 0)),
        ],
        out_shape=[
            jax.ShapeDtypeStruct((T, D_MODEL), BF16),
            jax.ShapeDtypeStruct((T, D_FF), BF16),
            jax.ShapeDtypeStruct((T, D_FF), BF16),
            jax.ShapeDtypeStruct((T, D_FF), BF16),
            jax.ShapeDtypeStruct((T, D_MODEL), F32),
            jax.ShapeDtypeStruct((8, LANES), F32),
            jax.ShapeDtypeStruct((1, D_MODEL), F32),
        ],
        compiler_params=_params(("arbitrary",)),
    )(x1, g2, gf, tgt, w_gate_t, w_up_t, w_down)


def _ffn_bwd(dx2, gate, up, x1, g2, w_gate_t, w_up_t, w_down):
    T = x1.shape[0]
    tm = min(T, FF_ROW_TILE)
    nc = D_FF // FF_CHUNK

    def body(dx2_ref, gate_ref, up_ref, x1_ref, g2_ref, wg_ref, wu_ref, wd_ref, dgate_ref, dup_ref, dx1_ref, dg2_ref):
        dx2v = dx2_ref[...]
        dx2b = dx2v.astype(BF16)
        for c in range(nc):
            sl = slice(c * FF_CHUNK, (c + 1) * FF_CHUNK)
            dact = _mm_nt(dx2b, wd_ref[sl, :])
            gate = gate_ref[:, sl].astype(F32)
            sg = _sigmoid(gate)
            silu = gate * sg
            dgate = (dact * up_ref[:, sl].astype(F32) * (sg * (1.0 + gate * (1.0 - sg)))).astype(BF16)
            dup = (dact * silu).astype(BF16)
            dgate_ref[:, sl] = dgate
            dup_ref[:, sl] = dup
        dh2 = _mm(dgate_ref[...], wg_ref[...]) + _mm(dup_ref[...], wu_ref[...])
        g2v = g2_ref[...]
        _, xh, r = _rms_fwd(x1_ref[...], g2v)
        dxn, dgrow = _rms_bwd(dh2, xh, r, g2v)
        dx1_ref[...] = dx2v + dxn

        @pl.when(pl.program_id(0) == 0)
        def _():
            dg2_ref[...] = jnp.zeros_like(dg2_ref)

        dg2_ref[...] += jnp.sum(dgrow, axis=0, keepdims=True)

    row = lambda n: pl.BlockSpec((tm, n), lambda i: (i, 0))
    return pl.pallas_call(
        body,
        name="ffn_bwd",
        grid=(T // tm,),
        in_specs=[
            row(D_MODEL), row(D_FF), row(D_FF), row(D_MODEL), _const_spec((1, D_MODEL)),
            _const_spec(w_gate_t.shape), _const_spec(w_up_t.shape), _const_spec(w_down.shape),
        ],
        out_specs=[row(D_FF), row(D_FF), row(D_MODEL), pl.BlockSpec((1, D_MODEL), lambda i: (0, 0))],
        out_shape=[
            jax.ShapeDtypeStruct((T, D_FF), BF16),
            jax.ShapeDtypeStruct((T, D_FF), BF16),
            jax.ShapeDtypeStruct((T, D_MODEL), F32),
            jax.ShapeDtypeStruct((1, D_MODEL), F32),
        ],
        compiler_params=_params(("arbitrary",), VMEM_LIMIT_MAX),
    )(dx2, gate, up, x1, g2, w_gate_t, w_up_t, w_down)


def _mix_bwd(dx1, gates, pool_y, attn_y, p2, scale, w_out, w_ao, w_po, token):
    T = dx1.shape[0]
    tm = ROW_TILE

    def body(dx1_ref, gt_ref, py_ref, ay_ref, p2_ref, sc_ref, wout_ref, wao_ref, wpo_ref, token_ref, dgt_ref, dpy_ref, day_ref, da_ref, dp2_ref, dsc_ref):
        dm = _mm_nt(dx1_ref[...].astype(BF16), wout_ref[...])
        sp = _sigmoid(gt_ref[:, :D_MODEL].astype(F32))
        sa = _sigmoid(gt_ref[:, D_MODEL:].astype(F32))
        dgt_ref[:, :D_MODEL] = (dm * py_ref[...].astype(F32) * (sp * (1.0 - sp))).astype(BF16)
        dgt_ref[:, D_MODEL:] = (dm * ay_ref[...].astype(F32) * (sa * (1.0 - sa))).astype(BF16)
        dpy = (dm * sp).astype(BF16)
        day = (dm * sa).astype(BF16)
        dpy_ref[...] = dpy
        day_ref[...] = day
        da_ref[...] = _mm_nt(day, _whole_cols(wao_ref)).astype(BF16)
        dp3 = _mm_nt(dpy, _whole_cols(wpo_ref))
        dp2_ref[...] = (dp3 * sc_ref[...]).astype(BF16)

        @pl.when(pl.program_id(0) == 0)
        def _():
            dsc_ref[...] = jnp.zeros_like(dsc_ref)

        dsc_ref[...] += jnp.sum(dp3 * p2_ref[...], axis=0, keepdims=True)

    row = lambda n: pl.BlockSpec((tm, n), lambda i: (i, 0))
    return pl.pallas_call(
        body,
        name="mix_bwd",
        grid=(T // tm,),
        in_specs=[
            row(D_MODEL), row(2 * D_MODEL), row(D_MODEL), row(D_MODEL), row(POOL_WIDTH), _const_spec((1, POOL_WIDTH)),
            _const_spec(w_out.shape), _const_spec(w_ao.shape), _const_spec(w_po.shape), _HBM,
        ],
        out_specs=[row(2 * D_MODEL), row(D_MODEL), row(D_MODEL), row(ATTN_WIDTH), row(POOL_WIDTH), pl.BlockSpec((1, POOL_WIDTH), lambda i: (0, 0))],
        out_shape=[
            jax.ShapeDtypeStruct((T, 2 * D_MODEL), BF16),
            jax.ShapeDtypeStruct((T, D_MODEL), BF16),
            jax.ShapeDtypeStruct((T, D_MODEL), BF16),
            jax.ShapeDtypeStruct((T, ATTN_WIDTH), BF16),
            jax.ShapeDtypeStruct((T, POOL_WIDTH), BF16),
            jax.ShapeDtypeStruct((1, POOL_WIDTH), F32),
        ],
        compiler_params=_params(("arbitrary",)),
    )(dx1, gates, pool_y, attn_y, p2, scale, w_out, w_ao, w_po, token)


def _pool_bwd(dp2, pm, mix_b, token, n_seq, S):
    T = n_seq * S

    def body(dp2_ref, pm_ref, mix_ref, token_ref, du_ref, dmix_ref):
        g = pl.program_id(0)
        dp2v = dp2_ref[...]
        dpm = _mm_nt(dp2v, mix_ref[...])
        row = lax.broadcasted_iota(jnp.int32, dpm.shape, 0)
        w = _window_pick(g, 2.0, 4.0, 8.0, 16.0)
        e = dpm / jnp.minimum((row + 1).astype(F32), w)

        def ahead(a, k):
            return jnp.where(row < S - k, pltpu.roll(a, S - k, 0), 0.0)

        r2 = e + ahead(e, 1)
        r4 = r2 + ahead(r2, 2)
        r8 = r4 + ahead(r4, 4)
        r16 = r8 + ahead(r8, 8)
        du_ref[...] = (_window_pick(g, r2, r4, r8, r16) - dpm).astype(BF16)

        @pl.when(pl.program_id(1) == 0)
        def _():
            dmix_ref[...] = jnp.zeros_like(dmix_ref)

        dmix_ref[...] += _mm_tn(pm_ref[...], dp2v)

    grp = pl.BlockSpec((S, GROUP_DIM), lambda g, s: (s, g))
    mixs = pl.BlockSpec((None, GROUP_DIM, GROUP_DIM), lambda g, s: (g, 0, 0))
    return pl.pallas_call(
        body,
        name="pool_bwd",
        grid=(len(POOL_WINDOWS), n_seq),
        in_specs=[grp, grp, mixs, _HBM],
        out_specs=[grp, mixs],
        out_shape=[jax.ShapeDtypeStruct((T, POOL_WIDTH), BF16), jax.ShapeDtypeStruct((len(POOL_WINDOWS), GROUP_DIM, GROUP_DIM), F32)],
        compiler_params=_params(("parallel", "arbitrary")),
    )(dp2, pm, mix_b, token)


def _attn_bwd(qkv, da, a, fcol, lse, n_seq, S):
    T = n_seq * S
    tb = ATTN_BLOCK
    nb = S // tb
    scale = HEAD_DIM ** -0.5

    def body(q_ref, k_ref, v_ref, do_ref, o_ref, fc_ref, st_ref, dq_ref, dk_ref, dv_ref, dfk_ref, dfq_ref,
             qa_sc, doa_sc, qat_sc, doat_sc, dq_acc, ka_sc, va_sc, dkt_sc, dvt_sc):
        j = pl.program_id(1)
        lane = lax.broadcasted_iota(jnp.int32, (1, LANES), 1)
        low = lane < HEAD_DIM

        @pl.when(j == 0)
        def _():
            dq_acc[...] = jnp.zeros_like(dq_acc)
            place = _bias_placement(0)

            def rows_q(i, carry):
                r0 = pl.multiple_of(i * tb, tb)
                delta = jnp.zeros((tb, LANES), F32)
                for h in range(N_HEADS):
                    pair = slice((h // 2) * LANES, (h // 2 + 1) * LANES)
                    prod = do_ref[pl.ds(r0, tb), pair].astype(F32) * o_ref[pl.ds(r0, tb), pair].astype(F32)
                    head = (lane >= HEAD_DIM * (h % 2)) & (lane < HEAD_DIM * (h % 2 + 1))
                    delta = jnp.where(lane == h, jnp.sum(jnp.where(head, prod, 0.0), axis=1, keepdims=True), delta)
                cq = fc_ref[pl.ds(r0, tb), :] - st_ref[pl.ds(r0, tb), :]
                q_bias = _mm(_bias_lanes(cq), place).astype(BF16)
                do_bias = _mm(_bias_lanes(-delta), place).astype(BF16)
                for h in range(N_HEADS):
                    pair = slice((h // 2) * LANES, (h // 2 + 1) * LANES)
                    qa = _augment(q_ref[pl.ds(r0, tb), pair], h, q_bias, 1)
                    doa = _augment(do_ref[pl.ds(r0, tb), pair], h, do_bias, None)
                    qa_sc[h, pl.ds(r0, tb), :] = qa
                    doa_sc[h, pl.ds(r0, tb), :] = doa
                    qat_sc[h, i] = qa.astype(F32).T.astype(BF16)
                    doat_sc[h, i] = doa.astype(F32).T.astype(BF16)
                return carry

            lax.fori_loop(0, nb, rows_q, 0)

        c0 = pl.multiple_of(j * tb, tb)
        k_bias = _mm(_bias_lanes(-fc_ref[pl.ds(c0, tb), :]), _bias_placement(1)).astype(BF16)
        for h in range(N_HEADS):
            pair = slice((h // 2) * LANES, (h // 2 + 1) * LANES)
            ka_sc[h] = _augment(k_ref[:, pair] * scale, h, k_bias, 0)
            va_sc[h] = _augment(v_ref[:, pair], h, None, 0)
        dkt_sc[...] = jnp.zeros_like(dkt_sc)
        dvt_sc[...] = jnp.zeros_like(dvt_sc)
        causal = lax.broadcasted_iota(jnp.int32, (tb, tb), 1) <= lax.broadcasted_iota(jnp.int32, (tb, tb), 0)

        def step(i, masked):
            r0 = pl.multiple_of(i * tb, tb)
            for h in range(N_HEADS):
                s = _mm_nt(qa_sc[h, pl.ds(r0, tb), :], ka_sc[h])
                if masked:
                    s = jnp.where(causal, s, -jnp.inf)
                pr = jnp.exp(s)
                dvt_sc[h] += _mm(doat_sc[h, i], pr.astype(BF16))
                dsb = (pr * _mm_nt(doa_sc[h, pl.ds(r0, tb), :], va_sc[h])).astype(BF16)
                dkt_sc[h] += _mm(qat_sc[h, i], dsb)
                dq_acc[h, pl.ds(r0, tb), :] += _mm(dsb, ka_sc[h])

        step(j, True)

        def loop_body(i, carry):
            step(i, False)
            return carry

        lax.fori_loop(j + 1, nb, loop_body, 0)
        dfk = jnp.zeros((tb, LANES), F32)
        for p in range(N_PAIRS):
            dk = [dkt_sc[2 * p + hh].T for hh in range(2)]
            dv = [dvt_sc[2 * p + hh].T for hh in range(2)]
            dk_ref[:, p * LANES : (p + 1) * LANES] = (jnp.where(low, dk[0], dk[1]) * scale).astype(BF16)
            dv_ref[:, p * LANES : (p + 1) * LANES] = jnp.where(low, dv[0], dv[1]).astype(BF16)
            for hh in range(2):
                b = HEAD_DIM * (1 - hh) + 3
                dfk = jnp.where(lane == 2 * p + hh, -dk[hh][:, b : b + 1], dfk)
        dfk_ref[...] = dfk

        @pl.when(j == nb - 1)
        def _():
            def rows_dq(i, carry):
                r0 = pl.multiple_of(i * tb, tb)
                dfq = jnp.zeros((tb, LANES), F32)
                for p in range(N_PAIRS):
                    parts = [dq_acc[2 * p + hh, pl.ds(r0, tb), :] for hh in range(2)]
                    dq_ref[pl.ds(r0, tb), p * LANES : (p + 1) * LANES] = jnp.where(low, parts[0], parts[1]).astype(BF16)
                    for hh in range(2):
                        b = HEAD_DIM * (1 - hh)
                        dfq = jnp.where(lane == 2 * p + hh, parts[hh][:, b : b + 1], dfq)
                dfq_ref[pl.ds(r0, tb), :] = dfq
                return carry

            lax.fori_loop(0, nb, rows_dq, 0)

    seq = lambda w, col: pl.BlockSpec((S, w), lambda s, j: (s, col))
    seq_in = lambda w, col: pl.BlockSpec((S, w), lambda s, j: (s, col), pipeline_mode=pl.Buffered(1))
    blk = lambda w, col: pl.BlockSpec((tb, w), lambda s, j: (s * nb + j, col))
    return pl.pallas_call(
        body,
        name="attn_bwd",
        grid=(n_seq, nb),
        in_specs=[seq_in(ATTN_WIDTH, 0), blk(ATTN_WIDTH, 1), blk(ATTN_WIDTH, 2), seq_in(ATTN_WIDTH, 0), seq_in(ATTN_WIDTH, 0), seq_in(LANES, 0), seq_in(LANES, 0)],
        out_specs=[seq(ATTN_WIDTH, 0), blk(ATTN_WIDTH, 0), blk(ATTN_WIDTH, 0), blk(LANES, 0), seq(LANES, 0)],
        out_shape=[
            jax.ShapeDtypeStruct((T, ATTN_WIDTH), BF16),
            jax.ShapeDtypeStruct((T, ATTN_WIDTH), BF16),
            jax.ShapeDtypeStruct((T, ATTN_WIDTH), BF16),
            jax.ShapeDtypeStruct((T, LANES), F32),
            jax.ShapeDtypeStruct((T, LANES), F32),
        ],
        scratch_shapes=[
            pltpu.VMEM((N_HEADS, S, LANES), BF16),
            pltpu.VMEM((N_HEADS, S, LANES), BF16),
            pltpu.VMEM((N_HEADS, nb, LANES, tb), BF16),
            pltpu.VMEM((N_HEADS, nb, LANES, tb), BF16),
            pltpu.VMEM((N_HEADS, S, LANES), F32),
            pltpu.VMEM((N_HEADS, tb, LANES), BF16),
            pltpu.VMEM((N_HEADS, tb, LANES), BF16),
            pltpu.VMEM((N_HEADS, LANES, tb), F32),
            pltpu.VMEM((N_HEADS, LANES, tb), F32),
        ],
        compiler_params=_params(("parallel", "arbitrary"), VMEM_LIMIT_MAX),
    )(qkv, qkv, qkv, da, a, fcol, lse)


def _forget_bwd(dfk, dfq, fl, b_pad, n_seq, S):
    def body(df_ref, dfq_ref, fl_ref, b_ref, dfl_ref, db_ref):
        t = (df_ref[...] + dfq_ref[...]).T
        lane = lax.broadcasted_iota(jnp.int32, t.shape, 1)
        k = 1
        while k < S:
            t = t + jnp.where(lane < S - k, pltpu.roll(t, S - k, 1), 0.0)
            k *= 2
        dfl = t.T * _sigmoid(-(fl_ref[...] + b_ref[...]))
        dfl_ref[...] = dfl.astype(BF16)

        @pl.when(pl.program_id(0) == 0)
        def _():
            db_ref[...] = jnp.zeros_like(db_ref)

        db_ref[...] += jnp.sum(dfl, axis=0, keepdims=True)

    return pl.pallas_call(
        body,
        name="forget_bwd",
        grid=(n_seq,),
        in_specs=[
            pl.BlockSpec((S, LANES), lambda s: (s, 0)),
            pl.BlockSpec((S, LANES), lambda s: (s, 0)),
            pl.BlockSpec((S, FL_PAD), lambda s: (s, 0)),
            _const_spec((1, FL_PAD)),
        ],
        out_specs=[pl.BlockSpec((S, FL_PAD), lambda s: (s, 0)), pl.BlockSpec((1, FL_PAD), lambda s: (0, 0))],
        out_shape=[jax.ShapeDtypeStruct((n_seq * S, FL_PAD), BF16), jax.ShapeDtypeStruct((1, FL_PAD), F32)],
        compiler_params=_params(("arbitrary",)),
    )(dfk, dfq, fl, b_pad)


def _in_proj_bwd(du, dq, dk, dv, dfl, dgates, x, dx1, g1, w_uqkv, w_fl, w_g, token):
    T = x.shape[0]
    tm = ROW_TILE

    def body(du_ref, dq_ref, dk_ref, dv_ref, dfl_ref, dgt_ref, x_ref, dx1_ref, g_ref, wa_ref, wf_ref, wg_ref, token_ref, dx_ref, dg_ref):
        dz = jnp.concatenate([du_ref[...], dq_ref[...], dk_ref[...], dv_ref[...]], axis=1)
        dh = _mm_nt(dz, wa_ref[...]) + _mm_nt(dgt_ref[...], wg_ref[...]) + _mm_nt(dfl_ref[...], wf_ref[...])
        gv = g_ref[...]
        _, xh, r = _rms_fwd(x_ref[...], gv)
        dxn, dgrow = _rms_bwd(dh, xh, r, gv)
        dx_ref[...] = dx1_ref[...] + dxn

        @pl.when(pl.program_id(0) == 0)
        def _():
            dg_ref[...] = jnp.zeros_like(dg_ref)

        dg_ref[...] += jnp.sum(dgrow, axis=0, keepdims=True)

    row = lambda n: pl.BlockSpec((tm, n), lambda i: (i, 0))
    return pl.pallas_call(
        body,
        name="in_proj_bwd",
        grid=(T // tm,),
        in_specs=[
            row(512), row(512), row(512), row(512), row(FL_PAD), row(2 * D_MODEL), row(D_MODEL), row(D_MODEL), _const_spec((1, D_MODEL)),
            _const_spec(w_uqkv.shape), _const_spec(w_fl.shape), _const_spec(w_g.shape), _HBM,
        ],
        out_specs=[row(D_MODEL), pl.BlockSpec((1, D_MODEL), lambda i: (0, 0))],
        out_shape=[jax.ShapeDtypeStruct((T, D_MODEL), F32), jax.ShapeDtypeStruct((1, D_MODEL), F32)],
        compiler_params=_params(("arbitrary",)),
    )(du, dq, dk, dv, dfl, dgates, x, dx1, g1, w_uqkv, w_fl, w_g, token)


def _pick_block(n):
    for b in (1024, 512, 1408, 256, 128):
        if n % b == 0:
            return b
    raise ValueError(n)


def _matmul_tn(a, b, name, col_chunks=False):
    T, K = a.shape
    N = b.shape[1]
    bt, bk, bn = min(T, DW_TOKENS), _pick_block(K), _pick_block(N)
    nt = T // bt
    c = N // N_DEV
    assert not col_chunks or (bn == N and c % LANES == 0)

    def body(a_ref, b_ref, o_ref, acc):
        @pl.when(pl.program_id(2) == 0)
        def _():
            acc[...] = jnp.zeros_like(acc)

        acc[...] += _mm_tn(a_ref[...].astype(BF16), b_ref[...].astype(BF16))

        @pl.when(pl.program_id(2) == nt - 1)
        def _():
            if col_chunks:
                for d in range(N_DEV):
                    o_ref[d] = acc[:, d * c : (d + 1) * c].astype(BF16)
            else:
                o_ref[...] = acc[...].astype(BF16)

    if col_chunks:
        out_spec, out_shape = pl.BlockSpec((N_DEV, bk, c), lambda k, n, t: (0, k, 0)), (N_DEV, K, c)
    else:
        out_spec, out_shape = pl.BlockSpec((bk, bn), lambda k, n, t: (k, n)), (K, N)
    return pl.pallas_call(
        body,
        name=name,
        grid=(K // bk, N // bn, nt),
        in_specs=[pl.BlockSpec((bt, bk), lambda k, n, t: (t, k)), pl.BlockSpec((bt, bn), lambda k, n, t: (t, n))],
        out_specs=out_spec,
        out_shape=jax.ShapeDtypeStruct(out_shape, BF16),
        scratch_shapes=[pltpu.VMEM((bk, bn), F32)],
        compiler_params=_params(("parallel", "parallel", "arbitrary")),
    )(a, b)


W_IN_A = POOL_WIDTH + 3 * ATTN_WIDTH
W_IN_SHARD = (W_IN_A + N_HEADS + 2 * D_MODEL) // N_DEV
_W_IN_PIECES = ((0, W_IN_A), (W_IN_A, W_IN_A + N_HEADS), (W_IN_A + N_HEADS, W_IN_A + N_HEADS + 2 * D_MODEL))


def _w_in_segments(d):
    lo, hi = d * W_IN_SHARD, (d + 1) * W_IN_SHARD
    out = []
    for p, (a, b) in enumerate(_W_IN_PIECES):
        s, e = max(lo, a), min(hi, b)
        if s < e:
            out.append((p, s - a, s - lo, e - s))
    return out


def _w_in_pieces(gathered):
    tm = ROW_TILE // 2

    def body(g_ref, wa_ref, wf_ref, wg_ref):
        outs = (wa_ref, wf_ref, wg_ref)
        wf_ref[...] = jnp.zeros_like(wf_ref)
        for d in range(N_DEV):
            for p, at, frm, n in _w_in_segments(d):
                outs[p][:, at : at + n] = g_ref[d, :, frm : frm + n]

    return pl.pallas_call(
        body,
        name="w_in_pieces",
        grid=(D_MODEL // tm,),
        in_specs=[pl.BlockSpec((N_DEV, tm, W_IN_SHARD), lambda i: (0, i, 0))],
        out_specs=[pl.BlockSpec((tm, W_IN_A), lambda i: (i, 0)), pl.BlockSpec((tm, FL_PAD), lambda i: (i, 0)), pl.BlockSpec((tm, 2 * D_MODEL), lambda i: (i, 0))],
        out_shape=[
            jax.ShapeDtypeStruct((D_MODEL, W_IN_A), gathered.dtype),
            jax.ShapeDtypeStruct((D_MODEL, FL_PAD), gathered.dtype),
            jax.ShapeDtypeStruct((D_MODEL, 2 * D_MODEL), gathered.dtype),
        ],
        compiler_params=_params(("parallel",)),
    )(gathered)


def _dw_in(h, du, dq, dk, dv, dfl, dgates, token):
    T = h.shape[0]
    bt, bk = min(T, DW_TOKENS // 2), 512
    nt = T // bt
    pieces = (du, dq, dk, dv, dfl, dgates)
    offs = [0]
    for p in pieces:
        offs.append(offs[-1] + p.shape[1])

    def body(h_ref, *rest):
        refs, o_ref, acc = rest[: len(pieces)], rest[-2], rest[-1]

        @pl.when(pl.program_id(1) == 0)
        def _():
            acc[...] = jnp.zeros_like(acc)

        ht = h_ref[...].T
        for ref, at in zip(refs, offs):
            acc[:, at : at + ref.shape[1]] += _mm(ht, ref[...])

        @pl.when(pl.program_id(1) == nt - 1)
        def _():
            starts = (0, W_IN_A, W_IN_A + FL_PAD)
            for d in range(N_DEV):
                for p, at, to, n in _w_in_segments(d):
                    o_ref[d % 2, d // 2, :, to : to + n] = acc[:, starts[p] + at : starts[p] + at + n].astype(BF16)

    return pl.pallas_call(
        body,
        name="dw_in",
        grid=(D_MODEL // bk, nt),
        in_specs=[pl.BlockSpec((bt, bk), lambda k, t: (t, k))] + [pl.BlockSpec((bt, p.shape[1]), lambda k, t: (t, 0)) for p in pieces] + [_HBM],
        out_specs=pl.BlockSpec((2, 4, bk, W_IN_SHARD), lambda k, t: (0, 0, k, 0)),
        out_shape=jax.ShapeDtypeStruct((2, 4, D_MODEL, W_IN_SHARD), BF16),
        scratch_shapes=[pltpu.VMEM((bk, offs[-1]), F32)],
        compiler_params=_params(("parallel", "arbitrary")),
    )(h, *pieces, token)


def _position():
    return lax.axis_index("x"), lax.axis_index("y"), lax.axis_index("c")


_HBM = pl.BlockSpec(memory_space=pl.ANY)


def _all_gather(blocks, name):
    n = len(blocks)

    def body(*refs):
        xs, outs = refs[:n], refs[n : 2 * n]
        send_sems, recv_sems, local_sems = refs[2 * n :]
        x, y, c = _position()
        me, sibling = (x, y, c), (x, y, 1 - c)
        chips = [(1 - x, y), (x, 1 - y), (1 - x, 1 - y)]

        def rows(a, px, py, pc):
            return outs[a].at[4 * px + 2 * py + pc]

        def copy(a, k, blk, to, src=None):
            return pltpu.make_async_remote_copy(
                src_ref=rows(a, *blk) if src is None else src, dst_ref=rows(a, *blk),
                send_sem=send_sems.at[7 * a + k], recv_sem=recv_sems.at[7 * a + k], device_id=to, device_id_type=MESH,
            )

        mine = [pltpu.make_async_copy(xs[a], rows(a, *me), local_sems.at[a]) for a in range(n)]
        for cp in mine:
            cp.start()
        first = []
        for a in range(n):
            first.append(copy(a, 0, me, sibling, src=xs[a]))
            first += [copy(a, 1 + j, me, (*chip, c), src=xs[a]) for j, chip in enumerate(chips)]
        for cp in first:
            cp.start()
        passed = []
        for j, chip in enumerate(chips):
            for a in range(n):
                copy(a, 1 + j, (*chip, c), me).wait_recv()
                passed.append(copy(a, 4 + j, (*chip, c), sibling))
                passed[-1].start()
        for a in range(n):
            copy(a, 0, sibling, me).wait_recv()
        for j, chip in enumerate(chips):
            for a in range(n):
                copy(a, 4 + j, (*chip, 1 - c), me).wait_recv()
        for cp in first + passed:
            cp.wait_send()
        for cp in mine:
            cp.wait()

    return pl.pallas_call(
        body,
        name=name,
        out_shape=[jax.ShapeDtypeStruct((N_DEV, *b.shape), b.dtype) for b in blocks],
        in_specs=[_HBM] * n,
        out_specs=[_HBM] * n,
        scratch_shapes=[pltpu.SemaphoreType.DMA((7 * n,)), pltpu.SemaphoreType.DMA((7 * n,)), pltpu.SemaphoreType.DMA((n,))],
    )(*blocks)


_SEM = pl.BlockSpec(memory_space=pltpu.SEMAPHORE)
_HBM_ONLY = pl.BlockSpec(memory_space=pltpu.HBM)
_SIDE_EFFECT = pltpu.SideEffectType.DATAFLOW_SIDE_EFFECTING


def _peer(x, y, c, k):
    return (1 - x if k & 4 else x, 1 - y if k & 2 else y, 1 - c if k & 1 else c)


_PEER_BITS = {"gather": range(1, N_DEV), "gather_half": (1, 4, 2, 6), "forward": (4, 2, 6), "scatter": range(1, N_DEV), "chips": (4, 2, 6)}
_LAND_SLOTS = {"gather": N_DEV, "gather_half": N_DEV, "scatter": N_DEV, "chips": 3}
_GATHERS = ("gather", "gather_half")


def _exchange_copies(src_refs, land_refs, send_sems, recv_sems, pattern, receive_side):
    x, y, c = _position()
    me = 4 * x + 2 * y + c
    bits = _PEER_BITS[pattern]
    cps = []
    for j, k in enumerate(bits):
        px, py, pc = _peer(x, y, c, k)
        peer = 4 * px + 2 * py + pc
        for a, (src, land) in enumerate(zip(src_refs, land_refs)):
            to = (px, py, pc)
            if pattern == "chips":
                s, slot = src.at[2 * px + py], j
            elif pattern == "forward":
                slot = 4 * px + 2 * py + (1 - c if receive_side else c)
                s, to = land.at[slot], (x, y, 1 - c)
            else:
                s, slot = (src if pattern in _GATHERS else src.at[peer]), (peer if receive_side else me)
            cps.append(pltpu.make_async_remote_copy(
                src_ref=s, dst_ref=land.at[slot],
                send_sem=send_sems.at[len(bits) * a + j], recv_sem=recv_sems.at[len(bits) * a + j],
                device_id=to, device_id_type=MESH,
            ))
    return cps


def _own_copies(src_refs, land_refs, own_sems):
    x, y, c = _position()
    return [
        pltpu.make_async_copy(src, land.at[4 * x + 2 * y + c], own_sems.at[a])
        for a, (src, land) in enumerate(zip(src_refs, land_refs))
    ]


def _exchange_start(srcs, after, name, pattern):
    n = len(srcs)
    m = len(_PEER_BITS[pattern])
    lands = [jax.ShapeDtypeStruct((_LAND_SLOTS[pattern], *s.shape[-2:]), s.dtype) for s in srcs]

    def body(*refs):
        src_refs, land_refs = refs[1 : 1 + n], refs[1 + n : 1 + 2 * n]
        send_sems, recv_sems, own_sems = refs[1 + 2 * n : 4 + 2 * n]
        token = refs[-1]
        if pattern in _GATHERS:
            for cp in _own_copies(src_refs, land_refs, own_sems):
                cp.start()
        for cp in _exchange_copies(src_refs, land_refs, send_sems, recv_sems, pattern, receive_side=False):
            cp.start()
        token[...] = jnp.zeros_like(token)

    hbm = lambda t: pltpu.with_memory_space_constraint(t, pltpu.HBM)
    out = pl.pallas_call(
        body,
        name=name,
        out_shape=(
            pltpu.SemaphoreType.DMA((m * n,)), pltpu.SemaphoreType.DMA((m * n,)), pltpu.SemaphoreType.DMA((n,)),
            *[pltpu.HBM(s.shape, s.dtype) for s in srcs], *[pltpu.HBM(l.shape, l.dtype) for l in lands],
            jax.ShapeDtypeStruct((8, LANES), F32),
        ),
        in_specs=(_HBM, *[_HBM_ONLY] * (2 * n)),
        out_specs=(_SEM, _SEM, _SEM, *[_HBM_ONLY] * (2 * n), pl.BlockSpec(memory_space=pltpu.VMEM)),
        input_output_aliases={1 + i: 3 + i for i in range(2 * n)},
        compiler_params=pltpu.CompilerParams(has_side_effects=_SIDE_EFFECT),
    )(after, *[hbm(s) for s in srcs], *[hbm(lax.empty(l.shape, l.dtype)) for l in lands])
    return out[:3], out[3 : 3 + n], out[3 + n : 3 + 2 * n], out[-1]


def _exchange_wait(sems, srcs, lands, after, name, pattern):
    n = len(srcs)

    def body(*refs):
        src_refs, land_refs = refs[:n], refs[n : 2 * n]
        send_sems, recv_sems, own_sems = refs[2 * n : 2 * n + 3]
        if pattern in _GATHERS:
            for cp in _own_copies(src_refs, land_refs, own_sems):
                cp.wait()
        for cp in _exchange_copies(src_refs, land_refs, send_sems, recv_sems, pattern, receive_side=True):
            cp.wait_send()
            cp.wait_recv()

    out = pl.pallas_call(
        body,
        name=name,
        out_shape=(*[pltpu.HBM(s.shape, s.dtype) for s in srcs], *[pltpu.HBM(l.shape, l.dtype) for l in lands]),
        in_specs=(*[_HBM_ONLY] * (2 * n), _SEM, _SEM, _SEM, _HBM),
        out_specs=tuple([_HBM_ONLY] * (2 * n)),
        input_output_aliases={i: i for i in range(2 * n)},
        compiler_params=pltpu.CompilerParams(has_side_effects=_SIDE_EFFECT),
    )(*srcs, *lands, *sems, after)
    return out[:n], out[n:]


def _gather_forward(sems, srcs, lands, after, name):
    n = len(srcs)
    m = len(_PEER_BITS["forward"])

    def body(*refs):
        src_refs, land_refs = refs[:n], refs[n : 2 * n]
        send_sems, recv_sems, own_sems = refs[2 * n : 2 * n + 3]
        forward_send, forward_recv, token = refs[2 * n + 4], refs[2 * n + 5], refs[-1]
        for cp in _own_copies(src_refs, land_refs, own_sems):
            cp.wait()
        for cp in _exchange_copies(src_refs, land_refs, send_sems, recv_sems, "gather_half", receive_side=True):
            cp.wait_send()
            cp.wait_recv()
        for cp in _exchange_copies(land_refs, land_refs, forward_send, forward_recv, "forward", receive_side=False):
            cp.start()
        token[...] = jnp.zeros_like(token)

    out = pl.pallas_call(
        body,
        name=name,
        out_shape=(
            pltpu.SemaphoreType.DMA((m * n,)), pltpu.SemaphoreType.DMA((m * n,)),
            *[pltpu.HBM(l.shape, l.dtype) for l in lands], jax.ShapeDtypeStruct((8, LANES), F32),
        ),
        in_specs=(*[_HBM_ONLY] * (2 * n), _SEM, _SEM, _SEM, _HBM),
        out_specs=(_SEM, _SEM, *[_HBM_ONLY] * n, pl.BlockSpec(memory_space=pltpu.VMEM)),
        input_output_aliases={n + i: 2 + i for i in range(n)},
        compiler_params=pltpu.CompilerParams(has_side_effects=_SIDE_EFFECT),
    )(*srcs, *lands, *sems, after)
    return out[:2], out[2 : 2 + n], out[-1]


def _forward_wait(sems, lands, after, name):
    n = len(lands)

    def body(*refs):
        land_refs = refs[:n]
        for cp in _exchange_copies(land_refs, land_refs, refs[n], refs[n + 1], "forward", receive_side=True):
            cp.wait_send()
            cp.wait_recv()

    return pl.pallas_call(
        body,
        name=name,
        out_shape=tuple(pltpu.HBM(l.shape, l.dtype) for l in lands),
        in_specs=(*[_HBM_ONLY] * n, _SEM, _SEM, _HBM),
        out_specs=tuple([_HBM_ONLY] * n),
        input_output_aliases={i: i for i in range(n)},
        compiler_params=pltpu.CompilerParams(has_side_effects=_SIDE_EFFECT),
    )(*lands, *sems, after)


def _sibling_exchange(sends):
    n = len(sends)

    def body(*refs):
        srcs, dsts = refs[:n], refs[n : 2 * n]
        send_sems, recv_sems = refs[2 * n :]
        x, y, c = _position()
        cps = [
            pltpu.make_async_remote_copy(
                src_ref=srcs[a].at[1 - c], dst_ref=dsts[a], send_sem=send_sems.at[a], recv_sem=recv_sems.at[a],
                device_id=(x, y, 1 - c), device_id_type=MESH,
            )
            for a in range(n)
        ]
        for cp in cps:
            cp.start()
        for cp in cps:
            cp.wait()

    return pl.pallas_call(
        body,
        name="rs_sibling",
        out_shape=[jax.ShapeDtypeStruct(s.shape[1:], s.dtype) for s in sends],
        in_specs=[_HBM] * n,
        out_specs=[_HBM] * n,
        scratch_shapes=[pltpu.SemaphoreType.DMA((n,)), pltpu.SemaphoreType.DMA((n,))],
    )(*sends)


def _rows_tile(r):
    return ROW_TILE if r % ROW_TILE == 0 else r


def _pair_sum(send, got, core, name):
    _, _, r, c = send.shape
    br = _rows_tile(r)

    def body(core_ref, a_ref, b_ref, o_ref):
        o_ref[...] = (a_ref[...].astype(F32) + b_ref[...].astype(F32)).astype(o_ref.dtype)

    return pl.pallas_call(
        body,
        name=name,
        grid_spec=pltpu.PrefetchScalarGridSpec(
            num_scalar_prefetch=1,
            grid=(4, r // br),
            in_specs=[
                pl.BlockSpec((None, None, br, c), lambda n, i, core: (core[0], n, i, 0)),
                pl.BlockSpec((None, br, c), lambda n, i, core: (n, i, 0)),
            ],
            out_specs=pl.BlockSpec((None, br, c), lambda n, i, core: (n, i, 0)),
        ),
        out_shape=jax.ShapeDtypeStruct((4, r, c), send.dtype),
        compiler_params=_params(("parallel", "parallel")),
    )(core, send, got)


def _adamw(w, g, m, v):
    m = ADAM_B1 * m + (1.0 - ADAM_B1) * g
    v = ADAM_B2 * v + (1.0 - ADAM_B2) * (g * g)
    m_hat = m / (1.0 - ADAM_B1 ** ADAM_STEP)
    v_hat = v / (1.0 - ADAM_B2 ** ADAM_STEP)
    delta = -ADAM_LR * (m_hat / (jnp.sqrt(v_hat) + ADAM_EPS) + ADAM_WD * w)
    return delta, m, v


def _shard_update(send, got, recv, w, m, v, pos, name):
    _, r, c = w.shape
    br = _rows_tile(r)

    def body(pos_ref, a_ref, b_ref, r_ref, w_ref, m_ref, v_ref, g_ref, d_ref, nm_ref, nv_ref):
        g = a_ref[...].astype(F32) + b_ref[...].astype(F32)
        for n in range(3):
            g = g + r_ref[n].astype(F32)
        g_ref[...] = g
        d_ref[...], nm_ref[...], nv_ref[...] = _adamw(w_ref[...], g, m_ref[...], v_ref[...])

    own = pl.BlockSpec((None, br, c), lambda i, pos: (0, i, 0))
    return pl.pallas_call(
        body,
        name=name,
        grid_spec=pltpu.PrefetchScalarGridSpec(
            num_scalar_prefetch=1,
            grid=(r // br,),
            in_specs=[
                pl.BlockSpec((None, None, br, c), lambda i, pos: (pos[0], pos[1], i, 0)),
                pl.BlockSpec((None, br, c), lambda i, pos: (pos[1], i, 0)),
                pl.BlockSpec((3, br, c), lambda i, pos: (0, i, 0)),
                own, own, own,
            ],
            out_specs=[own, own, own, own],
        ),
        out_shape=[jax.ShapeDtypeStruct((1, r, c), F32)] * 4,
        compiler_params=_params(("parallel",)),
    )(pos, send, got, recv, w, m, v)


def _shard_update_direct(parts, chunks, w, m, v, me, name):
    _, r, c = w.shape
    br = _rows_tile(r)

    def body(me_ref, p_ref, own_ref, w_ref, m_ref, v_ref, g_ref, d_ref, nm_ref, nv_ref):
        g = None
        for n in range(N_DEV):
            part = jnp.where(me_ref[0] == n, own_ref[...], p_ref[n]).astype(F32)
            g = part if g is None else g + part
        g_ref[...] = g
        d_ref[...], nm_ref[...], nv_ref[...] = _adamw(w_ref[...], g, m_ref[...], v_ref[...])

    shard = pl.BlockSpec((None, br, c), lambda i, me: (0, i, 0))
    return pl.pallas_call(
        body,
        name=name,
        grid_spec=pltpu.PrefetchScalarGridSpec(
            num_scalar_prefetch=1,
            grid=(r // br,),
            in_specs=[
                pl.BlockSpec((N_DEV, br, c), lambda i, me: (0, i, 0)),
                pl.BlockSpec((None, br, c), lambda i, me: (me[0], i, 0)),
                shard, shard, shard,
            ],
            out_specs=[shard, shard, shard, shard],
        ),
        out_shape=[jax.ShapeDtypeStruct((1, r, c), F32)] * 4,
        compiler_params=_params(("parallel",)),
    )(me, parts, chunks, w, m, v)


def _small_update(parts, first_rows, ws, ms, vs):
    k = len(ws)

    def unpacked(rows, shape):
        if len(shape) == 2 and shape[1] <= LANES:
            return rows[0:1, : shape[1]]
        if len(shape) == 2:
            return jnp.concatenate([rows[r : r + 1] for r in range(shape[1] // LANES)], axis=1)
        return rows.reshape(shape)

    def body(p_ref, f_ref, *refs):
        w_refs, m_refs, v_refs = refs[:k], refs[k : 2 * k], refs[2 * k : 3 * k]
        outs, loss_ref = refs[3 * k : 7 * k], refs[7 * k]
        g, first = p_ref[0], f_ref[0]
        for n in range(1, N_DEV):
            g = g + p_ref[n]
            first = first + f_ref[n]
        g = jnp.concatenate([g[:8] + first, g[8:]], axis=0)
        off = 0
        for i, (_, rows) in enumerate(_SMALL):
            gi = unpacked(g[off : off + rows], w_refs[i].shape)
            off += rows
            outs[i][...] = gi
            outs[k + i][...], outs[2 * k + i][...], outs[3 * k + i][...] = _adamw(w_refs[i][...], gi, m_refs[i][...], v_refs[i][...])
        loss_ref[...] = g[off : off + 1, 0:1]

    out = pl.pallas_call(
        body,
        name="small_update",
        out_shape=[jax.ShapeDtypeStruct(w.shape, F32) for _ in range(4) for w in ws] + [jax.ShapeDtypeStruct((1, 1), F32)],
        compiler_params=pltpu.CompilerParams(vmem_limit_bytes=VMEM_LIMIT),
    )(parts, first_rows, *ws, *ms, *vs)
    return [out[a * k : (a + 1) * k] for a in range(4)], out[4 * k]


_SHARD_AXIS = (1, 1, 1, 0, 0, 0, 0)
_TRANSPOSED = (False, False, False, False, True, True, False)


def _full_from_gathered(t, axis):
    if axis == 0:
        return t.reshape(N_DEV * t.shape[1], t.shape[2])
    return t


_SMALL = (("norm1_g", 8), ("norm2_g", 8), ("norm_f_g", 8), ("b_forget", 8), ("pool_scale", 8), ("pool_mix", 512))


def _pack_small(vals, loss_row):
    parts = []
    for (name, rows), t in zip(_SMALL, vals):
        f = t.astype(F32).reshape(-1)
        f = jnp.concatenate([f, jnp.zeros((rows * LANES - f.shape[0],), F32)]).reshape(rows, LANES)
        parts.append(f)
    parts.append(loss_row)
    return jnp.concatenate(parts, axis=0)


def _local_grads(x, tgt, g1, g2, gf, b_forget, pool_mix, pool_scale, w_in, fwd_token, out_weights, ffn_weights, ffn_grads_out, out_grads_out, small_grads_out, in_grads_out, norm1_grad_out):
    n_seq, S, _ = x.shape
    T = n_seq * S
    x2 = x.reshape(T, D_MODEL)
    tg2 = tgt.reshape(T, D_MODEL)
    w_uqkv, w_fl, w_g = w_in
    b_pad = jnp.concatenate([b_forget.reshape(1, N_HEADS), jnp.zeros((1, FL_PAD - N_HEADS), F32)], axis=1)
    mix_b = pool_mix.reshape(len(POOL_WINDOWS), GROUP_DIM, GROUP_DIM).astype(BF16)
    scale = pool_scale.reshape(1, POOL_WIDTH)
    g1 = g1.reshape(1, D_MODEL)
    g2 = g2.reshape(1, D_MODEL)
    gf = gf.reshape(1, D_MODEL)

    h, u, qkv, fl, gates = _in_proj(x2, g1, w_uqkv, w_fl, w_g, fwd_token)
    fcol = _forget_fwd(fl, b_pad, n_seq, S)
    pm, p2, p3 = _pool_fwd(u, mix_b, scale, n_seq, S)
    a, lse = _attn_fwd(qkv, fcol, n_seq, S)
    w_po, w_ao, w_out = out_weights(a)
    merged, x1, attn_y, pool_y = _mix_out(a, p3, gates, x2, w_ao, w_po, w_out)
    w_gate_t, w_up_t, w_down = ffn_weights(x1)
    h2, gate, up, act, dx2, loss_rows, dgf = _ffn_fwd(x1, g2, gf, tg2, w_gate_t, w_up_t, w_down)

    dgate, dup, dx1, dg2 = _ffn_bwd(dx2, gate, up, x1, g2, w_gate_t, w_up_t, w_down)
    bwd_token = ffn_grads_out(_matmul_tn(dgate, h2, "dw_ffn_gate"), _matmul_tn(dup, h2, "dw_ffn_up"), _matmul_tn(act, dx2, "dw_ffn_down"))
    dgates, dpy, day, da, dp2, dscale = _mix_bwd(dx1, gates, pool_y, attn_y, p2, scale, w_out, w_ao, w_po, bwd_token)
    out_token = out_grads_out(
        _matmul_tn(p3, dpy, "dw_pool_out", col_chunks=True), _matmul_tn(a, day, "dw_attn_out", col_chunks=True), _matmul_tn(merged, dx1, "dw_out")
    )
    du, dmix = _pool_bwd(dp2, pm, mix_b, out_token, n_seq, S)
    dq, dk, dv, dfk, dfq = _attn_bwd(qkv, da, a, fcol, lse, n_seq, S)
    dfl, db = _forget_bwd(dfk, dfq, fl, b_pad, n_seq, S)
    small_token = small_grads_out((jnp.zeros_like(g1), dg2, dgf, db[:, :N_HEADS], dscale, dmix), loss_rows)
    in_token = in_grads_out(_dw_in(h, du, dq, dk, dv, dfl, dgates, small_token))
    dx, dg1 = _in_proj_bwd(du, dq, dk, dv, dfl, dgates, x2, dx1, g1, w_uqkv, w_fl, w_g, in_token)
    norm1_grad_out(dg1)
    return dx.reshape(n_seq, S, D_MODEL)


def kernel(x, norm1_g, w_in, b_forget, pool_mix, pool_scale, w_pool_out, w_attn_out, w_out, norm2_g, w_ffn_gate, w_ffn_up, w_ffn_down, norm_f_g, loss_target, m_norm1_g, m_w_in, m_b_forget, m_pool_mix, m_pool_scale, m_w_pool_out, m_w_attn_out, m_w_out, m_norm2_g, m_w_ffn_gate, m_w_ffn_up, m_w_ffn_down, m_norm_f_g, v_norm1_g, v_w_in, v_b_forget, v_pool_mix, v_pool_scale, v_w_pool_out, v_w_attn_out, v_w_out, v_norm2_g, v_w_ffn_gate, v_w_ffn_up, v_w_ffn_down, v_norm_f_g):
    names = ("w_in", "w_pool_out", "w_attn_out", "w_out", "w_ffn_gate", "w_ffn_up", "w_ffn_down")
    w_sh = (w_in, w_pool_out, w_attn_out, w_out, w_ffn_gate, w_ffn_up, w_ffn_down)
    m_sh = (m_w_in, m_w_pool_out, m_w_attn_out, m_w_out, m_w_ffn_gate, m_w_ffn_up, m_w_ffn_down)
    v_sh = (v_w_in, v_w_pool_out, v_w_attn_out, v_w_out, v_w_ffn_gate, v_w_ffn_up, v_w_ffn_down)

    cx, cy, cc = _position()
    me = 4 * cx + 2 * cy + cc
    def stored(t, transposed):
        return jnp.transpose(t, (0, 2, 1)) if transposed else t

    w_sh, m_sh, v_sh = ([stored(t, tr) for t, tr in zip(ts, _TRANSPOSED)] for ts in (w_sh, m_sh, v_sh))
    shards = [w[0].astype(BF16) for w in w_sh]
    (gathered_in,) = _all_gather(shards[:1], "w_in_all_gather")
    out_sems = _exchange_start(shards[1:4], gathered_in, "out_weights_gather_start", "gather")
    ffn_sems = _exchange_start(shards[4:], out_sems[3], "ffn_weights_gather_start", "gather_half")
    no_order = jnp.zeros((8, LANES), F32)
    started = {}

    def out_weights(after):
        forward_sems, lands, token = _gather_forward(*ffn_sems[:3], after, "ffn_weights_forward_start")
        started["forward"] = (forward_sems, lands)
        _, lands = _exchange_wait(*out_sems[:3], token, "out_weights_gather_wait", "gather")
        return [_full_from_gathered(t, axis) for t, axis in zip(lands, _SHARD_AXIS[out])]

    def ffn_weights(after):
        lands = _forward_wait(*started["forward"], after, "ffn_weights_gather_wait")
        return [_full_from_gathered(t, axis) for t, axis in zip(lands, _SHARD_AXIS[ffn])]

    def scatter_grads(key, name):
        def start(*whole_grads):
            chunks = [
                t if axis == 1 else t.reshape(N_DEV, -1, t.shape[1])
                for t, axis in zip(whole_grads, _SHARD_AXIS[key])
            ]
            started[key] = _exchange_start(chunks, no_order, name, "scatter")
            return started[key][3]

        return start

    def gather_small(small, loss_rows):
        started["small"] = _exchange_start([_pack_small(small, loss_rows)], no_order, "small_grads_gather_start", "gather")
        return started["small"][3]

    core = jnp.reshape(cc, (1,)).astype(jnp.int32)
    pos = jnp.stack([cc, 2 * cx + cy]).astype(jnp.int32)

    def reduce_w_in(send_in):
        (got_in,) = _sibling_exchange([send_in])
        pair_in = _pair_sum(send_in, got_in, core, "pair_sum_w_in")
        started["in"] = (send_in, got_in, _exchange_start([pair_in], no_order, "w_in_grads_chips_start", "chips"))
        return started["in"][2][3]

    def gather_norm1(dg1):
        rows = jnp.reshape(dg1, (8, LANES))
        started["norm1"] = _exchange_start([rows], no_order, "norm1_grad_gather_start", "gather")

    ffn, out = slice(4, 7), slice(1, 4)
    grad_x = _local_grads(
        x, loss_target, norm1_g, norm2_g, norm_f_g, b_forget, pool_mix, pool_scale, _w_in_pieces(gathered_in), ffn_sems[3],
        out_weights, ffn_weights,
        scatter_grads(ffn, "ffn_grads_scatter_start"), scatter_grads(out, "out_grads_scatter_start"), gather_small, reduce_w_in, gather_norm1,
    )
    send_in, got_in, chip_sems = started["in"]

    def scattered_updates(key, after, name):
        srcs, lands = _exchange_wait(*started[key][:3], after, name, "scatter")
        return [
            _shard_update_direct(p, s, w, m, v, jnp.reshape(me, (1,)).astype(jnp.int32), "update_" + n)
            for p, s, w, m, v, n in zip(lands, srcs, w_sh[key], m_sh[key], v_sh[key], names[key])
        ]

    updates_out = scattered_updates(out, grad_x, "out_grads_scatter_wait")
    updates_ffn = scattered_updates(ffn, grad_x, "ffn_grads_scatter_wait")

    small_w = (norm1_g, norm2_g, norm_f_g, b_forget, pool_scale, pool_mix)
    small_m = (m_norm1_g, m_norm2_g, m_norm_f_g, m_b_forget, m_pool_scale, m_pool_mix)
    small_v = (v_norm1_g, v_norm2_g, v_norm_f_g, v_b_forget, v_pool_scale, v_pool_mix)
    _, (recv_in,) = _exchange_wait(*chip_sems[:3], updates_ffn[-1][0], "w_in_grads_chips_wait", "chips")
    update_in = _shard_update(send_in, got_in, recv_in, w_in, m_w_in, v_w_in, pos, "update_w_in")

    def gathered_small(key, after, name):
        _, lands = _exchange_wait(*started[key][:3], after, name, "gather")
        return lands[0]

    parts = gathered_small("small", update_in[0], "small_grads_gather_wait")
    first_rows = gathered_small("norm1", parts, "norm1_grad_gather_wait")
    (g_s, d_s, nm_s, nv_s), loss = _small_update(parts, first_rows, small_w, small_m, small_v)
    g_w, d_w, nm_w, nv_w = zip(*(
        [stored(t, tr) for t in u] for u, tr in zip([update_in] + updates_out + updates_ffn, _TRANSPOSED)
    ))
    loss = loss.reshape(())
    (g1, g2, gf, gb, gsc, gmix), (d1, d2, df, db_, dsc, dmx) = g_s, d_s
    (m1, m2, mf, mb, msc, mmx), (v1, v2, vf, vb, vsc, vmx) = nm_s, nv_s

    def ordered(n1, win, b, mix, sc, wpo, wao, wout, n2, wg, wu, wd, nf):
        return (n1, win, b, mix, sc, wpo, wao, wout, n2, wg, wu, wd, nf)

    grads = ordered(g1, g_w[0], gb, gmix, gsc, g_w[1], g_w[2], g_w[3], g2, g_w[4], g_w[5], g_w[6], gf)
    deltas = ordered(d1, d_w[0], db_, dmx, dsc, d_w[1], d_w[2], d_w[3], d2, d_w[4], d_w[5], d_w[6], df)
    new_m = ordered(m1, nm_w[0], mb, mmx, msc, nm_w[1], nm_w[2], nm_w[3], m2, nm_w[4], nm_w[5], nm_w[6], mf)
    new_v = ordered(v1, nv_w[0], vb, vmx, vsc, nv_w[1], nv_w[2], nv_w[3], v2, nv_w[4], nv_w[5], nv_w[6], vf)
    return (loss, grad_x, *grads, *deltas, *new_m, *new_v)
```

```python
import jax
import jax.numpy as jnp
from jax import lax
from jax.experimental import pallas as pl
from jax.experimental.pallas import tpu as pltpu

F32 = jnp.float32
BF16 = jnp.bfloat16
MESH = pl.DeviceIdType.MESH

D_MODEL = 1024
POOL_WINDOWS = (2, 4, 8, 16)
POOL_WIDTH = 512
GROUP_DIM = 128
ATTN_WIDTH = 512
HEAD_DIM = 64
N_HEADS = 8
N_PAIRS = 4
D_FF = 2816
RMS_EPS = 1e-6
N_DEV = 8
LANES = 128
FL_PAD = 128

ADAM_LR = 0.001
ADAM_B1 = 0.9
ADAM_B2 = 0.999
ADAM_EPS = 1e-08
ADAM_WD = 0.01
ADAM_STEP = 10

VMEM_LIMIT = 56 * 1024 * 1024
VMEM_LIMIT_MAX = 60 * 1024 * 1024
ROW_TILE = 512
ATTN_BLOCK = 512
FF_CHUNK = 256
FF_ROW_TILE = 512
DW_TOKENS = 2048


def _mm(a, b):
    return jnp.dot(a, b, preferred_element_type=F32)


def _mm_nt(a, b):
    return lax.dot_general(a, b, (((1,), (1,)), ((), ())), preferred_element_type=F32)


def _mm_tn(a, b):
    return lax.dot_general(a, b, (((0,), (0,)), ((), ())), preferred_element_type=F32)


def _whole_cols(w_ref):
    if len(w_ref.shape) == 2:
        return w_ref[...]
    return jnp.concatenate([w_ref[d] for d in range(w_ref.shape[0])], axis=1)


def _sigmoid(x):
    return 1.0 / (1.0 + jnp.exp(-x))


def _params(sem, vmem=VMEM_LIMIT):
    return pltpu.CompilerParams(dimension_semantics=sem, vmem_limit_bytes=vmem)


def _const_spec(shape):
    nd = len(shape)
    return pl.BlockSpec(shape, lambda *_: (0,) * nd, pipeline_mode=pl.Buffered(1))


def _rms_fwd(x, g):
    r = lax.rsqrt(jnp.mean(x * x, axis=-1, keepdims=True) + RMS_EPS)
    xh = x * r
    return xh * g, xh, r


def _rms_bwd(dy, xh, r, g):
    dxh = dy * g
    dx = r * (dxh - xh * jnp.mean(dxh * xh, axis=-1, keepdims=True))
    return dx, dy * xh


def _in_proj(x, g1, w_uqkv, w_fl, w_g, token):
    T = x.shape[0]
    tm = ROW_TILE

    def body(x_ref, g_ref, wa_ref, wf_ref, wg_ref, token_ref, h_ref, u_ref, qkv_ref, fl_ref, gt_ref):
        h, _, _ = _rms_fwd(x_ref[...], g_ref[...])
        hb = h.astype(BF16)
        h_ref[...] = hb
        z = _mm(hb, wa_ref[...])
        u_ref[...] = z[:, :POOL_WIDTH]
        qkv_ref[...] = z[:, POOL_WIDTH:].astype(BF16)
        fl_ref[...] = _mm(hb, wf_ref[...])
        gt_ref[...] = _mm(hb, wg_ref[...]).astype(BF16)

    row = lambda n: pl.BlockSpec((tm, n), lambda i: (i, 0))
    return pl.pallas_call(
        body,
        name="in_proj",
        grid=(T // tm,),
        in_specs=[row(D_MODEL), _const_spec((1, D_MODEL)), _const_spec(w_uqkv.shape), _const_spec(w_fl.shape), _const_spec(w_g.shape), _HBM],
        out_specs=[row(D_MODEL), row(POOL_WIDTH), row(3 * ATTN_WIDTH), row(FL_PAD), row(2 * D_MODEL)],
        out_shape=[
            jax.ShapeDtypeStruct((T, D_MODEL), BF16),
            jax.ShapeDtypeStruct((T, POOL_WIDTH), F32),
            jax.ShapeDtypeStruct((T, 3 * ATTN_WIDTH), BF16),
            jax.ShapeDtypeStruct((T, FL_PAD), F32),
            jax.ShapeDtypeStruct((T, 2 * D_MODEL), BF16),
        ],
        compiler_params=_params(("parallel",)),
    )(x, g1, w_uqkv, w_fl, w_g, token)


def _log_sigmoid(x):
    return jnp.minimum(x, 0.0) - jnp.log(1.0 + jnp.exp(-jnp.abs(x)))


def _forget_fwd(fl, b_pad, n_seq, S):
    def body(fl_ref, b_ref, fcol_ref):
        lf = _log_sigmoid(fl_ref[...] + b_ref[...])
        t = lf.T
        lane = lax.broadcasted_iota(jnp.int32, t.shape, 1)
        k = 1
        while k < S:
            t = t + jnp.where(lane >= k, pltpu.roll(t, k, 1), 0.0)
            k *= 2
        fcol_ref[...] = t.T

    return pl.pallas_call(
        body,
        name="forget_fwd",
        grid=(n_seq,),
        in_specs=[pl.BlockSpec((S, FL_PAD), lambda s: (s, 0)), _const_spec((1, FL_PAD))],
        out_specs=pl.BlockSpec((S, FL_PAD), lambda s: (s, 0)),
        out_shape=jax.ShapeDtypeStruct((n_seq * S, FL_PAD), F32),
        compiler_params=_params(("parallel",)),
    )(fl, b_pad)


def _window_pick(g, v2, v4, v8, v16):
    return jnp.where(g == 0, v2, jnp.where(g == 1, v4, jnp.where(g == 2, v8, v16)))


def _pool_fwd(u, mix_b, scale, n_seq, S):
    T = n_seq * S

    def body(u_ref, mix_ref, sc_ref, pm_ref, p2_ref, p3_ref):
        g = pl.program_id(1)
        uu = u_ref[...]
        row = lax.broadcasted_iota(jnp.int32, uu.shape, 0)

        def back(a, k):
            return jnp.where(row >= k, pltpu.roll(a, k, 0), 0.0)

        s2 = uu + back(uu, 1)
        s4 = s2 + back(s2, 2)
        s8 = s4 + back(s4, 4)
        s16 = s8 + back(s8, 8)
        w = _window_pick(g, 2.0, 4.0, 8.0, 16.0)
        cnt = jnp.minimum((row + 1).astype(F32), w)
        pm = _window_pick(g, s2, s4, s8, s16) / cnt - uu
        pmb = pm.astype(BF16)
        pm_ref[...] = pmb
        p2 = _mm(pmb, mix_ref[...])
        p2_ref[...] = p2
        p3_ref[...] = (p2 * sc_ref[...]).astype(BF16)

    grp = pl.BlockSpec((S, GROUP_DIM), lambda s, g: (s, g))
    return pl.pallas_call(
        body,
        name="pool_fwd",
        grid=(n_seq, len(POOL_WINDOWS)),
        in_specs=[
            grp,
            pl.BlockSpec((None, GROUP_DIM, GROUP_DIM), lambda s, g: (g, 0, 0)),
            pl.BlockSpec((1, GROUP_DIM), lambda s, g: (0, g)),
        ],
        out_specs=[grp, grp, grp],
        out_shape=[
            jax.ShapeDtypeStruct((T, POOL_WIDTH), BF16),
            jax.ShapeDtypeStruct((T, POOL_WIDTH), F32),
            jax.ShapeDtypeStruct((T, POOL_WIDTH), BF16),
        ],
        compiler_params=_params(("parallel", "parallel")),
    )(u, mix_b, scale)


def _split3(v):
    hi = v.astype(BF16).astype(F32)
    r = v - hi
    mid = r.astype(BF16).astype(F32)
    lo = (r - mid).astype(BF16).astype(F32)
    return hi, mid, lo


def _bias_lanes(v):
    hi, mid, lo = _split3(v)
    lane = lax.broadcasted_iota(jnp.int32, (1, LANES), 1)
    packed = jnp.where(lane < N_HEADS, hi, jnp.where(lane < 2 * N_HEADS, pltpu.roll(mid, N_HEADS, 1), pltpu.roll(lo, 2 * N_HEADS, 1)))
    return jnp.where(lane < 3 * N_HEADS, packed, 0.0).astype(BF16)


def _bias_placement(slot):
    row = lax.broadcasted_iota(jnp.int32, (LANES, N_HEADS * LANES), 0)
    col = lax.broadcasted_iota(jnp.int32, (LANES, N_HEADS * LANES), 1)
    h = col // LANES
    n = col % LANES - jnp.where(h % 2 == 0, HEAD_DIM, 0) - 3 * slot
    return ((n >= 0) & (n < 3) & (row == N_HEADS * n + h)).astype(BF16)


def _augment(xp, h, bias, ones_slot):
    lane = lax.broadcasted_iota(jnp.int32, (1, LANES), 1)
    hh = h % 2
    head = (lane >= HEAD_DIM * hh) & (lane < HEAD_DIM * (hh + 1))
    b = HEAD_DIM * (1 - hh)
    rest = jnp.zeros_like(xp) if bias is None else bias[:, h * LANES : (h + 1) * LANES]
    out = jnp.where(head, xp, rest)
    if ones_slot is not None:
        out = jnp.where((lane >= b + 3 * ones_slot) & (lane < b + 3 * ones_slot + 3), jnp.ones_like(xp), out)
    return out


def _attn_fwd(qkv, fcol, n_seq, S):
    T = n_seq * S
    tb = ATTN_BLOCK
    nq = S // tb
    scale = HEAD_DIM ** -0.5

    def body(q_ref, k_ref, v_ref, fc_ref, o_ref, st_ref, qa_sc, ka_sc, m_sc, l_sc, acc_sc):
        i = pl.program_id(1)
        lane = lax.broadcasted_iota(jnp.int32, (1, LANES), 1)
        low = lane < HEAD_DIM

        @pl.when(i == 0)
        def _():
            place = _bias_placement(1)

            def rows_ka(r, carry):
                r0 = pl.multiple_of(r * tb, tb)
                bias = _mm(_bias_lanes(-fc_ref[pl.ds(r0, tb), :]), place).astype(BF16)
                for h in range(N_HEADS):
                    kp = k_ref[pl.ds(r0, tb), (h // 2) * LANES : (h // 2 + 1) * LANES] * scale
                    ka_sc[h, pl.ds(r0, tb), :] = _augment(kp, h, bias, 0)
                return carry

            lax.fori_loop(0, nq, rows_ka, 0)

        q0 = pl.multiple_of(i * tb, tb)
        bias = _mm(_bias_lanes(fc_ref[pl.ds(q0, tb), :]), _bias_placement(0)).astype(BF16)
        for h in range(N_HEADS):
            qa_sc[h] = _augment(q_ref[:, (h // 2) * LANES : (h // 2 + 1) * LANES], h, bias, 1)
        m_sc[...] = jnp.full(m_sc.shape, -jnp.inf, F32)
        l_sc[...] = jnp.zeros_like(l_sc)
        acc_sc[...] = jnp.zeros_like(acc_sc)
        causal = lax.broadcasted_iota(jnp.int32, (tb, tb), 1) <= lax.broadcasted_iota(jnp.int32, (tb, tb), 0)

        def step(j, masked):
            c0 = pl.multiple_of(j * tb, tb)
            for p in range(N_PAIRS):
                vb = v_ref[pl.ds(c0, tb), p * LANES : (p + 1) * LANES]
                pv, al = [], []
                for hh in range(2):
                    h = 2 * p + hh
                    s = _mm_nt(qa_sc[h], ka_sc[h, pl.ds(c0, tb), :])
                    if masked:
                        s = jnp.where(causal, s, -jnp.inf)
                    m_old = m_sc[h]
                    m_new = jnp.maximum(m_old, jnp.max(s, axis=1, keepdims=True))
                    alpha = jnp.exp(m_old - m_new)
                    pe = jnp.exp(s - jnp.concatenate([m_new] * (tb // LANES), axis=1))
                    l_sc[h] = alpha * l_sc[h] + jnp.sum(pe, axis=1, keepdims=True)
                    m_sc[h] = m_new
                    pv.append(_mm(pe.astype(BF16), vb))
                    al.append(alpha)
                acc_sc[p] = jnp.where(low, al[0], al[1]) * acc_sc[p] + jnp.where(low, pv[0], pv[1])

        def loop_body(j, carry):
            step(j, False)
            return carry

        lax.fori_loop(0, i, loop_body, 0)
        step(i, True)
        st = jnp.zeros((tb, LANES), F32)
        for p in range(N_PAIRS):
            lp = jnp.where(low, l_sc[2 * p], l_sc[2 * p + 1])
            o_ref[:, p * LANES : (p + 1) * LANES] = (acc_sc[p] / lp).astype(BF16)
            for h in (2 * p, 2 * p + 1):
                st = jnp.where(lane == h, m_sc[h] + jnp.log(l_sc[h]), st)
        st_ref[...] = st

    return pl.pallas_call(
        body,
        name="attn_fwd",
        grid=(n_seq, nq),
        in_specs=[
            pl.BlockSpec((tb, ATTN_WIDTH), lambda s, i: (s * nq + i, 0)),
            pl.BlockSpec((S, ATTN_WIDTH), lambda s, i: (s, 1)),
            pl.BlockSpec((S, ATTN_WIDTH), lambda s, i: (s, 2)),
            pl.BlockSpec((S, LANES), lambda s, i: (s, 0)),
        ],
        out_specs=[
            pl.BlockSpec((tb, ATTN_WIDTH), lambda s, i: (s * nq + i, 0)),
            pl.BlockSpec((tb, LANES), lambda s, i: (s * nq + i, 0)),
        ],
        out_shape=[jax.ShapeDtypeStruct((T, ATTN_WIDTH), BF16), jax.ShapeDtypeStruct((T, LANES), F32)],
        scratch_shapes=[
            pltpu.VMEM((N_HEADS, tb, LANES), BF16),
            pltpu.VMEM((N_HEADS, S, LANES), BF16),
            pltpu.VMEM((N_HEADS, tb, LANES), F32),
            pltpu.VMEM((N_HEADS, tb, LANES), F32),
            pltpu.VMEM((N_PAIRS, tb, LANES), F32),
        ],
        compiler_params=_params(("parallel", "arbitrary")),
    )(qkv, qkv, qkv, fcol)


def _mix_out(a, p3, gates, x, w_ao, w_po, w_out):
    T = x.shape[0]
    tm = ROW_TILE

    def body(a_ref, p3_ref, gt_ref, x_ref, wao_ref, wpo_ref, wout_ref, mg_ref, x1_ref, ay_ref, py_ref):
        ay = _mm(a_ref[...], _whole_cols(wao_ref))
        py = _mm(p3_ref[...], _whole_cols(wpo_ref))
        ay_ref[...] = ay.astype(BF16)
        py_ref[...] = py.astype(BF16)
        sp = _sigmoid(gt_ref[:, :D_MODEL].astype(F32))
        sa = _sigmoid(gt_ref[:, D_MODEL:].astype(F32))
        mb = (sp * py + sa * ay).astype(BF16)
        mg_ref[...] = mb
        x1_ref[...] = x_ref[...] + _mm(mb, wout_ref[...])

    row = lambda n: pl.BlockSpec((tm, n), lambda i: (i, 0))
    return pl.pallas_call(
        body,
        name="mix_out",
        grid=(T // tm,),
        in_specs=[
            row(ATTN_WIDTH), row(POOL_WIDTH), row(2 * D_MODEL), row(D_MODEL),
            _const_spec(w_ao.shape), _const_spec(w_po.shape), _const_spec(w_out.shape),
        ],
        out_specs=[row(D_MODEL), row(D_MODEL), row(D_MODEL), row(D_MODEL)],
        out_shape=[
            jax.ShapeDtypeStruct((T, D_MODEL), BF16), jax.ShapeDtypeStruct((T, D_MODEL), F32),
            jax.ShapeDtypeStruct((T, D_MODEL), BF16), jax.ShapeDtypeStruct((T, D_MODEL), BF16),
        ],
        compiler_params=_params(("parallel",)),
    )(a, p3, gates, x, w_ao, w_po, w_out)


def _ffn_fwd(x1, g2, gf, tgt, w_gate_t, w_up_t, w_down):
    T = x1.shape[0]
    tm = min(T, FF_ROW_TILE)
    nt = T // tm
    nc = D_FF // FF_CHUNK

    def body(x1_ref, g2_ref, gf_ref, tg_ref, wg_ref, wu_ref, wd_ref, h2_ref, gate_ref, up_ref, act_ref, dx2_ref, loss_ref, dgf_ref):
        x1v = x1_ref[...]
        h2, _, _ = _rms_fwd(x1v, g2_ref[...])
        h2b = h2.astype(BF16)
        h2_ref[...] = h2b
        for c in range(nc):
            sl = slice(c * FF_CHUNK, (c + 1) * FF_CHUNK)
            gate = _mm_nt(h2b, wg_ref[sl, :])
            up = _mm_nt(h2b, wu_ref[sl, :])
            gate_ref[:, sl] = gate.astype(BF16)
            up_ref[:, sl] = up.astype(BF16)
            act_ref[:, sl] = (gate * _sigmoid(gate) * up).astype(BF16)
        acc = x1v + _mm(act_ref[...], wd_ref[...])
        gfv = gf_ref[...]
        y, xh, r = _rms_fwd(acc, gfv)
        err = y - tg_ref[...]
        part = 0.5 * jnp.sum(jnp.mean(err * err, axis=-1, keepdims=True), axis=0, keepdims=True)
        dx2, dgrow = _rms_bwd(err * (1.0 / D_MODEL), xh, r, gfv)
        dx2_ref[...] = dx2

        @pl.when(pl.program_id(0) == 0)
        def _():
            dgf_ref[...] = jnp.zeros_like(dgf_ref)
            loss_ref[...] = jnp.zeros_like(loss_ref)

        dgf_ref[...] += jnp.sum(dgrow, axis=0, keepdims=True)
        loss_ref[...] += jnp.broadcast_to(part, loss_ref.shape)

    row = lambda n: pl.BlockSpec((tm, n), lambda i: (i, 0))
    return pl.pallas_call(
        body,
        name="ffn_fwd",
        grid=(nt,),
        in_specs=[
            row(D_MODEL), _const_spec((1, D_MODEL)), _const_spec((1, D_MODEL)), row(D_MODEL),
            _const_spec(w_gate_t.shape), _const_spec(w_up_t.shape), _const_spec(w_down.shape),
        ],
        out_specs=[
            row(D_MODEL), row(D_FF), row(D_FF), row(D_FF), row(D_MODEL),
            pl.BlockSpec((8, LANES), lambda i: (0, 0)),
            pl.BlockSpec((1, D_MODEL), lambda i: (0, 0)),
        ],
        out_shape=[
            jax.ShapeDtypeStruct((T, D_MODEL), BF16),
            jax.ShapeDtypeStruct((T, D_FF), BF16),
            jax.ShapeDtypeStruct((T, D_FF), BF16),
            jax.ShapeDtypeStruct((T, D_FF), BF16),
            jax.ShapeDtypeStruct((T, D_MODEL), F32),
            jax.ShapeDtypeStruct((8, LANES), F32),
            jax.ShapeDtypeStruct((1, D_MODEL), F32),
        ],
        compiler_params=_params(("arbitrary",)),
    )(x1, g2, gf, tgt, w_gate_t, w_up_t, w_down)


def _ffn_bwd(dx2, gate, up, x1, g2, w_gate_t, w_up_t, w_down):
    T = x1.shape[0]
    tm = min(T, FF_ROW_TILE)
    nc = D_FF // FF_CHUNK

    def body(dx2_ref, gate_ref, up_ref, x1_ref, g2_ref, wg_ref, wu_ref, wd_ref, dgate_ref, dup_ref, dx1_ref, dg2_ref):
        dx2v = dx2_ref[...]
        dx2b = dx2v.astype(BF16)
        for c in range(nc):
            sl = slice(c * FF_CHUNK, (c + 1) * FF_CHUNK)
            dact = _mm_nt(dx2b, wd_ref[sl, :])
            gate = gate_ref[:, sl].astype(F32)
            sg = _sigmoid(gate)
            silu = gate * sg
            dgate = (dact * up_ref[:, sl].astype(F32) * (sg * (1.0 + gate * (1.0 - sg)))).astype(BF16)
            dup = (dact * silu).astype(BF16)
            dgate_ref[:, sl] = dgate
            dup_ref[:, sl] = dup
        dh2 = _mm(dgate_ref[...], wg_ref[...]) + _mm(dup_ref[...], wu_ref[...])
        g2v = g2_ref[...]
        _, xh, r = _rms_fwd(x1_ref[...], g2v)
        dxn, dgrow = _rms_bwd(dh2, xh, r, g2v)
        dx1_ref[...] = dx2v + dxn

        @pl.when(pl.program_id(0) == 0)
        def _():
            dg2_ref[...] = jnp.zeros_like(dg2_ref)

        dg2_ref[...] += jnp.sum(dgrow, axis=0, keepdims=True)

    row = lambda n: pl.BlockSpec((tm, n), lambda i: (i, 0))
    return pl.pallas_call(
        body,
        name="ffn_bwd",
        grid=(T // tm,),
        in_specs=[
            row(D_MODEL), row(D_FF), row(D_FF), row(D_MODEL), _const_spec((1, D_MODEL)),
            _const_spec(w_gate_t.shape), _const_spec(w_up_t.shape), _const_spec(w_down.shape),
        ],
        out_specs=[row(D_FF), row(D_FF), row(D_MODEL), pl.BlockSpec((1, D_MODEL), lambda i: (0, 0))],
        out_shape=[
            jax.ShapeDtypeStruct((T, D_FF), BF16),
            jax.ShapeDtypeStruct((T, D_FF), BF16),
            jax.ShapeDtypeStruct((T, D_MODEL), F32),
            jax.ShapeDtypeStruct((1, D_MODEL), F32),
        ],
        compiler_params=_params(("arbitrary",), VMEM_LIMIT_MAX),
    )(dx2, gate, up, x1, g2, w_gate_t, w_up_t, w_down)


def _mix_bwd(dx1, gates, pool_y, attn_y, p2, scale, w_out, w_ao, w_po, token):
    T = dx1.shape[0]
    tm = ROW_TILE

    def body(dx1_ref, gt_ref, py_ref, ay_ref, p2_ref, sc_ref, wout_ref, wao_ref, wpo_ref, token_ref, dgt_ref, dpy_ref, day_ref, da_ref, dp2_ref, dsc_ref):
        dm = _mm_nt(dx1_ref[...].astype(BF16), wout_ref[...])
        sp = _sigmoid(gt_ref[:, :D_MODEL].astype(F32))
        sa = _sigmoid(gt_ref[:, D_MODEL:].astype(F32))
        dgt_ref[:, :D_MODEL] = (dm * py_ref[...].astype(F32) * (sp * (1.0 - sp))).astype(BF16)
        dgt_ref[:, D_MODEL:] = (dm * ay_ref[...].astype(F32) * (sa * (1.0 - sa))).astype(BF16)
        dpy = (dm * sp).astype(BF16)
        day = (dm * sa).astype(BF16)
        dpy_ref[...] = dpy
        day_ref[...] = day
        da_ref[...] = _mm_nt(day, _whole_cols(wao_ref)).astype(BF16)
        dp3 = _mm_nt(dpy, _whole_cols(wpo_ref))
        dp2_ref[...] = (dp3 * sc_ref[...]).astype(BF16)

        @pl.when(pl.program_id(0) == 0)
        def _():
            dsc_ref[...] = jnp.zeros_like(dsc_ref)

        dsc_ref[...] += jnp.sum(dp3 * p2_ref[...], axis=0, keepdims=True)

    row = lambda n: pl.BlockSpec((tm, n), lambda i: (i, 0))
    return pl.pallas_call(
        body,
        name="mix_bwd",
        grid=(T // tm,),
        in_specs=[
            row(D_MODEL), row(2 * D_MODEL), row(D_MODEL), row(D_MODEL), row(POOL_WIDTH), _const_spec((1, POOL_WIDTH)),
            _const_spec(w_out.shape), _const_spec(w_ao.shape), _const_spec(w_po.shape), _HBM,
        ],
        out_specs=[row(2 * D_MODEL), row(D_MODEL), row(D_MODEL), row(ATTN_WIDTH), row(POOL_WIDTH), pl.BlockSpec((1, POOL_WIDTH), lambda i: (0, 0))],
        out_shape=[
            jax.ShapeDtypeStruct((T, 2 * D_MODEL), BF16),
            jax.ShapeDtypeStruct((T, D_MODEL), BF16),
            jax.ShapeDtypeStruct((T, D_MODEL), BF16),
            jax.ShapeDtypeStruct((T, ATTN_WIDTH), BF16),
            jax.ShapeDtypeStruct((T, POOL_WIDTH), BF16),
            jax.ShapeDtypeStruct((1, POOL_WIDTH), F32),
        ],
        compiler_params=_params(("arbitrary",)),
    )(dx1, gates, pool_y, attn_y, p2, scale, w_out, w_ao, w_po, token)


def _pool_bwd(dp2, pm, mix_b, token, n_seq, S):
    T = n_seq * S

    def body(dp2_ref, pm_ref, mix_ref, token_ref, du_ref, dmix_ref):
        g = pl.program_id(0)
        dp2v = dp2_ref[...]
        dpm = _mm_nt(dp2v, mix_ref[...])
        row = lax.broadcasted_iota(jnp.int32, dpm.shape, 0)
        w = _window_pick(g, 2.0, 4.0, 8.0, 16.0)
        e = dpm / jnp.minimum((row + 1).astype(F32), w)

        def ahead(a, k):
            return jnp.where(row < S - k, pltpu.roll(a, S - k, 0), 0.0)

        r2 = e + ahead(e, 1)
        r4 = r2 + ahead(r2, 2)
        r8 = r4 + ahead(r4, 4)
        r16 = r8 + ahead(r8, 8)
        du_ref[...] = (_window_pick(g, r2, r4, r8, r16) - dpm).astype(BF16)

        @pl.when(pl.program_id(1) == 0)
        def _():
            dmix_ref[...] = jnp.zeros_like(dmix_ref)

        dmix_ref[...] += _mm_tn(pm_ref[...], dp2v)

    grp = pl.BlockSpec((S, GROUP_DIM), lambda g, s: (s, g))
    mixs = pl.BlockSpec((None, GROUP_DIM, GROUP_DIM), lambda g, s: (g, 0, 0))
    return pl.pallas_call(
        body,
        name="pool_bwd",
        grid=(len(POOL_WINDOWS), n_seq),
        in_specs=[grp, grp, mixs, _HBM],
        out_specs=[grp, mixs],
        out_shape=[jax.ShapeDtypeStruct((T, POOL_WIDTH), BF16), jax.ShapeDtypeStruct((len(POOL_WINDOWS), GROUP_DIM, GROUP_DIM), F32)],
        compiler_params=_params(("parallel", "arbitrary")),
    )(dp2, pm, mix_b, token)


def _attn_bwd(qkv, da, a, fcol, lse, n_seq, S):
    T = n_seq * S
    tb = ATTN_BLOCK
    nb = S // tb
    scale = HEAD_DIM ** -0.5

    def body(q_ref, k_ref, v_ref, do_ref, o_ref, fc_ref, st_ref, dq_ref, dk_ref, dv_ref, dfk_ref, dfq_ref,
             qa_sc, doa_sc, qat_sc, doat_sc, dq_acc, ka_sc, va_sc, dkt_sc, dvt_sc):
        j = pl.program_id(1)
        lane = lax.broadcasted_iota(jnp.int32, (1, LANES), 1)
        low = lane < HEAD_DIM

        @pl.when(j == 0)
        def _():
            dq_acc[...] = jnp.zeros_like(dq_acc)
            place = _bias_placement(0)

            def rows_q(i, carry):
                r0 = pl.multiple_of(i * tb, tb)
                delta = jnp.zeros((tb, LANES), F32)
                for h in range(N_HEADS):
                    pair = slice((h // 2) * LANES, (h // 2 + 1) * LANES)
                    prod = do_ref[pl.ds(r0, tb), pair].astype(F32) * o_ref[pl.ds(r0, tb), pair].astype(F32)
                    head = (lane >= HEAD_DIM * (h % 2)) & (lane < HEAD_DIM * (h % 2 + 1))
                    delta = jnp.where(lane == h, jnp.sum(jnp.where(head, prod, 0.0), axis=1, keepdims=True), delta)
                cq = fc_ref[pl.ds(r0, tb), :] - st_ref[pl.ds(r0, tb), :]
                q_bias = _mm(_bias_lanes(cq), place).astype(BF16)
                do_bias = _mm(_bias_lanes(-delta), place).astype(BF16)
                for h in range(N_HEADS):
                    pair = slice((h // 2) * LANES, (h // 2 + 1) * LANES)
                    qa = _augment(q_ref[pl.ds(r0, tb), pair], h, q_bias, 1)
                    doa = _augment(do_ref[pl.ds(r0, tb), pair], h, do_bias, None)
                    qa_sc[h, pl.ds(r0, tb), :] = qa
                    doa_sc[h, pl.ds(r0, tb), :] = doa
                    qat_sc[h, i] = qa.astype(F32).T.astype(BF16)
                    doat_sc[h, i] = doa.astype(F32).T.astype(BF16)
                return carry

            lax.fori_loop(0, nb, rows_q, 0)

        c0 = pl.multiple_of(j * tb, tb)
        k_bias = _mm(_bias_lanes(-fc_ref[pl.ds(c0, tb), :]), _bias_placement(1)).astype(BF16)
        for h in range(N_HEADS):
            pair = slice((h // 2) * LANES, (h // 2 + 1) * LANES)
            ka_sc[h] = _augment(k_ref[:, pair] * scale, h, k_bias, 0)
            va_sc[h] = _augment(v_ref[:, pair], h, None, 0)
        dkt_sc[...] = jnp.zeros_like(dkt_sc)
        dvt_sc[...] = jnp.zeros_like(dvt_sc)
        causal = lax.broadcasted_iota(jnp.int32, (tb, tb), 1) <= lax.broadcasted_iota(jnp.int32, (tb, tb), 0)

        def step(i, masked):
            r0 = pl.multiple_of(i * tb, tb)
            for h in range(N_HEADS):
                s = _mm_nt(qa_sc[h, pl.ds(r0, tb), :], ka_sc[h])
                if masked:
                    s = jnp.where(causal, s, -jnp.inf)
                pr = jnp.exp(s)
                dvt_sc[h] += _mm(doat_sc[h, i], pr.astype(BF16))
                dsb = (pr * _mm_nt(doa_sc[h, pl.ds(r0, tb), :], va_sc[h])).astype(BF16)
                dkt_sc[h] += _mm(qat_sc[h, i], dsb)
                dq_acc[h, pl.ds(r0, tb), :] += _mm(dsb, ka_sc[h])

        step(j, True)

        def loop_body(i, carry):
            step(i, False)
            return carry

        lax.fori_loop(j + 1, nb, loop_body, 0)
        dfk = jnp.zeros((tb, LANES), F32)
        for p in range(N_PAIRS):
            dk = [dkt_sc[2 * p + hh].T for hh in range(2)]
            dv = [dvt_sc[2 * p + hh].T for hh in range(2)]
            dk_ref[:, p * LANES : (p + 1) * LANES] = (jnp.where(low, dk[0], dk[1]) * scale).astype(BF16)
            dv_ref[:, p * LANES : (p + 1) * LANES] = jnp.where(low, dv[0], dv[1]).astype(BF16)
            for hh in range(2):
                b = HEAD_DIM * (1 - hh) + 3
                dfk = jnp.where(lane == 2 * p + hh, -dk[hh][:, b : b + 1], dfk)
        dfk_ref[...] = dfk

        @pl.when(j == nb - 1)
        def _():
            def rows_dq(i, carry):
                r0 = pl.multiple_of(i * tb, tb)
                dfq = jnp.zeros((tb, LANES), F32)
                for p in range(N_PAIRS):
                    parts = [dq_acc[2 * p + hh, pl.ds(r0, tb), :] for hh in range(2)]
                    dq_ref[pl.ds(r0, tb), p * LANES : (p + 1) * LANES] = jnp.where(low, parts[0], parts[1]).astype(BF16)
                    for hh in range(2):
                        b = HEAD_DIM * (1 - hh)
                        dfq = jnp.where(lane == 2 * p + hh, parts[hh][:, b : b + 1], dfq)
                dfq_ref[pl.ds(r0, tb), :] = dfq
                return carry

            lax.fori_loop(0, nb, rows_dq, 0)

    seq = lambda w, col: pl.BlockSpec((S, w), lambda s, j: (s, col))
    seq_in = lambda w, col: pl.BlockSpec((S, w), lambda s, j: (s, col), pipeline_mode=pl.Buffered(1))
    blk = lambda w, col: pl.BlockSpec((tb, w), lambda s, j: (s * nb + j, col))
    return pl.pallas_call(
        body,
        name="attn_bwd",
        grid=(n_seq, nb),
        in_specs=[seq_in(ATTN_WIDTH, 0), blk(ATTN_WIDTH, 1), blk(ATTN_WIDTH, 2), seq_in(ATTN_WIDTH, 0), seq_in(ATTN_WIDTH, 0), seq_in(LANES, 0), seq_in(LANES, 0)],
        out_specs=[seq(ATTN_WIDTH, 0), blk(ATTN_WIDTH, 0), blk(ATTN_WIDTH, 0), blk(LANES, 0), seq(LANES, 0)],
        out_shape=[
            jax.ShapeDtypeStruct((T, ATTN_WIDTH), BF16),
            jax.ShapeDtypeStruct((T, ATTN_WIDTH), BF16),
            jax.ShapeDtypeStruct((T, ATTN_WIDTH), BF16),
            jax.ShapeDtypeStruct((T, LANES), F32),
            jax.ShapeDtypeStruct((T, LANES), F32),
        ],
        scratch_shapes=[
            pltpu.VMEM((N_HEADS, S, LANES), BF16),
            pltpu.VMEM((N_HEADS, S, LANES), BF16),
            pltpu.VMEM((N_HEADS, nb, LANES, tb), BF16),
            pltpu.VMEM((N_HEADS, nb, LANES, tb), BF16),
            pltpu.VMEM((N_HEADS, S, LANES), F32),
            pltpu.VMEM((N_HEADS, tb, LANES), BF16),
            pltpu.VMEM((N_HEADS, tb, LANES), BF16),
            pltpu.VMEM((N_HEADS, LANES, tb), F32),
            pltpu.VMEM((N_HEADS, LANES, tb), F32),
        ],
        compiler_params=_params(("parallel", "arbitrary"), VMEM_LIMIT_MAX),
    )(qkv, qkv, qkv, da, a, fcol, lse)


def _forget_bwd(dfk, dfq, fl, b_pad, n_seq, S):
    def body(df_ref, dfq_ref, fl_ref, b_ref, dfl_ref, db_ref):
        t = (df_ref[...] + dfq_ref[...]).T
        lane = lax.broadcasted_iota(jnp.int32, t.shape, 1)
        k = 1
        while k < S:
            t = t + jnp.where(lane < S - k, pltpu.roll(t, S - k, 1), 0.0)
            k *= 2
        dfl = t.T * _sigmoid(-(fl_ref[...] + b_ref[...]))
        dfl_ref[...] = dfl.astype(BF16)

        @pl.when(pl.program_id(0) == 0)
        def _():
            db_ref[...] = jnp.zeros_like(db_ref)

        db_ref[...] += jnp.sum(dfl, axis=0, keepdims=True)

    return pl.pallas_call(
        body,
        name="forget_bwd",
        grid=(n_seq,),
        in_specs=[
            pl.BlockSpec((S, LANES), lambda s: (s, 0)),
            pl.BlockSpec((S, LANES), lambda s: (s, 0)),
            pl.BlockSpec((S, FL_PAD), lambda s: (s, 0)),
            _const_spec((1, FL_PAD)),
        ],
        out_specs=[pl.BlockSpec((S, FL_PAD), lambda s: (s, 0)), pl.BlockSpec((1, FL_PAD), lambda s: (0, 0))],
        out_shape=[jax.ShapeDtypeStruct((n_seq * S, FL_PAD), BF16), jax.ShapeDtypeStruct((1, FL_PAD), F32)],
        compiler_params=_params(("arbitrary",)),
    )(dfk, dfq, fl, b_pad)


def _in_proj_bwd(du, dq, dk, dv, dfl, dgates, x, dx1, g1, w_uqkv, w_fl, w_g, token):
    T = x.shape[0]
    tm = ROW_TILE

    def body(du_ref, dq_ref, dk_ref, dv_ref, dfl_ref, dgt_ref, x_ref, dx1_ref, g_ref, wa_ref, wf_ref, wg_ref, token_ref, dx_ref, dg_ref):
        dz = jnp.concatenate([du_ref[...], dq_ref[...], dk_ref[...], dv_ref[...]], axis=1)
        dh = _mm_nt(dz, wa_ref[...]) + _mm_nt(dgt_ref[...], wg_ref[...]) + _mm_nt(dfl_ref[...], wf_ref[...])
        gv = g_ref[...]
        _, xh, r = _rms_fwd(x_ref[...], gv)
        dxn, dgrow = _rms_bwd(dh, xh, r, gv)
        dx_ref[...] = dx1_ref[...] + dxn

        @pl.when(pl.program_id(0) == 0)
        def _():
            dg_ref[...] = jnp.zeros_like(dg_ref)

        dg_ref[...] += jnp.sum(dgrow, axis=0, keepdims=True)

    row = lambda n: pl.BlockSpec((tm, n), lambda i: (i, 0))
    return pl.pallas_call(
        body,
        name="in_proj_bwd",
        grid=(T // tm,),
        in_specs=[
            row(512), row(512), row(512), row(512), row(FL_PAD), row(2 * D_MODEL), row(D_MODEL), row(D_MODEL), _const_spec((1, D_MODEL)),
            _const_spec(w_uqkv.shape), _const_spec(w_fl.shape), _const_spec(w_g.shape), _HBM,
        ],
        out_specs=[row(D_MODEL), pl.BlockSpec((1, D_MODEL), lambda i: (0, 0))],
        out_shape=[jax.ShapeDtypeStruct((T, D_MODEL), F32), jax.ShapeDtypeStruct((1, D_MODEL), F32)],
        compiler_params=_params(("arbitrary",)),
    )(du, dq, dk, dv, dfl, dgates, x, dx1, g1, w_uqkv, w_fl, w_g, token)


def _pick_block(n):
    for b in (1024, 512, 1408, 256, 128):
        if n % b == 0:
            return b
    raise ValueError(n)


def _matmul_tn(a, b, name, col_chunks=False):
    T, K = a.shape
    N = b.shape[1]
    bt, bk, bn = min(T, DW_TOKENS), _pick_block(K), _pick_block(N)
    nt = T // bt
    c = N // N_DEV
    assert not col_chunks or (bn == N and c % LANES == 0)

    def body(a_ref, b_ref, o_ref, acc):
        @pl.when(pl.program_id(2) == 0)
        def _():
            acc[...] = jnp.zeros_like(acc)

        acc[...] += _mm_tn(a_ref[...].astype(BF16), b_ref[...].astype(BF16))

        @pl.when(pl.program_id(2) == nt - 1)
        def _():
            if col_chunks:
                for d in range(N_DEV):
                    o_ref[d] = acc[:, d * c : (d + 1) * c].astype(BF16)
            else:
                o_ref[...] = acc[...].astype(BF16)

    if col_chunks:
        out_spec, out_shape = pl.BlockSpec((N_DEV, bk, c), lambda k, n, t: (0, k, 0)), (N_DEV, K, c)
    else:
        out_spec, out_shape = pl.BlockSpec((bk, bn), lambda k, n, t: (k, n)), (K, N)
    return pl.pallas_call(
        body,
        name=name,
        grid=(K // bk, N // bn, nt),
        in_specs=[pl.BlockSpec((bt, bk), lambda k, n, t: (t, k)), pl.BlockSpec((bt, bn), lambda k, n, t: (t, n))],
        out_specs=out_spec,
        out_shape=jax.ShapeDtypeStruct(out_shape, BF16),
        scratch_shapes=[pltpu.VMEM((bk, bn), F32)],
        compiler_params=_params(("parallel", "parallel", "arbitrary")),
    )(a, b)


W_IN_A = POOL_WIDTH + 3 * ATTN_WIDTH
W_IN_SHARD = (W_IN_A + N_HEADS + 2 * D_MODEL) // N_DEV
_W_IN_PIECES = ((0, W_IN_A), (W_IN_A, W_IN_A + N_HEADS), (W_IN_A + N_HEADS, W_IN_A + N_HEADS + 2 * D_MODEL))


def _w_in_segments(d):
    lo, hi = d * W_IN_SHARD, (d + 1) * W_IN_SHARD
    out = []
    for p, (a, b) in enumerate(_W_IN_PIECES):
        s, e = max(lo, a), min(hi, b)
        if s < e:
            out.append((p, s - a, s - lo, e - s))
    return out


def _w_in_pieces(gathered):
    tm = ROW_TILE // 2

    def body(g_ref, wa_ref, wf_ref, wg_ref):
        outs = (wa_ref, wf_ref, wg_ref)
        wf_ref[...] = jnp.zeros_like(wf_ref)
        for d in range(N_DEV):
            for p, at, frm, n in _w_in_segments(d):
                outs[p][:, at : at + n] = g_ref[d, :, frm : frm + n]

    return pl.pallas_call(
        body,
        name="w_in_pieces",
        grid=(D_MODEL // tm,),
        in_specs=[pl.BlockSpec((N_DEV, tm, W_IN_SHARD), lambda i: (0, i, 0))],
        out_specs=[pl.BlockSpec((tm, W_IN_A), lambda i: (i, 0)), pl.BlockSpec((tm, FL_PAD), lambda i: (i, 0)), pl.BlockSpec((tm, 2 * D_MODEL), lambda i: (i, 0))],
        out_shape=[
            jax.ShapeDtypeStruct((D_MODEL, W_IN_A), gathered.dtype),
            jax.ShapeDtypeStruct((D_MODEL, FL_PAD), gathered.dtype),
            jax.ShapeDtypeStruct((D_MODEL, 2 * D_MODEL), gathered.dtype),
        ],
        compiler_params=_params(("parallel",)),
    )(gathered)


def _dw_in(h, du, dq, dk, dv, dfl, dgates, token):
    T = h.shape[0]
    bt, bk = min(T, DW_TOKENS // 2), 512
    nt = T // bt
    pieces = (du, dq, dk, dv, dfl, dgates)
    offs = [0]
    for p in pieces:
        offs.append(offs[-1] + p.shape[1])

    def body(h_ref, *rest):
        refs, o_ref, acc = rest[: len(pieces)], rest[-2], rest[-1]

        @pl.when(pl.program_id(1) == 0)
        def _():
            acc[...] = jnp.zeros_like(acc)

        ht = h_ref[...].T
        for ref, at in zip(refs, offs):
            acc[:, at : at + ref.shape[1]] += _mm(ht, ref[...])

        @pl.when(pl.program_id(1) == nt - 1)
        def _():
            starts = (0, W_IN_A, W_IN_A + FL_PAD)
            for d in range(N_DEV):
                for p, at, to, n in _w_in_segments(d):
                    o_ref[d % 2, d // 2, :, to : to + n] = acc[:, starts[p] + at : starts[p] + at + n].astype(BF16)

    return pl.pallas_call(
        body,
        name="dw_in",
        grid=(D_MODEL // bk, nt),
        in_specs=[pl.BlockSpec((bt, bk), lambda k, t: (t, k))] + [pl.BlockSpec((bt, p.shape[1]), lambda k, t: (t, 0)) for p in pieces] + [_HBM],
        out_specs=pl.BlockSpec((2, 4, bk, W_IN_SHARD), lambda k, t: (0, 0, k, 0)),
        out_shape=jax.ShapeDtypeStruct((2, 4, D_MODEL, W_IN_SHARD), BF16),
        scratch_shapes=[pltpu.VMEM((bk, offs[-1]), F32)],
        compiler_params=_params(("parallel", "arbitrary")),
    )(h, *pieces, token)


def _position():
    return lax.axis_index("x"), lax.axis_index("y"), lax.axis_index("c")


_HBM = pl.BlockSpec(memory_space=pl.ANY)


def _all_gather(blocks, name):
    n = len(blocks)

    def body(*refs):
        xs, outs = refs[:n], refs[n : 2 * n]
        send_sems, recv_sems, local_sems = refs[2 * n :]
        x, y, c = _position()
        me, sibling = (x, y, c), (x, y, 1 - c)
        chips = [(1 - x, y), (x, 1 - y), (1 - x, 1 - y)]

        def rows(a, px, py, pc):
            return outs[a].at[4 * px + 2 * py + pc]

        def copy(a, k, blk, to, src=None):
            return pltpu.make_async_remote_copy(
                src_ref=rows(a, *blk) if src is None else src, dst_ref=rows(a, *blk),
                send_sem=send_sems.at[7 * a + k], recv_sem=recv_sems.at[7 * a + k], device_id=to, device_id_type=MESH,
            )

        first = []
        for a in range(n):
            first += [copy(a, 1 + j, me, (*chip, c), src=xs[a]) for j, chip in enumerate(chips)]
            first.append(copy(a, 0, me, sibling, src=xs[a]))
        mine = [pltpu.make_async_copy(xs[a], rows(a, *me), local_sems.at[a]) for a in range(n)]
        for cp in first + mine:
            cp.start()
        passed = []
        for j, chip in enumerate(chips):
            for a in range(n):
                copy(a, 1 + j, (*chip, c), me).wait_recv()
                passed.append(copy(a, 4 + j, (*chip, c), sibling))
                passed[-1].start()
        for a in range(n):
            copy(a, 0, sibling, me).wait_recv()
        for j, chip in enumerate(chips):
            for a in range(n):
                copy(a, 4 + j, (*chip, 1 - c), me).wait_recv()
        for cp in first + passed:
            cp.wait_send()
        for cp in mine:
            cp.wait()

    return pl.pallas_call(
        body,
        name=name,
        out_shape=[jax.ShapeDtypeStruct((N_DEV, *b.shape), b.dtype) for b in blocks],
        in_specs=[_HBM] * n,
        out_specs=[_HBM] * n,
        scratch_shapes=[pltpu.SemaphoreType.DMA((7 * n,)), pltpu.SemaphoreType.DMA((7 * n,)), pltpu.SemaphoreType.DMA((n,))],
    )(*blocks)


_SEM = pl.BlockSpec(memory_space=pltpu.SEMAPHORE)
_HBM_ONLY = pl.BlockSpec(memory_space=pltpu.HBM)
_SIDE_EFFECT = pltpu.SideEffectType.DATAFLOW_SIDE_EFFECTING


def _peer(x, y, c, k):
    return (1 - x if k & 4 else x, 1 - y if k & 2 else y, 1 - c if k & 1 else c)


_PEER_BITS = {"gather": range(1, N_DEV), "gather_half": (1, 4, 2, 6), "forward": (4, 2, 6), "scatter": range(1, N_DEV), "chips": (4, 2, 6)}
_LAND_SLOTS = {"gather": N_DEV, "gather_half": N_DEV, "scatter": N_DEV, "chips": 3}
_GATHERS = ("gather", "gather_half")


def _exchange_copies(src_refs, land_refs, send_sems, recv_sems, pattern, receive_side):
    x, y, c = _position()
    me = 4 * x + 2 * y + c
    bits = _PEER_BITS[pattern]
    cps = []
    for j, k in enumerate(bits):
        px, py, pc = _peer(x, y, c, k)
        peer = 4 * px + 2 * py + pc
        for a, (src, land) in enumerate(zip(src_refs, land_refs)):
            to = (px, py, pc)
            if pattern == "chips":
                s, slot = src.at[2 * px + py], j
            elif pattern == "forward":
                slot = 4 * px + 2 * py + (1 - c if receive_side else c)
                s, to = land.at[slot], (x, y, 1 - c)
            else:
                s, slot = (src if pattern in _GATHERS else src.at[peer]), (peer if receive_side else me)
            cps.append(pltpu.make_async_remote_copy(
                src_ref=s, dst_ref=land.at[slot],
                send_sem=send_sems.at[len(bits) * a + j], recv_sem=recv_sems.at[len(bits) * a + j],
                device_id=to, device_id_type=MESH,
            ))
    return cps


def _own_copies(src_refs, land_refs, own_sems):
    x, y, c = _position()
    return [
        pltpu.make_async_copy(src, land.at[4 * x + 2 * y + c], own_sems.at[a])
        for a, (src, land) in enumerate(zip(src_refs, land_refs))
    ]


def _exchange_start(srcs, after, name, pattern):
    n = len(srcs)
    m = len(_PEER_BITS[pattern])
    lands = [jax.ShapeDtypeStruct((_LAND_SLOTS[pattern], *s.shape[-2:]), s.dtype) for s in srcs]

    def body(*refs):
        src_refs, land_refs = refs[1 : 1 + n], refs[1 + n : 1 + 2 * n]
        send_sems, recv_sems, own_sems = refs[1 + 2 * n : 4 + 2 * n]
        token = refs[-1]
        if pattern in _GATHERS:
            for cp in _own_copies(src_refs, land_refs, own_sems):
                cp.start()
        for cp in _exchange_copies(src_refs, land_refs, send_sems, recv_sems, pattern, receive_side=False):
            cp.start()
        token[...] = jnp.zeros_like(token)

    hbm = lambda t: pltpu.with_memory_space_constraint(t, pltpu.HBM)
    out = pl.pallas_call(
        body,
        name=name,
        out_shape=(
            pltpu.SemaphoreType.DMA((m * n,)), pltpu.SemaphoreType.DMA((m * n,)), pltpu.SemaphoreType.DMA((n,)),
            *[pltpu.HBM(s.shape, s.dtype) for s in srcs], *[pltpu.HBM(l.shape, l.dtype) for l in lands],
            jax.ShapeDtypeStruct((8, LANES), F32),
        ),
        in_specs=(_HBM, *[_HBM_ONLY] * (2 * n)),
        out_specs=(_SEM, _SEM, _SEM, *[_HBM_ONLY] * (2 * n), pl.BlockSpec(memory_space=pltpu.VMEM)),
        input_output_aliases={1 + i: 3 + i for i in range(2 * n)},
        compiler_params=pltpu.CompilerParams(has_side_effects=_SIDE_EFFECT),
    )(after, *[hbm(s) for s in srcs], *[hbm(lax.empty(l.shape, l.dtype)) for l in lands])
    return out[:3], out[3 : 3 + n], out[3 + n : 3 + 2 * n], out[-1]


def _exchange_wait(sems, srcs, lands, after, name, pattern):
    n = len(srcs)

    def body(*refs):
        src_refs, land_refs = refs[:n], refs[n : 2 * n]
        send_sems, recv_sems, own_sems = refs[2 * n : 2 * n + 3]
        if pattern in _GATHERS:
            for cp in _own_copies(src_refs, land_refs, own_sems):
                cp.wait()
        for cp in _exchange_copies(src_refs, land_refs, send_sems, recv_sems, pattern, receive_side=True):
            cp.wait_send()
            cp.wait_recv()

    out = pl.pallas_call(
        body,
        name=name,
        out_shape=(*[pltpu.HBM(s.shape, s.dtype) for s in srcs], *[pltpu.HBM(l.shape, l.dtype) for l in lands]),
        in_specs=(*[_HBM_ONLY] * (2 * n), _SEM, _SEM, _SEM, _HBM),
        out_specs=tuple([_HBM_ONLY] * (2 * n)),
        input_output_aliases={i: i for i in range(2 * n)},
        compiler_params=pltpu.CompilerParams(has_side_effects=_SIDE_EFFECT),
    )(*srcs, *lands, *sems, after)
    return out[:n], out[n:]


def _gather_forward(sems, srcs, lands, after, name):
    n = len(srcs)
    m = len(_PEER_BITS["forward"])

    def body(*refs):
        src_refs, land_refs = refs[:n], refs[n : 2 * n]
        send_sems, recv_sems, own_sems = refs[2 * n : 2 * n + 3]
        forward_send, forward_recv, token = refs[2 * n + 4], refs[2 * n + 5], refs[-1]
        for cp in _own_copies(src_refs, land_refs, own_sems):
            cp.wait()
        for cp in _exchange_copies(src_refs, land_refs, send_sems, recv_sems, "gather_half", receive_side=True):
            cp.wait_send()
            cp.wait_recv()
        for cp in _exchange_copies(land_refs, land_refs, forward_send, forward_recv, "forward", receive_side=False):
            cp.start()
        token[...] = jnp.zeros_like(token)

    out = pl.pallas_call(
        body,
        name=name,
        out_shape=(
            pltpu.SemaphoreType.DMA((m * n,)), pltpu.SemaphoreType.DMA((m * n,)),
            *[pltpu.HBM(l.shape, l.dtype) for l in lands], jax.ShapeDtypeStruct((8, LANES), F32),
        ),
        in_specs=(*[_HBM_ONLY] * (2 * n), _SEM, _SEM, _SEM, _HBM),
        out_specs=(_SEM, _SEM, *[_HBM_ONLY] * n, pl.BlockSpec(memory_space=pltpu.VMEM)),
        input_output_aliases={n + i: 2 + i for i in range(n)},
        compiler_params=pltpu.CompilerParams(has_side_effects=_SIDE_EFFECT),
    )(*srcs, *lands, *sems, after)
    return out[:2], out[2 : 2 + n], out[-1]


def _forward_wait(sems, lands, after, name):
    n = len(lands)

    def body(*refs):
        land_refs = refs[:n]
        for cp in _exchange_copies(land_refs, land_refs, refs[n], refs[n + 1], "forward", receive_side=True):
            cp.wait_send()
            cp.wait_recv()

    return pl.pallas_call(
        body,
        name=name,
        out_shape=tuple(pltpu.HBM(l.shape, l.dtype) for l in lands),
        in_specs=(*[_HBM_ONLY] * n, _SEM, _SEM, _HBM),
        out_specs=tuple([_HBM_ONLY] * n),
        input_output_aliases={i: i for i in range(n)},
        compiler_params=pltpu.CompilerParams(has_side_effects=_SIDE_EFFECT),
    )(*lands, *sems, after)


def _sibling_exchange(sends):
    n = len(sends)

    def body(*refs):
        srcs, dsts = refs[:n], refs[n : 2 * n]
        send_sems, recv_sems = refs[2 * n :]
        x, y, c = _position()
        cps = [
            pltpu.make_async_remote_copy(
                src_ref=srcs[a].at[1 - c], dst_ref=dsts[a], send_sem=send_sems.at[a], recv_sem=recv_sems.at[a],
                device_id=(x, y, 1 - c), device_id_type=MESH,
            )
            for a in range(n)
        ]
        for cp in cps:
            cp.start()
        for cp in cps:
            cp.wait()

    return pl.pallas_call(
        body,
        name="rs_sibling",
        out_shape=[jax.ShapeDtypeStruct(s.shape[1:], s.dtype) for s in sends],
        in_specs=[_HBM] * n,
        out_specs=[_HBM] * n,
        scratch_shapes=[pltpu.SemaphoreType.DMA((n,)), pltpu.SemaphoreType.DMA((n,))],
    )(*sends)


def _rows_tile(r):
    return ROW_TILE if r % ROW_TILE == 0 else r


def _pair_sum(send, got, core, name):
    _, _, r, c = send.shape
    br = _rows_tile(r)

    def body(core_ref, a_ref, b_ref, o_ref):
        o_ref[...] = (a_ref[...].astype(F32) + b_ref[...].astype(F32)).astype(o_ref.dtype)

    return pl.pallas_call(
        body,
        name=name,
        grid_spec=pltpu.PrefetchScalarGridSpec(
            num_scalar_prefetch=1,
            grid=(4, r // br),
            in_specs=[
                pl.BlockSpec((None, None, br, c), lambda n, i, core: (core[0], n, i, 0)),
                pl.BlockSpec((None, br, c), lambda n, i, core: (n, i, 0)),
            ],
            out_specs=pl.BlockSpec((None, br, c), lambda n, i, core: (n, i, 0)),
        ),
        out_shape=jax.ShapeDtypeStruct((4, r, c), send.dtype),
        compiler_params=_params(("parallel", "parallel")),
    )(core, send, got)


def _adamw(w, g, m, v):
    m = ADAM_B1 * m + (1.0 - ADAM_B1) * g
    v = ADAM_B2 * v + (1.0 - ADAM_B2) * (g * g)
    m_hat = m / (1.0 - ADAM_B1 ** ADAM_STEP)
    v_hat = v / (1.0 - ADAM_B2 ** ADAM_STEP)
    delta = -ADAM_LR * (m_hat / (jnp.sqrt(v_hat) + ADAM_EPS) + ADAM_WD * w)
    return delta, m, v


def _shard_update(send, got, recv, w, m, v, pos, name):
    _, r, c = w.shape
    br = _rows_tile(r)

    def body(pos_ref, a_ref, b_ref, r_ref, w_ref, m_ref, v_ref, g_ref, d_ref, nm_ref, nv_ref):
        g = a_ref[...].astype(F32) + b_ref[...].astype(F32)
        for n in range(3):
            g = g + r_ref[n].astype(F32)
        g_ref[...] = g
        d_ref[...], nm_ref[...], nv_ref[...] = _adamw(w_ref[...], g, m_ref[...], v_ref[...])

    own = pl.BlockSpec((None, br, c), lambda i, pos: (0, i, 0))
    return pl.pallas_call(
        body,
        name=name,
        grid_spec=pltpu.PrefetchScalarGridSpec(
            num_scalar_prefetch=1,
            grid=(r // br,),
            in_specs=[
                pl.BlockSpec((None, None, br, c), lambda i, pos: (pos[0], pos[1], i, 0)),
                pl.BlockSpec((None, br, c), lambda i, pos: (pos[1], i, 0)),
                pl.BlockSpec((3, br, c), lambda i, pos: (0, i, 0)),
                own, own, own,
            ],
            out_specs=[own, own, own, own],
        ),
        out_shape=[jax.ShapeDtypeStruct((1, r, c), F32)] * 4,
        compiler_params=_params(("parallel",)),
    )(pos, send, got, recv, w, m, v)


def _shard_update_direct(parts, chunks, w, m, v, me, name):
    _, r, c = w.shape
    br = _rows_tile(r)

    def body(me_ref, p_ref, own_ref, w_ref, m_ref, v_ref, g_ref, d_ref, nm_ref, nv_ref):
        g = None
        for n in range(N_DEV):
            part = jnp.where(me_ref[0] == n, own_ref[...], p_ref[n]).astype(F32)
            g = part if g is None else g + part
        g_ref[...] = g
        d_ref[...], nm_ref[...], nv_ref[...] = _adamw(w_ref[...], g, m_ref[...], v_ref[...])

    shard = pl.BlockSpec((None, br, c), lambda i, me: (0, i, 0))
    return pl.pallas_call(
        body,
        name=name,
        grid_spec=pltpu.PrefetchScalarGridSpec(
            num_scalar_prefetch=1,
            grid=(r // br,),
            in_specs=[
                pl.BlockSpec((N_DEV, br, c), lambda i, me: (0, i, 0)),
                pl.BlockSpec((None, br, c), lambda i, me: (me[0], i, 0)),
                shard, shard, shard,
            ],
            out_specs=[shard, shard, shard, shard],
        ),
        out_shape=[jax.ShapeDtypeStruct((1, r, c), F32)] * 4,
        compiler_params=_params(("parallel",)),
    )(me, parts, chunks, w, m, v)


def _small_update(parts, first_rows, ws, ms, vs):
    k = len(ws)

    def unpacked(rows, shape):
        if len(shape) == 2 and shape[1] <= LANES:
            return rows[0:1, : shape[1]]
        if len(shape) == 2:
            return jnp.concatenate([rows[r : r + 1] for r in range(shape[1] // LANES)], axis=1)
        return rows.reshape(shape)

    def body(p_ref, f_ref, *refs):
        w_refs, m_refs, v_refs = refs[:k], refs[k : 2 * k], refs[2 * k : 3 * k]
        outs, loss_ref = refs[3 * k : 7 * k], refs[7 * k]
        g, first = p_ref[0], f_ref[0]
        for n in range(1, N_DEV):
            g = g + p_ref[n]
            first = first + f_ref[n]
        g = jnp.concatenate([g[:8] + first, g[8:]], axis=0)
        off = 0
        for i, (_, rows) in enumerate(_SMALL):
            gi = unpacked(g[off : off + rows], w_refs[i].shape)
            off += rows
            outs[i][...] = gi
            outs[k + i][...], outs[2 * k + i][...], outs[3 * k + i][...] = _adamw(w_refs[i][...], gi, m_refs[i][...], v_refs[i][...])
        loss_ref[...] = g[off : off + 1, 0:1]

    out = pl.pallas_call(
        body,
        name="small_update",
        out_shape=[jax.ShapeDtypeStruct(w.shape, F32) for _ in range(4) for w in ws] + [jax.ShapeDtypeStruct((1, 1), F32)],
        compiler_params=pltpu.CompilerParams(vmem_limit_bytes=VMEM_LIMIT),
    )(parts, first_rows, *ws, *ms, *vs)
    return [out[a * k : (a + 1) * k] for a in range(4)], out[4 * k]


_SHARD_AXIS = (1, 1, 1, 0, 0, 0, 0)
_TRANSPOSED = (False, False, False, False, True, True, False)


def _full_from_gathered(t, axis):
    if axis == 0:
        return t.reshape(N_DEV * t.shape[1], t.shape[2])
    return t


_SMALL = (("norm1_g", 8), ("norm2_g", 8), ("norm_f_g", 8), ("b_forget", 8), ("pool_scale", 8), ("pool_mix", 512))


def _pack_small(vals, loss_row):
    parts = []
    for (name, rows), t in zip(_SMALL, vals):
        f = t.astype(F32).reshape(-1)
        f = jnp.concatenate([f, jnp.zeros((rows * LANES - f.shape[0],), F32)]).reshape(rows, LANES)
        parts.append(f)
    parts.append(loss_row)
    return jnp.concatenate(parts, axis=0)


def _local_grads(x, tgt, g1, g2, gf, b_forget, pool_mix, pool_scale, w_in, fwd_token, out_weights, ffn_weights, ffn_grads_out, out_grads_out, small_grads_out, in_grads_out, norm1_grad_out):
    n_seq, S, _ = x.shape
    T = n_seq * S
    x2 = x.reshape(T, D_MODEL)
    tg2 = tgt.reshape(T, D_MODEL)
    w_uqkv, w_fl, w_g = w_in
    b_pad = jnp.concatenate([b_forget.reshape(1, N_HEADS), jnp.zeros((1, FL_PAD - N_HEADS), F32)], axis=1)
    mix_b = pool_mix.reshape(len(POOL_WINDOWS), GROUP_DIM, GROUP_DIM).astype(BF16)
    scale = pool_scale.reshape(1, POOL_WIDTH)
    g1 = g1.reshape(1, D_MODEL)
    g2 = g2.reshape(1, D_MODEL)
    gf = gf.reshape(1, D_MODEL)

    h, u, qkv, fl, gates = _in_proj(x2, g1, w_uqkv, w_fl, w_g, fwd_token)
    fcol = _forget_fwd(fl, b_pad, n_seq, S)
    pm, p2, p3 = _pool_fwd(u, mix_b, scale, n_seq, S)
    a, lse = _attn_fwd(qkv, fcol, n_seq, S)
    w_po, w_ao, w_out = out_weights(a)
    merged, x1, attn_y, pool_y = _mix_out(a, p3, gates, x2, w_ao, w_po, w_out)
    w_gate_t, w_up_t, w_down = ffn_weights(x1)
    h2, gate, up, act, dx2, loss_rows, dgf = _ffn_fwd(x1, g2, gf, tg2, w_gate_t, w_up_t, w_down)

    dgate, dup, dx1, dg2 = _ffn_bwd(dx2, gate, up, x1, g2, w_gate_t, w_up_t, w_down)
    bwd_token = ffn_grads_out(_matmul_tn(dgate, h2, "dw_ffn_gate"), _matmul_tn(dup, h2, "dw_ffn_up"), _matmul_tn(act, dx2, "dw_ffn_down"))
    dgates, dpy, day, da, dp2, dscale = _mix_bwd(dx1, gates, pool_y, attn_y, p2, scale, w_out, w_ao, w_po, bwd_token)
    out_token = out_grads_out(
        _matmul_tn(p3, dpy, "dw_pool_out", col_chunks=True), _matmul_tn(a, day, "dw_attn_out", col_chunks=True), _matmul_tn(merged, dx1, "dw_out")
    )
    du, dmix = _pool_bwd(dp2, pm, mix_b, out_token, n_seq, S)
    dq, dk, dv, dfk, dfq = _attn_bwd(qkv, da, a, fcol, lse, n_seq, S)
    dfl, db = _forget_bwd(dfk, dfq, fl, b_pad, n_seq, S)
    small_token = small_grads_out((jnp.zeros_like(g1), dg2, dgf, db[:, :N_HEADS], dscale, dmix), loss_rows)
    in_token = in_grads_out(_dw_in(h, du, dq, dk, dv, dfl, dgates, small_token))
    dx, dg1 = _in_proj_bwd(du, dq, dk, dv, dfl, dgates, x2, dx1, g1, w_uqkv, w_fl, w_g, in_token)
    norm1_grad_out(dg1)
    return dx.reshape(n_seq, S, D_MODEL)


def kernel(x, norm1_g, w_in, b_forget, pool_mix, pool_scale, w_pool_out, w_attn_out, w_out, norm2_g, w_ffn_gate, w_ffn_up, w_ffn_down, norm_f_g, loss_target, m_norm1_g, m_w_in, m_b_forget, m_pool_mix, m_pool_scale, m_w_pool_out, m_w_attn_out, m_w_out, m_norm2_g, m_w_ffn_gate, m_w_ffn_up, m_w_ffn_down, m_norm_f_g, v_norm1_g, v_w_in, v_b_forget, v_pool_mix, v_pool_scale, v_w_pool_out, v_w_attn_out, v_w_out, v_norm2_g, v_w_ffn_gate, v_w_ffn_up, v_w_ffn_down, v_norm_f_g):
    names = ("w_in", "w_pool_out", "w_attn_out", "w_out", "w_ffn_gate", "w_ffn_up", "w_ffn_down")
    w_sh = (w_in, w_pool_out, w_attn_out, w_out, w_ffn_gate, w_ffn_up, w_ffn_down)
    m_sh = (m_w_in, m_w_pool_out, m_w_attn_out, m_w_out, m_w_ffn_gate, m_w_ffn_up, m_w_ffn_down)
    v_sh = (v_w_in, v_w_pool_out, v_w_attn_out, v_w_out, v_w_ffn_gate, v_w_ffn_up, v_w_ffn_down)

    cx, cy, cc = _position()
    me = 4 * cx + 2 * cy + cc
    def stored(t, transposed):
        return jnp.transpose(t, (0, 2, 1)) if transposed else t

    w_sh, m_sh, v_sh = ([stored(t, tr) for t, tr in zip(ts, _TRANSPOSED)] for ts in (w_sh, m_sh, v_sh))
    shards = [w[0].astype(BF16) for w in w_sh]
    (gathered_in,) = _all_gather(shards[:1], "w_in_all_gather")
    out_sems = _exchange_start(shards[1:4], gathered_in, "out_weights_gather_start", "gather")
    ffn_sems = _exchange_start(shards[4:], out_sems[3], "ffn_weights_gather_start", "gather_half")
    no_order = jnp.zeros((8, LANES), F32)
    started = {}

    def out_weights(after):
        forward_sems, lands, token = _gather_forward(*ffn_sems[:3], after, "ffn_weights_forward_start")
        started["forward"] = (forward_sems, lands)
        _, lands = _exchange_wait(*out_sems[:3], token, "out_weights_gather_wait", "gather")
        return [_full_from_gathered(t, axis) for t, axis in zip(lands, _SHARD_AXIS[out])]

    def ffn_weights(after):
        lands = _forward_wait(*started["forward"], after, "ffn_weights_gather_wait")
        return [_full_from_gathered(t, axis) for t, axis in zip(lands, _SHARD_AXIS[ffn])]

    def hold_ffn_grads(*whole_grads):
        started["held"] = whole_grads
        return no_order

    def scatter_grads(*out_grads):
        chunks = [
            t if axis == 1 else t.reshape(N_DEV, -1, t.shape[1])
            for t, axis in zip((*out_grads, *started["held"]), _SHARD_AXIS[scattered])
        ]
        started["scatter"] = _exchange_start(chunks, no_order, "grads_scatter_start", "scatter")
        return started["scatter"][3]

    def gather_small(small, loss_rows):
        started["small"] = _exchange_start([_pack_small(small, loss_rows)], no_order, "small_grads_gather_start", "gather")
        return started["small"][3]

    core = jnp.reshape(cc, (1,)).astype(jnp.int32)
    pos = jnp.stack([cc, 2 * cx + cy]).astype(jnp.int32)

    def reduce_w_in(send_in):
        (got_in,) = _sibling_exchange([send_in])
        pair_in = _pair_sum(send_in, got_in, core, "pair_sum_w_in")
        started["in"] = (send_in, got_in, _exchange_start([pair_in], no_order, "w_in_grads_chips_start", "chips"))
        return started["in"][2][3]

    def gather_norm1(dg1):
        rows = jnp.reshape(dg1, (8, LANES))
        started["norm1"] = _exchange_start([rows], no_order, "norm1_grad_gather_start", "gather")

    ffn, out, scattered = slice(4, 7), slice(1, 4), slice(1, 7)
    grad_x = _local_grads(
        x, loss_target, norm1_g, norm2_g, norm_f_g, b_forget, pool_mix, pool_scale, _w_in_pieces(gathered_in), ffn_sems[3],
        out_weights, ffn_weights, hold_ffn_grads, scatter_grads, gather_small, reduce_w_in, gather_norm1,
    )
    send_in, got_in, chip_sems = started["in"]

    srcs, lands = _exchange_wait(*started["scatter"][:3], started["norm1"][3], "grads_scatter_wait", "scatter")
    updates = [
        _shard_update_direct(p, s, w, m, v, jnp.reshape(me, (1,)).astype(jnp.int32), "update_" + n)
        for p, s, w, m, v, n in zip(lands, srcs, w_sh[scattered], m_sh[scattered], v_sh[scattered], names[scattered])
    ]
    updates_out, updates_ffn = updates[:3], updates[3:]

    small_w = (norm1_g, norm2_g, norm_f_g, b_forget, pool_scale, pool_mix)
    small_m = (m_norm1_g, m_norm2_g, m_norm_f_g, m_b_forget, m_pool_scale, m_pool_mix)
    small_v = (v_norm1_g, v_norm2_g, v_norm_f_g, v_b_forget, v_pool_scale, v_pool_mix)
    _, (recv_in,) = _exchange_wait(*chip_sems[:3], updates_ffn[-1][0], "w_in_grads_chips_wait", "chips")
    update_in = _shard_update(send_in, got_in, recv_in, w_in, m_w_in, v_w_in, pos, "update_w_in")

    def gathered_small(key, after, name):
        _, lands = _exchange_wait(*started[key][:3], after, name, "gather")
        return lands[0]

    parts = gathered_small("small", update_in[0], "small_grads_gather_wait")
    first_rows = gathered_small("norm1", parts, "norm1_grad_gather_wait")
    (g_s, d_s, nm_s, nv_s), loss = _small_update(parts, first_rows, small_w, small_m, small_v)
    g_w, d_w, nm_w, nv_w = zip(*(
        [stored(t, tr) for t in u] for u, tr in zip([update_in] + updates_out + updates_ffn, _TRANSPOSED)
    ))
    loss = loss.reshape(())
    (g1, g2, gf, gb, gsc, gmix), (d1, d2, df, db_, dsc, dmx) = g_s, d_s
    (m1, m2, mf, mb, msc, mmx), (v1, v2, vf, vb, vsc, vmx) = nm_s, nv_s

    def ordered(n1, win, b, mix, sc, wpo, wao, wout, n2, wg, wu, wd, nf):
        return (n1, win, b, mix, sc, wpo, wao, wout, n2, wg, wu, wd, nf)

    grads = ordered(g1, g_w[0], gb, gmix, gsc, g_w[1], g_w[2], g_w[3], g2, g_w[4], g_w[5], g_w[6], gf)
    deltas = ordered(d1, d_w[0], db_, dmx, dsc, d_w[1], d_w[2], d_w[3], d2, d_w[4], d_w[5], d_w[6], df)
    new_m = ordered(m1, nm_w[0], mb, mmx, msc, nm_w[1], nm_w[2], nm_w[3], m2, nm_w[4], nm_w[5], nm_w[6], mf)
    new_v = ordered(v1, nv_w[0], vb, vmx, vsc, nv_w[1], nv_w[2], nv_w[3], v2, nv_w[4], nv_w[5], nv_w[6], vf)
    return (loss, grad_x, *grads, *deltas, *new_m, *new_v)
```

```python
import jax
import jax.numpy as jnp
from jax import lax
from jax.experimental import pallas as pl
from jax.experimental.pallas import tpu as pltpu

F32 = jnp.float32
BF16 = jnp.bfloat16
MESH = pl.DeviceIdType.MESH

D_MODEL = 1024
POOL_WINDOWS = (2, 4, 8, 16)
POOL_WIDTH = 512
GROUP_DIM = 128
ATTN_WIDTH = 512
HEAD_DIM = 64
N_HEADS = 8
N_PAIRS = 4
D_FF = 2816
RMS_EPS = 1e-6
N_DEV = 8
LANES = 128
FL_PAD = 128

ADAM_LR = 0.001
ADAM_B1 = 0.9
ADAM_B2 = 0.999
ADAM_EPS = 1e-08
ADAM_WD = 0.01
ADAM_STEP = 10

VMEM_LIMIT = 56 * 1024 * 1024
VMEM_LIMIT_MAX = 60 * 1024 * 1024
ROW_TILE = 512
ATTN_BLOCK = 512
FF_CHUNK = 256
FF_ROW_TILE = 512
DW_TOKENS = 2048


def _mm(a, b):
    return jnp.dot(a, b, preferred_element_type=F32)


def _mm_nt(a, b):
    return lax.dot_general(a, b, (((1,), (1,)), ((), ())), preferred_element_type=F32)


def _mm_tn(a, b):
    return lax.dot_general(a, b, (((0,), (0,)), ((), ())), preferred_element_type=F32)


def _whole_cols(w_ref):
    if len(w_ref.shape) == 2:
        return w_ref[...]
    return jnp.concatenate([w_ref[d] for d in range(w_ref.shape[0])], axis=1)


def _sigmoid(x):
    return 1.0 / (1.0 + jnp.exp(-x))


def _params(sem, vmem=VMEM_LIMIT):
    return pltpu.CompilerParams(dimension_semantics=sem, vmem_limit_bytes=vmem)


def _const_spec(shape):
    nd = len(shape)
    return pl.BlockSpec(shape, lambda *_: (0,) * nd, pipeline_mode=pl.Buffered(1))


def _rms_fwd(x, g):
    r = lax.rsqrt(jnp.mean(x * x, axis=-1, keepdims=True) + RMS_EPS)
    xh = x * r
    return xh * g, xh, r


def _rms_bwd(dy, xh, r, g):
    dxh = dy * g
    dx = r * (dxh - xh * jnp.mean(dxh * xh, axis=-1, keepdims=True))
    return dx, dy * xh


def _in_proj(x, g1, w_uqkv, w_fl, w_g, token):
    T = x.shape[0]
    tm = ROW_TILE

    def body(x_ref, g_ref, wa_ref, wf_ref, wg_ref, token_ref, h_ref, u_ref, qkv_ref, fl_ref, gt_ref):
        h, _, _ = _rms_fwd(x_ref[...], g_ref[...])
        hb = h.astype(BF16)
        h_ref[...] = hb
        z = _mm(hb, wa_ref[...])
        u_ref[...] = z[:, :POOL_WIDTH]
        qkv_ref[...] = z[:, POOL_WIDTH:].astype(BF16)
        fl_ref[...] = _mm(hb, wf_ref[...])
        gt_ref[...] = _mm(hb, wg_ref[...]).astype(BF16)

    row = lambda n: pl.BlockSpec((tm, n), lambda i: (i, 0))
    return pl.pallas_call(
        body,
        name="in_proj",
        grid=(T // tm,),
        in_specs=[row(D_MODEL), _const_spec((1, D_MODEL)), _const_spec(w_uqkv.shape), _const_spec(w_fl.shape), _const_spec(w_g.shape), _HBM],
        out_specs=[row(D_MODEL), row(POOL_WIDTH), row(3 * ATTN_WIDTH), row(FL_PAD), row(2 * D_MODEL)],
        out_shape=[
            jax.ShapeDtypeStruct((T, D_MODEL), BF16),
            jax.ShapeDtypeStruct((T, POOL_WIDTH), F32),
            jax.ShapeDtypeStruct((T, 3 * ATTN_WIDTH), BF16),
            jax.ShapeDtypeStruct((T, FL_PAD), F32),
            jax.ShapeDtypeStruct((T, 2 * D_MODEL), BF16),
        ],
        compiler_params=_params(("parallel",)),
    )(x, g1, w_uqkv, w_fl, w_g, token)


def _log_sigmoid(x):
    return jnp.minimum(x, 0.0) - jnp.log(1.0 + jnp.exp(-jnp.abs(x)))


def _forget_fwd(fl, b_pad, n_seq, S):
    def body(fl_ref, b_ref, fcol_ref):
        lf = _log_sigmoid(fl_ref[...] + b_ref[...])
        t = lf.T
        lane = lax.broadcasted_iota(jnp.int32, t.shape, 1)
        k = 1
        while k < S:
            t = t + jnp.where(lane >= k, pltpu.roll(t, k, 1), 0.0)
            k *= 2
        fcol_ref[...] = t.T

    return pl.pallas_call(
        body,
        name="forget_fwd",
        grid=(n_seq,),
        in_specs=[pl.BlockSpec((S, FL_PAD), lambda s: (s, 0)), _const_spec((1, FL_PAD))],
        out_specs=pl.BlockSpec((S, FL_PAD), lambda s: (s, 0)),
        out_shape=jax.ShapeDtypeStruct((n_seq * S, FL_PAD), F32),
        compiler_params=_params(("parallel",)),
    )(fl, b_pad)


def _window_pick(g, v2, v4, v8, v16):
    return jnp.where(g == 0, v2, jnp.where(g == 1, v4, jnp.where(g == 2, v8, v16)))


def _pool_fwd(u, mix_b, scale, n_seq, S):
    T = n_seq * S

    def body(u_ref, mix_ref, sc_ref, pm_ref, p2_ref, p3_ref):
        g = pl.program_id(1)
        uu = u_ref[...]
        row = lax.broadcasted_iota(jnp.int32, uu.shape, 0)

        def back(a, k):
            return jnp.where(row >= k, pltpu.roll(a, k, 0), 0.0)

        s2 = uu + back(uu, 1)
        s4 = s2 + back(s2, 2)
        s8 = s4 + back(s4, 4)
        s16 = s8 + back(s8, 8)
        w = _window_pick(g, 2.0, 4.0, 8.0, 16.0)
        cnt = jnp.minimum((row + 1).astype(F32), w)
        pm = _window_pick(g, s2, s4, s8, s16) / cnt - uu
        pmb = pm.astype(BF16)
        pm_ref[...] = pmb
        p2 = _mm(pmb, mix_ref[...])
        p2_ref[...] = p2
        p3_ref[...] = (p2 * sc_ref[...]).astype(BF16)

    grp = pl.BlockSpec((S, GROUP_DIM), lambda s, g: (s, g))
    return pl.pallas_call(
        body,
        name="pool_fwd",
        grid=(n_seq, len(POOL_WINDOWS)),
        in_specs=[
            grp,
            pl.BlockSpec((None, GROUP_DIM, GROUP_DIM), lambda s, g: (g, 0, 0)),
            pl.BlockSpec((1, GROUP_DIM), lambda s, g: (0, g)),
        ],
        out_specs=[grp, grp, grp],
        out_shape=[
            jax.ShapeDtypeStruct((T, POOL_WIDTH), BF16),
            jax.ShapeDtypeStruct((T, POOL_WIDTH), F32),
            jax.ShapeDtypeStruct((T, POOL_WIDTH), BF16),
        ],
        compiler_params=_params(("parallel", "parallel")),
    )(u, mix_b, scale)


def _split3(v):
    hi = v.astype(BF16).astype(F32)
    r = v - hi
    mid = r.astype(BF16).astype(F32)
    lo = (r - mid).astype(BF16).astype(F32)
    return hi, mid, lo


def _bias_lanes(v):
    hi, mid, lo = _split3(v)
    lane = lax.broadcasted_iota(jnp.int32, (1, LANES), 1)
    packed = jnp.where(lane < N_HEADS, hi, jnp.where(lane < 2 * N_HEADS, pltpu.roll(mid, N_HEADS, 1), pltpu.roll(lo, 2 * N_HEADS, 1)))
    return jnp.where(lane < 3 * N_HEADS, packed, 0.0).astype(BF16)


def _bias_placement(slot):
    row = lax.broadcasted_iota(jnp.int32, (LANES, N_HEADS * LANES), 0)
    col = lax.broadcasted_iota(jnp.int32, (LANES, N_HEADS * LANES), 1)
    h = col // LANES
    n = col % LANES - jnp.where(h % 2 == 0, HEAD_DIM, 0) - 3 * slot
    return ((n >= 0) & (n < 3) & (row == N_HEADS * n + h)).astype(BF16)


def _augment(xp, h, bias, ones_slot):
    lane = lax.broadcasted_iota(jnp.int32, (1, LANES), 1)
    hh = h % 2
    head = (lane >= HEAD_DIM * hh) & (lane < HEAD_DIM * (hh + 1))
    b = HEAD_DIM * (1 - hh)
    rest = jnp.zeros_like(xp) if bias is None else bias[:, h * LANES : (h + 1) * LANES]
    out = jnp.where(head, xp, rest)
    if ones_slot is not None:
        out = jnp.where((lane >= b + 3 * ones_slot) & (lane < b + 3 * ones_slot + 3), jnp.ones_like(xp), out)
    return out


def _attn_fwd(qkv, fcol, n_seq, S):
    T = n_seq * S
    tb = ATTN_BLOCK
    nq = S // tb
    scale = HEAD_DIM ** -0.5

    def body(q_ref, k_ref, v_ref, fc_ref, o_ref, st_ref, qa_sc, ka_sc, m_sc, l_sc, acc_sc):
        i = pl.program_id(1)
        lane = lax.broadcasted_iota(jnp.int32, (1, LANES), 1)
        low = lane < HEAD_DIM

        @pl.when(i == 0)
        def _():
            place = _bias_placement(1)

            def rows_ka(r, carry):
                r0 = pl.multiple_of(r * tb, tb)
                bias = _mm(_bias_lanes(-fc_ref[pl.ds(r0, tb), :]), place).astype(BF16)
                for h in range(N_HEADS):
                    kp = k_ref[pl.ds(r0, tb), (h // 2) * LANES : (h // 2 + 1) * LANES] * scale
                    ka_sc[h, pl.ds(r0, tb), :] = _augment(kp, h, bias, 0)
                return carry

            lax.fori_loop(0, nq, rows_ka, 0)

        q0 = pl.multiple_of(i * tb, tb)
        bias = _mm(_bias_lanes(fc_ref[pl.ds(q0, tb), :]), _bias_placement(0)).astype(BF16)
        for h in range(N_HEADS):
            qa_sc[h] = _augment(q_ref[:, (h // 2) * LANES : (h // 2 + 1) * LANES], h, bias, 1)
        m_sc[...] = jnp.full(m_sc.shape, -jnp.inf, F32)
        l_sc[...] = jnp.zeros_like(l_sc)
        acc_sc[...] = jnp.zeros_like(acc_sc)
        causal = lax.broadcasted_iota(jnp.int32, (tb, tb), 1) <= lax.broadcasted_iota(jnp.int32, (tb, tb), 0)

        def step(j, masked):
            c0 = pl.multiple_of(j * tb, tb)
            for p in range(N_PAIRS):
                vb = v_ref[pl.ds(c0, tb), p * LANES : (p + 1) * LANES]
                pv, al = [], []
                for hh in range(2):
                    h = 2 * p + hh
                    s = _mm_nt(qa_sc[h], ka_sc[h, pl.ds(c0, tb), :])
                    if masked:
                        s = jnp.where(causal, s, -jnp.inf)
                    m_old = m_sc[h]
                    m_new = jnp.maximum(m_old, jnp.max(s, axis=1, keepdims=True))
                    alpha = jnp.exp(m_old - m_new)
                    pe = jnp.exp(s - jnp.concatenate([m_new] * (tb // LANES), axis=1))
                    l_sc[h] = alpha * l_sc[h] + jnp.sum(pe, axis=1, keepdims=True)
                    m_sc[h] = m_new
                    pv.append(_mm(pe.astype(BF16), vb))
                    al.append(alpha)
                acc_sc[p] = jnp.where(low, al[0], al[1]) * acc_sc[p] + jnp.where(low, pv[0], pv[1])

        def loop_body(j, carry):
            step(j, False)
            return carry

        lax.fori_loop(0, i, loop_body, 0)
        step(i, True)
        st = jnp.zeros((tb, LANES), F32)
        for p in range(N_PAIRS):
            lp = jnp.where(low, l_sc[2 * p], l_sc[2 * p + 1])
            o_ref[:, p * LANES : (p + 1) * LANES] = (acc_sc[p] / lp).astype(BF16)
            for h in (2 * p, 2 * p + 1):
                st = jnp.where(lane == h, m_sc[h] + jnp.log(l_sc[h]), st)
        st_ref[...] = st

    return pl.pallas_call(
        body,
        name="attn_fwd",
        grid=(n_seq, nq),
        in_specs=[
            pl.BlockSpec((tb, ATTN_WIDTH), lambda s, i: (s * nq + i, 0)),
            pl.BlockSpec((S, ATTN_WIDTH), lambda s, i: (s, 1)),
            pl.BlockSpec((S, ATTN_WIDTH), lambda s, i: (s, 2)),
            pl.BlockSpec((S, LANES), lambda s, i: (s, 0)),
        ],
        out_specs=[
            pl.BlockSpec((tb, ATTN_WIDTH), lambda s, i: (s * nq + i, 0)),
            pl.BlockSpec((tb, LANES), lambda s, i: (s * nq + i, 0)),
        ],
        out_shape=[jax.ShapeDtypeStruct((T, ATTN_WIDTH), BF16), jax.ShapeDtypeStruct((T, LANES), F32)],
        scratch_shapes=[
            pltpu.VMEM((N_HEADS, tb, LANES), BF16),
            pltpu.VMEM((N_HEADS, S, LANES), BF16),
            pltpu.VMEM((N_HEADS, tb, LANES), F32),
            pltpu.VMEM((N_HEADS, tb, LANES), F32),
            pltpu.VMEM((N_PAIRS, tb, LANES), F32),
        ],
        compiler_params=_params(("parallel", "arbitrary")),
    )(qkv, qkv, qkv, fcol)


def _mix_out(a, p3, gates, x, w_ao, w_po, w_out):
    T = x.shape[0]
    tm = ROW_TILE

    def body(a_ref, p3_ref, gt_ref, x_ref, wao_ref, wpo_ref, wout_ref, mg_ref, x1_ref, ay_ref, py_ref):
        ay = _mm(a_ref[...], _whole_cols(wao_ref))
        py = _mm(p3_ref[...], _whole_cols(wpo_ref))
        ay_ref[...] = ay.astype(BF16)
        py_ref[...] = py.astype(BF16)
        sp = _sigmoid(gt_ref[:, :D_MODEL].astype(F32))
        sa = _sigmoid(gt_ref[:, D_MODEL:].astype(F32))
        mb = (sp * py + sa * ay).astype(BF16)
        mg_ref[...] = mb
        x1_ref[...] = x_ref[...] + _mm(mb, wout_ref[...])

    row = lambda n: pl.BlockSpec((tm, n), lambda i: (i, 0))
    return pl.pallas_call(
        body,
        name="mix_out",
        grid=(T // tm,),
        in_specs=[
            row(ATTN_WIDTH), row(POOL_WIDTH), row(2 * D_MODEL), row(D_MODEL),
            _const_spec(w_ao.shape), _const_spec(w_po.shape), _const_spec(w_out.shape),
        ],
        out_specs=[row(D_MODEL), row(D_MODEL), row(D_MODEL), row(D_MODEL)],
        out_shape=[
            jax.ShapeDtypeStruct((T, D_MODEL), BF16), jax.ShapeDtypeStruct((T, D_MODEL), F32),
            jax.ShapeDtypeStruct((T, D_MODEL), BF16), jax.ShapeDtypeStruct((T, D_MODEL), BF16),
        ],
        compiler_params=_params(("parallel",)),
    )(a, p3, gates, x, w_ao, w_po, w_out)


def _ffn_fwd(x1, g2, gf, tgt, w_gate_t, w_up_t, w_down):
    T = x1.shape[0]
    tm = min(T, FF_ROW_TILE)
    nt = T // tm
    nc = D_FF // FF_CHUNK

    def body(x1_ref, g2_ref, gf_ref, tg_ref, wg_ref, wu_ref, wd_ref, h2_ref, gate_ref, up_ref, act_ref, dx2_ref, loss_ref, dgf_ref):
        x1v = x1_ref[...]
        h2, _, _ = _rms_fwd(x1v, g2_ref[...])
        h2b = h2.astype(BF16)
        h2_ref[...] = h2b
        for c in range(nc):
            sl = slice(c * FF_CHUNK, (c + 1) * FF_CHUNK)
            gate = _mm_nt(h2b, wg_ref[sl, :])
            up = _mm_nt(h2b, wu_ref[sl, :])
            gate_ref[:, sl] = gate.astype(BF16)
            up_ref[:, sl] = up.astype(BF16)
            act_ref[:, sl] = (gate * _sigmoid(gate) * up).astype(BF16)
        acc = x1v + _mm(act_ref[...], wd_ref[...])
        gfv = gf_ref[...]
        y, xh, r = _rms_fwd(acc, gfv)
        err = y - tg_ref[...]
        part = 0.5 * jnp.sum(jnp.mean(err * err, axis=-1, keepdims=True), axis=0, keepdims=True)
        dx2, dgrow = _rms_bwd(err * (1.0 / D_MODEL), xh, r, gfv)
        dx2_ref[...] = dx2

        @pl.when(pl.program_id(0) == 0)
        def _():
            dgf_ref[...] = jnp.zeros_like(dgf_ref)
            loss_ref[...] = jnp.zeros_like(loss_ref)

        dgf_ref[...] += jnp.sum(dgrow, axis=0, keepdims=True)
        loss_ref[...] += jnp.broadcast_to(part, loss_ref.shape)

    row = lambda n: pl.BlockSpec((tm, n), lambda i: (i, 0))
    return pl.pallas_call(
        body,
        name="ffn_fwd",
        grid=(nt,),
        in_specs=[
            row(D_MODEL), _const_spec((1, D_MODEL)), _const_spec((1, D_MODEL)), row(D_MODEL),
            _const_spec(w_gate_t.shape), _const_spec(w_up_t.shape), _const_spec(w_down.shape),
        ],
        out_specs=[
            row(D_MODEL), row(D_FF), row(D_FF), row(D_FF), row(D_MODEL),
            pl.BlockSpec((8, LANES), lambda i: (0, 0)),
            pl.BlockSpec((1, D_MODEL), lambda i: (0, 0)),
        ],
        out_shape=[
            jax.ShapeDtypeStruct((T, D_MODEL), BF16),
            jax.ShapeDtypeStruct((T, D_FF), BF16),
            jax.ShapeDtypeStruct((T, D_FF), BF16),
            jax.ShapeDtypeStruct((T, D_FF), BF16),
            jax.ShapeDtypeStruct((T, D_MODEL), F32),
            jax.ShapeDtypeStruct((8, LANES), F32),
            jax.ShapeDtypeStruct((1, D_MODEL), F32),
        ],
        compiler_params=_params(("arbitrary",)),
    )(x1, g2, gf, tgt, w_gate_t, w_up_t, w_down)


def _ffn_bwd(dx2, gate, up, x1, g2, w_gate_t, w_up_t, w_down):
    T = x1.shape[0]
    tm = min(T, FF_ROW_TILE)
    nc = D_FF // FF_CHUNK

    def body(dx2_ref, gate_ref, up_ref, x1_ref, g2_ref, wg_ref, wu_ref, wd_ref, dgate_ref, dup_ref, dx1_ref, dg2_ref):
        dx2v = dx2_ref[...]
        dx2b = dx2v.astype(BF16)
        for c in range(nc):
            sl = slice(c * FF_CHUNK, (c + 1) * FF_CHUNK)
            dact = _mm_nt(dx2b, wd_ref[sl, :])
            gate = gate_ref[:, sl].astype(F32)
            sg = _sigmoid(gate)
            silu = gate * sg
            dgate = (dact * up_ref[:, sl].astype(F32) * (sg * (1.0 + gate * (1.0 - sg)))).astype(BF16)
            dup = (dact * silu).astype(BF16)
            dgate_ref[:, sl] = dgate
            dup_ref[:, sl] = dup
        dh2 = _mm(dgate_ref[...], wg_ref[...]) + _mm(dup_ref[...], wu_ref[...])
        g2v = g2_ref[...]
        _, xh, r = _rms_fwd(x1_ref[...], g2v)
        dxn, dgrow = _rms_bwd(dh2, xh, r, g2v)
        dx1_ref[...] = dx2v + dxn

        @pl.when(pl.program_id(0) == 0)
        def _():
            dg2_ref[...] = jnp.zeros_like(dg2_ref)

        dg2_ref[...] += jnp.sum(dgrow, axis=0, keepdims=True)

    row = lambda n: pl.BlockSpec((tm, n), lambda i: (i, 0))
    return pl.pallas_call(
        body,
        name="ffn_bwd",
        grid=(T // tm,),
        in_specs=[
            row(D_MODEL), row(D_FF), row(D_FF), row(D_MODEL), _const_spec((1, D_MODEL)),
            _const_spec(w_gate_t.shape), _const_spec(w_up_t.shape), _const_spec(w_down.shape),
        ],
        out_specs=[row(D_FF), row(D_FF), row(D_MODEL), pl.BlockSpec((1, D_MODEL), lambda i: (0, 0))],
        out_shape=[
            jax.ShapeDtypeStruct((T, D_FF), BF16),
            jax.ShapeDtypeStruct((T, D_FF), BF16),
            jax.ShapeDtypeStruct((T, D_MODEL), F32),
            jax.ShapeDtypeStruct((1, D_MODEL), F32),
        ],
        compiler_params=_params(("arbitrary",), VMEM_LIMIT_MAX),
    )(dx2, gate, up, x1, g2, w_gate_t, w_up_t, w_down)


def _mix_bwd(dx1, gates, pool_y, attn_y, p2, scale, w_out, w_ao, w_po, token):
    T = dx1.shape[0]
    tm = ROW_TILE

    def body(dx1_ref, gt_ref, py_ref, ay_ref, p2_ref, sc_ref, wout_ref, wao_ref, wpo_ref, token_ref, dgt_ref, dpy_ref, day_ref, da_ref, dp2_ref, dsc_ref):
        dm = _mm_nt(dx1_ref[...].astype(BF16), wout_ref[...])
        sp = _sigmoid(gt_ref[:, :D_MODEL].astype(F32))
        sa = _sigmoid(gt_ref[:, D_MODEL:].astype(F32))
        dgt_ref[:, :D_MODEL] = (dm * py_ref[...].astype(F32) * (sp * (1.0 - sp))).astype(BF16)
        dgt_ref[:, D_MODEL:] = (dm * ay_ref[...].astype(F32) * (sa * (1.0 - sa))).astype(BF16)
        dpy = (dm * sp).astype(BF16)
        day = (dm * sa).astype(BF16)
        dpy_ref[...] = dpy
        day_ref[...] = day
        da_ref[...] = _mm_nt(day, _whole_cols(wao_ref)).astype(BF16)
        dp3 = _mm_nt(dpy, _whole_cols(wpo_ref))
        dp2_ref[...] = (dp3 * sc_ref[...]).astype(BF16)

        @pl.when(pl.program_id(0) == 0)
        def _():
            dsc_ref[...] = jnp.zeros_like(dsc_ref)

        dsc_ref[...] += jnp.sum(dp3 * p2_ref[...], axis=0, keepdims=True)

    row = lambda n: pl.BlockSpec((tm, n), lambda i: (i, 0))
    return pl.pallas_call(
        body,
        name="mix_bwd",
        grid=(T // tm,),
        in_specs=[
            row(D_MODEL), row(2 * D_MODEL), row(D_MODEL), row(D_MODEL), row(POOL_WIDTH), _const_spec((1, POOL_WIDTH)),
            _const_spec(w_out.shape), _const_spec(w_ao.shape), _const_spec(w_po.shape), _HBM,
        ],
        out_specs=[row(2 * D_MODEL), row(D_MODEL), row(D_MODEL), row(ATTN_WIDTH), row(POOL_WIDTH), pl.BlockSpec((1, POOL_WIDTH), lambda i: (0, 0))],
        out_shape=[
            jax.ShapeDtypeStruct((T, 2 * D_MODEL), BF16),
            jax.ShapeDtypeStruct((T, D_MODEL), BF16),
            jax.ShapeDtypeStruct((T, D_MODEL), BF16),
            jax.ShapeDtypeStruct((T, ATTN_WIDTH), BF16),
            jax.ShapeDtypeStruct((T, POOL_WIDTH), BF16),
            jax.ShapeDtypeStruct((1, POOL_WIDTH), F32),
        ],
        compiler_params=_params(("arbitrary",)),
    )(dx1, gates, pool_y, attn_y, p2, scale, w_out, w_ao, w_po, token)


def _pool_bwd(dp2, pm, mix_b, token, n_seq, S):
    T = n_seq * S

    def body(dp2_ref, pm_ref, mix_ref, token_ref, du_ref, dmix_ref):
        g = pl.program_id(0)
        dp2v = dp2_ref[...]
        dpm = _mm_nt(dp2v, mix_ref[...])
        row = lax.broadcasted_iota(jnp.int32, dpm.shape, 0)
        w = _window_pick(g, 2.0, 4.0, 8.0, 16.0)
        e = dpm / jnp.minimum((row + 1).astype(F32), w)

        def ahead(a, k):
            return jnp.where(row < S - k, pltpu.roll(a, S - k, 0), 0.0)

        r2 = e + ahead(e, 1)
        r4 = r2 + ahead(r2, 2)
        r8 = r4 + ahead(r4, 4)
        r16 = r8 + ahead(r8, 8)
        du_ref[...] = (_window_pick(g, r2, r4, r8, r16) - dpm).astype(BF16)

        @pl.when(pl.program_id(1) == 0)
        def _():
            dmix_ref[...] = jnp.zeros_like(dmix_ref)

        dmix_ref[...] += _mm_tn(pm_ref[...], dp2v)

    grp = pl.BlockSpec((S, GROUP_DIM), lambda g, s: (s, g))
    mixs = pl.BlockSpec((None, GROUP_DIM, GROUP_DIM), lambda g, s: (g, 0, 0))
    return pl.pallas_call(
        body,
        name="pool_bwd",
        grid=(len(POOL_WINDOWS), n_seq),
        in_specs=[grp, grp, mixs, _HBM],
        out_specs=[grp, mixs],
        out_shape=[jax.ShapeDtypeStruct((T, POOL_WIDTH), BF16), jax.ShapeDtypeStruct((len(POOL_WINDOWS), GROUP_DIM, GROUP_DIM), F32)],
        compiler_params=_params(("parallel", "arbitrary")),
    )(dp2, pm, mix_b, token)


def _attn_bwd(qkv, da, a, fcol, lse, n_seq, S):
    T = n_seq * S
    tb = ATTN_BLOCK
    nb = S // tb
    scale = HEAD_DIM ** -0.5

    def body(q_ref, k_ref, v_ref, do_ref, o_ref, fc_ref, st_ref, dq_ref, dk_ref, dv_ref, dfk_ref, dfq_ref,
             qa_sc, doa_sc, qat_sc, doat_sc, dq_acc, ka_sc, va_sc, dkt_sc, dvt_sc):
        j = pl.program_id(1)
        lane = lax.broadcasted_iota(jnp.int32, (1, LANES), 1)
        low = lane < HEAD_DIM

        @pl.when(j == 0)
        def _():
            dq_acc[...] = jnp.zeros_like(dq_acc)
            place = _bias_placement(0)

            def rows_q(i, carry):
                r0 = pl.multiple_of(i * tb, tb)
                delta = jnp.zeros((tb, LANES), F32)
                for h in range(N_HEADS):
                    pair = slice((h // 2) * LANES, (h // 2 + 1) * LANES)
                    prod = do_ref[pl.ds(r0, tb), pair].astype(F32) * o_ref[pl.ds(r0, tb), pair].astype(F32)
                    head = (lane >= HEAD_DIM * (h % 2)) & (lane < HEAD_DIM * (h % 2 + 1))
                    delta = jnp.where(lane == h, jnp.sum(jnp.where(head, prod, 0.0), axis=1, keepdims=True), delta)
                cq = fc_ref[pl.ds(r0, tb), :] - st_ref[pl.ds(r0, tb), :]
                q_bias = _mm(_bias_lanes(cq), place).astype(BF16)
                do_bias = _mm(_bias_lanes(-delta), place).astype(BF16)
                for h in range(N_HEADS):
                    pair = slice((h // 2) * LANES, (h // 2 + 1) * LANES)
                    qa = _augment(q_ref[pl.ds(r0, tb), pair], h, q_bias, 1)
                    doa = _augment(do_ref[pl.ds(r0, tb), pair], h, do_bias, None)
                    qa_sc[h, pl.ds(r0, tb), :] = qa
                    doa_sc[h, pl.ds(r0, tb), :] = doa
                    qat_sc[h, i] = qa.astype(F32).T.astype(BF16)
                    doat_sc[h, i] = doa.astype(F32).T.astype(BF16)
                return carry

            lax.fori_loop(0, nb, rows_q, 0)

        c0 = pl.multiple_of(j * tb, tb)
        k_bias = _mm(_bias_lanes(-fc_ref[pl.ds(c0, tb), :]), _bias_placement(1)).astype(BF16)
        for h in range(N_HEADS):
            pair = slice((h // 2) * LANES, (h // 2 + 1) * LANES)
            ka_sc[h] = _augment(k_ref[:, pair] * scale, h, k_bias, 0)
            va_sc[h] = _augment(v_ref[:, pair], h, None, 0)
        dkt_sc[...] = jnp.zeros_like(dkt_sc)
        dvt_sc[...] = jnp.zeros_like(dvt_sc)
        causal = lax.broadcasted_iota(jnp.int32, (tb, tb), 1) <= lax.broadcasted_iota(jnp.int32, (tb, tb), 0)

        def step(i, masked):
            r0 = pl.multiple_of(i * tb, tb)
            for h in range(N_HEADS):
                s = _mm_nt(qa_sc[h, pl.ds(r0, tb), :], ka_sc[h])
                if masked:
                    s = jnp.where(causal, s, -jnp.inf)
                pr = jnp.exp(s)
                dvt_sc[h] += _mm(doat_sc[h, i], pr.astype(BF16))
                dsb = (pr * _mm_nt(doa_sc[h, pl.ds(r0, tb), :], va_sc[h])).astype(BF16)
                dkt_sc[h] += _mm(qat_sc[h, i], dsb)
                dq_acc[h, pl.ds(r0, tb), :] += _mm(dsb, ka_sc[h])

        step(j, True)

        def loop_body(i, carry):
            step(i, False)
            return carry

        lax.fori_loop(j + 1, nb, loop_body, 0)
        dfk = jnp.zeros((tb, LANES), F32)
        for p in range(N_PAIRS):
            dk = [dkt_sc[2 * p + hh].T for hh in range(2)]
            dv = [dvt_sc[2 * p + hh].T for hh in range(2)]
            dk_ref[:, p * LANES : (p + 1) * LANES] = (jnp.where(low, dk[0], dk[1]) * scale).astype(BF16)
            dv_ref[:, p * LANES : (p + 1) * LANES] = jnp.where(low, dv[0], dv[1]).astype(BF16)
            for hh in range(2):
                b = HEAD_DIM * (1 - hh) + 3
                dfk = jnp.where(lane == 2 * p + hh, -dk[hh][:, b : b + 1], dfk)
        dfk_ref[...] = dfk

        @pl.when(j == nb - 1)
        def _():
            def rows_dq(i, carry):
                r0 = pl.multiple_of(i * tb, tb)
                dfq = jnp.zeros((tb, LANES), F32)
                for p in range(N_PAIRS):
                    parts = [dq_acc[2 * p + hh, pl.ds(r0, tb), :] for hh in range(2)]
                    dq_ref[pl.ds(r0, tb), p * LANES : (p + 1) * LANES] = jnp.where(low, parts[0], parts[1]).astype(BF16)
                    for hh in range(2):
                        b = HEAD_DIM * (1 - hh)
                        dfq = jnp.where(lane == 2 * p + hh, parts[hh][:, b : b + 1], dfq)
                dfq_ref[pl.ds(r0, tb), :] = dfq
                return carry

            lax.fori_loop(0, nb, rows_dq, 0)

    seq = lambda w, col: pl.BlockSpec((S, w), lambda s, j: (s, col))
    seq_in = lambda w, col: pl.BlockSpec((S, w), lambda s, j: (s, col), pipeline_mode=pl.Buffered(1))
    blk = lambda w, col: pl.BlockSpec((tb, w), lambda s, j: (s * nb + j, col))
    return pl.pallas_call(
        body,
        name="attn_bwd",
        grid=(n_seq, nb),
        in_specs=[seq_in(ATTN_WIDTH, 0), blk(ATTN_WIDTH, 1), blk(ATTN_WIDTH, 2), seq_in(ATTN_WIDTH, 0), seq_in(ATTN_WIDTH, 0), seq_in(LANES, 0), seq_in(LANES, 0)],
        out_specs=[seq(ATTN_WIDTH, 0), blk(ATTN_WIDTH, 0), blk(ATTN_WIDTH, 0), blk(LANES, 0), seq(LANES, 0)],
        out_shape=[
            jax.ShapeDtypeStruct((T, ATTN_WIDTH), BF16),
            jax.ShapeDtypeStruct((T, ATTN_WIDTH), BF16),
            jax.ShapeDtypeStruct((T, ATTN_WIDTH), BF16),
            jax.ShapeDtypeStruct((T, LANES), F32),
            jax.ShapeDtypeStruct((T, LANES), F32),
        ],
        scratch_shapes=[
            pltpu.VMEM((N_HEADS, S, LANES), BF16),
            pltpu.VMEM((N_HEADS, S, LANES), BF16),
            pltpu.VMEM((N_HEADS, nb, LANES, tb), BF16),
            pltpu.VMEM((N_HEADS, nb, LANES, tb), BF16),
            pltpu.VMEM((N_HEADS, S, LANES), F32),
            pltpu.VMEM((N_HEADS, tb, LANES), BF16),
            pltpu.VMEM((N_HEADS, tb, LANES), BF16),
            pltpu.VMEM((N_HEADS, LANES, tb), F32),
            pltpu.VMEM((N_HEADS, LANES, tb), F32),
        ],
        compiler_params=_params(("parallel", "arbitrary"), VMEM_LIMIT_MAX),
    )(qkv, qkv, qkv, da, a, fcol, lse)


def _forget_bwd(dfk, dfq, fl, b_pad, n_seq, S):
    def body(df_ref, dfq_ref, fl_ref, b_ref, dfl_ref, db_ref):
        t = (df_ref[...] + dfq_ref[...]).T
        lane = lax.broadcasted_iota(jnp.int32, t.shape, 1)
        k = 1
        while k < S:
            t = t + jnp.where(lane < S - k, pltpu.roll(t, S - k, 1), 0.0)
            k *= 2
        dfl = t.T * _sigmoid(-(fl_ref[...] + b_ref[...]))
        dfl_ref[...] = dfl.astype(BF16)

        @pl.when(pl.program_id(0) == 0)
        def _():
            db_ref[...] = jnp.zeros_like(db_ref)

        db_ref[...] += jnp.sum(dfl, axis=0, keepdims=True)

    return pl.pallas_call(
        body,
        name="forget_bwd",
        grid=(n_seq,),
        in_specs=[
            pl.BlockSpec((S, LANES), lambda s: (s, 0)),
            pl.BlockSpec((S, LANES), lambda s: (s, 0)),
            pl.BlockSpec((S, FL_PAD), lambda s: (s, 0)),
            _const_spec((1, FL_PAD)),
        ],
        out_specs=[pl.BlockSpec((S, FL_PAD), lambda s: (s, 0)), pl.BlockSpec((1, FL_PAD), lambda s: (0, 0))],
        out_shape=[jax.ShapeDtypeStruct((n_seq * S, FL_PAD), BF16), jax.ShapeDtypeStruct((1, FL_PAD), F32)],
        compiler_params=_params(("arbitrary",)),
    )(dfk, dfq, fl, b_pad)


def _in_proj_bwd(du, dq, dk, dv, dfl, dgates, x, dx1, g1, w_uqkv, w_fl, w_g, token):
    T = x.shape[0]
    tm = ROW_TILE

    def body(du_ref, dq_ref, dk_ref, dv_ref, dfl_ref, dgt_ref, x_ref, dx1_ref, g_ref, wa_ref, wf_ref, wg_ref, token_ref, dx_ref, dg_ref):
        dz = jnp.concatenate([du_ref[...], dq_ref[...], dk_ref[...], dv_ref[...]], axis=1)
        dh = _mm_nt(dz, wa_ref[...]) + _mm_nt(dgt_ref[...], wg_ref[...]) + _mm_nt(dfl_ref[...], wf_ref[...])
        gv = g_ref[...]
        _, xh, r = _rms_fwd(x_ref[...], gv)
        dxn, dgrow = _rms_bwd(dh, xh, r, gv)
        dx_ref[...] = dx1_ref[...] + dxn

        @pl.when(pl.program_id(0) == 0)
        def _():
            dg_ref[...] = jnp.zeros_like(dg_ref)

        dg_ref[...] += jnp.sum(dgrow, axis=0, keepdims=True)

    row = lambda n: pl.BlockSpec((tm, n), lambda i: (i, 0))
    return pl.pallas_call(
        body,
        name="in_proj_bwd",
        grid=(T // tm,),
        in_specs=[
            row(512), row(512), row(512), row(512), row(FL_PAD), row(2 * D_MODEL), row(D_MODEL), row(D_MODEL), _const_spec((1, D_MODEL)),
            _const_spec(w_uqkv.shape), _const_spec(w_fl.shape), _const_spec(w_g.shape), _HBM,
        ],
        out_specs=[row(D_MODEL), pl.BlockSpec((1, D_MODEL), lambda i: (0, 0))],
        out_shape=[jax.ShapeDtypeStruct((T, D_MODEL), F32), jax.ShapeDtypeStruct((1, D_MODEL), F32)],
        compiler_params=_params(("arbitrary",)),
    )(du, dq, dk, dv, dfl, dgates, x, dx1, g1, w_uqkv, w_fl, w_g, token)


def _pick_block(n):
    for b in (1024, 512, 1408, 256, 128):
        if n % b == 0:
            return b
    raise ValueError(n)


def _matmul_tn(a, b, name, col_chunks=False):
    T, K = a.shape
    N = b.shape[1]
    bt, bk, bn = min(T, DW_TOKENS), _pick_block(K), _pick_block(N)
    nt = T // bt
    c = N // N_DEV
    assert not col_chunks or (bn == N and c % LANES == 0)

    def body(a_ref, b_ref, o_ref, acc):
        @pl.when(pl.program_id(2) == 0)
        def _():
            acc[...] = jnp.zeros_like(acc)

        acc[...] += _mm_tn(a_ref[...].astype(BF16), b_ref[...].astype(BF16))

        @pl.when(pl.program_id(2) == nt - 1)
        def _():
            if col_chunks:
                for d in range(N_DEV):
                    o_ref[d] = acc[:, d * c : (d + 1) * c].astype(BF16)
            else:
                o_ref[...] = acc[...].astype(BF16)

    if col_chunks:
        out_spec, out_shape = pl.BlockSpec((N_DEV, bk, c), lambda k, n, t: (0, k, 0)), (N_DEV, K, c)
    else:
        out_spec, out_shape = pl.BlockSpec((bk, bn), lambda k, n, t: (k, n)), (K, N)
    return pl.pallas_call(
        body,
        name=name,
        grid=(K // bk, N // bn, nt),
        in_specs=[pl.BlockSpec((bt, bk), lambda k, n, t: (t, k)), pl.BlockSpec((bt, bn), lambda k, n, t: (t, n))],
        out_specs=out_spec,
        out_shape=jax.ShapeDtypeStruct(out_shape, BF16),
        scratch_shapes=[pltpu.VMEM((bk, bn), F32)],
        compiler_params=_params(("parallel", "parallel", "arbitrary")),
    )(a, b)


W_IN_A = POOL_WIDTH + 3 * ATTN_WIDTH
W_IN_SHARD = (W_IN_A + N_HEADS + 2 * D_MODEL) // N_DEV
_W_IN_PIECES = ((0, W_IN_A), (W_IN_A, W_IN_A + N_HEADS), (W_IN_A + N_HEADS, W_IN_A + N_HEADS + 2 * D_MODEL))


def _w_in_segments(d):
    lo, hi = d * W_IN_SHARD, (d + 1) * W_IN_SHARD
    out = []
    for p, (a, b) in enumerate(_W_IN_PIECES):
        s, e = max(lo, a), min(hi, b)
        if s < e:
            out.append((p, s - a, s - lo, e - s))
    return out


def _w_in_pieces(gathered):
    tm = ROW_TILE // 2

    def body(g_ref, wa_ref, wf_ref, wg_ref):
        outs = (wa_ref, wf_ref, wg_ref)
        wf_ref[...] = jnp.zeros_like(wf_ref)
        for d in range(N_DEV):
            for p, at, frm, n in _w_in_segments(d):
                outs[p][:, at : at + n] = g_ref[d, :, frm : frm + n]

    return pl.pallas_call(
        body,
        name="w_in_pieces",
        grid=(D_MODEL // tm,),
        in_specs=[pl.BlockSpec((N_DEV, tm, W_IN_SHARD), lambda i: (0, i, 0))],
        out_specs=[pl.BlockSpec((tm, W_IN_A), lambda i: (i, 0)), pl.BlockSpec((tm, FL_PAD), lambda i: (i, 0)), pl.BlockSpec((tm, 2 * D_MODEL), lambda i: (i, 0))],
        out_shape=[
            jax.ShapeDtypeStruct((D_MODEL, W_IN_A), gathered.dtype),
            jax.ShapeDtypeStruct((D_MODEL, FL_PAD), gathered.dtype),
            jax.ShapeDtypeStruct((D_MODEL, 2 * D_MODEL), gathered.dtype),
        ],
        compiler_params=_params(("parallel",)),
    )(gathered)


def _dw_in(h, du, dq, dk, dv, dfl, dgates, token):
    T = h.shape[0]
    bt, bk = min(T, DW_TOKENS // 2), 512
    nt = T // bt
    pieces = (du, dq, dk, dv, dfl, dgates)
    offs = [0]
    for p in pieces:
        offs.append(offs[-1] + p.shape[1])

    def body(h_ref, *rest):
        refs, o_ref, acc = rest[: len(pieces)], rest[-2], rest[-1]

        @pl.when(pl.program_id(1) == 0)
        def _():
            acc[...] = jnp.zeros_like(acc)

        ht = h_ref[...].T
        for ref, at in zip(refs, offs):
            acc[:, at : at + ref.shape[1]] += _mm(ht, ref[...])

        @pl.when(pl.program_id(1) == nt - 1)
        def _():
            starts = (0, W_IN_A, W_IN_A + FL_PAD)
            for d in range(N_DEV):
                for p, at, to, n in _w_in_segments(d):
                    o_ref[d, :, to : to + n] = acc[:, starts[p] + at : starts[p] + at + n].astype(BF16)

    return pl.pallas_call(
        body,
        name="dw_in",
        grid=(D_MODEL // bk, nt),
        in_specs=[pl.BlockSpec((bt, bk), lambda k, t: (t, k))] + [pl.BlockSpec((bt, p.shape[1]), lambda k, t: (t, 0)) for p in pieces] + [_HBM],
        out_specs=pl.BlockSpec((N_DEV, bk, W_IN_SHARD), lambda k, t: (0, k, 0)),
        out_shape=jax.ShapeDtypeStruct((N_DEV, D_MODEL, W_IN_SHARD), BF16),
        scratch_shapes=[pltpu.VMEM((bk, offs[-1]), F32)],
        compiler_params=_params(("parallel", "arbitrary")),
    )(h, *pieces, token)


def _position():
    return lax.axis_index("x"), lax.axis_index("y"), lax.axis_index("c")


_HBM = pl.BlockSpec(memory_space=pl.ANY)


def _all_gather(blocks, name):
    n = len(blocks)

    def body(*refs):
        xs, outs = refs[:n], refs[n : 2 * n]
        send_sems, recv_sems, local_sems = refs[2 * n :]
        x, y, c = _position()
        me, sibling = (x, y, c), (x, y, 1 - c)
        chips = [(1 - x, y), (x, 1 - y), (1 - x, 1 - y)]

        def rows(a, px, py, pc):
            return outs[a].at[4 * px + 2 * py + pc]

        def copy(a, k, blk, to, src=None):
            return pltpu.make_async_remote_copy(
                src_ref=rows(a, *blk) if src is None else src, dst_ref=rows(a, *blk),
                send_sem=send_sems.at[7 * a + k], recv_sem=recv_sems.at[7 * a + k], device_id=to, device_id_type=MESH,
            )

        first = []
        for a in range(n):
            first += [copy(a, 1 + j, me, (*chip, c), src=xs[a]) for j, chip in enumerate(chips)]
            first.append(copy(a, 0, me, sibling, src=xs[a]))
        mine = [pltpu.make_async_copy(xs[a], rows(a, *me), local_sems.at[a]) for a in range(n)]
        for cp in first + mine:
            cp.start()
        passed = []
        for j, chip in enumerate(chips):
            for a in range(n):
                copy(a, 1 + j, (*chip, c), me).wait_recv()
                passed.append(copy(a, 4 + j, (*chip, c), sibling))
                passed[-1].start()
        for a in range(n):
            copy(a, 0, sibling, me).wait_recv()
        for j, chip in enumerate(chips):
            for a in range(n):
                copy(a, 4 + j, (*chip, 1 - c), me).wait_recv()
        for cp in first + passed:
            cp.wait_send()
        for cp in mine:
            cp.wait()

    return pl.pallas_call(
        body,
        name=name,
        out_shape=[jax.ShapeDtypeStruct((N_DEV, *b.shape), b.dtype) for b in blocks],
        in_specs=[_HBM] * n,
        out_specs=[_HBM] * n,
        scratch_shapes=[pltpu.SemaphoreType.DMA((7 * n,)), pltpu.SemaphoreType.DMA((7 * n,)), pltpu.SemaphoreType.DMA((n,))],
    )(*blocks)


_SEM = pl.BlockSpec(memory_space=pltpu.SEMAPHORE)
_HBM_ONLY = pl.BlockSpec(memory_space=pltpu.HBM)
_SIDE_EFFECT = pltpu.SideEffectType.DATAFLOW_SIDE_EFFECTING


def _peer(x, y, c, k):
    return (1 - x if k & 4 else x, 1 - y if k & 2 else y, 1 - c if k & 1 else c)


_PEER_BITS = {"gather": range(1, N_DEV), "gather_half": (1, 4, 2, 6), "forward": (4, 2, 6), "scatter": range(1, N_DEV)}
_GATHERS = ("gather", "gather_half")


def _exchange_copies(src_refs, land_refs, send_sems, recv_sems, pattern, receive_side):
    x, y, c = _position()
    me = 4 * x + 2 * y + c
    bits = _PEER_BITS[pattern]
    cps = []
    for j, k in enumerate(bits):
        px, py, pc = _peer(x, y, c, k)
        peer = 4 * px + 2 * py + pc
        for a, (src, land) in enumerate(zip(src_refs, land_refs)):
            to = (px, py, pc)
            if pattern == "forward":
                slot = 4 * px + 2 * py + (1 - c if receive_side else c)
                s, to = land.at[slot], (x, y, 1 - c)
            else:
                s, slot = (src if pattern in _GATHERS else src.at[peer]), (peer if receive_side else me)
            cps.append(pltpu.make_async_remote_copy(
                src_ref=s, dst_ref=land.at[slot],
                send_sem=send_sems.at[len(bits) * a + j], recv_sem=recv_sems.at[len(bits) * a + j],
                device_id=to, device_id_type=MESH,
            ))
    return cps


def _own_copies(src_refs, land_refs, own_sems):
    x, y, c = _position()
    return [
        pltpu.make_async_copy(src, land.at[4 * x + 2 * y + c], own_sems.at[a])
        for a, (src, land) in enumerate(zip(src_refs, land_refs))
    ]


def _exchange_start(srcs, after, name, pattern):
    n = len(srcs)
    m = len(_PEER_BITS[pattern])
    lands = [jax.ShapeDtypeStruct((N_DEV, *s.shape[-2:]), s.dtype) for s in srcs]

    def body(*refs):
        src_refs, land_refs = refs[1 : 1 + n], refs[1 + n : 1 + 2 * n]
        send_sems, recv_sems, own_sems = refs[1 + 2 * n : 4 + 2 * n]
        token = refs[-1]
        if pattern in _GATHERS:
            for cp in _own_copies(src_refs, land_refs, own_sems):
                cp.start()
        for cp in _exchange_copies(src_refs, land_refs, send_sems, recv_sems, pattern, receive_side=False):
            cp.start()
        token[...] = jnp.zeros_like(token)

    hbm = lambda t: pltpu.with_memory_space_constraint(t, pltpu.HBM)
    out = pl.pallas_call(
        body,
        name=name,
        out_shape=(
            pltpu.SemaphoreType.DMA((m * n,)), pltpu.SemaphoreType.DMA((m * n,)), pltpu.SemaphoreType.DMA((n,)),
            *[pltpu.HBM(s.shape, s.dtype) for s in srcs], *[pltpu.HBM(l.shape, l.dtype) for l in lands],
            jax.ShapeDtypeStruct((8, LANES), F32),
        ),
        in_specs=(_HBM, *[_HBM_ONLY] * (2 * n)),
        out_specs=(_SEM, _SEM, _SEM, *[_HBM_ONLY] * (2 * n), pl.BlockSpec(memory_space=pltpu.VMEM)),
        input_output_aliases={1 + i: 3 + i for i in range(2 * n)},
        compiler_params=pltpu.CompilerParams(has_side_effects=_SIDE_EFFECT),
    )(after, *[hbm(s) for s in srcs], *[hbm(lax.empty(l.shape, l.dtype)) for l in lands])
    return out[:3], out[3 : 3 + n], out[3 + n : 3 + 2 * n], out[-1]


def _exchange_wait(sems, srcs, lands, after, name, pattern):
    n = len(srcs)

    def body(*refs):
        src_refs, land_refs = refs[:n], refs[n : 2 * n]
        send_sems, recv_sems, own_sems = refs[2 * n : 2 * n + 3]
        if pattern in _GATHERS:
            for cp in _own_copies(src_refs, land_refs, own_sems):
                cp.wait()
        for cp in _exchange_copies(src_refs, land_refs, send_sems, recv_sems, pattern, receive_side=True):
            cp.wait_send()
            cp.wait_recv()

    out = pl.pallas_call(
        body,
        name=name,
        out_shape=(*[pltpu.HBM(s.shape, s.dtype) for s in srcs], *[pltpu.HBM(l.shape, l.dtype) for l in lands]),
        in_specs=(*[_HBM_ONLY] * (2 * n), _SEM, _SEM, _SEM, _HBM),
        out_specs=tuple([_HBM_ONLY] * (2 * n)),
        input_output_aliases={i: i for i in range(2 * n)},
        compiler_params=pltpu.CompilerParams(has_side_effects=_SIDE_EFFECT),
    )(*srcs, *lands, *sems, after)
    return out[:n], out[n:]


def _gather_forward(sems, srcs, lands, after, name):
    n = len(srcs)
    m = len(_PEER_BITS["forward"])

    def body(*refs):
        src_refs, land_refs = refs[:n], refs[n : 2 * n]
        send_sems, recv_sems, own_sems = refs[2 * n : 2 * n + 3]
        forward_send, forward_recv, token = refs[2 * n + 4], refs[2 * n + 5], refs[-1]
        for cp in _own_copies(src_refs, land_refs, own_sems):
            cp.wait()
        for cp in _exchange_copies(src_refs, land_refs, send_sems, recv_sems, "gather_half", receive_side=True):
            cp.wait_send()
            cp.wait_recv()
        for cp in _exchange_copies(land_refs, land_refs, forward_send, forward_recv, "forward", receive_side=False):
            cp.start()
        token[...] = jnp.zeros_like(token)

    out = pl.pallas_call(
        body,
        name=name,
        out_shape=(
            pltpu.SemaphoreType.DMA((m * n,)), pltpu.SemaphoreType.DMA((m * n,)),
            *[pltpu.HBM(l.shape, l.dtype) for l in lands], jax.ShapeDtypeStruct((8, LANES), F32),
        ),
        in_specs=(*[_HBM_ONLY] * (2 * n), _SEM, _SEM, _SEM, _HBM),
        out_specs=(_SEM, _SEM, *[_HBM_ONLY] * n, pl.BlockSpec(memory_space=pltpu.VMEM)),
        input_output_aliases={n + i: 2 + i for i in range(n)},
        compiler_params=pltpu.CompilerParams(has_side_effects=_SIDE_EFFECT),
    )(*srcs, *lands, *sems, after)
    return out[:2], out[2 : 2 + n], out[-1]


def _forward_wait(sems, lands, after, name):
    n = len(lands)

    def body(*refs):
        land_refs = refs[:n]
        for cp in _exchange_copies(land_refs, land_refs, refs[n], refs[n + 1], "forward", receive_side=True):
            cp.wait_send()
            cp.wait_recv()

    return pl.pallas_call(
        body,
        name=name,
        out_shape=tuple(pltpu.HBM(l.shape, l.dtype) for l in lands),
        in_specs=(*[_HBM_ONLY] * n, _SEM, _SEM, _HBM),
        out_specs=tuple([_HBM_ONLY] * n),
        input_output_aliases={i: i for i in range(n)},
        compiler_params=pltpu.CompilerParams(has_side_effects=_SIDE_EFFECT),
    )(*lands, *sems, after)


def _rows_tile(r):
    return ROW_TILE if r % ROW_TILE == 0 else r


def _adamw(w, g, m, v):
    m = ADAM_B1 * m + (1.0 - ADAM_B1) * g
    v = ADAM_B2 * v + (1.0 - ADAM_B2) * (g * g)
    m_hat = m / (1.0 - ADAM_B1 ** ADAM_STEP)
    v_hat = v / (1.0 - ADAM_B2 ** ADAM_STEP)
    delta = -ADAM_LR * (m_hat / (jnp.sqrt(v_hat) + ADAM_EPS) + ADAM_WD * w)
    return delta, m, v


def _shard_update_direct(parts, chunks, w, m, v, me, name):
    _, r, c = w.shape
    br = _rows_tile(r)

    def body(me_ref, p_ref, own_ref, w_ref, m_ref, v_ref, g_ref, d_ref, nm_ref, nv_ref):
        g = None
        for n in range(N_DEV):
            part = jnp.where(me_ref[0] == n, own_ref[...], p_ref[n]).astype(F32)
            g = part if g is None else g + part
        g_ref[...] = g
        d_ref[...], nm_ref[...], nv_ref[...] = _adamw(w_ref[...], g, m_ref[...], v_ref[...])

    shard = pl.BlockSpec((None, br, c), lambda i, me: (0, i, 0))
    return pl.pallas_call(
        body,
        name=name,
        grid_spec=pltpu.PrefetchScalarGridSpec(
            num_scalar_prefetch=1,
            grid=(r // br,),
            in_specs=[
                pl.BlockSpec((N_DEV, br, c), lambda i, me: (0, i, 0)),
                pl.BlockSpec((None, br, c), lambda i, me: (me[0], i, 0)),
                shard, shard, shard,
            ],
            out_specs=[shard, shard, shard, shard],
        ),
        out_shape=[jax.ShapeDtypeStruct((1, r, c), F32)] * 4,
        compiler_params=_params(("parallel",)),
    )(me, parts, chunks, w, m, v)


def _small_update(parts, first_rows, ws, ms, vs):
    k = len(ws)

    def unpacked(rows, shape):
        if len(shape) == 2 and shape[1] <= LANES:
            return rows[0:1, : shape[1]]
        if len(shape) == 2:
            return jnp.concatenate([rows[r : r + 1] for r in range(shape[1] // LANES)], axis=1)
        return rows.reshape(shape)

    def body(p_ref, f_ref, *refs):
        w_refs, m_refs, v_refs = refs[:k], refs[k : 2 * k], refs[2 * k : 3 * k]
        outs, loss_ref = refs[3 * k : 7 * k], refs[7 * k]
        g, first = p_ref[0], f_ref[0]
        for n in range(1, N_DEV):
            g = g + p_ref[n]
            first = first + f_ref[n]
        g = jnp.concatenate([g[:8] + first, g[8:]], axis=0)
        off = 0
        for i, (_, rows) in enumerate(_SMALL):
            gi = unpacked(g[off : off + rows], w_refs[i].shape)
            off += rows
            outs[i][...] = gi
            outs[k + i][...], outs[2 * k + i][...], outs[3 * k + i][...] = _adamw(w_refs[i][...], gi, m_refs[i][...], v_refs[i][...])
        loss_ref[...] = g[off : off + 1, 0:1]

    out = pl.pallas_call(
        body,
        name="small_update",
        out_shape=[jax.ShapeDtypeStruct(w.shape, F32) for _ in range(4) for w in ws] + [jax.ShapeDtypeStruct((1, 1), F32)],
        compiler_params=pltpu.CompilerParams(vmem_limit_bytes=VMEM_LIMIT),
    )(parts, first_rows, *ws, *ms, *vs)
    return [out[a * k : (a + 1) * k] for a in range(4)], out[4 * k]


_SHARD_AXIS = (1, 1, 1, 0, 0, 0, 0)
_TRANSPOSED = (False, False, False, False, True, True, False)


def _full_from_gathered(t, axis):
    if axis == 0:
        return t.reshape(N_DEV * t.shape[1], t.shape[2])
    return t


_SMALL = (("norm1_g", 8), ("norm2_g", 8), ("norm_f_g", 8), ("b_forget", 8), ("pool_scale", 8), ("pool_mix", 512))


def _pack_small(vals, loss_row):
    parts = []
    for (name, rows), t in zip(_SMALL, vals):
        f = t.astype(F32).reshape(-1)
        f = jnp.concatenate([f, jnp.zeros((rows * LANES - f.shape[0],), F32)]).reshape(rows, LANES)
        parts.append(f)
    parts.append(loss_row)
    return jnp.concatenate(parts, axis=0)


def _local_grads(x, tgt, g1, g2, gf, b_forget, pool_mix, pool_scale, w_in, fwd_token, out_weights, ffn_weights, ffn_grads_out, out_grads_out, small_grads_out, in_grads_out, norm1_grad_out):
    n_seq, S, _ = x.shape
    T = n_seq * S
    x2 = x.reshape(T, D_MODEL)
    tg2 = tgt.reshape(T, D_MODEL)
    w_uqkv, w_fl, w_g = w_in
    b_pad = jnp.concatenate([b_forget.reshape(1, N_HEADS), jnp.zeros((1, FL_PAD - N_HEADS), F32)], axis=1)
    mix_b = pool_mix.reshape(len(POOL_WINDOWS), GROUP_DIM, GROUP_DIM).astype(BF16)
    scale = pool_scale.reshape(1, POOL_WIDTH)
    g1 = g1.reshape(1, D_MODEL)
    g2 = g2.reshape(1, D_MODEL)
    gf = gf.reshape(1, D_MODEL)

    h, u, qkv, fl, gates = _in_proj(x2, g1, w_uqkv, w_fl, w_g, fwd_token)
    fcol = _forget_fwd(fl, b_pad, n_seq, S)
    pm, p2, p3 = _pool_fwd(u, mix_b, scale, n_seq, S)
    a, lse = _attn_fwd(qkv, fcol, n_seq, S)
    w_po, w_ao, w_out = out_weights(a)
    merged, x1, attn_y, pool_y = _mix_out(a, p3, gates, x2, w_ao, w_po, w_out)
    w_gate_t, w_up_t, w_down = ffn_weights(x1)
    h2, gate, up, act, dx2, loss_rows, dgf = _ffn_fwd(x1, g2, gf, tg2, w_gate_t, w_up_t, w_down)

    dgate, dup, dx1, dg2 = _ffn_bwd(dx2, gate, up, x1, g2, w_gate_t, w_up_t, w_down)
    bwd_token = ffn_grads_out(_matmul_tn(dgate, h2, "dw_ffn_gate"), _matmul_tn(dup, h2, "dw_ffn_up"), _matmul_tn(act, dx2, "dw_ffn_down"))
    dgates, dpy, day, da, dp2, dscale = _mix_bwd(dx1, gates, pool_y, attn_y, p2, scale, w_out, w_ao, w_po, bwd_token)
    out_token = out_grads_out(
        _matmul_tn(p3, dpy, "dw_pool_out", col_chunks=True), _matmul_tn(a, day, "dw_attn_out", col_chunks=True), _matmul_tn(merged, dx1, "dw_out")
    )
    du, dmix = _pool_bwd(dp2, pm, mix_b, out_token, n_seq, S)
    dq, dk, dv, dfk, dfq = _attn_bwd(qkv, da, a, fcol, lse, n_seq, S)
    dfl, db = _forget_bwd(dfk, dfq, fl, b_pad, n_seq, S)
    small_token = small_grads_out((jnp.zeros_like(g1), dg2, dgf, db[:, :N_HEADS], dscale, dmix), loss_rows)
    in_token = in_grads_out(_dw_in(h, du, dq, dk, dv, dfl, dgates, small_token))
    dx, dg1 = _in_proj_bwd(du, dq, dk, dv, dfl, dgates, x2, dx1, g1, w_uqkv, w_fl, w_g, in_token)
    norm1_grad_out(dg1)
    return dx.reshape(n_seq, S, D_MODEL)


def kernel(x, norm1_g, w_in, b_forget, pool_mix, pool_scale, w_pool_out, w_attn_out, w_out, norm2_g, w_ffn_gate, w_ffn_up, w_ffn_down, norm_f_g, loss_target, m_norm1_g, m_w_in, m_b_forget, m_pool_mix, m_pool_scale, m_w_pool_out, m_w_attn_out, m_w_out, m_norm2_g, m_w_ffn_gate, m_w_ffn_up, m_w_ffn_down, m_norm_f_g, v_norm1_g, v_w_in, v_b_forget, v_pool_mix, v_pool_scale, v_w_pool_out, v_w_attn_out, v_w_out, v_norm2_g, v_w_ffn_gate, v_w_ffn_up, v_w_ffn_down, v_norm_f_g):
    names = ("w_in", "w_pool_out", "w_attn_out", "w_out", "w_ffn_gate", "w_ffn_up", "w_ffn_down")
    w_sh = (w_in, w_pool_out, w_attn_out, w_out, w_ffn_gate, w_ffn_up, w_ffn_down)
    m_sh = (m_w_in, m_w_pool_out, m_w_attn_out, m_w_out, m_w_ffn_gate, m_w_ffn_up, m_w_ffn_down)
    v_sh = (v_w_in, v_w_pool_out, v_w_attn_out, v_w_out, v_w_ffn_gate, v_w_ffn_up, v_w_ffn_down)

    cx, cy, cc = _position()
    me = 4 * cx + 2 * cy + cc
    def stored(t, transposed):
        return jnp.transpose(t, (0, 2, 1)) if transposed else t

    w_sh, m_sh, v_sh = ([stored(t, tr) for t, tr in zip(ts, _TRANSPOSED)] for ts in (w_sh, m_sh, v_sh))
    shards = [w[0].astype(BF16) for w in w_sh]
    (gathered_in,) = _all_gather(shards[:1], "w_in_all_gather")
    out_sems = _exchange_start(shards[1:4], gathered_in, "out_weights_gather_start", "gather")
    ffn_sems = _exchange_start(shards[4:], out_sems[3], "ffn_weights_gather_start", "gather_half")
    no_order = jnp.zeros((8, LANES), F32)
    started = {}

    def out_weights(after):
        forward_sems, lands, token = _gather_forward(*ffn_sems[:3], after, "ffn_weights_forward_start")
        started["forward"] = (forward_sems, lands)
        _, lands = _exchange_wait(*out_sems[:3], token, "out_weights_gather_wait", "gather")
        return [_full_from_gathered(t, axis) for t, axis in zip(lands, _SHARD_AXIS[out])]

    def ffn_weights(after):
        lands = _forward_wait(*started["forward"], after, "ffn_weights_gather_wait")
        return [_full_from_gathered(t, axis) for t, axis in zip(lands, _SHARD_AXIS[ffn])]

    def hold_ffn_grads(*whole_grads):
        started["held"] = whole_grads
        return no_order

    def scatter_grads(*out_grads):
        chunks = [
            t if axis == 1 else t.reshape(N_DEV, -1, t.shape[1])
            for t, axis in zip((*out_grads, *started["held"]), _SHARD_AXIS[scattered])
        ]
        started["scatter"] = _exchange_start(chunks, no_order, "grads_scatter_start", "scatter")
        return started["scatter"][3]

    def gather_small(small, loss_rows):
        started["small"] = _exchange_start([_pack_small(small, loss_rows)], no_order, "small_grads_gather_start", "gather")
        return started["small"][3]

    def scatter_w_in(chunks_in):
        started["in"] = _exchange_start([chunks_in], no_order, "w_in_grads_scatter_start", "scatter")
        return started["in"][3]

    def gather_norm1(dg1):
        rows = jnp.reshape(dg1, (8, LANES))
        started["norm1"] = _exchange_start([rows], no_order, "norm1_grad_gather_start", "gather")

    ffn, out, scattered = slice(4, 7), slice(1, 4), slice(1, 7)
    grad_x = _local_grads(
        x, loss_target, norm1_g, norm2_g, norm_f_g, b_forget, pool_mix, pool_scale, _w_in_pieces(gathered_in), ffn_sems[3],
        out_weights, ffn_weights, hold_ffn_grads, scatter_grads, gather_small, scatter_w_in, gather_norm1,
    )
    me_index = jnp.reshape(me, (1,)).astype(jnp.int32)

    srcs, lands = _exchange_wait(*started["scatter"][:3], started["norm1"][3], "grads_scatter_wait", "scatter")
    updates = [
        _shard_update_direct(p, s, w, m, v, me_index, "update_" + n)
        for p, s, w, m, v, n in zip(lands, srcs, w_sh[scattered], m_sh[scattered], v_sh[scattered], names[scattered])
    ]
    updates_out, updates_ffn = updates[:3], updates[3:]

    small_w = (norm1_g, norm2_g, norm_f_g, b_forget, pool_scale, pool_mix)
    small_m = (m_norm1_g, m_norm2_g, m_norm_f_g, m_b_forget, m_pool_scale, m_pool_mix)
    small_v = (v_norm1_g, v_norm2_g, v_norm_f_g, v_b_forget, v_pool_scale, v_pool_mix)
    (sent_in,), (parts_in,) = _exchange_wait(*started["in"][:3], updates_ffn[-1][0], "w_in_grads_scatter_wait", "scatter")
    update_in = _shard_update_direct(parts_in, sent_in, w_in, m_w_in, v_w_in, me_index, "update_w_in")

    def gathered_small(key, after, name):
        _, lands = _exchange_wait(*started[key][:3], after, name, "gather")
        return lands[0]

    parts = gathered_small("small", update_in[0], "small_grads_gather_wait")
    first_rows = gathered_small("norm1", parts, "norm1_grad_gather_wait")
    (g_s, d_s, nm_s, nv_s), loss = _small_update(parts, first_rows, small_w, small_m, small_v)
    g_w, d_w, nm_w, nv_w = zip(*(
        [stored(t, tr) for t in u] for u, tr in zip([update_in] + updates_out + updates_ffn, _TRANSPOSED)
    ))
    loss = loss.reshape(())
    (g1, g2, gf, gb, gsc, gmix), (d1, d2, df, db_, dsc, dmx) = g_s, d_s
    (m1, m2, mf, mb, msc, mmx), (v1, v2, vf, vb, vsc, vmx) = nm_s, nv_s

    def ordered(n1, win, b, mix, sc, wpo, wao, wout, n2, wg, wu, wd, nf):
        return (n1, win, b, mix, sc, wpo, wao, wout, n2, wg, wu, wd, nf)

    grads = ordered(g1, g_w[0], gb, gmix, gsc, g_w[1], g_w[2], g_w[3], g2, g_w[4], g_w[5], g_w[6], gf)
    deltas = ordered(d1, d_w[0], db_, dmx, dsc, d_w[1], d_w[2], d_w[3], d2, d_w[4], d_w[5], d_w[6], df)
    new_m = ordered(m1, nm_w[0], mb, mmx, msc, nm_w[1], nm_w[2], nm_w[3], m2, nm_w[4], nm_w[5], nm_w[6], mf)
    new_v = ordered(v1, nv_w[0], vb, vmx, vsc, nv_w[1], nv_w[2], nv_w[3], v2, nv_w[4], nv_w[5], nv_w[6], vf)
    return (loss, grad_x, *grads, *deltas, *new_m, *new_v)
```

```python
import jax
import jax.numpy as jnp
from jax import lax
from jax.experimental import pallas as pl
from jax.experimental.pallas import tpu as pltpu

F32 = jnp.float32
BF16 = jnp.bfloat16
MESH = pl.DeviceIdType.MESH

D_MODEL = 1024
POOL_WINDOWS = (2, 4, 8, 16)
POOL_WIDTH = 512
GROUP_DIM = 128
ATTN_WIDTH = 512
HEAD_DIM = 64
N_HEADS = 8
N_PAIRS = 4
D_FF = 2816
RMS_EPS = 1e-6
N_DEV = 8
LANES = 128
FL_PAD = 128

ADAM_LR = 0.001
ADAM_B1 = 0.9
ADAM_B2 = 0.999
ADAM_EPS = 1e-08
ADAM_WD = 0.01
ADAM_STEP = 10

VMEM_LIMIT = 56 * 1024 * 1024
VMEM_LIMIT_MAX = 60 * 1024 * 1024
ROW_TILE = 512
ATTN_BLOCK = 512
FF_CHUNK = 256
FF_ROW_TILE = 512
DW_TOKENS = 2048


def _mm(a, b):
    return jnp.dot(a, b, preferred_element_type=F32)


def _mm_nt(a, b):
    return lax.dot_general(a, b, (((1,), (1,)), ((), ())), preferred_element_type=F32)


def _mm_tn(a, b):
    return lax.dot_general(a, b, (((0,), (0,)), ((), ())), preferred_element_type=F32)


def _whole_cols(w_ref):
    if len(w_ref.shape) == 2:
        return w_ref[...]
    return jnp.concatenate([w_ref[d] for d in range(w_ref.shape[0])], axis=1)


def _sigmoid(x):
    return 1.0 / (1.0 + jnp.exp(-x))


def _params(sem, vmem=VMEM_LIMIT):
    return pltpu.CompilerParams(dimension_semantics=sem, vmem_limit_bytes=vmem)


def _const_spec(shape):
    nd = len(shape)
    return pl.BlockSpec(shape, lambda *_: (0,) * nd, pipeline_mode=pl.Buffered(1))


def _rms_fwd(x, g):
    r = lax.rsqrt(jnp.mean(x * x, axis=-1, keepdims=True) + RMS_EPS)
    xh = x * r
    return xh * g, xh, r


def _rms_bwd(dy, xh, r, g):
    dxh = dy * g
    dx = r * (dxh - xh * jnp.mean(dxh * xh, axis=-1, keepdims=True))
    return dx, dy * xh


def _in_proj(x, g1, w_uqkv, w_fl, w_g, token):
    T = x.shape[0]
    tm = ROW_TILE

    def body(x_ref, g_ref, wa_ref, wf_ref, wg_ref, token_ref, h_ref, u_ref, qkv_ref, fl_ref, gt_ref):
        h, _, _ = _rms_fwd(x_ref[...], g_ref[...])
        hb = h.astype(BF16)
        h_ref[...] = hb
        z = _mm(hb, wa_ref[...])
        u_ref[...] = z[:, :POOL_WIDTH]
        qkv_ref[...] = z[:, POOL_WIDTH:].astype(BF16)
        fl_ref[...] = _mm(hb, wf_ref[...])
        gt_ref[...] = _mm(hb, wg_ref[...]).astype(BF16)

    row = lambda n: pl.BlockSpec((tm, n), lambda i: (i, 0))
    return pl.pallas_call(
        body,
        name="in_proj",
        grid=(T // tm,),
        in_specs=[row(D_MODEL), _const_spec((1, D_MODEL)), _const_spec(w_uqkv.shape), _const_spec(w_fl.shape), _const_spec(w_g.shape), _HBM],
        out_specs=[row(D_MODEL), row(POOL_WIDTH), row(3 * ATTN_WIDTH), row(FL_PAD), row(2 * D_MODEL)],
        out_shape=[
            jax.ShapeDtypeStruct((T, D_MODEL), BF16),
            jax.ShapeDtypeStruct((T, POOL_WIDTH), F32),
            jax.ShapeDtypeStruct((T, 3 * ATTN_WIDTH), BF16),
            jax.ShapeDtypeStruct((T, FL_PAD), F32),
            jax.ShapeDtypeStruct((T, 2 * D_MODEL), BF16),
        ],
        compiler_params=_params(("parallel",)),
    )(x, g1, w_uqkv, w_fl, w_g, token)


def _log_sigmoid(x):
    return jnp.minimum(x, 0.0) - jnp.log(1.0 + jnp.exp(-jnp.abs(x)))


def _forget_fwd(fl, b_pad, n_seq, S):
    def body(fl_ref, b_ref, fcol_ref):
        lf = _log_sigmoid(fl_ref[...] + b_ref[...])
        t = lf.T
        lane = lax.broadcasted_iota(jnp.int32, t.shape, 1)
        k = 1
        while k < S:
            t = t + jnp.where(lane >= k, pltpu.roll(t, k, 1), 0.0)
            k *= 2
        fcol_ref[...] = t.T

    return pl.pallas_call(
        body,
        name="forget_fwd",
        grid=(n_seq,),
        in_specs=[pl.BlockSpec((S, FL_PAD), lambda s: (s, 0)), _const_spec((1, FL_PAD))],
        out_specs=pl.BlockSpec((S, FL_PAD), lambda s: (s, 0)),
        out_shape=jax.ShapeDtypeStruct((n_seq * S, FL_PAD), F32),
        compiler_params=_params(("parallel",)),
    )(fl, b_pad)


def _window_pick(g, v2, v4, v8, v16):
    return jnp.where(g == 0, v2, jnp.where(g == 1, v4, jnp.where(g == 2, v8, v16)))


def _pool_fwd(u, mix_b, scale, n_seq, S):
    T = n_seq * S

    def body(u_ref, mix_ref, sc_ref, pm_ref, p2_ref, p3_ref):
        g = pl.program_id(1)
        uu = u_ref[...]
        row = lax.broadcasted_iota(jnp.int32, uu.shape, 0)

        def back(a, k):
            return jnp.where(row >= k, pltpu.roll(a, k, 0), 0.0)

        s2 = uu + back(uu, 1)
        s4 = s2 + back(s2, 2)
        s8 = s4 + back(s4, 4)
        s16 = s8 + back(s8, 8)
        w = _window_pick(g, 2.0, 4.0, 8.0, 16.0)
        cnt = jnp.minimum((row + 1).astype(F32), w)
        pm = _window_pick(g, s2, s4, s8, s16) / cnt - uu
        pmb = pm.astype(BF16)
        pm_ref[...] = pmb
        p2 = _mm(pmb, mix_ref[...])
        p2_ref[...] = p2
        p3_ref[...] = (p2 * sc_ref[...]).astype(BF16)

    grp = pl.BlockSpec((S, GROUP_DIM), lambda s, g: (s, g))
    return pl.pallas_call(
        body,
        name="pool_fwd",
        grid=(n_seq, len(POOL_WINDOWS)),
        in_specs=[
            grp,
            pl.BlockSpec((None, GROUP_DIM, GROUP_DIM), lambda s, g: (g, 0, 0)),
            pl.BlockSpec((1, GROUP_DIM), lambda s, g: (0, g)),
        ],
        out_specs=[grp, grp, grp],
        out_shape=[
            jax.ShapeDtypeStruct((T, POOL_WIDTH), BF16),
            jax.ShapeDtypeStruct((T, POOL_WIDTH), F32),
            jax.ShapeDtypeStruct((T, POOL_WIDTH), BF16),
        ],
        compiler_params=_params(("parallel", "parallel")),
    )(u, mix_b, scale)


def _split3(v):
    hi = v.astype(BF16).astype(F32)
    r = v - hi
    mid = r.astype(BF16).astype(F32)
    lo = (r - mid).astype(BF16).astype(F32)
    return hi, mid, lo


def _bias_lanes(v):
    hi, mid, lo = _split3(v)
    lane = lax.broadcasted_iota(jnp.int32, (1, LANES), 1)
    packed = jnp.where(lane < N_HEADS, hi, jnp.where(lane < 2 * N_HEADS, pltpu.roll(mid, N_HEADS, 1), pltpu.roll(lo, 2 * N_HEADS, 1)))
    return jnp.where(lane < 3 * N_HEADS, packed, 0.0).astype(BF16)


def _bias_placement(slot):
    row = lax.broadcasted_iota(jnp.int32, (LANES, N_HEADS * LANES), 0)
    col = lax.broadcasted_iota(jnp.int32, (LANES, N_HEADS * LANES), 1)
    h = col // LANES
    n = col % LANES - jnp.where(h % 2 == 0, HEAD_DIM, 0) - 3 * slot
    return ((n >= 0) & (n < 3) & (row == N_HEADS * n + h)).astype(BF16)


def _augment(xp, h, bias, ones_slot):
    lane = lax.broadcasted_iota(jnp.int32, (1, LANES), 1)
    hh = h % 2
    head = (lane >= HEAD_DIM * hh) & (lane < HEAD_DIM * (hh + 1))
    b = HEAD_DIM * (1 - hh)
    rest = jnp.zeros_like(xp) if bias is None else bias[:, h * LANES : (h + 1) * LANES]
    out = jnp.where(head, xp, rest)
    if ones_slot is not None:
        out = jnp.where((lane >= b + 3 * ones_slot) & (lane < b + 3 * ones_slot + 3), jnp.ones_like(xp), out)
    return out


def _attn_fwd(qkv, fcol, n_seq, S):
    T = n_seq * S
    tb = ATTN_BLOCK
    nq = S // tb
    scale = HEAD_DIM ** -0.5

    def body(q_ref, k_ref, v_ref, fc_ref, o_ref, st_ref, qa_sc, ka_sc, m_sc, l_sc, acc_sc):
        i = pl.program_id(1)
        lane = lax.broadcasted_iota(jnp.int32, (1, LANES), 1)
        low = lane < HEAD_DIM

        @pl.when(i == 0)
        def _():
            place = _bias_placement(1)

            def rows_ka(r, carry):
                r0 = pl.multiple_of(r * tb, tb)
                bias = _mm(_bias_lanes(-fc_ref[pl.ds(r0, tb), :]), place).astype(BF16)
                for h in range(N_HEADS):
                    kp = k_ref[pl.ds(r0, tb), (h // 2) * LANES : (h // 2 + 1) * LANES] * scale
                    ka_sc[h, pl.ds(r0, tb), :] = _augment(kp, h, bias, 0)
                return carry

            lax.fori_loop(0, nq, rows_ka, 0)

        q0 = pl.multiple_of(i * tb, tb)
        bias = _mm(_bias_lanes(fc_ref[pl.ds(q0, tb), :]), _bias_placement(0)).astype(BF16)
        for h in range(N_HEADS):
            qa_sc[h] = _augment(q_ref[:, (h // 2) * LANES : (h // 2 + 1) * LANES], h, bias, 1)
        m_sc[...] = jnp.full(m_sc.shape, -jnp.inf, F32)
        l_sc[...] = jnp.zeros_like(l_sc)
        acc_sc[...] = jnp.zeros_like(acc_sc)
        causal = lax.broadcasted_iota(jnp.int32, (tb, tb), 1) <= lax.broadcasted_iota(jnp.int32, (tb, tb), 0)

        def step(j, masked):
            c0 = pl.multiple_of(j * tb, tb)
            for p in range(N_PAIRS):
                vb = v_ref[pl.ds(c0, tb), p * LANES : (p + 1) * LANES]
                pv, al = [], []
                for hh in range(2):
                    h = 2 * p + hh
                    s = _mm_nt(qa_sc[h], ka_sc[h, pl.ds(c0, tb), :])
                    if masked:
                        s = jnp.where(causal, s, -jnp.inf)
                    m_old = m_sc[h]
                    m_new = jnp.maximum(m_old, jnp.max(s, axis=1, keepdims=True))
                    alpha = jnp.exp(m_old - m_new)
                    pe = jnp.exp(s - jnp.concatenate([m_new] * (tb // LANES), axis=1))
                    l_sc[h] = alpha * l_sc[h] + jnp.sum(pe, axis=1, keepdims=True)
                    m_sc[h] = m_new
                    pv.append(_mm(pe.astype(BF16), vb))
                    al.append(alpha)
                acc_sc[p] = jnp.where(low, al[0], al[1]) * acc_sc[p] + jnp.where(low, pv[0], pv[1])

        def loop_body(j, carry):
            step(j, False)
            return carry

        lax.fori_loop(0, i, loop_body, 0)
        step(i, True)
        st = jnp.zeros((tb, LANES), F32)
        for p in range(N_PAIRS):
            lp = jnp.where(low, l_sc[2 * p], l_sc[2 * p + 1])
            o_ref[:, p * LANES : (p + 1) * LANES] = (acc_sc[p] / lp).astype(BF16)
            for h in (2 * p, 2 * p + 1):
                st = jnp.where(lane == h, m_sc[h] + jnp.log(l_sc[h]), st)
        st_ref[...] = st

    return pl.pallas_call(
        body,
        name="attn_fwd",
        grid=(n_seq, nq),
        in_specs=[
            pl.BlockSpec((tb, ATTN_WIDTH), lambda s, i: (s * nq + i, 0)),
            pl.BlockSpec((S, ATTN_WIDTH), lambda s, i: (s, 1)),
            pl.BlockSpec((S, ATTN_WIDTH), lambda s, i: (s, 2)),
            pl.BlockSpec((S, LANES), lambda s, i: (s, 0)),
        ],
        out_specs=[
            pl.BlockSpec((tb, ATTN_WIDTH), lambda s, i: (s * nq + i, 0)),
            pl.BlockSpec((tb, LANES), lambda s, i: (s * nq + i, 0)),
        ],
        out_shape=[jax.ShapeDtypeStruct((T, ATTN_WIDTH), BF16), jax.ShapeDtypeStruct((T, LANES), F32)],
        scratch_shapes=[
            pltpu.VMEM((N_HEADS, tb, LANES), BF16),
            pltpu.VMEM((N_HEADS, S, LANES), BF16),
            pltpu.VMEM((N_HEADS, tb, LANES), F32),
            pltpu.VMEM((N_HEADS, tb, LANES), F32),
            pltpu.VMEM((N_PAIRS, tb, LANES), F32),
        ],
        compiler_params=_params(("parallel", "arbitrary")),
    )(qkv, qkv, qkv, fcol)


def _mix_out(a, p3, gates, x, w_ao, w_po, w_out):
    T = x.shape[0]
    tm = ROW_TILE

    def body(a_ref, p3_ref, gt_ref, x_ref, wao_ref, wpo_ref, wout_ref, mg_ref, x1_ref, ay_ref, py_ref):
        ay = _mm(a_ref[...], _whole_cols(wao_ref))
        py = _mm(p3_ref[...], _whole_cols(wpo_ref))
        ay_ref[...] = ay.astype(BF16)
        py_ref[...] = py.astype(BF16)
        sp = _sigmoid(gt_ref[:, :D_MODEL].astype(F32))
        sa = _sigmoid(gt_ref[:, D_MODEL:].astype(F32))
        mb = (sp * py + sa * ay).astype(BF16)
        mg_ref[...] = mb
        x1_ref[...] = x_ref[...] + _mm(mb, wout_ref[...])

    row = lambda n: pl.BlockSpec((tm, n), lambda i: (i, 0))
    return pl.pallas_call(
        body,
        name="mix_out",
        grid=(T // tm,),
        in_specs=[
            row(ATTN_WIDTH), row(POOL_WIDTH), row(2 * D_MODEL), row(D_MODEL),
            _const_spec(w_ao.shape), _const_spec(w_po.shape), _const_spec(w_out.shape),
        ],
        out_specs=[row(D_MODEL), row(D_MODEL), row(D_MODEL), row(D_MODEL)],
        out_shape=[
            jax.ShapeDtypeStruct((T, D_MODEL), BF16), jax.ShapeDtypeStruct((T, D_MODEL), F32),
            jax.ShapeDtypeStruct((T, D_MODEL), BF16), jax.ShapeDtypeStruct((T, D_MODEL), BF16),
        ],
        compiler_params=_params(("parallel",)),
    )(a, p3, gates, x, w_ao, w_po, w_out)


def _ffn_fwd(x1, g2, gf, tgt, w_gate_t, w_up_t, w_down):
    T = x1.shape[0]
    tm = min(T, FF_ROW_TILE)
    nt = T // tm
    nc = D_FF // FF_CHUNK

    def body(x1_ref, g2_ref, gf_ref, tg_ref, wg_ref, wu_ref, wd_ref, h2_ref, gate_ref, up_ref, act_ref, dx2_ref, loss_ref, dgf_ref):
        x1v = x1_ref[...]
        h2, _, _ = _rms_fwd(x1v, g2_ref[...])
        h2b = h2.astype(BF16)
        h2_ref[...] = h2b
        for c in range(nc):
            sl = slice(c * FF_CHUNK, (c + 1) * FF_CHUNK)
            gate = _mm_nt(h2b, wg_ref[sl, :])
            up = _mm_nt(h2b, wu_ref[sl, :])
            gate_ref[:, sl] = gate.astype(BF16)
            up_ref[:, sl] = up.astype(BF16)
            act_ref[:, sl] = (gate * _sigmoid(gate) * up).astype(BF16)
        acc = x1v + _mm(act_ref[...], wd_ref[...])
        gfv = gf_ref[...]
        y, xh, r = _rms_fwd(acc, gfv)
        err = y - tg_ref[...]
        part = 0.5 * jnp.sum(jnp.mean(err * err, axis=-1, keepdims=True), axis=0, keepdims=True)
        dx2, dgrow = _rms_bwd(err * (1.0 / D_MODEL), xh, r, gfv)
        dx2_ref[...] = dx2

        @pl.when(pl.program_id(0) == 0)
        def _():
            dgf_ref[...] = jnp.zeros_like(dgf_ref)
            loss_ref[...] = jnp.zeros_like(loss_ref)

        dgf_ref[...] += jnp.sum(dgrow, axis=0, keepdims=True)
        loss_ref[...] += jnp.broadcast_to(part, loss_ref.shape)

    row = lambda n: pl.BlockSpec((tm, n), lambda i: (i, 0))
    return pl.pallas_call(
        body,
        name="ffn_fwd",
        grid=(nt,),
        in_specs=[
            row(D_MODEL), _const_spec((1, D_MODEL)), _const_spec((1, D_MODEL)), row(D_MODEL),
            _const_spec(w_gate_t.shape), _const_spec(w_up_t.shape), _const_spec(w_down.shape),
        ],
        out_specs=[
            row(D_MODEL), row(D_FF), row(D_FF), row(D_FF), row(D_MODEL),
            pl.BlockSpec((8, LANES), lambda i: (0, 0)),
            pl.BlockSpec((1, D_MODEL), lambda i: (0, 0)),
        ],
        out_shape=[
            jax.ShapeDtypeStruct((T, D_MODEL), BF16),
            jax.ShapeDtypeStruct((T, D_FF), BF16),
            jax.ShapeDtypeStruct((T, D_FF), BF16),
            jax.ShapeDtypeStruct((T, D_FF), BF16),
            jax.ShapeDtypeStruct((T, D_MODEL), F32),
            jax.ShapeDtypeStruct((8, LANES), F32),
            jax.ShapeDtypeStruct((1, D_MODEL), F32),
        ],
        compiler_params=_params(("arbitrary",)),
    )(x1, g2, gf, tgt, w_gate_t, w_up_t, w_down)


def _ffn_bwd(dx2, gate, up, x1, g2, w_gate_t, w_up_t, w_down):
    T = x1.shape[0]
    tm = min(T, FF_ROW_TILE)
    nc = D_FF // FF_CHUNK

    def body(dx2_ref, gate_ref, up_ref, x1_ref, g2_ref, wg_ref, wu_ref, wd_ref, dgate_ref, dup_ref, dx1_ref, dg2_ref):
        dx2v = dx2_ref[...]
        dx2b = dx2v.astype(BF16)
        for c in range(nc):
            sl = slice(c * FF_CHUNK, (c + 1) * FF_CHUNK)
            dact = _mm_nt(dx2b, wd_ref[sl, :])
            gate = gate_ref[:, sl].astype(F32)
            sg = _sigmoid(gate)
            silu = gate * sg
            dgate = (dact * up_ref[:, sl].astype(F32) * (sg * (1.0 + gate * (1.0 - sg)))).astype(BF16)
            dup = (dact * silu).astype(BF16)
            dgate_ref[:, sl] = dgate
            dup_ref[:, sl] = dup
        dh2 = _mm(dgate_ref[...], wg_ref[...]) + _mm(dup_ref[...], wu_ref[...])
        g2v = g2_ref[...]
        _, xh, r = _rms_fwd(x1_ref[...], g2v)
        dxn, dgrow = _rms_bwd(dh2, xh, r, g2v)
        dx1_ref[...] = dx2v + dxn

        @pl.when(pl.program_id(0) == 0)
        def _():
            dg2_ref[...] = jnp.zeros_like(dg2_ref)

        dg2_ref[...] += jnp.sum(dgrow, axis=0, keepdims=True)

    row = lambda n: pl.BlockSpec((tm, n), lambda i: (i, 0))
    return pl.pallas_call(
        body,
        name="ffn_bwd",
        grid=(T // tm,),
        in_specs=[
            row(D_MODEL), row(D_FF), row(D_FF), row(D_MODEL), _const_spec((1, D_MODEL)),
            _const_spec(w_gate_t.shape), _const_spec(w_up_t.shape), _const_spec(w_down.shape),
        ],
        out_specs=[row(D_FF), row(D_FF), row(D_MODEL), pl.BlockSpec((1, D_MODEL), lambda i: (0, 0))],
        out_shape=[
            jax.ShapeDtypeStruct((T, D_FF), BF16),
            jax.ShapeDtypeStruct((T, D_FF), BF16),
            jax.ShapeDtypeStruct((T, D_MODEL), F32),
            jax.ShapeDtypeStruct((1, D_MODEL), F32),
        ],
        compiler_params=_params(("arbitrary",), VMEM_LIMIT_MAX),
    )(dx2, gate, up, x1, g2, w_gate_t, w_up_t, w_down)


def _mix_bwd(dx1, gates, pool_y, attn_y, p2, scale, w_out, w_ao, w_po, token):
    T = dx1.shape[0]
    tm = ROW_TILE

    def body(dx1_ref, gt_ref, py_ref, ay_ref, p2_ref, sc_ref, wout_ref, wao_ref, wpo_ref, token_ref, dgt_ref, dpy_ref, day_ref, da_ref, dp2_ref, dsc_ref):
        dm = _mm_nt(dx1_ref[...].astype(BF16), wout_ref[...])
        sp = _sigmoid(gt_ref[:, :D_MODEL].astype(F32))
        sa = _sigmoid(gt_ref[:, D_MODEL:].astype(F32))
        dgt_ref[:, :D_MODEL] = (dm * py_ref[...].astype(F32) * (sp * (1.0 - sp))).astype(BF16)
        dgt_ref[:, D_MODEL:] = (dm * ay_ref[...].astype(F32) * (sa * (1.0 - sa))).astype(BF16)
        dpy = (dm * sp).astype(BF16)
        day = (dm * sa).astype(BF16)
        dpy_ref[...] = dpy
        day_ref[...] = day
        da_ref[...] = _mm_nt(day, _whole_cols(wao_ref)).astype(BF16)
        dp3 = _mm_nt(dpy, _whole_cols(wpo_ref))
        dp2_ref[...] = (dp3 * sc_ref[...]).astype(BF16)

        @pl.when(pl.program_id(0) == 0)
        def _():
            dsc_ref[...] = jnp.zeros_like(dsc_ref)

        dsc_ref[...] += jnp.sum(dp3 * p2_ref[...], axis=0, keepdims=True)

    row = lambda n: pl.BlockSpec((tm, n), lambda i: (i, 0))
    return pl.pallas_call(
        body,
        name="mix_bwd",
        grid=(T // tm,),
        in_specs=[
            row(D_MODEL), row(2 * D_MODEL), row(D_MODEL), row(D_MODEL), row(POOL_WIDTH), _const_spec((1, POOL_WIDTH)),
            _const_spec(w_out.shape), _const_spec(w_ao.shape), _const_spec(w_po.shape), _HBM,
        ],
        out_specs=[row(2 * D_MODEL), row(D_MODEL), row(D_MODEL), row(ATTN_WIDTH), row(POOL_WIDTH), pl.BlockSpec((1, POOL_WIDTH), lambda i: (0, 0))],
        out_shape=[
            jax.ShapeDtypeStruct((T, 2 * D_MODEL), BF16),
            jax.ShapeDtypeStruct((T, D_MODEL), BF16),
            jax.ShapeDtypeStruct((T, D_MODEL), BF16),
            jax.ShapeDtypeStruct((T, ATTN_WIDTH), BF16),
            jax.ShapeDtypeStruct((T, POOL_WIDTH), BF16),
            jax.ShapeDtypeStruct((1, POOL_WIDTH), F32),
        ],
        compiler_params=_params(("arbitrary",)),
    )(dx1, gates, pool_y, attn_y, p2, scale, w_out, w_ao, w_po, token)


def _pool_bwd(dp2, pm, mix_b, token, n_seq, S):
    T = n_seq * S

    def body(dp2_ref, pm_ref, mix_ref, token_ref, du_ref, dmix_ref):
        g = pl.program_id(0)
        dp2v = dp2_ref[...]
        dpm = _mm_nt(dp2v, mix_ref[...])
        row = lax.broadcasted_iota(jnp.int32, dpm.shape, 0)
        w = _window_pick(g, 2.0, 4.0, 8.0, 16.0)
        e = dpm / jnp.minimum((row + 1).astype(F32), w)

        def ahead(a, k):
            return jnp.where(row < S - k, pltpu.roll(a, S - k, 0), 0.0)

        r2 = e + ahead(e, 1)
        r4 = r2 + ahead(r2, 2)
        r8 = r4 + ahead(r4, 4)
        r16 = r8 + ahead(r8, 8)
        du_ref[...] = (_window_pick(g, r2, r4, r8, r16) - dpm).astype(BF16)

        @pl.when(pl.program_id(1) == 0)
        def _():
            dmix_ref[...] = jnp.zeros_like(dmix_ref)

        dmix_ref[...] += _mm_tn(pm_ref[...], dp2v)

    grp = pl.BlockSpec((S, GROUP_DIM), lambda g, s: (s, g))
    mixs = pl.BlockSpec((None, GROUP_DIM, GROUP_DIM), lambda g, s: (g, 0, 0))
    return pl.pallas_call(
        body,
        name="pool_bwd",
        grid=(len(POOL_WINDOWS), n_seq),
        in_specs=[grp, grp, mixs, _HBM],
        out_specs=[grp, mixs],
        out_shape=[jax.ShapeDtypeStruct((T, POOL_WIDTH), BF16), jax.ShapeDtypeStruct((len(POOL_WINDOWS), GROUP_DIM, GROUP_DIM), F32)],
        compiler_params=_params(("parallel", "arbitrary")),
    )(dp2, pm, mix_b, token)


def _attn_bwd(qkv, da, a, fcol, lse, n_seq, S):
    T = n_seq * S
    tb = ATTN_BLOCK
    nb = S // tb
    scale = HEAD_DIM ** -0.5

    def body(q_ref, k_ref, v_ref, do_ref, o_ref, fc_ref, st_ref, dq_ref, dk_ref, dv_ref, dfk_ref, dfq_ref,
             qa_sc, doa_sc, qat_sc, doat_sc, dq_acc, ka_sc, va_sc, dkt_sc, dvt_sc):
        j = pl.program_id(1)
        lane = lax.broadcasted_iota(jnp.int32, (1, LANES), 1)
        low = lane < HEAD_DIM

        @pl.when(j == 0)
        def _():
            dq_acc[...] = jnp.zeros_like(dq_acc)
            place = _bias_placement(0)

            def rows_q(i, carry):
                r0 = pl.multiple_of(i * tb, tb)
                delta = jnp.zeros((tb, LANES), F32)
                for h in range(N_HEADS):
                    pair = slice((h // 2) * LANES, (h // 2 + 1) * LANES)
                    prod = do_ref[pl.ds(r0, tb), pair].astype(F32) * o_ref[pl.ds(r0, tb), pair].astype(F32)
                    head = (lane >= HEAD_DIM * (h % 2)) & (lane < HEAD_DIM * (h % 2 + 1))
                    delta = jnp.where(lane == h, jnp.sum(jnp.where(head, prod, 0.0), axis=1, keepdims=True), delta)
                cq = fc_ref[pl.ds(r0, tb), :] - st_ref[pl.ds(r0, tb), :]
                q_bias = _mm(_bias_lanes(cq), place).astype(BF16)
                do_bias = _mm(_bias_lanes(-delta), place).astype(BF16)
                for h in range(N_HEADS):
                    pair = slice((h // 2) * LANES, (h // 2 + 1) * LANES)
                    qa = _augment(q_ref[pl.ds(r0, tb), pair], h, q_bias, 1)
                    doa = _augment(do_ref[pl.ds(r0, tb), pair], h, do_bias, None)
                    qa_sc[h, pl.ds(r0, tb), :] = qa
                    doa_sc[h, pl.ds(r0, tb), :] = doa
                    qat_sc[h, i] = qa.astype(F32).T.astype(BF16)
                    doat_sc[h, i] = doa.astype(F32).T.astype(BF16)
                return carry

            lax.fori_loop(0, nb, rows_q, 0)

        c0 = pl.multiple_of(j * tb, tb)
        k_bias = _mm(_bias_lanes(-fc_ref[pl.ds(c0, tb), :]), _bias_placement(1)).astype(BF16)
        for h in range(N_HEADS):
            pair = slice((h // 2) * LANES, (h // 2 + 1) * LANES)
            ka_sc[h] = _augment(k_ref[:, pair] * scale, h, k_bias, 0)
            va_sc[h] = _augment(v_ref[:, pair], h, None, 0)
        dkt_sc[...] = jnp.zeros_like(dkt_sc)
        dvt_sc[...] = jnp.zeros_like(dvt_sc)
        causal = lax.broadcasted_iota(jnp.int32, (tb, tb), 1) <= lax.broadcasted_iota(jnp.int32, (tb, tb), 0)

        def step(i, masked):
            r0 = pl.multiple_of(i * tb, tb)
            for h in range(N_HEADS):
                s = _mm_nt(qa_sc[h, pl.ds(r0, tb), :], ka_sc[h])
                if masked:
                    s = jnp.where(causal, s, -jnp.inf)
                pr = jnp.exp(s)
                dvt_sc[h] += _mm(doat_sc[h, i], pr.astype(BF16))
                dsb = (pr * _mm_nt(doa_sc[h, pl.ds(r0, tb), :], va_sc[h])).astype(BF16)
                dkt_sc[h] += _mm(qat_sc[h, i], dsb)
                dq_acc[h, pl.ds(r0, tb), :] += _mm(dsb, ka_sc[h])

        step(j, True)

        def loop_body(i, carry):
            step(i, False)
            return carry

        lax.fori_loop(j + 1, nb, loop_body, 0)
        dfk = jnp.zeros((tb, LANES), F32)
        for p in range(N_PAIRS):
            dk = [dkt_sc[2 * p + hh].T for hh in range(2)]
            dv = [dvt_sc[2 * p + hh].T for hh in range(2)]
            dk_ref[:, p * LANES : (p + 1) * LANES] = (jnp.where(low, dk[0], dk[1]) * scale).astype(BF16)
            dv_ref[:, p * LANES : (p + 1) * LANES] = jnp.where(low, dv[0], dv[1]).astype(BF16)
            for hh in range(2):
                b = HEAD_DIM * (1 - hh) + 3
                dfk = jnp.where(lane == 2 * p + hh, -dk[hh][:, b : b + 1], dfk)
        dfk_ref[...] = dfk

        @pl.when(j == nb - 1)
        def _():
            def rows_dq(i, carry):
                r0 = pl.multiple_of(i * tb, tb)
                dfq = jnp.zeros((tb, LANES), F32)
                for p in range(N_PAIRS):
                    parts = [dq_acc[2 * p + hh, pl.ds(r0, tb), :] for hh in range(2)]
                    dq_ref[pl.ds(r0, tb), p * LANES : (p + 1) * LANES] = jnp.where(low, parts[0], parts[1]).astype(BF16)
                    for hh in range(2):
                        b = HEAD_DIM * (1 - hh)
                        dfq = jnp.where(lane == 2 * p + hh, parts[hh][:, b : b + 1], dfq)
                dfq_ref[pl.ds(r0, tb), :] = dfq
                return carry

            lax.fori_loop(0, nb, rows_dq, 0)

    seq = lambda w, col: pl.BlockSpec((S, w), lambda s, j: (s, col))
    seq_in = lambda w, col: pl.BlockSpec((S, w), lambda s, j: (s, col), pipeline_mode=pl.Buffered(1))
    blk = lambda w, col: pl.BlockSpec((tb, w), lambda s, j: (s * nb + j, col))
    return pl.pallas_call(
        body,
        name="attn_bwd",
        grid=(n_seq, nb),
        in_specs=[seq_in(ATTN_WIDTH, 0), blk(ATTN_WIDTH, 1), blk(ATTN_WIDTH, 2), seq_in(ATTN_WIDTH, 0), seq_in(ATTN_WIDTH, 0), seq_in(LANES, 0), seq_in(LANES, 0)],
        out_specs=[seq(ATTN_WIDTH, 0), blk(ATTN_WIDTH, 0), blk(ATTN_WIDTH, 0), blk(LANES, 0), seq(LANES, 0)],
        out_shape=[
            jax.ShapeDtypeStruct((T, ATTN_WIDTH), BF16),
            jax.ShapeDtypeStruct((T, ATTN_WIDTH), BF16),
            jax.ShapeDtypeStruct((T, ATTN_WIDTH), BF16),
            jax.ShapeDtypeStruct((T, LANES), F32),
            jax.ShapeDtypeStruct((T, LANES), F32),
        ],
        scratch_shapes=[
            pltpu.VMEM((N_HEADS, S, LANES), BF16),
            pltpu.VMEM((N_HEADS, S, LANES), BF16),
            pltpu.VMEM((N_HEADS, nb, LANES, tb), BF16),
            pltpu.VMEM((N_HEADS, nb, LANES, tb), BF16),
            pltpu.VMEM((N_HEADS, S, LANES), F32),
            pltpu.VMEM((N_HEADS, tb, LANES), BF16),
            pltpu.VMEM((N_HEADS, tb, LANES), BF16),
            pltpu.VMEM((N_HEADS, LANES, tb), F32),
            pltpu.VMEM((N_HEADS, LANES, tb), F32),
        ],
        compiler_params=_params(("parallel", "arbitrary"), VMEM_LIMIT_MAX),
    )(qkv, qkv, qkv, da, a, fcol, lse)


def _forget_bwd(dfk, dfq, fl, b_pad, n_seq, S):
    def body(df_ref, dfq_ref, fl_ref, b_ref, dfl_ref, db_ref):
        t = (df_ref[...] + dfq_ref[...]).T
        lane = lax.broadcasted_iota(jnp.int32, t.shape, 1)
        k = 1
        while k < S:
            t = t + jnp.where(lane < S - k, pltpu.roll(t, S - k, 1), 0.0)
            k *= 2
        dfl = t.T * _sigmoid(-(fl_ref[...] + b_ref[...]))
        dfl_ref[...] = dfl.astype(BF16)

        @pl.when(pl.program_id(0) == 0)
        def _():
            db_ref[...] = jnp.zeros_like(db_ref)

        db_ref[...] += jnp.sum(dfl, axis=0, keepdims=True)

    return pl.pallas_call(
        body,
        name="forget_bwd",
        grid=(n_seq,),
        in_specs=[
            pl.BlockSpec((S, LANES), lambda s: (s, 0)),
            pl.BlockSpec((S, LANES), lambda s: (s, 0)),
            pl.BlockSpec((S, FL_PAD), lambda s: (s, 0)),
            _const_spec((1, FL_PAD)),
        ],
        out_specs=[pl.BlockSpec((S, FL_PAD), lambda s: (s, 0)), pl.BlockSpec((1, FL_PAD), lambda s: (0, 0))],
        out_shape=[jax.ShapeDtypeStruct((n_seq * S, FL_PAD), BF16), jax.ShapeDtypeStruct((1, FL_PAD), F32)],
        compiler_params=_params(("arbitrary",)),
    )(dfk, dfq, fl, b_pad)


def _in_proj_bwd(du, dq, dk, dv, dfl, dgates, x, dx1, g1, w_uqkv, w_fl, w_g, token):
    T = x.shape[0]
    tm = ROW_TILE

    def body(du_ref, dq_ref, dk_ref, dv_ref, dfl_ref, dgt_ref, x_ref, dx1_ref, g_ref, wa_ref, wf_ref, wg_ref, token_ref, dx_ref, dg_ref):
        dz = jnp.concatenate([du_ref[...], dq_ref[...], dk_ref[...], dv_ref[...]], axis=1)
        dh = _mm_nt(dz, wa_ref[...]) + _mm_nt(dgt_ref[...], wg_ref[...]) + _mm_nt(dfl_ref[...], wf_ref[...])
        gv = g_ref[...]
        _, xh, r = _rms_fwd(x_ref[...], gv)
        dxn, dgrow = _rms_bwd(dh, xh, r, gv)
        dx_ref[...] = dx1_ref[...] + dxn

        @pl.when(pl.program_id(0) == 0)
        def _():
            dg_ref[...] = jnp.zeros_like(dg_ref)

        dg_ref[...] += jnp.sum(dgrow, axis=0, keepdims=True)

    row = lambda n: pl.BlockSpec((tm, n), lambda i: (i, 0))
    return pl.pallas_call(
        body,
        name="in_proj_bwd",
        grid=(T // tm,),
        in_specs=[
            row(512), row(512), row(512), row(512), row(FL_PAD), row(2 * D_MODEL), row(D_MODEL), row(D_MODEL), _const_spec((1, D_MODEL)),
            _const_spec(w_uqkv.shape), _const_spec(w_fl.shape), _const_spec(w_g.shape), _HBM,
        ],
        out_specs=[row(D_MODEL), pl.BlockSpec((1, D_MODEL), lambda i: (0, 0))],
        out_shape=[jax.ShapeDtypeStruct((T, D_MODEL), F32), jax.ShapeDtypeStruct((1, D_MODEL), F32)],
        compiler_params=_params(("arbitrary",)),
    )(du, dq, dk, dv, dfl, dgates, x, dx1, g1, w_uqkv, w_fl, w_g, token)


def _pick_block(n):
    for b in (1024, 512, 1408, 256, 128):
        if n % b == 0:
            return b
    raise ValueError(n)


def _matmul_tn(a, b, name, col_chunks=False):
    T, K = a.shape
    N = b.shape[1]
    bt, bk, bn = min(T, DW_TOKENS), _pick_block(K), _pick_block(N)
    nt = T // bt
    c = N // N_DEV
    assert not col_chunks or (bn == N and c % LANES == 0)

    def body(a_ref, b_ref, o_ref, acc):
        @pl.when(pl.program_id(2) == 0)
        def _():
            acc[...] = jnp.zeros_like(acc)

        acc[...] += _mm_tn(a_ref[...].astype(BF16), b_ref[...].astype(BF16))

        @pl.when(pl.program_id(2) == nt - 1)
        def _():
            if col_chunks:
                for d in range(N_DEV):
                    o_ref[d] = acc[:, d * c : (d + 1) * c].astype(BF16)
            else:
                o_ref[...] = acc[...].astype(BF16)

    if col_chunks:
        out_spec, out_shape = pl.BlockSpec((N_DEV, bk, c), lambda k, n, t: (0, k, 0)), (N_DEV, K, c)
    else:
        out_spec, out_shape = pl.BlockSpec((bk, bn), lambda k, n, t: (k, n)), (K, N)
    return pl.pallas_call(
        body,
        name=name,
        grid=(K // bk, N // bn, nt),
        in_specs=[pl.BlockSpec((bt, bk), lambda k, n, t: (t, k)), pl.BlockSpec((bt, bn), lambda k, n, t: (t, n))],
        out_specs=out_spec,
        out_shape=jax.ShapeDtypeStruct(out_shape, BF16),
        scratch_shapes=[pltpu.VMEM((bk, bn), F32)],
        compiler_params=_params(("parallel", "parallel", "arbitrary")),
    )(a, b)


W_IN_A = POOL_WIDTH + 3 * ATTN_WIDTH
W_IN_SHARD = (W_IN_A + N_HEADS + 2 * D_MODEL) // N_DEV
_W_IN_PIECES = ((0, W_IN_A), (W_IN_A, W_IN_A + N_HEADS), (W_IN_A + N_HEADS, W_IN_A + N_HEADS + 2 * D_MODEL))


def _w_in_segments(d):
    lo, hi = d * W_IN_SHARD, (d + 1) * W_IN_SHARD
    out = []
    for p, (a, b) in enumerate(_W_IN_PIECES):
        s, e = max(lo, a), min(hi, b)
        if s < e:
            out.append((p, s - a, s - lo, e - s))
    return out


def _w_in_pieces(gathered):
    tm = ROW_TILE // 2

    def body(g_ref, wa_ref, wf_ref, wg_ref):
        outs = (wa_ref, wf_ref, wg_ref)
        wf_ref[...] = jnp.zeros_like(wf_ref)
        for d in range(N_DEV):
            for p, at, frm, n in _w_in_segments(d):
                outs[p][:, at : at + n] = g_ref[d, :, frm : frm + n]

    return pl.pallas_call(
        body,
        name="w_in_pieces",
        grid=(D_MODEL // tm,),
        in_specs=[pl.BlockSpec((N_DEV, tm, W_IN_SHARD), lambda i: (0, i, 0))],
        out_specs=[pl.BlockSpec((tm, W_IN_A), lambda i: (i, 0)), pl.BlockSpec((tm, FL_PAD), lambda i: (i, 0)), pl.BlockSpec((tm, 2 * D_MODEL), lambda i: (i, 0))],
        out_shape=[
            jax.ShapeDtypeStruct((D_MODEL, W_IN_A), gathered.dtype),
            jax.ShapeDtypeStruct((D_MODEL, FL_PAD), gathered.dtype),
            jax.ShapeDtypeStruct((D_MODEL, 2 * D_MODEL), gathered.dtype),
        ],
        compiler_params=_params(("parallel",)),
    )(gathered)


def _dw_in(h, du, dq, dk, dv, dfl, dgates, token):
    T = h.shape[0]
    bt, bk = min(T, DW_TOKENS // 2), 512
    nt = T // bt
    pieces = (du, dq, dk, dv, dfl, dgates)
    offs = [0]
    for p in pieces:
        offs.append(offs[-1] + p.shape[1])

    def body(h_ref, *rest):
        refs, o_ref, acc = rest[: len(pieces)], rest[-2], rest[-1]

        @pl.when(pl.program_id(1) == 0)
        def _():
            acc[...] = jnp.zeros_like(acc)

        ht = h_ref[...].T
        for ref, at in zip(refs, offs):
            acc[:, at : at + ref.shape[1]] += _mm(ht, ref[...])

        @pl.when(pl.program_id(1) == nt - 1)
        def _():
            starts = (0, W_IN_A, W_IN_A + FL_PAD)
            for d in range(N_DEV):
                for p, at, to, n in _w_in_segments(d):
                    o_ref[d, :, to : to + n] = acc[:, starts[p] + at : starts[p] + at + n].astype(BF16)

    return pl.pallas_call(
        body,
        name="dw_in",
        grid=(D_MODEL // bk, nt),
        in_specs=[pl.BlockSpec((bt, bk), lambda k, t: (t, k))] + [pl.BlockSpec((bt, p.shape[1]), lambda k, t: (t, 0)) for p in pieces] + [_HBM],
        out_specs=pl.BlockSpec((N_DEV, bk, W_IN_SHARD), lambda k, t: (0, k, 0)),
        out_shape=jax.ShapeDtypeStruct((N_DEV, D_MODEL, W_IN_SHARD), BF16),
        scratch_shapes=[pltpu.VMEM((bk, offs[-1]), F32)],
        compiler_params=_params(("parallel", "arbitrary")),
    )(h, *pieces, token)


def _position():
    return lax.axis_index("x"), lax.axis_index("y"), lax.axis_index("c")


_HBM = pl.BlockSpec(memory_space=pl.ANY)


def _all_gather(blocks, name):
    n = len(blocks)

    def body(*refs):
        xs, outs = refs[:n], refs[n : 2 * n]
        send_sems, recv_sems, local_sems = refs[2 * n :]
        x, y, c = _position()
        me, sibling = (x, y, c), (x, y, 1 - c)
        chips = [(1 - x, y), (x, 1 - y), (1 - x, 1 - y)]

        def rows(a, px, py, pc):
            return outs[a].at[4 * px + 2 * py + pc]

        def copy(a, k, blk, to, src=None):
            return pltpu.make_async_remote_copy(
                src_ref=rows(a, *blk) if src is None else src, dst_ref=rows(a, *blk),
                send_sem=send_sems.at[7 * a + k], recv_sem=recv_sems.at[7 * a + k], device_id=to, device_id_type=MESH,
            )

        first = []
        for a in range(n):
            first += [copy(a, 1 + j, me, (*chip, c), src=xs[a]) for j, chip in enumerate(chips)]
            first.append(copy(a, 0, me, sibling, src=xs[a]))
        mine = [pltpu.make_async_copy(xs[a], rows(a, *me), local_sems.at[a]) for a in range(n)]
        for cp in first + mine:
            cp.start()
        passed = []
        for j, chip in enumerate(chips):
            for a in range(n):
                copy(a, 1 + j, (*chip, c), me).wait_recv()
                passed.append(copy(a, 4 + j, (*chip, c), sibling))
                passed[-1].start()
        for a in range(n):
            copy(a, 0, sibling, me).wait_recv()
        for j, chip in enumerate(chips):
            for a in range(n):
                copy(a, 4 + j, (*chip, 1 - c), me).wait_recv()
        for cp in first + passed:
            cp.wait_send()
        for cp in mine:
            cp.wait()

    return pl.pallas_call(
        body,
        name=name,
        out_shape=[jax.ShapeDtypeStruct((N_DEV, *b.shape), b.dtype) for b in blocks],
        in_specs=[_HBM] * n,
        out_specs=[_HBM] * n,
        scratch_shapes=[pltpu.SemaphoreType.DMA((7 * n,)), pltpu.SemaphoreType.DMA((7 * n,)), pltpu.SemaphoreType.DMA((n,))],
    )(*blocks)


_SEM = pl.BlockSpec(memory_space=pltpu.SEMAPHORE)
_HBM_ONLY = pl.BlockSpec(memory_space=pltpu.HBM)
_SIDE_EFFECT = pltpu.SideEffectType.DATAFLOW_SIDE_EFFECTING


def _peer(x, y, c, k):
    return (1 - x if k & 4 else x, 1 - y if k & 2 else y, 1 - c if k & 1 else c)


_PEER_BITS = {"gather": range(1, N_DEV), "gather_half": (1, 4, 2, 6), "forward": (4, 2, 6), "scatter": range(1, N_DEV)}
_GATHERS = ("gather", "gather_half")


def _exchange_copies(src_refs, land_refs, send_sems, recv_sems, pattern, receive_side):
    x, y, c = _position()
    me = 4 * x + 2 * y + c
    bits = _PEER_BITS[pattern]
    cps = []
    for j, k in enumerate(bits):
        px, py, pc = _peer(x, y, c, k)
        peer = 4 * px + 2 * py + pc
        for a, (src, land) in enumerate(zip(src_refs, land_refs)):
            to = (px, py, pc)
            if pattern == "forward":
                slot = 4 * px + 2 * py + (1 - c if receive_side else c)
                s, to = land.at[slot], (x, y, 1 - c)
            else:
                s, slot = (src if pattern in _GATHERS else src.at[peer]), (peer if receive_side else me)
            cps.append(pltpu.make_async_remote_copy(
                src_ref=s, dst_ref=land.at[slot],
                send_sem=send_sems.at[len(bits) * a + j], recv_sem=recv_sems.at[len(bits) * a + j],
                device_id=to, device_id_type=MESH,
            ))
    return cps


def _own_copies(src_refs, land_refs, own_sems):
    x, y, c = _position()
    return [
        pltpu.make_async_copy(src, land.at[4 * x + 2 * y + c], own_sems.at[a])
        for a, (src, land) in enumerate(zip(src_refs, land_refs))
    ]


def _exchange_start(srcs, after, name, pattern):
    n = len(srcs)
    m = len(_PEER_BITS[pattern])
    lands = [jax.ShapeDtypeStruct((N_DEV, *s.shape[-2:]), s.dtype) for s in srcs]

    def body(*refs):
        src_refs, land_refs = refs[1 : 1 + n], refs[1 + n : 1 + 2 * n]
        send_sems, recv_sems, own_sems = refs[1 + 2 * n : 4 + 2 * n]
        token = refs[-1]
        if pattern in _GATHERS:
            for cp in _own_copies(src_refs, land_refs, own_sems):
                cp.start()
        for cp in _exchange_copies(src_refs, land_refs, send_sems, recv_sems, pattern, receive_side=False):
            cp.start()
        token[...] = jnp.zeros_like(token)

    hbm = lambda t: pltpu.with_memory_space_constraint(t, pltpu.HBM)
    out = pl.pallas_call(
        body,
        name=name,
        out_shape=(
            pltpu.SemaphoreType.DMA((m * n,)), pltpu.SemaphoreType.DMA((m * n,)), pltpu.SemaphoreType.DMA((n,)),
            *[pltpu.HBM(s.shape, s.dtype) for s in srcs], *[pltpu.HBM(l.shape, l.dtype) for l in lands],
            jax.ShapeDtypeStruct((8, LANES), F32),
        ),
        in_specs=(_HBM, *[_HBM_ONLY] * (2 * n)),
        out_specs=(_SEM, _SEM, _SEM, *[_HBM_ONLY] * (2 * n), pl.BlockSpec(memory_space=pltpu.VMEM)),
        input_output_aliases={1 + i: 3 + i for i in range(2 * n)},
        compiler_params=pltpu.CompilerParams(has_side_effects=_SIDE_EFFECT),
    )(after, *[hbm(s) for s in srcs], *[hbm(lax.empty(l.shape, l.dtype)) for l in lands])
    return out[:3], out[3 : 3 + n], out[3 + n : 3 + 2 * n], out[-1]


def _exchange_wait(sems, srcs, lands, after, name, pattern):
    n = len(srcs)

    def body(*refs):
        src_refs, land_refs = refs[:n], refs[n : 2 * n]
        send_sems, recv_sems, own_sems = refs[2 * n : 2 * n + 3]
        if pattern in _GATHERS:
            for cp in _own_copies(src_refs, land_refs, own_sems):
                cp.wait()
        for cp in _exchange_copies(src_refs, land_refs, send_sems, recv_sems, pattern, receive_side=True):
            cp.wait_send()
            cp.wait_recv()

    out = pl.pallas_call(
        body,
        name=name,
        out_shape=(*[pltpu.HBM(s.shape, s.dtype) for s in srcs], *[pltpu.HBM(l.shape, l.dtype) for l in lands]),
        in_specs=(*[_HBM_ONLY] * (2 * n), _SEM, _SEM, _SEM, _HBM),
        out_specs=tuple([_HBM_ONLY] * (2 * n)),
        input_output_aliases={i: i for i in range(2 * n)},
        compiler_params=pltpu.CompilerParams(has_side_effects=_SIDE_EFFECT),
    )(*srcs, *lands, *sems, after)
    return out[:n], out[n:]


def _gather_forward(sems, srcs, lands, after, name):
    n = len(srcs)
    m = len(_PEER_BITS["forward"])

    def body(*refs):
        src_refs, land_refs = refs[:n], refs[n : 2 * n]
        send_sems, recv_sems, own_sems = refs[2 * n : 2 * n + 3]
        forward_send, forward_recv, token = refs[2 * n + 4], refs[2 * n + 5], refs[-1]
        for cp in _own_copies(src_refs, land_refs, own_sems):
            cp.wait()
        for cp in _exchange_copies(src_refs, land_refs, send_sems, recv_sems, "gather_half", receive_side=True):
            cp.wait_send()
            cp.wait_recv()
        for cp in _exchange_copies(land_refs, land_refs, forward_send, forward_recv, "forward", receive_side=False):
            cp.start()
        token[...] = jnp.zeros_like(token)

    out = pl.pallas_call(
        body,
        name=name,
        out_shape=(
            pltpu.SemaphoreType.DMA((m * n,)), pltpu.SemaphoreType.DMA((m * n,)),
            *[pltpu.HBM(l.shape, l.dtype) for l in lands], jax.ShapeDtypeStruct((8, LANES), F32),
        ),
        in_specs=(*[_HBM_ONLY] * (2 * n), _SEM, _SEM, _SEM, _HBM),
        out_specs=(_SEM, _SEM, *[_HBM_ONLY] * n, pl.BlockSpec(memory_space=pltpu.VMEM)),
        input_output_aliases={n + i: 2 + i for i in range(n)},
        compiler_params=pltpu.CompilerParams(has_side_effects=_SIDE_EFFECT),
    )(*srcs, *lands, *sems, after)
    return out[:2], out[2 : 2 + n], out[-1]


def _forward_wait(sems, lands, after, name):
    n = len(lands)

    def body(*refs):
        land_refs = refs[:n]
        for cp in _exchange_copies(land_refs, land_refs, refs[n], refs[n + 1], "forward", receive_side=True):
            cp.wait_send()
            cp.wait_recv()

    return pl.pallas_call(
        body,
        name=name,
        out_shape=tuple(pltpu.HBM(l.shape, l.dtype) for l in lands),
        in_specs=(*[_HBM_ONLY] * n, _SEM, _SEM, _HBM),
        out_specs=tuple([_HBM_ONLY] * n),
        input_output_aliases={i: i for i in range(n)},
        compiler_params=pltpu.CompilerParams(has_side_effects=_SIDE_EFFECT),
    )(*lands, *sems, after)


def _rows_tile(r):
    return ROW_TILE if r % ROW_TILE == 0 else r


def _adamw(w, g, m, v):
    m = ADAM_B1 * m + (1.0 - ADAM_B1) * g
    v = ADAM_B2 * v + (1.0 - ADAM_B2) * (g * g)
    m_hat = m / (1.0 - ADAM_B1 ** ADAM_STEP)
    v_hat = v / (1.0 - ADAM_B2 ** ADAM_STEP)
    delta = -ADAM_LR * (m_hat / (jnp.sqrt(v_hat) + ADAM_EPS) + ADAM_WD * w)
    return delta, m, v


def _shard_update_direct(parts, chunks, w, m, v, me, name):
    _, r, c = w.shape
    br = _rows_tile(r)

    def body(me_ref, p_ref, own_ref, w_ref, m_ref, v_ref, g_ref, d_ref, nm_ref, nv_ref):
        g = None
        for n in range(N_DEV):
            part = jnp.where(me_ref[0] == n, own_ref[...], p_ref[n]).astype(F32)
            g = part if g is None else g + part
        g_ref[...] = g
        d_ref[...], nm_ref[...], nv_ref[...] = _adamw(w_ref[...], g, m_ref[...], v_ref[...])

    shard = pl.BlockSpec((None, br, c), lambda i, me: (0, i, 0))
    return pl.pallas_call(
        body,
        name=name,
        grid_spec=pltpu.PrefetchScalarGridSpec(
            num_scalar_prefetch=1,
            grid=(r // br,),
            in_specs=[
                pl.BlockSpec((N_DEV, br, c), lambda i, me: (0, i, 0)),
                pl.BlockSpec((None, br, c), lambda i, me: (me[0], i, 0)),
                shard, shard, shard,
            ],
            out_specs=[shard, shard, shard, shard],
        ),
        out_shape=[jax.ShapeDtypeStruct((1, r, c), F32)] * 4,
        compiler_params=_params(("parallel",)),
    )(me, parts, chunks, w, m, v)


def _small_update(parts, first_rows, ws, ms, vs):
    k = len(ws)

    def unpacked(rows, shape):
        if len(shape) == 2 and shape[1] <= LANES:
            return rows[0:1, : shape[1]]
        if len(shape) == 2:
            return jnp.concatenate([rows[r : r + 1] for r in range(shape[1] // LANES)], axis=1)
        return rows.reshape(shape)

    def body(p_ref, f_ref, *refs):
        w_refs, m_refs, v_refs = refs[:k], refs[k : 2 * k], refs[2 * k : 3 * k]
        outs, loss_ref = refs[3 * k : 7 * k], refs[7 * k]
        g, first = p_ref[0], f_ref[0]
        for n in range(1, N_DEV):
            g = g + p_ref[n]
            first = first + f_ref[n]
        g = jnp.concatenate([g[:8] + first, g[8:]], axis=0)
        off = 0
        for i, (_, rows) in enumerate(_SMALL):
            gi = unpacked(g[off : off + rows], w_refs[i].shape)
            off += rows
            outs[i][...] = gi
            outs[k + i][...], outs[2 * k + i][...], outs[3 * k + i][...] = _adamw(w_refs[i][...], gi, m_refs[i][...], v_refs[i][...])
        loss_ref[...] = g[off : off + 1, 0:1]

    out = pl.pallas_call(
        body,
        name="small_update",
        out_shape=[jax.ShapeDtypeStruct(w.shape, F32) for _ in range(4) for w in ws] + [jax.ShapeDtypeStruct((1, 1), F32)],
        compiler_params=pltpu.CompilerParams(vmem_limit_bytes=VMEM_LIMIT),
    )(parts, first_rows, *ws, *ms, *vs)
    return [out[a * k : (a + 1) * k] for a in range(4)], out[4 * k]


_SHARD_AXIS = (1, 1, 1, 0, 0, 0, 0)
_TRANSPOSED = (False, False, False, False, True, True, False)


def _full_from_gathered(t, axis):
    if axis == 0:
        return t.reshape(N_DEV * t.shape[1], t.shape[2])
    return t


_SMALL = (("norm1_g", 8), ("norm2_g", 8), ("norm_f_g", 8), ("b_forget", 8), ("pool_scale", 8), ("pool_mix", 512))


def _pack_small(vals, loss_row):
    parts = []
    for (name, rows), t in zip(_SMALL, vals):
        f = t.astype(F32).reshape(-1)
        f = jnp.concatenate([f, jnp.zeros((rows * LANES - f.shape[0],), F32)]).reshape(rows, LANES)
        parts.append(f)
    parts.append(loss_row)
    return jnp.concatenate(parts, axis=0)


def _local_grads(x, tgt, g1, g2, gf, b_forget, pool_mix, pool_scale, w_in, fwd_token, out_weights, ffn_weights, ffn_grads_out, out_grads_out, small_grads_out, in_grads_out, norm1_grad_out):
    n_seq, S, _ = x.shape
    T = n_seq * S
    x2 = x.reshape(T, D_MODEL)
    tg2 = tgt.reshape(T, D_MODEL)
    w_uqkv, w_fl, w_g = w_in
    b_pad = jnp.concatenate([b_forget.reshape(1, N_HEADS), jnp.zeros((1, FL_PAD - N_HEADS), F32)], axis=1)
    mix_b = pool_mix.reshape(len(POOL_WINDOWS), GROUP_DIM, GROUP_DIM).astype(BF16)
    scale = pool_scale.reshape(1, POOL_WIDTH)
    g1 = g1.reshape(1, D_MODEL)
    g2 = g2.reshape(1, D_MODEL)
    gf = gf.reshape(1, D_MODEL)

    h, u, qkv, fl, gates = _in_proj(x2, g1, w_uqkv, w_fl, w_g, fwd_token)
    fcol = _forget_fwd(fl, b_pad, n_seq, S)
    pm, p2, p3 = _pool_fwd(u, mix_b, scale, n_seq, S)
    a, lse = _attn_fwd(qkv, fcol, n_seq, S)
    w_po, w_ao, w_out = out_weights(a)
    merged, x1, attn_y, pool_y = _mix_out(a, p3, gates, x2, w_ao, w_po, w_out)
    w_gate_t, w_up_t, w_down = ffn_weights(x1)
    h2, gate, up, act, dx2, loss_rows, dgf = _ffn_fwd(x1, g2, gf, tg2, w_gate_t, w_up_t, w_down)

    dgate, dup, dx1, dg2 = _ffn_bwd(dx2, gate, up, x1, g2, w_gate_t, w_up_t, w_down)
    bwd_token = ffn_grads_out(_matmul_tn(dgate, h2, "dw_ffn_gate"), _matmul_tn(dup, h2, "dw_ffn_up"), _matmul_tn(act, dx2, "dw_ffn_down"))
    dgates, dpy, day, da, dp2, dscale = _mix_bwd(dx1, gates, pool_y, attn_y, p2, scale, w_out, w_ao, w_po, bwd_token)
    out_token = out_grads_out(
        _matmul_tn(p3, dpy, "dw_pool_out", col_chunks=True), _matmul_tn(a, day, "dw_attn_out", col_chunks=True), _matmul_tn(merged, dx1, "dw_out")
    )
    du, dmix = _pool_bwd(dp2, pm, mix_b, out_token, n_seq, S)
    dq, dk, dv, dfk, dfq = _attn_bwd(qkv, da, a, fcol, lse, n_seq, S)
    dfl, db = _forget_bwd(dfk, dfq, fl, b_pad, n_seq, S)
    small_token = small_grads_out((jnp.zeros_like(g1), dg2, dgf, db[:, :N_HEADS], dscale, dmix), loss_rows)
    in_token = in_grads_out(_dw_in(h, du, dq, dk, dv, dfl, dgates, small_token))
    dx, dg1 = _in_proj_bwd(du, dq, dk, dv, dfl, dgates, x2, dx1, g1, w_uqkv, w_fl, w_g, in_token)
    norm1_grad_out(dg1)
    return dx.reshape(n_seq, S, D_MODEL)


def kernel(x, norm1_g, w_in, b_forget, pool_mix, pool_scale, w_pool_out, w_attn_out, w_out, norm2_g, w_ffn_gate, w_ffn_up, w_ffn_down, norm_f_g, loss_target, m_norm1_g, m_w_in, m_b_forget, m_pool_mix, m_pool_scale, m_w_pool_out, m_w_attn_out, m_w_out, m_norm2_g, m_w_ffn_gate, m_w_ffn_up, m_w_ffn_down, m_norm_f_g, v_norm1_g, v_w_in, v_b_forget, v_pool_mix, v_pool_scale, v_w_pool_out, v_w_attn_out, v_w_out, v_norm2_g, v_w_ffn_gate, v_w_ffn_up, v_w_ffn_down, v_norm_f_g):
    names = ("w_in", "w_pool_out", "w_attn_out", "w_out", "w_ffn_gate", "w_ffn_up", "w_ffn_down")
    w_sh = (w_in, w_pool_out, w_attn_out, w_out, w_ffn_gate, w_ffn_up, w_ffn_down)
    m_sh = (m_w_in, m_w_pool_out, m_w_attn_out, m_w_out, m_w_ffn_gate, m_w_ffn_up, m_w_ffn_down)
    v_sh = (v_w_in, v_w_pool_out, v_w_attn_out, v_w_out, v_w_ffn_gate, v_w_ffn_up, v_w_ffn_down)

    cx, cy, cc = _position()
    me = 4 * cx + 2 * cy + cc
    def stored(t, transposed):
        return jnp.transpose(t, (0, 2, 1)) if transposed else t

    w_sh, m_sh, v_sh = ([stored(t, tr) for t, tr in zip(ts, _TRANSPOSED)] for ts in (w_sh, m_sh, v_sh))
    shards = [w[0].astype(BF16) for w in w_sh]
    (gathered_in,) = _all_gather(shards[:1], "w_in_all_gather")
    out_sems = _exchange_start(shards[1:4], gathered_in, "out_weights_gather_start", "gather")
    ffn_sems = _exchange_start(shards[4:], out_sems[3], "ffn_weights_gather_start", "gather_half")
    no_order = jnp.zeros((8, LANES), F32)
    started = {}

    def out_weights(after):
        forward_sems, lands, token = _gather_forward(*ffn_sems[:3], after, "ffn_weights_forward_start")
        started["forward"] = (forward_sems, lands)
        _, lands = _exchange_wait(*out_sems[:3], token, "out_weights_gather_wait", "gather")
        return [_full_from_gathered(t, axis) for t, axis in zip(lands, _SHARD_AXIS[out])]

    def ffn_weights(after):
        lands = _forward_wait(*started["forward"], after, "ffn_weights_gather_wait")
        return [_full_from_gathered(t, axis) for t, axis in zip(lands, _SHARD_AXIS[ffn])]

    def scatter_grads(key, name):
        def start(*whole_grads):
            chunks = [
                t if axis == 1 else t.reshape(N_DEV, -1, t.shape[1])
                for t, axis in zip(whole_grads, _SHARD_AXIS[key])
            ]
            started[name] = _exchange_start(chunks, no_order, name + "_grads_scatter_start", "scatter")
            return started[name][3]

        return start

    def gather_small(small, loss_rows):
        started["small"] = _exchange_start([_pack_small(small, loss_rows)], no_order, "small_grads_gather_start", "gather")
        return started["small"][3]

    def scatter_w_in(chunks_in):
        started["in"] = _exchange_start([chunks_in], no_order, "w_in_grads_scatter_start", "scatter")
        return started["in"][3]

    def gather_norm1(dg1):
        rows = jnp.reshape(dg1, (8, LANES))
        started["norm1"] = _exchange_start([rows], no_order, "norm1_grad_gather_start", "gather")

    ffn, out = slice(4, 7), slice(1, 4)
    grad_x = _local_grads(
        x, loss_target, norm1_g, norm2_g, norm_f_g, b_forget, pool_mix, pool_scale, _w_in_pieces(gathered_in), ffn_sems[3],
        out_weights, ffn_weights, scatter_grads(ffn, "ffn"), scatter_grads(out, "out"), gather_small, scatter_w_in, gather_norm1,
    )
    me_index = jnp.reshape(me, (1,)).astype(jnp.int32)

    def scattered_updates(key, name):
        srcs, lands = _exchange_wait(*started[name][:3], started["norm1"][3], name + "_grads_scatter_wait", "scatter")
        return [
            _shard_update_direct(p, s, w, m, v, me_index, "update_" + n)
            for p, s, w, m, v, n in zip(lands, srcs, w_sh[key], m_sh[key], v_sh[key], names[key])
        ]

    updates_out, updates_ffn = scattered_updates(out, "out"), scattered_updates(ffn, "ffn")

    small_w = (norm1_g, norm2_g, norm_f_g, b_forget, pool_scale, pool_mix)
    small_m = (m_norm1_g, m_norm2_g, m_norm_f_g, m_b_forget, m_pool_scale, m_pool_mix)
    small_v = (v_norm1_g, v_norm2_g, v_norm_f_g, v_b_forget, v_pool_scale, v_pool_mix)
    (sent_in,), (parts_in,) = _exchange_wait(*started["in"][:3], updates_ffn[-1][0], "w_in_grads_scatter_wait", "scatter")
    update_in = _shard_update_direct(parts_in, sent_in, w_in, m_w_in, v_w_in, me_index, "update_w_in")

    def gathered_small(key, after, name):
        _, lands = _exchange_wait(*started[key][:3], after, name, "gather")
        return lands[0]

    parts = gathered_small("small", update_in[0], "small_grads_gather_wait")
    first_rows = gathered_small("norm1", parts, "norm1_grad_gather_wait")
    (g_s, d_s, nm_s, nv_s), loss = _small_update(parts, first_rows, small_w, small_m, small_v)
    g_w, d_w, nm_w, nv_w = zip(*(
        [stored(t, tr) for t in u] for u, tr in zip([update_in] + updates_out + updates_ffn, _TRANSPOSED)
    ))
    loss = loss.reshape(())
    (g1, g2, gf, gb, gsc, gmix), (d1, d2, df, db_, dsc, dmx) = g_s, d_s
    (m1, m2, mf, mb, msc, mmx), (v1, v2, vf, vb, vsc, vmx) = nm_s, nv_s

    def ordered(n1, win, b, mix, sc, wpo, wao, wout, n2, wg, wu, wd, nf):
        return (n1, win, b, mix, sc, wpo, wao, wout, n2, wg, wu, wd, nf)

    grads = ordered(g1, g_w[0], gb, gmix, gsc, g_w[1], g_w[2], g_w[3], g2, g_w[4], g_w[5], g_w[6], gf)
    deltas = ordered(d1, d_w[0], db_, dmx, dsc, d_w[1], d_w[2], d_w[3], d2, d_w[4], d_w[5], d_w[6], df)
    new_m = ordered(m1, nm_w[0], mb, mmx, msc, nm_w[1], nm_w[2], nm_w[3], m2, nm_w[4], nm_w[5], nm_w[6], mf)
    new_v = ordered(v1, nv_w[0], vb, vmx, vsc, nv_w[1], nv_w[2], nv_w[3], v2, nv_w[4], nv_w[5], nv_w[6], vf)
    return (loss, grad_x, *grads, *deltas, *new_m, *new_v)
```

```python
import jax
import jax.numpy as jnp
from jax import lax
from jax.experimental import pallas as pl
from jax.experimental.pallas import tpu as pltpu

F32 = jnp.float32
BF16 = jnp.bfloat16
MESH = pl.DeviceIdType.MESH

D_MODEL = 1024
POOL_WINDOWS = (2, 4, 8, 16)
POOL_WIDTH = 512
GROUP_DIM = 128
ATTN_WIDTH = 512
HEAD_DIM = 64
N_HEADS = 8
N_PAIRS = 4
D_FF = 2816
RMS_EPS = 1e-6
N_DEV = 8
LANES = 128
FL_PAD = 128

ADAM_LR = 0.001
ADAM_B1 = 0.9
ADAM_B2 = 0.999
ADAM_EPS = 1e-08
ADAM_WD = 0.01
ADAM_STEP = 10

VMEM_LIMIT = 56 * 1024 * 1024
VMEM_LIMIT_MAX = 60 * 1024 * 1024
ROW_TILE = 512
ATTN_BLOCK = 512
FF_CHUNK = 256
FF_ROW_TILE = 512
DW_TOKENS = 2048


def _mm(a, b):
    return jnp.dot(a, b, preferred_element_type=F32)


def _mm_nt(a, b):
    return lax.dot_general(a, b, (((1,), (1,)), ((), ())), preferred_element_type=F32)


def _mm_tn(a, b):
    return lax.dot_general(a, b, (((0,), (0,)), ((), ())), preferred_element_type=F32)


def _whole_cols(w_ref):
    if len(w_ref.shape) == 2:
        return w_ref[...]
    return jnp.concatenate([w_ref[d] for d in range(w_ref.shape[0])], axis=1)


def _sigmoid(x):
    return 1.0 / (1.0 + jnp.exp(-x))


def _params(sem, vmem=VMEM_LIMIT):
    return pltpu.CompilerParams(dimension_semantics=sem, vmem_limit_bytes=vmem)


def _const_spec(shape):
    nd = len(shape)
    return pl.BlockSpec(shape, lambda *_: (0,) * nd, pipeline_mode=pl.Buffered(1))


def _rms_fwd(x, g):
    r = lax.rsqrt(jnp.mean(x * x, axis=-1, keepdims=True) + RMS_EPS)
    xh = x * r
    return xh * g, xh, r


def _rms_bwd(dy, xh, r, g):
    dxh = dy * g
    dx = r * (dxh - xh * jnp.mean(dxh * xh, axis=-1, keepdims=True))
    return dx, dy * xh


def _in_proj(x, g1, w_uqkv, w_fl, w_g, token):
    T = x.shape[0]
    tm = ROW_TILE

    def body(x_ref, g_ref, wa_ref, wf_ref, wg_ref, token_ref, h_ref, u_ref, qkv_ref, fl_ref, gt_ref):
        h, _, _ = _rms_fwd(x_ref[...], g_ref[...])
        hb = h.astype(BF16)
        h_ref[...] = hb
        z = _mm(hb, wa_ref[...])
        u_ref[...] = z[:, :POOL_WIDTH]
        qkv_ref[...] = z[:, POOL_WIDTH:].astype(BF16)
        fl_ref[...] = _mm(hb, wf_ref[...])
        gt_ref[...] = _mm(hb, wg_ref[...]).astype(BF16)

    row = lambda n: pl.BlockSpec((tm, n), lambda i: (i, 0))
    return pl.pallas_call(
        body,
        name="in_proj",
        grid=(T // tm,),
        in_specs=[row(D_MODEL), _const_spec((1, D_MODEL)), _const_spec(w_uqkv.shape), _const_spec(w_fl.shape), _const_spec(w_g.shape), _HBM],
        out_specs=[row(D_MODEL), row(POOL_WIDTH), row(3 * ATTN_WIDTH), row(FL_PAD), row(2 * D_MODEL)],
        out_shape=[
            jax.ShapeDtypeStruct((T, D_MODEL), BF16),
            jax.ShapeDtypeStruct((T, POOL_WIDTH), F32),
            jax.ShapeDtypeStruct((T, 3 * ATTN_WIDTH), BF16),
            jax.ShapeDtypeStruct((T, FL_PAD), F32),
            jax.ShapeDtypeStruct((T, 2 * D_MODEL), BF16),
        ],
        compiler_params=_params(("parallel",)),
    )(x, g1, w_uqkv, w_fl, w_g, token)


def _log_sigmoid(x):
    return jnp.minimum(x, 0.0) - jnp.log(1.0 + jnp.exp(-jnp.abs(x)))


def _forget_fwd(fl, b_pad, n_seq, S):
    def body(fl_ref, b_ref, fcol_ref):
        lf = _log_sigmoid(fl_ref[...] + b_ref[...])
        t = lf.T
        lane = lax.broadcasted_iota(jnp.int32, t.shape, 1)
        k = 1
        while k < S:
            t = t + jnp.where(lane >= k, pltpu.roll(t, k, 1), 0.0)
            k *= 2
        fcol_ref[...] = t.T

    return pl.pallas_call(
        body,
        name="forget_fwd",
        grid=(n_seq,),
        in_specs=[pl.BlockSpec((S, FL_PAD), lambda s: (s, 0)), _const_spec((1, FL_PAD))],
        out_specs=pl.BlockSpec((S, FL_PAD), lambda s: (s, 0)),
        out_shape=jax.ShapeDtypeStruct((n_seq * S, FL_PAD), F32),
        compiler_params=_params(("parallel",)),
    )(fl, b_pad)


def _window_pick(g, v2, v4, v8, v16):
    return jnp.where(g == 0, v2, jnp.where(g == 1, v4, jnp.where(g == 2, v8, v16)))


def _pool_fwd(u, mix_b, scale, n_seq, S):
    T = n_seq * S

    def body(u_ref, mix_ref, sc_ref, pm_ref, p2_ref, p3_ref):
        g = pl.program_id(1)
        uu = u_ref[...]
        row = lax.broadcasted_iota(jnp.int32, uu.shape, 0)

        def back(a, k):
            return jnp.where(row >= k, pltpu.roll(a, k, 0), 0.0)

        s2 = uu + back(uu, 1)
        s4 = s2 + back(s2, 2)
        s8 = s4 + back(s4, 4)
        s16 = s8 + back(s8, 8)
        w = _window_pick(g, 2.0, 4.0, 8.0, 16.0)
        cnt = jnp.minimum((row + 1).astype(F32), w)
        pm = _window_pick(g, s2, s4, s8, s16) / cnt - uu
        pmb = pm.astype(BF16)
        pm_ref[...] = pmb
        p2 = _mm(pmb, mix_ref[...])
        p2_ref[...] = p2
        p3_ref[...] = (p2 * sc_ref[...]).astype(BF16)

    grp = pl.BlockSpec((S, GROUP_DIM), lambda s, g: (s, g))
    return pl.pallas_call(
        body,
        name="pool_fwd",
        grid=(n_seq, len(POOL_WINDOWS)),
        in_specs=[
            grp,
            pl.BlockSpec((None, GROUP_DIM, GROUP_DIM), lambda s, g: (g, 0, 0)),
            pl.BlockSpec((1, GROUP_DIM), lambda s, g: (0, g)),
        ],
        out_specs=[grp, grp, grp],
        out_shape=[
            jax.ShapeDtypeStruct((T, POOL_WIDTH), BF16),
            jax.ShapeDtypeStruct((T, POOL_WIDTH), F32),
            jax.ShapeDtypeStruct((T, POOL_WIDTH), BF16),
        ],
        compiler_params=_params(("parallel", "parallel")),
    )(u, mix_b, scale)


def _split3(v):
    hi = v.astype(BF16).astype(F32)
    r = v - hi
    mid = r.astype(BF16).astype(F32)
    lo = (r - mid).astype(BF16).astype(F32)
    return hi, mid, lo


def _bias_lanes(v):
    hi, mid, lo = _split3(v)
    lane = lax.broadcasted_iota(jnp.int32, (1, LANES), 1)
    packed = jnp.where(lane < N_HEADS, hi, jnp.where(lane < 2 * N_HEADS, pltpu.roll(mid, N_HEADS, 1), pltpu.roll(lo, 2 * N_HEADS, 1)))
    return jnp.where(lane < 3 * N_HEADS, packed, 0.0).astype(BF16)


def _bias_placement(slot):
    row = lax.broadcasted_iota(jnp.int32, (LANES, N_HEADS * LANES), 0)
    col = lax.broadcasted_iota(jnp.int32, (LANES, N_HEADS * LANES), 1)
    h = col // LANES
    n = col % LANES - jnp.where(h % 2 == 0, HEAD_DIM, 0) - 3 * slot
    return ((n >= 0) & (n < 3) & (row == N_HEADS * n + h)).astype(BF16)


def _augment(xp, h, bias, ones_slot):
    lane = lax.broadcasted_iota(jnp.int32, (1, LANES), 1)
    hh = h % 2
    head = (lane >= HEAD_DIM * hh) & (lane < HEAD_DIM * (hh + 1))
    b = HEAD_DIM * (1 - hh)
    rest = jnp.zeros_like(xp) if bias is None else bias[:, h * LANES : (h + 1) * LANES]
    out = jnp.where(head, xp, rest)
    if ones_slot is not None:
        out = jnp.where((lane >= b + 3 * ones_slot) & (lane < b + 3 * ones_slot + 3), jnp.ones_like(xp), out)
    return out


def _attn_fwd(qkv, fcol, n_seq, S):
    T = n_seq * S
    tb = ATTN_BLOCK
    nq = S // tb
    scale = HEAD_DIM ** -0.5

    def body(q_ref, k_ref, v_ref, fc_ref, o_ref, st_ref, qa_sc, ka_sc, m_sc, l_sc, acc_sc):
        i = pl.program_id(1)
        lane = lax.broadcasted_iota(jnp.int32, (1, LANES), 1)
        low = lane < HEAD_DIM

        @pl.when(i == 0)
        def _():
            place = _bias_placement(1)

            def rows_ka(r, carry):
                r0 = pl.multiple_of(r * tb, tb)
                bias = _mm(_bias_lanes(-fc_ref[pl.ds(r0, tb), :]), place).astype(BF16)
                for h in range(N_HEADS):
                    kp = k_ref[pl.ds(r0, tb), (h // 2) * LANES : (h // 2 + 1) * LANES] * scale
                    ka_sc[h, pl.ds(r0, tb), :] = _augment(kp, h, bias, 0)
                return carry

            lax.fori_loop(0, nq, rows_ka, 0)

        q0 = pl.multiple_of(i * tb, tb)
        bias = _mm(_bias_lanes(fc_ref[pl.ds(q0, tb), :]), _bias_placement(0)).astype(BF16)
        for h in range(N_HEADS):
            qa_sc[h] = _augment(q_ref[:, (h // 2) * LANES : (h // 2 + 1) * LANES], h, bias, 1)
        m_sc[...] = jnp.full(m_sc.shape, -jnp.inf, F32)
        l_sc[...] = jnp.zeros_like(l_sc)
        acc_sc[...] = jnp.zeros_like(acc_sc)
        causal = lax.broadcasted_iota(jnp.int32, (tb, tb), 1) <= lax.broadcasted_iota(jnp.int32, (tb, tb), 0)

        def step(j, masked):
            c0 = pl.multiple_of(j * tb, tb)
            for p in range(N_PAIRS):
                vb = v_ref[pl.ds(c0, tb), p * LANES : (p + 1) * LANES]
                pv, al = [], []
                for hh in range(2):
                    h = 2 * p + hh
                    s = _mm_nt(qa_sc[h], ka_sc[h, pl.ds(c0, tb), :])
                    if masked:
                        s = jnp.where(causal, s, -jnp.inf)
                    m_old = m_sc[h]
                    m_new = jnp.maximum(m_old, jnp.max(s, axis=1, keepdims=True))
                    alpha = jnp.exp(m_old - m_new)
                    pe = jnp.exp(s - jnp.concatenate([m_new] * (tb // LANES), axis=1))
                    l_sc[h] = alpha * l_sc[h] + jnp.sum(pe, axis=1, keepdims=True)
                    m_sc[h] = m_new
                    pv.append(_mm(pe.astype(BF16), vb))
                    al.append(alpha)
                acc_sc[p] = jnp.where(low, al[0], al[1]) * acc_sc[p] + jnp.where(low, pv[0], pv[1])

        def loop_body(j, carry):
            step(j, False)
            return carry

        lax.fori_loop(0, i, loop_body, 0)
        step(i, True)
        st = jnp.zeros((tb, LANES), F32)
        for p in range(N_PAIRS):
            lp = jnp.where(low, l_sc[2 * p], l_sc[2 * p + 1])
            o_ref[:, p * LANES : (p + 1) * LANES] = (acc_sc[p] / lp).astype(BF16)
            for h in (2 * p, 2 * p + 1):
                st = jnp.where(lane == h, m_sc[h] + jnp.log(l_sc[h]), st)
        st_ref[...] = st

    return pl.pallas_call(
        body,
        name="attn_fwd",
        grid=(n_seq, nq),
        in_specs=[
            pl.BlockSpec((tb, ATTN_WIDTH), lambda s, i: (s * nq + i, 0)),
            pl.BlockSpec((S, ATTN_WIDTH), lambda s, i: (s, 1)),
            pl.BlockSpec((S, ATTN_WIDTH), lambda s, i: (s, 2)),
            pl.BlockSpec((S, LANES), lambda s, i: (s, 0)),
        ],
        out_specs=[
            pl.BlockSpec((tb, ATTN_WIDTH), lambda s, i: (s * nq + i, 0)),
            pl.BlockSpec((tb, LANES), lambda s, i: (s * nq + i, 0)),
        ],
        out_shape=[jax.ShapeDtypeStruct((T, ATTN_WIDTH), BF16), jax.ShapeDtypeStruct((T, LANES), F32)],
        scratch_shapes=[
            pltpu.VMEM((N_HEADS, tb, LANES), BF16),
            pltpu.VMEM((N_HEADS, S, LANES), BF16),
            pltpu.VMEM((N_HEADS, tb, LANES), F32),
            pltpu.VMEM((N_HEADS, tb, LANES), F32),
            pltpu.VMEM((N_PAIRS, tb, LANES), F32),
        ],
        compiler_params=_params(("parallel", "arbitrary")),
    )(qkv, qkv, qkv, fcol)


def _mix_out(a, p3, gates, x, w_ao, w_po, w_out):
    T = x.shape[0]
    tm = ROW_TILE

    def body(a_ref, p3_ref, gt_ref, x_ref, wao_ref, wpo_ref, wout_ref, mg_ref, x1_ref, ay_ref, py_ref):
        ay = _mm(a_ref[...], _whole_cols(wao_ref))
        py = _mm(p3_ref[...], _whole_cols(wpo_ref))
        ay_ref[...] = ay.astype(BF16)
        py_ref[...] = py.astype(BF16)
        sp = _sigmoid(gt_ref[:, :D_MODEL].astype(F32))
        sa = _sigmoid(gt_ref[:, D_MODEL:].astype(F32))
        mb = (sp * py + sa * ay).astype(BF16)
        mg_ref[...] = mb
        x1_ref[...] = x_ref[...] + _mm(mb, wout_ref[...])

    row = lambda n: pl.BlockSpec((tm, n), lambda i: (i, 0))
    return pl.pallas_call(
        body,
        name="mix_out",
        grid=(T // tm,),
        in_specs=[
            row(ATTN_WIDTH), row(POOL_WIDTH), row(2 * D_MODEL), row(D_MODEL),
            _const_spec(w_ao.shape), _const_spec(w_po.shape), _const_spec(w_out.shape),
        ],
        out_specs=[row(D_MODEL), row(D_MODEL), row(D_MODEL), row(D_MODEL)],
        out_shape=[
            jax.ShapeDtypeStruct((T, D_MODEL), BF16), jax.ShapeDtypeStruct((T, D_MODEL), F32),
            jax.ShapeDtypeStruct((T, D_MODEL), BF16), jax.ShapeDtypeStruct((T, D_MODEL), BF16),
        ],
        compiler_params=_params(("parallel",)),
    )(a, p3, gates, x, w_ao, w_po, w_out)


def _ffn_fwd(x1, g2, gf, tgt, w_gate_t, w_up_t, w_down):
    T = x1.shape[0]
    tm = min(T, FF_ROW_TILE)
    nt = T // tm
    nc = D_FF // FF_CHUNK

    def body(x1_ref, g2_ref, gf_ref, tg_ref, wg_ref, wu_ref, wd_ref, h2_ref, gate_ref, up_ref, act_ref, dx2_ref, loss_ref, dgf_ref):
        x1v = x1_ref[...]
        h2, _, _ = _rms_fwd(x1v, g2_ref[...])
        h2b = h2.astype(BF16)
        h2_ref[...] = h2b
        for c in range(nc):
            sl = slice(c * FF_CHUNK, (c + 1) * FF_CHUNK)
            gate = _mm_nt(h2b, wg_ref[sl, :])
            up = _mm_nt(h2b, wu_ref[sl, :])
            gate_ref[:, sl] = gate.astype(BF16)
            up_ref[:, sl] = up.astype(BF16)
            act_ref[:, sl] = (gate * _sigmoid(gate) * up).astype(BF16)
        acc = x1v + _mm(act_ref[...], wd_ref[...])
        gfv = gf_ref[...]
        y, xh, r = _rms_fwd(acc, gfv)
        err = y - tg_ref[...]
        part = 0.5 * jnp.sum(jnp.mean(err * err, axis=-1, keepdims=True), axis=0, keepdims=True)
        dx2, dgrow = _rms_bwd(err * (1.0 / D_MODEL), xh, r, gfv)
        dx2_ref[...] = dx2

        @pl.when(pl.program_id(0) == 0)
        def _():
            dgf_ref[...] = jnp.zeros_like(dgf_ref)
            loss_ref[...] = jnp.zeros_like(loss_ref)

        dgf_ref[...] += jnp.sum(dgrow, axis=0, keepdims=True)
        loss_ref[...] += jnp.broadcast_to(part, loss_ref.shape)

    row = lambda n: pl.BlockSpec((tm, n), lambda i: (i, 0))
    return pl.pallas_call(
        body,
        name="ffn_fwd",
        grid=(nt,),
        in_specs=[
            row(D_MODEL), _const_spec((1, D_MODEL)), _const_spec((1, D_MODEL)), row(D_MODEL),
            _const_spec(w_gate_t.shape), _const_spec(w_up_t.shape), _const_spec(w_down.shape),
        ],
        out_specs=[
            row(D_MODEL), row(D_FF), row(D_FF), row(D_FF), row(D_MODEL),
            pl.BlockSpec((8, LANES), lambda i: (0, 0)),
            pl.BlockSpec((1, D_MODEL), lambda i: (0, 0)),
        ],
        out_shape=[
            jax.ShapeDtypeStruct((T, D_MODEL), BF16),
            jax.ShapeDtypeStruct((T, D_FF), BF16),
            jax.ShapeDtypeStruct((T, D_FF), BF16),
            jax.ShapeDtypeStruct((T, D_FF), BF16),
            jax.ShapeDtypeStruct((T, D_MODEL), F32),
            jax.ShapeDtypeStruct((8, LANES), F32),
            jax.ShapeDtypeStruct((1, D_MODEL), F32),
        ],
        compiler_params=_params(("arbitrary",)),
    )(x1, g2, gf, tgt, w_gate_t, w_up_t, w_down)


def _ffn_bwd(dx2, gate, up, x1, g2, w_gate_t, w_up_t, w_down):
    T = x1.shape[0]
    tm = min(T, FF_ROW_TILE)
    nc = D_FF // FF_CHUNK

    def body(dx2_ref, gate_ref, up_ref, x1_ref, g2_ref, wg_ref, wu_ref, wd_ref, dgate_ref, dup_ref, dx1_ref, dg2_ref):
        dx2v = dx2_ref[...]
        dx2b = dx2v.astype(BF16)
        for c in range(nc):
            sl = slice(c * FF_CHUNK, (c + 1) * FF_CHUNK)
            dact = _mm_nt(dx2b, wd_ref[sl, :])
            gate = gate_ref[:, sl].astype(F32)
            sg = _sigmoid(gate)
            silu = gate * sg
            dgate = (dact * up_ref[:, sl].astype(F32) * (sg * (1.0 + gate * (1.0 - sg)))).astype(BF16)
            dup = (dact * silu).astype(BF16)
            dgate_ref[:, sl] = dgate
            dup_ref[:, sl] = dup
        dh2 = _mm(dgate_ref[...], wg_ref[...]) + _mm(dup_ref[...], wu_ref[...])
        g2v = g2_ref[...]
        _, xh, r = _rms_fwd(x1_ref[...], g2v)
        dxn, dgrow = _rms_bwd(dh2, xh, r, g2v)
        dx1_ref[...] = dx2v + dxn

        @pl.when(pl.program_id(0) == 0)
        def _():
            dg2_ref[...] = jnp.zeros_like(dg2_ref)

        dg2_ref[...] += jnp.sum(dgrow, axis=0, keepdims=True)

    row = lambda n: pl.BlockSpec((tm, n), lambda i: (i, 0))
    return pl.pallas_call(
        body,
        name="ffn_bwd",
        grid=(T // tm,),
        in_specs=[
            row(D_MODEL), row(D_FF), row(D_FF), row(D_MODEL), _const_spec((1, D_MODEL)),
            _const_spec(w_gate_t.shape), _const_spec(w_up_t.shape), _const_spec(w_down.shape),
        ],
        out_specs=[row(D_FF), row(D_FF), row(D_MODEL), pl.BlockSpec((1, D_MODEL), lambda i: (0, 0))],
        out_shape=[
            jax.ShapeDtypeStruct((T, D_FF), BF16),
            jax.ShapeDtypeStruct((T, D_FF), BF16),
            jax.ShapeDtypeStruct((T, D_MODEL), F32),
            jax.ShapeDtypeStruct((1, D_MODEL), F32),
        ],
        compiler_params=_params(("arbitrary",), VMEM_LIMIT_MAX),
    )(dx2, gate, up, x1, g2, w_gate_t, w_up_t, w_down)


def _mix_bwd(dx1, gates, pool_y, attn_y, p2, scale, w_out, w_ao, w_po, token):
    T = dx1.shape[0]
    tm = ROW_TILE

    def body(dx1_ref, gt_ref, py_ref, ay_ref, p2_ref, sc_ref, wout_ref, wao_ref, wpo_ref, token_ref, dgt_ref, dpy_ref, day_ref, da_ref, dp2_ref, dsc_ref):
        dm = _mm_nt(dx1_ref[...].astype(BF16), wout_ref[...])
        sp = _sigmoid(gt_ref[:, :D_MODEL].astype(F32))
        sa = _sigmoid(gt_ref[:, D_MODEL:].astype(F32))
        dgt_ref[:, :D_MODEL] = (dm * py_ref[...].astype(F32) * (sp * (1.0 - sp))).astype(BF16)
        dgt_ref[:, D_MODEL:] = (dm * ay_ref[...].astype(F32) * (sa * (1.0 - sa))).astype(BF16)
        dpy = (dm * sp).astype(BF16)
        day = (dm * sa).astype(BF16)
        dpy_ref[...] = dpy
        day_ref[...] = day
        da_ref[...] = _mm_nt(day, _whole_cols(wao_ref)).astype(BF16)
        dp3 = _mm_nt(dpy, _whole_cols(wpo_ref))
        dp2_ref[...] = (dp3 * sc_ref[...]).astype(BF16)

        @pl.when(pl.program_id(0) == 0)
        def _():
            dsc_ref[...] = jnp.zeros_like(dsc_ref)

        dsc_ref[...] += jnp.sum(dp3 * p2_ref[...], axis=0, keepdims=True)

    row = lambda n: pl.BlockSpec((tm, n), lambda i: (i, 0))
    return pl.pallas_call(
        body,
        name="mix_bwd",
        grid=(T // tm,),
        in_specs=[
            row(D_MODEL), row(2 * D_MODEL), row(D_MODEL), row(D_MODEL), row(POOL_WIDTH), _const_spec((1, POOL_WIDTH)),
            _const_spec(w_out.shape), _const_spec(w_ao.shape), _const_spec(w_po.shape), _HBM,
        ],
        out_specs=[row(2 * D_MODEL), row(D_MODEL), row(D_MODEL), row(ATTN_WIDTH), row(POOL_WIDTH), pl.BlockSpec((1, POOL_WIDTH), lambda i: (0, 0))],
        out_shape=[
            jax.ShapeDtypeStruct((T, 2 * D_MODEL), BF16),
            jax.ShapeDtypeStruct((T, D_MODEL), BF16),
            jax.ShapeDtypeStruct((T, D_MODEL), BF16),
            jax.ShapeDtypeStruct((T, ATTN_WIDTH), BF16),
            jax.ShapeDtypeStruct((T, POOL_WIDTH), BF16),
            jax.ShapeDtypeStruct((1, POOL_WIDTH), F32),
        ],
        compiler_params=_params(("arbitrary",)),
    )(dx1, gates, pool_y, attn_y, p2, scale, w_out, w_ao, w_po, token)


def _pool_bwd(dp2, pm, mix_b, token, n_seq, S):
    T = n_seq * S

    def body(dp2_ref, pm_ref, mix_ref, token_ref, du_ref, dmix_ref):
        g = pl.program_id(0)
        dp2v = dp2_ref[...]
        dpm = _mm_nt(dp2v, mix_ref[...])
        row = lax.broadcasted_iota(jnp.int32, dpm.shape, 0)
        w = _window_pick(g, 2.0, 4.0, 8.0, 16.0)
        e = dpm / jnp.minimum((row + 1).astype(F32), w)

        def ahead(a, k):
            return jnp.where(row < S - k, pltpu.roll(a, S - k, 0), 0.0)

        r2 = e + ahead(e, 1)
        r4 = r2 + ahead(r2, 2)
        r8 = r4 + ahead(r4, 4)
        r16 = r8 + ahead(r8, 8)
        du_ref[...] = (_window_pick(g, r2, r4, r8, r16) - dpm).astype(BF16)

        @pl.when(pl.program_id(1) == 0)
        def _():
            dmix_ref[...] = jnp.zeros_like(dmix_ref)

        dmix_ref[...] += _mm_tn(pm_ref[...], dp2v)

    grp = pl.BlockSpec((S, GROUP_DIM), lambda g, s: (s, g))
    mixs = pl.BlockSpec((None, GROUP_DIM, GROUP_DIM), lambda g, s: (g, 0, 0))
    return pl.pallas_call(
        body,
        name="pool_bwd",
        grid=(len(POOL_WINDOWS), n_seq),
        in_specs=[grp, grp, mixs, _HBM],
        out_specs=[grp, mixs],
        out_shape=[jax.ShapeDtypeStruct((T, POOL_WIDTH), BF16), jax.ShapeDtypeStruct((len(POOL_WINDOWS), GROUP_DIM, GROUP_DIM), F32)],
        compiler_params=_params(("parallel", "arbitrary")),
    )(dp2, pm, mix_b, token)


def _attn_bwd(qkv, da, a, fcol, lse, n_seq, S):
    T = n_seq * S
    tb = ATTN_BLOCK
    nb = S // tb
    scale = HEAD_DIM ** -0.5

    def body(q_ref, k_ref, v_ref, do_ref, o_ref, fc_ref, st_ref, dq_ref, dk_ref, dv_ref, dfk_ref, dfq_ref,
             qa_sc, doa_sc, qat_sc, doat_sc, dq_acc, ka_sc, va_sc, dkt_sc, dvt_sc):
        j = pl.program_id(1)
        lane = lax.broadcasted_iota(jnp.int32, (1, LANES), 1)
        low = lane < HEAD_DIM

        @pl.when(j == 0)
        def _():
            dq_acc[...] = jnp.zeros_like(dq_acc)
            place = _bias_placement(0)

            def rows_q(i, carry):
                r0 = pl.multiple_of(i * tb, tb)
                delta = jnp.zeros((tb, LANES), F32)
                for h in range(N_HEADS):
                    pair = slice((h // 2) * LANES, (h // 2 + 1) * LANES)
                    prod = do_ref[pl.ds(r0, tb), pair].astype(F32) * o_ref[pl.ds(r0, tb), pair].astype(F32)
                    head = (lane >= HEAD_DIM * (h % 2)) & (lane < HEAD_DIM * (h % 2 + 1))
                    delta = jnp.where(lane == h, jnp.sum(jnp.where(head, prod, 0.0), axis=1, keepdims=True), delta)
                cq = fc_ref[pl.ds(r0, tb), :] - st_ref[pl.ds(r0, tb), :]
                q_bias = _mm(_bias_lanes(cq), place).astype(BF16)
                do_bias = _mm(_bias_lanes(-delta), place).astype(BF16)
                for h in range(N_HEADS):
                    pair = slice((h // 2) * LANES, (h // 2 + 1) * LANES)
                    qa = _augment(q_ref[pl.ds(r0, tb), pair], h, q_bias, 1)
                    doa = _augment(do_ref[pl.ds(r0, tb), pair], h, do_bias, None)
                    qa_sc[h, pl.ds(r0, tb), :] = qa
                    doa_sc[h, pl.ds(r0, tb), :] = doa
                    qat_sc[h, i] = qa.astype(F32).T.astype(BF16)
                    doat_sc[h, i] = doa.astype(F32).T.astype(BF16)
                return carry

            lax.fori_loop(0, nb, rows_q, 0)

        c0 = pl.multiple_of(j * tb, tb)
        k_bias = _mm(_bias_lanes(-fc_ref[pl.ds(c0, tb), :]), _bias_placement(1)).astype(BF16)
        for h in range(N_HEADS):
            pair = slice((h // 2) * LANES, (h // 2 + 1) * LANES)
            ka_sc[h] = _augment(k_ref[:, pair] * scale, h, k_bias, 0)
            va_sc[h] = _augment(v_ref[:, pair], h, None, 0)
        dkt_sc[...] = jnp.zeros_like(dkt_sc)
        dvt_sc[...] = jnp.zeros_like(dvt_sc)
        causal = lax.broadcasted_iota(jnp.int32, (tb, tb), 1) <= lax.broadcasted_iota(jnp.int32, (tb, tb), 0)

        def step(i, masked):
            r0 = pl.multiple_of(i * tb, tb)
            for h in range(N_HEADS):
                s = _mm_nt(qa_sc[h, pl.ds(r0, tb), :], ka_sc[h])
                if masked:
                    s = jnp.where(causal, s, -jnp.inf)
                pr = jnp.exp(s)
                dvt_sc[h] += _mm(doat_sc[h, i], pr.astype(BF16))
                dsb = (pr * _mm_nt(doa_sc[h, pl.ds(r0, tb), :], va_sc[h])).astype(BF16)
                dkt_sc[h] += _mm(qat_sc[h, i], dsb)
                dq_acc[h, pl.ds(r0, tb), :] += _mm(dsb, ka_sc[h])

        step(j, True)

        def loop_body(i, carry):
            step(i, False)
            return carry

        lax.fori_loop(j + 1, nb, loop_body, 0)
        dfk = jnp.zeros((tb, LANES), F32)
        for p in range(N_PAIRS):
            dk = [dkt_sc[2 * p + hh].T for hh in range(2)]
            dv = [dvt_sc[2 * p + hh].T for hh in range(2)]
            dk_ref[:, p * LANES : (p + 1) * LANES] = (jnp.where(low, dk[0], dk[1]) * scale).astype(BF16)
            dv_ref[:, p * LANES : (p + 1) * LANES] = jnp.where(low, dv[0], dv[1]).astype(BF16)
            for hh in range(2):
                b = HEAD_DIM * (1 - hh) + 3
                dfk = jnp.where(lane == 2 * p + hh, -dk[hh][:, b : b + 1], dfk)
        dfk_ref[...] = dfk

        @pl.when(j == nb - 1)
        def _():
            def rows_dq(i, carry):
                r0 = pl.multiple_of(i * tb, tb)
                dfq = jnp.zeros((tb, LANES), F32)
                for p in range(N_PAIRS):
                    parts = [dq_acc[2 * p + hh, pl.ds(r0, tb), :] for hh in range(2)]
                    dq_ref[pl.ds(r0, tb), p * LANES : (p + 1) * LANES] = jnp.where(low, parts[0], parts[1]).astype(BF16)
                    for hh in range(2):
                        b = HEAD_DIM * (1 - hh)
                        dfq = jnp.where(lane == 2 * p + hh, parts[hh][:, b : b + 1], dfq)
                dfq_ref[pl.ds(r0, tb), :] = dfq
                return carry

            lax.fori_loop(0, nb, rows_dq, 0)

    seq = lambda w, col: pl.BlockSpec((S, w), lambda s, j: (s, col))
    seq_in = lambda w, col: pl.BlockSpec((S, w), lambda s, j: (s, col), pipeline_mode=pl.Buffered(1))
    blk = lambda w, col: pl.BlockSpec((tb, w), lambda s, j: (s * nb + j, col))
    return pl.pallas_call(
        body,
        name="attn_bwd",
        grid=(n_seq, nb),
        in_specs=[seq_in(ATTN_WIDTH, 0), blk(ATTN_WIDTH, 1), blk(ATTN_WIDTH, 2), seq_in(ATTN_WIDTH, 0), seq_in(ATTN_WIDTH, 0), seq_in(LANES, 0), seq_in(LANES, 0)],
        out_specs=[seq(ATTN_WIDTH, 0), blk(ATTN_WIDTH, 0), blk(ATTN_WIDTH, 0), blk(LANES, 0), seq(LANES, 0)],
        out_shape=[
            jax.ShapeDtypeStruct((T, ATTN_WIDTH), BF16),
            jax.ShapeDtypeStruct((T, ATTN_WIDTH), BF16),
            jax.ShapeDtypeStruct((T, ATTN_WIDTH), BF16),
            jax.ShapeDtypeStruct((T, LANES), F32),
            jax.ShapeDtypeStruct((T, LANES), F32),
        ],
        scratch_shapes=[
            pltpu.VMEM((N_HEADS, S, LANES), BF16),
            pltpu.VMEM((N_HEADS, S, LANES), BF16),
            pltpu.VMEM((N_HEADS, nb, LANES, tb), BF16),
            pltpu.VMEM((N_HEADS, nb, LANES, tb), BF16),
            pltpu.VMEM((N_HEADS, S, LANES), F32),
            pltpu.VMEM((N_HEADS, tb, LANES), BF16),
            pltpu.VMEM((N_HEADS, tb, LANES), BF16),
            pltpu.VMEM((N_HEADS, LANES, tb), F32),
            pltpu.VMEM((N_HEADS, LANES, tb), F32),
        ],
        compiler_params=_params(("parallel", "arbitrary"), VMEM_LIMIT_MAX),
    )(qkv, qkv, qkv, da, a, fcol, lse)


def _forget_bwd(dfk, dfq, fl, b_pad, n_seq, S):
    def body(df_ref, dfq_ref, fl_ref, b_ref, dfl_ref, db_ref):
        t = (df_ref[...] + dfq_ref[...]).T
        lane = lax.broadcasted_iota(jnp.int32, t.shape, 1)
        k = 1
        while k < S:
            t = t + jnp.where(lane < S - k, pltpu.roll(t, S - k, 1), 0.0)
            k *= 2
        dfl = t.T * _sigmoid(-(fl_ref[...] + b_ref[...]))
        dfl_ref[...] = dfl.astype(BF16)

        @pl.when(pl.program_id(0) == 0)
        def _():
            db_ref[...] = jnp.zeros_like(db_ref)

        db_ref[...] += jnp.sum(dfl, axis=0, keepdims=True)

    return pl.pallas_call(
        body,
        name="forget_bwd",
        grid=(n_seq,),
        in_specs=[
            pl.BlockSpec((S, LANES), lambda s: (s, 0)),
            pl.BlockSpec((S, LANES), lambda s: (s, 0)),
            pl.BlockSpec((S, FL_PAD), lambda s: (s, 0)),
            _const_spec((1, FL_PAD)),
        ],
        out_specs=[pl.BlockSpec((S, FL_PAD), lambda s: (s, 0)), pl.BlockSpec((1, FL_PAD), lambda s: (0, 0))],
        out_shape=[jax.ShapeDtypeStruct((n_seq * S, FL_PAD), BF16), jax.ShapeDtypeStruct((1, FL_PAD), F32)],
        compiler_params=_params(("arbitrary",)),
    )(dfk, dfq, fl, b_pad)


def _in_proj_bwd(du, dq, dk, dv, dfl, dgates, x, dx1, g1, w_uqkv, w_fl, w_g, token):
    T = x.shape[0]
    tm = ROW_TILE

    def body(du_ref, dq_ref, dk_ref, dv_ref, dfl_ref, dgt_ref, x_ref, dx1_ref, g_ref, wa_ref, wf_ref, wg_ref, token_ref, dx_ref, dg_ref):
        dz = jnp.concatenate([du_ref[...], dq_ref[...], dk_ref[...], dv_ref[...]], axis=1)
        dh = _mm_nt(dz, wa_ref[...]) + _mm_nt(dgt_ref[...], wg_ref[...]) + _mm_nt(dfl_ref[...], wf_ref[...])
        gv = g_ref[...]
        _, xh, r = _rms_fwd(x_ref[...], gv)
        dxn, dgrow = _rms_bwd(dh, xh, r, gv)
        dx_ref[...] = dx1_ref[...] + dxn

        @pl.when(pl.program_id(0) == 0)
        def _():
            dg_ref[...] = jnp.zeros_like(dg_ref)

        dg_ref[...] += jnp.sum(dgrow, axis=0, keepdims=True)

    row = lambda n: pl.BlockSpec((tm, n), lambda i: (i, 0))
    return pl.pallas_call(
        body,
        name="in_proj_bwd",
        grid=(T // tm,),
        in_specs=[
            row(512), row(512), row(512), row(512), row(FL_PAD), row(2 * D_MODEL), row(D_MODEL), row(D_MODEL), _const_spec((1, D_MODEL)),
            _const_spec(w_uqkv.shape), _const_spec(w_fl.shape), _const_spec(w_g.shape), _HBM,
        ],
        out_specs=[row(D_MODEL), pl.BlockSpec((1, D_MODEL), lambda i: (0, 0))],
        out_shape=[jax.ShapeDtypeStruct((T, D_MODEL), F32), jax.ShapeDtypeStruct((1, D_MODEL), F32)],
        compiler_params=_params(("arbitrary",)),
    )(du, dq, dk, dv, dfl, dgates, x, dx1, g1, w_uqkv, w_fl, w_g, token)


def _pick_block(n):
    for b in (1024, 512, 1408, 256, 128):
        if n % b == 0:
            return b
    raise ValueError(n)


def _matmul_tn(a, b, name, col_chunks=False):
    T, K = a.shape
    N = b.shape[1]
    bt, bk, bn = min(T, DW_TOKENS), _pick_block(K), _pick_block(N)
    nt = T // bt
    c = N // N_DEV
    assert not col_chunks or (bn == N and c % LANES == 0)

    def body(a_ref, b_ref, o_ref, acc):
        @pl.when(pl.program_id(2) == 0)
        def _():
            acc[...] = jnp.zeros_like(acc)

        acc[...] += _mm_tn(a_ref[...].astype(BF16), b_ref[...].astype(BF16))

        @pl.when(pl.program_id(2) == nt - 1)
        def _():
            if col_chunks:
                for d in range(N_DEV):
                    o_ref[d] = acc[:, d * c : (d + 1) * c].astype(BF16)
            else:
                o_ref[...] = acc[...].astype(BF16)

    if col_chunks:
        out_spec, out_shape = pl.BlockSpec((N_DEV, bk, c), lambda k, n, t: (0, k, 0)), (N_DEV, K, c)
    else:
        out_spec, out_shape = pl.BlockSpec((bk, bn), lambda k, n, t: (k, n)), (K, N)
    return pl.pallas_call(
        body,
        name=name,
        grid=(K // bk, N // bn, nt),
        in_specs=[pl.BlockSpec((bt, bk), lambda k, n, t: (t, k)), pl.BlockSpec((bt, bn), lambda k, n, t: (t, n))],
        out_specs=out_spec,
        out_shape=jax.ShapeDtypeStruct(out_shape, BF16),
        scratch_shapes=[pltpu.VMEM((bk, bn), F32)],
        compiler_params=_params(("parallel", "parallel", "arbitrary")),
    )(a, b)


W_IN_A = POOL_WIDTH + 3 * ATTN_WIDTH
W_IN_SHARD = (W_IN_A + N_HEADS + 2 * D_MODEL) // N_DEV
_W_IN_PIECES = ((0, W_IN_A), (W_IN_A, W_IN_A + N_HEADS), (W_IN_A + N_HEADS, W_IN_A + N_HEADS + 2 * D_MODEL))


def _w_in_segments(d):
    lo, hi = d * W_IN_SHARD, (d + 1) * W_IN_SHARD
    out = []
    for p, (a, b) in enumerate(_W_IN_PIECES):
        s, e = max(lo, a), min(hi, b)
        if s < e:
            out.append((p, s - a, s - lo, e - s))
    return out


def _w_in_pieces(gathered, tails):
    tm = ROW_TILE // 2
    tail_rows = tm // LANES
    aligned = W_IN_SHARD - 1

    def body(g_ref, t_ref, wa_ref, wf_ref, wg_ref):
        outs = (wa_ref, wf_ref, wg_ref)
        wf_ref[...] = jnp.zeros_like(wf_ref)
        diagonal = lax.broadcasted_iota(jnp.int32, (LANES, LANES), 0) == lax.broadcasted_iota(jnp.int32, (LANES, LANES), 1)
        for d in range(N_DEV):
            for p, at, frm, n in _w_in_segments(d):
                m = min(n, aligned - frm)
                if m > 0:
                    outs[p][:, at : at + m] = g_ref[d, :, frm : frm + m]
                if frm + n == W_IN_SHARD:
                    column = [
                        jnp.sum(jnp.where(diagonal, jnp.broadcast_to(t_ref[d, k : k + 1, :], (LANES, LANES)), 0.0), axis=1, keepdims=True)
                        for k in range(tail_rows)
                    ]
                    outs[p][:, at + n - 1 : at + n] = jnp.concatenate(column, axis=0).astype(outs[p].dtype)

    return pl.pallas_call(
        body,
        name="w_in_pieces",
        grid=(D_MODEL // tm,),
        in_specs=[
            pl.BlockSpec((N_DEV, tm, aligned), lambda i: (0, i, 0)),
            pl.BlockSpec((N_DEV, None, tail_rows, LANES), lambda i: (0, i, 0, 0)),
        ],
        out_specs=[pl.BlockSpec((tm, W_IN_A), lambda i: (i, 0)), pl.BlockSpec((tm, FL_PAD), lambda i: (i, 0)), pl.BlockSpec((tm, 2 * D_MODEL), lambda i: (i, 0))],
        out_shape=[
            jax.ShapeDtypeStruct((D_MODEL, W_IN_A), gathered.dtype),
            jax.ShapeDtypeStruct((D_MODEL, FL_PAD), gathered.dtype),
            jax.ShapeDtypeStruct((D_MODEL, 2 * D_MODEL), gathered.dtype),
        ],
        compiler_params=_params(("parallel",)),
    )(gathered, tails.reshape(N_DEV, D_MODEL // tm, tail_rows, LANES))


def _dw_in(h, du, dq, dk, dv, dfl, dgates, token):
    T = h.shape[0]
    bt, bk = min(T, DW_TOKENS // 2), 512
    nt = T // bt
    pieces = (du, dq, dk, dv, dfl, dgates)
    offs = [0]
    for p in pieces:
        offs.append(offs[-1] + p.shape[1])

    def body(h_ref, *rest):
        refs, o_ref, acc = rest[: len(pieces)], rest[-2], rest[-1]

        @pl.when(pl.program_id(1) == 0)
        def _():
            acc[...] = jnp.zeros_like(acc)

        ht = h_ref[...].T
        for ref, at in zip(refs, offs):
            acc[:, at : at + ref.shape[1]] += _mm(ht, ref[...])

        @pl.when(pl.program_id(1) == nt - 1)
        def _():
            starts = (0, W_IN_A, W_IN_A + FL_PAD)
            for d in range(N_DEV):
                for p, at, to, n in _w_in_segments(d):
                    o_ref[d, :, to : to + n] = acc[:, starts[p] + at : starts[p] + at + n].astype(BF16)

    return pl.pallas_call(
        body,
        name="dw_in",
        grid=(D_MODEL // bk, nt),
        in_specs=[pl.BlockSpec((bt, bk), lambda k, t: (t, k))] + [pl.BlockSpec((bt, p.shape[1]), lambda k, t: (t, 0)) for p in pieces] + [_HBM],
        out_specs=pl.BlockSpec((N_DEV, bk, W_IN_SHARD), lambda k, t: (0, k, 0)),
        out_shape=jax.ShapeDtypeStruct((N_DEV, D_MODEL, W_IN_SHARD), BF16),
        scratch_shapes=[pltpu.VMEM((bk, offs[-1]), F32)],
        compiler_params=_params(("parallel", "arbitrary")),
    )(h, *pieces, token)


def _position():
    return lax.axis_index("x"), lax.axis_index("y"), lax.axis_index("c")


_HBM = pl.BlockSpec(memory_space=pl.ANY)


def _all_gather(blocks, name):
    n = len(blocks)

    def body(*refs):
        xs, outs = refs[:n], refs[n : 2 * n]
        send_sems, recv_sems, local_sems = refs[2 * n :]
        x, y, c = _position()
        me, sibling = (x, y, c), (x, y, 1 - c)
        chips = [(1 - x, y), (x, 1 - y), (1 - x, 1 - y)]

        def rows(a, px, py, pc):
            return outs[a].at[4 * px + 2 * py + pc]

        def copy(a, k, blk, to, src=None):
            return pltpu.make_async_remote_copy(
                src_ref=rows(a, *blk) if src is None else src, dst_ref=rows(a, *blk),
                send_sem=send_sems.at[7 * a + k], recv_sem=recv_sems.at[7 * a + k], device_id=to, device_id_type=MESH,
            )

        first = []
        for a in range(n):
            first += [copy(a, 1 + j, me, (*chip, c), src=xs[a]) for j, chip in enumerate(chips)]
            first.append(copy(a, 0, me, sibling, src=xs[a]))
        mine = [pltpu.make_async_copy(xs[a], rows(a, *me), local_sems.at[a]) for a in range(n)]
        for cp in first + mine:
            cp.start()
        passed = []
        for j, chip in enumerate(chips):
            for a in range(n):
                copy(a, 1 + j, (*chip, c), me).wait_recv()
                passed.append(copy(a, 4 + j, (*chip, c), sibling))
                passed[-1].start()
        for a in range(n):
            copy(a, 0, sibling, me).wait_recv()
        for j, chip in enumerate(chips):
            for a in range(n):
                copy(a, 4 + j, (*chip, 1 - c), me).wait_recv()
        for cp in first + passed:
            cp.wait_send()
        for cp in mine:
            cp.wait()

    return pl.pallas_call(
        body,
        name=name,
        out_shape=[jax.ShapeDtypeStruct((N_DEV, *b.shape), b.dtype) for b in blocks],
        in_specs=[_HBM] * n,
        out_specs=[_HBM] * n,
        scratch_shapes=[pltpu.SemaphoreType.DMA((7 * n,)), pltpu.SemaphoreType.DMA((7 * n,)), pltpu.SemaphoreType.DMA((n,))],
    )(*blocks)


_SEM = pl.BlockSpec(memory_space=pltpu.SEMAPHORE)
_HBM_ONLY = pl.BlockSpec(memory_space=pltpu.HBM)
_SIDE_EFFECT = pltpu.SideEffectType.DATAFLOW_SIDE_EFFECTING


def _peer(x, y, c, k):
    return (1 - x if k & 4 else x, 1 - y if k & 2 else y, 1 - c if k & 1 else c)


_PEER_BITS = {"gather": range(1, N_DEV), "gather_half": (1, 4, 2, 6), "forward": (4, 2, 6), "scatter": range(1, N_DEV)}
_GATHERS = ("gather", "gather_half")


def _exchange_copies(src_refs, land_refs, send_sems, recv_sems, pattern, receive_side):
    x, y, c = _position()
    me = 4 * x + 2 * y + c
    bits = _PEER_BITS[pattern]
    cps = []
    for j, k in enumerate(bits):
        px, py, pc = _peer(x, y, c, k)
        peer = 4 * px + 2 * py + pc
        for a, (src, land) in enumerate(zip(src_refs, land_refs)):
            to = (px, py, pc)
            if pattern == "forward":
                slot = 4 * px + 2 * py + (1 - c if receive_side else c)
                s, to = land.at[slot], (x, y, 1 - c)
            else:
                s, slot = (src if pattern in _GATHERS else src.at[peer]), (peer if receive_side else me)
            cps.append(pltpu.make_async_remote_copy(
                src_ref=s, dst_ref=land.at[slot],
                send_sem=send_sems.at[len(bits) * a + j], recv_sem=recv_sems.at[len(bits) * a + j],
                device_id=to, device_id_type=MESH,
            ))
    return cps


def _own_copies(src_refs, land_refs, own_sems):
    x, y, c = _position()
    return [
        pltpu.make_async_copy(src, land.at[4 * x + 2 * y + c], own_sems.at[a])
        for a, (src, land) in enumerate(zip(src_refs, land_refs))
    ]


def _exchange_start(srcs, after, name, pattern):
    n = len(srcs)
    m = len(_PEER_BITS[pattern])
    lands = [jax.ShapeDtypeStruct((N_DEV, *s.shape[-2:]), s.dtype) for s in srcs]

    def body(*refs):
        src_refs, land_refs = refs[1 : 1 + n], refs[1 + n : 1 + 2 * n]
        send_sems, recv_sems, own_sems = refs[1 + 2 * n : 4 + 2 * n]
        token = refs[-1]
        if pattern in _GATHERS:
            for cp in _own_copies(src_refs, land_refs, own_sems):
                cp.start()
        for cp in _exchange_copies(src_refs, land_refs, send_sems, recv_sems, pattern, receive_side=False):
            cp.start()
        token[...] = jnp.zeros_like(token)

    hbm = lambda t: pltpu.with_memory_space_constraint(t, pltpu.HBM)
    out = pl.pallas_call(
        body,
        name=name,
        out_shape=(
            pltpu.SemaphoreType.DMA((m * n,)), pltpu.SemaphoreType.DMA((m * n,)), pltpu.SemaphoreType.DMA((n,)),
            *[pltpu.HBM(s.shape, s.dtype) for s in srcs], *[pltpu.HBM(l.shape, l.dtype) for l in lands],
            jax.ShapeDtypeStruct((8, LANES), F32),
        ),
        in_specs=(_HBM, *[_HBM_ONLY] * (2 * n)),
        out_specs=(_SEM, _SEM, _SEM, *[_HBM_ONLY] * (2 * n), pl.BlockSpec(memory_space=pltpu.VMEM)),
        input_output_aliases={1 + i: 3 + i for i in range(2 * n)},
        compiler_params=pltpu.CompilerParams(has_side_effects=_SIDE_EFFECT),
    )(after, *[hbm(s) for s in srcs], *[hbm(lax.empty(l.shape, l.dtype)) for l in lands])
    return out[:3], out[3 : 3 + n], out[3 + n : 3 + 2 * n], out[-1]


def _exchange_wait(sems, srcs, lands, after, name, pattern):
    n = len(srcs)

    def body(*refs):
        src_refs, land_refs = refs[:n], refs[n : 2 * n]
        send_sems, recv_sems, own_sems = refs[2 * n : 2 * n + 3]
        if pattern in _GATHERS:
            for cp in _own_copies(src_refs, land_refs, own_sems):
                cp.wait()
        for cp in _exchange_copies(src_refs, land_refs, send_sems, recv_sems, pattern, receive_side=True):
            cp.wait_send()
            cp.wait_recv()

    out = pl.pallas_call(
        body,
        name=name,
        out_shape=(*[pltpu.HBM(s.shape, s.dtype) for s in srcs], *[pltpu.HBM(l.shape, l.dtype) for l in lands]),
        in_specs=(*[_HBM_ONLY] * (2 * n), _SEM, _SEM, _SEM, _HBM),
        out_specs=tuple([_HBM_ONLY] * (2 * n)),
        input_output_aliases={i: i for i in range(2 * n)},
        compiler_params=pltpu.CompilerParams(has_side_effects=_SIDE_EFFECT),
    )(*srcs, *lands, *sems, after)
    return out[:n], out[n:]


def _gather_forward(sems, srcs, lands, after, name):
    n = len(srcs)
    m = len(_PEER_BITS["forward"])

    def body(*refs):
        src_refs, land_refs = refs[:n], refs[n : 2 * n]
        send_sems, recv_sems, own_sems = refs[2 * n : 2 * n + 3]
        forward_send, forward_recv, token = refs[2 * n + 4], refs[2 * n + 5], refs[-1]
        for cp in _own_copies(src_refs, land_refs, own_sems):
            cp.wait()
        for cp in _exchange_copies(src_refs, land_refs, send_sems, recv_sems, "gather_half", receive_side=True):
            cp.wait_send()
            cp.wait_recv()
        for cp in _exchange_copies(land_refs, land_refs, forward_send, forward_recv, "forward", receive_side=False):
            cp.start()
        token[...] = jnp.zeros_like(token)

    out = pl.pallas_call(
        body,
        name=name,
        out_shape=(
            pltpu.SemaphoreType.DMA((m * n,)), pltpu.SemaphoreType.DMA((m * n,)),
            *[pltpu.HBM(l.shape, l.dtype) for l in lands], jax.ShapeDtypeStruct((8, LANES), F32),
        ),
        in_specs=(*[_HBM_ONLY] * (2 * n), _SEM, _SEM, _SEM, _HBM),
        out_specs=(_SEM, _SEM, *[_HBM_ONLY] * n, pl.BlockSpec(memory_space=pltpu.VMEM)),
        input_output_aliases={n + i: 2 + i for i in range(n)},
        compiler_params=pltpu.CompilerParams(has_side_effects=_SIDE_EFFECT),
    )(*srcs, *lands, *sems, after)
    return out[:2], out[2 : 2 + n], out[-1]


def _forward_wait(sems, lands, after, name):
    n = len(lands)

    def body(*refs):
        land_refs = refs[:n]
        for cp in _exchange_copies(land_refs, land_refs, refs[n], refs[n + 1], "forward", receive_side=True):
            cp.wait_send()
            cp.wait_recv()

    return pl.pallas_call(
        body,
        name=name,
        out_shape=tuple(pltpu.HBM(l.shape, l.dtype) for l in lands),
        in_specs=(*[_HBM_ONLY] * n, _SEM, _SEM, _HBM),
        out_specs=tuple([_HBM_ONLY] * n),
        input_output_aliases={i: i for i in range(n)},
        compiler_params=pltpu.CompilerParams(has_side_effects=_SIDE_EFFECT),
    )(*lands, *sems, after)


def _rows_tile(r):
    return ROW_TILE if r % ROW_TILE == 0 else r


def _adamw(w, g, m, v):
    m = ADAM_B1 * m + (1.0 - ADAM_B1) * g
    v = ADAM_B2 * v + (1.0 - ADAM_B2) * (g * g)
    m_hat = m / (1.0 - ADAM_B1 ** ADAM_STEP)
    v_hat = v / (1.0 - ADAM_B2 ** ADAM_STEP)
    delta = -ADAM_LR * (m_hat / (jnp.sqrt(v_hat) + ADAM_EPS) + ADAM_WD * w)
    return delta, m, v


def _shard_update_direct(parts, chunks, w, m, v, me, name):
    _, r, c = w.shape
    br = _rows_tile(r)

    def body(me_ref, p_ref, own_ref, w_ref, m_ref, v_ref, g_ref, d_ref, nm_ref, nv_ref):
        g = None
        for n in range(N_DEV):
            part = jnp.where(me_ref[0] == n, own_ref[...], p_ref[n]).astype(F32)
            g = part if g is None else g + part
        g_ref[...] = g
        d_ref[...], nm_ref[...], nv_ref[...] = _adamw(w_ref[...], g, m_ref[...], v_ref[...])

    shard = pl.BlockSpec((None, br, c), lambda i, me: (0, i, 0))
    return pl.pallas_call(
        body,
        name=name,
        grid_spec=pltpu.PrefetchScalarGridSpec(
            num_scalar_prefetch=1,
            grid=(r // br,),
            in_specs=[
                pl.BlockSpec((N_DEV, br, c), lambda i, me: (0, i, 0)),
                pl.BlockSpec((None, br, c), lambda i, me: (me[0], i, 0)),
                shard, shard, shard,
            ],
            out_specs=[shard, shard, shard, shard],
        ),
        out_shape=[jax.ShapeDtypeStruct((1, r, c), F32)] * 4,
        compiler_params=_params(("parallel",)),
    )(me, parts, chunks, w, m, v)


def _small_update(parts, first_rows, ws, ms, vs):
    k = len(ws)

    def unpacked(rows, shape):
        if len(shape) == 2 and shape[1] <= LANES:
            return rows[0:1, : shape[1]]
        if len(shape) == 2:
            return jnp.concatenate([rows[r : r + 1] for r in range(shape[1] // LANES)], axis=1)
        return rows.reshape(shape)

    def body(p_ref, f_ref, *refs):
        w_refs, m_refs, v_refs = refs[:k], refs[k : 2 * k], refs[2 * k : 3 * k]
        outs, loss_ref = refs[3 * k : 7 * k], refs[7 * k]
        g, first = p_ref[0], f_ref[0]
        for n in range(1, N_DEV):
            g = g + p_ref[n]
            first = first + f_ref[n]
        g = jnp.concatenate([g[:8] + first, g[8:]], axis=0)
        off = 0
        for i, (_, rows) in enumerate(_SMALL):
            gi = unpacked(g[off : off + rows], w_refs[i].shape)
            off += rows
            outs[i][...] = gi
            outs[k + i][...], outs[2 * k + i][...], outs[3 * k + i][...] = _adamw(w_refs[i][...], gi, m_refs[i][...], v_refs[i][...])
        loss_ref[...] = g[off : off + 1, 0:1]

    out = pl.pallas_call(
        body,
        name="small_update",
        out_shape=[jax.ShapeDtypeStruct(w.shape, F32) for _ in range(4) for w in ws] + [jax.ShapeDtypeStruct((1, 1), F32)],
        compiler_params=pltpu.CompilerParams(vmem_limit_bytes=VMEM_LIMIT),
    )(parts, first_rows, *ws, *ms, *vs)
    return [out[a * k : (a + 1) * k] for a in range(4)], out[4 * k]


_SHARD_AXIS = (1, 1, 1, 0, 0, 0, 0)
_TRANSPOSED = (False, False, False, False, True, True, False)


def _full_from_gathered(t, axis):
    if axis == 0:
        return t.reshape(N_DEV * t.shape[1], t.shape[2])
    return t


_SMALL = (("norm1_g", 8), ("norm2_g", 8), ("norm_f_g", 8), ("b_forget", 8), ("pool_scale", 8), ("pool_mix", 512))


def _pack_small(vals, loss_row):
    parts = []
    for (name, rows), t in zip(_SMALL, vals):
        f = t.astype(F32).reshape(-1)
        f = jnp.concatenate([f, jnp.zeros((rows * LANES - f.shape[0],), F32)]).reshape(rows, LANES)
        parts.append(f)
    parts.append(loss_row)
    return jnp.concatenate(parts, axis=0)


def _local_grads(x, tgt, g1, g2, gf, b_forget, pool_mix, pool_scale, w_in, fwd_token, out_weights, ffn_weights, ffn_grads_out, out_grads_out, small_grads_out, in_grads_out, norm1_grad_out):
    n_seq, S, _ = x.shape
    T = n_seq * S
    x2 = x.reshape(T, D_MODEL)
    tg2 = tgt.reshape(T, D_MODEL)
    w_uqkv, w_fl, w_g = w_in
    b_pad = jnp.concatenate([b_forget.reshape(1, N_HEADS), jnp.zeros((1, FL_PAD - N_HEADS), F32)], axis=1)
    mix_b = pool_mix.reshape(len(POOL_WINDOWS), GROUP_DIM, GROUP_DIM).astype(BF16)
    scale = pool_scale.reshape(1, POOL_WIDTH)
    g1 = g1.reshape(1, D_MODEL)
    g2 = g2.reshape(1, D_MODEL)
    gf = gf.reshape(1, D_MODEL)

    h, u, qkv, fl, gates = _in_proj(x2, g1, w_uqkv, w_fl, w_g, fwd_token)
    fcol = _forget_fwd(fl, b_pad, n_seq, S)
    pm, p2, p3 = _pool_fwd(u, mix_b, scale, n_seq, S)
    a, lse = _attn_fwd(qkv, fcol, n_seq, S)
    w_po, w_ao, w_out = out_weights(a)
    merged, x1, attn_y, pool_y = _mix_out(a, p3, gates, x2, w_ao, w_po, w_out)
    w_gate_t, w_up_t, w_down = ffn_weights(x1)
    h2, gate, up, act, dx2, loss_rows, dgf = _ffn_fwd(x1, g2, gf, tg2, w_gate_t, w_up_t, w_down)

    dgate, dup, dx1, dg2 = _ffn_bwd(dx2, gate, up, x1, g2, w_gate_t, w_up_t, w_down)
    bwd_token = ffn_grads_out(_matmul_tn(dgate, h2, "dw_ffn_gate"), _matmul_tn(dup, h2, "dw_ffn_up"), _matmul_tn(act, dx2, "dw_ffn_down"))
    dgates, dpy, day, da, dp2, dscale = _mix_bwd(dx1, gates, pool_y, attn_y, p2, scale, w_out, w_ao, w_po, bwd_token)
    out_token = out_grads_out(
        _matmul_tn(p3, dpy, "dw_pool_out", col_chunks=True), _matmul_tn(a, day, "dw_attn_out", col_chunks=True), _matmul_tn(merged, dx1, "dw_out")
    )
    du, dmix = _pool_bwd(dp2, pm, mix_b, out_token, n_seq, S)
    dq, dk, dv, dfk, dfq = _attn_bwd(qkv, da, a, fcol, lse, n_seq, S)
    dfl, db = _forget_bwd(dfk, dfq, fl, b_pad, n_seq, S)
    small_token = small_grads_out((jnp.zeros_like(g1), dg2, dgf, db[:, :N_HEADS], dscale, dmix), loss_rows)
    in_token = in_grads_out(_dw_in(h, du, dq, dk, dv, dfl, dgates, small_token))
    dx, dg1 = _in_proj_bwd(du, dq, dk, dv, dfl, dgates, x2, dx1, g1, w_uqkv, w_fl, w_g, in_token)
    norm1_grad_out(dg1)
    return dx.reshape(n_seq, S, D_MODEL)


def kernel(x, norm1_g, w_in, b_forget, pool_mix, pool_scale, w_pool_out, w_attn_out, w_out, norm2_g, w_ffn_gate, w_ffn_up, w_ffn_down, norm_f_g, loss_target, m_norm1_g, m_w_in, m_b_forget, m_pool_mix, m_pool_scale, m_w_pool_out, m_w_attn_out, m_w_out, m_norm2_g, m_w_ffn_gate, m_w_ffn_up, m_w_ffn_down, m_norm_f_g, v_norm1_g, v_w_in, v_b_forget, v_pool_mix, v_pool_scale, v_w_pool_out, v_w_attn_out, v_w_out, v_norm2_g, v_w_ffn_gate, v_w_ffn_up, v_w_ffn_down, v_norm_f_g):
    names = ("w_in", "w_pool_out", "w_attn_out", "w_out", "w_ffn_gate", "w_ffn_up", "w_ffn_down")
    w_sh = (w_in, w_pool_out, w_attn_out, w_out, w_ffn_gate, w_ffn_up, w_ffn_down)
    m_sh = (m_w_in, m_w_pool_out, m_w_attn_out, m_w_out, m_w_ffn_gate, m_w_ffn_up, m_w_ffn_down)
    v_sh = (v_w_in, v_w_pool_out, v_w_attn_out, v_w_out, v_w_ffn_gate, v_w_ffn_up, v_w_ffn_down)

    cx, cy, cc = _position()
    me = 4 * cx + 2 * cy + cc
    def stored(t, transposed):
        return jnp.transpose(t, (0, 2, 1)) if transposed else t

    w_sh, m_sh, v_sh = ([stored(t, tr) for t, tr in zip(ts, _TRANSPOSED)] for ts in (w_sh, m_sh, v_sh))
    shards = [w[0].astype(BF16) for w in w_sh]
    last_in = shards[0][:, W_IN_SHARD - 1].astype(F32).reshape(D_MODEL // LANES, LANES)
    gathered_in, tails_in = _all_gather([shards[0][:, : W_IN_SHARD - 1], last_in], "w_in_all_gather")
    out_sems = _exchange_start(shards[1:4], gathered_in, "out_weights_gather_start", "gather")
    ffn_sems = _exchange_start(shards[4:], out_sems[3], "ffn_weights_gather_start", "gather_half")
    no_order = jnp.zeros((8, LANES), F32)
    started = {}

    def out_weights(after):
        forward_sems, lands, token = _gather_forward(*ffn_sems[:3], after, "ffn_weights_forward_start")
        started["forward"] = (forward_sems, lands)
        _, lands = _exchange_wait(*out_sems[:3], token, "out_weights_gather_wait", "gather")
        return [_full_from_gathered(t, axis) for t, axis in zip(lands, _SHARD_AXIS[out])]

    def ffn_weights(after):
        lands = _forward_wait(*started["forward"], after, "ffn_weights_gather_wait")
        return [_full_from_gathered(t, axis) for t, axis in zip(lands, _SHARD_AXIS[ffn])]

    def hold_ffn_grads(*whole_grads):
        started["held"] = whole_grads
        return no_order

    def scatter_grads(*out_grads):
        chunks = [
            t if axis == 1 else t.reshape(N_DEV, -1, t.shape[1])
            for t, axis in zip((*out_grads, *started["held"]), _SHARD_AXIS[scattered])
        ]
        started["scatter"] = _exchange_start(chunks, no_order, "grads_scatter_start", "scatter")
        return started["scatter"][3]

    def gather_small(small, loss_rows):
        started["small"] = _exchange_start([_pack_small(small, loss_rows)], no_order, "small_grads_gather_start", "gather")
        return started["small"][3]

    def scatter_w_in(chunks_in):
        started["in"] = _exchange_start([chunks_in], no_order, "w_in_grads_scatter_start", "scatter")
        return started["in"][3]

    def gather_norm1(dg1):
        rows = jnp.reshape(dg1, (8, LANES))
        started["norm1"] = _exchange_start([rows], no_order, "norm1_grad_gather_start", "gather")

    ffn, out, scattered = slice(4, 7), slice(1, 4), slice(1, 7)
    grad_x = _local_grads(
        x, loss_target, norm1_g, norm2_g, norm_f_g, b_forget, pool_mix, pool_scale, _w_in_pieces(gathered_in, tails_in), ffn_sems[3],
        out_weights, ffn_weights, hold_ffn_grads, scatter_grads, gather_small, scatter_w_in, gather_norm1,
    )
    me_index = jnp.reshape(me, (1,)).astype(jnp.int32)

    srcs, lands = _exchange_wait(*started["scatter"][:3], started["norm1"][3], "grads_scatter_wait", "scatter")
    updates = [
        _shard_update_direct(p, s, w, m, v, me_index, "update_" + n)
        for p, s, w, m, v, n in zip(lands, srcs, w_sh[scattered], m_sh[scattered], v_sh[scattered], names[scattered])
    ]
    updates_out, updates_ffn = updates[:3], updates[3:]

    small_w = (norm1_g, norm2_g, norm_f_g, b_forget, pool_scale, pool_mix)
    small_m = (m_norm1_g, m_norm2_g, m_norm_f_g, m_b_forget, m_pool_scale, m_pool_mix)
    small_v = (v_norm1_g, v_norm2_g, v_norm_f_g, v_b_forget, v_pool_scale, v_pool_mix)
    (sent_in,), (parts_in,) = _exchange_wait(*started["in"][:3], updates_ffn[-1][0], "w_in_grads_scatter_wait", "scatter")
    update_in = _shard_update_direct(parts_in, sent_in, w_in, m_w_in, v_w_in, me_index, "update_w_in")

    def gathered_small(key, after, name):
        _, lands = _exchange_wait(*started[key][:3], after, name, "gather")
        return lands[0]

    parts = gathered_small("small", update_in[0], "small_grads_gather_wait")
    first_rows = gathered_small("norm1", parts, "norm1_grad_gather_wait")
    (g_s, d_s, nm_s, nv_s), loss = _small_update(parts, first_rows, small_w, small_m, small_v)
    g_w, d_w, nm_w, nv_w = zip(*(
        [stored(t, tr) for t in u] for u, tr in zip([update_in] + updates_out + updates_ffn, _TRANSPOSED)
    ))
    loss = loss.reshape(())
    (g1, g2, gf, gb, gsc, gmix), (d1, d2, df, db_, dsc, dmx) = g_s, d_s
    (m1, m2, mf, mb, msc, mmx), (v1, v2, vf, vb, vsc, vmx) = nm_s, nv_s

    def ordered(n1, win, b, mix, sc, wpo, wao, wout, n2, wg, wu, wd, nf):
        return (n1, win, b, mix, sc, wpo, wao, wout, n2, wg, wu, wd, nf)

    grads = ordered(g1, g_w[0], gb, gmix, gsc, g_w[1], g_w[2], g_w[3], g2, g_w[4], g_w[5], g_w[6], gf)
    deltas = ordered(d1, d_w[0], db_, dmx, dsc, d_w[1], d_w[2], d_w[3], d2, d_w[4], d_w[5], d_w[6], df)
    new_m = ordered(m1, nm_w[0], mb, mmx, msc, nm_w[1], nm_w[2], nm_w[3], m2, nm_w[4], nm_w[5], nm_w[6], mf)
    new_v = ordered(v1, nv_w[0], vb, vmx, vsc, nv_w[1], nv_w[2], nv_w[3], v2, nv_w[4], nv_w[5], nv_w[6], vf)
    return (loss, grad_x, *grads, *deltas, *new_m, *new_v)
```

```python
import jax
import jax.numpy as jnp
from jax import lax
from jax.experimental import pallas as pl
from jax.experimental.pallas import tpu as pltpu

F32 = jnp.float32
BF16 = jnp.bfloat16
MESH = pl.DeviceIdType.MESH

D_MODEL = 1024
POOL_WINDOWS = (2, 4, 8, 16)
POOL_WIDTH = 512
GROUP_DIM = 128
ATTN_WIDTH = 512
HEAD_DIM = 64
N_HEADS = 8
N_PAIRS = 4
D_FF = 2816
RMS_EPS = 1e-6
N_DEV = 8
LANES = 128
FL_PAD = 128

ADAM_LR = 0.001
ADAM_B1 = 0.9
ADAM_B2 = 0.999
ADAM_EPS = 1e-08
ADAM_WD = 0.01
ADAM_STEP = 10

VMEM_LIMIT = 56 * 1024 * 1024
VMEM_LIMIT_MAX = 60 * 1024 * 1024
ROW_TILE = 512
ATTN_BLOCK = 512
FF_CHUNK = 256
FF_ROW_TILE = 512
DW_TOKENS = 2048


def _mm(a, b):
    return jnp.dot(a, b, preferred_element_type=F32)


def _mm_nt(a, b):
    return lax.dot_general(a, b, (((1,), (1,)), ((), ())), preferred_element_type=F32)


def _mm_tn(a, b):
    return lax.dot_general(a, b, (((0,), (0,)), ((), ())), preferred_element_type=F32)


def _whole_cols(w_ref):
    if len(w_ref.shape) == 2:
        return w_ref[...]
    return jnp.concatenate([w_ref[d] for d in range(w_ref.shape[0])], axis=1)


def _sigmoid(x):
    return 1.0 / (1.0 + jnp.exp(-x))


def _params(sem, vmem=VMEM_LIMIT):
    return pltpu.CompilerParams(dimension_semantics=sem, vmem_limit_bytes=vmem)


def _const_spec(shape):
    nd = len(shape)
    return pl.BlockSpec(shape, lambda *_: (0,) * nd, pipeline_mode=pl.Buffered(1))


def _rms_fwd(x, g):
    r = lax.rsqrt(jnp.mean(x * x, axis=-1, keepdims=True) + RMS_EPS)
    xh = x * r
    return xh * g, xh, r


def _rms_bwd(dy, xh, r, g):
    dxh = dy * g
    dx = r * (dxh - xh * jnp.mean(dxh * xh, axis=-1, keepdims=True))
    return dx, dy * xh


def _in_proj(x, g1, w_uqkv, w_fl, w_g, token):
    T = x.shape[0]
    tm = ROW_TILE

    def body(x_ref, g_ref, wa_ref, wf_ref, wg_ref, token_ref, h_ref, u_ref, qkv_ref, fl_ref, gt_ref):
        h, _, _ = _rms_fwd(x_ref[...], g_ref[...])
        hb = h.astype(BF16)
        h_ref[...] = hb
        z = _mm(hb, wa_ref[...])
        u_ref[...] = z[:, :POOL_WIDTH]
        qkv_ref[...] = z[:, POOL_WIDTH:].astype(BF16)
        fl_ref[...] = _mm(hb, wf_ref[...])
        gt_ref[...] = _mm(hb, wg_ref[...]).astype(BF16)

    row = lambda n: pl.BlockSpec((tm, n), lambda i: (i, 0))
    return pl.pallas_call(
        body,
        name="in_proj",
        grid=(T // tm,),
        in_specs=[row(D_MODEL), _const_spec((1, D_MODEL)), _const_spec(w_uqkv.shape), _const_spec(w_fl.shape), _const_spec(w_g.shape), _HBM],
        out_specs=[row(D_MODEL), row(POOL_WIDTH), row(3 * ATTN_WIDTH), row(FL_PAD), row(2 * D_MODEL)],
        out_shape=[
            jax.ShapeDtypeStruct((T, D_MODEL), BF16),
            jax.ShapeDtypeStruct((T, POOL_WIDTH), F32),
            jax.ShapeDtypeStruct((T, 3 * ATTN_WIDTH), BF16),
            jax.ShapeDtypeStruct((T, FL_PAD), F32),
            jax.ShapeDtypeStruct((T, 2 * D_MODEL), BF16),
        ],
        compiler_params=_params(("parallel",)),
    )(x, g1, w_uqkv, w_fl, w_g, token)


def _log_sigmoid(x):
    return jnp.minimum(x, 0.0) - jnp.log(1.0 + jnp.exp(-jnp.abs(x)))


def _forget_fwd(fl, b_pad, n_seq, S):
    def body(fl_ref, b_ref, fcol_ref):
        lf = _log_sigmoid(fl_ref[...] + b_ref[...])
        t = lf.T
        lane = lax.broadcasted_iota(jnp.int32, t.shape, 1)
        k = 1
        while k < S:
            t = t + jnp.where(lane >= k, pltpu.roll(t, k, 1), 0.0)
            k *= 2
        fcol_ref[...] = t.T

    return pl.pallas_call(
        body,
        name="forget_fwd",
        grid=(n_seq,),
        in_specs=[pl.BlockSpec((S, FL_PAD), lambda s: (s, 0)), _const_spec((1, FL_PAD))],
        out_specs=pl.BlockSpec((S, FL_PAD), lambda s: (s, 0)),
        out_shape=jax.ShapeDtypeStruct((n_seq * S, FL_PAD), F32),
        compiler_params=_params(("parallel",)),
    )(fl, b_pad)


def _window_pick(g, v2, v4, v8, v16):
    return jnp.where(g == 0, v2, jnp.where(g == 1, v4, jnp.where(g == 2, v8, v16)))


def _pool_fwd(u, mix_b, scale, n_seq, S):
    T = n_seq * S

    def body(u_ref, mix_ref, sc_ref, pm_ref, p2_ref, p3_ref):
        g = pl.program_id(1)
        uu = u_ref[...]
        row = lax.broadcasted_iota(jnp.int32, uu.shape, 0)

        def back(a, k):
            return jnp.where(row >= k, pltpu.roll(a, k, 0), 0.0)

        s2 = uu + back(uu, 1)
        s4 = s2 + back(s2, 2)
        s8 = s4 + back(s4, 4)
        s16 = s8 + back(s8, 8)
        w = _window_pick(g, 2.0, 4.0, 8.0, 16.0)
        cnt = jnp.minimum((row + 1).astype(F32), w)
        pm = _window_pick(g, s2, s4, s8, s16) / cnt - uu
        pmb = pm.astype(BF16)
        pm_ref[...] = pmb
        p2 = _mm(pmb, mix_ref[...])
        p2_ref[...] = p2
        p3_ref[...] = (p2 * sc_ref[...]).astype(BF16)

    grp = pl.BlockSpec((S, GROUP_DIM), lambda s, g: (s, g))
    return pl.pallas_call(
        body,
        name="pool_fwd",
        grid=(n_seq, len(POOL_WINDOWS)),
        in_specs=[
            grp,
            pl.BlockSpec((None, GROUP_DIM, GROUP_DIM), lambda s, g: (g, 0, 0)),
            pl.BlockSpec((1, GROUP_DIM), lambda s, g: (0, g)),
        ],
        out_specs=[grp, grp, grp],
        out_shape=[
            jax.ShapeDtypeStruct((T, POOL_WIDTH), BF16),
            jax.ShapeDtypeStruct((T, POOL_WIDTH), F32),
            jax.ShapeDtypeStruct((T, POOL_WIDTH), BF16),
        ],
        compiler_params=_params(("parallel", "parallel")),
    )(u, mix_b, scale)


def _split3(v):
    hi = v.astype(BF16).astype(F32)
    r = v - hi
    mid = r.astype(BF16).astype(F32)
    lo = (r - mid).astype(BF16).astype(F32)
    return hi, mid, lo


def _bias_lanes(v):
    hi, mid, lo = _split3(v)
    lane = lax.broadcasted_iota(jnp.int32, (1, LANES), 1)
    packed = jnp.where(lane < N_HEADS, hi, jnp.where(lane < 2 * N_HEADS, pltpu.roll(mid, N_HEADS, 1), pltpu.roll(lo, 2 * N_HEADS, 1)))
    return jnp.where(lane < 3 * N_HEADS, packed, 0.0).astype(BF16)


def _bias_placement(slot):
    row = lax.broadcasted_iota(jnp.int32, (LANES, N_HEADS * LANES), 0)
    col = lax.broadcasted_iota(jnp.int32, (LANES, N_HEADS * LANES), 1)
    h = col // LANES
    n = col % LANES - jnp.where(h % 2 == 0, HEAD_DIM, 0) - 3 * slot
    return ((n >= 0) & (n < 3) & (row == N_HEADS * n + h)).astype(BF16)


def _augment(xp, h, bias, ones_slot):
    lane = lax.broadcasted_iota(jnp.int32, (1, LANES), 1)
    hh = h % 2
    head = (lane >= HEAD_DIM * hh) & (lane < HEAD_DIM * (hh + 1))
    b = HEAD_DIM * (1 - hh)
    rest = jnp.zeros_like(xp) if bias is None else bias[:, h * LANES : (h + 1) * LANES]
    out = jnp.where(head, xp, rest)
    if ones_slot is not None:
        out = jnp.where((lane >= b + 3 * ones_slot) & (lane < b + 3 * ones_slot + 3), jnp.ones_like(xp), out)
    return out


def _attn_fwd(qkv, fcol, n_seq, S):
    T = n_seq * S
    tb = ATTN_BLOCK
    nq = S // tb
    scale = HEAD_DIM ** -0.5

    def body(q_ref, k_ref, v_ref, fc_ref, o_ref, st_ref, qa_sc, ka_sc, m_sc, l_sc, acc_sc):
        i = pl.program_id(1)
        lane = lax.broadcasted_iota(jnp.int32, (1, LANES), 1)
        low = lane < HEAD_DIM

        @pl.when(i == 0)
        def _():
            place = _bias_placement(1)

            def rows_ka(r, carry):
                r0 = pl.multiple_of(r * tb, tb)
                bias = _mm(_bias_lanes(-fc_ref[pl.ds(r0, tb), :]), place).astype(BF16)
                for h in range(N_HEADS):
                    kp = k_ref[pl.ds(r0, tb), (h // 2) * LANES : (h // 2 + 1) * LANES] * scale
                    ka_sc[h, pl.ds(r0, tb), :] = _augment(kp, h, bias, 0)
                return carry

            lax.fori_loop(0, nq, rows_ka, 0)

        q0 = pl.multiple_of(i * tb, tb)
        bias = _mm(_bias_lanes(fc_ref[pl.ds(q0, tb), :]), _bias_placement(0)).astype(BF16)
        for h in range(N_HEADS):
            qa_sc[h] = _augment(q_ref[:, (h // 2) * LANES : (h // 2 + 1) * LANES], h, bias, 1)
        m_sc[...] = jnp.full(m_sc.shape, -jnp.inf, F32)
        l_sc[...] = jnp.zeros_like(l_sc)
        acc_sc[...] = jnp.zeros_like(acc_sc)
        causal = lax.broadcasted_iota(jnp.int32, (tb, tb), 1) <= lax.broadcasted_iota(jnp.int32, (tb, tb), 0)

        def step(j, masked):
            c0 = pl.multiple_of(j * tb, tb)
            for p in range(N_PAIRS):
                vb = v_ref[pl.ds(c0, tb), p * LANES : (p + 1) * LANES]
                pv, al = [], []
                for hh in range(2):
                    h = 2 * p + hh
                    s = _mm_nt(qa_sc[h], ka_sc[h, pl.ds(c0, tb), :])
                    if masked:
                        s = jnp.where(causal, s, -jnp.inf)
                    m_old = m_sc[h]
                    m_new = jnp.maximum(m_old, jnp.max(s, axis=1, keepdims=True))
                    alpha = jnp.exp(m_old - m_new)
                    pe = jnp.exp(s - jnp.concatenate([m_new] * (tb // LANES), axis=1))
                    l_sc[h] = alpha * l_sc[h] + jnp.sum(pe, axis=1, keepdims=True)
                    m_sc[h] = m_new
                    pv.append(_mm(pe.astype(BF16), vb))
                    al.append(alpha)
                acc_sc[p] = jnp.where(low, al[0], al[1]) * acc_sc[p] + jnp.where(low, pv[0], pv[1])

        def loop_body(j, carry):
            step(j, False)
            return carry

        lax.fori_loop(0, i, loop_body, 0)
        step(i, True)
        st = jnp.zeros((tb, LANES), F32)
        for p in range(N_PAIRS):
            lp = jnp.where(low, l_sc[2 * p], l_sc[2 * p + 1])
            o_ref[:, p * LANES : (p + 1) * LANES] = (acc_sc[p] / lp).astype(BF16)
            for h in (2 * p, 2 * p + 1):
                st = jnp.where(lane == h, m_sc[h] + jnp.log(l_sc[h]), st)
        st_ref[...] = st

    return pl.pallas_call(
        body,
        name="attn_fwd",
        grid=(n_seq, nq),
        in_specs=[
            pl.BlockSpec((tb, ATTN_WIDTH), lambda s, i: (s * nq + i, 0)),
            pl.BlockSpec((S, ATTN_WIDTH), lambda s, i: (s, 1)),
            pl.BlockSpec((S, ATTN_WIDTH), lambda s, i: (s, 2)),
            pl.BlockSpec((S, LANES), lambda s, i: (s, 0)),
        ],
        out_specs=[
            pl.BlockSpec((tb, ATTN_WIDTH), lambda s, i: (s * nq + i, 0)),
            pl.BlockSpec((tb, LANES), lambda s, i: (s * nq + i, 0)),
        ],
        out_shape=[jax.ShapeDtypeStruct((T, ATTN_WIDTH), BF16), jax.ShapeDtypeStruct((T, LANES), F32)],
        scratch_shapes=[
            pltpu.VMEM((N_HEADS, tb, LANES), BF16),
            pltpu.VMEM((N_HEADS, S, LANES), BF16),
            pltpu.VMEM((N_HEADS, tb, LANES), F32),
            pltpu.VMEM((N_HEADS, tb, LANES), F32),
            pltpu.VMEM((N_PAIRS, tb, LANES), F32),
        ],
        compiler_params=_params(("parallel", "arbitrary")),
    )(qkv, qkv, qkv, fcol)


def _mix_out(a, p3, gates, x, w_ao, w_po, w_out):
    T = x.shape[0]
    tm = ROW_TILE

    def body(a_ref, p3_ref, gt_ref, x_ref, wao_ref, wpo_ref, wout_ref, mg_ref, x1_ref, ay_ref, py_ref):
        ay = _mm(a_ref[...], _whole_cols(wao_ref))
        py = _mm(p3_ref[...], _whole_cols(wpo_ref))
        ay_ref[...] = ay.astype(BF16)
        py_ref[...] = py.astype(BF16)
        sp = _sigmoid(gt_ref[:, :D_MODEL].astype(F32))
        sa = _sigmoid(gt_ref[:, D_MODEL:].astype(F32))
        mb = (sp * py + sa * ay).astype(BF16)
        mg_ref[...] = mb
        x1_ref[...] = x_ref[...] + _mm(mb, wout_ref[...])

    row = lambda n: pl.BlockSpec((tm, n), lambda i: (i, 0))
    return pl.pallas_call(
        body,
        name="mix_out",
        grid=(T // tm,),
        in_specs=[
            row(ATTN_WIDTH), row(POOL_WIDTH), row(2 * D_MODEL), row(D_MODEL),
            _const_spec(w_ao.shape), _const_spec(w_po.shape), _const_spec(w_out.shape),
        ],
        out_specs=[row(D_MODEL), row(D_MODEL), row(D_MODEL), row(D_MODEL)],
        out_shape=[
            jax.ShapeDtypeStruct((T, D_MODEL), BF16), jax.ShapeDtypeStruct((T, D_MODEL), F32),
            jax.ShapeDtypeStruct((T, D_MODEL), BF16), jax.ShapeDtypeStruct((T, D_MODEL), BF16),
        ],
        compiler_params=_params(("parallel",)),
    )(a, p3, gates, x, w_ao, w_po, w_out)


def _ffn_fwd(x1, g2, gf, tgt, w_gate_t, w_up_t, w_down):
    T = x1.shape[0]
    tm = min(T, FF_ROW_TILE)
    nt = T // tm
    nc = D_FF // FF_CHUNK

    def body(x1_ref, g2_ref, gf_ref, tg_ref, wg_ref, wu_ref, wd_ref, h2_ref, gate_ref, up_ref, act_ref, dx2_ref, loss_ref, dgf_ref):
        x1v = x1_ref[...]
        h2, _, _ = _rms_fwd(x1v, g2_ref[...])
        h2b = h2.astype(BF16)
        h2_ref[...] = h2b
        for c in range(nc):
            sl = slice(c * FF_CHUNK, (c + 1) * FF_CHUNK)
            gate = _mm_nt(h2b, wg_ref[sl, :])
            up = _mm_nt(h2b, wu_ref[sl, :])
            gate_ref[:, sl] = gate.astype(BF16)
            up_ref[:, sl] = up.astype(BF16)
            act_ref[:, sl] = (gate * _sigmoid(gate) * up).astype(BF16)
        acc = x1v + _mm(act_ref[...], wd_ref[...])
        gfv = gf_ref[...]
        y, xh, r = _rms_fwd(acc, gfv)
        err = y - tg_ref[...]
        part = 0.5 * jnp.sum(jnp.mean(err * err, axis=-1, keepdims=True), axis=0, keepdims=True)
        dx2, dgrow = _rms_bwd(err * (1.0 / D_MODEL), xh, r, gfv)
        dx2_ref[...] = dx2

        @pl.when(pl.program_id(0) == 0)
        def _():
            dgf_ref[...] = jnp.zeros_like(dgf_ref)
            loss_ref[...] = jnp.zeros_like(loss_ref)

        dgf_ref[...] += jnp.sum(dgrow, axis=0, keepdims=True)
        loss_ref[...] += jnp.broadcast_to(part, loss_ref.shape)

    row = lambda n: pl.BlockSpec((tm, n), lambda i: (i, 0))
    return pl.pallas_call(
        body,
        name="ffn_fwd",
        grid=(nt,),
        in_specs=[
            row(D_MODEL), _const_spec((1, D_MODEL)), _const_spec((1, D_MODEL)), row(D_MODEL),
            _const_spec(w_gate_t.shape), _const_spec(w_up_t.shape), _const_spec(w_down.shape),
        ],
        out_specs=[
            row(D_MODEL), row(D_FF), row(D_FF), row(D_FF), row(D_MODEL),
            pl.BlockSpec((8, LANES), lambda i: (0, 0)),
            pl.BlockSpec((1, D_MODEL), lambda i: (0, 0)),
        ],
        out_shape=[
            jax.ShapeDtypeStruct((T, D_MODEL), BF16),
            jax.ShapeDtypeStruct((T, D_FF), BF16),
            jax.ShapeDtypeStruct((T, D_FF), BF16),
            jax.ShapeDtypeStruct((T, D_FF), BF16),
            jax.ShapeDtypeStruct((T, D_MODEL), F32),
            jax.ShapeDtypeStruct((8, LANES), F32),
            jax.ShapeDtypeStruct((1, D_MODEL), F32),
        ],
        compiler_params=_params(("arbitrary",)),
    )(x1, g2, gf, tgt, w_gate_t, w_up_t, w_down)


def _ffn_bwd(dx2, gate, up, x1, g2, w_gate_t, w_up_t, w_down):
    T = x1.shape[0]
    tm = min(T, FF_ROW_TILE)
    nc = D_FF // FF_CHUNK

    def body(dx2_ref, gate_ref, up_ref, x1_ref, g2_ref, wg_ref, wu_ref, wd_ref, dgate_ref, dup_ref, dx1_ref, dg2_ref):
        dx2v = dx2_ref[...]
        dx2b = dx2v.astype(BF16)
        for c in range(nc):
            sl = slice(c * FF_CHUNK, (c + 1) * FF_CHUNK)
            dact = _mm_nt(dx2b, wd_ref[sl, :])
            gate = gate_ref[:, sl].astype(F32)
            sg = _sigmoid(gate)
            silu = gate * sg
            dgate = (dact * up_ref[:, sl].astype(F32) * (sg * (1.0 + gate * (1.0 - sg)))).astype(BF16)
            dup = (dact * silu).astype(BF16)
            dgate_ref[:, sl] = dgate
            dup_ref[:, sl] = dup
        dh2 = _mm(dgate_ref[...], wg_ref[...]) + _mm(dup_ref[...], wu_ref[...])
        g2v = g2_ref[...]
        _, xh, r = _rms_fwd(x1_ref[...], g2v)
        dxn, dgrow = _rms_bwd(dh2, xh, r, g2v)
        dx1_ref[...] = dx2v + dxn

        @pl.when(pl.program_id(0) == 0)
        def _():
            dg2_ref[...] = jnp.zeros_like(dg2_ref)

        dg2_ref[...] += jnp.sum(dgrow, axis=0, keepdims=True)

    row = lambda n: pl.BlockSpec((tm, n), lambda i: (i, 0))
    return pl.pallas_call(
        body,
        name="ffn_bwd",
        grid=(T // tm,),
        in_specs=[
            row(D_MODEL), row(D_FF), row(D_FF), row(D_MODEL), _const_spec((1, D_MODEL)),
            _const_spec(w_gate_t.shape), _const_spec(w_up_t.shape), _const_spec(w_down.shape),
        ],
        out_specs=[row(D_FF), row(D_FF), row(D_MODEL), pl.BlockSpec((1, D_MODEL), lambda i: (0, 0))],
        out_shape=[
            jax.ShapeDtypeStruct((T, D_FF), BF16),
            jax.ShapeDtypeStruct((T, D_FF), BF16),
            jax.ShapeDtypeStruct((T, D_MODEL), F32),
            jax.ShapeDtypeStruct((1, D_MODEL), F32),
        ],
        compiler_params=_params(("arbitrary",), VMEM_LIMIT_MAX),
    )(dx2, gate, up, x1, g2, w_gate_t, w_up_t, w_down)


def _mix_bwd(dx1, gates, pool_y, attn_y, p2, scale, w_out, w_ao, w_po, token):
    T = dx1.shape[0]
    tm = ROW_TILE

    def body(dx1_ref, gt_ref, py_ref, ay_ref, p2_ref, sc_ref, wout_ref, wao_ref, wpo_ref, token_ref, dgt_ref, dpy_ref, day_ref, da_ref, dp2_ref, dsc_ref):
        dm = _mm_nt(dx1_ref[...].astype(BF16), wout_ref[...])
        sp = _sigmoid(gt_ref[:, :D_MODEL].astype(F32))
        sa = _sigmoid(gt_ref[:, D_MODEL:].astype(F32))
        dgt_ref[:, :D_MODEL] = (dm * py_ref[...].astype(F32) * (sp * (1.0 - sp))).astype(BF16)
        dgt_ref[:, D_MODEL:] = (dm * ay_ref[...].astype(F32) * (sa * (1.0 - sa))).astype(BF16)
        dpy = (dm * sp).astype(BF16)
        day = (dm * sa).astype(BF16)
        dpy_ref[...] = dpy
        day_ref[...] = day
        da_ref[...] = _mm_nt(day, _whole_cols(wao_ref)).astype(BF16)
        dp3 = _mm_nt(dpy, _whole_cols(wpo_ref))
        dp2_ref[...] = (dp3 * sc_ref[...]).astype(BF16)

        @pl.when(pl.program_id(0) == 0)
        def _():
            dsc_ref[...] = jnp.zeros_like(dsc_ref)

        dsc_ref[...] += jnp.sum(dp3 * p2_ref[...], axis=0, keepdims=True)

    row = lambda n: pl.BlockSpec((tm, n), lambda i: (i, 0))
    return pl.pallas_call(
        body,
        name="mix_bwd",
        grid=(T // tm,),
        in_specs=[
            row(D_MODEL), row(2 * D_MODEL), row(D_MODEL), row(D_MODEL), row(POOL_WIDTH), _const_spec((1, POOL_WIDTH)),
            _const_spec(w_out.shape), _const_spec(w_ao.shape), _const_spec(w_po.shape), _HBM,
        ],
        out_specs=[row(2 * D_MODEL), row(D_MODEL), row(D_MODEL), row(ATTN_WIDTH), row(POOL_WIDTH), pl.BlockSpec((1, POOL_WIDTH), lambda i: (0, 0))],
        out_shape=[
            jax.ShapeDtypeStruct((T, 2 * D_MODEL), BF16),
            jax.ShapeDtypeStruct((T, D_MODEL), BF16),
            jax.ShapeDtypeStruct((T, D_MODEL), BF16),
            jax.ShapeDtypeStruct((T, ATTN_WIDTH), BF16),
            jax.ShapeDtypeStruct((T, POOL_WIDTH), BF16),
            jax.ShapeDtypeStruct((1, POOL_WIDTH), F32),
        ],
        compiler_params=_params(("arbitrary",)),
    )(dx1, gates, pool_y, attn_y, p2, scale, w_out, w_ao, w_po, token)


def _pool_bwd(dp2, pm, mix_b, token, n_seq, S):
    T = n_seq * S

    def body(dp2_ref, pm_ref, mix_ref, token_ref, du_ref, dmix_ref):
        g = pl.program_id(0)
        dp2v = dp2_ref[...]
        dpm = _mm_nt(dp2v, mix_ref[...])
        row = lax.broadcasted_iota(jnp.int32, dpm.shape, 0)
        w = _window_pick(g, 2.0, 4.0, 8.0, 16.0)
        e = dpm / jnp.minimum((row + 1).astype(F32), w)

        def ahead(a, k):
            return jnp.where(row < S - k, pltpu.roll(a, S - k, 0), 0.0)

        r2 = e + ahead(e, 1)
        r4 = r2 + ahead(r2, 2)
        r8 = r4 + ahead(r4, 4)
        r16 = r8 + ahead(r8, 8)
        du_ref[...] = (_window_pick(g, r2, r4, r8, r16) - dpm).astype(BF16)

        @pl.when(pl.program_id(1) == 0)
        def _():
            dmix_ref[...] = jnp.zeros_like(dmix_ref)

        dmix_ref[...] += _mm_tn(pm_ref[...], dp2v)

    grp = pl.BlockSpec((S, GROUP_DIM), lambda g, s: (s, g))
    mixs = pl.BlockSpec((None, GROUP_DIM, GROUP_DIM), lambda g, s: (g, 0, 0))
    return pl.pallas_call(
        body,
        name="pool_bwd",
        grid=(len(POOL_WINDOWS), n_seq),
        in_specs=[grp, grp, mixs, _HBM],
        out_specs=[grp, mixs],
        out_shape=[jax.ShapeDtypeStruct((T, POOL_WIDTH), BF16), jax.ShapeDtypeStruct((len(POOL_WINDOWS), GROUP_DIM, GROUP_DIM), F32)],
        compiler_params=_params(("parallel", "arbitrary")),
    )(dp2, pm, mix_b, token)


def _attn_bwd(qkv, da, a, fcol, lse, n_seq, S):
    T = n_seq * S
    tb = ATTN_BLOCK
    nb = S // tb
    scale = HEAD_DIM ** -0.5

    def body(q_ref, k_ref, v_ref, do_ref, o_ref, fc_ref, st_ref, dq_ref, dk_ref, dv_ref, dfk_ref, dfq_ref,
             qa_sc, doa_sc, qat_sc, doat_sc, dq_acc, ka_sc, va_sc, dkt_sc, dvt_sc):
        j = pl.program_id(1)
        lane = lax.broadcasted_iota(jnp.int32, (1, LANES), 1)
        low = lane < HEAD_DIM

        @pl.when(j == 0)
        def _():
            dq_acc[...] = jnp.zeros_like(dq_acc)
            place = _bias_placement(0)

            def rows_q(i, carry):
                r0 = pl.multiple_of(i * tb, tb)
                delta = jnp.zeros((tb, LANES), F32)
                for h in range(N_HEADS):
                    pair = slice((h // 2) * LANES, (h // 2 + 1) * LANES)
                    prod = do_ref[pl.ds(r0, tb), pair].astype(F32) * o_ref[pl.ds(r0, tb), pair].astype(F32)
                    head = (lane >= HEAD_DIM * (h % 2)) & (lane < HEAD_DIM * (h % 2 + 1))
                    delta = jnp.where(lane == h, jnp.sum(jnp.where(head, prod, 0.0), axis=1, keepdims=True), delta)
                cq = fc_ref[pl.ds(r0, tb), :] - st_ref[pl.ds(r0, tb), :]
                q_bias = _mm(_bias_lanes(cq), place).astype(BF16)
                do_bias = _mm(_bias_lanes(-delta), place).astype(BF16)
                for h in range(N_HEADS):
                    pair = slice((h // 2) * LANES, (h // 2 + 1) * LANES)
                    qa = _augment(q_ref[pl.ds(r0, tb), pair], h, q_bias, 1)
                    doa = _augment(do_ref[pl.ds(r0, tb), pair], h, do_bias, None)
                    qa_sc[h, pl.ds(r0, tb), :] = qa
                    doa_sc[h, pl.ds(r0, tb), :] = doa
                    qat_sc[h, i] = qa.astype(F32).T.astype(BF16)
                    doat_sc[h, i] = doa.astype(F32).T.astype(BF16)
                return carry

            lax.fori_loop(0, nb, rows_q, 0)

        c0 = pl.multiple_of(j * tb, tb)
        k_bias = _mm(_bias_lanes(-fc_ref[pl.ds(c0, tb), :]), _bias_placement(1)).astype(BF16)
        for h in range(N_HEADS):
            pair = slice((h // 2) * LANES, (h // 2 + 1) * LANES)
            ka_sc[h] = _augment(k_ref[:, pair] * scale, h, k_bias, 0)
            va_sc[h] = _augment(v_ref[:, pair], h, None, 0)
        dkt_sc[...] = jnp.zeros_like(dkt_sc)
        dvt_sc[...] = jnp.zeros_like(dvt_sc)
        causal = lax.broadcasted_iota(jnp.int32, (tb, tb), 1) <= lax.broadcasted_iota(jnp.int32, (tb, tb), 0)

        def step(i, masked):
            r0 = pl.multiple_of(i * tb, tb)
            for h in range(N_HEADS):
                s = _mm_nt(qa_sc[h, pl.ds(r0, tb), :], ka_sc[h])
                if masked:
                    s = jnp.where(causal, s, -jnp.inf)
                pr = jnp.exp(s)
                dvt_sc[h] += _mm(doat_sc[h, i], pr.astype(BF16))
                dsb = (pr * _mm_nt(doa_sc[h, pl.ds(r0, tb), :], va_sc[h])).astype(BF16)
                dkt_sc[h] += _mm(qat_sc[h, i], dsb)
                dq_acc[h, pl.ds(r0, tb), :] += _mm(dsb, ka_sc[h])

        step(j, True)

        def loop_body(i, carry):
            step(i, False)
            return carry

        lax.fori_loop(j + 1, nb, loop_body, 0)
        dfk = jnp.zeros((tb, LANES), F32)
        for p in range(N_PAIRS):
            dk = [dkt_sc[2 * p + hh].T for hh in range(2)]
            dv = [dvt_sc[2 * p + hh].T for hh in range(2)]
            dk_ref[:, p * LANES : (p + 1) * LANES] = (jnp.where(low, dk[0], dk[1]) * scale).astype(BF16)
            dv_ref[:, p * LANES : (p + 1) * LANES] = jnp.where(low, dv[0], dv[1]).astype(BF16)
            for hh in range(2):
                b = HEAD_DIM * (1 - hh) + 3
                dfk = jnp.where(lane == 2 * p + hh, -dk[hh][:, b : b + 1], dfk)
        dfk_ref[...] = dfk

        @pl.when(j == nb - 1)
        def _():
            def rows_dq(i, carry):
                r0 = pl.multiple_of(i * tb, tb)
                dfq = jnp.zeros((tb, LANES), F32)
                for p in range(N_PAIRS):
                    parts = [dq_acc[2 * p + hh, pl.ds(r0, tb), :] for hh in range(2)]
                    dq_ref[pl.ds(r0, tb), p * LANES : (p + 1) * LANES] = jnp.where(low, parts[0], parts[1]).astype(BF16)
                    for hh in range(2):
                        b = HEAD_DIM * (1 - hh)
                        dfq = jnp.where(lane == 2 * p + hh, parts[hh][:, b : b + 1], dfq)
                dfq_ref[pl.ds(r0, tb), :] = dfq
                return carry

            lax.fori_loop(0, nb, rows_dq, 0)

    seq = lambda w, col: pl.BlockSpec((S, w), lambda s, j: (s, col))
    seq_in = lambda w, col: pl.BlockSpec((S, w), lambda s, j: (s, col), pipeline_mode=pl.Buffered(1))
    blk = lambda w, col: pl.BlockSpec((tb, w), lambda s, j: (s * nb + j, col))
    return pl.pallas_call(
        body,
        name="attn_bwd",
        grid=(n_seq, nb),
        in_specs=[seq_in(ATTN_WIDTH, 0), blk(ATTN_WIDTH, 1), blk(ATTN_WIDTH, 2), seq_in(ATTN_WIDTH, 0), seq_in(ATTN_WIDTH, 0), seq_in(LANES, 0), seq_in(LANES, 0)],
        out_specs=[seq(ATTN_WIDTH, 0), blk(ATTN_WIDTH, 0), blk(ATTN_WIDTH, 0), blk(LANES, 0), seq(LANES, 0)],
        out_shape=[
            jax.ShapeDtypeStruct((T, ATTN_WIDTH), BF16),
            jax.ShapeDtypeStruct((T, ATTN_WIDTH), BF16),
            jax.ShapeDtypeStruct((T, ATTN_WIDTH), BF16),
            jax.ShapeDtypeStruct((T, LANES), F32),
            jax.ShapeDtypeStruct((T, LANES), F32),
        ],
        scratch_shapes=[
            pltpu.VMEM((N_HEADS, S, LANES), BF16),
            pltpu.VMEM((N_HEADS, S, LANES), BF16),
            pltpu.VMEM((N_HEADS, nb, LANES, tb), BF16),
            pltpu.VMEM((N_HEADS, nb, LANES, tb), BF16),
            pltpu.VMEM((N_HEADS, S, LANES), F32),
            pltpu.VMEM((N_HEADS, tb, LANES), BF16),
            pltpu.VMEM((N_HEADS, tb, LANES), BF16),
            pltpu.VMEM((N_HEADS, LANES, tb), F32),
            pltpu.VMEM((N_HEADS, LANES, tb), F32),
        ],
        compiler_params=_params(("parallel", "arbitrary"), VMEM_LIMIT_MAX),
    )(qkv, qkv, qkv, da, a, fcol, lse)


def _forget_bwd(dfk, dfq, fl, b_pad, n_seq, S):
    def body(df_ref, dfq_ref, fl_ref, b_ref, dfl_ref, db_ref):
        t = (df_ref[...] + dfq_ref[...]).T
        lane = lax.broadcasted_iota(jnp.int32, t.shape, 1)
        k = 1
        while k < S:
            t = t + jnp.where(lane < S - k, pltpu.roll(t, S - k, 1), 0.0)
            k *= 2
        dfl = t.T * _sigmoid(-(fl_ref[...] + b_ref[...]))
        dfl_ref[...] = dfl.astype(BF16)

        @pl.when(pl.program_id(0) == 0)
        def _():
            db_ref[...] = jnp.zeros_like(db_ref)

        db_ref[...] += jnp.sum(dfl, axis=0, keepdims=True)

    return pl.pallas_call(
        body,
        name="forget_bwd",
        grid=(n_seq,),
        in_specs=[
            pl.BlockSpec((S, LANES), lambda s: (s, 0)),
            pl.BlockSpec((S, LANES), lambda s: (s, 0)),
            pl.BlockSpec((S, FL_PAD), lambda s: (s, 0)),
            _const_spec((1, FL_PAD)),
        ],
        out_specs=[pl.BlockSpec((S, FL_PAD), lambda s: (s, 0)), pl.BlockSpec((1, FL_PAD), lambda s: (0, 0))],
        out_shape=[jax.ShapeDtypeStruct((n_seq * S, FL_PAD), BF16), jax.ShapeDtypeStruct((1, FL_PAD), F32)],
        compiler_params=_params(("arbitrary",)),
    )(dfk, dfq, fl, b_pad)


def _in_proj_bwd(du, dq, dk, dv, dfl, dgates, x, dx1, g1, w_uqkv, w_fl, w_g, token):
    T = x.shape[0]
    tm = ROW_TILE

    def body(du_ref, dq_ref, dk_ref, dv_ref, dfl_ref, dgt_ref, x_ref, dx1_ref, g_ref, wa_ref, wf_ref, wg_ref, token_ref, dx_ref, dg_ref):
        dz = jnp.concatenate([du_ref[...], dq_ref[...], dk_ref[...], dv_ref[...]], axis=1)
        dh = _mm_nt(dz, wa_ref[...]) + _mm_nt(dgt_ref[...], wg_ref[...]) + _mm_nt(dfl_ref[...], wf_ref[...])
        gv = g_ref[...]
        _, xh, r = _rms_fwd(x_ref[...], gv)
        dxn, dgrow = _rms_bwd(dh, xh, r, gv)
        dx_ref[...] = dx1_ref[...] + dxn

        @pl.when(pl.program_id(0) == 0)
        def _():
            dg_ref[...] = jnp.zeros_like(dg_ref)

        dg_ref[...] += jnp.sum(dgrow, axis=0, keepdims=True)

    row = lambda n: pl.BlockSpec((tm, n), lambda i: (i, 0))
    return pl.pallas_call(
        body,
        name="in_proj_bwd",
        grid=(T // tm,),
        in_specs=[
            row(512), row(512), row(512), row(512), row(FL_PAD), row(2 * D_MODEL), row(D_MODEL), row(D_MODEL), _const_spec((1, D_MODEL)),
            _const_spec(w_uqkv.shape), _const_spec(w_fl.shape), _const_spec(w_g.shape), _HBM,
        ],
        out_specs=[row(D_MODEL), pl.BlockSpec((1, D_MODEL), lambda i: (0, 0))],
        out_shape=[jax.ShapeDtypeStruct((T, D_MODEL), F32), jax.ShapeDtypeStruct((1, D_MODEL), F32)],
        compiler_params=_params(("arbitrary",)),
    )(du, dq, dk, dv, dfl, dgates, x, dx1, g1, w_uqkv, w_fl, w_g, token)


def _pick_block(n):
    for b in (1024, 512, 1408, 256, 128):
        if n % b == 0:
            return b
    raise ValueError(n)


def _matmul_tn(a, b, name, col_chunks=False):
    T, K = a.shape
    N = b.shape[1]
    bt, bk, bn = min(T, DW_TOKENS), _pick_block(K), _pick_block(N)
    nt = T // bt
    c = N // N_DEV
    assert not col_chunks or (bn == N and c % LANES == 0)

    def body(a_ref, b_ref, o_ref, acc):
        @pl.when(pl.program_id(2) == 0)
        def _():
            acc[...] = jnp.zeros_like(acc)

        acc[...] += _mm_tn(a_ref[...].astype(BF16), b_ref[...].astype(BF16))

        @pl.when(pl.program_id(2) == nt - 1)
        def _():
            if col_chunks:
                for d in range(N_DEV):
                    o_ref[d] = acc[:, d * c : (d + 1) * c].astype(BF16)
            else:
                o_ref[...] = acc[...].astype(BF16)

    if col_chunks:
        out_spec, out_shape = pl.BlockSpec((N_DEV, bk, c), lambda k, n, t: (0, k, 0)), (N_DEV, K, c)
    else:
        out_spec, out_shape = pl.BlockSpec((bk, bn), lambda k, n, t: (k, n)), (K, N)
    return pl.pallas_call(
        body,
        name=name,
        grid=(K // bk, N // bn, nt),
        in_specs=[pl.BlockSpec((bt, bk), lambda k, n, t: (t, k)), pl.BlockSpec((bt, bn), lambda k, n, t: (t, n))],
        out_specs=out_spec,
        out_shape=jax.ShapeDtypeStruct(out_shape, BF16),
        scratch_shapes=[pltpu.VMEM((bk, bn), F32)],
        compiler_params=_params(("parallel", "parallel", "arbitrary")),
    )(a, b)


W_IN_A = POOL_WIDTH + 3 * ATTN_WIDTH
W_IN_SHARD = (W_IN_A + N_HEADS + 2 * D_MODEL) // N_DEV
_W_IN_PIECES = ((0, W_IN_A), (W_IN_A, W_IN_A + N_HEADS), (W_IN_A + N_HEADS, W_IN_A + N_HEADS + 2 * D_MODEL))


def _w_in_segments(d):
    lo, hi = d * W_IN_SHARD, (d + 1) * W_IN_SHARD
    out = []
    for p, (a, b) in enumerate(_W_IN_PIECES):
        s, e = max(lo, a), min(hi, b)
        if s < e:
            out.append((p, s - a, s - lo, e - s))
    return out


def _w_in_pieces(gathered, tails):
    tm = ROW_TILE // 2
    tail_rows = tm // LANES
    aligned = W_IN_SHARD - 1

    def body(g_ref, t_ref, wa_ref, wf_ref, wg_ref):
        outs = (wa_ref, wf_ref, wg_ref)
        wf_ref[...] = jnp.zeros_like(wf_ref)
        diagonal = lax.broadcasted_iota(jnp.int32, (LANES, LANES), 0) == lax.broadcasted_iota(jnp.int32, (LANES, LANES), 1)
        for d in range(N_DEV):
            for p, at, frm, n in _w_in_segments(d):
                m = min(n, aligned - frm)
                if m > 0:
                    outs[p][:, at : at + m] = g_ref[d, :, frm : frm + m]
                if frm + n == W_IN_SHARD:
                    column = [
                        jnp.sum(jnp.where(diagonal, jnp.broadcast_to(t_ref[d, k : k + 1, :], (LANES, LANES)), 0.0), axis=1, keepdims=True)
                        for k in range(tail_rows)
                    ]
                    outs[p][:, at + n - 1 : at + n] = jnp.concatenate(column, axis=0).astype(outs[p].dtype)

    return pl.pallas_call(
        body,
        name="w_in_pieces",
        grid=(D_MODEL // tm,),
        in_specs=[
            pl.BlockSpec((N_DEV, tm, aligned), lambda i: (0, i, 0)),
            pl.BlockSpec((N_DEV, None, tail_rows, LANES), lambda i: (0, i, 0, 0)),
        ],
        out_specs=[pl.BlockSpec((tm, W_IN_A), lambda i: (i, 0)), pl.BlockSpec((tm, FL_PAD), lambda i: (i, 0)), pl.BlockSpec((tm, 2 * D_MODEL), lambda i: (i, 0))],
        out_shape=[
            jax.ShapeDtypeStruct((D_MODEL, W_IN_A), gathered.dtype),
            jax.ShapeDtypeStruct((D_MODEL, FL_PAD), gathered.dtype),
            jax.ShapeDtypeStruct((D_MODEL, 2 * D_MODEL), gathered.dtype),
        ],
        compiler_params=_params(("parallel",)),
    )(gathered, tails.reshape(N_DEV, D_MODEL // tm, tail_rows, LANES))


def _dw_in(h, du, dq, dk, dv, dfl, dgates, token):
    T = h.shape[0]
    bt, bk = min(T, DW_TOKENS // 2), 512
    nt = T // bt
    pieces = (du, dq, dk, dv, dfl, dgates)
    offs = [0]
    for p in pieces:
        offs.append(offs[-1] + p.shape[1])

    aligned = W_IN_SHARD - 1
    tail_rows = bk // LANES

    def body(h_ref, *rest):
        refs, o_ref, t_ref, acc = rest[: len(pieces)], rest[-3], rest[-2], rest[-1]

        @pl.when(pl.program_id(1) == 0)
        def _():
            acc[...] = jnp.zeros_like(acc)

        ht = h_ref[...].T
        for ref, at in zip(refs, offs):
            acc[:, at : at + ref.shape[1]] += _mm(ht, ref[...])

        @pl.when(pl.program_id(1) == nt - 1)
        def _():
            starts = (0, W_IN_A, W_IN_A + FL_PAD)
            diagonal = lax.broadcasted_iota(jnp.int32, (LANES, LANES), 0) == lax.broadcasted_iota(jnp.int32, (LANES, LANES), 1)
            for d in range(N_DEV):
                for p, at, to, n in _w_in_segments(d):
                    m = min(n, aligned - to)
                    if m > 0:
                        o_ref[d, :, to : to + m] = acc[:, starts[p] + at : starts[p] + at + m].astype(BF16)
                    if to + n == W_IN_SHARD:
                        last = starts[p] + at + n - 1
                        column = acc[:, last : last + 1].astype(BF16).astype(F32)
                        for k in range(tail_rows):
                            rows = jnp.broadcast_to(column[k * LANES : (k + 1) * LANES], (LANES, LANES))
                            t_ref[d, k : k + 1, :] = jnp.sum(jnp.where(diagonal, rows, 0.0), axis=0, keepdims=True)

    main, tails = pl.pallas_call(
        body,
        name="dw_in",
        grid=(D_MODEL // bk, nt),
        in_specs=[pl.BlockSpec((bt, bk), lambda k, t: (t, k))] + [pl.BlockSpec((bt, p.shape[1]), lambda k, t: (t, 0)) for p in pieces] + [_HBM],
        out_specs=[
            pl.BlockSpec((N_DEV, bk, aligned), lambda k, t: (0, k, 0)),
            pl.BlockSpec((N_DEV, None, tail_rows, LANES), lambda k, t: (0, k, 0, 0)),
        ],
        out_shape=[
            jax.ShapeDtypeStruct((N_DEV, D_MODEL, aligned), BF16),
            jax.ShapeDtypeStruct((N_DEV, D_MODEL // bk, tail_rows, LANES), F32),
        ],
        scratch_shapes=[pltpu.VMEM((bk, offs[-1]), F32)],
        compiler_params=_params(("parallel", "arbitrary")),
    )(h, *pieces, token)
    return main, tails.reshape(N_DEV, D_MODEL // LANES, LANES)


def _position():
    return lax.axis_index("x"), lax.axis_index("y"), lax.axis_index("c")


_HBM = pl.BlockSpec(memory_space=pl.ANY)


def _all_gather(blocks, name):
    n = len(blocks)

    def body(*refs):
        xs, outs = refs[:n], refs[n : 2 * n]
        send_sems, recv_sems, local_sems = refs[2 * n :]
        x, y, c = _position()
        me, sibling = (x, y, c), (x, y, 1 - c)
        chips = [(1 - x, y), (x, 1 - y), (1 - x, 1 - y)]

        def rows(a, px, py, pc):
            return outs[a].at[4 * px + 2 * py + pc]

        def copy(a, k, blk, to, src=None):
            return pltpu.make_async_remote_copy(
                src_ref=rows(a, *blk) if src is None else src, dst_ref=rows(a, *blk),
                send_sem=send_sems.at[7 * a + k], recv_sem=recv_sems.at[7 * a + k], device_id=to, device_id_type=MESH,
            )

        first = []
        for a in range(n):
            first += [copy(a, 1 + j, me, (*chip, c), src=xs[a]) for j, chip in enumerate(chips)]
            first.append(copy(a, 0, me, sibling, src=xs[a]))
        mine = [pltpu.make_async_copy(xs[a], rows(a, *me), local_sems.at[a]) for a in range(n)]
        for cp in first + mine:
            cp.start()
        passed = []
        for j, chip in enumerate(chips):
            for a in range(n):
                copy(a, 1 + j, (*chip, c), me).wait_recv()
                passed.append(copy(a, 4 + j, (*chip, c), sibling))
                passed[-1].start()
        for a in range(n):
            copy(a, 0, sibling, me).wait_recv()
        for j, chip in enumerate(chips):
            for a in range(n):
                copy(a, 4 + j, (*chip, 1 - c), me).wait_recv()
        for cp in first + passed:
            cp.wait_send()
        for cp in mine:
            cp.wait()

    return pl.pallas_call(
        body,
        name=name,
        out_shape=[jax.ShapeDtypeStruct((N_DEV, *b.shape), b.dtype) for b in blocks],
        in_specs=[_HBM] * n,
        out_specs=[_HBM] * n,
        scratch_shapes=[pltpu.SemaphoreType.DMA((7 * n,)), pltpu.SemaphoreType.DMA((7 * n,)), pltpu.SemaphoreType.DMA((n,))],
    )(*blocks)


_SEM = pl.BlockSpec(memory_space=pltpu.SEMAPHORE)
_HBM_ONLY = pl.BlockSpec(memory_space=pltpu.HBM)
_SIDE_EFFECT = pltpu.SideEffectType.DATAFLOW_SIDE_EFFECTING


def _peer(x, y, c, k):
    return (1 - x if k & 4 else x, 1 - y if k & 2 else y, 1 - c if k & 1 else c)


_PEER_BITS = {"gather": range(1, N_DEV), "gather_half": (1, 4, 2, 6), "forward": (4, 2, 6), "scatter": range(1, N_DEV)}
_GATHERS = ("gather", "gather_half")


def _exchange_copies(src_refs, land_refs, send_sems, recv_sems, pattern, receive_side):
    x, y, c = _position()
    me = 4 * x + 2 * y + c
    bits = _PEER_BITS[pattern]
    cps = []
    for j, k in enumerate(bits):
        px, py, pc = _peer(x, y, c, k)
        peer = 4 * px + 2 * py + pc
        for a, (src, land) in enumerate(zip(src_refs, land_refs)):
            to = (px, py, pc)
            if pattern == "forward":
                slot = 4 * px + 2 * py + (1 - c if receive_side else c)
                s, to = land.at[slot], (x, y, 1 - c)
            else:
                s, slot = (src if pattern in _GATHERS else src.at[peer]), (peer if receive_side else me)
            cps.append(pltpu.make_async_remote_copy(
                src_ref=s, dst_ref=land.at[slot],
                send_sem=send_sems.at[len(bits) * a + j], recv_sem=recv_sems.at[len(bits) * a + j],
                device_id=to, device_id_type=MESH,
            ))
    return cps


def _own_copies(src_refs, land_refs, own_sems):
    x, y, c = _position()
    return [
        pltpu.make_async_copy(src, land.at[4 * x + 2 * y + c], own_sems.at[a])
        for a, (src, land) in enumerate(zip(src_refs, land_refs))
    ]


def _exchange_start(srcs, after, name, pattern):
    n = len(srcs)
    m = len(_PEER_BITS[pattern])
    lands = [jax.ShapeDtypeStruct((N_DEV, *s.shape[-2:]), s.dtype) for s in srcs]

    def body(*refs):
        src_refs, land_refs = refs[1 : 1 + n], refs[1 + n : 1 + 2 * n]
        send_sems, recv_sems, own_sems = refs[1 + 2 * n : 4 + 2 * n]
        token = refs[-1]
        if pattern in _GATHERS:
            for cp in _own_copies(src_refs, land_refs, own_sems):
                cp.start()
        for cp in _exchange_copies(src_refs, land_refs, send_sems, recv_sems, pattern, receive_side=False):
            cp.start()
        token[...] = jnp.zeros_like(token)

    hbm = lambda t: pltpu.with_memory_space_constraint(t, pltpu.HBM)
    out = pl.pallas_call(
        body,
        name=name,
        out_shape=(
            pltpu.SemaphoreType.DMA((m * n,)), pltpu.SemaphoreType.DMA((m * n,)), pltpu.SemaphoreType.DMA((n,)),
            *[pltpu.HBM(s.shape, s.dtype) for s in srcs], *[pltpu.HBM(l.shape, l.dtype) for l in lands],
            jax.ShapeDtypeStruct((8, LANES), F32),
        ),
        in_specs=(_HBM, *[_HBM_ONLY] * (2 * n)),
        out_specs=(_SEM, _SEM, _SEM, *[_HBM_ONLY] * (2 * n), pl.BlockSpec(memory_space=pltpu.VMEM)),
        input_output_aliases={1 + i: 3 + i for i in range(2 * n)},
        compiler_params=pltpu.CompilerParams(has_side_effects=_SIDE_EFFECT),
    )(after, *[hbm(s) for s in srcs], *[hbm(lax.empty(l.shape, l.dtype)) for l in lands])
    return out[:3], out[3 : 3 + n], out[3 + n : 3 + 2 * n], out[-1]


def _exchange_wait(sems, srcs, lands, after, name, pattern):
    n = len(srcs)

    def body(*refs):
        src_refs, land_refs = refs[:n], refs[n : 2 * n]
        send_sems, recv_sems, own_sems = refs[2 * n : 2 * n + 3]
        if pattern in _GATHERS:
            for cp in _own_copies(src_refs, land_refs, own_sems):
                cp.wait()
        for cp in _exchange_copies(src_refs, land_refs, send_sems, recv_sems, pattern, receive_side=True):
            cp.wait_send()
            cp.wait_recv()

    out = pl.pallas_call(
        body,
        name=name,
        out_shape=(*[pltpu.HBM(s.shape, s.dtype) for s in srcs], *[pltpu.HBM(l.shape, l.dtype) for l in lands]),
        in_specs=(*[_HBM_ONLY] * (2 * n), _SEM, _SEM, _SEM, _HBM),
        out_specs=tuple([_HBM_ONLY] * (2 * n)),
        input_output_aliases={i: i for i in range(2 * n)},
        compiler_params=pltpu.CompilerParams(has_side_effects=_SIDE_EFFECT),
    )(*srcs, *lands, *sems, after)
    return out[:n], out[n:]


def _gather_forward(sems, srcs, lands, after, name):
    n = len(srcs)
    m = len(_PEER_BITS["forward"])

    def body(*refs):
        src_refs, land_refs = refs[:n], refs[n : 2 * n]
        send_sems, recv_sems, own_sems = refs[2 * n : 2 * n + 3]
        forward_send, forward_recv, token = refs[2 * n + 4], refs[2 * n + 5], refs[-1]
        for cp in _own_copies(src_refs, land_refs, own_sems):
            cp.wait()
        for cp in _exchange_copies(src_refs, land_refs, send_sems, recv_sems, "gather_half", receive_side=True):
            cp.wait_send()
            cp.wait_recv()
        for cp in _exchange_copies(land_refs, land_refs, forward_send, forward_recv, "forward", receive_side=False):
            cp.start()
        token[...] = jnp.zeros_like(token)

    out = pl.pallas_call(
        body,
        name=name,
        out_shape=(
            pltpu.SemaphoreType.DMA((m * n,)), pltpu.SemaphoreType.DMA((m * n,)),
            *[pltpu.HBM(l.shape, l.dtype) for l in lands], jax.ShapeDtypeStruct((8, LANES), F32),
        ),
        in_specs=(*[_HBM_ONLY] * (2 * n), _SEM, _SEM, _SEM, _HBM),
        out_specs=(_SEM, _SEM, *[_HBM_ONLY] * n, pl.BlockSpec(memory_space=pltpu.VMEM)),
        input_output_aliases={n + i: 2 + i for i in range(n)},
        compiler_params=pltpu.CompilerParams(has_side_effects=_SIDE_EFFECT),
    )(*srcs, *lands, *sems, after)
    return out[:2], out[2 : 2 + n], out[-1]


def _forward_wait(sems, lands, after, name):
    n = len(lands)

    def body(*refs):
        land_refs = refs[:n]
        for cp in _exchange_copies(land_refs, land_refs, refs[n], refs[n + 1], "forward", receive_side=True):
            cp.wait_send()
            cp.wait_recv()

    return pl.pallas_call(
        body,
        name=name,
        out_shape=tuple(pltpu.HBM(l.shape, l.dtype) for l in lands),
        in_specs=(*[_HBM_ONLY] * n, _SEM, _SEM, _HBM),
        out_specs=tuple([_HBM_ONLY] * n),
        input_output_aliases={i: i for i in range(n)},
        compiler_params=pltpu.CompilerParams(has_side_effects=_SIDE_EFFECT),
    )(*lands, *sems, after)


def _rows_tile(r):
    return ROW_TILE if r % ROW_TILE == 0 else r


def _adamw(w, g, m, v):
    m = ADAM_B1 * m + (1.0 - ADAM_B1) * g
    v = ADAM_B2 * v + (1.0 - ADAM_B2) * (g * g)
    m_hat = m / (1.0 - ADAM_B1 ** ADAM_STEP)
    v_hat = v / (1.0 - ADAM_B2 ** ADAM_STEP)
    delta = -ADAM_LR * (m_hat / (jnp.sqrt(v_hat) + ADAM_EPS) + ADAM_WD * w)
    return delta, m, v


def _shard_update_direct(parts, chunks, w, m, v, me, name):
    _, r, c = w.shape
    br = _rows_tile(r)

    def body(me_ref, p_ref, own_ref, w_ref, m_ref, v_ref, g_ref, d_ref, nm_ref, nv_ref):
        g = None
        for n in range(N_DEV):
            part = jnp.where(me_ref[0] == n, own_ref[...], p_ref[n]).astype(F32)
            g = part if g is None else g + part
        g_ref[...] = g
        d_ref[...], nm_ref[...], nv_ref[...] = _adamw(w_ref[...], g, m_ref[...], v_ref[...])

    shard = pl.BlockSpec((None, br, c), lambda i, me: (0, i, 0))
    return pl.pallas_call(
        body,
        name=name,
        grid_spec=pltpu.PrefetchScalarGridSpec(
            num_scalar_prefetch=1,
            grid=(r // br,),
            in_specs=[
                pl.BlockSpec((N_DEV, br, c), lambda i, me: (0, i, 0)),
                pl.BlockSpec((None, br, c), lambda i, me: (me[0], i, 0)),
                shard, shard, shard,
            ],
            out_specs=[shard, shard, shard, shard],
        ),
        out_shape=[jax.ShapeDtypeStruct((1, r, c), F32)] * 4,
        compiler_params=_params(("parallel",)),
    )(me, parts, chunks, w, m, v)


def _w_in_update(parts, chunks, tail_parts, tail_chunks, w, m, v, me, name):
    _, r, c = w.shape
    br = _rows_tile(r)
    tail_rows = br // LANES

    def body(me_ref, p_ref, own_ref, tp_ref, town_ref, w_ref, m_ref, v_ref, g_ref, d_ref, nm_ref, nv_ref):
        g = tail = None
        for n in range(N_DEV):
            mine = me_ref[0] == n
            part = jnp.where(mine, own_ref[...], p_ref[n]).astype(F32)
            last = jnp.where(mine, town_ref[...], tp_ref[n])
            g = part if g is None else g + part
            tail = last if tail is None else tail + last
        diagonal = lax.broadcasted_iota(jnp.int32, (LANES, LANES), 0) == lax.broadcasted_iota(jnp.int32, (LANES, LANES), 1)
        column = jnp.concatenate(
            [
                jnp.sum(jnp.where(diagonal, jnp.broadcast_to(tail[k : k + 1, :], (LANES, LANES)), 0.0), axis=1, keepdims=True)
                for k in range(tail_rows)
            ],
            axis=0,
        )
        for lo, hi, grad in ((0, c - 1, g), (c - 1, c, column)):
            g_ref[:, lo:hi] = grad
            d_ref[:, lo:hi], nm_ref[:, lo:hi], nv_ref[:, lo:hi] = _adamw(w_ref[:, lo:hi], grad, m_ref[:, lo:hi], v_ref[:, lo:hi])

    shard = pl.BlockSpec((None, br, c), lambda i, me: (0, i, 0))
    by_block = lambda t: t.reshape(N_DEV, r // br, tail_rows, LANES)
    return pl.pallas_call(
        body,
        name=name,
        grid_spec=pltpu.PrefetchScalarGridSpec(
            num_scalar_prefetch=1,
            grid=(r // br,),
            in_specs=[
                pl.BlockSpec((N_DEV, br, c - 1), lambda i, me: (0, i, 0)),
                pl.BlockSpec((None, br, c - 1), lambda i, me: (me[0], i, 0)),
                pl.BlockSpec((N_DEV, None, tail_rows, LANES), lambda i, me: (0, i, 0, 0)),
                pl.BlockSpec((None, None, tail_rows, LANES), lambda i, me: (me[0], i, 0, 0)),
                shard, shard, shard,
            ],
            out_specs=[shard, shard, shard, shard],
        ),
        out_shape=[jax.ShapeDtypeStruct((1, r, c), F32)] * 4,
        compiler_params=_params(("parallel",)),
    )(me, parts, chunks, by_block(tail_parts), by_block(tail_chunks), w, m, v)


def _small_update(parts, first_rows, ws, ms, vs):
    k = len(ws)

    def unpacked(rows, shape):
        if len(shape) == 2 and shape[1] <= LANES:
            return rows[0:1, : shape[1]]
        if len(shape) == 2:
            return jnp.concatenate([rows[r : r + 1] for r in range(shape[1] // LANES)], axis=1)
        return rows.reshape(shape)

    def body(p_ref, f_ref, *refs):
        w_refs, m_refs, v_refs = refs[:k], refs[k : 2 * k], refs[2 * k : 3 * k]
        outs, loss_ref = refs[3 * k : 7 * k], refs[7 * k]
        g, first = p_ref[0], f_ref[0]
        for n in range(1, N_DEV):
            g = g + p_ref[n]
            first = first + f_ref[n]
        g = jnp.concatenate([g[:8] + first, g[8:]], axis=0)
        off = 0
        for i, (_, rows) in enumerate(_SMALL):
            gi = unpacked(g[off : off + rows], w_refs[i].shape)
            off += rows
            outs[i][...] = gi
            outs[k + i][...], outs[2 * k + i][...], outs[3 * k + i][...] = _adamw(w_refs[i][...], gi, m_refs[i][...], v_refs[i][...])
        loss_ref[...] = g[off : off + 1, 0:1]

    out = pl.pallas_call(
        body,
        name="small_update",
        out_shape=[jax.ShapeDtypeStruct(w.shape, F32) for _ in range(4) for w in ws] + [jax.ShapeDtypeStruct((1, 1), F32)],
        compiler_params=pltpu.CompilerParams(vmem_limit_bytes=VMEM_LIMIT),
    )(parts, first_rows, *ws, *ms, *vs)
    return [out[a * k : (a + 1) * k] for a in range(4)], out[4 * k]


_SHARD_AXIS = (1, 1, 1, 0, 0, 0, 0)
_TRANSPOSED = (False, False, False, False, True, True, False)


def _full_from_gathered(t, axis):
    if axis == 0:
        return t.reshape(N_DEV * t.shape[1], t.shape[2])
    return t


_SMALL = (("norm1_g", 8), ("norm2_g", 8), ("norm_f_g", 8), ("b_forget", 8), ("pool_scale", 8), ("pool_mix", 512))


def _pack_small(vals, loss_row):
    parts = []
    for (name, rows), t in zip(_SMALL, vals):
        f = t.astype(F32).reshape(-1)
        f = jnp.concatenate([f, jnp.zeros((rows * LANES - f.shape[0],), F32)]).reshape(rows, LANES)
        parts.append(f)
    parts.append(loss_row)
    return jnp.concatenate(parts, axis=0)


def _local_grads(x, tgt, g1, g2, gf, b_forget, pool_mix, pool_scale, w_in, fwd_token, out_weights, ffn_weights, ffn_grads_out, out_grads_out, small_grads_out, in_grads_out, norm1_grad_out):
    n_seq, S, _ = x.shape
    T = n_seq * S
    x2 = x.reshape(T, D_MODEL)
    tg2 = tgt.reshape(T, D_MODEL)
    w_uqkv, w_fl, w_g = w_in
    b_pad = jnp.concatenate([b_forget.reshape(1, N_HEADS), jnp.zeros((1, FL_PAD - N_HEADS), F32)], axis=1)
    mix_b = pool_mix.reshape(len(POOL_WINDOWS), GROUP_DIM, GROUP_DIM).astype(BF16)
    scale = pool_scale.reshape(1, POOL_WIDTH)
    g1 = g1.reshape(1, D_MODEL)
    g2 = g2.reshape(1, D_MODEL)
    gf = gf.reshape(1, D_MODEL)

    h, u, qkv, fl, gates = _in_proj(x2, g1, w_uqkv, w_fl, w_g, fwd_token)
    fcol = _forget_fwd(fl, b_pad, n_seq, S)
    pm, p2, p3 = _pool_fwd(u, mix_b, scale, n_seq, S)
    a, lse = _attn_fwd(qkv, fcol, n_seq, S)
    w_po, w_ao, w_out = out_weights(a)
    merged, x1, attn_y, pool_y = _mix_out(a, p3, gates, x2, w_ao, w_po, w_out)
    w_gate_t, w_up_t, w_down = ffn_weights(x1)
    h2, gate, up, act, dx2, loss_rows, dgf = _ffn_fwd(x1, g2, gf, tg2, w_gate_t, w_up_t, w_down)

    dgate, dup, dx1, dg2 = _ffn_bwd(dx2, gate, up, x1, g2, w_gate_t, w_up_t, w_down)
    bwd_token = ffn_grads_out(_matmul_tn(dgate, h2, "dw_ffn_gate"), _matmul_tn(dup, h2, "dw_ffn_up"), _matmul_tn(act, dx2, "dw_ffn_down"))
    dgates, dpy, day, da, dp2, dscale = _mix_bwd(dx1, gates, pool_y, attn_y, p2, scale, w_out, w_ao, w_po, bwd_token)
    out_token = out_grads_out(
        _matmul_tn(p3, dpy, "dw_pool_out", col_chunks=True), _matmul_tn(a, day, "dw_attn_out", col_chunks=True), _matmul_tn(merged, dx1, "dw_out")
    )
    du, dmix = _pool_bwd(dp2, pm, mix_b, out_token, n_seq, S)
    dq, dk, dv, dfk, dfq = _attn_bwd(qkv, da, a, fcol, lse, n_seq, S)
    dfl, db = _forget_bwd(dfk, dfq, fl, b_pad, n_seq, S)
    small_token = small_grads_out((jnp.zeros_like(g1), dg2, dgf, db[:, :N_HEADS], dscale, dmix), loss_rows)
    in_token = in_grads_out(*_dw_in(h, du, dq, dk, dv, dfl, dgates, small_token))
    dx, dg1 = _in_proj_bwd(du, dq, dk, dv, dfl, dgates, x2, dx1, g1, w_uqkv, w_fl, w_g, in_token)
    norm1_grad_out(dg1)
    return dx.reshape(n_seq, S, D_MODEL)


def kernel(x, norm1_g, w_in, b_forget, pool_mix, pool_scale, w_pool_out, w_attn_out, w_out, norm2_g, w_ffn_gate, w_ffn_up, w_ffn_down, norm_f_g, loss_target, m_norm1_g, m_w_in, m_b_forget, m_pool_mix, m_pool_scale, m_w_pool_out, m_w_attn_out, m_w_out, m_norm2_g, m_w_ffn_gate, m_w_ffn_up, m_w_ffn_down, m_norm_f_g, v_norm1_g, v_w_in, v_b_forget, v_pool_mix, v_pool_scale, v_w_pool_out, v_w_attn_out, v_w_out, v_norm2_g, v_w_ffn_gate, v_w_ffn_up, v_w_ffn_down, v_norm_f_g):
    names = ("w_in", "w_pool_out", "w_attn_out", "w_out", "w_ffn_gate", "w_ffn_up", "w_ffn_down")
    w_sh = (w_in, w_pool_out, w_attn_out, w_out, w_ffn_gate, w_ffn_up, w_ffn_down)
    m_sh = (m_w_in, m_w_pool_out, m_w_attn_out, m_w_out, m_w_ffn_gate, m_w_ffn_up, m_w_ffn_down)
    v_sh = (v_w_in, v_w_pool_out, v_w_attn_out, v_w_out, v_w_ffn_gate, v_w_ffn_up, v_w_ffn_down)

    cx, cy, cc = _position()
    me = 4 * cx + 2 * cy + cc
    def stored(t, transposed):
        return jnp.transpose(t, (0, 2, 1)) if transposed else t

    w_sh, m_sh, v_sh = ([stored(t, tr) for t, tr in zip(ts, _TRANSPOSED)] for ts in (w_sh, m_sh, v_sh))
    shards = [w[0].astype(BF16) for w in w_sh]
    last_in = shards[0][:, W_IN_SHARD - 1].astype(F32).reshape(D_MODEL // LANES, LANES)
    gathered_in, tails_in = _all_gather([shards[0][:, : W_IN_SHARD - 1], last_in], "w_in_all_gather")
    out_sems = _exchange_start(shards[1:4], gathered_in, "out_weights_gather_start", "gather")
    ffn_sems = _exchange_start(shards[4:], out_sems[3], "ffn_weights_gather_start", "gather_half")
    no_order = jnp.zeros((8, LANES), F32)
    started = {}

    def out_weights(after):
        forward_sems, lands, token = _gather_forward(*ffn_sems[:3], after, "ffn_weights_forward_start")
        started["forward"] = (forward_sems, lands)
        _, lands = _exchange_wait(*out_sems[:3], token, "out_weights_gather_wait", "gather")
        return [_full_from_gathered(t, axis) for t, axis in zip(lands, _SHARD_AXIS[out])]

    def ffn_weights(after):
        lands = _forward_wait(*started["forward"], after, "ffn_weights_gather_wait")
        return [_full_from_gathered(t, axis) for t, axis in zip(lands, _SHARD_AXIS[ffn])]

    def hold_ffn_grads(*whole_grads):
        started["held"] = whole_grads
        return no_order

    def scatter_grads(*out_grads):
        chunks = [
            t if axis == 1 else t.reshape(N_DEV, -1, t.shape[1])
            for t, axis in zip((*out_grads, *started["held"]), _SHARD_AXIS[scattered])
        ]
        started["scatter"] = _exchange_start(chunks, no_order, "grads_scatter_start", "scatter")
        return started["scatter"][3]

    def gather_small(small, loss_rows):
        started["small"] = _exchange_start([_pack_small(small, loss_rows)], no_order, "small_grads_gather_start", "gather")
        return started["small"][3]

    def scatter_w_in(chunks_in, tails_in):
        started["in"] = _exchange_start([chunks_in, tails_in], no_order, "w_in_grads_scatter_start", "scatter")
        return started["in"][3]

    def gather_norm1(dg1):
        rows = jnp.reshape(dg1, (8, LANES))
        started["norm1"] = _exchange_start([rows], no_order, "norm1_grad_gather_start", "gather")

    ffn, out, scattered = slice(4, 7), slice(1, 4), slice(1, 7)
    grad_x = _local_grads(
        x, loss_target, norm1_g, norm2_g, norm_f_g, b_forget, pool_mix, pool_scale, _w_in_pieces(gathered_in, tails_in), ffn_sems[3],
        out_weights, ffn_weights, hold_ffn_grads, scatter_grads, gather_small, scatter_w_in, gather_norm1,
    )
    me_index = jnp.reshape(me, (1,)).astype(jnp.int32)

    srcs, lands = _exchange_wait(*started["scatter"][:3], started["norm1"][3], "grads_scatter_wait", "scatter")
    updates = [
        _shard_update_direct(p, s, w, m, v, me_index, "update_" + n)
        for p, s, w, m, v, n in zip(lands, srcs, w_sh[scattered], m_sh[scattered], v_sh[scattered], names[scattered])
    ]
    updates_out, updates_ffn = updates[:3], updates[3:]

    small_w = (norm1_g, norm2_g, norm_f_g, b_forget, pool_scale, pool_mix)
    small_m = (m_norm1_g, m_norm2_g, m_norm_f_g, m_b_forget, m_pool_scale, m_pool_mix)
    small_v = (v_norm1_g, v_norm2_g, v_norm_f_g, v_b_forget, v_pool_scale, v_pool_mix)
    (sent_in, sent_tails), (parts_in, parts_tails) = _exchange_wait(*started["in"][:3], updates_ffn[-1][0], "w_in_grads_scatter_wait", "scatter")
    update_in = _w_in_update(parts_in, sent_in, parts_tails, sent_tails, w_in, m_w_in, v_w_in, me_index, "update_w_in")

    def gathered_small(key, after, name):
        _, lands = _exchange_wait(*started[key][:3], after, name, "gather")
        return lands[0]

    parts = gathered_small("small", update_in[0], "small_grads_gather_wait")
    first_rows = gathered_small("norm1", parts, "norm1_grad_gather_wait")
    (g_s, d_s, nm_s, nv_s), loss = _small_update(parts, first_rows, small_w, small_m, small_v)
    g_w, d_w, nm_w, nv_w = zip(*(
        [stored(t, tr) for t in u] for u, tr in zip([update_in] + updates_out + updates_ffn, _TRANSPOSED)
    ))
    loss = loss.reshape(())
    (g1, g2, gf, gb, gsc, gmix), (d1, d2, df, db_, dsc, dmx) = g_s, d_s
    (m1, m2, mf, mb, msc, mmx), (v1, v2, vf, vb, vsc, vmx) = nm_s, nv_s

    def ordered(n1, win, b, mix, sc, wpo, wao, wout, n2, wg, wu, wd, nf):
        return (n1, win, b, mix, sc, wpo, wao, wout, n2, wg, wu, wd, nf)

    grads = ordered(g1, g_w[0], gb, gmix, gsc, g_w[1], g_w[2], g_w[3], g2, g_w[4], g_w[5], g_w[6], gf)
    deltas = ordered(d1, d_w[0], db_, dmx, dsc, d_w[1], d_w[2], d_w[3], d2, d_w[4], d_w[5], d_w[6], df)
    new_m = ordered(m1, nm_w[0], mb, mmx, msc, nm_w[1], nm_w[2], nm_w[3], m2, nm_w[4], nm_w[5], nm_w[6], mf)
    new_v = ordered(v1, nv_w[0], vb, vmx, vsc, nv_w[1], nv_w[2], nv_w[3], v2, nv_w[4], nv_w[5], nv_w[6], vf)
    return (loss, grad_x, *grads, *deltas, *new_m, *new_v)
```

```python
import jax
import jax.numpy as jnp
from jax import lax
from jax.experimental import pallas as pl
from jax.experimental.pallas import tpu as pltpu

F32 = jnp.float32
BF16 = jnp.bfloat16
MESH = pl.DeviceIdType.MESH

D_MODEL = 1024
POOL_WINDOWS = (2, 4, 8, 16)
POOL_WIDTH = 512
GROUP_DIM = 128
ATTN_WIDTH = 512
HEAD_DIM = 64
N_HEADS = 8
N_PAIRS = 4
D_FF = 2816
RMS_EPS = 1e-6
N_DEV = 8
LANES = 128
FL_PAD = 128

ADAM_LR = 0.001
ADAM_B1 = 0.9
ADAM_B2 = 0.999
ADAM_EPS = 1e-08
ADAM_WD = 0.01
ADAM_STEP = 10

VMEM_LIMIT = 56 * 1024 * 1024
VMEM_LIMIT_MAX = 60 * 1024 * 1024
ROW_TILE = 512
ATTN_BLOCK = 512
FF_CHUNK = 256
FF_ROW_TILE = 512
DW_TOKENS = 2048


def _mm(a, b):
    return jnp.dot(a, b, preferred_element_type=F32)


def _mm_nt(a, b):
    return lax.dot_general(a, b, (((1,), (1,)), ((), ())), preferred_element_type=F32)


def _mm_tn(a, b):
    return lax.dot_general(a, b, (((0,), (0,)), ((), ())), preferred_element_type=F32)


def _whole_cols(w_ref):
    if len(w_ref.shape) == 2:
        return w_ref[...]
    return jnp.concatenate([w_ref[d] for d in range(w_ref.shape[0])], axis=1)


def _sigmoid(x):
    return 1.0 / (1.0 + jnp.exp(-x))


def _params(sem, vmem=VMEM_LIMIT):
    return pltpu.CompilerParams(dimension_semantics=sem, vmem_limit_bytes=vmem)


def _const_spec(shape):
    nd = len(shape)
    return pl.BlockSpec(shape, lambda *_: (0,) * nd, pipeline_mode=pl.Buffered(1))


def _rms_fwd(x, g):
    r = lax.rsqrt(jnp.mean(x * x, axis=-1, keepdims=True) + RMS_EPS)
    xh = x * r
    return xh * g, xh, r


def _rms_bwd(dy, xh, r, g):
    dxh = dy * g
    dx = r * (dxh - xh * jnp.mean(dxh * xh, axis=-1, keepdims=True))
    return dx, dy * xh


def _in_proj(x, g1, w_uqkv, w_fl, w_g, token):
    T = x.shape[0]
    tm = ROW_TILE

    def body(x_ref, g_ref, wa_ref, wf_ref, wg_ref, token_ref, h_ref, u_ref, qkv_ref, fl_ref, gt_ref):
        h, _, _ = _rms_fwd(x_ref[...], g_ref[...])
        hb = h.astype(BF16)
        h_ref[...] = hb
        z = _mm(hb, wa_ref[...])
        u_ref[...] = z[:, :POOL_WIDTH]
        qkv_ref[...] = z[:, POOL_WIDTH:].astype(BF16)
        fl_ref[...] = _mm(hb, wf_ref[...])
        gt_ref[...] = _mm(hb, wg_ref[...]).astype(BF16)

    row = lambda n: pl.BlockSpec((tm, n), lambda i: (i, 0))
    return pl.pallas_call(
        body,
        name="in_proj",
        grid=(T // tm,),
        in_specs=[row(D_MODEL), _const_spec((1, D_MODEL)), _const_spec(w_uqkv.shape), _const_spec(w_fl.shape), _const_spec(w_g.shape), _HBM],
        out_specs=[row(D_MODEL), row(POOL_WIDTH), row(3 * ATTN_WIDTH), row(FL_PAD), row(2 * D_MODEL)],
        out_shape=[
            jax.ShapeDtypeStruct((T, D_MODEL), BF16),
            jax.ShapeDtypeStruct((T, POOL_WIDTH), F32),
            jax.ShapeDtypeStruct((T, 3 * ATTN_WIDTH), BF16),
            jax.ShapeDtypeStruct((T, FL_PAD), F32),
            jax.ShapeDtypeStruct((T, 2 * D_MODEL), BF16),
        ],
        compiler_params=_params(("parallel",)),
    )(x, g1, w_uqkv, w_fl, w_g, token)


def _log_sigmoid(x):
    return jnp.minimum(x, 0.0) - jnp.log(1.0 + jnp.exp(-jnp.abs(x)))


def _forget_fwd(fl, b_pad, n_seq, S):
    def body(fl_ref, b_ref, fcol_ref):
        lf = _log_sigmoid(fl_ref[...] + b_ref[...])
        t = lf.T
        lane = lax.broadcasted_iota(jnp.int32, t.shape, 1)
        k = 1
        while k < S:
            t = t + jnp.where(lane >= k, pltpu.roll(t, k, 1), 0.0)
            k *= 2
        fcol_ref[...] = t.T

    return pl.pallas_call(
        body,
        name="forget_fwd",
        grid=(n_seq,),
        in_specs=[pl.BlockSpec((S, FL_PAD), lambda s: (s, 0)), _const_spec((1, FL_PAD))],
        out_specs=pl.BlockSpec((S, FL_PAD), lambda s: (s, 0)),
        out_shape=jax.ShapeDtypeStruct((n_seq * S, FL_PAD), F32),
        compiler_params=_params(("parallel",)),
    )(fl, b_pad)


def _window_pick(g, v2, v4, v8, v16):
    return jnp.where(g == 0, v2, jnp.where(g == 1, v4, jnp.where(g == 2, v8, v16)))


def _pool_fwd(u, mix_b, scale, n_seq, S):
    T = n_seq * S

    def body(u_ref, mix_ref, sc_ref, pm_ref, p2_ref, p3_ref):
        g = pl.program_id(1)
        uu = u_ref[...]
        row = lax.broadcasted_iota(jnp.int32, uu.shape, 0)

        def back(a, k):
            return jnp.where(row >= k, pltpu.roll(a, k, 0), 0.0)

        s2 = uu + back(uu, 1)
        s4 = s2 + back(s2, 2)
        s8 = s4 + back(s4, 4)
        s16 = s8 + back(s8, 8)
        w = _window_pick(g, 2.0, 4.0, 8.0, 16.0)
        cnt = jnp.minimum((row + 1).astype(F32), w)
        pm = _window_pick(g, s2, s4, s8, s16) / cnt - uu
        pmb = pm.astype(BF16)
        pm_ref[...] = pmb
        p2 = _mm(pmb, mix_ref[...])
        p2_ref[...] = p2
        p3_ref[...] = (p2 * sc_ref[...]).astype(BF16)

    grp = pl.BlockSpec((S, GROUP_DIM), lambda s, g: (s, g))
    return pl.pallas_call(
        body,
        name="pool_fwd",
        grid=(n_seq, len(POOL_WINDOWS)),
        in_specs=[
            grp,
            pl.BlockSpec((None, GROUP_DIM, GROUP_DIM), lambda s, g: (g, 0, 0)),
            pl.BlockSpec((1, GROUP_DIM), lambda s, g: (0, g)),
        ],
        out_specs=[grp, grp, grp],
        out_shape=[
            jax.ShapeDtypeStruct((T, POOL_WIDTH), BF16),
            jax.ShapeDtypeStruct((T, POOL_WIDTH), F32),
            jax.ShapeDtypeStruct((T, POOL_WIDTH), BF16),
        ],
        compiler_params=_params(("parallel", "parallel")),
    )(u, mix_b, scale)


def _split3(v):
    hi = v.astype(BF16).astype(F32)
    r = v - hi
    mid = r.astype(BF16).astype(F32)
    lo = (r - mid).astype(BF16).astype(F32)
    return hi, mid, lo


def _bias_lanes(v):
    hi, mid, lo = _split3(v)
    lane = lax.broadcasted_iota(jnp.int32, (1, LANES), 1)
    packed = jnp.where(lane < N_HEADS, hi, jnp.where(lane < 2 * N_HEADS, pltpu.roll(mid, N_HEADS, 1), pltpu.roll(lo, 2 * N_HEADS, 1)))
    return jnp.where(lane < 3 * N_HEADS, packed, 0.0).astype(BF16)


def _bias_placement(slot):
    row = lax.broadcasted_iota(jnp.int32, (LANES, N_HEADS * LANES), 0)
    col = lax.broadcasted_iota(jnp.int32, (LANES, N_HEADS * LANES), 1)
    h = col // LANES
    n = col % LANES - jnp.where(h % 2 == 0, HEAD_DIM, 0) - 3 * slot
    return ((n >= 0) & (n < 3) & (row == N_HEADS * n + h)).astype(BF16)


def _augment(xp, h, bias, ones_slot):
    lane = lax.broadcasted_iota(jnp.int32, (1, LANES), 1)
    hh = h % 2
    head = (lane >= HEAD_DIM * hh) & (lane < HEAD_DIM * (hh + 1))
    b = HEAD_DIM * (1 - hh)
    rest = jnp.zeros_like(xp) if bias is None else bias[:, h * LANES : (h + 1) * LANES]
    out = jnp.where(head, xp, rest)
    if ones_slot is not None:
        out = jnp.where((lane >= b + 3 * ones_slot) & (lane < b + 3 * ones_slot + 3), jnp.ones_like(xp), out)
    return out


def _attn_fwd(qkv, fcol, n_seq, S):
    T = n_seq * S
    tb = ATTN_BLOCK
    nq = S // tb
    scale = HEAD_DIM ** -0.5

    def body(q_ref, k_ref, v_ref, fc_ref, o_ref, st_ref, qa_sc, ka_sc, m_sc, l_sc, acc_sc):
        i = pl.program_id(1)
        lane = lax.broadcasted_iota(jnp.int32, (1, LANES), 1)
        low = lane < HEAD_DIM

        @pl.when(i == 0)
        def _():
            place = _bias_placement(1)

            def rows_ka(r, carry):
                r0 = pl.multiple_of(r * tb, tb)
                bias = _mm(_bias_lanes(-fc_ref[pl.ds(r0, tb), :]), place).astype(BF16)
                for h in range(N_HEADS):
                    kp = k_ref[pl.ds(r0, tb), (h // 2) * LANES : (h // 2 + 1) * LANES] * scale
                    ka_sc[h, pl.ds(r0, tb), :] = _augment(kp, h, bias, 0)
                return carry

            lax.fori_loop(0, nq, rows_ka, 0)

        q0 = pl.multiple_of(i * tb, tb)
        bias = _mm(_bias_lanes(fc_ref[pl.ds(q0, tb), :]), _bias_placement(0)).astype(BF16)
        for h in range(N_HEADS):
            qa_sc[h] = _augment(q_ref[:, (h // 2) * LANES : (h // 2 + 1) * LANES], h, bias, 1)
        m_sc[...] = jnp.full(m_sc.shape, -jnp.inf, F32)
        l_sc[...] = jnp.zeros_like(l_sc)
        acc_sc[...] = jnp.zeros_like(acc_sc)
        causal = lax.broadcasted_iota(jnp.int32, (tb, tb), 1) <= lax.broadcasted_iota(jnp.int32, (tb, tb), 0)

        def step(j, masked):
            c0 = pl.multiple_of(j * tb, tb)
            for p in range(N_PAIRS):
                vb = v_ref[pl.ds(c0, tb), p * LANES : (p + 1) * LANES]
                pv, al = [], []
                for hh in range(2):
                    h = 2 * p + hh
                    s = _mm_nt(qa_sc[h], ka_sc[h, pl.ds(c0, tb), :])
                    if masked:
                        s = jnp.where(causal, s, -jnp.inf)
                    m_old = m_sc[h]
                    m_new = jnp.maximum(m_old, jnp.max(s, axis=1, keepdims=True))
                    alpha = jnp.exp(m_old - m_new)
                    pe = jnp.exp(s - jnp.concatenate([m_new] * (tb // LANES), axis=1))
                    l_sc[h] = alpha * l_sc[h] + jnp.sum(pe, axis=1, keepdims=True)
                    m_sc[h] = m_new
                    pv.append(_mm(pe.astype(BF16), vb))
                    al.append(alpha)
                acc_sc[p] = jnp.where(low, al[0], al[1]) * acc_sc[p] + jnp.where(low, pv[0], pv[1])

        def loop_body(j, carry):
            step(j, False)
            return carry

        lax.fori_loop(0, i, loop_body, 0)
        step(i, True)
        st = jnp.zeros((tb, LANES), F32)
        for p in range(N_PAIRS):
            lp = jnp.where(low, l_sc[2 * p], l_sc[2 * p + 1])
            o_ref[:, p * LANES : (p + 1) * LANES] = (acc_sc[p] / lp).astype(BF16)
            for h in (2 * p, 2 * p + 1):
                st = jnp.where(lane == h, m_sc[h] + jnp.log(l_sc[h]), st)
        st_ref[...] = st

    return pl.pallas_call(
        body,
        name="attn_fwd",
        grid=(n_seq, nq),
        in_specs=[
            pl.BlockSpec((tb, ATTN_WIDTH), lambda s, i: (s * nq + i, 0)),
            pl.BlockSpec((S, ATTN_WIDTH), lambda s, i: (s, 1)),
            pl.BlockSpec((S, ATTN_WIDTH), lambda s, i: (s, 2)),
            pl.BlockSpec((S, LANES), lambda s, i: (s, 0)),
        ],
        out_specs=[
            pl.BlockSpec((tb, ATTN_WIDTH), lambda s, i: (s * nq + i, 0)),
            pl.BlockSpec((tb, LANES), lambda s, i: (s * nq + i, 0)),
        ],
        out_shape=[jax.ShapeDtypeStruct((T, ATTN_WIDTH), BF16), jax.ShapeDtypeStruct((T, LANES), F32)],
        scratch_shapes=[
            pltpu.VMEM((N_HEADS, tb, LANES), BF16),
            pltpu.VMEM((N_HEADS, S, LANES), BF16),
            pltpu.VMEM((N_HEADS, tb, LANES), F32),
            pltpu.VMEM((N_HEADS, tb, LANES), F32),
            pltpu.VMEM((N_PAIRS, tb, LANES), F32),
        ],
        compiler_params=_params(("parallel", "arbitrary")),
    )(qkv, qkv, qkv, fcol)


def _mix_out(a, p3, gates, x, w_ao, w_po, w_out):
    T = x.shape[0]
    tm = ROW_TILE

    def body(a_ref, p3_ref, gt_ref, x_ref, wao_ref, wpo_ref, wout_ref, mg_ref, x1_ref, ay_ref, py_ref):
        ay = _mm(a_ref[...], _whole_cols(wao_ref))
        py = _mm(p3_ref[...], _whole_cols(wpo_ref))
        ay_ref[...] = ay.astype(BF16)
        py_ref[...] = py.astype(BF16)
        sp = _sigmoid(gt_ref[:, :D_MODEL].astype(F32))
        sa = _sigmoid(gt_ref[:, D_MODEL:].astype(F32))
        mb = (sp * py + sa * ay).astype(BF16)
        mg_ref[...] = mb
        x1_ref[...] = x_ref[...] + _mm(mb, wout_ref[...])

    row = lambda n: pl.BlockSpec((tm, n), lambda i: (i, 0))
    return pl.pallas_call(
        body,
        name="mix_out",
        grid=(T // tm,),
        in_specs=[
            row(ATTN_WIDTH), row(POOL_WIDTH), row(2 * D_MODEL), row(D_MODEL),
            _const_spec(w_ao.shape), _const_spec(w_po.shape), _const_spec(w_out.shape),
        ],
        out_specs=[row(D_MODEL), row(D_MODEL), row(D_MODEL), row(D_MODEL)],
        out_shape=[
            jax.ShapeDtypeStruct((T, D_MODEL), BF16), jax.ShapeDtypeStruct((T, D_MODEL), F32),
            jax.ShapeDtypeStruct((T, D_MODEL), BF16), jax.ShapeDtypeStruct((T, D_MODEL), BF16),
        ],
        compiler_params=_params(("parallel",)),
    )(a, p3, gates, x, w_ao, w_po, w_out)


def _ffn_fwd(x1, g2, gf, tgt, w_gate_t, w_up_t, w_down):
    T = x1.shape[0]
    tm = min(T, FF_ROW_TILE)
    nt = T // tm
    nc = D_FF // FF_CHUNK

    def body(x1_ref, g2_ref, gf_ref, tg_ref, wg_ref, wu_ref, wd_ref, h2_ref, gate_ref, up_ref, act_ref, dx2_ref, loss_ref, dgf_ref):
        x1v = x1_ref[...]
        h2, _, _ = _rms_fwd(x1v, g2_ref[...])
        h2b = h2.astype(BF16)
        h2_ref[...] = h2b
        for c in range(nc):
            sl = slice(c * FF_CHUNK, (c + 1) * FF_CHUNK)
            gate = _mm_nt(h2b, wg_ref[sl, :])
            up = _mm_nt(h2b, wu_ref[sl, :])
            gate_ref[:, sl] = gate.astype(BF16)
            up_ref[:, sl] = up.astype(BF16)
            act_ref[:, sl] = (gate * _sigmoid(gate) * up).astype(BF16)
        acc = x1v + _mm(act_ref[...], wd_ref[...])
        gfv = gf_ref[...]
        y, xh, r = _rms_fwd(acc, gfv)
        err = y - tg_ref[...]
        part = 0.5 * jnp.sum(jnp.mean(err * err, axis=-1, keepdims=True), axis=0, keepdims=True)
        dx2, dgrow = _rms_bwd(err * (1.0 / D_MODEL), xh, r, gfv)
        dx2_ref[...] = dx2

        @pl.when(pl.program_id(0) == 0)
        def _():
            dgf_ref[...] = jnp.zeros_like(dgf_ref)
            loss_ref[...] = jnp.zeros_like(loss_ref)

        dgf_ref[...] += jnp.sum(dgrow, axis=0, keepdims=True)
        loss_ref[...] += jnp.broadcast_to(part, loss_ref.shape)

    row = lambda n: pl.BlockSpec((tm, n), lambda i: (i, 0))
    return pl.pallas_call(
        body,
        name="ffn_fwd",
        grid=(nt,),
        in_specs=[
            row(D_MODEL), _const_spec((1, D_MODEL)), _const_spec((1, D_MODEL)), row(D_MODEL),
            _const_spec(w_gate_t.shape), _const_spec(w_up_t.shape), _const_spec(w_down.shape),
        ],
        out_specs=[
            row(D_MODEL), row(D_FF), row(D_FF), row(D_FF), row(D_MODEL),
            pl.BlockSpec((8, LANES), lambda i: (0, 0)),
            pl.BlockSpec((1, D_MODEL), lambda i: (0, 0)),
        ],
        out_shape=[
            jax.ShapeDtypeStruct((T, D_MODEL), BF16),
            jax.ShapeDtypeStruct((T, D_FF), BF16),
            jax.ShapeDtypeStruct((T, D_FF), BF16),
            jax.ShapeDtypeStruct((T, D_FF), BF16),
            jax.ShapeDtypeStruct((T, D_MODEL), F32),
            jax.ShapeDtypeStruct((8, LANES), F32),
            jax.ShapeDtypeStruct((1, D_MODEL), F32),
        ],
        compiler_params=_params(("arbitrary",)),
    )(x1, g2, gf, tgt, w_gate_t, w_up_t, w_down)


def _ffn_bwd(dx2, gate, up, x1, g2, w_gate_t, w_up_t, w_down):
    T = x1.shape[0]
    tm = min(T, FF_ROW_TILE)
    nc = D_FF // FF_CHUNK

    def body(dx2_ref, gate_ref, up_ref, x1_ref, g2_ref, wg_ref, wu_ref, wd_ref, dgate_ref, dup_ref, dx1_ref, dg2_ref):
        dx2v = dx2_ref[...]
        dx2b = dx2v.astype(BF16)
        for c in range(nc):
            sl = slice(c * FF_CHUNK, (c + 1) * FF_CHUNK)
            dact = _mm_nt(dx2b, wd_ref[sl, :])
            gate = gate_ref[:, sl].astype(F32)
            sg = _sigmoid(gate)
            silu = gate * sg
            dgate = (dact * up_ref[:, sl].astype(F32) * (sg * (1.0 + gate * (1.0 - sg)))).astype(BF16)
            dup = (dact * silu).astype(BF16)
            dgate_ref[:, sl] = dgate
            dup_ref[:, sl] = dup
        dh2 = _mm(dgate_ref[...], wg_ref[...]) + _mm(dup_ref[...], wu_ref[...])
        g2v = g2_ref[...]
        _, xh, r = _rms_fwd(x1_ref[...], g2v)
        dxn, dgrow = _rms_bwd(dh2, xh, r, g2v)
        dx1_ref[...] = dx2v + dxn

        @pl.when(pl.program_id(0) == 0)
        def _():
            dg2_ref[...] = jnp.zeros_like(dg2_ref)

        dg2_ref[...] += jnp.sum(dgrow, axis=0, keepdims=True)

    row = lambda n: pl.BlockSpec((tm, n), lambda i: (i, 0))
    return pl.pallas_call(
        body,
        name="ffn_bwd",
        grid=(T // tm,),
        in_specs=[
            row(D_MODEL), row(D_FF), row(D_FF), row(D_MODEL), _const_spec((1, D_MODEL)),
            _const_spec(w_gate_t.shape), _const_spec(w_up_t.shape), _const_spec(w_down.shape),
        ],
        out_specs=[row(D_FF), row(D_FF), row(D_MODEL), pl.BlockSpec((1, D_MODEL), lambda i: (0, 0))],
        out_shape=[
            jax.ShapeDtypeStruct((T, D_FF), BF16),
            jax.ShapeDtypeStruct((T, D_FF), BF16),
            jax.ShapeDtypeStruct((T, D_MODEL), F32),
            jax.ShapeDtypeStruct((1, D_MODEL), F32),
        ],
        compiler_params=_params(("arbitrary",), VMEM_LIMIT_MAX),
    )(dx2, gate, up, x1, g2, w_gate_t, w_up_t, w_down)


def _mix_bwd(dx1, gates, pool_y, attn_y, p2, scale, w_out, w_ao, w_po, token):
    T = dx1.shape[0]
    tm = ROW_TILE

    def body(dx1_ref, gt_ref, py_ref, ay_ref, p2_ref, sc_ref, wout_ref, wao_ref, wpo_ref, token_ref, dgt_ref, dpy_ref, day_ref, da_ref, dp2_ref, dsc_ref):
        dm = _mm_nt(dx1_ref[...].astype(BF16), wout_ref[...])
        sp = _sigmoid(gt_ref[:, :D_MODEL].astype(F32))
        sa = _sigmoid(gt_ref[:, D_MODEL:].astype(F32))
        dgt_ref[:, :D_MODEL] = (dm * py_ref[...].astype(F32) * (sp * (1.0 - sp))).astype(BF16)
        dgt_ref[:, D_MODEL:] = (dm * ay_ref[...].astype(F32) * (sa * (1.0 - sa))).astype(BF16)
        dpy = (dm * sp).astype(BF16)
        day = (dm * sa).astype(BF16)
        dpy_ref[...] = dpy
        day_ref[...] = day
        da_ref[...] = _mm_nt(day, _whole_cols(wao_ref)).astype(BF16)
        dp3 = _mm_nt(dpy, _whole_cols(wpo_ref))
        dp2_ref[...] = (dp3 * sc_ref[...]).astype(BF16)

        @pl.when(pl.program_id(0) == 0)
        def _():
            dsc_ref[...] = jnp.zeros_like(dsc_ref)

        dsc_ref[...] += jnp.sum(dp3 * p2_ref[...], axis=0, keepdims=True)

    row = lambda n: pl.BlockSpec((tm, n), lambda i: (i, 0))
    return pl.pallas_call(
        body,
        name="mix_bwd",
        grid=(T // tm,),
        in_specs=[
            row(D_MODEL), row(2 * D_MODEL), row(D_MODEL), row(D_MODEL), row(POOL_WIDTH), _const_spec((1, POOL_WIDTH)),
            _const_spec(w_out.shape), _const_spec(w_ao.shape), _const_spec(w_po.shape), _HBM,
        ],
        out_specs=[row(2 * D_MODEL), row(D_MODEL), row(D_MODEL), row(ATTN_WIDTH), row(POOL_WIDTH), pl.BlockSpec((1, POOL_WIDTH), lambda i: (0, 0))],
        out_shape=[
            jax.ShapeDtypeStruct((T, 2 * D_MODEL), BF16),
            jax.ShapeDtypeStruct((T, D_MODEL), BF16),
            jax.ShapeDtypeStruct((T, D_MODEL), BF16),
            jax.ShapeDtypeStruct((T, ATTN_WIDTH), BF16),
            jax.ShapeDtypeStruct((T, POOL_WIDTH), BF16),
            jax.ShapeDtypeStruct((1, POOL_WIDTH), F32),
        ],
        compiler_params=_params(("arbitrary",)),
    )(dx1, gates, pool_y, attn_y, p2, scale, w_out, w_ao, w_po, token)


def _pool_bwd(dp2, pm, mix_b, token, n_seq, S):
    T = n_seq * S

    def body(dp2_ref, pm_ref, mix_ref, token_ref, du_ref, dmix_ref):
        g = pl.program_id(0)
        dp2v = dp2_ref[...]
        dpm = _mm_nt(dp2v, mix_ref[...])
        row = lax.broadcasted_iota(jnp.int32, dpm.shape, 0)
        w = _window_pick(g, 2.0, 4.0, 8.0, 16.0)
        e = dpm / jnp.minimum((row + 1).astype(F32), w)

        def ahead(a, k):
            return jnp.where(row < S - k, pltpu.roll(a, S - k, 0), 0.0)

        r2 = e + ahead(e, 1)
        r4 = r2 + ahead(r2, 2)
        r8 = r4 + ahead(r4, 4)
        r16 = r8 + ahead(r8, 8)
        du_ref[...] = (_window_pick(g, r2, r4, r8, r16) - dpm).astype(BF16)

        @pl.when(pl.program_id(1) == 0)
        def _():
            dmix_ref[...] = jnp.zeros_like(dmix_ref)

        dmix_ref[...] += _mm_tn(pm_ref[...], dp2v)

    grp = pl.BlockSpec((S, GROUP_DIM), lambda g, s: (s, g))
    mixs = pl.BlockSpec((None, GROUP_DIM, GROUP_DIM), lambda g, s: (g, 0, 0))
    return pl.pallas_call(
        body,
        name="pool_bwd",
        grid=(len(POOL_WINDOWS), n_seq),
        in_specs=[grp, grp, mixs, _HBM],
        out_specs=[grp, mixs],
        out_shape=[jax.ShapeDtypeStruct((T, POOL_WIDTH), BF16), jax.ShapeDtypeStruct((len(POOL_WINDOWS), GROUP_DIM, GROUP_DIM), F32)],
        compiler_params=_params(("parallel", "arbitrary")),
    )(dp2, pm, mix_b, token)


def _attn_bwd(qkv, da, a, fcol, lse, n_seq, S):
    T = n_seq * S
    tb = ATTN_BLOCK
    nb = S // tb
    scale = HEAD_DIM ** -0.5

    def body(q_ref, k_ref, v_ref, do_ref, o_ref, fc_ref, st_ref, dq_ref, dk_ref, dv_ref, dfk_ref, dfq_ref,
             qa_sc, doa_sc, qat_sc, doat_sc, dq_acc, ka_sc, va_sc, dkt_sc, dvt_sc):
        j = pl.program_id(1)
        lane = lax.broadcasted_iota(jnp.int32, (1, LANES), 1)
        low = lane < HEAD_DIM

        @pl.when(j == 0)
        def _():
            dq_acc[...] = jnp.zeros_like(dq_acc)
            place = _bias_placement(0)

            def rows_q(i, carry):
                r0 = pl.multiple_of(i * tb, tb)
                delta = jnp.zeros((tb, LANES), F32)
                for h in range(N_HEADS):
                    pair = slice((h // 2) * LANES, (h // 2 + 1) * LANES)
                    prod = do_ref[pl.ds(r0, tb), pair].astype(F32) * o_ref[pl.ds(r0, tb), pair].astype(F32)
                    head = (lane >= HEAD_DIM * (h % 2)) & (lane < HEAD_DIM * (h % 2 + 1))
                    delta = jnp.where(lane == h, jnp.sum(jnp.where(head, prod, 0.0), axis=1, keepdims=True), delta)
                cq = fc_ref[pl.ds(r0, tb), :] - st_ref[pl.ds(r0, tb), :]
                q_bias = _mm(_bias_lanes(cq), place).astype(BF16)
                do_bias = _mm(_bias_lanes(-delta), place).astype(BF16)
                for h in range(N_HEADS):
                    pair = slice((h // 2) * LANES, (h // 2 + 1) * LANES)
                    qa = _augment(q_ref[pl.ds(r0, tb), pair], h, q_bias, 1)
                    doa = _augment(do_ref[pl.ds(r0, tb), pair], h, do_bias, None)
                    qa_sc[h, pl.ds(r0, tb), :] = qa
                    doa_sc[h, pl.ds(r0, tb), :] = doa
                    qat_sc[h, i] = qa.astype(F32).T.astype(BF16)
                    doat_sc[h, i] = doa.astype(F32).T.astype(BF16)
                return carry

            lax.fori_loop(0, nb, rows_q, 0)

        c0 = pl.multiple_of(j * tb, tb)
        k_bias = _mm(_bias_lanes(-fc_ref[pl.ds(c0, tb), :]), _bias_placement(1)).astype(BF16)
        for h in range(N_HEADS):
            pair = slice((h // 2) * LANES, (h // 2 + 1) * LANES)
            ka_sc[h] = _augment(k_ref[:, pair] * scale, h, k_bias, 0)
            va_sc[h] = _augment(v_ref[:, pair], h, None, 0)
        dkt_sc[...] = jnp.zeros_like(dkt_sc)
        dvt_sc[...] = jnp.zeros_like(dvt_sc)
        causal = lax.broadcasted_iota(jnp.int32, (tb, tb), 1) <= lax.broadcasted_iota(jnp.int32, (tb, tb), 0)

        def step(i, masked):
            r0 = pl.multiple_of(i * tb, tb)
            for h in range(N_HEADS):
                s = _mm_nt(qa_sc[h, pl.ds(r0, tb), :], ka_sc[h])
                if masked:
                    s = jnp.where(causal, s, -jnp.inf)
                pr = jnp.exp(s)
                dvt_sc[h] += _mm(doat_sc[h, i], pr.astype(BF16))
                dsb = (pr * _mm_nt(doa_sc[h, pl.ds(r0, tb), :], va_sc[h])).astype(BF16)
                dkt_sc[h] += _mm(qat_sc[h, i], dsb)
                dq_acc[h, pl.ds(r0, tb), :] += _mm(dsb, ka_sc[h])

        step(j, True)

        def loop_body(i, carry):
            step(i, False)
            return carry

        lax.fori_loop(j + 1, nb, loop_body, 0)
        dfk = jnp.zeros((tb, LANES), F32)
        for p in range(N_PAIRS):
            dk = [dkt_sc[2 * p + hh].T for hh in range(2)]
            dv = [dvt_sc[2 * p + hh].T for hh in range(2)]
            dk_ref[:, p * LANES : (p + 1) * LANES] = (jnp.where(low, dk[0], dk[1]) * scale).astype(BF16)
            dv_ref[:, p * LANES : (p + 1) * LANES] = jnp.where(low, dv[0], dv[1]).astype(BF16)
            for hh in range(2):
                b = HEAD_DIM * (1 - hh) + 3
                dfk = jnp.where(lane == 2 * p + hh, -dk[hh][:, b : b + 1], dfk)
        dfk_ref[...] = dfk

        @pl.when(j == nb - 1)
        def _():
            def rows_dq(i, carry):
                r0 = pl.multiple_of(i * tb, tb)
                dfq = jnp.zeros((tb, LANES), F32)
                for p in range(N_PAIRS):
                    parts = [dq_acc[2 * p + hh, pl.ds(r0, tb), :] for hh in range(2)]
                    dq_ref[pl.ds(r0, tb), p * LANES : (p + 1) * LANES] = jnp.where(low, parts[0], parts[1]).astype(BF16)
                    for hh in range(2):
                        b = HEAD_DIM * (1 - hh)
                        dfq = jnp.where(lane == 2 * p + hh, parts[hh][:, b : b + 1], dfq)
                dfq_ref[pl.ds(r0, tb), :] = dfq
                return carry

            lax.fori_loop(0, nb, rows_dq, 0)

    seq = lambda w, col: pl.BlockSpec((S, w), lambda s, j: (s, col))
    seq_in = lambda w, col: pl.BlockSpec((S, w), lambda s, j: (s, col), pipeline_mode=pl.Buffered(1))
    blk = lambda w, col: pl.BlockSpec((tb, w), lambda s, j: (s * nb + j, col))
    return pl.pallas_call(
        body,
        name="attn_bwd",
        grid=(n_seq, nb),
        in_specs=[seq_in(ATTN_WIDTH, 0), blk(ATTN_WIDTH, 1), blk(ATTN_WIDTH, 2), seq_in(ATTN_WIDTH, 0), seq_in(ATTN_WIDTH, 0), seq_in(LANES, 0), seq_in(LANES, 0)],
        out_specs=[seq(ATTN_WIDTH, 0), blk(ATTN_WIDTH, 0), blk(ATTN_WIDTH, 0), blk(LANES, 0), seq(LANES, 0)],
        out_shape=[
            jax.ShapeDtypeStruct((T, ATTN_WIDTH), BF16),
            jax.ShapeDtypeStruct((T, ATTN_WIDTH), BF16),
            jax.ShapeDtypeStruct((T, ATTN_WIDTH), BF16),
            jax.ShapeDtypeStruct((T, LANES), F32),
            jax.ShapeDtypeStruct((T, LANES), F32),
        ],
        scratch_shapes=[
            pltpu.VMEM((N_HEADS, S, LANES), BF16),
            pltpu.VMEM((N_HEADS, S, LANES), BF16),
            pltpu.VMEM((N_HEADS, nb, LANES, tb), BF16),
            pltpu.VMEM((N_HEADS, nb, LANES, tb), BF16),
            pltpu.VMEM((N_HEADS, S, LANES), F32),
            pltpu.VMEM((N_HEADS, tb, LANES), BF16),
            pltpu.VMEM((N_HEADS, tb, LANES), BF16),
            pltpu.VMEM((N_HEADS, LANES, tb), F32),
            pltpu.VMEM((N_HEADS, LANES, tb), F32),
        ],
        compiler_params=_params(("parallel", "arbitrary"), VMEM_LIMIT_MAX),
    )(qkv, qkv, qkv, da, a, fcol, lse)


def _forget_bwd(dfk, dfq, fl, b_pad, n_seq, S):
    def body(df_ref, dfq_ref, fl_ref, b_ref, dfl_ref, db_ref):
        t = (df_ref[...] + dfq_ref[...]).T
        lane = lax.broadcasted_iota(jnp.int32, t.shape, 1)
        k = 1
        while k < S:
            t = t + jnp.where(lane < S - k, pltpu.roll(t, S - k, 1), 0.0)
            k *= 2
        dfl = t.T * _sigmoid(-(fl_ref[...] + b_ref[...]))
        dfl_ref[...] = dfl.astype(BF16)

        @pl.when(pl.program_id(0) == 0)
        def _():
            db_ref[...] = jnp.zeros_like(db_ref)

        db_ref[...] += jnp.sum(dfl, axis=0, keepdims=True)

    return pl.pallas_call(
        body,
        name="forget_bwd",
        grid=(n_seq,),
        in_specs=[
            pl.BlockSpec((S, LANES), lambda s: (s, 0)),
            pl.BlockSpec((S, LANES), lambda s: (s, 0)),
            pl.BlockSpec((S, FL_PAD), lambda s: (s, 0)),
            _const_spec((1, FL_PAD)),
        ],
        out_specs=[pl.BlockSpec((S, FL_PAD), lambda s: (s, 0)), pl.BlockSpec((1, FL_PAD), lambda s: (0, 0))],
        out_shape=[jax.ShapeDtypeStruct((n_seq * S, FL_PAD), BF16), jax.ShapeDtypeStruct((1, FL_PAD), F32)],
        compiler_params=_params(("arbitrary",)),
    )(dfk, dfq, fl, b_pad)


def _in_proj_bwd(du, dq, dk, dv, dfl, dgates, x, dx1, g1, w_uqkv, w_fl, w_g, token):
    T = x.shape[0]
    tm = ROW_TILE

    def body(du_ref, dq_ref, dk_ref, dv_ref, dfl_ref, dgt_ref, x_ref, dx1_ref, g_ref, wa_ref, wf_ref, wg_ref, token_ref, dx_ref, dg_ref):
        dz = jnp.concatenate([du_ref[...], dq_ref[...], dk_ref[...], dv_ref[...]], axis=1)
        dh = _mm_nt(dz, wa_ref[...]) + _mm_nt(dgt_ref[...], wg_ref[...]) + _mm_nt(dfl_ref[...], wf_ref[...])
        gv = g_ref[...]
        _, xh, r = _rms_fwd(x_ref[...], gv)
        dxn, dgrow = _rms_bwd(dh, xh, r, gv)
        dx_ref[...] = dx1_ref[...] + dxn

        @pl.when(pl.program_id(0) == 0)
        def _():
            dg_ref[...] = jnp.zeros_like(dg_ref)

        dg_ref[...] += jnp.sum(dgrow, axis=0, keepdims=True)

    row = lambda n: pl.BlockSpec((tm, n), lambda i: (i, 0))
    return pl.pallas_call(
        body,
        name="in_proj_bwd",
        grid=(T // tm,),
        in_specs=[
            row(512), row(512), row(512), row(512), row(FL_PAD), row(2 * D_MODEL), row(D_MODEL), row(D_MODEL), _const_spec((1, D_MODEL)),
            _const_spec(w_uqkv.shape), _const_spec(w_fl.shape), _const_spec(w_g.shape), _HBM,
        ],
        out_specs=[row(D_MODEL), pl.BlockSpec((1, D_MODEL), lambda i: (0, 0))],
        out_shape=[jax.ShapeDtypeStruct((T, D_MODEL), F32), jax.ShapeDtypeStruct((1, D_MODEL), F32)],
        compiler_params=_params(("arbitrary",)),
    )(du, dq, dk, dv, dfl, dgates, x, dx1, g1, w_uqkv, w_fl, w_g, token)


def _pick_block(n):
    for b in (1024, 512, 1408, 256, 128):
        if n % b == 0:
            return b
    raise ValueError(n)


def _matmul_tn(a, b, name, col_chunks=False):
    T, K = a.shape
    N = b.shape[1]
    bt, bk, bn = min(T, DW_TOKENS), _pick_block(K), _pick_block(N)
    nt = T // bt
    c = N // N_DEV
    assert not col_chunks or (bn == N and c % LANES == 0)

    def body(a_ref, b_ref, o_ref, acc):
        @pl.when(pl.program_id(2) == 0)
        def _():
            acc[...] = jnp.zeros_like(acc)

        acc[...] += _mm_tn(a_ref[...].astype(BF16), b_ref[...].astype(BF16))

        @pl.when(pl.program_id(2) == nt - 1)
        def _():
            if col_chunks:
                for d in range(N_DEV):
                    o_ref[d] = acc[:, d * c : (d + 1) * c].astype(BF16)
            else:
                o_ref[...] = acc[...].astype(BF16)

    if col_chunks:
        out_spec, out_shape = pl.BlockSpec((N_DEV, bk, c), lambda k, n, t: (0, k, 0)), (N_DEV, K, c)
    else:
        out_spec, out_shape = pl.BlockSpec((bk, bn), lambda k, n, t: (k, n)), (K, N)
    return pl.pallas_call(
        body,
        name=name,
        grid=(K // bk, N // bn, nt),
        in_specs=[pl.BlockSpec((bt, bk), lambda k, n, t: (t, k)), pl.BlockSpec((bt, bn), lambda k, n, t: (t, n))],
        out_specs=out_spec,
        out_shape=jax.ShapeDtypeStruct(out_shape, BF16),
        scratch_shapes=[pltpu.VMEM((bk, bn), F32)],
        compiler_params=_params(("parallel", "parallel", "arbitrary")),
    )(a, b)


W_IN_A = POOL_WIDTH + 3 * ATTN_WIDTH
W_IN_SHARD = (W_IN_A + N_HEADS + 2 * D_MODEL) // N_DEV
_W_IN_PIECES = ((0, W_IN_A), (W_IN_A, W_IN_A + N_HEADS), (W_IN_A + N_HEADS, W_IN_A + N_HEADS + 2 * D_MODEL))


def _w_in_segments(d):
    lo, hi = d * W_IN_SHARD, (d + 1) * W_IN_SHARD
    out = []
    for p, (a, b) in enumerate(_W_IN_PIECES):
        s, e = max(lo, a), min(hi, b)
        if s < e:
            out.append((p, s - a, s - lo, e - s))
    return out


def _w_in_pieces(gathered, tails):
    tm = ROW_TILE // 2
    tail_rows = tm // LANES
    aligned = W_IN_SHARD - 1

    def body(g_ref, t_ref, wa_ref, wf_ref, wg_ref):
        outs = (wa_ref, wf_ref, wg_ref)
        wf_ref[...] = jnp.zeros_like(wf_ref)
        diagonal = lax.broadcasted_iota(jnp.int32, (LANES, LANES), 0) == lax.broadcasted_iota(jnp.int32, (LANES, LANES), 1)
        for d in range(N_DEV):
            for p, at, frm, n in _w_in_segments(d):
                m = min(n, aligned - frm)
                if m > 0:
                    outs[p][:, at : at + m] = g_ref[d, :, frm : frm + m]
                if frm + n == W_IN_SHARD:
                    column = [
                        jnp.sum(jnp.where(diagonal, jnp.broadcast_to(t_ref[d, k : k + 1, :], (LANES, LANES)), 0.0), axis=1, keepdims=True)
                        for k in range(tail_rows)
                    ]
                    outs[p][:, at + n - 1 : at + n] = jnp.concatenate(column, axis=0).astype(outs[p].dtype)

    return pl.pallas_call(
        body,
        name="w_in_pieces",
        grid=(D_MODEL // tm,),
        in_specs=[
            pl.BlockSpec((N_DEV, tm, aligned), lambda i: (0, i, 0)),
            pl.BlockSpec((N_DEV, None, tail_rows, LANES), lambda i: (0, i, 0, 0)),
        ],
        out_specs=[pl.BlockSpec((tm, W_IN_A), lambda i: (i, 0)), pl.BlockSpec((tm, FL_PAD), lambda i: (i, 0)), pl.BlockSpec((tm, 2 * D_MODEL), lambda i: (i, 0))],
        out_shape=[
            jax.ShapeDtypeStruct((D_MODEL, W_IN_A), gathered.dtype),
            jax.ShapeDtypeStruct((D_MODEL, FL_PAD), gathered.dtype),
            jax.ShapeDtypeStruct((D_MODEL, 2 * D_MODEL), gathered.dtype),
        ],
        compiler_params=_params(("parallel",)),
    )(gathered, tails.reshape(N_DEV, D_MODEL // tm, tail_rows, LANES))


def _dw_in(h, du, dq, dk, dv, dfl, dgates, token):
    T = h.shape[0]
    bt, bk = min(T, DW_TOKENS // 2), 512
    nt = T // bt
    pieces = (du, dq, dk, dv, dfl, dgates)
    offs = [0]
    for p in pieces:
        offs.append(offs[-1] + p.shape[1])

    aligned = W_IN_SHARD - 1
    tail_rows = bk // LANES

    def body(h_ref, *rest):
        refs, o_ref, t_ref, acc = rest[: len(pieces)], rest[-3], rest[-2], rest[-1]

        @pl.when(pl.program_id(1) == 0)
        def _():
            acc[...] = jnp.zeros_like(acc)

        ht = h_ref[...].T
        for ref, at in zip(refs, offs):
            acc[:, at : at + ref.shape[1]] += _mm(ht, ref[...])

        @pl.when(pl.program_id(1) == nt - 1)
        def _():
            starts = (0, W_IN_A, W_IN_A + FL_PAD)
            diagonal = lax.broadcasted_iota(jnp.int32, (LANES, LANES), 0) == lax.broadcasted_iota(jnp.int32, (LANES, LANES), 1)
            for d in range(N_DEV):
                for p, at, to, n in _w_in_segments(d):
                    m = min(n, aligned - to)
                    if m > 0:
                        o_ref[d, :, to : to + m] = acc[:, starts[p] + at : starts[p] + at + m].astype(BF16)
                    if to + n == W_IN_SHARD:
                        last = starts[p] + at + n - 1
                        column = acc[:, last : last + 1].astype(BF16).astype(F32)
                        for k in range(tail_rows):
                            rows = jnp.broadcast_to(column[k * LANES : (k + 1) * LANES], (LANES, LANES))
                            t_ref[d, k : k + 1, :] = jnp.sum(jnp.where(diagonal, rows, 0.0), axis=0, keepdims=True)

    main, tails = pl.pallas_call(
        body,
        name="dw_in",
        grid=(D_MODEL // bk, nt),
        in_specs=[pl.BlockSpec((bt, bk), lambda k, t: (t, k))] + [pl.BlockSpec((bt, p.shape[1]), lambda k, t: (t, 0)) for p in pieces] + [_HBM],
        out_specs=[
            pl.BlockSpec((N_DEV, bk, aligned), lambda k, t: (0, k, 0)),
            pl.BlockSpec((N_DEV, None, tail_rows, LANES), lambda k, t: (0, k, 0, 0)),
        ],
        out_shape=[
            jax.ShapeDtypeStruct((N_DEV, D_MODEL, aligned), BF16),
            jax.ShapeDtypeStruct((N_DEV, D_MODEL // bk, tail_rows, LANES), F32),
        ],
        scratch_shapes=[pltpu.VMEM((bk, offs[-1]), F32)],
        compiler_params=_params(("parallel", "arbitrary")),
    )(h, *pieces, token)
    return main, tails.reshape(N_DEV, D_MODEL // LANES, LANES)


def _position():
    return lax.axis_index("x"), lax.axis_index("y"), lax.axis_index("c")


_HBM = pl.BlockSpec(memory_space=pl.ANY)


def _all_gather(blocks, name):
    n = len(blocks)
    parts = [(a, q * (b.shape[0] // 4), b.shape[0] // 4) for a, b in enumerate(blocks) if b.shape[0] >= ROW_TILE for q in range(4)]
    parts += [(a, 0, b.shape[0]) for a, b in enumerate(blocks) if b.shape[0] < ROW_TILE]

    def body(*refs):
        xs, outs = refs[:n], refs[n : 2 * n]
        send_sems, recv_sems, local_sems = refs[2 * n :]
        x, y, c = _position()
        me, sibling = (x, y, c), (x, y, 1 - c)
        chips = [(1 - x, y), (x, 1 - y), (1 - x, 1 - y)]

        def rows(u, px, py, pc):
            a, lo, size = parts[u]
            return outs[a].at[4 * px + 2 * py + pc, pl.ds(lo, size)]

        def own(u):
            a, lo, size = parts[u]
            return xs[a].at[pl.ds(lo, size)]

        def copy(u, k, blk, to, src=None):
            return pltpu.make_async_remote_copy(
                src_ref=rows(u, *blk) if src is None else src, dst_ref=rows(u, *blk),
                send_sem=send_sems.at[7 * u + k], recv_sem=recv_sems.at[7 * u + k], device_id=to, device_id_type=MESH,
            )

        first = []
        for u in range(len(parts)):
            first += [copy(u, 1 + j, me, (*chip, c), src=own(u)) for j, chip in enumerate(chips)]
            first.append(copy(u, 0, me, sibling, src=own(u)))
        mine = [pltpu.make_async_copy(xs[a], outs[a].at[4 * x + 2 * y + c], local_sems.at[a]) for a in range(n)]
        for cp in first + mine:
            cp.start()
        passed = []
        for u in range(len(parts)):
            for j, chip in enumerate(chips):
                copy(u, 1 + j, (*chip, c), me).wait_recv()
                passed.append(copy(u, 4 + j, (*chip, c), sibling))
                passed[-1].start()
        for u in range(len(parts)):
            copy(u, 0, sibling, me).wait_recv()
            for j, chip in enumerate(chips):
                copy(u, 4 + j, (*chip, 1 - c), me).wait_recv()
        for cp in first + passed:
            cp.wait_send()
        for cp in mine:
            cp.wait()

    return pl.pallas_call(
        body,
        name=name,
        out_shape=[jax.ShapeDtypeStruct((N_DEV, *b.shape), b.dtype) for b in blocks],
        in_specs=[_HBM] * n,
        out_specs=[_HBM] * n,
        scratch_shapes=[pltpu.SemaphoreType.DMA((7 * len(parts),)), pltpu.SemaphoreType.DMA((7 * len(parts),)), pltpu.SemaphoreType.DMA((n,))],
    )(*blocks)


_SEM = pl.BlockSpec(memory_space=pltpu.SEMAPHORE)
_HBM_ONLY = pl.BlockSpec(memory_space=pltpu.HBM)
_SIDE_EFFECT = pltpu.SideEffectType.DATAFLOW_SIDE_EFFECTING


def _peer(x, y, c, k):
    return (1 - x if k & 4 else x, 1 - y if k & 2 else y, 1 - c if k & 1 else c)


_PEER_BITS = {"gather": range(1, N_DEV), "gather_half": (1, 4, 2, 6), "forward": (4, 2, 6), "scatter": range(1, N_DEV)}
_GATHERS = ("gather", "gather_half")


def _exchange_copies(src_refs, land_refs, send_sems, recv_sems, pattern, receive_side):
    x, y, c = _position()
    me = 4 * x + 2 * y + c
    bits = _PEER_BITS[pattern]
    cps = []
    for j, k in enumerate(bits):
        px, py, pc = _peer(x, y, c, k)
        peer = 4 * px + 2 * py + pc
        for a, (src, land) in enumerate(zip(src_refs, land_refs)):
            to = (px, py, pc)
            if pattern == "forward":
                slot = 4 * px + 2 * py + (1 - c if receive_side else c)
                s, to = land.at[slot], (x, y, 1 - c)
            else:
                s, slot = (src if pattern in _GATHERS else src.at[peer]), (peer if receive_side else me)
            cps.append(pltpu.make_async_remote_copy(
                src_ref=s, dst_ref=land.at[slot],
                send_sem=send_sems.at[len(bits) * a + j], recv_sem=recv_sems.at[len(bits) * a + j],
                device_id=to, device_id_type=MESH,
            ))
    return cps


def _own_copies(src_refs, land_refs, own_sems):
    x, y, c = _position()
    return [
        pltpu.make_async_copy(src, land.at[4 * x + 2 * y + c], own_sems.at[a])
        for a, (src, land) in enumerate(zip(src_refs, land_refs))
    ]


def _exchange_start(srcs, after, name, pattern):
    n = len(srcs)
    m = len(_PEER_BITS[pattern])
    lands = [jax.ShapeDtypeStruct((N_DEV, *s.shape[-2:]), s.dtype) for s in srcs]

    def body(*refs):
        src_refs, land_refs = refs[1 : 1 + n], refs[1 + n : 1 + 2 * n]
        send_sems, recv_sems, own_sems = refs[1 + 2 * n : 4 + 2 * n]
        token = refs[-1]
        if pattern in _GATHERS:
            for cp in _own_copies(src_refs, land_refs, own_sems):
                cp.start()
        for cp in _exchange_copies(src_refs, land_refs, send_sems, recv_sems, pattern, receive_side=False):
            cp.start()
        token[...] = jnp.zeros_like(token)

    hbm = lambda t: pltpu.with_memory_space_constraint(t, pltpu.HBM)
    out = pl.pallas_call(
        body,
        name=name,
        out_shape=(
            pltpu.SemaphoreType.DMA((m * n,)), pltpu.SemaphoreType.DMA((m * n,)), pltpu.SemaphoreType.DMA((n,)),
            *[pltpu.HBM(s.shape, s.dtype) for s in srcs], *[pltpu.HBM(l.shape, l.dtype) for l in lands],
            jax.ShapeDtypeStruct((8, LANES), F32),
        ),
        in_specs=(_HBM, *[_HBM_ONLY] * (2 * n)),
        out_specs=(_SEM, _SEM, _SEM, *[_HBM_ONLY] * (2 * n), pl.BlockSpec(memory_space=pltpu.VMEM)),
        input_output_aliases={1 + i: 3 + i for i in range(2 * n)},
        compiler_params=pltpu.CompilerParams(has_side_effects=_SIDE_EFFECT),
    )(after, *[hbm(s) for s in srcs], *[hbm(lax.empty(l.shape, l.dtype)) for l in lands])
    return out[:3], out[3 : 3 + n], out[3 + n : 3 + 2 * n], out[-1]


def _exchange_wait(sems, srcs, lands, after, name, pattern):
    n = len(srcs)

    def body(*refs):
        src_refs, land_refs = refs[:n], refs[n : 2 * n]
        send_sems, recv_sems, own_sems = refs[2 * n : 2 * n + 3]
        if pattern in _GATHERS:
            for cp in _own_copies(src_refs, land_refs, own_sems):
                cp.wait()
        for cp in _exchange_copies(src_refs, land_refs, send_sems, recv_sems, pattern, receive_side=True):
            cp.wait_send()
            cp.wait_recv()

    out = pl.pallas_call(
        body,
        name=name,
        out_shape=(*[pltpu.HBM(s.shape, s.dtype) for s in srcs], *[pltpu.HBM(l.shape, l.dtype) for l in lands]),
        in_specs=(*[_HBM_ONLY] * (2 * n), _SEM, _SEM, _SEM, _HBM),
        out_specs=tuple([_HBM_ONLY] * (2 * n)),
        input_output_aliases={i: i for i in range(2 * n)},
        compiler_params=pltpu.CompilerParams(has_side_effects=_SIDE_EFFECT),
    )(*srcs, *lands, *sems, after)
    return out[:n], out[n:]


def _gather_forward(sems, srcs, lands, after, name):
    n = len(srcs)
    m = len(_PEER_BITS["forward"])

    def body(*refs):
        src_refs, land_refs = refs[:n], refs[n : 2 * n]
        send_sems, recv_sems, own_sems = refs[2 * n : 2 * n + 3]
        forward_send, forward_recv, token = refs[2 * n + 4], refs[2 * n + 5], refs[-1]
        for cp in _own_copies(src_refs, land_refs, own_sems):
            cp.wait()
        for cp in _exchange_copies(src_refs, land_refs, send_sems, recv_sems, "gather_half", receive_side=True):
            cp.wait_send()
            cp.wait_recv()
        for cp in _exchange_copies(land_refs, land_refs, forward_send, forward_recv, "forward", receive_side=False):
            cp.start()
        token[...] = jnp.zeros_like(token)

    out = pl.pallas_call(
        body,
        name=name,
        out_shape=(
            pltpu.SemaphoreType.DMA((m * n,)), pltpu.SemaphoreType.DMA((m * n,)),
            *[pltpu.HBM(l.shape, l.dtype) for l in lands], jax.ShapeDtypeStruct((8, LANES), F32),
        ),
        in_specs=(*[_HBM_ONLY] * (2 * n), _SEM, _SEM, _SEM, _HBM),
        out_specs=(_SEM, _SEM, *[_HBM_ONLY] * n, pl.BlockSpec(memory_space=pltpu.VMEM)),
        input_output_aliases={n + i: 2 + i for i in range(n)},
        compiler_params=pltpu.CompilerParams(has_side_effects=_SIDE_EFFECT),
    )(*srcs, *lands, *sems, after)
    return out[:2], out[2 : 2 + n], out[-1]


def _forward_wait(sems, lands, after, name):
    n = len(lands)

    def body(*refs):
        land_refs = refs[:n]
        for cp in _exchange_copies(land_refs, land_refs, refs[n], refs[n + 1], "forward", receive_side=True):
            cp.wait_send()
            cp.wait_recv()

    return pl.pallas_call(
        body,
        name=name,
        out_shape=tuple(pltpu.HBM(l.shape, l.dtype) for l in lands),
        in_specs=(*[_HBM_ONLY] * n, _SEM, _SEM, _HBM),
        out_specs=tuple([_HBM_ONLY] * n),
        input_output_aliases={i: i for i in range(n)},
        compiler_params=pltpu.CompilerParams(has_side_effects=_SIDE_EFFECT),
    )(*lands, *sems, after)


def _rows_tile(r):
    return ROW_TILE if r % ROW_TILE == 0 else r


def _adamw(w, g, m, v):
    m = ADAM_B1 * m + (1.0 - ADAM_B1) * g
    v = ADAM_B2 * v + (1.0 - ADAM_B2) * (g * g)
    m_hat = m / (1.0 - ADAM_B1 ** ADAM_STEP)
    v_hat = v / (1.0 - ADAM_B2 ** ADAM_STEP)
    delta = -ADAM_LR * (m_hat / (jnp.sqrt(v_hat) + ADAM_EPS) + ADAM_WD * w)
    return delta, m, v


def _shard_update_direct(parts, chunks, w, m, v, me, name):
    _, r, c = w.shape
    br = _rows_tile(r)

    def body(me_ref, p_ref, own_ref, w_ref, m_ref, v_ref, g_ref, d_ref, nm_ref, nv_ref):
        g = None
        for n in range(N_DEV):
            part = jnp.where(me_ref[0] == n, own_ref[...], p_ref[n]).astype(F32)
            g = part if g is None else g + part
        g_ref[...] = g
        d_ref[...], nm_ref[...], nv_ref[...] = _adamw(w_ref[...], g, m_ref[...], v_ref[...])

    shard = pl.BlockSpec((None, br, c), lambda i, me: (0, i, 0))
    return pl.pallas_call(
        body,
        name=name,
        grid_spec=pltpu.PrefetchScalarGridSpec(
            num_scalar_prefetch=1,
            grid=(r // br,),
            in_specs=[
                pl.BlockSpec((N_DEV, br, c), lambda i, me: (0, i, 0)),
                pl.BlockSpec((None, br, c), lambda i, me: (me[0], i, 0)),
                shard, shard, shard,
            ],
            out_specs=[shard, shard, shard, shard],
        ),
        out_shape=[jax.ShapeDtypeStruct((1, r, c), F32)] * 4,
        compiler_params=_params(("parallel",)),
    )(me, parts, chunks, w, m, v)


def _w_in_update(parts, chunks, tail_parts, tail_chunks, w, m, v, me, name):
    _, r, c = w.shape
    br = _rows_tile(r)
    tail_rows = br // LANES

    def body(me_ref, p_ref, own_ref, tp_ref, town_ref, w_ref, m_ref, v_ref, g_ref, d_ref, nm_ref, nv_ref):
        g = tail = None
        for n in range(N_DEV):
            mine = me_ref[0] == n
            part = jnp.where(mine, own_ref[...], p_ref[n]).astype(F32)
            last = jnp.where(mine, town_ref[...], tp_ref[n])
            g = part if g is None else g + part
            tail = last if tail is None else tail + last
        diagonal = lax.broadcasted_iota(jnp.int32, (LANES, LANES), 0) == lax.broadcasted_iota(jnp.int32, (LANES, LANES), 1)
        column = jnp.concatenate(
            [
                jnp.sum(jnp.where(diagonal, jnp.broadcast_to(tail[k : k + 1, :], (LANES, LANES)), 0.0), axis=1, keepdims=True)
                for k in range(tail_rows)
            ],
            axis=0,
        )
        for lo, hi, grad in ((0, c - 1, g), (c - 1, c, column)):
            g_ref[:, lo:hi] = grad
            d_ref[:, lo:hi], nm_ref[:, lo:hi], nv_ref[:, lo:hi] = _adamw(w_ref[:, lo:hi], grad, m_ref[:, lo:hi], v_ref[:, lo:hi])

    shard = pl.BlockSpec((None, br, c), lambda i, me: (0, i, 0))
    by_block = lambda t: t.reshape(N_DEV, r // br, tail_rows, LANES)
    return pl.pallas_call(
        body,
        name=name,
        grid_spec=pltpu.PrefetchScalarGridSpec(
            num_scalar_prefetch=1,
            grid=(r // br,),
            in_specs=[
                pl.BlockSpec((N_DEV, br, c - 1), lambda i, me: (0, i, 0)),
                pl.BlockSpec((None, br, c - 1), lambda i, me: (me[0], i, 0)),
                pl.BlockSpec((N_DEV, None, tail_rows, LANES), lambda i, me: (0, i, 0, 0)),
                pl.BlockSpec((None, None, tail_rows, LANES), lambda i, me: (me[0], i, 0, 0)),
                shard, shard, shard,
            ],
            out_specs=[shard, shard, shard, shard],
        ),
        out_shape=[jax.ShapeDtypeStruct((1, r, c), F32)] * 4,
        compiler_params=_params(("parallel",)),
    )(me, parts, chunks, by_block(tail_parts), by_block(tail_chunks), w, m, v)


def _small_update(parts, first_rows, ws, ms, vs):
    k = len(ws)

    def unpacked(rows, shape):
        if len(shape) == 2 and shape[1] <= LANES:
            return rows[0:1, : shape[1]]
        if len(shape) == 2:
            return jnp.concatenate([rows[r : r + 1] for r in range(shape[1] // LANES)], axis=1)
        return rows.reshape(shape)

    def body(p_ref, f_ref, *refs):
        w_refs, m_refs, v_refs = refs[:k], refs[k : 2 * k], refs[2 * k : 3 * k]
        outs, loss_ref = refs[3 * k : 7 * k], refs[7 * k]
        g, first = p_ref[0], f_ref[0]
        for n in range(1, N_DEV):
            g = g + p_ref[n]
            first = first + f_ref[n]
        g = jnp.concatenate([g[:8] + first, g[8:]], axis=0)
        off = 0
        for i, (_, rows) in enumerate(_SMALL):
            gi = unpacked(g[off : off + rows], w_refs[i].shape)
            off += rows
            outs[i][...] = gi
            outs[k + i][...], outs[2 * k + i][...], outs[3 * k + i][...] = _adamw(w_refs[i][...], gi, m_refs[i][...], v_refs[i][...])
        loss_ref[...] = g[off : off + 1, 0:1]

    out = pl.pallas_call(
        body,
        name="small_update",
        out_shape=[jax.ShapeDtypeStruct(w.shape, F32) for _ in range(4) for w in ws] + [jax.ShapeDtypeStruct((1, 1), F32)],
        compiler_params=pltpu.CompilerParams(vmem_limit_bytes=VMEM_LIMIT),
    )(parts, first_rows, *ws, *ms, *vs)
    return [out[a * k : (a + 1) * k] for a in range(4)], out[4 * k]


_SHARD_AXIS = (1, 1, 1, 0, 0, 0, 0)
_TRANSPOSED = (False, False, False, False, True, True, False)


def _full_from_gathered(t, axis):
    if axis == 0:
        return t.reshape(N_DEV * t.shape[1], t.shape[2])
    return t


_SMALL = (("norm1_g", 8), ("norm2_g", 8), ("norm_f_g", 8), ("b_forget", 8), ("pool_scale", 8), ("pool_mix", 512))


def _pack_small(vals, loss_row):
    parts = []
    for (name, rows), t in zip(_SMALL, vals):
        f = t.astype(F32).reshape(-1)
        f = jnp.concatenate([f, jnp.zeros((rows * LANES - f.shape[0],), F32)]).reshape(rows, LANES)
        parts.append(f)
    parts.append(loss_row)
    return jnp.concatenate(parts, axis=0)


def _local_grads(x, tgt, g1, g2, gf, b_forget, pool_mix, pool_scale, w_in, fwd_token, out_weights, ffn_weights, ffn_grads_out, out_grads_out, small_grads_out, in_grads_out, norm1_grad_out):
    n_seq, S, _ = x.shape
    T = n_seq * S
    x2 = x.reshape(T, D_MODEL)
    tg2 = tgt.reshape(T, D_MODEL)
    w_uqkv, w_fl, w_g = w_in
    b_pad = jnp.concatenate([b_forget.reshape(1, N_HEADS), jnp.zeros((1, FL_PAD - N_HEADS), F32)], axis=1)
    mix_b = pool_mix.reshape(len(POOL_WINDOWS), GROUP_DIM, GROUP_DIM).astype(BF16)
    scale = pool_scale.reshape(1, POOL_WIDTH)
    g1 = g1.reshape(1, D_MODEL)
    g2 = g2.reshape(1, D_MODEL)
    gf = gf.reshape(1, D_MODEL)

    h, u, qkv, fl, gates = _in_proj(x2, g1, w_uqkv, w_fl, w_g, fwd_token)
    fcol = _forget_fwd(fl, b_pad, n_seq, S)
    pm, p2, p3 = _pool_fwd(u, mix_b, scale, n_seq, S)
    a, lse = _attn_fwd(qkv, fcol, n_seq, S)
    w_po, w_ao, w_out = out_weights(a)
    merged, x1, attn_y, pool_y = _mix_out(a, p3, gates, x2, w_ao, w_po, w_out)
    w_gate_t, w_up_t, w_down = ffn_weights(x1)
    h2, gate, up, act, dx2, loss_rows, dgf = _ffn_fwd(x1, g2, gf, tg2, w_gate_t, w_up_t, w_down)

    dgate, dup, dx1, dg2 = _ffn_bwd(dx2, gate, up, x1, g2, w_gate_t, w_up_t, w_down)
    bwd_token = ffn_grads_out(_matmul_tn(dgate, h2, "dw_ffn_gate"), _matmul_tn(dup, h2, "dw_ffn_up"), _matmul_tn(act, dx2, "dw_ffn_down"))
    dgates, dpy, day, da, dp2, dscale = _mix_bwd(dx1, gates, pool_y, attn_y, p2, scale, w_out, w_ao, w_po, bwd_token)
    out_token = out_grads_out(
        _matmul_tn(p3, dpy, "dw_pool_out", col_chunks=True), _matmul_tn(a, day, "dw_attn_out", col_chunks=True), _matmul_tn(merged, dx1, "dw_out")
    )
    du, dmix = _pool_bwd(dp2, pm, mix_b, out_token, n_seq, S)
    dq, dk, dv, dfk, dfq = _attn_bwd(qkv, da, a, fcol, lse, n_seq, S)
    dfl, db = _forget_bwd(dfk, dfq, fl, b_pad, n_seq, S)
    small_token = small_grads_out((jnp.zeros_like(g1), dg2, dgf, db[:, :N_HEADS], dscale, dmix), loss_rows)
    in_token = in_grads_out(*_dw_in(h, du, dq, dk, dv, dfl, dgates, small_token))
    dx, dg1 = _in_proj_bwd(du, dq, dk, dv, dfl, dgates, x2, dx1, g1, w_uqkv, w_fl, w_g, in_token)
    norm1_grad_out(dg1)
    return dx.reshape(n_seq, S, D_MODEL)


def kernel(x, norm1_g, w_in, b_forget, pool_mix, pool_scale, w_pool_out, w_attn_out, w_out, norm2_g, w_ffn_gate, w_ffn_up, w_ffn_down, norm_f_g, loss_target, m_norm1_g, m_w_in, m_b_forget, m_pool_mix, m_pool_scale, m_w_pool_out, m_w_attn_out, m_w_out, m_norm2_g, m_w_ffn_gate, m_w_ffn_up, m_w_ffn_down, m_norm_f_g, v_norm1_g, v_w_in, v_b_forget, v_pool_mix, v_pool_scale, v_w_pool_out, v_w_attn_out, v_w_out, v_norm2_g, v_w_ffn_gate, v_w_ffn_up, v_w_ffn_down, v_norm_f_g):
    names = ("w_in", "w_pool_out", "w_attn_out", "w_out", "w_ffn_gate", "w_ffn_up", "w_ffn_down")
    w_sh = (w_in, w_pool_out, w_attn_out, w_out, w_ffn_gate, w_ffn_up, w_ffn_down)
    m_sh = (m_w_in, m_w_pool_out, m_w_attn_out, m_w_out, m_w_ffn_gate, m_w_ffn_up, m_w_ffn_down)
    v_sh = (v_w_in, v_w_pool_out, v_w_attn_out, v_w_out, v_w_ffn_gate, v_w_ffn_up, v_w_ffn_down)

    cx, cy, cc = _position()
    me = 4 * cx + 2 * cy + cc
    def stored(t, transposed):
        return jnp.transpose(t, (0, 2, 1)) if transposed else t

    w_sh, m_sh, v_sh = ([stored(t, tr) for t, tr in zip(ts, _TRANSPOSED)] for ts in (w_sh, m_sh, v_sh))
    shards = [w[0].astype(BF16) for w in w_sh]
    last_in = shards[0][:, W_IN_SHARD - 1].astype(F32).reshape(D_MODEL // LANES, LANES)
    gathered_in, tails_in = _all_gather([shards[0][:, : W_IN_SHARD - 1], last_in], "w_in_all_gather")
    out_sems = _exchange_start(shards[1:4], gathered_in, "out_weights_gather_start", "gather")
    ffn_sems = _exchange_start(shards[4:], out_sems[3], "ffn_weights_gather_start", "gather_half")
    no_order = jnp.zeros((8, LANES), F32)
    started = {}

    def out_weights(after):
        forward_sems, lands, token = _gather_forward(*ffn_sems[:3], after, "ffn_weights_forward_start")
        started["forward"] = (forward_sems, lands)
        _, lands = _exchange_wait(*out_sems[:3], token, "out_weights_gather_wait", "gather")
        return [_full_from_gathered(t, axis) for t, axis in zip(lands, _SHARD_AXIS[out])]

    def ffn_weights(after):
        lands = _forward_wait(*started["forward"], after, "ffn_weights_gather_wait")
        return [_full_from_gathered(t, axis) for t, axis in zip(lands, _SHARD_AXIS[ffn])]

    def hold_ffn_grads(*whole_grads):
        started["held"] = whole_grads
        return no_order

    def scatter_grads(*out_grads):
        chunks = [
            t if axis == 1 else t.reshape(N_DEV, -1, t.shape[1])
            for t, axis in zip((*out_grads, *started["held"]), _SHARD_AXIS[scattered])
        ]
        started["scatter"] = _exchange_start(chunks, no_order, "grads_scatter_start", "scatter")
        return started["scatter"][3]

    def gather_small(small, loss_rows):
        started["small"] = _exchange_start([_pack_small(small, loss_rows)], no_order, "small_grads_gather_start", "gather")
        return started["small"][3]

    def scatter_w_in(chunks_in, tails_in):
        started["in"] = _exchange_start([chunks_in, tails_in], no_order, "w_in_grads_scatter_start", "scatter")
        return started["in"][3]

    def gather_norm1(dg1):
        rows = jnp.reshape(dg1, (8, LANES))
        started["norm1"] = _exchange_start([rows], no_order, "norm1_grad_gather_start", "gather")

    ffn, out, scattered = slice(4, 7), slice(1, 4), slice(1, 7)
    grad_x = _local_grads(
        x, loss_target, norm1_g, norm2_g, norm_f_g, b_forget, pool_mix, pool_scale, _w_in_pieces(gathered_in, tails_in), ffn_sems[3],
        out_weights, ffn_weights, hold_ffn_grads, scatter_grads, gather_small, scatter_w_in, gather_norm1,
    )
    me_index = jnp.reshape(me, (1,)).astype(jnp.int32)

    srcs, lands = _exchange_wait(*started["scatter"][:3], started["norm1"][3], "grads_scatter_wait", "scatter")
    updates = [
        _shard_update_direct(p, s, w, m, v, me_index, "update_" + n)
        for p, s, w, m, v, n in zip(lands, srcs, w_sh[scattered], m_sh[scattered], v_sh[scattered], names[scattered])
    ]
    updates_out, updates_ffn = updates[:3], updates[3:]

    small_w = (norm1_g, norm2_g, norm_f_g, b_forget, pool_scale, pool_mix)
    small_m = (m_norm1_g, m_norm2_g, m_norm_f_g, m_b_forget, m_pool_scale, m_pool_mix)
    small_v = (v_norm1_g, v_norm2_g, v_norm_f_g, v_b_forget, v_pool_scale, v_pool_mix)
    (sent_in, sent_tails), (parts_in, parts_tails) = _exchange_wait(*started["in"][:3], updates_ffn[-1][0], "w_in_grads_scatter_wait", "scatter")
    update_in = _w_in_update(parts_in, sent_in, parts_tails, sent_tails, w_in, m_w_in, v_w_in, me_index, "update_w_in")

    def gathered_small(key, after, name):
        _, lands = _exchange_wait(*started[key][:3], after, name, "gather")
        return lands[0]

    parts = gathered_small("small", update_in[0], "small_grads_gather_wait")
    first_rows = gathered_small("norm1", parts, "norm1_grad_gather_wait")
    (g_s, d_s, nm_s, nv_s), loss = _small_update(parts, first_rows, small_w, small_m, small_v)
    g_w, d_w, nm_w, nv_w = zip(*(
        [stored(t, tr) for t in u] for u, tr in zip([update_in] + updates_out + updates_ffn, _TRANSPOSED)
    ))
    loss = loss.reshape(())
    (g1, g2, gf, gb, gsc, gmix), (d1, d2, df, db_, dsc, dmx) = g_s, d_s
    (m1, m2, mf, mb, msc, mmx), (v1, v2, vf, vb, vsc, vmx) = nm_s, nv_s

    def ordered(n1, win, b, mix, sc, wpo, wao, wout, n2, wg, wu, wd, nf):
        return (n1, win, b, mix, sc, wpo, wao, wout, n2, wg, wu, wd, nf)

    grads = ordered(g1, g_w[0], gb, gmix, gsc, g_w[1], g_w[2], g_w[3], g2, g_w[4], g_w[5], g_w[6], gf)
    deltas = ordered(d1, d_w[0], db_, dmx, dsc, d_w[1], d_w[2], d_w[3], d2, d_w[4], d_w[5], d_w[6], df)
    new_m = ordered(m1, nm_w[0], mb, mmx, msc, nm_w[1], nm_w[2], nm_w[3], m2, nm_w[4], nm_w[5], nm_w[6], mf)
    new_v = ordered(v1, nv_w[0], vb, vmx, vsc, nv_w[1], nv_w[2], nv_w[3], v2, nv_w[4], nv_w[5], nv_w[6], vf)
    return (loss, grad_x, *grads, *deltas, *new_m, *new_v)
```

```python
import jax
import jax.numpy as jnp
from jax import lax
from jax.experimental import pallas as pl
from jax.experimental.pallas import tpu as pltpu

F32 = jnp.float32
BF16 = jnp.bfloat16
MESH = pl.DeviceIdType.MESH

D_MODEL = 1024
POOL_WINDOWS = (2, 4, 8, 16)
POOL_WIDTH = 512
GROUP_DIM = 128
ATTN_WIDTH = 512
HEAD_DIM = 64
N_HEADS = 8
N_PAIRS = 4
D_FF = 2816
RMS_EPS = 1e-6
N_DEV = 8
LANES = 128
FL_PAD = 128

ADAM_LR = 0.001
ADAM_B1 = 0.9
ADAM_B2 = 0.999
ADAM_EPS = 1e-08
ADAM_WD = 0.01
ADAM_STEP = 10

VMEM_LIMIT = 56 * 1024 * 1024
VMEM_LIMIT_MAX = 60 * 1024 * 1024
ROW_TILE = 512
ATTN_BLOCK = 512
FF_CHUNK = 256
FF_ROW_TILE = 512
DW_TOKENS = 2048


def _mm(a, b):
    return jnp.dot(a, b, preferred_element_type=F32)


def _mm_nt(a, b):
    return lax.dot_general(a, b, (((1,), (1,)), ((), ())), preferred_element_type=F32)


def _mm_tn(a, b):
    return lax.dot_general(a, b, (((0,), (0,)), ((), ())), preferred_element_type=F32)


def _whole_cols(w_ref):
    if len(w_ref.shape) == 2:
        return w_ref[...]
    return jnp.concatenate([w_ref[d] for d in range(w_ref.shape[0])], axis=1)


def _sigmoid(x):
    return 1.0 / (1.0 + jnp.exp(-x))


def _params(sem, vmem=VMEM_LIMIT):
    return pltpu.CompilerParams(dimension_semantics=sem, vmem_limit_bytes=vmem)


def _const_spec(shape):
    nd = len(shape)
    return pl.BlockSpec(shape, lambda *_: (0,) * nd, pipeline_mode=pl.Buffered(1))


def _rms_fwd(x, g):
    r = lax.rsqrt(jnp.mean(x * x, axis=-1, keepdims=True) + RMS_EPS)
    xh = x * r
    return xh * g, xh, r


def _rms_bwd(dy, xh, r, g):
    dxh = dy * g
    dx = r * (dxh - xh * jnp.mean(dxh * xh, axis=-1, keepdims=True))
    return dx, dy * xh


def _in_proj(x, g1, w_uqkv, w_fl, w_g, token):
    T = x.shape[0]
    tm = ROW_TILE

    def body(x_ref, g_ref, wa_ref, wf_ref, wg_ref, token_ref, h_ref, u_ref, qkv_ref, fl_ref, gt_ref):
        h, _, _ = _rms_fwd(x_ref[...], g_ref[...])
        hb = h.astype(BF16)
        h_ref[...] = hb
        z = _mm(hb, wa_ref[...])
        u_ref[...] = z[:, :POOL_WIDTH]
        qkv_ref[...] = z[:, POOL_WIDTH:].astype(BF16)
        fl_ref[...] = _mm(hb, wf_ref[...])
        gt_ref[...] = _mm(hb, wg_ref[...]).astype(BF16)

    row = lambda n: pl.BlockSpec((tm, n), lambda i: (i, 0))
    return pl.pallas_call(
        body,
        name="in_proj",
        grid=(T // tm,),
        in_specs=[row(D_MODEL), _const_spec((1, D_MODEL)), _const_spec(w_uqkv.shape), _const_spec(w_fl.shape), _const_spec(w_g.shape), _HBM],
        out_specs=[row(D_MODEL), row(POOL_WIDTH), row(3 * ATTN_WIDTH), row(FL_PAD), row(2 * D_MODEL)],
        out_shape=[
            jax.ShapeDtypeStruct((T, D_MODEL), BF16),
            jax.ShapeDtypeStruct((T, POOL_WIDTH), F32),
            jax.ShapeDtypeStruct((T, 3 * ATTN_WIDTH), BF16),
            jax.ShapeDtypeStruct((T, FL_PAD), F32),
            jax.ShapeDtypeStruct((T, 2 * D_MODEL), BF16),
        ],
        compiler_params=_params(("parallel",)),
    )(x, g1, w_uqkv, w_fl, w_g, token)


def _log_sigmoid(x):
    return jnp.minimum(x, 0.0) - jnp.log(1.0 + jnp.exp(-jnp.abs(x)))


def _forget_fwd(fl, b_pad, n_seq, S):
    def body(fl_ref, b_ref, fcol_ref):
        lf = _log_sigmoid(fl_ref[...] + b_ref[...])
        t = lf.T
        lane = lax.broadcasted_iota(jnp.int32, t.shape, 1)
        k = 1
        while k < S:
            t = t + jnp.where(lane >= k, pltpu.roll(t, k, 1), 0.0)
            k *= 2
        fcol_ref[...] = t.T

    return pl.pallas_call(
        body,
        name="forget_fwd",
        grid=(n_seq,),
        in_specs=[pl.BlockSpec((S, FL_PAD), lambda s: (s, 0)), _const_spec((1, FL_PAD))],
        out_specs=pl.BlockSpec((S, FL_PAD), lambda s: (s, 0)),
        out_shape=jax.ShapeDtypeStruct((n_seq * S, FL_PAD), F32),
        compiler_params=_params(("parallel",)),
    )(fl, b_pad)


def _window_pick(g, v2, v4, v8, v16):
    return jnp.where(g == 0, v2, jnp.where(g == 1, v4, jnp.where(g == 2, v8, v16)))


def _pool_fwd(u, mix_b, scale, n_seq, S):
    T = n_seq * S

    def body(u_ref, mix_ref, sc_ref, pm_ref, p2_ref, p3_ref):
        g = pl.program_id(1)
        uu = u_ref[...]
        row = lax.broadcasted_iota(jnp.int32, uu.shape, 0)

        def back(a, k):
            return jnp.where(row >= k, pltpu.roll(a, k, 0), 0.0)

        s2 = uu + back(uu, 1)
        s4 = s2 + back(s2, 2)
        s8 = s4 + back(s4, 4)
        s16 = s8 + back(s8, 8)
        w = _window_pick(g, 2.0, 4.0, 8.0, 16.0)
        cnt = jnp.minimum((row + 1).astype(F32), w)
        pm = _window_pick(g, s2, s4, s8, s16) / cnt - uu
        pmb = pm.astype(BF16)
        pm_ref[...] = pmb
        p2 = _mm(pmb, mix_ref[...])
        p2_ref[...] = p2
        p3_ref[...] = (p2 * sc_ref[...]).astype(BF16)

    grp = pl.BlockSpec((S, GROUP_DIM), lambda s, g: (s, g))
    return pl.pallas_call(
        body,
        name="pool_fwd",
        grid=(n_seq, len(POOL_WINDOWS)),
        in_specs=[
            grp,
            pl.BlockSpec((None, GROUP_DIM, GROUP_DIM), lambda s, g: (g, 0, 0)),
            pl.BlockSpec((1, GROUP_DIM), lambda s, g: (0, g)),
        ],
        out_specs=[grp, grp, grp],
        out_shape=[
            jax.ShapeDtypeStruct((T, POOL_WIDTH), BF16),
            jax.ShapeDtypeStruct((T, POOL_WIDTH), F32),
            jax.ShapeDtypeStruct((T, POOL_WIDTH), BF16),
        ],
        compiler_params=_params(("parallel", "parallel")),
    )(u, mix_b, scale)


def _split3(v):
    hi = v.astype(BF16).astype(F32)
    r = v - hi
    mid = r.astype(BF16).astype(F32)
    lo = (r - mid).astype(BF16).astype(F32)
    return hi, mid, lo


def _bias_lanes(v):
    hi, mid, lo = _split3(v)
    lane = lax.broadcasted_iota(jnp.int32, (1, LANES), 1)
    packed = jnp.where(lane < N_HEADS, hi, jnp.where(lane < 2 * N_HEADS, pltpu.roll(mid, N_HEADS, 1), pltpu.roll(lo, 2 * N_HEADS, 1)))
    return jnp.where(lane < 3 * N_HEADS, packed, 0.0).astype(BF16)


def _bias_placement(slot):
    row = lax.broadcasted_iota(jnp.int32, (LANES, N_HEADS * LANES), 0)
    col = lax.broadcasted_iota(jnp.int32, (LANES, N_HEADS * LANES), 1)
    h = col // LANES
    n = col % LANES - jnp.where(h % 2 == 0, HEAD_DIM, 0) - 3 * slot
    return ((n >= 0) & (n < 3) & (row == N_HEADS * n + h)).astype(BF16)


def _augment(xp, h, bias, ones_slot):
    lane = lax.broadcasted_iota(jnp.int32, (1, LANES), 1)
    hh = h % 2
    head = (lane >= HEAD_DIM * hh) & (lane < HEAD_DIM * (hh + 1))
    b = HEAD_DIM * (1 - hh)
    rest = jnp.zeros_like(xp) if bias is None else bias[:, h * LANES : (h + 1) * LANES]
    out = jnp.where(head, xp, rest)
    if ones_slot is not None:
        out = jnp.where((lane >= b + 3 * ones_slot) & (lane < b + 3 * ones_slot + 3), jnp.ones_like(xp), out)
    return out


def _attn_fwd(qkv, fcol, n_seq, S):
    T = n_seq * S
    tb = ATTN_BLOCK
    nq = S // tb
    scale = HEAD_DIM ** -0.5

    def body(q_ref, k_ref, v_ref, fc_ref, o_ref, st_ref, qa_sc, ka_sc, m_sc, l_sc, acc_sc):
        i = pl.program_id(1)
        lane = lax.broadcasted_iota(jnp.int32, (1, LANES), 1)
        low = lane < HEAD_DIM

        @pl.when(i == 0)
        def _():
            place = _bias_placement(1)

            def rows_ka(r, carry):
                r0 = pl.multiple_of(r * tb, tb)
                bias = _mm(_bias_lanes(-fc_ref[pl.ds(r0, tb), :]), place).astype(BF16)
                for h in range(N_HEADS):
                    kp = k_ref[pl.ds(r0, tb), (h // 2) * LANES : (h // 2 + 1) * LANES] * scale
                    ka_sc[h, pl.ds(r0, tb), :] = _augment(kp, h, bias, 0)
                return carry

            lax.fori_loop(0, nq, rows_ka, 0)

        q0 = pl.multiple_of(i * tb, tb)
        bias = _mm(_bias_lanes(fc_ref[pl.ds(q0, tb), :]), _bias_placement(0)).astype(BF16)
        for h in range(N_HEADS):
            qa_sc[h] = _augment(q_ref[:, (h // 2) * LANES : (h // 2 + 1) * LANES], h, bias, 1)
        m_sc[...] = jnp.full(m_sc.shape, -jnp.inf, F32)
        l_sc[...] = jnp.zeros_like(l_sc)
        acc_sc[...] = jnp.zeros_like(acc_sc)
        causal = lax.broadcasted_iota(jnp.int32, (tb, tb), 1) <= lax.broadcasted_iota(jnp.int32, (tb, tb), 0)

        def step(j, masked):
            c0 = pl.multiple_of(j * tb, tb)
            for p in range(N_PAIRS):
                vb = v_ref[pl.ds(c0, tb), p * LANES : (p + 1) * LANES]
                pv, al = [], []
                for hh in range(2):
                    h = 2 * p + hh
                    s = _mm_nt(qa_sc[h], ka_sc[h, pl.ds(c0, tb), :])
                    if masked:
                        s = jnp.where(causal, s, -jnp.inf)
                    m_old = m_sc[h]
                    m_new = jnp.maximum(m_old, jnp.max(s, axis=1, keepdims=True))
                    alpha = jnp.exp(m_old - m_new)
                    pe = jnp.exp(s - jnp.concatenate([m_new] * (tb // LANES), axis=1))
                    l_sc[h] = alpha * l_sc[h] + jnp.sum(pe, axis=1, keepdims=True)
                    m_sc[h] = m_new
                    pv.append(_mm(pe.astype(BF16), vb))
                    al.append(alpha)
                acc_sc[p] = jnp.where(low, al[0], al[1]) * acc_sc[p] + jnp.where(low, pv[0], pv[1])

        def loop_body(j, carry):
            step(j, False)
            return carry

        lax.fori_loop(0, i, loop_body, 0)
        step(i, True)
        st = jnp.zeros((tb, LANES), F32)
        for p in range(N_PAIRS):
            lp = jnp.where(low, l_sc[2 * p], l_sc[2 * p + 1])
            o_ref[:, p * LANES : (p + 1) * LANES] = (acc_sc[p] / lp).astype(BF16)
            for h in (2 * p, 2 * p + 1):
                st = jnp.where(lane == h, m_sc[h] + jnp.log(l_sc[h]), st)
        st_ref[...] = st

    return pl.pallas_call(
        body,
        name="attn_fwd",
        grid=(n_seq, nq),
        in_specs=[
            pl.BlockSpec((tb, ATTN_WIDTH), lambda s, i: (s * nq + i, 0)),
            pl.BlockSpec((S, ATTN_WIDTH), lambda s, i: (s, 1)),
            pl.BlockSpec((S, ATTN_WIDTH), lambda s, i: (s, 2)),
            pl.BlockSpec((S, LANES), lambda s, i: (s, 0)),
        ],
        out_specs=[
            pl.BlockSpec((tb, ATTN_WIDTH), lambda s, i: (s * nq + i, 0)),
            pl.BlockSpec((tb, LANES), lambda s, i: (s * nq + i, 0)),
        ],
        out_shape=[jax.ShapeDtypeStruct((T, ATTN_WIDTH), BF16), jax.ShapeDtypeStruct((T, LANES), F32)],
        scratch_shapes=[
            pltpu.VMEM((N_HEADS, tb, LANES), BF16),
            pltpu.VMEM((N_HEADS, S, LANES), BF16),
            pltpu.VMEM((N_HEADS, tb, LANES), F32),
            pltpu.VMEM((N_HEADS, tb, LANES), F32),
            pltpu.VMEM((N_PAIRS, tb, LANES), F32),
        ],
        compiler_params=_params(("parallel", "arbitrary")),
    )(qkv, qkv, qkv, fcol)


def _mix_out(a, p3, gates, x, w_ao, w_po, w_out):
    T = x.shape[0]
    tm = ROW_TILE

    def body(a_ref, p3_ref, gt_ref, x_ref, wao_ref, wpo_ref, wout_ref, mg_ref, x1_ref, ay_ref, py_ref):
        ay = _mm(a_ref[...], _whole_cols(wao_ref))
        py = _mm(p3_ref[...], _whole_cols(wpo_ref))
        ay_ref[...] = ay.astype(BF16)
        py_ref[...] = py.astype(BF16)
        sp = _sigmoid(gt_ref[:, :D_MODEL].astype(F32))
        sa = _sigmoid(gt_ref[:, D_MODEL:].astype(F32))
        mb = (sp * py + sa * ay).astype(BF16)
        mg_ref[...] = mb
        x1_ref[...] = x_ref[...] + _mm(mb, wout_ref[...])

    row = lambda n: pl.BlockSpec((tm, n), lambda i: (i, 0))
    return pl.pallas_call(
        body,
        name="mix_out",
        grid=(T // tm,),
        in_specs=[
            row(ATTN_WIDTH), row(POOL_WIDTH), row(2 * D_MODEL), row(D_MODEL),
            _const_spec(w_ao.shape), _const_spec(w_po.shape), _const_spec(w_out.shape),
        ],
        out_specs=[row(D_MODEL), row(D_MODEL), row(D_MODEL), row(D_MODEL)],
        out_shape=[
            jax.ShapeDtypeStruct((T, D_MODEL), BF16), jax.ShapeDtypeStruct((T, D_MODEL), F32),
            jax.ShapeDtypeStruct((T, D_MODEL), BF16), jax.ShapeDtypeStruct((T, D_MODEL), BF16),
        ],
        compiler_params=_params(("parallel",)),
    )(a, p3, gates, x, w_ao, w_po, w_out)


def _ffn_fwd(x1, g2, gf, tgt, w_gate_t, w_up_t, w_down):
    T = x1.shape[0]
    tm = min(T, FF_ROW_TILE)
    nt = T // tm
    nc = D_FF // FF_CHUNK

    def body(x1_ref, g2_ref, gf_ref, tg_ref, wg_ref, wu_ref, wd_ref, h2_ref, gate_ref, up_ref, act_ref, dx2_ref, loss_ref, dgf_ref):
        x1v = x1_ref[...]
        h2, _, _ = _rms_fwd(x1v, g2_ref[...])
        h2b = h2.astype(BF16)
        h2_ref[...] = h2b
        for c in range(nc):
            sl = slice(c * FF_CHUNK, (c + 1) * FF_CHUNK)
            gate = _mm_nt(h2b, wg_ref[sl, :])
            up = _mm_nt(h2b, wu_ref[sl, :])
            gate_ref[:, sl] = gate.astype(BF16)
            up_ref[:, sl] = up.astype(BF16)
            act_ref[:, sl] = (gate * _sigmoid(gate) * up).astype(BF16)
        acc = x1v + _mm(act_ref[...], wd_ref[...])
        gfv = gf_ref[...]
        y, xh, r = _rms_fwd(acc, gfv)
        err = y - tg_ref[...]
        part = 0.5 * jnp.sum(jnp.mean(err * err, axis=-1, keepdims=True), axis=0, keepdims=True)
        dx2, dgrow = _rms_bwd(err * (1.0 / D_MODEL), xh, r, gfv)
        dx2_ref[...] = dx2

        @pl.when(pl.program_id(0) == 0)
        def _():
            dgf_ref[...] = jnp.zeros_like(dgf_ref)
            loss_ref[...] = jnp.zeros_like(loss_ref)

        dgf_ref[...] += jnp.sum(dgrow, axis=0, keepdims=True)
        loss_ref[...] += jnp.broadcast_to(part, loss_ref.shape)

    row = lambda n: pl.BlockSpec((tm, n), lambda i: (i, 0))
    return pl.pallas_call(
        body,
        name="ffn_fwd",
        grid=(nt,),
        in_specs=[
            row(D_MODEL), _const_spec((1, D_MODEL)), _const_spec((1, D_MODEL)), row(D_MODEL),
            _const_spec(w_gate_t.shape), _const_spec(w_up_t.shape), _const_spec(w_down.shape),
        ],
        out_specs=[
            row(D_MODEL), row(D_FF), row(D_FF), row(D_FF), row(D_MODEL),
            pl.BlockSpec((8, LANES), lambda i: (0, 0)),
            pl.BlockSpec((1, D_MODEL), lambda i: (0, 0)),
        ],
        out_shape=[
            jax.ShapeDtypeStruct((T, D_MODEL), BF16),
            jax.ShapeDtypeStruct((T, D_FF), BF16),
            jax.ShapeDtypeStruct((T, D_FF), BF16),
            jax.ShapeDtypeStruct((T, D_FF), BF16),
            jax.ShapeDtypeStruct((T, D_MODEL), F32),
            jax.ShapeDtypeStruct((8, LANES), F32),
            jax.ShapeDtypeStruct((1, D_MODEL), F32),
        ],
        compiler_params=_params(("arbitrary",)),
    )(x1, g2, gf, tgt, w_gate_t, w_up_t, w_down)


def _ffn_bwd(dx2, gate, up, x1, g2, w_gate_t, w_up_t, w_down):
    T = x1.shape[0]
    tm = min(T, FF_ROW_TILE)
    nc = D_FF // FF_CHUNK

    def body(dx2_ref, gate_ref, up_ref, x1_ref, g2_ref, wg_ref, wu_ref, wd_ref, dgate_ref, dup_ref, dx1_ref, dg2_ref):
        dx2v = dx2_ref[...]
        dx2b = dx2v.astype(BF16)
        for c in range(nc):
            sl = slice(c * FF_CHUNK, (c + 1) * FF_CHUNK)
            dact = _mm_nt(dx2b, wd_ref[sl, :])
            gate = gate_ref[:, sl].astype(F32)
            sg = _sigmoid(gate)
            silu = gate * sg
            dgate = (dact * up_ref[:, sl].astype(F32) * (sg * (1.0 + gate * (1.0 - sg)))).astype(BF16)
            dup = (dact * silu).astype(BF16)
            dgate_ref[:, sl] = dgate
            dup_ref[:, sl] = dup
        dh2 = _mm(dgate_ref[...], wg_ref[...]) + _mm(dup_ref[...], wu_ref[...])
        g2v = g2_ref[...]
        _, xh, r = _rms_fwd(x1_ref[...], g2v)
        dxn, dgrow = _rms_bwd(dh2, xh, r, g2v)
        dx1_ref[...] = dx2v + dxn

        @pl.when(pl.program_id(0) == 0)
        def _():
            dg2_ref[...] = jnp.zeros_like(dg2_ref)

        dg2_ref[...] += jnp.sum(dgrow, axis=0, keepdims=True)

    row = lambda n: pl.BlockSpec((tm, n), lambda i: (i, 0))
    return pl.pallas_call(
        body,
        name="ffn_bwd",
        grid=(T // tm,),
        in_specs=[
            row(D_MODEL), row(D_FF), row(D_FF), row(D_MODEL), _const_spec((1, D_MODEL)),
            _const_spec(w_gate_t.shape), _const_spec(w_up_t.shape), _const_spec(w_down.shape),
        ],
        out_specs=[row(D_FF), row(D_FF), row(D_MODEL), pl.BlockSpec((1, D_MODEL), lambda i: (0, 0))],
        out_shape=[
            jax.ShapeDtypeStruct((T, D_FF), BF16),
            jax.ShapeDtypeStruct((T, D_FF), BF16),
            jax.ShapeDtypeStruct((T, D_MODEL), F32),
            jax.ShapeDtypeStruct((1, D_MODEL), F32),
        ],
        compiler_params=_params(("arbitrary",), VMEM_LIMIT_MAX),
    )(dx2, gate, up, x1, g2, w_gate_t, w_up_t, w_down)


def _mix_bwd(dx1, gates, pool_y, attn_y, p2, scale, w_out, w_ao, w_po, token):
    T = dx1.shape[0]
    tm = ROW_TILE

    def body(dx1_ref, gt_ref, py_ref, ay_ref, p2_ref, sc_ref, wout_ref, wao_ref, wpo_ref, token_ref, dgt_ref, dpy_ref, day_ref, da_ref, dp2_ref, dsc_ref):
        dm = _mm_nt(dx1_ref[...].astype(BF16), wout_ref[...])
        sp = _sigmoid(gt_ref[:, :D_MODEL].astype(F32))
        sa = _sigmoid(gt_ref[:, D_MODEL:].astype(F32))
        dgt_ref[:, :D_MODEL] = (dm * py_ref[...].astype(F32) * (sp * (1.0 - sp))).astype(BF16)
        dgt_ref[:, D_MODEL:] = (dm * ay_ref[...].astype(F32) * (sa * (1.0 - sa))).astype(BF16)
        dpy = (dm * sp).astype(BF16)
        day = (dm * sa).astype(BF16)
        dpy_ref[...] = dpy
        day_ref[...] = day
        da_ref[...] = _mm_nt(day, _whole_cols(wao_ref)).astype(BF16)
        dp3 = _mm_nt(dpy, _whole_cols(wpo_ref))
        dp2_ref[...] = (dp3 * sc_ref[...]).astype(BF16)

        @pl.when(pl.program_id(0) == 0)
        def _():
            dsc_ref[...] = jnp.zeros_like(dsc_ref)

        dsc_ref[...] += jnp.sum(dp3 * p2_ref[...], axis=0, keepdims=True)

    row = lambda n: pl.BlockSpec((tm, n), lambda i: (i, 0))
    return pl.pallas_call(
        body,
        name="mix_bwd",
        grid=(T // tm,),
        in_specs=[
            row(D_MODEL), row(2 * D_MODEL), row(D_MODEL), row(D_MODEL), row(POOL_WIDTH), _const_spec((1, POOL_WIDTH)),
            _const_spec(w_out.shape), _const_spec(w_ao.shape), _const_spec(w_po.shape), _HBM,
        ],
        out_specs=[row(2 * D_MODEL), row(D_MODEL), row(D_MODEL), row(ATTN_WIDTH), row(POOL_WIDTH), pl.BlockSpec((1, POOL_WIDTH), lambda i: (0, 0))],
        out_shape=[
            jax.ShapeDtypeStruct((T, 2 * D_MODEL), BF16),
            jax.ShapeDtypeStruct((T, D_MODEL), BF16),
            jax.ShapeDtypeStruct((T, D_MODEL), BF16),
            jax.ShapeDtypeStruct((T, ATTN_WIDTH), BF16),
            jax.ShapeDtypeStruct((T, POOL_WIDTH), BF16),
            jax.ShapeDtypeStruct((1, POOL_WIDTH), F32),
        ],
        compiler_params=_params(("arbitrary",)),
    )(dx1, gates, pool_y, attn_y, p2, scale, w_out, w_ao, w_po, token)


def _pool_bwd(dp2, pm, mix_b, token, n_seq, S):
    T = n_seq * S

    def body(dp2_ref, pm_ref, mix_ref, token_ref, du_ref, dmix_ref):
        g = pl.program_id(0)
        dp2v = dp2_ref[...]
        dpm = _mm_nt(dp2v, mix_ref[...])
        row = lax.broadcasted_iota(jnp.int32, dpm.shape, 0)
        w = _window_pick(g, 2.0, 4.0, 8.0, 16.0)
        e = dpm / jnp.minimum((row + 1).astype(F32), w)

        def ahead(a, k):
            return jnp.where(row < S - k, pltpu.roll(a, S - k, 0), 0.0)

        r2 = e + ahead(e, 1)
        r4 = r2 + ahead(r2, 2)
        r8 = r4 + ahead(r4, 4)
        r16 = r8 + ahead(r8, 8)
        du_ref[...] = (_window_pick(g, r2, r4, r8, r16) - dpm).astype(BF16)

        @pl.when(pl.program_id(1) == 0)
        def _():
            dmix_ref[...] = jnp.zeros_like(dmix_ref)

        dmix_ref[...] += _mm_tn(pm_ref[...], dp2v)

    grp = pl.BlockSpec((S, GROUP_DIM), lambda g, s: (s, g))
    mixs = pl.BlockSpec((None, GROUP_DIM, GROUP_DIM), lambda g, s: (g, 0, 0))
    return pl.pallas_call(
        body,
        name="pool_bwd",
        grid=(len(POOL_WINDOWS), n_seq),
        in_specs=[grp, grp, mixs, _HBM],
        out_specs=[grp, mixs],
        out_shape=[jax.ShapeDtypeStruct((T, POOL_WIDTH), BF16), jax.ShapeDtypeStruct((len(POOL_WINDOWS), GROUP_DIM, GROUP_DIM), F32)],
        compiler_params=_params(("parallel", "arbitrary")),
    )(dp2, pm, mix_b, token)


def _attn_bwd(qkv, da, a, fcol, lse, n_seq, S):
    T = n_seq * S
    tb = ATTN_BLOCK
    nb = S // tb
    scale = HEAD_DIM ** -0.5

    def body(q_ref, k_ref, v_ref, do_ref, o_ref, fc_ref, st_ref, dq_ref, dk_ref, dv_ref, dfk_ref, dfq_ref,
             qa_sc, doa_sc, qat_sc, doat_sc, dq_acc, ka_sc, va_sc, dkt_sc, dvt_sc):
        j = pl.program_id(1)
        lane = lax.broadcasted_iota(jnp.int32, (1, LANES), 1)
        low = lane < HEAD_DIM

        @pl.when(j == 0)
        def _():
            dq_acc[...] = jnp.zeros_like(dq_acc)
            place = _bias_placement(0)

            def rows_q(i, carry):
                r0 = pl.multiple_of(i * tb, tb)
                delta = jnp.zeros((tb, LANES), F32)
                for h in range(N_HEADS):
                    pair = slice((h // 2) * LANES, (h // 2 + 1) * LANES)
                    prod = do_ref[pl.ds(r0, tb), pair].astype(F32) * o_ref[pl.ds(r0, tb), pair].astype(F32)
                    head = (lane >= HEAD_DIM * (h % 2)) & (lane < HEAD_DIM * (h % 2 + 1))
                    delta = jnp.where(lane == h, jnp.sum(jnp.where(head, prod, 0.0), axis=1, keepdims=True), delta)
                cq = fc_ref[pl.ds(r0, tb), :] - st_ref[pl.ds(r0, tb), :]
                q_bias = _mm(_bias_lanes(cq), place).astype(BF16)
                do_bias = _mm(_bias_lanes(-delta), place).astype(BF16)
                for h in range(N_HEADS):
                    pair = slice((h // 2) * LANES, (h // 2 + 1) * LANES)
                    qa = _augment(q_ref[pl.ds(r0, tb), pair], h, q_bias, 1)
                    doa = _augment(do_ref[pl.ds(r0, tb), pair], h, do_bias, None)
                    qa_sc[h, pl.ds(r0, tb), :] = qa
                    doa_sc[h, pl.ds(r0, tb), :] = doa
                    qat_sc[h, i] = qa.astype(F32).T.astype(BF16)
                    doat_sc[h, i] = doa.astype(F32).T.astype(BF16)
                return carry

            lax.fori_loop(0, nb, rows_q, 0)

        c0 = pl.multiple_of(j * tb, tb)
        k_bias = _mm(_bias_lanes(-fc_ref[pl.ds(c0, tb), :]), _bias_placement(1)).astype(BF16)
        for h in range(N_HEADS):
            pair = slice((h // 2) * LANES, (h // 2 + 1) * LANES)
            ka_sc[h] = _augment(k_ref[:, pair] * scale, h, k_bias, 0)
            va_sc[h] = _augment(v_ref[:, pair], h, None, 0)
        dkt_sc[...] = jnp.zeros_like(dkt_sc)
        dvt_sc[...] = jnp.zeros_like(dvt_sc)
        causal = lax.broadcasted_iota(jnp.int32, (tb, tb), 1) <= lax.broadcasted_iota(jnp.int32, (tb, tb), 0)

        def step(i, masked):
            r0 = pl.multiple_of(i * tb, tb)
            for h in range(N_HEADS):
                s = _mm_nt(qa_sc[h, pl.ds(r0, tb), :], ka_sc[h])
                if masked:
                    s = jnp.where(causal, s, -jnp.inf)
                pr = jnp.exp(s)
                dvt_sc[h] += _mm(doat_sc[h, i], pr.astype(BF16))
                dsb = (pr * _mm_nt(doa_sc[h, pl.ds(r0, tb), :], va_sc[h])).astype(BF16)
                dkt_sc[h] += _mm(qat_sc[h, i], dsb)
                dq_acc[h, pl.ds(r0, tb), :] += _mm(dsb, ka_sc[h])

        step(j, True)

        def loop_body(i, carry):
            step(i, False)
            return carry

        lax.fori_loop(j + 1, nb, loop_body, 0)
        dfk = jnp.zeros((tb, LANES), F32)
        for p in range(N_PAIRS):
            dk = [dkt_sc[2 * p + hh].T for hh in range(2)]
            dv = [dvt_sc[2 * p + hh].T for hh in range(2)]
            dk_ref[:, p * LANES : (p + 1) * LANES] = (jnp.where(low, dk[0], dk[1]) * scale).astype(BF16)
            dv_ref[:, p * LANES : (p + 1) * LANES] = jnp.where(low, dv[0], dv[1]).astype(BF16)
            for hh in range(2):
                b = HEAD_DIM * (1 - hh) + 3
                dfk = jnp.where(lane == 2 * p + hh, -dk[hh][:, b : b + 1], dfk)
        dfk_ref[...] = dfk

        @pl.when(j == nb - 1)
        def _():
            def rows_dq(i, carry):
                r0 = pl.multiple_of(i * tb, tb)
                dfq = jnp.zeros((tb, LANES), F32)
                for p in range(N_PAIRS):
                    parts = [dq_acc[2 * p + hh, pl.ds(r0, tb), :] for hh in range(2)]
                    dq_ref[pl.ds(r0, tb), p * LANES : (p + 1) * LANES] = jnp.where(low, parts[0], parts[1]).astype(BF16)
                    for hh in range(2):
                        b = HEAD_DIM * (1 - hh)
                        dfq = jnp.where(lane == 2 * p + hh, parts[hh][:, b : b + 1], dfq)
                dfq_ref[pl.ds(r0, tb), :] = dfq
                return carry

            lax.fori_loop(0, nb, rows_dq, 0)

    seq = lambda w, col: pl.BlockSpec((S, w), lambda s, j: (s, col))
    seq_in = lambda w, col: pl.BlockSpec((S, w), lambda s, j: (s, col), pipeline_mode=pl.Buffered(1))
    blk = lambda w, col: pl.BlockSpec((tb, w), lambda s, j: (s * nb + j, col))
    return pl.pallas_call(
        body,
        name="attn_bwd",
        grid=(n_seq, nb),
        in_specs=[seq(ATTN_WIDTH, 0), blk(ATTN_WIDTH, 1), blk(ATTN_WIDTH, 2), seq(ATTN_WIDTH, 0), seq(ATTN_WIDTH, 0), seq_in(LANES, 0), seq_in(LANES, 0)],
        out_specs=[seq(ATTN_WIDTH, 0), blk(ATTN_WIDTH, 0), blk(ATTN_WIDTH, 0), blk(LANES, 0), seq(LANES, 0)],
        out_shape=[
            jax.ShapeDtypeStruct((T, ATTN_WIDTH), BF16),
            jax.ShapeDtypeStruct((T, ATTN_WIDTH), BF16),
            jax.ShapeDtypeStruct((T, ATTN_WIDTH), BF16),
            jax.ShapeDtypeStruct((T, LANES), F32),
            jax.ShapeDtypeStruct((T, LANES), F32),
        ],
        scratch_shapes=[
            pltpu.VMEM((N_HEADS, S, LANES), BF16),
            pltpu.VMEM((N_HEADS, S, LANES), BF16),
            pltpu.VMEM((N_HEADS, nb, LANES, tb), BF16),
            pltpu.VMEM((N_HEADS, nb, LANES, tb), BF16),
            pltpu.VMEM((N_HEADS, S, LANES), F32),
            pltpu.VMEM((N_HEADS, tb, LANES), BF16),
            pltpu.VMEM((N_HEADS, tb, LANES), BF16),
            pltpu.VMEM((N_HEADS, LANES, tb), F32),
            pltpu.VMEM((N_HEADS, LANES, tb), F32),
        ],
        compiler_params=_params(("parallel", "arbitrary"), VMEM_LIMIT_MAX),
    )(qkv, qkv, qkv, da, a, fcol, lse)


def _forget_bwd(dfk, dfq, fl, b_pad, n_seq, S):
    def body(df_ref, dfq_ref, fl_ref, b_ref, dfl_ref, db_ref):
        t = (df_ref[...] + dfq_ref[...]).T
        lane = lax.broadcasted_iota(jnp.int32, t.shape, 1)
        k = 1
        while k < S:
            t = t + jnp.where(lane < S - k, pltpu.roll(t, S - k, 1), 0.0)
            k *= 2
        dfl = t.T * _sigmoid(-(fl_ref[...] + b_ref[...]))
        dfl_ref[...] = dfl.astype(BF16)

        @pl.when(pl.program_id(0) == 0)
        def _():
            db_ref[...] = jnp.zeros_like(db_ref)

        db_ref[...] += jnp.sum(dfl, axis=0, keepdims=True)

    return pl.pallas_call(
        body,
        name="forget_bwd",
        grid=(n_seq,),
        in_specs=[
            pl.BlockSpec((S, LANES), lambda s: (s, 0)),
            pl.BlockSpec((S, LANES), lambda s: (s, 0)),
            pl.BlockSpec((S, FL_PAD), lambda s: (s, 0)),
            _const_spec((1, FL_PAD)),
        ],
        out_specs=[pl.BlockSpec((S, FL_PAD), lambda s: (s, 0)), pl.BlockSpec((1, FL_PAD), lambda s: (0, 0))],
        out_shape=[jax.ShapeDtypeStruct((n_seq * S, FL_PAD), BF16), jax.ShapeDtypeStruct((1, FL_PAD), F32)],
        compiler_params=_params(("arbitrary",)),
    )(dfk, dfq, fl, b_pad)


def _in_proj_bwd(du, dq, dk, dv, dfl, dgates, x, dx1, g1, w_uqkv, w_fl, w_g, token):
    T = x.shape[0]
    tm = ROW_TILE

    def body(du_ref, dq_ref, dk_ref, dv_ref, dfl_ref, dgt_ref, x_ref, dx1_ref, g_ref, wa_ref, wf_ref, wg_ref, token_ref, dx_ref, dg_ref):
        dz = jnp.concatenate([du_ref[...], dq_ref[...], dk_ref[...], dv_ref[...]], axis=1)
        dh = _mm_nt(dz, wa_ref[...]) + _mm_nt(dgt_ref[...], wg_ref[...]) + _mm_nt(dfl_ref[...], wf_ref[...])
        gv = g_ref[...]
        _, xh, r = _rms_fwd(x_ref[...], gv)
        dxn, dgrow = _rms_bwd(dh, xh, r, gv)
        dx_ref[...] = dx1_ref[...] + dxn

        @pl.when(pl.program_id(0) == 0)
        def _():
            dg_ref[...] = jnp.zeros_like(dg_ref)

        dg_ref[...] += jnp.sum(dgrow, axis=0, keepdims=True)

    row = lambda n: pl.BlockSpec((tm, n), lambda i: (i, 0))
    return pl.pallas_call(
        body,
        name="in_proj_bwd",
        grid=(T // tm,),
        in_specs=[
            row(512), row(512), row(512), row(512), row(FL_PAD), row(2 * D_MODEL), row(D_MODEL), row(D_MODEL), _const_spec((1, D_MODEL)),
            _const_spec(w_uqkv.shape), _const_spec(w_fl.shape), _const_spec(w_g.shape), _HBM,
        ],
        out_specs=[row(D_MODEL), pl.BlockSpec((1, D_MODEL), lambda i: (0, 0))],
        out_shape=[jax.ShapeDtypeStruct((T, D_MODEL), F32), jax.ShapeDtypeStruct((1, D_MODEL), F32)],
        compiler_params=_params(("arbitrary",)),
    )(du, dq, dk, dv, dfl, dgates, x, dx1, g1, w_uqkv, w_fl, w_g, token)


def _pick_block(n):
    for b in (1024, 512, 1408, 256, 128):
        if n % b == 0:
            return b
    raise ValueError(n)


def _matmul_tn(a, b, name, col_chunks=False):
    T, K = a.shape
    N = b.shape[1]
    bt, bk, bn = min(T, DW_TOKENS), _pick_block(K), _pick_block(N)
    nt = T // bt
    c = N // N_DEV
    assert not col_chunks or (bn == N and c % LANES == 0)

    def body(a_ref, b_ref, o_ref, acc):
        @pl.when(pl.program_id(2) == 0)
        def _():
            acc[...] = jnp.zeros_like(acc)

        acc[...] += _mm_tn(a_ref[...].astype(BF16), b_ref[...].astype(BF16))

        @pl.when(pl.program_id(2) == nt - 1)
        def _():
            if col_chunks:
                for d in range(N_DEV):
                    o_ref[d] = acc[:, d * c : (d + 1) * c].astype(BF16)
            else:
                o_ref[...] = acc[...].astype(BF16)

    if col_chunks:
        out_spec, out_shape = pl.BlockSpec((N_DEV, bk, c), lambda k, n, t: (0, k, 0)), (N_DEV, K, c)
    else:
        out_spec, out_shape = pl.BlockSpec((bk, bn), lambda k, n, t: (k, n)), (K, N)
    return pl.pallas_call(
        body,
        name=name,
        grid=(K // bk, N // bn, nt),
        in_specs=[pl.BlockSpec((bt, bk), lambda k, n, t: (t, k)), pl.BlockSpec((bt, bn), lambda k, n, t: (t, n))],
        out_specs=out_spec,
        out_shape=jax.ShapeDtypeStruct(out_shape, BF16),
        scratch_shapes=[pltpu.VMEM((bk, bn), F32)],
        compiler_params=_params(("parallel", "parallel", "arbitrary")),
    )(a, b)


W_IN_A = POOL_WIDTH + 3 * ATTN_WIDTH
W_IN_SHARD = (W_IN_A + N_HEADS + 2 * D_MODEL) // N_DEV
_W_IN_PIECES = ((0, W_IN_A), (W_IN_A, W_IN_A + N_HEADS), (W_IN_A + N_HEADS, W_IN_A + N_HEADS + 2 * D_MODEL))


def _w_in_segments(d):
    lo, hi = d * W_IN_SHARD, (d + 1) * W_IN_SHARD
    out = []
    for p, (a, b) in enumerate(_W_IN_PIECES):
        s, e = max(lo, a), min(hi, b)
        if s < e:
            out.append((p, s - a, s - lo, e - s))
    return out


def _w_in_pieces(gathered, tails):
    tm = ROW_TILE // 2
    tail_rows = tm // LANES
    aligned = W_IN_SHARD - 1

    def body(g_ref, t_ref, wa_ref, wf_ref, wg_ref):
        outs = (wa_ref, wf_ref, wg_ref)
        wf_ref[...] = jnp.zeros_like(wf_ref)
        diagonal = lax.broadcasted_iota(jnp.int32, (LANES, LANES), 0) == lax.broadcasted_iota(jnp.int32, (LANES, LANES), 1)
        for d in range(N_DEV):
            for p, at, frm, n in _w_in_segments(d):
                m = min(n, aligned - frm)
                if m > 0:
                    outs[p][:, at : at + m] = g_ref[d, :, frm : frm + m]
                if frm + n == W_IN_SHARD:
                    column = [
                        jnp.sum(jnp.where(diagonal, jnp.broadcast_to(t_ref[d, k : k + 1, :], (LANES, LANES)), 0.0), axis=1, keepdims=True)
                        for k in range(tail_rows)
                    ]
                    outs[p][:, at + n - 1 : at + n] = jnp.concatenate(column, axis=0).astype(outs[p].dtype)

    return pl.pallas_call(
        body,
        name="w_in_pieces",
        grid=(D_MODEL // tm,),
        in_specs=[
            pl.BlockSpec((N_DEV, tm, aligned), lambda i: (0, i, 0)),
            pl.BlockSpec((N_DEV, None, tail_rows, LANES), lambda i: (0, i, 0, 0)),
        ],
        out_specs=[pl.BlockSpec((tm, W_IN_A), lambda i: (i, 0)), pl.BlockSpec((tm, FL_PAD), lambda i: (i, 0)), pl.BlockSpec((tm, 2 * D_MODEL), lambda i: (i, 0))],
        out_shape=[
            jax.ShapeDtypeStruct((D_MODEL, W_IN_A), gathered.dtype),
            jax.ShapeDtypeStruct((D_MODEL, FL_PAD), gathered.dtype),
            jax.ShapeDtypeStruct((D_MODEL, 2 * D_MODEL), gathered.dtype),
        ],
        compiler_params=_params(("parallel",)),
    )(gathered, tails.reshape(N_DEV, D_MODEL // tm, tail_rows, LANES))


def _dw_in(h, du, dq, dk, dv, dfl, dgates, token):
    T = h.shape[0]
    bt, bk = min(T, DW_TOKENS // 2), 512
    nt = T // bt
    pieces = (du, dq, dk, dv, dfl, dgates)
    offs = [0]
    for p in pieces:
        offs.append(offs[-1] + p.shape[1])

    aligned = W_IN_SHARD - 1
    tail_rows = bk // LANES

    def body(h_ref, *rest):
        refs, o_ref, t_ref, acc = rest[: len(pieces)], rest[-3], rest[-2], rest[-1]

        @pl.when(pl.program_id(1) == 0)
        def _():
            acc[...] = jnp.zeros_like(acc)

        ht = h_ref[...].T
        for ref, at in zip(refs, offs):
            acc[:, at : at + ref.shape[1]] += _mm(ht, ref[...])

        @pl.when(pl.program_id(1) == nt - 1)
        def _():
            starts = (0, W_IN_A, W_IN_A + FL_PAD)
            diagonal = lax.broadcasted_iota(jnp.int32, (LANES, LANES), 0) == lax.broadcasted_iota(jnp.int32, (LANES, LANES), 1)
            for d in range(N_DEV):
                for p, at, to, n in _w_in_segments(d):
                    m = min(n, aligned - to)
                    if m > 0:
                        o_ref[d, :, to : to + m] = acc[:, starts[p] + at : starts[p] + at + m].astype(BF16)
                    if to + n == W_IN_SHARD:
                        last = starts[p] + at + n - 1
                        column = acc[:, last : last + 1].astype(BF16).astype(F32)
                        for k in range(tail_rows):
                            rows = jnp.broadcast_to(column[k * LANES : (k + 1) * LANES], (LANES, LANES))
                            t_ref[d, k : k + 1, :] = jnp.sum(jnp.where(diagonal, rows, 0.0), axis=0, keepdims=True)

    main, tails = pl.pallas_call(
        body,
        name="dw_in",
        grid=(D_MODEL // bk, nt),
        in_specs=[pl.BlockSpec((bt, bk), lambda k, t: (t, k))] + [pl.BlockSpec((bt, p.shape[1]), lambda k, t: (t, 0)) for p in pieces] + [_HBM],
        out_specs=[
            pl.BlockSpec((N_DEV, bk, aligned), lambda k, t: (0, k, 0)),
            pl.BlockSpec((N_DEV, None, tail_rows, LANES), lambda k, t: (0, k, 0, 0)),
        ],
        out_shape=[
            jax.ShapeDtypeStruct((N_DEV, D_MODEL, aligned), BF16),
            jax.ShapeDtypeStruct((N_DEV, D_MODEL // bk, tail_rows, LANES), F32),
        ],
        scratch_shapes=[pltpu.VMEM((bk, offs[-1]), F32)],
        compiler_params=_params(("parallel", "arbitrary")),
    )(h, *pieces, token)
    return main, tails.reshape(N_DEV, D_MODEL // LANES, LANES)


def _position():
    return lax.axis_index("x"), lax.axis_index("y"), lax.axis_index("c")


_HBM = pl.BlockSpec(memory_space=pl.ANY)


def _all_gather(blocks, name):
    n = len(blocks)
    parts = [(a, q * (b.shape[0] // 4), b.shape[0] // 4) for a, b in enumerate(blocks) if b.shape[0] >= ROW_TILE for q in range(4)]
    parts += [(a, 0, b.shape[0]) for a, b in enumerate(blocks) if b.shape[0] < ROW_TILE]

    def body(*refs):
        xs, outs = refs[:n], refs[n : 2 * n]
        send_sems, recv_sems, local_sems = refs[2 * n :]
        x, y, c = _position()
        me, sibling = (x, y, c), (x, y, 1 - c)
        chips = [(1 - x, y), (x, 1 - y), (1 - x, 1 - y)]

        def rows(u, px, py, pc):
            a, lo, size = parts[u]
            return outs[a].at[4 * px + 2 * py + pc, pl.ds(lo, size)]

        def own(u):
            a, lo, size = parts[u]
            return xs[a].at[pl.ds(lo, size)]

        def copy(u, k, blk, to, src=None):
            return pltpu.make_async_remote_copy(
                src_ref=rows(u, *blk) if src is None else src, dst_ref=rows(u, *blk),
                send_sem=send_sems.at[7 * u + k], recv_sem=recv_sems.at[7 * u + k], device_id=to, device_id_type=MESH,
            )

        first = []
        for u in range(len(parts)):
            first += [copy(u, 1 + j, me, (*chip, c), src=own(u)) for j, chip in enumerate(chips)]
            first.append(copy(u, 0, me, sibling, src=own(u)))
        mine = [pltpu.make_async_copy(xs[a], outs[a].at[4 * x + 2 * y + c], local_sems.at[a]) for a in range(n)]
        for cp in first + mine:
            cp.start()
        passed = []
        for u in range(len(parts)):
            for j, chip in enumerate(chips):
                copy(u, 1 + j, (*chip, c), me).wait_recv()
                passed.append(copy(u, 4 + j, (*chip, c), sibling))
                passed[-1].start()
        for u in range(len(parts)):
            copy(u, 0, sibling, me).wait_recv()
            for j, chip in enumerate(chips):
                copy(u, 4 + j, (*chip, 1 - c), me).wait_recv()
        for cp in first + passed:
            cp.wait_send()
        for cp in mine:
            cp.wait()

    return pl.pallas_call(
        body,
        name=name,
        out_shape=[jax.ShapeDtypeStruct((N_DEV, *b.shape), b.dtype) for b in blocks],
        in_specs=[_HBM] * n,
        out_specs=[_HBM] * n,
        scratch_shapes=[pltpu.SemaphoreType.DMA((7 * len(parts),)), pltpu.SemaphoreType.DMA((7 * len(parts),)), pltpu.SemaphoreType.DMA((n,))],
    )(*blocks)


_SEM = pl.BlockSpec(memory_space=pltpu.SEMAPHORE)
_HBM_ONLY = pl.BlockSpec(memory_space=pltpu.HBM)
_SIDE_EFFECT = pltpu.SideEffectType.DATAFLOW_SIDE_EFFECTING


def _peer(x, y, c, k):
    return (1 - x if k & 4 else x, 1 - y if k & 2 else y, 1 - c if k & 1 else c)


_PEER_BITS = {"gather": range(1, N_DEV), "gather_half": (1, 4, 2, 6), "forward": (4, 2, 6), "scatter": range(1, N_DEV)}
_GATHERS = ("gather", "gather_half")


def _exchange_copies(src_refs, land_refs, send_sems, recv_sems, pattern, receive_side):
    x, y, c = _position()
    me = 4 * x + 2 * y + c
    bits = _PEER_BITS[pattern]
    cps = []
    for j, k in enumerate(bits):
        px, py, pc = _peer(x, y, c, k)
        peer = 4 * px + 2 * py + pc
        for a, (src, land) in enumerate(zip(src_refs, land_refs)):
            to = (px, py, pc)
            if pattern == "forward":
                slot = 4 * px + 2 * py + (1 - c if receive_side else c)
                s, to = land.at[slot], (x, y, 1 - c)
            else:
                s, slot = (src if pattern in _GATHERS else src.at[peer]), (peer if receive_side else me)
            cps.append(pltpu.make_async_remote_copy(
                src_ref=s, dst_ref=land.at[slot],
                send_sem=send_sems.at[len(bits) * a + j], recv_sem=recv_sems.at[len(bits) * a + j],
                device_id=to, device_id_type=MESH,
            ))
    return cps


def _own_copies(src_refs, land_refs, own_sems):
    x, y, c = _position()
    return [
        pltpu.make_async_copy(src, land.at[4 * x + 2 * y + c], own_sems.at[a])
        for a, (src, land) in enumerate(zip(src_refs, land_refs))
    ]


def _exchange_start(srcs, after, name, pattern):
    n = len(srcs)
    m = len(_PEER_BITS[pattern])
    lands = [jax.ShapeDtypeStruct((N_DEV, *s.shape[-2:]), s.dtype) for s in srcs]

    def body(*refs):
        src_refs, land_refs = refs[1 : 1 + n], refs[1 + n : 1 + 2 * n]
        send_sems, recv_sems, own_sems = refs[1 + 2 * n : 4 + 2 * n]
        token = refs[-1]
        if pattern in _GATHERS:
            for cp in _own_copies(src_refs, land_refs, own_sems):
                cp.start()
        for cp in _exchange_copies(src_refs, land_refs, send_sems, recv_sems, pattern, receive_side=False):
            cp.start()
        token[...] = jnp.zeros_like(token)

    hbm = lambda t: pltpu.with_memory_space_constraint(t, pltpu.HBM)
    out = pl.pallas_call(
        body,
        name=name,
        out_shape=(
            pltpu.SemaphoreType.DMA((m * n,)), pltpu.SemaphoreType.DMA((m * n,)), pltpu.SemaphoreType.DMA((n,)),
            *[pltpu.HBM(s.shape, s.dtype) for s in srcs], *[pltpu.HBM(l.shape, l.dtype) for l in lands],
            jax.ShapeDtypeStruct((8, LANES), F32),
        ),
        in_specs=(_HBM, *[_HBM_ONLY] * (2 * n)),
        out_specs=(_SEM, _SEM, _SEM, *[_HBM_ONLY] * (2 * n), pl.BlockSpec(memory_space=pltpu.VMEM)),
        input_output_aliases={1 + i: 3 + i for i in range(2 * n)},
        compiler_params=pltpu.CompilerParams(has_side_effects=_SIDE_EFFECT),
    )(after, *[hbm(s) for s in srcs], *[hbm(lax.empty(l.shape, l.dtype)) for l in lands])
    return out[:3], out[3 : 3 + n], out[3 + n : 3 + 2 * n], out[-1]


def _exchange_wait(sems, srcs, lands, after, name, pattern):
    n = len(srcs)

    def body(*refs):
        src_refs, land_refs = refs[:n], refs[n : 2 * n]
        send_sems, recv_sems, own_sems = refs[2 * n : 2 * n + 3]
        if pattern in _GATHERS:
            for cp in _own_copies(src_refs, land_refs, own_sems):
                cp.wait()
        for cp in _exchange_copies(src_refs, land_refs, send_sems, recv_sems, pattern, receive_side=True):
            cp.wait_send()
            cp.wait_recv()

    out = pl.pallas_call(
        body,
        name=name,
        out_shape=(*[pltpu.HBM(s.shape, s.dtype) for s in srcs], *[pltpu.HBM(l.shape, l.dtype) for l in lands]),
        in_specs=(*[_HBM_ONLY] * (2 * n), _SEM, _SEM, _SEM, _HBM),
        out_specs=tuple([_HBM_ONLY] * (2 * n)),
        input_output_aliases={i: i for i in range(2 * n)},
        compiler_params=pltpu.CompilerParams(has_side_effects=_SIDE_EFFECT),
    )(*srcs, *lands, *sems, after)
    return out[:n], out[n:]


def _gather_forward(sems, srcs, lands, after, name):
    n = len(srcs)
    m = len(_PEER_BITS["forward"])

    def body(*refs):
        src_refs, land_refs = refs[:n], refs[n : 2 * n]
        send_sems, recv_sems, own_sems = refs[2 * n : 2 * n + 3]
        forward_send, forward_recv, token = refs[2 * n + 4], refs[2 * n + 5], refs[-1]
        for cp in _own_copies(src_refs, land_refs, own_sems):
            cp.wait()
        for cp in _exchange_copies(src_refs, land_refs, send_sems, recv_sems, "gather_half", receive_side=True):
            cp.wait_send()
            cp.wait_recv()
        for cp in _exchange_copies(land_refs, land_refs, forward_send, forward_recv, "forward", receive_side=False):
            cp.start()
        token[...] = jnp.zeros_like(token)

    out = pl.pallas_call(
        body,
        name=name,
        out_shape=(
            pltpu.SemaphoreType.DMA((m * n,)), pltpu.SemaphoreType.DMA((m * n,)),
            *[pltpu.HBM(l.shape, l.dtype) for l in lands], jax.ShapeDtypeStruct((8, LANES), F32),
        ),
        in_specs=(*[_HBM_ONLY] * (2 * n), _SEM, _SEM, _SEM, _HBM),
        out_specs=(_SEM, _SEM, *[_HBM_ONLY] * n, pl.BlockSpec(memory_space=pltpu.VMEM)),
        input_output_aliases={n + i: 2 + i for i in range(n)},
        compiler_params=pltpu.CompilerParams(has_side_effects=_SIDE_EFFECT),
    )(*srcs, *lands, *sems, after)
    return out[:2], out[2 : 2 + n], out[-1]


def _forward_wait(sems, lands, after, name):
    n = len(lands)

    def body(*refs):
        land_refs = refs[:n]
        for cp in _exchange_copies(land_refs, land_refs, refs[n], refs[n + 1], "forward", receive_side=True):
            cp.wait_send()
            cp.wait_recv()

    return pl.pallas_call(
        body,
        name=name,
        out_shape=tuple(pltpu.HBM(l.shape, l.dtype) for l in lands),
        in_specs=(*[_HBM_ONLY] * n, _SEM, _SEM, _HBM),
        out_specs=tuple([_HBM_ONLY] * n),
        input_output_aliases={i: i for i in range(n)},
        compiler_params=pltpu.CompilerParams(has_side_effects=_SIDE_EFFECT),
    )(*lands, *sems, after)


def _rows_tile(r):
    return ROW_TILE if r % ROW_TILE == 0 else r


def _adamw(w, g, m, v):
    m = ADAM_B1 * m + (1.0 - ADAM_B1) * g
    v = ADAM_B2 * v + (1.0 - ADAM_B2) * (g * g)
    m_hat = m / (1.0 - ADAM_B1 ** ADAM_STEP)
    v_hat = v / (1.0 - ADAM_B2 ** ADAM_STEP)
    delta = -ADAM_LR * (m_hat / (jnp.sqrt(v_hat) + ADAM_EPS) + ADAM_WD * w)
    return delta, m, v


def _shard_update_direct(parts, chunks, w, m, v, me, name):
    _, r, c = w.shape
    br = _rows_tile(r)

    def body(me_ref, p_ref, own_ref, w_ref, m_ref, v_ref, g_ref, d_ref, nm_ref, nv_ref):
        g = None
        for n in range(N_DEV):
            part = jnp.where(me_ref[0] == n, own_ref[...], p_ref[n]).astype(F32)
            g = part if g is None else g + part
        g_ref[...] = g
        d_ref[...], nm_ref[...], nv_ref[...] = _adamw(w_ref[...], g, m_ref[...], v_ref[...])

    shard = pl.BlockSpec((None, br, c), lambda i, me: (0, i, 0))
    return pl.pallas_call(
        body,
        name=name,
        grid_spec=pltpu.PrefetchScalarGridSpec(
            num_scalar_prefetch=1,
            grid=(r // br,),
            in_specs=[
                pl.BlockSpec((N_DEV, br, c), lambda i, me: (0, i, 0)),
                pl.BlockSpec((None, br, c), lambda i, me: (me[0], i, 0)),
                shard, shard, shard,
            ],
            out_specs=[shard, shard, shard, shard],
        ),
        out_shape=[jax.ShapeDtypeStruct((1, r, c), F32)] * 4,
        compiler_params=_params(("parallel",)),
    )(me, parts, chunks, w, m, v)


def _w_in_update(parts, chunks, tail_parts, tail_chunks, w, m, v, me, name):
    _, r, c = w.shape
    br = _rows_tile(r)
    tail_rows = br // LANES

    def body(me_ref, p_ref, own_ref, tp_ref, town_ref, w_ref, m_ref, v_ref, g_ref, d_ref, nm_ref, nv_ref):
        g = tail = None
        for n in range(N_DEV):
            mine = me_ref[0] == n
            part = jnp.where(mine, own_ref[...], p_ref[n]).astype(F32)
            last = jnp.where(mine, town_ref[...], tp_ref[n])
            g = part if g is None else g + part
            tail = last if tail is None else tail + last
        diagonal = lax.broadcasted_iota(jnp.int32, (LANES, LANES), 0) == lax.broadcasted_iota(jnp.int32, (LANES, LANES), 1)
        column = jnp.concatenate(
            [
                jnp.sum(jnp.where(diagonal, jnp.broadcast_to(tail[k : k + 1, :], (LANES, LANES)), 0.0), axis=1, keepdims=True)
                for k in range(tail_rows)
            ],
            axis=0,
        )
        for lo, hi, grad in ((0, c - 1, g), (c - 1, c, column)):
            g_ref[:, lo:hi] = grad
            d_ref[:, lo:hi], nm_ref[:, lo:hi], nv_ref[:, lo:hi] = _adamw(w_ref[:, lo:hi], grad, m_ref[:, lo:hi], v_ref[:, lo:hi])

    shard = pl.BlockSpec((None, br, c), lambda i, me: (0, i, 0))
    by_block = lambda t: t.reshape(N_DEV, r // br, tail_rows, LANES)
    return pl.pallas_call(
        body,
        name=name,
        grid_spec=pltpu.PrefetchScalarGridSpec(
            num_scalar_prefetch=1,
            grid=(r // br,),
            in_specs=[
                pl.BlockSpec((N_DEV, br, c - 1), lambda i, me: (0, i, 0)),
                pl.BlockSpec((None, br, c - 1), lambda i, me: (me[0], i, 0)),
                pl.BlockSpec((N_DEV, None, tail_rows, LANES), lambda i, me: (0, i, 0, 0)),
                pl.BlockSpec((None, None, tail_rows, LANES), lambda i, me: (me[0], i, 0, 0)),
                shard, shard, shard,
            ],
            out_specs=[shard, shard, shard, shard],
        ),
        out_shape=[jax.ShapeDtypeStruct((1, r, c), F32)] * 4,
        compiler_params=_params(("parallel",)),
    )(me, parts, chunks, by_block(tail_parts), by_block(tail_chunks), w, m, v)


def _small_update(parts, first_rows, ws, ms, vs):
    k = len(ws)

    def unpacked(rows, shape):
        if len(shape) == 2 and shape[1] <= LANES:
            return rows[0:1, : shape[1]]
        if len(shape) == 2:
            return jnp.concatenate([rows[r : r + 1] for r in range(shape[1] // LANES)], axis=1)
        return rows.reshape(shape)

    def body(p_ref, f_ref, *refs):
        w_refs, m_refs, v_refs = refs[:k], refs[k : 2 * k], refs[2 * k : 3 * k]
        outs, loss_ref = refs[3 * k : 7 * k], refs[7 * k]
        g, first = p_ref[0], f_ref[0]
        for n in range(1, N_DEV):
            g = g + p_ref[n]
            first = first + f_ref[n]
        g = jnp.concatenate([g[:8] + first, g[8:]], axis=0)
        off = 0
        for i, (_, rows) in enumerate(_SMALL):
            gi = unpacked(g[off : off + rows], w_refs[i].shape)
            off += rows
            outs[i][...] = gi
            outs[k + i][...], outs[2 * k + i][...], outs[3 * k + i][...] = _adamw(w_refs[i][...], gi, m_refs[i][...], v_refs[i][...])
        loss_ref[...] = g[off : off + 1, 0:1]

    out = pl.pallas_call(
        body,
        name="small_update",
        out_shape=[jax.ShapeDtypeStruct(w.shape, F32) for _ in range(4) for w in ws] + [jax.ShapeDtypeStruct((1, 1), F32)],
        compiler_params=pltpu.CompilerParams(vmem_limit_bytes=VMEM_LIMIT),
    )(parts, first_rows, *ws, *ms, *vs)
    return [out[a * k : (a + 1) * k] for a in range(4)], out[4 * k]


_SHARD_AXIS = (1, 1, 1, 0, 0, 0, 0)
_TRANSPOSED = (False, False, False, False, True, True, False)


def _full_from_gathered(t, axis):
    if axis == 0:
        return t.reshape(N_DEV * t.shape[1], t.shape[2])
    return t


_SMALL = (("norm1_g", 8), ("norm2_g", 8), ("norm_f_g", 8), ("b_forget", 8), ("pool_scale", 8), ("pool_mix", 512))


def _pack_small(vals, loss_row):
    parts = []
    for (name, rows), t in zip(_SMALL, vals):
        f = t.astype(F32).reshape(-1)
        f = jnp.concatenate([f, jnp.zeros((rows * LANES - f.shape[0],), F32)]).reshape(rows, LANES)
        parts.append(f)
    parts.append(loss_row)
    return jnp.concatenate(parts, axis=0)


def _local_grads(x, tgt, g1, g2, gf, b_forget, pool_mix, pool_scale, w_in, fwd_token, out_weights, ffn_weights, ffn_grads_out, out_grads_out, small_grads_out, in_grads_out, norm1_grad_out):
    n_seq, S, _ = x.shape
    T = n_seq * S
    x2 = x.reshape(T, D_MODEL)
    tg2 = tgt.reshape(T, D_MODEL)
    w_uqkv, w_fl, w_g = w_in
    b_pad = jnp.concatenate([b_forget.reshape(1, N_HEADS), jnp.zeros((1, FL_PAD - N_HEADS), F32)], axis=1)
    mix_b = pool_mix.reshape(len(POOL_WINDOWS), GROUP_DIM, GROUP_DIM).astype(BF16)
    scale = pool_scale.reshape(1, POOL_WIDTH)
    g1 = g1.reshape(1, D_MODEL)
    g2 = g2.reshape(1, D_MODEL)
    gf = gf.reshape(1, D_MODEL)

    h, u, qkv, fl, gates = _in_proj(x2, g1, w_uqkv, w_fl, w_g, fwd_token)
    fcol = _forget_fwd(fl, b_pad, n_seq, S)
    pm, p2, p3 = _pool_fwd(u, mix_b, scale, n_seq, S)
    a, lse = _attn_fwd(qkv, fcol, n_seq, S)
    w_po, w_ao, w_out = out_weights(a)
    merged, x1, attn_y, pool_y = _mix_out(a, p3, gates, x2, w_ao, w_po, w_out)
    w_gate_t, w_up_t, w_down = ffn_weights(x1)
    h2, gate, up, act, dx2, loss_rows, dgf = _ffn_fwd(x1, g2, gf, tg2, w_gate_t, w_up_t, w_down)

    dgate, dup, dx1, dg2 = _ffn_bwd(dx2, gate, up, x1, g2, w_gate_t, w_up_t, w_down)
    bwd_token = ffn_grads_out(_matmul_tn(dgate, h2, "dw_ffn_gate"), _matmul_tn(dup, h2, "dw_ffn_up"), _matmul_tn(act, dx2, "dw_ffn_down"))
    dgates, dpy, day, da, dp2, dscale = _mix_bwd(dx1, gates, pool_y, attn_y, p2, scale, w_out, w_ao, w_po, bwd_token)
    out_token = out_grads_out(
        _matmul_tn(p3, dpy, "dw_pool_out", col_chunks=True), _matmul_tn(a, day, "dw_attn_out", col_chunks=True), _matmul_tn(merged, dx1, "dw_out")
    )
    du, dmix = _pool_bwd(dp2, pm, mix_b, out_token, n_seq, S)
    dq, dk, dv, dfk, dfq = _attn_bwd(qkv, da, a, fcol, lse, n_seq, S)
    dfl, db = _forget_bwd(dfk, dfq, fl, b_pad, n_seq, S)
    small_token = small_grads_out((jnp.zeros_like(g1), dg2, dgf, db[:, :N_HEADS], dscale, dmix), loss_rows)
    in_token = in_grads_out(*_dw_in(h, du, dq, dk, dv, dfl, dgates, small_token))
    dx, dg1 = _in_proj_bwd(du, dq, dk, dv, dfl, dgates, x2, dx1, g1, w_uqkv, w_fl, w_g, in_token)
    norm1_grad_out(dg1)
    return dx.reshape(n_seq, S, D_MODEL)


def kernel(x, norm1_g, w_in, b_forget, pool_mix, pool_scale, w_pool_out, w_attn_out, w_out, norm2_g, w_ffn_gate, w_ffn_up, w_ffn_down, norm_f_g, loss_target, m_norm1_g, m_w_in, m_b_forget, m_pool_mix, m_pool_scale, m_w_pool_out, m_w_attn_out, m_w_out, m_norm2_g, m_w_ffn_gate, m_w_ffn_up, m_w_ffn_down, m_norm_f_g, v_norm1_g, v_w_in, v_b_forget, v_pool_mix, v_pool_scale, v_w_pool_out, v_w_attn_out, v_w_out, v_norm2_g, v_w_ffn_gate, v_w_ffn_up, v_w_ffn_down, v_norm_f_g):
    names = ("w_in", "w_pool_out", "w_attn_out", "w_out", "w_ffn_gate", "w_ffn_up", "w_ffn_down")
    w_sh = (w_in, w_pool_out, w_attn_out, w_out, w_ffn_gate, w_ffn_up, w_ffn_down)
    m_sh = (m_w_in, m_w_pool_out, m_w_attn_out, m_w_out, m_w_ffn_gate, m_w_ffn_up, m_w_ffn_down)
    v_sh = (v_w_in, v_w_pool_out, v_w_attn_out, v_w_out, v_w_ffn_gate, v_w_ffn_up, v_w_ffn_down)

    cx, cy, cc = _position()
    me = 4 * cx + 2 * cy + cc
    def stored(t, transposed):
        return jnp.transpose(t, (0, 2, 1)) if transposed else t

    w_sh, m_sh, v_sh = ([stored(t, tr) for t, tr in zip(ts, _TRANSPOSED)] for ts in (w_sh, m_sh, v_sh))
    shards = [w[0].astype(BF16) for w in w_sh]
    last_in = shards[0][:, W_IN_SHARD - 1].astype(F32).reshape(D_MODEL // LANES, LANES)
    gathered_in, tails_in = _all_gather([shards[0][:, : W_IN_SHARD - 1], last_in], "w_in_all_gather")
    out_sems = _exchange_start(shards[1:4], gathered_in, "out_weights_gather_start", "gather")
    ffn_sems = _exchange_start(shards[4:], out_sems[3], "ffn_weights_gather_start", "gather_half")
    no_order = jnp.zeros((8, LANES), F32)
    started = {}

    def out_weights(after):
        forward_sems, lands, token = _gather_forward(*ffn_sems[:3], after, "ffn_weights_forward_start")
        started["forward"] = (forward_sems, lands)
        _, lands = _exchange_wait(*out_sems[:3], token, "out_weights_gather_wait", "gather")
        return [_full_from_gathered(t, axis) for t, axis in zip(lands, _SHARD_AXIS[out])]

    def ffn_weights(after):
        lands = _forward_wait(*started["forward"], after, "ffn_weights_gather_wait")
        return [_full_from_gathered(t, axis) for t, axis in zip(lands, _SHARD_AXIS[ffn])]

    def hold_ffn_grads(*whole_grads):
        started["held"] = whole_grads
        return no_order

    def scatter_grads(*out_grads):
        chunks = [
            t if axis == 1 else t.reshape(N_DEV, -1, t.shape[1])
            for t, axis in zip((*out_grads, *started["held"]), _SHARD_AXIS[scattered])
        ]
        started["scatter"] = _exchange_start(chunks, no_order, "grads_scatter_start", "scatter")
        return started["scatter"][3]

    def gather_small(small, loss_rows):
        started["small"] = _exchange_start([_pack_small(small, loss_rows)], no_order, "small_grads_gather_start", "gather")
        return started["small"][3]

    def scatter_w_in(chunks_in, tails_in):
        started["in"] = _exchange_start([chunks_in, tails_in], no_order, "w_in_grads_scatter_start", "scatter")
        return started["in"][3]

    def gather_norm1(dg1):
        rows = jnp.reshape(dg1, (8, LANES))
        started["norm1"] = _exchange_start([rows], no_order, "norm1_grad_gather_start", "gather")

    ffn, out, scattered = slice(4, 7), slice(1, 4), slice(1, 7)
    grad_x = _local_grads(
        x, loss_target, norm1_g, norm2_g, norm_f_g, b_forget, pool_mix, pool_scale, _w_in_pieces(gathered_in, tails_in), ffn_sems[3],
        out_weights, ffn_weights, hold_ffn_grads, scatter_grads, gather_small, scatter_w_in, gather_norm1,
    )
    me_index = jnp.reshape(me, (1,)).astype(jnp.int32)

    srcs, lands = _exchange_wait(*started["scatter"][:3], started["norm1"][3], "grads_scatter_wait", "scatter")
    updates = [
        _shard_update_direct(p, s, w, m, v, me_index, "update_" + n)
        for p, s, w, m, v, n in zip(lands, srcs, w_sh[scattered], m_sh[scattered], v_sh[scattered], names[scattered])
    ]
    updates_out, updates_ffn = updates[:3], updates[3:]

    small_w = (norm1_g, norm2_g, norm_f_g, b_forget, pool_scale, pool_mix)
    small_m = (m_norm1_g, m_norm2_g, m_norm_f_g, m_b_forget, m_pool_scale, m_pool_mix)
    small_v = (v_norm1_g, v_norm2_g, v_norm_f_g, v_b_forget, v_pool_scale, v_pool_mix)
    (sent_in, sent_tails), (parts_in, parts_tails) = _exchange_wait(*started["in"][:3], updates_ffn[-1][0], "w_in_grads_scatter_wait", "scatter")
    update_in = _w_in_update(parts_in, sent_in, parts_tails, sent_tails, w_in, m_w_in, v_w_in, me_index, "update_w_in")

    def gathered_small(key, after, name):
        _, lands = _exchange_wait(*started[key][:3], after, name, "gather")
        return lands[0]

    parts = gathered_small("small", update_in[0], "small_grads_gather_wait")
    first_rows = gathered_small("norm1", parts, "norm1_grad_gather_wait")
    (g_s, d_s, nm_s, nv_s), loss = _small_update(parts, first_rows, small_w, small_m, small_v)
    g_w, d_w, nm_w, nv_w = zip(*(
        [stored(t, tr) for t in u] for u, tr in zip([update_in] + updates_out + updates_ffn, _TRANSPOSED)
    ))
    loss = loss.reshape(())
    (g1, g2, gf, gb, gsc, gmix), (d1, d2, df, db_, dsc, dmx) = g_s, d_s
    (m1, m2, mf, mb, msc, mmx), (v1, v2, vf, vb, vsc, vmx) = nm_s, nv_s

    def ordered(n1, win, b, mix, sc, wpo, wao, wout, n2, wg, wu, wd, nf):
        return (n1, win, b, mix, sc, wpo, wao, wout, n2, wg, wu, wd, nf)

    grads = ordered(g1, g_w[0], gb, gmix, gsc, g_w[1], g_w[2], g_w[3], g2, g_w[4], g_w[5], g_w[6], gf)
    deltas = ordered(d1, d_w[0], db_, dmx, dsc, d_w[1], d_w[2], d_w[3], d2, d_w[4], d_w[5], d_w[6], df)
    new_m = ordered(m1, nm_w[0], mb, mmx, msc, nm_w[1], nm_w[2], nm_w[3], m2, nm_w[4], nm_w[5], nm_w[6], mf)
    new_v = ordered(v1, nv_w[0], vb, vmx, vsc, nv_w[1], nv_w[2], nv_w[3], v2, nv_w[4], nv_w[5], nv_w[6], vf)
    return (loss, grad_x, *grads, *deltas, *new_m, *new_v)
```

```python
import jax
import jax.numpy as jnp
from jax import lax
from jax.experimental import pallas as pl
from jax.experimental.pallas import tpu as pltpu

F32 = jnp.float32
BF16 = jnp.bfloat16
MESH = pl.DeviceIdType.MESH

D_MODEL = 1024
POOL_WINDOWS = (2, 4, 8, 16)
POOL_WIDTH = 512
GROUP_DIM = 128
ATTN_WIDTH = 512
HEAD_DIM = 64
N_HEADS = 8
N_PAIRS = 4
D_FF = 2816
RMS_EPS = 1e-6
N_DEV = 8
LANES = 128
FL_PAD = 128

ADAM_LR = 0.001
ADAM_B1 = 0.9
ADAM_B2 = 0.999
ADAM_EPS = 1e-08
ADAM_WD = 0.01
ADAM_STEP = 10

VMEM_LIMIT = 56 * 1024 * 1024
VMEM_LIMIT_MAX = 60 * 1024 * 1024
ROW_TILE = 512
ATTN_BLOCK = 512
FF_CHUNK = 256
FF_ROW_TILE = 512
DW_TOKENS = 2048


def _mm(a, b):
    return jnp.dot(a, b, preferred_element_type=F32)


def _mm_nt(a, b):
    return lax.dot_general(a, b, (((1,), (1,)), ((), ())), preferred_element_type=F32)


def _mm_tn(a, b):
    return lax.dot_general(a, b, (((0,), (0,)), ((), ())), preferred_element_type=F32)


def _whole_cols(w_ref):
    if len(w_ref.shape) == 2:
        return w_ref[...]
    return jnp.concatenate([w_ref[d] for d in range(w_ref.shape[0])], axis=1)


def _sigmoid(x):
    return 1.0 / (1.0 + jnp.exp(-x))


def _params(sem, vmem=VMEM_LIMIT):
    return pltpu.CompilerParams(dimension_semantics=sem, vmem_limit_bytes=vmem)


def _const_spec(shape):
    nd = len(shape)
    return pl.BlockSpec(shape, lambda *_: (0,) * nd, pipeline_mode=pl.Buffered(1))


def _rms_fwd(x, g):
    r = lax.rsqrt(jnp.mean(x * x, axis=-1, keepdims=True) + RMS_EPS)
    xh = x * r
    return xh * g, xh, r


def _rms_bwd(dy, xh, r, g):
    dxh = dy * g
    dx = r * (dxh - xh * jnp.mean(dxh * xh, axis=-1, keepdims=True))
    return dx, dy * xh


def _in_proj(x, g1, w_uqkv, w_fl, w_g, token):
    T = x.shape[0]
    tm = ROW_TILE

    def body(x_ref, g_ref, wa_ref, wf_ref, wg_ref, token_ref, h_ref, u_ref, qkv_ref, fl_ref, gt_ref):
        h, _, _ = _rms_fwd(x_ref[...], g_ref[...])
        hb = h.astype(BF16)
        h_ref[...] = hb
        z = _mm(hb, wa_ref[...])
        u_ref[...] = z[:, :POOL_WIDTH]
        qkv_ref[...] = z[:, POOL_WIDTH:].astype(BF16)
        fl_ref[...] = _mm(hb, wf_ref[...])
        gt_ref[...] = _mm(hb, wg_ref[...]).astype(BF16)

    row = lambda n: pl.BlockSpec((tm, n), lambda i: (i, 0))
    return pl.pallas_call(
        body,
        name="in_proj",
        grid=(T // tm,),
        in_specs=[row(D_MODEL), _const_spec((1, D_MODEL)), _const_spec(w_uqkv.shape), _const_spec(w_fl.shape), _const_spec(w_g.shape), _HBM],
        out_specs=[row(D_MODEL), row(POOL_WIDTH), row(3 * ATTN_WIDTH), row(FL_PAD), row(2 * D_MODEL)],
        out_shape=[
            jax.ShapeDtypeStruct((T, D_MODEL), BF16),
            jax.ShapeDtypeStruct((T, POOL_WIDTH), F32),
            jax.ShapeDtypeStruct((T, 3 * ATTN_WIDTH), BF16),
            jax.ShapeDtypeStruct((T, FL_PAD), F32),
            jax.ShapeDtypeStruct((T, 2 * D_MODEL), BF16),
        ],
        compiler_params=_params(("parallel",)),
    )(x, g1, w_uqkv, w_fl, w_g, token)


def _log_sigmoid(x):
    return jnp.minimum(x, 0.0) - jnp.log(1.0 + jnp.exp(-jnp.abs(x)))


def _forget_fwd(fl, b_pad, n_seq, S):
    def body(fl_ref, b_ref, fcol_ref):
        lf = _log_sigmoid(fl_ref[...] + b_ref[...])
        t = lf.T
        lane = lax.broadcasted_iota(jnp.int32, t.shape, 1)
        k = 1
        while k < S:
            t = t + jnp.where(lane >= k, pltpu.roll(t, k, 1), 0.0)
            k *= 2
        fcol_ref[...] = t.T

    return pl.pallas_call(
        body,
        name="forget_fwd",
        grid=(n_seq,),
        in_specs=[pl.BlockSpec((S, FL_PAD), lambda s: (s, 0)), _const_spec((1, FL_PAD))],
        out_specs=pl.BlockSpec((S, FL_PAD), lambda s: (s, 0)),
        out_shape=jax.ShapeDtypeStruct((n_seq * S, FL_PAD), F32),
        compiler_params=_params(("parallel",)),
    )(fl, b_pad)


def _window_pick(g, v2, v4, v8, v16):
    return jnp.where(g == 0, v2, jnp.where(g == 1, v4, jnp.where(g == 2, v8, v16)))


def _pool_fwd(u, mix_b, scale, n_seq, S):
    T = n_seq * S

    def body(u_ref, mix_ref, sc_ref, pm_ref, p2_ref, p3_ref):
        g = pl.program_id(1)
        uu = u_ref[...]
        row = lax.broadcasted_iota(jnp.int32, uu.shape, 0)

        def back(a, k):
            return jnp.where(row >= k, pltpu.roll(a, k, 0), 0.0)

        s2 = uu + back(uu, 1)
        s4 = s2 + back(s2, 2)
        s8 = s4 + back(s4, 4)
        s16 = s8 + back(s8, 8)
        w = _window_pick(g, 2.0, 4.0, 8.0, 16.0)
        cnt = jnp.minimum((row + 1).astype(F32), w)
        pm = _window_pick(g, s2, s4, s8, s16) / cnt - uu
        pmb = pm.astype(BF16)
        pm_ref[...] = pmb
        p2 = _mm(pmb, mix_ref[...])
        p2_ref[...] = p2
        p3_ref[...] = (p2 * sc_ref[...]).astype(BF16)

    grp = pl.BlockSpec((S, GROUP_DIM), lambda s, g: (s, g))
    return pl.pallas_call(
        body,
        name="pool_fwd",
        grid=(n_seq, len(POOL_WINDOWS)),
        in_specs=[
            grp,
            pl.BlockSpec((None, GROUP_DIM, GROUP_DIM), lambda s, g: (g, 0, 0)),
            pl.BlockSpec((1, GROUP_DIM), lambda s, g: (0, g)),
        ],
        out_specs=[grp, grp, grp],
        out_shape=[
            jax.ShapeDtypeStruct((T, POOL_WIDTH), BF16),
            jax.ShapeDtypeStruct((T, POOL_WIDTH), F32),
            jax.ShapeDtypeStruct((T, POOL_WIDTH), BF16),
        ],
        compiler_params=_params(("parallel", "parallel")),
    )(u, mix_b, scale)


def _split3(v):
    hi = v.astype(BF16).astype(F32)
    r = v - hi
    mid = r.astype(BF16).astype(F32)
    lo = (r - mid).astype(BF16).astype(F32)
    return hi, mid, lo


def _bias_lanes(v):
    hi, mid, lo = _split3(v)
    lane = lax.broadcasted_iota(jnp.int32, (1, LANES), 1)
    packed = jnp.where(lane < N_HEADS, hi, jnp.where(lane < 2 * N_HEADS, pltpu.roll(mid, N_HEADS, 1), pltpu.roll(lo, 2 * N_HEADS, 1)))
    return jnp.where(lane < 3 * N_HEADS, packed, 0.0).astype(BF16)


def _bias_placement(slot):
    row = lax.broadcasted_iota(jnp.int32, (LANES, N_HEADS * LANES), 0)
    col = lax.broadcasted_iota(jnp.int32, (LANES, N_HEADS * LANES), 1)
    h = col // LANES
    n = col % LANES - jnp.where(h % 2 == 0, HEAD_DIM, 0) - 3 * slot
    return ((n >= 0) & (n < 3) & (row == N_HEADS * n + h)).astype(BF16)


def _augment(xp, h, bias, ones_slot):
    lane = lax.broadcasted_iota(jnp.int32, (1, LANES), 1)
    hh = h % 2
    head = (lane >= HEAD_DIM * hh) & (lane < HEAD_DIM * (hh + 1))
    b = HEAD_DIM * (1 - hh)
    rest = jnp.zeros_like(xp) if bias is None else bias[:, h * LANES : (h + 1) * LANES]
    out = jnp.where(head, xp, rest)
    if ones_slot is not None:
        out = jnp.where((lane >= b + 3 * ones_slot) & (lane < b + 3 * ones_slot + 3), jnp.ones_like(xp), out)
    return out


def _attn_fwd(qkv, fcol, n_seq, S):
    T = n_seq * S
    tb = ATTN_BLOCK
    nq = S // tb
    scale = HEAD_DIM ** -0.5

    def body(q_ref, k_ref, v_ref, fc_ref, o_ref, st_ref, qa_sc, ka_sc, m_sc, l_sc, acc_sc):
        i = pl.program_id(1)
        lane = lax.broadcasted_iota(jnp.int32, (1, LANES), 1)
        low = lane < HEAD_DIM

        @pl.when(i == 0)
        def _():
            place = _bias_placement(1)

            def rows_ka(r, carry):
                r0 = pl.multiple_of(r * tb, tb)
                bias = _mm(_bias_lanes(-fc_ref[pl.ds(r0, tb), :]), place).astype(BF16)
                for h in range(N_HEADS):
                    kp = k_ref[pl.ds(r0, tb), (h // 2) * LANES : (h // 2 + 1) * LANES] * scale
                    ka_sc[h, pl.ds(r0, tb), :] = _augment(kp, h, bias, 0)
                return carry

            lax.fori_loop(0, nq, rows_ka, 0)

        q0 = pl.multiple_of(i * tb, tb)
        bias = _mm(_bias_lanes(fc_ref[pl.ds(q0, tb), :]), _bias_placement(0)).astype(BF16)
        for h in range(N_HEADS):
            qa_sc[h] = _augment(q_ref[:, (h // 2) * LANES : (h // 2 + 1) * LANES], h, bias, 1)
        m_sc[...] = jnp.full(m_sc.shape, -jnp.inf, F32)
        l_sc[...] = jnp.zeros_like(l_sc)
        acc_sc[...] = jnp.zeros_like(acc_sc)
        causal = lax.broadcasted_iota(jnp.int32, (tb, tb), 1) <= lax.broadcasted_iota(jnp.int32, (tb, tb), 0)

        def step(j, masked):
            c0 = pl.multiple_of(j * tb, tb)
            for p in range(N_PAIRS):
                vb = v_ref[pl.ds(c0, tb), p * LANES : (p + 1) * LANES]
                pv, al = [], []
                for hh in range(2):
                    h = 2 * p + hh
                    s = _mm_nt(qa_sc[h], ka_sc[h, pl.ds(c0, tb), :])
                    if masked:
                        s = jnp.where(causal, s, -jnp.inf)
                    m_old = m_sc[h]
                    m_new = jnp.maximum(m_old, jnp.max(s, axis=1, keepdims=True))
                    alpha = jnp.exp(m_old - m_new)
                    pe = jnp.exp(s - jnp.concatenate([m_new] * (tb // LANES), axis=1))
                    l_sc[h] = alpha * l_sc[h] + jnp.sum(pe, axis=1, keepdims=True)
                    m_sc[h] = m_new
                    pv.append(_mm(pe.astype(BF16), vb))
                    al.append(alpha)
                acc_sc[p] = jnp.where(low, al[0], al[1]) * acc_sc[p] + jnp.where(low, pv[0], pv[1])

        def loop_body(j, carry):
            step(j, False)
            return carry

        lax.fori_loop(0, i, loop_body, 0)
        step(i, True)
        st = jnp.zeros((tb, LANES), F32)
        for p in range(N_PAIRS):
            lp = jnp.where(low, l_sc[2 * p], l_sc[2 * p + 1])
            o_ref[:, p * LANES : (p + 1) * LANES] = (acc_sc[p] / lp).astype(BF16)
            for h in (2 * p, 2 * p + 1):
                st = jnp.where(lane == h, m_sc[h] + jnp.log(l_sc[h]), st)
        st_ref[...] = st

    return pl.pallas_call(
        body,
        name="attn_fwd",
        grid=(n_seq, nq),
        in_specs=[
            pl.BlockSpec((tb, ATTN_WIDTH), lambda s, i: (s * nq + i, 0)),
            pl.BlockSpec((S, ATTN_WIDTH), lambda s, i: (s, 1)),
            pl.BlockSpec((S, ATTN_WIDTH), lambda s, i: (s, 2)),
            pl.BlockSpec((S, LANES), lambda s, i: (s, 0)),
        ],
        out_specs=[
            pl.BlockSpec((tb, ATTN_WIDTH), lambda s, i: (s * nq + i, 0)),
            pl.BlockSpec((tb, LANES), lambda s, i: (s * nq + i, 0)),
        ],
        out_shape=[jax.ShapeDtypeStruct((T, ATTN_WIDTH), BF16), jax.ShapeDtypeStruct((T, LANES), F32)],
        scratch_shapes=[
            pltpu.VMEM((N_HEADS, tb, LANES), BF16),
            pltpu.VMEM((N_HEADS, S, LANES), BF16),
            pltpu.VMEM((N_HEADS, tb, LANES), F32),
            pltpu.VMEM((N_HEADS, tb, LANES), F32),
            pltpu.VMEM((N_PAIRS, tb, LANES), F32),
        ],
        compiler_params=_params(("parallel", "arbitrary")),
    )(qkv, qkv, qkv, fcol)


def _mix_out(a, p3, gates, x, w_ao, w_po, w_out):
    T = x.shape[0]
    tm = ROW_TILE

    def body(a_ref, p3_ref, gt_ref, x_ref, wao_ref, wpo_ref, wout_ref, mg_ref, x1_ref, ay_ref, py_ref):
        ay = _mm(a_ref[...], _whole_cols(wao_ref))
        py = _mm(p3_ref[...], _whole_cols(wpo_ref))
        ay_ref[...] = ay.astype(BF16)
        py_ref[...] = py.astype(BF16)
        sp = _sigmoid(gt_ref[:, :D_MODEL].astype(F32))
        sa = _sigmoid(gt_ref[:, D_MODEL:].astype(F32))
        mb = (sp * py + sa * ay).astype(BF16)
        mg_ref[...] = mb
        x1_ref[...] = x_ref[...] + _mm(mb, wout_ref[...])

    row = lambda n: pl.BlockSpec((tm, n), lambda i: (i, 0))
    return pl.pallas_call(
        body,
        name="mix_out",
        grid=(T // tm,),
        in_specs=[
            row(ATTN_WIDTH), row(POOL_WIDTH), row(2 * D_MODEL), row(D_MODEL),
            _const_spec(w_ao.shape), _const_spec(w_po.shape), _const_spec(w_out.shape),
        ],
        out_specs=[row(D_MODEL), row(D_MODEL), row(D_MODEL), row(D_MODEL)],
        out_shape=[
            jax.ShapeDtypeStruct((T, D_MODEL), BF16), jax.ShapeDtypeStruct((T, D_MODEL), F32),
            jax.ShapeDtypeStruct((T, D_MODEL), BF16), jax.ShapeDtypeStruct((T, D_MODEL), BF16),
        ],
        compiler_params=_params(("parallel",)),
    )(a, p3, gates, x, w_ao, w_po, w_out)


def _ffn_fwd(x1, g2, gf, tgt, w_gate_t, w_up_t, w_down):
    T = x1.shape[0]
    tm = min(T, FF_ROW_TILE)
    nt = T // tm
    nc = D_FF // FF_CHUNK

    def body(x1_ref, g2_ref, gf_ref, tg_ref, wg_ref, wu_ref, wd_hbm, h2_ref, gate_ref, up_ref, act_ref, dx2_ref, loss_ref, dgf_ref, wd_ref, sem):
        first = pl.program_id(0) == 0
        late = pltpu.make_async_copy(wd_hbm, wd_ref, sem.at[0])

        @pl.when(first)
        def _():
            late.start()

        x1v = x1_ref[...]
        h2, _, _ = _rms_fwd(x1v, g2_ref[...])
        h2b = h2.astype(BF16)
        h2_ref[...] = h2b
        for c in range(nc):
            sl = slice(c * FF_CHUNK, (c + 1) * FF_CHUNK)
            gate = _mm_nt(h2b, wg_ref[sl, :])
            up = _mm_nt(h2b, wu_ref[sl, :])
            gate_ref[:, sl] = gate.astype(BF16)
            up_ref[:, sl] = up.astype(BF16)
            act_ref[:, sl] = (gate * _sigmoid(gate) * up).astype(BF16)
        @pl.when(first)
        def _():
            late.wait()

        acc = x1v + _mm(act_ref[...], wd_ref[...])
        gfv = gf_ref[...]
        y, xh, r = _rms_fwd(acc, gfv)
        err = y - tg_ref[...]
        part = 0.5 * jnp.sum(jnp.mean(err * err, axis=-1, keepdims=True), axis=0, keepdims=True)
        dx2, dgrow = _rms_bwd(err * (1.0 / D_MODEL), xh, r, gfv)
        dx2_ref[...] = dx2

        @pl.when(pl.program_id(0) == 0)
        def _():
            dgf_ref[...] = jnp.zeros_like(dgf_ref)
            loss_ref[...] = jnp.zeros_like(loss_ref)

        dgf_ref[...] += jnp.sum(dgrow, axis=0, keepdims=True)
        loss_ref[...] += jnp.broadcast_to(part, loss_ref.shape)

    row = lambda n: pl.BlockSpec((tm, n), lambda i: (i, 0))
    return pl.pallas_call(
        body,
        name="ffn_fwd",
        grid=(nt,),
        in_specs=[
            row(D_MODEL), _const_spec((1, D_MODEL)), _const_spec((1, D_MODEL)), row(D_MODEL),
            _const_spec(w_gate_t.shape), _const_spec(w_up_t.shape), _HBM,
        ],
        scratch_shapes=[pltpu.VMEM(w_down.shape, w_down.dtype), pltpu.SemaphoreType.DMA((1,))],
        out_specs=[
            row(D_MODEL), row(D_FF), row(D_FF), row(D_FF), row(D_MODEL),
            pl.BlockSpec((8, LANES), lambda i: (0, 0)),
            pl.BlockSpec((1, D_MODEL), lambda i: (0, 0)),
        ],
        out_shape=[
            jax.ShapeDtypeStruct((T, D_MODEL), BF16),
            jax.ShapeDtypeStruct((T, D_FF), BF16),
            jax.ShapeDtypeStruct((T, D_FF), BF16),
            jax.ShapeDtypeStruct((T, D_FF), BF16),
            jax.ShapeDtypeStruct((T, D_MODEL), F32),
            jax.ShapeDtypeStruct((8, LANES), F32),
            jax.ShapeDtypeStruct((1, D_MODEL), F32),
        ],
        compiler_params=_params(("arbitrary",)),
    )(x1, g2, gf, tgt, w_gate_t, w_up_t, w_down)


def _ffn_bwd(dx2, gate, up, x1, g2, w_gate_t, w_up_t, w_down):
    T = x1.shape[0]
    tm = min(T, FF_ROW_TILE)
    nc = D_FF // FF_CHUNK

    def body(dx2_ref, gate_ref, up_ref, x1_ref, g2_ref, wg_ref, wu_ref, wd_ref, dgate_ref, dup_ref, dx1_ref, dg2_ref):
        dx2v = dx2_ref[...]
        dx2b = dx2v.astype(BF16)
        for c in range(nc):
            sl = slice(c * FF_CHUNK, (c + 1) * FF_CHUNK)
            dact = _mm_nt(dx2b, wd_ref[sl, :])
            gate = gate_ref[:, sl].astype(F32)
            sg = _sigmoid(gate)
            silu = gate * sg
            dgate = (dact * up_ref[:, sl].astype(F32) * (sg * (1.0 + gate * (1.0 - sg)))).astype(BF16)
            dup = (dact * silu).astype(BF16)
            dgate_ref[:, sl] = dgate
            dup_ref[:, sl] = dup
        dh2 = _mm(dgate_ref[...], wg_ref[...]) + _mm(dup_ref[...], wu_ref[...])
        g2v = g2_ref[...]
        _, xh, r = _rms_fwd(x1_ref[...], g2v)
        dxn, dgrow = _rms_bwd(dh2, xh, r, g2v)
        dx1_ref[...] = dx2v + dxn

        @pl.when(pl.program_id(0) == 0)
        def _():
            dg2_ref[...] = jnp.zeros_like(dg2_ref)

        dg2_ref[...] += jnp.sum(dgrow, axis=0, keepdims=True)

    row = lambda n: pl.BlockSpec((tm, n), lambda i: (i, 0))
    return pl.pallas_call(
        body,
        name="ffn_bwd",
        grid=(T // tm,),
        in_specs=[
            row(D_MODEL), row(D_FF), row(D_FF), row(D_MODEL), _const_spec((1, D_MODEL)),
            _const_spec(w_gate_t.shape), _const_spec(w_up_t.shape), _const_spec(w_down.shape),
        ],
        out_specs=[row(D_FF), row(D_FF), row(D_MODEL), pl.BlockSpec((1, D_MODEL), lambda i: (0, 0))],
        out_shape=[
            jax.ShapeDtypeStruct((T, D_FF), BF16),
            jax.ShapeDtypeStruct((T, D_FF), BF16),
            jax.ShapeDtypeStruct((T, D_MODEL), F32),
            jax.ShapeDtypeStruct((1, D_MODEL), F32),
        ],
        compiler_params=_params(("arbitrary",), VMEM_LIMIT_MAX),
    )(dx2, gate, up, x1, g2, w_gate_t, w_up_t, w_down)


def _mix_bwd(dx1, gates, pool_y, attn_y, p2, scale, w_out, w_ao, w_po, token):
    T = dx1.shape[0]
    tm = ROW_TILE

    def body(dx1_ref, gt_ref, py_ref, ay_ref, p2_ref, sc_ref, wout_ref, wao_ref, wpo_ref, token_ref, dgt_ref, dpy_ref, day_ref, da_ref, dp2_ref, dsc_ref):
        dm = _mm_nt(dx1_ref[...].astype(BF16), wout_ref[...])
        sp = _sigmoid(gt_ref[:, :D_MODEL].astype(F32))
        sa = _sigmoid(gt_ref[:, D_MODEL:].astype(F32))
        dgt_ref[:, :D_MODEL] = (dm * py_ref[...].astype(F32) * (sp * (1.0 - sp))).astype(BF16)
        dgt_ref[:, D_MODEL:] = (dm * ay_ref[...].astype(F32) * (sa * (1.0 - sa))).astype(BF16)
        dpy = (dm * sp).astype(BF16)
        day = (dm * sa).astype(BF16)
        dpy_ref[...] = dpy
        day_ref[...] = day
        da_ref[...] = _mm_nt(day, _whole_cols(wao_ref)).astype(BF16)
        dp3 = _mm_nt(dpy, _whole_cols(wpo_ref))
        dp2_ref[...] = (dp3 * sc_ref[...]).astype(BF16)

        @pl.when(pl.program_id(0) == 0)
        def _():
            dsc_ref[...] = jnp.zeros_like(dsc_ref)

        dsc_ref[...] += jnp.sum(dp3 * p2_ref[...], axis=0, keepdims=True)

    row = lambda n: pl.BlockSpec((tm, n), lambda i: (i, 0))
    return pl.pallas_call(
        body,
        name="mix_bwd",
        grid=(T // tm,),
        in_specs=[
            row(D_MODEL), row(2 * D_MODEL), row(D_MODEL), row(D_MODEL), row(POOL_WIDTH), _const_spec((1, POOL_WIDTH)),
            _const_spec(w_out.shape), _const_spec(w_ao.shape), _const_spec(w_po.shape), _HBM,
        ],
        out_specs=[row(2 * D_MODEL), row(D_MODEL), row(D_MODEL), row(ATTN_WIDTH), row(POOL_WIDTH), pl.BlockSpec((1, POOL_WIDTH), lambda i: (0, 0))],
        out_shape=[
            jax.ShapeDtypeStruct((T, 2 * D_MODEL), BF16),
            jax.ShapeDtypeStruct((T, D_MODEL), BF16),
            jax.ShapeDtypeStruct((T, D_MODEL), BF16),
            jax.ShapeDtypeStruct((T, ATTN_WIDTH), BF16),
            jax.ShapeDtypeStruct((T, POOL_WIDTH), BF16),
            jax.ShapeDtypeStruct((1, POOL_WIDTH), F32),
        ],
        compiler_params=_params(("arbitrary",)),
    )(dx1, gates, pool_y, attn_y, p2, scale, w_out, w_ao, w_po, token)


def _pool_bwd(dp2, pm, mix_b, token, n_seq, S):
    T = n_seq * S

    def body(dp2_ref, pm_ref, mix_ref, token_ref, du_ref, dmix_ref):
        g = pl.program_id(0)
        dp2v = dp2_ref[...]
        dpm = _mm_nt(dp2v, mix_ref[...])
        row = lax.broadcasted_iota(jnp.int32, dpm.shape, 0)
        w = _window_pick(g, 2.0, 4.0, 8.0, 16.0)
        e = dpm / jnp.minimum((row + 1).astype(F32), w)

        def ahead(a, k):
            return jnp.where(row < S - k, pltpu.roll(a, S - k, 0), 0.0)

        r2 = e + ahead(e, 1)
        r4 = r2 + ahead(r2, 2)
        r8 = r4 + ahead(r4, 4)
        r16 = r8 + ahead(r8, 8)
        du_ref[...] = (_window_pick(g, r2, r4, r8, r16) - dpm).astype(BF16)

        @pl.when(pl.program_id(1) == 0)
        def _():
            dmix_ref[...] = jnp.zeros_like(dmix_ref)

        dmix_ref[...] += _mm_tn(pm_ref[...], dp2v)

    grp = pl.BlockSpec((S, GROUP_DIM), lambda g, s: (s, g))
    mixs = pl.BlockSpec((None, GROUP_DIM, GROUP_DIM), lambda g, s: (g, 0, 0))
    return pl.pallas_call(
        body,
        name="pool_bwd",
        grid=(len(POOL_WINDOWS), n_seq),
        in_specs=[grp, grp, mixs, _HBM],
        out_specs=[grp, mixs],
        out_shape=[jax.ShapeDtypeStruct((T, POOL_WIDTH), BF16), jax.ShapeDtypeStruct((len(POOL_WINDOWS), GROUP_DIM, GROUP_DIM), F32)],
        compiler_params=_params(("parallel", "arbitrary")),
    )(dp2, pm, mix_b, token)


def _attn_bwd(qkv, da, a, fcol, lse, n_seq, S):
    T = n_seq * S
    tb = ATTN_BLOCK
    nb = S // tb
    scale = HEAD_DIM ** -0.5

    def body(q_ref, k_ref, v_ref, do_ref, o_ref, fc_ref, st_ref, dq_ref, dk_ref, dv_ref, dfk_ref, dfq_ref,
             qa_sc, doa_sc, qat_sc, doat_sc, dq_acc, ka_sc, va_sc, dkt_sc, dvt_sc):
        j = pl.program_id(1)
        lane = lax.broadcasted_iota(jnp.int32, (1, LANES), 1)
        low = lane < HEAD_DIM

        @pl.when(j == 0)
        def _():
            dq_acc[...] = jnp.zeros_like(dq_acc)
            place = _bias_placement(0)

            def rows_q(i, carry):
                r0 = pl.multiple_of(i * tb, tb)
                delta = jnp.zeros((tb, LANES), F32)
                for h in range(N_HEADS):
                    pair = slice((h // 2) * LANES, (h // 2 + 1) * LANES)
                    prod = do_ref[pl.ds(r0, tb), pair].astype(F32) * o_ref[pl.ds(r0, tb), pair].astype(F32)
                    head = (lane >= HEAD_DIM * (h % 2)) & (lane < HEAD_DIM * (h % 2 + 1))
                    delta = jnp.where(lane == h, jnp.sum(jnp.where(head, prod, 0.0), axis=1, keepdims=True), delta)
                cq = fc_ref[pl.ds(r0, tb), :] - st_ref[pl.ds(r0, tb), :]
                q_bias = _mm(_bias_lanes(cq), place).astype(BF16)
                do_bias = _mm(_bias_lanes(-delta), place).astype(BF16)
                for h in range(N_HEADS):
                    pair = slice((h // 2) * LANES, (h // 2 + 1) * LANES)
                    qa = _augment(q_ref[pl.ds(r0, tb), pair], h, q_bias, 1)
                    doa = _augment(do_ref[pl.ds(r0, tb), pair], h, do_bias, None)
                    qa_sc[h, pl.ds(r0, tb), :] = qa
                    doa_sc[h, pl.ds(r0, tb), :] = doa
                    qat_sc[h, i] = qa.astype(F32).T.astype(BF16)
                    doat_sc[h, i] = doa.astype(F32).T.astype(BF16)
                return carry

            lax.fori_loop(0, nb, rows_q, 0)

        c0 = pl.multiple_of(j * tb, tb)
        k_bias = _mm(_bias_lanes(-fc_ref[pl.ds(c0, tb), :]), _bias_placement(1)).astype(BF16)
        for h in range(N_HEADS):
            pair = slice((h // 2) * LANES, (h // 2 + 1) * LANES)
            ka_sc[h] = _augment(k_ref[:, pair] * scale, h, k_bias, 0)
            va_sc[h] = _augment(v_ref[:, pair], h, None, 0)
        dkt_sc[...] = jnp.zeros_like(dkt_sc)
        dvt_sc[...] = jnp.zeros_like(dvt_sc)
        causal = lax.broadcasted_iota(jnp.int32, (tb, tb), 1) <= lax.broadcasted_iota(jnp.int32, (tb, tb), 0)

        def step(i, masked):
            r0 = pl.multiple_of(i * tb, tb)
            for h in range(N_HEADS):
                s = _mm_nt(qa_sc[h, pl.ds(r0, tb), :], ka_sc[h])
                if masked:
                    s = jnp.where(causal, s, -jnp.inf)
                pr = jnp.exp(s)
                dvt_sc[h] += _mm(doat_sc[h, i], pr.astype(BF16))
                dsb = (pr * _mm_nt(doa_sc[h, pl.ds(r0, tb), :], va_sc[h])).astype(BF16)
                dkt_sc[h] += _mm(qat_sc[h, i], dsb)
                dq_acc[h, pl.ds(r0, tb), :] += _mm(dsb, ka_sc[h])

        step(j, True)

        def loop_body(i, carry):
            step(i, False)
            return carry

        lax.fori_loop(j + 1, nb, loop_body, 0)
        dfk = jnp.zeros((tb, LANES), F32)
        for p in range(N_PAIRS):
            dk = [dkt_sc[2 * p + hh].T for hh in range(2)]
            dv = [dvt_sc[2 * p + hh].T for hh in range(2)]
            dk_ref[:, p * LANES : (p + 1) * LANES] = (jnp.where(low, dk[0], dk[1]) * scale).astype(BF16)
            dv_ref[:, p * LANES : (p + 1) * LANES] = jnp.where(low, dv[0], dv[1]).astype(BF16)
            for hh in range(2):
                b = HEAD_DIM * (1 - hh) + 3
                dfk = jnp.where(lane == 2 * p + hh, -dk[hh][:, b : b + 1], dfk)
        dfk_ref[...] = dfk

        @pl.when(j == nb - 1)
        def _():
            def rows_dq(i, carry):
                r0 = pl.multiple_of(i * tb, tb)
                dfq = jnp.zeros((tb, LANES), F32)
                for p in range(N_PAIRS):
                    parts = [dq_acc[2 * p + hh, pl.ds(r0, tb), :] for hh in range(2)]
                    dq_ref[pl.ds(r0, tb), p * LANES : (p + 1) * LANES] = jnp.where(low, parts[0], parts[1]).astype(BF16)
                    for hh in range(2):
                        b = HEAD_DIM * (1 - hh)
                        dfq = jnp.where(lane == 2 * p + hh, parts[hh][:, b : b + 1], dfq)
                dfq_ref[pl.ds(r0, tb), :] = dfq
                return carry

            lax.fori_loop(0, nb, rows_dq, 0)

    seq = lambda w, col: pl.BlockSpec((S, w), lambda s, j: (s, col))
    seq_in = lambda w, col: pl.BlockSpec((S, w), lambda s, j: (s, col), pipeline_mode=pl.Buffered(1))
    blk = lambda w, col: pl.BlockSpec((tb, w), lambda s, j: (s * nb + j, col))
    return pl.pallas_call(
        body,
        name="attn_bwd",
        grid=(n_seq, nb),
        in_specs=[seq(ATTN_WIDTH, 0), blk(ATTN_WIDTH, 1), blk(ATTN_WIDTH, 2), seq(ATTN_WIDTH, 0), seq(ATTN_WIDTH, 0), seq_in(LANES, 0), seq_in(LANES, 0)],
        out_specs=[seq(ATTN_WIDTH, 0), blk(ATTN_WIDTH, 0), blk(ATTN_WIDTH, 0), blk(LANES, 0), seq(LANES, 0)],
        out_shape=[
            jax.ShapeDtypeStruct((T, ATTN_WIDTH), BF16),
            jax.ShapeDtypeStruct((T, ATTN_WIDTH), BF16),
            jax.ShapeDtypeStruct((T, ATTN_WIDTH), BF16),
            jax.ShapeDtypeStruct((T, LANES), F32),
            jax.ShapeDtypeStruct((T, LANES), F32),
        ],
        scratch_shapes=[
            pltpu.VMEM((N_HEADS, S, LANES), BF16),
            pltpu.VMEM((N_HEADS, S, LANES), BF16),
            pltpu.VMEM((N_HEADS, nb, LANES, tb), BF16),
            pltpu.VMEM((N_HEADS, nb, LANES, tb), BF16),
            pltpu.VMEM((N_HEADS, S, LANES), F32),
            pltpu.VMEM((N_HEADS, tb, LANES), BF16),
            pltpu.VMEM((N_HEADS, tb, LANES), BF16),
            pltpu.VMEM((N_HEADS, LANES, tb), F32),
            pltpu.VMEM((N_HEADS, LANES, tb), F32),
        ],
        compiler_params=_params(("parallel", "arbitrary"), VMEM_LIMIT_MAX),
    )(qkv, qkv, qkv, da, a, fcol, lse)


def _forget_bwd(dfk, dfq, fl, b_pad, n_seq, S):
    def body(df_ref, dfq_ref, fl_ref, b_ref, dfl_ref, db_ref):
        t = (df_ref[...] + dfq_ref[...]).T
        lane = lax.broadcasted_iota(jnp.int32, t.shape, 1)
        k = 1
        while k < S:
            t = t + jnp.where(lane < S - k, pltpu.roll(t, S - k, 1), 0.0)
            k *= 2
        dfl = t.T * _sigmoid(-(fl_ref[...] + b_ref[...]))
        dfl_ref[...] = dfl.astype(BF16)

        @pl.when(pl.program_id(0) == 0)
        def _():
            db_ref[...] = jnp.zeros_like(db_ref)

        db_ref[...] += jnp.sum(dfl, axis=0, keepdims=True)

    return pl.pallas_call(
        body,
        name="forget_bwd",
        grid=(n_seq,),
        in_specs=[
            pl.BlockSpec((S, LANES), lambda s: (s, 0)),
            pl.BlockSpec((S, LANES), lambda s: (s, 0)),
            pl.BlockSpec((S, FL_PAD), lambda s: (s, 0)),
            _const_spec((1, FL_PAD)),
        ],
        out_specs=[pl.BlockSpec((S, FL_PAD), lambda s: (s, 0)), pl.BlockSpec((1, FL_PAD), lambda s: (0, 0))],
        out_shape=[jax.ShapeDtypeStruct((n_seq * S, FL_PAD), BF16), jax.ShapeDtypeStruct((1, FL_PAD), F32)],
        compiler_params=_params(("arbitrary",)),
    )(dfk, dfq, fl, b_pad)


def _in_proj_bwd(du, dq, dk, dv, dfl, dgates, x, dx1, g1, w_uqkv, w_fl, w_g, token):
    T = x.shape[0]
    tm = ROW_TILE

    def body(du_ref, dq_ref, dk_ref, dv_ref, dfl_ref, dgt_ref, x_ref, dx1_ref, g_ref, wa_ref, wf_ref, wg_ref, token_ref, dx_ref, dg_ref):
        dz = jnp.concatenate([du_ref[...], dq_ref[...], dk_ref[...], dv_ref[...]], axis=1)
        dh = _mm_nt(dz, wa_ref[...]) + _mm_nt(dgt_ref[...], wg_ref[...]) + _mm_nt(dfl_ref[...], wf_ref[...])
        gv = g_ref[...]
        _, xh, r = _rms_fwd(x_ref[...], gv)
        dxn, dgrow = _rms_bwd(dh, xh, r, gv)
        dx_ref[...] = dx1_ref[...] + dxn

        @pl.when(pl.program_id(0) == 0)
        def _():
            dg_ref[...] = jnp.zeros_like(dg_ref)

        dg_ref[...] += jnp.sum(dgrow, axis=0, keepdims=True)

    row = lambda n: pl.BlockSpec((tm, n), lambda i: (i, 0))
    return pl.pallas_call(
        body,
        name="in_proj_bwd",
        grid=(T // tm,),
        in_specs=[
            row(512), row(512), row(512), row(512), row(FL_PAD), row(2 * D_MODEL), row(D_MODEL), row(D_MODEL), _const_spec((1, D_MODEL)),
            _const_spec(w_uqkv.shape), _const_spec(w_fl.shape), _const_spec(w_g.shape), _HBM,
        ],
        out_specs=[row(D_MODEL), pl.BlockSpec((1, D_MODEL), lambda i: (0, 0))],
        out_shape=[jax.ShapeDtypeStruct((T, D_MODEL), F32), jax.ShapeDtypeStruct((1, D_MODEL), F32)],
        compiler_params=_params(("arbitrary",)),
    )(du, dq, dk, dv, dfl, dgates, x, dx1, g1, w_uqkv, w_fl, w_g, token)


def _pick_block(n):
    for b in (1024, 512, 1408, 256, 128):
        if n % b == 0:
            return b
    raise ValueError(n)


def _matmul_tn(a, b, name, col_chunks=False):
    T, K = a.shape
    N = b.shape[1]
    bt, bk, bn = min(T, DW_TOKENS), _pick_block(K), _pick_block(N)
    nt = T // bt
    c = N // N_DEV
    assert not col_chunks or (bn == N and c % LANES == 0)

    def body(a_ref, b_ref, o_ref, acc):
        @pl.when(pl.program_id(2) == 0)
        def _():
            acc[...] = jnp.zeros_like(acc)

        acc[...] += _mm_tn(a_ref[...].astype(BF16), b_ref[...].astype(BF16))

        @pl.when(pl.program_id(2) == nt - 1)
        def _():
            if col_chunks:
                for d in range(N_DEV):
                    o_ref[d] = acc[:, d * c : (d + 1) * c].astype(BF16)
            else:
                o_ref[...] = acc[...].astype(BF16)

    if col_chunks:
        out_spec, out_shape = pl.BlockSpec((N_DEV, bk, c), lambda k, n, t: (0, k, 0)), (N_DEV, K, c)
    else:
        out_spec, out_shape = pl.BlockSpec((bk, bn), lambda k, n, t: (k, n)), (K, N)
    return pl.pallas_call(
        body,
        name=name,
        grid=(K // bk, N // bn, nt),
        in_specs=[pl.BlockSpec((bt, bk), lambda k, n, t: (t, k)), pl.BlockSpec((bt, bn), lambda k, n, t: (t, n))],
        out_specs=out_spec,
        out_shape=jax.ShapeDtypeStruct(out_shape, BF16),
        scratch_shapes=[pltpu.VMEM((bk, bn), F32)],
        compiler_params=_params(("parallel", "parallel", "arbitrary")),
    )(a, b)


W_IN_A = POOL_WIDTH + 3 * ATTN_WIDTH
W_IN_SHARD = (W_IN_A + N_HEADS + 2 * D_MODEL) // N_DEV
_W_IN_PIECES = ((0, W_IN_A), (W_IN_A, W_IN_A + N_HEADS), (W_IN_A + N_HEADS, W_IN_A + N_HEADS + 2 * D_MODEL))


def _w_in_segments(d):
    lo, hi = d * W_IN_SHARD, (d + 1) * W_IN_SHARD
    out = []
    for p, (a, b) in enumerate(_W_IN_PIECES):
        s, e = max(lo, a), min(hi, b)
        if s < e:
            out.append((p, s - a, s - lo, e - s))
    return out


def _w_in_pieces(gathered, tails):
    tm = ROW_TILE // 2
    tail_rows = tm // LANES
    aligned = W_IN_SHARD - 1

    def body(g_ref, t_ref, wa_ref, wf_ref, wg_ref):
        outs = (wa_ref, wf_ref, wg_ref)
        wf_ref[...] = jnp.zeros_like(wf_ref)
        diagonal = lax.broadcasted_iota(jnp.int32, (LANES, LANES), 0) == lax.broadcasted_iota(jnp.int32, (LANES, LANES), 1)
        for d in range(N_DEV):
            for p, at, frm, n in _w_in_segments(d):
                m = min(n, aligned - frm)
                if m > 0:
                    outs[p][:, at : at + m] = g_ref[d, :, frm : frm + m]
                if frm + n == W_IN_SHARD:
                    column = [
                        jnp.sum(jnp.where(diagonal, jnp.broadcast_to(t_ref[d, k : k + 1, :], (LANES, LANES)), 0.0), axis=1, keepdims=True)
                        for k in range(tail_rows)
                    ]
                    outs[p][:, at + n - 1 : at + n] = jnp.concatenate(column, axis=0).astype(outs[p].dtype)

    return pl.pallas_call(
        body,
        name="w_in_pieces",
        grid=(D_MODEL // tm,),
        in_specs=[
            pl.BlockSpec((N_DEV, tm, aligned), lambda i: (0, i, 0)),
            pl.BlockSpec((N_DEV, None, tail_rows, LANES), lambda i: (0, i, 0, 0)),
        ],
        out_specs=[pl.BlockSpec((tm, W_IN_A), lambda i: (i, 0)), pl.BlockSpec((tm, FL_PAD), lambda i: (i, 0)), pl.BlockSpec((tm, 2 * D_MODEL), lambda i: (i, 0))],
        out_shape=[
            jax.ShapeDtypeStruct((D_MODEL, W_IN_A), gathered.dtype),
            jax.ShapeDtypeStruct((D_MODEL, FL_PAD), gathered.dtype),
            jax.ShapeDtypeStruct((D_MODEL, 2 * D_MODEL), gathered.dtype),
        ],
        compiler_params=_params(("parallel",)),
    )(gathered, tails.reshape(N_DEV, D_MODEL // tm, tail_rows, LANES))


def _dw_in(h, du, dq, dk, dv, dfl, dgates, token):
    T = h.shape[0]
    bt, bk = min(T, DW_TOKENS // 2), 512
    nt = T // bt
    pieces = (du, dq, dk, dv, dfl, dgates)
    offs = [0]
    for p in pieces:
        offs.append(offs[-1] + p.shape[1])

    aligned = W_IN_SHARD - 1
    tail_rows = bk // LANES

    def body(h_ref, *rest):
        refs, o_ref, t_ref, acc = rest[: len(pieces)], rest[-3], rest[-2], rest[-1]

        @pl.when(pl.program_id(1) == 0)
        def _():
            acc[...] = jnp.zeros_like(acc)

        ht = h_ref[...].T
        for ref, at in zip(refs, offs):
            acc[:, at : at + ref.shape[1]] += _mm(ht, ref[...])

        @pl.when(pl.program_id(1) == nt - 1)
        def _():
            starts = (0, W_IN_A, W_IN_A + FL_PAD)
            diagonal = lax.broadcasted_iota(jnp.int32, (LANES, LANES), 0) == lax.broadcasted_iota(jnp.int32, (LANES, LANES), 1)
            for d in range(N_DEV):
                for p, at, to, n in _w_in_segments(d):
                    m = min(n, aligned - to)
                    if m > 0:
                        o_ref[d, :, to : to + m] = acc[:, starts[p] + at : starts[p] + at + m].astype(BF16)
                    if to + n == W_IN_SHARD:
                        last = starts[p] + at + n - 1
                        column = acc[:, last : last + 1].astype(BF16).astype(F32)
                        for k in range(tail_rows):
                            rows = jnp.broadcast_to(column[k * LANES : (k + 1) * LANES], (LANES, LANES))
                            t_ref[d, k : k + 1, :] = jnp.sum(jnp.where(diagonal, rows, 0.0), axis=0, keepdims=True)

    main, tails = pl.pallas_call(
        body,
        name="dw_in",
        grid=(D_MODEL // bk, nt),
        in_specs=[pl.BlockSpec((bt, bk), lambda k, t: (t, k))] + [pl.BlockSpec((bt, p.shape[1]), lambda k, t: (t, 0)) for p in pieces] + [_HBM],
        out_specs=[
            pl.BlockSpec((N_DEV, bk, aligned), lambda k, t: (0, k, 0)),
            pl.BlockSpec((N_DEV, None, tail_rows, LANES), lambda k, t: (0, k, 0, 0)),
        ],
        out_shape=[
            jax.ShapeDtypeStruct((N_DEV, D_MODEL, aligned), BF16),
            jax.ShapeDtypeStruct((N_DEV, D_MODEL // bk, tail_rows, LANES), F32),
        ],
        scratch_shapes=[pltpu.VMEM((bk, offs[-1]), F32)],
        compiler_params=_params(("parallel", "arbitrary")),
    )(h, *pieces, token)
    return main, tails.reshape(N_DEV, D_MODEL // LANES, LANES)


def _position():
    return lax.axis_index("x"), lax.axis_index("y"), lax.axis_index("c")


_HBM = pl.BlockSpec(memory_space=pl.ANY)


def _all_gather(blocks, name):
    n = len(blocks)
    parts = [(a, q * (b.shape[0] // 4), b.shape[0] // 4) for a, b in enumerate(blocks) if b.shape[0] >= ROW_TILE for q in range(4)]
    parts += [(a, 0, b.shape[0]) for a, b in enumerate(blocks) if b.shape[0] < ROW_TILE]

    def body(*refs):
        xs, outs = refs[:n], refs[n : 2 * n]
        send_sems, recv_sems, local_sems = refs[2 * n :]
        x, y, c = _position()
        me, sibling = (x, y, c), (x, y, 1 - c)
        chips = [(1 - x, y), (x, 1 - y), (1 - x, 1 - y)]

        def rows(u, px, py, pc):
            a, lo, size = parts[u]
            return outs[a].at[4 * px + 2 * py + pc, pl.ds(lo, size)]

        def own(u):
            a, lo, size = parts[u]
            return xs[a].at[pl.ds(lo, size)]

        def copy(u, k, blk, to, src=None):
            return pltpu.make_async_remote_copy(
                src_ref=rows(u, *blk) if src is None else src, dst_ref=rows(u, *blk),
                send_sem=send_sems.at[7 * u + k], recv_sem=recv_sems.at[7 * u + k], device_id=to, device_id_type=MESH,
            )

        first = []
        for u in range(len(parts)):
            first += [copy(u, 1 + j, me, (*chip, c), src=own(u)) for j, chip in enumerate(chips)]
            first.append(copy(u, 0, me, sibling, src=own(u)))
        mine = [pltpu.make_async_copy(xs[a], outs[a].at[4 * x + 2 * y + c], local_sems.at[a]) for a in range(n)]
        for cp in first + mine:
            cp.start()
        passed = []
        for u in range(len(parts)):
            for j, chip in enumerate(chips):
                copy(u, 1 + j, (*chip, c), me).wait_recv()
                passed.append(copy(u, 4 + j, (*chip, c), sibling))
                passed[-1].start()
        for u in range(len(parts)):
            copy(u, 0, sibling, me).wait_recv()
            for j, chip in enumerate(chips):
                copy(u, 4 + j, (*chip, 1 - c), me).wait_recv()
        for cp in first + passed:
            cp.wait_send()
        for cp in mine:
            cp.wait()

    return pl.pallas_call(
        body,
        name=name,
        out_shape=[jax.ShapeDtypeStruct((N_DEV, *b.shape), b.dtype) for b in blocks],
        in_specs=[_HBM] * n,
        out_specs=[_HBM] * n,
        scratch_shapes=[pltpu.SemaphoreType.DMA((7 * len(parts),)), pltpu.SemaphoreType.DMA((7 * len(parts),)), pltpu.SemaphoreType.DMA((n,))],
    )(*blocks)


_SEM = pl.BlockSpec(memory_space=pltpu.SEMAPHORE)
_HBM_ONLY = pl.BlockSpec(memory_space=pltpu.HBM)
_SIDE_EFFECT = pltpu.SideEffectType.DATAFLOW_SIDE_EFFECTING


def _peer(x, y, c, k):
    return (1 - x if k & 4 else x, 1 - y if k & 2 else y, 1 - c if k & 1 else c)


_PEER_BITS = {"gather": range(1, N_DEV), "gather_half": (1, 4, 2, 6), "forward": (4, 2, 6), "scatter": range(1, N_DEV)}
_GATHERS = ("gather", "gather_half")


def _exchange_copies(src_refs, land_refs, send_sems, recv_sems, pattern, receive_side):
    x, y, c = _position()
    me = 4 * x + 2 * y + c
    bits = _PEER_BITS[pattern]
    cps = []
    for j, k in enumerate(bits):
        px, py, pc = _peer(x, y, c, k)
        peer = 4 * px + 2 * py + pc
        for a, (src, land) in enumerate(zip(src_refs, land_refs)):
            to = (px, py, pc)
            if pattern == "forward":
                slot = 4 * px + 2 * py + (1 - c if receive_side else c)
                s, to = land.at[slot], (x, y, 1 - c)
            else:
                s, slot = (src if pattern in _GATHERS else src.at[peer]), (peer if receive_side else me)
            cps.append(pltpu.make_async_remote_copy(
                src_ref=s, dst_ref=land.at[slot],
                send_sem=send_sems.at[len(bits) * a + j], recv_sem=recv_sems.at[len(bits) * a + j],
                device_id=to, device_id_type=MESH,
            ))
    return cps


def _own_copies(src_refs, land_refs, own_sems):
    x, y, c = _position()
    return [
        pltpu.make_async_copy(src, land.at[4 * x + 2 * y + c], own_sems.at[a])
        for a, (src, land) in enumerate(zip(src_refs, land_refs))
    ]


def _exchange_start(srcs, after, name, pattern):
    n = len(srcs)
    m = len(_PEER_BITS[pattern])
    lands = [jax.ShapeDtypeStruct((N_DEV, *s.shape[-2:]), s.dtype) for s in srcs]

    def body(*refs):
        src_refs, land_refs = refs[1 : 1 + n], refs[1 + n : 1 + 2 * n]
        send_sems, recv_sems, own_sems = refs[1 + 2 * n : 4 + 2 * n]
        token = refs[-1]
        if pattern in _GATHERS:
            for cp in _own_copies(src_refs, land_refs, own_sems):
                cp.start()
        for cp in _exchange_copies(src_refs, land_refs, send_sems, recv_sems, pattern, receive_side=False):
            cp.start()
        token[...] = jnp.zeros_like(token)

    hbm = lambda t: pltpu.with_memory_space_constraint(t, pltpu.HBM)
    out = pl.pallas_call(
        body,
        name=name,
        out_shape=(
            pltpu.SemaphoreType.DMA((m * n,)), pltpu.SemaphoreType.DMA((m * n,)), pltpu.SemaphoreType.DMA((n,)),
            *[pltpu.HBM(s.shape, s.dtype) for s in srcs], *[pltpu.HBM(l.shape, l.dtype) for l in lands],
            jax.ShapeDtypeStruct((8, LANES), F32),
        ),
        in_specs=(_HBM, *[_HBM_ONLY] * (2 * n)),
        out_specs=(_SEM, _SEM, _SEM, *[_HBM_ONLY] * (2 * n), pl.BlockSpec(memory_space=pltpu.VMEM)),
        input_output_aliases={1 + i: 3 + i for i in range(2 * n)},
        compiler_params=pltpu.CompilerParams(has_side_effects=_SIDE_EFFECT),
    )(after, *[hbm(s) for s in srcs], *[hbm(lax.empty(l.shape, l.dtype)) for l in lands])
    return out[:3], out[3 : 3 + n], out[3 + n : 3 + 2 * n], out[-1]


def _exchange_wait(sems, srcs, lands, after, name, pattern):
    n = len(srcs)

    def body(*refs):
        src_refs, land_refs = refs[:n], refs[n : 2 * n]
        send_sems, recv_sems, own_sems = refs[2 * n : 2 * n + 3]
        if pattern in _GATHERS:
            for cp in _own_copies(src_refs, land_refs, own_sems):
                cp.wait()
        for cp in _exchange_copies(src_refs, land_refs, send_sems, recv_sems, pattern, receive_side=True):
            cp.wait_send()
            cp.wait_recv()

    out = pl.pallas_call(
        body,
        name=name,
        out_shape=(*[pltpu.HBM(s.shape, s.dtype) for s in srcs], *[pltpu.HBM(l.shape, l.dtype) for l in lands]),
        in_specs=(*[_HBM_ONLY] * (2 * n), _SEM, _SEM, _SEM, _HBM),
        out_specs=tuple([_HBM_ONLY] * (2 * n)),
        input_output_aliases={i: i for i in range(2 * n)},
        compiler_params=pltpu.CompilerParams(has_side_effects=_SIDE_EFFECT),
    )(*srcs, *lands, *sems, after)
    return out[:n], out[n:]


def _gather_forward(sems, srcs, lands, after, name):
    n = len(srcs)
    m = len(_PEER_BITS["forward"])

    def body(*refs):
        src_refs, land_refs = refs[:n], refs[n : 2 * n]
        send_sems, recv_sems, own_sems = refs[2 * n : 2 * n + 3]
        forward_send, forward_recv, token = refs[2 * n + 4], refs[2 * n + 5], refs[-1]
        for cp in _own_copies(src_refs, land_refs, own_sems):
            cp.wait()
        for cp in _exchange_copies(src_refs, land_refs, send_sems, recv_sems, "gather_half", receive_side=True):
            cp.wait_send()
            cp.wait_recv()
        for cp in _exchange_copies(land_refs, land_refs, forward_send, forward_recv, "forward", receive_side=False):
            cp.start()
        token[...] = jnp.zeros_like(token)

    out = pl.pallas_call(
        body,
        name=name,
        out_shape=(
            pltpu.SemaphoreType.DMA((m * n,)), pltpu.SemaphoreType.DMA((m * n,)),
            *[pltpu.HBM(l.shape, l.dtype) for l in lands], jax.ShapeDtypeStruct((8, LANES), F32),
        ),
        in_specs=(*[_HBM_ONLY] * (2 * n), _SEM, _SEM, _SEM, _HBM),
        out_specs=(_SEM, _SEM, *[_HBM_ONLY] * n, pl.BlockSpec(memory_space=pltpu.VMEM)),
        input_output_aliases={n + i: 2 + i for i in range(n)},
        compiler_params=pltpu.CompilerParams(has_side_effects=_SIDE_EFFECT),
    )(*srcs, *lands, *sems, after)
    return out[:2], out[2 : 2 + n], out[-1]


def _forward_wait(sems, lands, after, name):
    n = len(lands)

    def body(*refs):
        land_refs = refs[:n]
        for cp in _exchange_copies(land_refs, land_refs, refs[n], refs[n + 1], "forward", receive_side=True):
            cp.wait_send()
            cp.wait_recv()

    return pl.pallas_call(
        body,
        name=name,
        out_shape=tuple(pltpu.HBM(l.shape, l.dtype) for l in lands),
        in_specs=(*[_HBM_ONLY] * n, _SEM, _SEM, _HBM),
        out_specs=tuple([_HBM_ONLY] * n),
        input_output_aliases={i: i for i in range(n)},
        compiler_params=pltpu.CompilerParams(has_side_effects=_SIDE_EFFECT),
    )(*lands, *sems, after)


def _rows_tile(r):
    return ROW_TILE if r % ROW_TILE == 0 else r


def _adamw(w, g, m, v):
    m = ADAM_B1 * m + (1.0 - ADAM_B1) * g
    v = ADAM_B2 * v + (1.0 - ADAM_B2) * (g * g)
    m_hat = m / (1.0 - ADAM_B1 ** ADAM_STEP)
    v_hat = v / (1.0 - ADAM_B2 ** ADAM_STEP)
    delta = -ADAM_LR * (m_hat / (jnp.sqrt(v_hat) + ADAM_EPS) + ADAM_WD * w)
    return delta, m, v


def _shard_update_direct(parts, chunks, w, m, v, me, name):
    _, r, c = w.shape
    br = _rows_tile(r)

    def body(me_ref, p_ref, own_ref, w_ref, m_ref, v_ref, g_ref, d_ref, nm_ref, nv_ref):
        g = None
        for n in range(N_DEV):
            part = jnp.where(me_ref[0] == n, own_ref[...], p_ref[n]).astype(F32)
            g = part if g is None else g + part
        g_ref[...] = g
        d_ref[...], nm_ref[...], nv_ref[...] = _adamw(w_ref[...], g, m_ref[...], v_ref[...])

    shard = pl.BlockSpec((None, br, c), lambda i, me: (0, i, 0))
    return pl.pallas_call(
        body,
        name=name,
        grid_spec=pltpu.PrefetchScalarGridSpec(
            num_scalar_prefetch=1,
            grid=(r // br,),
            in_specs=[
                pl.BlockSpec((N_DEV, br, c), lambda i, me: (0, i, 0)),
                pl.BlockSpec((None, br, c), lambda i, me: (me[0], i, 0)),
                shard, shard, shard,
            ],
            out_specs=[shard, shard, shard, shard],
        ),
        out_shape=[jax.ShapeDtypeStruct((1, r, c), F32)] * 4,
        compiler_params=_params(("parallel",)),
    )(me, parts, chunks, w, m, v)


def _w_in_update(parts, chunks, tail_parts, tail_chunks, w, m, v, me, name):
    _, r, c = w.shape
    br = _rows_tile(r)
    tail_rows = br // LANES

    def body(me_ref, p_ref, own_ref, tp_ref, town_ref, w_ref, m_ref, v_ref, g_ref, d_ref, nm_ref, nv_ref):
        g = tail = None
        for n in range(N_DEV):
            mine = me_ref[0] == n
            part = jnp.where(mine, own_ref[...], p_ref[n]).astype(F32)
            last = jnp.where(mine, town_ref[...], tp_ref[n])
            g = part if g is None else g + part
            tail = last if tail is None else tail + last
        diagonal = lax.broadcasted_iota(jnp.int32, (LANES, LANES), 0) == lax.broadcasted_iota(jnp.int32, (LANES, LANES), 1)
        column = jnp.concatenate(
            [
                jnp.sum(jnp.where(diagonal, jnp.broadcast_to(tail[k : k + 1, :], (LANES, LANES)), 0.0), axis=1, keepdims=True)
                for k in range(tail_rows)
            ],
            axis=0,
        )
        for lo, hi, grad in ((0, c - 1, g), (c - 1, c, column)):
            g_ref[:, lo:hi] = grad
            d_ref[:, lo:hi], nm_ref[:, lo:hi], nv_ref[:, lo:hi] = _adamw(w_ref[:, lo:hi], grad, m_ref[:, lo:hi], v_ref[:, lo:hi])

    shard = pl.BlockSpec((None, br, c), lambda i, me: (0, i, 0))
    by_block = lambda t: t.reshape(N_DEV, r // br, tail_rows, LANES)
    return pl.pallas_call(
        body,
        name=name,
        grid_spec=pltpu.PrefetchScalarGridSpec(
            num_scalar_prefetch=1,
            grid=(r // br,),
            in_specs=[
                pl.BlockSpec((N_DEV, br, c - 1), lambda i, me: (0, i, 0)),
                pl.BlockSpec((None, br, c - 1), lambda i, me: (me[0], i, 0)),
                pl.BlockSpec((N_DEV, None, tail_rows, LANES), lambda i, me: (0, i, 0, 0)),
                pl.BlockSpec((None, None, tail_rows, LANES), lambda i, me: (me[0], i, 0, 0)),
                shard, shard, shard,
            ],
            out_specs=[shard, shard, shard, shard],
        ),
        out_shape=[jax.ShapeDtypeStruct((1, r, c), F32)] * 4,
        compiler_params=_params(("parallel",)),
    )(me, parts, chunks, by_block(tail_parts), by_block(tail_chunks), w, m, v)


def _small_update(parts, first_rows, ws, ms, vs):
    k = len(ws)

    def unpacked(rows, shape):
        if len(shape) == 2 and shape[1] <= LANES:
            return rows[0:1, : shape[1]]
        if len(shape) == 2:
            return jnp.concatenate([rows[r : r + 1] for r in range(shape[1] // LANES)], axis=1)
        return rows.reshape(shape)

    def body(p_ref, f_ref, *refs):
        w_refs, m_refs, v_refs = refs[:k], refs[k : 2 * k], refs[2 * k : 3 * k]
        outs, loss_ref = refs[3 * k : 7 * k], refs[7 * k]
        g, first = p_ref[0], f_ref[0]
        for n in range(1, N_DEV):
            g = g + p_ref[n]
            first = first + f_ref[n]
        g = jnp.concatenate([g[:8] + first, g[8:]], axis=0)
        off = 0
        for i, (_, rows) in enumerate(_SMALL):
            gi = unpacked(g[off : off + rows], w_refs[i].shape)
            off += rows
            outs[i][...] = gi
            outs[k + i][...], outs[2 * k + i][...], outs[3 * k + i][...] = _adamw(w_refs[i][...], gi, m_refs[i][...], v_refs[i][...])
        loss_ref[...] = g[off : off + 1, 0:1]

    out = pl.pallas_call(
        body,
        name="small_update",
        out_shape=[jax.ShapeDtypeStruct(w.shape, F32) for _ in range(4) for w in ws] + [jax.ShapeDtypeStruct((1, 1), F32)],
        compiler_params=pltpu.CompilerParams(vmem_limit_bytes=VMEM_LIMIT),
    )(parts, first_rows, *ws, *ms, *vs)
    return [out[a * k : (a + 1) * k] for a in range(4)], out[4 * k]


_SHARD_AXIS = (1, 1, 1, 0, 0, 0, 0)
_TRANSPOSED = (False, False, False, False, True, True, False)


def _full_from_gathered(t, axis):
    if axis == 0:
        return t.reshape(N_DEV * t.shape[1], t.shape[2])
    return t


_SMALL = (("norm1_g", 8), ("norm2_g", 8), ("norm_f_g", 8), ("b_forget", 8), ("pool_scale", 8), ("pool_mix", 512))


def _pack_small(vals, loss_row):
    parts = []
    for (name, rows), t in zip(_SMALL, vals):
        f = t.astype(F32).reshape(-1)
        f = jnp.concatenate([f, jnp.zeros((rows * LANES - f.shape[0],), F32)]).reshape(rows, LANES)
        parts.append(f)
    parts.append(loss_row)
    return jnp.concatenate(parts, axis=0)


def _local_grads(x, tgt, g1, g2, gf, b_forget, pool_mix, pool_scale, w_in, fwd_token, out_weights, ffn_weights, ffn_grads_out, out_grads_out, small_grads_out, in_grads_out, norm1_grad_out):
    n_seq, S, _ = x.shape
    T = n_seq * S
    x2 = x.reshape(T, D_MODEL)
    tg2 = tgt.reshape(T, D_MODEL)
    w_uqkv, w_fl, w_g = w_in
    b_pad = jnp.concatenate([b_forget.reshape(1, N_HEADS), jnp.zeros((1, FL_PAD - N_HEADS), F32)], axis=1)
    mix_b = pool_mix.reshape(len(POOL_WINDOWS), GROUP_DIM, GROUP_DIM).astype(BF16)
    scale = pool_scale.reshape(1, POOL_WIDTH)
    g1 = g1.reshape(1, D_MODEL)
    g2 = g2.reshape(1, D_MODEL)
    gf = gf.reshape(1, D_MODEL)

    h, u, qkv, fl, gates = _in_proj(x2, g1, w_uqkv, w_fl, w_g, fwd_token)
    fcol = _forget_fwd(fl, b_pad, n_seq, S)
    pm, p2, p3 = _pool_fwd(u, mix_b, scale, n_seq, S)
    a, lse = _attn_fwd(qkv, fcol, n_seq, S)
    w_po, w_ao, w_out = out_weights(a)
    merged, x1, attn_y, pool_y = _mix_out(a, p3, gates, x2, w_ao, w_po, w_out)
    w_gate_t, w_up_t, w_down = ffn_weights(x1)
    h2, gate, up, act, dx2, loss_rows, dgf = _ffn_fwd(x1, g2, gf, tg2, w_gate_t, w_up_t, w_down)

    dgate, dup, dx1, dg2 = _ffn_bwd(dx2, gate, up, x1, g2, w_gate_t, w_up_t, w_down)
    bwd_token = ffn_grads_out(_matmul_tn(dgate, h2, "dw_ffn_gate"), _matmul_tn(dup, h2, "dw_ffn_up"), _matmul_tn(act, dx2, "dw_ffn_down"))
    dgates, dpy, day, da, dp2, dscale = _mix_bwd(dx1, gates, pool_y, attn_y, p2, scale, w_out, w_ao, w_po, bwd_token)
    out_token = out_grads_out(
        _matmul_tn(p3, dpy, "dw_pool_out", col_chunks=True), _matmul_tn(a, day, "dw_attn_out", col_chunks=True), _matmul_tn(merged, dx1, "dw_out")
    )
    du, dmix = _pool_bwd(dp2, pm, mix_b, out_token, n_seq, S)
    dq, dk, dv, dfk, dfq = _attn_bwd(qkv, da, a, fcol, lse, n_seq, S)
    dfl, db = _forget_bwd(dfk, dfq, fl, b_pad, n_seq, S)
    small_token = small_grads_out((jnp.zeros_like(g1), dg2, dgf, db[:, :N_HEADS], dscale, dmix), loss_rows)
    in_token = in_grads_out(*_dw_in(h, du, dq, dk, dv, dfl, dgates, small_token))
    dx, dg1 = _in_proj_bwd(du, dq, dk, dv, dfl, dgates, x2, dx1, g1, w_uqkv, w_fl, w_g, in_token)
    norm1_grad_out(dg1)
    return dx.reshape(n_seq, S, D_MODEL)


def kernel(x, norm1_g, w_in, b_forget, pool_mix, pool_scale, w_pool_out, w_attn_out, w_out, norm2_g, w_ffn_gate, w_ffn_up, w_ffn_down, norm_f_g, loss_target, m_norm1_g, m_w_in, m_b_forget, m_pool_mix, m_pool_scale, m_w_pool_out, m_w_attn_out, m_w_out, m_norm2_g, m_w_ffn_gate, m_w_ffn_up, m_w_ffn_down, m_norm_f_g, v_norm1_g, v_w_in, v_b_forget, v_pool_mix, v_pool_scale, v_w_pool_out, v_w_attn_out, v_w_out, v_norm2_g, v_w_ffn_gate, v_w_ffn_up, v_w_ffn_down, v_norm_f_g):
    names = ("w_in", "w_pool_out", "w_attn_out", "w_out", "w_ffn_gate", "w_ffn_up", "w_ffn_down")
    w_sh = (w_in, w_pool_out, w_attn_out, w_out, w_ffn_gate, w_ffn_up, w_ffn_down)
    m_sh = (m_w_in, m_w_pool_out, m_w_attn_out, m_w_out, m_w_ffn_gate, m_w_ffn_up, m_w_ffn_down)
    v_sh = (v_w_in, v_w_pool_out, v_w_attn_out, v_w_out, v_w_ffn_gate, v_w_ffn_up, v_w_ffn_down)

    cx, cy, cc = _position()
    me = 4 * cx + 2 * cy + cc
    def stored(t, transposed):
        return jnp.transpose(t, (0, 2, 1)) if transposed else t

    w_sh, m_sh, v_sh = ([stored(t, tr) for t, tr in zip(ts, _TRANSPOSED)] for ts in (w_sh, m_sh, v_sh))
    shards = [w[0].astype(BF16) for w in w_sh]
    last_in = shards[0][:, W_IN_SHARD - 1].astype(F32).reshape(D_MODEL // LANES, LANES)
    gathered_in, tails_in = _all_gather([shards[0][:, : W_IN_SHARD - 1], last_in], "w_in_all_gather")
    out_sems = _exchange_start(shards[1:4], gathered_in, "out_weights_gather_start", "gather")
    ffn_sems = _exchange_start(shards[4:], out_sems[3], "ffn_weights_gather_start", "gather_half")
    no_order = jnp.zeros((8, LANES), F32)
    started = {}

    def out_weights(after):
        forward_sems, lands, token = _gather_forward(*ffn_sems[:3], after, "ffn_weights_forward_start")
        started["forward"] = (forward_sems, lands)
        _, lands = _exchange_wait(*out_sems[:3], token, "out_weights_gather_wait", "gather")
        return [_full_from_gathered(t, axis) for t, axis in zip(lands, _SHARD_AXIS[out])]

    def ffn_weights(after):
        lands = _forward_wait(*started["forward"], after, "ffn_weights_gather_wait")
        return [_full_from_gathered(t, axis) for t, axis in zip(lands, _SHARD_AXIS[ffn])]

    def hold_ffn_grads(*whole_grads):
        started["held"] = whole_grads
        return no_order

    def scatter_grads(*out_grads):
        chunks = [
            t if axis == 1 else t.reshape(N_DEV, -1, t.shape[1])
            for t, axis in zip((*out_grads, *started["held"]), _SHARD_AXIS[scattered])
        ]
        started["scatter"] = _exchange_start(chunks, no_order, "grads_scatter_start", "scatter")
        return started["scatter"][3]

    def gather_small(small, loss_rows):
        started["small"] = _exchange_start([_pack_small(small, loss_rows)], no_order, "small_grads_gather_start", "gather")
        return started["small"][3]

    def scatter_w_in(chunks_in, tails_in):
        started["in"] = _exchange_start([chunks_in, tails_in], no_order, "w_in_grads_scatter_start", "scatter")
        return started["in"][3]

    def gather_norm1(dg1):
        rows = jnp.reshape(dg1, (8, LANES))
        started["norm1"] = _exchange_start([rows], no_order, "norm1_grad_gather_start", "gather")

    ffn, out, scattered = slice(4, 7), slice(1, 4), slice(1, 7)
    grad_x = _local_grads(
        x, loss_target, norm1_g, norm2_g, norm_f_g, b_forget, pool_mix, pool_scale, _w_in_pieces(gathered_in, tails_in), ffn_sems[3],
        out_weights, ffn_weights, hold_ffn_grads, scatter_grads, gather_small, scatter_w_in, gather_norm1,
    )
    me_index = jnp.reshape(me, (1,)).astype(jnp.int32)

    srcs, lands = _exchange_wait(*started["scatter"][:3], started["norm1"][3], "grads_scatter_wait", "scatter")
    updates = [
        _shard_update_direct(p, s, w, m, v, me_index, "update_" + n)
        for p, s, w, m, v, n in zip(lands, srcs, w_sh[scattered], m_sh[scattered], v_sh[scattered], names[scattered])
    ]
    updates_out, updates_ffn = updates[:3], updates[3:]

    small_w = (norm1_g, norm2_g, norm_f_g, b_forget, pool_scale, pool_mix)
    small_m = (m_norm1_g, m_norm2_g, m_norm_f_g, m_b_forget, m_pool_scale, m_pool_mix)
    small_v = (v_norm1_g, v_norm2_g, v_norm_f_g, v_b_forget, v_pool_scale, v_pool_mix)
    (sent_in, sent_tails), (parts_in, parts_tails) = _exchange_wait(*started["in"][:3], updates_ffn[-1][0], "w_in_grads_scatter_wait", "scatter")
    update_in = _w_in_update(parts_in, sent_in, parts_tails, sent_tails, w_in, m_w_in, v_w_in, me_index, "update_w_in")

    def gathered_small(key, after, name):
        _, lands = _exchange_wait(*started[key][:3], after, name, "gather")
        return lands[0]

    parts = gathered_small("small", update_in[0], "small_grads_gather_wait")
    first_rows = gathered_small("norm1", parts, "norm1_grad_gather_wait")
    (g_s, d_s, nm_s, nv_s), loss = _small_update(parts, first_rows, small_w, small_m, small_v)
    g_w, d_w, nm_w, nv_w = zip(*(
        [stored(t, tr) for t in u] for u, tr in zip([update_in] + updates_out + updates_ffn, _TRANSPOSED)
    ))
    loss = loss.reshape(())
    (g1, g2, gf, gb, gsc, gmix), (d1, d2, df, db_, dsc, dmx) = g_s, d_s
    (m1, m2, mf, mb, msc, mmx), (v1, v2, vf, vb, vsc, vmx) = nm_s, nv_s

    def ordered(n1, win, b, mix, sc, wpo, wao, wout, n2, wg, wu, wd, nf):
        return (n1, win, b, mix, sc, wpo, wao, wout, n2, wg, wu, wd, nf)

    grads = ordered(g1, g_w[0], gb, gmix, gsc, g_w[1], g_w[2], g_w[3], g2, g_w[4], g_w[5], g_w[6], gf)
    deltas = ordered(d1, d_w[0], db_, dmx, dsc, d_w[1], d_w[2], d_w[3], d2, d_w[4], d_w[5], d_w[6], df)
    new_m = ordered(m1, nm_w[0], mb, mmx, msc, nm_w[1], nm_w[2], nm_w[3], m2, nm_w[4], nm_w[5], nm_w[6], mf)
    new_v = ordered(v1, nv_w[0], vb, vmx, vsc, nv_w[1], nv_w[2], nv_w[3], v2, nv_w[4], nv_w[5], nv_w[6], vf)
    return (loss, grad_x, *grads, *deltas, *new_m, *new_v)
```

```python
import jax
import jax.numpy as jnp
from jax import lax
from jax.experimental import pallas as pl
from jax.experimental.pallas import tpu as pltpu

F32 = jnp.float32
BF16 = jnp.bfloat16
MESH = pl.DeviceIdType.MESH

D_MODEL = 1024
POOL_WINDOWS = (2, 4, 8, 16)
POOL_WIDTH = 512
GROUP_DIM = 128
ATTN_WIDTH = 512
HEAD_DIM = 64
N_HEADS = 8
N_PAIRS = 4
D_FF = 2816
RMS_EPS = 1e-6
N_DEV = 8
LANES = 128
FL_PAD = 128

ADAM_LR = 0.001
ADAM_B1 = 0.9
ADAM_B2 = 0.999
ADAM_EPS = 1e-08
ADAM_WD = 0.01
ADAM_STEP = 10

VMEM_LIMIT = 56 * 1024 * 1024
VMEM_LIMIT_MAX = 60 * 1024 * 1024
ROW_TILE = 512
ATTN_BLOCK = 512
FF_CHUNK = 256
FF_ROW_TILE = 512
DW_TOKENS = 2048


def _mm(a, b):
    return jnp.dot(a, b, preferred_element_type=F32)


def _mm_nt(a, b):
    return lax.dot_general(a, b, (((1,), (1,)), ((), ())), preferred_element_type=F32)


def _mm_tn(a, b):
    return lax.dot_general(a, b, (((0,), (0,)), ((), ())), preferred_element_type=F32)


def _whole_cols(w_ref):
    if len(w_ref.shape) == 2:
        return w_ref[...]
    return jnp.concatenate([w_ref[d] for d in range(w_ref.shape[0])], axis=1)


def _sigmoid(x):
    return 1.0 / (1.0 + jnp.exp(-x))


def _params(sem, vmem=VMEM_LIMIT):
    return pltpu.CompilerParams(dimension_semantics=sem, vmem_limit_bytes=vmem)


def _const_spec(shape):
    nd = len(shape)
    return pl.BlockSpec(shape, lambda *_: (0,) * nd, pipeline_mode=pl.Buffered(1))


def _rms_fwd(x, g):
    r = lax.rsqrt(jnp.mean(x * x, axis=-1, keepdims=True) + RMS_EPS)
    xh = x * r
    return xh * g, xh, r


def _rms_bwd(dy, xh, r, g):
    dxh = dy * g
    dx = r * (dxh - xh * jnp.mean(dxh * xh, axis=-1, keepdims=True))
    return dx, dy * xh


def _in_proj(x, g1, w_uqkv, w_fl, w_g, token):
    T = x.shape[0]
    tm = ROW_TILE

    def body(x_ref, g_ref, wa_ref, wf_ref, wg_ref, token_ref, h_ref, u_ref, qkv_ref, fl_ref, gt_ref):
        h, _, _ = _rms_fwd(x_ref[...], g_ref[...])
        hb = h.astype(BF16)
        h_ref[...] = hb
        z = _mm(hb, wa_ref[...])
        u_ref[...] = z[:, :POOL_WIDTH]
        qkv_ref[...] = z[:, POOL_WIDTH:].astype(BF16)
        fl_ref[...] = _mm(hb, wf_ref[...])
        gt_ref[...] = _mm(hb, wg_ref[...]).astype(BF16)

    row = lambda n: pl.BlockSpec((tm, n), lambda i: (i, 0))
    return pl.pallas_call(
        body,
        name="in_proj",
        grid=(T // tm,),
        in_specs=[row(D_MODEL), _const_spec((1, D_MODEL)), _const_spec(w_uqkv.shape), _const_spec(w_fl.shape), _const_spec(w_g.shape), _HBM],
        out_specs=[row(D_MODEL), row(POOL_WIDTH), row(3 * ATTN_WIDTH), row(FL_PAD), row(2 * D_MODEL)],
        out_shape=[
            jax.ShapeDtypeStruct((T, D_MODEL), BF16),
            jax.ShapeDtypeStruct((T, POOL_WIDTH), F32),
            jax.ShapeDtypeStruct((T, 3 * ATTN_WIDTH), BF16),
            jax.ShapeDtypeStruct((T, FL_PAD), F32),
            jax.ShapeDtypeStruct((T, 2 * D_MODEL), BF16),
        ],
        compiler_params=_params(("parallel",)),
    )(x, g1, w_uqkv, w_fl, w_g, token)


def _log_sigmoid(x):
    return jnp.minimum(x, 0.0) - jnp.log(1.0 + jnp.exp(-jnp.abs(x)))


def _forget_fwd(fl, b_pad, n_seq, S):
    def body(fl_ref, b_ref, fcol_ref):
        lf = _log_sigmoid(fl_ref[...] + b_ref[...])
        t = lf.T
        lane = lax.broadcasted_iota(jnp.int32, t.shape, 1)
        k = 1
        while k < S:
            t = t + jnp.where(lane >= k, pltpu.roll(t, k, 1), 0.0)
            k *= 2
        fcol_ref[...] = t.T

    return pl.pallas_call(
        body,
        name="forget_fwd",
        grid=(n_seq,),
        in_specs=[pl.BlockSpec((S, FL_PAD), lambda s: (s, 0)), _const_spec((1, FL_PAD))],
        out_specs=pl.BlockSpec((S, FL_PAD), lambda s: (s, 0)),
        out_shape=jax.ShapeDtypeStruct((n_seq * S, FL_PAD), F32),
        compiler_params=_params(("parallel",)),
    )(fl, b_pad)


def _window_pick(g, v2, v4, v8, v16):
    return jnp.where(g == 0, v2, jnp.where(g == 1, v4, jnp.where(g == 2, v8, v16)))


def _pool_fwd(u, mix_b, scale, n_seq, S):
    T = n_seq * S

    def body(u_ref, mix_ref, sc_ref, pm_ref, p2_ref, p3_ref):
        g = pl.program_id(1)
        uu = u_ref[...]
        row = lax.broadcasted_iota(jnp.int32, uu.shape, 0)

        def back(a, k):
            return jnp.where(row >= k, pltpu.roll(a, k, 0), 0.0)

        s2 = uu + back(uu, 1)
        s4 = s2 + back(s2, 2)
        s8 = s4 + back(s4, 4)
        s16 = s8 + back(s8, 8)
        w = _window_pick(g, 2.0, 4.0, 8.0, 16.0)
        cnt = jnp.minimum((row + 1).astype(F32), w)
        pm = _window_pick(g, s2, s4, s8, s16) / cnt - uu
        pmb = pm.astype(BF16)
        pm_ref[...] = pmb
        p2 = _mm(pmb, mix_ref[...])
        p2_ref[...] = p2
        p3_ref[...] = (p2 * sc_ref[...]).astype(BF16)

    grp = pl.BlockSpec((S, GROUP_DIM), lambda s, g: (s, g))
    return pl.pallas_call(
        body,
        name="pool_fwd",
        grid=(n_seq, len(POOL_WINDOWS)),
        in_specs=[
            grp,
            pl.BlockSpec((None, GROUP_DIM, GROUP_DIM), lambda s, g: (g, 0, 0)),
            pl.BlockSpec((1, GROUP_DIM), lambda s, g: (0, g)),
        ],
        out_specs=[grp, grp, grp],
        out_shape=[
            jax.ShapeDtypeStruct((T, POOL_WIDTH), BF16),
            jax.ShapeDtypeStruct((T, POOL_WIDTH), F32),
            jax.ShapeDtypeStruct((T, POOL_WIDTH), BF16),
        ],
        compiler_params=_params(("parallel", "parallel")),
    )(u, mix_b, scale)


def _split3(v):
    hi = v.astype(BF16).astype(F32)
    r = v - hi
    mid = r.astype(BF16).astype(F32)
    lo = (r - mid).astype(BF16).astype(F32)
    return hi, mid, lo


def _bias_lanes(v):
    hi, mid, lo = _split3(v)
    lane = lax.broadcasted_iota(jnp.int32, (1, LANES), 1)
    packed = jnp.where(lane < N_HEADS, hi, jnp.where(lane < 2 * N_HEADS, pltpu.roll(mid, N_HEADS, 1), pltpu.roll(lo, 2 * N_HEADS, 1)))
    return jnp.where(lane < 3 * N_HEADS, packed, 0.0).astype(BF16)


def _bias_placement(slot):
    row = lax.broadcasted_iota(jnp.int32, (LANES, N_HEADS * LANES), 0)
    col = lax.broadcasted_iota(jnp.int32, (LANES, N_HEADS * LANES), 1)
    h = col // LANES
    n = col % LANES - jnp.where(h % 2 == 0, HEAD_DIM, 0) - 3 * slot
    return ((n >= 0) & (n < 3) & (row == N_HEADS * n + h)).astype(BF16)


def _augment(xp, h, bias, ones_slot):
    lane = lax.broadcasted_iota(jnp.int32, (1, LANES), 1)
    hh = h % 2
    head = (lane >= HEAD_DIM * hh) & (lane < HEAD_DIM * (hh + 1))
    b = HEAD_DIM * (1 - hh)
    rest = jnp.zeros_like(xp) if bias is None else bias[:, h * LANES : (h + 1) * LANES]
    out = jnp.where(head, xp, rest)
    if ones_slot is not None:
        out = jnp.where((lane >= b + 3 * ones_slot) & (lane < b + 3 * ones_slot + 3), jnp.ones_like(xp), out)
    return out


def _attn_fwd(qkv, fcol, n_seq, S):
    T = n_seq * S
    tb = ATTN_BLOCK
    nq = S // tb
    scale = HEAD_DIM ** -0.5

    def body(q_ref, k_ref, v_ref, fc_ref, o_ref, st_ref, qa_sc, ka_sc, m_sc, l_sc, acc_sc):
        i = pl.program_id(1)
        lane = lax.broadcasted_iota(jnp.int32, (1, LANES), 1)
        low = lane < HEAD_DIM

        @pl.when(i == 0)
        def _():
            place = _bias_placement(1)

            def rows_ka(r, carry):
                r0 = pl.multiple_of(r * tb, tb)
                bias = _mm(_bias_lanes(-fc_ref[pl.ds(r0, tb), :]), place).astype(BF16)
                for h in range(N_HEADS):
                    kp = k_ref[pl.ds(r0, tb), (h // 2) * LANES : (h // 2 + 1) * LANES] * scale
                    ka_sc[h, pl.ds(r0, tb), :] = _augment(kp, h, bias, 0)
                return carry

            lax.fori_loop(0, nq, rows_ka, 0)

        q0 = pl.multiple_of(i * tb, tb)
        bias = _mm(_bias_lanes(fc_ref[pl.ds(q0, tb), :]), _bias_placement(0)).astype(BF16)
        for h in range(N_HEADS):
            qa_sc[h] = _augment(q_ref[:, (h // 2) * LANES : (h // 2 + 1) * LANES], h, bias, 1)
        m_sc[...] = jnp.full(m_sc.shape, -jnp.inf, F32)
        l_sc[...] = jnp.zeros_like(l_sc)
        acc_sc[...] = jnp.zeros_like(acc_sc)
        causal = lax.broadcasted_iota(jnp.int32, (tb, tb), 1) <= lax.broadcasted_iota(jnp.int32, (tb, tb), 0)

        def step(j, masked):
            c0 = pl.multiple_of(j * tb, tb)
            for p in range(N_PAIRS):
                vb = v_ref[pl.ds(c0, tb), p * LANES : (p + 1) * LANES]
                pv, al = [], []
                for hh in range(2):
                    h = 2 * p + hh
                    s = _mm_nt(qa_sc[h], ka_sc[h, pl.ds(c0, tb), :])
                    if masked:
                        s = jnp.where(causal, s, -jnp.inf)
                    m_old = m_sc[h]
                    m_new = jnp.maximum(m_old, jnp.max(s, axis=1, keepdims=True))
                    alpha = jnp.exp(m_old - m_new)
                    pe = jnp.exp(s - jnp.concatenate([m_new] * (tb // LANES), axis=1))
                    l_sc[h] = alpha * l_sc[h] + jnp.sum(pe, axis=1, keepdims=True)
                    m_sc[h] = m_new
                    pv.append(_mm(pe.astype(BF16), vb))
                    al.append(alpha)
                acc_sc[p] = jnp.where(low, al[0], al[1]) * acc_sc[p] + jnp.where(low, pv[0], pv[1])

        def loop_body(j, carry):
            step(j, False)
            return carry

        lax.fori_loop(0, i, loop_body, 0)
        step(i, True)
        st = jnp.zeros((tb, LANES), F32)
        for p in range(N_PAIRS):
            lp = jnp.where(low, l_sc[2 * p], l_sc[2 * p + 1])
            o_ref[:, p * LANES : (p + 1) * LANES] = (acc_sc[p] / lp).astype(BF16)
            for h in (2 * p, 2 * p + 1):
                st = jnp.where(lane == h, m_sc[h] + jnp.log(l_sc[h]), st)
        st_ref[...] = st

    return pl.pallas_call(
        body,
        name="attn_fwd",
        grid=(n_seq, nq),
        in_specs=[
            pl.BlockSpec((tb, ATTN_WIDTH), lambda s, i: (s * nq + i, 0)),
            pl.BlockSpec((S, ATTN_WIDTH), lambda s, i: (s, 1)),
            pl.BlockSpec((S, ATTN_WIDTH), lambda s, i: (s, 2)),
            pl.BlockSpec((S, LANES), lambda s, i: (s, 0)),
        ],
        out_specs=[
            pl.BlockSpec((tb, ATTN_WIDTH), lambda s, i: (s * nq + i, 0)),
            pl.BlockSpec((tb, LANES), lambda s, i: (s * nq + i, 0)),
        ],
        out_shape=[jax.ShapeDtypeStruct((T, ATTN_WIDTH), BF16), jax.ShapeDtypeStruct((T, LANES), F32)],
        scratch_shapes=[
            pltpu.VMEM((N_HEADS, tb, LANES), BF16),
            pltpu.VMEM((N_HEADS, S, LANES), BF16),
            pltpu.VMEM((N_HEADS, tb, LANES), F32),
            pltpu.VMEM((N_HEADS, tb, LANES), F32),
            pltpu.VMEM((N_PAIRS, tb, LANES), F32),
        ],
        compiler_params=_params(("parallel", "arbitrary")),
    )(qkv, qkv, qkv, fcol)


def _mix_out(a, p3, gates, x, w_ao, w_po, w_out):
    T = x.shape[0]
    tm = ROW_TILE

    def body(a_ref, p3_ref, gt_ref, x_ref, wao_ref, wpo_ref, wout_ref, mg_ref, x1_ref, ay_ref, py_ref):
        ay = _mm(a_ref[...], _whole_cols(wao_ref))
        py = _mm(p3_ref[...], _whole_cols(wpo_ref))
        ay_ref[...] = ay.astype(BF16)
        py_ref[...] = py.astype(BF16)
        sp = _sigmoid(gt_ref[:, :D_MODEL].astype(F32))
        sa = _sigmoid(gt_ref[:, D_MODEL:].astype(F32))
        mb = (sp * py + sa * ay).astype(BF16)
        mg_ref[...] = mb
        x1_ref[...] = x_ref[...] + _mm(mb, wout_ref[...])

    row = lambda n: pl.BlockSpec((tm, n), lambda i: (i, 0))
    return pl.pallas_call(
        body,
        name="mix_out",
        grid=(T // tm,),
        in_specs=[
            row(ATTN_WIDTH), row(POOL_WIDTH), row(2 * D_MODEL), row(D_MODEL),
            _const_spec(w_ao.shape), _const_spec(w_po.shape), _const_spec(w_out.shape),
        ],
        out_specs=[row(D_MODEL), row(D_MODEL), row(D_MODEL), row(D_MODEL)],
        out_shape=[
            jax.ShapeDtypeStruct((T, D_MODEL), BF16), jax.ShapeDtypeStruct((T, D_MODEL), F32),
            jax.ShapeDtypeStruct((T, D_MODEL), BF16), jax.ShapeDtypeStruct((T, D_MODEL), BF16),
        ],
        compiler_params=_params(("parallel",)),
    )(a, p3, gates, x, w_ao, w_po, w_out)


def _ffn_fwd(x1, g2, gf, tgt, w_gate_t, w_up_t, w_down):
    T = x1.shape[0]
    tm = min(T, FF_ROW_TILE)
    nt = T // tm
    nc = D_FF // FF_CHUNK

    def body(x1_ref, g2_ref, gf_ref, tg_ref, wg_ref, wu_ref, wd_ref, h2_ref, gate_ref, up_ref, act_ref, dx2_ref, loss_ref, dgf_ref):
        x1v = x1_ref[...]
        h2, _, _ = _rms_fwd(x1v, g2_ref[...])
        h2b = h2.astype(BF16)
        h2_ref[...] = h2b
        for c in range(nc):
            sl = slice(c * FF_CHUNK, (c + 1) * FF_CHUNK)
            gate = _mm_nt(h2b, wg_ref[sl, :])
            up = _mm_nt(h2b, wu_ref[sl, :])
            gate_ref[:, sl] = gate.astype(BF16)
            up_ref[:, sl] = up.astype(BF16)
            act_ref[:, sl] = (gate * _sigmoid(gate) * up).astype(BF16)
        acc = x1v + _mm(act_ref[...], wd_ref[...])
        gfv = gf_ref[...]
        y, xh, r = _rms_fwd(acc, gfv)
        err = y - tg_ref[...]
        part = 0.5 * jnp.sum(jnp.mean(err * err, axis=-1, keepdims=True), axis=0, keepdims=True)
        dx2, dgrow = _rms_bwd(err * (1.0 / D_MODEL), xh, r, gfv)
        dx2_ref[...] = dx2

        @pl.when(pl.program_id(0) == 0)
        def _():
            dgf_ref[...] = jnp.zeros_like(dgf_ref)
            loss_ref[...] = jnp.zeros_like(loss_ref)

        dgf_ref[...] += jnp.sum(dgrow, axis=0, keepdims=True)
        loss_ref[...] += jnp.broadcast_to(part, loss_ref.shape)

    row = lambda n: pl.BlockSpec((tm, n), lambda i: (i, 0))
    return pl.pallas_call(
        body,
        name="ffn_fwd",
        grid=(nt,),
        in_specs=[
            row(D_MODEL), _const_spec((1, D_MODEL)), _const_spec((1, D_MODEL)), row(D_MODEL),
            _const_spec(w_gate_t.shape), _const_spec(w_up_t.shape), _const_spec(w_down.shape),
        ],
        out_specs=[
            row(D_MODEL), row(D_FF), row(D_FF), row(D_FF), row(D_MODEL),
            pl.BlockSpec((8, LANES), lambda i: (0, 0)),
            pl.BlockSpec((1, D_MODEL), lambda i: (0, 0)),
        ],
        out_shape=[
            jax.ShapeDtypeStruct((T, D_MODEL), BF16),
            jax.ShapeDtypeStruct((T, D_FF), BF16),
            jax.ShapeDtypeStruct((T, D_FF), BF16),
            jax.ShapeDtypeStruct((T, D_FF), BF16),
            jax.ShapeDtypeStruct((T, D_MODEL), F32),
            jax.ShapeDtypeStruct((8, LANES), F32),
            jax.ShapeDtypeStruct((1, D_MODEL), F32),
        ],
        compiler_params=_params(("arbitrary",)),
    )(x1, g2, gf, tgt, w_gate_t, w_up_t, w_down)


def _ffn_bwd(dx2, gate, up, x1, g2, w_gate_t, w_up_t, w_down):
    T = x1.shape[0]
    tm = min(T, FF_ROW_TILE)
    nc = D_FF // FF_CHUNK

    def body(dx2_ref, gate_ref, up_ref, x1_ref, g2_ref, wg_ref, wu_ref, wd_ref, dgate_ref, dup_ref, dx1_ref, dg2_ref):
        dx2v = dx2_ref[...]
        dx2b = dx2v.astype(BF16)
        for c in range(nc):
            sl = slice(c * FF_CHUNK, (c + 1) * FF_CHUNK)
            dact = _mm_nt(dx2b, wd_ref[sl, :])
            gate = gate_ref[:, sl].astype(F32)
            sg = _sigmoid(gate)
            silu = gate * sg
            dgate = (dact * up_ref[:, sl].astype(F32) * (sg * (1.0 + gate * (1.0 - sg)))).astype(BF16)
            dup = (dact * silu).astype(BF16)
            dgate_ref[:, sl] = dgate
            dup_ref[:, sl] = dup
        dh2 = _mm(dgate_ref[...], wg_ref[...]) + _mm(dup_ref[...], wu_ref[...])
        g2v = g2_ref[...]
        _, xh, r = _rms_fwd(x1_ref[...], g2v)
        dxn, dgrow = _rms_bwd(dh2, xh, r, g2v)
        dx1_ref[...] = dx2v + dxn

        @pl.when(pl.program_id(0) == 0)
        def _():
            dg2_ref[...] = jnp.zeros_like(dg2_ref)

        dg2_ref[...] += jnp.sum(dgrow, axis=0, keepdims=True)

    row = lambda n: pl.BlockSpec((tm, n), lambda i: (i, 0))
    return pl.pallas_call(
        body,
        name="ffn_bwd",
        grid=(T // tm,),
        in_specs=[
            row(D_MODEL), row(D_FF), row(D_FF), row(D_MODEL), _const_spec((1, D_MODEL)),
            _const_spec(w_gate_t.shape), _const_spec(w_up_t.shape), _const_spec(w_down.shape),
        ],
        out_specs=[row(D_FF), row(D_FF), row(D_MODEL), pl.BlockSpec((1, D_MODEL), lambda i: (0, 0))],
        out_shape=[
            jax.ShapeDtypeStruct((T, D_FF), BF16),
            jax.ShapeDtypeStruct((T, D_FF), BF16),
            jax.ShapeDtypeStruct((T, D_MODEL), F32),
            jax.ShapeDtypeStruct((1, D_MODEL), F32),
        ],
        compiler_params=_params(("arbitrary",), VMEM_LIMIT_MAX),
    )(dx2, gate, up, x1, g2, w_gate_t, w_up_t, w_down)


def _mix_bwd(dx1, gates, pool_y, attn_y, p2, scale, w_out, w_ao, w_po, token):
    T = dx1.shape[0]
    tm = ROW_TILE

    def body(dx1_ref, gt_ref, py_ref, ay_ref, p2_ref, sc_ref, wout_ref, wao_ref, wpo_ref, token_ref, dgt_ref, dpy_ref, day_ref, da_ref, dp2_ref, dsc_ref):
        dm = _mm_nt(dx1_ref[...].astype(BF16), wout_ref[...])
        sp = _sigmoid(gt_ref[:, :D_MODEL].astype(F32))
        sa = _sigmoid(gt_ref[:, D_MODEL:].astype(F32))
        dgt_ref[:, :D_MODEL] = (dm * py_ref[...].astype(F32) * (sp * (1.0 - sp))).astype(BF16)
        dgt_ref[:, D_MODEL:] = (dm * ay_ref[...].astype(F32) * (sa * (1.0 - sa))).astype(BF16)
        dpy = (dm * sp).astype(BF16)
        day = (dm * sa).astype(BF16)
        dpy_ref[...] = dpy
        day_ref[...] = day
        da_ref[...] = _mm_nt(day, _whole_cols(wao_ref)).astype(BF16)
        dp3 = _mm_nt(dpy, _whole_cols(wpo_ref))
        dp2_ref[...] = (dp3 * sc_ref[...]).astype(BF16)

        @pl.when(pl.program_id(0) == 0)
        def _():
            dsc_ref[...] = jnp.zeros_like(dsc_ref)

        dsc_ref[...] += jnp.sum(dp3 * p2_ref[...], axis=0, keepdims=True)

    row = lambda n: pl.BlockSpec((tm, n), lambda i: (i, 0))
    return pl.pallas_call(
        body,
        name="mix_bwd",
        grid=(T // tm,),
        in_specs=[
            row(D_MODEL), row(2 * D_MODEL), row(D_MODEL), row(D_MODEL), row(POOL_WIDTH), _const_spec((1, POOL_WIDTH)),
            _const_spec(w_out.shape), _const_spec(w_ao.shape), _const_spec(w_po.shape), _HBM,
        ],
        out_specs=[row(2 * D_MODEL), row(D_MODEL), row(D_MODEL), row(ATTN_WIDTH), row(POOL_WIDTH), pl.BlockSpec((1, POOL_WIDTH), lambda i: (0, 0))],
        out_shape=[
            jax.ShapeDtypeStruct((T, 2 * D_MODEL), BF16),
            jax.ShapeDtypeStruct((T, D_MODEL), BF16),
            jax.ShapeDtypeStruct((T, D_MODEL), BF16),
            jax.ShapeDtypeStruct((T, ATTN_WIDTH), BF16),
            jax.ShapeDtypeStruct((T, POOL_WIDTH), BF16),
            jax.ShapeDtypeStruct((1, POOL_WIDTH), F32),
        ],
        compiler_params=_params(("arbitrary",)),
    )(dx1, gates, pool_y, attn_y, p2, scale, w_out, w_ao, w_po, token)


def _pool_bwd(dp2, pm, mix_b, token, n_seq, S):
    T = n_seq * S

    def body(dp2_ref, pm_ref, mix_ref, token_ref, du_ref, dmix_ref):
        g = pl.program_id(0)
        dp2v = dp2_ref[...]
        dpm = _mm_nt(dp2v, mix_ref[...])
        row = lax.broadcasted_iota(jnp.int32, dpm.shape, 0)
        w = _window_pick(g, 2.0, 4.0, 8.0, 16.0)
        e = dpm / jnp.minimum((row + 1).astype(F32), w)

        def ahead(a, k):
            return jnp.where(row < S - k, pltpu.roll(a, S - k, 0), 0.0)

        r2 = e + ahead(e, 1)
        r4 = r2 + ahead(r2, 2)
        r8 = r4 + ahead(r4, 4)
        r16 = r8 + ahead(r8, 8)
        du_ref[...] = (_window_pick(g, r2, r4, r8, r16) - dpm).astype(BF16)

        @pl.when(pl.program_id(1) == 0)
        def _():
            dmix_ref[...] = jnp.zeros_like(dmix_ref)

        dmix_ref[...] += _mm_tn(pm_ref[...], dp2v)

    grp = pl.BlockSpec((S, GROUP_DIM), lambda g, s: (s, g))
    mixs = pl.BlockSpec((None, GROUP_DIM, GROUP_DIM), lambda g, s: (g, 0, 0))
    return pl.pallas_call(
        body,
        name="pool_bwd",
        grid=(len(POOL_WINDOWS), n_seq),
        in_specs=[grp, grp, mixs, _HBM],
        out_specs=[grp, mixs],
        out_shape=[jax.ShapeDtypeStruct((T, POOL_WIDTH), BF16), jax.ShapeDtypeStruct((len(POOL_WINDOWS), GROUP_DIM, GROUP_DIM), F32)],
        compiler_params=_params(("parallel", "arbitrary")),
    )(dp2, pm, mix_b, token)


def _attn_bwd(qkv, da, a, fcol, lse, n_seq, S):
    T = n_seq * S
    tb = ATTN_BLOCK
    nb = S // tb
    scale = HEAD_DIM ** -0.5

    def body(q_ref, k_ref, v_ref, do_ref, o_ref, fc_ref, st_ref, dq_ref, dk_ref, dv_ref, dfk_ref, dfq_ref,
             qa_sc, doa_sc, qat_sc, doat_sc, dq_acc, ka_sc, va_sc, dkt_sc, dvt_sc):
        j = pl.program_id(1)
        lane = lax.broadcasted_iota(jnp.int32, (1, LANES), 1)
        low = lane < HEAD_DIM

        @pl.when(j == 0)
        def _():
            dq_acc[...] = jnp.zeros_like(dq_acc)
            place = _bias_placement(0)

            def rows_q(i, carry):
                r0 = pl.multiple_of(i * tb, tb)
                delta = jnp.zeros((tb, LANES), F32)
                for h in range(N_HEADS):
                    pair = slice((h // 2) * LANES, (h // 2 + 1) * LANES)
                    prod = do_ref[pl.ds(r0, tb), pair].astype(F32) * o_ref[pl.ds(r0, tb), pair].astype(F32)
                    head = (lane >= HEAD_DIM * (h % 2)) & (lane < HEAD_DIM * (h % 2 + 1))
                    delta = jnp.where(lane == h, jnp.sum(jnp.where(head, prod, 0.0), axis=1, keepdims=True), delta)
                cq = fc_ref[pl.ds(r0, tb), :] - st_ref[pl.ds(r0, tb), :]
                q_bias = _mm(_bias_lanes(cq), place).astype(BF16)
                do_bias = _mm(_bias_lanes(-delta), place).astype(BF16)
                for h in range(N_HEADS):
                    pair = slice((h // 2) * LANES, (h // 2 + 1) * LANES)
                    qa = _augment(q_ref[pl.ds(r0, tb), pair], h, q_bias, 1)
                    doa = _augment(do_ref[pl.ds(r0, tb), pair], h, do_bias, None)
                    qa_sc[h, pl.ds(r0, tb), :] = qa
                    doa_sc[h, pl.ds(r0, tb), :] = doa
                    qat_sc[h, i] = qa.astype(F32).T.astype(BF16)
                    doat_sc[h, i] = doa.astype(F32).T.astype(BF16)
                return carry

            lax.fori_loop(0, nb, rows_q, 0)

        c0 = pl.multiple_of(j * tb, tb)
        k_bias = _mm(_bias_lanes(-fc_ref[pl.ds(c0, tb), :]), _bias_placement(1)).astype(BF16)
        for h in range(N_HEADS):
            pair = slice((h // 2) * LANES, (h // 2 + 1) * LANES)
            ka_sc[h] = _augment(k_ref[:, pair] * scale, h, k_bias, 0)
            va_sc[h] = _augment(v_ref[:, pair], h, None, 0)
        dkt_sc[...] = jnp.zeros_like(dkt_sc)
        dvt_sc[...] = jnp.zeros_like(dvt_sc)
        causal = lax.broadcasted_iota(jnp.int32, (tb, tb), 1) <= lax.broadcasted_iota(jnp.int32, (tb, tb), 0)

        def step(i, masked):
            r0 = pl.multiple_of(i * tb, tb)
            for h in range(N_HEADS):
                s = _mm_nt(qa_sc[h, pl.ds(r0, tb), :], ka_sc[h])
                if masked:
                    s = jnp.where(causal, s, -jnp.inf)
                pr = jnp.exp(s)
                dvt_sc[h] += _mm(doat_sc[h, i], pr.astype(BF16))
                dsb = (pr * _mm_nt(doa_sc[h, pl.ds(r0, tb), :], va_sc[h])).astype(BF16)
                dkt_sc[h] += _mm(qat_sc[h, i], dsb)
                dq_acc[h, pl.ds(r0, tb), :] += _mm(dsb, ka_sc[h])

        step(j, True)

        def loop_body(i, carry):
            step(i, False)
            return carry

        lax.fori_loop(j + 1, nb, loop_body, 0)
        dfk = jnp.zeros((tb, LANES), F32)
        for p in range(N_PAIRS):
            dk = [dkt_sc[2 * p + hh].T for hh in range(2)]
            dv = [dvt_sc[2 * p + hh].T for hh in range(2)]
            dk_ref[:, p * LANES : (p + 1) * LANES] = (jnp.where(low, dk[0], dk[1]) * scale).astype(BF16)
            dv_ref[:, p * LANES : (p + 1) * LANES] = jnp.where(low, dv[0], dv[1]).astype(BF16)
            for hh in range(2):
                b = HEAD_DIM * (1 - hh) + 3
                dfk = jnp.where(lane == 2 * p + hh, -dk[hh][:, b : b + 1], dfk)
        dfk_ref[...] = dfk

        @pl.when(j == nb - 1)
        def _():
            def rows_dq(i, carry):
                r0 = pl.multiple_of(i * tb, tb)
                dfq = jnp.zeros((tb, LANES), F32)
                for p in range(N_PAIRS):
                    parts = [dq_acc[2 * p + hh, pl.ds(r0, tb), :] for hh in range(2)]
                    dq_ref[pl.ds(r0, tb), p * LANES : (p + 1) * LANES] = jnp.where(low, parts[0], parts[1]).astype(BF16)
                    for hh in range(2):
                        b = HEAD_DIM * (1 - hh)
                        dfq = jnp.where(lane == 2 * p + hh, parts[hh][:, b : b + 1], dfq)
                dfq_ref[pl.ds(r0, tb), :] = dfq
                return carry

            lax.fori_loop(0, nb, rows_dq, 0)

    seq = lambda w, col: pl.BlockSpec((S, w), lambda s, j: (s, col))
    seq_in = lambda w, col: pl.BlockSpec((S, w), lambda s, j: (s, col), pipeline_mode=pl.Buffered(1))
    blk = lambda w, col: pl.BlockSpec((tb, w), lambda s, j: (s * nb + j, col))
    return pl.pallas_call(
        body,
        name="attn_bwd",
        grid=(n_seq, nb),
        in_specs=[seq(ATTN_WIDTH, 0), blk(ATTN_WIDTH, 1), blk(ATTN_WIDTH, 2), seq(ATTN_WIDTH, 0), seq(ATTN_WIDTH, 0), seq_in(LANES, 0), seq_in(LANES, 0)],
        out_specs=[seq(ATTN_WIDTH, 0), blk(ATTN_WIDTH, 0), blk(ATTN_WIDTH, 0), blk(LANES, 0), seq(LANES, 0)],
        out_shape=[
            jax.ShapeDtypeStruct((T, ATTN_WIDTH), BF16),
            jax.ShapeDtypeStruct((T, ATTN_WIDTH), BF16),
            jax.ShapeDtypeStruct((T, ATTN_WIDTH), BF16),
            jax.ShapeDtypeStruct((T, LANES), F32),
            jax.ShapeDtypeStruct((T, LANES), F32),
        ],
        scratch_shapes=[
            pltpu.VMEM((N_HEADS, S, LANES), BF16),
            pltpu.VMEM((N_HEADS, S, LANES), BF16),
            pltpu.VMEM((N_HEADS, nb, LANES, tb), BF16),
            pltpu.VMEM((N_HEADS, nb, LANES, tb), BF16),
            pltpu.VMEM((N_HEADS, S, LANES), F32),
            pltpu.VMEM((N_HEADS, tb, LANES), BF16),
            pltpu.VMEM((N_HEADS, tb, LANES), BF16),
            pltpu.VMEM((N_HEADS, LANES, tb), F32),
            pltpu.VMEM((N_HEADS, LANES, tb), F32),
        ],
        compiler_params=_params(("parallel", "arbitrary"), VMEM_LIMIT_MAX),
    )(qkv, qkv, qkv, da, a, fcol, lse)


def _forget_bwd(dfk, dfq, fl, b_pad, n_seq, S):
    def body(df_ref, dfq_ref, fl_ref, b_ref, dfl_ref, db_ref):
        t = (df_ref[...] + dfq_ref[...]).T
        lane = lax.broadcasted_iota(jnp.int32, t.shape, 1)
        k = 1
        while k < S:
            t = t + jnp.where(lane < S - k, pltpu.roll(t, S - k, 1), 0.0)
            k *= 2
        dfl = t.T * _sigmoid(-(fl_ref[...] + b_ref[...]))
        dfl_ref[...] = dfl.astype(BF16)

        @pl.when(pl.program_id(0) == 0)
        def _():
            db_ref[...] = jnp.zeros_like(db_ref)

        db_ref[...] += jnp.sum(dfl, axis=0, keepdims=True)

    return pl.pallas_call(
        body,
        name="forget_bwd",
        grid=(n_seq,),
        in_specs=[
            pl.BlockSpec((S, LANES), lambda s: (s, 0)),
            pl.BlockSpec((S, LANES), lambda s: (s, 0)),
            pl.BlockSpec((S, FL_PAD), lambda s: (s, 0)),
            _const_spec((1, FL_PAD)),
        ],
        out_specs=[pl.BlockSpec((S, FL_PAD), lambda s: (s, 0)), pl.BlockSpec((1, FL_PAD), lambda s: (0, 0))],
        out_shape=[jax.ShapeDtypeStruct((n_seq * S, FL_PAD), BF16), jax.ShapeDtypeStruct((1, FL_PAD), F32)],
        compiler_params=_params(("arbitrary",)),
    )(dfk, dfq, fl, b_pad)


def _in_proj_bwd(du, dq, dk, dv, dfl, dgates, x, dx1, g1, w_uqkv, w_fl, w_g, token):
    T = x.shape[0]
    tm = ROW_TILE

    def body(du_ref, dq_ref, dk_ref, dv_ref, dfl_ref, dgt_ref, x_ref, dx1_ref, g_ref, wa_ref, wf_ref, wg_ref, token_ref, dx_ref, dg_ref):
        dz = jnp.concatenate([du_ref[...], dq_ref[...], dk_ref[...], dv_ref[...]], axis=1)
        dh = _mm_nt(dz, wa_ref[...]) + _mm_nt(dgt_ref[...], wg_ref[...]) + _mm_nt(dfl_ref[...], wf_ref[...])
        gv = g_ref[...]
        _, xh, r = _rms_fwd(x_ref[...], gv)
        dxn, dgrow = _rms_bwd(dh, xh, r, gv)
        dx_ref[...] = dx1_ref[...] + dxn

        @pl.when(pl.program_id(0) == 0)
        def _():
            dg_ref[...] = jnp.zeros_like(dg_ref)

        dg_ref[...] += jnp.sum(dgrow, axis=0, keepdims=True)

    row = lambda n: pl.BlockSpec((tm, n), lambda i: (i, 0))
    return pl.pallas_call(
        body,
        name="in_proj_bwd",
        grid=(T // tm,),
        in_specs=[
            row(512), row(512), row(512), row(512), row(FL_PAD), row(2 * D_MODEL), row(D_MODEL), row(D_MODEL), _const_spec((1, D_MODEL)),
            _const_spec(w_uqkv.shape), _const_spec(w_fl.shape), _const_spec(w_g.shape), _HBM,
        ],
        out_specs=[row(D_MODEL), pl.BlockSpec((1, D_MODEL), lambda i: (0, 0))],
        out_shape=[jax.ShapeDtypeStruct((T, D_MODEL), F32), jax.ShapeDtypeStruct((1, D_MODEL), F32)],
        compiler_params=_params(("arbitrary",)),
    )(du, dq, dk, dv, dfl, dgates, x, dx1, g1, w_uqkv, w_fl, w_g, token)


def _pick_block(n):
    for b in (1024, 512, 1408, 256, 128):
        if n % b == 0:
            return b
    raise ValueError(n)


def _matmul_tn(a, b, name, col_chunks=False):
    T, K = a.shape
    N = b.shape[1]
    bt, bk, bn = min(T, DW_TOKENS), _pick_block(K), _pick_block(N)
    nt = T // bt
    c = N // N_DEV
    assert not col_chunks or (bn == N and c % LANES == 0)

    def body(a_ref, b_ref, o_ref, acc):
        @pl.when(pl.program_id(2) == 0)
        def _():
            acc[...] = jnp.zeros_like(acc)

        acc[...] += _mm_tn(a_ref[...].astype(BF16), b_ref[...].astype(BF16))

        @pl.when(pl.program_id(2) == nt - 1)
        def _():
            if col_chunks:
                for d in range(N_DEV):
                    o_ref[d] = acc[:, d * c : (d + 1) * c].astype(BF16)
            else:
                o_ref[...] = acc[...].astype(BF16)

    if col_chunks:
        out_spec, out_shape = pl.BlockSpec((N_DEV, bk, c), lambda k, n, t: (0, k, 0)), (N_DEV, K, c)
    else:
        out_spec, out_shape = pl.BlockSpec((bk, bn), lambda k, n, t: (k, n)), (K, N)
    return pl.pallas_call(
        body,
        name=name,
        grid=(K // bk, N // bn, nt),
        in_specs=[pl.BlockSpec((bt, bk), lambda k, n, t: (t, k)), pl.BlockSpec((bt, bn), lambda k, n, t: (t, n))],
        out_specs=out_spec,
        out_shape=jax.ShapeDtypeStruct(out_shape, BF16),
        scratch_shapes=[pltpu.VMEM((bk, bn), F32)],
        compiler_params=_params(("parallel", "parallel", "arbitrary")),
    )(a, b)


def _matmul_tn_pair(a1, b1, a2, b2, name):
    T, K = a1.shape
    N = b1.shape[1]
    bt = min(T, DW_TOKENS)
    nt = T // bt
    c = N // N_DEV

    def body(a1_ref, b1_ref, a2_ref, b2_ref, o1_ref, o2_ref, acc1, acc2):
        @pl.when(pl.program_id(0) == 0)
        def _():
            acc1[...] = jnp.zeros_like(acc1)
            acc2[...] = jnp.zeros_like(acc2)

        acc1[...] += _mm_tn(a1_ref[...].astype(BF16), b1_ref[...].astype(BF16))
        acc2[...] += _mm_tn(a2_ref[...].astype(BF16), b2_ref[...].astype(BF16))

        @pl.when(pl.program_id(0) == nt - 1)
        def _():
            for d in range(N_DEV):
                o1_ref[d] = acc1[:, d * c : (d + 1) * c].astype(BF16)
                o2_ref[d] = acc2[:, d * c : (d + 1) * c].astype(BF16)

    lhs, rhs = pl.BlockSpec((bt, K), lambda t: (t, 0)), pl.BlockSpec((bt, N), lambda t: (t, 0))
    whole = pl.BlockSpec((N_DEV, K, c), lambda t: (0, 0, 0))
    return pl.pallas_call(
        body,
        name=name,
        grid=(nt,),
        in_specs=[lhs, rhs, lhs, rhs],
        out_specs=[whole, whole],
        out_shape=[jax.ShapeDtypeStruct((N_DEV, K, c), BF16)] * 2,
        scratch_shapes=[pltpu.VMEM((K, N), F32), pltpu.VMEM((K, N), F32)],
        compiler_params=_params(("arbitrary",)),
    )(a1, b1, a2, b2)


W_IN_A = POOL_WIDTH + 3 * ATTN_WIDTH
W_IN_SHARD = (W_IN_A + N_HEADS + 2 * D_MODEL) // N_DEV
_W_IN_PIECES = ((0, W_IN_A), (W_IN_A, W_IN_A + N_HEADS), (W_IN_A + N_HEADS, W_IN_A + N_HEADS + 2 * D_MODEL))


def _w_in_segments(d):
    lo, hi = d * W_IN_SHARD, (d + 1) * W_IN_SHARD
    out = []
    for p, (a, b) in enumerate(_W_IN_PIECES):
        s, e = max(lo, a), min(hi, b)
        if s < e:
            out.append((p, s - a, s - lo, e - s))
    return out


def _w_in_pieces(gathered, tails):
    tm = ROW_TILE // 2
    tail_rows = tm // LANES
    aligned = W_IN_SHARD - 1

    def body(g_ref, t_ref, wa_ref, wf_ref, wg_ref):
        outs = (wa_ref, wf_ref, wg_ref)
        wf_ref[...] = jnp.zeros_like(wf_ref)
        diagonal = lax.broadcasted_iota(jnp.int32, (LANES, LANES), 0) == lax.broadcasted_iota(jnp.int32, (LANES, LANES), 1)
        for d in range(N_DEV):
            for p, at, frm, n in _w_in_segments(d):
                m = min(n, aligned - frm)
                if m > 0:
                    outs[p][:, at : at + m] = g_ref[d, :, frm : frm + m]
                if frm + n == W_IN_SHARD:
                    column = [
                        jnp.sum(jnp.where(diagonal, jnp.broadcast_to(t_ref[d, k : k + 1, :], (LANES, LANES)), 0.0), axis=1, keepdims=True)
                        for k in range(tail_rows)
                    ]
                    outs[p][:, at + n - 1 : at + n] = jnp.concatenate(column, axis=0).astype(outs[p].dtype)

    return pl.pallas_call(
        body,
        name="w_in_pieces",
        grid=(D_MODEL // tm,),
        in_specs=[
            pl.BlockSpec((N_DEV, tm, aligned), lambda i: (0, i, 0)),
            pl.BlockSpec((N_DEV, None, tail_rows, LANES), lambda i: (0, i, 0, 0)),
        ],
        out_specs=[pl.BlockSpec((tm, W_IN_A), lambda i: (i, 0)), pl.BlockSpec((tm, FL_PAD), lambda i: (i, 0)), pl.BlockSpec((tm, 2 * D_MODEL), lambda i: (i, 0))],
        out_shape=[
            jax.ShapeDtypeStruct((D_MODEL, W_IN_A), gathered.dtype),
            jax.ShapeDtypeStruct((D_MODEL, FL_PAD), gathered.dtype),
            jax.ShapeDtypeStruct((D_MODEL, 2 * D_MODEL), gathered.dtype),
        ],
        compiler_params=_params(("parallel",)),
    )(gathered, tails.reshape(N_DEV, D_MODEL // tm, tail_rows, LANES))


def _dw_in(h, du, dq, dk, dv, dfl, dgates, token):
    T = h.shape[0]
    bt, bk = min(T, DW_TOKENS // 2), 512
    nt = T // bt
    pieces = (du, dq, dk, dv, dfl, dgates)
    offs = [0]
    for p in pieces:
        offs.append(offs[-1] + p.shape[1])

    aligned = W_IN_SHARD - 1
    tail_rows = bk // LANES

    def body(h_ref, *rest):
        refs, o_ref, t_ref, acc = rest[: len(pieces)], rest[-3], rest[-2], rest[-1]

        @pl.when(pl.program_id(1) == 0)
        def _():
            acc[...] = jnp.zeros_like(acc)

        ht = h_ref[...].T
        for ref, at in zip(refs, offs):
            acc[:, at : at + ref.shape[1]] += _mm(ht, ref[...])

        @pl.when(pl.program_id(1) == nt - 1)
        def _():
            starts = (0, W_IN_A, W_IN_A + FL_PAD)
            diagonal = lax.broadcasted_iota(jnp.int32, (LANES, LANES), 0) == lax.broadcasted_iota(jnp.int32, (LANES, LANES), 1)
            for d in range(N_DEV):
                for p, at, to, n in _w_in_segments(d):
                    m = min(n, aligned - to)
                    if m > 0:
                        o_ref[d, :, to : to + m] = acc[:, starts[p] + at : starts[p] + at + m].astype(BF16)
                    if to + n == W_IN_SHARD:
                        last = starts[p] + at + n - 1
                        column = acc[:, last : last + 1].astype(BF16).astype(F32)
                        for k in range(tail_rows):
                            rows = jnp.broadcast_to(column[k * LANES : (k + 1) * LANES], (LANES, LANES))
                            t_ref[d, k : k + 1, :] = jnp.sum(jnp.where(diagonal, rows, 0.0), axis=0, keepdims=True)

    main, tails = pl.pallas_call(
        body,
        name="dw_in",
        grid=(D_MODEL // bk, nt),
        in_specs=[pl.BlockSpec((bt, bk), lambda k, t: (t, k))] + [pl.BlockSpec((bt, p.shape[1]), lambda k, t: (t, 0)) for p in pieces] + [_HBM],
        out_specs=[
            pl.BlockSpec((N_DEV, bk, aligned), lambda k, t: (0, k, 0)),
            pl.BlockSpec((N_DEV, None, tail_rows, LANES), lambda k, t: (0, k, 0, 0)),
        ],
        out_shape=[
            jax.ShapeDtypeStruct((N_DEV, D_MODEL, aligned), BF16),
            jax.ShapeDtypeStruct((N_DEV, D_MODEL // bk, tail_rows, LANES), F32),
        ],
        scratch_shapes=[pltpu.VMEM((bk, offs[-1]), F32)],
        compiler_params=_params(("parallel", "arbitrary")),
    )(h, *pieces, token)
    return main, tails.reshape(N_DEV, D_MODEL // LANES, LANES)


def _position():
    return lax.axis_index("x"), lax.axis_index("y"), lax.axis_index("c")


_HBM = pl.BlockSpec(memory_space=pl.ANY)


def _all_gather(blocks, name):
    n = len(blocks)
    parts = [(a, q * (b.shape[0] // 4), b.shape[0] // 4) for a, b in enumerate(blocks) if b.shape[0] >= ROW_TILE for q in range(4)]
    parts += [(a, 0, b.shape[0]) for a, b in enumerate(blocks) if b.shape[0] < ROW_TILE]

    def body(*refs):
        xs, outs = refs[:n], refs[n : 2 * n]
        send_sems, recv_sems, local_sems = refs[2 * n :]
        x, y, c = _position()
        me, sibling = (x, y, c), (x, y, 1 - c)
        chips = [(1 - x, y), (x, 1 - y), (1 - x, 1 - y)]

        def rows(u, px, py, pc):
            a, lo, size = parts[u]
            return outs[a].at[4 * px + 2 * py + pc, pl.ds(lo, size)]

        def own(u):
            a, lo, size = parts[u]
            return xs[a].at[pl.ds(lo, size)]

        def copy(u, k, blk, to, src=None):
            return pltpu.make_async_remote_copy(
                src_ref=rows(u, *blk) if src is None else src, dst_ref=rows(u, *blk),
                send_sem=send_sems.at[7 * u + k], recv_sem=recv_sems.at[7 * u + k], device_id=to, device_id_type=MESH,
            )

        first = []
        for u in range(len(parts)):
            first += [copy(u, 1 + j, me, (*chip, c), src=own(u)) for j, chip in enumerate(chips)]
            first.append(copy(u, 0, me, sibling, src=own(u)))
        mine = [pltpu.make_async_copy(xs[a], outs[a].at[4 * x + 2 * y + c], local_sems.at[a]) for a in range(n)]
        for cp in first + mine:
            cp.start()
        passed = []
        for u in range(len(parts)):
            for j, chip in enumerate(chips):
                copy(u, 1 + j, (*chip, c), me).wait_recv()
                passed.append(copy(u, 4 + j, (*chip, c), sibling))
                passed[-1].start()
        for u in range(len(parts)):
            copy(u, 0, sibling, me).wait_recv()
            for j, chip in enumerate(chips):
                copy(u, 4 + j, (*chip, 1 - c), me).wait_recv()
        for cp in first + passed:
            cp.wait_send()
        for cp in mine:
            cp.wait()

    return pl.pallas_call(
        body,
        name=name,
        out_shape=[jax.ShapeDtypeStruct((N_DEV, *b.shape), b.dtype) for b in blocks],
        in_specs=[_HBM] * n,
        out_specs=[_HBM] * n,
        scratch_shapes=[pltpu.SemaphoreType.DMA((7 * len(parts),)), pltpu.SemaphoreType.DMA((7 * len(parts),)), pltpu.SemaphoreType.DMA((n,))],
    )(*blocks)


_SEM = pl.BlockSpec(memory_space=pltpu.SEMAPHORE)
_HBM_ONLY = pl.BlockSpec(memory_space=pltpu.HBM)
_SIDE_EFFECT = pltpu.SideEffectType.DATAFLOW_SIDE_EFFECTING


def _peer(x, y, c, k):
    return (1 - x if k & 4 else x, 1 - y if k & 2 else y, 1 - c if k & 1 else c)


_PEER_BITS = {"gather": range(1, N_DEV), "gather_half": (1, 4, 2, 6), "forward": (4, 2, 6), "scatter": range(1, N_DEV)}
_GATHERS = ("gather", "gather_half")


def _exchange_copies(src_refs, land_refs, send_sems, recv_sems, pattern, receive_side):
    x, y, c = _position()
    me = 4 * x + 2 * y + c
    bits = _PEER_BITS[pattern]
    cps = []
    for j, k in enumerate(bits):
        px, py, pc = _peer(x, y, c, k)
        peer = 4 * px + 2 * py + pc
        for a, (src, land) in enumerate(zip(src_refs, land_refs)):
            to = (px, py, pc)
            if pattern == "forward":
                slot = 4 * px + 2 * py + (1 - c if receive_side else c)
                s, to = land.at[slot], (x, y, 1 - c)
            else:
                s, slot = (src if pattern in _GATHERS else src.at[peer]), (peer if receive_side else me)
            cps.append(pltpu.make_async_remote_copy(
                src_ref=s, dst_ref=land.at[slot],
                send_sem=send_sems.at[len(bits) * a + j], recv_sem=recv_sems.at[len(bits) * a + j],
                device_id=to, device_id_type=MESH,
            ))
    return cps


def _own_copies(src_refs, land_refs, own_sems):
    x, y, c = _position()
    return [
        pltpu.make_async_copy(src, land.at[4 * x + 2 * y + c], own_sems.at[a])
        for a, (src, land) in enumerate(zip(src_refs, land_refs))
    ]


def _exchange_start(srcs, after, name, pattern):
    n = len(srcs)
    m = len(_PEER_BITS[pattern])
    lands = [jax.ShapeDtypeStruct((N_DEV, *s.shape[-2:]), s.dtype) for s in srcs]

    def body(*refs):
        src_refs, land_refs = refs[1 : 1 + n], refs[1 + n : 1 + 2 * n]
        send_sems, recv_sems, own_sems = refs[1 + 2 * n : 4 + 2 * n]
        token = refs[-1]
        if pattern in _GATHERS:
            for cp in _own_copies(src_refs, land_refs, own_sems):
                cp.start()
        for cp in _exchange_copies(src_refs, land_refs, send_sems, recv_sems, pattern, receive_side=False):
            cp.start()
        token[...] = jnp.zeros_like(token)

    hbm = lambda t: pltpu.with_memory_space_constraint(t, pltpu.HBM)
    out = pl.pallas_call(
        body,
        name=name,
        out_shape=(
            pltpu.SemaphoreType.DMA((m * n,)), pltpu.SemaphoreType.DMA((m * n,)), pltpu.SemaphoreType.DMA((n,)),
            *[pltpu.HBM(s.shape, s.dtype) for s in srcs], *[pltpu.HBM(l.shape, l.dtype) for l in lands],
            jax.ShapeDtypeStruct((8, LANES), F32),
        ),
        in_specs=(_HBM, *[_HBM_ONLY] * (2 * n)),
        out_specs=(_SEM, _SEM, _SEM, *[_HBM_ONLY] * (2 * n), pl.BlockSpec(memory_space=pltpu.VMEM)),
        input_output_aliases={1 + i: 3 + i for i in range(2 * n)},
        compiler_params=pltpu.CompilerParams(has_side_effects=_SIDE_EFFECT),
    )(after, *[hbm(s) for s in srcs], *[hbm(lax.empty(l.shape, l.dtype)) for l in lands])
    return out[:3], out[3 : 3 + n], out[3 + n : 3 + 2 * n], out[-1]


def _exchange_wait(sems, srcs, lands, after, name, pattern):
    n = len(srcs)

    def body(*refs):
        src_refs, land_refs = refs[:n], refs[n : 2 * n]
        send_sems, recv_sems, own_sems = refs[2 * n : 2 * n + 3]
        if pattern in _GATHERS:
            for cp in _own_copies(src_refs, land_refs, own_sems):
                cp.wait()
        for cp in _exchange_copies(src_refs, land_refs, send_sems, recv_sems, pattern, receive_side=True):
            cp.wait_send()
            cp.wait_recv()

    out = pl.pallas_call(
        body,
        name=name,
        out_shape=(*[pltpu.HBM(s.shape, s.dtype) for s in srcs], *[pltpu.HBM(l.shape, l.dtype) for l in lands]),
        in_specs=(*[_HBM_ONLY] * (2 * n), _SEM, _SEM, _SEM, _HBM),
        out_specs=tuple([_HBM_ONLY] * (2 * n)),
        input_output_aliases={i: i for i in range(2 * n)},
        compiler_params=pltpu.CompilerParams(has_side_effects=_SIDE_EFFECT),
    )(*srcs, *lands, *sems, after)
    return out[:n], out[n:]


def _gather_forward(sems, srcs, lands, after, name):
    n = len(srcs)
    m = len(_PEER_BITS["forward"])

    def body(*refs):
        src_refs, land_refs = refs[:n], refs[n : 2 * n]
        send_sems, recv_sems, own_sems = refs[2 * n : 2 * n + 3]
        forward_send, forward_recv, token = refs[2 * n + 4], refs[2 * n + 5], refs[-1]
        for cp in _own_copies(src_refs, land_refs, own_sems):
            cp.wait()
        for cp in _exchange_copies(src_refs, land_refs, send_sems, recv_sems, "gather_half", receive_side=True):
            cp.wait_send()
            cp.wait_recv()
        for cp in _exchange_copies(land_refs, land_refs, forward_send, forward_recv, "forward", receive_side=False):
            cp.start()
        token[...] = jnp.zeros_like(token)

    out = pl.pallas_call(
        body,
        name=name,
        out_shape=(
            pltpu.SemaphoreType.DMA((m * n,)), pltpu.SemaphoreType.DMA((m * n,)),
            *[pltpu.HBM(l.shape, l.dtype) for l in lands], jax.ShapeDtypeStruct((8, LANES), F32),
        ),
        in_specs=(*[_HBM_ONLY] * (2 * n), _SEM, _SEM, _SEM, _HBM),
        out_specs=(_SEM, _SEM, *[_HBM_ONLY] * n, pl.BlockSpec(memory_space=pltpu.VMEM)),
        input_output_aliases={n + i: 2 + i for i in range(n)},
        compiler_params=pltpu.CompilerParams(has_side_effects=_SIDE_EFFECT),
    )(*srcs, *lands, *sems, after)
    return out[:2], out[2 : 2 + n], out[-1]


def _forward_wait(sems, lands, after, name):
    n = len(lands)

    def body(*refs):
        land_refs = refs[:n]
        for cp in _exchange_copies(land_refs, land_refs, refs[n], refs[n + 1], "forward", receive_side=True):
            cp.wait_send()
            cp.wait_recv()

    return pl.pallas_call(
        body,
        name=name,
        out_shape=tuple(pltpu.HBM(l.shape, l.dtype) for l in lands),
        in_specs=(*[_HBM_ONLY] * n, _SEM, _SEM, _HBM),
        out_specs=tuple([_HBM_ONLY] * n),
        input_output_aliases={i: i for i in range(n)},
        compiler_params=pltpu.CompilerParams(has_side_effects=_SIDE_EFFECT),
    )(*lands, *sems, after)


def _rows_tile(r):
    return ROW_TILE if r % ROW_TILE == 0 else r


def _adamw(w, g, m, v):
    m = ADAM_B1 * m + (1.0 - ADAM_B1) * g
    v = ADAM_B2 * v + (1.0 - ADAM_B2) * (g * g)
    m_hat = m / (1.0 - ADAM_B1 ** ADAM_STEP)
    v_hat = v / (1.0 - ADAM_B2 ** ADAM_STEP)
    delta = -ADAM_LR * (m_hat / (jnp.sqrt(v_hat) + ADAM_EPS) + ADAM_WD * w)
    return delta, m, v


def _shard_update_direct(parts, chunks, w, m, v, me, name):
    _, r, c = w.shape
    br = _rows_tile(r)

    def body(me_ref, p_ref, own_ref, w_ref, m_ref, v_ref, g_ref, d_ref, nm_ref, nv_ref):
        g = None
        for n in range(N_DEV):
            part = jnp.where(me_ref[0] == n, own_ref[...], p_ref[n]).astype(F32)
            g = part if g is None else g + part
        g_ref[...] = g
        d_ref[...], nm_ref[...], nv_ref[...] = _adamw(w_ref[...], g, m_ref[...], v_ref[...])

    shard = pl.BlockSpec((None, br, c), lambda i, me: (0, i, 0))
    return pl.pallas_call(
        body,
        name=name,
        grid_spec=pltpu.PrefetchScalarGridSpec(
            num_scalar_prefetch=1,
            grid=(r // br,),
            in_specs=[
                pl.BlockSpec((N_DEV, br, c), lambda i, me: (0, i, 0)),
                pl.BlockSpec((None, br, c), lambda i, me: (me[0], i, 0)),
                shard, shard, shard,
            ],
            out_specs=[shard, shard, shard, shard],
        ),
        out_shape=[jax.ShapeDtypeStruct((1, r, c), F32)] * 4,
        compiler_params=_params(("parallel",)),
    )(me, parts, chunks, w, m, v)


def _w_in_update(parts, chunks, tail_parts, tail_chunks, w, m, v, me, name):
    _, r, c = w.shape
    br = _rows_tile(r)
    tail_rows = br // LANES

    def body(me_ref, p_ref, own_ref, tp_ref, town_ref, w_ref, m_ref, v_ref, g_ref, d_ref, nm_ref, nv_ref):
        g = tail = None
        for n in range(N_DEV):
            mine = me_ref[0] == n
            part = jnp.where(mine, own_ref[...], p_ref[n]).astype(F32)
            last = jnp.where(mine, town_ref[...], tp_ref[n])
            g = part if g is None else g + part
            tail = last if tail is None else tail + last
        diagonal = lax.broadcasted_iota(jnp.int32, (LANES, LANES), 0) == lax.broadcasted_iota(jnp.int32, (LANES, LANES), 1)
        column = jnp.concatenate(
            [
                jnp.sum(jnp.where(diagonal, jnp.broadcast_to(tail[k : k + 1, :], (LANES, LANES)), 0.0), axis=1, keepdims=True)
                for k in range(tail_rows)
            ],
            axis=0,
        )
        for lo, hi, grad in ((0, c - 1, g), (c - 1, c, column)):
            g_ref[:, lo:hi] = grad
            d_ref[:, lo:hi], nm_ref[:, lo:hi], nv_ref[:, lo:hi] = _adamw(w_ref[:, lo:hi], grad, m_ref[:, lo:hi], v_ref[:, lo:hi])

    shard = pl.BlockSpec((None, br, c), lambda i, me: (0, i, 0))
    by_block = lambda t: t.reshape(N_DEV, r // br, tail_rows, LANES)
    return pl.pallas_call(
        body,
        name=name,
        grid_spec=pltpu.PrefetchScalarGridSpec(
            num_scalar_prefetch=1,
            grid=(r // br,),
            in_specs=[
                pl.BlockSpec((N_DEV, br, c - 1), lambda i, me: (0, i, 0)),
                pl.BlockSpec((None, br, c - 1), lambda i, me: (me[0], i, 0)),
                pl.BlockSpec((N_DEV, None, tail_rows, LANES), lambda i, me: (0, i, 0, 0)),
                pl.BlockSpec((None, None, tail_rows, LANES), lambda i, me: (me[0], i, 0, 0)),
                shard, shard, shard,
            ],
            out_specs=[shard, shard, shard, shard],
        ),
        out_shape=[jax.ShapeDtypeStruct((1, r, c), F32)] * 4,
        compiler_params=_params(("parallel",)),
    )(me, parts, chunks, by_block(tail_parts), by_block(tail_chunks), w, m, v)


def _small_update(parts, first_rows, ws, ms, vs):
    k = len(ws)

    def unpacked(rows, shape):
        if len(shape) == 2 and shape[1] <= LANES:
            return rows[0:1, : shape[1]]
        if len(shape) == 2:
            return jnp.concatenate([rows[r : r + 1] for r in range(shape[1] // LANES)], axis=1)
        return rows.reshape(shape)

    def body(p_ref, f_ref, *refs):
        w_refs, m_refs, v_refs = refs[:k], refs[k : 2 * k], refs[2 * k : 3 * k]
        outs, loss_ref = refs[3 * k : 7 * k], refs[7 * k]
        g, first = p_ref[0], f_ref[0]
        for n in range(1, N_DEV):
            g = g + p_ref[n]
            first = first + f_ref[n]
        g = jnp.concatenate([g[:8] + first, g[8:]], axis=0)
        off = 0
        for i, (_, rows) in enumerate(_SMALL):
            gi = unpacked(g[off : off + rows], w_refs[i].shape)
            off += rows
            outs[i][...] = gi
            outs[k + i][...], outs[2 * k + i][...], outs[3 * k + i][...] = _adamw(w_refs[i][...], gi, m_refs[i][...], v_refs[i][...])
        loss_ref[...] = g[off : off + 1, 0:1]

    out = pl.pallas_call(
        body,
        name="small_update",
        out_shape=[jax.ShapeDtypeStruct(w.shape, F32) for _ in range(4) for w in ws] + [jax.ShapeDtypeStruct((1, 1), F32)],
        compiler_params=pltpu.CompilerParams(vmem_limit_bytes=VMEM_LIMIT),
    )(parts, first_rows, *ws, *ms, *vs)
    return [out[a * k : (a + 1) * k] for a in range(4)], out[4 * k]


_SHARD_AXIS = (1, 1, 1, 0, 0, 0, 0)
_TRANSPOSED = (False, False, False, False, True, True, False)


def _full_from_gathered(t, axis):
    if axis == 0:
        return t.reshape(N_DEV * t.shape[1], t.shape[2])
    return t


_SMALL = (("norm1_g", 8), ("norm2_g", 8), ("norm_f_g", 8), ("b_forget", 8), ("pool_scale", 8), ("pool_mix", 512))


def _pack_small(vals, loss_row):
    parts = []
    for (name, rows), t in zip(_SMALL, vals):
        f = t.astype(F32).reshape(-1)
        f = jnp.concatenate([f, jnp.zeros((rows * LANES - f.shape[0],), F32)]).reshape(rows, LANES)
        parts.append(f)
    parts.append(loss_row)
    return jnp.concatenate(parts, axis=0)


def _local_grads(x, tgt, g1, g2, gf, b_forget, pool_mix, pool_scale, w_in, fwd_token, out_weights, ffn_weights, ffn_grads_out, out_grads_out, small_grads_out, in_grads_out, norm1_grad_out):
    n_seq, S, _ = x.shape
    T = n_seq * S
    x2 = x.reshape(T, D_MODEL)
    tg2 = tgt.reshape(T, D_MODEL)
    w_uqkv, w_fl, w_g = w_in
    b_pad = jnp.concatenate([b_forget.reshape(1, N_HEADS), jnp.zeros((1, FL_PAD - N_HEADS), F32)], axis=1)
    mix_b = pool_mix.reshape(len(POOL_WINDOWS), GROUP_DIM, GROUP_DIM).astype(BF16)
    scale = pool_scale.reshape(1, POOL_WIDTH)
    g1 = g1.reshape(1, D_MODEL)
    g2 = g2.reshape(1, D_MODEL)
    gf = gf.reshape(1, D_MODEL)

    h, u, qkv, fl, gates = _in_proj(x2, g1, w_uqkv, w_fl, w_g, fwd_token)
    fcol = _forget_fwd(fl, b_pad, n_seq, S)
    pm, p2, p3 = _pool_fwd(u, mix_b, scale, n_seq, S)
    a, lse = _attn_fwd(qkv, fcol, n_seq, S)
    w_po, w_ao, w_out = out_weights(a)
    merged, x1, attn_y, pool_y = _mix_out(a, p3, gates, x2, w_ao, w_po, w_out)
    w_gate_t, w_up_t, w_down = ffn_weights(x1)
    h2, gate, up, act, dx2, loss_rows, dgf = _ffn_fwd(x1, g2, gf, tg2, w_gate_t, w_up_t, w_down)

    dgate, dup, dx1, dg2 = _ffn_bwd(dx2, gate, up, x1, g2, w_gate_t, w_up_t, w_down)
    bwd_token = ffn_grads_out(_matmul_tn(dgate, h2, "dw_ffn_gate"), _matmul_tn(dup, h2, "dw_ffn_up"), _matmul_tn(act, dx2, "dw_ffn_down"))
    dgates, dpy, day, da, dp2, dscale = _mix_bwd(dx1, gates, pool_y, attn_y, p2, scale, w_out, w_ao, w_po, bwd_token)
    out_token = out_grads_out(*_matmul_tn_pair(p3, dpy, a, day, "dw_pool_attn_out"), _matmul_tn(merged, dx1, "dw_out"))
    du, dmix = _pool_bwd(dp2, pm, mix_b, out_token, n_seq, S)
    dq, dk, dv, dfk, dfq = _attn_bwd(qkv, da, a, fcol, lse, n_seq, S)
    dfl, db = _forget_bwd(dfk, dfq, fl, b_pad, n_seq, S)
    small_token = small_grads_out((jnp.zeros_like(g1), dg2, dgf, db[:, :N_HEADS], dscale, dmix), loss_rows)
    in_token = in_grads_out(*_dw_in(h, du, dq, dk, dv, dfl, dgates, small_token))
    dx, dg1 = _in_proj_bwd(du, dq, dk, dv, dfl, dgates, x2, dx1, g1, w_uqkv, w_fl, w_g, in_token)
    norm1_grad_out(dg1)
    return dx.reshape(n_seq, S, D_MODEL)


def kernel(x, norm1_g, w_in, b_forget, pool_mix, pool_scale, w_pool_out, w_attn_out, w_out, norm2_g, w_ffn_gate, w_ffn_up, w_ffn_down, norm_f_g, loss_target, m_norm1_g, m_w_in, m_b_forget, m_pool_mix, m_pool_scale, m_w_pool_out, m_w_attn_out, m_w_out, m_norm2_g, m_w_ffn_gate, m_w_ffn_up, m_w_ffn_down, m_norm_f_g, v_norm1_g, v_w_in, v_b_forget, v_pool_mix, v_pool_scale, v_w_pool_out, v_w_attn_out, v_w_out, v_norm2_g, v_w_ffn_gate, v_w_ffn_up, v_w_ffn_down, v_norm_f_g):
    names = ("w_in", "w_pool_out", "w_attn_out", "w_out", "w_ffn_gate", "w_ffn_up", "w_ffn_down")
    w_sh = (w_in, w_pool_out, w_attn_out, w_out, w_ffn_gate, w_ffn_up, w_ffn_down)
    m_sh = (m_w_in, m_w_pool_out, m_w_attn_out, m_w_out, m_w_ffn_gate, m_w_ffn_up, m_w_ffn_down)
    v_sh = (v_w_in, v_w_pool_out, v_w_attn_out, v_w_out, v_w_ffn_gate, v_w_ffn_up, v_w_ffn_down)

    cx, cy, cc = _position()
    me = 4 * cx + 2 * cy + cc
    def stored(t, transposed):
        return jnp.transpose(t, (0, 2, 1)) if transposed else t

    w_sh, m_sh, v_sh = ([stored(t, tr) for t, tr in zip(ts, _TRANSPOSED)] for ts in (w_sh, m_sh, v_sh))
    shards = [w[0].astype(BF16) for w in w_sh]
    last_in = shards[0][:, W_IN_SHARD - 1].astype(F32).reshape(D_MODEL // LANES, LANES)
    gathered_in, tails_in = _all_gather([shards[0][:, : W_IN_SHARD - 1], last_in], "w_in_all_gather")
    out_sems = _exchange_start(shards[1:4], gathered_in, "out_weights_gather_start", "gather")
    ffn_sems = _exchange_start(shards[4:], out_sems[3], "ffn_weights_gather_start", "gather_half")
    no_order = jnp.zeros((8, LANES), F32)
    started = {}

    def out_weights(after):
        forward_sems, lands, token = _gather_forward(*ffn_sems[:3], after, "ffn_weights_forward_start")
        started["forward"] = (forward_sems, lands)
        _, lands = _exchange_wait(*out_sems[:3], token, "out_weights_gather_wait", "gather")
        return [_full_from_gathered(t, axis) for t, axis in zip(lands, _SHARD_AXIS[out])]

    def ffn_weights(after):
        lands = _forward_wait(*started["forward"], after, "ffn_weights_gather_wait")
        return [_full_from_gathered(t, axis) for t, axis in zip(lands, _SHARD_AXIS[ffn])]

    def hold_ffn_grads(*whole_grads):
        started["held"] = whole_grads
        return no_order

    def scatter_grads(*out_grads):
        chunks = [
            t if axis == 1 else t.reshape(N_DEV, -1, t.shape[1])
            for t, axis in zip((*out_grads, *started["held"]), _SHARD_AXIS[scattered])
        ]
        started["scatter"] = _exchange_start(chunks, no_order, "grads_scatter_start", "scatter")
        return started["scatter"][3]

    def gather_small(small, loss_rows):
        started["small"] = _exchange_start([_pack_small(small, loss_rows)], no_order, "small_grads_gather_start", "gather")
        return started["small"][3]

    def scatter_w_in(chunks_in, tails_in):
        started["in"] = _exchange_start([chunks_in, tails_in], no_order, "w_in_grads_scatter_start", "scatter")
        return started["in"][3]

    def gather_norm1(dg1):
        rows = jnp.reshape(dg1, (8, LANES))
        started["norm1"] = _exchange_start([rows], no_order, "norm1_grad_gather_start", "gather")

    ffn, out, scattered = slice(4, 7), slice(1, 4), slice(1, 7)
    grad_x = _local_grads(
        x, loss_target, norm1_g, norm2_g, norm_f_g, b_forget, pool_mix, pool_scale, _w_in_pieces(gathered_in, tails_in), ffn_sems[3],
        out_weights, ffn_weights, hold_ffn_grads, scatter_grads, gather_small, scatter_w_in, gather_norm1,
    )
    me_index = jnp.reshape(me, (1,)).astype(jnp.int32)

    srcs, lands = _exchange_wait(*started["scatter"][:3], started["norm1"][3], "grads_scatter_wait", "scatter")
    updates = [
        _shard_update_direct(p, s, w, m, v, me_index, "update_" + n)
        for p, s, w, m, v, n in zip(lands, srcs, w_sh[scattered], m_sh[scattered], v_sh[scattered], names[scattered])
    ]
    updates_out, updates_ffn = updates[:3], updates[3:]

    small_w = (norm1_g, norm2_g, norm_f_g, b_forget, pool_scale, pool_mix)
    small_m = (m_norm1_g, m_norm2_g, m_norm_f_g, m_b_forget, m_pool_scale, m_pool_mix)
    small_v = (v_norm1_g, v_norm2_g, v_norm_f_g, v_b_forget, v_pool_scale, v_pool_mix)
    (sent_in, sent_tails), (parts_in, parts_tails) = _exchange_wait(*started["in"][:3], updates_ffn[-1][0], "w_in_grads_scatter_wait", "scatter")
    update_in = _w_in_update(parts_in, sent_in, parts_tails, sent_tails, w_in, m_w_in, v_w_in, me_index, "update_w_in")

    def gathered_small(key, after, name):
        _, lands = _exchange_wait(*started[key][:3], after, name, "gather")
        return lands[0]

    parts = gathered_small("small", update_in[0], "small_grads_gather_wait")
    first_rows = gathered_small("norm1", parts, "norm1_grad_gather_wait")
    (g_s, d_s, nm_s, nv_s), loss = _small_update(parts, first_rows, small_w, small_m, small_v)
    g_w, d_w, nm_w, nv_w = zip(*(
        [stored(t, tr) for t in u] for u, tr in zip([update_in] + updates_out + updates_ffn, _TRANSPOSED)
    ))
    loss = loss.reshape(())
    (g1, g2, gf, gb, gsc, gmix), (d1, d2, df, db_, dsc, dmx) = g_s, d_s
    (m1, m2, mf, mb, msc, mmx), (v1, v2, vf, vb, vsc, vmx) = nm_s, nv_s

    def ordered(n1, win, b, mix, sc, wpo, wao, wout, n2, wg, wu, wd, nf):
        return (n1, win, b, mix, sc, wpo, wao, wout, n2, wg, wu, wd, nf)

    grads = ordered(g1, g_w[0], gb, gmix, gsc, g_w[1], g_w[2], g_w[3], g2, g_w[4], g_w[5], g_w[6], gf)
    deltas = ordered(d1, d_w[0], db_, dmx, dsc, d_w[1], d_w[2], d_w[3], d2, d_w[4], d_w[5], d_w[6], df)
    new_m = ordered(m1, nm_w[0], mb, mmx, msc, nm_w[1], nm_w[2], nm_w[3], m2, nm_w[4], nm_w[5], nm_w[6], mf)
    new_v = ordered(v1, nv_w[0], vb, vmx, vsc, nv_w[1], nv_w[2], nv_w[3], v2, nv_w[4], nv_w[5], nv_w[6], vf)
    return (loss, grad_x, *grads, *deltas, *new_m, *new_v)
```

```python
import jax
import jax.numpy as jnp
from jax import lax
from jax.experimental import pallas as pl
from jax.experimental.pallas import tpu as pltpu

F32 = jnp.float32
BF16 = jnp.bfloat16
MESH = pl.DeviceIdType.MESH

D_MODEL = 1024
POOL_WINDOWS = (2, 4, 8, 16)
POOL_WIDTH = 512
GROUP_DIM = 128
ATTN_WIDTH = 512
HEAD_DIM = 64
N_HEADS = 8
N_PAIRS = 4
D_FF = 2816
RMS_EPS = 1e-6
N_DEV = 8
LANES = 128
FL_PAD = 128

ADAM_LR = 0.001
ADAM_B1 = 0.9
ADAM_B2 = 0.999
ADAM_EPS = 1e-08
ADAM_WD = 0.01
ADAM_STEP = 10

VMEM_LIMIT = 56 * 1024 * 1024
VMEM_LIMIT_MAX = 60 * 1024 * 1024
ROW_TILE = 512
ATTN_BLOCK = 512
FF_CHUNK = 256
FF_ROW_TILE = 512
DW_TOKENS = 2048


def _mm(a, b):
    return jnp.dot(a, b, preferred_element_type=F32)


def _mm_nt(a, b):
    return lax.dot_general(a, b, (((1,), (1,)), ((), ())), preferred_element_type=F32)


def _mm_tn(a, b):
    return lax.dot_general(a, b, (((0,), (0,)), ((), ())), preferred_element_type=F32)


def _whole_cols(w_ref):
    if len(w_ref.shape) == 2:
        return w_ref[...]
    return jnp.concatenate([w_ref[d] for d in range(w_ref.shape[0])], axis=1)


def _sigmoid(x):
    return 1.0 / (1.0 + jnp.exp(-x))


def _params(sem, vmem=VMEM_LIMIT):
    return pltpu.CompilerParams(dimension_semantics=sem, vmem_limit_bytes=vmem)


def _const_spec(shape):
    nd = len(shape)
    return pl.BlockSpec(shape, lambda *_: (0,) * nd, pipeline_mode=pl.Buffered(1))


def _rms_fwd(x, g):
    r = lax.rsqrt(jnp.mean(x * x, axis=-1, keepdims=True) + RMS_EPS)
    xh = x * r
    return xh * g, xh, r


def _rms_bwd(dy, xh, r, g):
    dxh = dy * g
    dx = r * (dxh - xh * jnp.mean(dxh * xh, axis=-1, keepdims=True))
    return dx, dy * xh


def _in_proj(x, g1, w_uqkv, w_fl, w_g, token):
    T = x.shape[0]
    tm = ROW_TILE

    def body(x_ref, g_ref, wa_ref, wf_ref, wg_ref, token_ref, h_ref, u_ref, qkv_ref, fl_ref, gt_ref):
        h, _, _ = _rms_fwd(x_ref[...], g_ref[...])
        hb = h.astype(BF16)
        h_ref[...] = hb
        z = _mm(hb, wa_ref[...])
        u_ref[...] = z[:, :POOL_WIDTH]
        qkv_ref[...] = z[:, POOL_WIDTH:].astype(BF16)
        fl_ref[...] = _mm(hb, wf_ref[...])
        gt_ref[...] = _mm(hb, wg_ref[...]).astype(BF16)

    row = lambda n: pl.BlockSpec((tm, n), lambda i: (i, 0))
    return pl.pallas_call(
        body,
        name="in_proj",
        grid=(T // tm,),
        in_specs=[row(D_MODEL), _const_spec((1, D_MODEL)), _const_spec(w_uqkv.shape), _const_spec(w_fl.shape), _const_spec(w_g.shape), _HBM],
        out_specs=[row(D_MODEL), row(POOL_WIDTH), row(3 * ATTN_WIDTH), row(FL_PAD), row(2 * D_MODEL)],
        out_shape=[
            jax.ShapeDtypeStruct((T, D_MODEL), BF16),
            jax.ShapeDtypeStruct((T, POOL_WIDTH), F32),
            jax.ShapeDtypeStruct((T, 3 * ATTN_WIDTH), BF16),
            jax.ShapeDtypeStruct((T, FL_PAD), F32),
            jax.ShapeDtypeStruct((T, 2 * D_MODEL), BF16),
        ],
        compiler_params=_params(("parallel",)),
    )(x, g1, w_uqkv, w_fl, w_g, token)


def _log_sigmoid(x):
    return jnp.minimum(x, 0.0) - jnp.log(1.0 + jnp.exp(-jnp.abs(x)))


def _forget_fwd(fl, b_pad, n_seq, S):
    def body(fl_ref, b_ref, fcol_ref):
        lf = _log_sigmoid(fl_ref[...] + b_ref[...])
        t = lf.T
        lane = lax.broadcasted_iota(jnp.int32, t.shape, 1)
        k = 1
        while k < S:
            t = t + jnp.where(lane >= k, pltpu.roll(t, k, 1), 0.0)
            k *= 2
        fcol_ref[...] = t.T

    return pl.pallas_call(
        body,
        name="forget_fwd",
        grid=(n_seq,),
        in_specs=[pl.BlockSpec((S, FL_PAD), lambda s: (s, 0)), _const_spec((1, FL_PAD))],
        out_specs=pl.BlockSpec((S, FL_PAD), lambda s: (s, 0)),
        out_shape=jax.ShapeDtypeStruct((n_seq * S, FL_PAD), F32),
        compiler_params=_params(("parallel",)),
    )(fl, b_pad)


def _window_pick(g, v2, v4, v8, v16):
    return jnp.where(g == 0, v2, jnp.where(g == 1, v4, jnp.where(g == 2, v8, v16)))


def _pool_fwd(u, mix_b, scale, n_seq, S):
    T = n_seq * S

    def body(u_ref, mix_ref, sc_ref, pm_ref, p2_ref, p3_ref):
        g = pl.program_id(1)
        uu = u_ref[...]
        row = lax.broadcasted_iota(jnp.int32, uu.shape, 0)

        def back(a, k):
            return jnp.where(row >= k, pltpu.roll(a, k, 0), 0.0)

        s2 = uu + back(uu, 1)
        s4 = s2 + back(s2, 2)
        s8 = s4 + back(s4, 4)
        s16 = s8 + back(s8, 8)
        w = _window_pick(g, 2.0, 4.0, 8.0, 16.0)
        cnt = jnp.minimum((row + 1).astype(F32), w)
        pm = _window_pick(g, s2, s4, s8, s16) / cnt - uu
        pmb = pm.astype(BF16)
        pm_ref[...] = pmb
        p2 = _mm(pmb, mix_ref[...])
        p2_ref[...] = p2
        p3_ref[...] = (p2 * sc_ref[...]).astype(BF16)

    grp = pl.BlockSpec((S, GROUP_DIM), lambda s, g: (s, g))
    return pl.pallas_call(
        body,
        name="pool_fwd",
        grid=(n_seq, len(POOL_WINDOWS)),
        in_specs=[
            grp,
            pl.BlockSpec((None, GROUP_DIM, GROUP_DIM), lambda s, g: (g, 0, 0)),
            pl.BlockSpec((1, GROUP_DIM), lambda s, g: (0, g)),
        ],
        out_specs=[grp, grp, grp],
        out_shape=[
            jax.ShapeDtypeStruct((T, POOL_WIDTH), BF16),
            jax.ShapeDtypeStruct((T, POOL_WIDTH), F32),
            jax.ShapeDtypeStruct((T, POOL_WIDTH), BF16),
        ],
        compiler_params=pltpu.CompilerParams(
            dimension_semantics=("parallel", "parallel"), vmem_limit_bytes=VMEM_LIMIT, allow_input_fusion=[False, True, False]
        ),
    )(u, mix_b, scale)


def _split3(v):
    hi = v.astype(BF16).astype(F32)
    r = v - hi
    mid = r.astype(BF16).astype(F32)
    lo = (r - mid).astype(BF16).astype(F32)
    return hi, mid, lo


def _bias_lanes(v):
    hi, mid, lo = _split3(v)
    lane = lax.broadcasted_iota(jnp.int32, (1, LANES), 1)
    packed = jnp.where(lane < N_HEADS, hi, jnp.where(lane < 2 * N_HEADS, pltpu.roll(mid, N_HEADS, 1), pltpu.roll(lo, 2 * N_HEADS, 1)))
    return jnp.where(lane < 3 * N_HEADS, packed, 0.0).astype(BF16)


def _bias_placement(slot):
    row = lax.broadcasted_iota(jnp.int32, (LANES, N_HEADS * LANES), 0)
    col = lax.broadcasted_iota(jnp.int32, (LANES, N_HEADS * LANES), 1)
    h = col // LANES
    n = col % LANES - jnp.where(h % 2 == 0, HEAD_DIM, 0) - 3 * slot
    return ((n >= 0) & (n < 3) & (row == N_HEADS * n + h)).astype(BF16)


def _augment(xp, h, bias, ones_slot):
    lane = lax.broadcasted_iota(jnp.int32, (1, LANES), 1)
    hh = h % 2
    head = (lane >= HEAD_DIM * hh) & (lane < HEAD_DIM * (hh + 1))
    b = HEAD_DIM * (1 - hh)
    rest = jnp.zeros_like(xp) if bias is None else bias[:, h * LANES : (h + 1) * LANES]
    out = jnp.where(head, xp, rest)
    if ones_slot is not None:
        out = jnp.where((lane >= b + 3 * ones_slot) & (lane < b + 3 * ones_slot + 3), jnp.ones_like(xp), out)
    return out


def _attn_fwd(qkv, fcol, n_seq, S):
    T = n_seq * S
    tb = ATTN_BLOCK
    nq = S // tb
    scale = HEAD_DIM ** -0.5

    def body(q_ref, k_ref, v_ref, fc_ref, o_ref, st_ref, qa_sc, ka_sc, m_sc, l_sc, acc_sc):
        i = pl.program_id(1)
        lane = lax.broadcasted_iota(jnp.int32, (1, LANES), 1)
        low = lane < HEAD_DIM

        @pl.when(i == 0)
        def _():
            place = _bias_placement(1)

            def rows_ka(r, carry):
                r0 = pl.multiple_of(r * tb, tb)
                bias = _mm(_bias_lanes(-fc_ref[pl.ds(r0, tb), :]), place).astype(BF16)
                for h in range(N_HEADS):
                    kp = k_ref[pl.ds(r0, tb), (h // 2) * LANES : (h // 2 + 1) * LANES] * scale
                    ka_sc[h, pl.ds(r0, tb), :] = _augment(kp, h, bias, 0)
                return carry

            lax.fori_loop(0, nq, rows_ka, 0)

        q0 = pl.multiple_of(i * tb, tb)
        bias = _mm(_bias_lanes(fc_ref[pl.ds(q0, tb), :]), _bias_placement(0)).astype(BF16)
        for h in range(N_HEADS):
            qa_sc[h] = _augment(q_ref[:, (h // 2) * LANES : (h // 2 + 1) * LANES], h, bias, 1)
        m_sc[...] = jnp.full(m_sc.shape, -jnp.inf, F32)
        l_sc[...] = jnp.zeros_like(l_sc)
        acc_sc[...] = jnp.zeros_like(acc_sc)
        causal = lax.broadcasted_iota(jnp.int32, (tb, tb), 1) <= lax.broadcasted_iota(jnp.int32, (tb, tb), 0)

        def step(j, masked):
            c0 = pl.multiple_of(j * tb, tb)
            for p in range(N_PAIRS):
                vb = v_ref[pl.ds(c0, tb), p * LANES : (p + 1) * LANES]
                pv, al = [], []
                for hh in range(2):
                    h = 2 * p + hh
                    s = _mm_nt(qa_sc[h], ka_sc[h, pl.ds(c0, tb), :])
                    if masked:
                        s = jnp.where(causal, s, -jnp.inf)
                    m_old = m_sc[h]
                    m_new = jnp.maximum(m_old, jnp.max(s, axis=1, keepdims=True))
                    alpha = jnp.exp(m_old - m_new)
                    pe = jnp.exp(s - jnp.concatenate([m_new] * (tb // LANES), axis=1))
                    l_sc[h] = alpha * l_sc[h] + jnp.sum(pe, axis=1, keepdims=True)
                    m_sc[h] = m_new
                    pv.append(_mm(pe.astype(BF16), vb))
                    al.append(alpha)
                acc_sc[p] = jnp.where(low, al[0], al[1]) * acc_sc[p] + jnp.where(low, pv[0], pv[1])

        def loop_body(j, carry):
            step(j, False)
            return carry

        lax.fori_loop(0, i, loop_body, 0)
        step(i, True)
        st = jnp.zeros((tb, LANES), F32)
        for p in range(N_PAIRS):
            lp = jnp.where(low, l_sc[2 * p], l_sc[2 * p + 1])
            o_ref[:, p * LANES : (p + 1) * LANES] = (acc_sc[p] / lp).astype(BF16)
            for h in (2 * p, 2 * p + 1):
                st = jnp.where(lane == h, m_sc[h] + jnp.log(l_sc[h]), st)
        st_ref[...] = st

    return pl.pallas_call(
        body,
        name="attn_fwd",
        grid=(n_seq, nq),
        in_specs=[
            pl.BlockSpec((tb, ATTN_WIDTH), lambda s, i: (s * nq + i, 0)),
            pl.BlockSpec((S, ATTN_WIDTH), lambda s, i: (s, 1)),
            pl.BlockSpec((S, ATTN_WIDTH), lambda s, i: (s, 2)),
            pl.BlockSpec((S, LANES), lambda s, i: (s, 0)),
        ],
        out_specs=[
            pl.BlockSpec((tb, ATTN_WIDTH), lambda s, i: (s * nq + i, 0)),
            pl.BlockSpec((tb, LANES), lambda s, i: (s * nq + i, 0)),
        ],
        out_shape=[jax.ShapeDtypeStruct((T, ATTN_WIDTH), BF16), jax.ShapeDtypeStruct((T, LANES), F32)],
        scratch_shapes=[
            pltpu.VMEM((N_HEADS, tb, LANES), BF16),
            pltpu.VMEM((N_HEADS, S, LANES), BF16),
            pltpu.VMEM((N_HEADS, tb, LANES), F32),
            pltpu.VMEM((N_HEADS, tb, LANES), F32),
            pltpu.VMEM((N_PAIRS, tb, LANES), F32),
        ],
        compiler_params=_params(("parallel", "arbitrary")),
    )(qkv, qkv, qkv, fcol)


def _mix_out(a, p3, gates, x, w_ao, w_po, w_out):
    T = x.shape[0]
    tm = ROW_TILE

    def body(a_ref, p3_ref, gt_ref, x_ref, wao_ref, wpo_ref, wout_ref, mg_ref, x1_ref, ay_ref, py_ref):
        ay = _mm(a_ref[...], _whole_cols(wao_ref))
        py = _mm(p3_ref[...], _whole_cols(wpo_ref))
        ay_ref[...] = ay.astype(BF16)
        py_ref[...] = py.astype(BF16)
        sp = _sigmoid(gt_ref[:, :D_MODEL].astype(F32))
        sa = _sigmoid(gt_ref[:, D_MODEL:].astype(F32))
        mb = (sp * py + sa * ay).astype(BF16)
        mg_ref[...] = mb
        x1_ref[...] = x_ref[...] + _mm(mb, wout_ref[...])

    row = lambda n: pl.BlockSpec((tm, n), lambda i: (i, 0))
    return pl.pallas_call(
        body,
        name="mix_out",
        grid=(T // tm,),
        in_specs=[
            row(ATTN_WIDTH), row(POOL_WIDTH), row(2 * D_MODEL), row(D_MODEL),
            _const_spec(w_ao.shape), _const_spec(w_po.shape), _const_spec(w_out.shape),
        ],
        out_specs=[row(D_MODEL), row(D_MODEL), row(D_MODEL), row(D_MODEL)],
        out_shape=[
            jax.ShapeDtypeStruct((T, D_MODEL), BF16), jax.ShapeDtypeStruct((T, D_MODEL), F32),
            jax.ShapeDtypeStruct((T, D_MODEL), BF16), jax.ShapeDtypeStruct((T, D_MODEL), BF16),
        ],
        compiler_params=_params(("parallel",)),
    )(a, p3, gates, x, w_ao, w_po, w_out)


def _ffn_fwd(x1, g2, gf, tgt, w_gate_t, w_up_t, w_down):
    T = x1.shape[0]
    tm = min(T, FF_ROW_TILE)
    nt = T // tm
    nc = D_FF // FF_CHUNK

    def body(x1_ref, g2_ref, gf_ref, tg_ref, wg_ref, wu_ref, wd_ref, h2_ref, gate_ref, up_ref, act_ref, dx2_ref, loss_ref, dgf_ref):
        x1v = x1_ref[...]
        h2, _, _ = _rms_fwd(x1v, g2_ref[...])
        h2b = h2.astype(BF16)
        h2_ref[...] = h2b
        for c in range(nc):
            sl = slice(c * FF_CHUNK, (c + 1) * FF_CHUNK)
            gate = _mm_nt(h2b, wg_ref[sl, :])
            up = _mm_nt(h2b, wu_ref[sl, :])
            gate_ref[:, sl] = gate.astype(BF16)
            up_ref[:, sl] = up.astype(BF16)
            act_ref[:, sl] = (gate * _sigmoid(gate) * up).astype(BF16)
        acc = x1v + _mm(act_ref[...], wd_ref[...])
        gfv = gf_ref[...]
        y, xh, r = _rms_fwd(acc, gfv)
        err = y - tg_ref[...]
        part = 0.5 * jnp.sum(jnp.mean(err * err, axis=-1, keepdims=True), axis=0, keepdims=True)
        dx2, dgrow = _rms_bwd(err * (1.0 / D_MODEL), xh, r, gfv)
        dx2_ref[...] = dx2

        @pl.when(pl.program_id(0) == 0)
        def _():
            dgf_ref[...] = jnp.zeros_like(dgf_ref)
            loss_ref[...] = jnp.zeros_like(loss_ref)

        dgf_ref[...] += jnp.sum(dgrow, axis=0, keepdims=True)
        loss_ref[...] += jnp.broadcast_to(part, loss_ref.shape)

    row = lambda n: pl.BlockSpec((tm, n), lambda i: (i, 0))
    return pl.pallas_call(
        body,
        name="ffn_fwd",
        grid=(nt,),
        in_specs=[
            row(D_MODEL), _const_spec((1, D_MODEL)), _const_spec((1, D_MODEL)), row(D_MODEL),
            _const_spec(w_gate_t.shape), _const_spec(w_up_t.shape), _const_spec(w_down.shape),
        ],
        out_specs=[
            row(D_MODEL), row(D_FF), row(D_FF), row(D_FF), row(D_MODEL),
            pl.BlockSpec((8, LANES), lambda i: (0, 0)),
            pl.BlockSpec((1, D_MODEL), lambda i: (0, 0)),
        ],
        out_shape=[
            jax.ShapeDtypeStruct((T, D_MODEL), BF16),
            jax.ShapeDtypeStruct((T, D_FF), BF16),
            jax.ShapeDtypeStruct((T, D_FF), BF16),
            jax.ShapeDtypeStruct((T, D_FF), BF16),
            jax.ShapeDtypeStruct((T, D_MODEL), F32),
            jax.ShapeDtypeStruct((8, LANES), F32),
            jax.ShapeDtypeStruct((1, D_MODEL), F32),
        ],
        compiler_params=_params(("arbitrary",)),
    )(x1, g2, gf, tgt, w_gate_t, w_up_t, w_down)


def _ffn_bwd(dx2, gate, up, x1, g2, w_gate_t, w_up_t, w_down):
    T = x1.shape[0]
    tm = min(T, FF_ROW_TILE)
    nc = D_FF // FF_CHUNK

    def body(dx2_ref, gate_ref, up_ref, x1_ref, g2_ref, wg_ref, wu_ref, wd_ref, dgate_ref, dup_ref, dx1_ref, dg2_ref):
        dx2v = dx2_ref[...]
        dx2b = dx2v.astype(BF16)
        for c in range(nc):
            sl = slice(c * FF_CHUNK, (c + 1) * FF_CHUNK)
            dact = _mm_nt(dx2b, wd_ref[sl, :])
            gate = gate_ref[:, sl].astype(F32)
            sg = _sigmoid(gate)
            silu = gate * sg
            dgate = (dact * up_ref[:, sl].astype(F32) * (sg * (1.0 + gate * (1.0 - sg)))).astype(BF16)
            dup = (dact * silu).astype(BF16)
            dgate_ref[:, sl] = dgate
            dup_ref[:, sl] = dup
        dh2 = _mm(dgate_ref[...], wg_ref[...]) + _mm(dup_ref[...], wu_ref[...])
        g2v = g2_ref[...]
        _, xh, r = _rms_fwd(x1_ref[...], g2v)
        dxn, dgrow = _rms_bwd(dh2, xh, r, g2v)
        dx1_ref[...] = dx2v + dxn

        @pl.when(pl.program_id(0) == 0)
        def _():
            dg2_ref[...] = jnp.zeros_like(dg2_ref)

        dg2_ref[...] += jnp.sum(dgrow, axis=0, keepdims=True)

    row = lambda n: pl.BlockSpec((tm, n), lambda i: (i, 0))
    return pl.pallas_call(
        body,
        name="ffn_bwd",
        grid=(T // tm,),
        in_specs=[
            row(D_MODEL), row(D_FF), row(D_FF), row(D_MODEL), _const_spec((1, D_MODEL)),
            _const_spec(w_gate_t.shape), _const_spec(w_up_t.shape), _const_spec(w_down.shape),
        ],
        out_specs=[row(D_FF), row(D_FF), row(D_MODEL), pl.BlockSpec((1, D_MODEL), lambda i: (0, 0))],
        out_shape=[
            jax.ShapeDtypeStruct((T, D_FF), BF16),
            jax.ShapeDtypeStruct((T, D_FF), BF16),
            jax.ShapeDtypeStruct((T, D_MODEL), F32),
            jax.ShapeDtypeStruct((1, D_MODEL), F32),
        ],
        compiler_params=_params(("arbitrary",), VMEM_LIMIT_MAX),
    )(dx2, gate, up, x1, g2, w_gate_t, w_up_t, w_down)


def _mix_bwd(dx1, gates, pool_y, attn_y, p2, scale, w_out, w_ao, w_po, token):
    T = dx1.shape[0]
    tm = ROW_TILE

    def body(dx1_ref, gt_ref, py_ref, ay_ref, p2_ref, sc_ref, wout_ref, wao_ref, wpo_ref, token_ref, dgt_ref, dpy_ref, day_ref, da_ref, dp2_ref, dsc_ref):
        dm = _mm_nt(dx1_ref[...].astype(BF16), wout_ref[...])
        sp = _sigmoid(gt_ref[:, :D_MODEL].astype(F32))
        sa = _sigmoid(gt_ref[:, D_MODEL:].astype(F32))
        dgt_ref[:, :D_MODEL] = (dm * py_ref[...].astype(F32) * (sp * (1.0 - sp))).astype(BF16)
        dgt_ref[:, D_MODEL:] = (dm * ay_ref[...].astype(F32) * (sa * (1.0 - sa))).astype(BF16)
        dpy = (dm * sp).astype(BF16)
        day = (dm * sa).astype(BF16)
        dpy_ref[...] = dpy
        day_ref[...] = day
        da_ref[...] = _mm_nt(day, _whole_cols(wao_ref)).astype(BF16)
        dp3 = _mm_nt(dpy, _whole_cols(wpo_ref))
        dp2_ref[...] = (dp3 * sc_ref[...]).astype(BF16)

        @pl.when(pl.program_id(0) == 0)
        def _():
            dsc_ref[...] = jnp.zeros_like(dsc_ref)

        dsc_ref[...] += jnp.sum(dp3 * p2_ref[...], axis=0, keepdims=True)

    row = lambda n: pl.BlockSpec((tm, n), lambda i: (i, 0))
    return pl.pallas_call(
        body,
        name="mix_bwd",
        grid=(T // tm,),
        in_specs=[
            row(D_MODEL), row(2 * D_MODEL), row(D_MODEL), row(D_MODEL), row(POOL_WIDTH), _const_spec((1, POOL_WIDTH)),
            _const_spec(w_out.shape), _const_spec(w_ao.shape), _const_spec(w_po.shape), _HBM,
        ],
        out_specs=[row(2 * D_MODEL), row(D_MODEL), row(D_MODEL), row(ATTN_WIDTH), row(POOL_WIDTH), pl.BlockSpec((1, POOL_WIDTH), lambda i: (0, 0))],
        out_shape=[
            jax.ShapeDtypeStruct((T, 2 * D_MODEL), BF16),
            jax.ShapeDtypeStruct((T, D_MODEL), BF16),
            jax.ShapeDtypeStruct((T, D_MODEL), BF16),
            jax.ShapeDtypeStruct((T, ATTN_WIDTH), BF16),
            jax.ShapeDtypeStruct((T, POOL_WIDTH), BF16),
            jax.ShapeDtypeStruct((1, POOL_WIDTH), F32),
        ],
        compiler_params=_params(("arbitrary",)),
    )(dx1, gates, pool_y, attn_y, p2, scale, w_out, w_ao, w_po, token)


def _pool_bwd(dp2, pm, mix_b, token, n_seq, S):
    T = n_seq * S

    def body(dp2_ref, pm_ref, mix_ref, token_ref, du_ref, dmix_ref):
        g = pl.program_id(0)
        dp2v = dp2_ref[...]
        dpm = _mm_nt(dp2v, mix_ref[...])
        row = lax.broadcasted_iota(jnp.int32, dpm.shape, 0)
        w = _window_pick(g, 2.0, 4.0, 8.0, 16.0)
        e = dpm / jnp.minimum((row + 1).astype(F32), w)

        def ahead(a, k):
            return jnp.where(row < S - k, pltpu.roll(a, S - k, 0), 0.0)

        r2 = e + ahead(e, 1)
        r4 = r2 + ahead(r2, 2)
        r8 = r4 + ahead(r4, 4)
        r16 = r8 + ahead(r8, 8)
        du_ref[...] = (_window_pick(g, r2, r4, r8, r16) - dpm).astype(BF16)

        @pl.when(pl.program_id(1) == 0)
        def _():
            dmix_ref[...] = jnp.zeros_like(dmix_ref)

        dmix_ref[...] += _mm_tn(pm_ref[...], dp2v)

    grp = pl.BlockSpec((S, GROUP_DIM), lambda g, s: (s, g))
    mixs = pl.BlockSpec((None, GROUP_DIM, GROUP_DIM), lambda g, s: (g, 0, 0))
    return pl.pallas_call(
        body,
        name="pool_bwd",
        grid=(len(POOL_WINDOWS), n_seq),
        in_specs=[grp, grp, mixs, _HBM],
        out_specs=[grp, mixs],
        out_shape=[jax.ShapeDtypeStruct((T, POOL_WIDTH), BF16), jax.ShapeDtypeStruct((len(POOL_WINDOWS), GROUP_DIM, GROUP_DIM), F32)],
        compiler_params=pltpu.CompilerParams(
            dimension_semantics=("parallel", "arbitrary"), vmem_limit_bytes=VMEM_LIMIT, allow_input_fusion=[False, False, True, False]
        ),
    )(dp2, pm, mix_b, token)


def _attn_bwd(qkv, da, a, fcol, lse, n_seq, S):
    T = n_seq * S
    tb = ATTN_BLOCK
    nb = S // tb
    scale = HEAD_DIM ** -0.5

    def body(q_ref, k_ref, v_ref, do_ref, o_ref, fc_ref, st_ref, dq_ref, dk_ref, dv_ref, dfk_ref, dfq_ref,
             qa_sc, doa_sc, qat_sc, doat_sc, dq_acc, ka_sc, va_sc, dkt_sc, dvt_sc):
        j = pl.program_id(1)
        lane = lax.broadcasted_iota(jnp.int32, (1, LANES), 1)
        low = lane < HEAD_DIM

        @pl.when(j == 0)
        def _():
            dq_acc[...] = jnp.zeros_like(dq_acc)
            place = _bias_placement(0)

            def rows_q(i, carry):
                r0 = pl.multiple_of(i * tb, tb)
                delta = jnp.zeros((tb, LANES), F32)
                for h in range(N_HEADS):
                    pair = slice((h // 2) * LANES, (h // 2 + 1) * LANES)
                    prod = do_ref[pl.ds(r0, tb), pair].astype(F32) * o_ref[pl.ds(r0, tb), pair].astype(F32)
                    head = (lane >= HEAD_DIM * (h % 2)) & (lane < HEAD_DIM * (h % 2 + 1))
                    delta = jnp.where(lane == h, jnp.sum(jnp.where(head, prod, 0.0), axis=1, keepdims=True), delta)
                cq = fc_ref[pl.ds(r0, tb), :] - st_ref[pl.ds(r0, tb), :]
                q_bias = _mm(_bias_lanes(cq), place).astype(BF16)
                do_bias = _mm(_bias_lanes(-delta), place).astype(BF16)
                for h in range(N_HEADS):
                    pair = slice((h // 2) * LANES, (h // 2 + 1) * LANES)
                    qa = _augment(q_ref[pl.ds(r0, tb), pair], h, q_bias, 1)
                    doa = _augment(do_ref[pl.ds(r0, tb), pair], h, do_bias, None)
                    qa_sc[h, pl.ds(r0, tb), :] = qa
                    doa_sc[h, pl.ds(r0, tb), :] = doa
                    qat_sc[h, i] = qa.astype(F32).T.astype(BF16)
                    doat_sc[h, i] = doa.astype(F32).T.astype(BF16)
                return carry

            lax.fori_loop(0, nb, rows_q, 0)

        c0 = pl.multiple_of(j * tb, tb)
        k_bias = _mm(_bias_lanes(-fc_ref[pl.ds(c0, tb), :]), _bias_placement(1)).astype(BF16)
        for h in range(N_HEADS):
            pair = slice((h // 2) * LANES, (h // 2 + 1) * LANES)
            ka_sc[h] = _augment(k_ref[:, pair] * scale, h, k_bias, 0)
            va_sc[h] = _augment(v_ref[:, pair], h, None, 0)
        dkt_sc[...] = jnp.zeros_like(dkt_sc)
        dvt_sc[...] = jnp.zeros_like(dvt_sc)
        causal = lax.broadcasted_iota(jnp.int32, (tb, tb), 1) <= lax.broadcasted_iota(jnp.int32, (tb, tb), 0)

        def step(i, masked):
            r0 = pl.multiple_of(i * tb, tb)
            for h in range(N_HEADS):
                s = _mm_nt(qa_sc[h, pl.ds(r0, tb), :], ka_sc[h])
                if masked:
                    s = jnp.where(causal, s, -jnp.inf)
                pr = jnp.exp(s)
                dvt_sc[h] += _mm(doat_sc[h, i], pr.astype(BF16))
                dsb = (pr * _mm_nt(doa_sc[h, pl.ds(r0, tb), :], va_sc[h])).astype(BF16)
                dkt_sc[h] += _mm(qat_sc[h, i], dsb)
                dq_acc[h, pl.ds(r0, tb), :] += _mm(dsb, ka_sc[h])

        step(j, True)

        def loop_body(i, carry):
            step(i, False)
            return carry

        lax.fori_loop(j + 1, nb, loop_body, 0)
        dfk = jnp.zeros((tb, LANES), F32)
        for p in range(N_PAIRS):
            dk = [dkt_sc[2 * p + hh].T for hh in range(2)]
            dv = [dvt_sc[2 * p + hh].T for hh in range(2)]
            dk_ref[:, p * LANES : (p + 1) * LANES] = (jnp.where(low, dk[0], dk[1]) * scale).astype(BF16)
            dv_ref[:, p * LANES : (p + 1) * LANES] = jnp.where(low, dv[0], dv[1]).astype(BF16)
            for hh in range(2):
                b = HEAD_DIM * (1 - hh) + 3
                dfk = jnp.where(lane == 2 * p + hh, -dk[hh][:, b : b + 1], dfk)
        dfk_ref[...] = dfk

        @pl.when(j == nb - 1)
        def _():
            def rows_dq(i, carry):
                r0 = pl.multiple_of(i * tb, tb)
                dfq = jnp.zeros((tb, LANES), F32)
                for p in range(N_PAIRS):
                    parts = [dq_acc[2 * p + hh, pl.ds(r0, tb), :] for hh in range(2)]
                    dq_ref[pl.ds(r0, tb), p * LANES : (p + 1) * LANES] = jnp.where(low, parts[0], parts[1]).astype(BF16)
                    for hh in range(2):
                        b = HEAD_DIM * (1 - hh)
                        dfq = jnp.where(lane == 2 * p + hh, parts[hh][:, b : b + 1], dfq)
                dfq_ref[pl.ds(r0, tb), :] = dfq
                return carry

            lax.fori_loop(0, nb, rows_dq, 0)

    seq = lambda w, col: pl.BlockSpec((S, w), lambda s, j: (s, col))
    seq_in = lambda w, col: pl.BlockSpec((S, w), lambda s, j: (s, col), pipeline_mode=pl.Buffered(1))
    blk = lambda w, col: pl.BlockSpec((tb, w), lambda s, j: (s * nb + j, col))
    return pl.pallas_call(
        body,
        name="attn_bwd",
        grid=(n_seq, nb),
        in_specs=[seq(ATTN_WIDTH, 0), blk(ATTN_WIDTH, 1), blk(ATTN_WIDTH, 2), seq(ATTN_WIDTH, 0), seq(ATTN_WIDTH, 0), seq_in(LANES, 0), seq_in(LANES, 0)],
        out_specs=[seq(ATTN_WIDTH, 0), blk(ATTN_WIDTH, 0), blk(ATTN_WIDTH, 0), blk(LANES, 0), seq(LANES, 0)],
        out_shape=[
            jax.ShapeDtypeStruct((T, ATTN_WIDTH), BF16),
            jax.ShapeDtypeStruct((T, ATTN_WIDTH), BF16),
            jax.ShapeDtypeStruct((T, ATTN_WIDTH), BF16),
            jax.ShapeDtypeStruct((T, LANES), F32),
            jax.ShapeDtypeStruct((T, LANES), F32),
        ],
        scratch_shapes=[
            pltpu.VMEM((N_HEADS, S, LANES), BF16),
            pltpu.VMEM((N_HEADS, S, LANES), BF16),
            pltpu.VMEM((N_HEADS, nb, LANES, tb), BF16),
            pltpu.VMEM((N_HEADS, nb, LANES, tb), BF16),
            pltpu.VMEM((N_HEADS, S, LANES), F32),
            pltpu.VMEM((N_HEADS, tb, LANES), BF16),
            pltpu.VMEM((N_HEADS, tb, LANES), BF16),
            pltpu.VMEM((N_HEADS, LANES, tb), F32),
            pltpu.VMEM((N_HEADS, LANES, tb), F32),
        ],
        compiler_params=_params(("parallel", "arbitrary"), VMEM_LIMIT_MAX),
    )(qkv, qkv, qkv, da, a, fcol, lse)


def _forget_bwd(dfk, dfq, fl, b_pad, n_seq, S):
    def body(df_ref, dfq_ref, fl_ref, b_ref, dfl_ref, db_ref):
        t = (df_ref[...] + dfq_ref[...]).T
        lane = lax.broadcasted_iota(jnp.int32, t.shape, 1)
        k = 1
        while k < S:
            t = t + jnp.where(lane < S - k, pltpu.roll(t, S - k, 1), 0.0)
            k *= 2
        dfl = t.T * _sigmoid(-(fl_ref[...] + b_ref[...]))
        dfl_ref[...] = dfl.astype(BF16)

        @pl.when(pl.program_id(0) == 0)
        def _():
            db_ref[...] = jnp.zeros_like(db_ref)

        db_ref[...] += jnp.sum(dfl, axis=0, keepdims=True)

    return pl.pallas_call(
        body,
        name="forget_bwd",
        grid=(n_seq,),
        in_specs=[
            pl.BlockSpec((S, LANES), lambda s: (s, 0)),
            pl.BlockSpec((S, LANES), lambda s: (s, 0)),
            pl.BlockSpec((S, FL_PAD), lambda s: (s, 0)),
            _const_spec((1, FL_PAD)),
        ],
        out_specs=[pl.BlockSpec((S, FL_PAD), lambda s: (s, 0)), pl.BlockSpec((1, FL_PAD), lambda s: (0, 0))],
        out_shape=[jax.ShapeDtypeStruct((n_seq * S, FL_PAD), BF16), jax.ShapeDtypeStruct((1, FL_PAD), F32)],
        compiler_params=_params(("arbitrary",)),
    )(dfk, dfq, fl, b_pad)


def _in_proj_bwd(du, dq, dk, dv, dfl, dgates, x, dx1, g1, w_uqkv, w_fl, w_g, token):
    T = x.shape[0]
    tm = ROW_TILE

    def body(du_ref, dq_ref, dk_ref, dv_ref, dfl_ref, dgt_ref, x_ref, dx1_ref, g_ref, wa_ref, wf_ref, wg_ref, token_ref, dx_ref, dg_ref):
        dz = jnp.concatenate([du_ref[...], dq_ref[...], dk_ref[...], dv_ref[...]], axis=1)
        dh = _mm_nt(dz, wa_ref[...]) + _mm_nt(dgt_ref[...], wg_ref[...]) + _mm_nt(dfl_ref[...], wf_ref[...])
        gv = g_ref[...]
        _, xh, r = _rms_fwd(x_ref[...], gv)
        dxn, dgrow = _rms_bwd(dh, xh, r, gv)
        dx_ref[...] = dx1_ref[...] + dxn

        @pl.when(pl.program_id(0) == 0)
        def _():
            dg_ref[...] = jnp.zeros_like(dg_ref)

        dg_ref[...] += jnp.sum(dgrow, axis=0, keepdims=True)

    row = lambda n: pl.BlockSpec((tm, n), lambda i: (i, 0))
    return pl.pallas_call(
        body,
        name="in_proj_bwd",
        grid=(T // tm,),
        in_specs=[
            row(512), row(512), row(512), row(512), row(FL_PAD), row(2 * D_MODEL), row(D_MODEL), row(D_MODEL), _const_spec((1, D_MODEL)),
            _const_spec(w_uqkv.shape), _const_spec(w_fl.shape), _const_spec(w_g.shape), _HBM,
        ],
        out_specs=[row(D_MODEL), pl.BlockSpec((1, D_MODEL), lambda i: (0, 0))],
        out_shape=[jax.ShapeDtypeStruct((T, D_MODEL), F32), jax.ShapeDtypeStruct((1, D_MODEL), F32)],
        compiler_params=_params(("arbitrary",)),
    )(du, dq, dk, dv, dfl, dgates, x, dx1, g1, w_uqkv, w_fl, w_g, token)


def _pick_block(n):
    for b in (1024, 512, 1408, 256, 128):
        if n % b == 0:
            return b
    raise ValueError(n)


def _matmul_tn(a, b, name):
    T, K = a.shape
    N = b.shape[1]
    bt, bk, bn = min(T, DW_TOKENS), _pick_block(K), _pick_block(N)
    nt = T // bt

    def body(a_ref, b_ref, o_ref, acc):
        @pl.when(pl.program_id(2) == 0)
        def _():
            acc[...] = jnp.zeros_like(acc)

        acc[...] += _mm_tn(a_ref[...].astype(BF16), b_ref[...].astype(BF16))

        @pl.when(pl.program_id(2) == nt - 1)
        def _():
            o_ref[...] = acc[...].astype(BF16)

    return pl.pallas_call(
        body,
        name=name,
        grid=(K // bk, N // bn, nt),
        in_specs=[pl.BlockSpec((bt, bk), lambda k, n, t: (t, k)), pl.BlockSpec((bt, bn), lambda k, n, t: (t, n))],
        out_specs=pl.BlockSpec((bk, bn), lambda k, n, t: (k, n)),
        out_shape=jax.ShapeDtypeStruct((K, N), BF16),
        scratch_shapes=[pltpu.VMEM((bk, bn), F32)],
        compiler_params=_params(("parallel", "parallel", "arbitrary")),
    )(a, b)


def _matmul_tn_pair(a1, b1, a2, b2, name):
    T, K = a1.shape
    N = b1.shape[1]
    bt = min(T, DW_TOKENS)
    nt = T // bt
    c = N // N_DEV

    def body(a1_ref, b1_ref, a2_ref, b2_ref, o1_ref, o2_ref, acc1, acc2):
        @pl.when(pl.program_id(0) == 0)
        def _():
            acc1[...] = jnp.zeros_like(acc1)
            acc2[...] = jnp.zeros_like(acc2)

        acc1[...] += _mm_tn(a1_ref[...].astype(BF16), b1_ref[...].astype(BF16))
        acc2[...] += _mm_tn(a2_ref[...].astype(BF16), b2_ref[...].astype(BF16))

        @pl.when(pl.program_id(0) == nt - 1)
        def _():
            for d in range(N_DEV):
                o1_ref[d] = acc1[:, d * c : (d + 1) * c].astype(BF16)
                o2_ref[d] = acc2[:, d * c : (d + 1) * c].astype(BF16)

    lhs, rhs = pl.BlockSpec((bt, K), lambda t: (t, 0)), pl.BlockSpec((bt, N), lambda t: (t, 0))
    whole = pl.BlockSpec((N_DEV, K, c), lambda t: (0, 0, 0))
    return pl.pallas_call(
        body,
        name=name,
        grid=(nt,),
        in_specs=[lhs, rhs, lhs, rhs],
        out_specs=[whole, whole],
        out_shape=[jax.ShapeDtypeStruct((N_DEV, K, c), BF16)] * 2,
        scratch_shapes=[pltpu.VMEM((K, N), F32), pltpu.VMEM((K, N), F32)],
        compiler_params=_params(("arbitrary",)),
    )(a1, b1, a2, b2)


W_IN_A = POOL_WIDTH + 3 * ATTN_WIDTH
W_IN_SHARD = (W_IN_A + N_HEADS + 2 * D_MODEL) // N_DEV
_W_IN_PIECES = ((0, W_IN_A), (W_IN_A, W_IN_A + N_HEADS), (W_IN_A + N_HEADS, W_IN_A + N_HEADS + 2 * D_MODEL))


def _w_in_segments(d):
    lo, hi = d * W_IN_SHARD, (d + 1) * W_IN_SHARD
    out = []
    for p, (a, b) in enumerate(_W_IN_PIECES):
        s, e = max(lo, a), min(hi, b)
        if s < e:
            out.append((p, s - a, s - lo, e - s))
    return out


def _w_in_pieces(gathered, tails):
    tm = ROW_TILE // 2
    tail_rows = tm // LANES
    aligned = W_IN_SHARD - 1

    def body(g_ref, t_ref, wa_ref, wf_ref, wg_ref):
        outs = (wa_ref, wf_ref, wg_ref)
        wf_ref[...] = jnp.zeros_like(wf_ref)
        diagonal = lax.broadcasted_iota(jnp.int32, (LANES, LANES), 0) == lax.broadcasted_iota(jnp.int32, (LANES, LANES), 1)
        for d in range(N_DEV):
            for p, at, frm, n in _w_in_segments(d):
                m = min(n, aligned - frm)
                if m > 0:
                    outs[p][:, at : at + m] = g_ref[d, :, frm : frm + m]
                if frm + n == W_IN_SHARD:
                    column = [
                        jnp.sum(jnp.where(diagonal, jnp.broadcast_to(t_ref[d, k : k + 1, :], (LANES, LANES)), 0.0), axis=1, keepdims=True)
                        for k in range(tail_rows)
                    ]
                    outs[p][:, at + n - 1 : at + n] = jnp.concatenate(column, axis=0).astype(outs[p].dtype)

    return pl.pallas_call(
        body,
        name="w_in_pieces",
        grid=(D_MODEL // tm,),
        in_specs=[
            pl.BlockSpec((N_DEV, tm, aligned), lambda i: (0, i, 0)),
            pl.BlockSpec((N_DEV, None, tail_rows, LANES), lambda i: (0, i, 0, 0)),
        ],
        out_specs=[pl.BlockSpec((tm, W_IN_A), lambda i: (i, 0)), pl.BlockSpec((tm, FL_PAD), lambda i: (i, 0)), pl.BlockSpec((tm, 2 * D_MODEL), lambda i: (i, 0))],
        out_shape=[
            jax.ShapeDtypeStruct((D_MODEL, W_IN_A), gathered.dtype),
            jax.ShapeDtypeStruct((D_MODEL, FL_PAD), gathered.dtype),
            jax.ShapeDtypeStruct((D_MODEL, 2 * D_MODEL), gathered.dtype),
        ],
        compiler_params=_params(("parallel",)),
    )(gathered, tails.reshape(N_DEV, D_MODEL // tm, tail_rows, LANES))


def _dw_in(h, du, dq, dk, dv, dfl, dgates, token):
    T = h.shape[0]
    bt, bk = min(T, DW_TOKENS // 2), 512
    nt = T // bt
    pieces = (du, dq, dk, dv, dfl, dgates)
    offs = [0]
    for p in pieces:
        offs.append(offs[-1] + p.shape[1])

    aligned = W_IN_SHARD - 1
    tail_rows = bk // LANES

    def body(h_ref, *rest):
        refs, o_ref, t_ref, acc = rest[: len(pieces)], rest[-3], rest[-2], rest[-1]

        @pl.when(pl.program_id(1) == 0)
        def _():
            acc[...] = jnp.zeros_like(acc)

        ht = h_ref[...].T
        for ref, at in zip(refs, offs):
            acc[:, at : at + ref.shape[1]] += _mm(ht, ref[...])

        @pl.when(pl.program_id(1) == nt - 1)
        def _():
            starts = (0, W_IN_A, W_IN_A + FL_PAD)
            diagonal = lax.broadcasted_iota(jnp.int32, (LANES, LANES), 0) == lax.broadcasted_iota(jnp.int32, (LANES, LANES), 1)
            for d in range(N_DEV):
                for p, at, to, n in _w_in_segments(d):
                    m = min(n, aligned - to)
                    if m > 0:
                        o_ref[d, :, to : to + m] = acc[:, starts[p] + at : starts[p] + at + m].astype(BF16)
                    if to + n == W_IN_SHARD:
                        last = starts[p] + at + n - 1
                        column = acc[:, last : last + 1].astype(BF16).astype(F32)
                        for k in range(tail_rows):
                            rows = jnp.broadcast_to(column[k * LANES : (k + 1) * LANES], (LANES, LANES))
                            t_ref[d, k : k + 1, :] = jnp.sum(jnp.where(diagonal, rows, 0.0), axis=0, keepdims=True)

    main, tails = pl.pallas_call(
        body,
        name="dw_in",
        grid=(D_MODEL // bk, nt),
        in_specs=[pl.BlockSpec((bt, bk), lambda k, t: (t, k))] + [pl.BlockSpec((bt, p.shape[1]), lambda k, t: (t, 0)) for p in pieces] + [_HBM],
        out_specs=[
            pl.BlockSpec((N_DEV, bk, aligned), lambda k, t: (0, k, 0)),
            pl.BlockSpec((N_DEV, None, tail_rows, LANES), lambda k, t: (0, k, 0, 0)),
        ],
        out_shape=[
            jax.ShapeDtypeStruct((N_DEV, D_MODEL, aligned), BF16),
            jax.ShapeDtypeStruct((N_DEV, D_MODEL // bk, tail_rows, LANES), F32),
        ],
        scratch_shapes=[pltpu.VMEM((bk, offs[-1]), F32)],
        compiler_params=_params(("parallel", "arbitrary")),
    )(h, *pieces, token)
    return main, tails.reshape(N_DEV, D_MODEL // LANES, LANES)


def _position():
    return lax.axis_index("x"), lax.axis_index("y"), lax.axis_index("c")


_HBM = pl.BlockSpec(memory_space=pl.ANY)


def _all_gather(blocks, name):
    n = len(blocks)
    parts = [(a, q * (b.shape[0] // 4), b.shape[0] // 4) for a, b in enumerate(blocks) if b.shape[0] >= ROW_TILE for q in range(4)]
    parts += [(a, 0, b.shape[0]) for a, b in enumerate(blocks) if b.shape[0] < ROW_TILE]

    def body(*refs):
        xs, outs = refs[:n], refs[n : 2 * n]
        send_sems, recv_sems, local_sems = refs[2 * n :]
        x, y, c = _position()
        me, sibling = (x, y, c), (x, y, 1 - c)
        chips = [(1 - x, y), (x, 1 - y), (1 - x, 1 - y)]

        def rows(u, px, py, pc):
            a, lo, size = parts[u]
            return outs[a].at[4 * px + 2 * py + pc, pl.ds(lo, size)]

        def own(u):
            a, lo, size = parts[u]
            return xs[a].at[pl.ds(lo, size)]

        def copy(u, k, blk, to, src=None):
            return pltpu.make_async_remote_copy(
                src_ref=rows(u, *blk) if src is None else src, dst_ref=rows(u, *blk),
                send_sem=send_sems.at[7 * u + k], recv_sem=recv_sems.at[7 * u + k], device_id=to, device_id_type=MESH,
            )

        first = []
        for u in range(len(parts)):
            first += [copy(u, 1 + j, me, (*chip, c), src=own(u)) for j, chip in enumerate(chips)]
            first.append(copy(u, 0, me, sibling, src=own(u)))
        mine = [pltpu.make_async_copy(xs[a], outs[a].at[4 * x + 2 * y + c], local_sems.at[a]) for a in range(n)]
        for cp in first + mine:
            cp.start()
        passed = []
        for u in range(len(parts)):
            for j, chip in enumerate(chips):
                copy(u, 1 + j, (*chip, c), me).wait_recv()
                passed.append(copy(u, 4 + j, (*chip, c), sibling))
                passed[-1].start()
        for u in range(len(parts)):
            copy(u, 0, sibling, me).wait_recv()
            for j, chip in enumerate(chips):
                copy(u, 4 + j, (*chip, 1 - c), me).wait_recv()
        for cp in first + passed:
            cp.wait_send()
        for cp in mine:
            cp.wait()

    return pl.pallas_call(
        body,
        name=name,
        out_shape=[jax.ShapeDtypeStruct((N_DEV, *b.shape), b.dtype) for b in blocks],
        in_specs=[_HBM] * n,
        out_specs=[_HBM] * n,
        scratch_shapes=[pltpu.SemaphoreType.DMA((7 * len(parts),)), pltpu.SemaphoreType.DMA((7 * len(parts),)), pltpu.SemaphoreType.DMA((n,))],
    )(*blocks)


_SEM = pl.BlockSpec(memory_space=pltpu.SEMAPHORE)
_HBM_ONLY = pl.BlockSpec(memory_space=pltpu.HBM)
_SIDE_EFFECT = pltpu.SideEffectType.DATAFLOW_SIDE_EFFECTING


def _peer(x, y, c, k):
    return (1 - x if k & 4 else x, 1 - y if k & 2 else y, 1 - c if k & 1 else c)


_PEER_BITS = {"gather": range(1, N_DEV), "gather_half": (1, 4, 2, 6), "forward": (4, 2, 6), "scatter": range(1, N_DEV)}
_GATHERS = ("gather", "gather_half")


def _exchange_copies(src_refs, land_refs, send_sems, recv_sems, pattern, receive_side):
    x, y, c = _position()
    me = 4 * x + 2 * y + c
    bits = _PEER_BITS[pattern]
    cps = []
    for j, k in enumerate(bits):
        px, py, pc = _peer(x, y, c, k)
        peer = 4 * px + 2 * py + pc
        for a, (src, land) in enumerate(zip(src_refs, land_refs)):
            to = (px, py, pc)
            if pattern == "forward":
                slot = 4 * px + 2 * py + (1 - c if receive_side else c)
                s, to = land.at[slot], (x, y, 1 - c)
            else:
                s, slot = (src if pattern in _GATHERS else src.at[peer]), (peer if receive_side else me)
            cps.append(pltpu.make_async_remote_copy(
                src_ref=s, dst_ref=land.at[slot],
                send_sem=send_sems.at[len(bits) * a + j], recv_sem=recv_sems.at[len(bits) * a + j],
                device_id=to, device_id_type=MESH,
            ))
    return cps


def _own_copies(src_refs, land_refs, own_sems):
    x, y, c = _position()
    return [
        pltpu.make_async_copy(src, land.at[4 * x + 2 * y + c], own_sems.at[a])
        for a, (src, land) in enumerate(zip(src_refs, land_refs))
    ]


def _exchange_start(srcs, after, name, pattern):
    n = len(srcs)
    m = len(_PEER_BITS[pattern])
    lands = [jax.ShapeDtypeStruct((N_DEV, *s.shape[-2:]), s.dtype) for s in srcs]

    def body(*refs):
        src_refs, land_refs = refs[1 : 1 + n], refs[1 + n : 1 + 2 * n]
        send_sems, recv_sems, own_sems = refs[1 + 2 * n : 4 + 2 * n]
        token = refs[-1]
        if pattern in _GATHERS:
            for cp in _own_copies(src_refs, land_refs, own_sems):
                cp.start()
        for cp in _exchange_copies(src_refs, land_refs, send_sems, recv_sems, pattern, receive_side=False):
            cp.start()
        token[...] = jnp.zeros_like(token)

    hbm = lambda t: pltpu.with_memory_space_constraint(t, pltpu.HBM)
    out = pl.pallas_call(
        body,
        name=name,
        out_shape=(
            pltpu.SemaphoreType.DMA((m * n,)), pltpu.SemaphoreType.DMA((m * n,)), pltpu.SemaphoreType.DMA((n,)),
            *[pltpu.HBM(s.shape, s.dtype) for s in srcs], *[pltpu.HBM(l.shape, l.dtype) for l in lands],
            jax.ShapeDtypeStruct((8, LANES), F32),
        ),
        in_specs=(_HBM, *[_HBM_ONLY] * (2 * n)),
        out_specs=(_SEM, _SEM, _SEM, *[_HBM_ONLY] * (2 * n), pl.BlockSpec(memory_space=pltpu.VMEM)),
        input_output_aliases={1 + i: 3 + i for i in range(2 * n)},
        compiler_params=pltpu.CompilerParams(has_side_effects=_SIDE_EFFECT),
    )(after, *[hbm(s) for s in srcs], *[hbm(lax.empty(l.shape, l.dtype)) for l in lands])
    return out[:3], out[3 : 3 + n], out[3 + n : 3 + 2 * n], out[-1]


def _exchange_wait(sems, srcs, lands, after, name, pattern):
    n = len(srcs)

    def body(*refs):
        src_refs, land_refs = refs[:n], refs[n : 2 * n]
        send_sems, recv_sems, own_sems = refs[2 * n : 2 * n + 3]
        if pattern in _GATHERS:
            for cp in _own_copies(src_refs, land_refs, own_sems):
                cp.wait()
        for cp in _exchange_copies(src_refs, land_refs, send_sems, recv_sems, pattern, receive_side=True):
            cp.wait_send()
            cp.wait_recv()

    out = pl.pallas_call(
        body,
        name=name,
        out_shape=(*[pltpu.HBM(s.shape, s.dtype) for s in srcs], *[pltpu.HBM(l.shape, l.dtype) for l in lands]),
        in_specs=(*[_HBM_ONLY] * (2 * n), _SEM, _SEM, _SEM, _HBM),
        out_specs=tuple([_HBM_ONLY] * (2 * n)),
        input_output_aliases={i: i for i in range(2 * n)},
        compiler_params=pltpu.CompilerParams(has_side_effects=_SIDE_EFFECT),
    )(*srcs, *lands, *sems, after)
    return out[:n], out[n:]


def _gather_forward(sems, srcs, lands, after, name):
    n = len(srcs)
    m = len(_PEER_BITS["forward"])

    def body(*refs):
        src_refs, land_refs = refs[:n], refs[n : 2 * n]
        send_sems, recv_sems, own_sems = refs[2 * n : 2 * n + 3]
        forward_send, forward_recv, token = refs[2 * n + 4], refs[2 * n + 5], refs[-1]
        for cp in _own_copies(src_refs, land_refs, own_sems):
            cp.wait()
        for cp in _exchange_copies(src_refs, land_refs, send_sems, recv_sems, "gather_half", receive_side=True):
            cp.wait_send()
            cp.wait_recv()
        for cp in _exchange_copies(land_refs, land_refs, forward_send, forward_recv, "forward", receive_side=False):
            cp.start()
        token[...] = jnp.zeros_like(token)

    out = pl.pallas_call(
        body,
        name=name,
        out_shape=(
            pltpu.SemaphoreType.DMA((m * n,)), pltpu.SemaphoreType.DMA((m * n,)),
            *[pltpu.HBM(l.shape, l.dtype) for l in lands], jax.ShapeDtypeStruct((8, LANES), F32),
        ),
        in_specs=(*[_HBM_ONLY] * (2 * n), _SEM, _SEM, _SEM, _HBM),
        out_specs=(_SEM, _SEM, *[_HBM_ONLY] * n, pl.BlockSpec(memory_space=pltpu.VMEM)),
        input_output_aliases={n + i: 2 + i for i in range(n)},
        compiler_params=pltpu.CompilerParams(has_side_effects=_SIDE_EFFECT),
    )(*srcs, *lands, *sems, after)
    return out[:2], out[2 : 2 + n], out[-1]


def _forward_wait(sems, lands, after, name):
    n = len(lands)

    def body(*refs):
        land_refs = refs[:n]
        for cp in _exchange_copies(land_refs, land_refs, refs[n], refs[n + 1], "forward", receive_side=True):
            cp.wait_send()
            cp.wait_recv()

    return pl.pallas_call(
        body,
        name=name,
        out_shape=tuple(pltpu.HBM(l.shape, l.dtype) for l in lands),
        in_specs=(*[_HBM_ONLY] * n, _SEM, _SEM, _HBM),
        out_specs=tuple([_HBM_ONLY] * n),
        input_output_aliases={i: i for i in range(n)},
        compiler_params=pltpu.CompilerParams(has_side_effects=_SIDE_EFFECT),
    )(*lands, *sems, after)


def _rows_tile(r):
    return ROW_TILE if r % ROW_TILE == 0 else r


def _adamw(w, g, m, v):
    m = ADAM_B1 * m + (1.0 - ADAM_B1) * g
    v = ADAM_B2 * v + (1.0 - ADAM_B2) * (g * g)
    m_hat = m / (1.0 - ADAM_B1 ** ADAM_STEP)
    v_hat = v / (1.0 - ADAM_B2 ** ADAM_STEP)
    delta = -ADAM_LR * (m_hat / (jnp.sqrt(v_hat) + ADAM_EPS) + ADAM_WD * w)
    return delta, m, v


def _shard_update_direct(parts, chunks, w, m, v, me, name):
    _, r, c = w.shape
    br = _rows_tile(r)

    def body(me_ref, p_ref, own_ref, w_ref, m_ref, v_ref, g_ref, d_ref, nm_ref, nv_ref):
        g = None
        for n in range(N_DEV):
            part = jnp.where(me_ref[0] == n, own_ref[...], p_ref[n]).astype(F32)
            g = part if g is None else g + part
        g_ref[...] = g
        d_ref[...], nm_ref[...], nv_ref[...] = _adamw(w_ref[...], g, m_ref[...], v_ref[...])

    shard = pl.BlockSpec((None, br, c), lambda i, me: (0, i, 0))
    return pl.pallas_call(
        body,
        name=name,
        grid_spec=pltpu.PrefetchScalarGridSpec(
            num_scalar_prefetch=1,
            grid=(r // br,),
            in_specs=[
                pl.BlockSpec((N_DEV, br, c), lambda i, me: (0, i, 0)),
                pl.BlockSpec((None, br, c), lambda i, me: (me[0], i, 0)),
                shard, shard, shard,
            ],
            out_specs=[shard, shard, shard, shard],
        ),
        out_shape=[jax.ShapeDtypeStruct((1, r, c), F32)] * 4,
        compiler_params=_params(("parallel",)),
    )(me, parts, chunks, w, m, v)


def _w_in_update(parts, chunks, tail_parts, tail_chunks, w, m, v, me, name):
    _, r, c = w.shape
    br = _rows_tile(r)
    tail_rows = br // LANES

    def body(me_ref, p_ref, own_ref, tp_ref, town_ref, w_ref, m_ref, v_ref, g_ref, d_ref, nm_ref, nv_ref):
        g = tail = None
        for n in range(N_DEV):
            mine = me_ref[0] == n
            part = jnp.where(mine, own_ref[...], p_ref[n]).astype(F32)
            last = jnp.where(mine, town_ref[...], tp_ref[n])
            g = part if g is None else g + part
            tail = last if tail is None else tail + last
        diagonal = lax.broadcasted_iota(jnp.int32, (LANES, LANES), 0) == lax.broadcasted_iota(jnp.int32, (LANES, LANES), 1)
        column = jnp.concatenate(
            [
                jnp.sum(jnp.where(diagonal, jnp.broadcast_to(tail[k : k + 1, :], (LANES, LANES)), 0.0), axis=1, keepdims=True)
                for k in range(tail_rows)
            ],
            axis=0,
        )
        for lo, hi, grad in ((0, c - 1, g), (c - 1, c, column)):
            g_ref[:, lo:hi] = grad
            d_ref[:, lo:hi], nm_ref[:, lo:hi], nv_ref[:, lo:hi] = _adamw(w_ref[:, lo:hi], grad, m_ref[:, lo:hi], v_ref[:, lo:hi])

    shard = pl.BlockSpec((None, br, c), lambda i, me: (0, i, 0))
    by_block = lambda t: t.reshape(N_DEV, r // br, tail_rows, LANES)
    return pl.pallas_call(
        body,
        name=name,
        grid_spec=pltpu.PrefetchScalarGridSpec(
            num_scalar_prefetch=1,
            grid=(r // br,),
            in_specs=[
                pl.BlockSpec((N_DEV, br, c - 1), lambda i, me: (0, i, 0)),
                pl.BlockSpec((None, br, c - 1), lambda i, me: (me[0], i, 0)),
                pl.BlockSpec((N_DEV, None, tail_rows, LANES), lambda i, me: (0, i, 0, 0)),
                pl.BlockSpec((None, None, tail_rows, LANES), lambda i, me: (me[0], i, 0, 0)),
                shard, shard, shard,
            ],
            out_specs=[shard, shard, shard, shard],
        ),
        out_shape=[jax.ShapeDtypeStruct((1, r, c), F32)] * 4,
        compiler_params=_params(("parallel",)),
    )(me, parts, chunks, by_block(tail_parts), by_block(tail_chunks), w, m, v)


def _small_update(parts, first_rows, ws, ms, vs):
    k = len(ws)

    def unpacked(rows, shape):
        if len(shape) == 2 and shape[1] <= LANES:
            return rows[0:1, : shape[1]]
        if len(shape) == 2:
            return jnp.concatenate([rows[r : r + 1] for r in range(shape[1] // LANES)], axis=1)
        return rows.reshape(shape)

    def body(p_ref, f_ref, *refs):
        w_refs, m_refs, v_refs = refs[:k], refs[k : 2 * k], refs[2 * k : 3 * k]
        outs, loss_ref = refs[3 * k : 7 * k], refs[7 * k]
        g, first = p_ref[0], f_ref[0]
        for n in range(1, N_DEV):
            g = g + p_ref[n]
            first = first + f_ref[n]
        g = jnp.concatenate([g[:8] + first, g[8:]], axis=0)
        off = 0
        for i, (_, rows) in enumerate(_SMALL):
            gi = unpacked(g[off : off + rows], w_refs[i].shape)
            off += rows
            outs[i][...] = gi
            outs[k + i][...], outs[2 * k + i][...], outs[3 * k + i][...] = _adamw(w_refs[i][...], gi, m_refs[i][...], v_refs[i][...])
        loss_ref[...] = g[off : off + 1, 0:1]

    out = pl.pallas_call(
        body,
        name="small_update",
        out_shape=[jax.ShapeDtypeStruct(w.shape, F32) for _ in range(4) for w in ws] + [jax.ShapeDtypeStruct((1, 1), F32)],
        compiler_params=pltpu.CompilerParams(vmem_limit_bytes=VMEM_LIMIT),
    )(parts, first_rows, *ws, *ms, *vs)
    return [out[a * k : (a + 1) * k] for a in range(4)], out[4 * k]


_SHARD_AXIS = (1, 1, 1, 0, 0, 0, 0)
_TRANSPOSED = (False, False, False, False, True, True, False)


def _full_from_gathered(t, axis):
    if axis == 0:
        return t.reshape(N_DEV * t.shape[1], t.shape[2])
    return t


_SMALL = (("norm1_g", 8), ("norm2_g", 8), ("norm_f_g", 8), ("b_forget", 8), ("pool_scale", 8), ("pool_mix", 512))


def _pack_small(vals, loss_row):
    parts = []
    for (name, rows), t in zip(_SMALL, vals):
        f = t.astype(F32).reshape(-1)
        f = jnp.concatenate([f, jnp.zeros((rows * LANES - f.shape[0],), F32)]).reshape(rows, LANES)
        parts.append(f)
    parts.append(loss_row)
    return jnp.concatenate(parts, axis=0)


def _local_grads(x, tgt, g1, g2, gf, b_forget, pool_mix, pool_scale, w_in, fwd_token, out_weights, ffn_weights, ffn_grads_out, out_grads_out, small_grads_out, in_grads_out, norm1_grad_out):
    n_seq, S, _ = x.shape
    T = n_seq * S
    x2 = x.reshape(T, D_MODEL)
    tg2 = tgt.reshape(T, D_MODEL)
    w_uqkv, w_fl, w_g = w_in
    b_pad = jnp.concatenate([b_forget.reshape(1, N_HEADS), jnp.zeros((1, FL_PAD - N_HEADS), F32)], axis=1)
    mix_b = pool_mix.reshape(len(POOL_WINDOWS), GROUP_DIM, GROUP_DIM).astype(BF16)
    scale = pool_scale.reshape(1, POOL_WIDTH)
    g1 = g1.reshape(1, D_MODEL)
    g2 = g2.reshape(1, D_MODEL)
    gf = gf.reshape(1, D_MODEL)

    h, u, qkv, fl, gates = _in_proj(x2, g1, w_uqkv, w_fl, w_g, fwd_token)
    fcol = _forget_fwd(fl, b_pad, n_seq, S)
    pm, p2, p3 = _pool_fwd(u, mix_b, scale, n_seq, S)
    a, lse = _attn_fwd(qkv, fcol, n_seq, S)
    w_po, w_ao, w_out = out_weights(a)
    merged, x1, attn_y, pool_y = _mix_out(a, p3, gates, x2, w_ao, w_po, w_out)
    w_gate_t, w_up_t, w_down = ffn_weights(x1)
    h2, gate, up, act, dx2, loss_rows, dgf = _ffn_fwd(x1, g2, gf, tg2, w_gate_t, w_up_t, w_down)

    dgate, dup, dx1, dg2 = _ffn_bwd(dx2, gate, up, x1, g2, w_gate_t, w_up_t, w_down)
    bwd_token = ffn_grads_out(_matmul_tn(dgate, h2, "dw_ffn_gate"), _matmul_tn(dup, h2, "dw_ffn_up"), _matmul_tn(act, dx2, "dw_ffn_down"))
    dgates, dpy, day, da, dp2, dscale = _mix_bwd(dx1, gates, pool_y, attn_y, p2, scale, w_out, w_ao, w_po, bwd_token)
    out_token = out_grads_out(*_matmul_tn_pair(p3, dpy, a, day, "dw_pool_attn_out"), _matmul_tn(merged, dx1, "dw_out"))
    du, dmix = _pool_bwd(dp2, pm, mix_b, out_token, n_seq, S)
    dq, dk, dv, dfk, dfq = _attn_bwd(qkv, da, a, fcol, lse, n_seq, S)
    dfl, db = _forget_bwd(dfk, dfq, fl, b_pad, n_seq, S)
    small_token = small_grads_out((jnp.zeros_like(g1), dg2, dgf, db[:, :N_HEADS], dscale, dmix), loss_rows)
    in_token = in_grads_out(*_dw_in(h, du, dq, dk, dv, dfl, dgates, small_token))
    dx, dg1 = _in_proj_bwd(du, dq, dk, dv, dfl, dgates, x2, dx1, g1, w_uqkv, w_fl, w_g, in_token)
    norm1_grad_out(dg1)
    return dx.reshape(n_seq, S, D_MODEL)


def kernel(x, norm1_g, w_in, b_forget, pool_mix, pool_scale, w_pool_out, w_attn_out, w_out, norm2_g, w_ffn_gate, w_ffn_up, w_ffn_down, norm_f_g, loss_target, m_norm1_g, m_w_in, m_b_forget, m_pool_mix, m_pool_scale, m_w_pool_out, m_w_attn_out, m_w_out, m_norm2_g, m_w_ffn_gate, m_w_ffn_up, m_w_ffn_down, m_norm_f_g, v_norm1_g, v_w_in, v_b_forget, v_pool_mix, v_pool_scale, v_w_pool_out, v_w_attn_out, v_w_out, v_norm2_g, v_w_ffn_gate, v_w_ffn_up, v_w_ffn_down, v_norm_f_g):
    names = ("w_in", "w_pool_out", "w_attn_out", "w_out", "w_ffn_gate", "w_ffn_up", "w_ffn_down")
    w_sh = (w_in, w_pool_out, w_attn_out, w_out, w_ffn_gate, w_ffn_up, w_ffn_down)
    m_sh = (m_w_in, m_w_pool_out, m_w_attn_out, m_w_out, m_w_ffn_gate, m_w_ffn_up, m_w_ffn_down)
    v_sh = (v_w_in, v_w_pool_out, v_w_attn_out, v_w_out, v_w_ffn_gate, v_w_ffn_up, v_w_ffn_down)

    cx, cy, cc = _position()
    me = 4 * cx + 2 * cy + cc
    def stored(t, transposed):
        return jnp.transpose(t, (0, 2, 1)) if transposed else t

    w_sh, m_sh, v_sh = ([stored(t, tr) for t, tr in zip(ts, _TRANSPOSED)] for ts in (w_sh, m_sh, v_sh))
    shards = [w[0].astype(BF16) for w in w_sh]
    last_in = shards[0][:, W_IN_SHARD - 1].astype(F32).reshape(D_MODEL // LANES, LANES)
    gathered_in, tails_in = _all_gather([shards[0][:, : W_IN_SHARD - 1], last_in], "w_in_all_gather")
    out_sems = _exchange_start(shards[1:4], gathered_in, "out_weights_gather_start", "gather")
    ffn_sems = _exchange_start(shards[4:], out_sems[3], "ffn_weights_gather_start", "gather_half")
    no_order = jnp.zeros((8, LANES), F32)
    started = {}

    def out_weights(after):
        forward_sems, lands, token = _gather_forward(*ffn_sems[:3], after, "ffn_weights_forward_start")
        started["forward"] = (forward_sems, lands)
        _, lands = _exchange_wait(*out_sems[:3], token, "out_weights_gather_wait", "gather")
        return [_full_from_gathered(t, axis) for t, axis in zip(lands, _SHARD_AXIS[out])]

    def ffn_weights(after):
        lands = _forward_wait(*started["forward"], after, "ffn_weights_gather_wait")
        return [_full_from_gathered(t, axis) for t, axis in zip(lands, _SHARD_AXIS[ffn])]

    def hold_ffn_grads(*whole_grads):
        started["held"] = whole_grads
        return no_order

    def scatter_grads(*out_grads):
        chunks = [
            t if axis == 1 else t.reshape(N_DEV, -1, t.shape[1])
            for t, axis in zip((*out_grads, *started["held"]), _SHARD_AXIS[scattered])
        ]
        started["scatter"] = _exchange_start(chunks, no_order, "grads_scatter_start", "scatter")
        return started["scatter"][3]

    def gather_small(small, loss_rows):
        started["small"] = _exchange_start([_pack_small(small, loss_rows)], no_order, "small_grads_gather_start", "gather")
        return started["small"][3]

    def scatter_w_in(chunks_in, tails_in):
        started["in"] = _exchange_start([chunks_in, tails_in], no_order, "w_in_grads_scatter_start", "scatter")
        return started["in"][3]

    def gather_norm1(dg1):
        rows = jnp.reshape(dg1, (8, LANES))
        started["norm1"] = _exchange_start([rows], no_order, "norm1_grad_gather_start", "gather")

    ffn, out, scattered = slice(4, 7), slice(1, 4), slice(1, 7)
    grad_x = _local_grads(
        x, loss_target, norm1_g, norm2_g, norm_f_g, b_forget, pool_mix, pool_scale, _w_in_pieces(gathered_in, tails_in), ffn_sems[3],
        out_weights, ffn_weights, hold_ffn_grads, scatter_grads, gather_small, scatter_w_in, gather_norm1,
    )
    me_index = jnp.reshape(me, (1,)).astype(jnp.int32)

    srcs, lands = _exchange_wait(*started["scatter"][:3], started["norm1"][3], "grads_scatter_wait", "scatter")
    updates = [
        _shard_update_direct(p, s, w, m, v, me_index, "update_" + n)
        for p, s, w, m, v, n in zip(lands, srcs, w_sh[scattered], m_sh[scattered], v_sh[scattered], names[scattered])
    ]
    updates_out, updates_ffn = updates[:3], updates[3:]

    small_w = (norm1_g, norm2_g, norm_f_g, b_forget, pool_scale, pool_mix)
    small_m = (m_norm1_g, m_norm2_g, m_norm_f_g, m_b_forget, m_pool_scale, m_pool_mix)
    small_v = (v_norm1_g, v_norm2_g, v_norm_f_g, v_b_forget, v_pool_scale, v_pool_mix)
    (sent_in, sent_tails), (parts_in, parts_tails) = _exchange_wait(*started["in"][:3], updates_ffn[-1][0], "w_in_grads_scatter_wait", "scatter")
    update_in = _w_in_update(parts_in, sent_in, parts_tails, sent_tails, w_in, m_w_in, v_w_in, me_index, "update_w_in")

    def gathered_small(key, after, name):
        _, lands = _exchange_wait(*started[key][:3], after, name, "gather")
        return lands[0]

    parts = gathered_small("small", update_in[0], "small_grads_gather_wait")
    first_rows = gathered_small("norm1", parts, "norm1_grad_gather_wait")
    (g_s, d_s, nm_s, nv_s), loss = _small_update(parts, first_rows, small_w, small_m, small_v)
    g_w, d_w, nm_w, nv_w = zip(*(
        [stored(t, tr) for t in u] for u, tr in zip([update_in] + updates_out + updates_ffn, _TRANSPOSED)
    ))
    loss = loss.reshape(())
    (g1, g2, gf, gb, gsc, gmix), (d1, d2, df, db_, dsc, dmx) = g_s, d_s
    (m1, m2, mf, mb, msc, mmx), (v1, v2, vf, vb, vsc, vmx) = nm_s, nv_s

    def ordered(n1, win, b, mix, sc, wpo, wao, wout, n2, wg, wu, wd, nf):
        return (n1, win, b, mix, sc, wpo, wao, wout, n2, wg, wu, wd, nf)

    grads = ordered(g1, g_w[0], gb, gmix, gsc, g_w[1], g_w[2], g_w[3], g2, g_w[4], g_w[5], g_w[6], gf)
    deltas = ordered(d1, d_w[0], db_, dmx, dsc, d_w[1], d_w[2], d_w[3], d2, d_w[4], d_w[5], d_w[6], df)
    new_m = ordered(m1, nm_w[0], mb, mmx, msc, nm_w[1], nm_w[2], nm_w[3], m2, nm_w[4], nm_w[5], nm_w[6], mf)
    new_v = ordered(v1, nv_w[0], vb, vmx, vsc, nv_w[1], nv_w[2], nv_w[3], v2, nv_w[4], nv_w[5], nv_w[6], vf)
    return (loss, grad_x, *grads, *deltas, *new_m, *new_v)
```
